```python
import jax, jax.numpy as jnp
from jax import lax
import numpy as np

D_MODEL = 1024
BATCH = 8
SEQ = 4096
DEPTH = 1

MEM_LEN = 256
EPS = 1e-6
CONV_WIDTH = 3
A_WIDTH = D_MODEL // 2
HEAD_DIM = 64
ATTN_WIDTH = D_MODEL // 2
N_Q_HEADS = ATTN_WIDTH // HEAD_DIM
N_KV_HEADS = N_Q_HEADS // 4
KV_WIDTH = N_KV_HEADS * HEAD_DIM
WINDOW = 128
BLOCK = 128
ROPE_THETA = 500000.0
ROT_DIM = HEAD_DIM // 4
MEM_HEADS = 4
MEM_HEAD_DIM = D_MODEL // 8
MEM_WIDTH = MEM_HEADS * MEM_HEAD_DIM
N_BRANCHES = 3
IN_SIZES = (A_WIDTH, A_WIDTH, A_WIDTH, A_WIDTH,
            ATTN_WIDTH, KV_WIDTH, KV_WIDTH, ATTN_WIDTH,
            MEM_WIDTH, MEM_WIDTH,
            N_BRANCHES * D_MODEL)
IN_WIDTH = sum(IN_SIZES)

kernel_name = "hybrid_gated_conv_swa_memxattn_block"


def rms_norm(x, g):
    xf = x.astype(jnp.float32)
    y = xf * lax.rsqrt(jnp.mean(xf * xf, axis=-1, keepdims=True) + EPS)
    return (y * g.astype(jnp.float32)).astype(x.dtype)


def partial_rope(t, pos):
    half = ROT_DIM // 2
    inv_freq = jnp.power(jnp.float32(ROPE_THETA), -jnp.arange(half, dtype=jnp.float32) * (2.0 / ROT_DIM))
    ang = pos.astype(jnp.float32)[:, None] * inv_freq[None, :]
    cos = jnp.cos(ang)[None, :, None, :]
    sin = jnp.sin(ang)[None, :, None, :]
    tr = t[..., :ROT_DIM].astype(jnp.float32)
    t1, t2 = tr[..., :half], tr[..., half:]
    rot = jnp.concatenate([t1 * cos - t2 * sin, t2 * cos + t1 * sin], axis=-1).astype(t.dtype)
    return jnp.concatenate([rot, t[..., ROT_DIM:]], axis=-1)


def short_gated_conv(b, c, u, w_conv):
    s = u.shape[1]
    cu = jnp.pad(c * u, ((0, 0), (1, 1), (0, 0)))
    y = cu[:, :s] * w_conv[0] + cu[:, 1:s + 1] * w_conv[1] + cu[:, 2:] * w_conv[2]
    return b * y


def window_attention_with_sink(q, k, v, sink):
    bsz, s, hq, dh = q.shape
    hkv = k.shape[2]
    grp = hq // hkv
    nb = s // BLOCK
    qb = q.reshape(bsz, nb, BLOCK, hkv, grp, dh)
    pad = ((0, 0), (BLOCK, BLOCK), (0, 0), (0, 0))
    kp = jnp.pad(k, pad).reshape(bsz, nb + 2, BLOCK, hkv, dh)
    vp = jnp.pad(v, pad).reshape(bsz, nb + 2, BLOCK, hkv, dh)
    kw = jnp.concatenate([kp[:, :nb], kp[:, 1:nb + 1], kp[:, 2:]], axis=2)
    vw = jnp.concatenate([vp[:, :nb], vp[:, 1:nb + 1], vp[:, 2:]], axis=2)
    qpos = jnp.arange(nb)[:, None] * BLOCK + jnp.arange(BLOCK)[None, :]
    kpos = (jnp.arange(nb)[:, None] - 1) * BLOCK + jnp.arange(3 * BLOCK)[None, :]
    valid = ((jnp.abs(qpos[:, :, None] - kpos[:, None, :]) <= WINDOW)
             & (kpos >= 0)[:, None, :] & (kpos < s)[:, None, :])
    scores = jnp.einsum('bnqhgd,bnkhd->bnhgqk', qb, kw,
                        preferred_element_type=jnp.float32) * (dh ** -0.5)
    scores = jnp.where(valid[None, :, None, None], scores, -jnp.inf)
    sink_l = sink.astype(jnp.float32).reshape(hkv, grp)[None, None, :, :, None, None]
    m = jnp.maximum(jnp.max(scores, axis=-1, keepdims=True), sink_l)
    p = jnp.exp(scores - m)
    p = p / (jnp.sum(p, axis=-1, keepdims=True) + jnp.exp(sink_l - m))
    out = jnp.einsum('bnhgqk,bnkhd->bnqhgd', p.astype(v.dtype), vw)
    return out.reshape(bsz, s, hq * dh)


def memory_cross_attention(q, mk, mv):
    dm = q.shape[-1]
    scores = jnp.einsum('bshd,bmhd->bhsm', q, mk,
                        preferred_element_type=jnp.float32) * (dm ** -0.5)
    p = jax.nn.softmax(scores, axis=-1)
    out = jnp.einsum('bhsm,bmhd->bshd', p.astype(mv.dtype), mv)
    return out.reshape(q.shape[0], q.shape[1], -1)


def hybrid_layer(x, mem, g_pre, w_in, w_conv, attn_sink, g_mem, w_mem_kv,
                 w_up_a, w_up_b, w_up_m, w_out, g_post):
    bsz, s, d = x.shape
    h = rms_norm(x, g_pre)
    proj = h @ w_in
    idx = list(np.cumsum(IN_SIZES)[:-1])
    (a_b, a_c, a_x, a_z, b_q, b_k, b_v, b_z, m_q, m_z, gate_logits) = jnp.split(proj, idx, axis=-1)

    ya = short_gated_conv(a_b, a_c, a_x, w_conv) * jax.nn.silu(a_z)
    ua = ya @ w_up_a

    pos = jnp.arange(s)
    q = partial_rope(b_q.reshape(bsz, s, N_Q_HEADS, HEAD_DIM), pos)
    k = partial_rope(b_k.reshape(bsz, s, N_KV_HEADS, HEAD_DIM), pos)
    v = b_v.reshape(bsz, s, N_KV_HEADS, HEAD_DIM)
    yb = window_attention_with_sink(q, k, v, attn_sink) * jax.nn.silu(b_z)
    ub = yb @ w_up_b

    mn = rms_norm(mem, g_mem)
    mkv = mn @ w_mem_kv
    mk, mv = jnp.split(mkv, 2, axis=-1)
    mlen = mem.shape[1]
    ym = memory_cross_attention(m_q.reshape(bsz, s, MEM_HEADS, MEM_HEAD_DIM),
                                mk.reshape(bsz, mlen, MEM_HEADS, MEM_HEAD_DIM),
                                mv.reshape(bsz, mlen, MEM_HEADS, MEM_HEAD_DIM)) * jax.nn.silu(m_z)
    um = ym @ w_up_m

    gates = jax.nn.sigmoid(gate_logits.astype(jnp.float32)).astype(x.dtype).reshape(bsz, s, N_BRANCHES, d)
    merged = gates[:, :, 0] * ua + gates[:, :, 1] * ub + gates[:, :, 2] * um
    out = merged @ w_out
    return x + rms_norm(out, g_post)


def _fwd_setup_inputs(seed: int = 0) -> dict:
    key = jax.random.key(seed)
    ks = jax.random.split(key, 14)
    f32 = jnp.float32
    nrm = lambda k, shape, fan_in: jax.random.normal(k, shape, f32) * (fan_in ** -0.5)
    gain = lambda k, shape: 1.0 + 0.05 * jax.random.normal(k, shape, f32)
    return {
        "x": jax.random.normal(ks[0], (BATCH, SEQ, D_MODEL), f32),
        "mem": jax.random.normal(ks[1], (BATCH, MEM_LEN, D_MODEL), f32),
        "g_pre": gain(ks[2], (DEPTH, D_MODEL)),
        "w_in": nrm(ks[3], (DEPTH, D_MODEL, IN_WIDTH), D_MODEL),
        "w_conv": nrm(ks[4], (DEPTH, CONV_WIDTH, A_WIDTH), CONV_WIDTH),
        "attn_sink": 0.5 * jax.random.normal(ks[5], (DEPTH, N_Q_HEADS), f32),
        "g_mem": gain(ks[6], (DEPTH, D_MODEL)),
        "w_mem_kv": nrm(ks[7], (DEPTH, D_MODEL, 2 * MEM_WIDTH), D_MODEL),
        "w_up_a": nrm(ks[8], (DEPTH, A_WIDTH, D_MODEL), A_WIDTH),
        "w_up_b": nrm(ks[9], (DEPTH, ATTN_WIDTH, D_MODEL), ATTN_WIDTH),
        "w_up_m": nrm(ks[10], (DEPTH, MEM_WIDTH, D_MODEL), MEM_WIDTH),
        "w_out": nrm(ks[11], (DEPTH, D_MODEL, D_MODEL), D_MODEL),
        "g_post": gain(ks[12], (DEPTH, D_MODEL)),
    }


def _fwd_reference(x, mem, g_pre, w_in, w_conv, attn_sink, g_mem, w_mem_kv,
              w_up_a, w_up_b, w_up_m, w_out, g_post):
    for l in range(DEPTH):
        x = hybrid_layer(x, mem, g_pre[l], w_in[l], w_conv[l], attn_sink[l], g_mem[l],
                         w_mem_kv[l], w_up_a[l], w_up_b[l], w_up_m[l], w_out[l], g_post[l])
    return x


import jax as _jax
import jax.numpy as _jnp

TWIN_FORMAT = 'train_step'
FWD_PARAMS = ['x', 'mem', 'g_pre', 'w_in', 'w_conv', 'attn_sink', 'g_mem', 'w_mem_kv', 'w_up_a', 'w_up_b', 'w_up_m', 'w_out', 'g_post']
TWIN_WEIGHTS = ['g_pre', 'w_in', 'w_conv', 'attn_sink', 'g_mem', 'w_mem_kv', 'w_up_a', 'w_up_b', 'w_up_m', 'w_out', 'g_post']
TWIN_DIFF_INPUT = 'x'
TWIN_INPUTS = ['x', 'mem', 'g_pre', 'w_in', 'w_conv', 'attn_sink', 'g_mem', 'w_mem_kv', 'w_up_a', 'w_up_b', 'w_up_m', 'w_out', 'g_post', 'loss_target', 'm_g_pre', 'm_w_in', 'm_w_conv', 'm_attn_sink', 'm_g_mem', 'm_w_mem_kv', 'm_w_up_a', 'm_w_up_b', 'm_w_up_m', 'm_w_out', 'm_g_post', 'v_g_pre', 'v_w_in', 'v_w_conv', 'v_attn_sink', 'v_g_mem', 'v_w_mem_kv', 'v_w_up_a', 'v_w_up_b', 'v_w_up_m', 'v_w_out', 'v_g_post']
TWIN_OUTPUTS = ['loss', 'grad_x', 'grad_g_pre', 'grad_w_in', 'grad_w_conv', 'grad_attn_sink', 'grad_g_mem', 'grad_w_mem_kv', 'grad_w_up_a', 'grad_w_up_b', 'grad_w_up_m', 'grad_w_out', 'grad_g_post', 'delta_g_pre', 'delta_w_in', 'delta_w_conv', 'delta_attn_sink', 'delta_g_mem', 'delta_w_mem_kv', 'delta_w_up_a', 'delta_w_up_b', 'delta_w_up_m', 'delta_w_out', 'delta_g_post', 'new_m_g_pre', 'new_m_w_in', 'new_m_w_conv', 'new_m_attn_sink', 'new_m_g_mem', 'new_m_w_mem_kv', 'new_m_w_up_a', 'new_m_w_up_b', 'new_m_w_up_m', 'new_m_w_out', 'new_m_g_post', 'new_v_g_pre', 'new_v_w_in', 'new_v_w_conv', 'new_v_attn_sink', 'new_v_g_mem', 'new_v_w_mem_kv', 'new_v_w_up_a', 'new_v_w_up_b', 'new_v_w_up_m', 'new_v_w_out', 'new_v_g_post']
TWIN_LEAF_KINDS = {'loss': 'loss', 'grad_x': 'grad_x', 'grad_g_pre': 'grad_w', 'grad_w_in': 'grad_w', 'grad_w_conv': 'grad_w', 'grad_attn_sink': 'grad_w', 'grad_g_mem': 'grad_w', 'grad_w_mem_kv': 'grad_w', 'grad_w_up_a': 'grad_w', 'grad_w_up_b': 'grad_w', 'grad_w_up_m': 'grad_w', 'grad_w_out': 'grad_w', 'grad_g_post': 'grad_w', 'delta_g_pre': 'delta_w', 'delta_w_in': 'delta_w', 'delta_w_conv': 'delta_w', 'delta_attn_sink': 'delta_w', 'delta_g_mem': 'delta_w', 'delta_w_mem_kv': 'delta_w', 'delta_w_up_a': 'delta_w', 'delta_w_up_b': 'delta_w', 'delta_w_up_m': 'delta_w', 'delta_w_out': 'delta_w', 'delta_g_post': 'delta_w', 'new_m_g_pre': 'new_m', 'new_m_w_in': 'new_m', 'new_m_w_conv': 'new_m', 'new_m_attn_sink': 'new_m', 'new_m_g_mem': 'new_m', 'new_m_w_mem_kv': 'new_m', 'new_m_w_up_a': 'new_m', 'new_m_w_up_b': 'new_m', 'new_m_w_up_m': 'new_m', 'new_m_w_out': 'new_m', 'new_m_g_post': 'new_m', 'new_v_g_pre': 'new_v', 'new_v_w_in': 'new_v', 'new_v_w_conv': 'new_v', 'new_v_attn_sink': 'new_v', 'new_v_g_mem': 'new_v', 'new_v_w_mem_kv': 'new_v', 'new_v_w_up_a': 'new_v', 'new_v_w_up_b': 'new_v', 'new_v_w_up_m': 'new_v', 'new_v_w_out': 'new_v', 'new_v_g_post': 'new_v'}


def _forward(args):
    return _fwd_reference(*[args[k] for k in FWD_PARAMS])


def _output_shape():
    def fwd():
        inp = _fwd_setup_inputs(0)
        return _fwd_reference(*[inp[k] for k in FWD_PARAMS])
    out = _jax.eval_shape(fwd)
    return out.shape, out.dtype

N_MICROBATCH = 1
ADAM_LR = 0.001
ADAM_B1 = 0.9
ADAM_B2 = 0.999
ADAM_EPS = 1e-08
ADAM_WD = 0.01
ADAM_STEP = 10
PER_EXAMPLE_BATCH_AXIS = {'x': 0, 'mem': 0, 'loss_target': 0}
SHARED_INPUTS = []
_WEIGHT_DTYPES = {'g_pre': _jnp.float32, 'w_in': _jnp.float32, 'w_conv': _jnp.float32, 'attn_sink': _jnp.float32, 'g_mem': _jnp.float32, 'w_mem_kv': _jnp.float32, 'w_up_a': _jnp.float32, 'w_up_b': _jnp.float32, 'w_up_m': _jnp.float32, 'w_out': _jnp.float32, 'g_post': _jnp.float32}
MOMENT_SCALE = {'g_pre': 5.347579e-01, 'w_in': 1.981615e-01, 'w_conv': 3.581648e-01, 'attn_sink': 2.373334e-03, 'g_mem': 4.750032e-02, 'w_mem_kv': 4.615805e-02, 'w_up_a': 2.677195e-01, 'w_up_b': 3.537883e-02, 'w_up_m': 3.431516e-02, 'w_out': 2.780056e-01, 'g_post': 3.221153e+01}


def _to_microbatches(a, axis):
    t = _jnp.moveaxis(a, axis, 0)
    t = t.reshape((N_MICROBATCH, t.shape[0] // N_MICROBATCH) + t.shape[1:])
    return _jnp.moveaxis(t, 1, axis + 1)


def setup_inputs(seed: int = 0) -> dict:
    inp = _fwd_setup_inputs(seed)
    key = _jax.random.fold_in(_jax.random.key(seed), 7919)
    shape, _ = _output_shape()
    out = dict(inp)
    out["loss_target"] = _jax.random.normal(_jax.random.fold_in(key, 0), shape, _jnp.float32)
    for i, name in enumerate(TWIN_WEIGHTS):
        w = inp[name].astype(_jnp.float32)
        if MOMENT_SCALE is None:
            s = _jnp.sqrt(_jnp.mean(_jnp.square(w)) + 1e-30)
        else:
            s = MOMENT_SCALE[name]
        km, kv = _jax.random.split(_jax.random.fold_in(key, i + 1))
        out[name] = w
        out["m_" + name] = s * _jax.random.normal(km, w.shape, _jnp.float32)
        out["v_" + name] = (s * s) * _jax.random.uniform(kv, w.shape, _jnp.float32, 0.5, 1.5)
    if N_MICROBATCH > 1:
        for name, axis in PER_EXAMPLE_BATCH_AXIS.items():
            out[name] = _to_microbatches(out[name], axis)
    return {'x': out['x'], 'mem': out['mem'], 'g_pre': out['g_pre'], 'w_in': out['w_in'], 'w_conv': out['w_conv'], 'attn_sink': out['attn_sink'], 'g_mem': out['g_mem'], 'w_mem_kv': out['w_mem_kv'], 'w_up_a': out['w_up_a'], 'w_up_b': out['w_up_b'], 'w_up_m': out['w_up_m'], 'w_out': out['w_out'], 'g_post': out['g_post'], 'loss_target': out['loss_target'], 'm_g_pre': out['m_g_pre'], 'm_w_in': out['m_w_in'], 'm_w_conv': out['m_w_conv'], 'm_attn_sink': out['m_attn_sink'], 'm_g_mem': out['m_g_mem'], 'm_w_mem_kv': out['m_w_mem_kv'], 'm_w_up_a': out['m_w_up_a'], 'm_w_up_b': out['m_w_up_b'], 'm_w_up_m': out['m_w_up_m'], 'm_w_out': out['m_w_out'], 'm_g_post': out['m_g_post'], 'v_g_pre': out['v_g_pre'], 'v_w_in': out['v_w_in'], 'v_w_conv': out['v_w_conv'], 'v_attn_sink': out['v_attn_sink'], 'v_g_mem': out['v_g_mem'], 'v_w_mem_kv': out['v_w_mem_kv'], 'v_w_up_a': out['v_w_up_a'], 'v_w_up_b': out['v_w_up_b'], 'v_w_up_m': out['v_w_up_m'], 'v_w_out': out['v_w_out'], 'v_g_post': out['v_g_post']}


def _loss(weights, diff, rest, loss_target):
    with _jax.named_scope("forward"):
        args = {**rest, TWIN_DIFF_INPUT: diff, **{k: w.astype(_WEIGHT_DTYPES[k]) for k, w in weights.items()}}
        y = _forward(args)
    with _jax.named_scope("loss_head"):
        err = _jnp.square(y.astype(_jnp.float32) - loss_target)
        return 0.5 * _jnp.sum(_jnp.mean(err, axis=-1)) if err.ndim else 0.5 * err


def _adamw(w, g, m, v):
    m = ADAM_B1 * m + (1.0 - ADAM_B1) * g
    v = ADAM_B2 * v + (1.0 - ADAM_B2) * _jnp.square(g)
    m_hat = m / (1.0 - ADAM_B1 ** ADAM_STEP)
    v_hat = v / (1.0 - ADAM_B2 ** ADAM_STEP)
    delta = -ADAM_LR * (m_hat / (_jnp.sqrt(v_hat) + ADAM_EPS) + ADAM_WD * w)
    return delta, m, v


def reference(x, mem, g_pre, w_in, w_conv, attn_sink, g_mem, w_mem_kv, w_up_a, w_up_b, w_up_m, w_out, g_post, loss_target, m_g_pre, m_w_in, m_w_conv, m_attn_sink, m_g_mem, m_w_mem_kv, m_w_up_a, m_w_up_b, m_w_up_m, m_w_out, m_g_post, v_g_pre, v_w_in, v_w_conv, v_attn_sink, v_g_mem, v_w_mem_kv, v_w_up_a, v_w_up_b, v_w_up_m, v_w_out, v_g_post):
    given = dict(x=x, mem=mem, g_pre=g_pre, w_in=w_in, w_conv=w_conv, attn_sink=attn_sink, g_mem=g_mem, w_mem_kv=w_mem_kv, w_up_a=w_up_a, w_up_b=w_up_b, w_up_m=w_up_m, w_out=w_out, g_post=g_post, loss_target=loss_target, m_g_pre=m_g_pre, m_w_in=m_w_in, m_w_conv=m_w_conv, m_attn_sink=m_attn_sink, m_g_mem=m_g_mem, m_w_mem_kv=m_w_mem_kv, m_w_up_a=m_w_up_a, m_w_up_b=m_w_up_b, m_w_up_m=m_w_up_m, m_w_out=m_w_out, m_g_post=m_g_post, v_g_pre=v_g_pre, v_w_in=v_w_in, v_w_conv=v_w_conv, v_attn_sink=v_attn_sink, v_g_mem=v_g_mem, v_w_mem_kv=v_w_mem_kv, v_w_up_a=v_w_up_a, v_w_up_b=v_w_up_b, v_w_up_m=v_w_up_m, v_w_out=v_w_out, v_g_post=v_g_post)
    weights = {n: given[n] for n in TWIN_WEIGHTS}
    shared = {n: given[n] for n in SHARED_INPUTS}
    per_example = {n: given[n] for n in ['x', 'mem']}
    grad_fn = _jax.value_and_grad(_loss, argnums=(0, 1))

    def one_microbatch(ex, loss_target):
        ex = dict(ex)
        diff = ex.pop(TWIN_DIFF_INPUT)
        return grad_fn(weights, diff, {**shared, **ex}, loss_target)

    if N_MICROBATCH == 1:
        loss, (grad_w, grad_x) = one_microbatch(per_example, given["loss_target"])
    else:
        def body(carry, xs):
            loss_sum, grad_sum = carry
            l_k, (gw_k, gx_k) = one_microbatch(xs[0], xs[1])
            with _jax.named_scope("update"):
                return (loss_sum + l_k, _jax.tree.map(_jnp.add, grad_sum, gw_k)), gx_k

        init = (_jnp.zeros((), _jnp.float32), _jax.tree.map(_jnp.zeros_like, weights))
        (loss, grad_w), grad_x = _jax.lax.scan(body, init, (per_example, given["loss_target"]))
    with _jax.named_scope("update"):
        delta_w, new_m, new_v = {}, {}, {}
        for n in TWIN_WEIGHTS:
            delta_w[n], new_m[n], new_v[n] = _adamw(weights[n], grad_w[n], given["m_" + n], given["v_" + n])
    return (loss, grad_x, *[grad_w[n] for n in TWIN_WEIGHTS], *[delta_w[n] for n in TWIN_WEIGHTS],
            *[new_m[n] for n in TWIN_WEIGHTS], *[new_v[n] for n in TWIN_WEIGHTS])
```

```python
import functools

import jax
import jax.numpy as jnp
from jax import lax
from jax.experimental import pallas as pl
from jax.experimental.pallas import tpu as pltpu

F32 = jnp.float32
BF16 = jnp.bfloat16
MESH = pl.DeviceIdType.MESH

D_MODEL = 1024
EPS = 1e-6
A_WIDTH = 512
HEAD_DIM = 64
N_Q_HEADS = 8
WINDOW_BLOCK = 128
ROPE_THETA = 500000.0
ROT_DIM = 16
MEM_HEADS = 4
MEM_HEAD_DIM = 128
MEM_WIDTH = 512
IN_WIDTH = 7424
N_CHIPS = 4
LANES = 128
HALF_LANES = 64

PERM_SEGS = ((0, 2560), (2816, 3328), (4352, 7424), (3328, 4352), (2560, 2816))
UNPERM_SEGS = ((0, 2560), (7168, 7424), (2560, 3072), (6144, 7168), (3072, 6144))
COL_A, W_A = 0, 2048
COL_B, W_B = 2, 1024
COL_G, W_G = 1, 3072
COL_M, W_M = 6, 1024
COL_KV, W_KV = 28, 256

ADAM_LR = 0.001
ADAM_B1 = 0.9
ADAM_B2 = 0.999
ADAM_EPS = 1e-08
ADAM_WD = 0.01
ADAM_STEP = 10

VMEM_LIMIT_BYTES = 48 * 1024 * 1024


def _params(**kw):
    return pltpu.CompilerParams(vmem_limit_bytes=VMEM_LIMIT_BYTES, **kw)


def _sigmoid(v):
    return jax.nn.sigmoid(v)


_DIMS = {"nn": (((1,), (0,)), ((), ())), "nt": (((1,), (1,)), ((), ())), "tn": (((0,), (0,)), ((), ()))}


def _matmul(a, b, *, mode, out_dtype, tm, tn, tk, name, j_outer=False):
    if mode == "nn":
        (m, k), (_, n) = a.shape, b.shape
    elif mode == "nt":
        (m, k), (n, _) = a.shape, b.shape
    else:
        (k, m), (_, n) = a.shape, b.shape
    tm, tn, tk = min(tm, m), min(tn, n), min(tk, k)
    assert m % tm == 0 and n % tn == 0 and k % tk == 0
    ni, nj, nk = m // tm, n // tn, k // tk
    dims = _DIMS[mode]

    def ij(g0, g1):
        return (g1, g0) if j_outer else (g0, g1)

    if mode == "nn":
        a_spec = pl.BlockSpec((tm, tk), lambda g0, g1, kk: (ij(g0, g1)[0], kk))
        b_spec = pl.BlockSpec((tk, tn), lambda g0, g1, kk: (kk, ij(g0, g1)[1]))
    elif mode == "nt":
        a_spec = pl.BlockSpec((tm, tk), lambda g0, g1, kk: (ij(g0, g1)[0], kk))
        b_spec = pl.BlockSpec((tn, tk), lambda g0, g1, kk: (ij(g0, g1)[1], kk))
    else:
        a_spec = pl.BlockSpec((tk, tm), lambda g0, g1, kk: (kk, ij(g0, g1)[0]))
        b_spec = pl.BlockSpec((tk, tn), lambda g0, g1, kk: (kk, ij(g0, g1)[1]))
    o_spec = pl.BlockSpec((tm, tn), lambda g0, g1, kk: ij(g0, g1))

    def part(a_ref, b_ref):
        return lax.dot_general(a_ref[...].astype(BF16), b_ref[...].astype(BF16), dims,
                               preferred_element_type=F32)

    if nk == 1:
        def body(a_ref, b_ref, o_ref):
            o_ref[...] = part(a_ref, b_ref).astype(out_dtype)
        scratch = []
    else:
        def body(a_ref, b_ref, o_ref, acc_ref):
            kk = pl.program_id(2)

            @pl.when(kk == 0)
            def _():
                acc_ref[...] = part(a_ref, b_ref)

            @pl.when(kk > 0)
            def _():
                acc_ref[...] += part(a_ref, b_ref)

            @pl.when(kk == nk - 1)
            def _():
                o_ref[...] = acc_ref[...].astype(out_dtype)
        scratch = [pltpu.VMEM((tm, tn), F32)]

    grid = (nj, ni, nk) if j_outer else (ni, nj, nk)
    return pl.pallas_call(
        body, grid=grid, in_specs=[a_spec, b_spec], out_specs=o_spec,
        out_shape=jax.ShapeDtypeStruct((m, n), out_dtype), scratch_shapes=scratch,
        name=name, compiler_params=_params())(a, b)


def _rmsnorm_fwd(x, g, *, name):
    s, d = x.shape
    ts = min(512, s)

    def body(x_ref, g_ref, o_ref):
        xv = x_ref[...]
        r = lax.rsqrt(jnp.mean(xv * xv, axis=-1, keepdims=True) + EPS)
        o_ref[...] = ((xv * r) * g_ref[...]).astype(BF16)

    return pl.pallas_call(
        body, grid=(s // ts,),
        in_specs=[pl.BlockSpec((ts, d), lambda i: (i, 0)), pl.BlockSpec((1, d), lambda i: (0, 0))],
        out_specs=pl.BlockSpec((ts, d), lambda i: (i, 0)),
        out_shape=jax.ShapeDtypeStruct((s, d), BF16), name=name, compiler_params=_params())(x, g)


def _rmsnorm_bwd(dh, x, g, res, *, name):
    s, d = x.shape
    ts = min(256, s)

    def body(dh_ref, x_ref, g_ref, res_ref, dx_ref, dg_ref):
        xv = x_ref[...]
        r = lax.rsqrt(jnp.mean(xv * xv, axis=-1, keepdims=True) + EPS)
        xh = xv * r
        dhv = dh_ref[...]
        part = jnp.sum(dhv * xh, axis=0, keepdims=True)

        @pl.when(pl.program_id(0) == 0)
        def _():
            dg_ref[...] = part

        @pl.when(pl.program_id(0) > 0)
        def _():
            dg_ref[...] += part

        dxh = dhv * g_ref[...]
        dx_ref[...] = res_ref[...] + r * (dxh - xh * jnp.mean(dxh * xh, axis=-1, keepdims=True))

    row = pl.BlockSpec((ts, d), lambda i: (i, 0))
    vec = pl.BlockSpec((1, d), lambda i: (0, 0))
    return pl.pallas_call(
        body, grid=(s // ts,), in_specs=[row, row, vec, row], out_specs=[row, vec],
        out_shape=[jax.ShapeDtypeStruct((s, d), F32), jax.ShapeDtypeStruct((1, d), F32)],
        name=name, compiler_params=_params())(dh, x, g, res)


def _post_loss(out, x, tgt, g):
    s, d = out.shape
    ts = 256

    def body(o_ref, x_ref, t_ref, g_ref, do_ref, dy_ref, dg_ref, loss_ref):
        ov = o_ref[...]
        r = lax.rsqrt(jnp.mean(ov * ov, axis=-1, keepdims=True) + EPS)
        nh = ov * r
        gv = g_ref[...]
        e = (x_ref[...] + nh * gv) - t_ref[...]
        lpart = 0.5 * jnp.sum(jnp.mean(e * e, axis=-1, keepdims=True), axis=0, keepdims=True)
        dy = e * (1.0 / d)
        dgp = jnp.sum(dy * nh, axis=0, keepdims=True)

        @pl.when(pl.program_id(0) == 0)
        def _():
            dg_ref[...] = dgp
            loss_ref[...] = jnp.broadcast_to(lpart, loss_ref.shape)

        @pl.when(pl.program_id(0) > 0)
        def _():
            dg_ref[...] += dgp
            loss_ref[...] += jnp.broadcast_to(lpart, loss_ref.shape)

        dn = dy * gv
        dy_ref[...] = dy
        do_ref[...] = (r * (dn - nh * jnp.mean(dn * nh, axis=-1, keepdims=True))).astype(BF16)

    row = pl.BlockSpec((ts, d), lambda i: (i, 0))
    vec = pl.BlockSpec((1, d), lambda i: (0, 0))
    lsp = pl.BlockSpec((1, LANES), lambda i: (0, 0))
    return pl.pallas_call(
        body, grid=(s // ts,), in_specs=[row, row, row, vec], out_specs=[row, row, vec, lsp],
        out_shape=[jax.ShapeDtypeStruct((s, d), BF16), jax.ShapeDtypeStruct((s, d), F32),
                   jax.ShapeDtypeStruct((1, d), F32), jax.ShapeDtypeStruct((1, LANES), F32)],
        name="post_loss", compiler_params=_params())(out, x, tgt, g)


def _merge_fwd(ua, ub, um, proj):
    s, d = ua.shape
    ts = 256

    def body(ua_ref, ub_ref, um_ref, g_ref, o_ref):
        gl = g_ref[...]
        o_ref[...] = (_sigmoid(gl[:, :d]) * ua_ref[...] + _sigmoid(gl[:, d:2 * d]) * ub_ref[...]
                      + _sigmoid(gl[:, 2 * d:]) * um_ref[...]).astype(BF16)

    row = pl.BlockSpec((ts, d), lambda i: (i, 0))
    return pl.pallas_call(
        body, grid=(s // ts,),
        in_specs=[row, row, row, pl.BlockSpec((ts, W_G), lambda i: (i, COL_G))], out_specs=row,
        out_shape=jax.ShapeDtypeStruct((s, d), BF16), name="merge_fwd",
        compiler_params=_params())(ua, ub, um, proj)


def _merge_bwd(dm, ua, ub, um, proj):
    s, d = ua.shape
    ts = 256

    def body(dm_ref, ua_ref, ub_ref, um_ref, g_ref, dua_ref, dub_ref, dum_ref, dp_ref):
        dmv = dm_ref[...]
        gl = g_ref[...]
        for k, (u_ref, du_ref) in enumerate(((ua_ref, dua_ref), (ub_ref, dub_ref), (um_ref, dum_ref))):
            sg = _sigmoid(gl[:, k * d:(k + 1) * d])
            du_ref[...] = (sg * dmv).astype(BF16)
            dp_ref[:, k * d:(k + 1) * d] = ((dmv * u_ref[...]) * (sg * (1.0 - sg))).astype(BF16)

    row = pl.BlockSpec((ts, d), lambda i: (i, 0))
    gsp = pl.BlockSpec((ts, W_G), lambda i: (i, COL_G))
    return pl.pallas_call(
        body, grid=(s // ts,),
        in_specs=[row, row, row, row, gsp],
        out_specs=[row, row, row, gsp],
        out_shape=[jax.ShapeDtypeStruct((s, d), BF16)] * 3 + [jax.ShapeDtypeStruct((s, IN_WIDTH), BF16)],
        name="merge_bwd", compiler_params=_params())(dm, ua, ub, um, proj)


def _conv_core(blk, prev, nxt, w, i, last, ts):
    c = A_WIDTH
    ab, ac, ax, az = blk[:, :c], blk[:, c:2 * c], blk[:, 2 * c:3 * c], blk[:, 3 * c:]
    cu = ac * ax
    cu_prev = (prev[7:8, c:2 * c] * prev[7:8, 2 * c:3 * c]) * jnp.where(i > 0, 1.0, 0.0)
    cu_next = (nxt[0:1, c:2 * c] * nxt[0:1, 2 * c:3 * c]) * jnp.where(i < last, 1.0, 0.0)
    row = lax.broadcasted_iota(jnp.int32, (ts, c), 0)
    cm1 = jnp.where(row == 0, cu_prev, pltpu.roll(cu, 1, 0))
    cp1 = jnp.where(row == ts - 1, cu_next, pltpu.roll(cu, ts - 1, 0))
    yc = cm1 * w[0:1] + cu * w[1:2] + cp1 * w[2:3]
    return ab, ac, ax, az, cu, cm1, cp1, yc, row


def _halo_specs(ts, width, col, nblk8):
    prev = pl.BlockSpec((8, width), lambda i: (jnp.maximum(i * (ts // 8) - 1, 0), col))
    nxt = pl.BlockSpec((8, width), lambda i: (jnp.minimum((i + 1) * (ts // 8), nblk8 - 1), col))
    return prev, nxt


def _conv_fwd(proj, w_conv):
    s = proj.shape[0]
    ts = 256
    last = s // ts - 1

    def body(a_ref, ap_ref, an_ref, w_ref, ya_ref):
        i = pl.program_id(0)
        ab, _, _, az, _, _, _, yc, _ = _conv_core(a_ref[...], ap_ref[...], an_ref[...], w_ref[...], i, last, ts)
        ya_ref[...] = ((ab * yc) * (az * _sigmoid(az))).astype(BF16)

    prev, nxt = _halo_specs(ts, W_A, COL_A, s // 8)
    return pl.pallas_call(
        body, grid=(s // ts,),
        in_specs=[pl.BlockSpec((ts, W_A), lambda i: (i, COL_A)), prev, nxt,
                  pl.BlockSpec((3, A_WIDTH), lambda i: (0, 0))],
        out_specs=pl.BlockSpec((ts, A_WIDTH), lambda i: (i, 0)),
        out_shape=jax.ShapeDtypeStruct((s, A_WIDTH), BF16), name="conv_fwd",
        compiler_params=_params())(proj, proj, proj, w_conv)


def _conv_bwd(proj, w_conv, dya, dproj):
    s = proj.shape[0]
    ts = 256
    last = s // ts - 1
    c = A_WIDTH

    def body(a_ref, ap_ref, an_ref, w_ref, d_ref, dp_ref, dn_ref, _, dproj_ref, dw_ref):
        i = pl.program_id(0)
        w = w_ref[...]
        prev, nxt = ap_ref[...], an_ref[...]
        ab, ac, ax, az, cu, cm1, cp1, yc, row = _conv_core(a_ref[...], prev, nxt, w, i, last, ts)
        sg = _sigmoid(az)
        sz = az * sg
        dya_v = d_ref[...]
        dyc = dya_v * sz * ab
        dproj_ref[:, :c] = (dya_v * sz * yc).astype(BF16)
        dproj_ref[:, 3 * c:] = (dya_v * (ab * yc) * (sg * (1.0 + az * (1.0 - sg)))).astype(BF16)

        def halo_dyc(a_row, d_row):
            azr = a_row[:, 3 * c:]
            return d_row * (azr * _sigmoid(azr)) * a_row[:, :c]

        dyc_prev = halo_dyc(prev[7:8], dp_ref[...][7:8]) * jnp.where(i > 0, 1.0, 0.0)
        dyc_next = halo_dyc(nxt[0:1], dn_ref[...][0:1]) * jnp.where(i < last, 1.0, 0.0)
        dyc_m1 = jnp.where(row == 0, dyc_prev, pltpu.roll(dyc, 1, 0))
        dyc_p1 = jnp.where(row == ts - 1, dyc_next, pltpu.roll(dyc, ts - 1, 0))
        dcu = dyc_p1 * w[0:1] + dyc * w[1:2] + dyc_m1 * w[2:3]
        dproj_ref[:, c:2 * c] = (dcu * ax).astype(BF16)
        dproj_ref[:, 2 * c:3 * c] = (dcu * ac).astype(BF16)
        dw = [jnp.sum(dyc * t, axis=0, keepdims=True) for t in (cm1, cu, cp1)]

        @pl.when(i == 0)
        def _():
            for k in range(3):
                dw_ref[k:k + 1, :] = dw[k]

        @pl.when(i > 0)
        def _():
            for k in range(3):
                dw_ref[k:k + 1, :] += dw[k]

    prev, nxt = _halo_specs(ts, W_A, COL_A, s // 8)
    dprev, dnxt = _halo_specs(ts, A_WIDTH, 0, s // 8)
    return pl.pallas_call(
        body, grid=(s // ts,),
        in_specs=[pl.BlockSpec((ts, W_A), lambda i: (i, COL_A)), prev, nxt,
                  pl.BlockSpec((3, A_WIDTH), lambda i: (0, 0)),
                  pl.BlockSpec((ts, A_WIDTH), lambda i: (i, 0)), dprev, dnxt,
                  pl.BlockSpec(memory_space=pl.ANY)],
        out_specs=[pl.BlockSpec((ts, W_A), lambda i: (i, COL_A)), pl.BlockSpec((3, A_WIDTH), lambda i: (0, 0))],
        out_shape=[jax.ShapeDtypeStruct(dproj.shape, BF16), jax.ShapeDtypeStruct((3, A_WIDTH), F32)],
        input_output_aliases={7: 0}, name="conv_bwd",
        compiler_params=_params())(proj, proj, proj, w_conv, dya, dya, dya, dproj)


def _rope_tables(s):
    half = ROT_DIM // 2
    inv_freq = jnp.power(jnp.float32(ROPE_THETA), -jnp.arange(half, dtype=F32) * (2.0 / ROT_DIM))
    ang = jnp.arange(s).astype(F32)[:, None] * inv_freq[None, :]
    cos, sin = jnp.cos(ang), jnp.sin(ang)
    pad = jnp.zeros((s, HEAD_DIM - ROT_DIM), F32)
    c = jnp.concatenate([cos, cos, pad + 1.0], axis=1)
    s1 = jnp.concatenate([-sin, jnp.zeros_like(sin), pad], axis=1)
    s2 = jnp.concatenate([jnp.zeros_like(sin), sin, pad], axis=1)
    return jnp.concatenate([c, c, s1, s1, s2, s2], axis=1)


def _rope(t, tab):
    return (t * tab[:, :LANES] + pltpu.roll(t, LANES - 8, 1) * tab[:, LANES:2 * LANES]
            + pltpu.roll(t, 8, 1) * tab[:, 2 * LANES:])


def _rope_transpose(dt, tab):
    return (dt * tab[:, :LANES] + pltpu.roll(dt * tab[:, LANES:2 * LANES], 8, 1)
            + pltpu.roll(dt * tab[:, 2 * LANES:], LANES - 8, 1))


def _rope_kv(proj, tab):
    s = proj.shape[0]
    nb = s // WINDOW_BLOCK

    def body(kv_ref, t_ref, k_ref, v_ref):
        j = pl.program_id(0)
        inside = jnp.where((j > 0) & (j <= nb), 1.0, 0.0)
        kv = kv_ref[...]
        k_ref[...] = (_rope(kv[:, :LANES], t_ref[...]) * inside).astype(BF16)
        v_ref[...] = (kv[:, LANES:] * inside).astype(BF16)

    def src(j):
        return jnp.clip(j - 1, 0, nb - 1)

    o_spec = pl.BlockSpec((WINDOW_BLOCK, LANES), lambda j: (j, 0))
    shp = jax.ShapeDtypeStruct((s + 2 * WINDOW_BLOCK, LANES), BF16)
    return pl.pallas_call(
        body, grid=(nb + 2,),
        in_specs=[pl.BlockSpec((WINDOW_BLOCK, W_KV), lambda j: (src(j), COL_KV)),
                  pl.BlockSpec((WINDOW_BLOCK, 3 * LANES), lambda j: (src(j), 0))],
        out_specs=[o_spec, o_spec], out_shape=[shp, shp], name="rope_kv",
        compiler_params=_params())(proj, tab)


def _rope_kv_bwd(dkpad, dvpad, tab, dproj):
    s = tab.shape[0]
    nb = s // WINDOW_BLOCK

    def body(dk_ref, dv_ref, t_ref, _, dp_ref):
        dp_ref[:, :LANES] = _rope_transpose(dk_ref[...], t_ref[...]).astype(BF16)
        dp_ref[:, LANES:] = dv_ref[...].astype(BF16)

    pad_spec = pl.BlockSpec((WINDOW_BLOCK, LANES), lambda j: (j + 1, 0))
    return pl.pallas_call(
        body, grid=(nb,),
        in_specs=[pad_spec, pad_spec, pl.BlockSpec((WINDOW_BLOCK, 3 * LANES), lambda j: (j, 0)),
                  pl.BlockSpec(memory_space=pl.ANY)],
        out_specs=pl.BlockSpec((WINDOW_BLOCK, W_KV), lambda j: (j, COL_KV)),
        out_shape=jax.ShapeDtypeStruct(dproj.shape, BF16), input_output_aliases={3: 0},
        name="rope_kv_bwd", compiler_params=_params())(dkpad, dvpad, tab, dproj)


def _window_operands(k_ref, v_ref, n, lo):
    start = pl.multiple_of(n * WINDOW_BLOCK, WINDOW_BLOCK)
    kw = k_ref[pl.ds(start, 3 * WINDOW_BLOCK), :].astype(F32)
    vw = v_ref[pl.ds(start, 3 * WINDOW_BLOCK), :].astype(F32)
    kr, vr = pltpu.roll(kw, HALF_LANES, 1), pltpu.roll(vw, HALF_LANES, 1)
    k2 = (jnp.where(lo, kw, kr).astype(BF16), jnp.where(lo, kr, kw).astype(BF16))
    v2 = (jnp.where(lo, vw, vr).astype(BF16), jnp.where(lo, vr, vw).astype(BF16))
    return k2, v2


def _window_mask(n, s):
    wb = WINDOW_BLOCK
    qi = lax.broadcasted_iota(jnp.int32, (wb, 3 * wb), 0)
    kj = lax.broadcasted_iota(jnp.int32, (wb, 3 * wb), 1)
    kpos = kj + (n - 1) * wb
    return (kj >= qi) & (kj <= qi + 2 * wb) & (kpos >= 0) & (kpos < s)


def _head_probs(qh, k2g, valid, sink):
    sc = lax.dot_general(qh, k2g, _DIMS["nt"], preferred_element_type=F32) * (HEAD_DIM ** -0.5)
    sc = jnp.where(valid, sc, -jnp.inf)
    m = jnp.maximum(jnp.max(sc, axis=1, keepdims=True), sink)
    e = jnp.exp(sc - m)
    es = jnp.exp(sink - m)
    inv = 1.0 / (jnp.sum(e, axis=1, keepdims=True) + es)
    return e * inv, es * inv


def _swa_fwd(proj, kpad, vpad, tab, sink):
    s = proj.shape[0]
    wb = WINDOW_BLOCK

    def body(b_ref, k_ref, v_ref, t_ref, sink_ref, o_ref, y_ref):
        n = pl.program_id(0)
        lane = lax.broadcasted_iota(jnp.int32, (wb, LANES), 1)
        lo = lane < HALF_LANES
        lo_w = lax.broadcasted_iota(jnp.int32, (3 * wb, LANES), 1) < HALF_LANES
        k2, v2 = _window_operands(k_ref, v_ref, n, lo_w)
        valid = _window_mask(n, s)
        tab_v = t_ref[...]
        for p in range(N_Q_HEADS // 2):
            cols = slice(p * LANES, (p + 1) * LANES)
            qr = _rope(b_ref[:, cols], tab_v)
            g = p // 2
            outs = []
            for half in range(2):
                keep = lo if half == 0 else ~lo
                qh = jnp.where(keep, qr, 0.0).astype(BF16)
                prob, _ = _head_probs(qh, k2[g], valid, sink_ref[0, 2 * p + half])
                outs.append(jnp.dot(prob.astype(BF16), v2[g], preferred_element_type=F32))
            op = jnp.where(lo, outs[0], outs[1])
            o_ref[:, cols] = op
            zp = b_ref[:, A_WIDTH + p * LANES:A_WIDTH + (p + 1) * LANES]
            y_ref[:, cols] = (op * (zp * _sigmoid(zp))).astype(BF16)

    pad_spec = pl.BlockSpec((s + 2 * wb, LANES), lambda n: (0, 0))
    o_spec = pl.BlockSpec((wb, A_WIDTH), lambda n: (n, 0))
    return pl.pallas_call(
        body, grid=(s // wb,),
        in_specs=[pl.BlockSpec((wb, W_B), lambda n: (n, COL_B)), pad_spec, pad_spec,
                  pl.BlockSpec((wb, 3 * LANES), lambda n: (n, 0)),
                  pl.BlockSpec(memory_space=pltpu.SMEM)],
        out_specs=[o_spec, o_spec],
        out_shape=[jax.ShapeDtypeStruct((s, A_WIDTH), F32), jax.ShapeDtypeStruct((s, A_WIDTH), BF16)],
        name="swa_fwd", compiler_params=_params())(proj, kpad, vpad, tab, sink)


def _swa_bwd(proj, kpad, vpad, tab, sink, o_attn, dyb, dproj):
    s = proj.shape[0]
    wb = WINDOW_BLOCK
    scale = HEAD_DIM ** -0.5

    def body(b_ref, k_ref, v_ref, t_ref, sink_ref, o_ref, dy_ref, _, dp_ref, dk_ref, dv_ref, ds_ref):
        n = pl.program_id(0)

        @pl.when(n == 0)
        def _():
            dk_ref[...] = jnp.zeros_like(dk_ref)
            dv_ref[...] = jnp.zeros_like(dv_ref)
            ds_ref[...] = jnp.zeros_like(ds_ref)

        lane = lax.broadcasted_iota(jnp.int32, (wb, LANES), 1)
        lo = lane < HALF_LANES
        lo_w = lax.broadcasted_iota(jnp.int32, (3 * wb, LANES), 1) < HALF_LANES
        k2, v2 = _window_operands(k_ref, v_ref, n, lo_w)
        valid = _window_mask(n, s)
        tab_v = t_ref[...]
        dk2 = [jnp.zeros((3 * wb, LANES), F32), jnp.zeros((3 * wb, LANES), F32)]
        dv2 = [jnp.zeros((3 * wb, LANES), F32), jnp.zeros((3 * wb, LANES), F32)]
        for p in range(N_Q_HEADS // 2):
            cols = slice(p * LANES, (p + 1) * LANES)
            zcols = slice(A_WIDTH + p * LANES, A_WIDTH + (p + 1) * LANES)
            qr = _rope(b_ref[:, cols], tab_v)
            zp = b_ref[:, zcols]
            sg = _sigmoid(zp)
            op = o_ref[:, cols]
            dyp = dy_ref[:, cols]
            do_p = dyp * (zp * sg)
            dp_ref[:, zcols] = (dyp * op * (sg * (1.0 + zp * (1.0 - sg)))).astype(BF16)
            g = p // 2
            dqs = []
            for half in range(2):
                h = 2 * p + half
                keep = lo if half == 0 else ~lo
                qh = jnp.where(keep, qr, 0.0).astype(BF16)
                prob, psink = _head_probs(qh, k2[g], valid, sink_ref[0, h])
                doh = jnp.where(keep, do_p, 0.0)
                delta = jnp.sum(doh * op, axis=1, keepdims=True)
                dohb = doh.astype(BF16)
                dprob = lax.dot_general(dohb, v2[g], _DIMS["nt"], preferred_element_type=F32)
                dsc = (prob * (dprob - delta)).astype(BF16)
                dsink = -jnp.sum(psink * delta, axis=0, keepdims=True)
                ds_ref[h:h + 1, :] += jnp.broadcast_to(dsink, (1, LANES))
                dqs.append(jnp.dot(dsc, k2[g], preferred_element_type=F32) * scale)
                dk2[g] = dk2[g] + lax.dot_general(dsc, qh, _DIMS["tn"], preferred_element_type=F32) * scale
                dv2[g] = dv2[g] + lax.dot_general(prob.astype(BF16), dohb, _DIMS["tn"],
                                                  preferred_element_type=F32)
            dqr = jnp.where(lo, dqs[0], dqs[1])
            dp_ref[:, cols] = _rope_transpose(dqr, tab_v).astype(BF16)
        dks = [t + pltpu.roll(t, HALF_LANES, 1) for t in dk2]
        dvs = [t + pltpu.roll(t, HALF_LANES, 1) for t in dv2]
        start = pl.multiple_of(n * wb, wb)
        dk_ref[pl.ds(start, 3 * wb), :] += jnp.where(lo_w, dks[0], dks[1])
        dv_ref[pl.ds(start, 3 * wb), :] += jnp.where(lo_w, dvs[0], dvs[1])

    pad_spec = pl.BlockSpec((s + 2 * wb, LANES), lambda n: (0, 0))
    blk = pl.BlockSpec((wb, A_WIDTH), lambda n: (n, 0))
    bsp = pl.BlockSpec((wb, W_B), lambda n: (n, COL_B))
    pad_shape = jax.ShapeDtypeStruct((s + 2 * wb, LANES), F32)
    return pl.pallas_call(
        body, grid=(s // wb,),
        in_specs=[bsp, pad_spec, pad_spec, pl.BlockSpec((wb, 3 * LANES), lambda n: (n, 0)),
                  pl.BlockSpec(memory_space=pltpu.SMEM), blk, blk, pl.BlockSpec(memory_space=pl.ANY)],
        out_specs=[bsp, pad_spec, pad_spec, pl.BlockSpec((8, LANES), lambda n: (0, 0))],
        out_shape=[jax.ShapeDtypeStruct(dproj.shape, BF16), pad_shape, pad_shape,
                   jax.ShapeDtypeStruct((8, LANES), F32)],
        input_output_aliases={7: 0}, name="swa_bwd",
        compiler_params=_params())(proj, kpad, vpad, tab, sink, o_attn, dyb, dproj)


def _mem_probs(qh, mk):
    sc = lax.dot_general(qh, mk, _DIMS["nt"], preferred_element_type=F32) * (MEM_HEAD_DIM ** -0.5)
    e = jnp.exp(sc - jnp.max(sc, axis=1, keepdims=True))
    return e * (1.0 / jnp.sum(e, axis=1, keepdims=True))


def _mem_fwd(proj, mkv):
    s = proj.shape[0]
    ts = 256
    mlen = mkv.shape[0]

    def body(m_ref, kv_ref, o_ref, y_ref):
        for h in range(MEM_HEADS):
            cols = slice(h * LANES, (h + 1) * LANES)
            mk = kv_ref[:, cols].astype(BF16)
            mv = kv_ref[:, MEM_WIDTH + h * LANES:MEM_WIDTH + (h + 1) * LANES].astype(BF16)
            prob = _mem_probs(m_ref[:, cols].astype(BF16), mk)
            oh = jnp.dot(prob.astype(BF16), mv, preferred_element_type=F32)
            o_ref[:, cols] = oh
            zh = m_ref[:, MEM_WIDTH + h * LANES:MEM_WIDTH + (h + 1) * LANES]
            y_ref[:, cols] = (oh * (zh * _sigmoid(zh))).astype(BF16)

    o_spec = pl.BlockSpec((ts, MEM_WIDTH), lambda i: (i, 0))
    return pl.pallas_call(
        body, grid=(s // ts,),
        in_specs=[pl.BlockSpec((ts, W_M), lambda i: (i, COL_M)),
                  pl.BlockSpec((mlen, 2 * MEM_WIDTH), lambda i: (0, 0))],
        out_specs=[o_spec, o_spec],
        out_shape=[jax.ShapeDtypeStruct((s, MEM_WIDTH), F32), jax.ShapeDtypeStruct((s, MEM_WIDTH), BF16)],
        name="mem_fwd", compiler_params=_params())(proj, mkv)


def _mem_bwd(proj, mkv, o_mem, dym, dproj):
    s = proj.shape[0]
    ts = 256
    mlen = mkv.shape[0]
    scale = MEM_HEAD_DIM ** -0.5

    def body(m_ref, kv_ref, o_ref, dy_ref, _, dp_ref, dkv_ref):
        @pl.when(pl.program_id(0) == 0)
        def _():
            dkv_ref[...] = jnp.zeros_like(dkv_ref)

        for h in range(MEM_HEADS):
            cols = slice(h * LANES, (h + 1) * LANES)
            vcols = slice(MEM_WIDTH + h * LANES, MEM_WIDTH + (h + 1) * LANES)
            mk = kv_ref[:, cols].astype(BF16)
            mv = kv_ref[:, vcols].astype(BF16)
            qh = m_ref[:, cols].astype(BF16)
            zh = m_ref[:, vcols]
            sg = _sigmoid(zh)
            oh = o_ref[:, cols]
            dyh = dy_ref[:, cols]
            doh = dyh * (zh * sg)
            dp_ref[:, vcols] = (dyh * oh * (sg * (1.0 + zh * (1.0 - sg)))).astype(BF16)
            prob = _mem_probs(qh, mk)
            delta = jnp.sum(doh * oh, axis=1, keepdims=True)
            dohb = doh.astype(BF16)
            dprob = lax.dot_general(dohb, mv, _DIMS["nt"], preferred_element_type=F32)
            dsc = (prob * (dprob - delta)).astype(BF16)
            dp_ref[:, cols] = (jnp.dot(dsc, mk, preferred_element_type=F32) * scale).astype(BF16)
            dkv_ref[:, cols] += lax.dot_general(dsc, qh, _DIMS["tn"], preferred_element_type=F32) * scale
            dkv_ref[:, vcols] += lax.dot_general(prob.astype(BF16), dohb, _DIMS["tn"],
                                                 preferred_element_type=F32)

    blk = pl.BlockSpec((ts, MEM_WIDTH), lambda i: (i, 0))
    msp = pl.BlockSpec((ts, W_M), lambda i: (i, COL_M))
    kvsp = pl.BlockSpec((mlen, 2 * MEM_WIDTH), lambda i: (0, 0))
    return pl.pallas_call(
        body, grid=(s // ts,),
        in_specs=[msp, kvsp, blk, blk, pl.BlockSpec(memory_space=pl.ANY)],
        out_specs=[msp, kvsp],
        out_shape=[jax.ShapeDtypeStruct(dproj.shape, BF16), jax.ShapeDtypeStruct(mkv.shape, F32)],
        input_output_aliases={4: 0}, name="mem_bwd",
        compiler_params=_params())(proj, mkv, o_mem, dym, dproj)


def _local_grads(x, mem, tgt, g_pre, w_in_p, w_conv, sink, g_mem, w_kv, w_up_a, w_up_b, w_up_m, w_out, g_post):
    s = x.shape[0]
    tab = _rope_tables(s)

    h = _rmsnorm_fwd(x, g_pre, name="pre_norm")
    proj = _matmul(h, w_in_p, mode="nn", out_dtype=F32, tm=512, tn=3712, tk=D_MODEL, name="proj", j_outer=True)
    ya = _conv_fwd(proj, w_conv)
    kpad, vpad = _rope_kv(proj, tab)
    o_attn, yb = _swa_fwd(proj, kpad, vpad, tab, sink)
    mn = _rmsnorm_fwd(mem, g_mem, name="mem_norm")
    mkv = _matmul(mn, w_kv, mode="nn", out_dtype=F32, tm=256, tn=1024, tk=D_MODEL, name="mem_kv")
    o_mem, ym = _mem_fwd(proj, mkv)
    ua = _matmul(ya, w_up_a, mode="nn", out_dtype=F32, tm=512, tn=1024, tk=512, name="up_a")
    ub = _matmul(yb, w_up_b, mode="nn", out_dtype=F32, tm=512, tn=1024, tk=512, name="up_b")
    um = _matmul(ym, w_up_m, mode="nn", out_dtype=F32, tm=512, tn=1024, tk=512, name="up_m")
    merged = _merge_fwd(ua, ub, um, proj)
    out = _matmul(merged, w_out, mode="nn", out_dtype=F32, tm=512, tn=1024, tk=D_MODEL, name="out_proj")

    d_out, dy, dg_post, loss = _post_loss(out, x, tgt, g_post)
    dw_out = _matmul(merged, d_out, mode="tn", out_dtype=F32, tm=1024, tn=1024, tk=512, name="dw_out")
    d_merged = _matmul(d_out, w_out, mode="nt", out_dtype=F32, tm=512, tn=1024, tk=D_MODEL, name="d_merged")

    d_ua, d_ub, d_um, dproj = _merge_bwd(d_merged, ua, ub, um, proj)

    dw_up_a = _matmul(ya, d_ua, mode="tn", out_dtype=F32, tm=512, tn=1024, tk=512, name="dw_up_a")
    dw_up_b = _matmul(yb, d_ub, mode="tn", out_dtype=F32, tm=512, tn=1024, tk=512, name="dw_up_b")
    dw_up_m = _matmul(ym, d_um, mode="tn", out_dtype=F32, tm=512, tn=1024, tk=512, name="dw_up_m")
    d_ya = _matmul(d_ua, w_up_a, mode="nt", out_dtype=F32, tm=512, tn=512, tk=D_MODEL, name="d_ya")
    d_yb = _matmul(d_ub, w_up_b, mode="nt", out_dtype=F32, tm=512, tn=512, tk=D_MODEL, name="d_yb")
    d_ym = _matmul(d_um, w_up_m, mode="nt", out_dtype=F32, tm=512, tn=512, tk=D_MODEL, name="d_ym")

    dproj, dw_conv = _conv_bwd(proj, w_conv, d_ya, dproj)
    dproj, dkpad, dvpad, dsink = _swa_bwd(proj, kpad, vpad, tab, sink, o_attn, d_yb, dproj)
    dproj = _rope_kv_bwd(dkpad, dvpad, tab, dproj)
    dproj, d_mkv = _mem_bwd(proj, mkv, o_mem, d_ym, dproj)

    dw_kv = _matmul(mn, d_mkv, mode="tn", out_dtype=F32, tm=1024, tn=1024, tk=256, name="dw_kv")
    d_mn = _matmul(d_mkv, w_kv, mode="nt", out_dtype=F32, tm=256, tn=1024, tk=D_MODEL, name="d_mn")
    _, dg_mem = _rmsnorm_bwd(d_mn, mem, g_mem, d_mn, name="mem_norm_bwd")

    dw_in_p = _matmul(h, dproj, mode="tn", out_dtype=F32, tm=512, tn=3712, tk=512, name="dw_in")
    d_h = _matmul(dproj, w_in_p, mode="nt", out_dtype=F32, tm=512, tn=1024, tk=3712, name="d_h")
    grad_x, dg_pre = _rmsnorm_bwd(d_h, x, g_pre, dy, name="pre_norm_bwd")
    return dict(loss=loss, grad_x=grad_x, g_pre=dg_pre, w_in_p=dw_in_p, w_conv=dw_conv, sink=dsink, g_mem=dg_mem,
                w_kv=dw_kv, w_up_a=dw_up_a, w_up_b=dw_up_b, w_up_m=dw_up_m, w_out=dw_out, g_post=dg_post)


_HBM = pl.BlockSpec(memory_space=pltpu.HBM)
N_DEV = 8


def _position():
    return lax.axis_index("x"), lax.axis_index("y"), lax.axis_index("c")


def _other_chips(x, y):
    return (((1 - x, y), 2 * (1 - x) + y), ((x, 1 - y), 2 * x + (1 - y)), ((1 - x, 1 - y), 2 * (1 - x) + (1 - y)))


def _remote(src, dst, send_sems, recv_sems, k, device):
    return pltpu.make_async_remote_copy(src_ref=src, dst_ref=dst, send_sem=send_sems.at[k], recv_sem=recv_sems.at[k],
                                        device_id=device, device_id_type=MESH)


def _gather_shards(shards):
    n = len(shards)

    def body(*refs):
        ins, outs = refs[:n], refs[n:2 * n]
        send_sems, recv_sems, local_sems = refs[2 * n:]
        x, y, c = _position()
        me = 2 * x + y
        chips = _other_chips(x, y)
        local = [pltpu.make_async_copy(ins[a], outs[a].at[me], local_sems.at[a]) for a in range(n)]
        for cp in local:
            cp.start()
        sends = [_remote(ins[a], outs[a].at[me], send_sems, recv_sems, 3 * a + r, (*chip, c))
                 for a in range(n) for r, (chip, _) in enumerate(chips)]
        for cp in sends:
            cp.start()
        for a in range(n):
            for r, (chip, idx) in enumerate(chips):
                _remote(ins[a], outs[a].at[idx], send_sems, recv_sems, 3 * a + r, (*chip, c)).wait_recv()
        for cp in sends:
            cp.wait_send()
        for cp in local:
            cp.wait()

    return pl.pallas_call(
        body, in_specs=[_HBM] * n, out_specs=[_HBM] * n,
        out_shape=[jax.ShapeDtypeStruct((N_CHIPS,) + s.shape, s.dtype) for s in shards],
        scratch_shapes=[pltpu.SemaphoreType.DMA((3 * n,)), pltpu.SemaphoreType.DMA((3 * n,)),
                        pltpu.SemaphoreType.DMA((n,))],
        name="gather_weights")(*shards)


def _pair_exchange(parts):
    n = len(parts)

    def body(*refs):
        ins, outs = refs[:n], refs[n:2 * n]
        send_sems, recv_sems = refs[2 * n:]
        x, y, c = _position()
        copies = [_remote(ins[a].at[1 - c], outs[a], send_sems, recv_sems, a, (x, y, 1 - c)) for a in range(n)]
        for cp in copies:
            cp.start()
        for cp in copies:
            cp.wait()

    return pl.pallas_call(
        body, in_specs=[_HBM] * n, out_specs=[_HBM] * n,
        out_shape=[jax.ShapeDtypeStruct(p.shape[1:], p.dtype) for p in parts],
        scratch_shapes=[pltpu.SemaphoreType.DMA((n,)), pltpu.SemaphoreType.DMA((n,))],
        name="grad_pair_exchange")(*parts)


def _chip_exchange(sums):
    n = len(sums)

    def body(*refs):
        ins, outs = refs[:n], refs[n:2 * n]
        send_sems, recv_sems = refs[2 * n:]
        x, y, c = _position()
        copies = [_remote(ins[a].at[idx], outs[a].at[r], send_sems, recv_sems, 3 * a + r, (*chip, c))
                  for a in range(n) for r, (chip, idx) in enumerate(_other_chips(x, y))]
        for cp in copies:
            cp.start()
        for cp in copies:
            cp.wait()

    return pl.pallas_call(
        body, in_specs=[_HBM] * n, out_specs=[_HBM] * n,
        out_shape=[jax.ShapeDtypeStruct((3,) + s.shape[1:], s.dtype) for s in sums],
        scratch_shapes=[pltpu.SemaphoreType.DMA((3 * n,)), pltpu.SemaphoreType.DMA((3 * n,))],
        name="grad_chip_exchange")(*sums)


def _pair_share(halves):
    n = len(halves)

    def body(*refs):
        ins, outs = refs[:n], refs[n:2 * n]
        send_sems, recv_sems, local_sems = refs[2 * n:]
        x, y, c = _position()
        local = [pltpu.make_async_copy(ins[a], outs[a].at[c], local_sems.at[a]) for a in range(n)]
        sends = [_remote(ins[a], outs[a].at[c], send_sems, recv_sems, a, (x, y, 1 - c)) for a in range(n)]
        for cp in local + sends:
            cp.start()
        for a in range(n):
            _remote(ins[a], outs[a].at[1 - c], send_sems, recv_sems, a, (x, y, 1 - c)).wait_recv()
        for cp in sends:
            cp.wait_send()
        for cp in local:
            cp.wait()

    return pl.pallas_call(
        body, in_specs=[_HBM] * n, out_specs=[_HBM] * n,
        out_shape=[jax.ShapeDtypeStruct((2,) + h.shape, h.dtype) for h in halves],
        scratch_shapes=[pltpu.SemaphoreType.DMA((n,)), pltpu.SemaphoreType.DMA((n,)), pltpu.SemaphoreType.DMA((n,))],
        name="grad_pair_share")(*halves)


def _small_allreduce(pack):
    rows, width = pack.shape

    def body(p_ref, o_ref, buf, send_sems, recv_sems):
        x, y, c = _position()
        me = 4 * x + 2 * y + c
        buf[me] = p_ref[...]
        peers = []
        for r in range(1, N_DEV):
            fx, fy, fc = (r >> 2) & 1, (r >> 1) & 1, r & 1
            px, py, pc = (1 - x if fx else x), (1 - y if fy else y), (1 - c if fc else c)
            peers.append(((px, py, pc), 4 * px + 2 * py + pc))
        sends = [_remote(p_ref, buf.at[me], send_sems, recv_sems, r, dev) for r, (dev, _) in enumerate(peers)]
        for cp in sends:
            cp.start()
        for r, (dev, idx) in enumerate(peers):
            _remote(p_ref, buf.at[idx], send_sems, recv_sems, r, dev).wait_recv()
        for cp in sends:
            cp.wait_send()
        acc = buf[0]
        for k in range(1, N_DEV):
            acc = acc + buf[k]
        o_ref[...] = acc

    vm = pl.BlockSpec(memory_space=pltpu.VMEM)
    return pl.pallas_call(
        body, in_specs=[vm], out_specs=vm, out_shape=jax.ShapeDtypeStruct(pack.shape, F32),
        scratch_shapes=[pltpu.VMEM((N_DEV, rows, width), F32), pltpu.SemaphoreType.DMA((N_DEV - 1,)),
                        pltpu.SemaphoreType.DMA((N_DEV - 1,))],
        name="small_allreduce")(pack)


def _row_tile(rows):
    return rows if rows <= 256 else 256


def _pair_add(part, recv, c_idx, name):
    _, nj, rh, cols = part.shape
    tr = _row_tile(rh)

    def body(c_ref, p_ref, r_ref, o_ref):
        o_ref[...] = p_ref[...] + r_ref[...]

    blk = pl.BlockSpec((None, tr, cols), lambda j, i, c_ref: (j, i, 0))
    grid_spec = pltpu.PrefetchScalarGridSpec(
        num_scalar_prefetch=1, grid=(nj, rh // tr),
        in_specs=[pl.BlockSpec((None, None, tr, cols), lambda j, i, c_ref: (c_ref[0], j, i, 0)), blk],
        out_specs=blk)
    return pl.pallas_call(body, grid_spec=grid_spec, out_shape=jax.ShapeDtypeStruct(recv.shape, F32),
                          name=name, compiler_params=_params())(c_idx, part, recv)


def _chip_add(sums, recv, chip_idx, name):
    _, rh, cols = sums.shape
    tr = _row_tile(rh)

    def body(j_ref, s_ref, r_ref, o_ref):
        o_ref[...] = ((s_ref[...] + r_ref[0]) + r_ref[1]) + r_ref[2]

    grid_spec = pltpu.PrefetchScalarGridSpec(
        num_scalar_prefetch=1, grid=(rh // tr,),
        in_specs=[pl.BlockSpec((None, tr, cols), lambda i, j_ref: (j_ref[0], i, 0)),
                  pl.BlockSpec((3, tr, cols), lambda i, j_ref: (0, i, 0))],
        out_specs=pl.BlockSpec((tr, cols), lambda i, j_ref: (i, 0)))
    return pl.pallas_call(body, grid_spec=grid_spec, out_shape=jax.ShapeDtypeStruct((rh, cols), F32),
                          name=name, compiler_params=_params())(chip_idx, sums, recv)


def _adamw(w, g, m, v, name):
    rows, cols = w.shape
    tr = _row_tile(rows)
    assert rows % tr == 0

    def body(w_ref, g_ref, m_ref, v_ref, d_ref, mo_ref, vo_ref):
        gv = g_ref[...]
        m_new = ADAM_B1 * m_ref[...] + (1.0 - ADAM_B1) * gv
        v_new = ADAM_B2 * v_ref[...] + (1.0 - ADAM_B2) * jnp.square(gv)
        m_hat = m_new / (1.0 - ADAM_B1 ** ADAM_STEP)
        v_hat = v_new / (1.0 - ADAM_B2 ** ADAM_STEP)
        d_ref[...] = -ADAM_LR * (m_hat / (jnp.sqrt(v_hat) + ADAM_EPS) + ADAM_WD * w_ref[...])
        mo_ref[...] = m_new
        vo_ref[...] = v_new

    blk = pl.BlockSpec((tr, cols), lambda i: (i, 0))
    shp = jax.ShapeDtypeStruct((rows, cols), F32)
    return pl.pallas_call(body, grid=(rows // tr,), in_specs=[blk] * 4, out_specs=[blk] * 3,
                          out_shape=[shp] * 3, name=name, compiler_params=_params())(w, g, m, v)


def _half_major(a, n_rows, n_cols):
    r, c = a.shape
    if n_cols > 1:
        return a.reshape(2, r // 2, n_cols, c // n_cols).transpose(0, 2, 1, 3)
    return a.reshape(n_rows, 2, r // n_rows // 2, c).transpose(1, 0, 2, 3)


def kernel(x, mem, g_pre, w_in, w_conv, attn_sink, g_mem, w_mem_kv, w_up_a, w_up_b, w_up_m, w_out, g_post, loss_target, m_g_pre, m_w_in, m_w_conv, m_attn_sink, m_g_mem, m_w_mem_kv, m_w_up_a, m_w_up_b, m_w_up_m, m_w_out, m_g_post, v_g_pre, v_w_in, v_w_conv, v_attn_sink, v_g_mem, v_w_mem_kv, v_w_up_a, v_w_up_b, v_w_up_m, v_w_out, v_g_post):
    xi, yi, ci = _position()
    chip = 2 * xi + yi
    chip_idx = jnp.reshape(chip, (1,)).astype(jnp.int32)
    c_idx = jnp.reshape(ci, (1,)).astype(jnp.int32)

    up_shard = jnp.stack([w_up_a[0], w_up_b[0], w_up_m[0]]).astype(BF16)
    conv_shard = jnp.pad(w_conv[0], ((0, 5), (0, 0)))
    g_in, g_kv, g_up, g_out, g_conv = _gather_shards(
        [w_in[0].astype(BF16), w_mem_kv[0].astype(BF16), up_shard, w_out[0].astype(BF16), conv_shard])
    w_in_full = jnp.concatenate([g_in[j] for j in range(N_CHIPS)], axis=1)
    w_in_p = jnp.concatenate([w_in_full[:, a:b] for a, b in PERM_SEGS], axis=1)
    w_kv_full = g_kv.reshape(D_MODEL, 2 * MEM_WIDTH)
    w_out_full = g_out.reshape(D_MODEL, D_MODEL)
    ups = [jnp.concatenate([g_up[j, k] for j in range(N_CHIPS)], axis=1) for k in range(3)]
    w_conv_full = jnp.concatenate([g_conv[j, :3] for j in range(N_CHIPS)], axis=1)

    g = _local_grads(x[0], mem[0], loss_target[0], g_pre, w_in_p, w_conv_full, attn_sink, g_mem, w_kv_full,
                     ups[0], ups[1], ups[2], w_out_full, g_post)

    zeros512 = jnp.zeros((1, D_MODEL - A_WIDTH), F32)
    conv_rows = [jnp.concatenate([g["w_conv"][k:k + 1], zeros512], axis=1) for k in range(3)]
    sink_row = jnp.pad(g["sink"][:, 0].reshape(1, N_Q_HEADS), ((0, 0), (0, D_MODEL - N_Q_HEADS)))
    loss_row = jnp.pad(g["loss"], ((0, 0), (0, D_MODEL - LANES)))
    pack = jnp.concatenate([g["g_pre"], g["g_mem"], g["g_post"]] + conv_rows + [sink_row, loss_row], axis=0)
    red = _small_allreduce(pack)
    loss = red[7, 0]
    small_grads = dict(
        g_pre=red[0:1], g_mem=red[1:2], g_post=red[2:3], attn_sink=red[6:7, :N_Q_HEADS],
        w_conv=lax.dynamic_slice(red[3:6, :A_WIDTH], (0, chip * LANES), (3, LANES)))

    dw_in = jnp.concatenate([g["w_in_p"][:, a:b] for a, b in UNPERM_SEGS], axis=1)
    up_stack = jnp.concatenate([g["w_up_a"], g["w_up_b"], g["w_up_m"]], axis=0)
    up_parts = (up_stack.reshape(3, A_WIDTH, N_CHIPS, D_MODEL // N_CHIPS).transpose(2, 0, 1, 3)
                .reshape(N_CHIPS, 2, 3 * A_WIDTH // 2, D_MODEL // N_CHIPS).transpose(1, 0, 2, 3))
    parts = [_half_major(dw_in, 1, N_CHIPS), _half_major(g["w_kv"], N_CHIPS, 1), up_parts,
             _half_major(g["w_out"], N_CHIPS, 1)]
    names = ["w_in", "w_kv", "w_up", "w_out"]
    recv = _pair_exchange(parts)
    sums = [_pair_add(p, r, c_idx, "pair_add_" + nm) for p, r, nm in zip(parts, recv, names)]
    recv3 = _chip_exchange(sums)
    halves = [_chip_add(s, r, chip_idx, "chip_add_" + nm) for s, r, nm in zip(sums, recv3, names)]
    full = _pair_share(halves)
    gw_up = full[2].reshape(3, A_WIDTH, D_MODEL // N_CHIPS)
    grads = dict(small_grads, w_in=full[0].reshape(D_MODEL, IN_WIDTH // N_CHIPS),
                 w_mem_kv=full[1].reshape(D_MODEL // N_CHIPS, 2 * MEM_WIDTH),
                 w_up_a=gw_up[0], w_up_b=gw_up[1], w_up_m=gw_up[2],
                 w_out=full[3].reshape(D_MODEL // N_CHIPS, D_MODEL))

    weights = dict(g_pre=g_pre, w_in=w_in, w_conv=w_conv, attn_sink=attn_sink, g_mem=g_mem, w_mem_kv=w_mem_kv,
                   w_up_a=w_up_a, w_up_b=w_up_b, w_up_m=w_up_m, w_out=w_out, g_post=g_post)
    m_in = dict(g_pre=m_g_pre, w_in=m_w_in, w_conv=m_w_conv, attn_sink=m_attn_sink, g_mem=m_g_mem,
                w_mem_kv=m_w_mem_kv, w_up_a=m_w_up_a, w_up_b=m_w_up_b, w_up_m=m_w_up_m, w_out=m_w_out,
                g_post=m_g_post)
    v_in = dict(g_pre=v_g_pre, w_in=v_w_in, w_conv=v_w_conv, attn_sink=v_attn_sink, g_mem=v_g_mem,
                w_mem_kv=v_w_mem_kv, w_up_a=v_w_up_a, w_up_b=v_w_up_b, w_up_m=v_w_up_m, w_out=v_w_out,
                g_post=v_g_post)
    out_g, out_d, out_m, out_v = [], [], [], []
    for nm in ("g_pre", "w_in", "w_conv", "attn_sink", "g_mem", "w_mem_kv", "w_up_a", "w_up_b", "w_up_m", "w_out",
               "g_post"):
        shape = weights[nm].shape
        two_d = shape[-2:]
        gr = grads[nm].reshape(two_d)
        d, m_new, v_new = _adamw(weights[nm].reshape(two_d), gr, m_in[nm].reshape(two_d), v_in[nm].reshape(two_d),
                                 "adamw_" + nm)
        out_g.append(gr.reshape(shape))
        out_d.append(d.reshape(shape))
        out_m.append(m_new.reshape(shape))
        out_v.append(v_new.reshape(shape))
    return (loss, g["grad_x"].reshape(x.shape), *out_g, *out_d, *out_m, *out_v)
```

```python
import functools

import jax
import jax.numpy as jnp
from jax import lax
from jax.experimental import pallas as pl
from jax.experimental.pallas import tpu as pltpu

F32 = jnp.float32
BF16 = jnp.bfloat16
MESH = pl.DeviceIdType.MESH

D_MODEL = 1024
EPS = 1e-6
A_WIDTH = 512
HEAD_DIM = 64
N_Q_HEADS = 8
WINDOW_BLOCK = 128
ROPE_THETA = 500000.0
ROT_DIM = 16
MEM_HEADS = 4
MEM_HEAD_DIM = 128
MEM_WIDTH = 512
IN_WIDTH = 7424
N_CHIPS = 4
LANES = 128
HALF_LANES = 64

PERM_SEGS = ((0, 2560), (2816, 3328), (4352, 7424), (3328, 4352), (2560, 2816))
UNPERM_SEGS = ((0, 2560), (7168, 7424), (2560, 3072), (6144, 7168), (3072, 6144))
COL_A, W_A = 0, 2048
COL_B, W_B = 2, 1024
COL_G, W_G = 1, 3072
COL_M, W_M = 6, 1024
COL_KV, W_KV = 28, 256

ADAM_LR = 0.001
ADAM_B1 = 0.9
ADAM_B2 = 0.999
ADAM_EPS = 1e-08
ADAM_WD = 0.01
ADAM_STEP = 10

VMEM_LIMIT_BYTES = 48 * 1024 * 1024


def _params(**kw):
    return pltpu.CompilerParams(vmem_limit_bytes=VMEM_LIMIT_BYTES, **kw)


def _sigmoid(v):
    return jax.nn.sigmoid(v)


_DIMS = {"nn": (((1,), (0,)), ((), ())), "nt": (((1,), (1,)), ((), ())), "tn": (((0,), (0,)), ((), ()))}


def _matmul(a, b, *, mode, out_dtype, tm, tn, tk, name, j_outer=False):
    if mode == "nn":
        (m, k), (_, n) = a.shape, b.shape
    elif mode == "nt":
        (m, k), (n, _) = a.shape, b.shape
    else:
        (k, m), (_, n) = a.shape, b.shape
    tm, tn, tk = min(tm, m), min(tn, n), min(tk, k)
    assert m % tm == 0 and n % tn == 0 and k % tk == 0
    ni, nj, nk = m // tm, n // tn, k // tk
    dims = _DIMS[mode]

    def ij(g0, g1):
        return (g1, g0) if j_outer else (g0, g1)

    if mode == "nn":
        a_spec = pl.BlockSpec((tm, tk), lambda g0, g1, kk: (ij(g0, g1)[0], kk))
        b_spec = pl.BlockSpec((tk, tn), lambda g0, g1, kk: (kk, ij(g0, g1)[1]))
    elif mode == "nt":
        a_spec = pl.BlockSpec((tm, tk), lambda g0, g1, kk: (ij(g0, g1)[0], kk))
        b_spec = pl.BlockSpec((tn, tk), lambda g0, g1, kk: (ij(g0, g1)[1], kk))
    else:
        a_spec = pl.BlockSpec((tk, tm), lambda g0, g1, kk: (kk, ij(g0, g1)[0]))
        b_spec = pl.BlockSpec((tk, tn), lambda g0, g1, kk: (kk, ij(g0, g1)[1]))
    o_spec = pl.BlockSpec((tm, tn), lambda g0, g1, kk: ij(g0, g1))

    def part(a_ref, b_ref):
        return lax.dot_general(a_ref[...].astype(BF16), b_ref[...].astype(BF16), dims,
                               preferred_element_type=F32)

    if nk == 1:
        def body(a_ref, b_ref, o_ref):
            o_ref[...] = part(a_ref, b_ref).astype(out_dtype)
        scratch = []
    else:
        def body(a_ref, b_ref, o_ref, acc_ref):
            kk = pl.program_id(2)

            @pl.when(kk == 0)
            def _():
                acc_ref[...] = part(a_ref, b_ref)

            @pl.when(kk > 0)
            def _():
                acc_ref[...] += part(a_ref, b_ref)

            @pl.when(kk == nk - 1)
            def _():
                o_ref[...] = acc_ref[...].astype(out_dtype)
        scratch = [pltpu.VMEM((tm, tn), F32)]

    grid = (nj, ni, nk) if j_outer else (ni, nj, nk)
    return pl.pallas_call(
        body, grid=grid, in_specs=[a_spec, b_spec], out_specs=o_spec,
        out_shape=jax.ShapeDtypeStruct((m, n), out_dtype), scratch_shapes=scratch,
        name=name, compiler_params=_params())(a, b)


def _rmsnorm_fwd(x, g, *, name):
    s, d = x.shape
    ts = min(512, s)

    def body(x_ref, g_ref, o_ref):
        xv = x_ref[...]
        r = lax.rsqrt(jnp.mean(xv * xv, axis=-1, keepdims=True) + EPS)
        o_ref[...] = ((xv * r) * g_ref[...]).astype(BF16)

    return pl.pallas_call(
        body, grid=(s // ts,),
        in_specs=[pl.BlockSpec((ts, d), lambda i: (i, 0)), pl.BlockSpec((1, d), lambda i: (0, 0))],
        out_specs=pl.BlockSpec((ts, d), lambda i: (i, 0)),
        out_shape=jax.ShapeDtypeStruct((s, d), BF16), name=name, compiler_params=_params())(x, g)


def _rmsnorm_bwd(dh, x, g, res, *, name):
    s, d = x.shape
    ts = min(256, s)

    def body(dh_ref, x_ref, g_ref, res_ref, dx_ref, dg_ref):
        xv = x_ref[...]
        r = lax.rsqrt(jnp.mean(xv * xv, axis=-1, keepdims=True) + EPS)
        xh = xv * r
        dhv = dh_ref[...]
        part = jnp.sum(dhv * xh, axis=0, keepdims=True)

        @pl.when(pl.program_id(0) == 0)
        def _():
            dg_ref[...] = part

        @pl.when(pl.program_id(0) > 0)
        def _():
            dg_ref[...] += part

        dxh = dhv * g_ref[...]
        dx_ref[...] = res_ref[...] + r * (dxh - xh * jnp.mean(dxh * xh, axis=-1, keepdims=True))

    row = pl.BlockSpec((ts, d), lambda i: (i, 0))
    vec = pl.BlockSpec((1, d), lambda i: (0, 0))
    return pl.pallas_call(
        body, grid=(s // ts,), in_specs=[row, row, vec, row], out_specs=[row, vec],
        out_shape=[jax.ShapeDtypeStruct((s, d), F32), jax.ShapeDtypeStruct((1, d), F32)],
        name=name, compiler_params=_params())(dh, x, g, res)


def _post_loss(out, x, tgt, g):
    s, d = out.shape
    ts = 256

    def body(o_ref, x_ref, t_ref, g_ref, do_ref, dy_ref, dg_ref, loss_ref):
        ov = o_ref[...]
        r = lax.rsqrt(jnp.mean(ov * ov, axis=-1, keepdims=True) + EPS)
        nh = ov * r
        gv = g_ref[...]
        e = (x_ref[...] + nh * gv) - t_ref[...]
        lpart = 0.5 * jnp.sum(jnp.mean(e * e, axis=-1, keepdims=True), axis=0, keepdims=True)
        dy = e * (1.0 / d)
        dgp = jnp.sum(dy * nh, axis=0, keepdims=True)

        @pl.when(pl.program_id(0) == 0)
        def _():
            dg_ref[...] = dgp
            loss_ref[...] = jnp.broadcast_to(lpart, loss_ref.shape)

        @pl.when(pl.program_id(0) > 0)
        def _():
            dg_ref[...] += dgp
            loss_ref[...] += jnp.broadcast_to(lpart, loss_ref.shape)

        dn = dy * gv
        dy_ref[...] = dy
        do_ref[...] = (r * (dn - nh * jnp.mean(dn * nh, axis=-1, keepdims=True))).astype(BF16)

    row = pl.BlockSpec((ts, d), lambda i: (i, 0))
    vec = pl.BlockSpec((1, d), lambda i: (0, 0))
    lsp = pl.BlockSpec((1, LANES), lambda i: (0, 0))
    return pl.pallas_call(
        body, grid=(s // ts,), in_specs=[row, row, row, vec], out_specs=[row, row, vec, lsp],
        out_shape=[jax.ShapeDtypeStruct((s, d), BF16), jax.ShapeDtypeStruct((s, d), F32),
                   jax.ShapeDtypeStruct((1, d), F32), jax.ShapeDtypeStruct((1, LANES), F32)],
        name="post_loss", compiler_params=_params())(out, x, tgt, g)


def _merge_fwd(ua, ub, um, proj):
    s, d = ua.shape
    ts = 256

    def body(ua_ref, ub_ref, um_ref, g_ref, o_ref):
        gl = g_ref[...]
        o_ref[...] = (_sigmoid(gl[:, :d]) * ua_ref[...] + _sigmoid(gl[:, d:2 * d]) * ub_ref[...]
                      + _sigmoid(gl[:, 2 * d:]) * um_ref[...]).astype(BF16)

    row = pl.BlockSpec((ts, d), lambda i: (i, 0))
    return pl.pallas_call(
        body, grid=(s // ts,),
        in_specs=[row, row, row, pl.BlockSpec((ts, W_G), lambda i: (i, COL_G))], out_specs=row,
        out_shape=jax.ShapeDtypeStruct((s, d), BF16), name="merge_fwd",
        compiler_params=_params())(ua, ub, um, proj)


def _merge_bwd(dm, ua, ub, um, proj):
    s, d = ua.shape
    ts = 256

    def body(dm_ref, ua_ref, ub_ref, um_ref, g_ref, dua_ref, dub_ref, dum_ref, dp_ref):
        dmv = dm_ref[...]
        gl = g_ref[...]
        for k, (u_ref, du_ref) in enumerate(((ua_ref, dua_ref), (ub_ref, dub_ref), (um_ref, dum_ref))):
            sg = _sigmoid(gl[:, k * d:(k + 1) * d])
            du_ref[...] = (sg * dmv).astype(BF16)
            dp_ref[:, k * d:(k + 1) * d] = ((dmv * u_ref[...]) * (sg * (1.0 - sg))).astype(BF16)

    row = pl.BlockSpec((ts, d), lambda i: (i, 0))
    gsp = pl.BlockSpec((ts, W_G), lambda i: (i, COL_G))
    return pl.pallas_call(
        body, grid=(s // ts,),
        in_specs=[row, row, row, row, gsp],
        out_specs=[row, row, row, gsp],
        out_shape=[jax.ShapeDtypeStruct((s, d), BF16)] * 3 + [jax.ShapeDtypeStruct((s, IN_WIDTH), BF16)],
        name="merge_bwd", compiler_params=_params())(dm, ua, ub, um, proj)


def _conv_core(blk, prev, nxt, w, i, last, ts):
    c = A_WIDTH
    ab, ac, ax, az = blk[:, :c], blk[:, c:2 * c], blk[:, 2 * c:3 * c], blk[:, 3 * c:]
    cu = ac * ax
    cu_prev = (prev[7:8, c:2 * c] * prev[7:8, 2 * c:3 * c]) * jnp.where(i > 0, 1.0, 0.0)
    cu_next = (nxt[0:1, c:2 * c] * nxt[0:1, 2 * c:3 * c]) * jnp.where(i < last, 1.0, 0.0)
    row = lax.broadcasted_iota(jnp.int32, (ts, c), 0)
    cm1 = jnp.where(row == 0, cu_prev, pltpu.roll(cu, 1, 0))
    cp1 = jnp.where(row == ts - 1, cu_next, pltpu.roll(cu, ts - 1, 0))
    yc = cm1 * w[0:1] + cu * w[1:2] + cp1 * w[2:3]
    return ab, ac, ax, az, cu, cm1, cp1, yc, row


def _halo_specs(ts, width, col, nblk8):
    prev = pl.BlockSpec((8, width), lambda i: (jnp.maximum(i * (ts // 8) - 1, 0), col))
    nxt = pl.BlockSpec((8, width), lambda i: (jnp.minimum((i + 1) * (ts // 8), nblk8 - 1), col))
    return prev, nxt


def _conv_fwd(proj, w_conv):
    s = proj.shape[0]
    ts = 256
    last = s // ts - 1

    def body(a_ref, ap_ref, an_ref, w_ref, ya_ref):
        i = pl.program_id(0)
        ab, _, _, az, _, _, _, yc, _ = _conv_core(a_ref[...], ap_ref[...], an_ref[...], w_ref[...], i, last, ts)
        ya_ref[...] = ((ab * yc) * (az * _sigmoid(az))).astype(BF16)

    prev, nxt = _halo_specs(ts, W_A, COL_A, s // 8)
    return pl.pallas_call(
        body, grid=(s // ts,),
        in_specs=[pl.BlockSpec((ts, W_A), lambda i: (i, COL_A)), prev, nxt,
                  pl.BlockSpec((3, A_WIDTH), lambda i: (0, 0))],
        out_specs=pl.BlockSpec((ts, A_WIDTH), lambda i: (i, 0)),
        out_shape=jax.ShapeDtypeStruct((s, A_WIDTH), BF16), name="conv_fwd",
        compiler_params=_params())(proj, proj, proj, w_conv)


def _conv_bwd(proj, w_conv, dya, dproj):
    s = proj.shape[0]
    ts = 256
    last = s // ts - 1
    c = A_WIDTH

    def body(a_ref, ap_ref, an_ref, w_ref, d_ref, dp_ref, dn_ref, _, dproj_ref, dw_ref):
        i = pl.program_id(0)
        w = w_ref[...]
        prev, nxt = ap_ref[...], an_ref[...]
        ab, ac, ax, az, cu, cm1, cp1, yc, row = _conv_core(a_ref[...], prev, nxt, w, i, last, ts)
        sg = _sigmoid(az)
        sz = az * sg
        dya_v = d_ref[...]
        dyc = dya_v * sz * ab
        dproj_ref[:, :c] = (dya_v * sz * yc).astype(BF16)
        dproj_ref[:, 3 * c:] = (dya_v * (ab * yc) * (sg * (1.0 + az * (1.0 - sg)))).astype(BF16)

        def halo_dyc(a_row, d_row):
            azr = a_row[:, 3 * c:]
            return d_row * (azr * _sigmoid(azr)) * a_row[:, :c]

        dyc_prev = halo_dyc(prev[7:8], dp_ref[...][7:8]) * jnp.where(i > 0, 1.0, 0.0)
        dyc_next = halo_dyc(nxt[0:1], dn_ref[...][0:1]) * jnp.where(i < last, 1.0, 0.0)
        dyc_m1 = jnp.where(row == 0, dyc_prev, pltpu.roll(dyc, 1, 0))
        dyc_p1 = jnp.where(row == ts - 1, dyc_next, pltpu.roll(dyc, ts - 1, 0))
        dcu = dyc_p1 * w[0:1] + dyc * w[1:2] + dyc_m1 * w[2:3]
        dproj_ref[:, c:2 * c] = (dcu * ax).astype(BF16)
        dproj_ref[:, 2 * c:3 * c] = (dcu * ac).astype(BF16)
        dw = [jnp.sum(dyc * t, axis=0, keepdims=True) for t in (cm1, cu, cp1)]

        @pl.when(i == 0)
        def _():
            for k in range(3):
                dw_ref[k:k + 1, :] = dw[k]

        @pl.when(i > 0)
        def _():
            for k in range(3):
                dw_ref[k:k + 1, :] += dw[k]

    prev, nxt = _halo_specs(ts, W_A, COL_A, s // 8)
    dprev, dnxt = _halo_specs(ts, A_WIDTH, 0, s // 8)
    return pl.pallas_call(
        body, grid=(s // ts,),
        in_specs=[pl.BlockSpec((ts, W_A), lambda i: (i, COL_A)), prev, nxt,
                  pl.BlockSpec((3, A_WIDTH), lambda i: (0, 0)),
                  pl.BlockSpec((ts, A_WIDTH), lambda i: (i, 0)), dprev, dnxt,
                  pl.BlockSpec(memory_space=pl.ANY)],
        out_specs=[pl.BlockSpec((ts, W_A), lambda i: (i, COL_A)), pl.BlockSpec((3, A_WIDTH), lambda i: (0, 0))],
        out_shape=[jax.ShapeDtypeStruct(dproj.shape, BF16), jax.ShapeDtypeStruct((3, A_WIDTH), F32)],
        input_output_aliases={7: 0}, name="conv_bwd",
        compiler_params=_params())(proj, proj, proj, w_conv, dya, dya, dya, dproj)


def _rope_tables(s):
    half = ROT_DIM // 2
    inv_freq = jnp.power(jnp.float32(ROPE_THETA), -jnp.arange(half, dtype=F32) * (2.0 / ROT_DIM))
    ang = jnp.arange(s).astype(F32)[:, None] * inv_freq[None, :]
    cos, sin = jnp.cos(ang), jnp.sin(ang)
    pad = jnp.zeros((s, HEAD_DIM - ROT_DIM), F32)
    c = jnp.concatenate([cos, cos, pad + 1.0], axis=1)
    s1 = jnp.concatenate([-sin, jnp.zeros_like(sin), pad], axis=1)
    s2 = jnp.concatenate([jnp.zeros_like(sin), sin, pad], axis=1)
    return jnp.concatenate([c, c, s1, s1, s2, s2], axis=1)


def _rope(t, tab):
    return (t * tab[:, :LANES] + pltpu.roll(t, LANES - 8, 1) * tab[:, LANES:2 * LANES]
            + pltpu.roll(t, 8, 1) * tab[:, 2 * LANES:])


def _rope_transpose(dt, tab):
    return (dt * tab[:, :LANES] + pltpu.roll(dt * tab[:, LANES:2 * LANES], 8, 1)
            + pltpu.roll(dt * tab[:, 2 * LANES:], LANES - 8, 1))


def _rope_kv(proj, tab):
    s = proj.shape[0]
    nb = s // WINDOW_BLOCK

    def body(kv_ref, t_ref, k_ref, v_ref):
        j = pl.program_id(0)
        inside = jnp.where((j > 0) & (j <= nb), 1.0, 0.0)
        kv = kv_ref[...]
        k_ref[...] = (_rope(kv[:, :LANES], t_ref[...]) * inside).astype(BF16)
        v_ref[...] = (kv[:, LANES:] * inside).astype(BF16)

    def src(j):
        return jnp.clip(j - 1, 0, nb - 1)

    o_spec = pl.BlockSpec((WINDOW_BLOCK, LANES), lambda j: (j, 0))
    shp = jax.ShapeDtypeStruct((s + 2 * WINDOW_BLOCK, LANES), BF16)
    return pl.pallas_call(
        body, grid=(nb + 2,),
        in_specs=[pl.BlockSpec((WINDOW_BLOCK, W_KV), lambda j: (src(j), COL_KV)),
                  pl.BlockSpec((WINDOW_BLOCK, 3 * LANES), lambda j: (src(j), 0))],
        out_specs=[o_spec, o_spec], out_shape=[shp, shp], name="rope_kv",
        compiler_params=_params())(proj, tab)


def _rope_kv_bwd(dkpad, dvpad, tab, dproj):
    s = tab.shape[0]
    nb = s // WINDOW_BLOCK

    def body(dk_ref, dv_ref, t_ref, _, dp_ref):
        dp_ref[:, :LANES] = _rope_transpose(dk_ref[...], t_ref[...]).astype(BF16)
        dp_ref[:, LANES:] = dv_ref[...].astype(BF16)

    pad_spec = pl.BlockSpec((WINDOW_BLOCK, LANES), lambda j: (j + 1, 0))
    return pl.pallas_call(
        body, grid=(nb,),
        in_specs=[pad_spec, pad_spec, pl.BlockSpec((WINDOW_BLOCK, 3 * LANES), lambda j: (j, 0)),
                  pl.BlockSpec(memory_space=pl.ANY)],
        out_specs=pl.BlockSpec((WINDOW_BLOCK, W_KV), lambda j: (j, COL_KV)),
        out_shape=jax.ShapeDtypeStruct(dproj.shape, BF16), input_output_aliases={3: 0},
        name="rope_kv_bwd", compiler_params=_params())(dkpad, dvpad, tab, dproj)


def _window_operands(k_ref, v_ref, n, lo):
    start = pl.multiple_of(n * WINDOW_BLOCK, WINDOW_BLOCK)
    kw = k_ref[pl.ds(start, 3 * WINDOW_BLOCK), :].astype(F32)
    vw = v_ref[pl.ds(start, 3 * WINDOW_BLOCK), :].astype(F32)
    kr, vr = pltpu.roll(kw, HALF_LANES, 1), pltpu.roll(vw, HALF_LANES, 1)
    k2 = (jnp.where(lo, kw, kr).astype(BF16), jnp.where(lo, kr, kw).astype(BF16))
    v2 = (jnp.where(lo, vw, vr).astype(BF16), jnp.where(lo, vr, vw).astype(BF16))
    return k2, v2


def _window_mask(n, s):
    wb = WINDOW_BLOCK
    qi = lax.broadcasted_iota(jnp.int32, (wb, 3 * wb), 0)
    kj = lax.broadcasted_iota(jnp.int32, (wb, 3 * wb), 1)
    kpos = kj + (n - 1) * wb
    return (kj >= qi) & (kj <= qi + 2 * wb) & (kpos >= 0) & (kpos < s)


def _head_probs(qh, k2g, valid, sink):
    sc = lax.dot_general(qh, k2g, _DIMS["nt"], preferred_element_type=F32) * (HEAD_DIM ** -0.5)
    sc = jnp.where(valid, sc, -jnp.inf)
    m = jnp.maximum(jnp.max(sc, axis=1, keepdims=True), sink)
    e = jnp.exp(sc - m)
    es = jnp.exp(sink - m)
    inv = 1.0 / (jnp.sum(e, axis=1, keepdims=True) + es)
    return e * inv, es * inv


def _swa_fwd(proj, kpad, vpad, tab, sink):
    s = proj.shape[0]
    wb = WINDOW_BLOCK

    def body(b_ref, k_ref, v_ref, t_ref, sink_ref, o_ref, y_ref):
        n = pl.program_id(0)
        lane = lax.broadcasted_iota(jnp.int32, (wb, LANES), 1)
        lo = lane < HALF_LANES
        lo_w = lax.broadcasted_iota(jnp.int32, (3 * wb, LANES), 1) < HALF_LANES
        k2, v2 = _window_operands(k_ref, v_ref, n, lo_w)
        valid = _window_mask(n, s)
        tab_v = t_ref[...]
        for p in range(N_Q_HEADS // 2):
            cols = slice(p * LANES, (p + 1) * LANES)
            qr = _rope(b_ref[:, cols], tab_v)
            g = p // 2
            outs = []
            for half in range(2):
                keep = lo if half == 0 else ~lo
                qh = jnp.where(keep, qr, 0.0).astype(BF16)
                prob, _ = _head_probs(qh, k2[g], valid, sink_ref[0, 2 * p + half])
                outs.append(jnp.dot(prob.astype(BF16), v2[g], preferred_element_type=F32))
            op = jnp.where(lo, outs[0], outs[1])
            o_ref[:, cols] = op
            zp = b_ref[:, A_WIDTH + p * LANES:A_WIDTH + (p + 1) * LANES]
            y_ref[:, cols] = (op * (zp * _sigmoid(zp))).astype(BF16)

    pad_spec = pl.BlockSpec((s + 2 * wb, LANES), lambda n: (0, 0))
    o_spec = pl.BlockSpec((wb, A_WIDTH), lambda n: (n, 0))
    return pl.pallas_call(
        body, grid=(s // wb,),
        in_specs=[pl.BlockSpec((wb, W_B), lambda n: (n, COL_B)), pad_spec, pad_spec,
                  pl.BlockSpec((wb, 3 * LANES), lambda n: (n, 0)),
                  pl.BlockSpec(memory_space=pltpu.SMEM)],
        out_specs=[o_spec, o_spec],
        out_shape=[jax.ShapeDtypeStruct((s, A_WIDTH), F32), jax.ShapeDtypeStruct((s, A_WIDTH), BF16)],
        name="swa_fwd", compiler_params=_params())(proj, kpad, vpad, tab, sink)


def _swa_bwd(proj, kpad, vpad, tab, sink, o_attn, dyb, dproj):
    s = proj.shape[0]
    wb = WINDOW_BLOCK
    scale = HEAD_DIM ** -0.5

    def body(b_ref, k_ref, v_ref, t_ref, sink_ref, o_ref, dy_ref, _, dp_ref, dk_ref, dv_ref, ds_ref):
        n = pl.program_id(0)

        @pl.when(n == 0)
        def _():
            dk_ref[...] = jnp.zeros_like(dk_ref)
            dv_ref[...] = jnp.zeros_like(dv_ref)
            ds_ref[...] = jnp.zeros_like(ds_ref)

        lane = lax.broadcasted_iota(jnp.int32, (wb, LANES), 1)
        lo = lane < HALF_LANES
        lo_w = lax.broadcasted_iota(jnp.int32, (3 * wb, LANES), 1) < HALF_LANES
        k2, v2 = _window_operands(k_ref, v_ref, n, lo_w)
        valid = _window_mask(n, s)
        tab_v = t_ref[...]
        dk2 = [jnp.zeros((3 * wb, LANES), F32), jnp.zeros((3 * wb, LANES), F32)]
        dv2 = [jnp.zeros((3 * wb, LANES), F32), jnp.zeros((3 * wb, LANES), F32)]
        for p in range(N_Q_HEADS // 2):
            cols = slice(p * LANES, (p + 1) * LANES)
            zcols = slice(A_WIDTH + p * LANES, A_WIDTH + (p + 1) * LANES)
            qr = _rope(b_ref[:, cols], tab_v)
            zp = b_ref[:, zcols]
            sg = _sigmoid(zp)
            op = o_ref[:, cols]
            dyp = dy_ref[:, cols]
            do_p = dyp * (zp * sg)
            dp_ref[:, zcols] = (dyp * op * (sg * (1.0 + zp * (1.0 - sg)))).astype(BF16)
            g = p // 2
            dqs = []
            for half in range(2):
                h = 2 * p + half
                keep = lo if half == 0 else ~lo
                qh = jnp.where(keep, qr, 0.0).astype(BF16)
                prob, psink = _head_probs(qh, k2[g], valid, sink_ref[0, h])
                doh = jnp.where(keep, do_p, 0.0)
                delta = jnp.sum(doh * op, axis=1, keepdims=True)
                dohb = doh.astype(BF16)
                dprob = lax.dot_general(dohb, v2[g], _DIMS["nt"], preferred_element_type=F32)
                dsc = (prob * (dprob - delta)).astype(BF16)
                dsink = -jnp.sum(psink * delta, axis=0, keepdims=True)
                ds_ref[h:h + 1, :] += jnp.broadcast_to(dsink, (1, LANES))
                dqs.append(jnp.dot(dsc, k2[g], preferred_element_type=F32) * scale)
                dk2[g] = dk2[g] + lax.dot_general(dsc, qh, _DIMS["tn"], preferred_element_type=F32) * scale
                dv2[g] = dv2[g] + lax.dot_general(prob.astype(BF16), dohb, _DIMS["tn"],
                                                  preferred_element_type=F32)
            dqr = jnp.where(lo, dqs[0], dqs[1])
            dp_ref[:, cols] = _rope_transpose(dqr, tab_v).astype(BF16)
        dks = [t + pltpu.roll(t, HALF_LANES, 1) for t in dk2]
        dvs = [t + pltpu.roll(t, HALF_LANES, 1) for t in dv2]
        start = pl.multiple_of(n * wb, wb)
        dk_ref[pl.ds(start, 3 * wb), :] += jnp.where(lo_w, dks[0], dks[1])
        dv_ref[pl.ds(start, 3 * wb), :] += jnp.where(lo_w, dvs[0], dvs[1])

    pad_spec = pl.BlockSpec((s + 2 * wb, LANES), lambda n: (0, 0))
    blk = pl.BlockSpec((wb, A_WIDTH), lambda n: (n, 0))
    bsp = pl.BlockSpec((wb, W_B), lambda n: (n, COL_B))
    pad_shape = jax.ShapeDtypeStruct((s + 2 * wb, LANES), F32)
    return pl.pallas_call(
        body, grid=(s // wb,),
        in_specs=[bsp, pad_spec, pad_spec, pl.BlockSpec((wb, 3 * LANES), lambda n: (n, 0)),
                  pl.BlockSpec(memory_space=pltpu.SMEM), blk, blk, pl.BlockSpec(memory_space=pl.ANY)],
        out_specs=[bsp, pad_spec, pad_spec, pl.BlockSpec((8, LANES), lambda n: (0, 0))],
        out_shape=[jax.ShapeDtypeStruct(dproj.shape, BF16), pad_shape, pad_shape,
                   jax.ShapeDtypeStruct((8, LANES), F32)],
        input_output_aliases={7: 0}, name="swa_bwd",
        compiler_params=_params())(proj, kpad, vpad, tab, sink, o_attn, dyb, dproj)


def _mem_probs(qh, mk):
    sc = lax.dot_general(qh, mk, _DIMS["nt"], preferred_element_type=F32) * (MEM_HEAD_DIM ** -0.5)
    e = jnp.exp(sc - jnp.max(sc, axis=1, keepdims=True))
    return e * (1.0 / jnp.sum(e, axis=1, keepdims=True))


def _mem_fwd(proj, mkv):
    s = proj.shape[0]
    ts = 256
    mlen = mkv.shape[0]

    def body(m_ref, kv_ref, o_ref, y_ref):
        for h in range(MEM_HEADS):
            cols = slice(h * LANES, (h + 1) * LANES)
            mk = kv_ref[:, cols].astype(BF16)
            mv = kv_ref[:, MEM_WIDTH + h * LANES:MEM_WIDTH + (h + 1) * LANES].astype(BF16)
            prob = _mem_probs(m_ref[:, cols].astype(BF16), mk)
            oh = jnp.dot(prob.astype(BF16), mv, preferred_element_type=F32)
            o_ref[:, cols] = oh
            zh = m_ref[:, MEM_WIDTH + h * LANES:MEM_WIDTH + (h + 1) * LANES]
            y_ref[:, cols] = (oh * (zh * _sigmoid(zh))).astype(BF16)

    o_spec = pl.BlockSpec((ts, MEM_WIDTH), lambda i: (i, 0))
    return pl.pallas_call(
        body, grid=(s // ts,),
        in_specs=[pl.BlockSpec((ts, W_M), lambda i: (i, COL_M)),
                  pl.BlockSpec((mlen, 2 * MEM_WIDTH), lambda i: (0, 0))],
        out_specs=[o_spec, o_spec],
        out_shape=[jax.ShapeDtypeStruct((s, MEM_WIDTH), F32), jax.ShapeDtypeStruct((s, MEM_WIDTH), BF16)],
        name="mem_fwd", compiler_params=_params())(proj, mkv)


def _mem_bwd(proj, mkv, o_mem, dym, dproj):
    s = proj.shape[0]
    ts = 256
    mlen = mkv.shape[0]
    scale = MEM_HEAD_DIM ** -0.5

    def body(m_ref, kv_ref, o_ref, dy_ref, _, dp_ref, dkv_ref):
        @pl.when(pl.program_id(0) == 0)
        def _():
            dkv_ref[...] = jnp.zeros_like(dkv_ref)

        for h in range(MEM_HEADS):
            cols = slice(h * LANES, (h + 1) * LANES)
            vcols = slice(MEM_WIDTH + h * LANES, MEM_WIDTH + (h + 1) * LANES)
            mk = kv_ref[:, cols].astype(BF16)
            mv = kv_ref[:, vcols].astype(BF16)
            qh = m_ref[:, cols].astype(BF16)
            zh = m_ref[:, vcols]
            sg = _sigmoid(zh)
            oh = o_ref[:, cols]
            dyh = dy_ref[:, cols]
            doh = dyh * (zh * sg)
            dp_ref[:, vcols] = (dyh * oh * (sg * (1.0 + zh * (1.0 - sg)))).astype(BF16)
            prob = _mem_probs(qh, mk)
            delta = jnp.sum(doh * oh, axis=1, keepdims=True)
            dohb = doh.astype(BF16)
            dprob = lax.dot_general(dohb, mv, _DIMS["nt"], preferred_element_type=F32)
            dsc = (prob * (dprob - delta)).astype(BF16)
            dp_ref[:, cols] = (jnp.dot(dsc, mk, preferred_element_type=F32) * scale).astype(BF16)
            dkv_ref[:, cols] += lax.dot_general(dsc, qh, _DIMS["tn"], preferred_element_type=F32) * scale
            dkv_ref[:, vcols] += lax.dot_general(prob.astype(BF16), dohb, _DIMS["tn"],
                                                 preferred_element_type=F32)

    blk = pl.BlockSpec((ts, MEM_WIDTH), lambda i: (i, 0))
    msp = pl.BlockSpec((ts, W_M), lambda i: (i, COL_M))
    kvsp = pl.BlockSpec((mlen, 2 * MEM_WIDTH), lambda i: (0, 0))
    return pl.pallas_call(
        body, grid=(s // ts,),
        in_specs=[msp, kvsp, blk, blk, pl.BlockSpec(memory_space=pl.ANY)],
        out_specs=[msp, kvsp],
        out_shape=[jax.ShapeDtypeStruct(dproj.shape, BF16), jax.ShapeDtypeStruct(mkv.shape, F32)],
        input_output_aliases={4: 0}, name="mem_bwd",
        compiler_params=_params())(proj, mkv, o_mem, dym, dproj)


def _local_grads(x, mem, tgt, g_pre, w_in_p, w_conv, sink, g_mem, w_kv, w_up_a, w_up_b, w_up_m, w_out, g_post):
    s = x.shape[0]
    tab = _rope_tables(s)

    h = _rmsnorm_fwd(x, g_pre, name="pre_norm")
    proj = _matmul(h, w_in_p, mode="nn", out_dtype=F32, tm=512, tn=3712, tk=D_MODEL, name="proj", j_outer=True)
    ya = _conv_fwd(proj, w_conv)
    kpad, vpad = _rope_kv(proj, tab)
    o_attn, yb = _swa_fwd(proj, kpad, vpad, tab, sink)
    mn = _rmsnorm_fwd(mem, g_mem, name="mem_norm")
    mkv = _matmul(mn, w_kv, mode="nn", out_dtype=F32, tm=256, tn=1024, tk=D_MODEL, name="mem_kv")
    o_mem, ym = _mem_fwd(proj, mkv)
    ua = _matmul(ya, w_up_a, mode="nn", out_dtype=F32, tm=512, tn=1024, tk=512, name="up_a")
    ub = _matmul(yb, w_up_b, mode="nn", out_dtype=F32, tm=512, tn=1024, tk=512, name="up_b")
    um = _matmul(ym, w_up_m, mode="nn", out_dtype=F32, tm=512, tn=1024, tk=512, name="up_m")
    merged = _merge_fwd(ua, ub, um, proj)
    out = _matmul(merged, w_out, mode="nn", out_dtype=F32, tm=512, tn=1024, tk=D_MODEL, name="out_proj")

    d_out, dy, dg_post, loss = _post_loss(out, x, tgt, g_post)
    dw_out = _matmul(merged, d_out, mode="tn", out_dtype=F32, tm=1024, tn=1024, tk=512, name="dw_out")
    d_merged = _matmul(d_out, w_out, mode="nt", out_dtype=F32, tm=512, tn=1024, tk=D_MODEL, name="d_merged")

    d_ua, d_ub, d_um, dproj = _merge_bwd(d_merged, ua, ub, um, proj)

    dw_up_a = _matmul(ya, d_ua, mode="tn", out_dtype=F32, tm=512, tn=1024, tk=512, name="dw_up_a")
    dw_up_b = _matmul(yb, d_ub, mode="tn", out_dtype=F32, tm=512, tn=1024, tk=512, name="dw_up_b")
    dw_up_m = _matmul(ym, d_um, mode="tn", out_dtype=F32, tm=512, tn=1024, tk=512, name="dw_up_m")
    d_ya = _matmul(d_ua, w_up_a, mode="nt", out_dtype=F32, tm=512, tn=512, tk=D_MODEL, name="d_ya")
    d_yb = _matmul(d_ub, w_up_b, mode="nt", out_dtype=F32, tm=512, tn=512, tk=D_MODEL, name="d_yb")
    d_ym = _matmul(d_um, w_up_m, mode="nt", out_dtype=F32, tm=512, tn=512, tk=D_MODEL, name="d_ym")

    dproj, dw_conv = _conv_bwd(proj, w_conv, d_ya, dproj)
    dproj, dkpad, dvpad, dsink = _swa_bwd(proj, kpad, vpad, tab, sink, o_attn, d_yb, dproj)
    dproj = _rope_kv_bwd(dkpad, dvpad, tab, dproj)
    dproj, d_mkv = _mem_bwd(proj, mkv, o_mem, d_ym, dproj)

    dw_kv = _matmul(mn, d_mkv, mode="tn", out_dtype=F32, tm=1024, tn=1024, tk=256, name="dw_kv")
    d_mn = _matmul(d_mkv, w_kv, mode="nt", out_dtype=F32, tm=256, tn=1024, tk=D_MODEL, name="d_mn")
    _, dg_mem = _rmsnorm_bwd(d_mn, mem, g_mem, d_mn, name="mem_norm_bwd")

    dw_in_p = _matmul(h, dproj, mode="tn", out_dtype=F32, tm=512, tn=3712, tk=512, name="dw_in")
    d_h = _matmul(dproj, w_in_p, mode="nt", out_dtype=F32, tm=512, tn=1024, tk=3712, name="d_h")
    grad_x, dg_pre = _rmsnorm_bwd(d_h, x, g_pre, dy, name="pre_norm_bwd")
    return dict(loss=loss, grad_x=grad_x, g_pre=dg_pre, w_in_p=dw_in_p, w_conv=dw_conv, sink=dsink, g_mem=dg_mem,
                w_kv=dw_kv, w_up_a=dw_up_a, w_up_b=dw_up_b, w_up_m=dw_up_m, w_out=dw_out, g_post=dg_post)


_HBM = pl.BlockSpec(memory_space=pltpu.HBM)
N_DEV = 8


def _position():
    return lax.axis_index("x"), lax.axis_index("y"), lax.axis_index("c")


def _other_chips(x, y):
    return (((1 - x, y), 2 * (1 - x) + y), ((x, 1 - y), 2 * x + (1 - y)), ((1 - x, 1 - y), 2 * (1 - x) + (1 - y)))


def _remote(src, dst, send_sems, recv_sems, k, device):
    return pltpu.make_async_remote_copy(src_ref=src, dst_ref=dst, send_sem=send_sems.at[k], recv_sem=recv_sems.at[k],
                                        device_id=device, device_id_type=MESH)


def _rows_half(ref, hf):
    rh = ref.shape[0] // 2
    return ref.at[pl.ds(pl.multiple_of(hf * rh, 8), rh)]


def _gather_shards(shards, small):
    n = len(shards)

    def body(*refs):
        ins, small_in, outs, small_out = refs[:n], refs[n], refs[n + 1:2 * n + 1], refs[2 * n + 1]
        ici_send, ici_recv, d2d_send, d2d_recv = refs[2 * n + 2:]
        x, y, c = _position()
        me = 2 * x + y
        chips = _other_chips(x, y)
        sends = [_remote(_rows_half(ins[a], c), _rows_half(outs[a].at[me], c), ici_send, ici_recv, 3 * a + r, (*chip, c))
                 for a in range(n) for r, (chip, _) in enumerate(chips)]
        sends += [_remote(small_in, small_out.at[me], ici_send, ici_recv, 3 * n + r, (*chip, c))
                  for r, (chip, _) in enumerate(chips)]
        for cp in sends:
            cp.start()
        for a in range(n):
            for r, (chip, idx) in enumerate(chips):
                landed = _rows_half(outs[a].at[idx], c)
                _remote(landed, landed, ici_send, ici_recv, 3 * a + r, (*chip, c)).wait_recv()
                fwd = _remote(landed, landed, d2d_send, d2d_recv, 3 * a + r, (x, y, 1 - c))
                fwd.start()
                sends.append(fwd)
        for a in range(n):
            for r, (_, idx) in enumerate(chips):
                other = _rows_half(outs[a].at[idx], 1 - c)
                _remote(other, other, d2d_send, d2d_recv, 3 * a + r, (x, y, 1 - c)).wait_recv()
        for r, (chip, idx) in enumerate(chips):
            _remote(small_in, small_out.at[idx], ici_send, ici_recv, 3 * n + r, (*chip, c)).wait_recv()
        for cp in sends:
            cp.wait_send()

    return pl.pallas_call(
        body, in_specs=[_HBM] * (n + 1), out_specs=[_HBM] * (n + 1),
        out_shape=[jax.ShapeDtypeStruct((N_CHIPS,) + s.shape, s.dtype) for s in shards + [small]],
        scratch_shapes=[pltpu.SemaphoreType.DMA((3 * n + 3,)), pltpu.SemaphoreType.DMA((3 * n + 3,)),
                        pltpu.SemaphoreType.DMA((3 * n,)), pltpu.SemaphoreType.DMA((3 * n,))],
        name="gather_weights")(*shards, small)


def _pair_exchange(parts):
    n = len(parts)

    def body(*refs):
        ins, outs = refs[:n], refs[n:2 * n]
        send_sems, recv_sems = refs[2 * n:]
        x, y, c = _position()
        copies = [_remote(ins[a].at[1 - c], outs[a], send_sems, recv_sems, a, (x, y, 1 - c)) for a in range(n)]
        for cp in copies:
            cp.start()
        for cp in copies:
            cp.wait()

    return pl.pallas_call(
        body, in_specs=[_HBM] * n, out_specs=[_HBM] * n,
        out_shape=[jax.ShapeDtypeStruct(p.shape[1:], p.dtype) for p in parts],
        scratch_shapes=[pltpu.SemaphoreType.DMA((n,)), pltpu.SemaphoreType.DMA((n,))],
        name="grad_pair_exchange")(*parts)


def _chip_exchange(sums):
    n = len(sums)

    def body(*refs):
        ins, outs = refs[:n], refs[n:2 * n]
        send_sems, recv_sems = refs[2 * n:]
        x, y, c = _position()
        copies = [_remote(ins[a].at[idx], outs[a].at[r], send_sems, recv_sems, 3 * a + r, (*chip, c))
                  for a in range(n) for r, (chip, idx) in enumerate(_other_chips(x, y))]
        for cp in copies:
            cp.start()
        for cp in copies:
            cp.wait()

    return pl.pallas_call(
        body, in_specs=[_HBM] * n, out_specs=[_HBM] * n,
        out_shape=[jax.ShapeDtypeStruct((3,) + s.shape[1:], s.dtype) for s in sums],
        scratch_shapes=[pltpu.SemaphoreType.DMA((3 * n,)), pltpu.SemaphoreType.DMA((3 * n,))],
        name="grad_chip_exchange")(*sums)


def _pair_share(pairs):
    n = len(pairs)

    def body(*refs):
        outs = refs[n:2 * n]
        send_sems, recv_sems = refs[2 * n:]
        x, y, c = _position()
        sends = [_remote(outs[a].at[c], outs[a].at[c], send_sems, recv_sems, a, (x, y, 1 - c)) for a in range(n)]
        for cp in sends:
            cp.start()
        for a in range(n):
            _remote(outs[a].at[1 - c], outs[a].at[1 - c], send_sems, recv_sems, a, (x, y, 1 - c)).wait_recv()
        for cp in sends:
            cp.wait_send()

    return pl.pallas_call(
        body, in_specs=[_HBM] * n, out_specs=[_HBM] * n,
        out_shape=[jax.ShapeDtypeStruct(p.shape, p.dtype) for p in pairs],
        input_output_aliases={a: a for a in range(n)},
        scratch_shapes=[pltpu.SemaphoreType.DMA((n,)), pltpu.SemaphoreType.DMA((n,))],
        name="grad_pair_share")(*pairs)


def _small_allreduce(pack):
    rows, width = pack.shape

    def body(p_ref, o_ref, buf, send_sems, recv_sems):
        x, y, c = _position()
        me = 4 * x + 2 * y + c
        buf[me] = p_ref[...]
        peers = []
        for r in range(1, N_DEV):
            fx, fy, fc = (r >> 2) & 1, (r >> 1) & 1, r & 1
            px, py, pc = (1 - x if fx else x), (1 - y if fy else y), (1 - c if fc else c)
            peers.append(((px, py, pc), 4 * px + 2 * py + pc))
        sends = [_remote(p_ref, buf.at[me], send_sems, recv_sems, r, dev) for r, (dev, _) in enumerate(peers)]
        for cp in sends:
            cp.start()
        for r, (dev, idx) in enumerate(peers):
            _remote(p_ref, buf.at[idx], send_sems, recv_sems, r, dev).wait_recv()
        for cp in sends:
            cp.wait_send()
        acc = buf[0]
        for k in range(1, N_DEV):
            acc = acc + buf[k]
        o_ref[...] = acc

    vm = pl.BlockSpec(memory_space=pltpu.VMEM)
    return pl.pallas_call(
        body, in_specs=[vm], out_specs=vm, out_shape=jax.ShapeDtypeStruct(pack.shape, F32),
        scratch_shapes=[pltpu.VMEM((N_DEV, rows, width), F32), pltpu.SemaphoreType.DMA((N_DEV - 1,)),
                        pltpu.SemaphoreType.DMA((N_DEV - 1,))],
        name="small_allreduce")(pack)


def _row_tile(rows):
    return rows if rows <= 256 else 256


def _pair_add(part, recv, c_idx, name):
    _, nj, rh, cols = part.shape
    tr = _row_tile(rh)

    def body(c_ref, p_ref, r_ref, o_ref):
        o_ref[...] = (p_ref[...].astype(F32) + r_ref[...].astype(F32)).astype(BF16)

    blk = pl.BlockSpec((None, tr, cols), lambda j, i, c_ref: (j, i, 0))
    grid_spec = pltpu.PrefetchScalarGridSpec(
        num_scalar_prefetch=1, grid=(nj, rh // tr),
        in_specs=[pl.BlockSpec((None, None, tr, cols), lambda j, i, c_ref: (c_ref[0], j, i, 0)), blk],
        out_specs=blk)
    return pl.pallas_call(body, grid_spec=grid_spec, out_shape=jax.ShapeDtypeStruct(recv.shape, BF16),
                          name=name, compiler_params=_params())(c_idx, part, recv)


def _chip_add(sums, recv, where, name):
    _, rh, cols = sums.shape
    tr = _row_tile(rh)

    def body(w_ref, s_ref, r_ref, o_ref):
        o_ref[...] = ((s_ref[...].astype(F32) + r_ref[0].astype(F32)) + r_ref[1].astype(F32)) + r_ref[2].astype(F32)

    grid_spec = pltpu.PrefetchScalarGridSpec(
        num_scalar_prefetch=1, grid=(rh // tr,),
        in_specs=[pl.BlockSpec((None, tr, cols), lambda i, w_ref: (w_ref[0], i, 0)),
                  pl.BlockSpec((3, tr, cols), lambda i, w_ref: (0, i, 0))],
        out_specs=pl.BlockSpec((None, tr, cols), lambda i, w_ref: (w_ref[1], i, 0)))
    return pl.pallas_call(body, grid_spec=grid_spec, out_shape=jax.ShapeDtypeStruct((2, rh, cols), F32),
                          name=name, compiler_params=_params())(where, sums, recv)


def _adamw(w, g, m, v, name):
    rows, cols = w.shape
    tr = _row_tile(rows)
    assert rows % tr == 0

    def body(w_ref, g_ref, m_ref, v_ref, d_ref, mo_ref, vo_ref):
        gv = g_ref[...]
        m_new = ADAM_B1 * m_ref[...] + (1.0 - ADAM_B1) * gv
        v_new = ADAM_B2 * v_ref[...] + (1.0 - ADAM_B2) * jnp.square(gv)
        m_hat = m_new / (1.0 - ADAM_B1 ** ADAM_STEP)
        v_hat = v_new / (1.0 - ADAM_B2 ** ADAM_STEP)
        d_ref[...] = -ADAM_LR * (m_hat / (jnp.sqrt(v_hat) + ADAM_EPS) + ADAM_WD * w_ref[...])
        mo_ref[...] = m_new
        vo_ref[...] = v_new

    blk = pl.BlockSpec((tr, cols), lambda i: (i, 0))
    shp = jax.ShapeDtypeStruct((rows, cols), F32)
    return pl.pallas_call(body, grid=(rows // tr,), in_specs=[blk] * 4, out_specs=[blk] * 3,
                          out_shape=[shp] * 3, name=name, compiler_params=_params())(w, g, m, v)


SHARD_W = IN_WIDTH // N_CHIPS


def _half_major(a):
    r, c = a.shape
    return a.reshape(N_CHIPS, 2, r // N_CHIPS // 2, c).transpose(1, 0, 2, 3)


def _w_in_permuted(pieces):
    cols = []
    for a, b in PERM_SEGS:
        pos = a
        while pos < b:
            j = pos // SHARD_W
            stop = min(b, (j + 1) * SHARD_W)
            cols.append(pieces[j][:, pos - j * SHARD_W:stop - j * SHARD_W])
            pos = stop
    return jnp.concatenate(cols, axis=1)


def _ref_cols(a_p, lo, hi):
    out, ref_off = [], 0
    for a, b in UNPERM_SEGS:
        r0, r1 = ref_off, ref_off + (b - a)
        s, e = max(lo, r0), min(hi, r1)
        if s < e:
            out.append(a_p[:, a + (s - r0):a + (e - r0)])
        ref_off = r1
    return jnp.concatenate(out, axis=1)


def kernel(x, mem, g_pre, w_in, w_conv, attn_sink, g_mem, w_mem_kv, w_up_a, w_up_b, w_up_m, w_out, g_post, loss_target, m_g_pre, m_w_in, m_w_conv, m_attn_sink, m_g_mem, m_w_mem_kv, m_w_up_a, m_w_up_b, m_w_up_m, m_w_out, m_g_post, v_g_pre, v_w_in, v_w_conv, v_attn_sink, v_g_mem, v_w_mem_kv, v_w_up_a, v_w_up_b, v_w_up_m, v_w_out, v_g_post):
    xi, yi, ci = _position()
    chip = 2 * xi + yi
    where = jnp.stack([chip, ci]).astype(jnp.int32)
    c_idx = jnp.reshape(ci, (1,)).astype(jnp.int32)

    own = [w_in[0].astype(BF16), w_mem_kv[0].astype(BF16),
           jnp.concatenate([w_up_a[0], w_up_b[0], w_up_m[0]], axis=0).astype(BF16), w_out[0].astype(BF16)]
    own_conv = jnp.pad(w_conv[0], ((0, 5), (0, 0)))
    *gathered, g_conv = _gather_shards(own, own_conv)

    def pieces(mine, got):
        return [jnp.where(chip == j, mine, got[j]) for j in range(N_CHIPS)]

    w_in_p = _w_in_permuted(pieces(own[0], gathered[0]))
    w_kv_full = jnp.concatenate(pieces(own[1], gathered[1]), axis=0)
    up_pieces = pieces(own[2], gathered[2])
    ups = [jnp.concatenate([p[k * A_WIDTH:(k + 1) * A_WIDTH] for p in up_pieces], axis=1) for k in range(3)]
    w_out_full = jnp.concatenate(pieces(own[3], gathered[3]), axis=0)
    w_conv_full = jnp.concatenate([p[:3] for p in pieces(own_conv, g_conv)], axis=1)

    g = _local_grads(x[0], mem[0], loss_target[0], g_pre, w_in_p, w_conv_full, attn_sink, g_mem, w_kv_full,
                     ups[0], ups[1], ups[2], w_out_full, g_post)

    zeros512 = jnp.zeros((1, D_MODEL - A_WIDTH), F32)
    conv_rows = [jnp.concatenate([g["w_conv"][k:k + 1], zeros512], axis=1) for k in range(3)]
    sink_row = jnp.pad(g["sink"][:, 0].reshape(1, N_Q_HEADS), ((0, 0), (0, D_MODEL - N_Q_HEADS)))
    loss_row = jnp.pad(g["loss"], ((0, 0), (0, D_MODEL - LANES)))
    pack = jnp.concatenate([g["g_pre"], g["g_mem"], g["g_post"]] + conv_rows + [sink_row, loss_row], axis=0)
    red = _small_allreduce(pack)
    loss = red[7, 0]
    small_grads = dict(
        g_pre=red[0:1], g_mem=red[1:2], g_post=red[2:3], attn_sink=red[6:7, :N_Q_HEADS],
        w_conv=lax.dynamic_slice(red[3:6, :A_WIDTH], (0, chip * LANES), (3, LANES)))

    half_rows = D_MODEL // 2
    in_parts = jnp.stack([jnp.stack([_ref_cols(g["w_in_p"][hf * half_rows:(hf + 1) * half_rows],
                                               j * SHARD_W, (j + 1) * SHARD_W) for j in range(N_CHIPS)])
                          for hf in range(2)]).astype(BF16)
    up_stack = jnp.concatenate([g["w_up_a"], g["w_up_b"], g["w_up_m"]], axis=0)
    up_parts = (up_stack.reshape(3, A_WIDTH, N_CHIPS, D_MODEL // N_CHIPS).transpose(2, 0, 1, 3)
                .reshape(N_CHIPS, 2, 3 * A_WIDTH // 2, D_MODEL // N_CHIPS).transpose(1, 0, 2, 3)).astype(BF16)
    parts = [in_parts, _half_major(g["w_kv"]).astype(BF16), up_parts, _half_major(g["w_out"]).astype(BF16)]
    names = ["w_in", "w_kv", "w_up", "w_out"]
    recv = _pair_exchange(parts)
    sums = [_pair_add(p, r, c_idx, "pair_add_" + nm) for p, r, nm in zip(parts, recv, names)]
    recv3 = _chip_exchange(sums)
    pairs = [_chip_add(s, r, where, "chip_add_" + nm) for s, r, nm in zip(sums, recv3, names)]
    full = _pair_share(pairs)
    gw_up = full[2].reshape(3, A_WIDTH, D_MODEL // N_CHIPS)
    grads = dict(small_grads, w_in=full[0].reshape(D_MODEL, SHARD_W),
                 w_mem_kv=full[1].reshape(D_MODEL // N_CHIPS, 2 * MEM_WIDTH),
                 w_up_a=gw_up[0], w_up_b=gw_up[1], w_up_m=gw_up[2],
                 w_out=full[3].reshape(D_MODEL // N_CHIPS, D_MODEL))

    weights = dict(g_pre=g_pre, w_in=w_in, w_conv=w_conv, attn_sink=attn_sink, g_mem=g_mem, w_mem_kv=w_mem_kv,
                   w_up_a=w_up_a, w_up_b=w_up_b, w_up_m=w_up_m, w_out=w_out, g_post=g_post)
    m_in = dict(g_pre=m_g_pre, w_in=m_w_in, w_conv=m_w_conv, attn_sink=m_attn_sink, g_mem=m_g_mem,
                w_mem_kv=m_w_mem_kv, w_up_a=m_w_up_a, w_up_b=m_w_up_b, w_up_m=m_w_up_m, w_out=m_w_out,
                g_post=m_g_post)
    v_in = dict(g_pre=v_g_pre, w_in=v_w_in, w_conv=v_w_conv, attn_sink=v_attn_sink, g_mem=v_g_mem,
                w_mem_kv=v_w_mem_kv, w_up_a=v_w_up_a, w_up_b=v_w_up_b, w_up_m=v_w_up_m, w_out=v_w_out,
                g_post=v_g_post)
    out_g, out_d, out_m, out_v = [], [], [], []
    for nm in ("g_pre", "w_in", "w_conv", "attn_sink", "g_mem", "w_mem_kv", "w_up_a", "w_up_b", "w_up_m", "w_out",
               "g_post"):
        shape = weights[nm].shape
        two_d = shape[-2:]
        gr = grads[nm].reshape(two_d)
        d, m_new, v_new = _adamw(weights[nm].reshape(two_d), gr, m_in[nm].reshape(two_d), v_in[nm].reshape(two_d),
                                 "adamw_" + nm)
        out_g.append(gr.reshape(shape))
        out_d.append(d.reshape(shape))
        out_m.append(m_new.reshape(shape))
        out_v.append(v_new.reshape(shape))
    return (loss, g["grad_x"].reshape(x.shape), *out_g, *out_d, *out_m, *out_v)
```

```python
import functools

import jax
import jax.numpy as jnp
from jax import lax
from jax.experimental import pallas as pl
from jax.experimental.pallas import tpu as pltpu

F32 = jnp.float32
BF16 = jnp.bfloat16
MESH = pl.DeviceIdType.MESH

D_MODEL = 1024
EPS = 1e-6
A_WIDTH = 512
HEAD_DIM = 64
N_Q_HEADS = 8
WINDOW_BLOCK = 128
ROPE_THETA = 500000.0
ROT_DIM = 16
MEM_HEADS = 4
MEM_HEAD_DIM = 128
MEM_WIDTH = 512
IN_WIDTH = 7424
N_CHIPS = 4
LANES = 128
HALF_LANES = 64

PERM_SEGS = ((0, 2560), (2816, 3328), (4352, 7424), (3328, 4352), (2560, 2816))
UNPERM_SEGS = ((0, 2560), (7168, 7424), (2560, 3072), (6144, 7168), (3072, 6144))
COL_A, W_A = 0, 2048
COL_B, W_B = 2, 1024
COL_G, W_G = 1, 3072
COL_M, W_M = 6, 1024
COL_KV, W_KV = 28, 256

ADAM_LR = 0.001
ADAM_B1 = 0.9
ADAM_B2 = 0.999
ADAM_EPS = 1e-08
ADAM_WD = 0.01
ADAM_STEP = 10

VMEM_LIMIT_BYTES = 48 * 1024 * 1024


def _params(**kw):
    return pltpu.CompilerParams(vmem_limit_bytes=VMEM_LIMIT_BYTES, **kw)


def _sigmoid(v):
    return jax.nn.sigmoid(v)


_DIMS = {"nn": (((1,), (0,)), ((), ())), "nt": (((1,), (1,)), ((), ())), "tn": (((0,), (0,)), ((), ()))}


def _matmul(a, b, *, mode, out_dtype, tm, tn, tk, name, j_outer=False):
    if mode == "nn":
        (m, k), (_, n) = a.shape, b.shape
    elif mode == "nt":
        (m, k), (n, _) = a.shape, b.shape
    else:
        (k, m), (_, n) = a.shape, b.shape
    tm, tn, tk = min(tm, m), min(tn, n), min(tk, k)
    assert m % tm == 0 and n % tn == 0 and k % tk == 0
    ni, nj, nk = m // tm, n // tn, k // tk
    dims = _DIMS[mode]

    def ij(g0, g1):
        return (g1, g0) if j_outer else (g0, g1)

    if mode == "nn":
        a_spec = pl.BlockSpec((tm, tk), lambda g0, g1, kk: (ij(g0, g1)[0], kk))
        b_spec = pl.BlockSpec((tk, tn), lambda g0, g1, kk: (kk, ij(g0, g1)[1]))
    elif mode == "nt":
        a_spec = pl.BlockSpec((tm, tk), lambda g0, g1, kk: (ij(g0, g1)[0], kk))
        b_spec = pl.BlockSpec((tn, tk), lambda g0, g1, kk: (ij(g0, g1)[1], kk))
    else:
        a_spec = pl.BlockSpec((tk, tm), lambda g0, g1, kk: (kk, ij(g0, g1)[0]))
        b_spec = pl.BlockSpec((tk, tn), lambda g0, g1, kk: (kk, ij(g0, g1)[1]))
    o_spec = pl.BlockSpec((tm, tn), lambda g0, g1, kk: ij(g0, g1))

    def part(a_ref, b_ref):
        return lax.dot_general(a_ref[...].astype(BF16), b_ref[...].astype(BF16), dims,
                               preferred_element_type=F32)

    if nk == 1:
        def body(a_ref, b_ref, o_ref):
            o_ref[...] = part(a_ref, b_ref).astype(out_dtype)
        scratch = []
    else:
        def body(a_ref, b_ref, o_ref, acc_ref):
            kk = pl.program_id(2)

            @pl.when(kk == 0)
            def _():
                acc_ref[...] = part(a_ref, b_ref)

            @pl.when(kk > 0)
            def _():
                acc_ref[...] += part(a_ref, b_ref)

            @pl.when(kk == nk - 1)
            def _():
                o_ref[...] = acc_ref[...].astype(out_dtype)
        scratch = [pltpu.VMEM((tm, tn), F32)]

    grid = (nj, ni, nk) if j_outer else (ni, nj, nk)
    return pl.pallas_call(
        body, grid=grid, in_specs=[a_spec, b_spec], out_specs=o_spec,
        out_shape=jax.ShapeDtypeStruct((m, n), out_dtype), scratch_shapes=scratch,
        name=name, compiler_params=_params())(a, b)


def _rmsnorm_fwd(x, g, *, name):
    s, d = x.shape
    ts = min(512, s)

    def body(x_ref, g_ref, o_ref):
        xv = x_ref[...]
        r = lax.rsqrt(jnp.mean(xv * xv, axis=-1, keepdims=True) + EPS)
        o_ref[...] = ((xv * r) * g_ref[...]).astype(BF16)

    return pl.pallas_call(
        body, grid=(s // ts,),
        in_specs=[pl.BlockSpec((ts, d), lambda i: (i, 0)), pl.BlockSpec((1, d), lambda i: (0, 0))],
        out_specs=pl.BlockSpec((ts, d), lambda i: (i, 0)),
        out_shape=jax.ShapeDtypeStruct((s, d), BF16), name=name, compiler_params=_params())(x, g)


def _rmsnorm_bwd(dh, x, g, res, *, name):
    s, d = x.shape
    ts = min(256, s)

    def body(dh_ref, x_ref, g_ref, res_ref, dx_ref, dg_ref):
        xv = x_ref[...]
        r = lax.rsqrt(jnp.mean(xv * xv, axis=-1, keepdims=True) + EPS)
        xh = xv * r
        dhv = dh_ref[...]
        part = jnp.sum(dhv * xh, axis=0, keepdims=True)

        @pl.when(pl.program_id(0) == 0)
        def _():
            dg_ref[...] = part

        @pl.when(pl.program_id(0) > 0)
        def _():
            dg_ref[...] += part

        dxh = dhv * g_ref[...]
        dx_ref[...] = res_ref[...] + r * (dxh - xh * jnp.mean(dxh * xh, axis=-1, keepdims=True))

    row = pl.BlockSpec((ts, d), lambda i: (i, 0))
    vec = pl.BlockSpec((1, d), lambda i: (0, 0))
    return pl.pallas_call(
        body, grid=(s // ts,), in_specs=[row, row, vec, row], out_specs=[row, vec],
        out_shape=[jax.ShapeDtypeStruct((s, d), F32), jax.ShapeDtypeStruct((1, d), F32)],
        name=name, compiler_params=_params())(dh, x, g, res)


MID_TILE = 256


def _gated_branches(y_refs, wup_ref, gl):
    d = D_MODEL
    us = [jnp.dot(y_refs[k][...], wup_ref[k], preferred_element_type=F32) for k in range(3)]
    sg = [_sigmoid(gl[:, k * d:(k + 1) * d]) for k in range(3)]
    return us, sg


def _mid_fwd(ya, yb, ym, proj, x, tgt, w_up, w_out, g_post):
    s, d = x.shape
    ts = MID_TILE

    def body(ya_ref, yb_ref, ym_ref, g_ref, x_ref, t_ref, wup_ref, wout_ref, gp_ref,
             m_ref, do_ref, dy_ref, dg_ref, loss_ref):
        us, sg = _gated_branches((ya_ref, yb_ref, ym_ref), wup_ref, g_ref[...])
        merged = (sg[0] * us[0] + sg[1] * us[1] + sg[2] * us[2]).astype(BF16)
        m_ref[...] = merged
        ov = jnp.dot(merged, wout_ref[...], preferred_element_type=F32)
        r = lax.rsqrt(jnp.mean(ov * ov, axis=-1, keepdims=True) + EPS)
        nh = ov * r
        gv = gp_ref[...]
        e = (x_ref[...] + nh * gv) - t_ref[...]
        lpart = 0.5 * jnp.sum(jnp.mean(e * e, axis=-1, keepdims=True), axis=0, keepdims=True)
        dy = e * (1.0 / d)
        dgp = jnp.sum(dy * nh, axis=0, keepdims=True)

        @pl.when(pl.program_id(0) == 0)
        def _():
            dg_ref[...] = dgp
            loss_ref[...] = jnp.broadcast_to(lpart, loss_ref.shape)

        @pl.when(pl.program_id(0) > 0)
        def _():
            dg_ref[...] += dgp
            loss_ref[...] += jnp.broadcast_to(lpart, loss_ref.shape)

        dn = dy * gv
        dy_ref[...] = dy
        do_ref[...] = (r * (dn - nh * jnp.mean(dn * nh, axis=-1, keepdims=True))).astype(BF16)

    row = pl.BlockSpec((ts, d), lambda i: (i, 0))
    ysp = pl.BlockSpec((ts, A_WIDTH), lambda i: (i, 0))
    vec = pl.BlockSpec((1, d), lambda i: (0, 0))
    return pl.pallas_call(
        body, grid=(s // ts,),
        in_specs=[ysp, ysp, ysp, pl.BlockSpec((ts, W_G), lambda i: (i, COL_G)), row, row,
                  pl.BlockSpec((3, A_WIDTH, d), lambda i: (0, 0, 0)), pl.BlockSpec((d, d), lambda i: (0, 0)), vec],
        out_specs=[row, row, row, vec, pl.BlockSpec((1, LANES), lambda i: (0, 0))],
        out_shape=[jax.ShapeDtypeStruct((s, d), BF16), jax.ShapeDtypeStruct((s, d), BF16),
                   jax.ShapeDtypeStruct((s, d), F32), jax.ShapeDtypeStruct((1, d), F32),
                   jax.ShapeDtypeStruct((1, LANES), F32)],
        name="mid_fwd", compiler_params=_params())(ya, yb, ym, proj, x, tgt, w_up, w_out, g_post)


def _mid_bwd(d_out, merged, ya, yb, ym, proj, w_up, w_out):
    s, d = merged.shape
    ts = MID_TILE
    last = s // ts - 1

    def body(do_ref, m_ref, ya_ref, yb_ref, ym_ref, g_ref, wup_ref, wout_ref,
             dp_ref, dya_ref, dyb_ref, dym_ref, dwup_hbm, dwout_hbm, dwup_acc, dwout_acc):
        i = pl.program_id(0)

        @pl.when(i == 0)
        def _():
            dwup_acc[...] = jnp.zeros_like(dwup_acc)
            dwout_acc[...] = jnp.zeros_like(dwout_acc)

        y_refs = (ya_ref, yb_ref, ym_ref)
        us, sg = _gated_branches(y_refs, wup_ref, g_ref[...])
        dov = do_ref[...]
        dwout_acc[...] += lax.dot_general(m_ref[...], dov, _DIMS["tn"], preferred_element_type=F32)
        dm = lax.dot_general(dov, wout_ref[...], _DIMS["nt"], preferred_element_type=F32)
        for k, dy_ref in enumerate((dya_ref, dyb_ref, dym_ref)):
            dp_ref[:, k * d:(k + 1) * d] = ((dm * us[k]) * (sg[k] * (1.0 - sg[k]))).astype(BF16)
            du = (sg[k] * dm).astype(BF16)
            dy_ref[...] = lax.dot_general(du, wup_ref[k], _DIMS["nt"], preferred_element_type=F32)
            dwup_acc[k] += lax.dot_general(y_refs[k][...], du, _DIMS["tn"], preferred_element_type=F32)

        @pl.when(i == last)
        def _():
            pltpu.sync_copy(dwup_acc, dwup_hbm)
            pltpu.sync_copy(dwout_acc, dwout_hbm)

    row = pl.BlockSpec((ts, d), lambda i: (i, 0))
    ysp = pl.BlockSpec((ts, A_WIDTH), lambda i: (i, 0))
    gsp = pl.BlockSpec((ts, W_G), lambda i: (i, COL_G))
    anysp = pl.BlockSpec(memory_space=pl.ANY)
    yshape = jax.ShapeDtypeStruct((s, A_WIDTH), F32)
    return pl.pallas_call(
        body, grid=(s // ts,),
        in_specs=[row, row, ysp, ysp, ysp, gsp, pl.BlockSpec((3, A_WIDTH, d), lambda i: (0, 0, 0)),
                  pl.BlockSpec((d, d), lambda i: (0, 0))],
        out_specs=[gsp, ysp, ysp, ysp, anysp, anysp],
        out_shape=[jax.ShapeDtypeStruct((s, IN_WIDTH), BF16), yshape, yshape, yshape,
                   jax.ShapeDtypeStruct((3, A_WIDTH, d), F32), jax.ShapeDtypeStruct((d, d), F32)],
        scratch_shapes=[pltpu.VMEM((3, A_WIDTH, d), F32), pltpu.VMEM((d, d), F32)],
        name="mid_bwd", compiler_params=_params())(d_out, merged, ya, yb, ym, proj, w_up, w_out)


def _conv_core(blk, prev, nxt, w, i, last, ts):
    c = A_WIDTH
    ab, ac, ax, az = blk[:, :c], blk[:, c:2 * c], blk[:, 2 * c:3 * c], blk[:, 3 * c:]
    cu = ac * ax
    cu_prev = (prev[7:8, c:2 * c] * prev[7:8, 2 * c:3 * c]) * jnp.where(i > 0, 1.0, 0.0)
    cu_next = (nxt[0:1, c:2 * c] * nxt[0:1, 2 * c:3 * c]) * jnp.where(i < last, 1.0, 0.0)
    row = lax.broadcasted_iota(jnp.int32, (ts, c), 0)
    cm1 = jnp.where(row == 0, cu_prev, pltpu.roll(cu, 1, 0))
    cp1 = jnp.where(row == ts - 1, cu_next, pltpu.roll(cu, ts - 1, 0))
    yc = cm1 * w[0:1] + cu * w[1:2] + cp1 * w[2:3]
    return ab, ac, ax, az, cu, cm1, cp1, yc, row


def _halo_specs(ts, width, col, nblk8):
    prev = pl.BlockSpec((8, width), lambda i: (jnp.maximum(i * (ts // 8) - 1, 0), col))
    nxt = pl.BlockSpec((8, width), lambda i: (jnp.minimum((i + 1) * (ts // 8), nblk8 - 1), col))
    return prev, nxt


def _conv_fwd(proj, w_conv):
    s = proj.shape[0]
    ts = 256
    last = s // ts - 1

    def body(a_ref, ap_ref, an_ref, w_ref, ya_ref):
        i = pl.program_id(0)
        ab, _, _, az, _, _, _, yc, _ = _conv_core(a_ref[...], ap_ref[...], an_ref[...], w_ref[...], i, last, ts)
        ya_ref[...] = ((ab * yc) * (az * _sigmoid(az))).astype(BF16)

    prev, nxt = _halo_specs(ts, W_A, COL_A, s // 8)
    return pl.pallas_call(
        body, grid=(s // ts,),
        in_specs=[pl.BlockSpec((ts, W_A), lambda i: (i, COL_A)), prev, nxt,
                  pl.BlockSpec((3, A_WIDTH), lambda i: (0, 0))],
        out_specs=pl.BlockSpec((ts, A_WIDTH), lambda i: (i, 0)),
        out_shape=jax.ShapeDtypeStruct((s, A_WIDTH), BF16), name="conv_fwd",
        compiler_params=_params())(proj, proj, proj, w_conv)


def _conv_bwd(proj, w_conv, dya, dproj):
    s = proj.shape[0]
    ts = 256
    last = s // ts - 1
    c = A_WIDTH

    def body(a_ref, ap_ref, an_ref, w_ref, d_ref, dp_ref, dn_ref, _, dproj_ref, dw_ref):
        i = pl.program_id(0)
        w = w_ref[...]
        prev, nxt = ap_ref[...], an_ref[...]
        ab, ac, ax, az, cu, cm1, cp1, yc, row = _conv_core(a_ref[...], prev, nxt, w, i, last, ts)
        sg = _sigmoid(az)
        sz = az * sg
        dya_v = d_ref[...]
        dyc = dya_v * sz * ab
        dproj_ref[:, :c] = (dya_v * sz * yc).astype(BF16)
        dproj_ref[:, 3 * c:] = (dya_v * (ab * yc) * (sg * (1.0 + az * (1.0 - sg)))).astype(BF16)

        def halo_dyc(a_row, d_row):
            azr = a_row[:, 3 * c:]
            return d_row * (azr * _sigmoid(azr)) * a_row[:, :c]

        dyc_prev = halo_dyc(prev[7:8], dp_ref[...][7:8]) * jnp.where(i > 0, 1.0, 0.0)
        dyc_next = halo_dyc(nxt[0:1], dn_ref[...][0:1]) * jnp.where(i < last, 1.0, 0.0)
        dyc_m1 = jnp.where(row == 0, dyc_prev, pltpu.roll(dyc, 1, 0))
        dyc_p1 = jnp.where(row == ts - 1, dyc_next, pltpu.roll(dyc, ts - 1, 0))
        dcu = dyc_p1 * w[0:1] + dyc * w[1:2] + dyc_m1 * w[2:3]
        dproj_ref[:, c:2 * c] = (dcu * ax).astype(BF16)
        dproj_ref[:, 2 * c:3 * c] = (dcu * ac).astype(BF16)
        dw = [jnp.sum(dyc * t, axis=0, keepdims=True) for t in (cm1, cu, cp1)]

        @pl.when(i == 0)
        def _():
            for k in range(3):
                dw_ref[k:k + 1, :] = dw[k]

        @pl.when(i > 0)
        def _():
            for k in range(3):
                dw_ref[k:k + 1, :] += dw[k]

    prev, nxt = _halo_specs(ts, W_A, COL_A, s // 8)
    dprev, dnxt = _halo_specs(ts, A_WIDTH, 0, s // 8)
    return pl.pallas_call(
        body, grid=(s // ts,),
        in_specs=[pl.BlockSpec((ts, W_A), lambda i: (i, COL_A)), prev, nxt,
                  pl.BlockSpec((3, A_WIDTH), lambda i: (0, 0)),
                  pl.BlockSpec((ts, A_WIDTH), lambda i: (i, 0)), dprev, dnxt,
                  pl.BlockSpec(memory_space=pl.ANY)],
        out_specs=[pl.BlockSpec((ts, W_A), lambda i: (i, COL_A)), pl.BlockSpec((3, A_WIDTH), lambda i: (0, 0))],
        out_shape=[jax.ShapeDtypeStruct(dproj.shape, BF16), jax.ShapeDtypeStruct((3, A_WIDTH), F32)],
        input_output_aliases={7: 0}, name="conv_bwd",
        compiler_params=_params())(proj, proj, proj, w_conv, dya, dya, dya, dproj)


def _rope_tables(s):
    half = ROT_DIM // 2
    inv_freq = jnp.power(jnp.float32(ROPE_THETA), -jnp.arange(half, dtype=F32) * (2.0 / ROT_DIM))
    ang = jnp.arange(s).astype(F32)[:, None] * inv_freq[None, :]
    cos, sin = jnp.cos(ang), jnp.sin(ang)
    pad = jnp.zeros((s, HEAD_DIM - ROT_DIM), F32)
    c = jnp.concatenate([cos, cos, pad + 1.0], axis=1)
    s1 = jnp.concatenate([-sin, jnp.zeros_like(sin), pad], axis=1)
    s2 = jnp.concatenate([jnp.zeros_like(sin), sin, pad], axis=1)
    return jnp.concatenate([c, c, s1, s1, s2, s2], axis=1)


def _rope(t, tab):
    return (t * tab[:, :LANES] + pltpu.roll(t, LANES - 8, 1) * tab[:, LANES:2 * LANES]
            + pltpu.roll(t, 8, 1) * tab[:, 2 * LANES:])


def _rope_transpose(dt, tab):
    return (dt * tab[:, :LANES] + pltpu.roll(dt * tab[:, LANES:2 * LANES], 8, 1)
            + pltpu.roll(dt * tab[:, 2 * LANES:], LANES - 8, 1))


def _rope_kv(proj, tab):
    s = proj.shape[0]
    nb = s // WINDOW_BLOCK

    def body(kv_ref, t_ref, k_ref, v_ref):
        j = pl.program_id(0)
        inside = jnp.where((j > 0) & (j <= nb), 1.0, 0.0)
        kv = kv_ref[...]
        k_ref[...] = (_rope(kv[:, :LANES], t_ref[...]) * inside).astype(BF16)
        v_ref[...] = (kv[:, LANES:] * inside).astype(BF16)

    def src(j):
        return jnp.clip(j - 1, 0, nb - 1)

    o_spec = pl.BlockSpec((WINDOW_BLOCK, LANES), lambda j: (j, 0))
    shp = jax.ShapeDtypeStruct((s + 2 * WINDOW_BLOCK, LANES), BF16)
    return pl.pallas_call(
        body, grid=(nb + 2,),
        in_specs=[pl.BlockSpec((WINDOW_BLOCK, W_KV), lambda j: (src(j), COL_KV)),
                  pl.BlockSpec((WINDOW_BLOCK, 3 * LANES), lambda j: (src(j), 0))],
        out_specs=[o_spec, o_spec], out_shape=[shp, shp], name="rope_kv",
        compiler_params=_params())(proj, tab)


def _rope_kv_bwd(dkpad, dvpad, tab, dproj):
    s = tab.shape[0]
    nb = s // WINDOW_BLOCK

    def body(dk_ref, dv_ref, t_ref, _, dp_ref):
        dp_ref[:, :LANES] = _rope_transpose(dk_ref[...], t_ref[...]).astype(BF16)
        dp_ref[:, LANES:] = dv_ref[...].astype(BF16)

    pad_spec = pl.BlockSpec((WINDOW_BLOCK, LANES), lambda j: (j + 1, 0))
    return pl.pallas_call(
        body, grid=(nb,),
        in_specs=[pad_spec, pad_spec, pl.BlockSpec((WINDOW_BLOCK, 3 * LANES), lambda j: (j, 0)),
                  pl.BlockSpec(memory_space=pl.ANY)],
        out_specs=pl.BlockSpec((WINDOW_BLOCK, W_KV), lambda j: (j, COL_KV)),
        out_shape=jax.ShapeDtypeStruct(dproj.shape, BF16), input_output_aliases={3: 0},
        name="rope_kv_bwd", compiler_params=_params())(dkpad, dvpad, tab, dproj)


def _window_operands(k_ref, v_ref, n, lo):
    start = pl.multiple_of(n * WINDOW_BLOCK, WINDOW_BLOCK)
    kw = k_ref[pl.ds(start, 3 * WINDOW_BLOCK), :].astype(F32)
    vw = v_ref[pl.ds(start, 3 * WINDOW_BLOCK), :].astype(F32)
    kr, vr = pltpu.roll(kw, HALF_LANES, 1), pltpu.roll(vw, HALF_LANES, 1)
    k2 = (jnp.where(lo, kw, kr).astype(BF16), jnp.where(lo, kr, kw).astype(BF16))
    v2 = (jnp.where(lo, vw, vr).astype(BF16), jnp.where(lo, vr, vw).astype(BF16))
    return k2, v2


HEADS_PER_GROUP = 4


def _window_mask(n, s):
    wb = WINDOW_BLOCK
    shape = (HEADS_PER_GROUP * wb, 3 * wb)
    qi = lax.broadcasted_iota(jnp.int32, shape, 0) & (wb - 1)
    kj = lax.broadcasted_iota(jnp.int32, shape, 1)
    kpos = kj + (n - 1) * wb
    return (kj >= qi) & (kj <= qi + 2 * wb) & (kpos >= 0) & (kpos < s)


def _stack_heads(pair0, pair1, lo):
    return jnp.concatenate([jnp.where(lo, pair0, 0.0), jnp.where(lo, 0.0, pair0),
                            jnp.where(lo, pair1, 0.0), jnp.where(lo, 0.0, pair1)], axis=0)


def _unstack_pair(stacked, i, lo):
    wb = WINDOW_BLOCK
    return jnp.where(lo, stacked[2 * i * wb:(2 * i + 1) * wb], stacked[(2 * i + 1) * wb:(2 * i + 2) * wb])


def _sink_column(sink_ref, g):
    wb = WINDOW_BLOCK
    return jnp.concatenate([jnp.full((wb, 1), sink_ref[0, HEADS_PER_GROUP * g + i], F32)
                            for i in range(HEADS_PER_GROUP)], axis=0)


def _head_probs(q4, k2g, valid, sink):
    sc = lax.dot_general(q4, k2g, _DIMS["nt"], preferred_element_type=F32) * (HEAD_DIM ** -0.5)
    sc = jnp.where(valid, sc, -jnp.inf)
    m = jnp.maximum(jnp.max(sc, axis=1, keepdims=True), sink)
    e = jnp.exp(sc - m)
    es = jnp.exp(sink - m)
    inv = 1.0 / (jnp.sum(e, axis=1, keepdims=True) + es)
    return e * inv, es * inv


def _swa_fwd(proj, kpad, vpad, tab, sink):
    s = proj.shape[0]
    wb = WINDOW_BLOCK

    def body(b_ref, k_ref, v_ref, t_ref, sink_ref, o_ref, y_ref):
        n = pl.program_id(0)
        lo = lax.broadcasted_iota(jnp.int32, (wb, LANES), 1) < HALF_LANES
        lo_w = lax.broadcasted_iota(jnp.int32, (3 * wb, LANES), 1) < HALF_LANES
        k2, v2 = _window_operands(k_ref, v_ref, n, lo_w)
        valid = _window_mask(n, s)
        tab_v = t_ref[...]
        for g in range(2):
            qr = [_rope(b_ref[:, (2 * g + i) * LANES:(2 * g + i + 1) * LANES], tab_v) for i in range(2)]
            q4 = _stack_heads(qr[0], qr[1], lo).astype(BF16)
            prob, _ = _head_probs(q4, k2[g], valid, _sink_column(sink_ref, g))
            o4 = jnp.dot(prob.astype(BF16), v2[g], preferred_element_type=F32)
            for i in range(2):
                cols = slice((2 * g + i) * LANES, (2 * g + i + 1) * LANES)
                op = _unstack_pair(o4, i, lo)
                o_ref[:, cols] = op
                zp = b_ref[:, A_WIDTH + cols.start:A_WIDTH + cols.stop]
                y_ref[:, cols] = (op * (zp * _sigmoid(zp))).astype(BF16)

    pad_spec = pl.BlockSpec((s + 2 * wb, LANES), lambda n: (0, 0))
    o_spec = pl.BlockSpec((wb, A_WIDTH), lambda n: (n, 0))
    return pl.pallas_call(
        body, grid=(s // wb,),
        in_specs=[pl.BlockSpec((wb, W_B), lambda n: (n, COL_B)), pad_spec, pad_spec,
                  pl.BlockSpec((wb, 3 * LANES), lambda n: (n, 0)),
                  pl.BlockSpec(memory_space=pltpu.SMEM)],
        out_specs=[o_spec, o_spec],
        out_shape=[jax.ShapeDtypeStruct((s, A_WIDTH), F32), jax.ShapeDtypeStruct((s, A_WIDTH), BF16)],
        name="swa_fwd", compiler_params=_params())(proj, kpad, vpad, tab, sink)


def _swa_bwd(proj, kpad, vpad, tab, sink, o_attn, dyb, dproj):
    s = proj.shape[0]
    wb = WINDOW_BLOCK
    scale = HEAD_DIM ** -0.5

    def body(b_ref, k_ref, v_ref, t_ref, sink_ref, o_ref, dy_ref, _, dp_ref, dk_ref, dv_ref, ds_ref):
        n = pl.program_id(0)

        @pl.when(n == 0)
        def _():
            dk_ref[...] = jnp.zeros_like(dk_ref)
            dv_ref[...] = jnp.zeros_like(dv_ref)
            ds_ref[...] = jnp.zeros_like(ds_ref)

        lo = lax.broadcasted_iota(jnp.int32, (wb, LANES), 1) < HALF_LANES
        lo_w = lax.broadcasted_iota(jnp.int32, (3 * wb, LANES), 1) < HALF_LANES
        k2, v2 = _window_operands(k_ref, v_ref, n, lo_w)
        valid = _window_mask(n, s)
        tab_v = t_ref[...]
        dks, dvs = [], []
        for g in range(2):
            qr, op, do = [], [], []
            for i in range(2):
                cols = slice((2 * g + i) * LANES, (2 * g + i + 1) * LANES)
                zcols = slice(A_WIDTH + cols.start, A_WIDTH + cols.stop)
                qr.append(_rope(b_ref[:, cols], tab_v))
                zp = b_ref[:, zcols]
                sg = _sigmoid(zp)
                op.append(o_ref[:, cols])
                dyp = dy_ref[:, cols]
                do.append(dyp * (zp * sg))
                dp_ref[:, zcols] = (dyp * op[i] * (sg * (1.0 + zp * (1.0 - sg)))).astype(BF16)
            q4 = _stack_heads(qr[0], qr[1], lo).astype(BF16)
            do4 = _stack_heads(do[0], do[1], lo)
            o4 = jnp.concatenate([op[0], op[0], op[1], op[1]], axis=0)
            prob, psink = _head_probs(q4, k2[g], valid, _sink_column(sink_ref, g))
            delta = jnp.sum(do4 * o4, axis=1, keepdims=True)
            do4b = do4.astype(BF16)
            dprob = lax.dot_general(do4b, v2[g], _DIMS["nt"], preferred_element_type=F32)
            dsc = (prob * (dprob - delta)).astype(BF16)
            sink_terms = psink * delta
            for i in range(HEADS_PER_GROUP):
                h = HEADS_PER_GROUP * g + i
                dsink = -jnp.sum(sink_terms[i * wb:(i + 1) * wb], axis=0, keepdims=True)
                ds_ref[h:h + 1, :] += jnp.broadcast_to(dsink, (1, LANES))
            dq4 = jnp.dot(dsc, k2[g], preferred_element_type=F32) * scale
            for i in range(2):
                cols = slice((2 * g + i) * LANES, (2 * g + i + 1) * LANES)
                dp_ref[:, cols] = _rope_transpose(_unstack_pair(dq4, i, lo), tab_v).astype(BF16)
            dk2 = lax.dot_general(dsc, q4, _DIMS["tn"], preferred_element_type=F32) * scale
            dv2 = lax.dot_general(prob.astype(BF16), do4b, _DIMS["tn"], preferred_element_type=F32)
            dks.append(dk2 + pltpu.roll(dk2, HALF_LANES, 1))
            dvs.append(dv2 + pltpu.roll(dv2, HALF_LANES, 1))
        start = pl.multiple_of(n * wb, wb)
        dk_ref[pl.ds(start, 3 * wb), :] += jnp.where(lo_w, dks[0], dks[1])
        dv_ref[pl.ds(start, 3 * wb), :] += jnp.where(lo_w, dvs[0], dvs[1])

    pad_spec = pl.BlockSpec((s + 2 * wb, LANES), lambda n: (0, 0))
    blk = pl.BlockSpec((wb, A_WIDTH), lambda n: (n, 0))
    bsp = pl.BlockSpec((wb, W_B), lambda n: (n, COL_B))
    pad_shape = jax.ShapeDtypeStruct((s + 2 * wb, LANES), F32)
    return pl.pallas_call(
        body, grid=(s // wb,),
        in_specs=[bsp, pad_spec, pad_spec, pl.BlockSpec((wb, 3 * LANES), lambda n: (n, 0)),
                  pl.BlockSpec(memory_space=pltpu.SMEM), blk, blk, pl.BlockSpec(memory_space=pl.ANY)],
        out_specs=[bsp, pad_spec, pad_spec, pl.BlockSpec((8, LANES), lambda n: (0, 0))],
        out_shape=[jax.ShapeDtypeStruct(dproj.shape, BF16), pad_shape, pad_shape,
                   jax.ShapeDtypeStruct((8, LANES), F32)],
        input_output_aliases={7: 0}, name="swa_bwd",
        compiler_params=_params())(proj, kpad, vpad, tab, sink, o_attn, dyb, dproj)


def _mem_probs(qh, mk):
    sc = lax.dot_general(qh, mk, _DIMS["nt"], preferred_element_type=F32) * (MEM_HEAD_DIM ** -0.5)
    e = jnp.exp(sc - jnp.max(sc, axis=1, keepdims=True))
    return e * (1.0 / jnp.sum(e, axis=1, keepdims=True))


def _mem_fwd(proj, mkv):
    s = proj.shape[0]
    ts = 256
    mlen = mkv.shape[0]

    def body(m_ref, kv_ref, o_ref, y_ref):
        for h in range(MEM_HEADS):
            cols = slice(h * LANES, (h + 1) * LANES)
            mk = kv_ref[:, cols].astype(BF16)
            mv = kv_ref[:, MEM_WIDTH + h * LANES:MEM_WIDTH + (h + 1) * LANES].astype(BF16)
            prob = _mem_probs(m_ref[:, cols].astype(BF16), mk)
            oh = jnp.dot(prob.astype(BF16), mv, preferred_element_type=F32)
            o_ref[:, cols] = oh
            zh = m_ref[:, MEM_WIDTH + h * LANES:MEM_WIDTH + (h + 1) * LANES]
            y_ref[:, cols] = (oh * (zh * _sigmoid(zh))).astype(BF16)

    o_spec = pl.BlockSpec((ts, MEM_WIDTH), lambda i: (i, 0))
    return pl.pallas_call(
        body, grid=(s // ts,),
        in_specs=[pl.BlockSpec((ts, W_M), lambda i: (i, COL_M)),
                  pl.BlockSpec((mlen, 2 * MEM_WIDTH), lambda i: (0, 0))],
        out_specs=[o_spec, o_spec],
        out_shape=[jax.ShapeDtypeStruct((s, MEM_WIDTH), F32), jax.ShapeDtypeStruct((s, MEM_WIDTH), BF16)],
        name="mem_fwd", compiler_params=_params())(proj, mkv)


def _mem_bwd(proj, mkv, o_mem, dym, dproj):
    s = proj.shape[0]
    ts = 256
    mlen = mkv.shape[0]
    scale = MEM_HEAD_DIM ** -0.5

    def body(m_ref, kv_ref, o_ref, dy_ref, _, dp_ref, dkv_ref):
        @pl.when(pl.program_id(0) == 0)
        def _():
            dkv_ref[...] = jnp.zeros_like(dkv_ref)

        for h in range(MEM_HEADS):
            cols = slice(h * LANES, (h + 1) * LANES)
            vcols = slice(MEM_WIDTH + h * LANES, MEM_WIDTH + (h + 1) * LANES)
            mk = kv_ref[:, cols].astype(BF16)
            mv = kv_ref[:, vcols].astype(BF16)
            qh = m_ref[:, cols].astype(BF16)
            zh = m_ref[:, vcols]
            sg = _sigmoid(zh)
            oh = o_ref[:, cols]
            dyh = dy_ref[:, cols]
            doh = dyh * (zh * sg)
            dp_ref[:, vcols] = (dyh * oh * (sg * (1.0 + zh * (1.0 - sg)))).astype(BF16)
            prob = _mem_probs(qh, mk)
            delta = jnp.sum(doh * oh, axis=1, keepdims=True)
            dohb = doh.astype(BF16)
            dprob = lax.dot_general(dohb, mv, _DIMS["nt"], preferred_element_type=F32)
            dsc = (prob * (dprob - delta)).astype(BF16)
            dp_ref[:, cols] = (jnp.dot(dsc, mk, preferred_element_type=F32) * scale).astype(BF16)
            dkv_ref[:, cols] += lax.dot_general(dsc, qh, _DIMS["tn"], preferred_element_type=F32) * scale
            dkv_ref[:, vcols] += lax.dot_general(prob.astype(BF16), dohb, _DIMS["tn"],
                                                 preferred_element_type=F32)

    blk = pl.BlockSpec((ts, MEM_WIDTH), lambda i: (i, 0))
    msp = pl.BlockSpec((ts, W_M), lambda i: (i, COL_M))
    kvsp = pl.BlockSpec((mlen, 2 * MEM_WIDTH), lambda i: (0, 0))
    return pl.pallas_call(
        body, grid=(s // ts,),
        in_specs=[msp, kvsp, blk, blk, pl.BlockSpec(memory_space=pl.ANY)],
        out_specs=[msp, kvsp],
        out_shape=[jax.ShapeDtypeStruct(dproj.shape, BF16), jax.ShapeDtypeStruct(mkv.shape, F32)],
        input_output_aliases={4: 0}, name="mem_bwd",
        compiler_params=_params())(proj, mkv, o_mem, dym, dproj)


def _local_grads(x, mem, tgt, g_pre, w_in_p, w_conv, sink, g_mem, w_kv, w_up, w_out, g_post):
    s = x.shape[0]
    tab = _rope_tables(s)

    h = _rmsnorm_fwd(x, g_pre, name="pre_norm")
    proj = _matmul(h, w_in_p, mode="nn", out_dtype=F32, tm=512, tn=3712, tk=D_MODEL, name="proj", j_outer=True)
    ya = _conv_fwd(proj, w_conv)
    kpad, vpad = _rope_kv(proj, tab)
    o_attn, yb = _swa_fwd(proj, kpad, vpad, tab, sink)
    mn = _rmsnorm_fwd(mem, g_mem, name="mem_norm")
    mkv = _matmul(mn, w_kv, mode="nn", out_dtype=F32, tm=256, tn=1024, tk=D_MODEL, name="mem_kv")
    o_mem, ym = _mem_fwd(proj, mkv)
    merged, d_out, dy, dg_post, loss = _mid_fwd(ya, yb, ym, proj, x, tgt, w_up, w_out, g_post)
    dproj, d_ya, d_yb, d_ym, dw_up, dw_out = _mid_bwd(d_out, merged, ya, yb, ym, proj, w_up, w_out)

    dproj, dw_conv = _conv_bwd(proj, w_conv, d_ya, dproj)
    dproj, dkpad, dvpad, dsink = _swa_bwd(proj, kpad, vpad, tab, sink, o_attn, d_yb, dproj)
    dproj = _rope_kv_bwd(dkpad, dvpad, tab, dproj)
    dproj, d_mkv = _mem_bwd(proj, mkv, o_mem, d_ym, dproj)

    dw_kv = _matmul(mn, d_mkv, mode="tn", out_dtype=F32, tm=1024, tn=1024, tk=256, name="dw_kv")
    d_mn = _matmul(d_mkv, w_kv, mode="nt", out_dtype=F32, tm=256, tn=1024, tk=D_MODEL, name="d_mn")
    _, dg_mem = _rmsnorm_bwd(d_mn, mem, g_mem, d_mn, name="mem_norm_bwd")

    dw_in_p = _matmul(h, dproj, mode="tn", out_dtype=F32, tm=512, tn=3712, tk=512, name="dw_in")
    d_h = _matmul(dproj, w_in_p, mode="nt", out_dtype=F32, tm=512, tn=1024, tk=3712, name="d_h")
    grad_x, dg_pre = _rmsnorm_bwd(d_h, x, g_pre, dy, name="pre_norm_bwd")
    return dict(loss=loss, grad_x=grad_x, g_pre=dg_pre, w_in_p=dw_in_p, w_conv=dw_conv, sink=dsink, g_mem=dg_mem,
                w_kv=dw_kv, w_up=dw_up, w_out=dw_out, g_post=dg_post)


_HBM = pl.BlockSpec(memory_space=pltpu.HBM)
N_DEV = 8


def _position():
    return lax.axis_index("x"), lax.axis_index("y"), lax.axis_index("c")


def _other_chips(x, y):
    return (((1 - x, y), 2 * (1 - x) + y), ((x, 1 - y), 2 * x + (1 - y)), ((1 - x, 1 - y), 2 * (1 - x) + (1 - y)))


def _remote(src, dst, send_sems, recv_sems, k, device):
    return pltpu.make_async_remote_copy(src_ref=src, dst_ref=dst, send_sem=send_sems.at[k], recv_sem=recv_sems.at[k],
                                        device_id=device, device_id_type=MESH)


def _rows_half(ref, hf):
    rh = ref.shape[0] // 2
    return ref.at[pl.ds(pl.multiple_of(hf * rh, 8), rh)]


def _gather_shards(shards, small):
    n = len(shards)

    def body(*refs):
        ins, small_in, outs, small_out = refs[:n], refs[n], refs[n + 1:2 * n + 1], refs[2 * n + 1]
        ici_send, ici_recv, d2d_send, d2d_recv = refs[2 * n + 2:]
        x, y, c = _position()
        me = 2 * x + y
        chips = _other_chips(x, y)
        sends = [_remote(_rows_half(ins[a], c), _rows_half(outs[a].at[me], c), ici_send, ici_recv, 3 * a + r, (*chip, c))
                 for a in range(n) for r, (chip, _) in enumerate(chips)]
        sends += [_remote(small_in, small_out.at[me], ici_send, ici_recv, 3 * n + r, (*chip, c))
                  for r, (chip, _) in enumerate(chips)]
        for cp in sends:
            cp.start()
        for a in range(n):
            for r, (chip, idx) in enumerate(chips):
                landed = _rows_half(outs[a].at[idx], c)
                _remote(landed, landed, ici_send, ici_recv, 3 * a + r, (*chip, c)).wait_recv()
                fwd = _remote(landed, landed, d2d_send, d2d_recv, 3 * a + r, (x, y, 1 - c))
                fwd.start()
                sends.append(fwd)
        for a in range(n):
            for r, (_, idx) in enumerate(chips):
                other = _rows_half(outs[a].at[idx], 1 - c)
                _remote(other, other, d2d_send, d2d_recv, 3 * a + r, (x, y, 1 - c)).wait_recv()
        for r, (chip, idx) in enumerate(chips):
            _remote(small_in, small_out.at[idx], ici_send, ici_recv, 3 * n + r, (*chip, c)).wait_recv()
        for cp in sends:
            cp.wait_send()

    return pl.pallas_call(
        body, in_specs=[_HBM] * (n + 1), out_specs=[_HBM] * (n + 1),
        out_shape=[jax.ShapeDtypeStruct((N_CHIPS,) + s.shape, s.dtype) for s in shards + [small]],
        scratch_shapes=[pltpu.SemaphoreType.DMA((3 * n + 3,)), pltpu.SemaphoreType.DMA((3 * n + 3,)),
                        pltpu.SemaphoreType.DMA((3 * n,)), pltpu.SemaphoreType.DMA((3 * n,))],
        name="gather_weights")(*shards, small)


def _pair_exchange(parts):
    n = len(parts)

    def body(*refs):
        ins, outs = refs[:n], refs[n:2 * n]
        send_sems, recv_sems = refs[2 * n:]
        x, y, c = _position()
        copies = [_remote(ins[a].at[1 - c], outs[a], send_sems, recv_sems, a, (x, y, 1 - c)) for a in range(n)]
        for cp in copies:
            cp.start()
        for cp in copies:
            cp.wait()

    return pl.pallas_call(
        body, in_specs=[_HBM] * n, out_specs=[_HBM] * n,
        out_shape=[jax.ShapeDtypeStruct(p.shape[1:], p.dtype) for p in parts],
        scratch_shapes=[pltpu.SemaphoreType.DMA((n,)), pltpu.SemaphoreType.DMA((n,))],
        name="grad_pair_exchange")(*parts)


def _chip_exchange(sums):
    n = len(sums)

    def body(*refs):
        ins, outs = refs[:n], refs[n:2 * n]
        send_sems, recv_sems = refs[2 * n:]
        x, y, c = _position()
        copies = [_remote(ins[a].at[idx], outs[a].at[r], send_sems, recv_sems, 3 * a + r, (*chip, c))
                  for a in range(n) for r, (chip, idx) in enumerate(_other_chips(x, y))]
        for cp in copies:
            cp.start()
        for cp in copies:
            cp.wait()

    return pl.pallas_call(
        body, in_specs=[_HBM] * n, out_specs=[_HBM] * n,
        out_shape=[jax.ShapeDtypeStruct((3,) + s.shape[1:], s.dtype) for s in sums],
        scratch_shapes=[pltpu.SemaphoreType.DMA((3 * n,)), pltpu.SemaphoreType.DMA((3 * n,))],
        name="grad_chip_exchange")(*sums)


def _pair_share(pairs):
    n = len(pairs)

    def body(*refs):
        outs = refs[n:2 * n]
        send_sems, recv_sems = refs[2 * n:]
        x, y, c = _position()
        sends = [_remote(outs[a].at[c], outs[a].at[c], send_sems, recv_sems, a, (x, y, 1 - c)) for a in range(n)]
        for cp in sends:
            cp.start()
        for a in range(n):
            _remote(outs[a].at[1 - c], outs[a].at[1 - c], send_sems, recv_sems, a, (x, y, 1 - c)).wait_recv()
        for cp in sends:
            cp.wait_send()

    return pl.pallas_call(
        body, in_specs=[_HBM] * n, out_specs=[_HBM] * n,
        out_shape=[jax.ShapeDtypeStruct(p.shape, p.dtype) for p in pairs],
        input_output_aliases={a: a for a in range(n)},
        scratch_shapes=[pltpu.SemaphoreType.DMA((n,)), pltpu.SemaphoreType.DMA((n,))],
        name="grad_pair_share")(*pairs)


def _small_allreduce(pack):
    rows, width = pack.shape

    def body(p_ref, o_ref, buf, send_sems, recv_sems):
        x, y, c = _position()
        me = 4 * x + 2 * y + c
        buf[me] = p_ref[...]
        peers = []
        for r in range(1, N_DEV):
            fx, fy, fc = (r >> 2) & 1, (r >> 1) & 1, r & 1
            px, py, pc = (1 - x if fx else x), (1 - y if fy else y), (1 - c if fc else c)
            peers.append(((px, py, pc), 4 * px + 2 * py + pc))
        sends = [_remote(p_ref, buf.at[me], send_sems, recv_sems, r, dev) for r, (dev, _) in enumerate(peers)]
        for cp in sends:
            cp.start()
        for r, (dev, idx) in enumerate(peers):
            _remote(p_ref, buf.at[idx], send_sems, recv_sems, r, dev).wait_recv()
        for cp in sends:
            cp.wait_send()
        acc = buf[0]
        for k in range(1, N_DEV):
            acc = acc + buf[k]
        o_ref[...] = acc

    vm = pl.BlockSpec(memory_space=pltpu.VMEM)
    return pl.pallas_call(
        body, in_specs=[vm], out_specs=vm, out_shape=jax.ShapeDtypeStruct(pack.shape, F32),
        scratch_shapes=[pltpu.VMEM((N_DEV, rows, width), F32), pltpu.SemaphoreType.DMA((N_DEV - 1,)),
                        pltpu.SemaphoreType.DMA((N_DEV - 1,))],
        name="small_allreduce")(pack)


def _row_tile(rows):
    return rows if rows <= 256 else 256


def _pair_add(part, recv, c_idx, name):
    _, nj, rh, cols = part.shape
    tr = _row_tile(rh)

    def body(c_ref, p_ref, r_ref, o_ref):
        o_ref[...] = (p_ref[...].astype(F32) + r_ref[...].astype(F32)).astype(BF16)

    blk = pl.BlockSpec((None, tr, cols), lambda j, i, c_ref: (j, i, 0))
    grid_spec = pltpu.PrefetchScalarGridSpec(
        num_scalar_prefetch=1, grid=(nj, rh // tr),
        in_specs=[pl.BlockSpec((None, None, tr, cols), lambda j, i, c_ref: (c_ref[0], j, i, 0)), blk],
        out_specs=blk)
    return pl.pallas_call(body, grid_spec=grid_spec, out_shape=jax.ShapeDtypeStruct(recv.shape, BF16),
                          name=name, compiler_params=_params())(c_idx, part, recv)


def _chip_add(sums, recv, where, name):
    _, rh, cols = sums.shape
    tr = _row_tile(rh)

    def body(w_ref, s_ref, r_ref, o_ref):
        o_ref[...] = ((s_ref[...].astype(F32) + r_ref[0].astype(F32)) + r_ref[1].astype(F32)) + r_ref[2].astype(F32)

    grid_spec = pltpu.PrefetchScalarGridSpec(
        num_scalar_prefetch=1, grid=(rh // tr,),
        in_specs=[pl.BlockSpec((None, tr, cols), lambda i, w_ref: (w_ref[0], i, 0)),
                  pl.BlockSpec((3, tr, cols), lambda i, w_ref: (0, i, 0))],
        out_specs=pl.BlockSpec((None, tr, cols), lambda i, w_ref: (w_ref[1], i, 0)))
    return pl.pallas_call(body, grid_spec=grid_spec, out_shape=jax.ShapeDtypeStruct((2, rh, cols), F32),
                          name=name, compiler_params=_params())(where, sums, recv)


def _adamw(w, g, m, v, name):
    rows, cols = w.shape
    tr = _row_tile(rows)
    assert rows % tr == 0

    def body(w_ref, g_ref, m_ref, v_ref, d_ref, mo_ref, vo_ref):
        gv = g_ref[...]
        m_new = ADAM_B1 * m_ref[...] + (1.0 - ADAM_B1) * gv
        v_new = ADAM_B2 * v_ref[...] + (1.0 - ADAM_B2) * jnp.square(gv)
        m_hat = m_new / (1.0 - ADAM_B1 ** ADAM_STEP)
        v_hat = v_new / (1.0 - ADAM_B2 ** ADAM_STEP)
        d_ref[...] = -ADAM_LR * (m_hat / (jnp.sqrt(v_hat) + ADAM_EPS) + ADAM_WD * w_ref[...])
        mo_ref[...] = m_new
        vo_ref[...] = v_new

    blk = pl.BlockSpec((tr, cols), lambda i: (i, 0))
    shp = jax.ShapeDtypeStruct((rows, cols), F32)
    return pl.pallas_call(body, grid=(rows // tr,), in_specs=[blk] * 4, out_specs=[blk] * 3,
                          out_shape=[shp] * 3, name=name, compiler_params=_params())(w, g, m, v)


SHARD_W = IN_WIDTH // N_CHIPS


def _half_major(a):
    r, c = a.shape
    return a.reshape(N_CHIPS, 2, r // N_CHIPS // 2, c).transpose(1, 0, 2, 3)


def _w_in_permuted(pieces):
    cols = []
    for a, b in PERM_SEGS:
        pos = a
        while pos < b:
            j = pos // SHARD_W
            stop = min(b, (j + 1) * SHARD_W)
            cols.append(pieces[j][:, pos - j * SHARD_W:stop - j * SHARD_W])
            pos = stop
    return jnp.concatenate(cols, axis=1)


def _ref_cols(a_p, lo, hi):
    out, ref_off = [], 0
    for a, b in UNPERM_SEGS:
        r0, r1 = ref_off, ref_off + (b - a)
        s, e = max(lo, r0), min(hi, r1)
        if s < e:
            out.append(a_p[:, a + (s - r0):a + (e - r0)])
        ref_off = r1
    return jnp.concatenate(out, axis=1)


def kernel(x, mem, g_pre, w_in, w_conv, attn_sink, g_mem, w_mem_kv, w_up_a, w_up_b, w_up_m, w_out, g_post, loss_target, m_g_pre, m_w_in, m_w_conv, m_attn_sink, m_g_mem, m_w_mem_kv, m_w_up_a, m_w_up_b, m_w_up_m, m_w_out, m_g_post, v_g_pre, v_w_in, v_w_conv, v_attn_sink, v_g_mem, v_w_mem_kv, v_w_up_a, v_w_up_b, v_w_up_m, v_w_out, v_g_post):
    xi, yi, ci = _position()
    chip = 2 * xi + yi
    where = jnp.stack([chip, ci]).astype(jnp.int32)
    c_idx = jnp.reshape(ci, (1,)).astype(jnp.int32)

    own = [w_in[0].astype(BF16), w_mem_kv[0].astype(BF16),
           jnp.concatenate([w_up_a[0], w_up_b[0], w_up_m[0]], axis=0).astype(BF16), w_out[0].astype(BF16)]
    own_conv = jnp.pad(w_conv[0], ((0, 5), (0, 0)))
    *gathered, g_conv = _gather_shards(own, own_conv)

    def pieces(mine, got):
        return [jnp.where(chip == j, mine, got[j]) for j in range(N_CHIPS)]

    w_in_p = _w_in_permuted(pieces(own[0], gathered[0]))
    w_kv_full = jnp.concatenate(pieces(own[1], gathered[1]), axis=0)
    up_pieces = pieces(own[2], gathered[2])
    w_up_full = jnp.stack([jnp.concatenate([p[k * A_WIDTH:(k + 1) * A_WIDTH] for p in up_pieces], axis=1)
                           for k in range(3)])
    w_out_full = jnp.concatenate(pieces(own[3], gathered[3]), axis=0)
    w_conv_full = jnp.concatenate([p[:3] for p in pieces(own_conv, g_conv)], axis=1)

    g = _local_grads(x[0], mem[0], loss_target[0], g_pre, w_in_p, w_conv_full, attn_sink, g_mem, w_kv_full,
                     w_up_full, w_out_full, g_post)

    zeros512 = jnp.zeros((1, D_MODEL - A_WIDTH), F32)
    conv_rows = [jnp.concatenate([g["w_conv"][k:k + 1], zeros512], axis=1) for k in range(3)]
    sink_row = jnp.pad(g["sink"][:, 0].reshape(1, N_Q_HEADS), ((0, 0), (0, D_MODEL - N_Q_HEADS)))
    loss_row = jnp.pad(g["loss"], ((0, 0), (0, D_MODEL - LANES)))
    pack = jnp.concatenate([g["g_pre"], g["g_mem"], g["g_post"]] + conv_rows + [sink_row, loss_row], axis=0)
    red = _small_allreduce(pack)
    loss = red[7, 0]
    small_grads = dict(
        g_pre=red[0:1], g_mem=red[1:2], g_post=red[2:3], attn_sink=red[6:7, :N_Q_HEADS],
        w_conv=lax.dynamic_slice(red[3:6, :A_WIDTH], (0, chip * LANES), (3, LANES)))

    half_rows = D_MODEL // 2
    in_parts = jnp.stack([jnp.stack([_ref_cols(g["w_in_p"][hf * half_rows:(hf + 1) * half_rows],
                                               j * SHARD_W, (j + 1) * SHARD_W) for j in range(N_CHIPS)])
                          for hf in range(2)]).astype(BF16)
    up_parts = (g["w_up"].reshape(3, A_WIDTH, N_CHIPS, D_MODEL // N_CHIPS).transpose(2, 0, 1, 3)
                .reshape(N_CHIPS, 2, 3 * A_WIDTH // 2, D_MODEL // N_CHIPS).transpose(1, 0, 2, 3)).astype(BF16)
    parts = [in_parts, _half_major(g["w_kv"]).astype(BF16), up_parts, _half_major(g["w_out"]).astype(BF16)]
    names = ["w_in", "w_kv", "w_up", "w_out"]
    recv = _pair_exchange(parts)
    sums = [_pair_add(p, r, c_idx, "pair_add_" + nm) for p, r, nm in zip(parts, recv, names)]
    recv3 = _chip_exchange(sums)
    pairs = [_chip_add(s, r, where, "chip_add_" + nm) for s, r, nm in zip(sums, recv3, names)]
    full = _pair_share(pairs)
    gw_up = full[2].reshape(3, A_WIDTH, D_MODEL // N_CHIPS)
    grads = dict(small_grads, w_in=full[0].reshape(D_MODEL, SHARD_W),
                 w_mem_kv=full[1].reshape(D_MODEL // N_CHIPS, 2 * MEM_WIDTH),
                 w_up_a=gw_up[0], w_up_b=gw_up[1], w_up_m=gw_up[2],
                 w_out=full[3].reshape(D_MODEL // N_CHIPS, D_MODEL))

    weights = dict(g_pre=g_pre, w_in=w_in, w_conv=w_conv, attn_sink=attn_sink, g_mem=g_mem, w_mem_kv=w_mem_kv,
                   w_up_a=w_up_a, w_up_b=w_up_b, w_up_m=w_up_m, w_out=w_out, g_post=g_post)
    m_in = dict(g_pre=m_g_pre, w_in=m_w_in, w_conv=m_w_conv, attn_sink=m_attn_sink, g_mem=m_g_mem,
                w_mem_kv=m_w_mem_kv, w_up_a=m_w_up_a, w_up_b=m_w_up_b, w_up_m=m_w_up_m, w_out=m_w_out,
                g_post=m_g_post)
    v_in = dict(g_pre=v_g_pre, w_in=v_w_in, w_conv=v_w_conv, attn_sink=v_attn_sink, g_mem=v_g_mem,
                w_mem_kv=v_w_mem_kv, w_up_a=v_w_up_a, w_up_b=v_w_up_b, w_up_m=v_w_up_m, w_out=v_w_out,
                g_post=v_g_post)
    out_g, out_d, out_m, out_v = [], [], [], []
    for nm in ("g_pre", "w_in", "w_conv", "attn_sink", "g_mem", "w_mem_kv", "w_up_a", "w_up_b", "w_up_m", "w_out",
               "g_post"):
        shape = weights[nm].shape
        two_d = shape[-2:]
        gr = grads[nm].reshape(two_d)
        d, m_new, v_new = _adamw(weights[nm].reshape(two_d), gr, m_in[nm].reshape(two_d), v_in[nm].reshape(two_d),
                                 "adamw_" + nm)
        out_g.append(gr.reshape(shape))
        out_d.append(d.reshape(shape))
        out_m.append(m_new.reshape(shape))
        out_v.append(v_new.reshape(shape))
    return (loss, g["grad_x"].reshape(x.shape), *out_g, *out_d, *out_m, *out_v)
```

```python
import functools

import jax
import jax.numpy as jnp
from jax import lax
from jax.experimental import pallas as pl
from jax.experimental.pallas import tpu as pltpu

F32 = jnp.float32
BF16 = jnp.bfloat16
MESH = pl.DeviceIdType.MESH

D_MODEL = 1024
EPS = 1e-6
A_WIDTH = 512
HEAD_DIM = 64
N_Q_HEADS = 8
WINDOW_BLOCK = 128
ROPE_THETA = 500000.0
ROT_DIM = 16
MEM_HEADS = 4
MEM_HEAD_DIM = 128
MEM_WIDTH = 512
IN_WIDTH = 7424
N_CHIPS = 4
LANES = 128
HALF_LANES = 64

PERM_SEGS = ((0, 2560), (2816, 3328), (4352, 7424), (3328, 4352), (2560, 2816))
UNPERM_SEGS = ((0, 2560), (7168, 7424), (2560, 3072), (6144, 7168), (3072, 6144))
COL_A, W_A = 0, 2048
COL_B, W_B = 2, 1024
COL_G, W_G = 1, 3072
COL_M, W_M = 6, 1024
COL_KV, W_KV = 28, 256

ADAM_LR = 0.001
ADAM_B1 = 0.9
ADAM_B2 = 0.999
ADAM_EPS = 1e-08
ADAM_WD = 0.01
ADAM_STEP = 10

VMEM_LIMIT_BYTES = 48 * 1024 * 1024


_HBM = pl.BlockSpec(memory_space=pltpu.HBM)


def _params(**kw):
    return pltpu.CompilerParams(vmem_limit_bytes=VMEM_LIMIT_BYTES, **kw)


def _sigmoid(v):
    return jax.nn.sigmoid(v)


_DIMS = {"nn": (((1,), (0,)), ((), ())), "nt": (((1,), (1,)), ((), ())), "tn": (((0,), (0,)), ((), ()))}


class _Carry:
    def __init__(self, ins, out_shapes, sems, start, finish, aliases=None):
        self.ins, self.out_shapes, self.sems = list(ins), list(out_shapes), list(sems)
        self.start, self.finish, self.aliases = start, finish, dict(aliases or {})


def _carried_call(body, carry, *, grid, in_specs, out_specs, out_shape, scratch, operands, name):
    n_in, n_out, n_scr = len(in_specs), len(out_specs), len(scratch)
    c_in = len(carry.ins) if carry else 0
    c_out = len(carry.out_shapes) if carry else 0
    steps = 1
    for g in grid:
        steps *= g

    def wrapped(*refs):
        ins, cins = refs[:n_in], refs[n_in:n_in + c_in]
        outs = refs[n_in + c_in:n_in + c_in + n_out]
        couts = refs[n_in + c_in + n_out:n_in + c_in + n_out + c_out]
        rest = refs[n_in + c_in + n_out + c_out:]
        scr, sems = rest[:n_scr], rest[n_scr:]
        if carry:
            step = pl.program_id(0)
            for ax in range(1, len(grid)):
                step = step * grid[ax] + pl.program_id(ax)

            @pl.when(step == 0)
            def _():
                carry.start(cins, couts, sems)

        body(ins, outs, scr)
        if carry:
            @pl.when(step == steps - 1)
            def _():
                carry.finish(cins, couts, sems)

    aliases = {n_in + i: n_out + o for i, o in carry.aliases.items()} if carry else {}
    results = pl.pallas_call(
        wrapped, grid=grid, in_specs=list(in_specs) + [_HBM] * c_in, out_specs=list(out_specs) + [_HBM] * c_out,
        out_shape=list(out_shape) + (carry.out_shapes if carry else []),
        scratch_shapes=list(scratch) + (carry.sems if carry else []), input_output_aliases=aliases,
        name=name, compiler_params=_params())(*operands, *(carry.ins if carry else []))
    return results[:n_out], results[n_out:]


def _matmul(a, b, *, mode, out_dtype, tm, tn, tk, name, j_outer=False, carry=None):
    if mode == "nn":
        (m, k), (_, n) = a.shape, b.shape
    elif mode == "nt":
        (m, k), (n, _) = a.shape, b.shape
    else:
        (k, m), (_, n) = a.shape, b.shape
    tm, tn, tk = min(tm, m), min(tn, n), min(tk, k)
    assert m % tm == 0 and n % tn == 0 and k % tk == 0
    ni, nj, nk = m // tm, n // tn, k // tk
    dims = _DIMS[mode]

    def ij(g0, g1):
        return (g1, g0) if j_outer else (g0, g1)

    if mode == "nn":
        a_spec = pl.BlockSpec((tm, tk), lambda g0, g1, kk: (ij(g0, g1)[0], kk))
        b_spec = pl.BlockSpec((tk, tn), lambda g0, g1, kk: (kk, ij(g0, g1)[1]))
    elif mode == "nt":
        a_spec = pl.BlockSpec((tm, tk), lambda g0, g1, kk: (ij(g0, g1)[0], kk))
        b_spec = pl.BlockSpec((tn, tk), lambda g0, g1, kk: (ij(g0, g1)[1], kk))
    else:
        a_spec = pl.BlockSpec((tk, tm), lambda g0, g1, kk: (kk, ij(g0, g1)[0]))
        b_spec = pl.BlockSpec((tk, tn), lambda g0, g1, kk: (kk, ij(g0, g1)[1]))
    o_spec = pl.BlockSpec((tm, tn), lambda g0, g1, kk: ij(g0, g1))

    def part(a_ref, b_ref):
        return lax.dot_general(a_ref[...].astype(BF16), b_ref[...].astype(BF16), dims,
                               preferred_element_type=F32)

    if nk == 1:
        def body(ins, outs, scr):
            outs[0][...] = part(*ins).astype(out_dtype)
        scratch = []
    else:
        def body(ins, outs, scr):
            kk = pl.program_id(2)
            acc_ref = scr[0]

            @pl.when(kk == 0)
            def _():
                acc_ref[...] = part(*ins)

            @pl.when(kk > 0)
            def _():
                acc_ref[...] += part(*ins)

            @pl.when(kk == nk - 1)
            def _():
                outs[0][...] = acc_ref[...].astype(out_dtype)
        scratch = [pltpu.VMEM((tm, tn), F32)]

    grid = (nj, ni, nk) if j_outer else (ni, nj, nk)
    (out,), carried = _carried_call(
        body, carry, grid=grid, in_specs=[a_spec, b_spec], out_specs=[o_spec],
        out_shape=[jax.ShapeDtypeStruct((m, n), out_dtype)], scratch=scratch, operands=(a, b), name=name)
    return (out, carried) if carry else out


def _rmsnorm_fwd(x, g, *, name):
    s, d = x.shape
    ts = min(512, s)

    def body(x_ref, g_ref, o_ref):
        xv = x_ref[...]
        r = lax.rsqrt(jnp.mean(xv * xv, axis=-1, keepdims=True) + EPS)
        o_ref[...] = ((xv * r) * g_ref[...]).astype(BF16)

    return pl.pallas_call(
        body, grid=(s // ts,),
        in_specs=[pl.BlockSpec((ts, d), lambda i: (i, 0)), pl.BlockSpec((1, d), lambda i: (0, 0))],
        out_specs=pl.BlockSpec((ts, d), lambda i: (i, 0)),
        out_shape=jax.ShapeDtypeStruct((s, d), BF16), name=name, compiler_params=_params())(x, g)


def _rmsnorm_bwd(dh, x, g, res, *, name, carry=None):
    s, d = x.shape
    ts = min(256, s)

    def body(ins, outs, scr):
        dh_ref, x_ref, g_ref, res_ref = ins
        dx_ref, dg_ref = outs
        xv = x_ref[...]
        r = lax.rsqrt(jnp.mean(xv * xv, axis=-1, keepdims=True) + EPS)
        xh = xv * r
        dhv = dh_ref[...]
        part = jnp.sum(dhv * xh, axis=0, keepdims=True)

        @pl.when(pl.program_id(0) == 0)
        def _():
            dg_ref[...] = part

        @pl.when(pl.program_id(0) > 0)
        def _():
            dg_ref[...] += part

        dxh = dhv * g_ref[...]
        dx_ref[...] = res_ref[...] + r * (dxh - xh * jnp.mean(dxh * xh, axis=-1, keepdims=True))

    row = pl.BlockSpec((ts, d), lambda i: (i, 0))
    vec = pl.BlockSpec((1, d), lambda i: (0, 0))
    outs, carried = _carried_call(
        body, carry, grid=(s // ts,), in_specs=[row, row, vec, row], out_specs=[row, vec],
        out_shape=[jax.ShapeDtypeStruct((s, d), F32), jax.ShapeDtypeStruct((1, d), F32)],
        scratch=[], operands=(dh, x, g, res), name=name)
    return (*outs, carried) if carry else tuple(outs)


MID_TILE = 256


def _gated_branches(y_refs, wup_ref, gl):
    d = D_MODEL
    us = [jnp.dot(y_refs[k][...], wup_ref[k], preferred_element_type=F32) for k in range(3)]
    sg = [_sigmoid(gl[:, k * d:(k + 1) * d]) for k in range(3)]
    return us, sg


def _mid_fwd(ya, yb, ym, proj, x, tgt, w_up, w_out, g_post):
    s, d = x.shape
    ts = MID_TILE

    def body(ya_ref, yb_ref, ym_ref, g_ref, x_ref, t_ref, wup_ref, wout_ref, gp_ref,
             m_ref, do_ref, dy_ref, dg_ref, loss_ref):
        us, sg = _gated_branches((ya_ref, yb_ref, ym_ref), wup_ref, g_ref[...])
        merged = (sg[0] * us[0] + sg[1] * us[1] + sg[2] * us[2]).astype(BF16)
        m_ref[...] = merged
        ov = jnp.dot(merged, wout_ref[...], preferred_element_type=F32)
        r = lax.rsqrt(jnp.mean(ov * ov, axis=-1, keepdims=True) + EPS)
        nh = ov * r
        gv = gp_ref[...]
        e = (x_ref[...] + nh * gv) - t_ref[...]
        lpart = 0.5 * jnp.sum(jnp.mean(e * e, axis=-1, keepdims=True), axis=0, keepdims=True)
        dy = e * (1.0 / d)
        dgp = jnp.sum(dy * nh, axis=0, keepdims=True)

        @pl.when(pl.program_id(0) == 0)
        def _():
            dg_ref[...] = dgp
            loss_ref[...] = jnp.broadcast_to(lpart, loss_ref.shape)

        @pl.when(pl.program_id(0) > 0)
        def _():
            dg_ref[...] += dgp
            loss_ref[...] += jnp.broadcast_to(lpart, loss_ref.shape)

        dn = dy * gv
        dy_ref[...] = dy
        do_ref[...] = (r * (dn - nh * jnp.mean(dn * nh, axis=-1, keepdims=True))).astype(BF16)

    row = pl.BlockSpec((ts, d), lambda i: (i, 0))
    ysp = pl.BlockSpec((ts, A_WIDTH), lambda i: (i, 0))
    vec = pl.BlockSpec((1, d), lambda i: (0, 0))
    return pl.pallas_call(
        body, grid=(s // ts,),
        in_specs=[ysp, ysp, ysp, pl.BlockSpec((ts, W_G), lambda i: (i, COL_G)), row, row,
                  pl.BlockSpec((3, A_WIDTH, d), lambda i: (0, 0, 0)), pl.BlockSpec((d, d), lambda i: (0, 0)), vec],
        out_specs=[row, row, row, vec, pl.BlockSpec((1, LANES), lambda i: (0, 0))],
        out_shape=[jax.ShapeDtypeStruct((s, d), BF16), jax.ShapeDtypeStruct((s, d), BF16),
                   jax.ShapeDtypeStruct((s, d), F32), jax.ShapeDtypeStruct((1, d), F32),
                   jax.ShapeDtypeStruct((1, LANES), F32)],
        name="mid_fwd", compiler_params=_params())(ya, yb, ym, proj, x, tgt, w_up, w_out, g_post)


def _mid_bwd(d_out, merged, ya, yb, ym, proj, w_up, w_out):
    s, d = merged.shape
    ts = MID_TILE
    last = s // ts - 1

    def body(do_ref, m_ref, ya_ref, yb_ref, ym_ref, g_ref, wup_ref, wout_ref,
             dp_ref, dya_ref, dyb_ref, dym_ref, dwup_hbm, dwout_hbm, dwup_acc, dwout_acc):
        i = pl.program_id(0)

        @pl.when(i == 0)
        def _():
            dwup_acc[...] = jnp.zeros_like(dwup_acc)
            dwout_acc[...] = jnp.zeros_like(dwout_acc)

        y_refs = (ya_ref, yb_ref, ym_ref)
        us, sg = _gated_branches(y_refs, wup_ref, g_ref[...])
        dov = do_ref[...]
        dwout_acc[...] += lax.dot_general(m_ref[...], dov, _DIMS["tn"], preferred_element_type=F32)
        dm = lax.dot_general(dov, wout_ref[...], _DIMS["nt"], preferred_element_type=F32)
        for k, dy_ref in enumerate((dya_ref, dyb_ref, dym_ref)):
            dp_ref[:, k * d:(k + 1) * d] = ((dm * us[k]) * (sg[k] * (1.0 - sg[k]))).astype(BF16)
            du = (sg[k] * dm).astype(BF16)
            dy_ref[...] = lax.dot_general(du, wup_ref[k], _DIMS["nt"], preferred_element_type=F32)
            dwup_acc[k] += lax.dot_general(y_refs[k][...], du, _DIMS["tn"], preferred_element_type=F32)

        @pl.when(i == last)
        def _():
            pltpu.sync_copy(dwup_acc, dwup_hbm)
            pltpu.sync_copy(dwout_acc, dwout_hbm)

    row = pl.BlockSpec((ts, d), lambda i: (i, 0))
    ysp = pl.BlockSpec((ts, A_WIDTH), lambda i: (i, 0))
    gsp = pl.BlockSpec((ts, W_G), lambda i: (i, COL_G))
    anysp = pl.BlockSpec(memory_space=pl.ANY)
    yshape = jax.ShapeDtypeStruct((s, A_WIDTH), F32)
    return pl.pallas_call(
        body, grid=(s // ts,),
        in_specs=[row, row, ysp, ysp, ysp, gsp, pl.BlockSpec((3, A_WIDTH, d), lambda i: (0, 0, 0)),
                  pl.BlockSpec((d, d), lambda i: (0, 0))],
        out_specs=[gsp, ysp, ysp, ysp, anysp, anysp],
        out_shape=[jax.ShapeDtypeStruct((s, IN_WIDTH), BF16), yshape, yshape, yshape,
                   jax.ShapeDtypeStruct((3, A_WIDTH, d), F32), jax.ShapeDtypeStruct((d, d), F32)],
        scratch_shapes=[pltpu.VMEM((3, A_WIDTH, d), F32), pltpu.VMEM((d, d), F32)],
        name="mid_bwd", compiler_params=_params())(d_out, merged, ya, yb, ym, proj, w_up, w_out)


def _conv_core(blk, prev, nxt, w, i, last, ts):
    c = A_WIDTH
    ab, ac, ax, az = blk[:, :c], blk[:, c:2 * c], blk[:, 2 * c:3 * c], blk[:, 3 * c:]
    cu = ac * ax
    cu_prev = (prev[7:8, c:2 * c] * prev[7:8, 2 * c:3 * c]) * jnp.where(i > 0, 1.0, 0.0)
    cu_next = (nxt[0:1, c:2 * c] * nxt[0:1, 2 * c:3 * c]) * jnp.where(i < last, 1.0, 0.0)
    row = lax.broadcasted_iota(jnp.int32, (ts, c), 0)
    cm1 = jnp.where(row == 0, cu_prev, pltpu.roll(cu, 1, 0))
    cp1 = jnp.where(row == ts - 1, cu_next, pltpu.roll(cu, ts - 1, 0))
    yc = cm1 * w[0:1] + cu * w[1:2] + cp1 * w[2:3]
    return ab, ac, ax, az, cu, cm1, cp1, yc, row


def _halo_specs(ts, width, col, nblk8):
    prev = pl.BlockSpec((8, width), lambda i: (jnp.maximum(i * (ts // 8) - 1, 0), col))
    nxt = pl.BlockSpec((8, width), lambda i: (jnp.minimum((i + 1) * (ts // 8), nblk8 - 1), col))
    return prev, nxt


def _conv_fwd(proj, w_conv):
    s = proj.shape[0]
    ts = 256
    last = s // ts - 1

    def body(a_ref, ap_ref, an_ref, w_ref, ya_ref):
        i = pl.program_id(0)
        ab, _, _, az, _, _, _, yc, _ = _conv_core(a_ref[...], ap_ref[...], an_ref[...], w_ref[...], i, last, ts)
        ya_ref[...] = ((ab * yc) * (az * _sigmoid(az))).astype(BF16)

    prev, nxt = _halo_specs(ts, W_A, COL_A, s // 8)
    return pl.pallas_call(
        body, grid=(s // ts,),
        in_specs=[pl.BlockSpec((ts, W_A), lambda i: (i, COL_A)), prev, nxt,
                  pl.BlockSpec((3, A_WIDTH), lambda i: (0, 0))],
        out_specs=pl.BlockSpec((ts, A_WIDTH), lambda i: (i, 0)),
        out_shape=jax.ShapeDtypeStruct((s, A_WIDTH), BF16), name="conv_fwd",
        compiler_params=_params())(proj, proj, proj, w_conv)


def _conv_bwd(proj, w_conv, dya, dproj):
    s = proj.shape[0]
    ts = 256
    last = s // ts - 1
    c = A_WIDTH

    def body(a_ref, ap_ref, an_ref, w_ref, d_ref, dp_ref, dn_ref, _, dproj_ref, dw_ref):
        i = pl.program_id(0)
        w = w_ref[...]
        prev, nxt = ap_ref[...], an_ref[...]
        ab, ac, ax, az, cu, cm1, cp1, yc, row = _conv_core(a_ref[...], prev, nxt, w, i, last, ts)
        sg = _sigmoid(az)
        sz = az * sg
        dya_v = d_ref[...]
        dyc = dya_v * sz * ab
        dproj_ref[:, :c] = (dya_v * sz * yc).astype(BF16)
        dproj_ref[:, 3 * c:] = (dya_v * (ab * yc) * (sg * (1.0 + az * (1.0 - sg)))).astype(BF16)

        def halo_dyc(a_row, d_row):
            azr = a_row[:, 3 * c:]
            return d_row * (azr * _sigmoid(azr)) * a_row[:, :c]

        dyc_prev = halo_dyc(prev[7:8], dp_ref[...][7:8]) * jnp.where(i > 0, 1.0, 0.0)
        dyc_next = halo_dyc(nxt[0:1], dn_ref[...][0:1]) * jnp.where(i < last, 1.0, 0.0)
        dyc_m1 = jnp.where(row == 0, dyc_prev, pltpu.roll(dyc, 1, 0))
        dyc_p1 = jnp.where(row == ts - 1, dyc_next, pltpu.roll(dyc, ts - 1, 0))
        dcu = dyc_p1 * w[0:1] + dyc * w[1:2] + dyc_m1 * w[2:3]
        dproj_ref[:, c:2 * c] = (dcu * ax).astype(BF16)
        dproj_ref[:, 2 * c:3 * c] = (dcu * ac).astype(BF16)
        dw = [jnp.sum(dyc * t, axis=0, keepdims=True) for t in (cm1, cu, cp1)]

        @pl.when(i == 0)
        def _():
            for k in range(3):
                dw_ref[k:k + 1, :] = dw[k]

        @pl.when(i > 0)
        def _():
            for k in range(3):
                dw_ref[k:k + 1, :] += dw[k]

    prev, nxt = _halo_specs(ts, W_A, COL_A, s // 8)
    dprev, dnxt = _halo_specs(ts, A_WIDTH, 0, s // 8)
    return pl.pallas_call(
        body, grid=(s // ts,),
        in_specs=[pl.BlockSpec((ts, W_A), lambda i: (i, COL_A)), prev, nxt,
                  pl.BlockSpec((3, A_WIDTH), lambda i: (0, 0)),
                  pl.BlockSpec((ts, A_WIDTH), lambda i: (i, 0)), dprev, dnxt,
                  pl.BlockSpec(memory_space=pl.ANY)],
        out_specs=[pl.BlockSpec((ts, W_A), lambda i: (i, COL_A)), pl.BlockSpec((3, A_WIDTH), lambda i: (0, 0))],
        out_shape=[jax.ShapeDtypeStruct(dproj.shape, BF16), jax.ShapeDtypeStruct((3, A_WIDTH), F32)],
        input_output_aliases={7: 0}, name="conv_bwd",
        compiler_params=_params())(proj, proj, proj, w_conv, dya, dya, dya, dproj)


def _rope_tables(s):
    half = ROT_DIM // 2
    inv_freq = jnp.power(jnp.float32(ROPE_THETA), -jnp.arange(half, dtype=F32) * (2.0 / ROT_DIM))
    ang = jnp.arange(s).astype(F32)[:, None] * inv_freq[None, :]
    cos, sin = jnp.cos(ang), jnp.sin(ang)
    pad = jnp.zeros((s, HEAD_DIM - ROT_DIM), F32)
    c = jnp.concatenate([cos, cos, pad + 1.0], axis=1)
    s1 = jnp.concatenate([-sin, jnp.zeros_like(sin), pad], axis=1)
    s2 = jnp.concatenate([jnp.zeros_like(sin), sin, pad], axis=1)
    return jnp.concatenate([c, c, s1, s1, s2, s2], axis=1)


def _rope(t, tab):
    return (t * tab[:, :LANES] + pltpu.roll(t, LANES - 8, 1) * tab[:, LANES:2 * LANES]
            + pltpu.roll(t, 8, 1) * tab[:, 2 * LANES:])


def _rope_transpose(dt, tab):
    return (dt * tab[:, :LANES] + pltpu.roll(dt * tab[:, LANES:2 * LANES], 8, 1)
            + pltpu.roll(dt * tab[:, 2 * LANES:], LANES - 8, 1))


def _rope_kv(proj, tab):
    s = proj.shape[0]
    nb = s // WINDOW_BLOCK

    def body(kv_ref, t_ref, k_ref, v_ref):
        j = pl.program_id(0)
        inside = jnp.where((j > 0) & (j <= nb), 1.0, 0.0)
        kv = kv_ref[...]
        k_ref[...] = (_rope(kv[:, :LANES], t_ref[...]) * inside).astype(BF16)
        v_ref[...] = (kv[:, LANES:] * inside).astype(BF16)

    def src(j):
        return jnp.clip(j - 1, 0, nb - 1)

    o_spec = pl.BlockSpec((WINDOW_BLOCK, LANES), lambda j: (j, 0))
    shp = jax.ShapeDtypeStruct((s + 2 * WINDOW_BLOCK, LANES), BF16)
    return pl.pallas_call(
        body, grid=(nb + 2,),
        in_specs=[pl.BlockSpec((WINDOW_BLOCK, W_KV), lambda j: (src(j), COL_KV)),
                  pl.BlockSpec((WINDOW_BLOCK, 3 * LANES), lambda j: (src(j), 0))],
        out_specs=[o_spec, o_spec], out_shape=[shp, shp], name="rope_kv",
        compiler_params=_params())(proj, tab)


def _rope_kv_bwd(dkpad, dvpad, tab, dproj):
    s = tab.shape[0]
    nb = s // WINDOW_BLOCK

    def body(dk_ref, dv_ref, t_ref, _, dp_ref):
        dp_ref[:, :LANES] = _rope_transpose(dk_ref[...], t_ref[...]).astype(BF16)
        dp_ref[:, LANES:] = dv_ref[...].astype(BF16)

    pad_spec = pl.BlockSpec((WINDOW_BLOCK, LANES), lambda j: (j + 1, 0))
    return pl.pallas_call(
        body, grid=(nb,),
        in_specs=[pad_spec, pad_spec, pl.BlockSpec((WINDOW_BLOCK, 3 * LANES), lambda j: (j, 0)),
                  pl.BlockSpec(memory_space=pl.ANY)],
        out_specs=pl.BlockSpec((WINDOW_BLOCK, W_KV), lambda j: (j, COL_KV)),
        out_shape=jax.ShapeDtypeStruct(dproj.shape, BF16), input_output_aliases={3: 0},
        name="rope_kv_bwd", compiler_params=_params())(dkpad, dvpad, tab, dproj)


def _window_operands(k_ref, v_ref, n, lo):
    start = pl.multiple_of(n * WINDOW_BLOCK, WINDOW_BLOCK)
    kw = k_ref[pl.ds(start, 3 * WINDOW_BLOCK), :].astype(F32)
    vw = v_ref[pl.ds(start, 3 * WINDOW_BLOCK), :].astype(F32)
    kr, vr = pltpu.roll(kw, HALF_LANES, 1), pltpu.roll(vw, HALF_LANES, 1)
    k2 = (jnp.where(lo, kw, kr).astype(BF16), jnp.where(lo, kr, kw).astype(BF16))
    v2 = (jnp.where(lo, vw, vr).astype(BF16), jnp.where(lo, vr, vw).astype(BF16))
    return k2, v2


HEADS_PER_GROUP = 4


def _window_mask(n, s):
    wb = WINDOW_BLOCK
    shape = (HEADS_PER_GROUP * wb, 3 * wb)
    qi = lax.broadcasted_iota(jnp.int32, shape, 0) & (wb - 1)
    kj = lax.broadcasted_iota(jnp.int32, shape, 1)
    kpos = kj + (n - 1) * wb
    return (kj >= qi) & (kj <= qi + 2 * wb) & (kpos >= 0) & (kpos < s)


def _stack_heads(pair0, pair1, lo):
    return jnp.concatenate([jnp.where(lo, pair0, 0.0), jnp.where(lo, 0.0, pair0),
                            jnp.where(lo, pair1, 0.0), jnp.where(lo, 0.0, pair1)], axis=0)


def _unstack_pair(stacked, i, lo):
    wb = WINDOW_BLOCK
    return jnp.where(lo, stacked[2 * i * wb:(2 * i + 1) * wb], stacked[(2 * i + 1) * wb:(2 * i + 2) * wb])


def _sink_column(sink_ref, g):
    wb = WINDOW_BLOCK
    return jnp.concatenate([jnp.full((wb, 1), sink_ref[0, HEADS_PER_GROUP * g + i], F32)
                            for i in range(HEADS_PER_GROUP)], axis=0)


def _head_probs(q4, k2g, valid, sink):
    sc = lax.dot_general(q4, k2g, _DIMS["nt"], preferred_element_type=F32) * (HEAD_DIM ** -0.5)
    sc = jnp.where(valid, sc, -jnp.inf)
    m = jnp.maximum(jnp.max(sc, axis=1, keepdims=True), sink)
    e = jnp.exp(sc - m)
    es = jnp.exp(sink - m)
    inv = 1.0 / (jnp.sum(e, axis=1, keepdims=True) + es)
    return e * inv, es * inv


def _swa_fwd(proj, kpad, vpad, tab, sink):
    s = proj.shape[0]
    wb = WINDOW_BLOCK

    def body(b_ref, k_ref, v_ref, t_ref, sink_ref, o_ref, y_ref):
        n = pl.program_id(0)
        lo = lax.broadcasted_iota(jnp.int32, (wb, LANES), 1) < HALF_LANES
        lo_w = lax.broadcasted_iota(jnp.int32, (3 * wb, LANES), 1) < HALF_LANES
        k2, v2 = _window_operands(k_ref, v_ref, n, lo_w)
        valid = _window_mask(n, s)
        tab_v = t_ref[...]
        for g in range(2):
            qr = [_rope(b_ref[:, (2 * g + i) * LANES:(2 * g + i + 1) * LANES], tab_v) for i in range(2)]
            q4 = _stack_heads(qr[0], qr[1], lo).astype(BF16)
            prob, _ = _head_probs(q4, k2[g], valid, _sink_column(sink_ref, g))
            o4 = jnp.dot(prob.astype(BF16), v2[g], preferred_element_type=F32)
            for i in range(2):
                cols = slice((2 * g + i) * LANES, (2 * g + i + 1) * LANES)
                op = _unstack_pair(o4, i, lo)
                o_ref[:, cols] = op
                zp = b_ref[:, A_WIDTH + cols.start:A_WIDTH + cols.stop]
                y_ref[:, cols] = (op * (zp * _sigmoid(zp))).astype(BF16)

    pad_spec = pl.BlockSpec((s + 2 * wb, LANES), lambda n: (0, 0))
    o_spec = pl.BlockSpec((wb, A_WIDTH), lambda n: (n, 0))
    return pl.pallas_call(
        body, grid=(s // wb,),
        in_specs=[pl.BlockSpec((wb, W_B), lambda n: (n, COL_B)), pad_spec, pad_spec,
                  pl.BlockSpec((wb, 3 * LANES), lambda n: (n, 0)),
                  pl.BlockSpec(memory_space=pltpu.SMEM)],
        out_specs=[o_spec, o_spec],
        out_shape=[jax.ShapeDtypeStruct((s, A_WIDTH), F32), jax.ShapeDtypeStruct((s, A_WIDTH), BF16)],
        name="swa_fwd", compiler_params=_params())(proj, kpad, vpad, tab, sink)


def _swa_bwd(proj, kpad, vpad, tab, sink, o_attn, dyb, dproj):
    s = proj.shape[0]
    wb = WINDOW_BLOCK
    scale = HEAD_DIM ** -0.5

    def body(b_ref, k_ref, v_ref, t_ref, sink_ref, o_ref, dy_ref, _, dp_ref, dk_ref, dv_ref, ds_ref):
        n = pl.program_id(0)

        @pl.when(n == 0)
        def _():
            dk_ref[...] = jnp.zeros_like(dk_ref)
            dv_ref[...] = jnp.zeros_like(dv_ref)
            ds_ref[...] = jnp.zeros_like(ds_ref)

        lo = lax.broadcasted_iota(jnp.int32, (wb, LANES), 1) < HALF_LANES
        lo_w = lax.broadcasted_iota(jnp.int32, (3 * wb, LANES), 1) < HALF_LANES
        k2, v2 = _window_operands(k_ref, v_ref, n, lo_w)
        valid = _window_mask(n, s)
        tab_v = t_ref[...]
        dks, dvs = [], []
        for g in range(2):
            qr, op, do = [], [], []
            for i in range(2):
                cols = slice((2 * g + i) * LANES, (2 * g + i + 1) * LANES)
                zcols = slice(A_WIDTH + cols.start, A_WIDTH + cols.stop)
                qr.append(_rope(b_ref[:, cols], tab_v))
                zp = b_ref[:, zcols]
                sg = _sigmoid(zp)
                op.append(o_ref[:, cols])
                dyp = dy_ref[:, cols]
                do.append(dyp * (zp * sg))
                dp_ref[:, zcols] = (dyp * op[i] * (sg * (1.0 + zp * (1.0 - sg)))).astype(BF16)
            q4 = _stack_heads(qr[0], qr[1], lo).astype(BF16)
            do4 = _stack_heads(do[0], do[1], lo)
            o4 = jnp.concatenate([op[0], op[0], op[1], op[1]], axis=0)
            prob, psink = _head_probs(q4, k2[g], valid, _sink_column(sink_ref, g))
            delta = jnp.sum(do4 * o4, axis=1, keepdims=True)
            do4b = do4.astype(BF16)
            dprob = lax.dot_general(do4b, v2[g], _DIMS["nt"], preferred_element_type=F32)
            dsc = (prob * (dprob - delta)).astype(BF16)
            sink_terms = psink * delta
            for i in range(HEADS_PER_GROUP):
                h = HEADS_PER_GROUP * g + i
                dsink = -jnp.sum(sink_terms[i * wb:(i + 1) * wb], axis=0, keepdims=True)
                ds_ref[h:h + 1, :] += jnp.broadcast_to(dsink, (1, LANES))
            dq4 = jnp.dot(dsc, k2[g], preferred_element_type=F32) * scale
            for i in range(2):
                cols = slice((2 * g + i) * LANES, (2 * g + i + 1) * LANES)
                dp_ref[:, cols] = _rope_transpose(_unstack_pair(dq4, i, lo), tab_v).astype(BF16)
            dk2 = lax.dot_general(dsc, q4, _DIMS["tn"], preferred_element_type=F32) * scale
            dv2 = lax.dot_general(prob.astype(BF16), do4b, _DIMS["tn"], preferred_element_type=F32)
            dks.append(dk2 + pltpu.roll(dk2, HALF_LANES, 1))
            dvs.append(dv2 + pltpu.roll(dv2, HALF_LANES, 1))
        start = pl.multiple_of(n * wb, wb)
        dk_ref[pl.ds(start, 3 * wb), :] += jnp.where(lo_w, dks[0], dks[1])
        dv_ref[pl.ds(start, 3 * wb), :] += jnp.where(lo_w, dvs[0], dvs[1])

    pad_spec = pl.BlockSpec((s + 2 * wb, LANES), lambda n: (0, 0))
    blk = pl.BlockSpec((wb, A_WIDTH), lambda n: (n, 0))
    bsp = pl.BlockSpec((wb, W_B), lambda n: (n, COL_B))
    pad_shape = jax.ShapeDtypeStruct((s + 2 * wb, LANES), F32)
    return pl.pallas_call(
        body, grid=(s // wb,),
        in_specs=[bsp, pad_spec, pad_spec, pl.BlockSpec((wb, 3 * LANES), lambda n: (n, 0)),
                  pl.BlockSpec(memory_space=pltpu.SMEM), blk, blk, pl.BlockSpec(memory_space=pl.ANY)],
        out_specs=[bsp, pad_spec, pad_spec, pl.BlockSpec((8, LANES), lambda n: (0, 0))],
        out_shape=[jax.ShapeDtypeStruct(dproj.shape, BF16), pad_shape, pad_shape,
                   jax.ShapeDtypeStruct((8, LANES), F32)],
        input_output_aliases={7: 0}, name="swa_bwd",
        compiler_params=_params())(proj, kpad, vpad, tab, sink, o_attn, dyb, dproj)


def _mem_probs(qh, mk):
    sc = lax.dot_general(qh, mk, _DIMS["nt"], preferred_element_type=F32) * (MEM_HEAD_DIM ** -0.5)
    e = jnp.exp(sc - jnp.max(sc, axis=1, keepdims=True))
    return e * (1.0 / jnp.sum(e, axis=1, keepdims=True))


def _mem_fwd(proj, mkv):
    s = proj.shape[0]
    ts = 256
    mlen = mkv.shape[0]

    def body(m_ref, kv_ref, o_ref, y_ref):
        for h in range(MEM_HEADS):
            cols = slice(h * LANES, (h + 1) * LANES)
            mk = kv_ref[:, cols].astype(BF16)
            mv = kv_ref[:, MEM_WIDTH + h * LANES:MEM_WIDTH + (h + 1) * LANES].astype(BF16)
            prob = _mem_probs(m_ref[:, cols].astype(BF16), mk)
            oh = jnp.dot(prob.astype(BF16), mv, preferred_element_type=F32)
            o_ref[:, cols] = oh
            zh = m_ref[:, MEM_WIDTH + h * LANES:MEM_WIDTH + (h + 1) * LANES]
            y_ref[:, cols] = (oh * (zh * _sigmoid(zh))).astype(BF16)

    o_spec = pl.BlockSpec((ts, MEM_WIDTH), lambda i: (i, 0))
    return pl.pallas_call(
        body, grid=(s // ts,),
        in_specs=[pl.BlockSpec((ts, W_M), lambda i: (i, COL_M)),
                  pl.BlockSpec((mlen, 2 * MEM_WIDTH), lambda i: (0, 0))],
        out_specs=[o_spec, o_spec],
        out_shape=[jax.ShapeDtypeStruct((s, MEM_WIDTH), F32), jax.ShapeDtypeStruct((s, MEM_WIDTH), BF16)],
        name="mem_fwd", compiler_params=_params())(proj, mkv)


def _mem_bwd(proj, mkv, o_mem, dym, dproj):
    s = proj.shape[0]
    ts = 256
    mlen = mkv.shape[0]
    scale = MEM_HEAD_DIM ** -0.5

    def body(m_ref, kv_ref, o_ref, dy_ref, _, dp_ref, dkv_ref):
        @pl.when(pl.program_id(0) == 0)
        def _():
            dkv_ref[...] = jnp.zeros_like(dkv_ref)

        for h in range(MEM_HEADS):
            cols = slice(h * LANES, (h + 1) * LANES)
            vcols = slice(MEM_WIDTH + h * LANES, MEM_WIDTH + (h + 1) * LANES)
            mk = kv_ref[:, cols].astype(BF16)
            mv = kv_ref[:, vcols].astype(BF16)
            qh = m_ref[:, cols].astype(BF16)
            zh = m_ref[:, vcols]
            sg = _sigmoid(zh)
            oh = o_ref[:, cols]
            dyh = dy_ref[:, cols]
            doh = dyh * (zh * sg)
            dp_ref[:, vcols] = (dyh * oh * (sg * (1.0 + zh * (1.0 - sg)))).astype(BF16)
            prob = _mem_probs(qh, mk)
            delta = jnp.sum(doh * oh, axis=1, keepdims=True)
            dohb = doh.astype(BF16)
            dprob = lax.dot_general(dohb, mv, _DIMS["nt"], preferred_element_type=F32)
            dsc = (prob * (dprob - delta)).astype(BF16)
            dp_ref[:, cols] = (jnp.dot(dsc, mk, preferred_element_type=F32) * scale).astype(BF16)
            dkv_ref[:, cols] += lax.dot_general(dsc, qh, _DIMS["tn"], preferred_element_type=F32) * scale
            dkv_ref[:, vcols] += lax.dot_general(prob.astype(BF16), dohb, _DIMS["tn"],
                                                 preferred_element_type=F32)

    blk = pl.BlockSpec((ts, MEM_WIDTH), lambda i: (i, 0))
    msp = pl.BlockSpec((ts, W_M), lambda i: (i, COL_M))
    kvsp = pl.BlockSpec((mlen, 2 * MEM_WIDTH), lambda i: (0, 0))
    return pl.pallas_call(
        body, grid=(s // ts,),
        in_specs=[msp, kvsp, blk, blk, pl.BlockSpec(memory_space=pl.ANY)],
        out_specs=[msp, kvsp],
        out_shape=[jax.ShapeDtypeStruct(dproj.shape, BF16), jax.ShapeDtypeStruct(mkv.shape, F32)],
        input_output_aliases={4: 0}, name="mem_bwd",
        compiler_params=_params())(proj, mkv, o_mem, dym, dproj)


def _forward_backward(x, mem, tgt, g_pre, w_in_p, w_conv, sink, g_mem, w_kv, w_up, w_out, g_post):
    s = x.shape[0]
    tab = _rope_tables(s)

    h = _rmsnorm_fwd(x, g_pre, name="pre_norm")
    proj = _matmul(h, w_in_p, mode="nn", out_dtype=F32, tm=512, tn=3712, tk=D_MODEL, name="proj", j_outer=True)
    ya = _conv_fwd(proj, w_conv)
    kpad, vpad = _rope_kv(proj, tab)
    o_attn, yb = _swa_fwd(proj, kpad, vpad, tab, sink)
    mn = _rmsnorm_fwd(mem, g_mem, name="mem_norm")
    mkv = _matmul(mn, w_kv, mode="nn", out_dtype=F32, tm=256, tn=1024, tk=D_MODEL, name="mem_kv")
    o_mem, ym = _mem_fwd(proj, mkv)
    merged, d_out, dy, dg_post, loss = _mid_fwd(ya, yb, ym, proj, x, tgt, w_up, w_out, g_post)
    dproj, d_ya, d_yb, d_ym, dw_up, dw_out = _mid_bwd(d_out, merged, ya, yb, ym, proj, w_up, w_out)

    dproj, dw_conv = _conv_bwd(proj, w_conv, d_ya, dproj)
    dproj, dkpad, dvpad, dsink = _swa_bwd(proj, kpad, vpad, tab, sink, o_attn, d_yb, dproj)
    dproj = _rope_kv_bwd(dkpad, dvpad, tab, dproj)
    dproj, d_mkv = _mem_bwd(proj, mkv, o_mem, d_ym, dproj)

    dw_kv = _matmul(mn, d_mkv, mode="tn", out_dtype=F32, tm=1024, tn=1024, tk=256, name="dw_kv")
    d_mn = _matmul(d_mkv, w_kv, mode="nt", out_dtype=F32, tm=256, tn=1024, tk=D_MODEL, name="d_mn")
    _, dg_mem = _rmsnorm_bwd(d_mn, mem, g_mem, d_mn, name="mem_norm_bwd")

    return dict(loss=loss, h=h, dproj=dproj, dy=dy, w_conv=dw_conv, sink=dsink, g_mem=dg_mem,
                w_kv=dw_kv, w_up=dw_up, w_out=dw_out, g_post=dg_post)


N_DEV = 8


def _position():
    return lax.axis_index("x"), lax.axis_index("y"), lax.axis_index("c")


def _other_chips(x, y):
    return (((1 - x, y), 2 * (1 - x) + y), ((x, 1 - y), 2 * x + (1 - y)), ((1 - x, 1 - y), 2 * (1 - x) + (1 - y)))


def _remote(src, dst, send_sems, recv_sems, k, device):
    return pltpu.make_async_remote_copy(src_ref=src, dst_ref=dst, send_sem=send_sems.at[k], recv_sem=recv_sems.at[k],
                                        device_id=device, device_id_type=MESH)


def _rows_half(ref, hf):
    rh = ref.shape[0] // 2
    return ref.at[pl.ds(pl.multiple_of(hf * rh, 8), rh)]


def _gather_shards(shards, small):
    n = len(shards)

    def body(*refs):
        ins, small_in, outs, small_out = refs[:n], refs[n], refs[n + 1:2 * n + 1], refs[2 * n + 1]
        ici_send, ici_recv, d2d_send, d2d_recv = refs[2 * n + 2:]
        x, y, c = _position()
        me = 2 * x + y
        chips = _other_chips(x, y)
        sends = [_remote(_rows_half(ins[a], c), _rows_half(outs[a].at[me], c), ici_send, ici_recv, 3 * a + r, (*chip, c))
                 for a in range(n) for r, (chip, _) in enumerate(chips)]
        sends += [_remote(small_in, small_out.at[me], ici_send, ici_recv, 3 * n + r, (*chip, c))
                  for r, (chip, _) in enumerate(chips)]
        for cp in sends:
            cp.start()
        for a in range(n):
            for r, (chip, idx) in enumerate(chips):
                landed = _rows_half(outs[a].at[idx], c)
                _remote(landed, landed, ici_send, ici_recv, 3 * a + r, (*chip, c)).wait_recv()
                fwd = _remote(landed, landed, d2d_send, d2d_recv, 3 * a + r, (x, y, 1 - c))
                fwd.start()
                sends.append(fwd)
        for a in range(n):
            for r, (_, idx) in enumerate(chips):
                other = _rows_half(outs[a].at[idx], 1 - c)
                _remote(other, other, d2d_send, d2d_recv, 3 * a + r, (x, y, 1 - c)).wait_recv()
        for r, (chip, idx) in enumerate(chips):
            _remote(small_in, small_out.at[idx], ici_send, ici_recv, 3 * n + r, (*chip, c)).wait_recv()
        for cp in sends:
            cp.wait_send()

    return pl.pallas_call(
        body, in_specs=[_HBM] * (n + 1), out_specs=[_HBM] * (n + 1),
        out_shape=[jax.ShapeDtypeStruct((N_CHIPS,) + s.shape, s.dtype) for s in shards + [small]],
        scratch_shapes=[pltpu.SemaphoreType.DMA((3 * n + 3,)), pltpu.SemaphoreType.DMA((3 * n + 3,)),
                        pltpu.SemaphoreType.DMA((3 * n,)), pltpu.SemaphoreType.DMA((3 * n,))],
        name="gather_weights")(*shards, small)


def _pair_exchange(send):
    n = len(send)

    def copies(ins, outs, sems):
        x, y, c = _position()
        return [_remote(ins[a], outs[a], sems[0], sems[1], a, (x, y, 1 - c)) for a in range(n)]

    def start(ins, outs, sems):
        for cp in copies(ins, outs, sems):
            cp.start()

    def finish(ins, outs, sems):
        for cp in copies(ins, outs, sems):
            cp.wait()

    return _Carry(send, [jax.ShapeDtypeStruct(p.shape, p.dtype) for p in send],
                  [pltpu.SemaphoreType.DMA((n,)), pltpu.SemaphoreType.DMA((n,))], start, finish)


def _chip_exchange(sums):
    n = len(sums)

    def copies(ins, outs, sems):
        x, y, c = _position()
        return [_remote(ins[a].at[idx], outs[a].at[r], sems[0], sems[1], 3 * a + r, (*chip, c))
                for a in range(n) for r, (chip, idx) in enumerate(_other_chips(x, y))]

    def start(ins, outs, sems):
        for cp in copies(ins, outs, sems):
            cp.start()

    def finish(ins, outs, sems):
        for cp in copies(ins, outs, sems):
            cp.wait()

    return _Carry(sums, [jax.ShapeDtypeStruct((3,) + p.shape[1:], p.dtype) for p in sums],
                  [pltpu.SemaphoreType.DMA((3 * n,)), pltpu.SemaphoreType.DMA((3 * n,))], start, finish)


def _pair_share(pairs):
    n = len(pairs)

    def start(ins, outs, sems):
        x, y, c = _position()
        for a in range(n):
            _remote(outs[a].at[c], outs[a].at[c], sems[0], sems[1], a, (x, y, 1 - c)).start()

    def finish(ins, outs, sems):
        x, y, c = _position()
        for a in range(n):
            _remote(outs[a].at[1 - c], outs[a].at[1 - c], sems[0], sems[1], a, (x, y, 1 - c)).wait_recv()
        for a in range(n):
            _remote(outs[a].at[c], outs[a].at[c], sems[0], sems[1], a, (x, y, 1 - c)).wait_send()

    return _Carry(pairs, [jax.ShapeDtypeStruct(p.shape, p.dtype) for p in pairs],
                  [pltpu.SemaphoreType.DMA((n,)), pltpu.SemaphoreType.DMA((n,))], start, finish,
                  aliases={a: a for a in range(n)})


def _small_allreduce(pack):
    rows, width = pack.shape

    def body(p_ref, o_ref, buf, send_sems, recv_sems):
        x, y, c = _position()
        me = 4 * x + 2 * y + c
        buf[me] = p_ref[...]
        peers = []
        for r in range(1, N_DEV):
            fx, fy, fc = (r >> 2) & 1, (r >> 1) & 1, r & 1
            px, py, pc = (1 - x if fx else x), (1 - y if fy else y), (1 - c if fc else c)
            peers.append(((px, py, pc), 4 * px + 2 * py + pc))
        sends = [_remote(p_ref, buf.at[me], send_sems, recv_sems, r, dev) for r, (dev, _) in enumerate(peers)]
        for cp in sends:
            cp.start()
        for r, (dev, idx) in enumerate(peers):
            _remote(p_ref, buf.at[idx], send_sems, recv_sems, r, dev).wait_recv()
        for cp in sends:
            cp.wait_send()
        acc = buf[0]
        for k in range(1, N_DEV):
            acc = acc + buf[k]
        o_ref[...] = acc

    vm = pl.BlockSpec(memory_space=pltpu.VMEM)
    return pl.pallas_call(
        body, in_specs=[vm], out_specs=vm, out_shape=jax.ShapeDtypeStruct(pack.shape, F32),
        scratch_shapes=[pltpu.VMEM((N_DEV, rows, width), F32), pltpu.SemaphoreType.DMA((N_DEV - 1,)),
                        pltpu.SemaphoreType.DMA((N_DEV - 1,))],
        name="small_allreduce")(pack)


def _row_tile(rows):
    return rows if rows <= 256 else 256


def _pair_add(keep, recv, name):
    nj, rh, cols = keep.shape
    tr = _row_tile(rh)

    def body(k_ref, r_ref, o_ref):
        o_ref[...] = (k_ref[...].astype(F32) + r_ref[...].astype(F32)).astype(BF16)

    blk = pl.BlockSpec((None, tr, cols), lambda j, i: (j, i, 0))
    return pl.pallas_call(body, grid=(nj, rh // tr), in_specs=[blk, blk], out_specs=blk,
                          out_shape=jax.ShapeDtypeStruct(keep.shape, BF16), name=name,
                          compiler_params=_params())(keep, recv)


def _chip_add(sums, recv, where, name):
    _, rh, cols = sums.shape
    tr = _row_tile(rh)

    def body(w_ref, s_ref, r_ref, o_ref):
        o_ref[...] = ((s_ref[...].astype(F32) + r_ref[0].astype(F32)) + r_ref[1].astype(F32)) + r_ref[2].astype(F32)

    grid_spec = pltpu.PrefetchScalarGridSpec(
        num_scalar_prefetch=1, grid=(rh // tr,),
        in_specs=[pl.BlockSpec((None, tr, cols), lambda i, w_ref: (w_ref[0], i, 0)),
                  pl.BlockSpec((3, tr, cols), lambda i, w_ref: (0, i, 0))],
        out_specs=pl.BlockSpec((None, tr, cols), lambda i, w_ref: (w_ref[1], i, 0)))
    return pl.pallas_call(body, grid_spec=grid_spec, out_shape=jax.ShapeDtypeStruct((2, rh, cols), F32),
                          name=name, compiler_params=_params())(where, sums, recv)


def _adamw(w, g, m, v, name):
    rows, cols = w.shape
    tr = _row_tile(rows)
    assert rows % tr == 0

    def body(w_ref, g_ref, m_ref, v_ref, d_ref, mo_ref, vo_ref):
        gv = g_ref[...]
        m_new = ADAM_B1 * m_ref[...] + (1.0 - ADAM_B1) * gv
        v_new = ADAM_B2 * v_ref[...] + (1.0 - ADAM_B2) * jnp.square(gv)
        m_hat = m_new / (1.0 - ADAM_B1 ** ADAM_STEP)
        v_hat = v_new / (1.0 - ADAM_B2 ** ADAM_STEP)
        d_ref[...] = -ADAM_LR * (m_hat / (jnp.sqrt(v_hat) + ADAM_EPS) + ADAM_WD * w_ref[...])
        mo_ref[...] = m_new
        vo_ref[...] = v_new

    blk = pl.BlockSpec((tr, cols), lambda i: (i, 0))
    shp = jax.ShapeDtypeStruct((rows, cols), F32)
    return pl.pallas_call(body, grid=(rows // tr,), in_specs=[blk] * 4, out_specs=[blk] * 3,
                          out_shape=[shp] * 3, name=name, compiler_params=_params())(w, g, m, v)


SHARD_W = IN_WIDTH // N_CHIPS


def _half_major(a):
    r, c = a.shape
    return a.reshape(N_CHIPS, 2, r // N_CHIPS // 2, c).transpose(1, 0, 2, 3)


def _w_in_permuted(pieces):
    cols = []
    for a, b in PERM_SEGS:
        pos = a
        while pos < b:
            j = pos // SHARD_W
            stop = min(b, (j + 1) * SHARD_W)
            cols.append(pieces[j][:, pos - j * SHARD_W:stop - j * SHARD_W])
            pos = stop
    return jnp.concatenate(cols, axis=1)


def _ref_cols(a_p, lo, hi):
    out, ref_off = [], 0
    for a, b in UNPERM_SEGS:
        r0, r1 = ref_off, ref_off + (b - a)
        s, e = max(lo, r0), min(hi, r1)
        if s < e:
            out.append(a_p[:, a + (s - r0):a + (e - r0)])
        ref_off = r1
    return jnp.concatenate(out, axis=1)


def kernel(x, mem, g_pre, w_in, w_conv, attn_sink, g_mem, w_mem_kv, w_up_a, w_up_b, w_up_m, w_out, g_post, loss_target, m_g_pre, m_w_in, m_w_conv, m_attn_sink, m_g_mem, m_w_mem_kv, m_w_up_a, m_w_up_b, m_w_up_m, m_w_out, m_g_post, v_g_pre, v_w_in, v_w_conv, v_attn_sink, v_g_mem, v_w_mem_kv, v_w_up_a, v_w_up_b, v_w_up_m, v_w_out, v_g_post):
    xi, yi, ci = _position()
    chip = 2 * xi + yi
    where = jnp.stack([chip, ci]).astype(jnp.int32)

    own = [w_in[0].astype(BF16), w_mem_kv[0].astype(BF16),
           jnp.concatenate([w_up_a[0], w_up_b[0], w_up_m[0]], axis=0).astype(BF16), w_out[0].astype(BF16)]
    own_conv = jnp.pad(w_conv[0], ((0, 5), (0, 0)))
    *gathered, g_conv = _gather_shards(own, own_conv)

    def pieces(mine, got):
        return [jnp.where(chip == j, mine, got[j]) for j in range(N_CHIPS)]

    w_in_p = _w_in_permuted(pieces(own[0], gathered[0]))
    w_kv_full = jnp.concatenate(pieces(own[1], gathered[1]), axis=0)
    up_pieces = pieces(own[2], gathered[2])
    w_up_full = jnp.stack([jnp.concatenate([p[k * A_WIDTH:(k + 1) * A_WIDTH] for p in up_pieces], axis=1)
                           for k in range(3)])
    w_out_full = jnp.concatenate(pieces(own[3], gathered[3]), axis=0)
    w_conv_full = jnp.concatenate([p[:3] for p in pieces(own_conv, g_conv)], axis=1)

    g = _forward_backward(x[0], mem[0], loss_target[0], g_pre, w_in_p, w_conv_full, attn_sink, g_mem, w_kv_full,
                          w_up_full, w_out_full, g_post)

    half_rows = D_MODEL // 2
    up_parts = (g["w_up"].reshape(3, A_WIDTH, N_CHIPS, D_MODEL // N_CHIPS).transpose(2, 0, 1, 3)
                .reshape(N_CHIPS, 2, 3 * A_WIDTH // 2, D_MODEL // N_CHIPS).transpose(1, 0, 2, 3)).astype(BF16)
    small_parts = [_half_major(g["w_kv"]).astype(BF16), up_parts, _half_major(g["w_out"]).astype(BF16)]

    def dw_in_half(hf, name, carry=None):
        h_half = lax.dynamic_slice_in_dim(g["h"], hf * half_rows, half_rows, axis=1)
        return _matmul(h_half, g["dproj"], mode="tn", out_dtype=F32, tm=half_rows, tn=3712, tk=512, name=name,
                       carry=carry)

    def shard_major(dw_half):
        return jnp.stack([_ref_cols(dw_half, j * SHARD_W, (j + 1) * SHARD_W) for j in range(N_CHIPS)]).astype(BF16)

    def pick(parts, hf):
        return [lax.dynamic_index_in_dim(p, hf, 0, keepdims=False) for p in parts]

    names = ["w_in", "w_kv", "w_up", "w_out"]
    send = [shard_major(dw_in_half(1 - ci, "dw_in_send"))] + pick(small_parts, 1 - ci)
    dw_keep, recv = dw_in_half(ci, "dw_in_keep", _pair_exchange(send))
    keep = [shard_major(dw_keep)] + pick(small_parts, ci)
    sums = [_pair_add(k, r, "pair_add_" + nm) for k, r, nm in zip(keep, recv, names)]
    d_h, recv3 = _matmul(g["dproj"], w_in_p, mode="nt", out_dtype=F32, tm=512, tn=1024, tk=3712, name="d_h",
                         carry=_chip_exchange(sums))
    pairs = [_chip_add(s, r, where, "chip_add_" + nm) for s, r, nm in zip(sums, recv3, names)]
    grad_x, dg_pre, full = _rmsnorm_bwd(d_h, x[0], g_pre, g["dy"], name="pre_norm_bwd", carry=_pair_share(pairs))

    zeros512 = jnp.zeros((1, D_MODEL - A_WIDTH), F32)
    conv_rows = [jnp.concatenate([g["w_conv"][k:k + 1], zeros512], axis=1) for k in range(3)]
    sink_row = jnp.pad(g["sink"][:, 0].reshape(1, N_Q_HEADS), ((0, 0), (0, D_MODEL - N_Q_HEADS)))
    loss_row = jnp.pad(g["loss"], ((0, 0), (0, D_MODEL - LANES)))
    pack = jnp.concatenate([dg_pre, g["g_mem"], g["g_post"]] + conv_rows + [sink_row, loss_row], axis=0)
    red = _small_allreduce(pack)
    loss = red[7, 0]
    small_grads = dict(
        g_pre=red[0:1], g_mem=red[1:2], g_post=red[2:3], attn_sink=red[6:7, :N_Q_HEADS],
        w_conv=lax.dynamic_slice(red[3:6, :A_WIDTH], (0, chip * LANES), (3, LANES)))

    gw_up = full[2].reshape(3, A_WIDTH, D_MODEL // N_CHIPS)
    grads = dict(small_grads, w_in=full[0].reshape(D_MODEL, SHARD_W),
                 w_mem_kv=full[1].reshape(D_MODEL // N_CHIPS, 2 * MEM_WIDTH),
                 w_up_a=gw_up[0], w_up_b=gw_up[1], w_up_m=gw_up[2],
                 w_out=full[3].reshape(D_MODEL // N_CHIPS, D_MODEL))

    weights = dict(g_pre=g_pre, w_in=w_in, w_conv=w_conv, attn_sink=attn_sink, g_mem=g_mem, w_mem_kv=w_mem_kv,
                   w_up_a=w_up_a, w_up_b=w_up_b, w_up_m=w_up_m, w_out=w_out, g_post=g_post)
    m_in = dict(g_pre=m_g_pre, w_in=m_w_in, w_conv=m_w_conv, attn_sink=m_attn_sink, g_mem=m_g_mem,
                w_mem_kv=m_w_mem_kv, w_up_a=m_w_up_a, w_up_b=m_w_up_b, w_up_m=m_w_up_m, w_out=m_w_out,
                g_post=m_g_post)
    v_in = dict(g_pre=v_g_pre, w_in=v_w_in, w_conv=v_w_conv, attn_sink=v_attn_sink, g_mem=v_g_mem,
                w_mem_kv=v_w_mem_kv, w_up_a=v_w_up_a, w_up_b=v_w_up_b, w_up_m=v_w_up_m, w_out=v_w_out,
                g_post=v_g_post)
    out_g, out_d, out_m, out_v = [], [], [], []
    for nm in ("g_pre", "w_in", "w_conv", "attn_sink", "g_mem", "w_mem_kv", "w_up_a", "w_up_b", "w_up_m", "w_out",
               "g_post"):
        shape = weights[nm].shape
        two_d = shape[-2:]
        gr = grads[nm].reshape(two_d)
        d, m_new, v_new = _adamw(weights[nm].reshape(two_d), gr, m_in[nm].reshape(two_d), v_in[nm].reshape(two_d),
                                 "adamw_" + nm)
        out_g.append(gr.reshape(shape))
        out_d.append(d.reshape(shape))
        out_m.append(m_new.reshape(shape))
        out_v.append(v_new.reshape(shape))
    return (loss, grad_x.reshape(x.shape), *out_g, *out_d, *out_m, *out_v)
```

```python
import functools

import jax
import jax.numpy as jnp
from jax import lax
from jax.experimental import pallas as pl
from jax.experimental.pallas import tpu as pltpu

F32 = jnp.float32
BF16 = jnp.bfloat16
MESH = pl.DeviceIdType.MESH

D_MODEL = 1024
EPS = 1e-6
A_WIDTH = 512
HEAD_DIM = 64
N_Q_HEADS = 8
WINDOW_BLOCK = 128
KV_PAD = 512
ROPE_THETA = 500000.0
ROT_DIM = 16
MEM_HEADS = 4
MEM_HEAD_DIM = 128
MEM_WIDTH = 512
IN_WIDTH = 7424
N_CHIPS = 4
LANES = 128
HALF_LANES = 64

PERM_SEGS = ((0, 2560), (2816, 3328), (4352, 7424), (3328, 4352), (2560, 2816))
UNPERM_SEGS = ((0, 2560), (7168, 7424), (2560, 3072), (6144, 7168), (3072, 6144))
COL_A, W_A = 0, 2048
COL_B, W_B = 2, 1024
COL_G, W_G = 1, 3072
COL_M, W_M = 6, 1024
COL_KV, W_KV = 28, 256

ADAM_LR = 0.001
ADAM_B1 = 0.9
ADAM_B2 = 0.999
ADAM_EPS = 1e-08
ADAM_WD = 0.01
ADAM_STEP = 10

VMEM_LIMIT_BYTES = 48 * 1024 * 1024


_HBM = pl.BlockSpec(memory_space=pltpu.HBM)


def _params(**kw):
    return pltpu.CompilerParams(vmem_limit_bytes=VMEM_LIMIT_BYTES, **kw)


def _sigmoid(v):
    return jax.nn.sigmoid(v)


_DIMS = {"nn": (((1,), (0,)), ((), ())), "nt": (((1,), (1,)), ((), ())), "tn": (((0,), (0,)), ((), ()))}


class _Carry:
    def __init__(self, ins, out_shapes, sems, start, finish, aliases=None):
        self.ins, self.out_shapes, self.sems = list(ins), list(out_shapes), list(sems)
        self.start, self.finish, self.aliases = start, finish, dict(aliases or {})


def _carried_call(body, carry, *, grid, in_specs, out_specs, out_shape, scratch, operands, name):
    n_in, n_out, n_scr = len(in_specs), len(out_specs), len(scratch)
    c_in = len(carry.ins) if carry else 0
    c_out = len(carry.out_shapes) if carry else 0
    steps = 1
    for g in grid:
        steps *= g

    def wrapped(*refs):
        ins, cins = refs[:n_in], refs[n_in:n_in + c_in]
        outs = refs[n_in + c_in:n_in + c_in + n_out]
        couts = refs[n_in + c_in + n_out:n_in + c_in + n_out + c_out]
        rest = refs[n_in + c_in + n_out + c_out:]
        scr, sems = rest[:n_scr], rest[n_scr:]
        if carry:
            step = pl.program_id(0)
            for ax in range(1, len(grid)):
                step = step * grid[ax] + pl.program_id(ax)

            @pl.when(step == 0)
            def _():
                carry.start(cins, couts, sems)

        body(ins, outs, scr)
        if carry:
            @pl.when(step == steps - 1)
            def _():
                carry.finish(cins, couts, sems)

    aliases = {n_in + i: n_out + o for i, o in carry.aliases.items()} if carry else {}
    results = pl.pallas_call(
        wrapped, grid=grid, in_specs=list(in_specs) + [_HBM] * c_in, out_specs=list(out_specs) + [_HBM] * c_out,
        out_shape=list(out_shape) + (carry.out_shapes if carry else []),
        scratch_shapes=list(scratch) + (carry.sems if carry else []), input_output_aliases=aliases,
        name=name, compiler_params=_params())(*operands, *(carry.ins if carry else []))
    return results[:n_out], results[n_out:]


def _matmul(a, b, *, mode, out_dtype, tm, tn, tk, name, j_outer=False, carry=None):
    if mode == "nn":
        (m, k), (_, n) = a.shape, b.shape
    elif mode == "nt":
        (m, k), (n, _) = a.shape, b.shape
    else:
        (k, m), (_, n) = a.shape, b.shape
    tm, tn, tk = min(tm, m), min(tn, n), min(tk, k)
    assert m % tm == 0 and n % tn == 0 and k % tk == 0
    ni, nj, nk = m // tm, n // tn, k // tk
    dims = _DIMS[mode]

    def ij(g0, g1):
        return (g1, g0) if j_outer else (g0, g1)

    if mode == "nn":
        a_spec = pl.BlockSpec((tm, tk), lambda g0, g1, kk: (ij(g0, g1)[0], kk))
        b_spec = pl.BlockSpec((tk, tn), lambda g0, g1, kk: (kk, ij(g0, g1)[1]))
    elif mode == "nt":
        a_spec = pl.BlockSpec((tm, tk), lambda g0, g1, kk: (ij(g0, g1)[0], kk))
        b_spec = pl.BlockSpec((tn, tk), lambda g0, g1, kk: (ij(g0, g1)[1], kk))
    else:
        a_spec = pl.BlockSpec((tk, tm), lambda g0, g1, kk: (kk, ij(g0, g1)[0]))
        b_spec = pl.BlockSpec((tk, tn), lambda g0, g1, kk: (kk, ij(g0, g1)[1]))
    o_spec = pl.BlockSpec((tm, tn), lambda g0, g1, kk: ij(g0, g1))

    def part(a_ref, b_ref):
        return lax.dot_general(a_ref[...].astype(BF16), b_ref[...].astype(BF16), dims,
                               preferred_element_type=F32)

    if nk == 1:
        def body(ins, outs, scr):
            outs[0][...] = part(*ins).astype(out_dtype)
        scratch = []
    else:
        def body(ins, outs, scr):
            kk = pl.program_id(2)
            acc_ref = scr[0]

            @pl.when(kk == 0)
            def _():
                acc_ref[...] = part(*ins)

            @pl.when(kk > 0)
            def _():
                acc_ref[...] += part(*ins)

            @pl.when(kk == nk - 1)
            def _():
                outs[0][...] = acc_ref[...].astype(out_dtype)
        scratch = [pltpu.VMEM((tm, tn), F32)]

    grid = (nj, ni, nk) if j_outer else (ni, nj, nk)
    (out,), carried = _carried_call(
        body, carry, grid=grid, in_specs=[a_spec, b_spec], out_specs=[o_spec],
        out_shape=[jax.ShapeDtypeStruct((m, n), out_dtype)], scratch=scratch, operands=(a, b), name=name)
    return (out, carried) if carry else out


def _rmsnorm_fwd(x, g, *, name):
    s, d = x.shape
    ts = min(512, s)

    def body(x_ref, g_ref, o_ref):
        xv = x_ref[...]
        r = lax.rsqrt(jnp.mean(xv * xv, axis=-1, keepdims=True) + EPS)
        o_ref[...] = ((xv * r) * g_ref[...]).astype(BF16)

    return pl.pallas_call(
        body, grid=(s // ts,),
        in_specs=[pl.BlockSpec((ts, d), lambda i: (i, 0)), pl.BlockSpec((1, d), lambda i: (0, 0))],
        out_specs=pl.BlockSpec((ts, d), lambda i: (i, 0)),
        out_shape=jax.ShapeDtypeStruct((s, d), BF16), name=name, compiler_params=_params())(x, g)


def _rmsnorm_bwd(dh, x, g, res, *, name, carry=None):
    s, d = x.shape
    ts = min(256, s)

    def body(ins, outs, scr):
        dh_ref, x_ref, g_ref, res_ref = ins
        dx_ref, dg_ref = outs
        xv = x_ref[...]
        r = lax.rsqrt(jnp.mean(xv * xv, axis=-1, keepdims=True) + EPS)
        xh = xv * r
        dhv = dh_ref[...]
        part = jnp.sum(dhv * xh, axis=0, keepdims=True)

        @pl.when(pl.program_id(0) == 0)
        def _():
            dg_ref[...] = part

        @pl.when(pl.program_id(0) > 0)
        def _():
            dg_ref[...] += part

        dxh = dhv * g_ref[...]
        dx_ref[...] = res_ref[...] + r * (dxh - xh * jnp.mean(dxh * xh, axis=-1, keepdims=True))

    row = pl.BlockSpec((ts, d), lambda i: (i, 0))
    vec = pl.BlockSpec((1, d), lambda i: (0, 0))
    outs, carried = _carried_call(
        body, carry, grid=(s // ts,), in_specs=[row, row, vec, row], out_specs=[row, vec],
        out_shape=[jax.ShapeDtypeStruct((s, d), F32), jax.ShapeDtypeStruct((1, d), F32)],
        scratch=[], operands=(dh, x, g, res), name=name)
    return (*outs, carried) if carry else tuple(outs)


MID_TILE = 256


def _gated_branches(y_refs, wup_ref, gl):
    d = D_MODEL
    us = [jnp.dot(y_refs[k][...], wup_ref[k], preferred_element_type=F32) for k in range(3)]
    sg = [_sigmoid(gl[:, k * d:(k + 1) * d]) for k in range(3)]
    return us, sg


def _mid_fwd(ya, yb, ym, proj, x, tgt, w_up, w_out, g_post):
    s, d = x.shape
    ts = MID_TILE

    def body(ya_ref, yb_ref, ym_ref, g_ref, x_ref, t_ref, wup_ref, wout_ref, gp_ref,
             m_ref, do_ref, dy_ref, dg_ref, loss_ref):
        us, sg = _gated_branches((ya_ref, yb_ref, ym_ref), wup_ref, g_ref[...])
        merged = (sg[0] * us[0] + sg[1] * us[1] + sg[2] * us[2]).astype(BF16)
        m_ref[...] = merged
        ov = jnp.dot(merged, wout_ref[...], preferred_element_type=F32)
        r = lax.rsqrt(jnp.mean(ov * ov, axis=-1, keepdims=True) + EPS)
        nh = ov * r
        gv = gp_ref[...]
        e = (x_ref[...] + nh * gv) - t_ref[...]
        lpart = 0.5 * jnp.sum(jnp.mean(e * e, axis=-1, keepdims=True), axis=0, keepdims=True)
        dy = e * (1.0 / d)
        dgp = jnp.sum(dy * nh, axis=0, keepdims=True)

        @pl.when(pl.program_id(0) == 0)
        def _():
            dg_ref[...] = dgp
            loss_ref[...] = jnp.broadcast_to(lpart, loss_ref.shape)

        @pl.when(pl.program_id(0) > 0)
        def _():
            dg_ref[...] += dgp
            loss_ref[...] += jnp.broadcast_to(lpart, loss_ref.shape)

        dn = dy * gv
        dy_ref[...] = dy
        do_ref[...] = (r * (dn - nh * jnp.mean(dn * nh, axis=-1, keepdims=True))).astype(BF16)

    row = pl.BlockSpec((ts, d), lambda i: (i, 0))
    ysp = pl.BlockSpec((ts, A_WIDTH), lambda i: (i, 0))
    vec = pl.BlockSpec((1, d), lambda i: (0, 0))
    return pl.pallas_call(
        body, grid=(s // ts,),
        in_specs=[ysp, ysp, ysp, pl.BlockSpec((ts, W_G), lambda i: (i, COL_G)), row, row,
                  pl.BlockSpec((3, A_WIDTH, d), lambda i: (0, 0, 0)), pl.BlockSpec((d, d), lambda i: (0, 0)), vec],
        out_specs=[row, row, row, vec, pl.BlockSpec((1, LANES), lambda i: (0, 0))],
        out_shape=[jax.ShapeDtypeStruct((s, d), BF16), jax.ShapeDtypeStruct((s, d), BF16),
                   jax.ShapeDtypeStruct((s, d), F32), jax.ShapeDtypeStruct((1, d), F32),
                   jax.ShapeDtypeStruct((1, LANES), F32)],
        name="mid_fwd", compiler_params=_params())(ya, yb, ym, proj, x, tgt, w_up, w_out, g_post)


def _mid_bwd(d_out, merged, ya, yb, ym, proj, w_up, w_out):
    s, d = merged.shape
    ts = MID_TILE
    last = s // ts - 1

    def body(do_ref, m_ref, ya_ref, yb_ref, ym_ref, g_ref, wup_ref, wout_ref,
             dp_ref, dya_ref, dyb_ref, dym_ref, dwup_hbm, dwout_hbm, dwup_acc, dwout_acc):
        i = pl.program_id(0)

        @pl.when(i == 0)
        def _():
            dwup_acc[...] = jnp.zeros_like(dwup_acc)
            dwout_acc[...] = jnp.zeros_like(dwout_acc)

        y_refs = (ya_ref, yb_ref, ym_ref)
        us, sg = _gated_branches(y_refs, wup_ref, g_ref[...])
        dov = do_ref[...]
        dwout_acc[...] += lax.dot_general(m_ref[...], dov, _DIMS["tn"], preferred_element_type=F32)
        dm = lax.dot_general(dov, wout_ref[...], _DIMS["nt"], preferred_element_type=F32)
        for k, dy_ref in enumerate((dya_ref, dyb_ref, dym_ref)):
            dp_ref[:, k * d:(k + 1) * d] = ((dm * us[k]) * (sg[k] * (1.0 - sg[k]))).astype(BF16)
            du = (sg[k] * dm).astype(BF16)
            dy_ref[...] = lax.dot_general(du, wup_ref[k], _DIMS["nt"], preferred_element_type=F32)
            dwup_acc[k] += lax.dot_general(y_refs[k][...], du, _DIMS["tn"], preferred_element_type=F32)

        @pl.when(i == last)
        def _():
            pltpu.sync_copy(dwup_acc, dwup_hbm)
            pltpu.sync_copy(dwout_acc, dwout_hbm)

    row = pl.BlockSpec((ts, d), lambda i: (i, 0))
    ysp = pl.BlockSpec((ts, A_WIDTH), lambda i: (i, 0))
    gsp = pl.BlockSpec((ts, W_G), lambda i: (i, COL_G))
    anysp = pl.BlockSpec(memory_space=pl.ANY)
    yshape = jax.ShapeDtypeStruct((s, A_WIDTH), F32)
    return pl.pallas_call(
        body, grid=(s // ts,),
        in_specs=[row, row, ysp, ysp, ysp, gsp, pl.BlockSpec((3, A_WIDTH, d), lambda i: (0, 0, 0)),
                  pl.BlockSpec((d, d), lambda i: (0, 0))],
        out_specs=[gsp, ysp, ysp, ysp, anysp, anysp],
        out_shape=[jax.ShapeDtypeStruct((s, IN_WIDTH), BF16), yshape, yshape, yshape,
                   jax.ShapeDtypeStruct((3, A_WIDTH, d), F32), jax.ShapeDtypeStruct((d, d), F32)],
        scratch_shapes=[pltpu.VMEM((3, A_WIDTH, d), F32), pltpu.VMEM((d, d), F32)],
        name="mid_bwd", compiler_params=_params())(d_out, merged, ya, yb, ym, proj, w_up, w_out)


def _conv_core(blk, prev, nxt, w, i, last, ts):
    c = A_WIDTH
    ab, ac, ax, az = blk[:, :c], blk[:, c:2 * c], blk[:, 2 * c:3 * c], blk[:, 3 * c:]
    cu = ac * ax
    cu_prev = (prev[7:8, c:2 * c] * prev[7:8, 2 * c:3 * c]) * jnp.where(i > 0, 1.0, 0.0)
    cu_next = (nxt[0:1, c:2 * c] * nxt[0:1, 2 * c:3 * c]) * jnp.where(i < last, 1.0, 0.0)
    row = lax.broadcasted_iota(jnp.int32, (ts, c), 0)
    cm1 = jnp.where(row == 0, cu_prev, pltpu.roll(cu, 1, 0))
    cp1 = jnp.where(row == ts - 1, cu_next, pltpu.roll(cu, ts - 1, 0))
    yc = cm1 * w[0:1] + cu * w[1:2] + cp1 * w[2:3]
    return ab, ac, ax, az, cu, cm1, cp1, yc, row


def _halo_specs(ts, width, col, nblk8):
    prev = pl.BlockSpec((8, width), lambda i: (jnp.maximum(i * (ts // 8) - 1, 0), col))
    nxt = pl.BlockSpec((8, width), lambda i: (jnp.minimum((i + 1) * (ts // 8), nblk8 - 1), col))
    return prev, nxt


def _conv_fwd(proj, w_conv):
    s = proj.shape[0]
    ts = 256
    last = s // ts - 1

    def body(a_ref, ap_ref, an_ref, w_ref, ya_ref):
        i = pl.program_id(0)
        ab, _, _, az, _, _, _, yc, _ = _conv_core(a_ref[...], ap_ref[...], an_ref[...], w_ref[...], i, last, ts)
        ya_ref[...] = ((ab * yc) * (az * _sigmoid(az))).astype(BF16)

    prev, nxt = _halo_specs(ts, W_A, COL_A, s // 8)
    return pl.pallas_call(
        body, grid=(s // ts,),
        in_specs=[pl.BlockSpec((ts, W_A), lambda i: (i, COL_A)), prev, nxt,
                  pl.BlockSpec((3, A_WIDTH), lambda i: (0, 0))],
        out_specs=pl.BlockSpec((ts, A_WIDTH), lambda i: (i, 0)),
        out_shape=jax.ShapeDtypeStruct((s, A_WIDTH), BF16), name="conv_fwd",
        compiler_params=_params())(proj, proj, proj, w_conv)


def _conv_bwd(proj, w_conv, dya, dproj):
    s = proj.shape[0]
    ts = 256
    last = s // ts - 1
    c = A_WIDTH

    def body(a_ref, ap_ref, an_ref, w_ref, d_ref, dp_ref, dn_ref, _, dproj_ref, dw_ref):
        i = pl.program_id(0)
        w = w_ref[...]
        prev, nxt = ap_ref[...], an_ref[...]
        ab, ac, ax, az, cu, cm1, cp1, yc, row = _conv_core(a_ref[...], prev, nxt, w, i, last, ts)
        sg = _sigmoid(az)
        sz = az * sg
        dya_v = d_ref[...]
        dyc = dya_v * sz * ab
        dproj_ref[:, :c] = (dya_v * sz * yc).astype(BF16)
        dproj_ref[:, 3 * c:] = (dya_v * (ab * yc) * (sg * (1.0 + az * (1.0 - sg)))).astype(BF16)

        def halo_dyc(a_row, d_row):
            azr = a_row[:, 3 * c:]
            return d_row * (azr * _sigmoid(azr)) * a_row[:, :c]

        dyc_prev = halo_dyc(prev[7:8], dp_ref[...][7:8]) * jnp.where(i > 0, 1.0, 0.0)
        dyc_next = halo_dyc(nxt[0:1], dn_ref[...][0:1]) * jnp.where(i < last, 1.0, 0.0)
        dyc_m1 = jnp.where(row == 0, dyc_prev, pltpu.roll(dyc, 1, 0))
        dyc_p1 = jnp.where(row == ts - 1, dyc_next, pltpu.roll(dyc, ts - 1, 0))
        dcu = dyc_p1 * w[0:1] + dyc * w[1:2] + dyc_m1 * w[2:3]
        dproj_ref[:, c:2 * c] = (dcu * ax).astype(BF16)
        dproj_ref[:, 2 * c:3 * c] = (dcu * ac).astype(BF16)
        dw = [jnp.sum(dyc * t, axis=0, keepdims=True) for t in (cm1, cu, cp1)]

        @pl.when(i == 0)
        def _():
            for k in range(3):
                dw_ref[k:k + 1, :] = dw[k]

        @pl.when(i > 0)
        def _():
            for k in range(3):
                dw_ref[k:k + 1, :] += dw[k]

    prev, nxt = _halo_specs(ts, W_A, COL_A, s // 8)
    dprev, dnxt = _halo_specs(ts, A_WIDTH, 0, s // 8)
    return pl.pallas_call(
        body, grid=(s // ts,),
        in_specs=[pl.BlockSpec((ts, W_A), lambda i: (i, COL_A)), prev, nxt,
                  pl.BlockSpec((3, A_WIDTH), lambda i: (0, 0)),
                  pl.BlockSpec((ts, A_WIDTH), lambda i: (i, 0)), dprev, dnxt,
                  pl.BlockSpec(memory_space=pl.ANY)],
        out_specs=[pl.BlockSpec((ts, W_A), lambda i: (i, COL_A)), pl.BlockSpec((3, A_WIDTH), lambda i: (0, 0))],
        out_shape=[jax.ShapeDtypeStruct(dproj.shape, BF16), jax.ShapeDtypeStruct((3, A_WIDTH), F32)],
        input_output_aliases={7: 0}, name="conv_bwd",
        compiler_params=_params())(proj, proj, proj, w_conv, dya, dya, dya, dproj)


def _rope_tables(s):
    half = ROT_DIM // 2
    inv_freq = jnp.power(jnp.float32(ROPE_THETA), -jnp.arange(half, dtype=F32) * (2.0 / ROT_DIM))
    ang = jnp.arange(s).astype(F32)[:, None] * inv_freq[None, :]
    cos, sin = jnp.cos(ang), jnp.sin(ang)
    pad = jnp.zeros((s, HEAD_DIM - ROT_DIM), F32)
    c = jnp.concatenate([cos, cos, pad + 1.0], axis=1)
    s1 = jnp.concatenate([-sin, jnp.zeros_like(sin), pad], axis=1)
    s2 = jnp.concatenate([jnp.zeros_like(sin), sin, pad], axis=1)
    return jnp.concatenate([c, c, s1, s1, s2, s2], axis=1)


def _rope(t, tab):
    return (t * tab[:, :LANES] + pltpu.roll(t, LANES - 8, 1) * tab[:, LANES:2 * LANES]
            + pltpu.roll(t, 8, 1) * tab[:, 2 * LANES:])


def _rope_transpose(dt, tab):
    return (dt * tab[:, :LANES] + pltpu.roll(dt * tab[:, LANES:2 * LANES], 8, 1)
            + pltpu.roll(dt * tab[:, 2 * LANES:], LANES - 8, 1))


def _rope_kv(proj, tab):
    s = proj.shape[0]
    nb = s // KV_PAD

    def body(kv_ref, t_ref, k_ref, v_ref):
        j = pl.program_id(0)
        inside = jnp.where((j > 0) & (j <= nb), 1.0, 0.0)
        kv = kv_ref[...]
        k_ref[...] = (_rope(kv[:, :LANES], t_ref[...]) * inside).astype(BF16)
        v_ref[...] = (kv[:, LANES:] * inside).astype(BF16)

    def src(j):
        return jnp.clip(j - 1, 0, nb - 1)

    o_spec = pl.BlockSpec((KV_PAD, LANES), lambda j: (j, 0))
    shp = jax.ShapeDtypeStruct((s + 2 * KV_PAD, LANES), BF16)
    return pl.pallas_call(
        body, grid=(nb + 2,),
        in_specs=[pl.BlockSpec((KV_PAD, W_KV), lambda j: (src(j), COL_KV)),
                  pl.BlockSpec((KV_PAD, 3 * LANES), lambda j: (src(j), 0))],
        out_specs=[o_spec, o_spec], out_shape=[shp, shp], name="rope_kv",
        compiler_params=_params())(proj, tab)


def _rope_kv_bwd(dkpad, dvpad, tab, dproj):
    s = tab.shape[0]
    nb = s // KV_PAD

    def body(dk_ref, dv_ref, t_ref, _, dp_ref):
        dp_ref[:, :LANES] = _rope_transpose(dk_ref[...], t_ref[...]).astype(BF16)
        dp_ref[:, LANES:] = dv_ref[...].astype(BF16)

    pad_spec = pl.BlockSpec((KV_PAD, LANES), lambda j: (j + 1, 0))
    return pl.pallas_call(
        body, grid=(nb,),
        in_specs=[pad_spec, pad_spec, pl.BlockSpec((KV_PAD, 3 * LANES), lambda j: (j, 0)),
                  pl.BlockSpec(memory_space=pl.ANY)],
        out_specs=pl.BlockSpec((KV_PAD, W_KV), lambda j: (j, COL_KV)),
        out_shape=jax.ShapeDtypeStruct(dproj.shape, BF16), input_output_aliases={3: 0},
        name="rope_kv_bwd", compiler_params=_params())(dkpad, dvpad, tab, dproj)


def _window_start(n):
    return pl.multiple_of((n - 1) * WINDOW_BLOCK + KV_PAD, WINDOW_BLOCK)


def _window_operands(k_ref, v_ref, n, lo):
    start = _window_start(n)
    kw = k_ref[pl.ds(start, 3 * WINDOW_BLOCK), :].astype(F32)
    vw = v_ref[pl.ds(start, 3 * WINDOW_BLOCK), :].astype(F32)
    kr, vr = pltpu.roll(kw, HALF_LANES, 1), pltpu.roll(vw, HALF_LANES, 1)
    k2 = (jnp.where(lo, kw, kr).astype(BF16), jnp.where(lo, kr, kw).astype(BF16))
    v2 = (jnp.where(lo, vw, vr).astype(BF16), jnp.where(lo, vr, vw).astype(BF16))
    return k2, v2


HEADS_PER_GROUP = 4


def _window_mask(n, s):
    wb = WINDOW_BLOCK
    shape = (HEADS_PER_GROUP * wb, 3 * wb)
    qi = lax.broadcasted_iota(jnp.int32, shape, 0) & (wb - 1)
    kj = lax.broadcasted_iota(jnp.int32, shape, 1)
    kpos = kj + (n - 1) * wb
    return (kj >= qi) & (kj <= qi + 2 * wb) & (kpos >= 0) & (kpos < s)


def _stack_heads(pair0, pair1, lo):
    return jnp.concatenate([jnp.where(lo, pair0, 0.0), jnp.where(lo, 0.0, pair0),
                            jnp.where(lo, pair1, 0.0), jnp.where(lo, 0.0, pair1)], axis=0)


def _unstack_pair(stacked, i, lo):
    wb = WINDOW_BLOCK
    return jnp.where(lo, stacked[2 * i * wb:(2 * i + 1) * wb], stacked[(2 * i + 1) * wb:(2 * i + 2) * wb])


def _sink_column(sink_ref, g):
    wb = WINDOW_BLOCK
    return jnp.concatenate([jnp.full((wb, 1), sink_ref[0, HEADS_PER_GROUP * g + i], F32)
                            for i in range(HEADS_PER_GROUP)], axis=0)


def _head_probs(q4, k2g, valid, sink):
    sc = lax.dot_general(q4, k2g, _DIMS["nt"], preferred_element_type=F32) * (HEAD_DIM ** -0.5)
    sc = jnp.where(valid, sc, -jnp.inf)
    m = jnp.maximum(jnp.max(sc, axis=1, keepdims=True), sink)
    e = jnp.exp(sc - m)
    es = jnp.exp(sink - m)
    inv = 1.0 / (jnp.sum(e, axis=1, keepdims=True) + es)
    return e * inv, es * inv


def _swa_fwd(proj, kpad, vpad, tab, sink):
    s = proj.shape[0]
    wb = WINDOW_BLOCK

    def body(b_ref, k_ref, v_ref, t_ref, sink_ref, o_ref, y_ref):
        n = pl.program_id(0)
        lo = lax.broadcasted_iota(jnp.int32, (wb, LANES), 1) < HALF_LANES
        lo_w = lax.broadcasted_iota(jnp.int32, (3 * wb, LANES), 1) < HALF_LANES
        k2, v2 = _window_operands(k_ref, v_ref, n, lo_w)
        valid = _window_mask(n, s)
        tab_v = t_ref[...]
        for g in range(2):
            qr = [_rope(b_ref[:, (2 * g + i) * LANES:(2 * g + i + 1) * LANES], tab_v) for i in range(2)]
            q4 = _stack_heads(qr[0], qr[1], lo).astype(BF16)
            prob, _ = _head_probs(q4, k2[g], valid, _sink_column(sink_ref, g))
            o4 = jnp.dot(prob.astype(BF16), v2[g], preferred_element_type=F32)
            for i in range(2):
                cols = slice((2 * g + i) * LANES, (2 * g + i + 1) * LANES)
                op = _unstack_pair(o4, i, lo)
                o_ref[:, cols] = op
                zp = b_ref[:, A_WIDTH + cols.start:A_WIDTH + cols.stop]
                y_ref[:, cols] = (op * (zp * _sigmoid(zp))).astype(BF16)

    pad_spec = pl.BlockSpec((s + 2 * KV_PAD, LANES), lambda n: (0, 0))
    o_spec = pl.BlockSpec((wb, A_WIDTH), lambda n: (n, 0))
    return pl.pallas_call(
        body, grid=(s // wb,),
        in_specs=[pl.BlockSpec((wb, W_B), lambda n: (n, COL_B)), pad_spec, pad_spec,
                  pl.BlockSpec((wb, 3 * LANES), lambda n: (n, 0)),
                  pl.BlockSpec(memory_space=pltpu.SMEM)],
        out_specs=[o_spec, o_spec],
        out_shape=[jax.ShapeDtypeStruct((s, A_WIDTH), F32), jax.ShapeDtypeStruct((s, A_WIDTH), BF16)],
        name="swa_fwd", compiler_params=_params())(proj, kpad, vpad, tab, sink)


def _swa_bwd(proj, kpad, vpad, tab, sink, o_attn, dyb, dproj):
    s = proj.shape[0]
    wb = WINDOW_BLOCK
    scale = HEAD_DIM ** -0.5

    def body(b_ref, k_ref, v_ref, t_ref, sink_ref, o_ref, dy_ref, _, dp_ref, dk_ref, dv_ref, ds_ref):
        n = pl.program_id(0)

        @pl.when(n == 0)
        def _():
            dk_ref[...] = jnp.zeros_like(dk_ref)
            dv_ref[...] = jnp.zeros_like(dv_ref)
            ds_ref[...] = jnp.zeros_like(ds_ref)

        lo = lax.broadcasted_iota(jnp.int32, (wb, LANES), 1) < HALF_LANES
        lo_w = lax.broadcasted_iota(jnp.int32, (3 * wb, LANES), 1) < HALF_LANES
        k2, v2 = _window_operands(k_ref, v_ref, n, lo_w)
        valid = _window_mask(n, s)
        tab_v = t_ref[...]
        dks, dvs = [], []
        for g in range(2):
            qr, op, do = [], [], []
            for i in range(2):
                cols = slice((2 * g + i) * LANES, (2 * g + i + 1) * LANES)
                zcols = slice(A_WIDTH + cols.start, A_WIDTH + cols.stop)
                qr.append(_rope(b_ref[:, cols], tab_v))
                zp = b_ref[:, zcols]
                sg = _sigmoid(zp)
                op.append(o_ref[:, cols])
                dyp = dy_ref[:, cols]
                do.append(dyp * (zp * sg))
                dp_ref[:, zcols] = (dyp * op[i] * (sg * (1.0 + zp * (1.0 - sg)))).astype(BF16)
            q4 = _stack_heads(qr[0], qr[1], lo).astype(BF16)
            do4 = _stack_heads(do[0], do[1], lo)
            o4 = jnp.concatenate([op[0], op[0], op[1], op[1]], axis=0)
            prob, psink = _head_probs(q4, k2[g], valid, _sink_column(sink_ref, g))
            delta = jnp.sum(do4 * o4, axis=1, keepdims=True)
            do4b = do4.astype(BF16)
            dprob = lax.dot_general(do4b, v2[g], _DIMS["nt"], preferred_element_type=F32)
            dsc = (prob * (dprob - delta)).astype(BF16)
            sink_terms = psink * delta
            for i in range(HEADS_PER_GROUP):
                h = HEADS_PER_GROUP * g + i
                dsink = -jnp.sum(sink_terms[i * wb:(i + 1) * wb], axis=0, keepdims=True)
                ds_ref[h:h + 1, :] += jnp.broadcast_to(dsink, (1, LANES))
            dq4 = jnp.dot(dsc, k2[g], preferred_element_type=F32) * scale
            for i in range(2):
                cols = slice((2 * g + i) * LANES, (2 * g + i + 1) * LANES)
                dp_ref[:, cols] = _rope_transpose(_unstack_pair(dq4, i, lo), tab_v).astype(BF16)
            dk2 = lax.dot_general(dsc, q4, _DIMS["tn"], preferred_element_type=F32) * scale
            dv2 = lax.dot_general(prob.astype(BF16), do4b, _DIMS["tn"], preferred_element_type=F32)
            dks.append(dk2 + pltpu.roll(dk2, HALF_LANES, 1))
            dvs.append(dv2 + pltpu.roll(dv2, HALF_LANES, 1))
        start = _window_start(n)
        dk_ref[pl.ds(start, 3 * wb), :] += jnp.where(lo_w, dks[0], dks[1])
        dv_ref[pl.ds(start, 3 * wb), :] += jnp.where(lo_w, dvs[0], dvs[1])

    pad_spec = pl.BlockSpec((s + 2 * KV_PAD, LANES), lambda n: (0, 0))
    blk = pl.BlockSpec((wb, A_WIDTH), lambda n: (n, 0))
    bsp = pl.BlockSpec((wb, W_B), lambda n: (n, COL_B))
    pad_shape = jax.ShapeDtypeStruct((s + 2 * KV_PAD, LANES), F32)
    return pl.pallas_call(
        body, grid=(s // wb,),
        in_specs=[bsp, pad_spec, pad_spec, pl.BlockSpec((wb, 3 * LANES), lambda n: (n, 0)),
                  pl.BlockSpec(memory_space=pltpu.SMEM), blk, blk, pl.BlockSpec(memory_space=pl.ANY)],
        out_specs=[bsp, pad_spec, pad_spec, pl.BlockSpec((8, LANES), lambda n: (0, 0))],
        out_shape=[jax.ShapeDtypeStruct(dproj.shape, BF16), pad_shape, pad_shape,
                   jax.ShapeDtypeStruct((8, LANES), F32)],
        input_output_aliases={7: 0}, name="swa_bwd",
        compiler_params=_params())(proj, kpad, vpad, tab, sink, o_attn, dyb, dproj)


def _mem_probs(qh, mk):
    sc = lax.dot_general(qh, mk, _DIMS["nt"], preferred_element_type=F32) * (MEM_HEAD_DIM ** -0.5)
    e = jnp.exp(sc - jnp.max(sc, axis=1, keepdims=True))
    return e * (1.0 / jnp.sum(e, axis=1, keepdims=True))


def _mem_fwd(proj, mkv):
    s = proj.shape[0]
    ts = 512
    mlen = mkv.shape[0]

    def body(m_ref, kv_ref, o_ref, y_ref):
        for h in range(MEM_HEADS):
            cols = slice(h * LANES, (h + 1) * LANES)
            mk = kv_ref[:, cols].astype(BF16)
            mv = kv_ref[:, MEM_WIDTH + h * LANES:MEM_WIDTH + (h + 1) * LANES].astype(BF16)
            prob = _mem_probs(m_ref[:, cols].astype(BF16), mk)
            oh = jnp.dot(prob.astype(BF16), mv, preferred_element_type=F32)
            o_ref[:, cols] = oh
            zh = m_ref[:, MEM_WIDTH + h * LANES:MEM_WIDTH + (h + 1) * LANES]
            y_ref[:, cols] = (oh * (zh * _sigmoid(zh))).astype(BF16)

    o_spec = pl.BlockSpec((ts, MEM_WIDTH), lambda i: (i, 0))
    return pl.pallas_call(
        body, grid=(s // ts,),
        in_specs=[pl.BlockSpec((ts, W_M), lambda i: (i, COL_M)),
                  pl.BlockSpec((mlen, 2 * MEM_WIDTH), lambda i: (0, 0))],
        out_specs=[o_spec, o_spec],
        out_shape=[jax.ShapeDtypeStruct((s, MEM_WIDTH), F32), jax.ShapeDtypeStruct((s, MEM_WIDTH), BF16)],
        name="mem_fwd", compiler_params=_params())(proj, mkv)


def _mem_bwd(proj, mkv, o_mem, dym, dproj):
    s = proj.shape[0]
    ts = 512
    mlen = mkv.shape[0]
    scale = MEM_HEAD_DIM ** -0.5

    def body(m_ref, kv_ref, o_ref, dy_ref, _, dp_ref, dkv_ref):
        @pl.when(pl.program_id(0) == 0)
        def _():
            dkv_ref[...] = jnp.zeros_like(dkv_ref)

        for h in range(MEM_HEADS):
            cols = slice(h * LANES, (h + 1) * LANES)
            vcols = slice(MEM_WIDTH + h * LANES, MEM_WIDTH + (h + 1) * LANES)
            mk = kv_ref[:, cols].astype(BF16)
            mv = kv_ref[:, vcols].astype(BF16)
            qh = m_ref[:, cols].astype(BF16)
            zh = m_ref[:, vcols]
            sg = _sigmoid(zh)
            oh = o_ref[:, cols]
            dyh = dy_ref[:, cols]
            doh = dyh * (zh * sg)
            dp_ref[:, vcols] = (dyh * oh * (sg * (1.0 + zh * (1.0 - sg)))).astype(BF16)
            prob = _mem_probs(qh, mk)
            delta = jnp.sum(doh * oh, axis=1, keepdims=True)
            dohb = doh.astype(BF16)
            dprob = lax.dot_general(dohb, mv, _DIMS["nt"], preferred_element_type=F32)
            dsc = (prob * (dprob - delta)).astype(BF16)
            dp_ref[:, cols] = (jnp.dot(dsc, mk, preferred_element_type=F32) * scale).astype(BF16)
            dkv_ref[:, cols] += lax.dot_general(dsc, qh, _DIMS["tn"], preferred_element_type=F32) * scale
            dkv_ref[:, vcols] += lax.dot_general(prob.astype(BF16), dohb, _DIMS["tn"],
                                                 preferred_element_type=F32)

    blk = pl.BlockSpec((ts, MEM_WIDTH), lambda i: (i, 0))
    msp = pl.BlockSpec((ts, W_M), lambda i: (i, COL_M))
    kvsp = pl.BlockSpec((mlen, 2 * MEM_WIDTH), lambda i: (0, 0))
    return pl.pallas_call(
        body, grid=(s // ts,),
        in_specs=[msp, kvsp, blk, blk, pl.BlockSpec(memory_space=pl.ANY)],
        out_specs=[msp, kvsp],
        out_shape=[jax.ShapeDtypeStruct(dproj.shape, BF16), jax.ShapeDtypeStruct(mkv.shape, F32)],
        input_output_aliases={4: 0}, name="mem_bwd",
        compiler_params=_params())(proj, mkv, o_mem, dym, dproj)


def _forward_backward(x, mem, tgt, proj, w_conv, sink, g_mem, w_kv, w_up, w_out, g_post):
    s = x.shape[0]
    tab = _rope_tables(s)

    ya = _conv_fwd(proj, w_conv)
    kpad, vpad = _rope_kv(proj, tab)
    o_attn, yb = _swa_fwd(proj, kpad, vpad, tab, sink)
    mn = _rmsnorm_fwd(mem, g_mem, name="mem_norm")
    mkv = _matmul(mn, w_kv, mode="nn", out_dtype=F32, tm=256, tn=1024, tk=D_MODEL, name="mem_kv")
    o_mem, ym = _mem_fwd(proj, mkv)
    merged, d_out, dy, dg_post, loss = _mid_fwd(ya, yb, ym, proj, x, tgt, w_up, w_out, g_post)
    dproj, d_ya, d_yb, d_ym, dw_up, dw_out = _mid_bwd(d_out, merged, ya, yb, ym, proj, w_up, w_out)

    dproj, dw_conv = _conv_bwd(proj, w_conv, d_ya, dproj)
    dproj, dkpad, dvpad, dsink = _swa_bwd(proj, kpad, vpad, tab, sink, o_attn, d_yb, dproj)
    dproj = _rope_kv_bwd(dkpad, dvpad, tab, dproj)
    dproj, d_mkv = _mem_bwd(proj, mkv, o_mem, d_ym, dproj)

    dw_kv = _matmul(mn, d_mkv, mode="tn", out_dtype=F32, tm=1024, tn=1024, tk=256, name="dw_kv")
    d_mn = _matmul(d_mkv, w_kv, mode="nt", out_dtype=F32, tm=256, tn=1024, tk=D_MODEL, name="d_mn")
    _, dg_mem = _rmsnorm_bwd(d_mn, mem, g_mem, d_mn, name="mem_norm_bwd")

    return dict(loss=loss, dproj=dproj, dy=dy, w_conv=dw_conv, sink=dsink, g_mem=dg_mem,
                w_kv=dw_kv, w_up=dw_up, w_out=dw_out, g_post=dg_post)


N_DEV = 8


def _position():
    return lax.axis_index("x"), lax.axis_index("y"), lax.axis_index("c")


def _other_chips(x, y):
    return (((1 - x, y), 2 * (1 - x) + y), ((x, 1 - y), 2 * x + (1 - y)), ((1 - x, 1 - y), 2 * (1 - x) + (1 - y)))


def _remote(src, dst, send_sems, recv_sems, k, device):
    return pltpu.make_async_remote_copy(src_ref=src, dst_ref=dst, send_sem=send_sems.at[k], recv_sem=recv_sems.at[k],
                                        device_id=device, device_id_type=MESH)


def _rows_half(ref, hf):
    rh = ref.shape[0] // 2
    return ref.at[pl.ds(pl.multiple_of(hf * rh, 8), rh)]


def _gather_weights(shards, small=None):
    n = len(shards)
    k = 0 if small is None else 1

    def ici(ins, outs, sems, a, r, chip, src_chip, c):
        return _remote(_rows_half(ins[a], c), _rows_half(outs[a].at[src_chip], c), sems[0], sems[1], 3 * a + r,
                       (*chip, c))

    def whole(ins, outs, sems, r, chip, src_chip, c):
        return _remote(ins[n], outs[n].at[src_chip], sems[0], sems[1], 3 * n + r, (*chip, c))

    def d2d(outs, sems, a, r, idx, hf, x, y, c):
        half = _rows_half(outs[a].at[idx], hf)
        return _remote(half, half, sems[2], sems[3], 3 * a + r, (x, y, 1 - c))

    def start(ins, outs, sems):
        x, y, c = _position()
        me = 2 * x + y
        for a in range(n):
            for r, (chip, _) in enumerate(_other_chips(x, y)):
                ici(ins, outs, sems, a, r, chip, me, c).start()
        for r, (chip, _) in enumerate(_other_chips(x, y)):
            if k:
                whole(ins, outs, sems, r, chip, me, c).start()

    def finish(ins, outs, sems):
        x, y, c = _position()
        me = 2 * x + y
        chips = _other_chips(x, y)
        for a in range(n):
            for r, (chip, idx) in enumerate(chips):
                ici(ins, outs, sems, a, r, chip, idx, c).wait_recv()
                d2d(outs, sems, a, r, idx, c, x, y, c).start()
        for a in range(n):
            for r, (chip, idx) in enumerate(chips):
                d2d(outs, sems, a, r, idx, 1 - c, x, y, c).wait_recv()
        for r, (chip, idx) in enumerate(chips):
            if k:
                whole(ins, outs, sems, r, chip, idx, c).wait_recv()
                whole(ins, outs, sems, r, chip, me, c).wait_send()
        for a in range(n):
            for r, (chip, idx) in enumerate(chips):
                ici(ins, outs, sems, a, r, chip, me, c).wait_send()
                d2d(outs, sems, a, r, idx, c, x, y, c).wait_send()

    operands = list(shards) + ([small] if k else [])
    return _Carry(operands, [jax.ShapeDtypeStruct((N_CHIPS,) + s.shape, s.dtype) for s in operands],
                  [pltpu.SemaphoreType.DMA((3 * (n + k),)), pltpu.SemaphoreType.DMA((3 * (n + k),)),
                   pltpu.SemaphoreType.DMA((3 * n,)), pltpu.SemaphoreType.DMA((3 * n,))], start, finish)


def _run_carry(carry, name):
    _, results = _carried_call(lambda ins, outs, scr: None, carry, grid=(1,), in_specs=[], out_specs=[],
                               out_shape=[], scratch=[], operands=(), name=name)
    return results


def _pair_exchange(send):
    n = len(send)

    def copies(ins, outs, sems):
        x, y, c = _position()
        return [_remote(ins[a], outs[a], sems[0], sems[1], a, (x, y, 1 - c)) for a in range(n)]

    def start(ins, outs, sems):
        for cp in copies(ins, outs, sems):
            cp.start()

    def finish(ins, outs, sems):
        for cp in copies(ins, outs, sems):
            cp.wait()

    return _Carry(send, [jax.ShapeDtypeStruct(p.shape, p.dtype) for p in send],
                  [pltpu.SemaphoreType.DMA((n,)), pltpu.SemaphoreType.DMA((n,))], start, finish)


def _chip_exchange(sums):
    n = len(sums)

    def copies(ins, outs, sems):
        x, y, c = _position()
        return [_remote(ins[a].at[idx], outs[a].at[r], sems[0], sems[1], 3 * a + r, (*chip, c))
                for a in range(n) for r, (chip, idx) in enumerate(_other_chips(x, y))]

    def start(ins, outs, sems):
        for cp in copies(ins, outs, sems):
            cp.start()

    def finish(ins, outs, sems):
        for cp in copies(ins, outs, sems):
            cp.wait()

    return _Carry(sums, [jax.ShapeDtypeStruct((3,) + p.shape[1:], p.dtype) for p in sums],
                  [pltpu.SemaphoreType.DMA((3 * n,)), pltpu.SemaphoreType.DMA((3 * n,))], start, finish)


def _pair_share(pairs):
    n = len(pairs)

    def start(ins, outs, sems):
        x, y, c = _position()
        for a in range(n):
            _remote(outs[a].at[c], outs[a].at[c], sems[0], sems[1], a, (x, y, 1 - c)).start()

    def finish(ins, outs, sems):
        x, y, c = _position()
        for a in range(n):
            _remote(outs[a].at[1 - c], outs[a].at[1 - c], sems[0], sems[1], a, (x, y, 1 - c)).wait_recv()
        for a in range(n):
            _remote(outs[a].at[c], outs[a].at[c], sems[0], sems[1], a, (x, y, 1 - c)).wait_send()

    return _Carry(pairs, [jax.ShapeDtypeStruct(p.shape, p.dtype) for p in pairs],
                  [pltpu.SemaphoreType.DMA((n,)), pltpu.SemaphoreType.DMA((n,))], start, finish,
                  aliases={a: a for a in range(n)})


def _small_allreduce(pack):
    rows, width = pack.shape

    def body(p_ref, o_ref, buf, send_sems, recv_sems):
        x, y, c = _position()
        me = 4 * x + 2 * y + c
        buf[me] = p_ref[...]
        peers = []
        for r in range(1, N_DEV):
            fx, fy, fc = (r >> 2) & 1, (r >> 1) & 1, r & 1
            px, py, pc = (1 - x if fx else x), (1 - y if fy else y), (1 - c if fc else c)
            peers.append(((px, py, pc), 4 * px + 2 * py + pc))
        sends = [_remote(p_ref, buf.at[me], send_sems, recv_sems, r, dev) for r, (dev, _) in enumerate(peers)]
        for cp in sends:
            cp.start()
        for r, (dev, idx) in enumerate(peers):
            _remote(p_ref, buf.at[idx], send_sems, recv_sems, r, dev).wait_recv()
        for cp in sends:
            cp.wait_send()
        acc = buf[0]
        for k in range(1, N_DEV):
            acc = acc + buf[k]
        o_ref[...] = acc

    vm = pl.BlockSpec(memory_space=pltpu.VMEM)
    return pl.pallas_call(
        body, in_specs=[vm], out_specs=vm, out_shape=jax.ShapeDtypeStruct(pack.shape, F32),
        scratch_shapes=[pltpu.VMEM((N_DEV, rows, width), F32), pltpu.SemaphoreType.DMA((N_DEV - 1,)),
                        pltpu.SemaphoreType.DMA((N_DEV - 1,))],
        name="small_allreduce")(pack)


def _row_tile(rows):
    return rows if rows <= 256 else 256


def _pair_add(keep, recv, name):
    nj, rh, cols = keep.shape
    tr = _row_tile(rh)

    def body(k_ref, r_ref, o_ref):
        o_ref[...] = (k_ref[...].astype(F32) + r_ref[...].astype(F32)).astype(BF16)

    blk = pl.BlockSpec((None, tr, cols), lambda j, i: (j, i, 0))
    return pl.pallas_call(body, grid=(nj, rh // tr), in_specs=[blk, blk], out_specs=blk,
                          out_shape=jax.ShapeDtypeStruct(keep.shape, BF16), name=name,
                          compiler_params=_params())(keep, recv)


def _chip_add(sums, recv, where, name):
    _, rh, cols = sums.shape
    tr = _row_tile(rh)

    def body(w_ref, s_ref, r_ref, o_ref):
        o_ref[...] = ((s_ref[...].astype(F32) + r_ref[0].astype(F32)) + r_ref[1].astype(F32)) + r_ref[2].astype(F32)

    grid_spec = pltpu.PrefetchScalarGridSpec(
        num_scalar_prefetch=1, grid=(rh // tr,),
        in_specs=[pl.BlockSpec((None, tr, cols), lambda i, w_ref: (w_ref[0], i, 0)),
                  pl.BlockSpec((3, tr, cols), lambda i, w_ref: (0, i, 0))],
        out_specs=pl.BlockSpec((None, tr, cols), lambda i, w_ref: (w_ref[1], i, 0)))
    return pl.pallas_call(body, grid_spec=grid_spec, out_shape=jax.ShapeDtypeStruct((2, rh, cols), F32),
                          name=name, compiler_params=_params())(where, sums, recv)


def _adamw(w, g, m, v, name):
    rows, cols = w.shape
    tr = _row_tile(rows)
    assert rows % tr == 0

    def body(w_ref, g_ref, m_ref, v_ref, d_ref, mo_ref, vo_ref):
        gv = g_ref[...]
        m_new = ADAM_B1 * m_ref[...] + (1.0 - ADAM_B1) * gv
        v_new = ADAM_B2 * v_ref[...] + (1.0 - ADAM_B2) * jnp.square(gv)
        m_hat = m_new / (1.0 - ADAM_B1 ** ADAM_STEP)
        v_hat = v_new / (1.0 - ADAM_B2 ** ADAM_STEP)
        d_ref[...] = -ADAM_LR * (m_hat / (jnp.sqrt(v_hat) + ADAM_EPS) + ADAM_WD * w_ref[...])
        mo_ref[...] = m_new
        vo_ref[...] = v_new

    blk = pl.BlockSpec((tr, cols), lambda i: (i, 0))
    shp = jax.ShapeDtypeStruct((rows, cols), F32)
    return pl.pallas_call(body, grid=(rows // tr,), in_specs=[blk] * 4, out_specs=[blk] * 3,
                          out_shape=[shp] * 3, name=name, compiler_params=_params())(w, g, m, v)


SHARD_W = IN_WIDTH // N_CHIPS


def _half_major(a):
    r, c = a.shape
    return a.reshape(N_CHIPS, 2, r // N_CHIPS // 2, c).transpose(1, 0, 2, 3)


def _w_in_permuted(pieces):
    cols = []
    for a, b in PERM_SEGS:
        pos = a
        while pos < b:
            j = pos // SHARD_W
            stop = min(b, (j + 1) * SHARD_W)
            cols.append(pieces[j][:, pos - j * SHARD_W:stop - j * SHARD_W])
            pos = stop
    return jnp.concatenate(cols, axis=1)


def _ref_cols(a_p, lo, hi):
    out, ref_off = [], 0
    for a, b in UNPERM_SEGS:
        r0, r1 = ref_off, ref_off + (b - a)
        s, e = max(lo, r0), min(hi, r1)
        if s < e:
            out.append(a_p[:, a + (s - r0):a + (e - r0)])
        ref_off = r1
    return jnp.concatenate(out, axis=1)


def kernel(x, mem, g_pre, w_in, w_conv, attn_sink, g_mem, w_mem_kv, w_up_a, w_up_b, w_up_m, w_out, g_post, loss_target, m_g_pre, m_w_in, m_w_conv, m_attn_sink, m_g_mem, m_w_mem_kv, m_w_up_a, m_w_up_b, m_w_up_m, m_w_out, m_g_post, v_g_pre, v_w_in, v_w_conv, v_attn_sink, v_g_mem, v_w_mem_kv, v_w_up_a, v_w_up_b, v_w_up_m, v_w_out, v_g_post):
    xi, yi, ci = _position()
    chip = 2 * xi + yi
    where = jnp.stack([chip, ci]).astype(jnp.int32)

    own = [w_in[0].astype(BF16), w_mem_kv[0].astype(BF16),
           jnp.concatenate([w_up_a[0], w_up_b[0], w_up_m[0]], axis=0).astype(BF16), w_out[0].astype(BF16)]
    own_conv = jnp.pad(w_conv[0], ((0, 5), (0, 0)))

    def pieces(mine, got):
        return [jnp.where(chip == j, mine, got[j]) for j in range(N_CHIPS)]

    got_in, got_conv = _run_carry(_gather_weights(own[:1], own_conv), "gather_w_in")
    w_in_p = _w_in_permuted(pieces(own[0], got_in))
    h = _rmsnorm_fwd(x[0], g_pre, name="pre_norm")
    proj, gathered = _matmul(h, w_in_p, mode="nn", out_dtype=F32, tm=512, tn=3712, tk=D_MODEL, name="proj",
                             j_outer=True, carry=_gather_weights(own[1:]))
    w_kv_full = jnp.concatenate(pieces(own[1], gathered[0]), axis=0)
    up_pieces = pieces(own[2], gathered[1])
    w_up_full = jnp.stack([jnp.concatenate([p[k * A_WIDTH:(k + 1) * A_WIDTH] for p in up_pieces], axis=1)
                           for k in range(3)])
    w_out_full = jnp.concatenate(pieces(own[3], gathered[2]), axis=0)
    w_conv_full = jnp.concatenate([p[:3] for p in pieces(own_conv, got_conv)], axis=1)

    g = _forward_backward(x[0], mem[0], loss_target[0], proj, w_conv_full, attn_sink, g_mem, w_kv_full, w_up_full,
                          w_out_full, g_post)

    half_rows = D_MODEL // 2
    up_parts = (g["w_up"].reshape(3, A_WIDTH, N_CHIPS, D_MODEL // N_CHIPS).transpose(2, 0, 1, 3)
                .reshape(N_CHIPS, 2, 3 * A_WIDTH // 2, D_MODEL // N_CHIPS).transpose(1, 0, 2, 3)).astype(BF16)
    small_parts = [_half_major(g["w_kv"]).astype(BF16), up_parts, _half_major(g["w_out"]).astype(BF16)]

    def dw_in_half(hf, name, carry=None):
        h_half = lax.dynamic_slice_in_dim(h, hf * half_rows, half_rows, axis=1)
        return _matmul(h_half, g["dproj"], mode="tn", out_dtype=F32, tm=half_rows, tn=3712, tk=512, name=name,
                       carry=carry)

    def shard_major(dw_half):
        return jnp.stack([_ref_cols(dw_half, j * SHARD_W, (j + 1) * SHARD_W) for j in range(N_CHIPS)]).astype(BF16)

    def pick(parts, hf):
        return [lax.dynamic_index_in_dim(p, hf, 0, keepdims=False) for p in parts]

    names = ["w_in", "w_kv", "w_up", "w_out"]
    send = [shard_major(dw_in_half(1 - ci, "dw_in_send"))] + pick(small_parts, 1 - ci)
    dw_keep, recv = dw_in_half(ci, "dw_in_keep", _pair_exchange(send))
    keep = [shard_major(dw_keep)] + pick(small_parts, ci)
    sums = [_pair_add(k, r, "pair_add_" + nm) for k, r, nm in zip(keep, recv, names)]
    d_h, recv3 = _matmul(g["dproj"], w_in_p, mode="nt", out_dtype=F32, tm=512, tn=1024, tk=3712, name="d_h",
                         carry=_chip_exchange(sums))
    pairs = [_chip_add(s, r, where, "chip_add_" + nm) for s, r, nm in zip(sums, recv3, names)]
    grad_x, dg_pre, full = _rmsnorm_bwd(d_h, x[0], g_pre, g["dy"], name="pre_norm_bwd", carry=_pair_share(pairs))

    zeros512 = jnp.zeros((1, D_MODEL - A_WIDTH), F32)
    conv_rows = [jnp.concatenate([g["w_conv"][k:k + 1], zeros512], axis=1) for k in range(3)]
    sink_row = jnp.pad(g["sink"][:, 0].reshape(1, N_Q_HEADS), ((0, 0), (0, D_MODEL - N_Q_HEADS)))
    loss_row = jnp.pad(g["loss"], ((0, 0), (0, D_MODEL - LANES)))
    pack = jnp.concatenate([dg_pre, g["g_mem"], g["g_post"]] + conv_rows + [sink_row, loss_row], axis=0)
    red = _small_allreduce(pack)
    loss = red[7, 0]
    small_grads = dict(
        g_pre=red[0:1], g_mem=red[1:2], g_post=red[2:3], attn_sink=red[6:7, :N_Q_HEADS],
        w_conv=lax.dynamic_slice(red[3:6, :A_WIDTH], (0, chip * LANES), (3, LANES)))

    gw_up = full[2].reshape(3, A_WIDTH, D_MODEL // N_CHIPS)
    grads = dict(small_grads, w_in=full[0].reshape(D_MODEL, SHARD_W),
                 w_mem_kv=full[1].reshape(D_MODEL // N_CHIPS, 2 * MEM_WIDTH),
                 w_up_a=gw_up[0], w_up_b=gw_up[1], w_up_m=gw_up[2],
                 w_out=full[3].reshape(D_MODEL // N_CHIPS, D_MODEL))

    weights = dict(g_pre=g_pre, w_in=w_in, w_conv=w_conv, attn_sink=attn_sink, g_mem=g_mem, w_mem_kv=w_mem_kv,
                   w_up_a=w_up_a, w_up_b=w_up_b, w_up_m=w_up_m, w_out=w_out, g_post=g_post)
    m_in = dict(g_pre=m_g_pre, w_in=m_w_in, w_conv=m_w_conv, attn_sink=m_attn_sink, g_mem=m_g_mem,
                w_mem_kv=m_w_mem_kv, w_up_a=m_w_up_a, w_up_b=m_w_up_b, w_up_m=m_w_up_m, w_out=m_w_out,
                g_post=m_g_post)
    v_in = dict(g_pre=v_g_pre, w_in=v_w_in, w_conv=v_w_conv, attn_sink=v_attn_sink, g_mem=v_g_mem,
                w_mem_kv=v_w_mem_kv, w_up_a=v_w_up_a, w_up_b=v_w_up_b, w_up_m=v_w_up_m, w_out=v_w_out,
                g_post=v_g_post)
    out_g, out_d, out_m, out_v = [], [], [], []
    for nm in ("g_pre", "w_in", "w_conv", "attn_sink", "g_mem", "w_mem_kv", "w_up_a", "w_up_b", "w_up_m", "w_out",
               "g_post"):
        shape = weights[nm].shape
        two_d = shape[-2:]
        gr = grads[nm].reshape(two_d)
        d, m_new, v_new = _adamw(weights[nm].reshape(two_d), gr, m_in[nm].reshape(two_d), v_in[nm].reshape(two_d),
                                 "adamw_" + nm)
        out_g.append(gr.reshape(shape))
        out_d.append(d.reshape(shape))
        out_m.append(m_new.reshape(shape))
        out_v.append(v_new.reshape(shape))
    return (loss, grad_x.reshape(x.shape), *out_g, *out_d, *out_m, *out_v)
```

```python
import functools

import jax
import jax.numpy as jnp
from jax import lax
from jax.experimental import pallas as pl
from jax.experimental.pallas import tpu as pltpu

F32 = jnp.float32
BF16 = jnp.bfloat16
MESH = pl.DeviceIdType.MESH

D_MODEL = 1024
EPS = 1e-6
A_WIDTH = 512
HEAD_DIM = 64
N_Q_HEADS = 8
WINDOW_BLOCK = 128
KV_PAD = 512
ROPE_THETA = 500000.0
ROT_DIM = 16
MEM_HEADS = 4
MEM_HEAD_DIM = 128
MEM_WIDTH = 512
IN_WIDTH = 7424
N_CHIPS = 4
LANES = 128
HALF_LANES = 64

PERM_SEGS = ((0, 2560), (2816, 3328), (4352, 7424), (3328, 4352), (2560, 2816))
UNPERM_SEGS = ((0, 2560), (7168, 7424), (2560, 3072), (6144, 7168), (3072, 6144))
COL_A, W_A = 0, 2048
COL_B, W_B = 2, 1024
COL_G, W_G = 1, 3072
COL_M, W_M = 6, 1024
COL_KV, W_KV = 28, 256

ADAM_LR = 0.001
ADAM_B1 = 0.9
ADAM_B2 = 0.999
ADAM_EPS = 1e-08
ADAM_WD = 0.01
ADAM_STEP = 10

VMEM_LIMIT_BYTES = 48 * 1024 * 1024


_HBM = pl.BlockSpec(memory_space=pltpu.HBM)


def _params(**kw):
    return pltpu.CompilerParams(vmem_limit_bytes=VMEM_LIMIT_BYTES, **kw)


def _sigmoid(v):
    return jax.nn.sigmoid(v)


_DIMS = {"nn": (((1,), (0,)), ((), ())), "nt": (((1,), (1,)), ((), ())), "tn": (((0,), (0,)), ((), ()))}


class _Carry:
    def __init__(self, ins, out_shapes, sems, start, finish, aliases=None):
        self.ins, self.out_shapes, self.sems = list(ins), list(out_shapes), list(sems)
        self.start, self.finish, self.aliases = start, finish, dict(aliases or {})


def _carried_call(body, carry, *, grid, in_specs, out_specs, out_shape, scratch, operands, name):
    n_in, n_out, n_scr = len(in_specs), len(out_specs), len(scratch)
    c_in = len(carry.ins) if carry else 0
    c_out = len(carry.out_shapes) if carry else 0
    steps = 1
    for g in grid:
        steps *= g

    def wrapped(*refs):
        ins, cins = refs[:n_in], refs[n_in:n_in + c_in]
        outs = refs[n_in + c_in:n_in + c_in + n_out]
        couts = refs[n_in + c_in + n_out:n_in + c_in + n_out + c_out]
        rest = refs[n_in + c_in + n_out + c_out:]
        scr, sems = rest[:n_scr], rest[n_scr:]
        if carry:
            step = pl.program_id(0)
            for ax in range(1, len(grid)):
                step = step * grid[ax] + pl.program_id(ax)

            @pl.when(step == 0)
            def _():
                carry.start(cins, couts, sems)

        body(ins, outs, scr)
        if carry:
            @pl.when(step == steps - 1)
            def _():
                carry.finish(cins, couts, sems)

    aliases = {n_in + i: n_out + o for i, o in carry.aliases.items()} if carry else {}
    results = pl.pallas_call(
        wrapped, grid=grid, in_specs=list(in_specs) + [_HBM] * c_in, out_specs=list(out_specs) + [_HBM] * c_out,
        out_shape=list(out_shape) + (carry.out_shapes if carry else []),
        scratch_shapes=list(scratch) + (carry.sems if carry else []), input_output_aliases=aliases,
        name=name, compiler_params=_params())(*operands, *(carry.ins if carry else []))
    return list(results[:n_out]), list(results[n_out:])


def _matmul(a, b, *, mode, out_dtype, tm, tn, tk, name, j_outer=False, carry=None):
    if mode == "nn":
        (m, k), (_, n) = a.shape, b.shape
    elif mode == "nt":
        (m, k), (n, _) = a.shape, b.shape
    else:
        (k, m), (_, n) = a.shape, b.shape
    tm, tn, tk = min(tm, m), min(tn, n), min(tk, k)
    assert m % tm == 0 and n % tn == 0 and k % tk == 0
    ni, nj, nk = m // tm, n // tn, k // tk
    dims = _DIMS[mode]

    def ij(g0, g1):
        return (g1, g0) if j_outer else (g0, g1)

    if mode == "nn":
        a_spec = pl.BlockSpec((tm, tk), lambda g0, g1, kk: (ij(g0, g1)[0], kk))
        b_spec = pl.BlockSpec((tk, tn), lambda g0, g1, kk: (kk, ij(g0, g1)[1]))
    elif mode == "nt":
        a_spec = pl.BlockSpec((tm, tk), lambda g0, g1, kk: (ij(g0, g1)[0], kk))
        b_spec = pl.BlockSpec((tn, tk), lambda g0, g1, kk: (ij(g0, g1)[1], kk))
    else:
        a_spec = pl.BlockSpec((tk, tm), lambda g0, g1, kk: (kk, ij(g0, g1)[0]))
        b_spec = pl.BlockSpec((tk, tn), lambda g0, g1, kk: (kk, ij(g0, g1)[1]))
    o_spec = pl.BlockSpec((tm, tn), lambda g0, g1, kk: ij(g0, g1))

    def part(a_ref, b_ref):
        return lax.dot_general(a_ref[...].astype(BF16), b_ref[...].astype(BF16), dims,
                               preferred_element_type=F32)

    if nk == 1:
        def body(ins, outs, scr):
            outs[0][...] = part(*ins).astype(out_dtype)
        scratch = []
    else:
        def body(ins, outs, scr):
            kk = pl.program_id(2)
            acc_ref = scr[0]

            @pl.when(kk == 0)
            def _():
                acc_ref[...] = part(*ins)

            @pl.when(kk > 0)
            def _():
                acc_ref[...] += part(*ins)

            @pl.when(kk == nk - 1)
            def _():
                outs[0][...] = acc_ref[...].astype(out_dtype)
        scratch = [pltpu.VMEM((tm, tn), F32)]

    grid = (nj, ni, nk) if j_outer else (ni, nj, nk)
    (out,), carried = _carried_call(
        body, carry, grid=grid, in_specs=[a_spec, b_spec], out_specs=[o_spec],
        out_shape=[jax.ShapeDtypeStruct((m, n), out_dtype)], scratch=scratch, operands=(a, b), name=name)
    return (out, carried) if carry else out


def _rmsnorm_fwd(x, g, *, name):
    s, d = x.shape
    ts = min(512, s)

    def body(x_ref, g_ref, o_ref):
        xv = x_ref[...]
        r = lax.rsqrt(jnp.mean(xv * xv, axis=-1, keepdims=True) + EPS)
        o_ref[...] = ((xv * r) * g_ref[...]).astype(BF16)

    return pl.pallas_call(
        body, grid=(s // ts,),
        in_specs=[pl.BlockSpec((ts, d), lambda i: (i, 0)), pl.BlockSpec((1, d), lambda i: (0, 0))],
        out_specs=pl.BlockSpec((ts, d), lambda i: (i, 0)),
        out_shape=jax.ShapeDtypeStruct((s, d), BF16), name=name, compiler_params=_params())(x, g)


def _rmsnorm_bwd(dh, x, g, res, *, name, carry=None):
    s, d = x.shape
    ts = min(256, s)

    def body(ins, outs, scr):
        dh_ref, x_ref, g_ref, res_ref = ins
        dx_ref, dg_ref = outs
        xv = x_ref[...]
        r = lax.rsqrt(jnp.mean(xv * xv, axis=-1, keepdims=True) + EPS)
        xh = xv * r
        dhv = dh_ref[...]
        part = jnp.sum(dhv * xh, axis=0, keepdims=True)

        @pl.when(pl.program_id(0) == 0)
        def _():
            dg_ref[...] = part

        @pl.when(pl.program_id(0) > 0)
        def _():
            dg_ref[...] += part

        dxh = dhv * g_ref[...]
        dx_ref[...] = res_ref[...] + r * (dxh - xh * jnp.mean(dxh * xh, axis=-1, keepdims=True))

    row = pl.BlockSpec((ts, d), lambda i: (i, 0))
    vec = pl.BlockSpec((1, d), lambda i: (0, 0))
    outs, carried = _carried_call(
        body, carry, grid=(s // ts,), in_specs=[row, row, vec, row], out_specs=[row, vec],
        out_shape=[jax.ShapeDtypeStruct((s, d), F32), jax.ShapeDtypeStruct((1, d), F32)],
        scratch=[], operands=(dh, x, g, res), name=name)
    return (*outs, carried) if carry else tuple(outs)


MID_TILE = 256


def _gated_branches(y_refs, wup_ref, gl):
    d = D_MODEL
    us = [jnp.dot(y_refs[k][...], wup_ref[k], preferred_element_type=F32) for k in range(3)]
    sg = [_sigmoid(gl[:, k * d:(k + 1) * d]) for k in range(3)]
    return us, sg


def _mid_fwd(ya, yb, ym, proj, x, tgt, w_up, w_out, g_post):
    s, d = x.shape
    ts = MID_TILE

    def body(ya_ref, yb_ref, ym_ref, g_ref, x_ref, t_ref, wup_ref, wout_ref, gp_ref,
             m_ref, do_ref, dy_ref, dg_ref, loss_ref):
        us, sg = _gated_branches((ya_ref, yb_ref, ym_ref), wup_ref, g_ref[...])
        merged = (sg[0] * us[0] + sg[1] * us[1] + sg[2] * us[2]).astype(BF16)
        m_ref[...] = merged
        ov = jnp.dot(merged, wout_ref[...], preferred_element_type=F32)
        r = lax.rsqrt(jnp.mean(ov * ov, axis=-1, keepdims=True) + EPS)
        nh = ov * r
        gv = gp_ref[...]
        e = (x_ref[...] + nh * gv) - t_ref[...]
        lpart = 0.5 * jnp.sum(jnp.mean(e * e, axis=-1, keepdims=True), axis=0, keepdims=True)
        dy = e * (1.0 / d)
        dgp = jnp.sum(dy * nh, axis=0, keepdims=True)

        @pl.when(pl.program_id(0) == 0)
        def _():
            dg_ref[...] = dgp
            loss_ref[...] = jnp.broadcast_to(lpart, loss_ref.shape)

        @pl.when(pl.program_id(0) > 0)
        def _():
            dg_ref[...] += dgp
            loss_ref[...] += jnp.broadcast_to(lpart, loss_ref.shape)

        dn = dy * gv
        dy_ref[...] = dy
        do_ref[...] = (r * (dn - nh * jnp.mean(dn * nh, axis=-1, keepdims=True))).astype(BF16)

    row = pl.BlockSpec((ts, d), lambda i: (i, 0))
    ysp = pl.BlockSpec((ts, A_WIDTH), lambda i: (i, 0))
    vec = pl.BlockSpec((1, d), lambda i: (0, 0))
    return pl.pallas_call(
        body, grid=(s // ts,),
        in_specs=[ysp, ysp, ysp, pl.BlockSpec((ts, W_G), lambda i: (i, COL_G)), row, row,
                  pl.BlockSpec((3, A_WIDTH, d), lambda i: (0, 0, 0)), pl.BlockSpec((d, d), lambda i: (0, 0)), vec],
        out_specs=[row, row, row, vec, pl.BlockSpec((1, LANES), lambda i: (0, 0))],
        out_shape=[jax.ShapeDtypeStruct((s, d), BF16), jax.ShapeDtypeStruct((s, d), BF16),
                   jax.ShapeDtypeStruct((s, d), F32), jax.ShapeDtypeStruct((1, d), F32),
                   jax.ShapeDtypeStruct((1, LANES), F32)],
        name="mid_fwd", compiler_params=_params())(ya, yb, ym, proj, x, tgt, w_up, w_out, g_post)


def _mid_bwd(d_out, merged, ya, yb, ym, proj, w_up, w_out):
    s, d = merged.shape
    ts = MID_TILE
    last = s // ts - 1

    def body(do_ref, m_ref, ya_ref, yb_ref, ym_ref, g_ref, wup_ref, wout_ref,
             dp_ref, dya_ref, dyb_ref, dym_ref, dwup_hbm, dwout_hbm, dwup_acc, dwout_acc):
        i = pl.program_id(0)

        @pl.when(i == 0)
        def _():
            dwup_acc[...] = jnp.zeros_like(dwup_acc)
            dwout_acc[...] = jnp.zeros_like(dwout_acc)

        y_refs = (ya_ref, yb_ref, ym_ref)
        us, sg = _gated_branches(y_refs, wup_ref, g_ref[...])
        dov = do_ref[...]
        dwout_acc[...] += lax.dot_general(m_ref[...], dov, _DIMS["tn"], preferred_element_type=F32)
        dm = lax.dot_general(dov, wout_ref[...], _DIMS["nt"], preferred_element_type=F32)
        for k, dy_ref in enumerate((dya_ref, dyb_ref, dym_ref)):
            dp_ref[:, k * d:(k + 1) * d] = ((dm * us[k]) * (sg[k] * (1.0 - sg[k]))).astype(BF16)
            du = (sg[k] * dm).astype(BF16)
            dy_ref[...] = lax.dot_general(du, wup_ref[k], _DIMS["nt"], preferred_element_type=F32)
            dwup_acc[k] += lax.dot_general(y_refs[k][...], du, _DIMS["tn"], preferred_element_type=F32)

        @pl.when(i == last)
        def _():
            pltpu.sync_copy(dwup_acc, dwup_hbm)
            pltpu.sync_copy(dwout_acc, dwout_hbm)

    row = pl.BlockSpec((ts, d), lambda i: (i, 0))
    ysp = pl.BlockSpec((ts, A_WIDTH), lambda i: (i, 0))
    gsp = pl.BlockSpec((ts, W_G), lambda i: (i, COL_G))
    anysp = pl.BlockSpec(memory_space=pl.ANY)
    yshape = jax.ShapeDtypeStruct((s, A_WIDTH), F32)
    return pl.pallas_call(
        body, grid=(s // ts,),
        in_specs=[row, row, ysp, ysp, ysp, gsp, pl.BlockSpec((3, A_WIDTH, d), lambda i: (0, 0, 0)),
                  pl.BlockSpec((d, d), lambda i: (0, 0))],
        out_specs=[gsp, ysp, ysp, ysp, anysp, anysp],
        out_shape=[jax.ShapeDtypeStruct((s, IN_WIDTH), BF16), yshape, yshape, yshape,
                   jax.ShapeDtypeStruct((3, A_WIDTH, d), F32), jax.ShapeDtypeStruct((d, d), F32)],
        scratch_shapes=[pltpu.VMEM((3, A_WIDTH, d), F32), pltpu.VMEM((d, d), F32)],
        name="mid_bwd", compiler_params=_params())(d_out, merged, ya, yb, ym, proj, w_up, w_out)


def _conv_core(blk, prev, nxt, w, i, last, ts):
    c = A_WIDTH
    ab, ac, ax, az = blk[:, :c], blk[:, c:2 * c], blk[:, 2 * c:3 * c], blk[:, 3 * c:]
    cu = ac * ax
    cu_prev = (prev[7:8, c:2 * c] * prev[7:8, 2 * c:3 * c]) * jnp.where(i > 0, 1.0, 0.0)
    cu_next = (nxt[0:1, c:2 * c] * nxt[0:1, 2 * c:3 * c]) * jnp.where(i < last, 1.0, 0.0)
    row = lax.broadcasted_iota(jnp.int32, (ts, c), 0)
    cm1 = jnp.where(row == 0, cu_prev, pltpu.roll(cu, 1, 0))
    cp1 = jnp.where(row == ts - 1, cu_next, pltpu.roll(cu, ts - 1, 0))
    yc = cm1 * w[0:1] + cu * w[1:2] + cp1 * w[2:3]
    return ab, ac, ax, az, cu, cm1, cp1, yc, row


def _halo_specs(ts, width, col, nblk8):
    prev = pl.BlockSpec((8, width), lambda i: (jnp.maximum(i * (ts // 8) - 1, 0), col))
    nxt = pl.BlockSpec((8, width), lambda i: (jnp.minimum((i + 1) * (ts // 8), nblk8 - 1), col))
    return prev, nxt


def _conv_fwd(proj, w_conv):
    s = proj.shape[0]
    ts = 256
    last = s // ts - 1

    def body(a_ref, ap_ref, an_ref, w_ref, ya_ref):
        i = pl.program_id(0)
        ab, _, _, az, _, _, _, yc, _ = _conv_core(a_ref[...], ap_ref[...], an_ref[...], w_ref[...], i, last, ts)
        ya_ref[...] = ((ab * yc) * (az * _sigmoid(az))).astype(BF16)

    prev, nxt = _halo_specs(ts, W_A, COL_A, s // 8)
    return pl.pallas_call(
        body, grid=(s // ts,),
        in_specs=[pl.BlockSpec((ts, W_A), lambda i: (i, COL_A)), prev, nxt,
                  pl.BlockSpec((3, A_WIDTH), lambda i: (0, 0))],
        out_specs=pl.BlockSpec((ts, A_WIDTH), lambda i: (i, 0)),
        out_shape=jax.ShapeDtypeStruct((s, A_WIDTH), BF16), name="conv_fwd",
        compiler_params=_params())(proj, proj, proj, w_conv)


def _conv_bwd(proj, w_conv, dya, dproj):
    s = proj.shape[0]
    ts = 256
    last = s // ts - 1
    c = A_WIDTH

    def body(a_ref, ap_ref, an_ref, w_ref, d_ref, dp_ref, dn_ref, _, dproj_ref, dw_ref):
        i = pl.program_id(0)
        w = w_ref[...]
        prev, nxt = ap_ref[...], an_ref[...]
        ab, ac, ax, az, cu, cm1, cp1, yc, row = _conv_core(a_ref[...], prev, nxt, w, i, last, ts)
        sg = _sigmoid(az)
        sz = az * sg
        dya_v = d_ref[...]
        dyc = dya_v * sz * ab
        dproj_ref[:, :c] = (dya_v * sz * yc).astype(BF16)
        dproj_ref[:, 3 * c:] = (dya_v * (ab * yc) * (sg * (1.0 + az * (1.0 - sg)))).astype(BF16)

        def halo_dyc(a_row, d_row):
            azr = a_row[:, 3 * c:]
            return d_row * (azr * _sigmoid(azr)) * a_row[:, :c]

        dyc_prev = halo_dyc(prev[7:8], dp_ref[...][7:8]) * jnp.where(i > 0, 1.0, 0.0)
        dyc_next = halo_dyc(nxt[0:1], dn_ref[...][0:1]) * jnp.where(i < last, 1.0, 0.0)
        dyc_m1 = jnp.where(row == 0, dyc_prev, pltpu.roll(dyc, 1, 0))
        dyc_p1 = jnp.where(row == ts - 1, dyc_next, pltpu.roll(dyc, ts - 1, 0))
        dcu = dyc_p1 * w[0:1] + dyc * w[1:2] + dyc_m1 * w[2:3]
        dproj_ref[:, c:2 * c] = (dcu * ax).astype(BF16)
        dproj_ref[:, 2 * c:3 * c] = (dcu * ac).astype(BF16)
        dw = [jnp.sum(dyc * t, axis=0, keepdims=True) for t in (cm1, cu, cp1)]

        @pl.when(i == 0)
        def _():
            for k in range(3):
                dw_ref[k:k + 1, :] = dw[k]

        @pl.when(i > 0)
        def _():
            for k in range(3):
                dw_ref[k:k + 1, :] += dw[k]

    prev, nxt = _halo_specs(ts, W_A, COL_A, s // 8)
    dprev, dnxt = _halo_specs(ts, A_WIDTH, 0, s // 8)
    return pl.pallas_call(
        body, grid=(s // ts,),
        in_specs=[pl.BlockSpec((ts, W_A), lambda i: (i, COL_A)), prev, nxt,
                  pl.BlockSpec((3, A_WIDTH), lambda i: (0, 0)),
                  pl.BlockSpec((ts, A_WIDTH), lambda i: (i, 0)), dprev, dnxt,
                  pl.BlockSpec(memory_space=pl.ANY)],
        out_specs=[pl.BlockSpec((ts, W_A), lambda i: (i, COL_A)), pl.BlockSpec((3, A_WIDTH), lambda i: (0, 0))],
        out_shape=[jax.ShapeDtypeStruct(dproj.shape, BF16), jax.ShapeDtypeStruct((3, A_WIDTH), F32)],
        input_output_aliases={7: 0}, name="conv_bwd",
        compiler_params=_params())(proj, proj, proj, w_conv, dya, dya, dya, dproj)


def _rope_tables(s):
    half = ROT_DIM // 2
    inv_freq = jnp.power(jnp.float32(ROPE_THETA), -jnp.arange(half, dtype=F32) * (2.0 / ROT_DIM))
    ang = jnp.arange(s).astype(F32)[:, None] * inv_freq[None, :]
    cos, sin = jnp.cos(ang), jnp.sin(ang)
    pad = jnp.zeros((s, HEAD_DIM - ROT_DIM), F32)
    c = jnp.concatenate([cos, cos, pad + 1.0], axis=1)
    s1 = jnp.concatenate([-sin, jnp.zeros_like(sin), pad], axis=1)
    s2 = jnp.concatenate([jnp.zeros_like(sin), sin, pad], axis=1)
    return jnp.concatenate([c, c, s1, s1, s2, s2], axis=1)


def _rope(t, tab):
    return (t * tab[:, :LANES] + pltpu.roll(t, LANES - 8, 1) * tab[:, LANES:2 * LANES]
            + pltpu.roll(t, 8, 1) * tab[:, 2 * LANES:])


def _rope_transpose(dt, tab):
    return (dt * tab[:, :LANES] + pltpu.roll(dt * tab[:, LANES:2 * LANES], 8, 1)
            + pltpu.roll(dt * tab[:, 2 * LANES:], LANES - 8, 1))


def _rope_kv(proj, tab):
    s = proj.shape[0]
    nb = s // KV_PAD

    def body(kv_ref, t_ref, k_ref, v_ref):
        j = pl.program_id(0)
        inside = jnp.where((j > 0) & (j <= nb), 1.0, 0.0)
        kv = kv_ref[...]
        k_ref[...] = (_rope(kv[:, :LANES], t_ref[...]) * inside).astype(BF16)
        v_ref[...] = (kv[:, LANES:] * inside).astype(BF16)

    def src(j):
        return jnp.clip(j - 1, 0, nb - 1)

    o_spec = pl.BlockSpec((KV_PAD, LANES), lambda j: (j, 0))
    shp = jax.ShapeDtypeStruct((s + 2 * KV_PAD, LANES), BF16)
    return pl.pallas_call(
        body, grid=(nb + 2,),
        in_specs=[pl.BlockSpec((KV_PAD, W_KV), lambda j: (src(j), COL_KV)),
                  pl.BlockSpec((KV_PAD, 3 * LANES), lambda j: (src(j), 0))],
        out_specs=[o_spec, o_spec], out_shape=[shp, shp], name="rope_kv",
        compiler_params=_params())(proj, tab)


def _rope_kv_bwd(dkpad, dvpad, tab, dproj):
    s = tab.shape[0]
    nb = s // KV_PAD

    def body(dk_ref, dv_ref, t_ref, _, dp_ref):
        dp_ref[:, :LANES] = _rope_transpose(dk_ref[...], t_ref[...]).astype(BF16)
        dp_ref[:, LANES:] = dv_ref[...].astype(BF16)

    pad_spec = pl.BlockSpec((KV_PAD, LANES), lambda j: (j + 1, 0))
    return pl.pallas_call(
        body, grid=(nb,),
        in_specs=[pad_spec, pad_spec, pl.BlockSpec((KV_PAD, 3 * LANES), lambda j: (j, 0)),
                  pl.BlockSpec(memory_space=pl.ANY)],
        out_specs=pl.BlockSpec((KV_PAD, W_KV), lambda j: (j, COL_KV)),
        out_shape=jax.ShapeDtypeStruct(dproj.shape, BF16), input_output_aliases={3: 0},
        name="rope_kv_bwd", compiler_params=_params())(dkpad, dvpad, tab, dproj)


def _window_start(n):
    return pl.multiple_of((n - 1) * WINDOW_BLOCK + KV_PAD, WINDOW_BLOCK)


def _window_operands(k_ref, v_ref, n, lo):
    start = _window_start(n)
    kw = k_ref[pl.ds(start, 3 * WINDOW_BLOCK), :].astype(F32)
    vw = v_ref[pl.ds(start, 3 * WINDOW_BLOCK), :].astype(F32)
    kr, vr = pltpu.roll(kw, HALF_LANES, 1), pltpu.roll(vw, HALF_LANES, 1)
    k2 = (jnp.where(lo, kw, kr).astype(BF16), jnp.where(lo, kr, kw).astype(BF16))
    v2 = (jnp.where(lo, vw, vr).astype(BF16), jnp.where(lo, vr, vw).astype(BF16))
    return k2, v2


HEADS_PER_GROUP = 4
SWA_FWD_BLOCKS = 1
SWA_BWD_BLOCKS = 2


def _window_mask(n, s):
    wb = WINDOW_BLOCK
    shape = (HEADS_PER_GROUP * wb, 3 * wb)
    qi = lax.broadcasted_iota(jnp.int32, shape, 0) & (wb - 1)
    kj = lax.broadcasted_iota(jnp.int32, shape, 1)
    kpos = kj + (n - 1) * wb
    return (kj >= qi) & (kj <= qi + 2 * wb) & (kpos >= 0) & (kpos < s)


def _stack_heads(pair0, pair1, lo):
    return jnp.concatenate([jnp.where(lo, pair0, 0.0), jnp.where(lo, 0.0, pair0),
                            jnp.where(lo, pair1, 0.0), jnp.where(lo, 0.0, pair1)], axis=0)


def _unstack_pair(stacked, i, lo):
    wb = WINDOW_BLOCK
    return jnp.where(lo, stacked[2 * i * wb:(2 * i + 1) * wb], stacked[(2 * i + 1) * wb:(2 * i + 2) * wb])


def _sink_column(sink_ref, g):
    wb = WINDOW_BLOCK
    return jnp.concatenate([jnp.full((wb, 1), sink_ref[0, HEADS_PER_GROUP * g + i], F32)
                            for i in range(HEADS_PER_GROUP)], axis=0)


def _head_probs(q4, k2g, valid, sink):
    sc = lax.dot_general(q4, k2g, _DIMS["nt"], preferred_element_type=F32) * (HEAD_DIM ** -0.5)
    sc = jnp.where(valid, sc, -jnp.inf)
    m = jnp.maximum(jnp.max(sc, axis=1, keepdims=True), sink)
    e = jnp.exp(sc - m)
    es = jnp.exp(sink - m)
    inv = 1.0 / (jnp.sum(e, axis=1, keepdims=True) + es)
    return e * inv, es * inv


def _swa_fwd(proj, kpad, vpad, tab, sink):
    s = proj.shape[0]
    wb = WINDOW_BLOCK

    def body(b_ref, k_ref, v_ref, t_ref, sink_ref, o_ref, y_ref):
        lo = lax.broadcasted_iota(jnp.int32, (wb, LANES), 1) < HALF_LANES
        lo_w = lax.broadcasted_iota(jnp.int32, (3 * wb, LANES), 1) < HALF_LANES
        for sub in range(SWA_FWD_BLOCKS):
            n = pl.program_id(0) * SWA_FWD_BLOCKS + sub
            rows = slice(sub * wb, (sub + 1) * wb)
            k2, v2 = _window_operands(k_ref, v_ref, n, lo_w)
            valid = _window_mask(n, s)
            tab_v = t_ref[rows, :]
            for g in range(2):
                qr = [_rope(b_ref[rows, (2 * g + i) * LANES:(2 * g + i + 1) * LANES], tab_v) for i in range(2)]
                q4 = _stack_heads(qr[0], qr[1], lo).astype(BF16)
                prob, _ = _head_probs(q4, k2[g], valid, _sink_column(sink_ref, g))
                o4 = jnp.dot(prob.astype(BF16), v2[g], preferred_element_type=F32)
                for i in range(2):
                    cols = slice((2 * g + i) * LANES, (2 * g + i + 1) * LANES)
                    op = _unstack_pair(o4, i, lo)
                    o_ref[rows, cols] = op
                    zp = b_ref[rows, A_WIDTH + cols.start:A_WIDTH + cols.stop]
                    y_ref[rows, cols] = (op * (zp * _sigmoid(zp))).astype(BF16)

    tq = SWA_FWD_BLOCKS * wb
    pad_spec = pl.BlockSpec((s + 2 * KV_PAD, LANES), lambda n: (0, 0))
    o_spec = pl.BlockSpec((tq, A_WIDTH), lambda n: (n, 0))
    return pl.pallas_call(
        body, grid=(s // tq,),
        in_specs=[pl.BlockSpec((tq, W_B), lambda n: (n, COL_B)), pad_spec, pad_spec,
                  pl.BlockSpec((tq, 3 * LANES), lambda n: (n, 0)),
                  pl.BlockSpec(memory_space=pltpu.SMEM)],
        out_specs=[o_spec, o_spec],
        out_shape=[jax.ShapeDtypeStruct((s, A_WIDTH), F32), jax.ShapeDtypeStruct((s, A_WIDTH), BF16)],
        name="swa_fwd", compiler_params=_params())(proj, kpad, vpad, tab, sink)


def _swa_bwd(proj, kpad, vpad, tab, sink, o_attn, dyb, dproj):
    s = proj.shape[0]
    wb = WINDOW_BLOCK
    scale = HEAD_DIM ** -0.5

    def body(b_ref, k_ref, v_ref, t_ref, sink_ref, o_ref, dy_ref, _, dp_ref, dk_ref, dv_ref, ds_ref):
        @pl.when(pl.program_id(0) == 0)
        def _():
            dk_ref[...] = jnp.zeros_like(dk_ref)
            dv_ref[...] = jnp.zeros_like(dv_ref)
            ds_ref[...] = jnp.zeros_like(ds_ref)

        lo = lax.broadcasted_iota(jnp.int32, (wb, LANES), 1) < HALF_LANES
        lo_w = lax.broadcasted_iota(jnp.int32, (3 * wb, LANES), 1) < HALF_LANES
        for sub in range(SWA_BWD_BLOCKS):
            n = pl.program_id(0) * SWA_BWD_BLOCKS + sub
            rows = slice(sub * wb, (sub + 1) * wb)
            k2, v2 = _window_operands(k_ref, v_ref, n, lo_w)
            valid = _window_mask(n, s)
            tab_v = t_ref[rows, :]
            dks, dvs = [], []
            for g in range(2):
                qr, op, do = [], [], []
                for i in range(2):
                    cols = slice((2 * g + i) * LANES, (2 * g + i + 1) * LANES)
                    zcols = slice(A_WIDTH + cols.start, A_WIDTH + cols.stop)
                    qr.append(_rope(b_ref[rows, cols], tab_v))
                    zp = b_ref[rows, zcols]
                    sg = _sigmoid(zp)
                    op.append(o_ref[rows, cols])
                    dyp = dy_ref[rows, cols]
                    do.append(dyp * (zp * sg))
                    dp_ref[rows, zcols] = (dyp * op[i] * (sg * (1.0 + zp * (1.0 - sg)))).astype(BF16)
                q4 = _stack_heads(qr[0], qr[1], lo).astype(BF16)
                do4 = _stack_heads(do[0], do[1], lo)
                o4 = jnp.concatenate([op[0], op[0], op[1], op[1]], axis=0)
                prob, psink = _head_probs(q4, k2[g], valid, _sink_column(sink_ref, g))
                delta = jnp.sum(do4 * o4, axis=1, keepdims=True)
                do4b = do4.astype(BF16)
                dprob = lax.dot_general(do4b, v2[g], _DIMS["nt"], preferred_element_type=F32)
                dsc = (prob * (dprob - delta)).astype(BF16)
                sink_terms = psink * delta
                for i in range(HEADS_PER_GROUP):
                    h = HEADS_PER_GROUP * g + i
                    dsink = -jnp.sum(sink_terms[i * wb:(i + 1) * wb], axis=0, keepdims=True)
                    ds_ref[h:h + 1, :] += jnp.broadcast_to(dsink, (1, LANES))
                dq4 = jnp.dot(dsc, k2[g], preferred_element_type=F32) * scale
                for i in range(2):
                    cols = slice((2 * g + i) * LANES, (2 * g + i + 1) * LANES)
                    dp_ref[rows, cols] = _rope_transpose(_unstack_pair(dq4, i, lo), tab_v).astype(BF16)
                dk2 = lax.dot_general(dsc, q4, _DIMS["tn"], preferred_element_type=F32) * scale
                dv2 = lax.dot_general(prob.astype(BF16), do4b, _DIMS["tn"], preferred_element_type=F32)
                dks.append(dk2 + pltpu.roll(dk2, HALF_LANES, 1))
                dvs.append(dv2 + pltpu.roll(dv2, HALF_LANES, 1))
            start = _window_start(n)
            dk_ref[pl.ds(start, 3 * wb), :] += jnp.where(lo_w, dks[0], dks[1])
            dv_ref[pl.ds(start, 3 * wb), :] += jnp.where(lo_w, dvs[0], dvs[1])

    tq = SWA_BWD_BLOCKS * wb
    pad_spec = pl.BlockSpec((s + 2 * KV_PAD, LANES), lambda n: (0, 0))
    blk = pl.BlockSpec((tq, A_WIDTH), lambda n: (n, 0))
    bsp = pl.BlockSpec((tq, W_B), lambda n: (n, COL_B))
    pad_shape = jax.ShapeDtypeStruct((s + 2 * KV_PAD, LANES), F32)
    return pl.pallas_call(
        body, grid=(s // tq,),
        in_specs=[bsp, pad_spec, pad_spec, pl.BlockSpec((tq, 3 * LANES), lambda n: (n, 0)),
                  pl.BlockSpec(memory_space=pltpu.SMEM), blk, blk, pl.BlockSpec(memory_space=pl.ANY)],
        out_specs=[bsp, pad_spec, pad_spec, pl.BlockSpec((8, LANES), lambda n: (0, 0))],
        out_shape=[jax.ShapeDtypeStruct(dproj.shape, BF16), pad_shape, pad_shape,
                   jax.ShapeDtypeStruct((8, LANES), F32)],
        input_output_aliases={7: 0}, name="swa_bwd",
        compiler_params=_params())(proj, kpad, vpad, tab, sink, o_attn, dyb, dproj)


def _mem_probs(qh, mk):
    sc = lax.dot_general(qh, mk, _DIMS["nt"], preferred_element_type=F32) * (MEM_HEAD_DIM ** -0.5)
    e = jnp.exp(sc - jnp.max(sc, axis=1, keepdims=True))
    return e * (1.0 / jnp.sum(e, axis=1, keepdims=True))


def _mem_fwd(proj, mkv):
    s = proj.shape[0]
    ts = 512
    mlen = mkv.shape[0]

    def body(m_ref, kv_ref, o_ref, y_ref):
        for h in range(MEM_HEADS):
            cols = slice(h * LANES, (h + 1) * LANES)
            mk = kv_ref[:, cols].astype(BF16)
            mv = kv_ref[:, MEM_WIDTH + h * LANES:MEM_WIDTH + (h + 1) * LANES].astype(BF16)
            prob = _mem_probs(m_ref[:, cols].astype(BF16), mk)
            oh = jnp.dot(prob.astype(BF16), mv, preferred_element_type=F32)
            o_ref[:, cols] = oh
            zh = m_ref[:, MEM_WIDTH + h * LANES:MEM_WIDTH + (h + 1) * LANES]
            y_ref[:, cols] = (oh * (zh * _sigmoid(zh))).astype(BF16)

    o_spec = pl.BlockSpec((ts, MEM_WIDTH), lambda i: (i, 0))
    return pl.pallas_call(
        body, grid=(s // ts,),
        in_specs=[pl.BlockSpec((ts, W_M), lambda i: (i, COL_M)),
                  pl.BlockSpec((mlen, 2 * MEM_WIDTH), lambda i: (0, 0))],
        out_specs=[o_spec, o_spec],
        out_shape=[jax.ShapeDtypeStruct((s, MEM_WIDTH), F32), jax.ShapeDtypeStruct((s, MEM_WIDTH), BF16)],
        name="mem_fwd", compiler_params=_params())(proj, mkv)


def _mem_bwd(proj, mkv, o_mem, dym, dproj):
    s = proj.shape[0]
    ts = 512
    mlen = mkv.shape[0]
    scale = MEM_HEAD_DIM ** -0.5

    def body(m_ref, kv_ref, o_ref, dy_ref, _, dp_ref, dkv_ref):
        @pl.when(pl.program_id(0) == 0)
        def _():
            dkv_ref[...] = jnp.zeros_like(dkv_ref)

        for h in range(MEM_HEADS):
            cols = slice(h * LANES, (h + 1) * LANES)
            vcols = slice(MEM_WIDTH + h * LANES, MEM_WIDTH + (h + 1) * LANES)
            mk = kv_ref[:, cols].astype(BF16)
            mv = kv_ref[:, vcols].astype(BF16)
            qh = m_ref[:, cols].astype(BF16)
            zh = m_ref[:, vcols]
            sg = _sigmoid(zh)
            oh = o_ref[:, cols]
            dyh = dy_ref[:, cols]
            doh = dyh * (zh * sg)
            dp_ref[:, vcols] = (dyh * oh * (sg * (1.0 + zh * (1.0 - sg)))).astype(BF16)
            prob = _mem_probs(qh, mk)
            delta = jnp.sum(doh * oh, axis=1, keepdims=True)
            dohb = doh.astype(BF16)
            dprob = lax.dot_general(dohb, mv, _DIMS["nt"], preferred_element_type=F32)
            dsc = (prob * (dprob - delta)).astype(BF16)
            dp_ref[:, cols] = (jnp.dot(dsc, mk, preferred_element_type=F32) * scale).astype(BF16)
            dkv_ref[:, cols] += lax.dot_general(dsc, qh, _DIMS["tn"], preferred_element_type=F32) * scale
            dkv_ref[:, vcols] += lax.dot_general(prob.astype(BF16), dohb, _DIMS["tn"],
                                                 preferred_element_type=F32)

    blk = pl.BlockSpec((ts, MEM_WIDTH), lambda i: (i, 0))
    msp = pl.BlockSpec((ts, W_M), lambda i: (i, COL_M))
    kvsp = pl.BlockSpec((mlen, 2 * MEM_WIDTH), lambda i: (0, 0))
    return pl.pallas_call(
        body, grid=(s // ts,),
        in_specs=[msp, kvsp, blk, blk, pl.BlockSpec(memory_space=pl.ANY)],
        out_specs=[msp, kvsp],
        out_shape=[jax.ShapeDtypeStruct(dproj.shape, BF16), jax.ShapeDtypeStruct(mkv.shape, F32)],
        input_output_aliases={4: 0}, name="mem_bwd",
        compiler_params=_params())(proj, mkv, o_mem, dym, dproj)


def _forward_backward(x, mem, tgt, proj, w_conv, sink, g_mem, w_kv, w_up, w_out, g_post):
    s = x.shape[0]
    tab = _rope_tables(s)

    ya = _conv_fwd(proj, w_conv)
    kpad, vpad = _rope_kv(proj, tab)
    o_attn, yb = _swa_fwd(proj, kpad, vpad, tab, sink)
    mn = _rmsnorm_fwd(mem, g_mem, name="mem_norm")
    mkv = _matmul(mn, w_kv, mode="nn", out_dtype=F32, tm=256, tn=1024, tk=D_MODEL, name="mem_kv")
    o_mem, ym = _mem_fwd(proj, mkv)
    merged, d_out, dy, dg_post, loss = _mid_fwd(ya, yb, ym, proj, x, tgt, w_up, w_out, g_post)
    dproj, d_ya, d_yb, d_ym, dw_up, dw_out = _mid_bwd(d_out, merged, ya, yb, ym, proj, w_up, w_out)

    dproj, dw_conv = _conv_bwd(proj, w_conv, d_ya, dproj)
    dproj, dkpad, dvpad, dsink = _swa_bwd(proj, kpad, vpad, tab, sink, o_attn, d_yb, dproj)
    dproj = _rope_kv_bwd(dkpad, dvpad, tab, dproj)
    dproj, d_mkv = _mem_bwd(proj, mkv, o_mem, d_ym, dproj)

    dw_kv = _matmul(mn, d_mkv, mode="tn", out_dtype=F32, tm=1024, tn=1024, tk=256, name="dw_kv")
    d_mn = _matmul(d_mkv, w_kv, mode="nt", out_dtype=F32, tm=256, tn=1024, tk=D_MODEL, name="d_mn")
    _, dg_mem = _rmsnorm_bwd(d_mn, mem, g_mem, d_mn, name="mem_norm_bwd")

    return dict(loss=loss, dproj=dproj, dy=dy, w_conv=dw_conv, sink=dsink, g_mem=dg_mem,
                w_kv=dw_kv, w_up=dw_up, w_out=dw_out, g_post=dg_post)


N_DEV = 8


def _position():
    return lax.axis_index("x"), lax.axis_index("y"), lax.axis_index("c")


def _other_chips(x, y):
    return (((1 - x, y), 2 * (1 - x) + y), ((x, 1 - y), 2 * x + (1 - y)), ((1 - x, 1 - y), 2 * (1 - x) + (1 - y)))


def _remote(src, dst, send_sems, recv_sems, k, device):
    return pltpu.make_async_remote_copy(src_ref=src, dst_ref=dst, send_sem=send_sems.at[k], recv_sem=recv_sems.at[k],
                                        device_id=device, device_id_type=MESH)


def _rows_half(ref, hf):
    rh = ref.shape[0] // 2
    return ref.at[pl.ds(pl.multiple_of(hf * rh, 8), rh)]


def _gather_weights(shards, small=None):
    n = len(shards)
    k = 0 if small is None else 1

    def ici(ins, outs, sems, a, r, chip, src_chip, c):
        return _remote(_rows_half(ins[a], c), _rows_half(outs[a].at[src_chip], c), sems[0], sems[1], 3 * a + r,
                       (*chip, c))

    def whole(ins, outs, sems, r, chip, src_chip, c):
        return _remote(ins[n], outs[n].at[src_chip], sems[0], sems[1], 3 * n + r, (*chip, c))

    def d2d(outs, sems, a, r, idx, hf, x, y, c):
        half = _rows_half(outs[a].at[idx], hf)
        return _remote(half, half, sems[2], sems[3], 3 * a + r, (x, y, 1 - c))

    def start(ins, outs, sems):
        x, y, c = _position()
        me = 2 * x + y
        for a in range(n):
            for r, (chip, _) in enumerate(_other_chips(x, y)):
                ici(ins, outs, sems, a, r, chip, me, c).start()
        for r, (chip, _) in enumerate(_other_chips(x, y)):
            if k:
                whole(ins, outs, sems, r, chip, me, c).start()

    def finish(ins, outs, sems):
        x, y, c = _position()
        me = 2 * x + y
        chips = _other_chips(x, y)
        for a in range(n):
            for r, (chip, idx) in enumerate(chips):
                ici(ins, outs, sems, a, r, chip, idx, c).wait_recv()
                d2d(outs, sems, a, r, idx, c, x, y, c).start()
        for a in range(n):
            for r, (chip, idx) in enumerate(chips):
                d2d(outs, sems, a, r, idx, 1 - c, x, y, c).wait_recv()
        for r, (chip, idx) in enumerate(chips):
            if k:
                whole(ins, outs, sems, r, chip, idx, c).wait_recv()
                whole(ins, outs, sems, r, chip, me, c).wait_send()
        for a in range(n):
            for r, (chip, idx) in enumerate(chips):
                ici(ins, outs, sems, a, r, chip, me, c).wait_send()
                d2d(outs, sems, a, r, idx, c, x, y, c).wait_send()

    operands = list(shards) + ([small] if k else [])
    return _Carry(operands, [jax.ShapeDtypeStruct((N_CHIPS,) + s.shape, s.dtype) for s in operands],
                  [pltpu.SemaphoreType.DMA((3 * (n + k),)), pltpu.SemaphoreType.DMA((3 * (n + k),)),
                   pltpu.SemaphoreType.DMA((3 * n,)), pltpu.SemaphoreType.DMA((3 * n,))], start, finish)


def _run_carry(carry, name):
    _, results = _carried_call(lambda ins, outs, scr: None, carry, grid=(1,), in_specs=[], out_specs=[],
                               out_shape=[], scratch=[], operands=(), name=name)
    return results


def _pair_exchange(send):
    n = len(send)

    def copies(ins, outs, sems):
        x, y, c = _position()
        return [_remote(ins[a], outs[a], sems[0], sems[1], a, (x, y, 1 - c)) for a in range(n)]

    def start(ins, outs, sems):
        for cp in copies(ins, outs, sems):
            cp.start()

    def finish(ins, outs, sems):
        for cp in copies(ins, outs, sems):
            cp.wait()

    return _Carry(send, [jax.ShapeDtypeStruct(p.shape, p.dtype) for p in send],
                  [pltpu.SemaphoreType.DMA((n,)), pltpu.SemaphoreType.DMA((n,))], start, finish)


def _chip_exchange(sums):
    n = len(sums)

    def copies(ins, outs, sems):
        x, y, c = _position()
        return [_remote(ins[a].at[idx], outs[a].at[r], sems[0], sems[1], 3 * a + r, (*chip, c))
                for a in range(n) for r, (chip, idx) in enumerate(_other_chips(x, y))]

    def start(ins, outs, sems):
        for cp in copies(ins, outs, sems):
            cp.start()

    def finish(ins, outs, sems):
        for cp in copies(ins, outs, sems):
            cp.wait()

    return _Carry(sums, [jax.ShapeDtypeStruct((3,) + p.shape[1:], p.dtype) for p in sums],
                  [pltpu.SemaphoreType.DMA((3 * n,)), pltpu.SemaphoreType.DMA((3 * n,))], start, finish)


def _pair_share(pairs):
    n = len(pairs)

    def start(ins, outs, sems):
        x, y, c = _position()
        for a in range(n):
            _remote(outs[a].at[c], outs[a].at[c], sems[0], sems[1], a, (x, y, 1 - c)).start()

    def finish(ins, outs, sems):
        x, y, c = _position()
        for a in range(n):
            _remote(outs[a].at[1 - c], outs[a].at[1 - c], sems[0], sems[1], a, (x, y, 1 - c)).wait_recv()
        for a in range(n):
            _remote(outs[a].at[c], outs[a].at[c], sems[0], sems[1], a, (x, y, 1 - c)).wait_send()

    return _Carry(pairs, [jax.ShapeDtypeStruct(p.shape, p.dtype) for p in pairs],
                  [pltpu.SemaphoreType.DMA((n,)), pltpu.SemaphoreType.DMA((n,))], start, finish,
                  aliases={a: a for a in range(n)})


def _small_allreduce(pack, share):
    rows, width = pack.shape
    n_share = len(share.ins)

    def body(p_ref, *refs):
        share_in, o_ref, share_out = refs[:n_share], refs[n_share], refs[n_share + 1:2 * n_share + 1]
        buf, send_sems, recv_sems = refs[2 * n_share + 1:2 * n_share + 4]
        share_sems = refs[2 * n_share + 4:]
        share.start(share_in, share_out, share_sems)
        x, y, c = _position()
        me = 4 * x + 2 * y + c
        buf[me] = p_ref[...]
        peers = []
        for r in range(1, N_DEV):
            fx, fy, fc = (r >> 2) & 1, (r >> 1) & 1, r & 1
            px, py, pc = (1 - x if fx else x), (1 - y if fy else y), (1 - c if fc else c)
            peers.append(((px, py, pc), 4 * px + 2 * py + pc))
        sends = [_remote(p_ref, buf.at[me], send_sems, recv_sems, r, dev) for r, (dev, _) in enumerate(peers)]
        for cp in sends:
            cp.start()
        for r, (dev, idx) in enumerate(peers):
            _remote(p_ref, buf.at[idx], send_sems, recv_sems, r, dev).wait_recv()
        for cp in sends:
            cp.wait_send()
        acc = buf[0]
        for k in range(1, N_DEV):
            acc = acc + buf[k]
        o_ref[...] = acc
        share.finish(share_in, share_out, share_sems)

    vm = pl.BlockSpec(memory_space=pltpu.VMEM)
    red, *shared = pl.pallas_call(
        body, in_specs=[vm] + [_HBM] * n_share, out_specs=[vm] + [_HBM] * n_share,
        out_shape=[jax.ShapeDtypeStruct(pack.shape, F32)] + share.out_shapes,
        scratch_shapes=[pltpu.VMEM((N_DEV, rows, width), F32), pltpu.SemaphoreType.DMA((N_DEV - 1,)),
                        pltpu.SemaphoreType.DMA((N_DEV - 1,))] + share.sems,
        input_output_aliases={1 + i: 1 + o for i, o in share.aliases.items()},
        name="small_allreduce")(pack, *share.ins)
    return red, shared


def _row_tile(rows):
    return rows if rows <= 256 else 256


def _pair_add(keep, recv, name):
    nj, rh, cols = keep.shape
    tr = _row_tile(rh)

    def body(k_ref, r_ref, o_ref):
        o_ref[...] = (k_ref[...].astype(F32) + r_ref[...].astype(F32)).astype(BF16)

    blk = pl.BlockSpec((None, tr, cols), lambda j, i: (j, i, 0))
    return pl.pallas_call(body, grid=(nj, rh // tr), in_specs=[blk, blk], out_specs=blk,
                          out_shape=jax.ShapeDtypeStruct(keep.shape, BF16), name=name,
                          compiler_params=_params())(keep, recv)


def _chip_add(sums, recv, where, name):
    _, rh, cols = sums.shape
    tr = _row_tile(rh)

    def body(w_ref, s_ref, r_ref, o_ref):
        o_ref[...] = ((s_ref[...].astype(F32) + r_ref[0].astype(F32)) + r_ref[1].astype(F32)) + r_ref[2].astype(F32)

    grid_spec = pltpu.PrefetchScalarGridSpec(
        num_scalar_prefetch=1, grid=(rh // tr,),
        in_specs=[pl.BlockSpec((None, tr, cols), lambda i, w_ref: (w_ref[0], i, 0)),
                  pl.BlockSpec((3, tr, cols), lambda i, w_ref: (0, i, 0))],
        out_specs=pl.BlockSpec((None, tr, cols), lambda i, w_ref: (w_ref[1], i, 0)))
    return pl.pallas_call(body, grid_spec=grid_spec, out_shape=jax.ShapeDtypeStruct((2, rh, cols), F32),
                          name=name, compiler_params=_params())(where, sums, recv)


def _adamw(w, g, m, v, name):
    rows, cols = w.shape
    tr = _row_tile(rows)
    assert rows % tr == 0

    def body(w_ref, g_ref, m_ref, v_ref, d_ref, mo_ref, vo_ref):
        gv = g_ref[...]
        m_new = ADAM_B1 * m_ref[...] + (1.0 - ADAM_B1) * gv
        v_new = ADAM_B2 * v_ref[...] + (1.0 - ADAM_B2) * jnp.square(gv)
        m_hat = m_new / (1.0 - ADAM_B1 ** ADAM_STEP)
        v_hat = v_new / (1.0 - ADAM_B2 ** ADAM_STEP)
        d_ref[...] = -ADAM_LR * (m_hat / (jnp.sqrt(v_hat) + ADAM_EPS) + ADAM_WD * w_ref[...])
        mo_ref[...] = m_new
        vo_ref[...] = v_new

    blk = pl.BlockSpec((tr, cols), lambda i: (i, 0))
    shp = jax.ShapeDtypeStruct((rows, cols), F32)
    return pl.pallas_call(body, grid=(rows // tr,), in_specs=[blk] * 4, out_specs=[blk] * 3,
                          out_shape=[shp] * 3, name=name, compiler_params=_params())(w, g, m, v)


SHARD_W = IN_WIDTH // N_CHIPS


def _half_major(a):
    r, c = a.shape
    return a.reshape(N_CHIPS, 2, r // N_CHIPS // 2, c).transpose(1, 0, 2, 3)


def _w_in_permuted(pieces):
    cols = []
    for a, b in PERM_SEGS:
        pos = a
        while pos < b:
            j = pos // SHARD_W
            stop = min(b, (j + 1) * SHARD_W)
            cols.append(pieces[j][:, pos - j * SHARD_W:stop - j * SHARD_W])
            pos = stop
    return jnp.concatenate(cols, axis=1)


def _ref_cols(a_p, lo, hi):
    out, ref_off = [], 0
    for a, b in UNPERM_SEGS:
        r0, r1 = ref_off, ref_off + (b - a)
        s, e = max(lo, r0), min(hi, r1)
        if s < e:
            out.append(a_p[:, a + (s - r0):a + (e - r0)])
        ref_off = r1
    return jnp.concatenate(out, axis=1)


def kernel(x, mem, g_pre, w_in, w_conv, attn_sink, g_mem, w_mem_kv, w_up_a, w_up_b, w_up_m, w_out, g_post, loss_target, m_g_pre, m_w_in, m_w_conv, m_attn_sink, m_g_mem, m_w_mem_kv, m_w_up_a, m_w_up_b, m_w_up_m, m_w_out, m_g_post, v_g_pre, v_w_in, v_w_conv, v_attn_sink, v_g_mem, v_w_mem_kv, v_w_up_a, v_w_up_b, v_w_up_m, v_w_out, v_g_post):
    xi, yi, ci = _position()
    chip = 2 * xi + yi
    where = jnp.stack([chip, ci]).astype(jnp.int32)

    own = [w_in[0].astype(BF16), w_mem_kv[0].astype(BF16),
           jnp.concatenate([w_up_a[0], w_up_b[0], w_up_m[0]], axis=0).astype(BF16), w_out[0].astype(BF16)]
    own_conv = jnp.pad(w_conv[0], ((0, 5), (0, 0)))

    def pieces(mine, got):
        got = lax.dynamic_update_slice_in_dim(got, mine[None], chip, axis=0)
        return [got[j] for j in range(N_CHIPS)]

    got_in, got_conv = _run_carry(_gather_weights(own[:1], own_conv), "gather_w_in")
    w_in_p = _w_in_permuted(pieces(own[0], got_in))
    h = _rmsnorm_fwd(x[0], g_pre, name="pre_norm")
    proj, gathered = _matmul(h, w_in_p, mode="nn", out_dtype=F32, tm=512, tn=3712, tk=D_MODEL, name="proj",
                             j_outer=True, carry=_gather_weights(own[1:]))
    w_kv_full = jnp.concatenate(pieces(own[1], gathered[0]), axis=0)
    up_pieces = pieces(own[2], gathered[1])
    w_up_full = jnp.stack([jnp.concatenate([p[k * A_WIDTH:(k + 1) * A_WIDTH] for p in up_pieces], axis=1)
                           for k in range(3)])
    w_out_full = jnp.concatenate(pieces(own[3], gathered[2]), axis=0)
    w_conv_full = jnp.concatenate([p[:3] for p in pieces(own_conv, got_conv)], axis=1)

    g = _forward_backward(x[0], mem[0], loss_target[0], proj, w_conv_full, attn_sink, g_mem, w_kv_full, w_up_full,
                          w_out_full, g_post)

    half_rows = D_MODEL // 2
    up_parts = (g["w_up"].reshape(3, A_WIDTH, N_CHIPS, D_MODEL // N_CHIPS).transpose(2, 0, 1, 3)
                .reshape(N_CHIPS, 2, 3 * A_WIDTH // 2, D_MODEL // N_CHIPS).transpose(1, 0, 2, 3)).astype(BF16)
    small_parts = [_half_major(g["w_kv"]).astype(BF16), up_parts, _half_major(g["w_out"]).astype(BF16)]

    def dw_in_half(hf, name, carry=None):
        h_half = lax.dynamic_slice_in_dim(h, hf * half_rows, half_rows, axis=1)
        return _matmul(h_half, g["dproj"], mode="tn", out_dtype=F32, tm=half_rows, tn=3712, tk=512, name=name,
                       carry=carry)

    def shard_major(dw_half):
        return jnp.stack([_ref_cols(dw_half, j * SHARD_W, (j + 1) * SHARD_W) for j in range(N_CHIPS)]).astype(BF16)

    def pick(parts, hf):
        return [lax.dynamic_index_in_dim(p, hf, 0, keepdims=False) for p in parts]

    small_names = ["w_kv", "w_up", "w_out"]
    recv_small = _run_carry(_pair_exchange(pick(small_parts, 1 - ci)), "pair_exchange_small")
    sums_small = [_pair_add(k, r, "pair_add_" + nm)
                  for k, r, nm in zip(pick(small_parts, ci), recv_small, small_names)]
    dw_send, recv3_small = dw_in_half(1 - ci, "dw_in_send", _chip_exchange(sums_small))
    dw_keep, (recv_in,) = dw_in_half(ci, "dw_in_keep", _pair_exchange([shard_major(dw_send)]))
    sum_in = _pair_add(shard_major(dw_keep), recv_in, "pair_add_w_in")
    d_h, (recv3_in,) = _matmul(g["dproj"], w_in_p, mode="nt", out_dtype=F32, tm=512, tn=1024, tk=3712, name="d_h",
                               carry=_chip_exchange([sum_in]))
    pairs = [_chip_add(s, r, where, "chip_add_" + nm)
             for s, r, nm in zip([sum_in] + sums_small, [recv3_in] + recv3_small, ["w_in"] + small_names)]
    grad_x, dg_pre = _rmsnorm_bwd(d_h, x[0], g_pre, g["dy"], name="pre_norm_bwd")

    zeros512 = jnp.zeros((1, D_MODEL - A_WIDTH), F32)
    conv_rows = [jnp.concatenate([g["w_conv"][k:k + 1], zeros512], axis=1) for k in range(3)]
    sink_row = jnp.pad(g["sink"][:, 0].reshape(1, N_Q_HEADS), ((0, 0), (0, D_MODEL - N_Q_HEADS)))
    loss_row = jnp.pad(g["loss"], ((0, 0), (0, D_MODEL - LANES)))
    pack = jnp.concatenate([dg_pre, g["g_mem"], g["g_post"]] + conv_rows + [sink_row, loss_row], axis=0)
    red, full = _small_allreduce(pack, _pair_share(pairs))
    loss = red[7, 0]
    small_grads = dict(
        g_pre=red[0:1], g_mem=red[1:2], g_post=red[2:3], attn_sink=red[6:7, :N_Q_HEADS],
        w_conv=lax.dynamic_slice(red[3:6, :A_WIDTH], (0, chip * LANES), (3, LANES)))

    gw_up = full[2].reshape(3, A_WIDTH, D_MODEL // N_CHIPS)
    grads = dict(small_grads, w_in=full[0].reshape(D_MODEL, SHARD_W),
                 w_mem_kv=full[1].reshape(D_MODEL // N_CHIPS, 2 * MEM_WIDTH),
                 w_up_a=gw_up[0], w_up_b=gw_up[1], w_up_m=gw_up[2],
                 w_out=full[3].reshape(D_MODEL // N_CHIPS, D_MODEL))

    weights = dict(g_pre=g_pre, w_in=w_in, w_conv=w_conv, attn_sink=attn_sink, g_mem=g_mem, w_mem_kv=w_mem_kv,
                   w_up_a=w_up_a, w_up_b=w_up_b, w_up_m=w_up_m, w_out=w_out, g_post=g_post)
    m_in = dict(g_pre=m_g_pre, w_in=m_w_in, w_conv=m_w_conv, attn_sink=m_attn_sink, g_mem=m_g_mem,
                w_mem_kv=m_w_mem_kv, w_up_a=m_w_up_a, w_up_b=m_w_up_b, w_up_m=m_w_up_m, w_out=m_w_out,
                g_post=m_g_post)
    v_in = dict(g_pre=v_g_pre, w_in=v_w_in, w_conv=v_w_conv, attn_sink=v_attn_sink, g_mem=v_g_mem,
                w_mem_kv=v_w_mem_kv, w_up_a=v_w_up_a, w_up_b=v_w_up_b, w_up_m=v_w_up_m, w_out=v_w_out,
                g_post=v_g_post)
    out_g, out_d, out_m, out_v = [], [], [], []
    for nm in ("g_pre", "w_in", "w_conv", "attn_sink", "g_mem", "w_mem_kv", "w_up_a", "w_up_b", "w_up_m", "w_out",
               "g_post"):
        shape = weights[nm].shape
        two_d = shape[-2:]
        gr = grads[nm].reshape(two_d)
        d, m_new, v_new = _adamw(weights[nm].reshape(two_d), gr, m_in[nm].reshape(two_d), v_in[nm].reshape(two_d),
                                 "adamw_" + nm)
        out_g.append(gr.reshape(shape))
        out_d.append(d.reshape(shape))
        out_m.append(m_new.reshape(shape))
        out_v.append(v_new.reshape(shape))
    return (loss, grad_x.reshape(x.shape), *out_g, *out_d, *out_m, *out_v)
```

```python
import functools

import jax
import jax.numpy as jnp
from jax import lax
from jax.experimental import pallas as pl
from jax.experimental.pallas import tpu as pltpu

F32 = jnp.float32
BF16 = jnp.bfloat16
MESH = pl.DeviceIdType.MESH

D_MODEL = 1024
EPS = 1e-6
A_WIDTH = 512
HEAD_DIM = 64
N_Q_HEADS = 8
WINDOW_BLOCK = 128
KV_PAD = 512
ROPE_THETA = 500000.0
ROT_DIM = 16
MEM_HEADS = 4
MEM_HEAD_DIM = 128
MEM_WIDTH = 512
IN_WIDTH = 7424
N_CHIPS = 4
LANES = 128
HALF_LANES = 64

PERM_SEGS = ((0, 2560), (2816, 3328), (4352, 7424), (3328, 4352), (2560, 2816))
UNPERM_SEGS = ((0, 2560), (7168, 7424), (2560, 3072), (6144, 7168), (3072, 6144))
COL_A, W_A = 0, 2048
COL_B, W_B = 2, 1024
COL_G, W_G = 1, 3072
COL_M, W_M = 6, 1024
COL_KV, W_KV = 28, 256

ADAM_LR = 0.001
ADAM_B1 = 0.9
ADAM_B2 = 0.999
ADAM_EPS = 1e-08
ADAM_WD = 0.01
ADAM_STEP = 10

VMEM_LIMIT_BYTES = 48 * 1024 * 1024


_HBM = pl.BlockSpec(memory_space=pltpu.HBM)


def _params(**kw):
    return pltpu.CompilerParams(vmem_limit_bytes=VMEM_LIMIT_BYTES, **kw)


def _sigmoid(v):
    return jax.nn.sigmoid(v)


_DIMS = {"nn": (((1,), (0,)), ((), ())), "nt": (((1,), (1,)), ((), ())), "tn": (((0,), (0,)), ((), ()))}


class _Carry:
    def __init__(self, ins, out_shapes, sems, start, finish, aliases=None):
        self.ins, self.out_shapes, self.sems = list(ins), list(out_shapes), list(sems)
        self.start, self.finish, self.aliases = start, finish, dict(aliases or {})


def _carried_call(body, carry, *, grid, in_specs, out_specs, out_shape, scratch, operands, name):
    n_in, n_out, n_scr = len(in_specs), len(out_specs), len(scratch)
    c_in = len(carry.ins) if carry else 0
    c_out = len(carry.out_shapes) if carry else 0
    steps = 1
    for g in grid:
        steps *= g

    def wrapped(*refs):
        ins, cins = refs[:n_in], refs[n_in:n_in + c_in]
        outs = refs[n_in + c_in:n_in + c_in + n_out]
        couts = refs[n_in + c_in + n_out:n_in + c_in + n_out + c_out]
        rest = refs[n_in + c_in + n_out + c_out:]
        scr, sems = rest[:n_scr], rest[n_scr:]
        if carry:
            step = pl.program_id(0)
            for ax in range(1, len(grid)):
                step = step * grid[ax] + pl.program_id(ax)

            @pl.when(step == 0)
            def _():
                carry.start(cins, couts, sems)

        body(ins, outs, scr)
        if carry:
            @pl.when(step == steps - 1)
            def _():
                carry.finish(cins, couts, sems)

    aliases = {n_in + i: n_out + o for i, o in carry.aliases.items()} if carry else {}
    results = pl.pallas_call(
        wrapped, grid=grid, in_specs=list(in_specs) + [_HBM] * c_in, out_specs=list(out_specs) + [_HBM] * c_out,
        out_shape=list(out_shape) + (carry.out_shapes if carry else []),
        scratch_shapes=list(scratch) + (carry.sems if carry else []), input_output_aliases=aliases,
        name=name, compiler_params=_params())(*operands, *(carry.ins if carry else []))
    return list(results[:n_out]), list(results[n_out:])


def _matmul(a, b, *, mode, out_dtype, tm, tn, tk, name, j_outer=False, carry=None):
    if mode == "nn":
        (m, k), (_, n) = a.shape, b.shape
    elif mode == "nt":
        (m, k), (n, _) = a.shape, b.shape
    else:
        (k, m), (_, n) = a.shape, b.shape
    tm, tn, tk = min(tm, m), min(tn, n), min(tk, k)
    assert m % tm == 0 and n % tn == 0 and k % tk == 0
    ni, nj, nk = m // tm, n // tn, k // tk
    dims = _DIMS[mode]

    def ij(g0, g1):
        return (g1, g0) if j_outer else (g0, g1)

    if mode == "nn":
        a_spec = pl.BlockSpec((tm, tk), lambda g0, g1, kk: (ij(g0, g1)[0], kk))
        b_spec = pl.BlockSpec((tk, tn), lambda g0, g1, kk: (kk, ij(g0, g1)[1]))
    elif mode == "nt":
        a_spec = pl.BlockSpec((tm, tk), lambda g0, g1, kk: (ij(g0, g1)[0], kk))
        b_spec = pl.BlockSpec((tn, tk), lambda g0, g1, kk: (ij(g0, g1)[1], kk))
    else:
        a_spec = pl.BlockSpec((tk, tm), lambda g0, g1, kk: (kk, ij(g0, g1)[0]))
        b_spec = pl.BlockSpec((tk, tn), lambda g0, g1, kk: (kk, ij(g0, g1)[1]))
    o_spec = pl.BlockSpec((tm, tn), lambda g0, g1, kk: ij(g0, g1))

    def part(a_ref, b_ref):
        return lax.dot_general(a_ref[...].astype(BF16), b_ref[...].astype(BF16), dims,
                               preferred_element_type=F32)

    if nk == 1:
        def body(ins, outs, scr):
            outs[0][...] = part(*ins).astype(out_dtype)
        scratch = []
    else:
        def body(ins, outs, scr):
            kk = pl.program_id(2)
            acc_ref = scr[0]

            @pl.when(kk == 0)
            def _():
                acc_ref[...] = part(*ins)

            @pl.when(kk > 0)
            def _():
                acc_ref[...] += part(*ins)

            @pl.when(kk == nk - 1)
            def _():
                outs[0][...] = acc_ref[...].astype(out_dtype)
        scratch = [pltpu.VMEM((tm, tn), F32)]

    grid = (nj, ni, nk) if j_outer else (ni, nj, nk)
    (out,), carried = _carried_call(
        body, carry, grid=grid, in_specs=[a_spec, b_spec], out_specs=[o_spec],
        out_shape=[jax.ShapeDtypeStruct((m, n), out_dtype)], scratch=scratch, operands=(a, b), name=name)
    return (out, carried) if carry else out


def _rmsnorm_fwd(x, g, *, name):
    s, d = x.shape
    ts = min(512, s)

    def body(x_ref, g_ref, o_ref):
        xv = x_ref[...]
        r = lax.rsqrt(jnp.mean(xv * xv, axis=-1, keepdims=True) + EPS)
        o_ref[...] = ((xv * r) * g_ref[...]).astype(BF16)

    return pl.pallas_call(
        body, grid=(s // ts,),
        in_specs=[pl.BlockSpec((ts, d), lambda i: (i, 0)), pl.BlockSpec((1, d), lambda i: (0, 0))],
        out_specs=pl.BlockSpec((ts, d), lambda i: (i, 0)),
        out_shape=jax.ShapeDtypeStruct((s, d), BF16), name=name, compiler_params=_params())(x, g)


def _rmsnorm_bwd(dh, x, g, res, *, name, carry=None):
    s, d = x.shape
    ts = min(256, s)

    def body(ins, outs, scr):
        dh_ref, x_ref, g_ref, res_ref = ins
        dx_ref, dg_ref = outs
        xv = x_ref[...]
        r = lax.rsqrt(jnp.mean(xv * xv, axis=-1, keepdims=True) + EPS)
        xh = xv * r
        dhv = dh_ref[...]
        part = jnp.sum(dhv * xh, axis=0, keepdims=True)

        @pl.when(pl.program_id(0) == 0)
        def _():
            dg_ref[...] = part

        @pl.when(pl.program_id(0) > 0)
        def _():
            dg_ref[...] += part

        dxh = dhv * g_ref[...]
        dx_ref[...] = res_ref[...] + r * (dxh - xh * jnp.mean(dxh * xh, axis=-1, keepdims=True))

    row = pl.BlockSpec((ts, d), lambda i: (i, 0))
    vec = pl.BlockSpec((1, d), lambda i: (0, 0))
    outs, carried = _carried_call(
        body, carry, grid=(s // ts,), in_specs=[row, row, vec, row], out_specs=[row, vec],
        out_shape=[jax.ShapeDtypeStruct((s, d), F32), jax.ShapeDtypeStruct((1, d), F32)],
        scratch=[], operands=(dh, x, g, res), name=name)
    return (*outs, carried) if carry else tuple(outs)


MID_TILE = 256


def _gated_branches(y_refs, wup_ref, gl):
    d = D_MODEL
    us = [jnp.dot(y_refs[k][...], wup_ref[k], preferred_element_type=F32) for k in range(3)]
    sg = [_sigmoid(gl[:, k * d:(k + 1) * d]) for k in range(3)]
    return us, sg


def _mid_fwd(ya, yb, ym, proj, x, tgt, w_up, w_out, g_post):
    s, d = x.shape
    ts = MID_TILE

    def body(ya_ref, yb_ref, ym_ref, g_ref, x_ref, t_ref, wup_ref, wout_ref, gp_ref,
             m_ref, do_ref, dy_ref, dg_ref, loss_ref):
        us, sg = _gated_branches((ya_ref, yb_ref, ym_ref), wup_ref, g_ref[...])
        merged = (sg[0] * us[0] + sg[1] * us[1] + sg[2] * us[2]).astype(BF16)
        m_ref[...] = merged
        ov = jnp.dot(merged, wout_ref[...], preferred_element_type=F32)
        r = lax.rsqrt(jnp.mean(ov * ov, axis=-1, keepdims=True) + EPS)
        nh = ov * r
        gv = gp_ref[...]
        e = (x_ref[...] + nh * gv) - t_ref[...]
        lpart = 0.5 * jnp.sum(jnp.mean(e * e, axis=-1, keepdims=True), axis=0, keepdims=True)
        dy = e * (1.0 / d)
        dgp = jnp.sum(dy * nh, axis=0, keepdims=True)

        @pl.when(pl.program_id(0) == 0)
        def _():
            dg_ref[...] = dgp
            loss_ref[...] = jnp.broadcast_to(lpart, loss_ref.shape)

        @pl.when(pl.program_id(0) > 0)
        def _():
            dg_ref[...] += dgp
            loss_ref[...] += jnp.broadcast_to(lpart, loss_ref.shape)

        dn = dy * gv
        dy_ref[...] = dy
        do_ref[...] = (r * (dn - nh * jnp.mean(dn * nh, axis=-1, keepdims=True))).astype(BF16)

    row = pl.BlockSpec((ts, d), lambda i: (i, 0))
    ysp = pl.BlockSpec((ts, A_WIDTH), lambda i: (i, 0))
    vec = pl.BlockSpec((1, d), lambda i: (0, 0))
    return pl.pallas_call(
        body, grid=(s // ts,),
        in_specs=[ysp, ysp, ysp, pl.BlockSpec((ts, W_G), lambda i: (i, COL_G)), row, row,
                  pl.BlockSpec((3, A_WIDTH, d), lambda i: (0, 0, 0)), pl.BlockSpec((d, d), lambda i: (0, 0)), vec],
        out_specs=[row, row, row, vec, pl.BlockSpec((1, LANES), lambda i: (0, 0))],
        out_shape=[jax.ShapeDtypeStruct((s, d), BF16), jax.ShapeDtypeStruct((s, d), BF16),
                   jax.ShapeDtypeStruct((s, d), F32), jax.ShapeDtypeStruct((1, d), F32),
                   jax.ShapeDtypeStruct((1, LANES), F32)],
        name="mid_fwd", compiler_params=_params())(ya, yb, ym, proj, x, tgt, w_up, w_out, g_post)


def _mid_bwd(d_out, merged, ya, yb, ym, proj, w_up, w_out):
    s, d = merged.shape
    ts = MID_TILE
    last = s // ts - 1

    def body(do_ref, m_ref, ya_ref, yb_ref, ym_ref, g_ref, wup_ref, wout_ref,
             dp_ref, dya_ref, dyb_ref, dym_ref, dwup_hbm, dwout_hbm, dwup_acc, dwout_acc):
        i = pl.program_id(0)

        @pl.when(i == 0)
        def _():
            dwup_acc[...] = jnp.zeros_like(dwup_acc)
            dwout_acc[...] = jnp.zeros_like(dwout_acc)

        y_refs = (ya_ref, yb_ref, ym_ref)
        us, sg = _gated_branches(y_refs, wup_ref, g_ref[...])
        dov = do_ref[...]
        dwout_acc[...] += lax.dot_general(m_ref[...], dov, _DIMS["tn"], preferred_element_type=F32)
        dm = lax.dot_general(dov, wout_ref[...], _DIMS["nt"], preferred_element_type=F32)
        for k, dy_ref in enumerate((dya_ref, dyb_ref, dym_ref)):
            dp_ref[:, k * d:(k + 1) * d] = ((dm * us[k]) * (sg[k] * (1.0 - sg[k]))).astype(BF16)
            du = (sg[k] * dm).astype(BF16)
            dy_ref[...] = lax.dot_general(du, wup_ref[k], _DIMS["nt"], preferred_element_type=F32)
            dwup_acc[k] += lax.dot_general(y_refs[k][...], du, _DIMS["tn"], preferred_element_type=F32)

        @pl.when(i == last)
        def _():
            pltpu.sync_copy(dwup_acc, dwup_hbm)
            pltpu.sync_copy(dwout_acc, dwout_hbm)

    row = pl.BlockSpec((ts, d), lambda i: (i, 0))
    ysp = pl.BlockSpec((ts, A_WIDTH), lambda i: (i, 0))
    gsp = pl.BlockSpec((ts, W_G), lambda i: (i, COL_G))
    anysp = pl.BlockSpec(memory_space=pl.ANY)
    yshape = jax.ShapeDtypeStruct((s, A_WIDTH), F32)
    return pl.pallas_call(
        body, grid=(s // ts,),
        in_specs=[row, row, ysp, ysp, ysp, gsp, pl.BlockSpec((3, A_WIDTH, d), lambda i: (0, 0, 0)),
                  pl.BlockSpec((d, d), lambda i: (0, 0))],
        out_specs=[gsp, ysp, ysp, ysp, anysp, anysp],
        out_shape=[jax.ShapeDtypeStruct((s, IN_WIDTH), BF16), yshape, yshape, yshape,
                   jax.ShapeDtypeStruct((3, A_WIDTH, d), F32), jax.ShapeDtypeStruct((d, d), F32)],
        scratch_shapes=[pltpu.VMEM((3, A_WIDTH, d), F32), pltpu.VMEM((d, d), F32)],
        name="mid_bwd", compiler_params=_params())(d_out, merged, ya, yb, ym, proj, w_up, w_out)


def _conv_core(blk, prev, nxt, w, i, last, ts):
    c = A_WIDTH
    ab, ac, ax, az = blk[:, :c], blk[:, c:2 * c], blk[:, 2 * c:3 * c], blk[:, 3 * c:]
    cu = ac * ax
    cu_prev = (prev[7:8, c:2 * c] * prev[7:8, 2 * c:3 * c]) * jnp.where(i > 0, 1.0, 0.0)
    cu_next = (nxt[0:1, c:2 * c] * nxt[0:1, 2 * c:3 * c]) * jnp.where(i < last, 1.0, 0.0)
    row = lax.broadcasted_iota(jnp.int32, (ts, c), 0)
    cm1 = jnp.where(row == 0, cu_prev, pltpu.roll(cu, 1, 0))
    cp1 = jnp.where(row == ts - 1, cu_next, pltpu.roll(cu, ts - 1, 0))
    yc = cm1 * w[0:1] + cu * w[1:2] + cp1 * w[2:3]
    return ab, ac, ax, az, cu, cm1, cp1, yc, row


def _halo_specs(ts, width, col, nblk8):
    prev = pl.BlockSpec((8, width), lambda i: (jnp.maximum(i * (ts // 8) - 1, 0), col))
    nxt = pl.BlockSpec((8, width), lambda i: (jnp.minimum((i + 1) * (ts // 8), nblk8 - 1), col))
    return prev, nxt


def _conv_fwd(proj, w_conv):
    s = proj.shape[0]
    ts = 256
    last = s // ts - 1

    def body(a_ref, ap_ref, an_ref, w_ref, ya_ref):
        i = pl.program_id(0)
        ab, _, _, az, _, _, _, yc, _ = _conv_core(a_ref[...], ap_ref[...], an_ref[...], w_ref[...], i, last, ts)
        ya_ref[...] = ((ab * yc) * (az * _sigmoid(az))).astype(BF16)

    prev, nxt = _halo_specs(ts, W_A, COL_A, s // 8)
    return pl.pallas_call(
        body, grid=(s // ts,),
        in_specs=[pl.BlockSpec((ts, W_A), lambda i: (i, COL_A)), prev, nxt,
                  pl.BlockSpec((3, A_WIDTH), lambda i: (0, 0))],
        out_specs=pl.BlockSpec((ts, A_WIDTH), lambda i: (i, 0)),
        out_shape=jax.ShapeDtypeStruct((s, A_WIDTH), BF16), name="conv_fwd",
        compiler_params=_params())(proj, proj, proj, w_conv)


def _conv_bwd(proj, w_conv, dya, dproj):
    s = proj.shape[0]
    ts = 256
    last = s // ts - 1
    c = A_WIDTH

    def body(a_ref, ap_ref, an_ref, w_ref, d_ref, dp_ref, dn_ref, _, dproj_ref, dw_ref):
        i = pl.program_id(0)
        w = w_ref[...]
        prev, nxt = ap_ref[...], an_ref[...]
        ab, ac, ax, az, cu, cm1, cp1, yc, row = _conv_core(a_ref[...], prev, nxt, w, i, last, ts)
        sg = _sigmoid(az)
        sz = az * sg
        dya_v = d_ref[...]
        dyc = dya_v * sz * ab
        dproj_ref[:, :c] = (dya_v * sz * yc).astype(BF16)
        dproj_ref[:, 3 * c:] = (dya_v * (ab * yc) * (sg * (1.0 + az * (1.0 - sg)))).astype(BF16)

        def halo_dyc(a_row, d_row):
            azr = a_row[:, 3 * c:]
            return d_row * (azr * _sigmoid(azr)) * a_row[:, :c]

        dyc_prev = halo_dyc(prev[7:8], dp_ref[...][7:8]) * jnp.where(i > 0, 1.0, 0.0)
        dyc_next = halo_dyc(nxt[0:1], dn_ref[...][0:1]) * jnp.where(i < last, 1.0, 0.0)
        dyc_m1 = jnp.where(row == 0, dyc_prev, pltpu.roll(dyc, 1, 0))
        dyc_p1 = jnp.where(row == ts - 1, dyc_next, pltpu.roll(dyc, ts - 1, 0))
        dcu = dyc_p1 * w[0:1] + dyc * w[1:2] + dyc_m1 * w[2:3]
        dproj_ref[:, c:2 * c] = (dcu * ax).astype(BF16)
        dproj_ref[:, 2 * c:3 * c] = (dcu * ac).astype(BF16)
        dw = [jnp.sum(dyc * t, axis=0, keepdims=True) for t in (cm1, cu, cp1)]

        @pl.when(i == 0)
        def _():
            for k in range(3):
                dw_ref[k:k + 1, :] = dw[k]

        @pl.when(i > 0)
        def _():
            for k in range(3):
                dw_ref[k:k + 1, :] += dw[k]

    prev, nxt = _halo_specs(ts, W_A, COL_A, s // 8)
    dprev, dnxt = _halo_specs(ts, A_WIDTH, 0, s // 8)
    return pl.pallas_call(
        body, grid=(s // ts,),
        in_specs=[pl.BlockSpec((ts, W_A), lambda i: (i, COL_A)), prev, nxt,
                  pl.BlockSpec((3, A_WIDTH), lambda i: (0, 0)),
                  pl.BlockSpec((ts, A_WIDTH), lambda i: (i, 0)), dprev, dnxt,
                  pl.BlockSpec(memory_space=pl.ANY)],
        out_specs=[pl.BlockSpec((ts, W_A), lambda i: (i, COL_A)), pl.BlockSpec((3, A_WIDTH), lambda i: (0, 0))],
        out_shape=[jax.ShapeDtypeStruct(dproj.shape, BF16), jax.ShapeDtypeStruct((3, A_WIDTH), F32)],
        input_output_aliases={7: 0}, name="conv_bwd",
        compiler_params=_params())(proj, proj, proj, w_conv, dya, dya, dya, dproj)


def _rope_tables(s):
    half = ROT_DIM // 2
    inv_freq = jnp.power(jnp.float32(ROPE_THETA), -jnp.arange(half, dtype=F32) * (2.0 / ROT_DIM))
    ang = jnp.arange(s).astype(F32)[:, None] * inv_freq[None, :]
    cos, sin = jnp.cos(ang), jnp.sin(ang)
    pad = jnp.zeros((s, HEAD_DIM - ROT_DIM), F32)
    c = jnp.concatenate([cos, cos, pad + 1.0], axis=1)
    s1 = jnp.concatenate([-sin, jnp.zeros_like(sin), pad], axis=1)
    s2 = jnp.concatenate([jnp.zeros_like(sin), sin, pad], axis=1)
    return jnp.concatenate([c, c, s1, s1, s2, s2], axis=1)


def _rope(t, tab):
    return (t * tab[:, :LANES] + pltpu.roll(t, LANES - 8, 1) * tab[:, LANES:2 * LANES]
            + pltpu.roll(t, 8, 1) * tab[:, 2 * LANES:])


def _rope_transpose(dt, tab):
    return (dt * tab[:, :LANES] + pltpu.roll(dt * tab[:, LANES:2 * LANES], 8, 1)
            + pltpu.roll(dt * tab[:, 2 * LANES:], LANES - 8, 1))


def _rope_kv(proj, tab):
    s = proj.shape[0]
    nb = s // KV_PAD

    def body(kv_ref, t_ref, k_ref, v_ref):
        j = pl.program_id(0)
        inside = jnp.where((j > 0) & (j <= nb), 1.0, 0.0)
        kv = kv_ref[...]
        k_ref[...] = (_rope(kv[:, :LANES], t_ref[...]) * inside).astype(BF16)
        v_ref[...] = (kv[:, LANES:] * inside).astype(BF16)

    def src(j):
        return jnp.clip(j - 1, 0, nb - 1)

    o_spec = pl.BlockSpec((KV_PAD, LANES), lambda j: (j, 0))
    shp = jax.ShapeDtypeStruct((s + 2 * KV_PAD, LANES), BF16)
    return pl.pallas_call(
        body, grid=(nb + 2,),
        in_specs=[pl.BlockSpec((KV_PAD, W_KV), lambda j: (src(j), COL_KV)),
                  pl.BlockSpec((KV_PAD, 3 * LANES), lambda j: (src(j), 0))],
        out_specs=[o_spec, o_spec], out_shape=[shp, shp], name="rope_kv",
        compiler_params=_params())(proj, tab)


def _rope_kv_bwd(dkpad, dvpad, tab, dproj):
    s = tab.shape[0]
    nb = s // KV_PAD

    def body(dk_ref, dv_ref, t_ref, _, dp_ref):
        dp_ref[:, :LANES] = _rope_transpose(dk_ref[...], t_ref[...]).astype(BF16)
        dp_ref[:, LANES:] = dv_ref[...].astype(BF16)

    pad_spec = pl.BlockSpec((KV_PAD, LANES), lambda j: (j + 1, 0))
    return pl.pallas_call(
        body, grid=(nb,),
        in_specs=[pad_spec, pad_spec, pl.BlockSpec((KV_PAD, 3 * LANES), lambda j: (j, 0)),
                  pl.BlockSpec(memory_space=pl.ANY)],
        out_specs=pl.BlockSpec((KV_PAD, W_KV), lambda j: (j, COL_KV)),
        out_shape=jax.ShapeDtypeStruct(dproj.shape, BF16), input_output_aliases={3: 0},
        name="rope_kv_bwd", compiler_params=_params())(dkpad, dvpad, tab, dproj)


def _window_start(n):
    return pl.multiple_of((n - 1) * WINDOW_BLOCK + KV_PAD, WINDOW_BLOCK)


def _window_operands(k_ref, v_ref, n, lo):
    start = _window_start(n)
    kw = k_ref[pl.ds(start, 3 * WINDOW_BLOCK), :].astype(F32)
    vw = v_ref[pl.ds(start, 3 * WINDOW_BLOCK), :].astype(F32)
    kr, vr = pltpu.roll(kw, HALF_LANES, 1), pltpu.roll(vw, HALF_LANES, 1)
    k2 = (jnp.where(lo, kw, kr).astype(BF16), jnp.where(lo, kr, kw).astype(BF16))
    v2 = (jnp.where(lo, vw, vr).astype(BF16), jnp.where(lo, vr, vw).astype(BF16))
    return k2, v2


HEADS_PER_GROUP = 4
SWA_FWD_BLOCKS = 1
SWA_BWD_BLOCKS = 2


def _window_mask(n, s):
    wb = WINDOW_BLOCK
    shape = (HEADS_PER_GROUP * wb, 3 * wb)
    qi = lax.broadcasted_iota(jnp.int32, shape, 0) & (wb - 1)
    kj = lax.broadcasted_iota(jnp.int32, shape, 1)
    kpos = kj + (n - 1) * wb
    return (kj >= qi) & (kj <= qi + 2 * wb) & (kpos >= 0) & (kpos < s)


def _stack_heads(pair0, pair1, lo):
    return jnp.concatenate([jnp.where(lo, pair0, 0.0), jnp.where(lo, 0.0, pair0),
                            jnp.where(lo, pair1, 0.0), jnp.where(lo, 0.0, pair1)], axis=0)


def _unstack_pair(stacked, i, lo):
    wb = WINDOW_BLOCK
    return jnp.where(lo, stacked[2 * i * wb:(2 * i + 1) * wb], stacked[(2 * i + 1) * wb:(2 * i + 2) * wb])


def _sink_column(sink_ref, g):
    wb = WINDOW_BLOCK
    return jnp.concatenate([jnp.full((wb, 1), sink_ref[0, HEADS_PER_GROUP * g + i], F32)
                            for i in range(HEADS_PER_GROUP)], axis=0)


def _head_exp(q4, k2g, valid, sink):
    sc = lax.dot_general(q4, k2g, _DIMS["nt"], preferred_element_type=F32) * (HEAD_DIM ** -0.5)
    sc = jnp.where(valid, sc, -jnp.inf)
    m = jnp.maximum(jnp.max(sc, axis=1, keepdims=True), sink)
    return jnp.exp(sc - m).astype(BF16), jnp.exp(sink - m)


def _swa_fwd(proj, kpad, vpad, tab, sink):
    s = proj.shape[0]
    wb = WINDOW_BLOCK

    def body(b_ref, k_ref, v_ref, t_ref, sink_ref, o_ref, y_ref):
        lo = lax.broadcasted_iota(jnp.int32, (wb, LANES), 1) < HALF_LANES
        lo_w = lax.broadcasted_iota(jnp.int32, (3 * wb, LANES), 1) < HALF_LANES
        for sub in range(SWA_FWD_BLOCKS):
            n = pl.program_id(0) * SWA_FWD_BLOCKS + sub
            rows = slice(sub * wb, (sub + 1) * wb)
            k2, v2 = _window_operands(k_ref, v_ref, n, lo_w)
            valid = _window_mask(n, s)
            tab_v = t_ref[rows, :]
            ones = jnp.ones((3 * wb, LANES), BF16)
            for g in range(2):
                qr = [_rope(b_ref[rows, (2 * g + i) * LANES:(2 * g + i + 1) * LANES], tab_v) for i in range(2)]
                q4 = _stack_heads(qr[0], qr[1], lo).astype(BF16)
                e, es = _head_exp(q4, k2[g], valid, _sink_column(sink_ref, g))
                ox = jnp.dot(e, jnp.concatenate([v2[g], ones], axis=1), preferred_element_type=F32)
                o4 = ox[:, :LANES] * (1.0 / (ox[:, LANES:] + es))
                for i in range(2):
                    cols = slice((2 * g + i) * LANES, (2 * g + i + 1) * LANES)
                    op = _unstack_pair(o4, i, lo)
                    o_ref[rows, cols] = op
                    zp = b_ref[rows, A_WIDTH + cols.start:A_WIDTH + cols.stop]
                    y_ref[rows, cols] = (op * (zp * _sigmoid(zp))).astype(BF16)

    tq = SWA_FWD_BLOCKS * wb
    pad_spec = pl.BlockSpec((s + 2 * KV_PAD, LANES), lambda n: (0, 0))
    o_spec = pl.BlockSpec((tq, A_WIDTH), lambda n: (n, 0))
    return pl.pallas_call(
        body, grid=(s // tq,),
        in_specs=[pl.BlockSpec((tq, W_B), lambda n: (n, COL_B)), pad_spec, pad_spec,
                  pl.BlockSpec((tq, 3 * LANES), lambda n: (n, 0)),
                  pl.BlockSpec(memory_space=pltpu.SMEM)],
        out_specs=[o_spec, o_spec],
        out_shape=[jax.ShapeDtypeStruct((s, A_WIDTH), F32), jax.ShapeDtypeStruct((s, A_WIDTH), BF16)],
        name="swa_fwd", compiler_params=_params())(proj, kpad, vpad, tab, sink)


def _swa_bwd(proj, kpad, vpad, tab, sink, o_attn, dyb, dproj):
    s = proj.shape[0]
    wb = WINDOW_BLOCK
    scale = HEAD_DIM ** -0.5

    def body(b_ref, k_ref, v_ref, t_ref, sink_ref, o_ref, dy_ref, _, dp_ref, dk_ref, dv_ref, ds_ref):
        @pl.when(pl.program_id(0) == 0)
        def _():
            dk_ref[...] = jnp.zeros_like(dk_ref)
            dv_ref[...] = jnp.zeros_like(dv_ref)
            ds_ref[...] = jnp.zeros_like(ds_ref)

        lo = lax.broadcasted_iota(jnp.int32, (wb, LANES), 1) < HALF_LANES
        lo_w = lax.broadcasted_iota(jnp.int32, (3 * wb, LANES), 1) < HALF_LANES
        for sub in range(SWA_BWD_BLOCKS):
            n = pl.program_id(0) * SWA_BWD_BLOCKS + sub
            rows = slice(sub * wb, (sub + 1) * wb)
            k2, v2 = _window_operands(k_ref, v_ref, n, lo_w)
            valid = _window_mask(n, s)
            tab_v = t_ref[rows, :]
            ones = jnp.ones((3 * wb, LANES), BF16)
            dks, dvs = [], []
            for g in range(2):
                qr, op, do = [], [], []
                for i in range(2):
                    cols = slice((2 * g + i) * LANES, (2 * g + i + 1) * LANES)
                    zcols = slice(A_WIDTH + cols.start, A_WIDTH + cols.stop)
                    qr.append(_rope(b_ref[rows, cols], tab_v))
                    zp = b_ref[rows, zcols]
                    sg = _sigmoid(zp)
                    op.append(o_ref[rows, cols])
                    dyp = dy_ref[rows, cols]
                    do.append(dyp * (zp * sg))
                    dp_ref[rows, zcols] = (dyp * op[i] * (sg * (1.0 + zp * (1.0 - sg)))).astype(BF16)
                q4 = _stack_heads(qr[0], qr[1], lo).astype(BF16)
                do4 = _stack_heads(do[0], do[1], lo)
                o4 = jnp.concatenate([op[0], op[0], op[1], op[1]], axis=0)
                e, es = _head_exp(q4, k2[g], valid, _sink_column(sink_ref, g))
                inv = 1.0 / (jnp.dot(e, ones, preferred_element_type=F32) + es)
                prob = e.astype(F32) * jnp.concatenate([inv, inv, inv], axis=1)
                delta = jnp.sum(do4 * o4, axis=1, keepdims=True)
                do4b = do4.astype(BF16)
                dprob = lax.dot_general(do4b, v2[g], _DIMS["nt"], preferred_element_type=F32)
                dsc = (prob * (dprob - delta)).astype(BF16)
                sink_terms = (es * inv[:, :1]) * delta
                for i in range(HEADS_PER_GROUP):
                    h = HEADS_PER_GROUP * g + i
                    dsink = -jnp.sum(sink_terms[i * wb:(i + 1) * wb], axis=0, keepdims=True)
                    ds_ref[h:h + 1, :] += jnp.broadcast_to(dsink, (1, LANES))
                dq4 = jnp.dot(dsc, k2[g], preferred_element_type=F32) * scale
                for i in range(2):
                    cols = slice((2 * g + i) * LANES, (2 * g + i + 1) * LANES)
                    dp_ref[rows, cols] = _rope_transpose(_unstack_pair(dq4, i, lo), tab_v).astype(BF16)
                dk2 = lax.dot_general(dsc, q4, _DIMS["tn"], preferred_element_type=F32) * scale
                dv2 = lax.dot_general(prob.astype(BF16), do4b, _DIMS["tn"], preferred_element_type=F32)
                dks.append(dk2 + pltpu.roll(dk2, HALF_LANES, 1))
                dvs.append(dv2 + pltpu.roll(dv2, HALF_LANES, 1))
            start = _window_start(n)
            dk_ref[pl.ds(start, 3 * wb), :] += jnp.where(lo_w, dks[0], dks[1])
            dv_ref[pl.ds(start, 3 * wb), :] += jnp.where(lo_w, dvs[0], dvs[1])

    tq = SWA_BWD_BLOCKS * wb
    pad_spec = pl.BlockSpec((s + 2 * KV_PAD, LANES), lambda n: (0, 0))
    blk = pl.BlockSpec((tq, A_WIDTH), lambda n: (n, 0))
    bsp = pl.BlockSpec((tq, W_B), lambda n: (n, COL_B))
    pad_shape = jax.ShapeDtypeStruct((s + 2 * KV_PAD, LANES), F32)
    return pl.pallas_call(
        body, grid=(s // tq,),
        in_specs=[bsp, pad_spec, pad_spec, pl.BlockSpec((tq, 3 * LANES), lambda n: (n, 0)),
                  pl.BlockSpec(memory_space=pltpu.SMEM), blk, blk, pl.BlockSpec(memory_space=pl.ANY)],
        out_specs=[bsp, pad_spec, pad_spec, pl.BlockSpec((8, LANES), lambda n: (0, 0))],
        out_shape=[jax.ShapeDtypeStruct(dproj.shape, BF16), pad_shape, pad_shape,
                   jax.ShapeDtypeStruct((8, LANES), F32)],
        input_output_aliases={7: 0}, name="swa_bwd",
        compiler_params=_params())(proj, kpad, vpad, tab, sink, o_attn, dyb, dproj)


def _mem_exp(qh, mk):
    sc = lax.dot_general(qh, mk, _DIMS["nt"], preferred_element_type=F32) * (MEM_HEAD_DIM ** -0.5)
    return jnp.exp(sc - jnp.max(sc, axis=1, keepdims=True)).astype(BF16)


def _mem_fwd(proj, mkv):
    s = proj.shape[0]
    ts = 512
    mlen = mkv.shape[0]

    def body(m_ref, kv_ref, o_ref, y_ref):
        ones = jnp.ones((mlen, LANES), BF16)
        for h in range(MEM_HEADS):
            cols = slice(h * LANES, (h + 1) * LANES)
            mk = kv_ref[:, cols].astype(BF16)
            mv = kv_ref[:, MEM_WIDTH + h * LANES:MEM_WIDTH + (h + 1) * LANES].astype(BF16)
            e = _mem_exp(m_ref[:, cols].astype(BF16), mk)
            ox = jnp.dot(e, jnp.concatenate([mv, ones], axis=1), preferred_element_type=F32)
            oh = ox[:, :LANES] * (1.0 / ox[:, LANES:])
            o_ref[:, cols] = oh
            zh = m_ref[:, MEM_WIDTH + h * LANES:MEM_WIDTH + (h + 1) * LANES]
            y_ref[:, cols] = (oh * (zh * _sigmoid(zh))).astype(BF16)

    o_spec = pl.BlockSpec((ts, MEM_WIDTH), lambda i: (i, 0))
    return pl.pallas_call(
        body, grid=(s // ts,),
        in_specs=[pl.BlockSpec((ts, W_M), lambda i: (i, COL_M)),
                  pl.BlockSpec((mlen, 2 * MEM_WIDTH), lambda i: (0, 0))],
        out_specs=[o_spec, o_spec],
        out_shape=[jax.ShapeDtypeStruct((s, MEM_WIDTH), F32), jax.ShapeDtypeStruct((s, MEM_WIDTH), BF16)],
        name="mem_fwd", compiler_params=_params())(proj, mkv)


def _mem_bwd(proj, mkv, o_mem, dym, dproj):
    s = proj.shape[0]
    ts = 512
    mlen = mkv.shape[0]
    scale = MEM_HEAD_DIM ** -0.5

    def body(m_ref, kv_ref, o_ref, dy_ref, _, dp_ref, dkv_ref):
        @pl.when(pl.program_id(0) == 0)
        def _():
            dkv_ref[...] = jnp.zeros_like(dkv_ref)

        ones = jnp.ones((mlen, LANES), BF16)
        for h in range(MEM_HEADS):
            cols = slice(h * LANES, (h + 1) * LANES)
            vcols = slice(MEM_WIDTH + h * LANES, MEM_WIDTH + (h + 1) * LANES)
            mk = kv_ref[:, cols].astype(BF16)
            mv = kv_ref[:, vcols].astype(BF16)
            qh = m_ref[:, cols].astype(BF16)
            zh = m_ref[:, vcols]
            sg = _sigmoid(zh)
            oh = o_ref[:, cols]
            dyh = dy_ref[:, cols]
            doh = dyh * (zh * sg)
            dp_ref[:, vcols] = (dyh * oh * (sg * (1.0 + zh * (1.0 - sg)))).astype(BF16)
            e = _mem_exp(qh, mk)
            inv = 1.0 / jnp.dot(e, ones, preferred_element_type=F32)
            prob = e.astype(F32) * jnp.concatenate([inv] * (mlen // LANES), axis=1)
            delta = jnp.sum(doh * oh, axis=1, keepdims=True)
            dohb = doh.astype(BF16)
            dprob = lax.dot_general(dohb, mv, _DIMS["nt"], preferred_element_type=F32)
            dsc = (prob * (dprob - delta)).astype(BF16)
            dp_ref[:, cols] = (jnp.dot(dsc, mk, preferred_element_type=F32) * scale).astype(BF16)
            dkv_ref[:, cols] += lax.dot_general(dsc, qh, _DIMS["tn"], preferred_element_type=F32) * scale
            dkv_ref[:, vcols] += lax.dot_general(prob.astype(BF16), dohb, _DIMS["tn"],
                                                 preferred_element_type=F32)

    blk = pl.BlockSpec((ts, MEM_WIDTH), lambda i: (i, 0))
    msp = pl.BlockSpec((ts, W_M), lambda i: (i, COL_M))
    kvsp = pl.BlockSpec((mlen, 2 * MEM_WIDTH), lambda i: (0, 0))
    return pl.pallas_call(
        body, grid=(s // ts,),
        in_specs=[msp, kvsp, blk, blk, pl.BlockSpec(memory_space=pl.ANY)],
        out_specs=[msp, kvsp],
        out_shape=[jax.ShapeDtypeStruct(dproj.shape, BF16), jax.ShapeDtypeStruct(mkv.shape, F32)],
        input_output_aliases={4: 0}, name="mem_bwd",
        compiler_params=_params())(proj, mkv, o_mem, dym, dproj)


def _forward_backward(x, mem, tgt, proj, w_conv, sink, g_mem, w_kv, w_up, w_out, g_post):
    s = x.shape[0]
    tab = _rope_tables(s)

    ya = _conv_fwd(proj, w_conv)
    kpad, vpad = _rope_kv(proj, tab)
    o_attn, yb = _swa_fwd(proj, kpad, vpad, tab, sink)
    mn = _rmsnorm_fwd(mem, g_mem, name="mem_norm")
    mkv = _matmul(mn, w_kv, mode="nn", out_dtype=F32, tm=256, tn=1024, tk=D_MODEL, name="mem_kv")
    o_mem, ym = _mem_fwd(proj, mkv)
    merged, d_out, dy, dg_post, loss = _mid_fwd(ya, yb, ym, proj, x, tgt, w_up, w_out, g_post)
    dproj, d_ya, d_yb, d_ym, dw_up, dw_out = _mid_bwd(d_out, merged, ya, yb, ym, proj, w_up, w_out)

    dproj, dw_conv = _conv_bwd(proj, w_conv, d_ya, dproj)
    dproj, dkpad, dvpad, dsink = _swa_bwd(proj, kpad, vpad, tab, sink, o_attn, d_yb, dproj)
    dproj = _rope_kv_bwd(dkpad, dvpad, tab, dproj)
    dproj, d_mkv = _mem_bwd(proj, mkv, o_mem, d_ym, dproj)

    dw_kv = _matmul(mn, d_mkv, mode="tn", out_dtype=F32, tm=1024, tn=1024, tk=256, name="dw_kv")
    d_mn = _matmul(d_mkv, w_kv, mode="nt", out_dtype=F32, tm=256, tn=1024, tk=D_MODEL, name="d_mn")
    _, dg_mem = _rmsnorm_bwd(d_mn, mem, g_mem, d_mn, name="mem_norm_bwd")

    return dict(loss=loss, dproj=dproj, dy=dy, w_conv=dw_conv, sink=dsink, g_mem=dg_mem,
                w_kv=dw_kv, w_up=dw_up, w_out=dw_out, g_post=dg_post)


N_DEV = 8


def _position():
    return lax.axis_index("x"), lax.axis_index("y"), lax.axis_index("c")


def _other_chips(x, y):
    return (((1 - x, y), 2 * (1 - x) + y), ((x, 1 - y), 2 * x + (1 - y)), ((1 - x, 1 - y), 2 * (1 - x) + (1 - y)))


def _remote(src, dst, send_sems, recv_sems, k, device):
    return pltpu.make_async_remote_copy(src_ref=src, dst_ref=dst, send_sem=send_sems.at[k], recv_sem=recv_sems.at[k],
                                        device_id=device, device_id_type=MESH)


def _rows_half(ref, hf):
    rh = ref.shape[0] // 2
    return ref.at[pl.ds(pl.multiple_of(hf * rh, 8), rh)]


def _gather_weights(shards, small=None):
    n = len(shards)
    k = 0 if small is None else 1

    def ici(ins, outs, sems, a, r, chip, src_chip, c):
        return _remote(_rows_half(ins[a], c), _rows_half(outs[a].at[src_chip], c), sems[0], sems[1], 3 * a + r,
                       (*chip, c))

    def whole(ins, outs, sems, r, chip, src_chip, c):
        return _remote(ins[n], outs[n].at[src_chip], sems[0], sems[1], 3 * n + r, (*chip, c))

    def d2d(outs, sems, a, r, idx, hf, x, y, c):
        half = _rows_half(outs[a].at[idx], hf)
        return _remote(half, half, sems[2], sems[3], 3 * a + r, (x, y, 1 - c))

    def start(ins, outs, sems):
        x, y, c = _position()
        me = 2 * x + y
        for a in range(n):
            for r, (chip, _) in enumerate(_other_chips(x, y)):
                ici(ins, outs, sems, a, r, chip, me, c).start()
        for r, (chip, _) in enumerate(_other_chips(x, y)):
            if k:
                whole(ins, outs, sems, r, chip, me, c).start()

    def finish(ins, outs, sems):
        x, y, c = _position()
        me = 2 * x + y
        chips = _other_chips(x, y)
        for a in range(n):
            for r, (chip, idx) in enumerate(chips):
                ici(ins, outs, sems, a, r, chip, idx, c).wait_recv()
                d2d(outs, sems, a, r, idx, c, x, y, c).start()
        for a in range(n):
            for r, (chip, idx) in enumerate(chips):
                d2d(outs, sems, a, r, idx, 1 - c, x, y, c).wait_recv()
        for r, (chip, idx) in enumerate(chips):
            if k:
                whole(ins, outs, sems, r, chip, idx, c).wait_recv()
                whole(ins, outs, sems, r, chip, me, c).wait_send()
        for a in range(n):
            for r, (chip, idx) in enumerate(chips):
                ici(ins, outs, sems, a, r, chip, me, c).wait_send()
                d2d(outs, sems, a, r, idx, c, x, y, c).wait_send()

    operands = list(shards) + ([small] if k else [])
    return _Carry(operands, [jax.ShapeDtypeStruct((N_CHIPS,) + s.shape, s.dtype) for s in operands],
                  [pltpu.SemaphoreType.DMA((3 * (n + k),)), pltpu.SemaphoreType.DMA((3 * (n + k),)),
                   pltpu.SemaphoreType.DMA((3 * n,)), pltpu.SemaphoreType.DMA((3 * n,))], start, finish)


def _run_carry(carry, name):
    _, results = _carried_call(lambda ins, outs, scr: None, carry, grid=(1,), in_specs=[], out_specs=[],
                               out_shape=[], scratch=[], operands=(), name=name)
    return results


def _pair_exchange(send):
    n = len(send)

    def copies(ins, outs, sems):
        x, y, c = _position()
        return [_remote(ins[a], outs[a], sems[0], sems[1], a, (x, y, 1 - c)) for a in range(n)]

    def start(ins, outs, sems):
        for cp in copies(ins, outs, sems):
            cp.start()

    def finish(ins, outs, sems):
        for cp in copies(ins, outs, sems):
            cp.wait()

    return _Carry(send, [jax.ShapeDtypeStruct(p.shape, p.dtype) for p in send],
                  [pltpu.SemaphoreType.DMA((n,)), pltpu.SemaphoreType.DMA((n,))], start, finish)


def _chip_exchange(sums):
    n = len(sums)

    def copies(ins, outs, sems):
        x, y, c = _position()
        return [_remote(ins[a].at[idx], outs[a].at[r], sems[0], sems[1], 3 * a + r, (*chip, c))
                for a in range(n) for r, (chip, idx) in enumerate(_other_chips(x, y))]

    def start(ins, outs, sems):
        for cp in copies(ins, outs, sems):
            cp.start()

    def finish(ins, outs, sems):
        for cp in copies(ins, outs, sems):
            cp.wait()

    return _Carry(sums, [jax.ShapeDtypeStruct((3,) + p.shape[1:], p.dtype) for p in sums],
                  [pltpu.SemaphoreType.DMA((3 * n,)), pltpu.SemaphoreType.DMA((3 * n,))], start, finish)


def _pair_share(pairs):
    n = len(pairs)

    def start(ins, outs, sems):
        x, y, c = _position()
        for a in range(n):
            _remote(outs[a].at[c], outs[a].at[c], sems[0], sems[1], a, (x, y, 1 - c)).start()

    def finish(ins, outs, sems):
        x, y, c = _position()
        for a in range(n):
            _remote(outs[a].at[1 - c], outs[a].at[1 - c], sems[0], sems[1], a, (x, y, 1 - c)).wait_recv()
        for a in range(n):
            _remote(outs[a].at[c], outs[a].at[c], sems[0], sems[1], a, (x, y, 1 - c)).wait_send()

    return _Carry(pairs, [jax.ShapeDtypeStruct(p.shape, p.dtype) for p in pairs],
                  [pltpu.SemaphoreType.DMA((n,)), pltpu.SemaphoreType.DMA((n,))], start, finish,
                  aliases={a: a for a in range(n)})


def _small_allreduce(pack, share):
    rows, width = pack.shape
    n_share = len(share.ins)

    def body(p_ref, *refs):
        share_in, o_ref, share_out = refs[:n_share], refs[n_share], refs[n_share + 1:2 * n_share + 1]
        buf, send_sems, recv_sems = refs[2 * n_share + 1:2 * n_share + 4]
        share_sems = refs[2 * n_share + 4:]
        share.start(share_in, share_out, share_sems)
        x, y, c = _position()
        me = 4 * x + 2 * y + c
        buf[me] = p_ref[...]
        peers = []
        for r in range(1, N_DEV):
            fx, fy, fc = (r >> 2) & 1, (r >> 1) & 1, r & 1
            px, py, pc = (1 - x if fx else x), (1 - y if fy else y), (1 - c if fc else c)
            peers.append(((px, py, pc), 4 * px + 2 * py + pc))
        sends = [_remote(p_ref, buf.at[me], send_sems, recv_sems, r, dev) for r, (dev, _) in enumerate(peers)]
        for cp in sends:
            cp.start()
        for r, (dev, idx) in enumerate(peers):
            _remote(p_ref, buf.at[idx], send_sems, recv_sems, r, dev).wait_recv()
        for cp in sends:
            cp.wait_send()
        acc = buf[0]
        for k in range(1, N_DEV):
            acc = acc + buf[k]
        o_ref[...] = acc
        share.finish(share_in, share_out, share_sems)

    vm = pl.BlockSpec(memory_space=pltpu.VMEM)
    red, *shared = pl.pallas_call(
        body, in_specs=[vm] + [_HBM] * n_share, out_specs=[vm] + [_HBM] * n_share,
        out_shape=[jax.ShapeDtypeStruct(pack.shape, F32)] + share.out_shapes,
        scratch_shapes=[pltpu.VMEM((N_DEV, rows, width), F32), pltpu.SemaphoreType.DMA((N_DEV - 1,)),
                        pltpu.SemaphoreType.DMA((N_DEV - 1,))] + share.sems,
        input_output_aliases={1 + i: 1 + o for i, o in share.aliases.items()},
        name="small_allreduce")(pack, *share.ins)
    return red, shared


def _row_tile(rows):
    return rows if rows <= 256 else 256


def _pair_add(keep, recv, name):
    nj, rh, cols = keep.shape
    tr = _row_tile(rh)

    def body(k_ref, r_ref, o_ref):
        o_ref[...] = (k_ref[...].astype(F32) + r_ref[...].astype(F32)).astype(BF16)

    blk = pl.BlockSpec((None, tr, cols), lambda j, i: (j, i, 0))
    return pl.pallas_call(body, grid=(nj, rh // tr), in_specs=[blk, blk], out_specs=blk,
                          out_shape=jax.ShapeDtypeStruct(keep.shape, BF16), name=name,
                          compiler_params=_params())(keep, recv)


def _chip_add(sums, recv, where, name):
    _, rh, cols = sums.shape
    tr = _row_tile(rh)

    def body(w_ref, s_ref, r_ref, o_ref):
        o_ref[...] = ((s_ref[...].astype(F32) + r_ref[0].astype(F32)) + r_ref[1].astype(F32)) + r_ref[2].astype(F32)

    grid_spec = pltpu.PrefetchScalarGridSpec(
        num_scalar_prefetch=1, grid=(rh // tr,),
        in_specs=[pl.BlockSpec((None, tr, cols), lambda i, w_ref: (w_ref[0], i, 0)),
                  pl.BlockSpec((3, tr, cols), lambda i, w_ref: (0, i, 0))],
        out_specs=pl.BlockSpec((None, tr, cols), lambda i, w_ref: (w_ref[1], i, 0)))
    return pl.pallas_call(body, grid_spec=grid_spec, out_shape=jax.ShapeDtypeStruct((2, rh, cols), F32),
                          name=name, compiler_params=_params())(where, sums, recv)


def _adamw(w, g, m, v, name):
    rows, cols = w.shape
    tr = _row_tile(rows)
    assert rows % tr == 0

    def body(w_ref, g_ref, m_ref, v_ref, d_ref, mo_ref, vo_ref):
        gv = g_ref[...]
        m_new = ADAM_B1 * m_ref[...] + (1.0 - ADAM_B1) * gv
        v_new = ADAM_B2 * v_ref[...] + (1.0 - ADAM_B2) * jnp.square(gv)
        m_hat = m_new / (1.0 - ADAM_B1 ** ADAM_STEP)
        v_hat = v_new / (1.0 - ADAM_B2 ** ADAM_STEP)
        d_ref[...] = -ADAM_LR * (m_hat / (jnp.sqrt(v_hat) + ADAM_EPS) + ADAM_WD * w_ref[...])
        mo_ref[...] = m_new
        vo_ref[...] = v_new

    blk = pl.BlockSpec((tr, cols), lambda i: (i, 0))
    shp = jax.ShapeDtypeStruct((rows, cols), F32)
    return pl.pallas_call(body, grid=(rows // tr,), in_specs=[blk] * 4, out_specs=[blk] * 3,
                          out_shape=[shp] * 3, name=name, compiler_params=_params())(w, g, m, v)


SHARD_W = IN_WIDTH // N_CHIPS


def _half_major(a):
    r, c = a.shape
    return a.reshape(N_CHIPS, 2, r // N_CHIPS // 2, c).transpose(1, 0, 2, 3)


def _w_in_permuted(pieces):
    cols = []
    for a, b in PERM_SEGS:
        pos = a
        while pos < b:
            j = pos // SHARD_W
            stop = min(b, (j + 1) * SHARD_W)
            cols.append(pieces[j][:, pos - j * SHARD_W:stop - j * SHARD_W])
            pos = stop
    return jnp.concatenate(cols, axis=1)


def _ref_cols(a_p, lo, hi):
    out, ref_off = [], 0
    for a, b in UNPERM_SEGS:
        r0, r1 = ref_off, ref_off + (b - a)
        s, e = max(lo, r0), min(hi, r1)
        if s < e:
            out.append(a_p[:, a + (s - r0):a + (e - r0)])
        ref_off = r1
    return jnp.concatenate(out, axis=1)


def kernel(x, mem, g_pre, w_in, w_conv, attn_sink, g_mem, w_mem_kv, w_up_a, w_up_b, w_up_m, w_out, g_post, loss_target, m_g_pre, m_w_in, m_w_conv, m_attn_sink, m_g_mem, m_w_mem_kv, m_w_up_a, m_w_up_b, m_w_up_m, m_w_out, m_g_post, v_g_pre, v_w_in, v_w_conv, v_attn_sink, v_g_mem, v_w_mem_kv, v_w_up_a, v_w_up_b, v_w_up_m, v_w_out, v_g_post):
    xi, yi, ci = _position()
    chip = 2 * xi + yi
    where = jnp.stack([chip, ci]).astype(jnp.int32)

    own = [w_in[0].astype(BF16), w_mem_kv[0].astype(BF16),
           jnp.concatenate([w_up_a[0], w_up_b[0], w_up_m[0]], axis=0).astype(BF16), w_out[0].astype(BF16)]
    own_conv = jnp.pad(w_conv[0], ((0, 5), (0, 0)))

    def pieces(mine, got):
        got = lax.dynamic_update_slice_in_dim(got, mine[None], chip, axis=0)
        return [got[j] for j in range(N_CHIPS)]

    got_in, got_conv = _run_carry(_gather_weights(own[:1], own_conv), "gather_w_in")
    w_in_p = _w_in_permuted(pieces(own[0], got_in))
    h = _rmsnorm_fwd(x[0], g_pre, name="pre_norm")
    proj, gathered = _matmul(h, w_in_p, mode="nn", out_dtype=F32, tm=512, tn=3712, tk=D_MODEL, name="proj",
                             j_outer=True, carry=_gather_weights(own[1:]))
    w_kv_full = jnp.concatenate(pieces(own[1], gathered[0]), axis=0)
    up_pieces = pieces(own[2], gathered[1])
    w_up_full = jnp.stack([jnp.concatenate([p[k * A_WIDTH:(k + 1) * A_WIDTH] for p in up_pieces], axis=1)
                           for k in range(3)])
    w_out_full = jnp.concatenate(pieces(own[3], gathered[2]), axis=0)
    w_conv_full = jnp.concatenate([p[:3] for p in pieces(own_conv, got_conv)], axis=1)

    g = _forward_backward(x[0], mem[0], loss_target[0], proj, w_conv_full, attn_sink, g_mem, w_kv_full, w_up_full,
                          w_out_full, g_post)

    half_rows = D_MODEL // 2
    up_parts = (g["w_up"].reshape(3, A_WIDTH, N_CHIPS, D_MODEL // N_CHIPS).transpose(2, 0, 1, 3)
                .reshape(N_CHIPS, 2, 3 * A_WIDTH // 2, D_MODEL // N_CHIPS).transpose(1, 0, 2, 3)).astype(BF16)
    small_parts = [_half_major(g["w_kv"]).astype(BF16), up_parts, _half_major(g["w_out"]).astype(BF16)]

    def dw_in_half(hf, name, carry=None):
        h_half = lax.dynamic_slice_in_dim(h, hf * half_rows, half_rows, axis=1)
        return _matmul(h_half, g["dproj"], mode="tn", out_dtype=BF16, tm=half_rows, tn=3712, tk=512, name=name,
                       carry=carry)

    def shard_major(dw_half):
        return jnp.stack([_ref_cols(dw_half, j * SHARD_W, (j + 1) * SHARD_W) for j in range(N_CHIPS)])

    def pick(parts, hf):
        return [lax.dynamic_index_in_dim(p, hf, 0, keepdims=False) for p in parts]

    small_names = ["w_kv", "w_up", "w_out"]
    recv_small = _run_carry(_pair_exchange(pick(small_parts, 1 - ci)), "pair_exchange_small")
    sums_small = [_pair_add(k, r, "pair_add_" + nm)
                  for k, r, nm in zip(pick(small_parts, ci), recv_small, small_names)]
    dw_send, recv3_small = dw_in_half(1 - ci, "dw_in_send", _chip_exchange(sums_small))
    dw_keep, (recv_in,) = dw_in_half(ci, "dw_in_keep", _pair_exchange([shard_major(dw_send)]))
    sum_in = _pair_add(shard_major(dw_keep), recv_in, "pair_add_w_in")
    d_h, (recv3_in,) = _matmul(g["dproj"], w_in_p, mode="nt", out_dtype=F32, tm=512, tn=1024, tk=3712, name="d_h",
                               carry=_chip_exchange([sum_in]))
    pairs = [_chip_add(s, r, where, "chip_add_" + nm)
             for s, r, nm in zip([sum_in] + sums_small, [recv3_in] + recv3_small, ["w_in"] + small_names)]
    grad_x, dg_pre = _rmsnorm_bwd(d_h, x[0], g_pre, g["dy"], name="pre_norm_bwd")

    zeros512 = jnp.zeros((1, D_MODEL - A_WIDTH), F32)
    conv_rows = [jnp.concatenate([g["w_conv"][k:k + 1], zeros512], axis=1) for k in range(3)]
    sink_row = jnp.pad(g["sink"][:, 0].reshape(1, N_Q_HEADS), ((0, 0), (0, D_MODEL - N_Q_HEADS)))
    loss_row = jnp.pad(g["loss"], ((0, 0), (0, D_MODEL - LANES)))
    pack = jnp.concatenate([dg_pre, g["g_mem"], g["g_post"]] + conv_rows + [sink_row, loss_row], axis=0)
    red, full = _small_allreduce(pack, _pair_share(pairs))
    loss = red[7, 0]
    small_grads = dict(
        g_pre=red[0:1], g_mem=red[1:2], g_post=red[2:3], attn_sink=red[6:7, :N_Q_HEADS],
        w_conv=lax.dynamic_slice(red[3:6, :A_WIDTH], (0, chip * LANES), (3, LANES)))

    gw_up = full[2].reshape(3, A_WIDTH, D_MODEL // N_CHIPS)
    grads = dict(small_grads, w_in=full[0].reshape(D_MODEL, SHARD_W),
                 w_mem_kv=full[1].reshape(D_MODEL // N_CHIPS, 2 * MEM_WIDTH),
                 w_up_a=gw_up[0], w_up_b=gw_up[1], w_up_m=gw_up[2],
                 w_out=full[3].reshape(D_MODEL // N_CHIPS, D_MODEL))

    weights = dict(g_pre=g_pre, w_in=w_in, w_conv=w_conv, attn_sink=attn_sink, g_mem=g_mem, w_mem_kv=w_mem_kv,
                   w_up_a=w_up_a, w_up_b=w_up_b, w_up_m=w_up_m, w_out=w_out, g_post=g_post)
    m_in = dict(g_pre=m_g_pre, w_in=m_w_in, w_conv=m_w_conv, attn_sink=m_attn_sink, g_mem=m_g_mem,
                w_mem_kv=m_w_mem_kv, w_up_a=m_w_up_a, w_up_b=m_w_up_b, w_up_m=m_w_up_m, w_out=m_w_out,
                g_post=m_g_post)
    v_in = dict(g_pre=v_g_pre, w_in=v_w_in, w_conv=v_w_conv, attn_sink=v_attn_sink, g_mem=v_g_mem,
                w_mem_kv=v_w_mem_kv, w_up_a=v_w_up_a, w_up_b=v_w_up_b, w_up_m=v_w_up_m, w_out=v_w_out,
                g_post=v_g_post)
    out_g, out_d, out_m, out_v = [], [], [], []
    for nm in ("g_pre", "w_in", "w_conv", "attn_sink", "g_mem", "w_mem_kv", "w_up_a", "w_up_b", "w_up_m", "w_out",
               "g_post"):
        shape = weights[nm].shape
        two_d = shape[-2:]
        gr = grads[nm].reshape(two_d)
        d, m_new, v_new = _adamw(weights[nm].reshape(two_d), gr, m_in[nm].reshape(two_d), v_in[nm].reshape(two_d),
                                 "adamw_" + nm)
        out_g.append(gr.reshape(shape))
        out_d.append(d.reshape(shape))
        out_m.append(m_new.reshape(shape))
        out_v.append(v_new.reshape(shape))
    return (loss, grad_x.reshape(x.shape), *out_g, *out_d, *out_m, *out_v)
```

```python
import functools

import jax
import jax.numpy as jnp
from jax import lax
from jax.experimental import pallas as pl
from jax.experimental.pallas import tpu as pltpu

F32 = jnp.float32
BF16 = jnp.bfloat16
MESH = pl.DeviceIdType.MESH

D_MODEL = 1024
EPS = 1e-6
A_WIDTH = 512
HEAD_DIM = 64
N_Q_HEADS = 8
WINDOW_BLOCK = 128
KV_PAD = 512
ROPE_THETA = 500000.0
ROT_DIM = 16
MEM_HEADS = 4
MEM_HEAD_DIM = 128
MEM_WIDTH = 512
IN_WIDTH = 7424
N_CHIPS = 4
LANES = 128
HALF_LANES = 64

PERM_SEGS = ((0, 2560), (2816, 3328), (4352, 7424), (3328, 4352), (2560, 2816))
UNPERM_SEGS = ((0, 2560), (7168, 7424), (2560, 3072), (6144, 7168), (3072, 6144))
COL_A, W_A = 0, 2048
COL_B, W_B = 2, 1024
COL_G, W_G = 1, 3072
COL_M, W_M = 6, 1024
COL_KV, W_KV = 28, 256

ADAM_LR = 0.001
ADAM_B1 = 0.9
ADAM_B2 = 0.999
ADAM_EPS = 1e-08
ADAM_WD = 0.01
ADAM_STEP = 10

VMEM_LIMIT_BYTES = 48 * 1024 * 1024


_HBM = pl.BlockSpec(memory_space=pltpu.HBM)


def _params(**kw):
    return pltpu.CompilerParams(vmem_limit_bytes=VMEM_LIMIT_BYTES, **kw)


def _sigmoid(v):
    return jax.nn.sigmoid(v)


_DIMS = {"nn": (((1,), (0,)), ((), ())), "nt": (((1,), (1,)), ((), ())), "tn": (((0,), (0,)), ((), ()))}


class _Carry:
    def __init__(self, ins, out_shapes, sems, start, finish, aliases=None):
        self.ins, self.out_shapes, self.sems = list(ins), list(out_shapes), list(sems)
        self.start, self.finish, self.aliases = start, finish, dict(aliases or {})


def _carried_call(body, carry, *, grid, in_specs, out_specs, out_shape, scratch, operands, name):
    n_in, n_out, n_scr = len(in_specs), len(out_specs), len(scratch)
    c_in = len(carry.ins) if carry else 0
    c_out = len(carry.out_shapes) if carry else 0
    steps = 1
    for g in grid:
        steps *= g

    def wrapped(*refs):
        ins, cins = refs[:n_in], refs[n_in:n_in + c_in]
        outs = refs[n_in + c_in:n_in + c_in + n_out]
        couts = refs[n_in + c_in + n_out:n_in + c_in + n_out + c_out]
        rest = refs[n_in + c_in + n_out + c_out:]
        scr, sems = rest[:n_scr], rest[n_scr:]
        if carry:
            step = pl.program_id(0)
            for ax in range(1, len(grid)):
                step = step * grid[ax] + pl.program_id(ax)

            @pl.when(step == 0)
            def _():
                carry.start(cins, couts, sems)

        body(ins, outs, scr)
        if carry:
            @pl.when(step == steps - 1)
            def _():
                carry.finish(cins, couts, sems)

    aliases = {n_in + i: n_out + o for i, o in carry.aliases.items()} if carry else {}
    results = pl.pallas_call(
        wrapped, grid=grid, in_specs=list(in_specs) + [_HBM] * c_in, out_specs=list(out_specs) + [_HBM] * c_out,
        out_shape=list(out_shape) + (carry.out_shapes if carry else []),
        scratch_shapes=list(scratch) + (carry.sems if carry else []), input_output_aliases=aliases,
        name=name, compiler_params=_params())(*operands, *(carry.ins if carry else []))
    return list(results[:n_out]), list(results[n_out:])


def _matmul(a, b, *, mode, out_dtype, tm, tn, tk, name, j_outer=False, carry=None):
    if mode == "nn":
        (m, k), (_, n) = a.shape, b.shape
    elif mode == "nt":
        (m, k), (n, _) = a.shape, b.shape
    else:
        (k, m), (_, n) = a.shape, b.shape
    tm, tn, tk = min(tm, m), min(tn, n), min(tk, k)
    assert m % tm == 0 and n % tn == 0 and k % tk == 0
    ni, nj, nk = m // tm, n // tn, k // tk
    dims = _DIMS[mode]

    def ij(g0, g1):
        return (g1, g0) if j_outer else (g0, g1)

    if mode == "nn":
        a_spec = pl.BlockSpec((tm, tk), lambda g0, g1, kk: (ij(g0, g1)[0], kk))
        b_spec = pl.BlockSpec((tk, tn), lambda g0, g1, kk: (kk, ij(g0, g1)[1]))
    elif mode == "nt":
        a_spec = pl.BlockSpec((tm, tk), lambda g0, g1, kk: (ij(g0, g1)[0], kk))
        b_spec = pl.BlockSpec((tn, tk), lambda g0, g1, kk: (ij(g0, g1)[1], kk))
    else:
        a_spec = pl.BlockSpec((tk, tm), lambda g0, g1, kk: (kk, ij(g0, g1)[0]))
        b_spec = pl.BlockSpec((tk, tn), lambda g0, g1, kk: (kk, ij(g0, g1)[1]))
    o_spec = pl.BlockSpec((tm, tn), lambda g0, g1, kk: ij(g0, g1))

    def part(a_ref, b_ref):
        return lax.dot_general(a_ref[...].astype(BF16), b_ref[...].astype(BF16), dims,
                               preferred_element_type=F32)

    if nk == 1:
        def body(ins, outs, scr):
            outs[0][...] = part(*ins).astype(out_dtype)
        scratch = []
    else:
        def body(ins, outs, scr):
            kk = pl.program_id(2)
            acc_ref = scr[0]

            @pl.when(kk == 0)
            def _():
                acc_ref[...] = part(*ins)

            @pl.when(kk > 0)
            def _():
                acc_ref[...] += part(*ins)

            @pl.when(kk == nk - 1)
            def _():
                outs[0][...] = acc_ref[...].astype(out_dtype)
        scratch = [pltpu.VMEM((tm, tn), F32)]

    grid = (nj, ni, nk) if j_outer else (ni, nj, nk)
    (out,), carried = _carried_call(
        body, carry, grid=grid, in_specs=[a_spec, b_spec], out_specs=[o_spec],
        out_shape=[jax.ShapeDtypeStruct((m, n), out_dtype)], scratch=scratch, operands=(a, b), name=name)
    return (out, carried) if carry else out


IN_BLOCK = 256
N_IN_BLOCKS = IN_WIDTH // IN_BLOCK
BLOCK_RUNS = tuple((a // IN_BLOCK, sum(d - c for c, d in PERM_SEGS[:k]) // IN_BLOCK, (b - a) // IN_BLOCK)
                   for k, (a, b) in enumerate(PERM_SEGS))


def _perm_block(r):
    p = r
    for ref0, perm0, n in BLOCK_RUNS:
        p = jnp.where((r >= ref0) & (r < ref0 + n), r - ref0 + perm0, p)
    return p


def _proj(h, w_t, *, carry=None):
    s, d = h.shape

    def body(ins, outs, scr):
        outs[0][...] = lax.dot_general(ins[0][...], ins[1][...], _DIMS["nt"], preferred_element_type=F32)

    (proj,), carried = _carried_call(
        body, carry, grid=(N_IN_BLOCKS,),
        in_specs=[pl.BlockSpec((s, d), lambda r: (0, 0)), pl.BlockSpec((IN_BLOCK, d), lambda r: (r, 0))],
        out_specs=[pl.BlockSpec((s, IN_BLOCK), lambda r: (0, _perm_block(r)))],
        out_shape=[jax.ShapeDtypeStruct((s, IN_WIDTH), F32)], scratch=[], operands=(h, w_t), name="proj")
    return (proj, carried) if carry else proj


def _dw_in_t(dproj, h_part, *, name, carry=None):
    s, c = h_part.shape

    def body(ins, outs, scr):
        outs[0][...] = lax.dot_general(ins[0][...], ins[1][...], _DIMS["tn"],
                                       preferred_element_type=F32).astype(BF16)

    (dw,), carried = _carried_call(
        body, carry, grid=(N_IN_BLOCKS,),
        in_specs=[pl.BlockSpec((s, IN_BLOCK), lambda r: (0, _perm_block(r))), pl.BlockSpec((s, c), lambda r: (0, 0))],
        out_specs=[pl.BlockSpec((IN_BLOCK, c), lambda r: (r, 0))],
        out_shape=[jax.ShapeDtypeStruct((IN_WIDTH, c), BF16)], scratch=[], operands=(dproj, h_part), name=name)
    return (dw, carried) if carry else dw


def _d_h(dproj, w_t, *, carry=None):
    s = dproj.shape[0]
    d = w_t.shape[1]
    tm = min(s, 256)

    def body(ins, outs, scr):
        a_ref, w_ref = ins
        acc = None
        for ref0, perm0, n in BLOCK_RUNS:
            part = jnp.dot(a_ref[:, perm0 * IN_BLOCK:(perm0 + n) * IN_BLOCK],
                           w_ref[ref0 * IN_BLOCK:(ref0 + n) * IN_BLOCK, :], preferred_element_type=F32)
            acc = part if acc is None else acc + part
        outs[0][...] = acc

    (dh,), carried = _carried_call(
        body, carry, grid=(s // tm,),
        in_specs=[pl.BlockSpec((tm, IN_WIDTH), lambda i: (i, 0)), pl.BlockSpec((IN_WIDTH, d), lambda i: (0, 0))],
        out_specs=[pl.BlockSpec((tm, d), lambda i: (i, 0))],
        out_shape=[jax.ShapeDtypeStruct((s, d), F32)], scratch=[], operands=(dproj, w_t), name="d_h")
    return (dh, carried) if carry else dh


def _rmsnorm_fwd(x, g, *, name):
    s, d = x.shape
    ts = min(512, s)

    def body(x_ref, g_ref, o_ref):
        xv = x_ref[...]
        r = lax.rsqrt(jnp.mean(xv * xv, axis=-1, keepdims=True) + EPS)
        o_ref[...] = ((xv * r) * g_ref[...]).astype(BF16)

    return pl.pallas_call(
        body, grid=(s // ts,),
        in_specs=[pl.BlockSpec((ts, d), lambda i: (i, 0)), pl.BlockSpec((1, d), lambda i: (0, 0))],
        out_specs=pl.BlockSpec((ts, d), lambda i: (i, 0)),
        out_shape=jax.ShapeDtypeStruct((s, d), BF16), name=name, compiler_params=_params())(x, g)


def _rmsnorm_bwd(dh, x, g, res, *, name, carry=None):
    s, d = x.shape
    ts = min(256, s)

    def body(ins, outs, scr):
        dh_ref, x_ref, g_ref, res_ref = ins
        dx_ref, dg_ref = outs
        xv = x_ref[...]
        r = lax.rsqrt(jnp.mean(xv * xv, axis=-1, keepdims=True) + EPS)
        xh = xv * r
        dhv = dh_ref[...]
        part = jnp.sum(dhv * xh, axis=0, keepdims=True)

        @pl.when(pl.program_id(0) == 0)
        def _():
            dg_ref[...] = part

        @pl.when(pl.program_id(0) > 0)
        def _():
            dg_ref[...] += part

        dxh = dhv * g_ref[...]
        dx_ref[...] = res_ref[...] + r * (dxh - xh * jnp.mean(dxh * xh, axis=-1, keepdims=True))

    row = pl.BlockSpec((ts, d), lambda i: (i, 0))
    vec = pl.BlockSpec((1, d), lambda i: (0, 0))
    outs, carried = _carried_call(
        body, carry, grid=(s // ts,), in_specs=[row, row, vec, row], out_specs=[row, vec],
        out_shape=[jax.ShapeDtypeStruct((s, d), F32), jax.ShapeDtypeStruct((1, d), F32)],
        scratch=[], operands=(dh, x, g, res), name=name)
    return (*outs, carried) if carry else tuple(outs)


MID_TILE = 256


def _gated_branches(y_refs, wup_ref, gl):
    d = D_MODEL
    us = [jnp.dot(y_refs[k][...], wup_ref[k], preferred_element_type=F32) for k in range(3)]
    sg = [_sigmoid(gl[:, k * d:(k + 1) * d]) for k in range(3)]
    return us, sg


def _mid_fwd(ya, yb, ym, proj, x, tgt, w_up, w_out, g_post):
    s, d = x.shape
    ts = MID_TILE

    def body(ya_ref, yb_ref, ym_ref, g_ref, x_ref, t_ref, wup_ref, wout_ref, gp_ref,
             m_ref, do_ref, dy_ref, dg_ref, loss_ref):
        us, sg = _gated_branches((ya_ref, yb_ref, ym_ref), wup_ref, g_ref[...])
        merged = (sg[0] * us[0] + sg[1] * us[1] + sg[2] * us[2]).astype(BF16)
        m_ref[...] = merged
        ov = jnp.dot(merged, wout_ref[...], preferred_element_type=F32)
        r = lax.rsqrt(jnp.mean(ov * ov, axis=-1, keepdims=True) + EPS)
        nh = ov * r
        gv = gp_ref[...]
        e = (x_ref[...] + nh * gv) - t_ref[...]
        lpart = 0.5 * jnp.sum(jnp.mean(e * e, axis=-1, keepdims=True), axis=0, keepdims=True)
        dy = e * (1.0 / d)
        dgp = jnp.sum(dy * nh, axis=0, keepdims=True)

        @pl.when(pl.program_id(0) == 0)
        def _():
            dg_ref[...] = dgp
            loss_ref[...] = jnp.broadcast_to(lpart, loss_ref.shape)

        @pl.when(pl.program_id(0) > 0)
        def _():
            dg_ref[...] += dgp
            loss_ref[...] += jnp.broadcast_to(lpart, loss_ref.shape)

        dn = dy * gv
        dy_ref[...] = dy
        do_ref[...] = (r * (dn - nh * jnp.mean(dn * nh, axis=-1, keepdims=True))).astype(BF16)

    row = pl.BlockSpec((ts, d), lambda i: (i, 0))
    ysp = pl.BlockSpec((ts, A_WIDTH), lambda i: (i, 0))
    vec = pl.BlockSpec((1, d), lambda i: (0, 0))
    return pl.pallas_call(
        body, grid=(s // ts,),
        in_specs=[ysp, ysp, ysp, pl.BlockSpec((ts, W_G), lambda i: (i, COL_G)), row, row,
                  pl.BlockSpec((3, A_WIDTH, d), lambda i: (0, 0, 0)), pl.BlockSpec((d, d), lambda i: (0, 0)), vec],
        out_specs=[row, row, row, vec, pl.BlockSpec((1, LANES), lambda i: (0, 0))],
        out_shape=[jax.ShapeDtypeStruct((s, d), BF16), jax.ShapeDtypeStruct((s, d), BF16),
                   jax.ShapeDtypeStruct((s, d), F32), jax.ShapeDtypeStruct((1, d), F32),
                   jax.ShapeDtypeStruct((1, LANES), F32)],
        name="mid_fwd", compiler_params=_params())(ya, yb, ym, proj, x, tgt, w_up, w_out, g_post)


def _mid_bwd(d_out, merged, ya, yb, ym, proj, w_up, w_out):
    s, d = merged.shape
    ts = MID_TILE
    last = s // ts - 1

    def body(do_ref, m_ref, ya_ref, yb_ref, ym_ref, g_ref, wup_ref, wout_ref,
             dp_ref, dya_ref, dyb_ref, dym_ref, dwup_hbm, dwout_hbm, dwup_acc, dwout_acc):
        i = pl.program_id(0)

        @pl.when(i == 0)
        def _():
            dwup_acc[...] = jnp.zeros_like(dwup_acc)
            dwout_acc[...] = jnp.zeros_like(dwout_acc)

        y_refs = (ya_ref, yb_ref, ym_ref)
        us, sg = _gated_branches(y_refs, wup_ref, g_ref[...])
        dov = do_ref[...]
        dwout_acc[...] += lax.dot_general(m_ref[...], dov, _DIMS["tn"], preferred_element_type=F32)
        dm = lax.dot_general(dov, wout_ref[...], _DIMS["nt"], preferred_element_type=F32)
        for k, dy_ref in enumerate((dya_ref, dyb_ref, dym_ref)):
            dp_ref[:, k * d:(k + 1) * d] = ((dm * us[k]) * (sg[k] * (1.0 - sg[k]))).astype(BF16)
            du = (sg[k] * dm).astype(BF16)
            dy_ref[...] = lax.dot_general(du, wup_ref[k], _DIMS["nt"], preferred_element_type=F32)
            dwup_acc[k] += lax.dot_general(y_refs[k][...], du, _DIMS["tn"], preferred_element_type=F32)

        @pl.when(i == last)
        def _():
            pltpu.sync_copy(dwup_acc, dwup_hbm)
            pltpu.sync_copy(dwout_acc, dwout_hbm)

    row = pl.BlockSpec((ts, d), lambda i: (i, 0))
    ysp = pl.BlockSpec((ts, A_WIDTH), lambda i: (i, 0))
    gsp = pl.BlockSpec((ts, W_G), lambda i: (i, COL_G))
    anysp = pl.BlockSpec(memory_space=pl.ANY)
    yshape = jax.ShapeDtypeStruct((s, A_WIDTH), F32)
    return pl.pallas_call(
        body, grid=(s // ts,),
        in_specs=[row, row, ysp, ysp, ysp, gsp, pl.BlockSpec((3, A_WIDTH, d), lambda i: (0, 0, 0)),
                  pl.BlockSpec((d, d), lambda i: (0, 0))],
        out_specs=[gsp, ysp, ysp, ysp, anysp, anysp],
        out_shape=[jax.ShapeDtypeStruct((s, IN_WIDTH), BF16), yshape, yshape, yshape,
                   jax.ShapeDtypeStruct((3, A_WIDTH, d), F32), jax.ShapeDtypeStruct((d, d), F32)],
        scratch_shapes=[pltpu.VMEM((3, A_WIDTH, d), F32), pltpu.VMEM((d, d), F32)],
        name="mid_bwd", compiler_params=_params())(d_out, merged, ya, yb, ym, proj, w_up, w_out)


def _conv_core(blk, prev, nxt, w, i, last, ts):
    c = A_WIDTH
    ab, ac, ax, az = blk[:, :c], blk[:, c:2 * c], blk[:, 2 * c:3 * c], blk[:, 3 * c:]
    cu = ac * ax
    cu_prev = (prev[7:8, c:2 * c] * prev[7:8, 2 * c:3 * c]) * jnp.where(i > 0, 1.0, 0.0)
    cu_next = (nxt[0:1, c:2 * c] * nxt[0:1, 2 * c:3 * c]) * jnp.where(i < last, 1.0, 0.0)
    row = lax.broadcasted_iota(jnp.int32, (ts, c), 0)
    cm1 = jnp.where(row == 0, cu_prev, pltpu.roll(cu, 1, 0))
    cp1 = jnp.where(row == ts - 1, cu_next, pltpu.roll(cu, ts - 1, 0))
    yc = cm1 * w[0:1] + cu * w[1:2] + cp1 * w[2:3]
    return ab, ac, ax, az, cu, cm1, cp1, yc, row


def _halo_specs(ts, width, col, nblk8):
    prev = pl.BlockSpec((8, width), lambda i: (jnp.maximum(i * (ts // 8) - 1, 0), col))
    nxt = pl.BlockSpec((8, width), lambda i: (jnp.minimum((i + 1) * (ts // 8), nblk8 - 1), col))
    return prev, nxt


def _conv_fwd(proj, w_conv):
    s = proj.shape[0]
    ts = 256
    last = s // ts - 1

    def body(a_ref, ap_ref, an_ref, w_ref, ya_ref):
        i = pl.program_id(0)
        ab, _, _, az, _, _, _, yc, _ = _conv_core(a_ref[...], ap_ref[...], an_ref[...], w_ref[...], i, last, ts)
        ya_ref[...] = ((ab * yc) * (az * _sigmoid(az))).astype(BF16)

    prev, nxt = _halo_specs(ts, W_A, COL_A, s // 8)
    return pl.pallas_call(
        body, grid=(s // ts,),
        in_specs=[pl.BlockSpec((ts, W_A), lambda i: (i, COL_A)), prev, nxt,
                  pl.BlockSpec((3, A_WIDTH), lambda i: (0, 0))],
        out_specs=pl.BlockSpec((ts, A_WIDTH), lambda i: (i, 0)),
        out_shape=jax.ShapeDtypeStruct((s, A_WIDTH), BF16), name="conv_fwd",
        compiler_params=_params())(proj, proj, proj, w_conv)


def _conv_bwd(proj, w_conv, dya, dproj):
    s = proj.shape[0]
    ts = 256
    last = s // ts - 1
    c = A_WIDTH

    def body(a_ref, ap_ref, an_ref, w_ref, d_ref, dp_ref, dn_ref, _, dproj_ref, dw_ref):
        i = pl.program_id(0)
        w = w_ref[...]
        prev, nxt = ap_ref[...], an_ref[...]
        ab, ac, ax, az, cu, cm1, cp1, yc, row = _conv_core(a_ref[...], prev, nxt, w, i, last, ts)
        sg = _sigmoid(az)
        sz = az * sg
        dya_v = d_ref[...]
        dyc = dya_v * sz * ab
        dproj_ref[:, :c] = (dya_v * sz * yc).astype(BF16)
        dproj_ref[:, 3 * c:] = (dya_v * (ab * yc) * (sg * (1.0 + az * (1.0 - sg)))).astype(BF16)

        def halo_dyc(a_row, d_row):
            azr = a_row[:, 3 * c:]
            return d_row * (azr * _sigmoid(azr)) * a_row[:, :c]

        dyc_prev = halo_dyc(prev[7:8], dp_ref[...][7:8]) * jnp.where(i > 0, 1.0, 0.0)
        dyc_next = halo_dyc(nxt[0:1], dn_ref[...][0:1]) * jnp.where(i < last, 1.0, 0.0)
        dyc_m1 = jnp.where(row == 0, dyc_prev, pltpu.roll(dyc, 1, 0))
        dyc_p1 = jnp.where(row == ts - 1, dyc_next, pltpu.roll(dyc, ts - 1, 0))
        dcu = dyc_p1 * w[0:1] + dyc * w[1:2] + dyc_m1 * w[2:3]
        dproj_ref[:, c:2 * c] = (dcu * ax).astype(BF16)
        dproj_ref[:, 2 * c:3 * c] = (dcu * ac).astype(BF16)
        dw = [jnp.sum(dyc * t, axis=0, keepdims=True) for t in (cm1, cu, cp1)]

        @pl.when(i == 0)
        def _():
            for k in range(3):
                dw_ref[k:k + 1, :] = dw[k]

        @pl.when(i > 0)
        def _():
            for k in range(3):
                dw_ref[k:k + 1, :] += dw[k]

    prev, nxt = _halo_specs(ts, W_A, COL_A, s // 8)
    dprev, dnxt = _halo_specs(ts, A_WIDTH, 0, s // 8)
    return pl.pallas_call(
        body, grid=(s // ts,),
        in_specs=[pl.BlockSpec((ts, W_A), lambda i: (i, COL_A)), prev, nxt,
                  pl.BlockSpec((3, A_WIDTH), lambda i: (0, 0)),
                  pl.BlockSpec((ts, A_WIDTH), lambda i: (i, 0)), dprev, dnxt,
                  pl.BlockSpec(memory_space=pl.ANY)],
        out_specs=[pl.BlockSpec((ts, W_A), lambda i: (i, COL_A)), pl.BlockSpec((3, A_WIDTH), lambda i: (0, 0))],
        out_shape=[jax.ShapeDtypeStruct(dproj.shape, BF16), jax.ShapeDtypeStruct((3, A_WIDTH), F32)],
        input_output_aliases={7: 0}, name="conv_bwd",
        compiler_params=_params())(proj, proj, proj, w_conv, dya, dya, dya, dproj)


def _rope_tables(s):
    half = ROT_DIM // 2
    inv_freq = jnp.power(jnp.float32(ROPE_THETA), -jnp.arange(half, dtype=F32) * (2.0 / ROT_DIM))
    ang = jnp.arange(s).astype(F32)[:, None] * inv_freq[None, :]
    cos, sin = jnp.cos(ang), jnp.sin(ang)
    pad = jnp.zeros((s, HEAD_DIM - ROT_DIM), F32)
    c = jnp.concatenate([cos, cos, pad + 1.0], axis=1)
    s1 = jnp.concatenate([-sin, jnp.zeros_like(sin), pad], axis=1)
    s2 = jnp.concatenate([jnp.zeros_like(sin), sin, pad], axis=1)
    return jnp.concatenate([c, c, s1, s1, s2, s2], axis=1)


def _rope(t, tab):
    return (t * tab[:, :LANES] + pltpu.roll(t, LANES - 8, 1) * tab[:, LANES:2 * LANES]
            + pltpu.roll(t, 8, 1) * tab[:, 2 * LANES:])


def _rope_transpose(dt, tab):
    return (dt * tab[:, :LANES] + pltpu.roll(dt * tab[:, LANES:2 * LANES], 8, 1)
            + pltpu.roll(dt * tab[:, 2 * LANES:], LANES - 8, 1))


def _rope_kv(proj, tab):
    s = proj.shape[0]
    nb = s // KV_PAD

    def body(kv_ref, t_ref, k_ref, v_ref):
        j = pl.program_id(0)
        inside = jnp.where((j > 0) & (j <= nb), 1.0, 0.0)
        kv = kv_ref[...]
        k_ref[...] = (_rope(kv[:, :LANES], t_ref[...]) * inside).astype(BF16)
        v_ref[...] = (kv[:, LANES:] * inside).astype(BF16)

    def src(j):
        return jnp.clip(j - 1, 0, nb - 1)

    o_spec = pl.BlockSpec((KV_PAD, LANES), lambda j: (j, 0))
    shp = jax.ShapeDtypeStruct((s + 2 * KV_PAD, LANES), BF16)
    return pl.pallas_call(
        body, grid=(nb + 2,),
        in_specs=[pl.BlockSpec((KV_PAD, W_KV), lambda j: (src(j), COL_KV)),
                  pl.BlockSpec((KV_PAD, 3 * LANES), lambda j: (src(j), 0))],
        out_specs=[o_spec, o_spec], out_shape=[shp, shp], name="rope_kv",
        compiler_params=_params())(proj, tab)


def _rope_kv_bwd(dkpad, dvpad, tab, dproj):
    s = tab.shape[0]
    nb = s // KV_PAD

    def body(dk_ref, dv_ref, t_ref, _, dp_ref):
        dp_ref[:, :LANES] = _rope_transpose(dk_ref[...], t_ref[...]).astype(BF16)
        dp_ref[:, LANES:] = dv_ref[...].astype(BF16)

    pad_spec = pl.BlockSpec((KV_PAD, LANES), lambda j: (j + 1, 0))
    return pl.pallas_call(
        body, grid=(nb,),
        in_specs=[pad_spec, pad_spec, pl.BlockSpec((KV_PAD, 3 * LANES), lambda j: (j, 0)),
                  pl.BlockSpec(memory_space=pl.ANY)],
        out_specs=pl.BlockSpec((KV_PAD, W_KV), lambda j: (j, COL_KV)),
        out_shape=jax.ShapeDtypeStruct(dproj.shape, BF16), input_output_aliases={3: 0},
        name="rope_kv_bwd", compiler_params=_params())(dkpad, dvpad, tab, dproj)


def _window_start(n):
    return pl.multiple_of((n - 1) * WINDOW_BLOCK + KV_PAD, WINDOW_BLOCK)


def _window_operands(k_ref, v_ref, n, lo):
    start = _window_start(n)
    kw = k_ref[pl.ds(start, 3 * WINDOW_BLOCK), :].astype(F32)
    vw = v_ref[pl.ds(start, 3 * WINDOW_BLOCK), :].astype(F32)
    kr, vr = pltpu.roll(kw, HALF_LANES, 1), pltpu.roll(vw, HALF_LANES, 1)
    k2 = (jnp.where(lo, kw, kr).astype(BF16), jnp.where(lo, kr, kw).astype(BF16))
    v2 = (jnp.where(lo, vw, vr).astype(BF16), jnp.where(lo, vr, vw).astype(BF16))
    return k2, v2


HEADS_PER_GROUP = 4
SWA_FWD_BLOCKS = 1
SWA_BWD_BLOCKS = 2


def _window_mask(n, s):
    wb = WINDOW_BLOCK
    shape = (HEADS_PER_GROUP * wb, 3 * wb)
    qi = lax.broadcasted_iota(jnp.int32, shape, 0) & (wb - 1)
    kj = lax.broadcasted_iota(jnp.int32, shape, 1)
    kpos = kj + (n - 1) * wb
    return (kj >= qi) & (kj <= qi + 2 * wb) & (kpos >= 0) & (kpos < s)


def _stack_heads(pair0, pair1, lo):
    return jnp.concatenate([jnp.where(lo, pair0, 0.0), jnp.where(lo, 0.0, pair0),
                            jnp.where(lo, pair1, 0.0), jnp.where(lo, 0.0, pair1)], axis=0)


def _unstack_pair(stacked, i, lo):
    wb = WINDOW_BLOCK
    return jnp.where(lo, stacked[2 * i * wb:(2 * i + 1) * wb], stacked[(2 * i + 1) * wb:(2 * i + 2) * wb])


def _sink_column(sink_ref, g):
    wb = WINDOW_BLOCK
    return jnp.concatenate([jnp.full((wb, 1), sink_ref[0, HEADS_PER_GROUP * g + i], F32)
                            for i in range(HEADS_PER_GROUP)], axis=0)


def _head_exp(q4, k2g, valid, sink):
    sc = lax.dot_general(q4, k2g, _DIMS["nt"], preferred_element_type=F32) * (HEAD_DIM ** -0.5)
    sc = jnp.where(valid, sc, -jnp.inf)
    m = jnp.maximum(jnp.max(sc, axis=1, keepdims=True), sink)
    return jnp.exp(sc - m).astype(BF16), jnp.exp(sink - m)


def _swa_fwd(proj, kpad, vpad, tab, sink):
    s = proj.shape[0]
    wb = WINDOW_BLOCK

    def body(b_ref, k_ref, v_ref, t_ref, sink_ref, o_ref, y_ref):
        lo = lax.broadcasted_iota(jnp.int32, (wb, LANES), 1) < HALF_LANES
        lo_w = lax.broadcasted_iota(jnp.int32, (3 * wb, LANES), 1) < HALF_LANES
        for sub in range(SWA_FWD_BLOCKS):
            n = pl.program_id(0) * SWA_FWD_BLOCKS + sub
            rows = slice(sub * wb, (sub + 1) * wb)
            k2, v2 = _window_operands(k_ref, v_ref, n, lo_w)
            valid = _window_mask(n, s)
            tab_v = t_ref[rows, :]
            ones = jnp.ones((3 * wb, LANES), BF16)
            for g in range(2):
                qr = [_rope(b_ref[rows, (2 * g + i) * LANES:(2 * g + i + 1) * LANES], tab_v) for i in range(2)]
                q4 = _stack_heads(qr[0], qr[1], lo).astype(BF16)
                e, es = _head_exp(q4, k2[g], valid, _sink_column(sink_ref, g))
                ox = jnp.dot(e, jnp.concatenate([v2[g], ones], axis=1), preferred_element_type=F32)
                o4 = ox[:, :LANES] * (1.0 / (ox[:, LANES:] + es))
                for i in range(2):
                    cols = slice((2 * g + i) * LANES, (2 * g + i + 1) * LANES)
                    op = _unstack_pair(o4, i, lo)
                    o_ref[rows, cols] = op
                    zp = b_ref[rows, A_WIDTH + cols.start:A_WIDTH + cols.stop]
                    y_ref[rows, cols] = (op * (zp * _sigmoid(zp))).astype(BF16)

    tq = SWA_FWD_BLOCKS * wb
    pad_spec = pl.BlockSpec((s + 2 * KV_PAD, LANES), lambda n: (0, 0))
    o_spec = pl.BlockSpec((tq, A_WIDTH), lambda n: (n, 0))
    return pl.pallas_call(
        body, grid=(s // tq,),
        in_specs=[pl.BlockSpec((tq, W_B), lambda n: (n, COL_B)), pad_spec, pad_spec,
                  pl.BlockSpec((tq, 3 * LANES), lambda n: (n, 0)),
                  pl.BlockSpec(memory_space=pltpu.SMEM)],
        out_specs=[o_spec, o_spec],
        out_shape=[jax.ShapeDtypeStruct((s, A_WIDTH), F32), jax.ShapeDtypeStruct((s, A_WIDTH), BF16)],
        name="swa_fwd", compiler_params=_params())(proj, kpad, vpad, tab, sink)


def _swa_bwd(proj, kpad, vpad, tab, sink, o_attn, dyb, dproj):
    s = proj.shape[0]
    wb = WINDOW_BLOCK
    scale = HEAD_DIM ** -0.5

    def body(b_ref, k_ref, v_ref, t_ref, sink_ref, o_ref, dy_ref, _, dp_ref, dk_ref, dv_ref, ds_ref):
        @pl.when(pl.program_id(0) == 0)
        def _():
            dk_ref[...] = jnp.zeros_like(dk_ref)
            dv_ref[...] = jnp.zeros_like(dv_ref)
            ds_ref[...] = jnp.zeros_like(ds_ref)

        lo = lax.broadcasted_iota(jnp.int32, (wb, LANES), 1) < HALF_LANES
        lo_w = lax.broadcasted_iota(jnp.int32, (3 * wb, LANES), 1) < HALF_LANES
        for sub in range(SWA_BWD_BLOCKS):
            n = pl.program_id(0) * SWA_BWD_BLOCKS + sub
            rows = slice(sub * wb, (sub + 1) * wb)
            k2, v2 = _window_operands(k_ref, v_ref, n, lo_w)
            valid = _window_mask(n, s)
            tab_v = t_ref[rows, :]
            ones = jnp.ones((3 * wb, LANES), BF16)
            dks, dvs = [], []
            for g in range(2):
                qr, op, do = [], [], []
                for i in range(2):
                    cols = slice((2 * g + i) * LANES, (2 * g + i + 1) * LANES)
                    zcols = slice(A_WIDTH + cols.start, A_WIDTH + cols.stop)
                    qr.append(_rope(b_ref[rows, cols], tab_v))
                    zp = b_ref[rows, zcols]
                    sg = _sigmoid(zp)
                    op.append(o_ref[rows, cols])
                    dyp = dy_ref[rows, cols]
                    do.append(dyp * (zp * sg))
                    dp_ref[rows, zcols] = (dyp * op[i] * (sg * (1.0 + zp * (1.0 - sg)))).astype(BF16)
                q4 = _stack_heads(qr[0], qr[1], lo).astype(BF16)
                do4 = _stack_heads(do[0], do[1], lo)
                o4 = jnp.concatenate([op[0], op[0], op[1], op[1]], axis=0)
                e, es = _head_exp(q4, k2[g], valid, _sink_column(sink_ref, g))
                inv = 1.0 / (jnp.dot(e, ones, preferred_element_type=F32) + es)
                prob = e.astype(F32) * jnp.concatenate([inv, inv, inv], axis=1)
                delta = jnp.sum(do4 * o4, axis=1, keepdims=True)
                do4b = do4.astype(BF16)
                dprob = lax.dot_general(do4b, v2[g], _DIMS["nt"], preferred_element_type=F32)
                dsc = (prob * (dprob - delta)).astype(BF16)
                sink_terms = (es * inv[:, :1]) * delta
                for i in range(HEADS_PER_GROUP):
                    h = HEADS_PER_GROUP * g + i
                    dsink = -jnp.sum(sink_terms[i * wb:(i + 1) * wb], axis=0, keepdims=True)
                    ds_ref[h:h + 1, :] += jnp.broadcast_to(dsink, (1, LANES))
                dq4 = jnp.dot(dsc, k2[g], preferred_element_type=F32) * scale
                for i in range(2):
                    cols = slice((2 * g + i) * LANES, (2 * g + i + 1) * LANES)
                    dp_ref[rows, cols] = _rope_transpose(_unstack_pair(dq4, i, lo), tab_v).astype(BF16)
                dk2 = lax.dot_general(dsc, q4, _DIMS["tn"], preferred_element_type=F32) * scale
                dv2 = lax.dot_general(prob.astype(BF16), do4b, _DIMS["tn"], preferred_element_type=F32)
                dks.append(dk2 + pltpu.roll(dk2, HALF_LANES, 1))
                dvs.append(dv2 + pltpu.roll(dv2, HALF_LANES, 1))
            start = _window_start(n)
            dk_ref[pl.ds(start, 3 * wb), :] += jnp.where(lo_w, dks[0], dks[1])
            dv_ref[pl.ds(start, 3 * wb), :] += jnp.where(lo_w, dvs[0], dvs[1])

    tq = SWA_BWD_BLOCKS * wb
    pad_spec = pl.BlockSpec((s + 2 * KV_PAD, LANES), lambda n: (0, 0))
    blk = pl.BlockSpec((tq, A_WIDTH), lambda n: (n, 0))
    bsp = pl.BlockSpec((tq, W_B), lambda n: (n, COL_B))
    pad_shape = jax.ShapeDtypeStruct((s + 2 * KV_PAD, LANES), F32)
    return pl.pallas_call(
        body, grid=(s // tq,),
        in_specs=[bsp, pad_spec, pad_spec, pl.BlockSpec((tq, 3 * LANES), lambda n: (n, 0)),
                  pl.BlockSpec(memory_space=pltpu.SMEM), blk, blk, pl.BlockSpec(memory_space=pl.ANY)],
        out_specs=[bsp, pad_spec, pad_spec, pl.BlockSpec((8, LANES), lambda n: (0, 0))],
        out_shape=[jax.ShapeDtypeStruct(dproj.shape, BF16), pad_shape, pad_shape,
                   jax.ShapeDtypeStruct((8, LANES), F32)],
        input_output_aliases={7: 0}, name="swa_bwd",
        compiler_params=_params())(proj, kpad, vpad, tab, sink, o_attn, dyb, dproj)


def _mem_exp(qh, mk):
    sc = lax.dot_general(qh, mk, _DIMS["nt"], preferred_element_type=F32) * (MEM_HEAD_DIM ** -0.5)
    return jnp.exp(sc - jnp.max(sc, axis=1, keepdims=True)).astype(BF16)


def _mem_fwd(proj, mkv):
    s = proj.shape[0]
    ts = 512
    mlen = mkv.shape[0]

    def body(m_ref, kv_ref, o_ref, y_ref):
        ones = jnp.ones((mlen, LANES), BF16)
        for h in range(MEM_HEADS):
            cols = slice(h * LANES, (h + 1) * LANES)
            mk = kv_ref[:, cols].astype(BF16)
            mv = kv_ref[:, MEM_WIDTH + h * LANES:MEM_WIDTH + (h + 1) * LANES].astype(BF16)
            e = _mem_exp(m_ref[:, cols].astype(BF16), mk)
            ox = jnp.dot(e, jnp.concatenate([mv, ones], axis=1), preferred_element_type=F32)
            oh = ox[:, :LANES] * (1.0 / ox[:, LANES:])
            o_ref[:, cols] = oh
            zh = m_ref[:, MEM_WIDTH + h * LANES:MEM_WIDTH + (h + 1) * LANES]
            y_ref[:, cols] = (oh * (zh * _sigmoid(zh))).astype(BF16)

    o_spec = pl.BlockSpec((ts, MEM_WIDTH), lambda i: (i, 0))
    return pl.pallas_call(
        body, grid=(s // ts,),
        in_specs=[pl.BlockSpec((ts, W_M), lambda i: (i, COL_M)),
                  pl.BlockSpec((mlen, 2 * MEM_WIDTH), lambda i: (0, 0))],
        out_specs=[o_spec, o_spec],
        out_shape=[jax.ShapeDtypeStruct((s, MEM_WIDTH), F32), jax.ShapeDtypeStruct((s, MEM_WIDTH), BF16)],
        name="mem_fwd", compiler_params=_params())(proj, mkv)


def _mem_bwd(proj, mkv, o_mem, dym, dproj):
    s = proj.shape[0]
    ts = 512
    mlen = mkv.shape[0]
    scale = MEM_HEAD_DIM ** -0.5

    def body(m_ref, kv_ref, o_ref, dy_ref, _, dp_ref, dkv_ref):
        @pl.when(pl.program_id(0) == 0)
        def _():
            dkv_ref[...] = jnp.zeros_like(dkv_ref)

        ones = jnp.ones((mlen, LANES), BF16)
        for h in range(MEM_HEADS):
            cols = slice(h * LANES, (h + 1) * LANES)
            vcols = slice(MEM_WIDTH + h * LANES, MEM_WIDTH + (h + 1) * LANES)
            mk = kv_ref[:, cols].astype(BF16)
            mv = kv_ref[:, vcols].astype(BF16)
            qh = m_ref[:, cols].astype(BF16)
            zh = m_ref[:, vcols]
            sg = _sigmoid(zh)
            oh = o_ref[:, cols]
            dyh = dy_ref[:, cols]
            doh = dyh * (zh * sg)
            dp_ref[:, vcols] = (dyh * oh * (sg * (1.0 + zh * (1.0 - sg)))).astype(BF16)
            e = _mem_exp(qh, mk)
            inv = 1.0 / jnp.dot(e, ones, preferred_element_type=F32)
            prob = e.astype(F32) * jnp.concatenate([inv] * (mlen // LANES), axis=1)
            delta = jnp.sum(doh * oh, axis=1, keepdims=True)
            dohb = doh.astype(BF16)
            dprob = lax.dot_general(dohb, mv, _DIMS["nt"], preferred_element_type=F32)
            dsc = (prob * (dprob - delta)).astype(BF16)
            dp_ref[:, cols] = (jnp.dot(dsc, mk, preferred_element_type=F32) * scale).astype(BF16)
            dkv_ref[:, cols] += lax.dot_general(dsc, qh, _DIMS["tn"], preferred_element_type=F32) * scale
            dkv_ref[:, vcols] += lax.dot_general(prob.astype(BF16), dohb, _DIMS["tn"],
                                                 preferred_element_type=F32)

    blk = pl.BlockSpec((ts, MEM_WIDTH), lambda i: (i, 0))
    msp = pl.BlockSpec((ts, W_M), lambda i: (i, COL_M))
    kvsp = pl.BlockSpec((mlen, 2 * MEM_WIDTH), lambda i: (0, 0))
    return pl.pallas_call(
        body, grid=(s // ts,),
        in_specs=[msp, kvsp, blk, blk, pl.BlockSpec(memory_space=pl.ANY)],
        out_specs=[msp, kvsp],
        out_shape=[jax.ShapeDtypeStruct(dproj.shape, BF16), jax.ShapeDtypeStruct(mkv.shape, F32)],
        input_output_aliases={4: 0}, name="mem_bwd",
        compiler_params=_params())(proj, mkv, o_mem, dym, dproj)


def _forward_backward(x, mem, tgt, proj, w_conv, sink, g_mem, w_kv, w_up, w_out, g_post):
    s = x.shape[0]
    tab = _rope_tables(s)

    ya = _conv_fwd(proj, w_conv)
    kpad, vpad = _rope_kv(proj, tab)
    o_attn, yb = _swa_fwd(proj, kpad, vpad, tab, sink)
    mn = _rmsnorm_fwd(mem, g_mem, name="mem_norm")
    mkv = _matmul(mn, w_kv, mode="nn", out_dtype=F32, tm=256, tn=1024, tk=D_MODEL, name="mem_kv")
    o_mem, ym = _mem_fwd(proj, mkv)
    merged, d_out, dy, dg_post, loss = _mid_fwd(ya, yb, ym, proj, x, tgt, w_up, w_out, g_post)
    dproj, d_ya, d_yb, d_ym, dw_up, dw_out = _mid_bwd(d_out, merged, ya, yb, ym, proj, w_up, w_out)

    dproj, dw_conv = _conv_bwd(proj, w_conv, d_ya, dproj)
    dproj, dkpad, dvpad, dsink = _swa_bwd(proj, kpad, vpad, tab, sink, o_attn, d_yb, dproj)
    dproj = _rope_kv_bwd(dkpad, dvpad, tab, dproj)
    dproj, d_mkv = _mem_bwd(proj, mkv, o_mem, d_ym, dproj)

    dw_kv = _matmul(mn, d_mkv, mode="tn", out_dtype=F32, tm=1024, tn=1024, tk=256, name="dw_kv")
    d_mn = _matmul(d_mkv, w_kv, mode="nt", out_dtype=F32, tm=256, tn=1024, tk=D_MODEL, name="d_mn")
    _, dg_mem = _rmsnorm_bwd(d_mn, mem, g_mem, d_mn, name="mem_norm_bwd")

    return dict(loss=loss, dproj=dproj, dy=dy, w_conv=dw_conv, sink=dsink, g_mem=dg_mem,
                w_kv=dw_kv, w_up=dw_up, w_out=dw_out, g_post=dg_post)


N_DEV = 8


def _position():
    return lax.axis_index("x"), lax.axis_index("y"), lax.axis_index("c")


def _other_chips(x, y):
    return (((1 - x, y), 2 * (1 - x) + y), ((x, 1 - y), 2 * x + (1 - y)), ((1 - x, 1 - y), 2 * (1 - x) + (1 - y)))


def _remote(src, dst, send_sems, recv_sems, k, device):
    return pltpu.make_async_remote_copy(src_ref=src, dst_ref=dst, send_sem=send_sems.at[k], recv_sem=recv_sems.at[k],
                                        device_id=device, device_id_type=MESH)


def _rows_half(ref, hf):
    rh = ref.shape[0] // 2
    return ref.at[pl.ds(pl.multiple_of(hf * rh, 8), rh)]


def _gather_weights(shards, small=None):
    n = len(shards)
    k = 0 if small is None else 1

    def ici(ins, outs, sems, a, r, chip, src_chip, c):
        return _remote(_rows_half(ins[a], c), _rows_half(outs[a].at[src_chip], c), sems[0], sems[1], 3 * a + r,
                       (*chip, c))

    def whole(ins, outs, sems, r, chip, src_chip, c):
        return _remote(ins[n], outs[n].at[src_chip], sems[0], sems[1], 3 * n + r, (*chip, c))

    def d2d(outs, sems, a, r, idx, hf, x, y, c):
        half = _rows_half(outs[a].at[idx], hf)
        return _remote(half, half, sems[2], sems[3], 3 * a + r, (x, y, 1 - c))

    def start(ins, outs, sems):
        x, y, c = _position()
        me = 2 * x + y
        for a in range(n):
            for r, (chip, _) in enumerate(_other_chips(x, y)):
                ici(ins, outs, sems, a, r, chip, me, c).start()
        for r, (chip, _) in enumerate(_other_chips(x, y)):
            if k:
                whole(ins, outs, sems, r, chip, me, c).start()

    def finish(ins, outs, sems):
        x, y, c = _position()
        me = 2 * x + y
        chips = _other_chips(x, y)
        for a in range(n):
            for r, (chip, idx) in enumerate(chips):
                ici(ins, outs, sems, a, r, chip, idx, c).wait_recv()
                d2d(outs, sems, a, r, idx, c, x, y, c).start()
        for a in range(n):
            for r, (chip, idx) in enumerate(chips):
                d2d(outs, sems, a, r, idx, 1 - c, x, y, c).wait_recv()
        for r, (chip, idx) in enumerate(chips):
            if k:
                whole(ins, outs, sems, r, chip, idx, c).wait_recv()
                whole(ins, outs, sems, r, chip, me, c).wait_send()
        for a in range(n):
            for r, (chip, idx) in enumerate(chips):
                ici(ins, outs, sems, a, r, chip, me, c).wait_send()
                d2d(outs, sems, a, r, idx, c, x, y, c).wait_send()

    operands = list(shards) + ([small] if k else [])
    return _Carry(operands, [jax.ShapeDtypeStruct((N_CHIPS,) + s.shape, s.dtype) for s in operands],
                  [pltpu.SemaphoreType.DMA((3 * (n + k),)), pltpu.SemaphoreType.DMA((3 * (n + k),)),
                   pltpu.SemaphoreType.DMA((3 * n,)), pltpu.SemaphoreType.DMA((3 * n,))], start, finish)


def _run_carry(carry, name):
    _, results = _carried_call(lambda ins, outs, scr: None, carry, grid=(1,), in_specs=[], out_specs=[],
                               out_shape=[], scratch=[], operands=(), name=name)
    return results


def _pair_exchange(send):
    n = len(send)

    def copies(ins, outs, sems):
        x, y, c = _position()
        return [_remote(ins[a], outs[a], sems[0], sems[1], a, (x, y, 1 - c)) for a in range(n)]

    def start(ins, outs, sems):
        for cp in copies(ins, outs, sems):
            cp.start()

    def finish(ins, outs, sems):
        for cp in copies(ins, outs, sems):
            cp.wait()

    return _Carry(send, [jax.ShapeDtypeStruct(p.shape, p.dtype) for p in send],
                  [pltpu.SemaphoreType.DMA((n,)), pltpu.SemaphoreType.DMA((n,))], start, finish)


def _chip_exchange(sums):
    n = len(sums)

    def copies(ins, outs, sems):
        x, y, c = _position()
        return [_remote(ins[a].at[idx], outs[a].at[r], sems[0], sems[1], 3 * a + r, (*chip, c))
                for a in range(n) for r, (chip, idx) in enumerate(_other_chips(x, y))]

    def start(ins, outs, sems):
        for cp in copies(ins, outs, sems):
            cp.start()

    def finish(ins, outs, sems):
        for cp in copies(ins, outs, sems):
            cp.wait()

    return _Carry(sums, [jax.ShapeDtypeStruct((3,) + p.shape[1:], p.dtype) for p in sums],
                  [pltpu.SemaphoreType.DMA((3 * n,)), pltpu.SemaphoreType.DMA((3 * n,))], start, finish)


def _pair_share(pairs):
    n = len(pairs)

    def start(ins, outs, sems):
        x, y, c = _position()
        for a in range(n):
            _remote(outs[a].at[c], outs[a].at[c], sems[0], sems[1], a, (x, y, 1 - c)).start()

    def finish(ins, outs, sems):
        x, y, c = _position()
        for a in range(n):
            _remote(outs[a].at[1 - c], outs[a].at[1 - c], sems[0], sems[1], a, (x, y, 1 - c)).wait_recv()
        for a in range(n):
            _remote(outs[a].at[c], outs[a].at[c], sems[0], sems[1], a, (x, y, 1 - c)).wait_send()

    return _Carry(pairs, [jax.ShapeDtypeStruct(p.shape, p.dtype) for p in pairs],
                  [pltpu.SemaphoreType.DMA((n,)), pltpu.SemaphoreType.DMA((n,))], start, finish,
                  aliases={a: a for a in range(n)})


def _small_allreduce(pack, share):
    rows, width = pack.shape
    n_share = len(share.ins)

    def body(p_ref, *refs):
        share_in, o_ref, share_out = refs[:n_share], refs[n_share], refs[n_share + 1:2 * n_share + 1]
        buf, send_sems, recv_sems = refs[2 * n_share + 1:2 * n_share + 4]
        share_sems = refs[2 * n_share + 4:]
        share.start(share_in, share_out, share_sems)
        x, y, c = _position()
        me = 4 * x + 2 * y + c
        buf[me] = p_ref[...]
        peers = []
        for r in range(1, N_DEV):
            fx, fy, fc = (r >> 2) & 1, (r >> 1) & 1, r & 1
            px, py, pc = (1 - x if fx else x), (1 - y if fy else y), (1 - c if fc else c)
            peers.append(((px, py, pc), 4 * px + 2 * py + pc))
        sends = [_remote(p_ref, buf.at[me], send_sems, recv_sems, r, dev) for r, (dev, _) in enumerate(peers)]
        for cp in sends:
            cp.start()
        for r, (dev, idx) in enumerate(peers):
            _remote(p_ref, buf.at[idx], send_sems, recv_sems, r, dev).wait_recv()
        for cp in sends:
            cp.wait_send()
        acc = buf[0]
        for k in range(1, N_DEV):
            acc = acc + buf[k]
        o_ref[...] = acc
        share.finish(share_in, share_out, share_sems)

    vm = pl.BlockSpec(memory_space=pltpu.VMEM)
    red, *shared = pl.pallas_call(
        body, in_specs=[vm] + [_HBM] * n_share, out_specs=[vm] + [_HBM] * n_share,
        out_shape=[jax.ShapeDtypeStruct(pack.shape, F32)] + share.out_shapes,
        scratch_shapes=[pltpu.VMEM((N_DEV, rows, width), F32), pltpu.SemaphoreType.DMA((N_DEV - 1,)),
                        pltpu.SemaphoreType.DMA((N_DEV - 1,))] + share.sems,
        input_output_aliases={1 + i: 1 + o for i, o in share.aliases.items()},
        name="small_allreduce")(pack, *share.ins)
    return red, shared


ROW_TILE_MAX = 512
BF16_SUBLANES = 16


def _row_tile(rows):
    if rows <= ROW_TILE_MAX:
        return rows
    return max(t for t in range(BF16_SUBLANES, ROW_TILE_MAX + 1, BF16_SUBLANES) if rows % t == 0)


def _pair_add(keep, recv, name):
    nj, rh, cols = keep.shape
    tr = _row_tile(rh)

    def body(k_ref, r_ref, o_ref):
        o_ref[...] = (k_ref[...].astype(F32) + r_ref[...].astype(F32)).astype(BF16)

    blk = pl.BlockSpec((None, tr, cols), lambda j, i: (j, i, 0))
    return pl.pallas_call(body, grid=(nj, rh // tr), in_specs=[blk, blk], out_specs=blk,
                          out_shape=jax.ShapeDtypeStruct(keep.shape, BF16), name=name,
                          compiler_params=_params())(keep, recv)


def _chip_add(sums, recv, where, name):
    _, rh, cols = sums.shape
    tr = _row_tile(rh)

    def body(w_ref, s_ref, r_ref, o_ref):
        o_ref[...] = ((s_ref[...].astype(F32) + r_ref[0].astype(F32)) + r_ref[1].astype(F32)) + r_ref[2].astype(F32)

    grid_spec = pltpu.PrefetchScalarGridSpec(
        num_scalar_prefetch=1, grid=(rh // tr,),
        in_specs=[pl.BlockSpec((None, tr, cols), lambda i, w_ref: (w_ref[0], i, 0)),
                  pl.BlockSpec((3, tr, cols), lambda i, w_ref: (0, i, 0))],
        out_specs=pl.BlockSpec((None, tr, cols), lambda i, w_ref: (w_ref[1], i, 0)))
    return pl.pallas_call(body, grid_spec=grid_spec, out_shape=jax.ShapeDtypeStruct((2, rh, cols), F32),
                          name=name, compiler_params=_params())(where, sums, recv)


def _adamw(w, g, m, v, name):
    rows, cols = w.shape
    tr = _row_tile(rows)
    assert rows % tr == 0

    def body(w_ref, g_ref, m_ref, v_ref, d_ref, mo_ref, vo_ref):
        gv = g_ref[...]
        m_new = ADAM_B1 * m_ref[...] + (1.0 - ADAM_B1) * gv
        v_new = ADAM_B2 * v_ref[...] + (1.0 - ADAM_B2) * jnp.square(gv)
        m_hat = m_new / (1.0 - ADAM_B1 ** ADAM_STEP)
        v_hat = v_new / (1.0 - ADAM_B2 ** ADAM_STEP)
        d_ref[...] = -ADAM_LR * (m_hat / (jnp.sqrt(v_hat) + ADAM_EPS) + ADAM_WD * w_ref[...])
        mo_ref[...] = m_new
        vo_ref[...] = v_new

    blk = pl.BlockSpec((tr, cols), lambda i: (i, 0))
    shp = jax.ShapeDtypeStruct((rows, cols), F32)
    return pl.pallas_call(body, grid=(rows // tr,), in_specs=[blk] * 4, out_specs=[blk] * 3,
                          out_shape=[shp] * 3, name=name, compiler_params=_params())(w, g, m, v)


def _adamw_halves(w, g2, m, v, name):
    rows, cols = w.shape
    half = cols // 2
    tr = _row_tile(rows)

    def body(w_ref, g_ref, m_ref, v_ref, go_ref, d_ref, mo_ref, vo_ref):
        gv = g_ref[...]
        go_ref[...] = gv
        m_new = ADAM_B1 * m_ref[...] + (1.0 - ADAM_B1) * gv
        v_new = ADAM_B2 * v_ref[...] + (1.0 - ADAM_B2) * jnp.square(gv)
        m_hat = m_new / (1.0 - ADAM_B1 ** ADAM_STEP)
        v_hat = v_new / (1.0 - ADAM_B2 ** ADAM_STEP)
        d_ref[...] = -ADAM_LR * (m_hat / (jnp.sqrt(v_hat) + ADAM_EPS) + ADAM_WD * w_ref[...])
        mo_ref[...] = m_new
        vo_ref[...] = v_new

    blk = pl.BlockSpec((tr, half), lambda hf, i: (i, hf))
    gsp = pl.BlockSpec((None, tr, half), lambda hf, i: (hf, i, 0))
    shp = jax.ShapeDtypeStruct((rows, cols), F32)
    return pl.pallas_call(body, grid=(2, rows // tr), in_specs=[blk, gsp, blk, blk], out_specs=[blk] * 4,
                          out_shape=[shp] * 4, name=name, compiler_params=_params())(w, g2, m, v)


SHARD_W = IN_WIDTH // N_CHIPS


def _half_major(a):
    r, c = a.shape
    return a.reshape(N_CHIPS, 2, r // N_CHIPS // 2, c).transpose(1, 0, 2, 3)


def kernel(x, mem, g_pre, w_in, w_conv, attn_sink, g_mem, w_mem_kv, w_up_a, w_up_b, w_up_m, w_out, g_post, loss_target, m_g_pre, m_w_in, m_w_conv, m_attn_sink, m_g_mem, m_w_mem_kv, m_w_up_a, m_w_up_b, m_w_up_m, m_w_out, m_g_post, v_g_pre, v_w_in, v_w_conv, v_attn_sink, v_g_mem, v_w_mem_kv, v_w_up_a, v_w_up_b, v_w_up_m, v_w_out, v_g_post):
    xi, yi, ci = _position()
    chip = 2 * xi + yi
    where = jnp.stack([chip, ci]).astype(jnp.int32)

    own = [w_in[0].T.astype(BF16), w_mem_kv[0].astype(BF16),
           jnp.concatenate([w_up_a[0], w_up_b[0], w_up_m[0]], axis=0).astype(BF16), w_out[0].astype(BF16)]
    own_conv = jnp.pad(w_conv[0], ((0, 5), (0, 0)))

    def pieces(mine, got):
        got = lax.dynamic_update_slice_in_dim(got, mine[None], chip, axis=0)
        return [got[j] for j in range(N_CHIPS)]

    got_in, got_conv = _run_carry(_gather_weights(own[:1], own_conv), "gather_w_in")
    w_in_t = lax.dynamic_update_slice_in_dim(got_in, own[0][None], chip, axis=0).reshape(IN_WIDTH, D_MODEL)
    h = _rmsnorm_fwd(x[0], g_pre, name="pre_norm")
    proj, gathered = _proj(h, w_in_t, carry=_gather_weights(own[1:]))
    w_kv_full = jnp.concatenate(pieces(own[1], gathered[0]), axis=0)
    up_pieces = pieces(own[2], gathered[1])
    w_up_full = jnp.stack([jnp.concatenate([p[k * A_WIDTH:(k + 1) * A_WIDTH] for p in up_pieces], axis=1)
                           for k in range(3)])
    w_out_full = jnp.concatenate(pieces(own[3], gathered[2]), axis=0)
    w_conv_full = jnp.concatenate([p[:3] for p in pieces(own_conv, got_conv)], axis=1)

    g = _forward_backward(x[0], mem[0], loss_target[0], proj, w_conv_full, attn_sink, g_mem, w_kv_full, w_up_full,
                          w_out_full, g_post)

    half_rows = D_MODEL // 2
    up_parts = (g["w_up"].reshape(3, A_WIDTH, N_CHIPS, D_MODEL // N_CHIPS).transpose(2, 0, 1, 3)
                .reshape(N_CHIPS, 2, 3 * A_WIDTH // 2, D_MODEL // N_CHIPS).transpose(1, 0, 2, 3)).astype(BF16)
    small_parts = [_half_major(g["w_kv"]).astype(BF16), up_parts, _half_major(g["w_out"]).astype(BF16)]

    def dw_in_half(hf, name, carry=None):
        h_half = lax.dynamic_slice_in_dim(h, hf * half_rows, half_rows, axis=1)
        out = _dw_in_t(g["dproj"], h_half, name=name, carry=carry)
        if carry is None:
            return out.reshape(N_CHIPS, SHARD_W, half_rows)
        return out[0].reshape(N_CHIPS, SHARD_W, half_rows), out[1]

    def pick(parts, hf):
        return [lax.dynamic_index_in_dim(p, hf, 0, keepdims=False) for p in parts]

    small_names = ["w_kv", "w_up", "w_out"]
    recv_small = _run_carry(_pair_exchange(pick(small_parts, 1 - ci)), "pair_exchange_small")
    sums_small = [_pair_add(k, r, "pair_add_" + nm)
                  for k, r, nm in zip(pick(small_parts, ci), recv_small, small_names)]
    dw_send, recv3_small = dw_in_half(1 - ci, "dw_in_send", _chip_exchange(sums_small))
    dw_keep, (recv_in,) = dw_in_half(ci, "dw_in_keep", _pair_exchange([dw_send]))
    sum_in = _pair_add(dw_keep, recv_in, "pair_add_w_in")
    d_h, (recv3_in,) = _d_h(g["dproj"], w_in_t, carry=_chip_exchange([sum_in]))
    pairs = [_chip_add(s, r, where, "chip_add_" + nm)
             for s, r, nm in zip([sum_in] + sums_small, [recv3_in] + recv3_small, ["w_in"] + small_names)]
    grad_x, dg_pre = _rmsnorm_bwd(d_h, x[0], g_pre, g["dy"], name="pre_norm_bwd")

    zeros512 = jnp.zeros((1, D_MODEL - A_WIDTH), F32)
    conv_rows = [jnp.concatenate([g["w_conv"][k:k + 1], zeros512], axis=1) for k in range(3)]
    sink_row = jnp.pad(g["sink"][:, 0].reshape(1, N_Q_HEADS), ((0, 0), (0, D_MODEL - N_Q_HEADS)))
    loss_row = jnp.pad(g["loss"], ((0, 0), (0, D_MODEL - LANES)))
    pack = jnp.concatenate([dg_pre, g["g_mem"], g["g_post"]] + conv_rows + [sink_row, loss_row], axis=0)
    red, full = _small_allreduce(pack, _pair_share(pairs))
    loss = red[7, 0]
    small_grads = dict(
        g_pre=red[0:1], g_mem=red[1:2], g_post=red[2:3], attn_sink=red[6:7, :N_Q_HEADS],
        w_conv=lax.dynamic_slice(red[3:6, :A_WIDTH], (0, chip * LANES), (3, LANES)))

    gw_up = full[2].reshape(3, A_WIDTH, D_MODEL // N_CHIPS)
    grads = dict(small_grads, w_mem_kv=full[1].reshape(D_MODEL // N_CHIPS, 2 * MEM_WIDTH),
                 w_up_a=gw_up[0], w_up_b=gw_up[1], w_up_m=gw_up[2],
                 w_out=full[3].reshape(D_MODEL // N_CHIPS, D_MODEL))

    weights = dict(g_pre=g_pre, w_in=w_in, w_conv=w_conv, attn_sink=attn_sink, g_mem=g_mem, w_mem_kv=w_mem_kv,
                   w_up_a=w_up_a, w_up_b=w_up_b, w_up_m=w_up_m, w_out=w_out, g_post=g_post)
    m_in = dict(g_pre=m_g_pre, w_in=m_w_in, w_conv=m_w_conv, attn_sink=m_attn_sink, g_mem=m_g_mem,
                w_mem_kv=m_w_mem_kv, w_up_a=m_w_up_a, w_up_b=m_w_up_b, w_up_m=m_w_up_m, w_out=m_w_out,
                g_post=m_g_post)
    v_in = dict(g_pre=v_g_pre, w_in=v_w_in, w_conv=v_w_conv, attn_sink=v_attn_sink, g_mem=v_g_mem,
                w_mem_kv=v_w_mem_kv, w_up_a=v_w_up_a, w_up_b=v_w_up_b, w_up_m=v_w_up_m, w_out=v_w_out,
                g_post=v_g_post)
    out_g, out_d, out_m, out_v = [], [], [], []
    for nm in ("g_pre", "w_in", "w_conv", "attn_sink", "g_mem", "w_mem_kv", "w_up_a", "w_up_b", "w_up_m", "w_out",
               "g_post"):
        shape = weights[nm].shape
        if nm == "w_in":
            results = _adamw_halves(w_in[0].T, full[0], m_w_in[0].T, v_w_in[0].T, "adamw_w_in")
            for out, t in zip((out_g, out_d, out_m, out_v), results):
                out.append(t.T.reshape(shape))
            continue
        two_d = shape[-2:]
        gr = grads[nm].reshape(two_d)
        d, m_new, v_new = _adamw(weights[nm].reshape(two_d), gr, m_in[nm].reshape(two_d), v_in[nm].reshape(two_d),
                                 "adamw_" + nm)
        out_g.append(gr.reshape(shape))
        out_d.append(d.reshape(shape))
        out_m.append(m_new.reshape(shape))
        out_v.append(v_new.reshape(shape))
    return (loss, grad_x.reshape(x.shape), *out_g, *out_d, *out_m, *out_v)
```

```python
import functools

import jax
import jax.numpy as jnp
from jax import lax
from jax.experimental import pallas as pl
from jax.experimental.pallas import tpu as pltpu

F32 = jnp.float32
BF16 = jnp.bfloat16
MESH = pl.DeviceIdType.MESH

D_MODEL = 1024
EPS = 1e-6
A_WIDTH = 512
HEAD_DIM = 64
N_Q_HEADS = 8
WINDOW_BLOCK = 128
KV_PAD = 512
ROPE_THETA = 500000.0
ROT_DIM = 16
MEM_HEADS = 4
MEM_HEAD_DIM = 128
MEM_WIDTH = 512
IN_WIDTH = 7424
N_CHIPS = 4
LANES = 128
HALF_LANES = 64

PERM_SEGS = ((0, 2560), (2816, 3328), (4352, 7424), (3328, 4352), (2560, 2816))
UNPERM_SEGS = ((0, 2560), (7168, 7424), (2560, 3072), (6144, 7168), (3072, 6144))
COL_A, W_A = 0, 2048
COL_B, W_B = 2, 1024
COL_G, W_G = 1, 3072
COL_M, W_M = 6, 1024
COL_KV, W_KV = 28, 256

ADAM_LR = 0.001
ADAM_B1 = 0.9
ADAM_B2 = 0.999
ADAM_EPS = 1e-08
ADAM_WD = 0.01
ADAM_STEP = 10

VMEM_LIMIT_BYTES = 48 * 1024 * 1024


_HBM = pl.BlockSpec(memory_space=pltpu.HBM)


def _params(**kw):
    return pltpu.CompilerParams(vmem_limit_bytes=VMEM_LIMIT_BYTES, **kw)


def _sigmoid(v):
    return jax.nn.sigmoid(v)


_DIMS = {"nn": (((1,), (0,)), ((), ())), "nt": (((1,), (1,)), ((), ())), "tn": (((0,), (0,)), ((), ()))}


class _Carry:
    def __init__(self, ins, out_shapes, sems, start, finish, aliases=None):
        self.ins, self.out_shapes, self.sems = list(ins), list(out_shapes), list(sems)
        self.start, self.finish, self.aliases = start, finish, dict(aliases or {})


def _carried_call(body, carry, *, grid, in_specs, out_specs, out_shape, scratch, operands, name, prefetch=None,
                  aliases=None):
    n_in, n_out, n_scr = len(in_specs), len(out_specs), len(scratch)
    c_in = len(carry.ins) if carry else 0
    c_out = len(carry.out_shapes) if carry else 0
    n_pre = 0 if prefetch is None else 1
    steps = 1
    for g in grid:
        steps *= g

    def wrapped(*refs):
        refs = refs[n_pre:]
        ins, cins = refs[:n_in], refs[n_in:n_in + c_in]
        outs = refs[n_in + c_in:n_in + c_in + n_out]
        couts = refs[n_in + c_in + n_out:n_in + c_in + n_out + c_out]
        rest = refs[n_in + c_in + n_out + c_out:]
        scr, sems = rest[:n_scr], rest[n_scr:]
        if carry:
            step = pl.program_id(0)
            for ax in range(1, len(grid)):
                step = step * grid[ax] + pl.program_id(ax)

            @pl.when(step == 0)
            def _():
                carry.start(cins, couts, sems)

        body(ins, outs, scr)
        if carry:
            @pl.when(step == steps - 1)
            def _():
                carry.finish(cins, couts, sems)

    all_aliases = {n_pre + i: o for i, o in (aliases or {}).items()}
    if carry:
        all_aliases.update({n_pre + n_in + i: n_out + o for i, o in carry.aliases.items()})
    all_in = list(in_specs) + [_HBM] * c_in
    all_out = list(out_specs) + [_HBM] * c_out
    all_scratch = list(scratch) + (carry.sems if carry else [])
    if n_pre:
        spec = dict(grid_spec=pltpu.PrefetchScalarGridSpec(num_scalar_prefetch=1, grid=grid, in_specs=all_in,
                                                           out_specs=all_out, scratch_shapes=all_scratch))
        pre = (prefetch,)
    else:
        spec = dict(grid=grid, in_specs=all_in, out_specs=all_out, scratch_shapes=all_scratch)
        pre = ()
    results = pl.pallas_call(
        wrapped, out_shape=list(out_shape) + (carry.out_shapes if carry else []), input_output_aliases=all_aliases,
        name=name, compiler_params=_params(), **spec)(*pre, *operands, *(carry.ins if carry else []))
    return list(results[:n_out]), list(results[n_out:])


def _matmul(a, b, *, mode, out_dtype, tm, tn, tk, name, j_outer=False, carry=None):
    if mode == "nn":
        (m, k), (_, n) = a.shape, b.shape
    elif mode == "nt":
        (m, k), (n, _) = a.shape, b.shape
    else:
        (k, m), (_, n) = a.shape, b.shape
    tm, tn, tk = min(tm, m), min(tn, n), min(tk, k)
    assert m % tm == 0 and n % tn == 0 and k % tk == 0
    ni, nj, nk = m // tm, n // tn, k // tk
    dims = _DIMS[mode]

    def ij(g0, g1):
        return (g1, g0) if j_outer else (g0, g1)

    if mode == "nn":
        a_spec = pl.BlockSpec((tm, tk), lambda g0, g1, kk: (ij(g0, g1)[0], kk))
        b_spec = pl.BlockSpec((tk, tn), lambda g0, g1, kk: (kk, ij(g0, g1)[1]))
    elif mode == "nt":
        a_spec = pl.BlockSpec((tm, tk), lambda g0, g1, kk: (ij(g0, g1)[0], kk))
        b_spec = pl.BlockSpec((tn, tk), lambda g0, g1, kk: (ij(g0, g1)[1], kk))
    else:
        a_spec = pl.BlockSpec((tk, tm), lambda g0, g1, kk: (kk, ij(g0, g1)[0]))
        b_spec = pl.BlockSpec((tk, tn), lambda g0, g1, kk: (kk, ij(g0, g1)[1]))
    o_spec = pl.BlockSpec((tm, tn), lambda g0, g1, kk: ij(g0, g1))

    def part(a_ref, b_ref):
        return lax.dot_general(a_ref[...].astype(BF16), b_ref[...].astype(BF16), dims,
                               preferred_element_type=F32)

    if nk == 1:
        def body(ins, outs, scr):
            outs[0][...] = part(*ins).astype(out_dtype)
        scratch = []
    else:
        def body(ins, outs, scr):
            kk = pl.program_id(2)
            acc_ref = scr[0]

            @pl.when(kk == 0)
            def _():
                acc_ref[...] = part(*ins)

            @pl.when(kk > 0)
            def _():
                acc_ref[...] += part(*ins)

            @pl.when(kk == nk - 1)
            def _():
                outs[0][...] = acc_ref[...].astype(out_dtype)
        scratch = [pltpu.VMEM((tm, tn), F32)]

    grid = (nj, ni, nk) if j_outer else (ni, nj, nk)
    (out,), carried = _carried_call(
        body, carry, grid=grid, in_specs=[a_spec, b_spec], out_specs=[o_spec],
        out_shape=[jax.ShapeDtypeStruct((m, n), out_dtype)], scratch=scratch, operands=(a, b), name=name)
    return (out, carried) if carry else out


IN_BLOCK = 256
N_IN_BLOCKS = IN_WIDTH // IN_BLOCK
SHARD_BLOCKS = (IN_WIDTH // N_CHIPS) // IN_BLOCK
BLOCK_RUNS = tuple((a // IN_BLOCK, sum(d - c for c, d in PERM_SEGS[:k]) // IN_BLOCK, (b - a) // IN_BLOCK)
                   for k, (a, b) in enumerate(PERM_SEGS))


def _perm_block(r):
    p = r
    for ref0, perm0, n in BLOCK_RUNS:
        p = jnp.where((r >= ref0) & (r < ref0 + n), r - ref0 + perm0, p)
    return p


def _proj(h, w_rows, *, n_blocks, block_of, where, name, first_row_block=None, into=None, carry=None):
    s, d = h.shape

    def body(ins, outs, scr):
        outs[0][...] = lax.dot_general(ins[0][...], ins[1][...], _DIMS["nt"], preferred_element_type=F32)

    def row_block(i, w):
        return block_of(i, w) - (0 if first_row_block is None else first_row_block(w))

    in_specs = [pl.BlockSpec((s, d), lambda i, w: (0, 0)), pl.BlockSpec((IN_BLOCK, d), lambda i, w: (row_block(i, w), 0))]
    operands = (h, w_rows)
    if into is not None:
        in_specs.append(pl.BlockSpec(memory_space=pl.ANY))
        operands += (into,)
    (proj,), carried = _carried_call(
        body, carry, grid=(n_blocks,), in_specs=in_specs,
        out_specs=[pl.BlockSpec((s, IN_BLOCK), lambda i, w: (0, _perm_block(block_of(i, w))))],
        out_shape=[jax.ShapeDtypeStruct((s, IN_WIDTH), F32)], scratch=[], operands=operands, name=name,
        prefetch=where, aliases=None if into is None else {2: 0})
    return (proj, carried) if carry else proj


def _dw_in_t(dproj, h_part, *, name, carry=None):
    s, c = h_part.shape

    def body(ins, outs, scr):
        outs[0][...] = lax.dot_general(ins[0][...], ins[1][...], _DIMS["tn"],
                                       preferred_element_type=F32).astype(BF16)

    (dw,), carried = _carried_call(
        body, carry, grid=(N_IN_BLOCKS,),
        in_specs=[pl.BlockSpec((s, IN_BLOCK), lambda r: (0, _perm_block(r))), pl.BlockSpec((s, c), lambda r: (0, 0))],
        out_specs=[pl.BlockSpec((IN_BLOCK, c), lambda r: (r, 0))],
        out_shape=[jax.ShapeDtypeStruct((IN_WIDTH, c), BF16)], scratch=[], operands=(dproj, h_part), name=name)
    return (dw, carried) if carry else dw


def _d_h(dproj, w_t, *, carry=None):
    s = dproj.shape[0]
    d = w_t.shape[1]
    tm = min(s, 256)

    def body(ins, outs, scr):
        a_ref, w_ref = ins
        acc = None
        for ref0, perm0, n in BLOCK_RUNS:
            part = jnp.dot(a_ref[:, perm0 * IN_BLOCK:(perm0 + n) * IN_BLOCK],
                           w_ref[ref0 * IN_BLOCK:(ref0 + n) * IN_BLOCK, :], preferred_element_type=F32)
            acc = part if acc is None else acc + part
        outs[0][...] = acc

    (dh,), carried = _carried_call(
        body, carry, grid=(s // tm,),
        in_specs=[pl.BlockSpec((tm, IN_WIDTH), lambda i: (i, 0)), pl.BlockSpec((IN_WIDTH, d), lambda i: (0, 0))],
        out_specs=[pl.BlockSpec((tm, d), lambda i: (i, 0))],
        out_shape=[jax.ShapeDtypeStruct((s, d), F32)], scratch=[], operands=(dproj, w_t), name="d_h")
    return (dh, carried) if carry else dh


def _rmsnorm_fwd(x, g, *, name):
    s, d = x.shape
    ts = min(512, s)

    def body(x_ref, g_ref, o_ref):
        xv = x_ref[...]
        r = lax.rsqrt(jnp.mean(xv * xv, axis=-1, keepdims=True) + EPS)
        o_ref[...] = ((xv * r) * g_ref[...]).astype(BF16)

    return pl.pallas_call(
        body, grid=(s // ts,),
        in_specs=[pl.BlockSpec((ts, d), lambda i: (i, 0)), pl.BlockSpec((1, d), lambda i: (0, 0))],
        out_specs=pl.BlockSpec((ts, d), lambda i: (i, 0)),
        out_shape=jax.ShapeDtypeStruct((s, d), BF16), name=name, compiler_params=_params())(x, g)


def _rmsnorm_bwd(dh, x, g, res, *, name, carry=None):
    s, d = x.shape
    ts = min(256, s)

    def body(ins, outs, scr):
        dh_ref, x_ref, g_ref, res_ref = ins
        dx_ref, dg_ref = outs
        xv = x_ref[...]
        r = lax.rsqrt(jnp.mean(xv * xv, axis=-1, keepdims=True) + EPS)
        xh = xv * r
        dhv = dh_ref[...]
        part = jnp.sum(dhv * xh, axis=0, keepdims=True)

        @pl.when(pl.program_id(0) == 0)
        def _():
            dg_ref[...] = part

        @pl.when(pl.program_id(0) > 0)
        def _():
            dg_ref[...] += part

        dxh = dhv * g_ref[...]
        dx_ref[...] = res_ref[...] + r * (dxh - xh * jnp.mean(dxh * xh, axis=-1, keepdims=True))

    row = pl.BlockSpec((ts, d), lambda i: (i, 0))
    vec = pl.BlockSpec((1, d), lambda i: (0, 0))
    outs, carried = _carried_call(
        body, carry, grid=(s // ts,), in_specs=[row, row, vec, row], out_specs=[row, vec],
        out_shape=[jax.ShapeDtypeStruct((s, d), F32), jax.ShapeDtypeStruct((1, d), F32)],
        scratch=[], operands=(dh, x, g, res), name=name)
    return (*outs, carried) if carry else tuple(outs)


MID_TILE = 256


def _gated_branches(y_refs, wup_ref, gl):
    d = D_MODEL
    us = [jnp.dot(y_refs[k][...], wup_ref[k], preferred_element_type=F32) for k in range(3)]
    sg = [_sigmoid(gl[:, k * d:(k + 1) * d]) for k in range(3)]
    return us, sg


def _mid_fwd(ya, yb, ym, proj, x, tgt, w_up, w_out, g_post):
    s, d = x.shape
    ts = MID_TILE

    def body(ya_ref, yb_ref, ym_ref, g_ref, x_ref, t_ref, wup_ref, wout_ref, gp_ref,
             m_ref, do_ref, dy_ref, dg_ref, loss_ref):
        us, sg = _gated_branches((ya_ref, yb_ref, ym_ref), wup_ref, g_ref[...])
        merged = (sg[0] * us[0] + sg[1] * us[1] + sg[2] * us[2]).astype(BF16)
        m_ref[...] = merged
        ov = jnp.dot(merged, wout_ref[...], preferred_element_type=F32)
        r = lax.rsqrt(jnp.mean(ov * ov, axis=-1, keepdims=True) + EPS)
        nh = ov * r
        gv = gp_ref[...]
        e = (x_ref[...] + nh * gv) - t_ref[...]
        lpart = 0.5 * jnp.sum(jnp.mean(e * e, axis=-1, keepdims=True), axis=0, keepdims=True)
        dy = e * (1.0 / d)
        dgp = jnp.sum(dy * nh, axis=0, keepdims=True)

        @pl.when(pl.program_id(0) == 0)
        def _():
            dg_ref[...] = dgp
            loss_ref[...] = jnp.broadcast_to(lpart, loss_ref.shape)

        @pl.when(pl.program_id(0) > 0)
        def _():
            dg_ref[...] += dgp
            loss_ref[...] += jnp.broadcast_to(lpart, loss_ref.shape)

        dn = dy * gv
        dy_ref[...] = dy
        do_ref[...] = (r * (dn - nh * jnp.mean(dn * nh, axis=-1, keepdims=True))).astype(BF16)

    row = pl.BlockSpec((ts, d), lambda i: (i, 0))
    ysp = pl.BlockSpec((ts, A_WIDTH), lambda i: (i, 0))
    vec = pl.BlockSpec((1, d), lambda i: (0, 0))
    return pl.pallas_call(
        body, grid=(s // ts,),
        in_specs=[ysp, ysp, ysp, pl.BlockSpec((ts, W_G), lambda i: (i, COL_G)), row, row,
                  pl.BlockSpec((3, A_WIDTH, d), lambda i: (0, 0, 0)), pl.BlockSpec((d, d), lambda i: (0, 0)), vec],
        out_specs=[row, row, row, vec, pl.BlockSpec((1, LANES), lambda i: (0, 0))],
        out_shape=[jax.ShapeDtypeStruct((s, d), BF16), jax.ShapeDtypeStruct((s, d), BF16),
                   jax.ShapeDtypeStruct((s, d), F32), jax.ShapeDtypeStruct((1, d), F32),
                   jax.ShapeDtypeStruct((1, LANES), F32)],
        name="mid_fwd", compiler_params=_params())(ya, yb, ym, proj, x, tgt, w_up, w_out, g_post)


def _mid_bwd(d_out, merged, ya, yb, ym, proj, w_up, w_out):
    s, d = merged.shape
    ts = MID_TILE
    last = s // ts - 1

    def body(do_ref, m_ref, ya_ref, yb_ref, ym_ref, g_ref, wup_ref, wout_ref,
             dp_ref, dya_ref, dyb_ref, dym_ref, dwup_hbm, dwout_hbm, dwup_acc, dwout_acc):
        i = pl.program_id(0)

        @pl.when(i == 0)
        def _():
            dwup_acc[...] = jnp.zeros_like(dwup_acc)
            dwout_acc[...] = jnp.zeros_like(dwout_acc)

        y_refs = (ya_ref, yb_ref, ym_ref)
        us, sg = _gated_branches(y_refs, wup_ref, g_ref[...])
        dov = do_ref[...]
        dwout_acc[...] += lax.dot_general(m_ref[...], dov, _DIMS["tn"], preferred_element_type=F32)
        dm = lax.dot_general(dov, wout_ref[...], _DIMS["nt"], preferred_element_type=F32)
        for k, dy_ref in enumerate((dya_ref, dyb_ref, dym_ref)):
            dp_ref[:, k * d:(k + 1) * d] = ((dm * us[k]) * (sg[k] * (1.0 - sg[k]))).astype(BF16)
            du = (sg[k] * dm).astype(BF16)
            dy_ref[...] = lax.dot_general(du, wup_ref[k], _DIMS["nt"], preferred_element_type=F32)
            dwup_acc[k] += lax.dot_general(y_refs[k][...], du, _DIMS["tn"], preferred_element_type=F32)

        @pl.when(i == last)
        def _():
            pltpu.sync_copy(dwup_acc, dwup_hbm)
            pltpu.sync_copy(dwout_acc, dwout_hbm)

    row = pl.BlockSpec((ts, d), lambda i: (i, 0))
    ysp = pl.BlockSpec((ts, A_WIDTH), lambda i: (i, 0))
    gsp = pl.BlockSpec((ts, W_G), lambda i: (i, COL_G))
    anysp = pl.BlockSpec(memory_space=pl.ANY)
    yshape = jax.ShapeDtypeStruct((s, A_WIDTH), F32)
    return pl.pallas_call(
        body, grid=(s // ts,),
        in_specs=[row, row, ysp, ysp, ysp, gsp, pl.BlockSpec((3, A_WIDTH, d), lambda i: (0, 0, 0)),
                  pl.BlockSpec((d, d), lambda i: (0, 0))],
        out_specs=[gsp, ysp, ysp, ysp, anysp, anysp],
        out_shape=[jax.ShapeDtypeStruct((s, IN_WIDTH), BF16), yshape, yshape, yshape,
                   jax.ShapeDtypeStruct((3, A_WIDTH, d), F32), jax.ShapeDtypeStruct((d, d), F32)],
        scratch_shapes=[pltpu.VMEM((3, A_WIDTH, d), F32), pltpu.VMEM((d, d), F32)],
        name="mid_bwd", compiler_params=_params())(d_out, merged, ya, yb, ym, proj, w_up, w_out)


def _conv_core(blk, prev, nxt, w, i, last, ts):
    c = A_WIDTH
    ab, ac, ax, az = blk[:, :c], blk[:, c:2 * c], blk[:, 2 * c:3 * c], blk[:, 3 * c:]
    cu = ac * ax
    cu_prev = (prev[7:8, c:2 * c] * prev[7:8, 2 * c:3 * c]) * jnp.where(i > 0, 1.0, 0.0)
    cu_next = (nxt[0:1, c:2 * c] * nxt[0:1, 2 * c:3 * c]) * jnp.where(i < last, 1.0, 0.0)
    row = lax.broadcasted_iota(jnp.int32, (ts, c), 0)
    cm1 = jnp.where(row == 0, cu_prev, pltpu.roll(cu, 1, 0))
    cp1 = jnp.where(row == ts - 1, cu_next, pltpu.roll(cu, ts - 1, 0))
    yc = cm1 * w[0:1] + cu * w[1:2] + cp1 * w[2:3]
    return ab, ac, ax, az, cu, cm1, cp1, yc, row


def _halo_specs(ts, width, col, nblk8):
    prev = pl.BlockSpec((8, width), lambda i: (jnp.maximum(i * (ts // 8) - 1, 0), col))
    nxt = pl.BlockSpec((8, width), lambda i: (jnp.minimum((i + 1) * (ts // 8), nblk8 - 1), col))
    return prev, nxt


def _conv_fwd(proj, w_conv):
    s = proj.shape[0]
    ts = 256
    last = s // ts - 1

    def body(a_ref, ap_ref, an_ref, w_ref, ya_ref):
        i = pl.program_id(0)
        ab, _, _, az, _, _, _, yc, _ = _conv_core(a_ref[...], ap_ref[...], an_ref[...], w_ref[...], i, last, ts)
        ya_ref[...] = ((ab * yc) * (az * _sigmoid(az))).astype(BF16)

    prev, nxt = _halo_specs(ts, W_A, COL_A, s // 8)
    return pl.pallas_call(
        body, grid=(s // ts,),
        in_specs=[pl.BlockSpec((ts, W_A), lambda i: (i, COL_A)), prev, nxt,
                  pl.BlockSpec((3, A_WIDTH), lambda i: (0, 0))],
        out_specs=pl.BlockSpec((ts, A_WIDTH), lambda i: (i, 0)),
        out_shape=jax.ShapeDtypeStruct((s, A_WIDTH), BF16), name="conv_fwd",
        compiler_params=_params())(proj, proj, proj, w_conv)


def _conv_bwd(proj, w_conv, dya, dproj):
    s = proj.shape[0]
    ts = 256
    last = s // ts - 1
    c = A_WIDTH

    def body(a_ref, ap_ref, an_ref, w_ref, d_ref, dp_ref, dn_ref, _, dproj_ref, dw_ref):
        i = pl.program_id(0)
        w = w_ref[...]
        prev, nxt = ap_ref[...], an_ref[...]
        ab, ac, ax, az, cu, cm1, cp1, yc, row = _conv_core(a_ref[...], prev, nxt, w, i, last, ts)
        sg = _sigmoid(az)
        sz = az * sg
        dya_v = d_ref[...]
        dyc = dya_v * sz * ab
        dproj_ref[:, :c] = (dya_v * sz * yc).astype(BF16)
        dproj_ref[:, 3 * c:] = (dya_v * (ab * yc) * (sg * (1.0 + az * (1.0 - sg)))).astype(BF16)

        def halo_dyc(a_row, d_row):
            azr = a_row[:, 3 * c:]
            return d_row * (azr * _sigmoid(azr)) * a_row[:, :c]

        dyc_prev = halo_dyc(prev[7:8], dp_ref[...][7:8]) * jnp.where(i > 0, 1.0, 0.0)
        dyc_next = halo_dyc(nxt[0:1], dn_ref[...][0:1]) * jnp.where(i < last, 1.0, 0.0)
        dyc_m1 = jnp.where(row == 0, dyc_prev, pltpu.roll(dyc, 1, 0))
        dyc_p1 = jnp.where(row == ts - 1, dyc_next, pltpu.roll(dyc, ts - 1, 0))
        dcu = dyc_p1 * w[0:1] + dyc * w[1:2] + dyc_m1 * w[2:3]
        dproj_ref[:, c:2 * c] = (dcu * ax).astype(BF16)
        dproj_ref[:, 2 * c:3 * c] = (dcu * ac).astype(BF16)
        dw = [jnp.sum(dyc * t, axis=0, keepdims=True) for t in (cm1, cu, cp1)]

        @pl.when(i == 0)
        def _():
            for k in range(3):
                dw_ref[k:k + 1, :] = dw[k]

        @pl.when(i > 0)
        def _():
            for k in range(3):
                dw_ref[k:k + 1, :] += dw[k]

    prev, nxt = _halo_specs(ts, W_A, COL_A, s // 8)
    dprev, dnxt = _halo_specs(ts, A_WIDTH, 0, s // 8)
    return pl.pallas_call(
        body, grid=(s // ts,),
        in_specs=[pl.BlockSpec((ts, W_A), lambda i: (i, COL_A)), prev, nxt,
                  pl.BlockSpec((3, A_WIDTH), lambda i: (0, 0)),
                  pl.BlockSpec((ts, A_WIDTH), lambda i: (i, 0)), dprev, dnxt,
                  pl.BlockSpec(memory_space=pl.ANY)],
        out_specs=[pl.BlockSpec((ts, W_A), lambda i: (i, COL_A)), pl.BlockSpec((3, A_WIDTH), lambda i: (0, 0))],
        out_shape=[jax.ShapeDtypeStruct(dproj.shape, BF16), jax.ShapeDtypeStruct((3, A_WIDTH), F32)],
        input_output_aliases={7: 0}, name="conv_bwd",
        compiler_params=_params())(proj, proj, proj, w_conv, dya, dya, dya, dproj)


def _rope_tables(s):
    half = ROT_DIM // 2
    dim = jnp.arange(LANES) % HEAD_DIM
    inv_freq = jnp.power(jnp.float32(ROPE_THETA), -(dim % half).astype(F32) * (2.0 / ROT_DIM))
    ang = jnp.arange(s).astype(F32)[:, None] * inv_freq[None, :]
    cos, sin = jnp.cos(ang), jnp.sin(ang)
    first, second = (dim < half)[None, :], ((dim >= half) & (dim < ROT_DIM))[None, :]
    c = jnp.where(first | second, cos, 1.0)
    s1 = jnp.where(first, -sin, 0.0)
    s2 = jnp.where(second, sin, 0.0)
    return jnp.concatenate([c, s1, s2], axis=1)


def _rope(t, tab):
    return (t * tab[:, :LANES] + pltpu.roll(t, LANES - 8, 1) * tab[:, LANES:2 * LANES]
            + pltpu.roll(t, 8, 1) * tab[:, 2 * LANES:])


def _rope_transpose(dt, tab):
    return (dt * tab[:, :LANES] + pltpu.roll(dt * tab[:, LANES:2 * LANES], 8, 1)
            + pltpu.roll(dt * tab[:, 2 * LANES:], LANES - 8, 1))


def _rope_kv(proj, tab):
    s = proj.shape[0]
    nb = s // KV_PAD

    def body(kv_ref, t_ref, k_ref, v_ref):
        j = pl.program_id(0)
        inside = jnp.where((j > 0) & (j <= nb), 1.0, 0.0)
        kv = kv_ref[...]
        k_ref[...] = (_rope(kv[:, :LANES], t_ref[...]) * inside).astype(BF16)
        v_ref[...] = (kv[:, LANES:] * inside).astype(BF16)

    def src(j):
        return jnp.clip(j - 1, 0, nb - 1)

    o_spec = pl.BlockSpec((KV_PAD, LANES), lambda j: (j, 0))
    shp = jax.ShapeDtypeStruct((s + 2 * KV_PAD, LANES), BF16)
    return pl.pallas_call(
        body, grid=(nb + 2,),
        in_specs=[pl.BlockSpec((KV_PAD, W_KV), lambda j: (src(j), COL_KV)),
                  pl.BlockSpec((KV_PAD, 3 * LANES), lambda j: (src(j), 0))],
        out_specs=[o_spec, o_spec], out_shape=[shp, shp], name="rope_kv",
        compiler_params=_params())(proj, tab)


def _rope_kv_bwd(dkpad, dvpad, tab, dproj):
    s = tab.shape[0]
    nb = s // KV_PAD

    def body(dk_ref, dv_ref, t_ref, _, dp_ref):
        dp_ref[:, :LANES] = _rope_transpose(dk_ref[...], t_ref[...]).astype(BF16)
        dp_ref[:, LANES:] = dv_ref[...].astype(BF16)

    pad_spec = pl.BlockSpec((KV_PAD, LANES), lambda j: (j + 1, 0))
    return pl.pallas_call(
        body, grid=(nb,),
        in_specs=[pad_spec, pad_spec, pl.BlockSpec((KV_PAD, 3 * LANES), lambda j: (j, 0)),
                  pl.BlockSpec(memory_space=pl.ANY)],
        out_specs=pl.BlockSpec((KV_PAD, W_KV), lambda j: (j, COL_KV)),
        out_shape=jax.ShapeDtypeStruct(dproj.shape, BF16), input_output_aliases={3: 0},
        name="rope_kv_bwd", compiler_params=_params())(dkpad, dvpad, tab, dproj)


def _window_start(n):
    return pl.multiple_of((n - 1) * WINDOW_BLOCK + KV_PAD, WINDOW_BLOCK)


def _window_operands(k_ref, v_ref, n, lo):
    start = _window_start(n)
    kw = k_ref[pl.ds(start, 3 * WINDOW_BLOCK), :].astype(F32)
    vw = v_ref[pl.ds(start, 3 * WINDOW_BLOCK), :].astype(F32)
    kr, vr = pltpu.roll(kw, HALF_LANES, 1), pltpu.roll(vw, HALF_LANES, 1)
    k2 = (jnp.where(lo, kw, kr).astype(BF16), jnp.where(lo, kr, kw).astype(BF16))
    v2 = (jnp.where(lo, vw, vr).astype(BF16), jnp.where(lo, vr, vw).astype(BF16))
    return k2, v2


HEADS_PER_GROUP = 4
SWA_FWD_BLOCKS = 1
SWA_BWD_BLOCKS = 2


def _window_mask(n, s):
    wb = WINDOW_BLOCK
    shape = (HEADS_PER_GROUP * wb, 3 * wb)
    qi = lax.broadcasted_iota(jnp.int32, shape, 0) & (wb - 1)
    kj = lax.broadcasted_iota(jnp.int32, shape, 1)
    kpos = kj + (n - 1) * wb
    return (kj >= qi) & (kj <= qi + 2 * wb) & (kpos >= 0) & (kpos < s)


def _stack_heads(pair0, pair1, lo):
    return jnp.concatenate([jnp.where(lo, pair0, 0.0), jnp.where(lo, 0.0, pair0),
                            jnp.where(lo, pair1, 0.0), jnp.where(lo, 0.0, pair1)], axis=0)


def _unstack_pair(stacked, i, lo):
    wb = WINDOW_BLOCK
    return jnp.where(lo, stacked[2 * i * wb:(2 * i + 1) * wb], stacked[(2 * i + 1) * wb:(2 * i + 2) * wb])


def _sink_column(sink_ref, g):
    wb = WINDOW_BLOCK
    return jnp.concatenate([jnp.full((wb, 1), sink_ref[0, HEADS_PER_GROUP * g + i], F32)
                            for i in range(HEADS_PER_GROUP)], axis=0)


def _head_exp(q4, k2g, valid, sink):
    sc = lax.dot_general(q4, k2g, _DIMS["nt"], preferred_element_type=F32) * (HEAD_DIM ** -0.5)
    sc = jnp.where(valid, sc, -jnp.inf)
    m = jnp.maximum(jnp.max(sc, axis=1, keepdims=True), sink)
    return jnp.exp(sc - m).astype(BF16), jnp.exp(sink - m)


def _swa_fwd(proj, kpad, vpad, tab, sink):
    s = proj.shape[0]
    wb = WINDOW_BLOCK

    def body(b_ref, k_ref, v_ref, t_ref, sink_ref, o_ref, y_ref):
        lo = lax.broadcasted_iota(jnp.int32, (wb, LANES), 1) < HALF_LANES
        lo_w = lax.broadcasted_iota(jnp.int32, (3 * wb, LANES), 1) < HALF_LANES
        for sub in range(SWA_FWD_BLOCKS):
            n = pl.program_id(0) * SWA_FWD_BLOCKS + sub
            rows = slice(sub * wb, (sub + 1) * wb)
            k2, v2 = _window_operands(k_ref, v_ref, n, lo_w)
            valid = _window_mask(n, s)
            tab_v = t_ref[rows, :]
            ones = jnp.ones((3 * wb, LANES), BF16)
            for g in range(2):
                qr = [_rope(b_ref[rows, (2 * g + i) * LANES:(2 * g + i + 1) * LANES], tab_v) for i in range(2)]
                q4 = _stack_heads(qr[0], qr[1], lo).astype(BF16)
                e, es = _head_exp(q4, k2[g], valid, _sink_column(sink_ref, g))
                ox = jnp.dot(e, jnp.concatenate([v2[g], ones], axis=1), preferred_element_type=F32)
                o4 = ox[:, :LANES] * (1.0 / (ox[:, LANES:] + es))
                for i in range(2):
                    cols = slice((2 * g + i) * LANES, (2 * g + i + 1) * LANES)
                    op = _unstack_pair(o4, i, lo)
                    o_ref[rows, cols] = op
                    zp = b_ref[rows, A_WIDTH + cols.start:A_WIDTH + cols.stop]
                    y_ref[rows, cols] = (op * (zp * _sigmoid(zp))).astype(BF16)

    tq = SWA_FWD_BLOCKS * wb
    pad_spec = pl.BlockSpec((s + 2 * KV_PAD, LANES), lambda n: (0, 0))
    o_spec = pl.BlockSpec((tq, A_WIDTH), lambda n: (n, 0))
    return pl.pallas_call(
        body, grid=(s // tq,),
        in_specs=[pl.BlockSpec((tq, W_B), lambda n: (n, COL_B)), pad_spec, pad_spec,
                  pl.BlockSpec((tq, 3 * LANES), lambda n: (n, 0)),
                  pl.BlockSpec(memory_space=pltpu.SMEM)],
        out_specs=[o_spec, o_spec],
        out_shape=[jax.ShapeDtypeStruct((s, A_WIDTH), F32), jax.ShapeDtypeStruct((s, A_WIDTH), BF16)],
        name="swa_fwd", compiler_params=_params())(proj, kpad, vpad, tab, sink)


def _swa_bwd(proj, kpad, vpad, tab, sink, o_attn, dyb, dproj):
    s = proj.shape[0]
    wb = WINDOW_BLOCK
    scale = HEAD_DIM ** -0.5

    def body(b_ref, k_ref, v_ref, t_ref, sink_ref, o_ref, dy_ref, _, dp_ref, dk_ref, dv_ref, ds_ref):
        @pl.when(pl.program_id(0) == 0)
        def _():
            dk_ref[...] = jnp.zeros_like(dk_ref)
            dv_ref[...] = jnp.zeros_like(dv_ref)
            ds_ref[...] = jnp.zeros_like(ds_ref)

        lo = lax.broadcasted_iota(jnp.int32, (wb, LANES), 1) < HALF_LANES
        lo_w = lax.broadcasted_iota(jnp.int32, (3 * wb, LANES), 1) < HALF_LANES
        for sub in range(SWA_BWD_BLOCKS):
            n = pl.program_id(0) * SWA_BWD_BLOCKS + sub
            rows = slice(sub * wb, (sub + 1) * wb)
            k2, v2 = _window_operands(k_ref, v_ref, n, lo_w)
            valid = _window_mask(n, s)
            tab_v = t_ref[rows, :]
            ones = jnp.ones((3 * wb, LANES), BF16)
            dks, dvs = [], []
            for g in range(2):
                qr, op, do = [], [], []
                for i in range(2):
                    cols = slice((2 * g + i) * LANES, (2 * g + i + 1) * LANES)
                    zcols = slice(A_WIDTH + cols.start, A_WIDTH + cols.stop)
                    qr.append(_rope(b_ref[rows, cols], tab_v))
                    zp = b_ref[rows, zcols]
                    sg = _sigmoid(zp)
                    op.append(o_ref[rows, cols])
                    dyp = dy_ref[rows, cols]
                    do.append(dyp * (zp * sg))
                    dp_ref[rows, zcols] = (dyp * op[i] * (sg * (1.0 + zp * (1.0 - sg)))).astype(BF16)
                q4 = _stack_heads(qr[0], qr[1], lo).astype(BF16)
                do4 = _stack_heads(do[0], do[1], lo)
                o4 = jnp.concatenate([op[0], op[0], op[1], op[1]], axis=0)
                e, es = _head_exp(q4, k2[g], valid, _sink_column(sink_ref, g))
                inv = 1.0 / (jnp.dot(e, ones, preferred_element_type=F32) + es)
                prob = e.astype(F32) * jnp.concatenate([inv, inv, inv], axis=1)
                delta = jnp.sum(do4 * o4, axis=1, keepdims=True)
                do4b = do4.astype(BF16)
                dprob = lax.dot_general(do4b, v2[g], _DIMS["nt"], preferred_element_type=F32)
                dsc = (prob * (dprob - delta)).astype(BF16)
                sink_terms = (es * inv[:, :1]) * delta
                for i in range(HEADS_PER_GROUP):
                    h = HEADS_PER_GROUP * g + i
                    dsink = -jnp.sum(sink_terms[i * wb:(i + 1) * wb], axis=0, keepdims=True)
                    ds_ref[h:h + 1, :] += jnp.broadcast_to(dsink, (1, LANES))
                dq4 = jnp.dot(dsc, k2[g], preferred_element_type=F32) * scale
                for i in range(2):
                    cols = slice((2 * g + i) * LANES, (2 * g + i + 1) * LANES)
                    dp_ref[rows, cols] = _rope_transpose(_unstack_pair(dq4, i, lo), tab_v).astype(BF16)
                dk2 = lax.dot_general(dsc, q4, _DIMS["tn"], preferred_element_type=F32) * scale
                dv2 = lax.dot_general(prob.astype(BF16), do4b, _DIMS["tn"], preferred_element_type=F32)
                dks.append(dk2 + pltpu.roll(dk2, HALF_LANES, 1))
                dvs.append(dv2 + pltpu.roll(dv2, HALF_LANES, 1))
            start = _window_start(n)
            dk_ref[pl.ds(start, 3 * wb), :] += jnp.where(lo_w, dks[0], dks[1])
            dv_ref[pl.ds(start, 3 * wb), :] += jnp.where(lo_w, dvs[0], dvs[1])

    tq = SWA_BWD_BLOCKS * wb
    pad_spec = pl.BlockSpec((s + 2 * KV_PAD, LANES), lambda n: (0, 0))
    blk = pl.BlockSpec((tq, A_WIDTH), lambda n: (n, 0))
    bsp = pl.BlockSpec((tq, W_B), lambda n: (n, COL_B))
    pad_shape = jax.ShapeDtypeStruct((s + 2 * KV_PAD, LANES), F32)
    return pl.pallas_call(
        body, grid=(s // tq,),
        in_specs=[bsp, pad_spec, pad_spec, pl.BlockSpec((tq, 3 * LANES), lambda n: (n, 0)),
                  pl.BlockSpec(memory_space=pltpu.SMEM), blk, blk, pl.BlockSpec(memory_space=pl.ANY)],
        out_specs=[bsp, pad_spec, pad_spec, pl.BlockSpec((8, LANES), lambda n: (0, 0))],
        out_shape=[jax.ShapeDtypeStruct(dproj.shape, BF16), pad_shape, pad_shape,
                   jax.ShapeDtypeStruct((8, LANES), F32)],
        input_output_aliases={7: 0}, name="swa_bwd",
        compiler_params=_params())(proj, kpad, vpad, tab, sink, o_attn, dyb, dproj)


def _mem_exp(qh, mk):
    sc = lax.dot_general(qh, mk, _DIMS["nt"], preferred_element_type=F32) * (MEM_HEAD_DIM ** -0.5)
    return jnp.exp(sc - jnp.max(sc, axis=1, keepdims=True)).astype(BF16)


def _mem_fwd(proj, mkv):
    s = proj.shape[0]
    ts = 512
    mlen = mkv.shape[0]

    def body(m_ref, kv_ref, o_ref, y_ref):
        ones = jnp.ones((mlen, LANES), BF16)
        for h in range(MEM_HEADS):
            cols = slice(h * LANES, (h + 1) * LANES)
            mk = kv_ref[:, cols].astype(BF16)
            mv = kv_ref[:, MEM_WIDTH + h * LANES:MEM_WIDTH + (h + 1) * LANES].astype(BF16)
            e = _mem_exp(m_ref[:, cols].astype(BF16), mk)
            ox = jnp.dot(e, jnp.concatenate([mv, ones], axis=1), preferred_element_type=F32)
            oh = ox[:, :LANES] * (1.0 / ox[:, LANES:])
            o_ref[:, cols] = oh
            zh = m_ref[:, MEM_WIDTH + h * LANES:MEM_WIDTH + (h + 1) * LANES]
            y_ref[:, cols] = (oh * (zh * _sigmoid(zh))).astype(BF16)

    o_spec = pl.BlockSpec((ts, MEM_WIDTH), lambda i: (i, 0))
    return pl.pallas_call(
        body, grid=(s // ts,),
        in_specs=[pl.BlockSpec((ts, W_M), lambda i: (i, COL_M)),
                  pl.BlockSpec((mlen, 2 * MEM_WIDTH), lambda i: (0, 0))],
        out_specs=[o_spec, o_spec],
        out_shape=[jax.ShapeDtypeStruct((s, MEM_WIDTH), F32), jax.ShapeDtypeStruct((s, MEM_WIDTH), BF16)],
        name="mem_fwd", compiler_params=_params())(proj, mkv)


def _mem_bwd(proj, mkv, o_mem, dym, dproj):
    s = proj.shape[0]
    ts = 512
    mlen = mkv.shape[0]
    scale = MEM_HEAD_DIM ** -0.5

    def body(m_ref, kv_ref, o_ref, dy_ref, _, dp_ref, dkv_ref):
        @pl.when(pl.program_id(0) == 0)
        def _():
            dkv_ref[...] = jnp.zeros_like(dkv_ref)

        ones = jnp.ones((mlen, LANES), BF16)
        for h in range(MEM_HEADS):
            cols = slice(h * LANES, (h + 1) * LANES)
            vcols = slice(MEM_WIDTH + h * LANES, MEM_WIDTH + (h + 1) * LANES)
            mk = kv_ref[:, cols].astype(BF16)
            mv = kv_ref[:, vcols].astype(BF16)
            qh = m_ref[:, cols].astype(BF16)
            zh = m_ref[:, vcols]
            sg = _sigmoid(zh)
            oh = o_ref[:, cols]
            dyh = dy_ref[:, cols]
            doh = dyh * (zh * sg)
            dp_ref[:, vcols] = (dyh * oh * (sg * (1.0 + zh * (1.0 - sg)))).astype(BF16)
            e = _mem_exp(qh, mk)
            inv = 1.0 / jnp.dot(e, ones, preferred_element_type=F32)
            prob = e.astype(F32) * jnp.concatenate([inv] * (mlen // LANES), axis=1)
            delta = jnp.sum(doh * oh, axis=1, keepdims=True)
            dohb = doh.astype(BF16)
            dprob = lax.dot_general(dohb, mv, _DIMS["nt"], preferred_element_type=F32)
            dsc = (prob * (dprob - delta)).astype(BF16)
            dp_ref[:, cols] = (jnp.dot(dsc, mk, preferred_element_type=F32) * scale).astype(BF16)
            dkv_ref[:, cols] += lax.dot_general(dsc, qh, _DIMS["tn"], preferred_element_type=F32) * scale
            dkv_ref[:, vcols] += lax.dot_general(prob.astype(BF16), dohb, _DIMS["tn"],
                                                 preferred_element_type=F32)

    blk = pl.BlockSpec((ts, MEM_WIDTH), lambda i: (i, 0))
    msp = pl.BlockSpec((ts, W_M), lambda i: (i, COL_M))
    kvsp = pl.BlockSpec((mlen, 2 * MEM_WIDTH), lambda i: (0, 0))
    return pl.pallas_call(
        body, grid=(s // ts,),
        in_specs=[msp, kvsp, blk, blk, pl.BlockSpec(memory_space=pl.ANY)],
        out_specs=[msp, kvsp],
        out_shape=[jax.ShapeDtypeStruct(dproj.shape, BF16), jax.ShapeDtypeStruct(mkv.shape, F32)],
        input_output_aliases={4: 0}, name="mem_bwd",
        compiler_params=_params())(proj, mkv, o_mem, dym, dproj)


def _forward_backward(x, mem, tgt, proj, w_conv, sink, g_mem, w_kv, w_up, w_out, g_post):
    s = x.shape[0]
    tab = _rope_tables(s)

    ya = _conv_fwd(proj, w_conv)
    kpad, vpad = _rope_kv(proj, tab)
    o_attn, yb = _swa_fwd(proj, kpad, vpad, tab, sink)
    mn = _rmsnorm_fwd(mem, g_mem, name="mem_norm")
    mkv = _matmul(mn, w_kv, mode="nn", out_dtype=F32, tm=256, tn=1024, tk=D_MODEL, name="mem_kv")
    o_mem, ym = _mem_fwd(proj, mkv)
    merged, d_out, dy, dg_post, loss = _mid_fwd(ya, yb, ym, proj, x, tgt, w_up, w_out, g_post)
    dproj, d_ya, d_yb, d_ym, dw_up, dw_out = _mid_bwd(d_out, merged, ya, yb, ym, proj, w_up, w_out)

    dproj, dw_conv = _conv_bwd(proj, w_conv, d_ya, dproj)
    dproj, dkpad, dvpad, dsink = _swa_bwd(proj, kpad, vpad, tab, sink, o_attn, d_yb, dproj)
    dproj = _rope_kv_bwd(dkpad, dvpad, tab, dproj)
    dproj, d_mkv = _mem_bwd(proj, mkv, o_mem, d_ym, dproj)

    dw_kv = _matmul(mn, d_mkv, mode="tn", out_dtype=F32, tm=1024, tn=1024, tk=256, name="dw_kv")
    d_mn = _matmul(d_mkv, w_kv, mode="nt", out_dtype=F32, tm=256, tn=1024, tk=D_MODEL, name="d_mn")
    _, dg_mem = _rmsnorm_bwd(d_mn, mem, g_mem, d_mn, name="mem_norm_bwd")

    return dict(loss=loss, dproj=dproj, dy=dy, w_conv=dw_conv, sink=dsink, g_mem=dg_mem,
                w_kv=dw_kv, w_up=dw_up, w_out=dw_out, g_post=dg_post)


N_DEV = 8


def _position():
    return lax.axis_index("x"), lax.axis_index("y"), lax.axis_index("c")


def _other_chips(x, y):
    return (((1 - x, y), 2 * (1 - x) + y), ((x, 1 - y), 2 * x + (1 - y)), ((1 - x, 1 - y), 2 * (1 - x) + (1 - y)))


def _remote(src, dst, send_sems, recv_sems, k, device):
    return pltpu.make_async_remote_copy(src_ref=src, dst_ref=dst, send_sem=send_sems.at[k], recv_sem=recv_sems.at[k],
                                        device_id=device, device_id_type=MESH)


def _rows_half(ref, hf):
    rh = ref.shape[0] // 2
    return ref.at[pl.ds(pl.multiple_of(hf * rh, 8), rh)]


def _gather_weights(shards, small=None, relations=(0, 1, 2)):
    n = len(shards)
    k = 0 if small is None else 1

    def peers(x, y):
        return [(r, chip, idx) for r, (chip, idx) in enumerate(_other_chips(x, y)) if r in relations]

    def ici(ins, outs, sems, a, r, chip, src_chip, c):
        return _remote(_rows_half(ins[a], c), _rows_half(outs[a].at[src_chip], c), sems[0], sems[1], 3 * a + r,
                       (*chip, c))

    def whole(ins, outs, sems, r, chip, src_chip, c):
        return _remote(ins[n], outs[n].at[src_chip], sems[0], sems[1], 3 * n + r, (*chip, c))

    def d2d(outs, sems, a, r, idx, hf, x, y, c):
        half = _rows_half(outs[a].at[idx], hf)
        return _remote(half, half, sems[2], sems[3], 3 * a + r, (x, y, 1 - c))

    def start(ins, outs, sems):
        x, y, c = _position()
        me = 2 * x + y
        for a in range(n):
            for r, chip, _ in peers(x, y):
                ici(ins, outs, sems, a, r, chip, me, c).start()
        for r, (chip, _) in enumerate(_other_chips(x, y)):
            if k:
                whole(ins, outs, sems, r, chip, me, c).start()

    def finish(ins, outs, sems):
        x, y, c = _position()
        me = 2 * x + y
        for a in range(n):
            for r, chip, idx in peers(x, y):
                ici(ins, outs, sems, a, r, chip, idx, c).wait_recv()
                d2d(outs, sems, a, r, idx, c, x, y, c).start()
        for a in range(n):
            for r, chip, idx in peers(x, y):
                d2d(outs, sems, a, r, idx, 1 - c, x, y, c).wait_recv()
        for r, (chip, idx) in enumerate(_other_chips(x, y)):
            if k:
                whole(ins, outs, sems, r, chip, idx, c).wait_recv()
                whole(ins, outs, sems, r, chip, me, c).wait_send()
        for a in range(n):
            for r, chip, idx in peers(x, y):
                ici(ins, outs, sems, a, r, chip, me, c).wait_send()
                d2d(outs, sems, a, r, idx, c, x, y, c).wait_send()

    operands = list(shards) + ([small] if k else [])
    return _Carry(operands, [jax.ShapeDtypeStruct((N_CHIPS,) + s.shape, s.dtype) for s in operands],
                  [pltpu.SemaphoreType.DMA((3 * (n + k),)), pltpu.SemaphoreType.DMA((3 * (n + k),)),
                   pltpu.SemaphoreType.DMA((3 * n,)), pltpu.SemaphoreType.DMA((3 * n,))], start, finish)


def _run_carry(carry, name):
    _, results = _carried_call(lambda ins, outs, scr: None, carry, grid=(1,), in_specs=[], out_specs=[],
                               out_shape=[], scratch=[], operands=(), name=name)
    return results


def _pair_exchange(send):
    n = len(send)

    def copies(ins, outs, sems):
        x, y, c = _position()
        return [_remote(ins[a], outs[a], sems[0], sems[1], a, (x, y, 1 - c)) for a in range(n)]

    def start(ins, outs, sems):
        for cp in copies(ins, outs, sems):
            cp.start()

    def finish(ins, outs, sems):
        for cp in copies(ins, outs, sems):
            cp.wait()

    return _Carry(send, [jax.ShapeDtypeStruct(p.shape, p.dtype) for p in send],
                  [pltpu.SemaphoreType.DMA((n,)), pltpu.SemaphoreType.DMA((n,))], start, finish)


def _chip_exchange(sums):
    n = len(sums)

    def copies(ins, outs, sems):
        x, y, c = _position()
        return [_remote(ins[a].at[idx], outs[a].at[r], sems[0], sems[1], 3 * a + r, (*chip, c))
                for a in range(n) for r, (chip, idx) in enumerate(_other_chips(x, y))]

    def start(ins, outs, sems):
        for cp in copies(ins, outs, sems):
            cp.start()

    def finish(ins, outs, sems):
        for cp in copies(ins, outs, sems):
            cp.wait()

    return _Carry(sums, [jax.ShapeDtypeStruct((3,) + p.shape[1:], p.dtype) for p in sums],
                  [pltpu.SemaphoreType.DMA((3 * n,)), pltpu.SemaphoreType.DMA((3 * n,))], start, finish)


def _pair_share(pairs):
    n = len(pairs)

    def start(ins, outs, sems):
        x, y, c = _position()
        for a in range(n):
            _remote(outs[a].at[c], outs[a].at[c], sems[0], sems[1], a, (x, y, 1 - c)).start()

    def finish(ins, outs, sems):
        x, y, c = _position()
        for a in range(n):
            _remote(outs[a].at[1 - c], outs[a].at[1 - c], sems[0], sems[1], a, (x, y, 1 - c)).wait_recv()
        for a in range(n):
            _remote(outs[a].at[c], outs[a].at[c], sems[0], sems[1], a, (x, y, 1 - c)).wait_send()

    return _Carry(pairs, [jax.ShapeDtypeStruct(p.shape, p.dtype) for p in pairs],
                  [pltpu.SemaphoreType.DMA((n,)), pltpu.SemaphoreType.DMA((n,))], start, finish,
                  aliases={a: a for a in range(n)})


def _small_allreduce(pack, share):
    rows, width = pack.shape
    n_share = len(share.ins)

    def body(p_ref, *refs):
        share_in, o_ref, share_out = refs[:n_share], refs[n_share], refs[n_share + 1:2 * n_share + 1]
        buf, send_sems, recv_sems = refs[2 * n_share + 1:2 * n_share + 4]
        share_sems = refs[2 * n_share + 4:]
        share.start(share_in, share_out, share_sems)
        x, y, c = _position()
        me = 4 * x + 2 * y + c
        buf[me] = p_ref[...]
        peers = []
        for r in range(1, N_DEV):
            fx, fy, fc = (r >> 2) & 1, (r >> 1) & 1, r & 1
            px, py, pc = (1 - x if fx else x), (1 - y if fy else y), (1 - c if fc else c)
            peers.append(((px, py, pc), 4 * px + 2 * py + pc))
        sends = [_remote(p_ref, buf.at[me], send_sems, recv_sems, r, dev) for r, (dev, _) in enumerate(peers)]
        for cp in sends:
            cp.start()
        for r, (dev, idx) in enumerate(peers):
            _remote(p_ref, buf.at[idx], send_sems, recv_sems, r, dev).wait_recv()
        for cp in sends:
            cp.wait_send()
        acc = buf[0]
        for k in range(1, N_DEV):
            acc = acc + buf[k]
        o_ref[...] = acc
        share.finish(share_in, share_out, share_sems)

    vm = pl.BlockSpec(memory_space=pltpu.VMEM)
    red, *shared = pl.pallas_call(
        body, in_specs=[vm] + [_HBM] * n_share, out_specs=[vm] + [_HBM] * n_share,
        out_shape=[jax.ShapeDtypeStruct(pack.shape, F32)] + share.out_shapes,
        scratch_shapes=[pltpu.VMEM((N_DEV, rows, width), F32), pltpu.SemaphoreType.DMA((N_DEV - 1,)),
                        pltpu.SemaphoreType.DMA((N_DEV - 1,))] + share.sems,
        input_output_aliases={1 + i: 1 + o for i, o in share.aliases.items()},
        name="small_allreduce")(pack, *share.ins)
    return red, shared


ROW_TILE_MAX = 512
BF16_SUBLANES = 16


def _row_tile(rows):
    if rows <= ROW_TILE_MAX:
        return rows
    return max(t for t in range(BF16_SUBLANES, ROW_TILE_MAX + 1, BF16_SUBLANES) if rows % t == 0)


def _pair_add(keep, recv, name):
    nj, rh, cols = keep.shape
    tr = _row_tile(rh)

    def body(k_ref, r_ref, o_ref):
        o_ref[...] = (k_ref[...].astype(F32) + r_ref[...].astype(F32)).astype(BF16)

    blk = pl.BlockSpec((None, tr, cols), lambda j, i: (j, i, 0))
    return pl.pallas_call(body, grid=(nj, rh // tr), in_specs=[blk, blk], out_specs=blk,
                          out_shape=jax.ShapeDtypeStruct(keep.shape, BF16), name=name,
                          compiler_params=_params())(keep, recv)


def _chip_add(sums, recv, where, name):
    _, rh, cols = sums.shape
    tr = _row_tile(rh)

    def body(w_ref, s_ref, r_ref, o_ref):
        o_ref[...] = ((s_ref[...].astype(F32) + r_ref[0].astype(F32)) + r_ref[1].astype(F32)) + r_ref[2].astype(F32)

    grid_spec = pltpu.PrefetchScalarGridSpec(
        num_scalar_prefetch=1, grid=(rh // tr,),
        in_specs=[pl.BlockSpec((None, tr, cols), lambda i, w_ref: (w_ref[0], i, 0)),
                  pl.BlockSpec((3, tr, cols), lambda i, w_ref: (0, i, 0))],
        out_specs=pl.BlockSpec((None, tr, cols), lambda i, w_ref: (w_ref[1], i, 0)))
    return pl.pallas_call(body, grid_spec=grid_spec, out_shape=jax.ShapeDtypeStruct((2, rh, cols), F32),
                          name=name, compiler_params=_params())(where, sums, recv)


def _adamw(w, g, m, v, name):
    rows, cols = w.shape
    tr = _row_tile(rows)
    assert rows % tr == 0

    def body(w_ref, g_ref, m_ref, v_ref, d_ref, mo_ref, vo_ref):
        gv = g_ref[...]
        m_new = ADAM_B1 * m_ref[...] + (1.0 - ADAM_B1) * gv
        v_new = ADAM_B2 * v_ref[...] + (1.0 - ADAM_B2) * jnp.square(gv)
        m_hat = m_new / (1.0 - ADAM_B1 ** ADAM_STEP)
        v_hat = v_new / (1.0 - ADAM_B2 ** ADAM_STEP)
        d_ref[...] = -ADAM_LR * (m_hat / (jnp.sqrt(v_hat) + ADAM_EPS) + ADAM_WD * w_ref[...])
        mo_ref[...] = m_new
        vo_ref[...] = v_new

    blk = pl.BlockSpec((tr, cols), lambda i: (i, 0))
    shp = jax.ShapeDtypeStruct((rows, cols), F32)
    return pl.pallas_call(body, grid=(rows // tr,), in_specs=[blk] * 4, out_specs=[blk] * 3,
                          out_shape=[shp] * 3, name=name, compiler_params=_params())(w, g, m, v)


def _adamw_halves(w, g2, m, v, name):
    rows, cols = w.shape
    half = cols // 2
    tr = _row_tile(rows)

    def body(w_ref, g_ref, m_ref, v_ref, go_ref, d_ref, mo_ref, vo_ref):
        gv = g_ref[...]
        go_ref[...] = gv
        m_new = ADAM_B1 * m_ref[...] + (1.0 - ADAM_B1) * gv
        v_new = ADAM_B2 * v_ref[...] + (1.0 - ADAM_B2) * jnp.square(gv)
        m_hat = m_new / (1.0 - ADAM_B1 ** ADAM_STEP)
        v_hat = v_new / (1.0 - ADAM_B2 ** ADAM_STEP)
        d_ref[...] = -ADAM_LR * (m_hat / (jnp.sqrt(v_hat) + ADAM_EPS) + ADAM_WD * w_ref[...])
        mo_ref[...] = m_new
        vo_ref[...] = v_new

    blk = pl.BlockSpec((tr, half), lambda hf, i: (i, hf))
    gsp = pl.BlockSpec((None, tr, half), lambda hf, i: (hf, i, 0))
    shp = jax.ShapeDtypeStruct((rows, cols), F32)
    return pl.pallas_call(body, grid=(2, rows // tr), in_specs=[blk, gsp, blk, blk], out_specs=[blk] * 4,
                          out_shape=[shp] * 4, name=name, compiler_params=_params())(w, g2, m, v)


SHARD_W = IN_WIDTH // N_CHIPS


def _half_major(a):
    r, c = a.shape
    return a.reshape(N_CHIPS, 2, r // N_CHIPS // 2, c).transpose(1, 0, 2, 3)


def kernel(x, mem, g_pre, w_in, w_conv, attn_sink, g_mem, w_mem_kv, w_up_a, w_up_b, w_up_m, w_out, g_post, loss_target, m_g_pre, m_w_in, m_w_conv, m_attn_sink, m_g_mem, m_w_mem_kv, m_w_up_a, m_w_up_b, m_w_up_m, m_w_out, m_g_post, v_g_pre, v_w_in, v_w_conv, v_attn_sink, v_g_mem, v_w_mem_kv, v_w_up_a, v_w_up_b, v_w_up_m, v_w_out, v_g_post):
    xi, yi, ci = _position()
    chip = 2 * xi + yi
    where = jnp.stack([chip, ci, N_CHIPS - 1 - chip]).astype(jnp.int32)

    own = [w_in[0].T.astype(BF16), w_mem_kv[0].astype(BF16),
           jnp.concatenate([w_up_a[0], w_up_b[0], w_up_m[0]], axis=0).astype(BF16), w_out[0].astype(BF16)]
    own_conv = jnp.pad(w_conv[0], ((0, 5), (0, 0)))

    def pieces(mine, got):
        got = lax.dynamic_update_slice_in_dim(got, mine[None], chip, axis=0)
        return [got[j] for j in range(N_CHIPS)]

    diag = N_CHIPS - 1 - chip
    diag_blocks = SHARD_BLOCKS + 1
    got_near, got_conv = _run_carry(_gather_weights(own[:1], own_conv, relations=(0, 1)), "gather_w_in")
    w_near = lax.dynamic_update_slice_in_dim(got_near, own[0][None], chip, axis=0).reshape(IN_WIDTH, D_MODEL)
    h = _rmsnorm_fwd(x[0], g_pre, name="pre_norm")
    proj, (got_far,) = _proj(
        h, w_near, n_blocks=N_IN_BLOCKS - diag_blocks, where=where, name="proj_near",
        block_of=lambda i, w: i + diag_blocks * (i >= SHARD_BLOCKS * w[2]).astype(jnp.int32),
        carry=_gather_weights(own[:1], relations=(2,)))
    far = lax.dynamic_index_in_dim(got_far, diag, 0, keepdims=False)
    w_far = lax.dynamic_update_slice_in_dim(
        lax.dynamic_slice_in_dim(w_near, SHARD_BLOCKS * IN_BLOCK * diag, diag_blocks * IN_BLOCK, axis=0),
        far, (SHARD_W - SHARD_BLOCKS * IN_BLOCK) * diag, axis=0)
    proj, gathered = _proj(h, w_far, n_blocks=diag_blocks, where=where, name="proj_far", into=proj,
                           block_of=lambda i, w: i + SHARD_BLOCKS * w[2], first_row_block=lambda w: SHARD_BLOCKS * w[2],
                           carry=_gather_weights(own[1:]))
    w_in_t = lax.dynamic_update_slice_in_dim(w_near, far, SHARD_W * diag, axis=0)
    w_kv_full = jnp.concatenate(pieces(own[1], gathered[0]), axis=0)
    up_pieces = pieces(own[2], gathered[1])
    w_up_full = jnp.stack([jnp.concatenate([p[k * A_WIDTH:(k + 1) * A_WIDTH] for p in up_pieces], axis=1)
                           for k in range(3)])
    w_out_full = jnp.concatenate(pieces(own[3], gathered[2]), axis=0)
    w_conv_full = jnp.concatenate([p[:3] for p in pieces(own_conv, got_conv)], axis=1)

    g = _forward_backward(x[0], mem[0], loss_target[0], proj, w_conv_full, attn_sink, g_mem, w_kv_full, w_up_full,
                          w_out_full, g_post)

    half_rows = D_MODEL // 2
    up_parts = (g["w_up"].reshape(3, A_WIDTH, N_CHIPS, D_MODEL // N_CHIPS).transpose(2, 0, 1, 3)
                .reshape(N_CHIPS, 2, 3 * A_WIDTH // 2, D_MODEL // N_CHIPS).transpose(1, 0, 2, 3)).astype(BF16)
    small_parts = [_half_major(g["w_kv"]).astype(BF16), up_parts, _half_major(g["w_out"]).astype(BF16)]

    def dw_in_half(hf, name, carry=None):
        h_half = lax.dynamic_slice_in_dim(h, hf * half_rows, half_rows, axis=1)
        out = _dw_in_t(g["dproj"], h_half, name=name, carry=carry)
        if carry is None:
            return out.reshape(N_CHIPS, SHARD_W, half_rows)
        return out[0].reshape(N_CHIPS, SHARD_W, half_rows), out[1]

    def pick(parts, hf):
        return [lax.dynamic_index_in_dim(p, hf, 0, keepdims=False) for p in parts]

    small_names = ["w_kv", "w_up", "w_out"]
    recv_small = _run_carry(_pair_exchange(pick(small_parts, 1 - ci)), "pair_exchange_small")
    sums_small = [_pair_add(k, r, "pair_add_" + nm)
                  for k, r, nm in zip(pick(small_parts, ci), recv_small, small_names)]
    dw_send, recv3_small = dw_in_half(1 - ci, "dw_in_send", _chip_exchange(sums_small))
    dw_keep, (recv_in,) = dw_in_half(ci, "dw_in_keep", _pair_exchange([dw_send]))
    sum_in = _pair_add(dw_keep, recv_in, "pair_add_w_in")
    d_h, (recv3_in,) = _d_h(g["dproj"], w_in_t, carry=_chip_exchange([sum_in]))
    pairs = [_chip_add(s, r, where, "chip_add_" + nm)
             for s, r, nm in zip([sum_in] + sums_small, [recv3_in] + recv3_small, ["w_in"] + small_names)]
    grad_x, dg_pre = _rmsnorm_bwd(d_h, x[0], g_pre, g["dy"], name="pre_norm_bwd")

    zeros512 = jnp.zeros((1, D_MODEL - A_WIDTH), F32)
    conv_rows = [jnp.concatenate([g["w_conv"][k:k + 1], zeros512], axis=1) for k in range(3)]
    sink_row = jnp.pad(g["sink"][:, 0].reshape(1, N_Q_HEADS), ((0, 0), (0, D_MODEL - N_Q_HEADS)))
    loss_row = jnp.pad(g["loss"], ((0, 0), (0, D_MODEL - LANES)))
    pack = jnp.concatenate([dg_pre, g["g_mem"], g["g_post"]] + conv_rows + [sink_row, loss_row], axis=0)
    red, full = _small_allreduce(pack, _pair_share(pairs))
    loss = red[7, 0]
    small_grads = dict(
        g_pre=red[0:1], g_mem=red[1:2], g_post=red[2:3], attn_sink=red[6:7, :N_Q_HEADS],
        w_conv=lax.dynamic_slice(red[3:6, :A_WIDTH], (0, chip * LANES), (3, LANES)))

    gw_up = full[2].reshape(3, A_WIDTH, D_MODEL // N_CHIPS)
    grads = dict(small_grads, w_mem_kv=full[1].reshape(D_MODEL // N_CHIPS, 2 * MEM_WIDTH),
                 w_up_a=gw_up[0], w_up_b=gw_up[1], w_up_m=gw_up[2],
                 w_out=full[3].reshape(D_MODEL // N_CHIPS, D_MODEL))

    weights = dict(g_pre=g_pre, w_in=w_in, w_conv=w_conv, attn_sink=attn_sink, g_mem=g_mem, w_mem_kv=w_mem_kv,
                   w_up_a=w_up_a, w_up_b=w_up_b, w_up_m=w_up_m, w_out=w_out, g_post=g_post)
    m_in = dict(g_pre=m_g_pre, w_in=m_w_in, w_conv=m_w_conv, attn_sink=m_attn_sink, g_mem=m_g_mem,
                w_mem_kv=m_w_mem_kv, w_up_a=m_w_up_a, w_up_b=m_w_up_b, w_up_m=m_w_up_m, w_out=m_w_out,
                g_post=m_g_post)
    v_in = dict(g_pre=v_g_pre, w_in=v_w_in, w_conv=v_w_conv, attn_sink=v_attn_sink, g_mem=v_g_mem,
                w_mem_kv=v_w_mem_kv, w_up_a=v_w_up_a, w_up_b=v_w_up_b, w_up_m=v_w_up_m, w_out=v_w_out,
                g_post=v_g_post)
    out_g, out_d, out_m, out_v = [], [], [], []
    for nm in ("g_pre", "w_in", "w_conv", "attn_sink", "g_mem", "w_mem_kv", "w_up_a", "w_up_b", "w_up_m", "w_out",
               "g_post"):
        shape = weights[nm].shape
        if nm == "w_in":
            results = _adamw_halves(w_in[0].T, full[0], m_w_in[0].T, v_w_in[0].T, "adamw_w_in")
            for out, t in zip((out_g, out_d, out_m, out_v), results):
                out.append(t.T.reshape(shape))
            continue
        two_d = shape[-2:]
        gr = grads[nm].reshape(two_d)
        d, m_new, v_new = _adamw(weights[nm].reshape(two_d), gr, m_in[nm].reshape(two_d), v_in[nm].reshape(two_d),
                                 "adamw_" + nm)
        out_g.append(gr.reshape(shape))
        out_d.append(d.reshape(shape))
        out_m.append(m_new.reshape(shape))
        out_v.append(v_new.reshape(shape))
    return (loss, grad_x.reshape(x.shape), *out_g, *out_d, *out_m, *out_v)
```

```python
import functools

import jax
import jax.numpy as jnp
from jax import lax
from jax.experimental import pallas as pl
from jax.experimental.pallas import tpu as pltpu

F32 = jnp.float32
BF16 = jnp.bfloat16
MESH = pl.DeviceIdType.MESH

D_MODEL = 1024
EPS = 1e-6
A_WIDTH = 512
HEAD_DIM = 64
N_Q_HEADS = 8
WINDOW_BLOCK = 128
KV_PAD = 512
ROPE_THETA = 500000.0
ROT_DIM = 16
MEM_HEADS = 4
MEM_HEAD_DIM = 128
MEM_WIDTH = 512
IN_WIDTH = 7424
N_CHIPS = 4
LANES = 128
HALF_LANES = 64

PERM_SEGS = ((0, 2560), (2816, 3328), (4352, 7424), (3328, 4352), (2560, 2816))
UNPERM_SEGS = ((0, 2560), (7168, 7424), (2560, 3072), (6144, 7168), (3072, 6144))
COL_A, W_A = 0, 2048
COL_B, W_B = 2, 1024
COL_G, W_G = 1, 3072
COL_M, W_M = 6, 1024
COL_KV, W_KV = 28, 256

ADAM_LR = 0.001
ADAM_B1 = 0.9
ADAM_B2 = 0.999
ADAM_EPS = 1e-08
ADAM_WD = 0.01
ADAM_STEP = 10

VMEM_LIMIT_BYTES = 48 * 1024 * 1024


_HBM = pl.BlockSpec(memory_space=pltpu.HBM)


def _params(**kw):
    return pltpu.CompilerParams(vmem_limit_bytes=VMEM_LIMIT_BYTES, **kw)


def _sigmoid(v):
    return jax.nn.sigmoid(v)


_DIMS = {"nn": (((1,), (0,)), ((), ())), "nt": (((1,), (1,)), ((), ())), "tn": (((0,), (0,)), ((), ()))}


class _Carry:
    def __init__(self, ins, out_shapes, sems, start, finish, aliases=None):
        self.ins, self.out_shapes, self.sems = list(ins), list(out_shapes), list(sems)
        self.start, self.finish, self.aliases = start, finish, dict(aliases or {})


def _join(*carries):
    def split(seq, counts):
        pos, parts = 0, []
        for n in counts:
            parts.append(seq[pos:pos + n])
            pos += n
        return parts

    n_in = [len(c.ins) for c in carries]
    n_out = [len(c.out_shapes) for c in carries]
    n_sem = [len(c.sems) for c in carries]

    def run(which):
        def go(ins, outs, sems):
            for c, i, o, sm in zip(carries, split(ins, n_in), split(outs, n_out), split(sems, n_sem)):
                getattr(c, which)(i, o, sm)
        return go

    aliases = {}
    for k, c in enumerate(carries):
        aliases.update({sum(n_in[:k]) + i: sum(n_out[:k]) + o for i, o in c.aliases.items()})
    return _Carry([a for c in carries for a in c.ins], [sh for c in carries for sh in c.out_shapes],
                  [sm for c in carries for sm in c.sems], run("start"), run("finish"), aliases)


def _carried_call(body, carry, *, grid, in_specs, out_specs, out_shape, scratch, operands, name, prefetch=None,
                  aliases=None):
    n_in, n_out, n_scr = len(in_specs), len(out_specs), len(scratch)
    c_in = len(carry.ins) if carry else 0
    c_out = len(carry.out_shapes) if carry else 0
    n_pre = 0 if prefetch is None else 1
    steps = 1
    for g in grid:
        steps *= g

    def wrapped(*refs):
        refs = refs[n_pre:]
        ins, cins = refs[:n_in], refs[n_in:n_in + c_in]
        outs = refs[n_in + c_in:n_in + c_in + n_out]
        couts = refs[n_in + c_in + n_out:n_in + c_in + n_out + c_out]
        rest = refs[n_in + c_in + n_out + c_out:]
        scr, sems = rest[:n_scr], rest[n_scr:]
        if carry:
            step = pl.program_id(0)
            for ax in range(1, len(grid)):
                step = step * grid[ax] + pl.program_id(ax)

            @pl.when(step == 0)
            def _():
                carry.start(cins, couts, sems)

        body(ins, outs, scr)
        if carry:
            @pl.when(step == steps - 1)
            def _():
                carry.finish(cins, couts, sems)

    all_aliases = {n_pre + i: o for i, o in (aliases or {}).items()}
    if carry:
        all_aliases.update({n_pre + n_in + i: n_out + o for i, o in carry.aliases.items()})
    all_in = list(in_specs) + [_HBM] * c_in
    all_out = list(out_specs) + [_HBM] * c_out
    all_scratch = list(scratch) + (carry.sems if carry else [])
    if n_pre:
        spec = dict(grid_spec=pltpu.PrefetchScalarGridSpec(num_scalar_prefetch=1, grid=grid, in_specs=all_in,
                                                           out_specs=all_out, scratch_shapes=all_scratch))
        pre = (prefetch,)
    else:
        spec = dict(grid=grid, in_specs=all_in, out_specs=all_out, scratch_shapes=all_scratch)
        pre = ()
    results = pl.pallas_call(
        wrapped, out_shape=list(out_shape) + (carry.out_shapes if carry else []), input_output_aliases=all_aliases,
        name=name, compiler_params=_params(), **spec)(*pre, *operands, *(carry.ins if carry else []))
    return list(results[:n_out]), list(results[n_out:])


def _matmul(a, b, *, mode, out_dtype, tm, tn, tk, name, j_outer=False, carry=None):
    if mode == "nn":
        (m, k), (_, n) = a.shape, b.shape
    elif mode == "nt":
        (m, k), (n, _) = a.shape, b.shape
    else:
        (k, m), (_, n) = a.shape, b.shape
    tm, tn, tk = min(tm, m), min(tn, n), min(tk, k)
    assert m % tm == 0 and n % tn == 0 and k % tk == 0
    ni, nj, nk = m // tm, n // tn, k // tk
    dims = _DIMS[mode]

    def ij(g0, g1):
        return (g1, g0) if j_outer else (g0, g1)

    if mode == "nn":
        a_spec = pl.BlockSpec((tm, tk), lambda g0, g1, kk: (ij(g0, g1)[0], kk))
        b_spec = pl.BlockSpec((tk, tn), lambda g0, g1, kk: (kk, ij(g0, g1)[1]))
    elif mode == "nt":
        a_spec = pl.BlockSpec((tm, tk), lambda g0, g1, kk: (ij(g0, g1)[0], kk))
        b_spec = pl.BlockSpec((tn, tk), lambda g0, g1, kk: (ij(g0, g1)[1], kk))
    else:
        a_spec = pl.BlockSpec((tk, tm), lambda g0, g1, kk: (kk, ij(g0, g1)[0]))
        b_spec = pl.BlockSpec((tk, tn), lambda g0, g1, kk: (kk, ij(g0, g1)[1]))
    o_spec = pl.BlockSpec((tm, tn), lambda g0, g1, kk: ij(g0, g1))

    def part(a_ref, b_ref):
        return lax.dot_general(a_ref[...].astype(BF16), b_ref[...].astype(BF16), dims,
                               preferred_element_type=F32)

    if nk == 1:
        def body(ins, outs, scr):
            outs[0][...] = part(*ins).astype(out_dtype)
        scratch = []
    else:
        def body(ins, outs, scr):
            kk = pl.program_id(2)
            acc_ref = scr[0]

            @pl.when(kk == 0)
            def _():
                acc_ref[...] = part(*ins)

            @pl.when(kk > 0)
            def _():
                acc_ref[...] += part(*ins)

            @pl.when(kk == nk - 1)
            def _():
                outs[0][...] = acc_ref[...].astype(out_dtype)
        scratch = [pltpu.VMEM((tm, tn), F32)]

    grid = (nj, ni, nk) if j_outer else (ni, nj, nk)
    (out,), carried = _carried_call(
        body, carry, grid=grid, in_specs=[a_spec, b_spec], out_specs=[o_spec],
        out_shape=[jax.ShapeDtypeStruct((m, n), out_dtype)], scratch=scratch, operands=(a, b), name=name)
    return (out, carried) if carry else out


IN_BLOCK = 256
N_IN_BLOCKS = IN_WIDTH // IN_BLOCK
SHARD_BLOCKS = (IN_WIDTH // N_CHIPS) // IN_BLOCK
BLOCK_RUNS = tuple((a // IN_BLOCK, sum(d - c for c, d in PERM_SEGS[:k]) // IN_BLOCK, (b - a) // IN_BLOCK)
                   for k, (a, b) in enumerate(PERM_SEGS))


def _perm_block(r):
    p = r
    for ref0, perm0, n in BLOCK_RUNS:
        p = jnp.where((r >= ref0) & (r < ref0 + n), r - ref0 + perm0, p)
    return p


def _proj(h, w_rows, *, n_blocks, block_of, where, name, first_row_block=None, into=None, carry=None):
    s, d = h.shape

    def body(ins, outs, scr):
        outs[0][...] = lax.dot_general(ins[0][...], ins[1][...], _DIMS["nt"], preferred_element_type=F32)

    def row_block(i, w):
        return block_of(i, w) - (0 if first_row_block is None else first_row_block(w))

    in_specs = [pl.BlockSpec((s, d), lambda i, w: (0, 0)), pl.BlockSpec((IN_BLOCK, d), lambda i, w: (row_block(i, w), 0))]
    operands = (h, w_rows)
    if into is not None:
        in_specs.append(pl.BlockSpec(memory_space=pl.ANY))
        operands += (into,)
    (proj,), carried = _carried_call(
        body, carry, grid=(n_blocks,), in_specs=in_specs,
        out_specs=[pl.BlockSpec((s, IN_BLOCK), lambda i, w: (0, _perm_block(block_of(i, w))))],
        out_shape=[jax.ShapeDtypeStruct((s, IN_WIDTH), F32)], scratch=[], operands=operands, name=name,
        prefetch=where, aliases=None if into is None else {2: 0})
    return (proj, carried) if carry else proj


def _dw_in_t(dproj, h_part, *, name, carry=None):
    s, c = h_part.shape

    def body(ins, outs, scr):
        outs[0][...] = lax.dot_general(ins[0][...], ins[1][...], _DIMS["tn"],
                                       preferred_element_type=F32).astype(BF16)

    (dw,), carried = _carried_call(
        body, carry, grid=(N_IN_BLOCKS,),
        in_specs=[pl.BlockSpec((s, IN_BLOCK), lambda r: (0, _perm_block(r))), pl.BlockSpec((s, c), lambda r: (0, 0))],
        out_specs=[pl.BlockSpec((IN_BLOCK, c), lambda r: (r, 0))],
        out_shape=[jax.ShapeDtypeStruct((IN_WIDTH, c), BF16)], scratch=[], operands=(dproj, h_part), name=name)
    return (dw, carried) if carry else dw


def _d_h(dproj, w_near, far, where, *, carry=None):
    s = dproj.shape[0]
    d = w_near.shape[1]
    tm = min(s, 256)

    def body(ins, outs, scr):
        where_ref, a_ref, w_hbm, far_hbm = ins
        w_ref, sem = scr

        @pl.when(pl.program_id(0) == 0)
        def _():
            whole = pltpu.make_async_copy(w_hbm, w_ref, sem)
            whole.start()
            whole.wait()
            rows = pl.ds(pl.multiple_of(where_ref[2] * SHARD_W, BF16_SUBLANES), SHARD_W)
            part = pltpu.make_async_copy(far_hbm, w_ref.at[rows], sem)
            part.start()
            part.wait()

        acc = None
        for ref0, perm0, n in BLOCK_RUNS:
            term = jnp.dot(a_ref[:, perm0 * IN_BLOCK:(perm0 + n) * IN_BLOCK],
                           w_ref[ref0 * IN_BLOCK:(ref0 + n) * IN_BLOCK, :], preferred_element_type=F32)
            acc = term if acc is None else acc + term
        outs[0][...] = acc

    anysp = pl.BlockSpec(memory_space=pl.ANY)
    (dh,), carried = _carried_call(
        body, carry, grid=(s // tm,),
        in_specs=[pl.BlockSpec(memory_space=pltpu.SMEM), pl.BlockSpec((tm, IN_WIDTH), lambda i: (i, 0)), anysp, anysp],
        out_specs=[pl.BlockSpec((tm, d), lambda i: (i, 0))],
        out_shape=[jax.ShapeDtypeStruct((s, d), F32)],
        scratch=[pltpu.VMEM((IN_WIDTH, d), BF16), pltpu.SemaphoreType.DMA],
        operands=(where, dproj, w_near, far), name="d_h")
    return (dh, carried) if carry else dh


def _rmsnorm_fwd(x, g, *, name):
    s, d = x.shape
    ts = min(512, s)

    def body(x_ref, g_ref, o_ref):
        xv = x_ref[...]
        r = lax.rsqrt(jnp.mean(xv * xv, axis=-1, keepdims=True) + EPS)
        o_ref[...] = ((xv * r) * g_ref[...]).astype(BF16)

    return pl.pallas_call(
        body, grid=(s // ts,),
        in_specs=[pl.BlockSpec((ts, d), lambda i: (i, 0)), pl.BlockSpec((1, d), lambda i: (0, 0))],
        out_specs=pl.BlockSpec((ts, d), lambda i: (i, 0)),
        out_shape=jax.ShapeDtypeStruct((s, d), BF16), name=name, compiler_params=_params())(x, g)


def _rmsnorm_bwd(dh, x, g, res, *, name, carry=None):
    s, d = x.shape
    ts = min(256, s)

    def body(ins, outs, scr):
        dh_ref, x_ref, g_ref, res_ref = ins
        dx_ref, dg_ref = outs
        xv = x_ref[...]
        r = lax.rsqrt(jnp.mean(xv * xv, axis=-1, keepdims=True) + EPS)
        xh = xv * r
        dhv = dh_ref[...]
        part = jnp.sum(dhv * xh, axis=0, keepdims=True)

        @pl.when(pl.program_id(0) == 0)
        def _():
            dg_ref[...] = part

        @pl.when(pl.program_id(0) > 0)
        def _():
            dg_ref[...] += part

        dxh = dhv * g_ref[...]
        dx_ref[...] = res_ref[...] + r * (dxh - xh * jnp.mean(dxh * xh, axis=-1, keepdims=True))

    row = pl.BlockSpec((ts, d), lambda i: (i, 0))
    vec = pl.BlockSpec((1, d), lambda i: (0, 0))
    outs, carried = _carried_call(
        body, carry, grid=(s // ts,), in_specs=[row, row, vec, row], out_specs=[row, vec],
        out_shape=[jax.ShapeDtypeStruct((s, d), F32), jax.ShapeDtypeStruct((1, d), F32)],
        scratch=[], operands=(dh, x, g, res), name=name)
    return (*outs, carried) if carry else tuple(outs)


MID_TILE = 256


def _gated_branches(y_refs, wup_ref, gl):
    d = D_MODEL
    us = [jnp.dot(y_refs[k][...], wup_ref[k], preferred_element_type=F32) for k in range(3)]
    sg = [_sigmoid(gl[:, k * d:(k + 1) * d]) for k in range(3)]
    return us, sg


def _mid_fwd(ya, yb, ym, proj, x, tgt, w_up, w_out, g_post):
    s, d = x.shape
    ts = MID_TILE

    def body(ya_ref, yb_ref, ym_ref, g_ref, x_ref, t_ref, wup_ref, wout_ref, gp_ref,
             m_ref, do_ref, dy_ref, dg_ref, loss_ref):
        us, sg = _gated_branches((ya_ref, yb_ref, ym_ref), wup_ref, g_ref[...])
        merged = (sg[0] * us[0] + sg[1] * us[1] + sg[2] * us[2]).astype(BF16)
        m_ref[...] = merged
        ov = jnp.dot(merged, wout_ref[...], preferred_element_type=F32)
        r = lax.rsqrt(jnp.mean(ov * ov, axis=-1, keepdims=True) + EPS)
        nh = ov * r
        gv = gp_ref[...]
        e = (x_ref[...] + nh * gv) - t_ref[...]
        lpart = 0.5 * jnp.sum(jnp.mean(e * e, axis=-1, keepdims=True), axis=0, keepdims=True)
        dy = e * (1.0 / d)
        dgp = jnp.sum(dy * nh, axis=0, keepdims=True)

        @pl.when(pl.program_id(0) == 0)
        def _():
            dg_ref[...] = dgp
            loss_ref[...] = jnp.broadcast_to(lpart, loss_ref.shape)

        @pl.when(pl.program_id(0) > 0)
        def _():
            dg_ref[...] += dgp
            loss_ref[...] += jnp.broadcast_to(lpart, loss_ref.shape)

        dn = dy * gv
        dy_ref[...] = dy
        do_ref[...] = (r * (dn - nh * jnp.mean(dn * nh, axis=-1, keepdims=True))).astype(BF16)

    row = pl.BlockSpec((ts, d), lambda i: (i, 0))
    ysp = pl.BlockSpec((ts, A_WIDTH), lambda i: (i, 0))
    vec = pl.BlockSpec((1, d), lambda i: (0, 0))
    return pl.pallas_call(
        body, grid=(s // ts,),
        in_specs=[ysp, ysp, ysp, pl.BlockSpec((ts, W_G), lambda i: (i, COL_G)), row, row,
                  pl.BlockSpec((3, A_WIDTH, d), lambda i: (0, 0, 0)), pl.BlockSpec((d, d), lambda i: (0, 0)), vec],
        out_specs=[row, row, row, vec, pl.BlockSpec((1, LANES), lambda i: (0, 0))],
        out_shape=[jax.ShapeDtypeStruct((s, d), BF16), jax.ShapeDtypeStruct((s, d), BF16),
                   jax.ShapeDtypeStruct((s, d), F32), jax.ShapeDtypeStruct((1, d), F32),
                   jax.ShapeDtypeStruct((1, LANES), F32)],
        name="mid_fwd", compiler_params=_params())(ya, yb, ym, proj, x, tgt, w_up, w_out, g_post)


def _mid_bwd(d_out, merged, ya, yb, ym, proj, w_up, w_out):
    s, d = merged.shape
    ts = MID_TILE
    last = s // ts - 1

    def body(do_ref, m_ref, ya_ref, yb_ref, ym_ref, g_ref, wup_ref, wout_ref,
             dp_ref, dya_ref, dyb_ref, dym_ref, dwup_hbm, dwout_hbm, dwup_acc, dwout_acc):
        i = pl.program_id(0)

        @pl.when(i == 0)
        def _():
            dwup_acc[...] = jnp.zeros_like(dwup_acc)
            dwout_acc[...] = jnp.zeros_like(dwout_acc)

        y_refs = (ya_ref, yb_ref, ym_ref)
        us, sg = _gated_branches(y_refs, wup_ref, g_ref[...])
        dov = do_ref[...]
        dwout_acc[...] += lax.dot_general(m_ref[...], dov, _DIMS["tn"], preferred_element_type=F32)
        dm = lax.dot_general(dov, wout_ref[...], _DIMS["nt"], preferred_element_type=F32)
        for k, dy_ref in enumerate((dya_ref, dyb_ref, dym_ref)):
            dp_ref[:, k * d:(k + 1) * d] = ((dm * us[k]) * (sg[k] * (1.0 - sg[k]))).astype(BF16)
            du = (sg[k] * dm).astype(BF16)
            dy_ref[...] = lax.dot_general(du, wup_ref[k], _DIMS["nt"], preferred_element_type=F32)
            dwup_acc[k] += lax.dot_general(y_refs[k][...], du, _DIMS["tn"], preferred_element_type=F32)

        @pl.when(i == last)
        def _():
            pltpu.sync_copy(dwup_acc, dwup_hbm)
            pltpu.sync_copy(dwout_acc, dwout_hbm)

    row = pl.BlockSpec((ts, d), lambda i: (i, 0))
    ysp = pl.BlockSpec((ts, A_WIDTH), lambda i: (i, 0))
    gsp = pl.BlockSpec((ts, W_G), lambda i: (i, COL_G))
    anysp = pl.BlockSpec(memory_space=pl.ANY)
    yshape = jax.ShapeDtypeStruct((s, A_WIDTH), F32)
    return pl.pallas_call(
        body, grid=(s // ts,),
        in_specs=[row, row, ysp, ysp, ysp, gsp, pl.BlockSpec((3, A_WIDTH, d), lambda i: (0, 0, 0)),
                  pl.BlockSpec((d, d), lambda i: (0, 0))],
        out_specs=[gsp, ysp, ysp, ysp, anysp, anysp],
        out_shape=[jax.ShapeDtypeStruct((s, IN_WIDTH), BF16), yshape, yshape, yshape,
                   jax.ShapeDtypeStruct((3, A_WIDTH, d), F32), jax.ShapeDtypeStruct((d, d), F32)],
        scratch_shapes=[pltpu.VMEM((3, A_WIDTH, d), F32), pltpu.VMEM((d, d), F32)],
        name="mid_bwd", compiler_params=_params())(d_out, merged, ya, yb, ym, proj, w_up, w_out)


def _conv_core(blk, prev, nxt, w, i, last, ts):
    c = A_WIDTH
    ab, ac, ax, az = blk[:, :c], blk[:, c:2 * c], blk[:, 2 * c:3 * c], blk[:, 3 * c:]
    cu = ac * ax
    cu_prev = (prev[7:8, c:2 * c] * prev[7:8, 2 * c:3 * c]) * jnp.where(i > 0, 1.0, 0.0)
    cu_next = (nxt[0:1, c:2 * c] * nxt[0:1, 2 * c:3 * c]) * jnp.where(i < last, 1.0, 0.0)
    row = lax.broadcasted_iota(jnp.int32, (ts, c), 0)
    cm1 = jnp.where(row == 0, cu_prev, pltpu.roll(cu, 1, 0))
    cp1 = jnp.where(row == ts - 1, cu_next, pltpu.roll(cu, ts - 1, 0))
    yc = cm1 * w[0:1] + cu * w[1:2] + cp1 * w[2:3]
    return ab, ac, ax, az, cu, cm1, cp1, yc, row


def _halo_specs(ts, width, col, nblk8):
    prev = pl.BlockSpec((8, width), lambda i: (jnp.maximum(i * (ts // 8) - 1, 0), col))
    nxt = pl.BlockSpec((8, width), lambda i: (jnp.minimum((i + 1) * (ts // 8), nblk8 - 1), col))
    return prev, nxt


def _conv_fwd(proj, w_conv):
    s = proj.shape[0]
    ts = 256
    last = s // ts - 1

    def body(a_ref, ap_ref, an_ref, w_ref, ya_ref):
        i = pl.program_id(0)
        ab, _, _, az, _, _, _, yc, _ = _conv_core(a_ref[...], ap_ref[...], an_ref[...], w_ref[...], i, last, ts)
        ya_ref[...] = ((ab * yc) * (az * _sigmoid(az))).astype(BF16)

    prev, nxt = _halo_specs(ts, W_A, COL_A, s // 8)
    return pl.pallas_call(
        body, grid=(s // ts,),
        in_specs=[pl.BlockSpec((ts, W_A), lambda i: (i, COL_A)), prev, nxt,
                  pl.BlockSpec((3, A_WIDTH), lambda i: (0, 0))],
        out_specs=pl.BlockSpec((ts, A_WIDTH), lambda i: (i, 0)),
        out_shape=jax.ShapeDtypeStruct((s, A_WIDTH), BF16), name="conv_fwd",
        compiler_params=_params())(proj, proj, proj, w_conv)


def _conv_bwd(proj, w_conv, dya, dproj):
    s = proj.shape[0]
    ts = 256
    last = s // ts - 1
    c = A_WIDTH

    def body(a_ref, ap_ref, an_ref, w_ref, d_ref, dp_ref, dn_ref, _, dproj_ref, dw_ref):
        i = pl.program_id(0)
        w = w_ref[...]
        prev, nxt = ap_ref[...], an_ref[...]
        ab, ac, ax, az, cu, cm1, cp1, yc, row = _conv_core(a_ref[...], prev, nxt, w, i, last, ts)
        sg = _sigmoid(az)
        sz = az * sg
        dya_v = d_ref[...]
        dyc = dya_v * sz * ab
        dproj_ref[:, :c] = (dya_v * sz * yc).astype(BF16)
        dproj_ref[:, 3 * c:] = (dya_v * (ab * yc) * (sg * (1.0 + az * (1.0 - sg)))).astype(BF16)

        def halo_dyc(a_row, d_row):
            azr = a_row[:, 3 * c:]
            return d_row * (azr * _sigmoid(azr)) * a_row[:, :c]

        dyc_prev = halo_dyc(prev[7:8], dp_ref[...][7:8]) * jnp.where(i > 0, 1.0, 0.0)
        dyc_next = halo_dyc(nxt[0:1], dn_ref[...][0:1]) * jnp.where(i < last, 1.0, 0.0)
        dyc_m1 = jnp.where(row == 0, dyc_prev, pltpu.roll(dyc, 1, 0))
        dyc_p1 = jnp.where(row == ts - 1, dyc_next, pltpu.roll(dyc, ts - 1, 0))
        dcu = dyc_p1 * w[0:1] + dyc * w[1:2] + dyc_m1 * w[2:3]
        dproj_ref[:, c:2 * c] = (dcu * ax).astype(BF16)
        dproj_ref[:, 2 * c:3 * c] = (dcu * ac).astype(BF16)
        dw = [jnp.sum(dyc * t, axis=0, keepdims=True) for t in (cm1, cu, cp1)]

        @pl.when(i == 0)
        def _():
            for k in range(3):
                dw_ref[k:k + 1, :] = dw[k]

        @pl.when(i > 0)
        def _():
            for k in range(3):
                dw_ref[k:k + 1, :] += dw[k]

    prev, nxt = _halo_specs(ts, W_A, COL_A, s // 8)
    dprev, dnxt = _halo_specs(ts, A_WIDTH, 0, s // 8)
    return pl.pallas_call(
        body, grid=(s // ts,),
        in_specs=[pl.BlockSpec((ts, W_A), lambda i: (i, COL_A)), prev, nxt,
                  pl.BlockSpec((3, A_WIDTH), lambda i: (0, 0)),
                  pl.BlockSpec((ts, A_WIDTH), lambda i: (i, 0)), dprev, dnxt,
                  pl.BlockSpec(memory_space=pl.ANY)],
        out_specs=[pl.BlockSpec((ts, W_A), lambda i: (i, COL_A)), pl.BlockSpec((3, A_WIDTH), lambda i: (0, 0))],
        out_shape=[jax.ShapeDtypeStruct(dproj.shape, BF16), jax.ShapeDtypeStruct((3, A_WIDTH), F32)],
        input_output_aliases={7: 0}, name="conv_bwd",
        compiler_params=_params())(proj, proj, proj, w_conv, dya, dya, dya, dproj)


def _rope_tables(s):
    half = ROT_DIM // 2
    dim = jnp.arange(LANES) % HEAD_DIM
    inv_freq = jnp.power(jnp.float32(ROPE_THETA), -(dim % half).astype(F32) * (2.0 / ROT_DIM))
    ang = jnp.arange(s).astype(F32)[:, None] * inv_freq[None, :]
    cos, sin = jnp.cos(ang), jnp.sin(ang)
    first, second = (dim < half)[None, :], ((dim >= half) & (dim < ROT_DIM))[None, :]
    c = jnp.where(first | second, cos, 1.0)
    s1 = jnp.where(first, -sin, 0.0)
    s2 = jnp.where(second, sin, 0.0)
    return jnp.concatenate([c, s1, s2], axis=1)


def _rope(t, tab):
    return (t * tab[:, :LANES] + pltpu.roll(t, LANES - 8, 1) * tab[:, LANES:2 * LANES]
            + pltpu.roll(t, 8, 1) * tab[:, 2 * LANES:])


def _rope_transpose(dt, tab):
    return (dt * tab[:, :LANES] + pltpu.roll(dt * tab[:, LANES:2 * LANES], 8, 1)
            + pltpu.roll(dt * tab[:, 2 * LANES:], LANES - 8, 1))


def _rope_kv(proj, tab):
    s = proj.shape[0]
    nb = s // KV_PAD

    def body(kv_ref, t_ref, k_ref, v_ref):
        j = pl.program_id(0)
        inside = jnp.where((j > 0) & (j <= nb), 1.0, 0.0)
        kv = kv_ref[...]
        k_ref[...] = (_rope(kv[:, :LANES], t_ref[...]) * inside).astype(BF16)
        v_ref[...] = (kv[:, LANES:] * inside).astype(BF16)

    def src(j):
        return jnp.clip(j - 1, 0, nb - 1)

    o_spec = pl.BlockSpec((KV_PAD, LANES), lambda j: (j, 0))
    shp = jax.ShapeDtypeStruct((s + 2 * KV_PAD, LANES), BF16)
    return pl.pallas_call(
        body, grid=(nb + 2,),
        in_specs=[pl.BlockSpec((KV_PAD, W_KV), lambda j: (src(j), COL_KV)),
                  pl.BlockSpec((KV_PAD, 3 * LANES), lambda j: (src(j), 0))],
        out_specs=[o_spec, o_spec], out_shape=[shp, shp], name="rope_kv",
        compiler_params=_params())(proj, tab)


def _rope_kv_bwd(dkpad, dvpad, tab, dproj):
    s = tab.shape[0]
    nb = s // KV_PAD

    def body(dk_ref, dv_ref, t_ref, _, dp_ref):
        dp_ref[:, :LANES] = _rope_transpose(dk_ref[...], t_ref[...]).astype(BF16)
        dp_ref[:, LANES:] = dv_ref[...].astype(BF16)

    pad_spec = pl.BlockSpec((KV_PAD, LANES), lambda j: (j + 1, 0))
    return pl.pallas_call(
        body, grid=(nb,),
        in_specs=[pad_spec, pad_spec, pl.BlockSpec((KV_PAD, 3 * LANES), lambda j: (j, 0)),
                  pl.BlockSpec(memory_space=pl.ANY)],
        out_specs=pl.BlockSpec((KV_PAD, W_KV), lambda j: (j, COL_KV)),
        out_shape=jax.ShapeDtypeStruct(dproj.shape, BF16), input_output_aliases={3: 0},
        name="rope_kv_bwd", compiler_params=_params())(dkpad, dvpad, tab, dproj)


def _window_start(n):
    return pl.multiple_of((n - 1) * WINDOW_BLOCK + KV_PAD, WINDOW_BLOCK)


def _window_operands(k_ref, v_ref, n, lo):
    start = _window_start(n)
    kw = k_ref[pl.ds(start, 3 * WINDOW_BLOCK), :].astype(F32)
    vw = v_ref[pl.ds(start, 3 * WINDOW_BLOCK), :].astype(F32)
    kr, vr = pltpu.roll(kw, HALF_LANES, 1), pltpu.roll(vw, HALF_LANES, 1)
    k2 = (jnp.where(lo, kw, kr).astype(BF16), jnp.where(lo, kr, kw).astype(BF16))
    v2 = (jnp.where(lo, vw, vr).astype(BF16), jnp.where(lo, vr, vw).astype(BF16))
    return k2, v2


HEADS_PER_GROUP = 4
SWA_FWD_BLOCKS = 1
SWA_BWD_BLOCKS = 2


def _window_mask(n, s):
    wb = WINDOW_BLOCK
    shape = (HEADS_PER_GROUP * wb, 3 * wb)
    qi = lax.broadcasted_iota(jnp.int32, shape, 0) & (wb - 1)
    kj = lax.broadcasted_iota(jnp.int32, shape, 1)
    kpos = kj + (n - 1) * wb
    return (kj >= qi) & (kj <= qi + 2 * wb) & (kpos >= 0) & (kpos < s)


def _stack_heads(pair0, pair1, lo):
    return jnp.concatenate([jnp.where(lo, pair0, 0.0), jnp.where(lo, 0.0, pair0),
                            jnp.where(lo, pair1, 0.0), jnp.where(lo, 0.0, pair1)], axis=0)


def _unstack_pair(stacked, i, lo):
    wb = WINDOW_BLOCK
    return jnp.where(lo, stacked[2 * i * wb:(2 * i + 1) * wb], stacked[(2 * i + 1) * wb:(2 * i + 2) * wb])


def _sink_column(sink_ref, g):
    wb = WINDOW_BLOCK
    return jnp.concatenate([jnp.full((wb, 1), sink_ref[0, HEADS_PER_GROUP * g + i], F32)
                            for i in range(HEADS_PER_GROUP)], axis=0)


def _head_exp(q4, k2g, valid, sink):
    sc = lax.dot_general(q4, k2g, _DIMS["nt"], preferred_element_type=F32) * (HEAD_DIM ** -0.5)
    sc = jnp.where(valid, sc, -jnp.inf)
    m = jnp.maximum(jnp.max(sc, axis=1, keepdims=True), sink)
    return jnp.exp(sc - m).astype(BF16), jnp.exp(sink - m)


def _swa_fwd(proj, kpad, vpad, tab, sink):
    s = proj.shape[0]
    wb = WINDOW_BLOCK

    def body(b_ref, k_ref, v_ref, t_ref, sink_ref, o_ref, y_ref):
        lo = lax.broadcasted_iota(jnp.int32, (wb, LANES), 1) < HALF_LANES
        lo_w = lax.broadcasted_iota(jnp.int32, (3 * wb, LANES), 1) < HALF_LANES
        for sub in range(SWA_FWD_BLOCKS):
            n = pl.program_id(0) * SWA_FWD_BLOCKS + sub
            rows = slice(sub * wb, (sub + 1) * wb)
            k2, v2 = _window_operands(k_ref, v_ref, n, lo_w)
            valid = _window_mask(n, s)
            tab_v = t_ref[rows, :]
            ones = jnp.ones((3 * wb, LANES), BF16)
            for g in range(2):
                qr = [_rope(b_ref[rows, (2 * g + i) * LANES:(2 * g + i + 1) * LANES], tab_v) for i in range(2)]
                q4 = _stack_heads(qr[0], qr[1], lo).astype(BF16)
                e, es = _head_exp(q4, k2[g], valid, _sink_column(sink_ref, g))
                ox = jnp.dot(e, jnp.concatenate([v2[g], ones], axis=1), preferred_element_type=F32)
                o4 = ox[:, :LANES] * (1.0 / (ox[:, LANES:] + es))
                for i in range(2):
                    cols = slice((2 * g + i) * LANES, (2 * g + i + 1) * LANES)
                    op = _unstack_pair(o4, i, lo)
                    o_ref[rows, cols] = op
                    zp = b_ref[rows, A_WIDTH + cols.start:A_WIDTH + cols.stop]
                    y_ref[rows, cols] = (op * (zp * _sigmoid(zp))).astype(BF16)

    tq = SWA_FWD_BLOCKS * wb
    pad_spec = pl.BlockSpec((s + 2 * KV_PAD, LANES), lambda n: (0, 0))
    o_spec = pl.BlockSpec((tq, A_WIDTH), lambda n: (n, 0))
    return pl.pallas_call(
        body, grid=(s // tq,),
        in_specs=[pl.BlockSpec((tq, W_B), lambda n: (n, COL_B)), pad_spec, pad_spec,
                  pl.BlockSpec((tq, 3 * LANES), lambda n: (n, 0)),
                  pl.BlockSpec(memory_space=pltpu.SMEM)],
        out_specs=[o_spec, o_spec],
        out_shape=[jax.ShapeDtypeStruct((s, A_WIDTH), F32), jax.ShapeDtypeStruct((s, A_WIDTH), BF16)],
        name="swa_fwd", compiler_params=_params())(proj, kpad, vpad, tab, sink)


def _swa_bwd(proj, kpad, vpad, tab, sink, o_attn, dyb, dproj):
    s = proj.shape[0]
    wb = WINDOW_BLOCK
    scale = HEAD_DIM ** -0.5

    def body(b_ref, k_ref, v_ref, t_ref, sink_ref, o_ref, dy_ref, _, dp_ref, dk_ref, dv_ref, ds_ref):
        @pl.when(pl.program_id(0) == 0)
        def _():
            dk_ref[...] = jnp.zeros_like(dk_ref)
            dv_ref[...] = jnp.zeros_like(dv_ref)
            ds_ref[...] = jnp.zeros_like(ds_ref)

        lo = lax.broadcasted_iota(jnp.int32, (wb, LANES), 1) < HALF_LANES
        lo_w = lax.broadcasted_iota(jnp.int32, (3 * wb, LANES), 1) < HALF_LANES
        for sub in range(SWA_BWD_BLOCKS):
            n = pl.program_id(0) * SWA_BWD_BLOCKS + sub
            rows = slice(sub * wb, (sub + 1) * wb)
            k2, v2 = _window_operands(k_ref, v_ref, n, lo_w)
            valid = _window_mask(n, s)
            tab_v = t_ref[rows, :]
            ones = jnp.ones((3 * wb, LANES), BF16)
            dks, dvs = [], []
            for g in range(2):
                qr, op, do = [], [], []
                for i in range(2):
                    cols = slice((2 * g + i) * LANES, (2 * g + i + 1) * LANES)
                    zcols = slice(A_WIDTH + cols.start, A_WIDTH + cols.stop)
                    qr.append(_rope(b_ref[rows, cols], tab_v))
                    zp = b_ref[rows, zcols]
                    sg = _sigmoid(zp)
                    op.append(o_ref[rows, cols])
                    dyp = dy_ref[rows, cols]
                    do.append(dyp * (zp * sg))
                    dp_ref[rows, zcols] = (dyp * op[i] * (sg * (1.0 + zp * (1.0 - sg)))).astype(BF16)
                q4 = _stack_heads(qr[0], qr[1], lo).astype(BF16)
                do4 = _stack_heads(do[0], do[1], lo)
                o4 = jnp.concatenate([op[0], op[0], op[1], op[1]], axis=0)
                e, es = _head_exp(q4, k2[g], valid, _sink_column(sink_ref, g))
                inv = 1.0 / (jnp.dot(e, ones, preferred_element_type=F32) + es)
                prob = e.astype(F32) * jnp.concatenate([inv, inv, inv], axis=1)
                delta = jnp.sum(do4 * o4, axis=1, keepdims=True)
                do4b = do4.astype(BF16)
                dprob = lax.dot_general(do4b, v2[g], _DIMS["nt"], preferred_element_type=F32)
                dsc = (prob * (dprob - delta)).astype(BF16)
                sink_terms = (es * inv[:, :1]) * delta
                for i in range(HEADS_PER_GROUP):
                    h = HEADS_PER_GROUP * g + i
                    dsink = -jnp.sum(sink_terms[i * wb:(i + 1) * wb], axis=0, keepdims=True)
                    ds_ref[h:h + 1, :] += jnp.broadcast_to(dsink, (1, LANES))
                dq4 = jnp.dot(dsc, k2[g], preferred_element_type=F32) * scale
                for i in range(2):
                    cols = slice((2 * g + i) * LANES, (2 * g + i + 1) * LANES)
                    dp_ref[rows, cols] = _rope_transpose(_unstack_pair(dq4, i, lo), tab_v).astype(BF16)
                dk2 = lax.dot_general(dsc, q4, _DIMS["tn"], preferred_element_type=F32) * scale
                dv2 = lax.dot_general(prob.astype(BF16), do4b, _DIMS["tn"], preferred_element_type=F32)
                dks.append(dk2 + pltpu.roll(dk2, HALF_LANES, 1))
                dvs.append(dv2 + pltpu.roll(dv2, HALF_LANES, 1))
            start = _window_start(n)
            dk_ref[pl.ds(start, 3 * wb), :] += jnp.where(lo_w, dks[0], dks[1])
            dv_ref[pl.ds(start, 3 * wb), :] += jnp.where(lo_w, dvs[0], dvs[1])

    tq = SWA_BWD_BLOCKS * wb
    pad_spec = pl.BlockSpec((s + 2 * KV_PAD, LANES), lambda n: (0, 0))
    blk = pl.BlockSpec((tq, A_WIDTH), lambda n: (n, 0))
    bsp = pl.BlockSpec((tq, W_B), lambda n: (n, COL_B))
    pad_shape = jax.ShapeDtypeStruct((s + 2 * KV_PAD, LANES), F32)
    return pl.pallas_call(
        body, grid=(s // tq,),
        in_specs=[bsp, pad_spec, pad_spec, pl.BlockSpec((tq, 3 * LANES), lambda n: (n, 0)),
                  pl.BlockSpec(memory_space=pltpu.SMEM), blk, blk, pl.BlockSpec(memory_space=pl.ANY)],
        out_specs=[bsp, pad_spec, pad_spec, pl.BlockSpec((8, LANES), lambda n: (0, 0))],
        out_shape=[jax.ShapeDtypeStruct(dproj.shape, BF16), pad_shape, pad_shape,
                   jax.ShapeDtypeStruct((8, LANES), F32)],
        input_output_aliases={7: 0}, name="swa_bwd",
        compiler_params=_params())(proj, kpad, vpad, tab, sink, o_attn, dyb, dproj)


def _mem_exp(qh, mk):
    sc = lax.dot_general(qh, mk, _DIMS["nt"], preferred_element_type=F32) * (MEM_HEAD_DIM ** -0.5)
    return jnp.exp(sc - jnp.max(sc, axis=1, keepdims=True)).astype(BF16)


def _mem_fwd(proj, mkv):
    s = proj.shape[0]
    ts = 512
    mlen = mkv.shape[0]

    def body(m_ref, kv_ref, o_ref, y_ref):
        ones = jnp.ones((mlen, LANES), BF16)
        for h in range(MEM_HEADS):
            cols = slice(h * LANES, (h + 1) * LANES)
            mk = kv_ref[:, cols].astype(BF16)
            mv = kv_ref[:, MEM_WIDTH + h * LANES:MEM_WIDTH + (h + 1) * LANES].astype(BF16)
            e = _mem_exp(m_ref[:, cols].astype(BF16), mk)
            ox = jnp.dot(e, jnp.concatenate([mv, ones], axis=1), preferred_element_type=F32)
            oh = ox[:, :LANES] * (1.0 / ox[:, LANES:])
            o_ref[:, cols] = oh
            zh = m_ref[:, MEM_WIDTH + h * LANES:MEM_WIDTH + (h + 1) * LANES]
            y_ref[:, cols] = (oh * (zh * _sigmoid(zh))).astype(BF16)

    o_spec = pl.BlockSpec((ts, MEM_WIDTH), lambda i: (i, 0))
    return pl.pallas_call(
        body, grid=(s // ts,),
        in_specs=[pl.BlockSpec((ts, W_M), lambda i: (i, COL_M)),
                  pl.BlockSpec((mlen, 2 * MEM_WIDTH), lambda i: (0, 0))],
        out_specs=[o_spec, o_spec],
        out_shape=[jax.ShapeDtypeStruct((s, MEM_WIDTH), F32), jax.ShapeDtypeStruct((s, MEM_WIDTH), BF16)],
        name="mem_fwd", compiler_params=_params())(proj, mkv)


def _mem_bwd(proj, mkv, o_mem, dym, dproj):
    s = proj.shape[0]
    ts = 512
    mlen = mkv.shape[0]
    scale = MEM_HEAD_DIM ** -0.5

    def body(m_ref, kv_ref, o_ref, dy_ref, _, dp_ref, dkv_ref):
        @pl.when(pl.program_id(0) == 0)
        def _():
            dkv_ref[...] = jnp.zeros_like(dkv_ref)

        ones = jnp.ones((mlen, LANES), BF16)
        for h in range(MEM_HEADS):
            cols = slice(h * LANES, (h + 1) * LANES)
            vcols = slice(MEM_WIDTH + h * LANES, MEM_WIDTH + (h + 1) * LANES)
            mk = kv_ref[:, cols].astype(BF16)
            mv = kv_ref[:, vcols].astype(BF16)
            qh = m_ref[:, cols].astype(BF16)
            zh = m_ref[:, vcols]
            sg = _sigmoid(zh)
            oh = o_ref[:, cols]
            dyh = dy_ref[:, cols]
            doh = dyh * (zh * sg)
            dp_ref[:, vcols] = (dyh * oh * (sg * (1.0 + zh * (1.0 - sg)))).astype(BF16)
            e = _mem_exp(qh, mk)
            inv = 1.0 / jnp.dot(e, ones, preferred_element_type=F32)
            prob = e.astype(F32) * jnp.concatenate([inv] * (mlen // LANES), axis=1)
            delta = jnp.sum(doh * oh, axis=1, keepdims=True)
            dohb = doh.astype(BF16)
            dprob = lax.dot_general(dohb, mv, _DIMS["nt"], preferred_element_type=F32)
            dsc = (prob * (dprob - delta)).astype(BF16)
            dp_ref[:, cols] = (jnp.dot(dsc, mk, preferred_element_type=F32) * scale).astype(BF16)
            dkv_ref[:, cols] += lax.dot_general(dsc, qh, _DIMS["tn"], preferred_element_type=F32) * scale
            dkv_ref[:, vcols] += lax.dot_general(prob.astype(BF16), dohb, _DIMS["tn"],
                                                 preferred_element_type=F32)

    blk = pl.BlockSpec((ts, MEM_WIDTH), lambda i: (i, 0))
    msp = pl.BlockSpec((ts, W_M), lambda i: (i, COL_M))
    kvsp = pl.BlockSpec((mlen, 2 * MEM_WIDTH), lambda i: (0, 0))
    return pl.pallas_call(
        body, grid=(s // ts,),
        in_specs=[msp, kvsp, blk, blk, pl.BlockSpec(memory_space=pl.ANY)],
        out_specs=[msp, kvsp],
        out_shape=[jax.ShapeDtypeStruct(dproj.shape, BF16), jax.ShapeDtypeStruct(mkv.shape, F32)],
        input_output_aliases={4: 0}, name="mem_bwd",
        compiler_params=_params())(proj, mkv, o_mem, dym, dproj)


def _forward_backward(x, mem, tgt, proj, w_conv, sink, g_mem, w_kv, w_up, w_out, g_post):
    s = x.shape[0]
    tab = _rope_tables(s)

    ya = _conv_fwd(proj, w_conv)
    kpad, vpad = _rope_kv(proj, tab)
    o_attn, yb = _swa_fwd(proj, kpad, vpad, tab, sink)
    mn = _rmsnorm_fwd(mem, g_mem, name="mem_norm")
    mkv = _matmul(mn, w_kv, mode="nn", out_dtype=F32, tm=256, tn=1024, tk=D_MODEL, name="mem_kv")
    o_mem, ym = _mem_fwd(proj, mkv)
    merged, d_out, dy, dg_post, loss = _mid_fwd(ya, yb, ym, proj, x, tgt, w_up, w_out, g_post)
    dproj, d_ya, d_yb, d_ym, dw_up, dw_out = _mid_bwd(d_out, merged, ya, yb, ym, proj, w_up, w_out)

    dproj, dw_conv = _conv_bwd(proj, w_conv, d_ya, dproj)
    dproj, dkpad, dvpad, dsink = _swa_bwd(proj, kpad, vpad, tab, sink, o_attn, d_yb, dproj)
    dproj = _rope_kv_bwd(dkpad, dvpad, tab, dproj)
    dproj, d_mkv = _mem_bwd(proj, mkv, o_mem, d_ym, dproj)

    dw_kv = _matmul(mn, d_mkv, mode="tn", out_dtype=F32, tm=1024, tn=1024, tk=256, name="dw_kv")
    d_mn = _matmul(d_mkv, w_kv, mode="nt", out_dtype=F32, tm=256, tn=1024, tk=D_MODEL, name="d_mn")
    _, dg_mem = _rmsnorm_bwd(d_mn, mem, g_mem, d_mn, name="mem_norm_bwd")

    return dict(loss=loss, dproj=dproj, dy=dy, w_conv=dw_conv, sink=dsink, g_mem=dg_mem,
                w_kv=dw_kv, w_up=dw_up, w_out=dw_out, g_post=dg_post)


N_DEV = 8


def _position():
    return lax.axis_index("x"), lax.axis_index("y"), lax.axis_index("c")


def _other_chips(x, y):
    return (((1 - x, y), 2 * (1 - x) + y), ((x, 1 - y), 2 * x + (1 - y)), ((1 - x, 1 - y), 2 * (1 - x) + (1 - y)))


def _remote(src, dst, send_sems, recv_sems, k, device):
    return pltpu.make_async_remote_copy(src_ref=src, dst_ref=dst, send_sem=send_sems.at[k], recv_sem=recv_sems.at[k],
                                        device_id=device, device_id_type=MESH)


def _rows_half(ref, hf):
    rh = ref.shape[0] // 2
    return ref.at[pl.ds(pl.multiple_of(hf * rh, 8), rh)]


def _gather_weights(shards, small=None, relations=(0, 1, 2), into=None):
    n = len(shards)
    k = 0 if small is None else 1

    def peers(x, y):
        return [(r, chip, idx) for r, (chip, idx) in enumerate(_other_chips(x, y)) if r in relations]

    def ici(ins, outs, sems, a, r, chip, src_chip, c):
        return _remote(_rows_half(ins[a], c), _rows_half(outs[a].at[src_chip], c), sems[0], sems[1], 3 * a + r,
                       (*chip, c))

    def whole(ins, outs, sems, r, chip, src_chip, c):
        return _remote(ins[n], outs[n].at[src_chip], sems[0], sems[1], 3 * n + r, (*chip, c))

    def d2d(outs, sems, a, r, idx, hf, x, y, c):
        half = _rows_half(outs[a].at[idx], hf)
        return _remote(half, half, sems[2], sems[3], 3 * a + r, (x, y, 1 - c))

    def start(ins, outs, sems):
        x, y, c = _position()
        me = 2 * x + y
        for a in range(n):
            for r, chip, _ in peers(x, y):
                ici(ins, outs, sems, a, r, chip, me, c).start()
        for r, (chip, _) in enumerate(_other_chips(x, y)):
            if k:
                whole(ins, outs, sems, r, chip, me, c).start()

    def finish(ins, outs, sems):
        x, y, c = _position()
        me = 2 * x + y
        for a in range(n):
            for r, chip, idx in peers(x, y):
                ici(ins, outs, sems, a, r, chip, idx, c).wait_recv()
                d2d(outs, sems, a, r, idx, c, x, y, c).start()
        for a in range(n):
            for r, chip, idx in peers(x, y):
                d2d(outs, sems, a, r, idx, 1 - c, x, y, c).wait_recv()
        for r, (chip, idx) in enumerate(_other_chips(x, y)):
            if k:
                whole(ins, outs, sems, r, chip, idx, c).wait_recv()
                whole(ins, outs, sems, r, chip, me, c).wait_send()
        for a in range(n):
            for r, chip, idx in peers(x, y):
                ici(ins, outs, sems, a, r, chip, me, c).wait_send()
                d2d(outs, sems, a, r, idx, c, x, y, c).wait_send()

    operands = list(shards) + ([small] if k else [])
    shapes = [jax.ShapeDtypeStruct((N_CHIPS,) + s.shape, s.dtype) for s in operands]
    aliases = {}
    if into is not None:
        assert len(into) == len(operands)
        aliases = {len(operands) + a: a for a in range(len(into))}
        operands += list(into)
    return _Carry(operands, shapes,
                  [pltpu.SemaphoreType.DMA((3 * (n + k),)), pltpu.SemaphoreType.DMA((3 * (n + k),)),
                   pltpu.SemaphoreType.DMA((3 * n,)), pltpu.SemaphoreType.DMA((3 * n,))], start, finish, aliases)


def _run_carry(carry, name):
    _, results = _carried_call(lambda ins, outs, scr: None, carry, grid=(1,), in_specs=[], out_specs=[],
                               out_shape=[], scratch=[], operands=(), name=name)
    return results


def _pair_exchange(send):
    n = len(send)

    def copies(ins, outs, sems):
        x, y, c = _position()
        return [_remote(ins[a], outs[a], sems[0], sems[1], a, (x, y, 1 - c)) for a in range(n)]

    def start(ins, outs, sems):
        for cp in copies(ins, outs, sems):
            cp.start()

    def finish(ins, outs, sems):
        for cp in copies(ins, outs, sems):
            cp.wait()

    return _Carry(send, [jax.ShapeDtypeStruct(p.shape, p.dtype) for p in send],
                  [pltpu.SemaphoreType.DMA((n,)), pltpu.SemaphoreType.DMA((n,))], start, finish)


def _chip_exchange(sums):
    n = len(sums)

    def copies(ins, outs, sems):
        x, y, c = _position()
        return [_remote(ins[a].at[idx], outs[a].at[r], sems[0], sems[1], 3 * a + r, (*chip, c))
                for a in range(n) for r, (chip, idx) in enumerate(_other_chips(x, y))]

    def start(ins, outs, sems):
        for cp in copies(ins, outs, sems):
            cp.start()

    def finish(ins, outs, sems):
        for cp in copies(ins, outs, sems):
            cp.wait()

    return _Carry(sums, [jax.ShapeDtypeStruct((3,) + p.shape[1:], p.dtype) for p in sums],
                  [pltpu.SemaphoreType.DMA((3 * n,)), pltpu.SemaphoreType.DMA((3 * n,))], start, finish)


def _pair_share(pairs):
    n = len(pairs)

    def start(ins, outs, sems):
        x, y, c = _position()
        for a in range(n):
            _remote(outs[a].at[c], outs[a].at[c], sems[0], sems[1], a, (x, y, 1 - c)).start()

    def finish(ins, outs, sems):
        x, y, c = _position()
        for a in range(n):
            _remote(outs[a].at[1 - c], outs[a].at[1 - c], sems[0], sems[1], a, (x, y, 1 - c)).wait_recv()
        for a in range(n):
            _remote(outs[a].at[c], outs[a].at[c], sems[0], sems[1], a, (x, y, 1 - c)).wait_send()

    return _Carry(pairs, [jax.ShapeDtypeStruct(p.shape, p.dtype) for p in pairs],
                  [pltpu.SemaphoreType.DMA((n,)), pltpu.SemaphoreType.DMA((n,))], start, finish,
                  aliases={a: a for a in range(n)})


def _small_allreduce(pack, share):
    rows, width = pack.shape
    n_share = len(share.ins)

    def body(p_ref, *refs):
        share_in, o_ref, share_out = refs[:n_share], refs[n_share], refs[n_share + 1:2 * n_share + 1]
        buf, send_sems, recv_sems = refs[2 * n_share + 1:2 * n_share + 4]
        share_sems = refs[2 * n_share + 4:]
        share.start(share_in, share_out, share_sems)
        x, y, c = _position()
        me = 4 * x + 2 * y + c
        buf[me] = p_ref[...]
        peers = []
        for r in range(1, N_DEV):
            fx, fy, fc = (r >> 2) & 1, (r >> 1) & 1, r & 1
            px, py, pc = (1 - x if fx else x), (1 - y if fy else y), (1 - c if fc else c)
            peers.append(((px, py, pc), 4 * px + 2 * py + pc))
        sends = [_remote(p_ref, buf.at[me], send_sems, recv_sems, r, dev) for r, (dev, _) in enumerate(peers)]
        for cp in sends:
            cp.start()
        for r, (dev, idx) in enumerate(peers):
            _remote(p_ref, buf.at[idx], send_sems, recv_sems, r, dev).wait_recv()
        for cp in sends:
            cp.wait_send()
        acc = buf[0]
        for k in range(1, N_DEV):
            acc = acc + buf[k]
        o_ref[...] = acc
        share.finish(share_in, share_out, share_sems)

    vm = pl.BlockSpec(memory_space=pltpu.VMEM)
    red, *shared = pl.pallas_call(
        body, in_specs=[vm] + [_HBM] * n_share, out_specs=[vm] + [_HBM] * n_share,
        out_shape=[jax.ShapeDtypeStruct(pack.shape, F32)] + share.out_shapes,
        scratch_shapes=[pltpu.VMEM((N_DEV, rows, width), F32), pltpu.SemaphoreType.DMA((N_DEV - 1,)),
                        pltpu.SemaphoreType.DMA((N_DEV - 1,))] + share.sems,
        input_output_aliases={1 + i: 1 + o for i, o in share.aliases.items()},
        name="small_allreduce")(pack, *share.ins)
    return red, shared


ROW_TILE_MAX = 512
BF16_SUBLANES = 16


def _row_tile(rows):
    if rows <= ROW_TILE_MAX:
        return rows
    return max(t for t in range(BF16_SUBLANES, ROW_TILE_MAX + 1, BF16_SUBLANES) if rows % t == 0)


def _pair_add(keep, recv, name):
    nj, rh, cols = keep.shape
    tr = _row_tile(rh)

    def body(k_ref, r_ref, o_ref):
        o_ref[...] = (k_ref[...].astype(F32) + r_ref[...].astype(F32)).astype(BF16)

    blk = pl.BlockSpec((None, tr, cols), lambda j, i: (j, i, 0))
    return pl.pallas_call(body, grid=(nj, rh // tr), in_specs=[blk, blk], out_specs=blk,
                          out_shape=jax.ShapeDtypeStruct(keep.shape, BF16), name=name,
                          compiler_params=_params())(keep, recv)


def _chip_add(sums, recv, where, name):
    _, rh, cols = sums.shape
    tr = _row_tile(rh)

    def body(w_ref, s_ref, r_ref, o_ref):
        o_ref[...] = ((s_ref[...].astype(F32) + r_ref[0].astype(F32)) + r_ref[1].astype(F32)) + r_ref[2].astype(F32)

    grid_spec = pltpu.PrefetchScalarGridSpec(
        num_scalar_prefetch=1, grid=(rh // tr,),
        in_specs=[pl.BlockSpec((None, tr, cols), lambda i, w_ref: (w_ref[0], i, 0)),
                  pl.BlockSpec((3, tr, cols), lambda i, w_ref: (0, i, 0))],
        out_specs=pl.BlockSpec((None, tr, cols), lambda i, w_ref: (w_ref[1], i, 0)))
    return pl.pallas_call(body, grid_spec=grid_spec, out_shape=jax.ShapeDtypeStruct((2, rh, cols), F32),
                          name=name, compiler_params=_params())(where, sums, recv)


def _adamw(w, g, m, v, name):
    rows, cols = w.shape
    tr = _row_tile(rows)
    assert rows % tr == 0

    def body(w_ref, g_ref, m_ref, v_ref, d_ref, mo_ref, vo_ref):
        gv = g_ref[...]
        m_new = ADAM_B1 * m_ref[...] + (1.0 - ADAM_B1) * gv
        v_new = ADAM_B2 * v_ref[...] + (1.0 - ADAM_B2) * jnp.square(gv)
        m_hat = m_new / (1.0 - ADAM_B1 ** ADAM_STEP)
        v_hat = v_new / (1.0 - ADAM_B2 ** ADAM_STEP)
        d_ref[...] = -ADAM_LR * (m_hat / (jnp.sqrt(v_hat) + ADAM_EPS) + ADAM_WD * w_ref[...])
        mo_ref[...] = m_new
        vo_ref[...] = v_new

    blk = pl.BlockSpec((tr, cols), lambda i: (i, 0))
    shp = jax.ShapeDtypeStruct((rows, cols), F32)
    return pl.pallas_call(body, grid=(rows // tr,), in_specs=[blk] * 4, out_specs=[blk] * 3,
                          out_shape=[shp] * 3, name=name, compiler_params=_params())(w, g, m, v)


def _adamw_halves(w, g2, m, v, name):
    rows, cols = w.shape
    half = cols // 2
    tr = _row_tile(rows)

    def body(w_ref, g_ref, m_ref, v_ref, go_ref, d_ref, mo_ref, vo_ref):
        gv = g_ref[...]
        go_ref[...] = gv
        m_new = ADAM_B1 * m_ref[...] + (1.0 - ADAM_B1) * gv
        v_new = ADAM_B2 * v_ref[...] + (1.0 - ADAM_B2) * jnp.square(gv)
        m_hat = m_new / (1.0 - ADAM_B1 ** ADAM_STEP)
        v_hat = v_new / (1.0 - ADAM_B2 ** ADAM_STEP)
        d_ref[...] = -ADAM_LR * (m_hat / (jnp.sqrt(v_hat) + ADAM_EPS) + ADAM_WD * w_ref[...])
        mo_ref[...] = m_new
        vo_ref[...] = v_new

    blk = pl.BlockSpec((tr, half), lambda hf, i: (i, hf))
    gsp = pl.BlockSpec((None, tr, half), lambda hf, i: (hf, i, 0))
    shp = jax.ShapeDtypeStruct((rows, cols), F32)
    return pl.pallas_call(body, grid=(2, rows // tr), in_specs=[blk, gsp, blk, blk], out_specs=[blk] * 4,
                          out_shape=[shp] * 4, name=name, compiler_params=_params())(w, g2, m, v)


SHARD_W = IN_WIDTH // N_CHIPS


def _half_major(a):
    r, c = a.shape
    return a.reshape(N_CHIPS, 2, r // N_CHIPS // 2, c).transpose(1, 0, 2, 3)


def kernel(x, mem, g_pre, w_in, w_conv, attn_sink, g_mem, w_mem_kv, w_up_a, w_up_b, w_up_m, w_out, g_post, loss_target, m_g_pre, m_w_in, m_w_conv, m_attn_sink, m_g_mem, m_w_mem_kv, m_w_up_a, m_w_up_b, m_w_up_m, m_w_out, m_g_post, v_g_pre, v_w_in, v_w_conv, v_attn_sink, v_g_mem, v_w_mem_kv, v_w_up_a, v_w_up_b, v_w_up_m, v_w_out, v_g_post):
    xi, yi, ci = _position()
    chip = 2 * xi + yi
    where = jnp.stack([chip, ci, N_CHIPS - 1 - chip]).astype(jnp.int32)

    own = [w_in[0].T.astype(BF16), w_mem_kv[0].astype(BF16),
           jnp.concatenate([w_up_a[0], w_up_b[0], w_up_m[0]], axis=0).astype(BF16), w_out[0].astype(BF16)]
    own_conv = jnp.pad(w_conv[0], ((0, 5), (0, 0)))

    def pieces(mine, got):
        got = lax.dynamic_update_slice_in_dim(got, mine[None], chip, axis=0)
        return [got[j] for j in range(N_CHIPS)]

    diag = N_CHIPS - 1 - chip
    diag_blocks = SHARD_BLOCKS + 1
    got_near, got_conv = _run_carry(_gather_weights(own[:1], own_conv, relations=(0, 1)), "gather_w_in")
    w_near = lax.dynamic_update_slice_in_dim(got_near, own[0][None], chip, axis=0).reshape(IN_WIDTH, D_MODEL)
    h = _rmsnorm_fwd(x[0], g_pre, name="pre_norm")
    proj, (got_far, *got_rest) = _proj(
        h, w_near, n_blocks=N_IN_BLOCKS - diag_blocks, where=where, name="proj_near",
        block_of=lambda i, w: i + diag_blocks * (i >= SHARD_BLOCKS * w[2]).astype(jnp.int32),
        carry=_join(_gather_weights(own[:1], relations=(2,)), _gather_weights(own[1:], relations=(0, 1))))
    far = lax.dynamic_index_in_dim(got_far, diag, 0, keepdims=False)
    w_far = lax.dynamic_update_slice_in_dim(
        lax.dynamic_slice_in_dim(w_near, SHARD_BLOCKS * IN_BLOCK * diag, diag_blocks * IN_BLOCK, axis=0),
        far, (SHARD_W - SHARD_BLOCKS * IN_BLOCK) * diag, axis=0)
    proj, gathered = _proj(h, w_far, n_blocks=diag_blocks, where=where, name="proj_far", into=proj,
                           block_of=lambda i, w: i + SHARD_BLOCKS * w[2], first_row_block=lambda w: SHARD_BLOCKS * w[2],
                           carry=_gather_weights(own[1:], relations=(2,), into=got_rest))
    w_kv_full = jnp.concatenate(pieces(own[1], gathered[0]), axis=0)
    up_pieces = pieces(own[2], gathered[1])
    w_up_full = jnp.stack([jnp.concatenate([p[k * A_WIDTH:(k + 1) * A_WIDTH] for p in up_pieces], axis=1)
                           for k in range(3)])
    w_out_full = jnp.concatenate(pieces(own[3], gathered[2]), axis=0)
    w_conv_full = jnp.concatenate([p[:3] for p in pieces(own_conv, got_conv)], axis=1)

    g = _forward_backward(x[0], mem[0], loss_target[0], proj, w_conv_full, attn_sink, g_mem, w_kv_full, w_up_full,
                          w_out_full, g_post)

    half_rows = D_MODEL // 2
    up_parts = (g["w_up"].reshape(3, A_WIDTH, N_CHIPS, D_MODEL // N_CHIPS).transpose(2, 0, 1, 3)
                .reshape(N_CHIPS, 2, 3 * A_WIDTH // 2, D_MODEL // N_CHIPS).transpose(1, 0, 2, 3)).astype(BF16)
    small_parts = [_half_major(g["w_kv"]).astype(BF16), up_parts, _half_major(g["w_out"]).astype(BF16)]

    def dw_in_half(hf, name, carry=None):
        h_half = lax.dynamic_slice_in_dim(h, hf * half_rows, half_rows, axis=1)
        out = _dw_in_t(g["dproj"], h_half, name=name, carry=carry)
        if carry is None:
            return out.reshape(N_CHIPS, SHARD_W, half_rows)
        return out[0].reshape(N_CHIPS, SHARD_W, half_rows), out[1]

    def pick(parts, hf):
        return [lax.dynamic_index_in_dim(p, hf, 0, keepdims=False) for p in parts]

    small_names = ["w_kv", "w_up", "w_out"]
    recv_small = _run_carry(_pair_exchange(pick(small_parts, 1 - ci)), "pair_exchange_small")
    sums_small = [_pair_add(k, r, "pair_add_" + nm)
                  for k, r, nm in zip(pick(small_parts, ci), recv_small, small_names)]
    dw_send, recv3_small = dw_in_half(1 - ci, "dw_in_send", _chip_exchange(sums_small))
    dw_keep, (recv_in,) = dw_in_half(ci, "dw_in_keep", _pair_exchange([dw_send]))
    sum_in = _pair_add(dw_keep, recv_in, "pair_add_w_in")
    d_h, (recv3_in,) = _d_h(g["dproj"], w_near, far, where, carry=_chip_exchange([sum_in]))
    pairs = [_chip_add(s, r, where, "chip_add_" + nm)
             for s, r, nm in zip([sum_in] + sums_small, [recv3_in] + recv3_small, ["w_in"] + small_names)]
    grad_x, dg_pre = _rmsnorm_bwd(d_h, x[0], g_pre, g["dy"], name="pre_norm_bwd")

    zeros512 = jnp.zeros((1, D_MODEL - A_WIDTH), F32)
    conv_rows = [jnp.concatenate([g["w_conv"][k:k + 1], zeros512], axis=1) for k in range(3)]
    sink_row = jnp.pad(g["sink"][:, 0].reshape(1, N_Q_HEADS), ((0, 0), (0, D_MODEL - N_Q_HEADS)))
    loss_row = jnp.pad(g["loss"], ((0, 0), (0, D_MODEL - LANES)))
    pack = jnp.concatenate([dg_pre, g["g_mem"], g["g_post"]] + conv_rows + [sink_row, loss_row], axis=0)
    red, full = _small_allreduce(pack, _pair_share(pairs))
    loss = red[7, 0]
    small_grads = dict(
        g_pre=red[0:1], g_mem=red[1:2], g_post=red[2:3], attn_sink=red[6:7, :N_Q_HEADS],
        w_conv=lax.dynamic_slice(red[3:6, :A_WIDTH], (0, chip * LANES), (3, LANES)))

    gw_up = full[2].reshape(3, A_WIDTH, D_MODEL // N_CHIPS)
    grads = dict(small_grads, w_mem_kv=full[1].reshape(D_MODEL // N_CHIPS, 2 * MEM_WIDTH),
                 w_up_a=gw_up[0], w_up_b=gw_up[1], w_up_m=gw_up[2],
                 w_out=full[3].reshape(D_MODEL // N_CHIPS, D_MODEL))

    weights = dict(g_pre=g_pre, w_in=w_in, w_conv=w_conv, attn_sink=attn_sink, g_mem=g_mem, w_mem_kv=w_mem_kv,
                   w_up_a=w_up_a, w_up_b=w_up_b, w_up_m=w_up_m, w_out=w_out, g_post=g_post)
    m_in = dict(g_pre=m_g_pre, w_in=m_w_in, w_conv=m_w_conv, attn_sink=m_attn_sink, g_mem=m_g_mem,
                w_mem_kv=m_w_mem_kv, w_up_a=m_w_up_a, w_up_b=m_w_up_b, w_up_m=m_w_up_m, w_out=m_w_out,
                g_post=m_g_post)
    v_in = dict(g_pre=v_g_pre, w_in=v_w_in, w_conv=v_w_conv, attn_sink=v_attn_sink, g_mem=v_g_mem,
                w_mem_kv=v_w_mem_kv, w_up_a=v_w_up_a, w_up_b=v_w_up_b, w_up_m=v_w_up_m, w_out=v_w_out,
                g_post=v_g_post)
    out_g, out_d, out_m, out_v = [], [], [], []
    for nm in ("g_pre", "w_in", "w_conv", "attn_sink", "g_mem", "w_mem_kv", "w_up_a", "w_up_b", "w_up_m", "w_out",
               "g_post"):
        shape = weights[nm].shape
        if nm == "w_in":
            results = _adamw_halves(w_in[0].T, full[0], m_w_in[0].T, v_w_in[0].T, "adamw_w_in")
            for out, t in zip((out_g, out_d, out_m, out_v), results):
                out.append(t.T.reshape(shape))
            continue
        two_d = shape[-2:]
        gr = grads[nm].reshape(two_d)
        d, m_new, v_new = _adamw(weights[nm].reshape(two_d), gr, m_in[nm].reshape(two_d), v_in[nm].reshape(two_d),
                                 "adamw_" + nm)
        out_g.append(gr.reshape(shape))
        out_d.append(d.reshape(shape))
        out_m.append(m_new.reshape(shape))
        out_v.append(v_new.reshape(shape))
    return (loss, grad_x.reshape(x.shape), *out_g, *out_d, *out_m, *out_v)
```

```python
import functools

import jax
import jax.numpy as jnp
from jax import lax
from jax.experimental import pallas as pl
from jax.experimental.pallas import tpu as pltpu

F32 = jnp.float32
BF16 = jnp.bfloat16
MESH = pl.DeviceIdType.MESH

D_MODEL = 1024
EPS = 1e-6
A_WIDTH = 512
HEAD_DIM = 64
N_Q_HEADS = 8
WINDOW_BLOCK = 128
KV_PAD = 512
ROPE_THETA = 500000.0
ROT_DIM = 16
MEM_HEADS = 4
MEM_HEAD_DIM = 128
MEM_WIDTH = 512
IN_WIDTH = 7424
N_CHIPS = 4
LANES = 128
HALF_LANES = 64

PERM_SEGS = ((0, 2560), (2816, 3328), (4352, 7424), (3328, 4352), (2560, 2816))
UNPERM_SEGS = ((0, 2560), (7168, 7424), (2560, 3072), (6144, 7168), (3072, 6144))
COL_A, W_A = 0, 2048
COL_B, W_B = 2, 1024
COL_G, W_G = 1, 3072
COL_M, W_M = 6, 1024
COL_KV, W_KV = 28, 256

ADAM_LR = 0.001
ADAM_B1 = 0.9
ADAM_B2 = 0.999
ADAM_EPS = 1e-08
ADAM_WD = 0.01
ADAM_STEP = 10

VMEM_LIMIT_BYTES = 48 * 1024 * 1024


_HBM = pl.BlockSpec(memory_space=pltpu.HBM)


def _params(**kw):
    return pltpu.CompilerParams(vmem_limit_bytes=VMEM_LIMIT_BYTES, **kw)


def _sigmoid(v):
    return jax.nn.sigmoid(v)


_DIMS = {"nn": (((1,), (0,)), ((), ())), "nt": (((1,), (1,)), ((), ())), "tn": (((0,), (0,)), ((), ()))}


class _Carry:
    def __init__(self, ins, out_shapes, sems, start, finish, aliases=None):
        self.ins, self.out_shapes, self.sems = list(ins), list(out_shapes), list(sems)
        self.start, self.finish, self.aliases = start, finish, dict(aliases or {})


def _join(*carries):
    def split(seq, counts):
        pos, parts = 0, []
        for n in counts:
            parts.append(seq[pos:pos + n])
            pos += n
        return parts

    n_in = [len(c.ins) for c in carries]
    n_out = [len(c.out_shapes) for c in carries]
    n_sem = [len(c.sems) for c in carries]

    def run(which):
        def go(ins, outs, sems):
            for c, i, o, sm in zip(carries, split(ins, n_in), split(outs, n_out), split(sems, n_sem)):
                getattr(c, which)(i, o, sm)
        return go

    aliases = {}
    for k, c in enumerate(carries):
        aliases.update({sum(n_in[:k]) + i: sum(n_out[:k]) + o for i, o in c.aliases.items()})
    return _Carry([a for c in carries for a in c.ins], [sh for c in carries for sh in c.out_shapes],
                  [sm for c in carries for sm in c.sems], run("start"), run("finish"), aliases)


def _carried_call(body, carry, *, grid, in_specs, out_specs, out_shape, scratch, operands, name, prefetch=None,
                  aliases=None):
    n_in, n_out, n_scr = len(in_specs), len(out_specs), len(scratch)
    c_in = len(carry.ins) if carry else 0
    c_out = len(carry.out_shapes) if carry else 0
    n_pre = 0 if prefetch is None else 1
    steps = 1
    for g in grid:
        steps *= g

    def wrapped(*refs):
        refs = refs[n_pre:]
        ins, cins = refs[:n_in], refs[n_in:n_in + c_in]
        outs = refs[n_in + c_in:n_in + c_in + n_out]
        couts = refs[n_in + c_in + n_out:n_in + c_in + n_out + c_out]
        rest = refs[n_in + c_in + n_out + c_out:]
        scr, sems = rest[:n_scr], rest[n_scr:]
        if carry:
            step = pl.program_id(0)
            for ax in range(1, len(grid)):
                step = step * grid[ax] + pl.program_id(ax)

            @pl.when(step == 0)
            def _():
                carry.start(cins, couts, sems)

        body(ins, outs, scr)
        if carry:
            @pl.when(step == steps - 1)
            def _():
                carry.finish(cins, couts, sems)

    all_aliases = {n_pre + i: o for i, o in (aliases or {}).items()}
    if carry:
        all_aliases.update({n_pre + n_in + i: n_out + o for i, o in carry.aliases.items()})
    all_in = list(in_specs) + [_HBM] * c_in
    all_out = list(out_specs) + [_HBM] * c_out
    all_scratch = list(scratch) + (carry.sems if carry else [])
    if n_pre:
        spec = dict(grid_spec=pltpu.PrefetchScalarGridSpec(num_scalar_prefetch=1, grid=grid, in_specs=all_in,
                                                           out_specs=all_out, scratch_shapes=all_scratch))
        pre = (prefetch,)
    else:
        spec = dict(grid=grid, in_specs=all_in, out_specs=all_out, scratch_shapes=all_scratch)
        pre = ()
    results = pl.pallas_call(
        wrapped, out_shape=list(out_shape) + (carry.out_shapes if carry else []), input_output_aliases=all_aliases,
        name=name, compiler_params=_params(), **spec)(*pre, *operands, *(carry.ins if carry else []))
    return list(results[:n_out]), list(results[n_out:])


def _matmul(a, b, *, mode, out_dtype, tm, tn, tk, name, j_outer=False, carry=None):
    if mode == "nn":
        (m, k), (_, n) = a.shape, b.shape
    elif mode == "nt":
        (m, k), (n, _) = a.shape, b.shape
    else:
        (k, m), (_, n) = a.shape, b.shape
    tm, tn, tk = min(tm, m), min(tn, n), min(tk, k)
    assert m % tm == 0 and n % tn == 0 and k % tk == 0
    ni, nj, nk = m // tm, n // tn, k // tk
    dims = _DIMS[mode]

    def ij(g0, g1):
        return (g1, g0) if j_outer else (g0, g1)

    if mode == "nn":
        a_spec = pl.BlockSpec((tm, tk), lambda g0, g1, kk: (ij(g0, g1)[0], kk))
        b_spec = pl.BlockSpec((tk, tn), lambda g0, g1, kk: (kk, ij(g0, g1)[1]))
    elif mode == "nt":
        a_spec = pl.BlockSpec((tm, tk), lambda g0, g1, kk: (ij(g0, g1)[0], kk))
        b_spec = pl.BlockSpec((tn, tk), lambda g0, g1, kk: (ij(g0, g1)[1], kk))
    else:
        a_spec = pl.BlockSpec((tk, tm), lambda g0, g1, kk: (kk, ij(g0, g1)[0]))
        b_spec = pl.BlockSpec((tk, tn), lambda g0, g1, kk: (kk, ij(g0, g1)[1]))
    o_spec = pl.BlockSpec((tm, tn), lambda g0, g1, kk: ij(g0, g1))

    def part(a_ref, b_ref):
        return lax.dot_general(a_ref[...].astype(BF16), b_ref[...].astype(BF16), dims,
                               preferred_element_type=F32)

    if nk == 1:
        def body(ins, outs, scr):
            outs[0][...] = part(*ins).astype(out_dtype)
        scratch = []
    else:
        def body(ins, outs, scr):
            kk = pl.program_id(2)
            acc_ref = scr[0]

            @pl.when(kk == 0)
            def _():
                acc_ref[...] = part(*ins)

            @pl.when(kk > 0)
            def _():
                acc_ref[...] += part(*ins)

            @pl.when(kk == nk - 1)
            def _():
                outs[0][...] = acc_ref[...].astype(out_dtype)
        scratch = [pltpu.VMEM((tm, tn), F32)]

    grid = (nj, ni, nk) if j_outer else (ni, nj, nk)
    (out,), carried = _carried_call(
        body, carry, grid=grid, in_specs=[a_spec, b_spec], out_specs=[o_spec],
        out_shape=[jax.ShapeDtypeStruct((m, n), out_dtype)], scratch=scratch, operands=(a, b), name=name)
    return (out, carried) if carry else out


IN_BLOCK = 256
N_IN_BLOCKS = IN_WIDTH // IN_BLOCK
SHARD_BLOCKS = (IN_WIDTH // N_CHIPS) // IN_BLOCK
BLOCK_RUNS = tuple((a // IN_BLOCK, sum(d - c for c, d in PERM_SEGS[:k]) // IN_BLOCK, (b - a) // IN_BLOCK)
                   for k, (a, b) in enumerate(PERM_SEGS))


def _perm_block(r):
    p = r
    for ref0, perm0, n in BLOCK_RUNS:
        p = jnp.where((r >= ref0) & (r < ref0 + n), r - ref0 + perm0, p)
    return p


def _proj(h, w_t, *, n_blocks, block_of, where, name, carry=None):
    s, d = h.shape

    def body(ins, outs, scr):
        outs[0][...] = lax.dot_general(ins[0][...], ins[1][...], _DIMS["nt"], preferred_element_type=F32)

    (proj,), carried = _carried_call(
        body, carry, grid=(n_blocks,),
        in_specs=[pl.BlockSpec((s, d), lambda i, w: (0, 0)), pl.BlockSpec((IN_BLOCK, d), lambda i, w: (block_of(i, w), 0))],
        out_specs=[pl.BlockSpec((s, IN_BLOCK), lambda i, w: (0, _perm_block(block_of(i, w))))],
        out_shape=[jax.ShapeDtypeStruct((s, IN_WIDTH), F32)], scratch=[], operands=(h, w_t), name=name,
        prefetch=where)
    return (proj, carried) if carry else proj


def _proj_far(h, w_near, far, where, *, into, carry=None):
    s, d = h.shape
    n_blocks = SHARD_BLOCKS + 1
    lead = IN_WIDTH // N_CHIPS - SHARD_BLOCKS * IN_BLOCK

    def body(ins, outs, scr):
        where_ref, h_ref, w_hbm, far_hbm, _ = ins
        win, sem = scr
        i = pl.program_id(0)

        @pl.when(i == 0)
        def _():
            dg = where_ref[2]
            rows = pl.ds(pl.multiple_of(dg * (SHARD_BLOCKS * IN_BLOCK), IN_BLOCK), n_blocks * IN_BLOCK)
            window = pltpu.make_async_copy(w_hbm.at[rows], win, sem)
            window.start()
            window.wait()
            shard = pltpu.make_async_copy(far_hbm, win.at[pl.ds(pl.multiple_of(dg * lead, BF16_SUBLANES), SHARD_W)], sem)
            shard.start()
            shard.wait()

        blk = win[pl.ds(pl.multiple_of(i * IN_BLOCK, IN_BLOCK), IN_BLOCK), :]
        outs[0][...] = lax.dot_general(h_ref[...], blk, _DIMS["nt"], preferred_element_type=F32)

    anysp = pl.BlockSpec(memory_space=pl.ANY)
    (proj,), carried = _carried_call(
        body, carry, grid=(n_blocks,),
        in_specs=[pl.BlockSpec(memory_space=pltpu.SMEM), pl.BlockSpec((s, d), lambda i, w: (0, 0)), anysp, anysp, anysp],
        out_specs=[pl.BlockSpec((s, IN_BLOCK), lambda i, w: (0, _perm_block(i + SHARD_BLOCKS * w[2])))],
        out_shape=[jax.ShapeDtypeStruct((s, IN_WIDTH), F32)],
        scratch=[pltpu.VMEM((n_blocks * IN_BLOCK, d), BF16), pltpu.SemaphoreType.DMA],
        operands=(where, h, w_near, far, into), name="proj_far", prefetch=where, aliases={4: 0})
    return (proj, carried) if carry else proj


def _dw_in_t(dproj, h, *, half_of, where, name, carry=None):
    s, d = h.shape
    c = d // 2

    def body(ins, outs, scr):
        outs[0][...] = lax.dot_general(ins[0][...], ins[1][...], _DIMS["tn"],
                                       preferred_element_type=F32).astype(BF16)

    (dw,), carried = _carried_call(
        body, carry, grid=(N_IN_BLOCKS,),
        in_specs=[pl.BlockSpec((s, IN_BLOCK), lambda r, w: (0, _perm_block(r))),
                  pl.BlockSpec((s, c), lambda r, w: (0, half_of(w)))],
        out_specs=[pl.BlockSpec((IN_BLOCK, c), lambda r, w: (r, 0))],
        out_shape=[jax.ShapeDtypeStruct((IN_WIDTH, c), BF16)], scratch=[], operands=(dproj, h), name=name,
        prefetch=where)
    return (dw, carried) if carry else dw


def _d_h(dproj, w_near, far, where, *, carry=None):
    s = dproj.shape[0]
    d = w_near.shape[1]
    tm = min(s, 256)

    def body(ins, outs, scr):
        where_ref, a_ref, w_hbm, far_hbm = ins
        w_ref, sem = scr

        @pl.when(pl.program_id(0) == 0)
        def _():
            whole = pltpu.make_async_copy(w_hbm, w_ref, sem)
            whole.start()
            whole.wait()
            rows = pl.ds(pl.multiple_of(where_ref[2] * SHARD_W, BF16_SUBLANES), SHARD_W)
            part = pltpu.make_async_copy(far_hbm, w_ref.at[rows], sem)
            part.start()
            part.wait()

        acc = None
        for ref0, perm0, n in BLOCK_RUNS:
            term = jnp.dot(a_ref[:, perm0 * IN_BLOCK:(perm0 + n) * IN_BLOCK],
                           w_ref[ref0 * IN_BLOCK:(ref0 + n) * IN_BLOCK, :], preferred_element_type=F32)
            acc = term if acc is None else acc + term
        outs[0][...] = acc

    anysp = pl.BlockSpec(memory_space=pl.ANY)
    (dh,), carried = _carried_call(
        body, carry, grid=(s // tm,),
        in_specs=[pl.BlockSpec(memory_space=pltpu.SMEM), pl.BlockSpec((tm, IN_WIDTH), lambda i: (i, 0)), anysp, anysp],
        out_specs=[pl.BlockSpec((tm, d), lambda i: (i, 0))],
        out_shape=[jax.ShapeDtypeStruct((s, d), F32)],
        scratch=[pltpu.VMEM((IN_WIDTH, d), BF16), pltpu.SemaphoreType.DMA],
        operands=(where, dproj, w_near, far), name="d_h")
    return (dh, carried) if carry else dh


def _rmsnorm_fwd(x, g, *, name):
    s, d = x.shape
    ts = min(512, s)

    def body(x_ref, g_ref, o_ref):
        xv = x_ref[...]
        r = lax.rsqrt(jnp.mean(xv * xv, axis=-1, keepdims=True) + EPS)
        o_ref[...] = ((xv * r) * g_ref[...]).astype(BF16)

    return pl.pallas_call(
        body, grid=(s // ts,),
        in_specs=[pl.BlockSpec((ts, d), lambda i: (i, 0)), pl.BlockSpec((1, d), lambda i: (0, 0))],
        out_specs=pl.BlockSpec((ts, d), lambda i: (i, 0)),
        out_shape=jax.ShapeDtypeStruct((s, d), BF16), name=name, compiler_params=_params())(x, g)


def _rmsnorm_bwd(dh, x, g, res, *, name, carry=None):
    s, d = x.shape
    ts = min(256, s)

    def body(ins, outs, scr):
        dh_ref, x_ref, g_ref, res_ref = ins
        dx_ref, dg_ref = outs
        xv = x_ref[...]
        r = lax.rsqrt(jnp.mean(xv * xv, axis=-1, keepdims=True) + EPS)
        xh = xv * r
        dhv = dh_ref[...]
        part = jnp.sum(dhv * xh, axis=0, keepdims=True)

        @pl.when(pl.program_id(0) == 0)
        def _():
            dg_ref[...] = part

        @pl.when(pl.program_id(0) > 0)
        def _():
            dg_ref[...] += part

        dxh = dhv * g_ref[...]
        dx_ref[...] = res_ref[...] + r * (dxh - xh * jnp.mean(dxh * xh, axis=-1, keepdims=True))

    row = pl.BlockSpec((ts, d), lambda i: (i, 0))
    vec = pl.BlockSpec((1, d), lambda i: (0, 0))
    outs, carried = _carried_call(
        body, carry, grid=(s // ts,), in_specs=[row, row, vec, row], out_specs=[row, vec],
        out_shape=[jax.ShapeDtypeStruct((s, d), F32), jax.ShapeDtypeStruct((1, d), F32)],
        scratch=[], operands=(dh, x, g, res), name=name)
    return (*outs, carried) if carry else tuple(outs)


MID_TILE = 256


def _gated_branches(y_refs, wup_ref, gl):
    d = D_MODEL
    us = [jnp.dot(y_refs[k][...], wup_ref[k], preferred_element_type=F32) for k in range(3)]
    sg = [_sigmoid(gl[:, k * d:(k + 1) * d]) for k in range(3)]
    return us, sg


def _mid_fwd(ya, yb, ym, proj, x, tgt, w_up, w_out, g_post):
    s, d = x.shape
    ts = MID_TILE

    def body(ya_ref, yb_ref, ym_ref, g_ref, x_ref, t_ref, wup_ref, wout_ref, gp_ref,
             m_ref, do_ref, dy_ref, dg_ref, loss_ref):
        us, sg = _gated_branches((ya_ref, yb_ref, ym_ref), wup_ref, g_ref[...])
        merged = (sg[0] * us[0] + sg[1] * us[1] + sg[2] * us[2]).astype(BF16)
        m_ref[...] = merged
        ov = jnp.dot(merged, wout_ref[...], preferred_element_type=F32)
        r = lax.rsqrt(jnp.mean(ov * ov, axis=-1, keepdims=True) + EPS)
        nh = ov * r
        gv = gp_ref[...]
        e = (x_ref[...] + nh * gv) - t_ref[...]
        lpart = 0.5 * jnp.sum(jnp.mean(e * e, axis=-1, keepdims=True), axis=0, keepdims=True)
        dy = e * (1.0 / d)
        dgp = jnp.sum(dy * nh, axis=0, keepdims=True)

        @pl.when(pl.program_id(0) == 0)
        def _():
            dg_ref[...] = dgp
            loss_ref[...] = jnp.broadcast_to(lpart, loss_ref.shape)

        @pl.when(pl.program_id(0) > 0)
        def _():
            dg_ref[...] += dgp
            loss_ref[...] += jnp.broadcast_to(lpart, loss_ref.shape)

        dn = dy * gv
        dy_ref[...] = dy
        do_ref[...] = (r * (dn - nh * jnp.mean(dn * nh, axis=-1, keepdims=True))).astype(BF16)

    row = pl.BlockSpec((ts, d), lambda i: (i, 0))
    ysp = pl.BlockSpec((ts, A_WIDTH), lambda i: (i, 0))
    vec = pl.BlockSpec((1, d), lambda i: (0, 0))
    return pl.pallas_call(
        body, grid=(s // ts,),
        in_specs=[ysp, ysp, ysp, pl.BlockSpec((ts, W_G), lambda i: (i, COL_G)), row, row,
                  pl.BlockSpec((3, A_WIDTH, d), lambda i: (0, 0, 0)), pl.BlockSpec((d, d), lambda i: (0, 0)), vec],
        out_specs=[row, row, row, vec, pl.BlockSpec((1, LANES), lambda i: (0, 0))],
        out_shape=[jax.ShapeDtypeStruct((s, d), BF16), jax.ShapeDtypeStruct((s, d), BF16),
                   jax.ShapeDtypeStruct((s, d), F32), jax.ShapeDtypeStruct((1, d), F32),
                   jax.ShapeDtypeStruct((1, LANES), F32)],
        name="mid_fwd", compiler_params=_params())(ya, yb, ym, proj, x, tgt, w_up, w_out, g_post)


def _mid_bwd(d_out, merged, ya, yb, ym, proj, w_up, w_out):
    s, d = merged.shape
    ts = MID_TILE
    last = s // ts - 1

    def body(do_ref, m_ref, ya_ref, yb_ref, ym_ref, g_ref, wup_ref, wout_ref,
             dp_ref, dya_ref, dyb_ref, dym_ref, dwup_hbm, dwout_hbm, dwup_acc, dwout_acc):
        i = pl.program_id(0)

        @pl.when(i == 0)
        def _():
            dwup_acc[...] = jnp.zeros_like(dwup_acc)
            dwout_acc[...] = jnp.zeros_like(dwout_acc)

        y_refs = (ya_ref, yb_ref, ym_ref)
        us, sg = _gated_branches(y_refs, wup_ref, g_ref[...])
        dov = do_ref[...]
        dwout_acc[...] += lax.dot_general(m_ref[...], dov, _DIMS["tn"], preferred_element_type=F32)
        dm = lax.dot_general(dov, wout_ref[...], _DIMS["nt"], preferred_element_type=F32)
        for k, dy_ref in enumerate((dya_ref, dyb_ref, dym_ref)):
            dp_ref[:, k * d:(k + 1) * d] = ((dm * us[k]) * (sg[k] * (1.0 - sg[k]))).astype(BF16)
            du = (sg[k] * dm).astype(BF16)
            dy_ref[...] = lax.dot_general(du, wup_ref[k], _DIMS["nt"], preferred_element_type=F32)
            dwup_acc[k] += lax.dot_general(y_refs[k][...], du, _DIMS["tn"], preferred_element_type=F32)

        @pl.when(i == last)
        def _():
            pltpu.sync_copy(dwup_acc, dwup_hbm)
            pltpu.sync_copy(dwout_acc, dwout_hbm)

    row = pl.BlockSpec((ts, d), lambda i: (i, 0))
    ysp = pl.BlockSpec((ts, A_WIDTH), lambda i: (i, 0))
    gsp = pl.BlockSpec((ts, W_G), lambda i: (i, COL_G))
    anysp = pl.BlockSpec(memory_space=pl.ANY)
    yshape = jax.ShapeDtypeStruct((s, A_WIDTH), F32)
    return pl.pallas_call(
        body, grid=(s // ts,),
        in_specs=[row, row, ysp, ysp, ysp, gsp, pl.BlockSpec((3, A_WIDTH, d), lambda i: (0, 0, 0)),
                  pl.BlockSpec((d, d), lambda i: (0, 0))],
        out_specs=[gsp, ysp, ysp, ysp, anysp, anysp],
        out_shape=[jax.ShapeDtypeStruct((s, IN_WIDTH), BF16), yshape, yshape, yshape,
                   jax.ShapeDtypeStruct((3, A_WIDTH, d), F32), jax.ShapeDtypeStruct((d, d), F32)],
        scratch_shapes=[pltpu.VMEM((3, A_WIDTH, d), F32), pltpu.VMEM((d, d), F32)],
        name="mid_bwd", compiler_params=_params())(d_out, merged, ya, yb, ym, proj, w_up, w_out)


def _conv_core(blk, prev, nxt, w, i, last, ts):
    c = A_WIDTH
    ab, ac, ax, az = blk[:, :c], blk[:, c:2 * c], blk[:, 2 * c:3 * c], blk[:, 3 * c:]
    cu = ac * ax
    cu_prev = (prev[7:8, c:2 * c] * prev[7:8, 2 * c:3 * c]) * jnp.where(i > 0, 1.0, 0.0)
    cu_next = (nxt[0:1, c:2 * c] * nxt[0:1, 2 * c:3 * c]) * jnp.where(i < last, 1.0, 0.0)
    row = lax.broadcasted_iota(jnp.int32, (ts, c), 0)
    cm1 = jnp.where(row == 0, cu_prev, pltpu.roll(cu, 1, 0))
    cp1 = jnp.where(row == ts - 1, cu_next, pltpu.roll(cu, ts - 1, 0))
    yc = cm1 * w[0:1] + cu * w[1:2] + cp1 * w[2:3]
    return ab, ac, ax, az, cu, cm1, cp1, yc, row


def _halo_specs(ts, width, col, nblk8):
    prev = pl.BlockSpec((8, width), lambda i: (jnp.maximum(i * (ts // 8) - 1, 0), col))
    nxt = pl.BlockSpec((8, width), lambda i: (jnp.minimum((i + 1) * (ts // 8), nblk8 - 1), col))
    return prev, nxt


def _conv_fwd(proj, w_conv):
    s = proj.shape[0]
    ts = 256
    last = s // ts - 1

    def body(a_ref, ap_ref, an_ref, w_ref, ya_ref):
        i = pl.program_id(0)
        ab, _, _, az, _, _, _, yc, _ = _conv_core(a_ref[...], ap_ref[...], an_ref[...], w_ref[...], i, last, ts)
        ya_ref[...] = ((ab * yc) * (az * _sigmoid(az))).astype(BF16)

    prev, nxt = _halo_specs(ts, W_A, COL_A, s // 8)
    return pl.pallas_call(
        body, grid=(s // ts,),
        in_specs=[pl.BlockSpec((ts, W_A), lambda i: (i, COL_A)), prev, nxt,
                  pl.BlockSpec((3, A_WIDTH), lambda i: (0, 0))],
        out_specs=pl.BlockSpec((ts, A_WIDTH), lambda i: (i, 0)),
        out_shape=jax.ShapeDtypeStruct((s, A_WIDTH), BF16), name="conv_fwd",
        compiler_params=_params())(proj, proj, proj, w_conv)


def _conv_bwd(proj, w_conv, dya, dproj):
    s = proj.shape[0]
    ts = 256
    last = s // ts - 1
    c = A_WIDTH

    def body(a_ref, ap_ref, an_ref, w_ref, d_ref, dp_ref, dn_ref, _, dproj_ref, dw_ref):
        i = pl.program_id(0)
        w = w_ref[...]
        prev, nxt = ap_ref[...], an_ref[...]
        ab, ac, ax, az, cu, cm1, cp1, yc, row = _conv_core(a_ref[...], prev, nxt, w, i, last, ts)
        sg = _sigmoid(az)
        sz = az * sg
        dya_v = d_ref[...]
        dyc = dya_v * sz * ab
        dproj_ref[:, :c] = (dya_v * sz * yc).astype(BF16)
        dproj_ref[:, 3 * c:] = (dya_v * (ab * yc) * (sg * (1.0 + az * (1.0 - sg)))).astype(BF16)

        def halo_dyc(a_row, d_row):
            azr = a_row[:, 3 * c:]
            return d_row * (azr * _sigmoid(azr)) * a_row[:, :c]

        dyc_prev = halo_dyc(prev[7:8], dp_ref[...][7:8]) * jnp.where(i > 0, 1.0, 0.0)
        dyc_next = halo_dyc(nxt[0:1], dn_ref[...][0:1]) * jnp.where(i < last, 1.0, 0.0)
        dyc_m1 = jnp.where(row == 0, dyc_prev, pltpu.roll(dyc, 1, 0))
        dyc_p1 = jnp.where(row == ts - 1, dyc_next, pltpu.roll(dyc, ts - 1, 0))
        dcu = dyc_p1 * w[0:1] + dyc * w[1:2] + dyc_m1 * w[2:3]
        dproj_ref[:, c:2 * c] = (dcu * ax).astype(BF16)
        dproj_ref[:, 2 * c:3 * c] = (dcu * ac).astype(BF16)
        dw = [jnp.sum(dyc * t, axis=0, keepdims=True) for t in (cm1, cu, cp1)]

        @pl.when(i == 0)
        def _():
            for k in range(3):
                dw_ref[k:k + 1, :] = dw[k]

        @pl.when(i > 0)
        def _():
            for k in range(3):
                dw_ref[k:k + 1, :] += dw[k]

    prev, nxt = _halo_specs(ts, W_A, COL_A, s // 8)
    dprev, dnxt = _halo_specs(ts, A_WIDTH, 0, s // 8)
    return pl.pallas_call(
        body, grid=(s // ts,),
        in_specs=[pl.BlockSpec((ts, W_A), lambda i: (i, COL_A)), prev, nxt,
                  pl.BlockSpec((3, A_WIDTH), lambda i: (0, 0)),
                  pl.BlockSpec((ts, A_WIDTH), lambda i: (i, 0)), dprev, dnxt,
                  pl.BlockSpec(memory_space=pl.ANY)],
        out_specs=[pl.BlockSpec((ts, W_A), lambda i: (i, COL_A)), pl.BlockSpec((3, A_WIDTH), lambda i: (0, 0))],
        out_shape=[jax.ShapeDtypeStruct(dproj.shape, BF16), jax.ShapeDtypeStruct((3, A_WIDTH), F32)],
        input_output_aliases={7: 0}, name="conv_bwd",
        compiler_params=_params())(proj, proj, proj, w_conv, dya, dya, dya, dproj)


def _rope_tables(s):
    half = ROT_DIM // 2
    dim = jnp.arange(LANES) % HEAD_DIM
    inv_freq = jnp.power(jnp.float32(ROPE_THETA), -(dim % half).astype(F32) * (2.0 / ROT_DIM))
    ang = jnp.arange(s).astype(F32)[:, None] * inv_freq[None, :]
    cos, sin = jnp.cos(ang), jnp.sin(ang)
    first, second = (dim < half)[None, :], ((dim >= half) & (dim < ROT_DIM))[None, :]
    c = jnp.where(first | second, cos, 1.0)
    s1 = jnp.where(first, -sin, 0.0)
    s2 = jnp.where(second, sin, 0.0)
    return jnp.concatenate([c, s1, s2], axis=1)


def _rope(t, tab):
    return (t * tab[:, :LANES] + pltpu.roll(t, LANES - 8, 1) * tab[:, LANES:2 * LANES]
            + pltpu.roll(t, 8, 1) * tab[:, 2 * LANES:])


def _rope_transpose(dt, tab):
    return (dt * tab[:, :LANES] + pltpu.roll(dt * tab[:, LANES:2 * LANES], 8, 1)
            + pltpu.roll(dt * tab[:, 2 * LANES:], LANES - 8, 1))


def _rope_kv(proj, tab):
    s = proj.shape[0]
    nb = s // KV_PAD

    def body(kv_ref, t_ref, k_ref, v_ref):
        j = pl.program_id(0)
        inside = jnp.where((j > 0) & (j <= nb), 1.0, 0.0)
        kv = kv_ref[...]
        k_ref[...] = (_rope(kv[:, :LANES], t_ref[...]) * inside).astype(BF16)
        v_ref[...] = (kv[:, LANES:] * inside).astype(BF16)

    def src(j):
        return jnp.clip(j - 1, 0, nb - 1)

    o_spec = pl.BlockSpec((KV_PAD, LANES), lambda j: (j, 0))
    shp = jax.ShapeDtypeStruct((s + 2 * KV_PAD, LANES), BF16)
    return pl.pallas_call(
        body, grid=(nb + 2,),
        in_specs=[pl.BlockSpec((KV_PAD, W_KV), lambda j: (src(j), COL_KV)),
                  pl.BlockSpec((KV_PAD, 3 * LANES), lambda j: (src(j), 0))],
        out_specs=[o_spec, o_spec], out_shape=[shp, shp], name="rope_kv",
        compiler_params=_params())(proj, tab)


def _rope_kv_bwd(dkpad, dvpad, tab, dproj):
    s = tab.shape[0]
    nb = s // KV_PAD

    def body(dk_ref, dv_ref, t_ref, _, dp_ref):
        dp_ref[:, :LANES] = _rope_transpose(dk_ref[...], t_ref[...]).astype(BF16)
        dp_ref[:, LANES:] = dv_ref[...].astype(BF16)

    pad_spec = pl.BlockSpec((KV_PAD, LANES), lambda j: (j + 1, 0))
    return pl.pallas_call(
        body, grid=(nb,),
        in_specs=[pad_spec, pad_spec, pl.BlockSpec((KV_PAD, 3 * LANES), lambda j: (j, 0)),
                  pl.BlockSpec(memory_space=pl.ANY)],
        out_specs=pl.BlockSpec((KV_PAD, W_KV), lambda j: (j, COL_KV)),
        out_shape=jax.ShapeDtypeStruct(dproj.shape, BF16), input_output_aliases={3: 0},
        name="rope_kv_bwd", compiler_params=_params())(dkpad, dvpad, tab, dproj)


def _window_start(n):
    return pl.multiple_of((n - 1) * WINDOW_BLOCK + KV_PAD, WINDOW_BLOCK)


def _window_operands(k_ref, v_ref, n, lo):
    start = _window_start(n)
    kw = k_ref[pl.ds(start, 3 * WINDOW_BLOCK), :].astype(F32)
    vw = v_ref[pl.ds(start, 3 * WINDOW_BLOCK), :].astype(F32)
    kr, vr = pltpu.roll(kw, HALF_LANES, 1), pltpu.roll(vw, HALF_LANES, 1)
    k2 = (jnp.where(lo, kw, kr).astype(BF16), jnp.where(lo, kr, kw).astype(BF16))
    v2 = (jnp.where(lo, vw, vr).astype(BF16), jnp.where(lo, vr, vw).astype(BF16))
    return k2, v2


HEADS_PER_GROUP = 4
SWA_FWD_BLOCKS = 1
SWA_BWD_BLOCKS = 2


def _window_mask(n, s):
    wb = WINDOW_BLOCK
    shape = (HEADS_PER_GROUP * wb, 3 * wb)
    qi = lax.broadcasted_iota(jnp.int32, shape, 0) & (wb - 1)
    kj = lax.broadcasted_iota(jnp.int32, shape, 1)
    kpos = kj + (n - 1) * wb
    return (kj >= qi) & (kj <= qi + 2 * wb) & (kpos >= 0) & (kpos < s)


def _stack_heads(pair0, pair1, lo):
    return jnp.concatenate([jnp.where(lo, pair0, 0.0), jnp.where(lo, 0.0, pair0),
                            jnp.where(lo, pair1, 0.0), jnp.where(lo, 0.0, pair1)], axis=0)


def _unstack_pair(stacked, i, lo):
    wb = WINDOW_BLOCK
    return jnp.where(lo, stacked[2 * i * wb:(2 * i + 1) * wb], stacked[(2 * i + 1) * wb:(2 * i + 2) * wb])


def _sink_column(sink_ref, g):
    wb = WINDOW_BLOCK
    return jnp.concatenate([jnp.full((wb, 1), sink_ref[0, HEADS_PER_GROUP * g + i], F32)
                            for i in range(HEADS_PER_GROUP)], axis=0)


def _head_exp(q4, k2g, valid, sink):
    sc = lax.dot_general(q4, k2g, _DIMS["nt"], preferred_element_type=F32) * (HEAD_DIM ** -0.5)
    sc = jnp.where(valid, sc, -jnp.inf)
    m = jnp.maximum(jnp.max(sc, axis=1, keepdims=True), sink)
    return jnp.exp(sc - m).astype(BF16), jnp.exp(sink - m)


def _swa_fwd(proj, kpad, vpad, tab, sink):
    s = proj.shape[0]
    wb = WINDOW_BLOCK

    def body(b_ref, k_ref, v_ref, t_ref, sink_ref, o_ref, y_ref):
        lo = lax.broadcasted_iota(jnp.int32, (wb, LANES), 1) < HALF_LANES
        lo_w = lax.broadcasted_iota(jnp.int32, (3 * wb, LANES), 1) < HALF_LANES
        for sub in range(SWA_FWD_BLOCKS):
            n = pl.program_id(0) * SWA_FWD_BLOCKS + sub
            rows = slice(sub * wb, (sub + 1) * wb)
            k2, v2 = _window_operands(k_ref, v_ref, n, lo_w)
            valid = _window_mask(n, s)
            tab_v = t_ref[rows, :]
            ones = jnp.ones((3 * wb, LANES), BF16)
            for g in range(2):
                qr = [_rope(b_ref[rows, (2 * g + i) * LANES:(2 * g + i + 1) * LANES], tab_v) for i in range(2)]
                q4 = _stack_heads(qr[0], qr[1], lo).astype(BF16)
                e, es = _head_exp(q4, k2[g], valid, _sink_column(sink_ref, g))
                ox = jnp.dot(e, jnp.concatenate([v2[g], ones], axis=1), preferred_element_type=F32)
                o4 = ox[:, :LANES] * (1.0 / (ox[:, LANES:] + es))
                for i in range(2):
                    cols = slice((2 * g + i) * LANES, (2 * g + i + 1) * LANES)
                    op = _unstack_pair(o4, i, lo)
                    o_ref[rows, cols] = op
                    zp = b_ref[rows, A_WIDTH + cols.start:A_WIDTH + cols.stop]
                    y_ref[rows, cols] = (op * (zp * _sigmoid(zp))).astype(BF16)

    tq = SWA_FWD_BLOCKS * wb
    pad_spec = pl.BlockSpec((s + 2 * KV_PAD, LANES), lambda n: (0, 0))
    o_spec = pl.BlockSpec((tq, A_WIDTH), lambda n: (n, 0))
    return pl.pallas_call(
        body, grid=(s // tq,),
        in_specs=[pl.BlockSpec((tq, W_B), lambda n: (n, COL_B)), pad_spec, pad_spec,
                  pl.BlockSpec((tq, 3 * LANES), lambda n: (n, 0)),
                  pl.BlockSpec(memory_space=pltpu.SMEM)],
        out_specs=[o_spec, o_spec],
        out_shape=[jax.ShapeDtypeStruct((s, A_WIDTH), F32), jax.ShapeDtypeStruct((s, A_WIDTH), BF16)],
        name="swa_fwd", compiler_params=_params())(proj, kpad, vpad, tab, sink)


def _swa_bwd(proj, kpad, vpad, tab, sink, o_attn, dyb, dproj):
    s = proj.shape[0]
    wb = WINDOW_BLOCK
    scale = HEAD_DIM ** -0.5

    def body(b_ref, k_ref, v_ref, t_ref, sink_ref, o_ref, dy_ref, _, dp_ref, dk_ref, dv_ref, ds_ref):
        @pl.when(pl.program_id(0) == 0)
        def _():
            dk_ref[...] = jnp.zeros_like(dk_ref)
            dv_ref[...] = jnp.zeros_like(dv_ref)
            ds_ref[...] = jnp.zeros_like(ds_ref)

        lo = lax.broadcasted_iota(jnp.int32, (wb, LANES), 1) < HALF_LANES
        lo_w = lax.broadcasted_iota(jnp.int32, (3 * wb, LANES), 1) < HALF_LANES
        for sub in range(SWA_BWD_BLOCKS):
            n = pl.program_id(0) * SWA_BWD_BLOCKS + sub
            rows = slice(sub * wb, (sub + 1) * wb)
            k2, v2 = _window_operands(k_ref, v_ref, n, lo_w)
            valid = _window_mask(n, s)
            tab_v = t_ref[rows, :]
            ones = jnp.ones((3 * wb, LANES), BF16)
            dks, dvs = [], []
            for g in range(2):
                qr, op, do = [], [], []
                for i in range(2):
                    cols = slice((2 * g + i) * LANES, (2 * g + i + 1) * LANES)
                    zcols = slice(A_WIDTH + cols.start, A_WIDTH + cols.stop)
                    qr.append(_rope(b_ref[rows, cols], tab_v))
                    zp = b_ref[rows, zcols]
                    sg = _sigmoid(zp)
                    op.append(o_ref[rows, cols])
                    dyp = dy_ref[rows, cols]
                    do.append(dyp * (zp * sg))
                    dp_ref[rows, zcols] = (dyp * op[i] * (sg * (1.0 + zp * (1.0 - sg)))).astype(BF16)
                q4 = _stack_heads(qr[0], qr[1], lo).astype(BF16)
                do4 = _stack_heads(do[0], do[1], lo)
                o4 = jnp.concatenate([op[0], op[0], op[1], op[1]], axis=0)
                e, es = _head_exp(q4, k2[g], valid, _sink_column(sink_ref, g))
                inv = 1.0 / (jnp.dot(e, ones, preferred_element_type=F32) + es)
                prob = e.astype(F32) * jnp.concatenate([inv, inv, inv], axis=1)
                delta = jnp.sum(do4 * o4, axis=1, keepdims=True)
                do4b = do4.astype(BF16)
                dprob = lax.dot_general(do4b, v2[g], _DIMS["nt"], preferred_element_type=F32)
                dsc = (prob * (dprob - delta)).astype(BF16)
                sink_terms = (es * inv[:, :1]) * delta
                for i in range(HEADS_PER_GROUP):
                    h = HEADS_PER_GROUP * g + i
                    dsink = -jnp.sum(sink_terms[i * wb:(i + 1) * wb], axis=0, keepdims=True)
                    ds_ref[h:h + 1, :] += jnp.broadcast_to(dsink, (1, LANES))
                dq4 = jnp.dot(dsc, k2[g], preferred_element_type=F32) * scale
                for i in range(2):
                    cols = slice((2 * g + i) * LANES, (2 * g + i + 1) * LANES)
                    dp_ref[rows, cols] = _rope_transpose(_unstack_pair(dq4, i, lo), tab_v).astype(BF16)
                dk2 = lax.dot_general(dsc, q4, _DIMS["tn"], preferred_element_type=F32) * scale
                dv2 = lax.dot_general(prob.astype(BF16), do4b, _DIMS["tn"], preferred_element_type=F32)
                dks.append(dk2 + pltpu.roll(dk2, HALF_LANES, 1))
                dvs.append(dv2 + pltpu.roll(dv2, HALF_LANES, 1))
            start = _window_start(n)
            dk_ref[pl.ds(start, 3 * wb), :] += jnp.where(lo_w, dks[0], dks[1])
            dv_ref[pl.ds(start, 3 * wb), :] += jnp.where(lo_w, dvs[0], dvs[1])

    tq = SWA_BWD_BLOCKS * wb
    pad_spec = pl.BlockSpec((s + 2 * KV_PAD, LANES), lambda n: (0, 0))
    blk = pl.BlockSpec((tq, A_WIDTH), lambda n: (n, 0))
    bsp = pl.BlockSpec((tq, W_B), lambda n: (n, COL_B))
    pad_shape = jax.ShapeDtypeStruct((s + 2 * KV_PAD, LANES), F32)
    return pl.pallas_call(
        body, grid=(s // tq,),
        in_specs=[bsp, pad_spec, pad_spec, pl.BlockSpec((tq, 3 * LANES), lambda n: (n, 0)),
                  pl.BlockSpec(memory_space=pltpu.SMEM), blk, blk, pl.BlockSpec(memory_space=pl.ANY)],
        out_specs=[bsp, pad_spec, pad_spec, pl.BlockSpec((8, LANES), lambda n: (0, 0))],
        out_shape=[jax.ShapeDtypeStruct(dproj.shape, BF16), pad_shape, pad_shape,
                   jax.ShapeDtypeStruct((8, LANES), F32)],
        input_output_aliases={7: 0}, name="swa_bwd",
        compiler_params=_params())(proj, kpad, vpad, tab, sink, o_attn, dyb, dproj)


def _mem_exp(qh, mk):
    sc = lax.dot_general(qh, mk, _DIMS["nt"], preferred_element_type=F32) * (MEM_HEAD_DIM ** -0.5)
    return jnp.exp(sc - jnp.max(sc, axis=1, keepdims=True)).astype(BF16)


def _mem_fwd(proj, mkv):
    s = proj.shape[0]
    ts = 512
    mlen = mkv.shape[0]

    def body(m_ref, kv_ref, o_ref, y_ref):
        ones = jnp.ones((mlen, LANES), BF16)
        for h in range(MEM_HEADS):
            cols = slice(h * LANES, (h + 1) * LANES)
            mk = kv_ref[:, cols].astype(BF16)
            mv = kv_ref[:, MEM_WIDTH + h * LANES:MEM_WIDTH + (h + 1) * LANES].astype(BF16)
            e = _mem_exp(m_ref[:, cols].astype(BF16), mk)
            ox = jnp.dot(e, jnp.concatenate([mv, ones], axis=1), preferred_element_type=F32)
            oh = ox[:, :LANES] * (1.0 / ox[:, LANES:])
            o_ref[:, cols] = oh
            zh = m_ref[:, MEM_WIDTH + h * LANES:MEM_WIDTH + (h + 1) * LANES]
            y_ref[:, cols] = (oh * (zh * _sigmoid(zh))).astype(BF16)

    o_spec = pl.BlockSpec((ts, MEM_WIDTH), lambda i: (i, 0))
    return pl.pallas_call(
        body, grid=(s // ts,),
        in_specs=[pl.BlockSpec((ts, W_M), lambda i: (i, COL_M)),
                  pl.BlockSpec((mlen, 2 * MEM_WIDTH), lambda i: (0, 0))],
        out_specs=[o_spec, o_spec],
        out_shape=[jax.ShapeDtypeStruct((s, MEM_WIDTH), F32), jax.ShapeDtypeStruct((s, MEM_WIDTH), BF16)],
        name="mem_fwd", compiler_params=_params())(proj, mkv)


def _mem_bwd(proj, mkv, o_mem, dym, dproj):
    s = proj.shape[0]
    ts = 512
    mlen = mkv.shape[0]
    scale = MEM_HEAD_DIM ** -0.5

    def body(m_ref, kv_ref, o_ref, dy_ref, _, dp_ref, dkv_ref):
        @pl.when(pl.program_id(0) == 0)
        def _():
            dkv_ref[...] = jnp.zeros_like(dkv_ref)

        ones = jnp.ones((mlen, LANES), BF16)
        for h in range(MEM_HEADS):
            cols = slice(h * LANES, (h + 1) * LANES)
            vcols = slice(MEM_WIDTH + h * LANES, MEM_WIDTH + (h + 1) * LANES)
            mk = kv_ref[:, cols].astype(BF16)
            mv = kv_ref[:, vcols].astype(BF16)
            qh = m_ref[:, cols].astype(BF16)
            zh = m_ref[:, vcols]
            sg = _sigmoid(zh)
            oh = o_ref[:, cols]
            dyh = dy_ref[:, cols]
            doh = dyh * (zh * sg)
            dp_ref[:, vcols] = (dyh * oh * (sg * (1.0 + zh * (1.0 - sg)))).astype(BF16)
            e = _mem_exp(qh, mk)
            inv = 1.0 / jnp.dot(e, ones, preferred_element_type=F32)
            prob = e.astype(F32) * jnp.concatenate([inv] * (mlen // LANES), axis=1)
            delta = jnp.sum(doh * oh, axis=1, keepdims=True)
            dohb = doh.astype(BF16)
            dprob = lax.dot_general(dohb, mv, _DIMS["nt"], preferred_element_type=F32)
            dsc = (prob * (dprob - delta)).astype(BF16)
            dp_ref[:, cols] = (jnp.dot(dsc, mk, preferred_element_type=F32) * scale).astype(BF16)
            dkv_ref[:, cols] += lax.dot_general(dsc, qh, _DIMS["tn"], preferred_element_type=F32) * scale
            dkv_ref[:, vcols] += lax.dot_general(prob.astype(BF16), dohb, _DIMS["tn"],
                                                 preferred_element_type=F32)

    blk = pl.BlockSpec((ts, MEM_WIDTH), lambda i: (i, 0))
    msp = pl.BlockSpec((ts, W_M), lambda i: (i, COL_M))
    kvsp = pl.BlockSpec((mlen, 2 * MEM_WIDTH), lambda i: (0, 0))
    return pl.pallas_call(
        body, grid=(s // ts,),
        in_specs=[msp, kvsp, blk, blk, pl.BlockSpec(memory_space=pl.ANY)],
        out_specs=[msp, kvsp],
        out_shape=[jax.ShapeDtypeStruct(dproj.shape, BF16), jax.ShapeDtypeStruct(mkv.shape, F32)],
        input_output_aliases={4: 0}, name="mem_bwd",
        compiler_params=_params())(proj, mkv, o_mem, dym, dproj)


def _forward_backward(x, mem, tgt, proj, w_conv, sink, g_mem, w_kv, w_up, w_out, g_post):
    s = x.shape[0]
    tab = _rope_tables(s)

    ya = _conv_fwd(proj, w_conv)
    kpad, vpad = _rope_kv(proj, tab)
    o_attn, yb = _swa_fwd(proj, kpad, vpad, tab, sink)
    mn = _rmsnorm_fwd(mem, g_mem, name="mem_norm")
    mkv = _matmul(mn, w_kv, mode="nn", out_dtype=F32, tm=256, tn=1024, tk=D_MODEL, name="mem_kv")
    o_mem, ym = _mem_fwd(proj, mkv)
    merged, d_out, dy, dg_post, loss = _mid_fwd(ya, yb, ym, proj, x, tgt, w_up, w_out, g_post)
    dproj, d_ya, d_yb, d_ym, dw_up, dw_out = _mid_bwd(d_out, merged, ya, yb, ym, proj, w_up, w_out)

    dproj, dw_conv = _conv_bwd(proj, w_conv, d_ya, dproj)
    dproj, dkpad, dvpad, dsink = _swa_bwd(proj, kpad, vpad, tab, sink, o_attn, d_yb, dproj)
    dproj = _rope_kv_bwd(dkpad, dvpad, tab, dproj)
    dproj, d_mkv = _mem_bwd(proj, mkv, o_mem, d_ym, dproj)

    dw_kv = _matmul(mn, d_mkv, mode="tn", out_dtype=F32, tm=1024, tn=1024, tk=256, name="dw_kv")
    d_mn = _matmul(d_mkv, w_kv, mode="nt", out_dtype=F32, tm=256, tn=1024, tk=D_MODEL, name="d_mn")
    _, dg_mem = _rmsnorm_bwd(d_mn, mem, g_mem, d_mn, name="mem_norm_bwd")

    return dict(loss=loss, dproj=dproj, dy=dy, w_conv=dw_conv, sink=dsink, g_mem=dg_mem,
                w_kv=dw_kv, w_up=dw_up, w_out=dw_out, g_post=dg_post)


N_DEV = 8


def _position():
    return lax.axis_index("x"), lax.axis_index("y"), lax.axis_index("c")


def _other_chips(x, y):
    return (((1 - x, y), 2 * (1 - x) + y), ((x, 1 - y), 2 * x + (1 - y)), ((1 - x, 1 - y), 2 * (1 - x) + (1 - y)))


def _remote(src, dst, send_sems, recv_sems, k, device):
    return pltpu.make_async_remote_copy(src_ref=src, dst_ref=dst, send_sem=send_sems.at[k], recv_sem=recv_sems.at[k],
                                        device_id=device, device_id_type=MESH)


def _rows_half(ref, hf):
    rh = ref.shape[0] // 2
    return ref.at[pl.ds(pl.multiple_of(hf * rh, 8), rh)]


def _gather_weights(shards, small=None, relations=(0, 1, 2), into=None):
    n = len(shards)
    k = 0 if small is None else 1

    def peers(x, y):
        return [(r, chip, idx) for r, (chip, idx) in enumerate(_other_chips(x, y)) if r in relations]

    def ici(ins, outs, sems, a, r, chip, src_chip, c):
        return _remote(_rows_half(ins[a], c), _rows_half(outs[a].at[src_chip], c), sems[0], sems[1], 3 * a + r,
                       (*chip, c))

    def whole(ins, outs, sems, r, chip, src_chip, c):
        return _remote(ins[n], outs[n].at[src_chip], sems[0], sems[1], 3 * n + r, (*chip, c))

    def d2d(outs, sems, a, r, idx, hf, x, y, c):
        half = _rows_half(outs[a].at[idx], hf)
        return _remote(half, half, sems[2], sems[3], 3 * a + r, (x, y, 1 - c))

    def start(ins, outs, sems):
        x, y, c = _position()
        me = 2 * x + y
        for a in range(n):
            for r, chip, _ in peers(x, y):
                ici(ins, outs, sems, a, r, chip, me, c).start()
        for r, (chip, _) in enumerate(_other_chips(x, y)):
            if k:
                whole(ins, outs, sems, r, chip, me, c).start()

    def finish(ins, outs, sems):
        x, y, c = _position()
        me = 2 * x + y
        for a in range(n):
            for r, chip, idx in peers(x, y):
                ici(ins, outs, sems, a, r, chip, idx, c).wait_recv()
                d2d(outs, sems, a, r, idx, c, x, y, c).start()
        for a in range(n):
            for r, chip, idx in peers(x, y):
                d2d(outs, sems, a, r, idx, 1 - c, x, y, c).wait_recv()
        for r, (chip, idx) in enumerate(_other_chips(x, y)):
            if k:
                whole(ins, outs, sems, r, chip, idx, c).wait_recv()
                whole(ins, outs, sems, r, chip, me, c).wait_send()
        for a in range(n):
            for r, chip, idx in peers(x, y):
                ici(ins, outs, sems, a, r, chip, me, c).wait_send()
                d2d(outs, sems, a, r, idx, c, x, y, c).wait_send()

    operands = list(shards) + ([small] if k else [])
    shapes = [jax.ShapeDtypeStruct((N_CHIPS,) + s.shape, s.dtype) for s in operands]
    aliases = {}
    if into is not None:
        assert len(into) == len(operands)
        aliases = {len(operands) + a: a for a in range(len(into))}
        operands += list(into)
    return _Carry(operands, shapes,
                  [pltpu.SemaphoreType.DMA((3 * (n + k),)), pltpu.SemaphoreType.DMA((3 * (n + k),)),
                   pltpu.SemaphoreType.DMA((3 * n,)), pltpu.SemaphoreType.DMA((3 * n,))], start, finish, aliases)


def _run_carry(carry, name):
    _, results = _carried_call(lambda ins, outs, scr: None, carry, grid=(1,), in_specs=[], out_specs=[],
                               out_shape=[], scratch=[], operands=(), name=name)
    return results


def _pair_exchange(send):
    n = len(send)

    def copies(ins, outs, sems):
        x, y, c = _position()
        return [_remote(ins[a], outs[a], sems[0], sems[1], a, (x, y, 1 - c)) for a in range(n)]

    def start(ins, outs, sems):
        for cp in copies(ins, outs, sems):
            cp.start()

    def finish(ins, outs, sems):
        for cp in copies(ins, outs, sems):
            cp.wait()

    return _Carry(send, [jax.ShapeDtypeStruct(p.shape, p.dtype) for p in send],
                  [pltpu.SemaphoreType.DMA((n,)), pltpu.SemaphoreType.DMA((n,))], start, finish)


def _chip_exchange(sums):
    n = len(sums)

    def copies(ins, outs, sems):
        x, y, c = _position()
        return [_remote(ins[a].at[idx], outs[a].at[r], sems[0], sems[1], 3 * a + r, (*chip, c))
                for a in range(n) for r, (chip, idx) in enumerate(_other_chips(x, y))]

    def start(ins, outs, sems):
        for cp in copies(ins, outs, sems):
            cp.start()

    def finish(ins, outs, sems):
        for cp in copies(ins, outs, sems):
            cp.wait()

    return _Carry(sums, [jax.ShapeDtypeStruct((3,) + p.shape[1:], p.dtype) for p in sums],
                  [pltpu.SemaphoreType.DMA((3 * n,)), pltpu.SemaphoreType.DMA((3 * n,))], start, finish)


def _pair_share(pairs):
    n = len(pairs)

    def start(ins, outs, sems):
        x, y, c = _position()
        for a in range(n):
            _remote(outs[a].at[c], outs[a].at[c], sems[0], sems[1], a, (x, y, 1 - c)).start()

    def finish(ins, outs, sems):
        x, y, c = _position()
        for a in range(n):
            _remote(outs[a].at[1 - c], outs[a].at[1 - c], sems[0], sems[1], a, (x, y, 1 - c)).wait_recv()
        for a in range(n):
            _remote(outs[a].at[c], outs[a].at[c], sems[0], sems[1], a, (x, y, 1 - c)).wait_send()

    return _Carry(pairs, [jax.ShapeDtypeStruct(p.shape, p.dtype) for p in pairs],
                  [pltpu.SemaphoreType.DMA((n,)), pltpu.SemaphoreType.DMA((n,))], start, finish,
                  aliases={a: a for a in range(n)})


def _small_allreduce(pack, share):
    rows, width = pack.shape
    n_share = len(share.ins)

    def body(p_ref, *refs):
        share_in, o_ref, share_out = refs[:n_share], refs[n_share], refs[n_share + 1:2 * n_share + 1]
        buf, send_sems, recv_sems = refs[2 * n_share + 1:2 * n_share + 4]
        share_sems = refs[2 * n_share + 4:]
        share.start(share_in, share_out, share_sems)
        x, y, c = _position()
        me = 4 * x + 2 * y + c
        buf[me] = p_ref[...]
        peers = []
        for r in range(1, N_DEV):
            fx, fy, fc = (r >> 2) & 1, (r >> 1) & 1, r & 1
            px, py, pc = (1 - x if fx else x), (1 - y if fy else y), (1 - c if fc else c)
            peers.append(((px, py, pc), 4 * px + 2 * py + pc))
        sends = [_remote(p_ref, buf.at[me], send_sems, recv_sems, r, dev) for r, (dev, _) in enumerate(peers)]
        for cp in sends:
            cp.start()
        for r, (dev, idx) in enumerate(peers):
            _remote(p_ref, buf.at[idx], send_sems, recv_sems, r, dev).wait_recv()
        for cp in sends:
            cp.wait_send()
        acc = buf[0]
        for k in range(1, N_DEV):
            acc = acc + buf[k]
        o_ref[...] = acc
        share.finish(share_in, share_out, share_sems)

    vm = pl.BlockSpec(memory_space=pltpu.VMEM)
    red, *shared = pl.pallas_call(
        body, in_specs=[vm] + [_HBM] * n_share, out_specs=[vm] + [_HBM] * n_share,
        out_shape=[jax.ShapeDtypeStruct(pack.shape, F32)] + share.out_shapes,
        scratch_shapes=[pltpu.VMEM((N_DEV, rows, width), F32), pltpu.SemaphoreType.DMA((N_DEV - 1,)),
                        pltpu.SemaphoreType.DMA((N_DEV - 1,))] + share.sems,
        input_output_aliases={1 + i: 1 + o for i, o in share.aliases.items()},
        name="small_allreduce")(pack, *share.ins)
    return red, shared


ROW_TILE_MAX = 512
BF16_SUBLANES = 16


def _row_tile(rows):
    if rows <= ROW_TILE_MAX:
        return rows
    return max(t for t in range(BF16_SUBLANES, ROW_TILE_MAX + 1, BF16_SUBLANES) if rows % t == 0)


def _pair_add(keep, recv, name):
    nj, rh, cols = keep.shape
    tr = _row_tile(rh)

    def body(k_ref, r_ref, o_ref):
        o_ref[...] = (k_ref[...].astype(F32) + r_ref[...].astype(F32)).astype(BF16)

    blk = pl.BlockSpec((None, tr, cols), lambda j, i: (j, i, 0))
    return pl.pallas_call(body, grid=(nj, rh // tr), in_specs=[blk, blk], out_specs=blk,
                          out_shape=jax.ShapeDtypeStruct(keep.shape, BF16), name=name,
                          compiler_params=_params())(keep, recv)


def _chip_add(sums, recv, where, name):
    _, rh, cols = sums.shape
    tr = _row_tile(rh)

    def body(w_ref, s_ref, r_ref, o_ref):
        o_ref[...] = ((s_ref[...].astype(F32) + r_ref[0].astype(F32)) + r_ref[1].astype(F32)) + r_ref[2].astype(F32)

    grid_spec = pltpu.PrefetchScalarGridSpec(
        num_scalar_prefetch=1, grid=(rh // tr,),
        in_specs=[pl.BlockSpec((None, tr, cols), lambda i, w_ref: (w_ref[0], i, 0)),
                  pl.BlockSpec((3, tr, cols), lambda i, w_ref: (0, i, 0))],
        out_specs=pl.BlockSpec((None, tr, cols), lambda i, w_ref: (w_ref[1], i, 0)))
    return pl.pallas_call(body, grid_spec=grid_spec, out_shape=jax.ShapeDtypeStruct((2, rh, cols), F32),
                          name=name, compiler_params=_params())(where, sums, recv)


def _adamw(w, g, m, v, name):
    rows, cols = w.shape
    tr = _row_tile(rows)
    assert rows % tr == 0

    def body(w_ref, g_ref, m_ref, v_ref, d_ref, mo_ref, vo_ref):
        gv = g_ref[...]
        m_new = ADAM_B1 * m_ref[...] + (1.0 - ADAM_B1) * gv
        v_new = ADAM_B2 * v_ref[...] + (1.0 - ADAM_B2) * jnp.square(gv)
        m_hat = m_new / (1.0 - ADAM_B1 ** ADAM_STEP)
        v_hat = v_new / (1.0 - ADAM_B2 ** ADAM_STEP)
        d_ref[...] = -ADAM_LR * (m_hat / (jnp.sqrt(v_hat) + ADAM_EPS) + ADAM_WD * w_ref[...])
        mo_ref[...] = m_new
        vo_ref[...] = v_new

    blk = pl.BlockSpec((tr, cols), lambda i: (i, 0))
    shp = jax.ShapeDtypeStruct((rows, cols), F32)
    return pl.pallas_call(body, grid=(rows // tr,), in_specs=[blk] * 4, out_specs=[blk] * 3,
                          out_shape=[shp] * 3, name=name, compiler_params=_params())(w, g, m, v)


def _adamw_halves(w, g2, m, v, name):
    rows, cols = w.shape
    half = cols // 2
    tr = _row_tile(rows)

    def body(w_ref, g_ref, m_ref, v_ref, go_ref, d_ref, mo_ref, vo_ref):
        gv = g_ref[...]
        go_ref[...] = gv
        m_new = ADAM_B1 * m_ref[...] + (1.0 - ADAM_B1) * gv
        v_new = ADAM_B2 * v_ref[...] + (1.0 - ADAM_B2) * jnp.square(gv)
        m_hat = m_new / (1.0 - ADAM_B1 ** ADAM_STEP)
        v_hat = v_new / (1.0 - ADAM_B2 ** ADAM_STEP)
        d_ref[...] = -ADAM_LR * (m_hat / (jnp.sqrt(v_hat) + ADAM_EPS) + ADAM_WD * w_ref[...])
        mo_ref[...] = m_new
        vo_ref[...] = v_new

    blk = pl.BlockSpec((tr, half), lambda hf, i: (i, hf))
    gsp = pl.BlockSpec((None, tr, half), lambda hf, i: (hf, i, 0))
    shp = jax.ShapeDtypeStruct((rows, cols), F32)
    return pl.pallas_call(body, grid=(2, rows // tr), in_specs=[blk, gsp, blk, blk], out_specs=[blk] * 4,
                          out_shape=[shp] * 4, name=name, compiler_params=_params())(w, g2, m, v)


SHARD_W = IN_WIDTH // N_CHIPS


def _half_major(a):
    r, c = a.shape
    return a.reshape(N_CHIPS, 2, r // N_CHIPS // 2, c).transpose(1, 0, 2, 3)


def kernel(x, mem, g_pre, w_in, w_conv, attn_sink, g_mem, w_mem_kv, w_up_a, w_up_b, w_up_m, w_out, g_post, loss_target, m_g_pre, m_w_in, m_w_conv, m_attn_sink, m_g_mem, m_w_mem_kv, m_w_up_a, m_w_up_b, m_w_up_m, m_w_out, m_g_post, v_g_pre, v_w_in, v_w_conv, v_attn_sink, v_g_mem, v_w_mem_kv, v_w_up_a, v_w_up_b, v_w_up_m, v_w_out, v_g_post):
    xi, yi, ci = _position()
    chip = 2 * xi + yi
    where = jnp.stack([chip, ci, N_CHIPS - 1 - chip]).astype(jnp.int32)

    own = [w_in[0].T.astype(BF16), w_mem_kv[0].astype(BF16),
           jnp.concatenate([w_up_a[0], w_up_b[0], w_up_m[0]], axis=0).astype(BF16), w_out[0].astype(BF16)]
    own_conv = jnp.pad(w_conv[0], ((0, 5), (0, 0)))

    def pieces(mine, got):
        got = lax.dynamic_update_slice_in_dim(got, mine[None], chip, axis=0)
        return [got[j] for j in range(N_CHIPS)]

    diag = N_CHIPS - 1 - chip
    diag_blocks = SHARD_BLOCKS + 1
    got_near, got_conv = _run_carry(_gather_weights(own[:1], own_conv, relations=(0, 1)), "gather_w_in")
    w_near = lax.dynamic_update_slice_in_dim(got_near, own[0][None], chip, axis=0).reshape(IN_WIDTH, D_MODEL)
    h = _rmsnorm_fwd(x[0], g_pre, name="pre_norm")
    proj, (got_far, *got_rest) = _proj(
        h, w_near, n_blocks=N_IN_BLOCKS - diag_blocks, where=where, name="proj_near",
        block_of=lambda i, w: i + diag_blocks * (i >= SHARD_BLOCKS * w[2]).astype(jnp.int32),
        carry=_join(_gather_weights(own[:1], relations=(2,)), _gather_weights(own[1:], relations=(0, 1))))
    far = lax.dynamic_index_in_dim(got_far, diag, 0, keepdims=False)
    proj, gathered = _proj_far(h, w_near, far, where, into=proj,
                               carry=_gather_weights(own[1:], relations=(2,), into=got_rest))
    w_kv_full = jnp.concatenate(pieces(own[1], gathered[0]), axis=0)
    up_pieces = pieces(own[2], gathered[1])
    w_up_full = jnp.stack([jnp.concatenate([p[k * A_WIDTH:(k + 1) * A_WIDTH] for p in up_pieces], axis=1)
                           for k in range(3)])
    w_out_full = jnp.concatenate(pieces(own[3], gathered[2]), axis=0)
    w_conv_full = jnp.concatenate([p[:3] for p in pieces(own_conv, got_conv)], axis=1)

    g = _forward_backward(x[0], mem[0], loss_target[0], proj, w_conv_full, attn_sink, g_mem, w_kv_full, w_up_full,
                          w_out_full, g_post)

    half_rows = D_MODEL // 2
    up_parts = (g["w_up"].reshape(3, A_WIDTH, N_CHIPS, D_MODEL // N_CHIPS).transpose(2, 0, 1, 3)
                .reshape(N_CHIPS, 2, 3 * A_WIDTH // 2, D_MODEL // N_CHIPS).transpose(1, 0, 2, 3)).astype(BF16)
    small_parts = [_half_major(g["w_kv"]).astype(BF16), up_parts, _half_major(g["w_out"]).astype(BF16)]

    def dw_in_half(half_of, name, carry):
        dw, carried = _dw_in_t(g["dproj"], h, half_of=half_of, where=where, name=name, carry=carry)
        return dw.reshape(N_CHIPS, SHARD_W, half_rows), carried

    def pick(parts, hf):
        return [lax.dynamic_index_in_dim(p, hf, 0, keepdims=False) for p in parts]

    small_names = ["w_kv", "w_up", "w_out"]
    recv_small = _run_carry(_pair_exchange(pick(small_parts, 1 - ci)), "pair_exchange_small")
    sums_small = [_pair_add(k, r, "pair_add_" + nm)
                  for k, r, nm in zip(pick(small_parts, ci), recv_small, small_names)]
    dw_send, recv3_small = dw_in_half(lambda w: 1 - w[1], "dw_in_send", _chip_exchange(sums_small))
    dw_keep, (recv_in,) = dw_in_half(lambda w: w[1], "dw_in_keep", _pair_exchange([dw_send]))
    sum_in = _pair_add(dw_keep, recv_in, "pair_add_w_in")
    d_h, (recv3_in,) = _d_h(g["dproj"], w_near, far, where, carry=_chip_exchange([sum_in]))
    pairs = [_chip_add(s, r, where, "chip_add_" + nm)
             for s, r, nm in zip([sum_in] + sums_small, [recv3_in] + recv3_small, ["w_in"] + small_names)]
    grad_x, dg_pre = _rmsnorm_bwd(d_h, x[0], g_pre, g["dy"], name="pre_norm_bwd")

    zeros512 = jnp.zeros((1, D_MODEL - A_WIDTH), F32)
    conv_rows = [jnp.concatenate([g["w_conv"][k:k + 1], zeros512], axis=1) for k in range(3)]
    sink_row = jnp.pad(g["sink"][:, 0].reshape(1, N_Q_HEADS), ((0, 0), (0, D_MODEL - N_Q_HEADS)))
    loss_row = jnp.pad(g["loss"], ((0, 0), (0, D_MODEL - LANES)))
    pack = jnp.concatenate([dg_pre, g["g_mem"], g["g_post"]] + conv_rows + [sink_row, loss_row], axis=0)
    red, full = _small_allreduce(pack, _pair_share(pairs))
    loss = red[7, 0]
    small_grads = dict(
        g_pre=red[0:1], g_mem=red[1:2], g_post=red[2:3], attn_sink=red[6:7, :N_Q_HEADS],
        w_conv=lax.dynamic_slice(red[3:6, :A_WIDTH], (0, chip * LANES), (3, LANES)))

    gw_up = full[2].reshape(3, A_WIDTH, D_MODEL // N_CHIPS)
    grads = dict(small_grads, w_mem_kv=full[1].reshape(D_MODEL // N_CHIPS, 2 * MEM_WIDTH),
                 w_up_a=gw_up[0], w_up_b=gw_up[1], w_up_m=gw_up[2],
                 w_out=full[3].reshape(D_MODEL // N_CHIPS, D_MODEL))

    weights = dict(g_pre=g_pre, w_in=w_in, w_conv=w_conv, attn_sink=attn_sink, g_mem=g_mem, w_mem_kv=w_mem_kv,
                   w_up_a=w_up_a, w_up_b=w_up_b, w_up_m=w_up_m, w_out=w_out, g_post=g_post)
    m_in = dict(g_pre=m_g_pre, w_in=m_w_in, w_conv=m_w_conv, attn_sink=m_attn_sink, g_mem=m_g_mem,
                w_mem_kv=m_w_mem_kv, w_up_a=m_w_up_a, w_up_b=m_w_up_b, w_up_m=m_w_up_m, w_out=m_w_out,
                g_post=m_g_post)
    v_in = dict(g_pre=v_g_pre, w_in=v_w_in, w_conv=v_w_conv, attn_sink=v_attn_sink, g_mem=v_g_mem,
                w_mem_kv=v_w_mem_kv, w_up_a=v_w_up_a, w_up_b=v_w_up_b, w_up_m=v_w_up_m, w_out=v_w_out,
                g_post=v_g_post)
    out_g, out_d, out_m, out_v = [], [], [], []
    for nm in ("g_pre", "w_in", "w_conv", "attn_sink", "g_mem", "w_mem_kv", "w_up_a", "w_up_b", "w_up_m", "w_out",
               "g_post"):
        shape = weights[nm].shape
        if nm == "w_in":
            results = _adamw_halves(w_in[0].T, full[0], m_w_in[0].T, v_w_in[0].T, "adamw_w_in")
            for out, t in zip((out_g, out_d, out_m, out_v), results):
                out.append(t.T.reshape(shape))
            continue
        two_d = shape[-2:]
        gr = grads[nm].reshape(two_d)
        d, m_new, v_new = _adamw(weights[nm].reshape(two_d), gr, m_in[nm].reshape(two_d), v_in[nm].reshape(two_d),
                                 "adamw_" + nm)
        out_g.append(gr.reshape(shape))
        out_d.append(d.reshape(shape))
        out_m.append(m_new.reshape(shape))
        out_v.append(v_new.reshape(shape))
    return (loss, grad_x.reshape(x.shape), *out_g, *out_d, *out_m, *out_v)
```

```python
import functools

import jax
import jax.numpy as jnp
from jax import lax
from jax.experimental import pallas as pl
from jax.experimental.pallas import tpu as pltpu

F32 = jnp.float32
BF16 = jnp.bfloat16
MESH = pl.DeviceIdType.MESH

D_MODEL = 1024
EPS = 1e-6
A_WIDTH = 512
HEAD_DIM = 64
N_Q_HEADS = 8
WINDOW_BLOCK = 128
KV_PAD = 512
ROPE_THETA = 500000.0
ROT_DIM = 16
MEM_HEADS = 4
MEM_HEAD_DIM = 128
MEM_WIDTH = 512
IN_WIDTH = 7424
N_CHIPS = 4
LANES = 128
HALF_LANES = 64

PERM_SEGS = ((0, 2560), (2816, 3328), (4352, 7424), (3328, 4352), (2560, 2816))
UNPERM_SEGS = ((0, 2560), (7168, 7424), (2560, 3072), (6144, 7168), (3072, 6144))
COL_A, W_A = 0, 2048
COL_B, W_B = 2, 1024
COL_G, W_G = 1, 3072
COL_M, W_M = 6, 1024
COL_KV, W_KV = 28, 256

ADAM_LR = 0.001
ADAM_B1 = 0.9
ADAM_B2 = 0.999
ADAM_EPS = 1e-08
ADAM_WD = 0.01
ADAM_STEP = 10

VMEM_LIMIT_BYTES = 48 * 1024 * 1024


_HBM = pl.BlockSpec(memory_space=pltpu.HBM)


def _params(**kw):
    return pltpu.CompilerParams(vmem_limit_bytes=VMEM_LIMIT_BYTES, **kw)


def _sigmoid(v):
    return jax.nn.sigmoid(v)


_DIMS = {"nn": (((1,), (0,)), ((), ())), "nt": (((1,), (1,)), ((), ())), "tn": (((0,), (0,)), ((), ()))}


class _Carry:
    def __init__(self, ins, out_shapes, sems, start, finish, aliases=None):
        self.ins, self.out_shapes, self.sems = list(ins), list(out_shapes), list(sems)
        self.start, self.finish, self.aliases = start, finish, dict(aliases or {})


def _join(*carries):
    def split(seq, counts):
        pos, parts = 0, []
        for n in counts:
            parts.append(seq[pos:pos + n])
            pos += n
        return parts

    n_in = [len(c.ins) for c in carries]
    n_out = [len(c.out_shapes) for c in carries]
    n_sem = [len(c.sems) for c in carries]

    def run(which):
        def go(ins, outs, sems):
            for c, i, o, sm in zip(carries, split(ins, n_in), split(outs, n_out), split(sems, n_sem)):
                getattr(c, which)(i, o, sm)
        return go

    aliases = {}
    for k, c in enumerate(carries):
        aliases.update({sum(n_in[:k]) + i: sum(n_out[:k]) + o for i, o in c.aliases.items()})
    return _Carry([a for c in carries for a in c.ins], [sh for c in carries for sh in c.out_shapes],
                  [sm for c in carries for sm in c.sems], run("start"), run("finish"), aliases)


def _carried_call(body, carry, *, grid, in_specs, out_specs, out_shape, scratch, operands, name, prefetch=None,
                  aliases=None):
    n_in, n_out, n_scr = len(in_specs), len(out_specs), len(scratch)
    c_in = len(carry.ins) if carry else 0
    c_out = len(carry.out_shapes) if carry else 0
    n_pre = 0 if prefetch is None else 1
    steps = 1
    for g in grid:
        steps *= g

    def wrapped(*refs):
        refs = refs[n_pre:]
        ins, cins = refs[:n_in], refs[n_in:n_in + c_in]
        outs = refs[n_in + c_in:n_in + c_in + n_out]
        couts = refs[n_in + c_in + n_out:n_in + c_in + n_out + c_out]
        rest = refs[n_in + c_in + n_out + c_out:]
        scr, sems = rest[:n_scr], rest[n_scr:]
        if carry:
            step = pl.program_id(0)
            for ax in range(1, len(grid)):
                step = step * grid[ax] + pl.program_id(ax)

            @pl.when(step == 0)
            def _():
                carry.start(cins, couts, sems)

        body(ins, outs, scr)
        if carry:
            @pl.when(step == steps - 1)
            def _():
                carry.finish(cins, couts, sems)

    all_aliases = {n_pre + i: o for i, o in (aliases or {}).items()}
    if carry:
        all_aliases.update({n_pre + n_in + i: n_out + o for i, o in carry.aliases.items()})
    all_in = list(in_specs) + [_HBM] * c_in
    all_out = list(out_specs) + [_HBM] * c_out
    all_scratch = list(scratch) + (carry.sems if carry else [])
    if n_pre:
        spec = dict(grid_spec=pltpu.PrefetchScalarGridSpec(num_scalar_prefetch=1, grid=grid, in_specs=all_in,
                                                           out_specs=all_out, scratch_shapes=all_scratch))
        pre = (prefetch,)
    else:
        spec = dict(grid=grid, in_specs=all_in, out_specs=all_out, scratch_shapes=all_scratch)
        pre = ()
    results = pl.pallas_call(
        wrapped, out_shape=list(out_shape) + (carry.out_shapes if carry else []), input_output_aliases=all_aliases,
        name=name, compiler_params=_params(), **spec)(*pre, *operands, *(carry.ins if carry else []))
    return list(results[:n_out]), list(results[n_out:])


def _matmul(a, b, *, mode, out_dtype, tm, tn, tk, name, j_outer=False, carry=None):
    if mode == "nn":
        (m, k), (_, n) = a.shape, b.shape
    elif mode == "nt":
        (m, k), (n, _) = a.shape, b.shape
    else:
        (k, m), (_, n) = a.shape, b.shape
    tm, tn, tk = min(tm, m), min(tn, n), min(tk, k)
    assert m % tm == 0 and n % tn == 0 and k % tk == 0
    ni, nj, nk = m // tm, n // tn, k // tk
    dims = _DIMS[mode]

    def ij(g0, g1):
        return (g1, g0) if j_outer else (g0, g1)

    if mode == "nn":
        a_spec = pl.BlockSpec((tm, tk), lambda g0, g1, kk: (ij(g0, g1)[0], kk))
        b_spec = pl.BlockSpec((tk, tn), lambda g0, g1, kk: (kk, ij(g0, g1)[1]))
    elif mode == "nt":
        a_spec = pl.BlockSpec((tm, tk), lambda g0, g1, kk: (ij(g0, g1)[0], kk))
        b_spec = pl.BlockSpec((tn, tk), lambda g0, g1, kk: (ij(g0, g1)[1], kk))
    else:
        a_spec = pl.BlockSpec((tk, tm), lambda g0, g1, kk: (kk, ij(g0, g1)[0]))
        b_spec = pl.BlockSpec((tk, tn), lambda g0, g1, kk: (kk, ij(g0, g1)[1]))
    o_spec = pl.BlockSpec((tm, tn), lambda g0, g1, kk: ij(g0, g1))

    def part(a_ref, b_ref):
        return lax.dot_general(a_ref[...].astype(BF16), b_ref[...].astype(BF16), dims,
                               preferred_element_type=F32)

    if nk == 1:
        def body(ins, outs, scr):
            outs[0][...] = part(*ins).astype(out_dtype)
        scratch = []
    else:
        def body(ins, outs, scr):
            kk = pl.program_id(2)
            acc_ref = scr[0]

            @pl.when(kk == 0)
            def _():
                acc_ref[...] = part(*ins)

            @pl.when(kk > 0)
            def _():
                acc_ref[...] += part(*ins)

            @pl.when(kk == nk - 1)
            def _():
                outs[0][...] = acc_ref[...].astype(out_dtype)
        scratch = [pltpu.VMEM((tm, tn), F32)]

    grid = (nj, ni, nk) if j_outer else (ni, nj, nk)
    (out,), carried = _carried_call(
        body, carry, grid=grid, in_specs=[a_spec, b_spec], out_specs=[o_spec],
        out_shape=[jax.ShapeDtypeStruct((m, n), out_dtype)], scratch=scratch, operands=(a, b), name=name)
    return (out, carried) if carry else out


IN_BLOCK = 256
N_IN_BLOCKS = IN_WIDTH // IN_BLOCK
SHARD_BLOCKS = (IN_WIDTH // N_CHIPS) // IN_BLOCK
BLOCK_RUNS = tuple((a // IN_BLOCK, sum(d - c for c, d in PERM_SEGS[:k]) // IN_BLOCK, (b - a) // IN_BLOCK)
                   for k, (a, b) in enumerate(PERM_SEGS))


def _perm_block(r):
    p = r
    for ref0, perm0, n in BLOCK_RUNS:
        p = jnp.where((r >= ref0) & (r < ref0 + n), r - ref0 + perm0, p)
    return p


def _proj(h, w_t, *, n_blocks, block_of, where, name, carry=None):
    s, d = h.shape

    def body(ins, outs, scr):
        outs[0][...] = lax.dot_general(ins[0][...], ins[1][...], _DIMS["nt"], preferred_element_type=F32)

    (proj,), carried = _carried_call(
        body, carry, grid=(n_blocks,),
        in_specs=[pl.BlockSpec((s, d), lambda i, w: (0, 0)), pl.BlockSpec((IN_BLOCK, d), lambda i, w: (block_of(i, w), 0))],
        out_specs=[pl.BlockSpec((s, IN_BLOCK), lambda i, w: (0, _perm_block(block_of(i, w))))],
        out_shape=[jax.ShapeDtypeStruct((s, IN_WIDTH), F32)], scratch=[], operands=(h, w_t), name=name,
        prefetch=where)
    return (proj, carried) if carry else proj


def _proj_far(h, w_near, far, where, *, into, carry=None):
    s, d = h.shape
    n_blocks = SHARD_BLOCKS + 1
    lead = IN_WIDTH // N_CHIPS - SHARD_BLOCKS * IN_BLOCK

    def body(ins, outs, scr):
        where_ref, h_ref, w_hbm, far_hbm, _ = ins
        win, sem = scr
        i = pl.program_id(0)

        @pl.when(i == 0)
        def _():
            dg = where_ref[2]
            rows = pl.ds(pl.multiple_of(dg * (SHARD_BLOCKS * IN_BLOCK), IN_BLOCK), n_blocks * IN_BLOCK)
            window = pltpu.make_async_copy(w_hbm.at[rows], win, sem)
            window.start()
            window.wait()
            shard = pltpu.make_async_copy(far_hbm, win.at[pl.ds(pl.multiple_of(dg * lead, BF16_SUBLANES), SHARD_W)], sem)
            shard.start()
            shard.wait()

        blk = win[pl.ds(pl.multiple_of(i * IN_BLOCK, IN_BLOCK), IN_BLOCK), :]
        outs[0][...] = lax.dot_general(h_ref[...], blk, _DIMS["nt"], preferred_element_type=F32)

    anysp = pl.BlockSpec(memory_space=pl.ANY)
    (proj,), carried = _carried_call(
        body, carry, grid=(n_blocks,),
        in_specs=[pl.BlockSpec(memory_space=pltpu.SMEM), pl.BlockSpec((s, d), lambda i, w: (0, 0)), anysp, anysp, anysp],
        out_specs=[pl.BlockSpec((s, IN_BLOCK), lambda i, w: (0, _perm_block(i + SHARD_BLOCKS * w[2])))],
        out_shape=[jax.ShapeDtypeStruct((s, IN_WIDTH), F32)],
        scratch=[pltpu.VMEM((n_blocks * IN_BLOCK, d), BF16), pltpu.SemaphoreType.DMA],
        operands=(where, h, w_near, far, into), name="proj_far", prefetch=where, aliases={4: 0})
    return (proj, carried) if carry else proj


def _dw_in_t(dproj, h_t, *, half_of, where, name, carry=None):
    d, s = h_t.shape
    c = d // 2

    def body(ins, outs, scr):
        outs[0][...] = lax.dot_general(ins[1][...], ins[0][...], _DIMS["nn"], preferred_element_type=F32).T.astype(BF16)

    (dw,), carried = _carried_call(
        body, carry, grid=(N_IN_BLOCKS,),
        in_specs=[pl.BlockSpec((s, IN_BLOCK), lambda r, w: (0, _perm_block(r))),
                  pl.BlockSpec((c, s), lambda r, w: (half_of(w), 0))],
        out_specs=[pl.BlockSpec((IN_BLOCK, c), lambda r, w: (r, 0))],
        out_shape=[jax.ShapeDtypeStruct((IN_WIDTH, c), BF16)], scratch=[], operands=(dproj, h_t), name=name,
        prefetch=where)
    return (dw, carried) if carry else dw


def _d_h(dproj, w_near, far, where, *, carry=None):
    s = dproj.shape[0]
    d = w_near.shape[1]
    tm = min(s, 256)

    def body(ins, outs, scr):
        where_ref, a_ref, w_hbm, far_hbm = ins
        w_ref, sem = scr

        @pl.when(pl.program_id(0) == 0)
        def _():
            whole = pltpu.make_async_copy(w_hbm, w_ref, sem)
            whole.start()
            whole.wait()
            rows = pl.ds(pl.multiple_of(where_ref[2] * SHARD_W, BF16_SUBLANES), SHARD_W)
            part = pltpu.make_async_copy(far_hbm, w_ref.at[rows], sem)
            part.start()
            part.wait()

        acc = None
        for ref0, perm0, n in BLOCK_RUNS:
            term = jnp.dot(a_ref[:, perm0 * IN_BLOCK:(perm0 + n) * IN_BLOCK],
                           w_ref[ref0 * IN_BLOCK:(ref0 + n) * IN_BLOCK, :], preferred_element_type=F32)
            acc = term if acc is None else acc + term
        outs[0][...] = acc

    anysp = pl.BlockSpec(memory_space=pl.ANY)
    (dh,), carried = _carried_call(
        body, carry, grid=(s // tm,),
        in_specs=[pl.BlockSpec(memory_space=pltpu.SMEM), pl.BlockSpec((tm, IN_WIDTH), lambda i: (i, 0)), anysp, anysp],
        out_specs=[pl.BlockSpec((tm, d), lambda i: (i, 0))],
        out_shape=[jax.ShapeDtypeStruct((s, d), F32)],
        scratch=[pltpu.VMEM((IN_WIDTH, d), BF16), pltpu.SemaphoreType.DMA],
        operands=(where, dproj, w_near, far), name="d_h")
    return (dh, carried) if carry else dh


def _rmsnorm_fwd(x, g, *, name, transposed=False, carry=None):
    s, d = x.shape
    ts = min(512, s)

    def body(ins, outs, scr):
        xv = ins[0][...]
        r = lax.rsqrt(jnp.mean(xv * xv, axis=-1, keepdims=True) + EPS)
        hv = (xv * r) * ins[1][...]
        outs[0][...] = hv.astype(BF16)
        if transposed:
            outs[1][...] = hv.T.astype(BF16)

    out_specs = [pl.BlockSpec((ts, d), lambda i: (i, 0))]
    out_shape = [jax.ShapeDtypeStruct((s, d), BF16)]
    if transposed:
        out_specs.append(pl.BlockSpec((d, ts), lambda i: (0, i)))
        out_shape.append(jax.ShapeDtypeStruct((d, s), BF16))
    outs, carried = _carried_call(
        body, carry, grid=(s // ts,),
        in_specs=[pl.BlockSpec((ts, d), lambda i: (i, 0)), pl.BlockSpec((1, d), lambda i: (0, 0))],
        out_specs=out_specs, out_shape=out_shape, scratch=[], operands=(x, g), name=name)
    result = tuple(outs) if transposed else outs[0]
    return (result, carried) if carry else result


def _rmsnorm_bwd(dh, x, g, res, *, name, carry=None):
    s, d = x.shape
    ts = min(256, s)

    def body(ins, outs, scr):
        dh_ref, x_ref, g_ref, res_ref = ins
        dx_ref, dg_ref = outs
        xv = x_ref[...]
        r = lax.rsqrt(jnp.mean(xv * xv, axis=-1, keepdims=True) + EPS)
        xh = xv * r
        dhv = dh_ref[...]
        part = jnp.sum(dhv * xh, axis=0, keepdims=True)

        @pl.when(pl.program_id(0) == 0)
        def _():
            dg_ref[...] = part

        @pl.when(pl.program_id(0) > 0)
        def _():
            dg_ref[...] += part

        dxh = dhv * g_ref[...]
        dx_ref[...] = res_ref[...] + r * (dxh - xh * jnp.mean(dxh * xh, axis=-1, keepdims=True))

    row = pl.BlockSpec((ts, d), lambda i: (i, 0))
    vec = pl.BlockSpec((1, d), lambda i: (0, 0))
    outs, carried = _carried_call(
        body, carry, grid=(s // ts,), in_specs=[row, row, vec, row], out_specs=[row, vec],
        out_shape=[jax.ShapeDtypeStruct((s, d), F32), jax.ShapeDtypeStruct((1, d), F32)],
        scratch=[], operands=(dh, x, g, res), name=name)
    return (*outs, carried) if carry else tuple(outs)


MID_TILE = 256


def _gated_branches(y_refs, wup_ref, gl):
    d = D_MODEL
    us = [jnp.dot(y_refs[k][...], wup_ref[k], preferred_element_type=F32) for k in range(3)]
    sg = [_sigmoid(gl[:, k * d:(k + 1) * d]) for k in range(3)]
    return us, sg


def _mid_fwd(ya, yb, ym, proj, x, tgt, w_up, w_out, g_post):
    s, d = x.shape
    ts = MID_TILE

    def body(ya_ref, yb_ref, ym_ref, g_ref, x_ref, t_ref, wup_ref, wout_ref, gp_ref,
             m_ref, do_ref, dy_ref, dg_ref, loss_ref):
        us, sg = _gated_branches((ya_ref, yb_ref, ym_ref), wup_ref, g_ref[...])
        merged = (sg[0] * us[0] + sg[1] * us[1] + sg[2] * us[2]).astype(BF16)
        m_ref[...] = merged
        ov = jnp.dot(merged, wout_ref[...], preferred_element_type=F32)
        r = lax.rsqrt(jnp.mean(ov * ov, axis=-1, keepdims=True) + EPS)
        nh = ov * r
        gv = gp_ref[...]
        e = (x_ref[...] + nh * gv) - t_ref[...]
        lpart = 0.5 * jnp.sum(jnp.mean(e * e, axis=-1, keepdims=True), axis=0, keepdims=True)
        dy = e * (1.0 / d)
        dgp = jnp.sum(dy * nh, axis=0, keepdims=True)

        @pl.when(pl.program_id(0) == 0)
        def _():
            dg_ref[...] = dgp
            loss_ref[...] = jnp.broadcast_to(lpart, loss_ref.shape)

        @pl.when(pl.program_id(0) > 0)
        def _():
            dg_ref[...] += dgp
            loss_ref[...] += jnp.broadcast_to(lpart, loss_ref.shape)

        dn = dy * gv
        dy_ref[...] = dy
        do_ref[...] = (r * (dn - nh * jnp.mean(dn * nh, axis=-1, keepdims=True))).astype(BF16)

    row = pl.BlockSpec((ts, d), lambda i: (i, 0))
    ysp = pl.BlockSpec((ts, A_WIDTH), lambda i: (i, 0))
    vec = pl.BlockSpec((1, d), lambda i: (0, 0))
    return pl.pallas_call(
        body, grid=(s // ts,),
        in_specs=[ysp, ysp, ysp, pl.BlockSpec((ts, W_G), lambda i: (i, COL_G)), row, row,
                  pl.BlockSpec((3, A_WIDTH, d), lambda i: (0, 0, 0)), pl.BlockSpec((d, d), lambda i: (0, 0)), vec],
        out_specs=[row, row, row, vec, pl.BlockSpec((1, LANES), lambda i: (0, 0))],
        out_shape=[jax.ShapeDtypeStruct((s, d), BF16), jax.ShapeDtypeStruct((s, d), BF16),
                   jax.ShapeDtypeStruct((s, d), F32), jax.ShapeDtypeStruct((1, d), F32),
                   jax.ShapeDtypeStruct((1, LANES), F32)],
        name="mid_fwd", compiler_params=_params())(ya, yb, ym, proj, x, tgt, w_up, w_out, g_post)


def _mid_bwd(d_out, merged, ya, yb, ym, proj, w_up, w_out):
    s, d = merged.shape
    ts = MID_TILE
    last = s // ts - 1

    def body(do_ref, m_ref, ya_ref, yb_ref, ym_ref, g_ref, wup_ref, wout_ref,
             dp_ref, dya_ref, dyb_ref, dym_ref, dwup_hbm, dwout_hbm, dwup_acc, dwout_acc):
        i = pl.program_id(0)

        @pl.when(i == 0)
        def _():
            dwup_acc[...] = jnp.zeros_like(dwup_acc)
            dwout_acc[...] = jnp.zeros_like(dwout_acc)

        y_refs = (ya_ref, yb_ref, ym_ref)
        us, sg = _gated_branches(y_refs, wup_ref, g_ref[...])
        dov = do_ref[...]
        dwout_acc[...] += lax.dot_general(m_ref[...], dov, _DIMS["tn"], preferred_element_type=F32)
        dm = lax.dot_general(dov, wout_ref[...], _DIMS["nt"], preferred_element_type=F32)
        for k, dy_ref in enumerate((dya_ref, dyb_ref, dym_ref)):
            dp_ref[:, k * d:(k + 1) * d] = ((dm * us[k]) * (sg[k] * (1.0 - sg[k]))).astype(BF16)
            du = (sg[k] * dm).astype(BF16)
            dy_ref[...] = lax.dot_general(du, wup_ref[k], _DIMS["nt"], preferred_element_type=F32)
            dwup_acc[k] += lax.dot_general(y_refs[k][...], du, _DIMS["tn"], preferred_element_type=F32)

        @pl.when(i == last)
        def _():
            pltpu.sync_copy(dwup_acc, dwup_hbm)
            pltpu.sync_copy(dwout_acc, dwout_hbm)

    row = pl.BlockSpec((ts, d), lambda i: (i, 0))
    ysp = pl.BlockSpec((ts, A_WIDTH), lambda i: (i, 0))
    gsp = pl.BlockSpec((ts, W_G), lambda i: (i, COL_G))
    anysp = pl.BlockSpec(memory_space=pl.ANY)
    yshape = jax.ShapeDtypeStruct((s, A_WIDTH), F32)
    return pl.pallas_call(
        body, grid=(s // ts,),
        in_specs=[row, row, ysp, ysp, ysp, gsp, pl.BlockSpec((3, A_WIDTH, d), lambda i: (0, 0, 0)),
                  pl.BlockSpec((d, d), lambda i: (0, 0))],
        out_specs=[gsp, ysp, ysp, ysp, anysp, anysp],
        out_shape=[jax.ShapeDtypeStruct((s, IN_WIDTH), BF16), yshape, yshape, yshape,
                   jax.ShapeDtypeStruct((3, A_WIDTH, d), F32), jax.ShapeDtypeStruct((d, d), F32)],
        scratch_shapes=[pltpu.VMEM((3, A_WIDTH, d), F32), pltpu.VMEM((d, d), F32)],
        name="mid_bwd", compiler_params=_params())(d_out, merged, ya, yb, ym, proj, w_up, w_out)


def _conv_core(blk, prev, nxt, w, i, last, ts):
    c = A_WIDTH
    ab, ac, ax, az = blk[:, :c], blk[:, c:2 * c], blk[:, 2 * c:3 * c], blk[:, 3 * c:]
    cu = ac * ax
    cu_prev = (prev[7:8, c:2 * c] * prev[7:8, 2 * c:3 * c]) * jnp.where(i > 0, 1.0, 0.0)
    cu_next = (nxt[0:1, c:2 * c] * nxt[0:1, 2 * c:3 * c]) * jnp.where(i < last, 1.0, 0.0)
    row = lax.broadcasted_iota(jnp.int32, (ts, c), 0)
    cm1 = jnp.where(row == 0, cu_prev, pltpu.roll(cu, 1, 0))
    cp1 = jnp.where(row == ts - 1, cu_next, pltpu.roll(cu, ts - 1, 0))
    yc = cm1 * w[0:1] + cu * w[1:2] + cp1 * w[2:3]
    return ab, ac, ax, az, cu, cm1, cp1, yc, row


def _halo_specs(ts, width, col, nblk8):
    prev = pl.BlockSpec((8, width), lambda i: (jnp.maximum(i * (ts // 8) - 1, 0), col))
    nxt = pl.BlockSpec((8, width), lambda i: (jnp.minimum((i + 1) * (ts // 8), nblk8 - 1), col))
    return prev, nxt


def _conv_fwd(proj, w_conv):
    s = proj.shape[0]
    ts = 256
    last = s // ts - 1

    def body(a_ref, ap_ref, an_ref, w_ref, ya_ref):
        i = pl.program_id(0)
        ab, _, _, az, _, _, _, yc, _ = _conv_core(a_ref[...], ap_ref[...], an_ref[...], w_ref[...], i, last, ts)
        ya_ref[...] = ((ab * yc) * (az * _sigmoid(az))).astype(BF16)

    prev, nxt = _halo_specs(ts, W_A, COL_A, s // 8)
    return pl.pallas_call(
        body, grid=(s // ts,),
        in_specs=[pl.BlockSpec((ts, W_A), lambda i: (i, COL_A)), prev, nxt,
                  pl.BlockSpec((3, A_WIDTH), lambda i: (0, 0))],
        out_specs=pl.BlockSpec((ts, A_WIDTH), lambda i: (i, 0)),
        out_shape=jax.ShapeDtypeStruct((s, A_WIDTH), BF16), name="conv_fwd",
        compiler_params=_params())(proj, proj, proj, w_conv)


def _conv_bwd(proj, w_conv, dya, dproj):
    s = proj.shape[0]
    ts = 256
    last = s // ts - 1
    c = A_WIDTH

    def body(a_ref, ap_ref, an_ref, w_ref, d_ref, dp_ref, dn_ref, _, dproj_ref, dw_ref):
        i = pl.program_id(0)
        w = w_ref[...]
        prev, nxt = ap_ref[...], an_ref[...]
        ab, ac, ax, az, cu, cm1, cp1, yc, row = _conv_core(a_ref[...], prev, nxt, w, i, last, ts)
        sg = _sigmoid(az)
        sz = az * sg
        dya_v = d_ref[...]
        dyc = dya_v * sz * ab
        dproj_ref[:, :c] = (dya_v * sz * yc).astype(BF16)
        dproj_ref[:, 3 * c:] = (dya_v * (ab * yc) * (sg * (1.0 + az * (1.0 - sg)))).astype(BF16)

        def halo_dyc(a_row, d_row):
            azr = a_row[:, 3 * c:]
            return d_row * (azr * _sigmoid(azr)) * a_row[:, :c]

        dyc_prev = halo_dyc(prev[7:8], dp_ref[...][7:8]) * jnp.where(i > 0, 1.0, 0.0)
        dyc_next = halo_dyc(nxt[0:1], dn_ref[...][0:1]) * jnp.where(i < last, 1.0, 0.0)
        dyc_m1 = jnp.where(row == 0, dyc_prev, pltpu.roll(dyc, 1, 0))
        dyc_p1 = jnp.where(row == ts - 1, dyc_next, pltpu.roll(dyc, ts - 1, 0))
        dcu = dyc_p1 * w[0:1] + dyc * w[1:2] + dyc_m1 * w[2:3]
        dproj_ref[:, c:2 * c] = (dcu * ax).astype(BF16)
        dproj_ref[:, 2 * c:3 * c] = (dcu * ac).astype(BF16)
        dw = [jnp.sum(dyc * t, axis=0, keepdims=True) for t in (cm1, cu, cp1)]

        @pl.when(i == 0)
        def _():
            for k in range(3):
                dw_ref[k:k + 1, :] = dw[k]

        @pl.when(i > 0)
        def _():
            for k in range(3):
                dw_ref[k:k + 1, :] += dw[k]

    prev, nxt = _halo_specs(ts, W_A, COL_A, s // 8)
    dprev, dnxt = _halo_specs(ts, A_WIDTH, 0, s // 8)
    return pl.pallas_call(
        body, grid=(s // ts,),
        in_specs=[pl.BlockSpec((ts, W_A), lambda i: (i, COL_A)), prev, nxt,
                  pl.BlockSpec((3, A_WIDTH), lambda i: (0, 0)),
                  pl.BlockSpec((ts, A_WIDTH), lambda i: (i, 0)), dprev, dnxt,
                  pl.BlockSpec(memory_space=pl.ANY)],
        out_specs=[pl.BlockSpec((ts, W_A), lambda i: (i, COL_A)), pl.BlockSpec((3, A_WIDTH), lambda i: (0, 0))],
        out_shape=[jax.ShapeDtypeStruct(dproj.shape, BF16), jax.ShapeDtypeStruct((3, A_WIDTH), F32)],
        input_output_aliases={7: 0}, name="conv_bwd",
        compiler_params=_params())(proj, proj, proj, w_conv, dya, dya, dya, dproj)


def _rope_tables(s):
    half = ROT_DIM // 2
    dim = jnp.arange(LANES) % HEAD_DIM
    inv_freq = jnp.power(jnp.float32(ROPE_THETA), -(dim % half).astype(F32) * (2.0 / ROT_DIM))
    ang = jnp.arange(s).astype(F32)[:, None] * inv_freq[None, :]
    cos, sin = jnp.cos(ang), jnp.sin(ang)
    first, second = (dim < half)[None, :], ((dim >= half) & (dim < ROT_DIM))[None, :]
    c = jnp.where(first | second, cos, 1.0)
    s1 = jnp.where(first, -sin, 0.0)
    s2 = jnp.where(second, sin, 0.0)
    return jnp.concatenate([c, s1, s2], axis=1)


def _rope(t, tab):
    return (t * tab[:, :LANES] + pltpu.roll(t, LANES - 8, 1) * tab[:, LANES:2 * LANES]
            + pltpu.roll(t, 8, 1) * tab[:, 2 * LANES:])


def _rope_transpose(dt, tab):
    return (dt * tab[:, :LANES] + pltpu.roll(dt * tab[:, LANES:2 * LANES], 8, 1)
            + pltpu.roll(dt * tab[:, 2 * LANES:], LANES - 8, 1))


def _rope_kv(proj, tab):
    s = proj.shape[0]
    nb = s // KV_PAD

    def body(kv_ref, t_ref, k_ref, v_ref):
        j = pl.program_id(0)
        inside = jnp.where((j > 0) & (j <= nb), 1.0, 0.0)
        kv = kv_ref[...]
        k_ref[...] = (_rope(kv[:, :LANES], t_ref[...]) * inside).astype(BF16)
        v_ref[...] = (kv[:, LANES:] * inside).astype(BF16)

    def src(j):
        return jnp.clip(j - 1, 0, nb - 1)

    o_spec = pl.BlockSpec((KV_PAD, LANES), lambda j: (j, 0))
    shp = jax.ShapeDtypeStruct((s + 2 * KV_PAD, LANES), BF16)
    return pl.pallas_call(
        body, grid=(nb + 2,),
        in_specs=[pl.BlockSpec((KV_PAD, W_KV), lambda j: (src(j), COL_KV)),
                  pl.BlockSpec((KV_PAD, 3 * LANES), lambda j: (src(j), 0))],
        out_specs=[o_spec, o_spec], out_shape=[shp, shp], name="rope_kv",
        compiler_params=_params())(proj, tab)


def _rope_kv_bwd(dkpad, dvpad, tab, dproj):
    s = tab.shape[0]
    nb = s // KV_PAD

    def body(dk_ref, dv_ref, t_ref, _, dp_ref):
        dp_ref[:, :LANES] = _rope_transpose(dk_ref[...], t_ref[...]).astype(BF16)
        dp_ref[:, LANES:] = dv_ref[...].astype(BF16)

    pad_spec = pl.BlockSpec((KV_PAD, LANES), lambda j: (j + 1, 0))
    return pl.pallas_call(
        body, grid=(nb,),
        in_specs=[pad_spec, pad_spec, pl.BlockSpec((KV_PAD, 3 * LANES), lambda j: (j, 0)),
                  pl.BlockSpec(memory_space=pl.ANY)],
        out_specs=pl.BlockSpec((KV_PAD, W_KV), lambda j: (j, COL_KV)),
        out_shape=jax.ShapeDtypeStruct(dproj.shape, BF16), input_output_aliases={3: 0},
        name="rope_kv_bwd", compiler_params=_params())(dkpad, dvpad, tab, dproj)


def _window_start(n):
    return pl.multiple_of((n - 1) * WINDOW_BLOCK + KV_PAD, WINDOW_BLOCK)


def _window_operands(k_ref, v_ref, n, lo):
    start = _window_start(n)
    kw = k_ref[pl.ds(start, 3 * WINDOW_BLOCK), :].astype(F32)
    vw = v_ref[pl.ds(start, 3 * WINDOW_BLOCK), :].astype(F32)
    kr, vr = pltpu.roll(kw, HALF_LANES, 1), pltpu.roll(vw, HALF_LANES, 1)
    k2 = (jnp.where(lo, kw, kr).astype(BF16), jnp.where(lo, kr, kw).astype(BF16))
    v2 = (jnp.where(lo, vw, vr).astype(BF16), jnp.where(lo, vr, vw).astype(BF16))
    return k2, v2


HEADS_PER_GROUP = 4
SWA_FWD_BLOCKS = 1
SWA_BWD_BLOCKS = 2


def _window_mask(n, s):
    wb = WINDOW_BLOCK
    shape = (HEADS_PER_GROUP * wb, 3 * wb)
    qi = lax.broadcasted_iota(jnp.int32, shape, 0) & (wb - 1)
    kj = lax.broadcasted_iota(jnp.int32, shape, 1)
    kpos = kj + (n - 1) * wb
    return (kj >= qi) & (kj <= qi + 2 * wb) & (kpos >= 0) & (kpos < s)


def _stack_heads(pair0, pair1, lo):
    return jnp.concatenate([jnp.where(lo, pair0, 0.0), jnp.where(lo, 0.0, pair0),
                            jnp.where(lo, pair1, 0.0), jnp.where(lo, 0.0, pair1)], axis=0)


def _unstack_pair(stacked, i, lo):
    wb = WINDOW_BLOCK
    return jnp.where(lo, stacked[2 * i * wb:(2 * i + 1) * wb], stacked[(2 * i + 1) * wb:(2 * i + 2) * wb])


def _sink_column(sink_ref, g):
    wb = WINDOW_BLOCK
    return jnp.concatenate([jnp.full((wb, 1), sink_ref[0, HEADS_PER_GROUP * g + i], F32)
                            for i in range(HEADS_PER_GROUP)], axis=0)


def _head_exp(q4, k2g, valid, sink):
    sc = lax.dot_general(q4, k2g, _DIMS["nt"], preferred_element_type=F32) * (HEAD_DIM ** -0.5)
    sc = jnp.where(valid, sc, -jnp.inf)
    m = jnp.maximum(jnp.max(sc, axis=1, keepdims=True), sink)
    return jnp.exp(sc - m).astype(BF16), jnp.exp(sink - m)


def _swa_fwd(proj, kpad, vpad, tab, sink):
    s = proj.shape[0]
    wb = WINDOW_BLOCK

    def body(b_ref, k_ref, v_ref, t_ref, sink_ref, o_ref, y_ref):
        lo = lax.broadcasted_iota(jnp.int32, (wb, LANES), 1) < HALF_LANES
        lo_w = lax.broadcasted_iota(jnp.int32, (3 * wb, LANES), 1) < HALF_LANES
        for sub in range(SWA_FWD_BLOCKS):
            n = pl.program_id(0) * SWA_FWD_BLOCKS + sub
            rows = slice(sub * wb, (sub + 1) * wb)
            k2, v2 = _window_operands(k_ref, v_ref, n, lo_w)
            valid = _window_mask(n, s)
            tab_v = t_ref[rows, :]
            ones = jnp.ones((3 * wb, LANES), BF16)
            for g in range(2):
                qr = [_rope(b_ref[rows, (2 * g + i) * LANES:(2 * g + i + 1) * LANES], tab_v) for i in range(2)]
                q4 = _stack_heads(qr[0], qr[1], lo).astype(BF16)
                e, es = _head_exp(q4, k2[g], valid, _sink_column(sink_ref, g))
                ox = jnp.dot(e, jnp.concatenate([v2[g], ones], axis=1), preferred_element_type=F32)
                o4 = ox[:, :LANES] * (1.0 / (ox[:, LANES:] + es))
                for i in range(2):
                    cols = slice((2 * g + i) * LANES, (2 * g + i + 1) * LANES)
                    op = _unstack_pair(o4, i, lo)
                    o_ref[rows, cols] = op
                    zp = b_ref[rows, A_WIDTH + cols.start:A_WIDTH + cols.stop]
                    y_ref[rows, cols] = (op * (zp * _sigmoid(zp))).astype(BF16)

    tq = SWA_FWD_BLOCKS * wb
    pad_spec = pl.BlockSpec((s + 2 * KV_PAD, LANES), lambda n: (0, 0))
    o_spec = pl.BlockSpec((tq, A_WIDTH), lambda n: (n, 0))
    return pl.pallas_call(
        body, grid=(s // tq,),
        in_specs=[pl.BlockSpec((tq, W_B), lambda n: (n, COL_B)), pad_spec, pad_spec,
                  pl.BlockSpec((tq, 3 * LANES), lambda n: (n, 0)),
                  pl.BlockSpec(memory_space=pltpu.SMEM)],
        out_specs=[o_spec, o_spec],
        out_shape=[jax.ShapeDtypeStruct((s, A_WIDTH), F32), jax.ShapeDtypeStruct((s, A_WIDTH), BF16)],
        name="swa_fwd", compiler_params=_params())(proj, kpad, vpad, tab, sink)


def _swa_bwd(proj, kpad, vpad, tab, sink, o_attn, dyb, dproj):
    s = proj.shape[0]
    wb = WINDOW_BLOCK
    scale = HEAD_DIM ** -0.5

    def body(b_ref, k_ref, v_ref, t_ref, sink_ref, o_ref, dy_ref, _, dp_ref, dk_ref, dv_ref, ds_ref):
        @pl.when(pl.program_id(0) == 0)
        def _():
            dk_ref[...] = jnp.zeros_like(dk_ref)
            dv_ref[...] = jnp.zeros_like(dv_ref)
            ds_ref[...] = jnp.zeros_like(ds_ref)

        lo = lax.broadcasted_iota(jnp.int32, (wb, LANES), 1) < HALF_LANES
        lo_w = lax.broadcasted_iota(jnp.int32, (3 * wb, LANES), 1) < HALF_LANES
        for sub in range(SWA_BWD_BLOCKS):
            n = pl.program_id(0) * SWA_BWD_BLOCKS + sub
            rows = slice(sub * wb, (sub + 1) * wb)
            k2, v2 = _window_operands(k_ref, v_ref, n, lo_w)
            valid = _window_mask(n, s)
            tab_v = t_ref[rows, :]
            ones = jnp.ones((3 * wb, LANES), BF16)
            dks, dvs = [], []
            for g in range(2):
                qr, op, do = [], [], []
                for i in range(2):
                    cols = slice((2 * g + i) * LANES, (2 * g + i + 1) * LANES)
                    zcols = slice(A_WIDTH + cols.start, A_WIDTH + cols.stop)
                    qr.append(_rope(b_ref[rows, cols], tab_v))
                    zp = b_ref[rows, zcols]
                    sg = _sigmoid(zp)
                    op.append(o_ref[rows, cols])
                    dyp = dy_ref[rows, cols]
                    do.append(dyp * (zp * sg))
                    dp_ref[rows, zcols] = (dyp * op[i] * (sg * (1.0 + zp * (1.0 - sg)))).astype(BF16)
                q4 = _stack_heads(qr[0], qr[1], lo).astype(BF16)
                do4 = _stack_heads(do[0], do[1], lo)
                o4 = jnp.concatenate([op[0], op[0], op[1], op[1]], axis=0)
                e, es = _head_exp(q4, k2[g], valid, _sink_column(sink_ref, g))
                inv = 1.0 / (jnp.dot(e, ones, preferred_element_type=F32) + es)
                prob = e.astype(F32) * jnp.concatenate([inv, inv, inv], axis=1)
                delta = jnp.sum(do4 * o4, axis=1, keepdims=True)
                do4b = do4.astype(BF16)
                dprob = lax.dot_general(do4b, v2[g], _DIMS["nt"], preferred_element_type=F32)
                dsc = (prob * (dprob - delta)).astype(BF16)
                sink_terms = (es * inv[:, :1]) * delta
                for i in range(HEADS_PER_GROUP):
                    h = HEADS_PER_GROUP * g + i
                    dsink = -jnp.sum(sink_terms[i * wb:(i + 1) * wb], axis=0, keepdims=True)
                    ds_ref[h:h + 1, :] += jnp.broadcast_to(dsink, (1, LANES))
                dq4 = jnp.dot(dsc, k2[g], preferred_element_type=F32) * scale
                for i in range(2):
                    cols = slice((2 * g + i) * LANES, (2 * g + i + 1) * LANES)
                    dp_ref[rows, cols] = _rope_transpose(_unstack_pair(dq4, i, lo), tab_v).astype(BF16)
                dk2 = lax.dot_general(dsc, q4, _DIMS["tn"], preferred_element_type=F32) * scale
                dv2 = lax.dot_general(prob.astype(BF16), do4b, _DIMS["tn"], preferred_element_type=F32)
                dks.append(dk2 + pltpu.roll(dk2, HALF_LANES, 1))
                dvs.append(dv2 + pltpu.roll(dv2, HALF_LANES, 1))
            start = _window_start(n)
            dk_ref[pl.ds(start, 3 * wb), :] += jnp.where(lo_w, dks[0], dks[1])
            dv_ref[pl.ds(start, 3 * wb), :] += jnp.where(lo_w, dvs[0], dvs[1])

    tq = SWA_BWD_BLOCKS * wb
    pad_spec = pl.BlockSpec((s + 2 * KV_PAD, LANES), lambda n: (0, 0))
    blk = pl.BlockSpec((tq, A_WIDTH), lambda n: (n, 0))
    bsp = pl.BlockSpec((tq, W_B), lambda n: (n, COL_B))
    pad_shape = jax.ShapeDtypeStruct((s + 2 * KV_PAD, LANES), F32)
    return pl.pallas_call(
        body, grid=(s // tq,),
        in_specs=[bsp, pad_spec, pad_spec, pl.BlockSpec((tq, 3 * LANES), lambda n: (n, 0)),
                  pl.BlockSpec(memory_space=pltpu.SMEM), blk, blk, pl.BlockSpec(memory_space=pl.ANY)],
        out_specs=[bsp, pad_spec, pad_spec, pl.BlockSpec((8, LANES), lambda n: (0, 0))],
        out_shape=[jax.ShapeDtypeStruct(dproj.shape, BF16), pad_shape, pad_shape,
                   jax.ShapeDtypeStruct((8, LANES), F32)],
        input_output_aliases={7: 0}, name="swa_bwd",
        compiler_params=_params())(proj, kpad, vpad, tab, sink, o_attn, dyb, dproj)


def _mem_exp(qh, mk):
    sc = lax.dot_general(qh, mk, _DIMS["nt"], preferred_element_type=F32) * (MEM_HEAD_DIM ** -0.5)
    return jnp.exp(sc - jnp.max(sc, axis=1, keepdims=True)).astype(BF16)


def _mem_fwd(proj, mkv):
    s = proj.shape[0]
    ts = 512
    mlen = mkv.shape[0]

    def body(m_ref, kv_ref, o_ref, y_ref):
        ones = jnp.ones((mlen, LANES), BF16)
        for h in range(MEM_HEADS):
            cols = slice(h * LANES, (h + 1) * LANES)
            mk = kv_ref[:, cols].astype(BF16)
            mv = kv_ref[:, MEM_WIDTH + h * LANES:MEM_WIDTH + (h + 1) * LANES].astype(BF16)
            e = _mem_exp(m_ref[:, cols].astype(BF16), mk)
            ox = jnp.dot(e, jnp.concatenate([mv, ones], axis=1), preferred_element_type=F32)
            oh = ox[:, :LANES] * (1.0 / ox[:, LANES:])
            o_ref[:, cols] = oh
            zh = m_ref[:, MEM_WIDTH + h * LANES:MEM_WIDTH + (h + 1) * LANES]
            y_ref[:, cols] = (oh * (zh * _sigmoid(zh))).astype(BF16)

    o_spec = pl.BlockSpec((ts, MEM_WIDTH), lambda i: (i, 0))
    return pl.pallas_call(
        body, grid=(s // ts,),
        in_specs=[pl.BlockSpec((ts, W_M), lambda i: (i, COL_M)),
                  pl.BlockSpec((mlen, 2 * MEM_WIDTH), lambda i: (0, 0))],
        out_specs=[o_spec, o_spec],
        out_shape=[jax.ShapeDtypeStruct((s, MEM_WIDTH), F32), jax.ShapeDtypeStruct((s, MEM_WIDTH), BF16)],
        name="mem_fwd", compiler_params=_params())(proj, mkv)


def _mem_bwd(proj, mkv, o_mem, dym, dproj):
    s = proj.shape[0]
    ts = 512
    mlen = mkv.shape[0]
    scale = MEM_HEAD_DIM ** -0.5

    def body(m_ref, kv_ref, o_ref, dy_ref, _, dp_ref, dkv_ref):
        @pl.when(pl.program_id(0) == 0)
        def _():
            dkv_ref[...] = jnp.zeros_like(dkv_ref)

        ones = jnp.ones((mlen, LANES), BF16)
        for h in range(MEM_HEADS):
            cols = slice(h * LANES, (h + 1) * LANES)
            vcols = slice(MEM_WIDTH + h * LANES, MEM_WIDTH + (h + 1) * LANES)
            mk = kv_ref[:, cols].astype(BF16)
            mv = kv_ref[:, vcols].astype(BF16)
            qh = m_ref[:, cols].astype(BF16)
            zh = m_ref[:, vcols]
            sg = _sigmoid(zh)
            oh = o_ref[:, cols]
            dyh = dy_ref[:, cols]
            doh = dyh * (zh * sg)
            dp_ref[:, vcols] = (dyh * oh * (sg * (1.0 + zh * (1.0 - sg)))).astype(BF16)
            e = _mem_exp(qh, mk)
            inv = 1.0 / jnp.dot(e, ones, preferred_element_type=F32)
            prob = e.astype(F32) * jnp.concatenate([inv] * (mlen // LANES), axis=1)
            delta = jnp.sum(doh * oh, axis=1, keepdims=True)
            dohb = doh.astype(BF16)
            dprob = lax.dot_general(dohb, mv, _DIMS["nt"], preferred_element_type=F32)
            dsc = (prob * (dprob - delta)).astype(BF16)
            dp_ref[:, cols] = (jnp.dot(dsc, mk, preferred_element_type=F32) * scale).astype(BF16)
            dkv_ref[:, cols] += lax.dot_general(dsc, qh, _DIMS["tn"], preferred_element_type=F32) * scale
            dkv_ref[:, vcols] += lax.dot_general(prob.astype(BF16), dohb, _DIMS["tn"],
                                                 preferred_element_type=F32)

    blk = pl.BlockSpec((ts, MEM_WIDTH), lambda i: (i, 0))
    msp = pl.BlockSpec((ts, W_M), lambda i: (i, COL_M))
    kvsp = pl.BlockSpec((mlen, 2 * MEM_WIDTH), lambda i: (0, 0))
    return pl.pallas_call(
        body, grid=(s // ts,),
        in_specs=[msp, kvsp, blk, blk, pl.BlockSpec(memory_space=pl.ANY)],
        out_specs=[msp, kvsp],
        out_shape=[jax.ShapeDtypeStruct(dproj.shape, BF16), jax.ShapeDtypeStruct(mkv.shape, F32)],
        input_output_aliases={4: 0}, name="mem_bwd",
        compiler_params=_params())(proj, mkv, o_mem, dym, dproj)


def _forward_backward(x, mem, tgt, proj, w_conv, sink, g_mem, w_kv, w_up, w_out, g_post):
    s = x.shape[0]
    tab = _rope_tables(s)

    ya = _conv_fwd(proj, w_conv)
    kpad, vpad = _rope_kv(proj, tab)
    o_attn, yb = _swa_fwd(proj, kpad, vpad, tab, sink)
    mn = _rmsnorm_fwd(mem, g_mem, name="mem_norm")
    mkv = _matmul(mn, w_kv, mode="nn", out_dtype=F32, tm=256, tn=1024, tk=D_MODEL, name="mem_kv")
    o_mem, ym = _mem_fwd(proj, mkv)
    merged, d_out, dy, dg_post, loss = _mid_fwd(ya, yb, ym, proj, x, tgt, w_up, w_out, g_post)
    dproj, d_ya, d_yb, d_ym, dw_up, dw_out = _mid_bwd(d_out, merged, ya, yb, ym, proj, w_up, w_out)

    dproj, dw_conv = _conv_bwd(proj, w_conv, d_ya, dproj)
    dproj, dkpad, dvpad, dsink = _swa_bwd(proj, kpad, vpad, tab, sink, o_attn, d_yb, dproj)
    dproj = _rope_kv_bwd(dkpad, dvpad, tab, dproj)
    dproj, d_mkv = _mem_bwd(proj, mkv, o_mem, d_ym, dproj)

    dw_kv = _matmul(mn, d_mkv, mode="tn", out_dtype=F32, tm=1024, tn=1024, tk=256, name="dw_kv")
    d_mn = _matmul(d_mkv, w_kv, mode="nt", out_dtype=F32, tm=256, tn=1024, tk=D_MODEL, name="d_mn")
    _, dg_mem = _rmsnorm_bwd(d_mn, mem, g_mem, d_mn, name="mem_norm_bwd")

    return dict(loss=loss, dproj=dproj, dy=dy, w_conv=dw_conv, sink=dsink, g_mem=dg_mem,
                w_kv=dw_kv, w_up=dw_up, w_out=dw_out, g_post=dg_post)


N_DEV = 8


def _position():
    return lax.axis_index("x"), lax.axis_index("y"), lax.axis_index("c")


def _other_chips(x, y):
    return (((1 - x, y), 2 * (1 - x) + y), ((x, 1 - y), 2 * x + (1 - y)), ((1 - x, 1 - y), 2 * (1 - x) + (1 - y)))


def _remote(src, dst, send_sems, recv_sems, k, device):
    return pltpu.make_async_remote_copy(src_ref=src, dst_ref=dst, send_sem=send_sems.at[k], recv_sem=recv_sems.at[k],
                                        device_id=device, device_id_type=MESH)


def _rows_half(ref, hf):
    rh = ref.shape[0] // 2
    return ref.at[pl.ds(pl.multiple_of(hf * rh, 8), rh)]


def _gather_weights(shards, small=None, relations=(0, 1, 2), into=None):
    n = len(shards)
    k = 0 if small is None else 1

    def peers(x, y):
        return [(r, chip, idx) for r, (chip, idx) in enumerate(_other_chips(x, y)) if r in relations]

    def ici(ins, outs, sems, a, r, chip, src_chip, c):
        return _remote(_rows_half(ins[a], c), _rows_half(outs[a].at[src_chip], c), sems[0], sems[1], 3 * a + r,
                       (*chip, c))

    def whole(ins, outs, sems, r, chip, src_chip, c):
        return _remote(ins[n], outs[n].at[src_chip], sems[0], sems[1], 3 * n + r, (*chip, c))

    def d2d(outs, sems, a, r, idx, hf, x, y, c):
        half = _rows_half(outs[a].at[idx], hf)
        return _remote(half, half, sems[2], sems[3], 3 * a + r, (x, y, 1 - c))

    def start(ins, outs, sems):
        x, y, c = _position()
        me = 2 * x + y
        for a in range(n):
            for r, chip, _ in peers(x, y):
                ici(ins, outs, sems, a, r, chip, me, c).start()
        for r, (chip, _) in enumerate(_other_chips(x, y)):
            if k:
                whole(ins, outs, sems, r, chip, me, c).start()

    def finish(ins, outs, sems):
        x, y, c = _position()
        me = 2 * x + y
        for a in range(n):
            for r, chip, idx in peers(x, y):
                ici(ins, outs, sems, a, r, chip, idx, c).wait_recv()
                d2d(outs, sems, a, r, idx, c, x, y, c).start()
        for a in range(n):
            for r, chip, idx in peers(x, y):
                d2d(outs, sems, a, r, idx, 1 - c, x, y, c).wait_recv()
        for r, (chip, idx) in enumerate(_other_chips(x, y)):
            if k:
                whole(ins, outs, sems, r, chip, idx, c).wait_recv()
                whole(ins, outs, sems, r, chip, me, c).wait_send()
        for a in range(n):
            for r, chip, idx in peers(x, y):
                ici(ins, outs, sems, a, r, chip, me, c).wait_send()
                d2d(outs, sems, a, r, idx, c, x, y, c).wait_send()

    operands = list(shards) + ([small] if k else [])
    shapes = [jax.ShapeDtypeStruct((N_CHIPS,) + s.shape, s.dtype) for s in operands]
    aliases = {}
    if into is not None:
        assert len(into) == len(operands)
        aliases = {len(operands) + a: a for a in range(len(into))}
        operands += list(into)
    return _Carry(operands, shapes,
                  [pltpu.SemaphoreType.DMA((3 * (n + k),)), pltpu.SemaphoreType.DMA((3 * (n + k),)),
                   pltpu.SemaphoreType.DMA((3 * n,)), pltpu.SemaphoreType.DMA((3 * n,))], start, finish, aliases)


def _run_carry(carry, name):
    _, results = _carried_call(lambda ins, outs, scr: None, carry, grid=(1,), in_specs=[], out_specs=[],
                               out_shape=[], scratch=[], operands=(), name=name)
    return results


def _pair_exchange(send):
    n = len(send)

    def copies(ins, outs, sems):
        x, y, c = _position()
        return [_remote(ins[a], outs[a], sems[0], sems[1], a, (x, y, 1 - c)) for a in range(n)]

    def start(ins, outs, sems):
        for cp in copies(ins, outs, sems):
            cp.start()

    def finish(ins, outs, sems):
        for cp in copies(ins, outs, sems):
            cp.wait()

    return _Carry(send, [jax.ShapeDtypeStruct(p.shape, p.dtype) for p in send],
                  [pltpu.SemaphoreType.DMA((n,)), pltpu.SemaphoreType.DMA((n,))], start, finish)


def _chip_exchange(sums):
    n = len(sums)

    def copies(ins, outs, sems):
        x, y, c = _position()
        return [_remote(ins[a].at[idx], outs[a].at[r], sems[0], sems[1], 3 * a + r, (*chip, c))
                for a in range(n) for r, (chip, idx) in enumerate(_other_chips(x, y))]

    def start(ins, outs, sems):
        for cp in copies(ins, outs, sems):
            cp.start()

    def finish(ins, outs, sems):
        for cp in copies(ins, outs, sems):
            cp.wait()

    return _Carry(sums, [jax.ShapeDtypeStruct((3,) + p.shape[1:], p.dtype) for p in sums],
                  [pltpu.SemaphoreType.DMA((3 * n,)), pltpu.SemaphoreType.DMA((3 * n,))], start, finish)


def _pair_share(pairs):
    n = len(pairs)

    def start(ins, outs, sems):
        x, y, c = _position()
        for a in range(n):
            _remote(outs[a].at[c], outs[a].at[c], sems[0], sems[1], a, (x, y, 1 - c)).start()

    def finish(ins, outs, sems):
        x, y, c = _position()
        for a in range(n):
            _remote(outs[a].at[1 - c], outs[a].at[1 - c], sems[0], sems[1], a, (x, y, 1 - c)).wait_recv()
        for a in range(n):
            _remote(outs[a].at[c], outs[a].at[c], sems[0], sems[1], a, (x, y, 1 - c)).wait_send()

    return _Carry(pairs, [jax.ShapeDtypeStruct(p.shape, p.dtype) for p in pairs],
                  [pltpu.SemaphoreType.DMA((n,)), pltpu.SemaphoreType.DMA((n,))], start, finish,
                  aliases={a: a for a in range(n)})


def _small_allreduce(pack, share):
    rows, width = pack.shape
    n_share = len(share.ins)

    def body(p_ref, *refs):
        share_in, o_ref, share_out = refs[:n_share], refs[n_share], refs[n_share + 1:2 * n_share + 1]
        buf, send_sems, recv_sems = refs[2 * n_share + 1:2 * n_share + 4]
        share_sems = refs[2 * n_share + 4:]
        share.start(share_in, share_out, share_sems)
        x, y, c = _position()
        me = 4 * x + 2 * y + c
        buf[me] = p_ref[...]
        peers = []
        for r in range(1, N_DEV):
            fx, fy, fc = (r >> 2) & 1, (r >> 1) & 1, r & 1
            px, py, pc = (1 - x if fx else x), (1 - y if fy else y), (1 - c if fc else c)
            peers.append(((px, py, pc), 4 * px + 2 * py + pc))
        sends = [_remote(p_ref, buf.at[me], send_sems, recv_sems, r, dev) for r, (dev, _) in enumerate(peers)]
        for cp in sends:
            cp.start()
        for r, (dev, idx) in enumerate(peers):
            _remote(p_ref, buf.at[idx], send_sems, recv_sems, r, dev).wait_recv()
        for cp in sends:
            cp.wait_send()
        acc = buf[0]
        for k in range(1, N_DEV):
            acc = acc + buf[k]
        o_ref[...] = acc
        share.finish(share_in, share_out, share_sems)

    vm = pl.BlockSpec(memory_space=pltpu.VMEM)
    red, *shared = pl.pallas_call(
        body, in_specs=[vm] + [_HBM] * n_share, out_specs=[vm] + [_HBM] * n_share,
        out_shape=[jax.ShapeDtypeStruct(pack.shape, F32)] + share.out_shapes,
        scratch_shapes=[pltpu.VMEM((N_DEV, rows, width), F32), pltpu.SemaphoreType.DMA((N_DEV - 1,)),
                        pltpu.SemaphoreType.DMA((N_DEV - 1,))] + share.sems,
        input_output_aliases={1 + i: 1 + o for i, o in share.aliases.items()},
        name="small_allreduce")(pack, *share.ins)
    return red, shared


ROW_TILE_MAX = 512
BF16_SUBLANES = 16


def _row_tile(rows):
    if rows <= ROW_TILE_MAX:
        return rows
    return max(t for t in range(BF16_SUBLANES, ROW_TILE_MAX + 1, BF16_SUBLANES) if rows % t == 0)


def _pair_add(keep, recv, name):
    nj, rh, cols = keep.shape
    tr = _row_tile(rh)

    def body(k_ref, r_ref, o_ref):
        o_ref[...] = (k_ref[...].astype(F32) + r_ref[...].astype(F32)).astype(BF16)

    blk = pl.BlockSpec((None, tr, cols), lambda j, i: (j, i, 0))
    return pl.pallas_call(body, grid=(nj, rh // tr), in_specs=[blk, blk], out_specs=blk,
                          out_shape=jax.ShapeDtypeStruct(keep.shape, BF16), name=name,
                          compiler_params=_params())(keep, recv)


def _chip_add(sums, recv, where, name):
    _, rh, cols = sums.shape
    tr = _row_tile(rh)

    def body(w_ref, s_ref, r_ref, o_ref):
        o_ref[...] = ((s_ref[...].astype(F32) + r_ref[0].astype(F32)) + r_ref[1].astype(F32)) + r_ref[2].astype(F32)

    grid_spec = pltpu.PrefetchScalarGridSpec(
        num_scalar_prefetch=1, grid=(rh // tr,),
        in_specs=[pl.BlockSpec((None, tr, cols), lambda i, w_ref: (w_ref[0], i, 0)),
                  pl.BlockSpec((3, tr, cols), lambda i, w_ref: (0, i, 0))],
        out_specs=pl.BlockSpec((None, tr, cols), lambda i, w_ref: (w_ref[1], i, 0)))
    return pl.pallas_call(body, grid_spec=grid_spec, out_shape=jax.ShapeDtypeStruct((2, rh, cols), F32),
                          name=name, compiler_params=_params())(where, sums, recv)


def _adamw(w, g, m, v, name):
    rows, cols = w.shape
    tr = _row_tile(rows)
    assert rows % tr == 0

    def body(w_ref, g_ref, m_ref, v_ref, d_ref, mo_ref, vo_ref):
        gv = g_ref[...]
        m_new = ADAM_B1 * m_ref[...] + (1.0 - ADAM_B1) * gv
        v_new = ADAM_B2 * v_ref[...] + (1.0 - ADAM_B2) * jnp.square(gv)
        m_hat = m_new / (1.0 - ADAM_B1 ** ADAM_STEP)
        v_hat = v_new / (1.0 - ADAM_B2 ** ADAM_STEP)
        d_ref[...] = -ADAM_LR * (m_hat / (jnp.sqrt(v_hat) + ADAM_EPS) + ADAM_WD * w_ref[...])
        mo_ref[...] = m_new
        vo_ref[...] = v_new

    blk = pl.BlockSpec((tr, cols), lambda i: (i, 0))
    shp = jax.ShapeDtypeStruct((rows, cols), F32)
    return pl.pallas_call(body, grid=(rows // tr,), in_specs=[blk] * 4, out_specs=[blk] * 3,
                          out_shape=[shp] * 3, name=name, compiler_params=_params())(w, g, m, v)


def _adamw_halves(w, g2, m, v, name):
    rows, cols = w.shape
    half = cols // 2
    tr = _row_tile(rows)

    def body(w_ref, g_ref, m_ref, v_ref, go_ref, d_ref, mo_ref, vo_ref):
        gv = g_ref[...]
        go_ref[...] = gv
        m_new = ADAM_B1 * m_ref[...] + (1.0 - ADAM_B1) * gv
        v_new = ADAM_B2 * v_ref[...] + (1.0 - ADAM_B2) * jnp.square(gv)
        m_hat = m_new / (1.0 - ADAM_B1 ** ADAM_STEP)
        v_hat = v_new / (1.0 - ADAM_B2 ** ADAM_STEP)
        d_ref[...] = -ADAM_LR * (m_hat / (jnp.sqrt(v_hat) + ADAM_EPS) + ADAM_WD * w_ref[...])
        mo_ref[...] = m_new
        vo_ref[...] = v_new

    blk = pl.BlockSpec((tr, half), lambda hf, i: (i, hf))
    gsp = pl.BlockSpec((None, tr, half), lambda hf, i: (hf, i, 0))
    shp = jax.ShapeDtypeStruct((rows, cols), F32)
    return pl.pallas_call(body, grid=(2, rows // tr), in_specs=[blk, gsp, blk, blk], out_specs=[blk] * 4,
                          out_shape=[shp] * 4, name=name, compiler_params=_params())(w, g2, m, v)


SHARD_W = IN_WIDTH // N_CHIPS


def _half_major(a):
    r, c = a.shape
    return a.reshape(N_CHIPS, 2, r // N_CHIPS // 2, c).transpose(1, 0, 2, 3)


def kernel(x, mem, g_pre, w_in, w_conv, attn_sink, g_mem, w_mem_kv, w_up_a, w_up_b, w_up_m, w_out, g_post, loss_target, m_g_pre, m_w_in, m_w_conv, m_attn_sink, m_g_mem, m_w_mem_kv, m_w_up_a, m_w_up_b, m_w_up_m, m_w_out, m_g_post, v_g_pre, v_w_in, v_w_conv, v_attn_sink, v_g_mem, v_w_mem_kv, v_w_up_a, v_w_up_b, v_w_up_m, v_w_out, v_g_post):
    xi, yi, ci = _position()
    chip = 2 * xi + yi
    where = jnp.stack([chip, ci, N_CHIPS - 1 - chip]).astype(jnp.int32)

    own = [w_in[0].T.astype(BF16), w_mem_kv[0].astype(BF16),
           jnp.concatenate([w_up_a[0], w_up_b[0], w_up_m[0]], axis=0).astype(BF16), w_out[0].astype(BF16)]
    own_conv = jnp.pad(w_conv[0], ((0, 5), (0, 0)))

    def pieces(mine, got):
        got = lax.dynamic_update_slice_in_dim(got, mine[None], chip, axis=0)
        return [got[j] for j in range(N_CHIPS)]

    diag = N_CHIPS - 1 - chip
    diag_blocks = SHARD_BLOCKS + 1
    (h, h_t), (got_near, got_conv) = _rmsnorm_fwd(x[0], g_pre, name="pre_norm", transposed=True,
                                                  carry=_gather_weights(own[:1], own_conv, relations=(0, 1)))
    w_near = lax.dynamic_update_slice_in_dim(got_near, own[0][None], chip, axis=0).reshape(IN_WIDTH, D_MODEL)
    proj, (got_far, *got_rest) = _proj(
        h, w_near, n_blocks=N_IN_BLOCKS - diag_blocks, where=where, name="proj_near",
        block_of=lambda i, w: i + diag_blocks * (i >= SHARD_BLOCKS * w[2]).astype(jnp.int32),
        carry=_join(_gather_weights(own[:1], relations=(2,)), _gather_weights(own[1:], relations=(0, 1))))
    far = lax.dynamic_index_in_dim(got_far, diag, 0, keepdims=False)
    proj, gathered = _proj_far(h, w_near, far, where, into=proj,
                               carry=_gather_weights(own[1:], relations=(2,), into=got_rest))
    w_kv_full = jnp.concatenate(pieces(own[1], gathered[0]), axis=0)
    up_pieces = pieces(own[2], gathered[1])
    w_up_full = jnp.stack([jnp.concatenate([p[k * A_WIDTH:(k + 1) * A_WIDTH] for p in up_pieces], axis=1)
                           for k in range(3)])
    w_out_full = jnp.concatenate(pieces(own[3], gathered[2]), axis=0)
    w_conv_full = jnp.concatenate([p[:3] for p in pieces(own_conv, got_conv)], axis=1)

    g = _forward_backward(x[0], mem[0], loss_target[0], proj, w_conv_full, attn_sink, g_mem, w_kv_full, w_up_full,
                          w_out_full, g_post)

    half_rows = D_MODEL // 2
    up_parts = (g["w_up"].reshape(3, A_WIDTH, N_CHIPS, D_MODEL // N_CHIPS).transpose(2, 0, 1, 3)
                .reshape(N_CHIPS, 2, 3 * A_WIDTH // 2, D_MODEL // N_CHIPS).transpose(1, 0, 2, 3)).astype(BF16)
    small_parts = [_half_major(g["w_kv"]).astype(BF16), up_parts, _half_major(g["w_out"]).astype(BF16)]

    def dw_in_half(half_of, name, carry):
        dw, carried = _dw_in_t(g["dproj"], h_t, half_of=half_of, where=where, name=name, carry=carry)
        return dw.reshape(N_CHIPS, SHARD_W, half_rows), carried

    def pick(parts, hf):
        return [lax.dynamic_index_in_dim(p, hf, 0, keepdims=False) for p in parts]

    small_names = ["w_kv", "w_up", "w_out"]
    recv_small = _run_carry(_pair_exchange(pick(small_parts, 1 - ci)), "pair_exchange_small")
    sums_small = [_pair_add(k, r, "pair_add_" + nm)
                  for k, r, nm in zip(pick(small_parts, ci), recv_small, small_names)]
    dw_send, recv3_small = dw_in_half(lambda w: 1 - w[1], "dw_in_send", _chip_exchange(sums_small))
    dw_keep, (recv_in,) = dw_in_half(lambda w: w[1], "dw_in_keep", _pair_exchange([dw_send]))
    sum_in = _pair_add(dw_keep, recv_in, "pair_add_w_in")
    d_h, (recv3_in,) = _d_h(g["dproj"], w_near, far, where, carry=_chip_exchange([sum_in]))
    pairs = [_chip_add(s, r, where, "chip_add_" + nm)
             for s, r, nm in zip([sum_in] + sums_small, [recv3_in] + recv3_small, ["w_in"] + small_names)]
    grad_x, dg_pre = _rmsnorm_bwd(d_h, x[0], g_pre, g["dy"], name="pre_norm_bwd")

    zeros512 = jnp.zeros((1, D_MODEL - A_WIDTH), F32)
    conv_rows = [jnp.concatenate([g["w_conv"][k:k + 1], zeros512], axis=1) for k in range(3)]
    sink_row = jnp.pad(g["sink"][:, 0].reshape(1, N_Q_HEADS), ((0, 0), (0, D_MODEL - N_Q_HEADS)))
    loss_row = jnp.pad(g["loss"], ((0, 0), (0, D_MODEL - LANES)))
    pack = jnp.concatenate([dg_pre, g["g_mem"], g["g_post"]] + conv_rows + [sink_row, loss_row], axis=0)
    red, full = _small_allreduce(pack, _pair_share(pairs))
    loss = red[7, 0]
    small_grads = dict(
        g_pre=red[0:1], g_mem=red[1:2], g_post=red[2:3], attn_sink=red[6:7, :N_Q_HEADS],
        w_conv=lax.dynamic_slice(red[3:6, :A_WIDTH], (0, chip * LANES), (3, LANES)))

    gw_up = full[2].reshape(3, A_WIDTH, D_MODEL // N_CHIPS)
    grads = dict(small_grads, w_mem_kv=full[1].reshape(D_MODEL // N_CHIPS, 2 * MEM_WIDTH),
                 w_up_a=gw_up[0], w_up_b=gw_up[1], w_up_m=gw_up[2],
                 w_out=full[3].reshape(D_MODEL // N_CHIPS, D_MODEL))

    weights = dict(g_pre=g_pre, w_in=w_in, w_conv=w_conv, attn_sink=attn_sink, g_mem=g_mem, w_mem_kv=w_mem_kv,
                   w_up_a=w_up_a, w_up_b=w_up_b, w_up_m=w_up_m, w_out=w_out, g_post=g_post)
    m_in = dict(g_pre=m_g_pre, w_in=m_w_in, w_conv=m_w_conv, attn_sink=m_attn_sink, g_mem=m_g_mem,
                w_mem_kv=m_w_mem_kv, w_up_a=m_w_up_a, w_up_b=m_w_up_b, w_up_m=m_w_up_m, w_out=m_w_out,
                g_post=m_g_post)
    v_in = dict(g_pre=v_g_pre, w_in=v_w_in, w_conv=v_w_conv, attn_sink=v_attn_sink, g_mem=v_g_mem,
                w_mem_kv=v_w_mem_kv, w_up_a=v_w_up_a, w_up_b=v_w_up_b, w_up_m=v_w_up_m, w_out=v_w_out,
                g_post=v_g_post)
    out_g, out_d, out_m, out_v = [], [], [], []
    for nm in ("g_pre", "w_in", "w_conv", "attn_sink", "g_mem", "w_mem_kv", "w_up_a", "w_up_b", "w_up_m", "w_out",
               "g_post"):
        shape = weights[nm].shape
        if nm == "w_in":
            results = _adamw_halves(w_in[0].T, full[0], m_w_in[0].T, v_w_in[0].T, "adamw_w_in")
            for out, t in zip((out_g, out_d, out_m, out_v), results):
                out.append(t.T.reshape(shape))
            continue
        two_d = shape[-2:]
        gr = grads[nm].reshape(two_d)
        d, m_new, v_new = _adamw(weights[nm].reshape(two_d), gr, m_in[nm].reshape(two_d), v_in[nm].reshape(two_d),
                                 "adamw_" + nm)
        out_g.append(gr.reshape(shape))
        out_d.append(d.reshape(shape))
        out_m.append(m_new.reshape(shape))
        out_v.append(v_new.reshape(shape))
    return (loss, grad_x.reshape(x.shape), *out_g, *out_d, *out_m, *out_v)
```

```python
import functools

import jax
import jax.numpy as jnp
from jax import lax
from jax.experimental import pallas as pl
from jax.experimental.pallas import tpu as pltpu

F32 = jnp.float32
BF16 = jnp.bfloat16
MESH = pl.DeviceIdType.MESH

D_MODEL = 1024
EPS = 1e-6
A_WIDTH = 512
HEAD_DIM = 64
N_Q_HEADS = 8
WINDOW_BLOCK = 128
KV_PAD = 512
ROPE_THETA = 500000.0
ROT_DIM = 16
MEM_HEADS = 4
MEM_HEAD_DIM = 128
MEM_WIDTH = 512
IN_WIDTH = 7424
N_CHIPS = 4
LANES = 128
HALF_LANES = 64

PERM_SEGS = ((0, 2560), (2816, 3328), (4352, 7424), (3328, 4352), (2560, 2816))
UNPERM_SEGS = ((0, 2560), (7168, 7424), (2560, 3072), (6144, 7168), (3072, 6144))
COL_A, W_A = 0, 2048
COL_B, W_B = 2, 1024
COL_G, W_G = 1, 3072
COL_M, W_M = 6, 1024
COL_KV, W_KV = 28, 256

ADAM_LR = 0.001
ADAM_B1 = 0.9
ADAM_B2 = 0.999
ADAM_EPS = 1e-08
ADAM_WD = 0.01
ADAM_STEP = 10

VMEM_LIMIT_BYTES = 56 * 1024 * 1024


_HBM = pl.BlockSpec(memory_space=pltpu.HBM)


def _params(**kw):
    return pltpu.CompilerParams(vmem_limit_bytes=VMEM_LIMIT_BYTES, **kw)


def _sigmoid(v):
    return jax.nn.sigmoid(v)


_DIMS = {"nn": (((1,), (0,)), ((), ())), "nt": (((1,), (1,)), ((), ())), "tn": (((0,), (0,)), ((), ()))}


class _Carry:
    def __init__(self, ins, out_shapes, sems, start, finish, aliases=None):
        self.ins, self.out_shapes, self.sems = list(ins), list(out_shapes), list(sems)
        self.start, self.finish, self.aliases = start, finish, dict(aliases or {})


def _join(*carries):
    def split(seq, counts):
        pos, parts = 0, []
        for n in counts:
            parts.append(seq[pos:pos + n])
            pos += n
        return parts

    n_in = [len(c.ins) for c in carries]
    n_out = [len(c.out_shapes) for c in carries]
    n_sem = [len(c.sems) for c in carries]

    def run(which):
        def go(ins, outs, sems):
            for c, i, o, sm in zip(carries, split(ins, n_in), split(outs, n_out), split(sems, n_sem)):
                getattr(c, which)(i, o, sm)
        return go

    aliases = {}
    for k, c in enumerate(carries):
        aliases.update({sum(n_in[:k]) + i: sum(n_out[:k]) + o for i, o in c.aliases.items()})
    return _Carry([a for c in carries for a in c.ins], [sh for c in carries for sh in c.out_shapes],
                  [sm for c in carries for sm in c.sems], run("start"), run("finish"), aliases)


def _carried_call(body, carry, *, grid, in_specs, out_specs, out_shape, scratch, operands, name, prefetch=None,
                  aliases=None):
    n_in, n_out, n_scr = len(in_specs), len(out_specs), len(scratch)
    c_in = len(carry.ins) if carry else 0
    c_out = len(carry.out_shapes) if carry else 0
    n_pre = 0 if prefetch is None else 1
    steps = 1
    for g in grid:
        steps *= g

    def wrapped(*refs):
        refs = refs[n_pre:]
        ins, cins = refs[:n_in], refs[n_in:n_in + c_in]
        outs = refs[n_in + c_in:n_in + c_in + n_out]
        couts = refs[n_in + c_in + n_out:n_in + c_in + n_out + c_out]
        rest = refs[n_in + c_in + n_out + c_out:]
        scr, sems = rest[:n_scr], rest[n_scr:]
        if carry:
            step = pl.program_id(0)
            for ax in range(1, len(grid)):
                step = step * grid[ax] + pl.program_id(ax)

            @pl.when(step == 0)
            def _():
                carry.start(cins, couts, sems)

        body(ins, outs, scr)
        if carry:
            @pl.when(step == steps - 1)
            def _():
                carry.finish(cins, couts, sems)

    all_aliases = {n_pre + i: o for i, o in (aliases or {}).items()}
    if carry:
        all_aliases.update({n_pre + n_in + i: n_out + o for i, o in carry.aliases.items()})
    all_in = list(in_specs) + [_HBM] * c_in
    all_out = list(out_specs) + [_HBM] * c_out
    all_scratch = list(scratch) + (carry.sems if carry else [])
    if n_pre:
        spec = dict(grid_spec=pltpu.PrefetchScalarGridSpec(num_scalar_prefetch=1, grid=grid, in_specs=all_in,
                                                           out_specs=all_out, scratch_shapes=all_scratch))
        pre = (prefetch,)
    else:
        spec = dict(grid=grid, in_specs=all_in, out_specs=all_out, scratch_shapes=all_scratch)
        pre = ()
    results = pl.pallas_call(
        wrapped, out_shape=list(out_shape) + (carry.out_shapes if carry else []), input_output_aliases=all_aliases,
        name=name, compiler_params=_params(), **spec)(*pre, *operands, *(carry.ins if carry else []))
    return list(results[:n_out]), list(results[n_out:])


def _matmul(a, b, *, mode, out_dtype, tm, tn, tk, name, j_outer=False, carry=None):
    if mode == "nn":
        (m, k), (_, n) = a.shape, b.shape
    elif mode == "nt":
        (m, k), (n, _) = a.shape, b.shape
    else:
        (k, m), (_, n) = a.shape, b.shape
    tm, tn, tk = min(tm, m), min(tn, n), min(tk, k)
    assert m % tm == 0 and n % tn == 0 and k % tk == 0
    ni, nj, nk = m // tm, n // tn, k // tk
    dims = _DIMS[mode]

    def ij(g0, g1):
        return (g1, g0) if j_outer else (g0, g1)

    if mode == "nn":
        a_spec = pl.BlockSpec((tm, tk), lambda g0, g1, kk: (ij(g0, g1)[0], kk))
        b_spec = pl.BlockSpec((tk, tn), lambda g0, g1, kk: (kk, ij(g0, g1)[1]))
    elif mode == "nt":
        a_spec = pl.BlockSpec((tm, tk), lambda g0, g1, kk: (ij(g0, g1)[0], kk))
        b_spec = pl.BlockSpec((tn, tk), lambda g0, g1, kk: (ij(g0, g1)[1], kk))
    else:
        a_spec = pl.BlockSpec((tk, tm), lambda g0, g1, kk: (kk, ij(g0, g1)[0]))
        b_spec = pl.BlockSpec((tk, tn), lambda g0, g1, kk: (kk, ij(g0, g1)[1]))
    o_spec = pl.BlockSpec((tm, tn), lambda g0, g1, kk: ij(g0, g1))

    def part(a_ref, b_ref):
        return lax.dot_general(a_ref[...].astype(BF16), b_ref[...].astype(BF16), dims,
                               preferred_element_type=F32)

    if nk == 1:
        def body(ins, outs, scr):
            outs[0][...] = part(*ins).astype(out_dtype)
        scratch = []
    else:
        def body(ins, outs, scr):
            kk = pl.program_id(2)
            acc_ref = scr[0]

            @pl.when(kk == 0)
            def _():
                acc_ref[...] = part(*ins)

            @pl.when(kk > 0)
            def _():
                acc_ref[...] += part(*ins)

            @pl.when(kk == nk - 1)
            def _():
                outs[0][...] = acc_ref[...].astype(out_dtype)
        scratch = [pltpu.VMEM((tm, tn), F32)]

    grid = (nj, ni, nk) if j_outer else (ni, nj, nk)
    (out,), carried = _carried_call(
        body, carry, grid=grid, in_specs=[a_spec, b_spec], out_specs=[o_spec],
        out_shape=[jax.ShapeDtypeStruct((m, n), out_dtype)], scratch=scratch, operands=(a, b), name=name)
    return (out, carried) if carry else out


IN_BLOCK = 256
N_IN_BLOCKS = IN_WIDTH // IN_BLOCK
SHARD_BLOCKS = (IN_WIDTH // N_CHIPS) // IN_BLOCK
BLOCK_RUNS = tuple((a // IN_BLOCK, sum(d - c for c, d in PERM_SEGS[:k]) // IN_BLOCK, (b - a) // IN_BLOCK)
                   for k, (a, b) in enumerate(PERM_SEGS))


def _perm_block(r):
    p = r
    for ref0, perm0, n in BLOCK_RUNS:
        p = jnp.where((r >= ref0) & (r < ref0 + n), r - ref0 + perm0, p)
    return p


def _proj(h, w_t, *, n_blocks, block_of, where, name, carry=None):
    s, d = h.shape

    def body(ins, outs, scr):
        outs[0][...] = lax.dot_general(ins[0][...], ins[1][...], _DIMS["nt"], preferred_element_type=F32)

    (proj,), carried = _carried_call(
        body, carry, grid=(n_blocks,),
        in_specs=[pl.BlockSpec((s, d), lambda i, w: (0, 0)), pl.BlockSpec((IN_BLOCK, d), lambda i, w: (block_of(i, w), 0))],
        out_specs=[pl.BlockSpec((s, IN_BLOCK), lambda i, w: (0, _perm_block(block_of(i, w))))],
        out_shape=[jax.ShapeDtypeStruct((s, IN_WIDTH), F32)], scratch=[], operands=(h, w_t), name=name,
        prefetch=where)
    return (proj, carried) if carry else proj


def _proj_far(h, w_near, far, where, *, into, carry=None):
    s, d = h.shape
    n_blocks = SHARD_BLOCKS + 1
    lead = IN_WIDTH // N_CHIPS - SHARD_BLOCKS * IN_BLOCK

    def body(ins, outs, scr):
        where_ref, h_ref, w_hbm, far_hbm, _ = ins
        win, sem = scr
        i = pl.program_id(0)

        @pl.when(i == 0)
        def _():
            dg = where_ref[2]
            rows = pl.ds(pl.multiple_of(dg * (SHARD_BLOCKS * IN_BLOCK), IN_BLOCK), n_blocks * IN_BLOCK)
            window = pltpu.make_async_copy(w_hbm.at[rows], win, sem)
            window.start()
            window.wait()
            shard = pltpu.make_async_copy(far_hbm, win.at[pl.ds(pl.multiple_of(dg * lead, BF16_SUBLANES), SHARD_W)], sem)
            shard.start()
            shard.wait()

        blk = win[pl.ds(pl.multiple_of(i * IN_BLOCK, IN_BLOCK), IN_BLOCK), :]
        outs[0][...] = lax.dot_general(h_ref[...], blk, _DIMS["nt"], preferred_element_type=F32)

    anysp = pl.BlockSpec(memory_space=pl.ANY)
    (proj,), carried = _carried_call(
        body, carry, grid=(n_blocks,),
        in_specs=[pl.BlockSpec(memory_space=pltpu.SMEM), pl.BlockSpec((s, d), lambda i, w: (0, 0)), anysp, anysp, anysp],
        out_specs=[pl.BlockSpec((s, IN_BLOCK), lambda i, w: (0, _perm_block(i + SHARD_BLOCKS * w[2])))],
        out_shape=[jax.ShapeDtypeStruct((s, IN_WIDTH), F32)],
        scratch=[pltpu.VMEM((n_blocks * IN_BLOCK, d), BF16), pltpu.SemaphoreType.DMA],
        operands=(where, h, w_near, far, into), name="proj_far", prefetch=where, aliases={4: 0})
    return (proj, carried) if carry else proj


def _dw_in_t(dproj, h_t, *, half_of, where, name, carry=None):
    d, s = h_t.shape
    c = d // 2

    def body(ins, outs, scr):
        outs[0][...] = lax.dot_general(ins[1][...], ins[0][...], _DIMS["nn"], preferred_element_type=F32).T.astype(BF16)

    (dw,), carried = _carried_call(
        body, carry, grid=(N_IN_BLOCKS,),
        in_specs=[pl.BlockSpec((s, IN_BLOCK), lambda r, w: (0, _perm_block(r))),
                  pl.BlockSpec((c, s), lambda r, w: (half_of(w), 0))],
        out_specs=[pl.BlockSpec((IN_BLOCK, c), lambda r, w: (r, 0))],
        out_shape=[jax.ShapeDtypeStruct((IN_WIDTH, c), BF16)], scratch=[], operands=(dproj, h_t), name=name,
        prefetch=where)
    return (dw, carried) if carry else dw


def _d_h(dproj, w_near, far, where, *, carry=None):
    s = dproj.shape[0]
    d = w_near.shape[1]
    tm = min(s, 256)

    def body(ins, outs, scr):
        where_ref, a_ref, w_hbm, far_hbm = ins
        w_ref, sem = scr

        @pl.when(pl.program_id(0) == 0)
        def _():
            whole = pltpu.make_async_copy(w_hbm, w_ref, sem)
            whole.start()
            whole.wait()
            rows = pl.ds(pl.multiple_of(where_ref[2] * SHARD_W, BF16_SUBLANES), SHARD_W)
            part = pltpu.make_async_copy(far_hbm, w_ref.at[rows], sem)
            part.start()
            part.wait()

        acc = None
        for ref0, perm0, n in BLOCK_RUNS:
            term = jnp.dot(a_ref[:, perm0 * IN_BLOCK:(perm0 + n) * IN_BLOCK],
                           w_ref[ref0 * IN_BLOCK:(ref0 + n) * IN_BLOCK, :], preferred_element_type=F32)
            acc = term if acc is None else acc + term
        outs[0][...] = acc

    anysp = pl.BlockSpec(memory_space=pl.ANY)
    (dh,), carried = _carried_call(
        body, carry, grid=(s // tm,),
        in_specs=[pl.BlockSpec(memory_space=pltpu.SMEM), pl.BlockSpec((tm, IN_WIDTH), lambda i: (i, 0)), anysp, anysp],
        out_specs=[pl.BlockSpec((tm, d), lambda i: (i, 0))],
        out_shape=[jax.ShapeDtypeStruct((s, d), F32)],
        scratch=[pltpu.VMEM((IN_WIDTH, d), BF16), pltpu.SemaphoreType.DMA],
        operands=(where, dproj, w_near, far), name="d_h")
    return (dh, carried) if carry else dh


def _rmsnorm_fwd(x, g, *, name, transposed=False, carry=None):
    s, d = x.shape
    ts = min(512, s)

    def body(ins, outs, scr):
        xv = ins[0][...]
        r = lax.rsqrt(jnp.mean(xv * xv, axis=-1, keepdims=True) + EPS)
        hv = (xv * r) * ins[1][...]
        outs[0][...] = hv.astype(BF16)
        if transposed:
            outs[1][...] = hv.T.astype(BF16)

    out_specs = [pl.BlockSpec((ts, d), lambda i: (i, 0))]
    out_shape = [jax.ShapeDtypeStruct((s, d), BF16)]
    if transposed:
        out_specs.append(pl.BlockSpec((d, ts), lambda i: (0, i)))
        out_shape.append(jax.ShapeDtypeStruct((d, s), BF16))
    outs, carried = _carried_call(
        body, carry, grid=(s // ts,),
        in_specs=[pl.BlockSpec((ts, d), lambda i: (i, 0)), pl.BlockSpec((1, d), lambda i: (0, 0))],
        out_specs=out_specs, out_shape=out_shape, scratch=[], operands=(x, g), name=name)
    result = tuple(outs) if transposed else outs[0]
    return (result, carried) if carry else result


def _rmsnorm_bwd(dh, x, g, res, *, name, carry=None):
    s, d = x.shape
    ts = min(256, s)

    def body(ins, outs, scr):
        dh_ref, x_ref, g_ref, res_ref = ins
        dx_ref, dg_ref = outs
        xv = x_ref[...]
        r = lax.rsqrt(jnp.mean(xv * xv, axis=-1, keepdims=True) + EPS)
        xh = xv * r
        dhv = dh_ref[...]
        part = jnp.sum(dhv * xh, axis=0, keepdims=True)

        @pl.when(pl.program_id(0) == 0)
        def _():
            dg_ref[...] = part

        @pl.when(pl.program_id(0) > 0)
        def _():
            dg_ref[...] += part

        dxh = dhv * g_ref[...]
        dx_ref[...] = res_ref[...] + r * (dxh - xh * jnp.mean(dxh * xh, axis=-1, keepdims=True))

    row = pl.BlockSpec((ts, d), lambda i: (i, 0))
    vec = pl.BlockSpec((1, d), lambda i: (0, 0))
    outs, carried = _carried_call(
        body, carry, grid=(s // ts,), in_specs=[row, row, vec, row], out_specs=[row, vec],
        out_shape=[jax.ShapeDtypeStruct((s, d), F32), jax.ShapeDtypeStruct((1, d), F32)],
        scratch=[], operands=(dh, x, g, res), name=name)
    return (*outs, carried) if carry else tuple(outs)


MID_TILE = 512


def _gated_branches(y_refs, wup_ref, gl):
    d = D_MODEL
    us = [jnp.dot(y_refs[k][...], wup_ref[k], preferred_element_type=F32) for k in range(3)]
    sg = [_sigmoid(gl[:, k * d:(k + 1) * d]) for k in range(3)]
    return us, sg


def _mid_fwd(ya, yb, ym, proj, x, tgt, w_up, w_out, g_post):
    s, d = x.shape
    ts = MID_TILE

    def body(ya_ref, yb_ref, ym_ref, g_ref, x_ref, t_ref, wup_ref, wout_ref, gp_ref,
             m_ref, do_ref, dy_ref, dg_ref, loss_ref):
        us, sg = _gated_branches((ya_ref, yb_ref, ym_ref), wup_ref, g_ref[...])
        merged = (sg[0] * us[0] + sg[1] * us[1] + sg[2] * us[2]).astype(BF16)
        m_ref[...] = merged
        ov = jnp.dot(merged, wout_ref[...], preferred_element_type=F32)
        r = lax.rsqrt(jnp.mean(ov * ov, axis=-1, keepdims=True) + EPS)
        nh = ov * r
        gv = gp_ref[...]
        e = (x_ref[...] + nh * gv) - t_ref[...]
        lpart = 0.5 * jnp.sum(jnp.mean(e * e, axis=-1, keepdims=True), axis=0, keepdims=True)
        dy = e * (1.0 / d)
        dgp = jnp.sum(dy * nh, axis=0, keepdims=True)

        @pl.when(pl.program_id(0) == 0)
        def _():
            dg_ref[...] = dgp
            loss_ref[...] = jnp.broadcast_to(lpart, loss_ref.shape)

        @pl.when(pl.program_id(0) > 0)
        def _():
            dg_ref[...] += dgp
            loss_ref[...] += jnp.broadcast_to(lpart, loss_ref.shape)

        dn = dy * gv
        dy_ref[...] = dy
        do_ref[...] = (r * (dn - nh * jnp.mean(dn * nh, axis=-1, keepdims=True))).astype(BF16)

    row = pl.BlockSpec((ts, d), lambda i: (i, 0))
    ysp = pl.BlockSpec((ts, A_WIDTH), lambda i: (i, 0))
    vec = pl.BlockSpec((1, d), lambda i: (0, 0))
    return pl.pallas_call(
        body, grid=(s // ts,),
        in_specs=[ysp, ysp, ysp, pl.BlockSpec((ts, W_G), lambda i: (i, COL_G)), row, row,
                  pl.BlockSpec((3, A_WIDTH, d), lambda i: (0, 0, 0)), pl.BlockSpec((d, d), lambda i: (0, 0)), vec],
        out_specs=[row, row, row, vec, pl.BlockSpec((1, LANES), lambda i: (0, 0))],
        out_shape=[jax.ShapeDtypeStruct((s, d), BF16), jax.ShapeDtypeStruct((s, d), BF16),
                   jax.ShapeDtypeStruct((s, d), F32), jax.ShapeDtypeStruct((1, d), F32),
                   jax.ShapeDtypeStruct((1, LANES), F32)],
        name="mid_fwd", compiler_params=_params())(ya, yb, ym, proj, x, tgt, w_up, w_out, g_post)


def _mid_bwd(d_out, merged, ya, yb, ym, proj, w_up, w_out):
    s, d = merged.shape
    ts = MID_TILE
    last = s // ts - 1

    def body(do_ref, m_ref, ya_ref, yb_ref, ym_ref, g_ref, wup_ref, wout_ref,
             dp_ref, dya_ref, dyb_ref, dym_ref, dwup_hbm, dwout_hbm, dwup_acc, dwout_acc):
        i = pl.program_id(0)

        @pl.when(i == 0)
        def _():
            dwup_acc[...] = jnp.zeros_like(dwup_acc)
            dwout_acc[...] = jnp.zeros_like(dwout_acc)

        y_refs = (ya_ref, yb_ref, ym_ref)
        us, sg = _gated_branches(y_refs, wup_ref, g_ref[...])
        dov = do_ref[...]
        dwout_acc[...] += lax.dot_general(m_ref[...], dov, _DIMS["tn"], preferred_element_type=F32)
        dm = lax.dot_general(dov, wout_ref[...], _DIMS["nt"], preferred_element_type=F32)
        for k, dy_ref in enumerate((dya_ref, dyb_ref, dym_ref)):
            dp_ref[:, k * d:(k + 1) * d] = ((dm * us[k]) * (sg[k] * (1.0 - sg[k]))).astype(BF16)
            du = (sg[k] * dm).astype(BF16)
            dy_ref[...] = lax.dot_general(du, wup_ref[k], _DIMS["nt"], preferred_element_type=F32)
            dwup_acc[k] += lax.dot_general(y_refs[k][...], du, _DIMS["tn"], preferred_element_type=F32)

        @pl.when(i == last)
        def _():
            pltpu.sync_copy(dwup_acc, dwup_hbm)
            pltpu.sync_copy(dwout_acc, dwout_hbm)

    row = pl.BlockSpec((ts, d), lambda i: (i, 0))
    ysp = pl.BlockSpec((ts, A_WIDTH), lambda i: (i, 0))
    gsp = pl.BlockSpec((ts, W_G), lambda i: (i, COL_G))
    anysp = pl.BlockSpec(memory_space=pl.ANY)
    yshape = jax.ShapeDtypeStruct((s, A_WIDTH), F32)
    return pl.pallas_call(
        body, grid=(s // ts,),
        in_specs=[row, row, ysp, ysp, ysp, gsp, pl.BlockSpec((3, A_WIDTH, d), lambda i: (0, 0, 0)),
                  pl.BlockSpec((d, d), lambda i: (0, 0))],
        out_specs=[gsp, ysp, ysp, ysp, anysp, anysp],
        out_shape=[jax.ShapeDtypeStruct((s, IN_WIDTH), BF16), yshape, yshape, yshape,
                   jax.ShapeDtypeStruct((3, A_WIDTH, d), F32), jax.ShapeDtypeStruct((d, d), F32)],
        scratch_shapes=[pltpu.VMEM((3, A_WIDTH, d), F32), pltpu.VMEM((d, d), F32)],
        name="mid_bwd", compiler_params=_params())(d_out, merged, ya, yb, ym, proj, w_up, w_out)


def _conv_core(blk, prev, nxt, w, i, last, ts):
    c = A_WIDTH
    ab, ac, ax, az = blk[:, :c], blk[:, c:2 * c], blk[:, 2 * c:3 * c], blk[:, 3 * c:]
    cu = ac * ax
    cu_prev = (prev[7:8, c:2 * c] * prev[7:8, 2 * c:3 * c]) * jnp.where(i > 0, 1.0, 0.0)
    cu_next = (nxt[0:1, c:2 * c] * nxt[0:1, 2 * c:3 * c]) * jnp.where(i < last, 1.0, 0.0)
    row = lax.broadcasted_iota(jnp.int32, (ts, c), 0)
    cm1 = jnp.where(row == 0, cu_prev, pltpu.roll(cu, 1, 0))
    cp1 = jnp.where(row == ts - 1, cu_next, pltpu.roll(cu, ts - 1, 0))
    yc = cm1 * w[0:1] + cu * w[1:2] + cp1 * w[2:3]
    return ab, ac, ax, az, cu, cm1, cp1, yc, row


def _halo_specs(ts, width, col, nblk8):
    prev = pl.BlockSpec((8, width), lambda i: (jnp.maximum(i * (ts // 8) - 1, 0), col))
    nxt = pl.BlockSpec((8, width), lambda i: (jnp.minimum((i + 1) * (ts // 8), nblk8 - 1), col))
    return prev, nxt


def _conv_fwd(proj, w_conv):
    s = proj.shape[0]
    ts = 256
    last = s // ts - 1

    def body(a_ref, ap_ref, an_ref, w_ref, ya_ref):
        i = pl.program_id(0)
        ab, _, _, az, _, _, _, yc, _ = _conv_core(a_ref[...], ap_ref[...], an_ref[...], w_ref[...], i, last, ts)
        ya_ref[...] = ((ab * yc) * (az * _sigmoid(az))).astype(BF16)

    prev, nxt = _halo_specs(ts, W_A, COL_A, s // 8)
    return pl.pallas_call(
        body, grid=(s // ts,),
        in_specs=[pl.BlockSpec((ts, W_A), lambda i: (i, COL_A)), prev, nxt,
                  pl.BlockSpec((3, A_WIDTH), lambda i: (0, 0))],
        out_specs=pl.BlockSpec((ts, A_WIDTH), lambda i: (i, 0)),
        out_shape=jax.ShapeDtypeStruct((s, A_WIDTH), BF16), name="conv_fwd",
        compiler_params=_params())(proj, proj, proj, w_conv)


def _conv_bwd(proj, w_conv, dya, dproj):
    s = proj.shape[0]
    ts = 256
    last = s // ts - 1
    c = A_WIDTH

    def body(a_ref, ap_ref, an_ref, w_ref, d_ref, dp_ref, dn_ref, _, dproj_ref, dw_ref):
        i = pl.program_id(0)
        w = w_ref[...]
        prev, nxt = ap_ref[...], an_ref[...]
        ab, ac, ax, az, cu, cm1, cp1, yc, row = _conv_core(a_ref[...], prev, nxt, w, i, last, ts)
        sg = _sigmoid(az)
        sz = az * sg
        dya_v = d_ref[...]
        dyc = dya_v * sz * ab
        dproj_ref[:, :c] = (dya_v * sz * yc).astype(BF16)
        dproj_ref[:, 3 * c:] = (dya_v * (ab * yc) * (sg * (1.0 + az * (1.0 - sg)))).astype(BF16)

        def halo_dyc(a_row, d_row):
            azr = a_row[:, 3 * c:]
            return d_row * (azr * _sigmoid(azr)) * a_row[:, :c]

        dyc_prev = halo_dyc(prev[7:8], dp_ref[...][7:8]) * jnp.where(i > 0, 1.0, 0.0)
        dyc_next = halo_dyc(nxt[0:1], dn_ref[...][0:1]) * jnp.where(i < last, 1.0, 0.0)
        dyc_m1 = jnp.where(row == 0, dyc_prev, pltpu.roll(dyc, 1, 0))
        dyc_p1 = jnp.where(row == ts - 1, dyc_next, pltpu.roll(dyc, ts - 1, 0))
        dcu = dyc_p1 * w[0:1] + dyc * w[1:2] + dyc_m1 * w[2:3]
        dproj_ref[:, c:2 * c] = (dcu * ax).astype(BF16)
        dproj_ref[:, 2 * c:3 * c] = (dcu * ac).astype(BF16)
        dw = [jnp.sum(dyc * t, axis=0, keepdims=True) for t in (cm1, cu, cp1)]

        @pl.when(i == 0)
        def _():
            for k in range(3):
                dw_ref[k:k + 1, :] = dw[k]

        @pl.when(i > 0)
        def _():
            for k in range(3):
                dw_ref[k:k + 1, :] += dw[k]

    prev, nxt = _halo_specs(ts, W_A, COL_A, s // 8)
    dprev, dnxt = _halo_specs(ts, A_WIDTH, 0, s // 8)
    return pl.pallas_call(
        body, grid=(s // ts,),
        in_specs=[pl.BlockSpec((ts, W_A), lambda i: (i, COL_A)), prev, nxt,
                  pl.BlockSpec((3, A_WIDTH), lambda i: (0, 0)),
                  pl.BlockSpec((ts, A_WIDTH), lambda i: (i, 0)), dprev, dnxt,
                  pl.BlockSpec(memory_space=pl.ANY)],
        out_specs=[pl.BlockSpec((ts, W_A), lambda i: (i, COL_A)), pl.BlockSpec((3, A_WIDTH), lambda i: (0, 0))],
        out_shape=[jax.ShapeDtypeStruct(dproj.shape, BF16), jax.ShapeDtypeStruct((3, A_WIDTH), F32)],
        input_output_aliases={7: 0}, name="conv_bwd",
        compiler_params=_params())(proj, proj, proj, w_conv, dya, dya, dya, dproj)


def _rope_tables(s):
    half = ROT_DIM // 2
    dim = jnp.arange(LANES) % HEAD_DIM
    inv_freq = jnp.power(jnp.float32(ROPE_THETA), -(dim % half).astype(F32) * (2.0 / ROT_DIM))
    ang = jnp.arange(s).astype(F32)[:, None] * inv_freq[None, :]
    cos, sin = jnp.cos(ang), jnp.sin(ang)
    first, second = (dim < half)[None, :], ((dim >= half) & (dim < ROT_DIM))[None, :]
    c = jnp.where(first | second, cos, 1.0)
    s1 = jnp.where(first, -sin, 0.0)
    s2 = jnp.where(second, sin, 0.0)
    return jnp.concatenate([c, s1, s2], axis=1)


def _rope(t, tab):
    return (t * tab[:, :LANES] + pltpu.roll(t, LANES - 8, 1) * tab[:, LANES:2 * LANES]
            + pltpu.roll(t, 8, 1) * tab[:, 2 * LANES:])


def _rope_transpose(dt, tab):
    return (dt * tab[:, :LANES] + pltpu.roll(dt * tab[:, LANES:2 * LANES], 8, 1)
            + pltpu.roll(dt * tab[:, 2 * LANES:], LANES - 8, 1))


def _rope_kv(proj, tab):
    s = proj.shape[0]
    nb = s // KV_PAD

    def body(kv_ref, t_ref, k_ref, v_ref):
        j = pl.program_id(0)
        inside = jnp.where((j > 0) & (j <= nb), 1.0, 0.0)
        kv = kv_ref[...]
        k_ref[...] = (_rope(kv[:, :LANES], t_ref[...]) * inside).astype(BF16)
        v_ref[...] = (kv[:, LANES:] * inside).astype(BF16)

    def src(j):
        return jnp.clip(j - 1, 0, nb - 1)

    o_spec = pl.BlockSpec((KV_PAD, LANES), lambda j: (j, 0))
    shp = jax.ShapeDtypeStruct((s + 2 * KV_PAD, LANES), BF16)
    return pl.pallas_call(
        body, grid=(nb + 2,),
        in_specs=[pl.BlockSpec((KV_PAD, W_KV), lambda j: (src(j), COL_KV)),
                  pl.BlockSpec((KV_PAD, 3 * LANES), lambda j: (src(j), 0))],
        out_specs=[o_spec, o_spec], out_shape=[shp, shp], name="rope_kv",
        compiler_params=_params())(proj, tab)


def _rope_kv_bwd(dkpad, dvpad, tab, dproj):
    s = tab.shape[0]
    nb = s // KV_PAD

    def body(dk_ref, dv_ref, t_ref, _, dp_ref):
        dp_ref[:, :LANES] = _rope_transpose(dk_ref[...], t_ref[...]).astype(BF16)
        dp_ref[:, LANES:] = dv_ref[...].astype(BF16)

    pad_spec = pl.BlockSpec((KV_PAD, LANES), lambda j: (j + 1, 0))
    return pl.pallas_call(
        body, grid=(nb,),
        in_specs=[pad_spec, pad_spec, pl.BlockSpec((KV_PAD, 3 * LANES), lambda j: (j, 0)),
                  pl.BlockSpec(memory_space=pl.ANY)],
        out_specs=pl.BlockSpec((KV_PAD, W_KV), lambda j: (j, COL_KV)),
        out_shape=jax.ShapeDtypeStruct(dproj.shape, BF16), input_output_aliases={3: 0},
        name="rope_kv_bwd", compiler_params=_params())(dkpad, dvpad, tab, dproj)


def _window_start(n):
    return pl.multiple_of((n - 1) * WINDOW_BLOCK + KV_PAD, WINDOW_BLOCK)


def _window_operands(k_ref, v_ref, n, lo):
    start = _window_start(n)
    kw = k_ref[pl.ds(start, 3 * WINDOW_BLOCK), :].astype(F32)
    vw = v_ref[pl.ds(start, 3 * WINDOW_BLOCK), :].astype(F32)
    kr, vr = pltpu.roll(kw, HALF_LANES, 1), pltpu.roll(vw, HALF_LANES, 1)
    k2 = (jnp.where(lo, kw, kr).astype(BF16), jnp.where(lo, kr, kw).astype(BF16))
    v2 = (jnp.where(lo, vw, vr).astype(BF16), jnp.where(lo, vr, vw).astype(BF16))
    return k2, v2


HEADS_PER_GROUP = 4
SWA_FWD_BLOCKS = 1
SWA_BWD_BLOCKS = 2


def _window_bias():
    wb = WINDOW_BLOCK
    qi = lax.broadcasted_iota(jnp.int32, (wb, 3 * wb), 0)
    kj = lax.broadcasted_iota(jnp.int32, (wb, 3 * wb), 1)
    band = (kj >= qi) & (kj <= qi + 2 * wb)
    cases = jnp.stack([band & (kj >= wb), band, band & (kj < 2 * wb)])
    return jnp.where(cases, 0.0, -jnp.inf).astype(F32)


def _block_bias(bias_ref, n, n_blocks):
    case = jnp.where(n == 0, 0, jnp.where(n == n_blocks - 1, 2, 1))
    one = bias_ref[case]
    return jnp.concatenate([one] * HEADS_PER_GROUP, axis=0)


def _stack_heads(pair0, pair1, lo):
    return jnp.concatenate([jnp.where(lo, pair0, 0.0), jnp.where(lo, 0.0, pair0),
                            jnp.where(lo, pair1, 0.0), jnp.where(lo, 0.0, pair1)], axis=0)


def _unstack_pair(stacked, i, lo):
    wb = WINDOW_BLOCK
    return jnp.where(lo, stacked[2 * i * wb:(2 * i + 1) * wb], stacked[(2 * i + 1) * wb:(2 * i + 2) * wb])


def _sink_column(sink_ref, g):
    wb = WINDOW_BLOCK
    return jnp.concatenate([jnp.full((wb, 1), sink_ref[0, HEADS_PER_GROUP * g + i], F32)
                            for i in range(HEADS_PER_GROUP)], axis=0)


def _head_exp(q4, k2g, bias, sink):
    sc = lax.dot_general(q4, k2g, _DIMS["nt"], preferred_element_type=F32) * (HEAD_DIM ** -0.5) + bias
    m = jnp.maximum(jnp.max(sc, axis=1, keepdims=True), sink)
    return jnp.exp(sc - m).astype(BF16), jnp.exp(sink - m)


def _swa_fwd(proj, kpad, vpad, tab, bias, sink):
    s = proj.shape[0]
    wb = WINDOW_BLOCK

    def body(b_ref, k_ref, v_ref, t_ref, bias_ref, sink_ref, o_ref, y_ref):
        lo = lax.broadcasted_iota(jnp.int32, (wb, LANES), 1) < HALF_LANES
        lo_w = lax.broadcasted_iota(jnp.int32, (3 * wb, LANES), 1) < HALF_LANES
        for sub in range(SWA_FWD_BLOCKS):
            n = pl.program_id(0) * SWA_FWD_BLOCKS + sub
            rows = slice(sub * wb, (sub + 1) * wb)
            k2, v2 = _window_operands(k_ref, v_ref, n, lo_w)
            valid = _block_bias(bias_ref, n, s // wb)
            tab_v = t_ref[rows, :]
            ones = jnp.ones((3 * wb, LANES), BF16)
            for g in range(2):
                qr = [_rope(b_ref[rows, (2 * g + i) * LANES:(2 * g + i + 1) * LANES], tab_v) for i in range(2)]
                q4 = _stack_heads(qr[0], qr[1], lo).astype(BF16)
                e, es = _head_exp(q4, k2[g], valid, _sink_column(sink_ref, g))
                ox = jnp.dot(e, jnp.concatenate([v2[g], ones], axis=1), preferred_element_type=F32)
                o4 = ox[:, :LANES] * (1.0 / (ox[:, LANES:] + es))
                for i in range(2):
                    cols = slice((2 * g + i) * LANES, (2 * g + i + 1) * LANES)
                    op = _unstack_pair(o4, i, lo)
                    o_ref[rows, cols] = op
                    zp = b_ref[rows, A_WIDTH + cols.start:A_WIDTH + cols.stop]
                    y_ref[rows, cols] = (op * (zp * _sigmoid(zp))).astype(BF16)

    tq = SWA_FWD_BLOCKS * wb
    pad_spec = pl.BlockSpec((s + 2 * KV_PAD, LANES), lambda n: (0, 0))
    o_spec = pl.BlockSpec((tq, A_WIDTH), lambda n: (n, 0))
    return pl.pallas_call(
        body, grid=(s // tq,),
        in_specs=[pl.BlockSpec((tq, W_B), lambda n: (n, COL_B)), pad_spec, pad_spec,
                  pl.BlockSpec((tq, 3 * LANES), lambda n: (n, 0)),
                  pl.BlockSpec(bias.shape, lambda n: (0, 0, 0)), pl.BlockSpec(memory_space=pltpu.SMEM)],
        out_specs=[o_spec, o_spec],
        out_shape=[jax.ShapeDtypeStruct((s, A_WIDTH), F32), jax.ShapeDtypeStruct((s, A_WIDTH), BF16)],
        name="swa_fwd", compiler_params=_params())(proj, kpad, vpad, tab, bias, sink)


def _swa_bwd(proj, kpad, vpad, tab, bias, sink, o_attn, dyb, dproj):
    s = proj.shape[0]
    wb = WINDOW_BLOCK
    scale = HEAD_DIM ** -0.5

    def body(b_ref, k_ref, v_ref, t_ref, bias_ref, sink_ref, o_ref, dy_ref, _, dp_ref, dk_ref, dv_ref, ds_ref):
        @pl.when(pl.program_id(0) == 0)
        def _():
            dk_ref[...] = jnp.zeros_like(dk_ref)
            dv_ref[...] = jnp.zeros_like(dv_ref)
            ds_ref[...] = jnp.zeros_like(ds_ref)

        lo = lax.broadcasted_iota(jnp.int32, (wb, LANES), 1) < HALF_LANES
        lo_w = lax.broadcasted_iota(jnp.int32, (3 * wb, LANES), 1) < HALF_LANES
        for sub in range(SWA_BWD_BLOCKS):
            n = pl.program_id(0) * SWA_BWD_BLOCKS + sub
            rows = slice(sub * wb, (sub + 1) * wb)
            k2, v2 = _window_operands(k_ref, v_ref, n, lo_w)
            valid = _block_bias(bias_ref, n, s // wb)
            tab_v = t_ref[rows, :]
            ones = jnp.ones((3 * wb, LANES), BF16)
            dks, dvs = [], []
            for g in range(2):
                qr, op, do = [], [], []
                for i in range(2):
                    cols = slice((2 * g + i) * LANES, (2 * g + i + 1) * LANES)
                    zcols = slice(A_WIDTH + cols.start, A_WIDTH + cols.stop)
                    qr.append(_rope(b_ref[rows, cols], tab_v))
                    zp = b_ref[rows, zcols]
                    sg = _sigmoid(zp)
                    op.append(o_ref[rows, cols])
                    dyp = dy_ref[rows, cols]
                    do.append(dyp * (zp * sg))
                    dp_ref[rows, zcols] = (dyp * op[i] * (sg * (1.0 + zp * (1.0 - sg)))).astype(BF16)
                q4 = _stack_heads(qr[0], qr[1], lo).astype(BF16)
                do4 = _stack_heads(do[0], do[1], lo)
                o4 = jnp.concatenate([op[0], op[0], op[1], op[1]], axis=0)
                e, es = _head_exp(q4, k2[g], valid, _sink_column(sink_ref, g))
                inv = 1.0 / (jnp.dot(e, ones, preferred_element_type=F32) + es)
                prob = e.astype(F32) * jnp.concatenate([inv, inv, inv], axis=1)
                delta = jnp.sum(do4 * o4, axis=1, keepdims=True)
                do4b = do4.astype(BF16)
                dprob = lax.dot_general(do4b, v2[g], _DIMS["nt"], preferred_element_type=F32)
                dsc = (prob * (dprob - delta)).astype(BF16)
                sink_terms = (es * inv[:, :1]) * delta
                for i in range(HEADS_PER_GROUP):
                    h = HEADS_PER_GROUP * g + i
                    dsink = -jnp.sum(sink_terms[i * wb:(i + 1) * wb], axis=0, keepdims=True)
                    ds_ref[h:h + 1, :] += jnp.broadcast_to(dsink, (1, LANES))
                dq4 = jnp.dot(dsc, k2[g], preferred_element_type=F32) * scale
                for i in range(2):
                    cols = slice((2 * g + i) * LANES, (2 * g + i + 1) * LANES)
                    dp_ref[rows, cols] = _rope_transpose(_unstack_pair(dq4, i, lo), tab_v).astype(BF16)
                dk2 = lax.dot_general(dsc, q4, _DIMS["tn"], preferred_element_type=F32) * scale
                dv2 = lax.dot_general(prob.astype(BF16), do4b, _DIMS["tn"], preferred_element_type=F32)
                dks.append(dk2 + pltpu.roll(dk2, HALF_LANES, 1))
                dvs.append(dv2 + pltpu.roll(dv2, HALF_LANES, 1))
            start = _window_start(n)
            dk_ref[pl.ds(start, 3 * wb), :] += jnp.where(lo_w, dks[0], dks[1])
            dv_ref[pl.ds(start, 3 * wb), :] += jnp.where(lo_w, dvs[0], dvs[1])

    tq = SWA_BWD_BLOCKS * wb
    pad_spec = pl.BlockSpec((s + 2 * KV_PAD, LANES), lambda n: (0, 0))
    blk = pl.BlockSpec((tq, A_WIDTH), lambda n: (n, 0))
    bsp = pl.BlockSpec((tq, W_B), lambda n: (n, COL_B))
    pad_shape = jax.ShapeDtypeStruct((s + 2 * KV_PAD, LANES), F32)
    return pl.pallas_call(
        body, grid=(s // tq,),
        in_specs=[bsp, pad_spec, pad_spec, pl.BlockSpec((tq, 3 * LANES), lambda n: (n, 0)),
                  pl.BlockSpec(bias.shape, lambda n: (0, 0, 0)), pl.BlockSpec(memory_space=pltpu.SMEM), blk, blk,
                  pl.BlockSpec(memory_space=pl.ANY)],
        out_specs=[bsp, pad_spec, pad_spec, pl.BlockSpec((8, LANES), lambda n: (0, 0))],
        out_shape=[jax.ShapeDtypeStruct(dproj.shape, BF16), pad_shape, pad_shape,
                   jax.ShapeDtypeStruct((8, LANES), F32)],
        input_output_aliases={8: 0}, name="swa_bwd",
        compiler_params=_params())(proj, kpad, vpad, tab, bias, sink, o_attn, dyb, dproj)


def _mem_exp(qh, mk):
    sc = lax.dot_general(qh, mk, _DIMS["nt"], preferred_element_type=F32) * (MEM_HEAD_DIM ** -0.5)
    return jnp.exp(sc - jnp.max(sc, axis=1, keepdims=True)).astype(BF16)


def _mem_fwd(proj, mkv):
    s = proj.shape[0]
    ts = 512
    mlen = mkv.shape[0]

    def body(m_ref, kv_ref, o_ref, y_ref):
        ones = jnp.ones((mlen, LANES), BF16)
        for h in range(MEM_HEADS):
            cols = slice(h * LANES, (h + 1) * LANES)
            mk = kv_ref[:, cols].astype(BF16)
            mv = kv_ref[:, MEM_WIDTH + h * LANES:MEM_WIDTH + (h + 1) * LANES].astype(BF16)
            e = _mem_exp(m_ref[:, cols].astype(BF16), mk)
            ox = jnp.dot(e, jnp.concatenate([mv, ones], axis=1), preferred_element_type=F32)
            oh = ox[:, :LANES] * (1.0 / ox[:, LANES:])
            o_ref[:, cols] = oh
            zh = m_ref[:, MEM_WIDTH + h * LANES:MEM_WIDTH + (h + 1) * LANES]
            y_ref[:, cols] = (oh * (zh * _sigmoid(zh))).astype(BF16)

    o_spec = pl.BlockSpec((ts, MEM_WIDTH), lambda i: (i, 0))
    return pl.pallas_call(
        body, grid=(s // ts,),
        in_specs=[pl.BlockSpec((ts, W_M), lambda i: (i, COL_M)),
                  pl.BlockSpec((mlen, 2 * MEM_WIDTH), lambda i: (0, 0))],
        out_specs=[o_spec, o_spec],
        out_shape=[jax.ShapeDtypeStruct((s, MEM_WIDTH), F32), jax.ShapeDtypeStruct((s, MEM_WIDTH), BF16)],
        name="mem_fwd", compiler_params=_params())(proj, mkv)


def _mem_bwd(proj, mkv, o_mem, dym, dproj):
    s = proj.shape[0]
    ts = 512
    mlen = mkv.shape[0]
    scale = MEM_HEAD_DIM ** -0.5

    def body(m_ref, kv_ref, o_ref, dy_ref, _, dp_ref, dkv_ref):
        @pl.when(pl.program_id(0) == 0)
        def _():
            dkv_ref[...] = jnp.zeros_like(dkv_ref)

        ones = jnp.ones((mlen, LANES), BF16)
        for h in range(MEM_HEADS):
            cols = slice(h * LANES, (h + 1) * LANES)
            vcols = slice(MEM_WIDTH + h * LANES, MEM_WIDTH + (h + 1) * LANES)
            mk = kv_ref[:, cols].astype(BF16)
            mv = kv_ref[:, vcols].astype(BF16)
            qh = m_ref[:, cols].astype(BF16)
            zh = m_ref[:, vcols]
            sg = _sigmoid(zh)
            oh = o_ref[:, cols]
            dyh = dy_ref[:, cols]
            doh = dyh * (zh * sg)
            dp_ref[:, vcols] = (dyh * oh * (sg * (1.0 + zh * (1.0 - sg)))).astype(BF16)
            e = _mem_exp(qh, mk)
            inv = 1.0 / jnp.dot(e, ones, preferred_element_type=F32)
            prob = e.astype(F32) * jnp.concatenate([inv] * (mlen // LANES), axis=1)
            delta = jnp.sum(doh * oh, axis=1, keepdims=True)
            dohb = doh.astype(BF16)
            dprob = lax.dot_general(dohb, mv, _DIMS["nt"], preferred_element_type=F32)
            dsc = (prob * (dprob - delta)).astype(BF16)
            dp_ref[:, cols] = (jnp.dot(dsc, mk, preferred_element_type=F32) * scale).astype(BF16)
            dkv_ref[:, cols] += lax.dot_general(dsc, qh, _DIMS["tn"], preferred_element_type=F32) * scale
            dkv_ref[:, vcols] += lax.dot_general(prob.astype(BF16), dohb, _DIMS["tn"],
                                                 preferred_element_type=F32)

    blk = pl.BlockSpec((ts, MEM_WIDTH), lambda i: (i, 0))
    msp = pl.BlockSpec((ts, W_M), lambda i: (i, COL_M))
    kvsp = pl.BlockSpec((mlen, 2 * MEM_WIDTH), lambda i: (0, 0))
    return pl.pallas_call(
        body, grid=(s // ts,),
        in_specs=[msp, kvsp, blk, blk, pl.BlockSpec(memory_space=pl.ANY)],
        out_specs=[msp, kvsp],
        out_shape=[jax.ShapeDtypeStruct(dproj.shape, BF16), jax.ShapeDtypeStruct(mkv.shape, F32)],
        input_output_aliases={4: 0}, name="mem_bwd",
        compiler_params=_params())(proj, mkv, o_mem, dym, dproj)


def _forward_backward(x, mem, tgt, proj, w_conv, sink, g_mem, w_kv, w_up, w_out, g_post):
    s = x.shape[0]
    tab = _rope_tables(s)
    bias = _window_bias()

    ya = _conv_fwd(proj, w_conv)
    kpad, vpad = _rope_kv(proj, tab)
    o_attn, yb = _swa_fwd(proj, kpad, vpad, tab, bias, sink)
    mn = _rmsnorm_fwd(mem, g_mem, name="mem_norm")
    mkv = _matmul(mn, w_kv, mode="nn", out_dtype=F32, tm=256, tn=1024, tk=D_MODEL, name="mem_kv")
    o_mem, ym = _mem_fwd(proj, mkv)
    merged, d_out, dy, dg_post, loss = _mid_fwd(ya, yb, ym, proj, x, tgt, w_up, w_out, g_post)
    dproj, d_ya, d_yb, d_ym, dw_up, dw_out = _mid_bwd(d_out, merged, ya, yb, ym, proj, w_up, w_out)

    dproj, dw_conv = _conv_bwd(proj, w_conv, d_ya, dproj)
    dproj, dkpad, dvpad, dsink = _swa_bwd(proj, kpad, vpad, tab, bias, sink, o_attn, d_yb, dproj)
    dproj = _rope_kv_bwd(dkpad, dvpad, tab, dproj)
    dproj, d_mkv = _mem_bwd(proj, mkv, o_mem, d_ym, dproj)

    dw_kv = _matmul(mn, d_mkv, mode="tn", out_dtype=F32, tm=1024, tn=1024, tk=256, name="dw_kv")
    d_mn = _matmul(d_mkv, w_kv, mode="nt", out_dtype=F32, tm=256, tn=1024, tk=D_MODEL, name="d_mn")
    _, dg_mem = _rmsnorm_bwd(d_mn, mem, g_mem, d_mn, name="mem_norm_bwd")

    return dict(loss=loss, dproj=dproj, dy=dy, w_conv=dw_conv, sink=dsink, g_mem=dg_mem,
                w_kv=dw_kv, w_up=dw_up, w_out=dw_out, g_post=dg_post)


N_DEV = 8


def _position():
    return lax.axis_index("x"), lax.axis_index("y"), lax.axis_index("c")


def _other_chips(x, y):
    return (((1 - x, y), 2 * (1 - x) + y), ((x, 1 - y), 2 * x + (1 - y)), ((1 - x, 1 - y), 2 * (1 - x) + (1 - y)))


def _remote(src, dst, send_sems, recv_sems, k, device):
    return pltpu.make_async_remote_copy(src_ref=src, dst_ref=dst, send_sem=send_sems.at[k], recv_sem=recv_sems.at[k],
                                        device_id=device, device_id_type=MESH)


def _rows_half(ref, hf):
    rh = ref.shape[0] // 2
    return ref.at[pl.ds(pl.multiple_of(hf * rh, 8), rh)]


def _gather_weights(shards, small=None, relations=(0, 1, 2), into=None):
    n = len(shards)
    k = 0 if small is None else 1

    def peers(x, y):
        return [(r, chip, idx) for r, (chip, idx) in enumerate(_other_chips(x, y)) if r in relations]

    def ici(ins, outs, sems, a, r, chip, src_chip, c):
        return _remote(_rows_half(ins[a], c), _rows_half(outs[a].at[src_chip], c), sems[0], sems[1], 3 * a + r,
                       (*chip, c))

    def whole(ins, outs, sems, r, chip, src_chip, c):
        return _remote(ins[n], outs[n].at[src_chip], sems[0], sems[1], 3 * n + r, (*chip, c))

    def d2d(outs, sems, a, r, idx, hf, x, y, c):
        half = _rows_half(outs[a].at[idx], hf)
        return _remote(half, half, sems[2], sems[3], 3 * a + r, (x, y, 1 - c))

    def start(ins, outs, sems):
        x, y, c = _position()
        me = 2 * x + y
        for a in range(n):
            for r, chip, _ in peers(x, y):
                ici(ins, outs, sems, a, r, chip, me, c).start()
        for r, (chip, _) in enumerate(_other_chips(x, y)):
            if k:
                whole(ins, outs, sems, r, chip, me, c).start()

    def finish(ins, outs, sems):
        x, y, c = _position()
        me = 2 * x + y
        for a in range(n):
            for r, chip, idx in peers(x, y):
                ici(ins, outs, sems, a, r, chip, idx, c).wait_recv()
                d2d(outs, sems, a, r, idx, c, x, y, c).start()
        for a in range(n):
            for r, chip, idx in peers(x, y):
                d2d(outs, sems, a, r, idx, 1 - c, x, y, c).wait_recv()
        for r, (chip, idx) in enumerate(_other_chips(x, y)):
            if k:
                whole(ins, outs, sems, r, chip, idx, c).wait_recv()
                whole(ins, outs, sems, r, chip, me, c).wait_send()
        for a in range(n):
            for r, chip, idx in peers(x, y):
                ici(ins, outs, sems, a, r, chip, me, c).wait_send()
                d2d(outs, sems, a, r, idx, c, x, y, c).wait_send()

    operands = list(shards) + ([small] if k else [])
    shapes = [jax.ShapeDtypeStruct((N_CHIPS,) + s.shape, s.dtype) for s in operands]
    aliases = {}
    if into is not None:
        assert len(into) == len(operands)
        aliases = {len(operands) + a: a for a in range(len(into))}
        operands += list(into)
    return _Carry(operands, shapes,
                  [pltpu.SemaphoreType.DMA((3 * (n + k),)), pltpu.SemaphoreType.DMA((3 * (n + k),)),
                   pltpu.SemaphoreType.DMA((3 * n,)), pltpu.SemaphoreType.DMA((3 * n,))], start, finish, aliases)


def _run_carry(carry, name):
    _, results = _carried_call(lambda ins, outs, scr: None, carry, grid=(1,), in_specs=[], out_specs=[],
                               out_shape=[], scratch=[], operands=(), name=name)
    return results


def _pair_exchange(send):
    n = len(send)

    def copies(ins, outs, sems):
        x, y, c = _position()
        return [_remote(ins[a], outs[a], sems[0], sems[1], a, (x, y, 1 - c)) for a in range(n)]

    def start(ins, outs, sems):
        for cp in copies(ins, outs, sems):
            cp.start()

    def finish(ins, outs, sems):
        for cp in copies(ins, outs, sems):
            cp.wait()

    return _Carry(send, [jax.ShapeDtypeStruct(p.shape, p.dtype) for p in send],
                  [pltpu.SemaphoreType.DMA((n,)), pltpu.SemaphoreType.DMA((n,))], start, finish)


def _chip_exchange(sums):
    n = len(sums)

    def copies(ins, outs, sems):
        x, y, c = _position()
        return [_remote(ins[a].at[idx], outs[a].at[r], sems[0], sems[1], 3 * a + r, (*chip, c))
                for a in range(n) for r, (chip, idx) in enumerate(_other_chips(x, y))]

    def start(ins, outs, sems):
        for cp in copies(ins, outs, sems):
            cp.start()

    def finish(ins, outs, sems):
        for cp in copies(ins, outs, sems):
            cp.wait()

    return _Carry(sums, [jax.ShapeDtypeStruct((3,) + p.shape[1:], p.dtype) for p in sums],
                  [pltpu.SemaphoreType.DMA((3 * n,)), pltpu.SemaphoreType.DMA((3 * n,))], start, finish)


def _pair_share(pairs):
    n = len(pairs)

    def start(ins, outs, sems):
        x, y, c = _position()
        for a in range(n):
            _remote(outs[a].at[c], outs[a].at[c], sems[0], sems[1], a, (x, y, 1 - c)).start()

    def finish(ins, outs, sems):
        x, y, c = _position()
        for a in range(n):
            _remote(outs[a].at[1 - c], outs[a].at[1 - c], sems[0], sems[1], a, (x, y, 1 - c)).wait_recv()
        for a in range(n):
            _remote(outs[a].at[c], outs[a].at[c], sems[0], sems[1], a, (x, y, 1 - c)).wait_send()

    return _Carry(pairs, [jax.ShapeDtypeStruct(p.shape, p.dtype) for p in pairs],
                  [pltpu.SemaphoreType.DMA((n,)), pltpu.SemaphoreType.DMA((n,))], start, finish,
                  aliases={a: a for a in range(n)})


def _small_allreduce(pack, share):
    rows, width = pack.shape
    n_share = len(share.ins)

    def body(p_ref, *refs):
        share_in, o_ref, share_out = refs[:n_share], refs[n_share], refs[n_share + 1:2 * n_share + 1]
        buf, send_sems, recv_sems = refs[2 * n_share + 1:2 * n_share + 4]
        share_sems = refs[2 * n_share + 4:]
        share.start(share_in, share_out, share_sems)
        x, y, c = _position()
        me = 4 * x + 2 * y + c
        buf[me] = p_ref[...]
        peers = []
        for r in range(1, N_DEV):
            fx, fy, fc = (r >> 2) & 1, (r >> 1) & 1, r & 1
            px, py, pc = (1 - x if fx else x), (1 - y if fy else y), (1 - c if fc else c)
            peers.append(((px, py, pc), 4 * px + 2 * py + pc))
        sends = [_remote(p_ref, buf.at[me], send_sems, recv_sems, r, dev) for r, (dev, _) in enumerate(peers)]
        for cp in sends:
            cp.start()
        for r, (dev, idx) in enumerate(peers):
            _remote(p_ref, buf.at[idx], send_sems, recv_sems, r, dev).wait_recv()
        for cp in sends:
            cp.wait_send()
        acc = buf[0]
        for k in range(1, N_DEV):
            acc = acc + buf[k]
        o_ref[...] = acc
        share.finish(share_in, share_out, share_sems)

    vm = pl.BlockSpec(memory_space=pltpu.VMEM)
    red, *shared = pl.pallas_call(
        body, in_specs=[vm] + [_HBM] * n_share, out_specs=[vm] + [_HBM] * n_share,
        out_shape=[jax.ShapeDtypeStruct(pack.shape, F32)] + share.out_shapes,
        scratch_shapes=[pltpu.VMEM((N_DEV, rows, width), F32), pltpu.SemaphoreType.DMA((N_DEV - 1,)),
                        pltpu.SemaphoreType.DMA((N_DEV - 1,))] + share.sems,
        input_output_aliases={1 + i: 1 + o for i, o in share.aliases.items()},
        name="small_allreduce")(pack, *share.ins)
    return red, shared


ROW_TILE_MAX = 512
SUM_TILE_MAX = 2048
BF16_SUBLANES = 16


def _row_tile(rows, most=ROW_TILE_MAX):
    if rows <= most:
        return rows
    return max(t for t in range(BF16_SUBLANES, most + 1, BF16_SUBLANES) if rows % t == 0)


def _pair_add(keep, recv, name):
    nj, rh, cols = keep.shape
    tr = _row_tile(rh, SUM_TILE_MAX)

    def body(k_ref, r_ref, o_ref):
        o_ref[...] = (k_ref[...].astype(F32) + r_ref[...].astype(F32)).astype(BF16)

    blk = pl.BlockSpec((None, tr, cols), lambda j, i: (j, i, 0))
    return pl.pallas_call(body, grid=(nj, rh // tr), in_specs=[blk, blk], out_specs=blk,
                          out_shape=jax.ShapeDtypeStruct(keep.shape, BF16), name=name,
                          compiler_params=_params())(keep, recv)


def _chip_add(sums, recv, where, name):
    _, rh, cols = sums.shape
    tr = _row_tile(rh, SUM_TILE_MAX)

    def body(w_ref, s_ref, r_ref, o_ref):
        o_ref[...] = ((s_ref[...].astype(F32) + r_ref[0].astype(F32)) + r_ref[1].astype(F32)) + r_ref[2].astype(F32)

    grid_spec = pltpu.PrefetchScalarGridSpec(
        num_scalar_prefetch=1, grid=(rh // tr,),
        in_specs=[pl.BlockSpec((None, tr, cols), lambda i, w_ref: (w_ref[0], i, 0)),
                  pl.BlockSpec((3, tr, cols), lambda i, w_ref: (0, i, 0))],
        out_specs=pl.BlockSpec((None, tr, cols), lambda i, w_ref: (w_ref[1], i, 0)))
    return pl.pallas_call(body, grid_spec=grid_spec, out_shape=jax.ShapeDtypeStruct((2, rh, cols), F32),
                          name=name, compiler_params=_params())(where, sums, recv)


def _adamw(w, g, m, v, name):
    rows, cols = w.shape
    tr = _row_tile(rows)
    assert rows % tr == 0

    def body(w_ref, g_ref, m_ref, v_ref, d_ref, mo_ref, vo_ref):
        gv = g_ref[...]
        m_new = ADAM_B1 * m_ref[...] + (1.0 - ADAM_B1) * gv
        v_new = ADAM_B2 * v_ref[...] + (1.0 - ADAM_B2) * jnp.square(gv)
        m_hat = m_new / (1.0 - ADAM_B1 ** ADAM_STEP)
        v_hat = v_new / (1.0 - ADAM_B2 ** ADAM_STEP)
        d_ref[...] = -ADAM_LR * (m_hat / (jnp.sqrt(v_hat) + ADAM_EPS) + ADAM_WD * w_ref[...])
        mo_ref[...] = m_new
        vo_ref[...] = v_new

    blk = pl.BlockSpec((tr, cols), lambda i: (i, 0))
    shp = jax.ShapeDtypeStruct((rows, cols), F32)
    return pl.pallas_call(body, grid=(rows // tr,), in_specs=[blk] * 4, out_specs=[blk] * 3,
                          out_shape=[shp] * 3, name=name, compiler_params=_params())(w, g, m, v)


def _adamw_halves(w, g2, m, v, name):
    rows, cols = w.shape
    half = cols // 2
    tr = _row_tile(rows)

    def body(w_ref, g_ref, m_ref, v_ref, go_ref, d_ref, mo_ref, vo_ref):
        gv = g_ref[...]
        go_ref[...] = gv
        m_new = ADAM_B1 * m_ref[...] + (1.0 - ADAM_B1) * gv
        v_new = ADAM_B2 * v_ref[...] + (1.0 - ADAM_B2) * jnp.square(gv)
        m_hat = m_new / (1.0 - ADAM_B1 ** ADAM_STEP)
        v_hat = v_new / (1.0 - ADAM_B2 ** ADAM_STEP)
        d_ref[...] = -ADAM_LR * (m_hat / (jnp.sqrt(v_hat) + ADAM_EPS) + ADAM_WD * w_ref[...])
        mo_ref[...] = m_new
        vo_ref[...] = v_new

    blk = pl.BlockSpec((tr, half), lambda hf, i: (i, hf))
    gsp = pl.BlockSpec((None, tr, half), lambda hf, i: (hf, i, 0))
    shp = jax.ShapeDtypeStruct((rows, cols), F32)
    return pl.pallas_call(body, grid=(2, rows // tr), in_specs=[blk, gsp, blk, blk], out_specs=[blk] * 4,
                          out_shape=[shp] * 4, name=name, compiler_params=_params())(w, g2, m, v)


SHARD_W = IN_WIDTH // N_CHIPS


def _half_major(a):
    r, c = a.shape
    return a.reshape(N_CHIPS, 2, r // N_CHIPS // 2, c).transpose(1, 0, 2, 3)


def kernel(x, mem, g_pre, w_in, w_conv, attn_sink, g_mem, w_mem_kv, w_up_a, w_up_b, w_up_m, w_out, g_post, loss_target, m_g_pre, m_w_in, m_w_conv, m_attn_sink, m_g_mem, m_w_mem_kv, m_w_up_a, m_w_up_b, m_w_up_m, m_w_out, m_g_post, v_g_pre, v_w_in, v_w_conv, v_attn_sink, v_g_mem, v_w_mem_kv, v_w_up_a, v_w_up_b, v_w_up_m, v_w_out, v_g_post):
    xi, yi, ci = _position()
    chip = 2 * xi + yi
    where = jnp.stack([chip, ci, N_CHIPS - 1 - chip]).astype(jnp.int32)

    own = [w_in[0].T.astype(BF16), w_mem_kv[0].astype(BF16),
           jnp.concatenate([w_up_a[0], w_up_b[0], w_up_m[0]], axis=0).astype(BF16), w_out[0].astype(BF16)]
    own_conv = jnp.pad(w_conv[0], ((0, 5), (0, 0)))

    def pieces(mine, got):
        got = lax.dynamic_update_slice_in_dim(got, mine[None], chip, axis=0)
        return [got[j] for j in range(N_CHIPS)]

    diag = N_CHIPS - 1 - chip
    diag_blocks = SHARD_BLOCKS + 1
    (h, h_t), (got_near, got_conv) = _rmsnorm_fwd(x[0], g_pre, name="pre_norm", transposed=True,
                                                  carry=_gather_weights(own[:1], own_conv, relations=(0, 1)))
    w_near = lax.dynamic_update_slice_in_dim(got_near, own[0][None], chip, axis=0).reshape(IN_WIDTH, D_MODEL)
    proj, (got_far, *got_rest) = _proj(
        h, w_near, n_blocks=N_IN_BLOCKS - diag_blocks, where=where, name="proj_near",
        block_of=lambda i, w: i + diag_blocks * (i >= SHARD_BLOCKS * w[2]).astype(jnp.int32),
        carry=_join(_gather_weights(own[:1], relations=(2,)), _gather_weights(own[1:], relations=(0, 1))))
    far = lax.dynamic_index_in_dim(got_far, diag, 0, keepdims=False)
    proj, gathered = _proj_far(h, w_near, far, where, into=proj,
                               carry=_gather_weights(own[1:], relations=(2,), into=got_rest))
    w_kv_full = jnp.concatenate(pieces(own[1], gathered[0]), axis=0)
    up_pieces = pieces(own[2], gathered[1])
    w_up_full = jnp.stack([jnp.concatenate([p[k * A_WIDTH:(k + 1) * A_WIDTH] for p in up_pieces], axis=1)
                           for k in range(3)])
    w_out_full = jnp.concatenate(pieces(own[3], gathered[2]), axis=0)
    w_conv_full = jnp.concatenate([p[:3] for p in pieces(own_conv, got_conv)], axis=1)

    g = _forward_backward(x[0], mem[0], loss_target[0], proj, w_conv_full, attn_sink, g_mem, w_kv_full, w_up_full,
                          w_out_full, g_post)

    half_rows = D_MODEL // 2
    up_parts = (g["w_up"].reshape(3, A_WIDTH, N_CHIPS, D_MODEL // N_CHIPS).transpose(2, 0, 1, 3)
                .reshape(N_CHIPS, 2, 3 * A_WIDTH // 2, D_MODEL // N_CHIPS).transpose(1, 0, 2, 3)).astype(BF16)
    small_parts = [_half_major(g["w_kv"]).astype(BF16), up_parts, _half_major(g["w_out"]).astype(BF16)]

    def dw_in_half(half_of, name, carry):
        dw, carried = _dw_in_t(g["dproj"], h_t, half_of=half_of, where=where, name=name, carry=carry)
        return dw.reshape(N_CHIPS, SHARD_W, half_rows), carried

    def pick(parts, hf):
        return [lax.dynamic_index_in_dim(p, hf, 0, keepdims=False) for p in parts]

    small_names = ["w_kv", "w_up", "w_out"]
    recv_small = _run_carry(_pair_exchange(pick(small_parts, 1 - ci)), "pair_exchange_small")
    sums_small = [_pair_add(k, r, "pair_add_" + nm)
                  for k, r, nm in zip(pick(small_parts, ci), recv_small, small_names)]
    dw_send, recv3_small = dw_in_half(lambda w: 1 - w[1], "dw_in_send", _chip_exchange(sums_small))
    dw_keep, (recv_in,) = dw_in_half(lambda w: w[1], "dw_in_keep", _pair_exchange([dw_send]))
    sum_in = _pair_add(dw_keep, recv_in, "pair_add_w_in")
    d_h, (recv3_in,) = _d_h(g["dproj"], w_near, far, where, carry=_chip_exchange([sum_in]))
    pairs = [_chip_add(s, r, where, "chip_add_" + nm)
             for s, r, nm in zip([sum_in] + sums_small, [recv3_in] + recv3_small, ["w_in"] + small_names)]
    grad_x, dg_pre = _rmsnorm_bwd(d_h, x[0], g_pre, g["dy"], name="pre_norm_bwd")

    zeros512 = jnp.zeros((1, D_MODEL - A_WIDTH), F32)
    conv_rows = [jnp.concatenate([g["w_conv"][k:k + 1], zeros512], axis=1) for k in range(3)]
    sink_row = jnp.pad(g["sink"][:, 0].reshape(1, N_Q_HEADS), ((0, 0), (0, D_MODEL - N_Q_HEADS)))
    loss_row = jnp.pad(g["loss"], ((0, 0), (0, D_MODEL - LANES)))
    pack = jnp.concatenate([dg_pre, g["g_mem"], g["g_post"]] + conv_rows + [sink_row, loss_row], axis=0)
    red, full = _small_allreduce(pack, _pair_share(pairs))
    loss = red[7, 0]
    small_grads = dict(
        g_pre=red[0:1], g_mem=red[1:2], g_post=red[2:3], attn_sink=red[6:7, :N_Q_HEADS],
        w_conv=lax.dynamic_slice(red[3:6, :A_WIDTH], (0, chip * LANES), (3, LANES)))

    gw_up = full[2].reshape(3, A_WIDTH, D_MODEL // N_CHIPS)
    grads = dict(small_grads, w_mem_kv=full[1].reshape(D_MODEL // N_CHIPS, 2 * MEM_WIDTH),
                 w_up_a=gw_up[0], w_up_b=gw_up[1], w_up_m=gw_up[2],
                 w_out=full[3].reshape(D_MODEL // N_CHIPS, D_MODEL))

    weights = dict(g_pre=g_pre, w_in=w_in, w_conv=w_conv, attn_sink=attn_sink, g_mem=g_mem, w_mem_kv=w_mem_kv,
                   w_up_a=w_up_a, w_up_b=w_up_b, w_up_m=w_up_m, w_out=w_out, g_post=g_post)
    m_in = dict(g_pre=m_g_pre, w_in=m_w_in, w_conv=m_w_conv, attn_sink=m_attn_sink, g_mem=m_g_mem,
                w_mem_kv=m_w_mem_kv, w_up_a=m_w_up_a, w_up_b=m_w_up_b, w_up_m=m_w_up_m, w_out=m_w_out,
                g_post=m_g_post)
    v_in = dict(g_pre=v_g_pre, w_in=v_w_in, w_conv=v_w_conv, attn_sink=v_attn_sink, g_mem=v_g_mem,
                w_mem_kv=v_w_mem_kv, w_up_a=v_w_up_a, w_up_b=v_w_up_b, w_up_m=v_w_up_m, w_out=v_w_out,
                g_post=v_g_post)
    out_g, out_d, out_m, out_v = [], [], [], []
    for nm in ("g_pre", "w_in", "w_conv", "attn_sink", "g_mem", "w_mem_kv", "w_up_a", "w_up_b", "w_up_m", "w_out",
               "g_post"):
        shape = weights[nm].shape
        if nm == "w_in":
            results = _adamw_halves(w_in[0].T, full[0], m_w_in[0].T, v_w_in[0].T, "adamw_w_in")
            for out, t in zip((out_g, out_d, out_m, out_v), results):
                out.append(t.T.reshape(shape))
            continue
        two_d = shape[-2:]
        gr = grads[nm].reshape(two_d)
        d, m_new, v_new = _adamw(weights[nm].reshape(two_d), gr, m_in[nm].reshape(two_d), v_in[nm].reshape(two_d),
                                 "adamw_" + nm)
        out_g.append(gr.reshape(shape))
        out_d.append(d.reshape(shape))
        out_m.append(m_new.reshape(shape))
        out_v.append(v_new.reshape(shape))
    return (loss, grad_x.reshape(x.shape), *out_g, *out_d, *out_m, *out_v)
```

```python
import functools

import jax
import jax.numpy as jnp
from jax import lax
from jax.experimental import pallas as pl
from jax.experimental.pallas import tpu as pltpu

F32 = jnp.float32
BF16 = jnp.bfloat16
MESH = pl.DeviceIdType.MESH

D_MODEL = 1024
EPS = 1e-6
A_WIDTH = 512
HEAD_DIM = 64
N_Q_HEADS = 8
WINDOW_BLOCK = 128
KV_PAD = 512
ROPE_THETA = 500000.0
ROT_DIM = 16
MEM_HEADS = 4
MEM_HEAD_DIM = 128
MEM_WIDTH = 512
IN_WIDTH = 7424
N_CHIPS = 4
LANES = 128
HALF_LANES = 64

PERM_SEGS = ((0, 2560), (2816, 3328), (4352, 7424), (3328, 4352), (2560, 2816))
UNPERM_SEGS = ((0, 2560), (7168, 7424), (2560, 3072), (6144, 7168), (3072, 6144))
COL_A, W_A = 0, 2048
COL_B, W_B = 2, 1024
COL_G, W_G = 1, 3072
COL_M, W_M = 6, 1024
COL_KV, W_KV = 28, 256

ADAM_LR = 0.001
ADAM_B1 = 0.9
ADAM_B2 = 0.999
ADAM_EPS = 1e-08
ADAM_WD = 0.01
ADAM_STEP = 10

VMEM_LIGHT_BYTES = 32 * 1024 * 1024
VMEM_HEAVY_BYTES = 48 * 1024 * 1024


_HBM = pl.BlockSpec(memory_space=pltpu.HBM)


def _params(heavy=False):
    return pltpu.CompilerParams(vmem_limit_bytes=VMEM_HEAVY_BYTES if heavy else VMEM_LIGHT_BYTES)


def _sigmoid(v):
    return jax.nn.sigmoid(v)


_DIMS = {"nn": (((1,), (0,)), ((), ())), "nt": (((1,), (1,)), ((), ())), "tn": (((0,), (0,)), ((), ()))}


class _Carry:
    def __init__(self, ins, out_shapes, sems, start, finish, aliases=None):
        self.ins, self.out_shapes, self.sems = list(ins), list(out_shapes), list(sems)
        self.start, self.finish, self.aliases = start, finish, dict(aliases or {})


def _join(*carries):
    def split(seq, counts):
        pos, parts = 0, []
        for n in counts:
            parts.append(seq[pos:pos + n])
            pos += n
        return parts

    n_in = [len(c.ins) for c in carries]
    n_out = [len(c.out_shapes) for c in carries]
    n_sem = [len(c.sems) for c in carries]

    def run(which):
        def go(ins, outs, sems):
            for c, i, o, sm in zip(carries, split(ins, n_in), split(outs, n_out), split(sems, n_sem)):
                getattr(c, which)(i, o, sm)
        return go

    aliases = {}
    for k, c in enumerate(carries):
        aliases.update({sum(n_in[:k]) + i: sum(n_out[:k]) + o for i, o in c.aliases.items()})
    return _Carry([a for c in carries for a in c.ins], [sh for c in carries for sh in c.out_shapes],
                  [sm for c in carries for sm in c.sems], run("start"), run("finish"), aliases)


def _carried_call(body, carry, *, grid, in_specs, out_specs, out_shape, scratch, operands, name, prefetch=None,
                  aliases=None, heavy=False):
    n_in, n_out, n_scr = len(in_specs), len(out_specs), len(scratch)
    c_in = len(carry.ins) if carry else 0
    c_out = len(carry.out_shapes) if carry else 0
    n_pre = 0 if prefetch is None else 1
    steps = 1
    for g in grid:
        steps *= g

    def wrapped(*refs):
        refs = refs[n_pre:]
        ins, cins = refs[:n_in], refs[n_in:n_in + c_in]
        outs = refs[n_in + c_in:n_in + c_in + n_out]
        couts = refs[n_in + c_in + n_out:n_in + c_in + n_out + c_out]
        rest = refs[n_in + c_in + n_out + c_out:]
        scr, sems = rest[:n_scr], rest[n_scr:]
        if carry:
            step = pl.program_id(0)
            for ax in range(1, len(grid)):
                step = step * grid[ax] + pl.program_id(ax)

            @pl.when(step == 0)
            def _():
                carry.start(cins, couts, sems)

        body(ins, outs, scr)
        if carry:
            @pl.when(step == steps - 1)
            def _():
                carry.finish(cins, couts, sems)

    all_aliases = {n_pre + i: o for i, o in (aliases or {}).items()}
    if carry:
        all_aliases.update({n_pre + n_in + i: n_out + o for i, o in carry.aliases.items()})
    all_in = list(in_specs) + [_HBM] * c_in
    all_out = list(out_specs) + [_HBM] * c_out
    all_scratch = list(scratch) + (carry.sems if carry else [])
    if n_pre:
        spec = dict(grid_spec=pltpu.PrefetchScalarGridSpec(num_scalar_prefetch=1, grid=grid, in_specs=all_in,
                                                           out_specs=all_out, scratch_shapes=all_scratch))
        pre = (prefetch,)
    else:
        spec = dict(grid=grid, in_specs=all_in, out_specs=all_out, scratch_shapes=all_scratch)
        pre = ()
    results = pl.pallas_call(
        wrapped, out_shape=list(out_shape) + (carry.out_shapes if carry else []), input_output_aliases=all_aliases,
        name=name, compiler_params=_params(heavy), **spec)(*pre, *operands, *(carry.ins if carry else []))
    return list(results[:n_out]), list(results[n_out:])


def _matmul(a, b, *, mode, out_dtype, tm, tn, tk, name, j_outer=False, carry=None):
    if mode == "nn":
        (m, k), (_, n) = a.shape, b.shape
    elif mode == "nt":
        (m, k), (n, _) = a.shape, b.shape
    else:
        (k, m), (_, n) = a.shape, b.shape
    tm, tn, tk = min(tm, m), min(tn, n), min(tk, k)
    assert m % tm == 0 and n % tn == 0 and k % tk == 0
    ni, nj, nk = m // tm, n // tn, k // tk
    dims = _DIMS[mode]

    def ij(g0, g1):
        return (g1, g0) if j_outer else (g0, g1)

    if mode == "nn":
        a_spec = pl.BlockSpec((tm, tk), lambda g0, g1, kk: (ij(g0, g1)[0], kk))
        b_spec = pl.BlockSpec((tk, tn), lambda g0, g1, kk: (kk, ij(g0, g1)[1]))
    elif mode == "nt":
        a_spec = pl.BlockSpec((tm, tk), lambda g0, g1, kk: (ij(g0, g1)[0], kk))
        b_spec = pl.BlockSpec((tn, tk), lambda g0, g1, kk: (ij(g0, g1)[1], kk))
    else:
        a_spec = pl.BlockSpec((tk, tm), lambda g0, g1, kk: (kk, ij(g0, g1)[0]))
        b_spec = pl.BlockSpec((tk, tn), lambda g0, g1, kk: (kk, ij(g0, g1)[1]))
    o_spec = pl.BlockSpec((tm, tn), lambda g0, g1, kk: ij(g0, g1))

    def part(a_ref, b_ref):
        return lax.dot_general(a_ref[...].astype(BF16), b_ref[...].astype(BF16), dims,
                               preferred_element_type=F32)

    if nk == 1:
        def body(ins, outs, scr):
            outs[0][...] = part(*ins).astype(out_dtype)
        scratch = []
    else:
        def body(ins, outs, scr):
            kk = pl.program_id(2)
            acc_ref = scr[0]

            @pl.when(kk == 0)
            def _():
                acc_ref[...] = part(*ins)

            @pl.when(kk > 0)
            def _():
                acc_ref[...] += part(*ins)

            @pl.when(kk == nk - 1)
            def _():
                outs[0][...] = acc_ref[...].astype(out_dtype)
        scratch = [pltpu.VMEM((tm, tn), F32)]

    grid = (nj, ni, nk) if j_outer else (ni, nj, nk)
    (out,), carried = _carried_call(
        body, carry, grid=grid, in_specs=[a_spec, b_spec], out_specs=[o_spec],
        out_shape=[jax.ShapeDtypeStruct((m, n), out_dtype)], scratch=scratch, operands=(a, b), name=name)
    return (out, carried) if carry else out


IN_BLOCK = 256
N_IN_BLOCKS = IN_WIDTH // IN_BLOCK
SHARD_BLOCKS = (IN_WIDTH // N_CHIPS) // IN_BLOCK
BLOCK_RUNS = tuple((a // IN_BLOCK, sum(d - c for c, d in PERM_SEGS[:k]) // IN_BLOCK, (b - a) // IN_BLOCK)
                   for k, (a, b) in enumerate(PERM_SEGS))


def _perm_block(r):
    p = r
    for ref0, perm0, n in BLOCK_RUNS:
        p = jnp.where((r >= ref0) & (r < ref0 + n), r - ref0 + perm0, p)
    return p


def _proj(h, w_t, *, n_blocks, block_of, where, name, carry=None):
    s, d = h.shape

    def body(ins, outs, scr):
        outs[0][...] = lax.dot_general(ins[0][...], ins[1][...], _DIMS["nt"], preferred_element_type=F32)

    (proj,), carried = _carried_call(
        body, carry, grid=(n_blocks,),
        in_specs=[pl.BlockSpec((s, d), lambda i, w: (0, 0)), pl.BlockSpec((IN_BLOCK, d), lambda i, w: (block_of(i, w), 0))],
        out_specs=[pl.BlockSpec((s, IN_BLOCK), lambda i, w: (0, _perm_block(block_of(i, w))))],
        out_shape=[jax.ShapeDtypeStruct((s, IN_WIDTH), F32)], scratch=[], operands=(h, w_t), name=name,
        prefetch=where)
    return (proj, carried) if carry else proj


def _proj_far(h, w_near, far, where, *, into, carry=None):
    s, d = h.shape
    n_blocks = SHARD_BLOCKS + 1
    lead = IN_WIDTH // N_CHIPS - SHARD_BLOCKS * IN_BLOCK

    def body(ins, outs, scr):
        where_ref, h_ref, w_hbm, far_hbm, _ = ins
        win, sem = scr
        i = pl.program_id(0)

        @pl.when(i == 0)
        def _():
            dg = where_ref[2]
            rows = pl.ds(pl.multiple_of(dg * (SHARD_BLOCKS * IN_BLOCK), IN_BLOCK), n_blocks * IN_BLOCK)
            window = pltpu.make_async_copy(w_hbm.at[rows], win, sem)
            window.start()
            window.wait()
            shard = pltpu.make_async_copy(far_hbm, win.at[pl.ds(pl.multiple_of(dg * lead, BF16_SUBLANES), SHARD_W)], sem)
            shard.start()
            shard.wait()

        blk = win[pl.ds(pl.multiple_of(i * IN_BLOCK, IN_BLOCK), IN_BLOCK), :]
        outs[0][...] = lax.dot_general(h_ref[...], blk, _DIMS["nt"], preferred_element_type=F32)

    anysp = pl.BlockSpec(memory_space=pl.ANY)
    (proj,), carried = _carried_call(
        body, carry, grid=(n_blocks,),
        in_specs=[pl.BlockSpec(memory_space=pltpu.SMEM), pl.BlockSpec((s, d), lambda i, w: (0, 0)), anysp, anysp, anysp],
        out_specs=[pl.BlockSpec((s, IN_BLOCK), lambda i, w: (0, _perm_block(i + SHARD_BLOCKS * w[2])))],
        out_shape=[jax.ShapeDtypeStruct((s, IN_WIDTH), F32)],
        scratch=[pltpu.VMEM((n_blocks * IN_BLOCK, d), BF16), pltpu.SemaphoreType.DMA],
        operands=(where, h, w_near, far, into), name="proj_far", prefetch=where, aliases={4: 0})
    return (proj, carried) if carry else proj


def _dw_in_t(dproj, h_t, *, half_of, where, name, carry=None):
    d, s = h_t.shape
    c = d // 2

    def body(ins, outs, scr):
        outs[0][...] = lax.dot_general(ins[1][...], ins[0][...], _DIMS["nn"], preferred_element_type=F32).T.astype(BF16)

    (dw,), carried = _carried_call(
        body, carry, grid=(N_IN_BLOCKS,),
        in_specs=[pl.BlockSpec((s, IN_BLOCK), lambda r, w: (0, _perm_block(r))),
                  pl.BlockSpec((c, s), lambda r, w: (half_of(w), 0))],
        out_specs=[pl.BlockSpec((IN_BLOCK, c), lambda r, w: (r, 0))],
        out_shape=[jax.ShapeDtypeStruct((IN_WIDTH, c), BF16)], scratch=[], operands=(dproj, h_t), name=name,
        prefetch=where)
    return (dw, carried) if carry else dw


def _d_h(dproj, w_near, far, where, *, carry=None):
    s = dproj.shape[0]
    d = w_near.shape[1]
    tm = min(s, 256)

    def body(ins, outs, scr):
        where_ref, a_ref, w_hbm, far_hbm = ins
        w_ref, sem = scr

        @pl.when(pl.program_id(0) == 0)
        def _():
            whole = pltpu.make_async_copy(w_hbm, w_ref, sem)
            whole.start()
            whole.wait()
            rows = pl.ds(pl.multiple_of(where_ref[2] * SHARD_W, BF16_SUBLANES), SHARD_W)
            part = pltpu.make_async_copy(far_hbm, w_ref.at[rows], sem)
            part.start()
            part.wait()

        acc = None
        for ref0, perm0, n in BLOCK_RUNS:
            term = jnp.dot(a_ref[:, perm0 * IN_BLOCK:(perm0 + n) * IN_BLOCK],
                           w_ref[ref0 * IN_BLOCK:(ref0 + n) * IN_BLOCK, :], preferred_element_type=F32)
            acc = term if acc is None else acc + term
        outs[0][...] = acc

    anysp = pl.BlockSpec(memory_space=pl.ANY)
    (dh,), carried = _carried_call(
        body, carry, grid=(s // tm,),
        in_specs=[pl.BlockSpec(memory_space=pltpu.SMEM), pl.BlockSpec((tm, IN_WIDTH), lambda i: (i, 0)), anysp, anysp],
        out_specs=[pl.BlockSpec((tm, d), lambda i: (i, 0))],
        out_shape=[jax.ShapeDtypeStruct((s, d), F32)],
        scratch=[pltpu.VMEM((IN_WIDTH, d), BF16), pltpu.SemaphoreType.DMA],
        operands=(where, dproj, w_near, far), name="d_h", heavy=True)
    return (dh, carried) if carry else dh


def _rmsnorm_fwd(x, g, *, name, transposed=False, carry=None):
    s, d = x.shape
    ts = min(512, s)

    def body(ins, outs, scr):
        xv = ins[0][...]
        r = lax.rsqrt(jnp.mean(xv * xv, axis=-1, keepdims=True) + EPS)
        hv = (xv * r) * ins[1][...]
        outs[0][...] = hv.astype(BF16)
        if transposed:
            outs[1][...] = hv.T.astype(BF16)

    out_specs = [pl.BlockSpec((ts, d), lambda i: (i, 0))]
    out_shape = [jax.ShapeDtypeStruct((s, d), BF16)]
    if transposed:
        out_specs.append(pl.BlockSpec((d, ts), lambda i: (0, i)))
        out_shape.append(jax.ShapeDtypeStruct((d, s), BF16))
    outs, carried = _carried_call(
        body, carry, grid=(s // ts,),
        in_specs=[pl.BlockSpec((ts, d), lambda i: (i, 0)), pl.BlockSpec((1, d), lambda i: (0, 0))],
        out_specs=out_specs, out_shape=out_shape, scratch=[], operands=(x, g), name=name)
    result = tuple(outs) if transposed else outs[0]
    return (result, carried) if carry else result


def _rmsnorm_bwd(dh, x, g, res, *, name, carry=None):
    s, d = x.shape
    ts = min(256, s)

    def body(ins, outs, scr):
        dh_ref, x_ref, g_ref, res_ref = ins
        dx_ref, dg_ref = outs
        xv = x_ref[...]
        r = lax.rsqrt(jnp.mean(xv * xv, axis=-1, keepdims=True) + EPS)
        xh = xv * r
        dhv = dh_ref[...]
        part = jnp.sum(dhv * xh, axis=0, keepdims=True)

        @pl.when(pl.program_id(0) == 0)
        def _():
            dg_ref[...] = part

        @pl.when(pl.program_id(0) > 0)
        def _():
            dg_ref[...] += part

        dxh = dhv * g_ref[...]
        dx_ref[...] = res_ref[...] + r * (dxh - xh * jnp.mean(dxh * xh, axis=-1, keepdims=True))

    row = pl.BlockSpec((ts, d), lambda i: (i, 0))
    vec = pl.BlockSpec((1, d), lambda i: (0, 0))
    outs, carried = _carried_call(
        body, carry, grid=(s // ts,), in_specs=[row, row, vec, row], out_specs=[row, vec],
        out_shape=[jax.ShapeDtypeStruct((s, d), F32), jax.ShapeDtypeStruct((1, d), F32)],
        scratch=[], operands=(dh, x, g, res), name=name)
    return (*outs, carried) if carry else tuple(outs)


MID_TILE = 256


def _gated_branches(y_refs, wup_ref, gl):
    d = D_MODEL
    us = [jnp.dot(y_refs[k][...], wup_ref[k], preferred_element_type=F32) for k in range(3)]
    sg = [_sigmoid(gl[:, k * d:(k + 1) * d]) for k in range(3)]
    return us, sg


def _mid_fwd(ya, yb, ym, proj, x, tgt, w_up, w_out, g_post):
    s, d = x.shape
    ts = MID_TILE

    def body(ya_ref, yb_ref, ym_ref, g_ref, x_ref, t_ref, wup_ref, wout_ref, gp_ref,
             m_ref, do_ref, dy_ref, dg_ref, loss_ref):
        us, sg = _gated_branches((ya_ref, yb_ref, ym_ref), wup_ref, g_ref[...])
        merged = (sg[0] * us[0] + sg[1] * us[1] + sg[2] * us[2]).astype(BF16)
        m_ref[...] = merged
        ov = jnp.dot(merged, wout_ref[...], preferred_element_type=F32)
        r = lax.rsqrt(jnp.mean(ov * ov, axis=-1, keepdims=True) + EPS)
        nh = ov * r
        gv = gp_ref[...]
        e = (x_ref[...] + nh * gv) - t_ref[...]
        lpart = 0.5 * jnp.sum(jnp.mean(e * e, axis=-1, keepdims=True), axis=0, keepdims=True)
        dy = e * (1.0 / d)
        dgp = jnp.sum(dy * nh, axis=0, keepdims=True)

        @pl.when(pl.program_id(0) == 0)
        def _():
            dg_ref[...] = dgp
            loss_ref[...] = jnp.broadcast_to(lpart, loss_ref.shape)

        @pl.when(pl.program_id(0) > 0)
        def _():
            dg_ref[...] += dgp
            loss_ref[...] += jnp.broadcast_to(lpart, loss_ref.shape)

        dn = dy * gv
        dy_ref[...] = dy
        do_ref[...] = (r * (dn - nh * jnp.mean(dn * nh, axis=-1, keepdims=True))).astype(BF16)

    row = pl.BlockSpec((ts, d), lambda i: (i, 0))
    ysp = pl.BlockSpec((ts, A_WIDTH), lambda i: (i, 0))
    vec = pl.BlockSpec((1, d), lambda i: (0, 0))
    return pl.pallas_call(
        body, grid=(s // ts,),
        in_specs=[ysp, ysp, ysp, pl.BlockSpec((ts, W_G), lambda i: (i, COL_G)), row, row,
                  pl.BlockSpec((3, A_WIDTH, d), lambda i: (0, 0, 0)), pl.BlockSpec((d, d), lambda i: (0, 0)), vec],
        out_specs=[row, row, row, vec, pl.BlockSpec((1, LANES), lambda i: (0, 0))],
        out_shape=[jax.ShapeDtypeStruct((s, d), BF16), jax.ShapeDtypeStruct((s, d), BF16),
                   jax.ShapeDtypeStruct((s, d), F32), jax.ShapeDtypeStruct((1, d), F32),
                   jax.ShapeDtypeStruct((1, LANES), F32)],
        name="mid_fwd", compiler_params=_params(heavy=True))(ya, yb, ym, proj, x, tgt, w_up, w_out, g_post)


def _mid_bwd(d_out, merged, ya, yb, ym, proj, w_up, w_out):
    s, d = merged.shape
    ts = MID_TILE
    last = s // ts - 1

    def body(do_ref, m_ref, ya_ref, yb_ref, ym_ref, g_ref, wup_ref, wout_ref,
             dp_ref, dya_ref, dyb_ref, dym_ref, dwup_hbm, dwout_hbm, dwup_acc, dwout_acc):
        i = pl.program_id(0)

        @pl.when(i == 0)
        def _():
            dwup_acc[...] = jnp.zeros_like(dwup_acc)
            dwout_acc[...] = jnp.zeros_like(dwout_acc)

        y_refs = (ya_ref, yb_ref, ym_ref)
        us, sg = _gated_branches(y_refs, wup_ref, g_ref[...])
        dov = do_ref[...]
        dwout_acc[...] += lax.dot_general(m_ref[...], dov, _DIMS["tn"], preferred_element_type=F32)
        dm = lax.dot_general(dov, wout_ref[...], _DIMS["nt"], preferred_element_type=F32)
        for k, dy_ref in enumerate((dya_ref, dyb_ref, dym_ref)):
            dp_ref[:, k * d:(k + 1) * d] = ((dm * us[k]) * (sg[k] * (1.0 - sg[k]))).astype(BF16)
            du = (sg[k] * dm).astype(BF16)
            dy_ref[...] = lax.dot_general(du, wup_ref[k], _DIMS["nt"], preferred_element_type=F32)
            dwup_acc[k] += lax.dot_general(y_refs[k][...], du, _DIMS["tn"], preferred_element_type=F32)

        @pl.when(i == last)
        def _():
            pltpu.sync_copy(dwup_acc, dwup_hbm)
            pltpu.sync_copy(dwout_acc, dwout_hbm)

    row = pl.BlockSpec((ts, d), lambda i: (i, 0))
    ysp = pl.BlockSpec((ts, A_WIDTH), lambda i: (i, 0))
    gsp = pl.BlockSpec((ts, W_G), lambda i: (i, COL_G))
    anysp = pl.BlockSpec(memory_space=pl.ANY)
    yshape = jax.ShapeDtypeStruct((s, A_WIDTH), F32)
    return pl.pallas_call(
        body, grid=(s // ts,),
        in_specs=[row, row, ysp, ysp, ysp, gsp, pl.BlockSpec((3, A_WIDTH, d), lambda i: (0, 0, 0)),
                  pl.BlockSpec((d, d), lambda i: (0, 0))],
        out_specs=[gsp, ysp, ysp, ysp, anysp, anysp],
        out_shape=[jax.ShapeDtypeStruct((s, IN_WIDTH), BF16), yshape, yshape, yshape,
                   jax.ShapeDtypeStruct((3, A_WIDTH, d), F32), jax.ShapeDtypeStruct((d, d), F32)],
        scratch_shapes=[pltpu.VMEM((3, A_WIDTH, d), F32), pltpu.VMEM((d, d), F32)],
        name="mid_bwd", compiler_params=_params(heavy=True))(d_out, merged, ya, yb, ym, proj, w_up, w_out)


def _conv_core(blk, prev, nxt, w, i, last, ts):
    c = A_WIDTH
    ab, ac, ax, az = blk[:, :c], blk[:, c:2 * c], blk[:, 2 * c:3 * c], blk[:, 3 * c:]
    cu = ac * ax
    cu_prev = (prev[7:8, c:2 * c] * prev[7:8, 2 * c:3 * c]) * jnp.where(i > 0, 1.0, 0.0)
    cu_next = (nxt[0:1, c:2 * c] * nxt[0:1, 2 * c:3 * c]) * jnp.where(i < last, 1.0, 0.0)
    row = lax.broadcasted_iota(jnp.int32, (ts, c), 0)
    cm1 = jnp.where(row == 0, cu_prev, pltpu.roll(cu, 1, 0))
    cp1 = jnp.where(row == ts - 1, cu_next, pltpu.roll(cu, ts - 1, 0))
    yc = cm1 * w[0:1] + cu * w[1:2] + cp1 * w[2:3]
    return ab, ac, ax, az, cu, cm1, cp1, yc, row


def _halo_specs(ts, width, col, nblk8):
    prev = pl.BlockSpec((8, width), lambda i: (jnp.maximum(i * (ts // 8) - 1, 0), col))
    nxt = pl.BlockSpec((8, width), lambda i: (jnp.minimum((i + 1) * (ts // 8), nblk8 - 1), col))
    return prev, nxt


def _conv_fwd(proj, w_conv):
    s = proj.shape[0]
    ts = 256
    last = s // ts - 1

    def body(a_ref, ap_ref, an_ref, w_ref, ya_ref):
        i = pl.program_id(0)
        ab, _, _, az, _, _, _, yc, _ = _conv_core(a_ref[...], ap_ref[...], an_ref[...], w_ref[...], i, last, ts)
        ya_ref[...] = ((ab * yc) * (az * _sigmoid(az))).astype(BF16)

    prev, nxt = _halo_specs(ts, W_A, COL_A, s // 8)
    return pl.pallas_call(
        body, grid=(s // ts,),
        in_specs=[pl.BlockSpec((ts, W_A), lambda i: (i, COL_A)), prev, nxt,
                  pl.BlockSpec((3, A_WIDTH), lambda i: (0, 0))],
        out_specs=pl.BlockSpec((ts, A_WIDTH), lambda i: (i, 0)),
        out_shape=jax.ShapeDtypeStruct((s, A_WIDTH), BF16), name="conv_fwd",
        compiler_params=_params())(proj, proj, proj, w_conv)


def _conv_bwd(proj, w_conv, dya, dproj):
    s = proj.shape[0]
    ts = 256
    last = s // ts - 1
    c = A_WIDTH

    def body(a_ref, ap_ref, an_ref, w_ref, d_ref, dp_ref, dn_ref, _, dproj_ref, dw_ref):
        i = pl.program_id(0)
        w = w_ref[...]
        prev, nxt = ap_ref[...], an_ref[...]
        ab, ac, ax, az, cu, cm1, cp1, yc, row = _conv_core(a_ref[...], prev, nxt, w, i, last, ts)
        sg = _sigmoid(az)
        sz = az * sg
        dya_v = d_ref[...]
        dyc = dya_v * sz * ab
        dproj_ref[:, :c] = (dya_v * sz * yc).astype(BF16)
        dproj_ref[:, 3 * c:] = (dya_v * (ab * yc) * (sg * (1.0 + az * (1.0 - sg)))).astype(BF16)

        def halo_dyc(a_row, d_row):
            azr = a_row[:, 3 * c:]
            return d_row * (azr * _sigmoid(azr)) * a_row[:, :c]

        dyc_prev = halo_dyc(prev[7:8], dp_ref[...][7:8]) * jnp.where(i > 0, 1.0, 0.0)
        dyc_next = halo_dyc(nxt[0:1], dn_ref[...][0:1]) * jnp.where(i < last, 1.0, 0.0)
        dyc_m1 = jnp.where(row == 0, dyc_prev, pltpu.roll(dyc, 1, 0))
        dyc_p1 = jnp.where(row == ts - 1, dyc_next, pltpu.roll(dyc, ts - 1, 0))
        dcu = dyc_p1 * w[0:1] + dyc * w[1:2] + dyc_m1 * w[2:3]
        dproj_ref[:, c:2 * c] = (dcu * ax).astype(BF16)
        dproj_ref[:, 2 * c:3 * c] = (dcu * ac).astype(BF16)
        dw = [jnp.sum(dyc * t, axis=0, keepdims=True) for t in (cm1, cu, cp1)]

        @pl.when(i == 0)
        def _():
            for k in range(3):
                dw_ref[k:k + 1, :] = dw[k]

        @pl.when(i > 0)
        def _():
            for k in range(3):
                dw_ref[k:k + 1, :] += dw[k]

    prev, nxt = _halo_specs(ts, W_A, COL_A, s // 8)
    dprev, dnxt = _halo_specs(ts, A_WIDTH, 0, s // 8)
    return pl.pallas_call(
        body, grid=(s // ts,),
        in_specs=[pl.BlockSpec((ts, W_A), lambda i: (i, COL_A)), prev, nxt,
                  pl.BlockSpec((3, A_WIDTH), lambda i: (0, 0)),
                  pl.BlockSpec((ts, A_WIDTH), lambda i: (i, 0)), dprev, dnxt,
                  pl.BlockSpec(memory_space=pl.ANY)],
        out_specs=[pl.BlockSpec((ts, W_A), lambda i: (i, COL_A)), pl.BlockSpec((3, A_WIDTH), lambda i: (0, 0))],
        out_shape=[jax.ShapeDtypeStruct(dproj.shape, BF16), jax.ShapeDtypeStruct((3, A_WIDTH), F32)],
        input_output_aliases={7: 0}, name="conv_bwd",
        compiler_params=_params())(proj, proj, proj, w_conv, dya, dya, dya, dproj)


def _rope_tables(s):
    half = ROT_DIM // 2
    dim = jnp.arange(LANES) % HEAD_DIM
    inv_freq = jnp.power(jnp.float32(ROPE_THETA), -(dim % half).astype(F32) * (2.0 / ROT_DIM))
    ang = jnp.arange(s).astype(F32)[:, None] * inv_freq[None, :]
    cos, sin = jnp.cos(ang), jnp.sin(ang)
    first, second = (dim < half)[None, :], ((dim >= half) & (dim < ROT_DIM))[None, :]
    c = jnp.where(first | second, cos, 1.0)
    s1 = jnp.where(first, -sin, 0.0)
    s2 = jnp.where(second, sin, 0.0)
    return jnp.concatenate([c, s1, s2], axis=1)


def _rope(t, tab):
    return (t * tab[:, :LANES] + pltpu.roll(t, LANES - 8, 1) * tab[:, LANES:2 * LANES]
            + pltpu.roll(t, 8, 1) * tab[:, 2 * LANES:])


def _rope_transpose(dt, tab):
    return (dt * tab[:, :LANES] + pltpu.roll(dt * tab[:, LANES:2 * LANES], 8, 1)
            + pltpu.roll(dt * tab[:, 2 * LANES:], LANES - 8, 1))


def _rope_kv(proj, tab):
    s = proj.shape[0]
    nb = s // KV_PAD

    def body(kv_ref, t_ref, k_ref, v_ref):
        j = pl.program_id(0)
        inside = jnp.where((j > 0) & (j <= nb), 1.0, 0.0)
        kv = kv_ref[...]
        k_ref[...] = (_rope(kv[:, :LANES], t_ref[...]) * inside).astype(BF16)
        v_ref[...] = (kv[:, LANES:] * inside).astype(BF16)

    def src(j):
        return jnp.clip(j - 1, 0, nb - 1)

    o_spec = pl.BlockSpec((KV_PAD, LANES), lambda j: (j, 0))
    shp = jax.ShapeDtypeStruct((s + 2 * KV_PAD, LANES), BF16)
    return pl.pallas_call(
        body, grid=(nb + 2,),
        in_specs=[pl.BlockSpec((KV_PAD, W_KV), lambda j: (src(j), COL_KV)),
                  pl.BlockSpec((KV_PAD, 3 * LANES), lambda j: (src(j), 0))],
        out_specs=[o_spec, o_spec], out_shape=[shp, shp], name="rope_kv",
        compiler_params=_params())(proj, tab)


def _rope_kv_bwd(dkpad, dvpad, tab, dproj):
    s = tab.shape[0]
    nb = s // KV_PAD

    def body(dk_ref, dv_ref, t_ref, _, dp_ref):
        dp_ref[:, :LANES] = _rope_transpose(dk_ref[...], t_ref[...]).astype(BF16)
        dp_ref[:, LANES:] = dv_ref[...].astype(BF16)

    pad_spec = pl.BlockSpec((KV_PAD, LANES), lambda j: (j + 1, 0))
    return pl.pallas_call(
        body, grid=(nb,),
        in_specs=[pad_spec, pad_spec, pl.BlockSpec((KV_PAD, 3 * LANES), lambda j: (j, 0)),
                  pl.BlockSpec(memory_space=pl.ANY)],
        out_specs=pl.BlockSpec((KV_PAD, W_KV), lambda j: (j, COL_KV)),
        out_shape=jax.ShapeDtypeStruct(dproj.shape, BF16), input_output_aliases={3: 0},
        name="rope_kv_bwd", compiler_params=_params())(dkpad, dvpad, tab, dproj)


def _window_start(n):
    return pl.multiple_of((n - 1) * WINDOW_BLOCK + KV_PAD, WINDOW_BLOCK)


def _window_operands(k_ref, v_ref, n, lo):
    start = _window_start(n)
    kw = k_ref[pl.ds(start, 3 * WINDOW_BLOCK), :].astype(F32)
    vw = v_ref[pl.ds(start, 3 * WINDOW_BLOCK), :].astype(F32)
    kr, vr = pltpu.roll(kw, HALF_LANES, 1), pltpu.roll(vw, HALF_LANES, 1)
    k2 = (jnp.where(lo, kw, kr).astype(BF16), jnp.where(lo, kr, kw).astype(BF16))
    v2 = (jnp.where(lo, vw, vr).astype(BF16), jnp.where(lo, vr, vw).astype(BF16))
    return k2, v2


HEADS_PER_GROUP = 4
SWA_FWD_BLOCKS = 1
SWA_BWD_BLOCKS = 2


def _window_bias():
    wb = WINDOW_BLOCK
    qi = lax.broadcasted_iota(jnp.int32, (wb, 3 * wb), 0)
    kj = lax.broadcasted_iota(jnp.int32, (wb, 3 * wb), 1)
    band = (kj >= qi) & (kj <= qi + 2 * wb)
    cases = jnp.stack([band & (kj >= wb), band, band & (kj < 2 * wb)])
    return jnp.where(cases, 0.0, -jnp.inf).astype(F32)


def _block_bias(bias_ref, n, n_blocks):
    case = jnp.where(n == 0, 0, jnp.where(n == n_blocks - 1, 2, 1))
    one = bias_ref[case]
    return jnp.concatenate([one] * HEADS_PER_GROUP, axis=0)


def _stack_heads(pair0, pair1, lo):
    return jnp.concatenate([jnp.where(lo, pair0, 0.0), jnp.where(lo, 0.0, pair0),
                            jnp.where(lo, pair1, 0.0), jnp.where(lo, 0.0, pair1)], axis=0)


def _unstack_pair(stacked, i, lo):
    wb = WINDOW_BLOCK
    return jnp.where(lo, stacked[2 * i * wb:(2 * i + 1) * wb], stacked[(2 * i + 1) * wb:(2 * i + 2) * wb])


def _sink_column(sink_ref, g):
    wb = WINDOW_BLOCK
    return jnp.concatenate([jnp.full((wb, 1), sink_ref[0, HEADS_PER_GROUP * g + i], F32)
                            for i in range(HEADS_PER_GROUP)], axis=0)


def _head_exp(q4, k2g, bias, sink):
    sc = lax.dot_general(q4, k2g, _DIMS["nt"], preferred_element_type=F32) * (HEAD_DIM ** -0.5) + bias
    m = jnp.maximum(jnp.max(sc, axis=1, keepdims=True), sink)
    return jnp.exp(sc - m).astype(BF16), jnp.exp(sink - m)


def _swa_fwd(proj, kpad, vpad, tab, bias, sink):
    s = proj.shape[0]
    wb = WINDOW_BLOCK

    def body(b_ref, k_ref, v_ref, t_ref, bias_ref, sink_ref, o_ref, y_ref):
        lo = lax.broadcasted_iota(jnp.int32, (wb, LANES), 1) < HALF_LANES
        lo_w = lax.broadcasted_iota(jnp.int32, (3 * wb, LANES), 1) < HALF_LANES
        for sub in range(SWA_FWD_BLOCKS):
            n = pl.program_id(0) * SWA_FWD_BLOCKS + sub
            rows = slice(sub * wb, (sub + 1) * wb)
            k2, v2 = _window_operands(k_ref, v_ref, n, lo_w)
            valid = _block_bias(bias_ref, n, s // wb)
            tab_v = t_ref[rows, :]
            ones = jnp.ones((3 * wb, LANES), BF16)
            for g in range(2):
                qr = [_rope(b_ref[rows, (2 * g + i) * LANES:(2 * g + i + 1) * LANES], tab_v) for i in range(2)]
                q4 = _stack_heads(qr[0], qr[1], lo).astype(BF16)
                e, es = _head_exp(q4, k2[g], valid, _sink_column(sink_ref, g))
                ox = jnp.dot(e, jnp.concatenate([v2[g], ones], axis=1), preferred_element_type=F32)
                o4 = ox[:, :LANES] * (1.0 / (ox[:, LANES:] + es))
                for i in range(2):
                    cols = slice((2 * g + i) * LANES, (2 * g + i + 1) * LANES)
                    op = _unstack_pair(o4, i, lo)
                    o_ref[rows, cols] = op
                    zp = b_ref[rows, A_WIDTH + cols.start:A_WIDTH + cols.stop]
                    y_ref[rows, cols] = (op * (zp * _sigmoid(zp))).astype(BF16)

    tq = SWA_FWD_BLOCKS * wb
    pad_spec = pl.BlockSpec((s + 2 * KV_PAD, LANES), lambda n: (0, 0))
    o_spec = pl.BlockSpec((tq, A_WIDTH), lambda n: (n, 0))
    return pl.pallas_call(
        body, grid=(s // tq,),
        in_specs=[pl.BlockSpec((tq, W_B), lambda n: (n, COL_B)), pad_spec, pad_spec,
                  pl.BlockSpec((tq, 3 * LANES), lambda n: (n, 0)),
                  pl.BlockSpec(bias.shape, lambda n: (0, 0, 0)), pl.BlockSpec(memory_space=pltpu.SMEM)],
        out_specs=[o_spec, o_spec],
        out_shape=[jax.ShapeDtypeStruct((s, A_WIDTH), F32), jax.ShapeDtypeStruct((s, A_WIDTH), BF16)],
        name="swa_fwd", compiler_params=_params())(proj, kpad, vpad, tab, bias, sink)


def _swa_bwd(proj, kpad, vpad, tab, bias, sink, o_attn, dyb, dproj):
    s = proj.shape[0]
    wb = WINDOW_BLOCK
    scale = HEAD_DIM ** -0.5

    def body(b_ref, k_ref, v_ref, t_ref, bias_ref, sink_ref, o_ref, dy_ref, _, dp_ref, dk_ref, dv_ref, ds_ref):
        @pl.when(pl.program_id(0) == 0)
        def _():
            dk_ref[...] = jnp.zeros_like(dk_ref)
            dv_ref[...] = jnp.zeros_like(dv_ref)
            ds_ref[...] = jnp.zeros_like(ds_ref)

        lo = lax.broadcasted_iota(jnp.int32, (wb, LANES), 1) < HALF_LANES
        lo_w = lax.broadcasted_iota(jnp.int32, (3 * wb, LANES), 1) < HALF_LANES
        for sub in range(SWA_BWD_BLOCKS):
            n = pl.program_id(0) * SWA_BWD_BLOCKS + sub
            rows = slice(sub * wb, (sub + 1) * wb)
            k2, v2 = _window_operands(k_ref, v_ref, n, lo_w)
            valid = _block_bias(bias_ref, n, s // wb)
            tab_v = t_ref[rows, :]
            ones = jnp.ones((3 * wb, LANES), BF16)
            dks, dvs = [], []
            for g in range(2):
                qr, op, do = [], [], []
                for i in range(2):
                    cols = slice((2 * g + i) * LANES, (2 * g + i + 1) * LANES)
                    zcols = slice(A_WIDTH + cols.start, A_WIDTH + cols.stop)
                    qr.append(_rope(b_ref[rows, cols], tab_v))
                    zp = b_ref[rows, zcols]
                    sg = _sigmoid(zp)
                    op.append(o_ref[rows, cols])
                    dyp = dy_ref[rows, cols]
                    do.append(dyp * (zp * sg))
                    dp_ref[rows, zcols] = (dyp * op[i] * (sg * (1.0 + zp * (1.0 - sg)))).astype(BF16)
                q4 = _stack_heads(qr[0], qr[1], lo).astype(BF16)
                do4 = _stack_heads(do[0], do[1], lo)
                o4 = jnp.concatenate([op[0], op[0], op[1], op[1]], axis=0)
                e, es = _head_exp(q4, k2[g], valid, _sink_column(sink_ref, g))
                inv = 1.0 / (jnp.dot(e, ones, preferred_element_type=F32) + es)
                prob = e.astype(F32) * jnp.concatenate([inv, inv, inv], axis=1)
                delta = jnp.sum(do4 * o4, axis=1, keepdims=True)
                do4b = do4.astype(BF16)
                dprob = lax.dot_general(do4b, v2[g], _DIMS["nt"], preferred_element_type=F32)
                dsc = (prob * (dprob - delta)).astype(BF16)
                sink_terms = (es * inv[:, :1]) * delta
                for i in range(HEADS_PER_GROUP):
                    h = HEADS_PER_GROUP * g + i
                    dsink = -jnp.sum(sink_terms[i * wb:(i + 1) * wb], axis=0, keepdims=True)
                    ds_ref[h:h + 1, :] += jnp.broadcast_to(dsink, (1, LANES))
                dq4 = jnp.dot(dsc, k2[g], preferred_element_type=F32) * scale
                for i in range(2):
                    cols = slice((2 * g + i) * LANES, (2 * g + i + 1) * LANES)
                    dp_ref[rows, cols] = _rope_transpose(_unstack_pair(dq4, i, lo), tab_v).astype(BF16)
                dk2 = lax.dot_general(dsc, q4, _DIMS["tn"], preferred_element_type=F32) * scale
                dv2 = lax.dot_general(prob.astype(BF16), do4b, _DIMS["tn"], preferred_element_type=F32)
                dks.append(dk2 + pltpu.roll(dk2, HALF_LANES, 1))
                dvs.append(dv2 + pltpu.roll(dv2, HALF_LANES, 1))
            start = _window_start(n)
            dk_ref[pl.ds(start, 3 * wb), :] += jnp.where(lo_w, dks[0], dks[1])
            dv_ref[pl.ds(start, 3 * wb), :] += jnp.where(lo_w, dvs[0], dvs[1])

    tq = SWA_BWD_BLOCKS * wb
    pad_spec = pl.BlockSpec((s + 2 * KV_PAD, LANES), lambda n: (0, 0))
    blk = pl.BlockSpec((tq, A_WIDTH), lambda n: (n, 0))
    bsp = pl.BlockSpec((tq, W_B), lambda n: (n, COL_B))
    pad_shape = jax.ShapeDtypeStruct((s + 2 * KV_PAD, LANES), F32)
    return pl.pallas_call(
        body, grid=(s // tq,),
        in_specs=[bsp, pad_spec, pad_spec, pl.BlockSpec((tq, 3 * LANES), lambda n: (n, 0)),
                  pl.BlockSpec(bias.shape, lambda n: (0, 0, 0)), pl.BlockSpec(memory_space=pltpu.SMEM), blk, blk,
                  pl.BlockSpec(memory_space=pl.ANY)],
        out_specs=[bsp, pad_spec, pad_spec, pl.BlockSpec((8, LANES), lambda n: (0, 0))],
        out_shape=[jax.ShapeDtypeStruct(dproj.shape, BF16), pad_shape, pad_shape,
                   jax.ShapeDtypeStruct((8, LANES), F32)],
        input_output_aliases={8: 0}, name="swa_bwd",
        compiler_params=_params())(proj, kpad, vpad, tab, bias, sink, o_attn, dyb, dproj)


def _mem_exp(qh, mk):
    sc = lax.dot_general(qh, mk, _DIMS["nt"], preferred_element_type=F32) * (MEM_HEAD_DIM ** -0.5)
    return jnp.exp(sc - jnp.max(sc, axis=1, keepdims=True)).astype(BF16)


def _mem_fwd(proj, mkv):
    s = proj.shape[0]
    ts = 512
    mlen = mkv.shape[0]

    def body(m_ref, kv_ref, o_ref, y_ref):
        ones = jnp.ones((mlen, LANES), BF16)
        for h in range(MEM_HEADS):
            cols = slice(h * LANES, (h + 1) * LANES)
            mk = kv_ref[:, cols].astype(BF16)
            mv = kv_ref[:, MEM_WIDTH + h * LANES:MEM_WIDTH + (h + 1) * LANES].astype(BF16)
            e = _mem_exp(m_ref[:, cols].astype(BF16), mk)
            ox = jnp.dot(e, jnp.concatenate([mv, ones], axis=1), preferred_element_type=F32)
            oh = ox[:, :LANES] * (1.0 / ox[:, LANES:])
            o_ref[:, cols] = oh
            zh = m_ref[:, MEM_WIDTH + h * LANES:MEM_WIDTH + (h + 1) * LANES]
            y_ref[:, cols] = (oh * (zh * _sigmoid(zh))).astype(BF16)

    o_spec = pl.BlockSpec((ts, MEM_WIDTH), lambda i: (i, 0))
    return pl.pallas_call(
        body, grid=(s // ts,),
        in_specs=[pl.BlockSpec((ts, W_M), lambda i: (i, COL_M)),
                  pl.BlockSpec((mlen, 2 * MEM_WIDTH), lambda i: (0, 0))],
        out_specs=[o_spec, o_spec],
        out_shape=[jax.ShapeDtypeStruct((s, MEM_WIDTH), F32), jax.ShapeDtypeStruct((s, MEM_WIDTH), BF16)],
        name="mem_fwd", compiler_params=_params())(proj, mkv)


def _mem_bwd(proj, mkv, o_mem, dym, dproj):
    s = proj.shape[0]
    ts = 512
    mlen = mkv.shape[0]
    scale = MEM_HEAD_DIM ** -0.5

    def body(m_ref, kv_ref, o_ref, dy_ref, _, dp_ref, dkv_ref):
        @pl.when(pl.program_id(0) == 0)
        def _():
            dkv_ref[...] = jnp.zeros_like(dkv_ref)

        ones = jnp.ones((mlen, LANES), BF16)
        for h in range(MEM_HEADS):
            cols = slice(h * LANES, (h + 1) * LANES)
            vcols = slice(MEM_WIDTH + h * LANES, MEM_WIDTH + (h + 1) * LANES)
            mk = kv_ref[:, cols].astype(BF16)
            mv = kv_ref[:, vcols].astype(BF16)
            qh = m_ref[:, cols].astype(BF16)
            zh = m_ref[:, vcols]
            sg = _sigmoid(zh)
            oh = o_ref[:, cols]
            dyh = dy_ref[:, cols]
            doh = dyh * (zh * sg)
            dp_ref[:, vcols] = (dyh * oh * (sg * (1.0 + zh * (1.0 - sg)))).astype(BF16)
            e = _mem_exp(qh, mk)
            inv = 1.0 / jnp.dot(e, ones, preferred_element_type=F32)
            prob = e.astype(F32) * jnp.concatenate([inv] * (mlen // LANES), axis=1)
            delta = jnp.sum(doh * oh, axis=1, keepdims=True)
            dohb = doh.astype(BF16)
            dprob = lax.dot_general(dohb, mv, _DIMS["nt"], preferred_element_type=F32)
            dsc = (prob * (dprob - delta)).astype(BF16)
            dp_ref[:, cols] = (jnp.dot(dsc, mk, preferred_element_type=F32) * scale).astype(BF16)
            dkv_ref[:, cols] += lax.dot_general(dsc, qh, _DIMS["tn"], preferred_element_type=F32) * scale
            dkv_ref[:, vcols] += lax.dot_general(prob.astype(BF16), dohb, _DIMS["tn"],
                                                 preferred_element_type=F32)

    blk = pl.BlockSpec((ts, MEM_WIDTH), lambda i: (i, 0))
    msp = pl.BlockSpec((ts, W_M), lambda i: (i, COL_M))
    kvsp = pl.BlockSpec((mlen, 2 * MEM_WIDTH), lambda i: (0, 0))
    return pl.pallas_call(
        body, grid=(s // ts,),
        in_specs=[msp, kvsp, blk, blk, pl.BlockSpec(memory_space=pl.ANY)],
        out_specs=[msp, kvsp],
        out_shape=[jax.ShapeDtypeStruct(dproj.shape, BF16), jax.ShapeDtypeStruct(mkv.shape, F32)],
        input_output_aliases={4: 0}, name="mem_bwd",
        compiler_params=_params())(proj, mkv, o_mem, dym, dproj)


def _forward_backward(x, mem, tgt, proj, w_conv, sink, g_mem, w_kv, w_up, w_out, g_post):
    s = x.shape[0]
    tab = _rope_tables(s)
    bias = _window_bias()

    ya = _conv_fwd(proj, w_conv)
    kpad, vpad = _rope_kv(proj, tab)
    o_attn, yb = _swa_fwd(proj, kpad, vpad, tab, bias, sink)
    mn = _rmsnorm_fwd(mem, g_mem, name="mem_norm")
    mkv = _matmul(mn, w_kv, mode="nn", out_dtype=F32, tm=256, tn=1024, tk=D_MODEL, name="mem_kv")
    o_mem, ym = _mem_fwd(proj, mkv)
    merged, d_out, dy, dg_post, loss = _mid_fwd(ya, yb, ym, proj, x, tgt, w_up, w_out, g_post)
    dproj, d_ya, d_yb, d_ym, dw_up, dw_out = _mid_bwd(d_out, merged, ya, yb, ym, proj, w_up, w_out)

    dproj, dw_conv = _conv_bwd(proj, w_conv, d_ya, dproj)
    dproj, dkpad, dvpad, dsink = _swa_bwd(proj, kpad, vpad, tab, bias, sink, o_attn, d_yb, dproj)
    dproj = _rope_kv_bwd(dkpad, dvpad, tab, dproj)
    dproj, d_mkv = _mem_bwd(proj, mkv, o_mem, d_ym, dproj)

    dw_kv = _matmul(mn, d_mkv, mode="tn", out_dtype=F32, tm=1024, tn=1024, tk=256, name="dw_kv")
    d_mn = _matmul(d_mkv, w_kv, mode="nt", out_dtype=F32, tm=256, tn=1024, tk=D_MODEL, name="d_mn")
    _, dg_mem = _rmsnorm_bwd(d_mn, mem, g_mem, d_mn, name="mem_norm_bwd")

    return dict(loss=loss, dproj=dproj, dy=dy, w_conv=dw_conv, sink=dsink, g_mem=dg_mem,
                w_kv=dw_kv, w_up=dw_up, w_out=dw_out, g_post=dg_post)


N_DEV = 8


def _position():
    return lax.axis_index("x"), lax.axis_index("y"), lax.axis_index("c")


def _other_chips(x, y):
    return (((1 - x, y), 2 * (1 - x) + y), ((x, 1 - y), 2 * x + (1 - y)), ((1 - x, 1 - y), 2 * (1 - x) + (1 - y)))


def _remote(src, dst, send_sems, recv_sems, k, device):
    return pltpu.make_async_remote_copy(src_ref=src, dst_ref=dst, send_sem=send_sems.at[k], recv_sem=recv_sems.at[k],
                                        device_id=device, device_id_type=MESH)


def _rows_half(ref, hf):
    rh = ref.shape[0] // 2
    return ref.at[pl.ds(pl.multiple_of(hf * rh, 8), rh)]


def _gather_weights(shards, small=None, relations=(0, 1, 2), into=None):
    n = len(shards)
    k = 0 if small is None else 1

    def peers(x, y):
        return [(r, chip, idx) for r, (chip, idx) in enumerate(_other_chips(x, y)) if r in relations]

    def ici(ins, outs, sems, a, r, chip, src_chip, c):
        return _remote(_rows_half(ins[a], c), _rows_half(outs[a].at[src_chip], c), sems[0], sems[1], 3 * a + r,
                       (*chip, c))

    def whole(ins, outs, sems, r, chip, src_chip, c):
        return _remote(ins[n], outs[n].at[src_chip], sems[0], sems[1], 3 * n + r, (*chip, c))

    def d2d(outs, sems, a, r, idx, hf, x, y, c):
        half = _rows_half(outs[a].at[idx], hf)
        return _remote(half, half, sems[2], sems[3], 3 * a + r, (x, y, 1 - c))

    def start(ins, outs, sems):
        x, y, c = _position()
        me = 2 * x + y
        for a in range(n):
            for r, chip, _ in peers(x, y):
                ici(ins, outs, sems, a, r, chip, me, c).start()
        for r, (chip, _) in enumerate(_other_chips(x, y)):
            if k:
                whole(ins, outs, sems, r, chip, me, c).start()

    def finish(ins, outs, sems):
        x, y, c = _position()
        me = 2 * x + y
        for a in range(n):
            for r, chip, idx in peers(x, y):
                ici(ins, outs, sems, a, r, chip, idx, c).wait_recv()
                d2d(outs, sems, a, r, idx, c, x, y, c).start()
        for a in range(n):
            for r, chip, idx in peers(x, y):
                d2d(outs, sems, a, r, idx, 1 - c, x, y, c).wait_recv()
        for r, (chip, idx) in enumerate(_other_chips(x, y)):
            if k:
                whole(ins, outs, sems, r, chip, idx, c).wait_recv()
                whole(ins, outs, sems, r, chip, me, c).wait_send()
        for a in range(n):
            for r, chip, idx in peers(x, y):
                ici(ins, outs, sems, a, r, chip, me, c).wait_send()
                d2d(outs, sems, a, r, idx, c, x, y, c).wait_send()

    operands = list(shards) + ([small] if k else [])
    shapes = [jax.ShapeDtypeStruct((N_CHIPS,) + s.shape, s.dtype) for s in operands]
    aliases = {}
    if into is not None:
        assert len(into) == len(operands)
        aliases = {len(operands) + a: a for a in range(len(into))}
        operands += list(into)
    return _Carry(operands, shapes,
                  [pltpu.SemaphoreType.DMA((3 * (n + k),)), pltpu.SemaphoreType.DMA((3 * (n + k),)),
                   pltpu.SemaphoreType.DMA((3 * n,)), pltpu.SemaphoreType.DMA((3 * n,))], start, finish, aliases)


def _run_carry(carry, name):
    _, results = _carried_call(lambda ins, outs, scr: None, carry, grid=(1,), in_specs=[], out_specs=[],
                               out_shape=[], scratch=[], operands=(), name=name)
    return results


def _pair_exchange(send):
    n = len(send)

    def copies(ins, outs, sems):
        x, y, c = _position()
        return [_remote(ins[a], outs[a], sems[0], sems[1], a, (x, y, 1 - c)) for a in range(n)]

    def start(ins, outs, sems):
        for cp in copies(ins, outs, sems):
            cp.start()

    def finish(ins, outs, sems):
        for cp in copies(ins, outs, sems):
            cp.wait()

    return _Carry(send, [jax.ShapeDtypeStruct(p.shape, p.dtype) for p in send],
                  [pltpu.SemaphoreType.DMA((n,)), pltpu.SemaphoreType.DMA((n,))], start, finish)


def _chip_exchange(sums):
    n = len(sums)

    def copies(ins, outs, sems):
        x, y, c = _position()
        return [_remote(ins[a].at[idx], outs[a].at[r], sems[0], sems[1], 3 * a + r, (*chip, c))
                for a in range(n) for r, (chip, idx) in enumerate(_other_chips(x, y))]

    def start(ins, outs, sems):
        for cp in copies(ins, outs, sems):
            cp.start()

    def finish(ins, outs, sems):
        for cp in copies(ins, outs, sems):
            cp.wait()

    return _Carry(sums, [jax.ShapeDtypeStruct((3,) + p.shape[1:], p.dtype) for p in sums],
                  [pltpu.SemaphoreType.DMA((3 * n,)), pltpu.SemaphoreType.DMA((3 * n,))], start, finish)


def _pair_share(pairs):
    n = len(pairs)

    def start(ins, outs, sems):
        x, y, c = _position()
        for a in range(n):
            _remote(outs[a].at[c], outs[a].at[c], sems[0], sems[1], a, (x, y, 1 - c)).start()

    def finish(ins, outs, sems):
        x, y, c = _position()
        for a in range(n):
            _remote(outs[a].at[1 - c], outs[a].at[1 - c], sems[0], sems[1], a, (x, y, 1 - c)).wait_recv()
        for a in range(n):
            _remote(outs[a].at[c], outs[a].at[c], sems[0], sems[1], a, (x, y, 1 - c)).wait_send()

    return _Carry(pairs, [jax.ShapeDtypeStruct(p.shape, p.dtype) for p in pairs],
                  [pltpu.SemaphoreType.DMA((n,)), pltpu.SemaphoreType.DMA((n,))], start, finish,
                  aliases={a: a for a in range(n)})


def _small_allreduce(pack, share):
    rows, width = pack.shape
    n_share = len(share.ins)

    def body(p_ref, *refs):
        share_in, o_ref, share_out = refs[:n_share], refs[n_share], refs[n_share + 1:2 * n_share + 1]
        buf, send_sems, recv_sems = refs[2 * n_share + 1:2 * n_share + 4]
        share_sems = refs[2 * n_share + 4:]
        share.start(share_in, share_out, share_sems)
        x, y, c = _position()
        me = 4 * x + 2 * y + c
        buf[me] = p_ref[...]
        peers = []
        for r in range(1, N_DEV):
            fx, fy, fc = (r >> 2) & 1, (r >> 1) & 1, r & 1
            px, py, pc = (1 - x if fx else x), (1 - y if fy else y), (1 - c if fc else c)
            peers.append(((px, py, pc), 4 * px + 2 * py + pc))
        sends = [_remote(p_ref, buf.at[me], send_sems, recv_sems, r, dev) for r, (dev, _) in enumerate(peers)]
        for cp in sends:
            cp.start()
        for r, (dev, idx) in enumerate(peers):
            _remote(p_ref, buf.at[idx], send_sems, recv_sems, r, dev).wait_recv()
        for cp in sends:
            cp.wait_send()
        acc = buf[0]
        for k in range(1, N_DEV):
            acc = acc + buf[k]
        o_ref[...] = acc
        share.finish(share_in, share_out, share_sems)

    vm = pl.BlockSpec(memory_space=pltpu.VMEM)
    red, *shared = pl.pallas_call(
        body, in_specs=[vm] + [_HBM] * n_share, out_specs=[vm] + [_HBM] * n_share,
        out_shape=[jax.ShapeDtypeStruct(pack.shape, F32)] + share.out_shapes,
        scratch_shapes=[pltpu.VMEM((N_DEV, rows, width), F32), pltpu.SemaphoreType.DMA((N_DEV - 1,)),
                        pltpu.SemaphoreType.DMA((N_DEV - 1,))] + share.sems,
        input_output_aliases={1 + i: 1 + o for i, o in share.aliases.items()},
        name="small_allreduce")(pack, *share.ins)
    return red, shared


ROW_TILE_MAX = 512
SUM_TILE_MAX = 2048
BF16_SUBLANES = 16


def _row_tile(rows, most=ROW_TILE_MAX):
    if rows <= most:
        return rows
    return max(t for t in range(BF16_SUBLANES, most + 1, BF16_SUBLANES) if rows % t == 0)


def _pair_add(keep, recv, name):
    nj, rh, cols = keep.shape
    tr = _row_tile(rh, SUM_TILE_MAX)

    def body(k_ref, r_ref, o_ref):
        o_ref[...] = (k_ref[...].astype(F32) + r_ref[...].astype(F32)).astype(BF16)

    blk = pl.BlockSpec((None, tr, cols), lambda j, i: (j, i, 0))
    return pl.pallas_call(body, grid=(nj, rh // tr), in_specs=[blk, blk], out_specs=blk,
                          out_shape=jax.ShapeDtypeStruct(keep.shape, BF16), name=name,
                          compiler_params=_params())(keep, recv)


def _chip_add(sums, recv, where, name):
    _, rh, cols = sums.shape
    tr = _row_tile(rh, SUM_TILE_MAX)

    def body(w_ref, s_ref, r_ref, o_ref):
        o_ref[...] = ((s_ref[...].astype(F32) + r_ref[0].astype(F32)) + r_ref[1].astype(F32)) + r_ref[2].astype(F32)

    grid_spec = pltpu.PrefetchScalarGridSpec(
        num_scalar_prefetch=1, grid=(rh // tr,),
        in_specs=[pl.BlockSpec((None, tr, cols), lambda i, w_ref: (w_ref[0], i, 0)),
                  pl.BlockSpec((3, tr, cols), lambda i, w_ref: (0, i, 0))],
        out_specs=pl.BlockSpec((None, tr, cols), lambda i, w_ref: (w_ref[1], i, 0)))
    return pl.pallas_call(body, grid_spec=grid_spec, out_shape=jax.ShapeDtypeStruct((2, rh, cols), F32),
                          name=name, compiler_params=_params())(where, sums, recv)


def _adamw(w, g, m, v, name):
    rows, cols = w.shape
    tr = _row_tile(rows)
    assert rows % tr == 0

    def body(w_ref, g_ref, m_ref, v_ref, d_ref, mo_ref, vo_ref):
        gv = g_ref[...]
        m_new = ADAM_B1 * m_ref[...] + (1.0 - ADAM_B1) * gv
        v_new = ADAM_B2 * v_ref[...] + (1.0 - ADAM_B2) * jnp.square(gv)
        m_hat = m_new / (1.0 - ADAM_B1 ** ADAM_STEP)
        v_hat = v_new / (1.0 - ADAM_B2 ** ADAM_STEP)
        d_ref[...] = -ADAM_LR * (m_hat / (jnp.sqrt(v_hat) + ADAM_EPS) + ADAM_WD * w_ref[...])
        mo_ref[...] = m_new
        vo_ref[...] = v_new

    blk = pl.BlockSpec((tr, cols), lambda i: (i, 0))
    shp = jax.ShapeDtypeStruct((rows, cols), F32)
    return pl.pallas_call(body, grid=(rows // tr,), in_specs=[blk] * 4, out_specs=[blk] * 3,
                          out_shape=[shp] * 3, name=name, compiler_params=_params())(w, g, m, v)


def _adamw_halves(w, g2, m, v, name):
    rows, cols = w.shape
    half = cols // 2
    tr = _row_tile(rows)

    def body(w_ref, g_ref, m_ref, v_ref, go_ref, d_ref, mo_ref, vo_ref):
        gv = g_ref[...]
        go_ref[...] = gv
        m_new = ADAM_B1 * m_ref[...] + (1.0 - ADAM_B1) * gv
        v_new = ADAM_B2 * v_ref[...] + (1.0 - ADAM_B2) * jnp.square(gv)
        m_hat = m_new / (1.0 - ADAM_B1 ** ADAM_STEP)
        v_hat = v_new / (1.0 - ADAM_B2 ** ADAM_STEP)
        d_ref[...] = -ADAM_LR * (m_hat / (jnp.sqrt(v_hat) + ADAM_EPS) + ADAM_WD * w_ref[...])
        mo_ref[...] = m_new
        vo_ref[...] = v_new

    blk = pl.BlockSpec((tr, half), lambda hf, i: (i, hf))
    gsp = pl.BlockSpec((None, tr, half), lambda hf, i: (hf, i, 0))
    shp = jax.ShapeDtypeStruct((rows, cols), F32)
    return pl.pallas_call(body, grid=(2, rows // tr), in_specs=[blk, gsp, blk, blk], out_specs=[blk] * 4,
                          out_shape=[shp] * 4, name=name, compiler_params=_params())(w, g2, m, v)


SHARD_W = IN_WIDTH // N_CHIPS


def _half_major(a):
    r, c = a.shape
    return a.reshape(N_CHIPS, 2, r // N_CHIPS // 2, c).transpose(1, 0, 2, 3)


def kernel(x, mem, g_pre, w_in, w_conv, attn_sink, g_mem, w_mem_kv, w_up_a, w_up_b, w_up_m, w_out, g_post, loss_target, m_g_pre, m_w_in, m_w_conv, m_attn_sink, m_g_mem, m_w_mem_kv, m_w_up_a, m_w_up_b, m_w_up_m, m_w_out, m_g_post, v_g_pre, v_w_in, v_w_conv, v_attn_sink, v_g_mem, v_w_mem_kv, v_w_up_a, v_w_up_b, v_w_up_m, v_w_out, v_g_post):
    xi, yi, ci = _position()
    chip = 2 * xi + yi
    where = jnp.stack([chip, ci, N_CHIPS - 1 - chip]).astype(jnp.int32)

    own = [w_in[0].T.astype(BF16), w_mem_kv[0].astype(BF16),
           jnp.concatenate([w_up_a[0], w_up_b[0], w_up_m[0]], axis=0).astype(BF16), w_out[0].astype(BF16)]
    own_conv = jnp.pad(w_conv[0], ((0, 5), (0, 0)))

    def pieces(mine, got):
        got = lax.dynamic_update_slice_in_dim(got, mine[None], chip, axis=0)
        return [got[j] for j in range(N_CHIPS)]

    diag = N_CHIPS - 1 - chip
    diag_blocks = SHARD_BLOCKS + 1
    (h, h_t), (got_near, got_conv) = _rmsnorm_fwd(x[0], g_pre, name="pre_norm", transposed=True,
                                                  carry=_gather_weights(own[:1], own_conv, relations=(0, 1)))
    w_near = lax.dynamic_update_slice_in_dim(got_near, own[0][None], chip, axis=0).reshape(IN_WIDTH, D_MODEL)
    proj, (got_far, *got_rest) = _proj(
        h, w_near, n_blocks=N_IN_BLOCKS - diag_blocks, where=where, name="proj_near",
        block_of=lambda i, w: i + diag_blocks * (i >= SHARD_BLOCKS * w[2]).astype(jnp.int32),
        carry=_join(_gather_weights(own[:1], relations=(2,)), _gather_weights(own[1:], relations=(0, 1))))
    far = lax.dynamic_index_in_dim(got_far, diag, 0, keepdims=False)
    proj, gathered = _proj_far(h, w_near, far, where, into=proj,
                               carry=_gather_weights(own[1:], relations=(2,), into=got_rest))
    w_kv_full = jnp.concatenate(pieces(own[1], gathered[0]), axis=0)
    up_pieces = pieces(own[2], gathered[1])
    w_up_full = jnp.stack([jnp.concatenate([p[k * A_WIDTH:(k + 1) * A_WIDTH] for p in up_pieces], axis=1)
                           for k in range(3)])
    w_out_full = jnp.concatenate(pieces(own[3], gathered[2]), axis=0)
    w_conv_full = jnp.concatenate([p[:3] for p in pieces(own_conv, got_conv)], axis=1)

    g = _forward_backward(x[0], mem[0], loss_target[0], proj, w_conv_full, attn_sink, g_mem, w_kv_full, w_up_full,
                          w_out_full, g_post)

    half_rows = D_MODEL // 2
    up_parts = (g["w_up"].reshape(3, A_WIDTH, N_CHIPS, D_MODEL // N_CHIPS).transpose(2, 0, 1, 3)
                .reshape(N_CHIPS, 2, 3 * A_WIDTH // 2, D_MODEL // N_CHIPS).transpose(1, 0, 2, 3)).astype(BF16)
    small_parts = [_half_major(g["w_kv"]).astype(BF16), up_parts, _half_major(g["w_out"]).astype(BF16)]

    def dw_in_half(half_of, name, carry):
        dw, carried = _dw_in_t(g["dproj"], h_t, half_of=half_of, where=where, name=name, carry=carry)
        return dw.reshape(N_CHIPS, SHARD_W, half_rows), carried

    def pick(parts, hf):
        return [lax.dynamic_index_in_dim(p, hf, 0, keepdims=False) for p in parts]

    small_names = ["w_kv", "w_up", "w_out"]
    recv_small = _run_carry(_pair_exchange(pick(small_parts, 1 - ci)), "pair_exchange_small")
    sums_small = [_pair_add(k, r, "pair_add_" + nm)
                  for k, r, nm in zip(pick(small_parts, ci), recv_small, small_names)]
    dw_send, recv3_small = dw_in_half(lambda w: 1 - w[1], "dw_in_send", _chip_exchange(sums_small))
    dw_keep, (recv_in,) = dw_in_half(lambda w: w[1], "dw_in_keep", _pair_exchange([dw_send]))
    sum_in = _pair_add(dw_keep, recv_in, "pair_add_w_in")
    d_h, (recv3_in,) = _d_h(g["dproj"], w_near, far, where, carry=_chip_exchange([sum_in]))
    pairs = [_chip_add(s, r, where, "chip_add_" + nm)
             for s, r, nm in zip([sum_in] + sums_small, [recv3_in] + recv3_small, ["w_in"] + small_names)]
    grad_x, dg_pre = _rmsnorm_bwd(d_h, x[0], g_pre, g["dy"], name="pre_norm_bwd")

    zeros512 = jnp.zeros((1, D_MODEL - A_WIDTH), F32)
    conv_rows = [jnp.concatenate([g["w_conv"][k:k + 1], zeros512], axis=1) for k in range(3)]
    sink_row = jnp.pad(g["sink"][:, 0].reshape(1, N_Q_HEADS), ((0, 0), (0, D_MODEL - N_Q_HEADS)))
    loss_row = jnp.pad(g["loss"], ((0, 0), (0, D_MODEL - LANES)))
    pack = jnp.concatenate([dg_pre, g["g_mem"], g["g_post"]] + conv_rows + [sink_row, loss_row], axis=0)
    red, full = _small_allreduce(pack, _pair_share(pairs))
    loss = red[7, 0]
    small_grads = dict(
        g_pre=red[0:1], g_mem=red[1:2], g_post=red[2:3], attn_sink=red[6:7, :N_Q_HEADS],
        w_conv=lax.dynamic_slice(red[3:6, :A_WIDTH], (0, chip * LANES), (3, LANES)))

    gw_up = full[2].reshape(3, A_WIDTH, D_MODEL // N_CHIPS)
    grads = dict(small_grads, w_mem_kv=full[1].reshape(D_MODEL // N_CHIPS, 2 * MEM_WIDTH),
                 w_up_a=gw_up[0], w_up_b=gw_up[1], w_up_m=gw_up[2],
                 w_out=full[3].reshape(D_MODEL // N_CHIPS, D_MODEL))

    weights = dict(g_pre=g_pre, w_in=w_in, w_conv=w_conv, attn_sink=attn_sink, g_mem=g_mem, w_mem_kv=w_mem_kv,
                   w_up_a=w_up_a, w_up_b=w_up_b, w_up_m=w_up_m, w_out=w_out, g_post=g_post)
    m_in = dict(g_pre=m_g_pre, w_in=m_w_in, w_conv=m_w_conv, attn_sink=m_attn_sink, g_mem=m_g_mem,
                w_mem_kv=m_w_mem_kv, w_up_a=m_w_up_a, w_up_b=m_w_up_b, w_up_m=m_w_up_m, w_out=m_w_out,
                g_post=m_g_post)
    v_in = dict(g_pre=v_g_pre, w_in=v_w_in, w_conv=v_w_conv, attn_sink=v_attn_sink, g_mem=v_g_mem,
                w_mem_kv=v_w_mem_kv, w_up_a=v_w_up_a, w_up_b=v_w_up_b, w_up_m=v_w_up_m, w_out=v_w_out,
                g_post=v_g_post)
    out_g, out_d, out_m, out_v = [], [], [], []
    for nm in ("g_pre", "w_in", "w_conv", "attn_sink", "g_mem", "w_mem_kv", "w_up_a", "w_up_b", "w_up_m", "w_out",
               "g_post"):
        shape = weights[nm].shape
        if nm == "w_in":
            results = _adamw_halves(w_in[0].T, full[0], m_w_in[0].T, v_w_in[0].T, "adamw_w_in")
            for out, t in zip((out_g, out_d, out_m, out_v), results):
                out.append(t.T.reshape(shape))
            continue
        two_d = shape[-2:]
        gr = grads[nm].reshape(two_d)
        d, m_new, v_new = _adamw(weights[nm].reshape(two_d), gr, m_in[nm].reshape(two_d), v_in[nm].reshape(two_d),
                                 "adamw_" + nm)
        out_g.append(gr.reshape(shape))
        out_d.append(d.reshape(shape))
        out_m.append(m_new.reshape(shape))
        out_v.append(v_new.reshape(shape))
    return (loss, grad_x.reshape(x.shape), *out_g, *out_d, *out_m, *out_v)
```

```python
import functools

import jax
import jax.numpy as jnp
from jax import lax
from jax.experimental import pallas as pl
from jax.experimental.pallas import tpu as pltpu

F32 = jnp.float32
BF16 = jnp.bfloat16
MESH = pl.DeviceIdType.MESH

D_MODEL = 1024
EPS = 1e-6
A_WIDTH = 512
HEAD_DIM = 64
N_Q_HEADS = 8
WINDOW_BLOCK = 128
KV_PAD = 512
ROPE_THETA = 500000.0
ROT_DIM = 16
MEM_HEADS = 4
MEM_HEAD_DIM = 128
MEM_WIDTH = 512
IN_WIDTH = 7424
N_CHIPS = 4
LANES = 128
HALF_LANES = 64

PERM_SEGS = ((0, 2560), (2816, 3328), (4352, 7424), (3328, 4352), (2560, 2816))
UNPERM_SEGS = ((0, 2560), (7168, 7424), (2560, 3072), (6144, 7168), (3072, 6144))
COL_A, W_A = 0, 2048
COL_B, W_B = 2, 1024
COL_G, W_G = 1, 3072
COL_M, W_M = 6, 1024
COL_KV, W_KV = 28, 256

ADAM_LR = 0.001
ADAM_B1 = 0.9
ADAM_B2 = 0.999
ADAM_EPS = 1e-08
ADAM_WD = 0.01
ADAM_STEP = 10

VMEM_LIGHT_BYTES = 48 * 1024 * 1024
VMEM_HEAVY_BYTES = 48 * 1024 * 1024


_HBM = pl.BlockSpec(memory_space=pltpu.HBM)


def _params(heavy=False):
    return pltpu.CompilerParams(vmem_limit_bytes=VMEM_HEAVY_BYTES if heavy else VMEM_LIGHT_BYTES)


def _sigmoid(v):
    return jax.nn.sigmoid(v)


_DIMS = {"nn": (((1,), (0,)), ((), ())), "nt": (((1,), (1,)), ((), ())), "tn": (((0,), (0,)), ((), ()))}


class _Carry:
    def __init__(self, ins, out_shapes, sems, start, finish, aliases=None):
        self.ins, self.out_shapes, self.sems = list(ins), list(out_shapes), list(sems)
        self.start, self.finish, self.aliases = start, finish, dict(aliases or {})


def _join(*carries):
    def split(seq, counts):
        pos, parts = 0, []
        for n in counts:
            parts.append(seq[pos:pos + n])
            pos += n
        return parts

    n_in = [len(c.ins) for c in carries]
    n_out = [len(c.out_shapes) for c in carries]
    n_sem = [len(c.sems) for c in carries]

    def run(which):
        def go(ins, outs, sems):
            for c, i, o, sm in zip(carries, split(ins, n_in), split(outs, n_out), split(sems, n_sem)):
                getattr(c, which)(i, o, sm)
        return go

    aliases = {}
    for k, c in enumerate(carries):
        aliases.update({sum(n_in[:k]) + i: sum(n_out[:k]) + o for i, o in c.aliases.items()})
    return _Carry([a for c in carries for a in c.ins], [sh for c in carries for sh in c.out_shapes],
                  [sm for c in carries for sm in c.sems], run("start"), run("finish"), aliases)


def _carried_call(body, carry, *, grid, in_specs, out_specs, out_shape, scratch, operands, name, prefetch=None,
                  aliases=None, heavy=False):
    n_in, n_out, n_scr = len(in_specs), len(out_specs), len(scratch)
    c_in = len(carry.ins) if carry else 0
    c_out = len(carry.out_shapes) if carry else 0
    n_pre = 0 if prefetch is None else 1
    steps = 1
    for g in grid:
        steps *= g

    def wrapped(*refs):
        refs = refs[n_pre:]
        ins, cins = refs[:n_in], refs[n_in:n_in + c_in]
        outs = refs[n_in + c_in:n_in + c_in + n_out]
        couts = refs[n_in + c_in + n_out:n_in + c_in + n_out + c_out]
        rest = refs[n_in + c_in + n_out + c_out:]
        scr, sems = rest[:n_scr], rest[n_scr:]
        if carry:
            step = pl.program_id(0)
            for ax in range(1, len(grid)):
                step = step * grid[ax] + pl.program_id(ax)

            @pl.when(step == 0)
            def _():
                carry.start(cins, couts, sems)

        body(ins, outs, scr)
        if carry:
            @pl.when(step == steps - 1)
            def _():
                carry.finish(cins, couts, sems)

    all_aliases = {n_pre + i: o for i, o in (aliases or {}).items()}
    if carry:
        all_aliases.update({n_pre + n_in + i: n_out + o for i, o in carry.aliases.items()})
    all_in = list(in_specs) + [_HBM] * c_in
    all_out = list(out_specs) + [_HBM] * c_out
    all_scratch = list(scratch) + (carry.sems if carry else [])
    if n_pre:
        spec = dict(grid_spec=pltpu.PrefetchScalarGridSpec(num_scalar_prefetch=1, grid=grid, in_specs=all_in,
                                                           out_specs=all_out, scratch_shapes=all_scratch))
        pre = (prefetch,)
    else:
        spec = dict(grid=grid, in_specs=all_in, out_specs=all_out, scratch_shapes=all_scratch)
        pre = ()
    results = pl.pallas_call(
        wrapped, out_shape=list(out_shape) + (carry.out_shapes if carry else []), input_output_aliases=all_aliases,
        name=name, compiler_params=_params(heavy), **spec)(*pre, *operands, *(carry.ins if carry else []))
    return list(results[:n_out]), list(results[n_out:])


def _matmul(a, b, *, mode, out_dtype, tm, tn, tk, name, j_outer=False, carry=None):
    if mode == "nn":
        (m, k), (_, n) = a.shape, b.shape
    elif mode == "nt":
        (m, k), (n, _) = a.shape, b.shape
    else:
        (k, m), (_, n) = a.shape, b.shape
    tm, tn, tk = min(tm, m), min(tn, n), min(tk, k)
    assert m % tm == 0 and n % tn == 0 and k % tk == 0
    ni, nj, nk = m // tm, n // tn, k // tk
    dims = _DIMS[mode]

    def ij(g0, g1):
        return (g1, g0) if j_outer else (g0, g1)

    if mode == "nn":
        a_spec = pl.BlockSpec((tm, tk), lambda g0, g1, kk: (ij(g0, g1)[0], kk))
        b_spec = pl.BlockSpec((tk, tn), lambda g0, g1, kk: (kk, ij(g0, g1)[1]))
    elif mode == "nt":
        a_spec = pl.BlockSpec((tm, tk), lambda g0, g1, kk: (ij(g0, g1)[0], kk))
        b_spec = pl.BlockSpec((tn, tk), lambda g0, g1, kk: (ij(g0, g1)[1], kk))
    else:
        a_spec = pl.BlockSpec((tk, tm), lambda g0, g1, kk: (kk, ij(g0, g1)[0]))
        b_spec = pl.BlockSpec((tk, tn), lambda g0, g1, kk: (kk, ij(g0, g1)[1]))
    o_spec = pl.BlockSpec((tm, tn), lambda g0, g1, kk: ij(g0, g1))

    def part(a_ref, b_ref):
        return lax.dot_general(a_ref[...].astype(BF16), b_ref[...].astype(BF16), dims,
                               preferred_element_type=F32)

    if nk == 1:
        def body(ins, outs, scr):
            outs[0][...] = part(*ins).astype(out_dtype)
        scratch = []
    else:
        def body(ins, outs, scr):
            kk = pl.program_id(2)
            acc_ref = scr[0]

            @pl.when(kk == 0)
            def _():
                acc_ref[...] = part(*ins)

            @pl.when(kk > 0)
            def _():
                acc_ref[...] += part(*ins)

            @pl.when(kk == nk - 1)
            def _():
                outs[0][...] = acc_ref[...].astype(out_dtype)
        scratch = [pltpu.VMEM((tm, tn), F32)]

    grid = (nj, ni, nk) if j_outer else (ni, nj, nk)
    (out,), carried = _carried_call(
        body, carry, grid=grid, in_specs=[a_spec, b_spec], out_specs=[o_spec],
        out_shape=[jax.ShapeDtypeStruct((m, n), out_dtype)], scratch=scratch, operands=(a, b), name=name)
    return (out, carried) if carry else out


IN_BLOCK = 256
N_IN_BLOCKS = IN_WIDTH // IN_BLOCK
SHARD_BLOCKS = (IN_WIDTH // N_CHIPS) // IN_BLOCK
BLOCK_RUNS = tuple((a // IN_BLOCK, sum(d - c for c, d in PERM_SEGS[:k]) // IN_BLOCK, (b - a) // IN_BLOCK)
                   for k, (a, b) in enumerate(PERM_SEGS))


def _perm_block(r):
    p = r
    for ref0, perm0, n in BLOCK_RUNS:
        p = jnp.where((r >= ref0) & (r < ref0 + n), r - ref0 + perm0, p)
    return p


def _proj(h, w_t, *, n_blocks, block_of, where, name, carry=None):
    s, d = h.shape

    def body(ins, outs, scr):
        outs[0][...] = lax.dot_general(ins[0][...], ins[1][...], _DIMS["nt"], preferred_element_type=F32)

    (proj,), carried = _carried_call(
        body, carry, grid=(n_blocks,),
        in_specs=[pl.BlockSpec((s, d), lambda i, w: (0, 0)), pl.BlockSpec((IN_BLOCK, d), lambda i, w: (block_of(i, w), 0))],
        out_specs=[pl.BlockSpec((s, IN_BLOCK), lambda i, w: (0, _perm_block(block_of(i, w))))],
        out_shape=[jax.ShapeDtypeStruct((s, IN_WIDTH), F32)], scratch=[], operands=(h, w_t), name=name,
        prefetch=where)
    return (proj, carried) if carry else proj


def _proj_far(h, w_near, far, where, *, into, carry=None):
    s, d = h.shape
    n_blocks = SHARD_BLOCKS + 1
    lead = IN_WIDTH // N_CHIPS - SHARD_BLOCKS * IN_BLOCK

    def body(ins, outs, scr):
        where_ref, h_ref, w_hbm, far_hbm, _ = ins
        win, sem = scr
        i = pl.program_id(0)

        @pl.when(i == 0)
        def _():
            dg = where_ref[2]
            rows = pl.ds(pl.multiple_of(dg * (SHARD_BLOCKS * IN_BLOCK), IN_BLOCK), n_blocks * IN_BLOCK)
            window = pltpu.make_async_copy(w_hbm.at[rows], win, sem)
            window.start()
            window.wait()
            shard = pltpu.make_async_copy(far_hbm, win.at[pl.ds(pl.multiple_of(dg * lead, BF16_SUBLANES), SHARD_W)], sem)
            shard.start()
            shard.wait()

        blk = win[pl.ds(pl.multiple_of(i * IN_BLOCK, IN_BLOCK), IN_BLOCK), :]
        outs[0][...] = lax.dot_general(h_ref[...], blk, _DIMS["nt"], preferred_element_type=F32)

    anysp = pl.BlockSpec(memory_space=pl.ANY)
    (proj,), carried = _carried_call(
        body, carry, grid=(n_blocks,),
        in_specs=[pl.BlockSpec(memory_space=pltpu.SMEM), pl.BlockSpec((s, d), lambda i, w: (0, 0)), anysp, anysp, anysp],
        out_specs=[pl.BlockSpec((s, IN_BLOCK), lambda i, w: (0, _perm_block(i + SHARD_BLOCKS * w[2])))],
        out_shape=[jax.ShapeDtypeStruct((s, IN_WIDTH), F32)],
        scratch=[pltpu.VMEM((n_blocks * IN_BLOCK, d), BF16), pltpu.SemaphoreType.DMA],
        operands=(where, h, w_near, far, into), name="proj_far", prefetch=where, aliases={4: 0})
    return (proj, carried) if carry else proj


def _dw_in_t(dproj, h_t, *, half_of, where, name, carry=None):
    d, s = h_t.shape
    c = d // 2

    def body(ins, outs, scr):
        outs[0][...] = lax.dot_general(ins[1][...], ins[0][...], _DIMS["nn"], preferred_element_type=F32).T.astype(BF16)

    (dw,), carried = _carried_call(
        body, carry, grid=(N_IN_BLOCKS,),
        in_specs=[pl.BlockSpec((s, IN_BLOCK), lambda r, w: (0, _perm_block(r))),
                  pl.BlockSpec((c, s), lambda r, w: (half_of(w), 0))],
        out_specs=[pl.BlockSpec((IN_BLOCK, c), lambda r, w: (r, 0))],
        out_shape=[jax.ShapeDtypeStruct((IN_WIDTH, c), BF16)], scratch=[], operands=(dproj, h_t), name=name,
        prefetch=where)
    return (dw, carried) if carry else dw


def _d_h(dproj, w_near, far, where, *, carry=None):
    s = dproj.shape[0]
    d = w_near.shape[1]
    tm = min(s, 256)

    def body(ins, outs, scr):
        where_ref, a_ref, w_hbm, far_hbm = ins
        w_ref, sem = scr

        @pl.when(pl.program_id(0) == 0)
        def _():
            whole = pltpu.make_async_copy(w_hbm, w_ref, sem)
            whole.start()
            whole.wait()
            rows = pl.ds(pl.multiple_of(where_ref[2] * SHARD_W, BF16_SUBLANES), SHARD_W)
            part = pltpu.make_async_copy(far_hbm, w_ref.at[rows], sem)
            part.start()
            part.wait()

        acc = None
        for ref0, perm0, n in BLOCK_RUNS:
            term = jnp.dot(a_ref[:, perm0 * IN_BLOCK:(perm0 + n) * IN_BLOCK],
                           w_ref[ref0 * IN_BLOCK:(ref0 + n) * IN_BLOCK, :], preferred_element_type=F32)
            acc = term if acc is None else acc + term
        outs[0][...] = acc

    anysp = pl.BlockSpec(memory_space=pl.ANY)
    (dh,), carried = _carried_call(
        body, carry, grid=(s // tm,),
        in_specs=[pl.BlockSpec(memory_space=pltpu.SMEM), pl.BlockSpec((tm, IN_WIDTH), lambda i: (i, 0)), anysp, anysp],
        out_specs=[pl.BlockSpec((tm, d), lambda i: (i, 0))],
        out_shape=[jax.ShapeDtypeStruct((s, d), F32)],
        scratch=[pltpu.VMEM((IN_WIDTH, d), BF16), pltpu.SemaphoreType.DMA],
        operands=(where, dproj, w_near, far), name="d_h", heavy=True)
    return (dh, carried) if carry else dh


def _rmsnorm_fwd(x, g, *, name, transposed=False, carry=None):
    s, d = x.shape
    ts = min(512, s)

    def body(ins, outs, scr):
        xv = ins[0][...]
        r = lax.rsqrt(jnp.mean(xv * xv, axis=-1, keepdims=True) + EPS)
        hv = (xv * r) * ins[1][...]
        outs[0][...] = hv.astype(BF16)
        if transposed:
            outs[1][...] = hv.T.astype(BF16)

    out_specs = [pl.BlockSpec((ts, d), lambda i: (i, 0))]
    out_shape = [jax.ShapeDtypeStruct((s, d), BF16)]
    if transposed:
        out_specs.append(pl.BlockSpec((d, ts), lambda i: (0, i)))
        out_shape.append(jax.ShapeDtypeStruct((d, s), BF16))
    outs, carried = _carried_call(
        body, carry, grid=(s // ts,),
        in_specs=[pl.BlockSpec((ts, d), lambda i: (i, 0)), pl.BlockSpec((1, d), lambda i: (0, 0))],
        out_specs=out_specs, out_shape=out_shape, scratch=[], operands=(x, g), name=name)
    result = tuple(outs) if transposed else outs[0]
    return (result, carried) if carry else result


def _rmsnorm_bwd(dh, x, g, res, *, name, carry=None):
    s, d = x.shape
    ts = min(256, s)

    def body(ins, outs, scr):
        dh_ref, x_ref, g_ref, res_ref = ins
        dx_ref, dg_ref = outs
        xv = x_ref[...]
        r = lax.rsqrt(jnp.mean(xv * xv, axis=-1, keepdims=True) + EPS)
        xh = xv * r
        dhv = dh_ref[...]
        part = jnp.sum(dhv * xh, axis=0, keepdims=True)

        @pl.when(pl.program_id(0) == 0)
        def _():
            dg_ref[...] = part

        @pl.when(pl.program_id(0) > 0)
        def _():
            dg_ref[...] += part

        dxh = dhv * g_ref[...]
        dx_ref[...] = res_ref[...] + r * (dxh - xh * jnp.mean(dxh * xh, axis=-1, keepdims=True))

    row = pl.BlockSpec((ts, d), lambda i: (i, 0))
    vec = pl.BlockSpec((1, d), lambda i: (0, 0))
    outs, carried = _carried_call(
        body, carry, grid=(s // ts,), in_specs=[row, row, vec, row], out_specs=[row, vec],
        out_shape=[jax.ShapeDtypeStruct((s, d), F32), jax.ShapeDtypeStruct((1, d), F32)],
        scratch=[], operands=(dh, x, g, res), name=name)
    return (*outs, carried) if carry else tuple(outs)


MID_TILE = 256


def _gated_branches(y_refs, wup_ref, gl):
    d = D_MODEL
    us = [jnp.dot(y_refs[k][...], wup_ref[k], preferred_element_type=F32) for k in range(3)]
    sg = [_sigmoid(gl[:, k * d:(k + 1) * d]) for k in range(3)]
    return us, sg


def _mid_fwd(ya, yb, ym, proj, x, tgt, w_up, w_out, g_post):
    s, d = x.shape
    ts = MID_TILE

    def body(ya_ref, yb_ref, ym_ref, g_ref, x_ref, t_ref, wup_ref, wout_ref, gp_ref,
             m_ref, do_ref, dy_ref, dg_ref, loss_ref):
        us, sg = _gated_branches((ya_ref, yb_ref, ym_ref), wup_ref, g_ref[...])
        merged = (sg[0] * us[0] + sg[1] * us[1] + sg[2] * us[2]).astype(BF16)
        m_ref[...] = merged
        ov = jnp.dot(merged, wout_ref[...], preferred_element_type=F32)
        r = lax.rsqrt(jnp.mean(ov * ov, axis=-1, keepdims=True) + EPS)
        nh = ov * r
        gv = gp_ref[...]
        e = (x_ref[...] + nh * gv) - t_ref[...]
        lpart = 0.5 * jnp.sum(jnp.mean(e * e, axis=-1, keepdims=True), axis=0, keepdims=True)
        dy = e * (1.0 / d)
        dgp = jnp.sum(dy * nh, axis=0, keepdims=True)

        @pl.when(pl.program_id(0) == 0)
        def _():
            dg_ref[...] = dgp
            loss_ref[...] = jnp.broadcast_to(lpart, loss_ref.shape)

        @pl.when(pl.program_id(0) > 0)
        def _():
            dg_ref[...] += dgp
            loss_ref[...] += jnp.broadcast_to(lpart, loss_ref.shape)

        dn = dy * gv
        dy_ref[...] = dy
        do_ref[...] = (r * (dn - nh * jnp.mean(dn * nh, axis=-1, keepdims=True))).astype(BF16)

    row = pl.BlockSpec((ts, d), lambda i: (i, 0))
    ysp = pl.BlockSpec((ts, A_WIDTH), lambda i: (i, 0))
    vec = pl.BlockSpec((1, d), lambda i: (0, 0))
    return pl.pallas_call(
        body, grid=(s // ts,),
        in_specs=[ysp, ysp, ysp, pl.BlockSpec((ts, W_G), lambda i: (i, COL_G)), row, row,
                  pl.BlockSpec((3, A_WIDTH, d), lambda i: (0, 0, 0)), pl.BlockSpec((d, d), lambda i: (0, 0)), vec],
        out_specs=[row, row, row, vec, pl.BlockSpec((1, LANES), lambda i: (0, 0))],
        out_shape=[jax.ShapeDtypeStruct((s, d), BF16), jax.ShapeDtypeStruct((s, d), BF16),
                   jax.ShapeDtypeStruct((s, d), F32), jax.ShapeDtypeStruct((1, d), F32),
                   jax.ShapeDtypeStruct((1, LANES), F32)],
        name="mid_fwd", compiler_params=_params(heavy=True))(ya, yb, ym, proj, x, tgt, w_up, w_out, g_post)


def _mid_bwd(d_out, merged, ya, yb, ym, proj, w_up, w_out):
    s, d = merged.shape
    ts = MID_TILE
    last = s // ts - 1

    def body(do_ref, m_ref, ya_ref, yb_ref, ym_ref, g_ref, wup_ref, wout_ref,
             dp_ref, dya_ref, dyb_ref, dym_ref, dwup_hbm, dwout_hbm, dwup_acc, dwout_acc):
        i = pl.program_id(0)

        @pl.when(i == 0)
        def _():
            dwup_acc[...] = jnp.zeros_like(dwup_acc)
            dwout_acc[...] = jnp.zeros_like(dwout_acc)

        y_refs = (ya_ref, yb_ref, ym_ref)
        us, sg = _gated_branches(y_refs, wup_ref, g_ref[...])
        dov = do_ref[...]
        dwout_acc[...] += lax.dot_general(m_ref[...], dov, _DIMS["tn"], preferred_element_type=F32)
        dm = lax.dot_general(dov, wout_ref[...], _DIMS["nt"], preferred_element_type=F32)
        for k, dy_ref in enumerate((dya_ref, dyb_ref, dym_ref)):
            dp_ref[:, k * d:(k + 1) * d] = ((dm * us[k]) * (sg[k] * (1.0 - sg[k]))).astype(BF16)
            du = (sg[k] * dm).astype(BF16)
            dy_ref[...] = lax.dot_general(du, wup_ref[k], _DIMS["nt"], preferred_element_type=F32)
            dwup_acc[k] += lax.dot_general(y_refs[k][...], du, _DIMS["tn"], preferred_element_type=F32)

        @pl.when(i == last)
        def _():
            pltpu.sync_copy(dwup_acc, dwup_hbm)
            pltpu.sync_copy(dwout_acc, dwout_hbm)

    row = pl.BlockSpec((ts, d), lambda i: (i, 0))
    ysp = pl.BlockSpec((ts, A_WIDTH), lambda i: (i, 0))
    gsp = pl.BlockSpec((ts, W_G), lambda i: (i, COL_G))
    anysp = pl.BlockSpec(memory_space=pl.ANY)
    yshape = jax.ShapeDtypeStruct((s, A_WIDTH), F32)
    return pl.pallas_call(
        body, grid=(s // ts,),
        in_specs=[row, row, ysp, ysp, ysp, gsp, pl.BlockSpec((3, A_WIDTH, d), lambda i: (0, 0, 0)),
                  pl.BlockSpec((d, d), lambda i: (0, 0))],
        out_specs=[gsp, ysp, ysp, ysp, anysp, anysp],
        out_shape=[jax.ShapeDtypeStruct((s, IN_WIDTH), BF16), yshape, yshape, yshape,
                   jax.ShapeDtypeStruct((3, A_WIDTH, d), F32), jax.ShapeDtypeStruct((d, d), F32)],
        scratch_shapes=[pltpu.VMEM((3, A_WIDTH, d), F32), pltpu.VMEM((d, d), F32)],
        name="mid_bwd", compiler_params=_params(heavy=True))(d_out, merged, ya, yb, ym, proj, w_up, w_out)


def _conv_core(blk, prev, nxt, w, i, last, ts):
    c = A_WIDTH
    ab, ac, ax, az = blk[:, :c], blk[:, c:2 * c], blk[:, 2 * c:3 * c], blk[:, 3 * c:]
    cu = ac * ax
    cu_prev = (prev[7:8, c:2 * c] * prev[7:8, 2 * c:3 * c]) * jnp.where(i > 0, 1.0, 0.0)
    cu_next = (nxt[0:1, c:2 * c] * nxt[0:1, 2 * c:3 * c]) * jnp.where(i < last, 1.0, 0.0)
    row = lax.broadcasted_iota(jnp.int32, (ts, c), 0)
    cm1 = jnp.where(row == 0, cu_prev, pltpu.roll(cu, 1, 0))
    cp1 = jnp.where(row == ts - 1, cu_next, pltpu.roll(cu, ts - 1, 0))
    yc = cm1 * w[0:1] + cu * w[1:2] + cp1 * w[2:3]
    return ab, ac, ax, az, cu, cm1, cp1, yc, row


def _halo_specs(ts, width, col, nblk8):
    prev = pl.BlockSpec((8, width), lambda i: (jnp.maximum(i * (ts // 8) - 1, 0), col))
    nxt = pl.BlockSpec((8, width), lambda i: (jnp.minimum((i + 1) * (ts // 8), nblk8 - 1), col))
    return prev, nxt


def _conv_fwd(proj, w_conv):
    s = proj.shape[0]
    ts = 256
    last = s // ts - 1

    def body(a_ref, ap_ref, an_ref, w_ref, ya_ref):
        i = pl.program_id(0)
        ab, _, _, az, _, _, _, yc, _ = _conv_core(a_ref[...], ap_ref[...], an_ref[...], w_ref[...], i, last, ts)
        ya_ref[...] = ((ab * yc) * (az * _sigmoid(az))).astype(BF16)

    prev, nxt = _halo_specs(ts, W_A, COL_A, s // 8)
    return pl.pallas_call(
        body, grid=(s // ts,),
        in_specs=[pl.BlockSpec((ts, W_A), lambda i: (i, COL_A)), prev, nxt,
                  pl.BlockSpec((3, A_WIDTH), lambda i: (0, 0))],
        out_specs=pl.BlockSpec((ts, A_WIDTH), lambda i: (i, 0)),
        out_shape=jax.ShapeDtypeStruct((s, A_WIDTH), BF16), name="conv_fwd",
        compiler_params=_params())(proj, proj, proj, w_conv)


def _conv_bwd(proj, w_conv, dya, dproj):
    s = proj.shape[0]
    ts = 256
    last = s // ts - 1
    c = A_WIDTH

    def body(a_ref, ap_ref, an_ref, w_ref, d_ref, dp_ref, dn_ref, _, dproj_ref, dw_ref):
        i = pl.program_id(0)
        w = w_ref[...]
        prev, nxt = ap_ref[...], an_ref[...]
        ab, ac, ax, az, cu, cm1, cp1, yc, row = _conv_core(a_ref[...], prev, nxt, w, i, last, ts)
        sg = _sigmoid(az)
        sz = az * sg
        dya_v = d_ref[...]
        dyc = dya_v * sz * ab
        dproj_ref[:, :c] = (dya_v * sz * yc).astype(BF16)
        dproj_ref[:, 3 * c:] = (dya_v * (ab * yc) * (sg * (1.0 + az * (1.0 - sg)))).astype(BF16)

        def halo_dyc(a_row, d_row):
            azr = a_row[:, 3 * c:]
            return d_row * (azr * _sigmoid(azr)) * a_row[:, :c]

        dyc_prev = halo_dyc(prev[7:8], dp_ref[...][7:8]) * jnp.where(i > 0, 1.0, 0.0)
        dyc_next = halo_dyc(nxt[0:1], dn_ref[...][0:1]) * jnp.where(i < last, 1.0, 0.0)
        dyc_m1 = jnp.where(row == 0, dyc_prev, pltpu.roll(dyc, 1, 0))
        dyc_p1 = jnp.where(row == ts - 1, dyc_next, pltpu.roll(dyc, ts - 1, 0))
        dcu = dyc_p1 * w[0:1] + dyc * w[1:2] + dyc_m1 * w[2:3]
        dproj_ref[:, c:2 * c] = (dcu * ax).astype(BF16)
        dproj_ref[:, 2 * c:3 * c] = (dcu * ac).astype(BF16)
        dw = [jnp.sum(dyc * t, axis=0, keepdims=True) for t in (cm1, cu, cp1)]

        @pl.when(i == 0)
        def _():
            for k in range(3):
                dw_ref[k:k + 1, :] = dw[k]

        @pl.when(i > 0)
        def _():
            for k in range(3):
                dw_ref[k:k + 1, :] += dw[k]

    prev, nxt = _halo_specs(ts, W_A, COL_A, s // 8)
    dprev, dnxt = _halo_specs(ts, A_WIDTH, 0, s // 8)
    return pl.pallas_call(
        body, grid=(s // ts,),
        in_specs=[pl.BlockSpec((ts, W_A), lambda i: (i, COL_A)), prev, nxt,
                  pl.BlockSpec((3, A_WIDTH), lambda i: (0, 0)),
                  pl.BlockSpec((ts, A_WIDTH), lambda i: (i, 0)), dprev, dnxt,
                  pl.BlockSpec(memory_space=pl.ANY)],
        out_specs=[pl.BlockSpec((ts, W_A), lambda i: (i, COL_A)), pl.BlockSpec((3, A_WIDTH), lambda i: (0, 0))],
        out_shape=[jax.ShapeDtypeStruct(dproj.shape, BF16), jax.ShapeDtypeStruct((3, A_WIDTH), F32)],
        input_output_aliases={7: 0}, name="conv_bwd",
        compiler_params=_params())(proj, proj, proj, w_conv, dya, dya, dya, dproj)


def _rope_tables(s):
    half = ROT_DIM // 2
    dim = jnp.arange(LANES) % HEAD_DIM
    inv_freq = jnp.power(jnp.float32(ROPE_THETA), -(dim % half).astype(F32) * (2.0 / ROT_DIM))
    ang = jnp.arange(s).astype(F32)[:, None] * inv_freq[None, :]
    cos, sin = jnp.cos(ang), jnp.sin(ang)
    first, second = (dim < half)[None, :], ((dim >= half) & (dim < ROT_DIM))[None, :]
    c = jnp.where(first | second, cos, 1.0)
    s1 = jnp.where(first, -sin, 0.0)
    s2 = jnp.where(second, sin, 0.0)
    return jnp.concatenate([c, s1, s2], axis=1)


def _rope(t, tab):
    return (t * tab[:, :LANES] + pltpu.roll(t, LANES - 8, 1) * tab[:, LANES:2 * LANES]
            + pltpu.roll(t, 8, 1) * tab[:, 2 * LANES:])


def _rope_transpose(dt, tab):
    return (dt * tab[:, :LANES] + pltpu.roll(dt * tab[:, LANES:2 * LANES], 8, 1)
            + pltpu.roll(dt * tab[:, 2 * LANES:], LANES - 8, 1))


def _rope_kv(proj, tab):
    s = proj.shape[0]
    nb = s // KV_PAD

    def body(kv_ref, t_ref, k_ref, v_ref):
        j = pl.program_id(0)
        inside = jnp.where((j > 0) & (j <= nb), 1.0, 0.0)
        kv = kv_ref[...]
        k_ref[...] = (_rope(kv[:, :LANES], t_ref[...]) * inside).astype(BF16)
        v_ref[...] = (kv[:, LANES:] * inside).astype(BF16)

    def src(j):
        return jnp.clip(j - 1, 0, nb - 1)

    o_spec = pl.BlockSpec((KV_PAD, LANES), lambda j: (j, 0))
    shp = jax.ShapeDtypeStruct((s + 2 * KV_PAD, LANES), BF16)
    return pl.pallas_call(
        body, grid=(nb + 2,),
        in_specs=[pl.BlockSpec((KV_PAD, W_KV), lambda j: (src(j), COL_KV)),
                  pl.BlockSpec((KV_PAD, 3 * LANES), lambda j: (src(j), 0))],
        out_specs=[o_spec, o_spec], out_shape=[shp, shp], name="rope_kv",
        compiler_params=_params())(proj, tab)


def _rope_kv_bwd(dkpad, dvpad, tab, dproj):
    s = tab.shape[0]
    nb = s // KV_PAD

    def body(dk_ref, dv_ref, t_ref, _, dp_ref):
        dp_ref[:, :LANES] = _rope_transpose(dk_ref[...], t_ref[...]).astype(BF16)
        dp_ref[:, LANES:] = dv_ref[...].astype(BF16)

    pad_spec = pl.BlockSpec((KV_PAD, LANES), lambda j: (j + 1, 0))
    return pl.pallas_call(
        body, grid=(nb,),
        in_specs=[pad_spec, pad_spec, pl.BlockSpec((KV_PAD, 3 * LANES), lambda j: (j, 0)),
                  pl.BlockSpec(memory_space=pl.ANY)],
        out_specs=pl.BlockSpec((KV_PAD, W_KV), lambda j: (j, COL_KV)),
        out_shape=jax.ShapeDtypeStruct(dproj.shape, BF16), input_output_aliases={3: 0},
        name="rope_kv_bwd", compiler_params=_params())(dkpad, dvpad, tab, dproj)


def _window_start(n):
    return pl.multiple_of((n - 1) * WINDOW_BLOCK + KV_PAD, WINDOW_BLOCK)


def _window_operands(k_ref, v_ref, n, lo):
    start = _window_start(n)
    kw = k_ref[pl.ds(start, 3 * WINDOW_BLOCK), :].astype(F32)
    vw = v_ref[pl.ds(start, 3 * WINDOW_BLOCK), :].astype(F32)
    kr, vr = pltpu.roll(kw, HALF_LANES, 1), pltpu.roll(vw, HALF_LANES, 1)
    k2 = (jnp.where(lo, kw, kr).astype(BF16), jnp.where(lo, kr, kw).astype(BF16))
    v2 = (jnp.where(lo, vw, vr).astype(BF16), jnp.where(lo, vr, vw).astype(BF16))
    return k2, v2


HEADS_PER_GROUP = 4
SWA_FWD_BLOCKS = 1
SWA_BWD_BLOCKS = 2


def _window_bias():
    wb = WINDOW_BLOCK
    qi = lax.broadcasted_iota(jnp.int32, (wb, 3 * wb), 0)
    kj = lax.broadcasted_iota(jnp.int32, (wb, 3 * wb), 1)
    band = (kj >= qi) & (kj <= qi + 2 * wb)
    cases = jnp.stack([band & (kj >= wb), band, band & (kj < 2 * wb)])
    return jnp.where(cases, 0.0, -jnp.inf).astype(F32)


def _block_bias(bias_ref, n, n_blocks):
    case = jnp.where(n == 0, 0, jnp.where(n == n_blocks - 1, 2, 1))
    one = bias_ref[case]
    return jnp.concatenate([one] * HEADS_PER_GROUP, axis=0)


def _stack_heads(pair0, pair1, lo):
    return jnp.concatenate([jnp.where(lo, pair0, 0.0), jnp.where(lo, 0.0, pair0),
                            jnp.where(lo, pair1, 0.0), jnp.where(lo, 0.0, pair1)], axis=0)


def _unstack_pair(stacked, i, lo):
    wb = WINDOW_BLOCK
    return jnp.where(lo, stacked[2 * i * wb:(2 * i + 1) * wb], stacked[(2 * i + 1) * wb:(2 * i + 2) * wb])


def _sink_column(sink_ref, g):
    wb = WINDOW_BLOCK
    return jnp.concatenate([jnp.full((wb, 1), sink_ref[0, HEADS_PER_GROUP * g + i], F32)
                            for i in range(HEADS_PER_GROUP)], axis=0)


def _head_exp(q4, k2g, bias, sink):
    sc = lax.dot_general(q4, k2g, _DIMS["nt"], preferred_element_type=F32) * (HEAD_DIM ** -0.5) + bias
    m = jnp.maximum(jnp.max(sc, axis=1, keepdims=True), sink)
    return jnp.exp(sc - m).astype(BF16), jnp.exp(sink - m)


def _swa_fwd(proj, kpad, vpad, tab, bias, sink):
    s = proj.shape[0]
    wb = WINDOW_BLOCK

    def body(b_ref, k_ref, v_ref, t_ref, bias_ref, sink_ref, o_ref, y_ref):
        lo = lax.broadcasted_iota(jnp.int32, (wb, LANES), 1) < HALF_LANES
        lo_w = lax.broadcasted_iota(jnp.int32, (3 * wb, LANES), 1) < HALF_LANES
        for sub in range(SWA_FWD_BLOCKS):
            n = pl.program_id(0) * SWA_FWD_BLOCKS + sub
            rows = slice(sub * wb, (sub + 1) * wb)
            k2, v2 = _window_operands(k_ref, v_ref, n, lo_w)
            valid = _block_bias(bias_ref, n, s // wb)
            tab_v = t_ref[rows, :]
            ones = jnp.ones((3 * wb, LANES), BF16)
            for g in range(2):
                qr = [_rope(b_ref[rows, (2 * g + i) * LANES:(2 * g + i + 1) * LANES], tab_v) for i in range(2)]
                q4 = _stack_heads(qr[0], qr[1], lo).astype(BF16)
                e, es = _head_exp(q4, k2[g], valid, _sink_column(sink_ref, g))
                ox = jnp.dot(e, jnp.concatenate([v2[g], ones], axis=1), preferred_element_type=F32)
                o4 = ox[:, :LANES] * (1.0 / (ox[:, LANES:] + es))
                for i in range(2):
                    cols = slice((2 * g + i) * LANES, (2 * g + i + 1) * LANES)
                    op = _unstack_pair(o4, i, lo)
                    o_ref[rows, cols] = op
                    zp = b_ref[rows, A_WIDTH + cols.start:A_WIDTH + cols.stop]
                    y_ref[rows, cols] = (op * (zp * _sigmoid(zp))).astype(BF16)

    tq = SWA_FWD_BLOCKS * wb
    pad_spec = pl.BlockSpec((s + 2 * KV_PAD, LANES), lambda n: (0, 0))
    o_spec = pl.BlockSpec((tq, A_WIDTH), lambda n: (n, 0))
    return pl.pallas_call(
        body, grid=(s // tq,),
        in_specs=[pl.BlockSpec((tq, W_B), lambda n: (n, COL_B)), pad_spec, pad_spec,
                  pl.BlockSpec((tq, 3 * LANES), lambda n: (n, 0)),
                  pl.BlockSpec(bias.shape, lambda n: (0, 0, 0)), pl.BlockSpec(memory_space=pltpu.SMEM)],
        out_specs=[o_spec, o_spec],
        out_shape=[jax.ShapeDtypeStruct((s, A_WIDTH), F32), jax.ShapeDtypeStruct((s, A_WIDTH), BF16)],
        name="swa_fwd", compiler_params=_params())(proj, kpad, vpad, tab, bias, sink)


def _swa_bwd(proj, kpad, vpad, tab, bias, sink, o_attn, dyb, dproj):
    s = proj.shape[0]
    wb = WINDOW_BLOCK
    scale = HEAD_DIM ** -0.5

    def body(b_ref, k_ref, v_ref, t_ref, bias_ref, sink_ref, o_ref, dy_ref, _, dp_ref, dk_ref, dv_ref, ds_ref):
        @pl.when(pl.program_id(0) == 0)
        def _():
            dk_ref[...] = jnp.zeros_like(dk_ref)
            dv_ref[...] = jnp.zeros_like(dv_ref)
            ds_ref[...] = jnp.zeros_like(ds_ref)

        lo = lax.broadcasted_iota(jnp.int32, (wb, LANES), 1) < HALF_LANES
        lo_w = lax.broadcasted_iota(jnp.int32, (3 * wb, LANES), 1) < HALF_LANES
        for sub in range(SWA_BWD_BLOCKS):
            n = pl.program_id(0) * SWA_BWD_BLOCKS + sub
            rows = slice(sub * wb, (sub + 1) * wb)
            k2, v2 = _window_operands(k_ref, v_ref, n, lo_w)
            valid = _block_bias(bias_ref, n, s // wb)
            tab_v = t_ref[rows, :]
            ones = jnp.ones((3 * wb, LANES), BF16)
            dks, dvs = [], []
            for g in range(2):
                qr, op, do = [], [], []
                for i in range(2):
                    cols = slice((2 * g + i) * LANES, (2 * g + i + 1) * LANES)
                    zcols = slice(A_WIDTH + cols.start, A_WIDTH + cols.stop)
                    qr.append(_rope(b_ref[rows, cols], tab_v))
                    zp = b_ref[rows, zcols]
                    sg = _sigmoid(zp)
                    op.append(o_ref[rows, cols])
                    dyp = dy_ref[rows, cols]
                    do.append(dyp * (zp * sg))
                    dp_ref[rows, zcols] = (dyp * op[i] * (sg * (1.0 + zp * (1.0 - sg)))).astype(BF16)
                q4 = _stack_heads(qr[0], qr[1], lo).astype(BF16)
                do4 = _stack_heads(do[0], do[1], lo)
                o4 = jnp.concatenate([op[0], op[0], op[1], op[1]], axis=0)
                e, es = _head_exp(q4, k2[g], valid, _sink_column(sink_ref, g))
                inv = 1.0 / (jnp.dot(e, ones, preferred_element_type=F32) + es)
                prob = e.astype(F32) * jnp.concatenate([inv, inv, inv], axis=1)
                delta = jnp.sum(do4 * o4, axis=1, keepdims=True)
                do4b = do4.astype(BF16)
                dprob = lax.dot_general(do4b, v2[g], _DIMS["nt"], preferred_element_type=F32)
                dsc = (prob * (dprob - delta)).astype(BF16)
                sink_terms = (es * inv[:, :1]) * delta
                for i in range(HEADS_PER_GROUP):
                    h = HEADS_PER_GROUP * g + i
                    dsink = -jnp.sum(sink_terms[i * wb:(i + 1) * wb], axis=0, keepdims=True)
                    ds_ref[h:h + 1, :] += jnp.broadcast_to(dsink, (1, LANES))
                dq4 = jnp.dot(dsc, k2[g], preferred_element_type=F32) * scale
                for i in range(2):
                    cols = slice((2 * g + i) * LANES, (2 * g + i + 1) * LANES)
                    dp_ref[rows, cols] = _rope_transpose(_unstack_pair(dq4, i, lo), tab_v).astype(BF16)
                dk2 = lax.dot_general(dsc, q4, _DIMS["tn"], preferred_element_type=F32) * scale
                dv2 = lax.dot_general(prob.astype(BF16), do4b, _DIMS["tn"], preferred_element_type=F32)
                dks.append(dk2 + pltpu.roll(dk2, HALF_LANES, 1))
                dvs.append(dv2 + pltpu.roll(dv2, HALF_LANES, 1))
            start = _window_start(n)
            dk_ref[pl.ds(start, 3 * wb), :] += jnp.where(lo_w, dks[0], dks[1])
            dv_ref[pl.ds(start, 3 * wb), :] += jnp.where(lo_w, dvs[0], dvs[1])

    tq = SWA_BWD_BLOCKS * wb
    pad_spec = pl.BlockSpec((s + 2 * KV_PAD, LANES), lambda n: (0, 0))
    blk = pl.BlockSpec((tq, A_WIDTH), lambda n: (n, 0))
    bsp = pl.BlockSpec((tq, W_B), lambda n: (n, COL_B))
    pad_shape = jax.ShapeDtypeStruct((s + 2 * KV_PAD, LANES), F32)
    return pl.pallas_call(
        body, grid=(s // tq,),
        in_specs=[bsp, pad_spec, pad_spec, pl.BlockSpec((tq, 3 * LANES), lambda n: (n, 0)),
                  pl.BlockSpec(bias.shape, lambda n: (0, 0, 0)), pl.BlockSpec(memory_space=pltpu.SMEM), blk, blk,
                  pl.BlockSpec(memory_space=pl.ANY)],
        out_specs=[bsp, pad_spec, pad_spec, pl.BlockSpec((8, LANES), lambda n: (0, 0))],
        out_shape=[jax.ShapeDtypeStruct(dproj.shape, BF16), pad_shape, pad_shape,
                   jax.ShapeDtypeStruct((8, LANES), F32)],
        input_output_aliases={8: 0}, name="swa_bwd",
        compiler_params=_params())(proj, kpad, vpad, tab, bias, sink, o_attn, dyb, dproj)


def _mem_exp(qh, mk):
    sc = lax.dot_general(qh, mk, _DIMS["nt"], preferred_element_type=F32) * (MEM_HEAD_DIM ** -0.5)
    return jnp.exp(sc - jnp.max(sc, axis=1, keepdims=True)).astype(BF16)


def _mem_fwd(proj, mkv):
    s = proj.shape[0]
    ts = 512
    mlen = mkv.shape[0]

    def body(m_ref, kv_ref, o_ref, y_ref):
        ones = jnp.ones((mlen, LANES), BF16)
        for h in range(MEM_HEADS):
            cols = slice(h * LANES, (h + 1) * LANES)
            mk = kv_ref[:, cols].astype(BF16)
            mv = kv_ref[:, MEM_WIDTH + h * LANES:MEM_WIDTH + (h + 1) * LANES].astype(BF16)
            e = _mem_exp(m_ref[:, cols].astype(BF16), mk)
            ox = jnp.dot(e, jnp.concatenate([mv, ones], axis=1), preferred_element_type=F32)
            oh = ox[:, :LANES] * (1.0 / ox[:, LANES:])
            o_ref[:, cols] = oh
            zh = m_ref[:, MEM_WIDTH + h * LANES:MEM_WIDTH + (h + 1) * LANES]
            y_ref[:, cols] = (oh * (zh * _sigmoid(zh))).astype(BF16)

    o_spec = pl.BlockSpec((ts, MEM_WIDTH), lambda i: (i, 0))
    return pl.pallas_call(
        body, grid=(s // ts,),
        in_specs=[pl.BlockSpec((ts, W_M), lambda i: (i, COL_M)),
                  pl.BlockSpec((mlen, 2 * MEM_WIDTH), lambda i: (0, 0))],
        out_specs=[o_spec, o_spec],
        out_shape=[jax.ShapeDtypeStruct((s, MEM_WIDTH), F32), jax.ShapeDtypeStruct((s, MEM_WIDTH), BF16)],
        name="mem_fwd", compiler_params=_params())(proj, mkv)


def _mem_bwd(proj, mkv, o_mem, dym, dproj):
    s = proj.shape[0]
    ts = 512
    mlen = mkv.shape[0]
    scale = MEM_HEAD_DIM ** -0.5

    def body(m_ref, kv_ref, o_ref, dy_ref, _, dp_ref, dkv_ref):
        @pl.when(pl.program_id(0) == 0)
        def _():
            dkv_ref[...] = jnp.zeros_like(dkv_ref)

        ones = jnp.ones((mlen, LANES), BF16)
        for h in range(MEM_HEADS):
            cols = slice(h * LANES, (h + 1) * LANES)
            vcols = slice(MEM_WIDTH + h * LANES, MEM_WIDTH + (h + 1) * LANES)
            mk = kv_ref[:, cols].astype(BF16)
            mv = kv_ref[:, vcols].astype(BF16)
            qh = m_ref[:, cols].astype(BF16)
            zh = m_ref[:, vcols]
            sg = _sigmoid(zh)
            oh = o_ref[:, cols]
            dyh = dy_ref[:, cols]
            doh = dyh * (zh * sg)
            dp_ref[:, vcols] = (dyh * oh * (sg * (1.0 + zh * (1.0 - sg)))).astype(BF16)
            e = _mem_exp(qh, mk)
            inv = 1.0 / jnp.dot(e, ones, preferred_element_type=F32)
            prob = e.astype(F32) * jnp.concatenate([inv] * (mlen // LANES), axis=1)
            delta = jnp.sum(doh * oh, axis=1, keepdims=True)
            dohb = doh.astype(BF16)
            dprob = lax.dot_general(dohb, mv, _DIMS["nt"], preferred_element_type=F32)
            dsc = (prob * (dprob - delta)).astype(BF16)
            dp_ref[:, cols] = (jnp.dot(dsc, mk, preferred_element_type=F32) * scale).astype(BF16)
            dkv_ref[:, cols] += lax.dot_general(dsc, qh, _DIMS["tn"], preferred_element_type=F32) * scale
            dkv_ref[:, vcols] += lax.dot_general(prob.astype(BF16), dohb, _DIMS["tn"],
                                                 preferred_element_type=F32)

    blk = pl.BlockSpec((ts, MEM_WIDTH), lambda i: (i, 0))
    msp = pl.BlockSpec((ts, W_M), lambda i: (i, COL_M))
    kvsp = pl.BlockSpec((mlen, 2 * MEM_WIDTH), lambda i: (0, 0))
    return pl.pallas_call(
        body, grid=(s // ts,),
        in_specs=[msp, kvsp, blk, blk, pl.BlockSpec(memory_space=pl.ANY)],
        out_specs=[msp, kvsp],
        out_shape=[jax.ShapeDtypeStruct(dproj.shape, BF16), jax.ShapeDtypeStruct(mkv.shape, F32)],
        input_output_aliases={4: 0}, name="mem_bwd",
        compiler_params=_params())(proj, mkv, o_mem, dym, dproj)


def _forward_backward(x, mem, tgt, proj, w_conv, sink, g_mem, w_kv, w_up, w_out, g_post):
    s = x.shape[0]
    tab = _rope_tables(s)
    bias = _window_bias()

    ya = _conv_fwd(proj, w_conv)
    kpad, vpad = _rope_kv(proj, tab)
    o_attn, yb = _swa_fwd(proj, kpad, vpad, tab, bias, sink)
    mn = _rmsnorm_fwd(mem, g_mem, name="mem_norm")
    mkv = _matmul(mn, w_kv, mode="nn", out_dtype=F32, tm=256, tn=1024, tk=D_MODEL, name="mem_kv")
    o_mem, ym = _mem_fwd(proj, mkv)
    merged, d_out, dy, dg_post, loss = _mid_fwd(ya, yb, ym, proj, x, tgt, w_up, w_out, g_post)
    dproj, d_ya, d_yb, d_ym, dw_up, dw_out = _mid_bwd(d_out, merged, ya, yb, ym, proj, w_up, w_out)

    dproj, dw_conv = _conv_bwd(proj, w_conv, d_ya, dproj)
    dproj, dkpad, dvpad, dsink = _swa_bwd(proj, kpad, vpad, tab, bias, sink, o_attn, d_yb, dproj)
    dproj = _rope_kv_bwd(dkpad, dvpad, tab, dproj)
    dproj, d_mkv = _mem_bwd(proj, mkv, o_mem, d_ym, dproj)

    dw_kv = _matmul(mn, d_mkv, mode="tn", out_dtype=F32, tm=1024, tn=1024, tk=256, name="dw_kv")
    d_mn = _matmul(d_mkv, w_kv, mode="nt", out_dtype=F32, tm=256, tn=1024, tk=D_MODEL, name="d_mn")
    _, dg_mem = _rmsnorm_bwd(d_mn, mem, g_mem, d_mn, name="mem_norm_bwd")

    return dict(loss=loss, dproj=dproj, dy=dy, w_conv=dw_conv, sink=dsink, g_mem=dg_mem,
                w_kv=dw_kv, w_up=dw_up, w_out=dw_out, g_post=dg_post)


N_DEV = 8


def _position():
    return lax.axis_index("x"), lax.axis_index("y"), lax.axis_index("c")


def _other_chips(x, y):
    return (((1 - x, y), 2 * (1 - x) + y), ((x, 1 - y), 2 * x + (1 - y)), ((1 - x, 1 - y), 2 * (1 - x) + (1 - y)))


def _remote(src, dst, send_sems, recv_sems, k, device):
    return pltpu.make_async_remote_copy(src_ref=src, dst_ref=dst, send_sem=send_sems.at[k], recv_sem=recv_sems.at[k],
                                        device_id=device, device_id_type=MESH)


def _rows_half(ref, hf):
    rh = ref.shape[0] // 2
    return ref.at[pl.ds(pl.multiple_of(hf * rh, 8), rh)]


def _gather_weights(shards, small=None, relations=(0, 1, 2), into=None):
    n = len(shards)
    k = 0 if small is None else 1

    def peers(x, y):
        return [(r, chip, idx) for r, (chip, idx) in enumerate(_other_chips(x, y)) if r in relations]

    def ici(ins, outs, sems, a, r, chip, src_chip, c):
        return _remote(_rows_half(ins[a], c), _rows_half(outs[a].at[src_chip], c), sems[0], sems[1], 3 * a + r,
                       (*chip, c))

    def whole(ins, outs, sems, r, chip, src_chip, c):
        return _remote(ins[n], outs[n].at[src_chip], sems[0], sems[1], 3 * n + r, (*chip, c))

    def d2d(outs, sems, a, r, idx, hf, x, y, c):
        half = _rows_half(outs[a].at[idx], hf)
        return _remote(half, half, sems[2], sems[3], 3 * a + r, (x, y, 1 - c))

    def start(ins, outs, sems):
        x, y, c = _position()
        me = 2 * x + y
        for a in range(n):
            for r, chip, _ in peers(x, y):
                ici(ins, outs, sems, a, r, chip, me, c).start()
        for r, (chip, _) in enumerate(_other_chips(x, y)):
            if k:
                whole(ins, outs, sems, r, chip, me, c).start()

    def finish(ins, outs, sems):
        x, y, c = _position()
        me = 2 * x + y
        for a in range(n):
            for r, chip, idx in peers(x, y):
                ici(ins, outs, sems, a, r, chip, idx, c).wait_recv()
                d2d(outs, sems, a, r, idx, c, x, y, c).start()
        for a in range(n):
            for r, chip, idx in peers(x, y):
                d2d(outs, sems, a, r, idx, 1 - c, x, y, c).wait_recv()
        for r, (chip, idx) in enumerate(_other_chips(x, y)):
            if k:
                whole(ins, outs, sems, r, chip, idx, c).wait_recv()
                whole(ins, outs, sems, r, chip, me, c).wait_send()
        for a in range(n):
            for r, chip, idx in peers(x, y):
                ici(ins, outs, sems, a, r, chip, me, c).wait_send()
                d2d(outs, sems, a, r, idx, c, x, y, c).wait_send()

    operands = list(shards) + ([small] if k else [])
    shapes = [jax.ShapeDtypeStruct((N_CHIPS,) + s.shape, s.dtype) for s in operands]
    aliases = {}
    if into is not None:
        assert len(into) == len(operands)
        aliases = {len(operands) + a: a for a in range(len(into))}
        operands += list(into)
    return _Carry(operands, shapes,
                  [pltpu.SemaphoreType.DMA((3 * (n + k),)), pltpu.SemaphoreType.DMA((3 * (n + k),)),
                   pltpu.SemaphoreType.DMA((3 * n,)), pltpu.SemaphoreType.DMA((3 * n,))], start, finish, aliases)


def _run_carry(carry, name):
    _, results = _carried_call(lambda ins, outs, scr: None, carry, grid=(1,), in_specs=[], out_specs=[],
                               out_shape=[], scratch=[], operands=(), name=name)
    return results


def _pair_exchange(send):
    n = len(send)

    def copies(ins, outs, sems):
        x, y, c = _position()
        return [_remote(ins[a], outs[a], sems[0], sems[1], a, (x, y, 1 - c)) for a in range(n)]

    def start(ins, outs, sems):
        for cp in copies(ins, outs, sems):
            cp.start()

    def finish(ins, outs, sems):
        for cp in copies(ins, outs, sems):
            cp.wait()

    return _Carry(send, [jax.ShapeDtypeStruct(p.shape, p.dtype) for p in send],
                  [pltpu.SemaphoreType.DMA((n,)), pltpu.SemaphoreType.DMA((n,))], start, finish)


def _chip_exchange(sums):
    n = len(sums)

    def copies(ins, outs, sems):
        x, y, c = _position()
        return [_remote(ins[a].at[idx], outs[a].at[r], sems[0], sems[1], 3 * a + r, (*chip, c))
                for a in range(n) for r, (chip, idx) in enumerate(_other_chips(x, y))]

    def start(ins, outs, sems):
        for cp in copies(ins, outs, sems):
            cp.start()

    def finish(ins, outs, sems):
        for cp in copies(ins, outs, sems):
            cp.wait()

    return _Carry(sums, [jax.ShapeDtypeStruct((3,) + p.shape[1:], p.dtype) for p in sums],
                  [pltpu.SemaphoreType.DMA((3 * n,)), pltpu.SemaphoreType.DMA((3 * n,))], start, finish)


def _pair_share(pairs):
    n = len(pairs)

    def start(ins, outs, sems):
        x, y, c = _position()
        for a in range(n):
            _remote(outs[a].at[c], outs[a].at[c], sems[0], sems[1], a, (x, y, 1 - c)).start()

    def finish(ins, outs, sems):
        x, y, c = _position()
        for a in range(n):
            _remote(outs[a].at[1 - c], outs[a].at[1 - c], sems[0], sems[1], a, (x, y, 1 - c)).wait_recv()
        for a in range(n):
            _remote(outs[a].at[c], outs[a].at[c], sems[0], sems[1], a, (x, y, 1 - c)).wait_send()

    return _Carry(pairs, [jax.ShapeDtypeStruct(p.shape, p.dtype) for p in pairs],
                  [pltpu.SemaphoreType.DMA((n,)), pltpu.SemaphoreType.DMA((n,))], start, finish,
                  aliases={a: a for a in range(n)})


def _small_allreduce(pack, share):
    rows, width = pack.shape
    n_share = len(share.ins)

    def body(p_ref, *refs):
        share_in, o_ref, share_out = refs[:n_share], refs[n_share], refs[n_share + 1:2 * n_share + 1]
        buf, send_sems, recv_sems = refs[2 * n_share + 1:2 * n_share + 4]
        share_sems = refs[2 * n_share + 4:]
        share.start(share_in, share_out, share_sems)
        x, y, c = _position()
        me = 4 * x + 2 * y + c
        buf[me] = p_ref[...]
        peers = []
        for r in range(1, N_DEV):
            fx, fy, fc = (r >> 2) & 1, (r >> 1) & 1, r & 1
            px, py, pc = (1 - x if fx else x), (1 - y if fy else y), (1 - c if fc else c)
            peers.append(((px, py, pc), 4 * px + 2 * py + pc))
        sends = [_remote(p_ref, buf.at[me], send_sems, recv_sems, r, dev) for r, (dev, _) in enumerate(peers)]
        for cp in sends:
            cp.start()
        for r, (dev, idx) in enumerate(peers):
            _remote(p_ref, buf.at[idx], send_sems, recv_sems, r, dev).wait_recv()
        for cp in sends:
            cp.wait_send()
        acc = buf[0]
        for k in range(1, N_DEV):
            acc = acc + buf[k]
        o_ref[...] = acc
        share.finish(share_in, share_out, share_sems)

    vm = pl.BlockSpec(memory_space=pltpu.VMEM)
    red, *shared = pl.pallas_call(
        body, in_specs=[vm] + [_HBM] * n_share, out_specs=[vm] + [_HBM] * n_share,
        out_shape=[jax.ShapeDtypeStruct(pack.shape, F32)] + share.out_shapes,
        scratch_shapes=[pltpu.VMEM((N_DEV, rows, width), F32), pltpu.SemaphoreType.DMA((N_DEV - 1,)),
                        pltpu.SemaphoreType.DMA((N_DEV - 1,))] + share.sems,
        input_output_aliases={1 + i: 1 + o for i, o in share.aliases.items()},
        name="small_allreduce")(pack, *share.ins)
    return red, shared


ROW_TILE_MAX = 512
SUM_TILE_MAX = 2048
BF16_SUBLANES = 16


def _row_tile(rows, most=ROW_TILE_MAX):
    if rows <= most:
        return rows
    return max(t for t in range(BF16_SUBLANES, most + 1, BF16_SUBLANES) if rows % t == 0)


def _pair_add(keep, recv, name):
    nj, rh, cols = keep.shape
    tr = _row_tile(rh, SUM_TILE_MAX)

    def body(k_ref, r_ref, o_ref):
        o_ref[...] = (k_ref[...].astype(F32) + r_ref[...].astype(F32)).astype(BF16)

    blk = pl.BlockSpec((None, tr, cols), lambda j, i: (j, i, 0))
    return pl.pallas_call(body, grid=(nj, rh // tr), in_specs=[blk, blk], out_specs=blk,
                          out_shape=jax.ShapeDtypeStruct(keep.shape, BF16), name=name,
                          compiler_params=_params())(keep, recv)


def _chip_add(sums, recv, where, name):
    _, rh, cols = sums.shape
    tr = _row_tile(rh, SUM_TILE_MAX)

    def body(w_ref, s_ref, r_ref, o_ref):
        o_ref[...] = ((s_ref[...].astype(F32) + r_ref[0].astype(F32)) + r_ref[1].astype(F32)) + r_ref[2].astype(F32)

    grid_spec = pltpu.PrefetchScalarGridSpec(
        num_scalar_prefetch=1, grid=(rh // tr,),
        in_specs=[pl.BlockSpec((None, tr, cols), lambda i, w_ref: (w_ref[0], i, 0)),
                  pl.BlockSpec((3, tr, cols), lambda i, w_ref: (0, i, 0))],
        out_specs=pl.BlockSpec((None, tr, cols), lambda i, w_ref: (w_ref[1], i, 0)))
    return pl.pallas_call(body, grid_spec=grid_spec, out_shape=jax.ShapeDtypeStruct((2, rh, cols), F32),
                          name=name, compiler_params=_params())(where, sums, recv)


def _adamw(w, g, m, v, name):
    rows, cols = w.shape
    tr = _row_tile(rows)
    assert rows % tr == 0

    def body(w_ref, g_ref, m_ref, v_ref, d_ref, mo_ref, vo_ref):
        gv = g_ref[...]
        m_new = ADAM_B1 * m_ref[...] + (1.0 - ADAM_B1) * gv
        v_new = ADAM_B2 * v_ref[...] + (1.0 - ADAM_B2) * jnp.square(gv)
        m_hat = m_new / (1.0 - ADAM_B1 ** ADAM_STEP)
        v_hat = v_new / (1.0 - ADAM_B2 ** ADAM_STEP)
        d_ref[...] = -ADAM_LR * (m_hat / (jnp.sqrt(v_hat) + ADAM_EPS) + ADAM_WD * w_ref[...])
        mo_ref[...] = m_new
        vo_ref[...] = v_new

    blk = pl.BlockSpec((tr, cols), lambda i: (i, 0))
    shp = jax.ShapeDtypeStruct((rows, cols), F32)
    return pl.pallas_call(body, grid=(rows // tr,), in_specs=[blk] * 4, out_specs=[blk] * 3,
                          out_shape=[shp] * 3, name=name, compiler_params=_params())(w, g, m, v)


def _adamw_halves(w, g2, m, v, name):
    rows, cols = w.shape
    half = cols // 2
    tr = _row_tile(rows)

    def body(w_ref, g_ref, m_ref, v_ref, go_ref, d_ref, mo_ref, vo_ref):
        gv = g_ref[...]
        go_ref[...] = gv
        m_new = ADAM_B1 * m_ref[...] + (1.0 - ADAM_B1) * gv
        v_new = ADAM_B2 * v_ref[...] + (1.0 - ADAM_B2) * jnp.square(gv)
        m_hat = m_new / (1.0 - ADAM_B1 ** ADAM_STEP)
        v_hat = v_new / (1.0 - ADAM_B2 ** ADAM_STEP)
        d_ref[...] = -ADAM_LR * (m_hat / (jnp.sqrt(v_hat) + ADAM_EPS) + ADAM_WD * w_ref[...])
        mo_ref[...] = m_new
        vo_ref[...] = v_new

    blk = pl.BlockSpec((tr, half), lambda hf, i: (i, hf))
    gsp = pl.BlockSpec((None, tr, half), lambda hf, i: (hf, i, 0))
    shp = jax.ShapeDtypeStruct((rows, cols), F32)
    return pl.pallas_call(body, grid=(2, rows // tr), in_specs=[blk, gsp, blk, blk], out_specs=[blk] * 4,
                          out_shape=[shp] * 4, name=name, compiler_params=_params())(w, g2, m, v)


SHARD_W = IN_WIDTH // N_CHIPS


def _half_major(a):
    r, c = a.shape
    return a.reshape(N_CHIPS, 2, r // N_CHIPS // 2, c).transpose(1, 0, 2, 3)


def kernel(x, mem, g_pre, w_in, w_conv, attn_sink, g_mem, w_mem_kv, w_up_a, w_up_b, w_up_m, w_out, g_post, loss_target, m_g_pre, m_w_in, m_w_conv, m_attn_sink, m_g_mem, m_w_mem_kv, m_w_up_a, m_w_up_b, m_w_up_m, m_w_out, m_g_post, v_g_pre, v_w_in, v_w_conv, v_attn_sink, v_g_mem, v_w_mem_kv, v_w_up_a, v_w_up_b, v_w_up_m, v_w_out, v_g_post):
    xi, yi, ci = _position()
    chip = 2 * xi + yi
    where = jnp.stack([chip, ci, N_CHIPS - 1 - chip]).astype(jnp.int32)

    own = [w_in[0].T.astype(BF16), w_mem_kv[0].astype(BF16),
           jnp.concatenate([w_up_a[0], w_up_b[0], w_up_m[0]], axis=0).astype(BF16), w_out[0].astype(BF16)]
    own_conv = jnp.pad(w_conv[0], ((0, 5), (0, 0)))

    def pieces(mine, got):
        got = lax.dynamic_update_slice_in_dim(got, mine[None], chip, axis=0)
        return [got[j] for j in range(N_CHIPS)]

    diag = N_CHIPS - 1 - chip
    diag_blocks = SHARD_BLOCKS + 1
    (h, h_t), (got_near, got_conv) = _rmsnorm_fwd(x[0], g_pre, name="pre_norm", transposed=True,
                                                  carry=_gather_weights(own[:1], own_conv, relations=(0, 1)))
    w_near = lax.dynamic_update_slice_in_dim(got_near, own[0][None], chip, axis=0).reshape(IN_WIDTH, D_MODEL)
    proj, (got_far, *got_rest) = _proj(
        h, w_near, n_blocks=N_IN_BLOCKS - diag_blocks, where=where, name="proj_near",
        block_of=lambda i, w: i + diag_blocks * (i >= SHARD_BLOCKS * w[2]).astype(jnp.int32),
        carry=_join(_gather_weights(own[:1], relations=(2,)), _gather_weights(own[1:], relations=(0, 1))))
    far = lax.dynamic_index_in_dim(got_far, diag, 0, keepdims=False)
    proj, gathered = _proj_far(h, w_near, far, where, into=proj,
                               carry=_gather_weights(own[1:], relations=(2,), into=got_rest))
    w_kv_full = jnp.concatenate(pieces(own[1], gathered[0]), axis=0)
    up_pieces = pieces(own[2], gathered[1])
    w_up_full = jnp.stack([jnp.concatenate([p[k * A_WIDTH:(k + 1) * A_WIDTH] for p in up_pieces], axis=1)
                           for k in range(3)])
    w_out_full = jnp.concatenate(pieces(own[3], gathered[2]), axis=0)
    w_conv_full = jnp.concatenate([p[:3] for p in pieces(own_conv, got_conv)], axis=1)

    g = _forward_backward(x[0], mem[0], loss_target[0], proj, w_conv_full, attn_sink, g_mem, w_kv_full, w_up_full,
                          w_out_full, g_post)

    half_rows = D_MODEL // 2
    up_parts = (g["w_up"].reshape(3, A_WIDTH, N_CHIPS, D_MODEL // N_CHIPS).transpose(2, 0, 1, 3)
                .reshape(N_CHIPS, 2, 3 * A_WIDTH // 2, D_MODEL // N_CHIPS).transpose(1, 0, 2, 3)).astype(BF16)
    small_parts = [_half_major(g["w_kv"]).astype(BF16), up_parts, _half_major(g["w_out"]).astype(BF16)]

    def dw_in_half(half_of, name, carry):
        dw, carried = _dw_in_t(g["dproj"], h_t, half_of=half_of, where=where, name=name, carry=carry)
        return dw.reshape(N_CHIPS, SHARD_W, half_rows), carried

    def pick(parts, hf):
        return [lax.dynamic_index_in_dim(p, hf, 0, keepdims=False) for p in parts]

    small_names = ["w_kv", "w_up", "w_out"]
    recv_small = _run_carry(_pair_exchange(pick(small_parts, 1 - ci)), "pair_exchange_small")
    sums_small = [_pair_add(k, r, "pair_add_" + nm)
                  for k, r, nm in zip(pick(small_parts, ci), recv_small, small_names)]
    dw_send, recv3_small = dw_in_half(lambda w: 1 - w[1], "dw_in_send", _chip_exchange(sums_small))
    dw_keep, (recv_in,) = dw_in_half(lambda w: w[1], "dw_in_keep", _pair_exchange([dw_send]))
    sum_in = _pair_add(dw_keep, recv_in, "pair_add_w_in")
    d_h, (recv3_in,) = _d_h(g["dproj"], w_near, far, where, carry=_chip_exchange([sum_in]))
    pairs = [_chip_add(s, r, where, "chip_add_" + nm)
             for s, r, nm in zip([sum_in] + sums_small, [recv3_in] + recv3_small, ["w_in"] + small_names)]
    grad_x, dg_pre = _rmsnorm_bwd(d_h, x[0], g_pre, g["dy"], name="pre_norm_bwd")

    zeros512 = jnp.zeros((1, D_MODEL - A_WIDTH), F32)
    conv_rows = [jnp.concatenate([g["w_conv"][k:k + 1], zeros512], axis=1) for k in range(3)]
    sink_row = jnp.pad(g["sink"][:, 0].reshape(1, N_Q_HEADS), ((0, 0), (0, D_MODEL - N_Q_HEADS)))
    loss_row = jnp.pad(g["loss"], ((0, 0), (0, D_MODEL - LANES)))
    pack = jnp.concatenate([dg_pre, g["g_mem"], g["g_post"]] + conv_rows + [sink_row, loss_row], axis=0)
    red, full = _small_allreduce(pack, _pair_share(pairs))
    loss = red[7, 0]
    small_grads = dict(
        g_pre=red[0:1], g_mem=red[1:2], g_post=red[2:3], attn_sink=red[6:7, :N_Q_HEADS],
        w_conv=lax.dynamic_slice(red[3:6, :A_WIDTH], (0, chip * LANES), (3, LANES)))

    gw_up = full[2].reshape(3, A_WIDTH, D_MODEL // N_CHIPS)
    grads = dict(small_grads, w_mem_kv=full[1].reshape(D_MODEL // N_CHIPS, 2 * MEM_WIDTH),
                 w_up_a=gw_up[0], w_up_b=gw_up[1], w_up_m=gw_up[2],
                 w_out=full[3].reshape(D_MODEL // N_CHIPS, D_MODEL))

    weights = dict(g_pre=g_pre, w_in=w_in, w_conv=w_conv, attn_sink=attn_sink, g_mem=g_mem, w_mem_kv=w_mem_kv,
                   w_up_a=w_up_a, w_up_b=w_up_b, w_up_m=w_up_m, w_out=w_out, g_post=g_post)
    m_in = dict(g_pre=m_g_pre, w_in=m_w_in, w_conv=m_w_conv, attn_sink=m_attn_sink, g_mem=m_g_mem,
                w_mem_kv=m_w_mem_kv, w_up_a=m_w_up_a, w_up_b=m_w_up_b, w_up_m=m_w_up_m, w_out=m_w_out,
                g_post=m_g_post)
    v_in = dict(g_pre=v_g_pre, w_in=v_w_in, w_conv=v_w_conv, attn_sink=v_attn_sink, g_mem=v_g_mem,
                w_mem_kv=v_w_mem_kv, w_up_a=v_w_up_a, w_up_b=v_w_up_b, w_up_m=v_w_up_m, w_out=v_w_out,
                g_post=v_g_post)
    out_g, out_d, out_m, out_v = [], [], [], []
    for nm in ("g_pre", "w_in", "w_conv", "attn_sink", "g_mem", "w_mem_kv", "w_up_a", "w_up_b", "w_up_m", "w_out",
               "g_post"):
        shape = weights[nm].shape
        if nm == "w_in":
            results = _adamw_halves(w_in[0].T, full[0], m_w_in[0].T, v_w_in[0].T, "adamw_w_in")
            for out, t in zip((out_g, out_d, out_m, out_v), results):
                out.append(t.T.reshape(shape))
            continue
        two_d = shape[-2:]
        gr = grads[nm].reshape(two_d)
        d, m_new, v_new = _adamw(weights[nm].reshape(two_d), gr, m_in[nm].reshape(two_d), v_in[nm].reshape(two_d),
                                 "adamw_" + nm)
        out_g.append(gr.reshape(shape))
        out_d.append(d.reshape(shape))
        out_m.append(m_new.reshape(shape))
        out_v.append(v_new.reshape(shape))
    return (loss, grad_x.reshape(x.shape), *out_g, *out_d, *out_m, *out_v)
```

```python
import functools

import jax
import jax.numpy as jnp
from jax import lax
from jax.experimental import pallas as pl
from jax.experimental.pallas import tpu as pltpu

F32 = jnp.float32
BF16 = jnp.bfloat16
MESH = pl.DeviceIdType.MESH

D_MODEL = 1024
EPS = 1e-6
A_WIDTH = 512
HEAD_DIM = 64
N_Q_HEADS = 8
WINDOW_BLOCK = 128
KV_PAD = 512
ROPE_THETA = 500000.0
ROT_DIM = 16
MEM_HEADS = 4
MEM_HEAD_DIM = 128
MEM_WIDTH = 512
IN_WIDTH = 7424
N_CHIPS = 4
LANES = 128
HALF_LANES = 64

PERM_SEGS = ((0, 2560), (2816, 3328), (4352, 7424), (3328, 4352), (2560, 2816))
UNPERM_SEGS = ((0, 2560), (7168, 7424), (2560, 3072), (6144, 7168), (3072, 6144))
COL_A, W_A = 0, 2048
COL_B, W_B = 2, 1024
COL_G, W_G = 1, 3072
COL_M, W_M = 6, 1024
COL_KV, W_KV = 28, 256

ADAM_LR = 0.001
ADAM_B1 = 0.9
ADAM_B2 = 0.999
ADAM_EPS = 1e-08
ADAM_WD = 0.01
ADAM_STEP = 10

VMEM_LIGHT_BYTES = 48 * 1024 * 1024
VMEM_HEAVY_BYTES = 48 * 1024 * 1024


_HBM = pl.BlockSpec(memory_space=pltpu.HBM)


def _params(heavy=False):
    return pltpu.CompilerParams(vmem_limit_bytes=VMEM_HEAVY_BYTES if heavy else VMEM_LIGHT_BYTES)


def _sigmoid(v):
    return jax.nn.sigmoid(v)


_DIMS = {"nn": (((1,), (0,)), ((), ())), "nt": (((1,), (1,)), ((), ())), "tn": (((0,), (0,)), ((), ()))}


class _Carry:
    def __init__(self, ins, out_shapes, sems, start, finish, aliases=None):
        self.ins, self.out_shapes, self.sems = list(ins), list(out_shapes), list(sems)
        self.start, self.finish, self.aliases = start, finish, dict(aliases or {})


def _join(*carries):
    def split(seq, counts):
        pos, parts = 0, []
        for n in counts:
            parts.append(seq[pos:pos + n])
            pos += n
        return parts

    n_in = [len(c.ins) for c in carries]
    n_out = [len(c.out_shapes) for c in carries]
    n_sem = [len(c.sems) for c in carries]

    def run(which):
        def go(ins, outs, sems):
            for c, i, o, sm in zip(carries, split(ins, n_in), split(outs, n_out), split(sems, n_sem)):
                getattr(c, which)(i, o, sm)
        return go

    aliases = {}
    for k, c in enumerate(carries):
        aliases.update({sum(n_in[:k]) + i: sum(n_out[:k]) + o for i, o in c.aliases.items()})
    return _Carry([a for c in carries for a in c.ins], [sh for c in carries for sh in c.out_shapes],
                  [sm for c in carries for sm in c.sems], run("start"), run("finish"), aliases)


def _carried_call(body, carry, *, grid, in_specs, out_specs, out_shape, scratch, operands, name, prefetch=None,
                  aliases=None, heavy=False):
    n_in, n_out, n_scr = len(in_specs), len(out_specs), len(scratch)
    c_in = len(carry.ins) if carry else 0
    c_out = len(carry.out_shapes) if carry else 0
    n_pre = 0 if prefetch is None else 1
    steps = 1
    for g in grid:
        steps *= g

    def wrapped(*refs):
        refs = refs[n_pre:]
        ins, cins = refs[:n_in], refs[n_in:n_in + c_in]
        outs = refs[n_in + c_in:n_in + c_in + n_out]
        couts = refs[n_in + c_in + n_out:n_in + c_in + n_out + c_out]
        rest = refs[n_in + c_in + n_out + c_out:]
        scr, sems = rest[:n_scr], rest[n_scr:]
        if carry:
            step = pl.program_id(0)
            for ax in range(1, len(grid)):
                step = step * grid[ax] + pl.program_id(ax)

            @pl.when(step == 0)
            def _():
                carry.start(cins, couts, sems)

        body(ins, outs, scr)
        if carry:
            @pl.when(step == steps - 1)
            def _():
                carry.finish(cins, couts, sems)

    all_aliases = {n_pre + i: o for i, o in (aliases or {}).items()}
    if carry:
        all_aliases.update({n_pre + n_in + i: n_out + o for i, o in carry.aliases.items()})
    all_in = list(in_specs) + [_HBM] * c_in
    all_out = list(out_specs) + [_HBM] * c_out
    all_scratch = list(scratch) + (carry.sems if carry else [])
    if n_pre:
        spec = dict(grid_spec=pltpu.PrefetchScalarGridSpec(num_scalar_prefetch=1, grid=grid, in_specs=all_in,
                                                           out_specs=all_out, scratch_shapes=all_scratch))
        pre = (prefetch,)
    else:
        spec = dict(grid=grid, in_specs=all_in, out_specs=all_out, scratch_shapes=all_scratch)
        pre = ()
    results = pl.pallas_call(
        wrapped, out_shape=list(out_shape) + (carry.out_shapes if carry else []), input_output_aliases=all_aliases,
        name=name, compiler_params=_params(heavy), **spec)(*pre, *operands, *(carry.ins if carry else []))
    return list(results[:n_out]), list(results[n_out:])


def _matmul(a, b, *, mode, out_dtype, tm, tn, tk, name, j_outer=False, carry=None):
    if mode == "nn":
        (m, k), (_, n) = a.shape, b.shape
    elif mode == "nt":
        (m, k), (n, _) = a.shape, b.shape
    else:
        (k, m), (_, n) = a.shape, b.shape
    tm, tn, tk = min(tm, m), min(tn, n), min(tk, k)
    assert m % tm == 0 and n % tn == 0 and k % tk == 0
    ni, nj, nk = m // tm, n // tn, k // tk
    dims = _DIMS[mode]

    def ij(g0, g1):
        return (g1, g0) if j_outer else (g0, g1)

    if mode == "nn":
        a_spec = pl.BlockSpec((tm, tk), lambda g0, g1, kk: (ij(g0, g1)[0], kk))
        b_spec = pl.BlockSpec((tk, tn), lambda g0, g1, kk: (kk, ij(g0, g1)[1]))
    elif mode == "nt":
        a_spec = pl.BlockSpec((tm, tk), lambda g0, g1, kk: (ij(g0, g1)[0], kk))
        b_spec = pl.BlockSpec((tn, tk), lambda g0, g1, kk: (ij(g0, g1)[1], kk))
    else:
        a_spec = pl.BlockSpec((tk, tm), lambda g0, g1, kk: (kk, ij(g0, g1)[0]))
        b_spec = pl.BlockSpec((tk, tn), lambda g0, g1, kk: (kk, ij(g0, g1)[1]))
    o_spec = pl.BlockSpec((tm, tn), lambda g0, g1, kk: ij(g0, g1))

    def part(a_ref, b_ref):
        return lax.dot_general(a_ref[...].astype(BF16), b_ref[...].astype(BF16), dims,
                               preferred_element_type=F32)

    if nk == 1:
        def body(ins, outs, scr):
            outs[0][...] = part(*ins).astype(out_dtype)
        scratch = []
    else:
        def body(ins, outs, scr):
            kk = pl.program_id(2)
            acc_ref = scr[0]

            @pl.when(kk == 0)
            def _():
                acc_ref[...] = part(*ins)

            @pl.when(kk > 0)
            def _():
                acc_ref[...] += part(*ins)

            @pl.when(kk == nk - 1)
            def _():
                outs[0][...] = acc_ref[...].astype(out_dtype)
        scratch = [pltpu.VMEM((tm, tn), F32)]

    grid = (nj, ni, nk) if j_outer else (ni, nj, nk)
    (out,), carried = _carried_call(
        body, carry, grid=grid, in_specs=[a_spec, b_spec], out_specs=[o_spec],
        out_shape=[jax.ShapeDtypeStruct((m, n), out_dtype)], scratch=scratch, operands=(a, b), name=name)
    return (out, carried) if carry else out


IN_BLOCK = 256
N_IN_BLOCKS = IN_WIDTH // IN_BLOCK
SHARD_BLOCKS = (IN_WIDTH // N_CHIPS) // IN_BLOCK
BLOCK_RUNS = tuple((a // IN_BLOCK, sum(d - c for c, d in PERM_SEGS[:k]) // IN_BLOCK, (b - a) // IN_BLOCK)
                   for k, (a, b) in enumerate(PERM_SEGS))


def _perm_block(r):
    p = r
    for ref0, perm0, n in BLOCK_RUNS:
        p = jnp.where((r >= ref0) & (r < ref0 + n), r - ref0 + perm0, p)
    return p


def _proj(h, w_t, *, n_blocks, block_of, where, name, carry=None):
    s, d = h.shape

    def body(ins, outs, scr):
        outs[0][...] = lax.dot_general(ins[0][...], ins[1][...], _DIMS["nt"], preferred_element_type=F32)

    (proj,), carried = _carried_call(
        body, carry, grid=(n_blocks,),
        in_specs=[pl.BlockSpec((s, d), lambda i, w: (0, 0)), pl.BlockSpec((IN_BLOCK, d), lambda i, w: (block_of(i, w), 0))],
        out_specs=[pl.BlockSpec((s, IN_BLOCK), lambda i, w: (0, _perm_block(block_of(i, w))))],
        out_shape=[jax.ShapeDtypeStruct((s, IN_WIDTH), F32)], scratch=[], operands=(h, w_t), name=name,
        prefetch=where)
    return (proj, carried) if carry else proj


def _proj_far(h, w_near, far, where, *, into, carry=None):
    s, d = h.shape
    n_blocks = SHARD_BLOCKS + 1
    lead = IN_WIDTH // N_CHIPS - SHARD_BLOCKS * IN_BLOCK

    def body(ins, outs, scr):
        where_ref, h_ref, w_hbm, far_hbm, _ = ins
        win, sem = scr
        i = pl.program_id(0)

        @pl.when(i == 0)
        def _():
            dg = where_ref[2]
            rows = pl.ds(pl.multiple_of(dg * (SHARD_BLOCKS * IN_BLOCK), IN_BLOCK), n_blocks * IN_BLOCK)
            window = pltpu.make_async_copy(w_hbm.at[rows], win, sem)
            window.start()
            window.wait()
            shard = pltpu.make_async_copy(far_hbm, win.at[pl.ds(pl.multiple_of(dg * lead, BF16_SUBLANES), SHARD_W)], sem)
            shard.start()
            shard.wait()

        blk = win[pl.ds(pl.multiple_of(i * IN_BLOCK, IN_BLOCK), IN_BLOCK), :]
        outs[0][...] = lax.dot_general(h_ref[...], blk, _DIMS["nt"], preferred_element_type=F32)

    anysp = pl.BlockSpec(memory_space=pl.ANY)
    (proj,), carried = _carried_call(
        body, carry, grid=(n_blocks,),
        in_specs=[pl.BlockSpec(memory_space=pltpu.SMEM), pl.BlockSpec((s, d), lambda i, w: (0, 0)), anysp, anysp, anysp],
        out_specs=[pl.BlockSpec((s, IN_BLOCK), lambda i, w: (0, _perm_block(i + SHARD_BLOCKS * w[2])))],
        out_shape=[jax.ShapeDtypeStruct((s, IN_WIDTH), F32)],
        scratch=[pltpu.VMEM((n_blocks * IN_BLOCK, d), BF16), pltpu.SemaphoreType.DMA],
        operands=(where, h, w_near, far, into), name="proj_far", prefetch=where, aliases={4: 0})
    return (proj, carried) if carry else proj


def _dw_in_t(dproj, h_t, *, half_of, where, name, carry=None):
    d, s = h_t.shape
    c = d // 2

    def body(ins, outs, scr):
        outs[0][...] = lax.dot_general(ins[1][...], ins[0][...], _DIMS["nn"], preferred_element_type=F32).T.astype(BF16)

    (dw,), carried = _carried_call(
        body, carry, grid=(N_IN_BLOCKS,),
        in_specs=[pl.BlockSpec((s, IN_BLOCK), lambda r, w: (0, _perm_block(r))),
                  pl.BlockSpec((c, s), lambda r, w: (half_of(w), 0))],
        out_specs=[pl.BlockSpec((IN_BLOCK, c), lambda r, w: (r, 0))],
        out_shape=[jax.ShapeDtypeStruct((IN_WIDTH, c), BF16)], scratch=[], operands=(dproj, h_t), name=name,
        prefetch=where)
    return (dw, carried) if carry else dw


def _d_h(dproj, w_near, far, where, *, carry=None):
    s = dproj.shape[0]
    d = w_near.shape[1]
    tm = min(s, 256)

    def body(ins, outs, scr):
        where_ref, a_ref, w_hbm, far_hbm = ins
        w_ref, sem = scr

        @pl.when(pl.program_id(0) == 0)
        def _():
            whole = pltpu.make_async_copy(w_hbm, w_ref, sem)
            whole.start()
            whole.wait()
            rows = pl.ds(pl.multiple_of(where_ref[2] * SHARD_W, BF16_SUBLANES), SHARD_W)
            part = pltpu.make_async_copy(far_hbm, w_ref.at[rows], sem)
            part.start()
            part.wait()

        acc = None
        for ref0, perm0, n in BLOCK_RUNS:
            term = jnp.dot(a_ref[:, perm0 * IN_BLOCK:(perm0 + n) * IN_BLOCK],
                           w_ref[ref0 * IN_BLOCK:(ref0 + n) * IN_BLOCK, :], preferred_element_type=F32)
            acc = term if acc is None else acc + term
        outs[0][...] = acc

    anysp = pl.BlockSpec(memory_space=pl.ANY)
    (dh,), carried = _carried_call(
        body, carry, grid=(s // tm,),
        in_specs=[pl.BlockSpec(memory_space=pltpu.SMEM), pl.BlockSpec((tm, IN_WIDTH), lambda i: (i, 0)), anysp, anysp],
        out_specs=[pl.BlockSpec((tm, d), lambda i: (i, 0))],
        out_shape=[jax.ShapeDtypeStruct((s, d), F32)],
        scratch=[pltpu.VMEM((IN_WIDTH, d), BF16), pltpu.SemaphoreType.DMA],
        operands=(where, dproj, w_near, far), name="d_h", heavy=True)
    return (dh, carried) if carry else dh


def _rmsnorm_fwd(x, g, *, name, transposed=False, carry=None):
    s, d = x.shape
    ts = min(512, s)

    def body(ins, outs, scr):
        xv = ins[0][...]
        r = lax.rsqrt(jnp.mean(xv * xv, axis=-1, keepdims=True) + EPS)
        hv = (xv * r) * ins[1][...]
        outs[0][...] = hv.astype(BF16)
        if transposed:
            outs[1][...] = hv.T.astype(BF16)

    out_specs = [pl.BlockSpec((ts, d), lambda i: (i, 0))]
    out_shape = [jax.ShapeDtypeStruct((s, d), BF16)]
    if transposed:
        out_specs.append(pl.BlockSpec((d, ts), lambda i: (0, i)))
        out_shape.append(jax.ShapeDtypeStruct((d, s), BF16))
    outs, carried = _carried_call(
        body, carry, grid=(s // ts,),
        in_specs=[pl.BlockSpec((ts, d), lambda i: (i, 0)), pl.BlockSpec((1, d), lambda i: (0, 0))],
        out_specs=out_specs, out_shape=out_shape, scratch=[], operands=(x, g), name=name)
    result = tuple(outs) if transposed else outs[0]
    return (result, carried) if carry else result


def _rmsnorm_bwd(dh, x, g, res, *, name, carry=None):
    s, d = x.shape
    ts = min(256, s)

    def body(ins, outs, scr):
        dh_ref, x_ref, g_ref, res_ref = ins
        dx_ref, dg_ref = outs
        xv = x_ref[...]
        r = lax.rsqrt(jnp.mean(xv * xv, axis=-1, keepdims=True) + EPS)
        xh = xv * r
        dhv = dh_ref[...]
        part = jnp.sum(dhv * xh, axis=0, keepdims=True)

        @pl.when(pl.program_id(0) == 0)
        def _():
            dg_ref[...] = part

        @pl.when(pl.program_id(0) > 0)
        def _():
            dg_ref[...] += part

        dxh = dhv * g_ref[...]
        dx_ref[...] = res_ref[...] + r * (dxh - xh * jnp.mean(dxh * xh, axis=-1, keepdims=True))

    row = pl.BlockSpec((ts, d), lambda i: (i, 0))
    vec = pl.BlockSpec((1, d), lambda i: (0, 0))
    outs, carried = _carried_call(
        body, carry, grid=(s // ts,), in_specs=[row, row, vec, row], out_specs=[row, vec],
        out_shape=[jax.ShapeDtypeStruct((s, d), F32), jax.ShapeDtypeStruct((1, d), F32)],
        scratch=[], operands=(dh, x, g, res), name=name)
    return (*outs, carried) if carry else tuple(outs)


MID_TILE = 256


def _gated_branches(y_refs, wup_ref, gl):
    d = D_MODEL
    us = [jnp.dot(y_refs[k][...], wup_ref[k], preferred_element_type=F32) for k in range(3)]
    sg = [_sigmoid(gl[:, k * d:(k + 1) * d]) for k in range(3)]
    return us, sg


def _mid_fwd(ya, yb, ym, proj, x, tgt, w_up, w_out, g_post):
    s, d = x.shape
    ts = MID_TILE

    def body(ya_ref, yb_ref, ym_ref, g_ref, x_ref, t_ref, wup_ref, wout_ref, gp_ref,
             m_ref, do_ref, dy_ref, dg_ref, loss_ref):
        us, sg = _gated_branches((ya_ref, yb_ref, ym_ref), wup_ref, g_ref[...])
        merged = (sg[0] * us[0] + sg[1] * us[1] + sg[2] * us[2]).astype(BF16)
        m_ref[...] = merged
        ov = jnp.dot(merged, wout_ref[...], preferred_element_type=F32)
        r = lax.rsqrt(jnp.mean(ov * ov, axis=-1, keepdims=True) + EPS)
        nh = ov * r
        gv = gp_ref[...]
        e = (x_ref[...] + nh * gv) - t_ref[...]
        lpart = 0.5 * jnp.sum(jnp.mean(e * e, axis=-1, keepdims=True), axis=0, keepdims=True)
        dy = e * (1.0 / d)
        dgp = jnp.sum(dy * nh, axis=0, keepdims=True)

        @pl.when(pl.program_id(0) == 0)
        def _():
            dg_ref[...] = dgp
            loss_ref[...] = jnp.broadcast_to(lpart, loss_ref.shape)

        @pl.when(pl.program_id(0) > 0)
        def _():
            dg_ref[...] += dgp
            loss_ref[...] += jnp.broadcast_to(lpart, loss_ref.shape)

        dn = dy * gv
        dy_ref[...] = dy
        do_ref[...] = (r * (dn - nh * jnp.mean(dn * nh, axis=-1, keepdims=True))).astype(BF16)

    row = pl.BlockSpec((ts, d), lambda i: (i, 0))
    ysp = pl.BlockSpec((ts, A_WIDTH), lambda i: (i, 0))
    vec = pl.BlockSpec((1, d), lambda i: (0, 0))
    return pl.pallas_call(
        body, grid=(s // ts,),
        in_specs=[ysp, ysp, ysp, pl.BlockSpec((ts, W_G), lambda i: (i, COL_G)), row, row,
                  pl.BlockSpec((3, A_WIDTH, d), lambda i: (0, 0, 0)), pl.BlockSpec((d, d), lambda i: (0, 0)), vec],
        out_specs=[row, row, row, vec, pl.BlockSpec((1, LANES), lambda i: (0, 0))],
        out_shape=[jax.ShapeDtypeStruct((s, d), BF16), jax.ShapeDtypeStruct((s, d), BF16),
                   jax.ShapeDtypeStruct((s, d), F32), jax.ShapeDtypeStruct((1, d), F32),
                   jax.ShapeDtypeStruct((1, LANES), F32)],
        name="mid_fwd", compiler_params=_params(heavy=True))(ya, yb, ym, proj, x, tgt, w_up, w_out, g_post)


def _mid_bwd(d_out, merged, ya, yb, ym, proj, w_up, w_out):
    s, d = merged.shape
    ts = MID_TILE
    last = s // ts - 1

    def body(do_ref, m_ref, ya_ref, yb_ref, ym_ref, g_ref, wup_ref, wout_ref,
             dp_ref, dya_ref, dyb_ref, dym_ref, dwup_hbm, dwout_hbm, dwup_acc, dwout_acc):
        i = pl.program_id(0)

        @pl.when(i == 0)
        def _():
            dwup_acc[...] = jnp.zeros_like(dwup_acc)
            dwout_acc[...] = jnp.zeros_like(dwout_acc)

        y_refs = (ya_ref, yb_ref, ym_ref)
        us, sg = _gated_branches(y_refs, wup_ref, g_ref[...])
        dov = do_ref[...]
        dwout_acc[...] += lax.dot_general(m_ref[...], dov, _DIMS["tn"], preferred_element_type=F32)
        dm = lax.dot_general(dov, wout_ref[...], _DIMS["nt"], preferred_element_type=F32)
        for k, dy_ref in enumerate((dya_ref, dyb_ref, dym_ref)):
            dp_ref[:, k * d:(k + 1) * d] = ((dm * us[k]) * (sg[k] * (1.0 - sg[k]))).astype(BF16)
            du = (sg[k] * dm).astype(BF16)
            dy_ref[...] = lax.dot_general(du, wup_ref[k], _DIMS["nt"], preferred_element_type=F32)
            dwup_acc[k] += lax.dot_general(y_refs[k][...], du, _DIMS["tn"], preferred_element_type=F32)

        @pl.when(i == last)
        def _():
            pltpu.sync_copy(dwup_acc, dwup_hbm)
            pltpu.sync_copy(dwout_acc, dwout_hbm)

    row = pl.BlockSpec((ts, d), lambda i: (i, 0))
    ysp = pl.BlockSpec((ts, A_WIDTH), lambda i: (i, 0))
    gsp = pl.BlockSpec((ts, W_G), lambda i: (i, COL_G))
    anysp = pl.BlockSpec(memory_space=pl.ANY)
    yshape = jax.ShapeDtypeStruct((s, A_WIDTH), F32)
    return pl.pallas_call(
        body, grid=(s // ts,),
        in_specs=[row, row, ysp, ysp, ysp, gsp, pl.BlockSpec((3, A_WIDTH, d), lambda i: (0, 0, 0)),
                  pl.BlockSpec((d, d), lambda i: (0, 0))],
        out_specs=[gsp, ysp, ysp, ysp, anysp, anysp],
        out_shape=[jax.ShapeDtypeStruct((s, IN_WIDTH), BF16), yshape, yshape, yshape,
                   jax.ShapeDtypeStruct((3, A_WIDTH, d), F32), jax.ShapeDtypeStruct((d, d), F32)],
        scratch_shapes=[pltpu.VMEM((3, A_WIDTH, d), F32), pltpu.VMEM((d, d), F32)],
        name="mid_bwd", compiler_params=_params(heavy=True))(d_out, merged, ya, yb, ym, proj, w_up, w_out)


def _conv_core(blk, prev, nxt, w, i, last, ts):
    c = A_WIDTH
    ab, ac, ax, az = blk[:, :c], blk[:, c:2 * c], blk[:, 2 * c:3 * c], blk[:, 3 * c:]
    cu = ac * ax
    cu_prev = (prev[7:8, c:2 * c] * prev[7:8, 2 * c:3 * c]) * jnp.where(i > 0, 1.0, 0.0)
    cu_next = (nxt[0:1, c:2 * c] * nxt[0:1, 2 * c:3 * c]) * jnp.where(i < last, 1.0, 0.0)
    row = lax.broadcasted_iota(jnp.int32, (ts, c), 0)
    cm1 = jnp.where(row == 0, cu_prev, pltpu.roll(cu, 1, 0))
    cp1 = jnp.where(row == ts - 1, cu_next, pltpu.roll(cu, ts - 1, 0))
    yc = cm1 * w[0:1] + cu * w[1:2] + cp1 * w[2:3]
    return ab, ac, ax, az, cu, cm1, cp1, yc, row


def _halo_specs(ts, width, col, nblk8):
    prev = pl.BlockSpec((8, width), lambda i: (jnp.maximum(i * (ts // 8) - 1, 0), col))
    nxt = pl.BlockSpec((8, width), lambda i: (jnp.minimum((i + 1) * (ts // 8), nblk8 - 1), col))
    return prev, nxt


def _conv_fwd(proj, w_conv):
    s = proj.shape[0]
    ts = 256
    last = s // ts - 1

    def body(a_ref, ap_ref, an_ref, w_ref, ya_ref):
        i = pl.program_id(0)
        ab, _, _, az, _, _, _, yc, _ = _conv_core(a_ref[...], ap_ref[...], an_ref[...], w_ref[...], i, last, ts)
        ya_ref[...] = ((ab * yc) * (az * _sigmoid(az))).astype(BF16)

    prev, nxt = _halo_specs(ts, W_A, COL_A, s // 8)
    return pl.pallas_call(
        body, grid=(s // ts,),
        in_specs=[pl.BlockSpec((ts, W_A), lambda i: (i, COL_A)), prev, nxt,
                  pl.BlockSpec((3, A_WIDTH), lambda i: (0, 0))],
        out_specs=pl.BlockSpec((ts, A_WIDTH), lambda i: (i, 0)),
        out_shape=jax.ShapeDtypeStruct((s, A_WIDTH), BF16), name="conv_fwd",
        compiler_params=_params())(proj, proj, proj, w_conv)


def _conv_bwd(proj, w_conv, dya, dproj):
    s = proj.shape[0]
    ts = 256
    last = s // ts - 1
    c = A_WIDTH

    def body(a_ref, ap_ref, an_ref, w_ref, d_ref, dp_ref, dn_ref, _, dproj_ref, dw_ref):
        i = pl.program_id(0)
        w = w_ref[...]
        prev, nxt = ap_ref[...], an_ref[...]
        ab, ac, ax, az, cu, cm1, cp1, yc, row = _conv_core(a_ref[...], prev, nxt, w, i, last, ts)
        sg = _sigmoid(az)
        sz = az * sg
        dya_v = d_ref[...]
        dyc = dya_v * sz * ab
        dproj_ref[:, :c] = (dya_v * sz * yc).astype(BF16)
        dproj_ref[:, 3 * c:] = (dya_v * (ab * yc) * (sg * (1.0 + az * (1.0 - sg)))).astype(BF16)

        def halo_dyc(a_row, d_row):
            azr = a_row[:, 3 * c:]
            return d_row * (azr * _sigmoid(azr)) * a_row[:, :c]

        dyc_prev = halo_dyc(prev[7:8], dp_ref[...][7:8]) * jnp.where(i > 0, 1.0, 0.0)
        dyc_next = halo_dyc(nxt[0:1], dn_ref[...][0:1]) * jnp.where(i < last, 1.0, 0.0)
        dyc_m1 = jnp.where(row == 0, dyc_prev, pltpu.roll(dyc, 1, 0))
        dyc_p1 = jnp.where(row == ts - 1, dyc_next, pltpu.roll(dyc, ts - 1, 0))
        dcu = dyc_p1 * w[0:1] + dyc * w[1:2] + dyc_m1 * w[2:3]
        dproj_ref[:, c:2 * c] = (dcu * ax).astype(BF16)
        dproj_ref[:, 2 * c:3 * c] = (dcu * ac).astype(BF16)
        dw = [jnp.sum(dyc * t, axis=0, keepdims=True) for t in (cm1, cu, cp1)]

        @pl.when(i == 0)
        def _():
            for k in range(3):
                dw_ref[k:k + 1, :] = dw[k]

        @pl.when(i > 0)
        def _():
            for k in range(3):
                dw_ref[k:k + 1, :] += dw[k]

    prev, nxt = _halo_specs(ts, W_A, COL_A, s // 8)
    dprev, dnxt = _halo_specs(ts, A_WIDTH, 0, s // 8)
    return pl.pallas_call(
        body, grid=(s // ts,),
        in_specs=[pl.BlockSpec((ts, W_A), lambda i: (i, COL_A)), prev, nxt,
                  pl.BlockSpec((3, A_WIDTH), lambda i: (0, 0)),
                  pl.BlockSpec((ts, A_WIDTH), lambda i: (i, 0)), dprev, dnxt,
                  pl.BlockSpec(memory_space=pl.ANY)],
        out_specs=[pl.BlockSpec((ts, W_A), lambda i: (i, COL_A)), pl.BlockSpec((3, A_WIDTH), lambda i: (0, 0))],
        out_shape=[jax.ShapeDtypeStruct(dproj.shape, BF16), jax.ShapeDtypeStruct((3, A_WIDTH), F32)],
        input_output_aliases={7: 0}, name="conv_bwd",
        compiler_params=_params())(proj, proj, proj, w_conv, dya, dya, dya, dproj)


def _rope_tables(s):
    half = ROT_DIM // 2
    dim = jnp.arange(LANES) % HEAD_DIM
    inv_freq = jnp.power(jnp.float32(ROPE_THETA), -(dim % half).astype(F32) * (2.0 / ROT_DIM))
    ang = jnp.arange(s).astype(F32)[:, None] * inv_freq[None, :]
    cos, sin = jnp.cos(ang), jnp.sin(ang)
    first, second = (dim < half)[None, :], ((dim >= half) & (dim < ROT_DIM))[None, :]
    c = jnp.where(first | second, cos, 1.0)
    s1 = jnp.where(first, -sin, 0.0)
    s2 = jnp.where(second, sin, 0.0)
    return jnp.concatenate([c, s1, s2], axis=1)


def _rope(t, tab):
    return (t * tab[:, :LANES] + pltpu.roll(t, LANES - 8, 1) * tab[:, LANES:2 * LANES]
            + pltpu.roll(t, 8, 1) * tab[:, 2 * LANES:])


def _rope_transpose(dt, tab):
    return (dt * tab[:, :LANES] + pltpu.roll(dt * tab[:, LANES:2 * LANES], 8, 1)
            + pltpu.roll(dt * tab[:, 2 * LANES:], LANES - 8, 1))


def _rope_kv(proj, tab):
    s = proj.shape[0]
    nb = s // KV_PAD

    def body(kv_ref, t_ref, k_ref, v_ref):
        j = pl.program_id(0)
        inside = jnp.where((j > 0) & (j <= nb), 1.0, 0.0)
        kv = kv_ref[...]
        k_ref[...] = (_rope(kv[:, :LANES], t_ref[...]) * inside).astype(BF16)
        v_ref[...] = (kv[:, LANES:] * inside).astype(BF16)

    def src(j):
        return jnp.clip(j - 1, 0, nb - 1)

    o_spec = pl.BlockSpec((KV_PAD, LANES), lambda j: (j, 0))
    shp = jax.ShapeDtypeStruct((s + 2 * KV_PAD, LANES), BF16)
    return pl.pallas_call(
        body, grid=(nb + 2,),
        in_specs=[pl.BlockSpec((KV_PAD, W_KV), lambda j: (src(j), COL_KV)),
                  pl.BlockSpec((KV_PAD, 3 * LANES), lambda j: (src(j), 0))],
        out_specs=[o_spec, o_spec], out_shape=[shp, shp], name="rope_kv",
        compiler_params=_params())(proj, tab)


def _rope_kv_bwd(dkpad, dvpad, tab, dproj):
    s = tab.shape[0]
    nb = s // KV_PAD

    def body(dk_ref, dv_ref, t_ref, _, dp_ref):
        dp_ref[:, :LANES] = _rope_transpose(dk_ref[...], t_ref[...]).astype(BF16)
        dp_ref[:, LANES:] = dv_ref[...].astype(BF16)

    pad_spec = pl.BlockSpec((KV_PAD, LANES), lambda j: (j + 1, 0))
    return pl.pallas_call(
        body, grid=(nb,),
        in_specs=[pad_spec, pad_spec, pl.BlockSpec((KV_PAD, 3 * LANES), lambda j: (j, 0)),
                  pl.BlockSpec(memory_space=pl.ANY)],
        out_specs=pl.BlockSpec((KV_PAD, W_KV), lambda j: (j, COL_KV)),
        out_shape=jax.ShapeDtypeStruct(dproj.shape, BF16), input_output_aliases={3: 0},
        name="rope_kv_bwd", compiler_params=_params())(dkpad, dvpad, tab, dproj)


def _window_start(n):
    return pl.multiple_of((n - 1) * WINDOW_BLOCK + KV_PAD, WINDOW_BLOCK)


def _window_operands(k_ref, v_ref, n, lo):
    start = _window_start(n)
    kw = k_ref[pl.ds(start, 3 * WINDOW_BLOCK), :].astype(F32)
    vw = v_ref[pl.ds(start, 3 * WINDOW_BLOCK), :].astype(F32)
    kr, vr = pltpu.roll(kw, HALF_LANES, 1), pltpu.roll(vw, HALF_LANES, 1)
    k2 = (jnp.where(lo, kw, kr).astype(BF16), jnp.where(lo, kr, kw).astype(BF16))
    v2 = (jnp.where(lo, vw, vr).astype(BF16), jnp.where(lo, vr, vw).astype(BF16))
    return k2, v2


HEADS_PER_GROUP = 4
SWA_FWD_BLOCKS = 1
SWA_BWD_BLOCKS = 2


def _window_bias():
    wb = WINDOW_BLOCK
    qi = lax.broadcasted_iota(jnp.int32, (wb, 3 * wb), 0)
    kj = lax.broadcasted_iota(jnp.int32, (wb, 3 * wb), 1)
    band = (kj >= qi) & (kj <= qi + 2 * wb)
    cases = jnp.stack([band & (kj >= wb), band, band & (kj < 2 * wb)])
    return jnp.where(cases, 0.0, -jnp.inf).astype(F32)


def _block_bias(bias_ref, n, n_blocks):
    case = jnp.where(n == 0, 0, jnp.where(n == n_blocks - 1, 2, 1))
    one = bias_ref[case]
    return jnp.concatenate([one] * HEADS_PER_GROUP, axis=0)


def _stack_heads(pair0, pair1, lo):
    return jnp.concatenate([jnp.where(lo, pair0, 0.0), jnp.where(lo, 0.0, pair0),
                            jnp.where(lo, pair1, 0.0), jnp.where(lo, 0.0, pair1)], axis=0)


def _unstack_pair(stacked, i, lo):
    wb = WINDOW_BLOCK
    return jnp.where(lo, stacked[2 * i * wb:(2 * i + 1) * wb], stacked[(2 * i + 1) * wb:(2 * i + 2) * wb])


def _sink_column(sink_ref, g):
    wb = WINDOW_BLOCK
    return jnp.concatenate([jnp.full((wb, 1), sink_ref[0, HEADS_PER_GROUP * g + i], F32)
                            for i in range(HEADS_PER_GROUP)], axis=0)


def _head_exp(q4, k2g, bias, sink):
    sc = lax.dot_general(q4, k2g, _DIMS["nt"], preferred_element_type=F32) * (HEAD_DIM ** -0.5) + bias
    m = jnp.maximum(jnp.max(sc, axis=1, keepdims=True), sink)
    return jnp.exp(sc - m).astype(BF16), jnp.exp(sink - m)


def _swa_fwd(proj, kpad, vpad, tab, bias, sink, *, carry=None):
    s = proj.shape[0]
    wb = WINDOW_BLOCK

    def body(b_ref, k_ref, v_ref, t_ref, bias_ref, sink_ref, o_ref, y_ref):
        lo = lax.broadcasted_iota(jnp.int32, (wb, LANES), 1) < HALF_LANES
        lo_w = lax.broadcasted_iota(jnp.int32, (3 * wb, LANES), 1) < HALF_LANES
        for sub in range(SWA_FWD_BLOCKS):
            n = pl.program_id(0) * SWA_FWD_BLOCKS + sub
            rows = slice(sub * wb, (sub + 1) * wb)
            k2, v2 = _window_operands(k_ref, v_ref, n, lo_w)
            valid = _block_bias(bias_ref, n, s // wb)
            tab_v = t_ref[rows, :]
            ones = jnp.ones((3 * wb, LANES), BF16)
            for g in range(2):
                qr = [_rope(b_ref[rows, (2 * g + i) * LANES:(2 * g + i + 1) * LANES], tab_v) for i in range(2)]
                q4 = _stack_heads(qr[0], qr[1], lo).astype(BF16)
                e, es = _head_exp(q4, k2[g], valid, _sink_column(sink_ref, g))
                ox = jnp.dot(e, jnp.concatenate([v2[g], ones], axis=1), preferred_element_type=F32)
                o4 = ox[:, :LANES] * (1.0 / (ox[:, LANES:] + es))
                for i in range(2):
                    cols = slice((2 * g + i) * LANES, (2 * g + i + 1) * LANES)
                    op = _unstack_pair(o4, i, lo)
                    o_ref[rows, cols] = op
                    zp = b_ref[rows, A_WIDTH + cols.start:A_WIDTH + cols.stop]
                    y_ref[rows, cols] = (op * (zp * _sigmoid(zp))).astype(BF16)

    tq = SWA_FWD_BLOCKS * wb
    pad_spec = pl.BlockSpec((s + 2 * KV_PAD, LANES), lambda n: (0, 0))
    o_spec = pl.BlockSpec((tq, A_WIDTH), lambda n: (n, 0))
    outs, carried = _carried_call(
        lambda ins, outs, scr: body(*ins, *outs), carry, grid=(s // tq,),
        in_specs=[pl.BlockSpec((tq, W_B), lambda n: (n, COL_B)), pad_spec, pad_spec,
                  pl.BlockSpec((tq, 3 * LANES), lambda n: (n, 0)),
                  pl.BlockSpec(bias.shape, lambda n: (0, 0, 0)), pl.BlockSpec(memory_space=pltpu.SMEM)],
        out_specs=[o_spec, o_spec],
        out_shape=[jax.ShapeDtypeStruct((s, A_WIDTH), F32), jax.ShapeDtypeStruct((s, A_WIDTH), BF16)],
        scratch=[], operands=(proj, kpad, vpad, tab, bias, sink), name="swa_fwd")
    return (*outs, carried) if carry else tuple(outs)


def _swa_bwd(proj, kpad, vpad, tab, bias, sink, o_attn, dyb, dproj):
    s = proj.shape[0]
    wb = WINDOW_BLOCK
    scale = HEAD_DIM ** -0.5

    def body(b_ref, k_ref, v_ref, t_ref, bias_ref, sink_ref, o_ref, dy_ref, _, dp_ref, dk_ref, dv_ref, ds_ref):
        @pl.when(pl.program_id(0) == 0)
        def _():
            dk_ref[...] = jnp.zeros_like(dk_ref)
            dv_ref[...] = jnp.zeros_like(dv_ref)
            ds_ref[...] = jnp.zeros_like(ds_ref)

        lo = lax.broadcasted_iota(jnp.int32, (wb, LANES), 1) < HALF_LANES
        lo_w = lax.broadcasted_iota(jnp.int32, (3 * wb, LANES), 1) < HALF_LANES
        for sub in range(SWA_BWD_BLOCKS):
            n = pl.program_id(0) * SWA_BWD_BLOCKS + sub
            rows = slice(sub * wb, (sub + 1) * wb)
            k2, v2 = _window_operands(k_ref, v_ref, n, lo_w)
            valid = _block_bias(bias_ref, n, s // wb)
            tab_v = t_ref[rows, :]
            ones = jnp.ones((3 * wb, LANES), BF16)
            dks, dvs = [], []
            for g in range(2):
                qr, op, do = [], [], []
                for i in range(2):
                    cols = slice((2 * g + i) * LANES, (2 * g + i + 1) * LANES)
                    zcols = slice(A_WIDTH + cols.start, A_WIDTH + cols.stop)
                    qr.append(_rope(b_ref[rows, cols], tab_v))
                    zp = b_ref[rows, zcols]
                    sg = _sigmoid(zp)
                    op.append(o_ref[rows, cols])
                    dyp = dy_ref[rows, cols]
                    do.append(dyp * (zp * sg))
                    dp_ref[rows, zcols] = (dyp * op[i] * (sg * (1.0 + zp * (1.0 - sg)))).astype(BF16)
                q4 = _stack_heads(qr[0], qr[1], lo).astype(BF16)
                do4 = _stack_heads(do[0], do[1], lo)
                o4 = jnp.concatenate([op[0], op[0], op[1], op[1]], axis=0)
                e, es = _head_exp(q4, k2[g], valid, _sink_column(sink_ref, g))
                inv = 1.0 / (jnp.dot(e, ones, preferred_element_type=F32) + es)
                prob = e.astype(F32) * jnp.concatenate([inv, inv, inv], axis=1)
                delta = jnp.sum(do4 * o4, axis=1, keepdims=True)
                do4b = do4.astype(BF16)
                dprob = lax.dot_general(do4b, v2[g], _DIMS["nt"], preferred_element_type=F32)
                dsc = (prob * (dprob - delta)).astype(BF16)
                sink_terms = (es * inv[:, :1]) * delta
                for i in range(HEADS_PER_GROUP):
                    h = HEADS_PER_GROUP * g + i
                    dsink = -jnp.sum(sink_terms[i * wb:(i + 1) * wb], axis=0, keepdims=True)
                    ds_ref[h:h + 1, :] += jnp.broadcast_to(dsink, (1, LANES))
                dq4 = jnp.dot(dsc, k2[g], preferred_element_type=F32) * scale
                for i in range(2):
                    cols = slice((2 * g + i) * LANES, (2 * g + i + 1) * LANES)
                    dp_ref[rows, cols] = _rope_transpose(_unstack_pair(dq4, i, lo), tab_v).astype(BF16)
                dk2 = lax.dot_general(dsc, q4, _DIMS["tn"], preferred_element_type=F32) * scale
                dv2 = lax.dot_general(prob.astype(BF16), do4b, _DIMS["tn"], preferred_element_type=F32)
                dks.append(dk2 + pltpu.roll(dk2, HALF_LANES, 1))
                dvs.append(dv2 + pltpu.roll(dv2, HALF_LANES, 1))
            start = _window_start(n)
            dk_ref[pl.ds(start, 3 * wb), :] += jnp.where(lo_w, dks[0], dks[1])
            dv_ref[pl.ds(start, 3 * wb), :] += jnp.where(lo_w, dvs[0], dvs[1])

    tq = SWA_BWD_BLOCKS * wb
    pad_spec = pl.BlockSpec((s + 2 * KV_PAD, LANES), lambda n: (0, 0))
    blk = pl.BlockSpec((tq, A_WIDTH), lambda n: (n, 0))
    bsp = pl.BlockSpec((tq, W_B), lambda n: (n, COL_B))
    pad_shape = jax.ShapeDtypeStruct((s + 2 * KV_PAD, LANES), F32)
    return pl.pallas_call(
        body, grid=(s // tq,),
        in_specs=[bsp, pad_spec, pad_spec, pl.BlockSpec((tq, 3 * LANES), lambda n: (n, 0)),
                  pl.BlockSpec(bias.shape, lambda n: (0, 0, 0)), pl.BlockSpec(memory_space=pltpu.SMEM), blk, blk,
                  pl.BlockSpec(memory_space=pl.ANY)],
        out_specs=[bsp, pad_spec, pad_spec, pl.BlockSpec((8, LANES), lambda n: (0, 0))],
        out_shape=[jax.ShapeDtypeStruct(dproj.shape, BF16), pad_shape, pad_shape,
                   jax.ShapeDtypeStruct((8, LANES), F32)],
        input_output_aliases={8: 0}, name="swa_bwd",
        compiler_params=_params())(proj, kpad, vpad, tab, bias, sink, o_attn, dyb, dproj)


def _mem_exp(qh, mk):
    sc = lax.dot_general(qh, mk, _DIMS["nt"], preferred_element_type=F32) * (MEM_HEAD_DIM ** -0.5)
    return jnp.exp(sc - jnp.max(sc, axis=1, keepdims=True)).astype(BF16)


def _mem_fwd(proj, mkv):
    s = proj.shape[0]
    ts = 512
    mlen = mkv.shape[0]

    def body(m_ref, kv_ref, o_ref, y_ref):
        ones = jnp.ones((mlen, LANES), BF16)
        for h in range(MEM_HEADS):
            cols = slice(h * LANES, (h + 1) * LANES)
            mk = kv_ref[:, cols].astype(BF16)
            mv = kv_ref[:, MEM_WIDTH + h * LANES:MEM_WIDTH + (h + 1) * LANES].astype(BF16)
            e = _mem_exp(m_ref[:, cols].astype(BF16), mk)
            ox = jnp.dot(e, jnp.concatenate([mv, ones], axis=1), preferred_element_type=F32)
            oh = ox[:, :LANES] * (1.0 / ox[:, LANES:])
            o_ref[:, cols] = oh
            zh = m_ref[:, MEM_WIDTH + h * LANES:MEM_WIDTH + (h + 1) * LANES]
            y_ref[:, cols] = (oh * (zh * _sigmoid(zh))).astype(BF16)

    o_spec = pl.BlockSpec((ts, MEM_WIDTH), lambda i: (i, 0))
    return pl.pallas_call(
        body, grid=(s // ts,),
        in_specs=[pl.BlockSpec((ts, W_M), lambda i: (i, COL_M)),
                  pl.BlockSpec((mlen, 2 * MEM_WIDTH), lambda i: (0, 0))],
        out_specs=[o_spec, o_spec],
        out_shape=[jax.ShapeDtypeStruct((s, MEM_WIDTH), F32), jax.ShapeDtypeStruct((s, MEM_WIDTH), BF16)],
        name="mem_fwd", compiler_params=_params())(proj, mkv)


def _mem_bwd(proj, mkv, o_mem, dym, dproj):
    s = proj.shape[0]
    ts = 512
    mlen = mkv.shape[0]
    scale = MEM_HEAD_DIM ** -0.5

    def body(m_ref, kv_ref, o_ref, dy_ref, _, dp_ref, dkv_ref):
        @pl.when(pl.program_id(0) == 0)
        def _():
            dkv_ref[...] = jnp.zeros_like(dkv_ref)

        ones = jnp.ones((mlen, LANES), BF16)
        for h in range(MEM_HEADS):
            cols = slice(h * LANES, (h + 1) * LANES)
            vcols = slice(MEM_WIDTH + h * LANES, MEM_WIDTH + (h + 1) * LANES)
            mk = kv_ref[:, cols].astype(BF16)
            mv = kv_ref[:, vcols].astype(BF16)
            qh = m_ref[:, cols].astype(BF16)
            zh = m_ref[:, vcols]
            sg = _sigmoid(zh)
            oh = o_ref[:, cols]
            dyh = dy_ref[:, cols]
            doh = dyh * (zh * sg)
            dp_ref[:, vcols] = (dyh * oh * (sg * (1.0 + zh * (1.0 - sg)))).astype(BF16)
            e = _mem_exp(qh, mk)
            inv = 1.0 / jnp.dot(e, ones, preferred_element_type=F32)
            prob = e.astype(F32) * jnp.concatenate([inv] * (mlen // LANES), axis=1)
            delta = jnp.sum(doh * oh, axis=1, keepdims=True)
            dohb = doh.astype(BF16)
            dprob = lax.dot_general(dohb, mv, _DIMS["nt"], preferred_element_type=F32)
            dsc = (prob * (dprob - delta)).astype(BF16)
            dp_ref[:, cols] = (jnp.dot(dsc, mk, preferred_element_type=F32) * scale).astype(BF16)
            dkv_ref[:, cols] += lax.dot_general(dsc, qh, _DIMS["tn"], preferred_element_type=F32) * scale
            dkv_ref[:, vcols] += lax.dot_general(prob.astype(BF16), dohb, _DIMS["tn"],
                                                 preferred_element_type=F32)

    blk = pl.BlockSpec((ts, MEM_WIDTH), lambda i: (i, 0))
    msp = pl.BlockSpec((ts, W_M), lambda i: (i, COL_M))
    kvsp = pl.BlockSpec((mlen, 2 * MEM_WIDTH), lambda i: (0, 0))
    return pl.pallas_call(
        body, grid=(s // ts,),
        in_specs=[msp, kvsp, blk, blk, pl.BlockSpec(memory_space=pl.ANY)],
        out_specs=[msp, kvsp],
        out_shape=[jax.ShapeDtypeStruct(dproj.shape, BF16), jax.ShapeDtypeStruct(mkv.shape, F32)],
        input_output_aliases={4: 0}, name="mem_bwd",
        compiler_params=_params())(proj, mkv, o_mem, dym, dproj)


def _forward_backward(x, mem, tgt, proj, w_conv, sink, g_mem, late_weights, g_post):
    s = x.shape[0]
    tab = _rope_tables(s)
    bias = _window_bias()

    ya = _conv_fwd(proj, w_conv)
    kpad, vpad = _rope_kv(proj, tab)
    o_attn, yb, *arrived = _swa_fwd(proj, kpad, vpad, tab, bias, sink, carry=late_weights[0])
    w_kv, w_up, w_out = late_weights[1](arrived[0] if arrived else None)
    mn = _rmsnorm_fwd(mem, g_mem, name="mem_norm")
    mkv = _matmul(mn, w_kv, mode="nn", out_dtype=F32, tm=256, tn=1024, tk=D_MODEL, name="mem_kv")
    o_mem, ym = _mem_fwd(proj, mkv)
    merged, d_out, dy, dg_post, loss = _mid_fwd(ya, yb, ym, proj, x, tgt, w_up, w_out, g_post)
    dproj, d_ya, d_yb, d_ym, dw_up, dw_out = _mid_bwd(d_out, merged, ya, yb, ym, proj, w_up, w_out)

    dproj, dw_conv = _conv_bwd(proj, w_conv, d_ya, dproj)
    dproj, dkpad, dvpad, dsink = _swa_bwd(proj, kpad, vpad, tab, bias, sink, o_attn, d_yb, dproj)
    dproj = _rope_kv_bwd(dkpad, dvpad, tab, dproj)
    dproj, d_mkv = _mem_bwd(proj, mkv, o_mem, d_ym, dproj)

    dw_kv = _matmul(mn, d_mkv, mode="tn", out_dtype=F32, tm=1024, tn=1024, tk=256, name="dw_kv")
    d_mn = _matmul(d_mkv, w_kv, mode="nt", out_dtype=F32, tm=256, tn=1024, tk=D_MODEL, name="d_mn")
    _, dg_mem = _rmsnorm_bwd(d_mn, mem, g_mem, d_mn, name="mem_norm_bwd")

    return dict(loss=loss, dproj=dproj, dy=dy, w_conv=dw_conv, sink=dsink, g_mem=dg_mem,
                w_kv=dw_kv, w_up=dw_up, w_out=dw_out, g_post=dg_post)


N_DEV = 8


def _position():
    return lax.axis_index("x"), lax.axis_index("y"), lax.axis_index("c")


def _other_chips(x, y):
    return (((1 - x, y), 2 * (1 - x) + y), ((x, 1 - y), 2 * x + (1 - y)), ((1 - x, 1 - y), 2 * (1 - x) + (1 - y)))


def _remote(src, dst, send_sems, recv_sems, k, device):
    return pltpu.make_async_remote_copy(src_ref=src, dst_ref=dst, send_sem=send_sems.at[k], recv_sem=recv_sems.at[k],
                                        device_id=device, device_id_type=MESH)


def _rows_half(ref, hf):
    rh = ref.shape[0] // 2
    return ref.at[pl.ds(pl.multiple_of(hf * rh, 8), rh)]


def _gather_weights(shards, small=None, relations=(0, 1, 2), into=None):
    n = len(shards)
    k = 0 if small is None else 1

    def peers(x, y):
        return [(r, chip, idx) for r, (chip, idx) in enumerate(_other_chips(x, y)) if r in relations]

    def ici(ins, outs, sems, a, r, chip, src_chip, c):
        return _remote(_rows_half(ins[a], c), _rows_half(outs[a].at[src_chip], c), sems[0], sems[1], 3 * a + r,
                       (*chip, c))

    def whole(ins, outs, sems, r, chip, src_chip, c):
        return _remote(ins[n], outs[n].at[src_chip], sems[0], sems[1], 3 * n + r, (*chip, c))

    def d2d(outs, sems, a, r, idx, hf, x, y, c):
        half = _rows_half(outs[a].at[idx], hf)
        return _remote(half, half, sems[2], sems[3], 3 * a + r, (x, y, 1 - c))

    def start(ins, outs, sems):
        x, y, c = _position()
        me = 2 * x + y
        for a in range(n):
            for r, chip, _ in peers(x, y):
                ici(ins, outs, sems, a, r, chip, me, c).start()
        for r, (chip, _) in enumerate(_other_chips(x, y)):
            if k:
                whole(ins, outs, sems, r, chip, me, c).start()

    def finish(ins, outs, sems):
        x, y, c = _position()
        me = 2 * x + y
        for a in range(n):
            for r, chip, idx in peers(x, y):
                ici(ins, outs, sems, a, r, chip, idx, c).wait_recv()
                d2d(outs, sems, a, r, idx, c, x, y, c).start()
        for a in range(n):
            for r, chip, idx in peers(x, y):
                d2d(outs, sems, a, r, idx, 1 - c, x, y, c).wait_recv()
        for r, (chip, idx) in enumerate(_other_chips(x, y)):
            if k:
                whole(ins, outs, sems, r, chip, idx, c).wait_recv()
                whole(ins, outs, sems, r, chip, me, c).wait_send()
        for a in range(n):
            for r, chip, idx in peers(x, y):
                ici(ins, outs, sems, a, r, chip, me, c).wait_send()
                d2d(outs, sems, a, r, idx, c, x, y, c).wait_send()

    operands = list(shards) + ([small] if k else [])
    shapes = [jax.ShapeDtypeStruct((N_CHIPS,) + s.shape, s.dtype) for s in operands]
    aliases = {}
    if into is not None:
        assert len(into) == len(operands)
        aliases = {len(operands) + a: a for a in range(len(into))}
        operands += list(into)
    return _Carry(operands, shapes,
                  [pltpu.SemaphoreType.DMA((3 * (n + k),)), pltpu.SemaphoreType.DMA((3 * (n + k),)),
                   pltpu.SemaphoreType.DMA((3 * n,)), pltpu.SemaphoreType.DMA((3 * n,))], start, finish, aliases)


def _run_carry(carry, name):
    _, results = _carried_call(lambda ins, outs, scr: None, carry, grid=(1,), in_specs=[], out_specs=[],
                               out_shape=[], scratch=[], operands=(), name=name)
    return results


def _pair_exchange(send):
    n = len(send)

    def copies(ins, outs, sems):
        x, y, c = _position()
        return [_remote(ins[a], outs[a], sems[0], sems[1], a, (x, y, 1 - c)) for a in range(n)]

    def start(ins, outs, sems):
        for cp in copies(ins, outs, sems):
            cp.start()

    def finish(ins, outs, sems):
        for cp in copies(ins, outs, sems):
            cp.wait()

    return _Carry(send, [jax.ShapeDtypeStruct(p.shape, p.dtype) for p in send],
                  [pltpu.SemaphoreType.DMA((n,)), pltpu.SemaphoreType.DMA((n,))], start, finish)


def _chip_exchange(sums):
    n = len(sums)

    def copies(ins, outs, sems):
        x, y, c = _position()
        return [_remote(ins[a].at[idx], outs[a].at[r], sems[0], sems[1], 3 * a + r, (*chip, c))
                for a in range(n) for r, (chip, idx) in enumerate(_other_chips(x, y))]

    def start(ins, outs, sems):
        for cp in copies(ins, outs, sems):
            cp.start()

    def finish(ins, outs, sems):
        for cp in copies(ins, outs, sems):
            cp.wait()

    return _Carry(sums, [jax.ShapeDtypeStruct((3,) + p.shape[1:], p.dtype) for p in sums],
                  [pltpu.SemaphoreType.DMA((3 * n,)), pltpu.SemaphoreType.DMA((3 * n,))], start, finish)


def _pair_share(pairs):
    n = len(pairs)

    def start(ins, outs, sems):
        x, y, c = _position()
        for a in range(n):
            _remote(outs[a].at[c], outs[a].at[c], sems[0], sems[1], a, (x, y, 1 - c)).start()

    def finish(ins, outs, sems):
        x, y, c = _position()
        for a in range(n):
            _remote(outs[a].at[1 - c], outs[a].at[1 - c], sems[0], sems[1], a, (x, y, 1 - c)).wait_recv()
        for a in range(n):
            _remote(outs[a].at[c], outs[a].at[c], sems[0], sems[1], a, (x, y, 1 - c)).wait_send()

    return _Carry(pairs, [jax.ShapeDtypeStruct(p.shape, p.dtype) for p in pairs],
                  [pltpu.SemaphoreType.DMA((n,)), pltpu.SemaphoreType.DMA((n,))], start, finish,
                  aliases={a: a for a in range(n)})


def _small_allreduce(pack, share):
    rows, width = pack.shape
    n_share = len(share.ins)

    def body(p_ref, *refs):
        share_in, o_ref, share_out = refs[:n_share], refs[n_share], refs[n_share + 1:2 * n_share + 1]
        buf, send_sems, recv_sems = refs[2 * n_share + 1:2 * n_share + 4]
        share_sems = refs[2 * n_share + 4:]
        share.start(share_in, share_out, share_sems)
        x, y, c = _position()
        me = 4 * x + 2 * y + c
        buf[me] = p_ref[...]
        peers = []
        for r in range(1, N_DEV):
            fx, fy, fc = (r >> 2) & 1, (r >> 1) & 1, r & 1
            px, py, pc = (1 - x if fx else x), (1 - y if fy else y), (1 - c if fc else c)
            peers.append(((px, py, pc), 4 * px + 2 * py + pc))
        sends = [_remote(p_ref, buf.at[me], send_sems, recv_sems, r, dev) for r, (dev, _) in enumerate(peers)]
        for cp in sends:
            cp.start()
        for r, (dev, idx) in enumerate(peers):
            _remote(p_ref, buf.at[idx], send_sems, recv_sems, r, dev).wait_recv()
        for cp in sends:
            cp.wait_send()
        acc = buf[0]
        for k in range(1, N_DEV):
            acc = acc + buf[k]
        o_ref[...] = acc
        share.finish(share_in, share_out, share_sems)

    vm = pl.BlockSpec(memory_space=pltpu.VMEM)
    red, *shared = pl.pallas_call(
        body, in_specs=[vm] + [_HBM] * n_share, out_specs=[vm] + [_HBM] * n_share,
        out_shape=[jax.ShapeDtypeStruct(pack.shape, F32)] + share.out_shapes,
        scratch_shapes=[pltpu.VMEM((N_DEV, rows, width), F32), pltpu.SemaphoreType.DMA((N_DEV - 1,)),
                        pltpu.SemaphoreType.DMA((N_DEV - 1,))] + share.sems,
        input_output_aliases={1 + i: 1 + o for i, o in share.aliases.items()},
        name="small_allreduce")(pack, *share.ins)
    return red, shared


ROW_TILE_MAX = 512
SUM_TILE_MAX = 2048
BF16_SUBLANES = 16


def _row_tile(rows, most=ROW_TILE_MAX):
    if rows <= most:
        return rows
    return max(t for t in range(BF16_SUBLANES, most + 1, BF16_SUBLANES) if rows % t == 0)


def _pair_add(keep, recv, name):
    nj, rh, cols = keep.shape
    tr = _row_tile(rh, SUM_TILE_MAX)

    def body(k_ref, r_ref, o_ref):
        o_ref[...] = (k_ref[...].astype(F32) + r_ref[...].astype(F32)).astype(BF16)

    blk = pl.BlockSpec((None, tr, cols), lambda j, i: (j, i, 0))
    return pl.pallas_call(body, grid=(nj, rh // tr), in_specs=[blk, blk], out_specs=blk,
                          out_shape=jax.ShapeDtypeStruct(keep.shape, BF16), name=name,
                          compiler_params=_params())(keep, recv)


def _chip_add(sums, recv, where, name):
    _, rh, cols = sums.shape
    tr = _row_tile(rh, SUM_TILE_MAX)

    def body(w_ref, s_ref, r_ref, o_ref):
        o_ref[...] = ((s_ref[...].astype(F32) + r_ref[0].astype(F32)) + r_ref[1].astype(F32)) + r_ref[2].astype(F32)

    grid_spec = pltpu.PrefetchScalarGridSpec(
        num_scalar_prefetch=1, grid=(rh // tr,),
        in_specs=[pl.BlockSpec((None, tr, cols), lambda i, w_ref: (w_ref[0], i, 0)),
                  pl.BlockSpec((3, tr, cols), lambda i, w_ref: (0, i, 0))],
        out_specs=pl.BlockSpec((None, tr, cols), lambda i, w_ref: (w_ref[1], i, 0)))
    return pl.pallas_call(body, grid_spec=grid_spec, out_shape=jax.ShapeDtypeStruct((2, rh, cols), F32),
                          name=name, compiler_params=_params())(where, sums, recv)


def _adamw(w, g, m, v, name):
    rows, cols = w.shape
    tr = _row_tile(rows)
    assert rows % tr == 0

    def body(w_ref, g_ref, m_ref, v_ref, d_ref, mo_ref, vo_ref):
        gv = g_ref[...]
        m_new = ADAM_B1 * m_ref[...] + (1.0 - ADAM_B1) * gv
        v_new = ADAM_B2 * v_ref[...] + (1.0 - ADAM_B2) * jnp.square(gv)
        m_hat = m_new / (1.0 - ADAM_B1 ** ADAM_STEP)
        v_hat = v_new / (1.0 - ADAM_B2 ** ADAM_STEP)
        d_ref[...] = -ADAM_LR * (m_hat / (jnp.sqrt(v_hat) + ADAM_EPS) + ADAM_WD * w_ref[...])
        mo_ref[...] = m_new
        vo_ref[...] = v_new

    blk = pl.BlockSpec((tr, cols), lambda i: (i, 0))
    shp = jax.ShapeDtypeStruct((rows, cols), F32)
    return pl.pallas_call(body, grid=(rows // tr,), in_specs=[blk] * 4, out_specs=[blk] * 3,
                          out_shape=[shp] * 3, name=name, compiler_params=_params())(w, g, m, v)


def _adamw_halves(w, g2, m, v, name):
    rows, cols = w.shape
    half = cols // 2
    tr = _row_tile(rows)

    def body(w_ref, g_ref, m_ref, v_ref, go_ref, d_ref, mo_ref, vo_ref):
        gv = g_ref[...]
        go_ref[...] = gv
        m_new = ADAM_B1 * m_ref[...] + (1.0 - ADAM_B1) * gv
        v_new = ADAM_B2 * v_ref[...] + (1.0 - ADAM_B2) * jnp.square(gv)
        m_hat = m_new / (1.0 - ADAM_B1 ** ADAM_STEP)
        v_hat = v_new / (1.0 - ADAM_B2 ** ADAM_STEP)
        d_ref[...] = -ADAM_LR * (m_hat / (jnp.sqrt(v_hat) + ADAM_EPS) + ADAM_WD * w_ref[...])
        mo_ref[...] = m_new
        vo_ref[...] = v_new

    blk = pl.BlockSpec((tr, half), lambda hf, i: (i, hf))
    gsp = pl.BlockSpec((None, tr, half), lambda hf, i: (hf, i, 0))
    shp = jax.ShapeDtypeStruct((rows, cols), F32)
    return pl.pallas_call(body, grid=(2, rows // tr), in_specs=[blk, gsp, blk, blk], out_specs=[blk] * 4,
                          out_shape=[shp] * 4, name=name, compiler_params=_params())(w, g2, m, v)


SHARD_W = IN_WIDTH // N_CHIPS


def _half_major(a):
    r, c = a.shape
    return a.reshape(N_CHIPS, 2, r // N_CHIPS // 2, c).transpose(1, 0, 2, 3)


def kernel(x, mem, g_pre, w_in, w_conv, attn_sink, g_mem, w_mem_kv, w_up_a, w_up_b, w_up_m, w_out, g_post, loss_target, m_g_pre, m_w_in, m_w_conv, m_attn_sink, m_g_mem, m_w_mem_kv, m_w_up_a, m_w_up_b, m_w_up_m, m_w_out, m_g_post, v_g_pre, v_w_in, v_w_conv, v_attn_sink, v_g_mem, v_w_mem_kv, v_w_up_a, v_w_up_b, v_w_up_m, v_w_out, v_g_post):
    xi, yi, ci = _position()
    chip = 2 * xi + yi
    where = jnp.stack([chip, ci, N_CHIPS - 1 - chip]).astype(jnp.int32)

    own = [w_in[0].T.astype(BF16), w_mem_kv[0].astype(BF16),
           jnp.concatenate([w_up_a[0], w_up_b[0], w_up_m[0]], axis=0).astype(BF16), w_out[0].astype(BF16)]
    own_conv = jnp.pad(w_conv[0], ((0, 5), (0, 0)))

    def pieces(mine, got):
        got = lax.dynamic_update_slice_in_dim(got, mine[None], chip, axis=0)
        return [got[j] for j in range(N_CHIPS)]

    diag = N_CHIPS - 1 - chip
    diag_blocks = SHARD_BLOCKS + 1
    (h, h_t), (got_near, got_conv) = _rmsnorm_fwd(x[0], g_pre, name="pre_norm", transposed=True,
                                                  carry=_gather_weights(own[:1], own_conv, relations=(0, 1)))
    w_near = lax.dynamic_update_slice_in_dim(got_near, own[0][None], chip, axis=0).reshape(IN_WIDTH, D_MODEL)
    proj, (got_far, *got_rest) = _proj(
        h, w_near, n_blocks=N_IN_BLOCKS - diag_blocks, where=where, name="proj_near",
        block_of=lambda i, w: i + diag_blocks * (i >= SHARD_BLOCKS * w[2]).astype(jnp.int32),
        carry=_join(_gather_weights(own[:1], relations=(2,)), _gather_weights(own[1:], relations=(0, 1))))
    far = lax.dynamic_index_in_dim(got_far, diag, 0, keepdims=False)
    proj = _proj_far(h, w_near, far, where, into=proj)
    w_conv_full = jnp.concatenate([p[:3] for p in pieces(own_conv, got_conv)], axis=1)

    def late_weights(gathered):
        w_kv_full = jnp.concatenate(pieces(own[1], gathered[0]), axis=0)
        up_pieces = pieces(own[2], gathered[1])
        w_up_full = jnp.stack([jnp.concatenate([p[k * A_WIDTH:(k + 1) * A_WIDTH] for p in up_pieces], axis=1)
                               for k in range(3)])
        return w_kv_full, w_up_full, jnp.concatenate(pieces(own[3], gathered[2]), axis=0)

    g = _forward_backward(x[0], mem[0], loss_target[0], proj, w_conv_full, attn_sink, g_mem,
                          (_gather_weights(own[1:], relations=(2,), into=got_rest), late_weights), g_post)

    half_rows = D_MODEL // 2
    up_parts = (g["w_up"].reshape(3, A_WIDTH, N_CHIPS, D_MODEL // N_CHIPS).transpose(2, 0, 1, 3)
                .reshape(N_CHIPS, 2, 3 * A_WIDTH // 2, D_MODEL // N_CHIPS).transpose(1, 0, 2, 3)).astype(BF16)
    small_parts = [_half_major(g["w_kv"]).astype(BF16), up_parts, _half_major(g["w_out"]).astype(BF16)]

    def dw_in_half(half_of, name, carry):
        dw, carried = _dw_in_t(g["dproj"], h_t, half_of=half_of, where=where, name=name, carry=carry)
        return dw.reshape(N_CHIPS, SHARD_W, half_rows), carried

    def pick(parts, hf):
        return [lax.dynamic_index_in_dim(p, hf, 0, keepdims=False) for p in parts]

    small_names = ["w_kv", "w_up", "w_out"]
    recv_small = _run_carry(_pair_exchange(pick(small_parts, 1 - ci)), "pair_exchange_small")
    sums_small = [_pair_add(k, r, "pair_add_" + nm)
                  for k, r, nm in zip(pick(small_parts, ci), recv_small, small_names)]
    dw_send, recv3_small = dw_in_half(lambda w: 1 - w[1], "dw_in_send", _chip_exchange(sums_small))
    dw_keep, (recv_in,) = dw_in_half(lambda w: w[1], "dw_in_keep", _pair_exchange([dw_send]))
    sum_in = _pair_add(dw_keep, recv_in, "pair_add_w_in")
    d_h, (recv3_in,) = _d_h(g["dproj"], w_near, far, where, carry=_chip_exchange([sum_in]))
    pairs = [_chip_add(s, r, where, "chip_add_" + nm)
             for s, r, nm in zip([sum_in] + sums_small, [recv3_in] + recv3_small, ["w_in"] + small_names)]
    grad_x, dg_pre = _rmsnorm_bwd(d_h, x[0], g_pre, g["dy"], name="pre_norm_bwd")

    zeros512 = jnp.zeros((1, D_MODEL - A_WIDTH), F32)
    conv_rows = [jnp.concatenate([g["w_conv"][k:k + 1], zeros512], axis=1) for k in range(3)]
    sink_row = jnp.pad(g["sink"][:, 0].reshape(1, N_Q_HEADS), ((0, 0), (0, D_MODEL - N_Q_HEADS)))
    loss_row = jnp.pad(g["loss"], ((0, 0), (0, D_MODEL - LANES)))
    pack = jnp.concatenate([dg_pre, g["g_mem"], g["g_post"]] + conv_rows + [sink_row, loss_row], axis=0)
    red, full = _small_allreduce(pack, _pair_share(pairs))
    loss = red[7, 0]
    small_grads = dict(
        g_pre=red[0:1], g_mem=red[1:2], g_post=red[2:3], attn_sink=red[6:7, :N_Q_HEADS],
        w_conv=lax.dynamic_slice(red[3:6, :A_WIDTH], (0, chip * LANES), (3, LANES)))

    gw_up = full[2].reshape(3, A_WIDTH, D_MODEL // N_CHIPS)
    grads = dict(small_grads, w_mem_kv=full[1].reshape(D_MODEL // N_CHIPS, 2 * MEM_WIDTH),
                 w_up_a=gw_up[0], w_up_b=gw_up[1], w_up_m=gw_up[2],
                 w_out=full[3].reshape(D_MODEL // N_CHIPS, D_MODEL))

    weights = dict(g_pre=g_pre, w_in=w_in, w_conv=w_conv, attn_sink=attn_sink, g_mem=g_mem, w_mem_kv=w_mem_kv,
                   w_up_a=w_up_a, w_up_b=w_up_b, w_up_m=w_up_m, w_out=w_out, g_post=g_post)
    m_in = dict(g_pre=m_g_pre, w_in=m_w_in, w_conv=m_w_conv, attn_sink=m_attn_sink, g_mem=m_g_mem,
                w_mem_kv=m_w_mem_kv, w_up_a=m_w_up_a, w_up_b=m_w_up_b, w_up_m=m_w_up_m, w_out=m_w_out,
                g_post=m_g_post)
    v_in = dict(g_pre=v_g_pre, w_in=v_w_in, w_conv=v_w_conv, attn_sink=v_attn_sink, g_mem=v_g_mem,
                w_mem_kv=v_w_mem_kv, w_up_a=v_w_up_a, w_up_b=v_w_up_b, w_up_m=v_w_up_m, w_out=v_w_out,
                g_post=v_g_post)
    out_g, out_d, out_m, out_v = [], [], [], []
    for nm in ("g_pre", "w_in", "w_conv", "attn_sink", "g_mem", "w_mem_kv", "w_up_a", "w_up_b", "w_up_m", "w_out",
               "g_post"):
        shape = weights[nm].shape
        if nm == "w_in":
            results = _adamw_halves(w_in[0].T, full[0], m_w_in[0].T, v_w_in[0].T, "adamw_w_in")
            for out, t in zip((out_g, out_d, out_m, out_v), results):
                out.append(t.T.reshape(shape))
            continue
        two_d = shape[-2:]
        gr = grads[nm].reshape(two_d)
        d, m_new, v_new = _adamw(weights[nm].reshape(two_d), gr, m_in[nm].reshape(two_d), v_in[nm].reshape(two_d),
                                 "adamw_" + nm)
        out_g.append(gr.reshape(shape))
        out_d.append(d.reshape(shape))
        out_m.append(m_new.reshape(shape))
        out_v.append(v_new.reshape(shape))
    return (loss, grad_x.reshape(x.shape), *out_g, *out_d, *out_m, *out_v)
```

```python
import jax
import jax.numpy as jnp
from jax import lax
from jax.experimental import pallas as pl
from jax.experimental.pallas import tpu as pltpu

F32 = jnp.float32
BF16 = jnp.bfloat16
MESH = pl.DeviceIdType.MESH

D_MODEL = 1024
EPS = 1e-6
A_WIDTH = 512
HEAD_DIM = 64
N_Q_HEADS = 8
WINDOW_BLOCK = 128
KV_PAD = 512
ROPE_THETA = 500000.0
ROT_DIM = 16
MEM_HEADS = 4
MEM_HEAD_DIM = 128
MEM_WIDTH = 512
IN_WIDTH = 7424
N_CHIPS = 4
LANES = 128
HALF_LANES = 64

PERM_SEGS = ((0, 2560), (2816, 3328), (4352, 7424), (3328, 4352), (2560, 2816))
COL_A, W_A = 0, 2048
COL_B, W_B = 2, 1024
COL_G, W_G = 1, 3072
COL_M, W_M = 6, 1024
COL_KV, W_KV = 28, 256

ADAM_LR = 0.001
ADAM_B1 = 0.9
ADAM_B2 = 0.999
ADAM_EPS = 1e-08
ADAM_WD = 0.01
ADAM_STEP = 10

VMEM_LIGHT_BYTES = 48 * 1024 * 1024
VMEM_HEAVY_BYTES = 48 * 1024 * 1024


_HBM = pl.BlockSpec(memory_space=pltpu.HBM)


def _params(heavy=False):
    return pltpu.CompilerParams(vmem_limit_bytes=VMEM_HEAVY_BYTES if heavy else VMEM_LIGHT_BYTES)


def _sigmoid(v):
    return jax.nn.sigmoid(v)


_DIMS = {"nn": (((1,), (0,)), ((), ())), "nt": (((1,), (1,)), ((), ())), "tn": (((0,), (0,)), ((), ()))}


class _Carry:
    def __init__(self, ins, out_shapes, sems, start, finish, aliases=None):
        self.ins, self.out_shapes, self.sems = list(ins), list(out_shapes), list(sems)
        self.start, self.finish, self.aliases = start, finish, dict(aliases or {})


def _join(*carries):
    def split(seq, counts):
        pos, parts = 0, []
        for n in counts:
            parts.append(seq[pos:pos + n])
            pos += n
        return parts

    n_in = [len(c.ins) for c in carries]
    n_out = [len(c.out_shapes) for c in carries]
    n_sem = [len(c.sems) for c in carries]

    def run(which):
        def go(ins, outs, sems):
            for c, i, o, sm in zip(carries, split(ins, n_in), split(outs, n_out), split(sems, n_sem)):
                getattr(c, which)(i, o, sm)
        return go

    aliases = {}
    for k, c in enumerate(carries):
        aliases.update({sum(n_in[:k]) + i: sum(n_out[:k]) + o for i, o in c.aliases.items()})
    return _Carry([a for c in carries for a in c.ins], [sh for c in carries for sh in c.out_shapes],
                  [sm for c in carries for sm in c.sems], run("start"), run("finish"), aliases)


def _carried_call(body, carry, *, grid, in_specs, out_specs, out_shape, scratch, operands, name, prefetch=None,
                  aliases=None, heavy=False):
    n_in, n_out, n_scr = len(in_specs), len(out_specs), len(scratch)
    c_in = len(carry.ins) if carry else 0
    c_out = len(carry.out_shapes) if carry else 0
    n_pre = 0 if prefetch is None else 1
    steps = 1
    for g in grid:
        steps *= g

    def wrapped(*refs):
        refs = refs[n_pre:]
        ins, cins = refs[:n_in], refs[n_in:n_in + c_in]
        outs = refs[n_in + c_in:n_in + c_in + n_out]
        couts = refs[n_in + c_in + n_out:n_in + c_in + n_out + c_out]
        rest = refs[n_in + c_in + n_out + c_out:]
        scr, sems = rest[:n_scr], rest[n_scr:]
        if carry:
            step = pl.program_id(0)
            for ax in range(1, len(grid)):
                step = step * grid[ax] + pl.program_id(ax)

            @pl.when(step == 0)
            def _():
                carry.start(cins, couts, sems)

        body(ins, outs, scr)
        if carry:
            @pl.when(step == steps - 1)
            def _():
                carry.finish(cins, couts, sems)

    all_aliases = {n_pre + i: o for i, o in (aliases or {}).items()}
    if carry:
        all_aliases.update({n_pre + n_in + i: n_out + o for i, o in carry.aliases.items()})
    all_in = list(in_specs) + [_HBM] * c_in
    all_out = list(out_specs) + [_HBM] * c_out
    all_scratch = list(scratch) + (carry.sems if carry else [])
    if n_pre:
        spec = dict(grid_spec=pltpu.PrefetchScalarGridSpec(num_scalar_prefetch=1, grid=grid, in_specs=all_in,
                                                           out_specs=all_out, scratch_shapes=all_scratch))
        pre = (prefetch,)
    else:
        spec = dict(grid=grid, in_specs=all_in, out_specs=all_out, scratch_shapes=all_scratch)
        pre = ()
    results = pl.pallas_call(
        wrapped, out_shape=list(out_shape) + (carry.out_shapes if carry else []), input_output_aliases=all_aliases,
        name=name, compiler_params=_params(heavy), **spec)(*pre, *operands, *(carry.ins if carry else []))
    return list(results[:n_out]), list(results[n_out:])


def _matmul(a, b, *, mode, out_dtype, tm, tn, tk, name):
    if mode == "nn":
        (m, k), (_, n) = a.shape, b.shape
    elif mode == "nt":
        (m, k), (n, _) = a.shape, b.shape
    else:
        (k, m), (_, n) = a.shape, b.shape
    tm, tn, tk = min(tm, m), min(tn, n), min(tk, k)
    assert m % tm == 0 and n % tn == 0 and k % tk == 0
    nk = k // tk
    dims = _DIMS[mode]

    if mode == "nn":
        a_spec = pl.BlockSpec((tm, tk), lambda i, j, kk: (i, kk))
        b_spec = pl.BlockSpec((tk, tn), lambda i, j, kk: (kk, j))
    elif mode == "nt":
        a_spec = pl.BlockSpec((tm, tk), lambda i, j, kk: (i, kk))
        b_spec = pl.BlockSpec((tn, tk), lambda i, j, kk: (j, kk))
    else:
        a_spec = pl.BlockSpec((tk, tm), lambda i, j, kk: (kk, i))
        b_spec = pl.BlockSpec((tk, tn), lambda i, j, kk: (kk, j))
    o_spec = pl.BlockSpec((tm, tn), lambda i, j, kk: (i, j))

    def part(a_ref, b_ref):
        return lax.dot_general(a_ref[...].astype(BF16), b_ref[...].astype(BF16), dims,
                               preferred_element_type=F32)

    if nk == 1:
        def body(a_ref, b_ref, o_ref):
            o_ref[...] = part(a_ref, b_ref).astype(out_dtype)
        scratch = []
    else:
        def body(a_ref, b_ref, o_ref, acc_ref):
            kk = pl.program_id(2)

            @pl.when(kk == 0)
            def _():
                acc_ref[...] = part(a_ref, b_ref)

            @pl.when(kk > 0)
            def _():
                acc_ref[...] += part(a_ref, b_ref)

            @pl.when(kk == nk - 1)
            def _():
                o_ref[...] = acc_ref[...].astype(out_dtype)
        scratch = [pltpu.VMEM((tm, tn), F32)]

    return pl.pallas_call(
        body, grid=(m // tm, n // tn, nk), in_specs=[a_spec, b_spec], out_specs=o_spec,
        out_shape=jax.ShapeDtypeStruct((m, n), out_dtype), scratch_shapes=scratch,
        name=name, compiler_params=_params())(a, b)


IN_BLOCK = 256
N_IN_BLOCKS = IN_WIDTH // IN_BLOCK
SHARD_BLOCKS = (IN_WIDTH // N_CHIPS) // IN_BLOCK
BLOCK_RUNS = tuple((a // IN_BLOCK, sum(d - c for c, d in PERM_SEGS[:k]) // IN_BLOCK, (b - a) // IN_BLOCK)
                   for k, (a, b) in enumerate(PERM_SEGS))


def _perm_block(r):
    p = r
    for ref0, perm0, n in BLOCK_RUNS:
        p = jnp.where((r >= ref0) & (r < ref0 + n), r - ref0 + perm0, p)
    return p


def _proj(h, w_t, *, n_blocks, block_of, where, name, carry=None):
    s, d = h.shape

    def body(ins, outs, scr):
        outs[0][...] = lax.dot_general(ins[0][...], ins[1][...], _DIMS["nt"], preferred_element_type=F32)

    (proj,), carried = _carried_call(
        body, carry, grid=(n_blocks,),
        in_specs=[pl.BlockSpec((s, d), lambda i, w: (0, 0)), pl.BlockSpec((IN_BLOCK, d), lambda i, w: (block_of(i, w), 0))],
        out_specs=[pl.BlockSpec((s, IN_BLOCK), lambda i, w: (0, _perm_block(block_of(i, w))))],
        out_shape=[jax.ShapeDtypeStruct((s, IN_WIDTH), F32)], scratch=[], operands=(h, w_t), name=name,
        prefetch=where)
    return (proj, carried) if carry else proj


def _proj_far(h, w_near, far, where, *, into, carry=None):
    s, d = h.shape
    n_blocks = SHARD_BLOCKS + 1
    lead = IN_WIDTH // N_CHIPS - SHARD_BLOCKS * IN_BLOCK

    def body(ins, outs, scr):
        where_ref, h_ref, w_hbm, far_hbm, _ = ins
        win, sem = scr
        i = pl.program_id(0)

        @pl.when(i == 0)
        def _():
            dg = where_ref[2]
            rows = pl.ds(pl.multiple_of(dg * (SHARD_BLOCKS * IN_BLOCK), IN_BLOCK), n_blocks * IN_BLOCK)
            window = pltpu.make_async_copy(w_hbm.at[rows], win, sem)
            window.start()
            window.wait()
            shard = pltpu.make_async_copy(far_hbm, win.at[pl.ds(pl.multiple_of(dg * lead, BF16_SUBLANES), SHARD_W)], sem)
            shard.start()
            shard.wait()

        blk = win[pl.ds(pl.multiple_of(i * IN_BLOCK, IN_BLOCK), IN_BLOCK), :]
        outs[0][...] = lax.dot_general(h_ref[...], blk, _DIMS["nt"], preferred_element_type=F32)

    anysp = pl.BlockSpec(memory_space=pl.ANY)
    (proj,), carried = _carried_call(
        body, carry, grid=(n_blocks,),
        in_specs=[pl.BlockSpec(memory_space=pltpu.SMEM), pl.BlockSpec((s, d), lambda i, w: (0, 0)), anysp, anysp, anysp],
        out_specs=[pl.BlockSpec((s, IN_BLOCK), lambda i, w: (0, _perm_block(i + SHARD_BLOCKS * w[2])))],
        out_shape=[jax.ShapeDtypeStruct((s, IN_WIDTH), F32)],
        scratch=[pltpu.VMEM((n_blocks * IN_BLOCK, d), BF16), pltpu.SemaphoreType.DMA],
        operands=(where, h, w_near, far, into), name="proj_far", prefetch=where, aliases={4: 0})
    return (proj, carried) if carry else proj


def _dw_in_t(dproj, h_t, *, half_of, where, name, carry=None):
    d, s = h_t.shape
    c = d // 2

    def body(ins, outs, scr):
        outs[0][...] = lax.dot_general(ins[1][...], ins[0][...], _DIMS["nn"], preferred_element_type=F32).T.astype(BF16)

    (dw,), carried = _carried_call(
        body, carry, grid=(N_IN_BLOCKS,),
        in_specs=[pl.BlockSpec((s, IN_BLOCK), lambda r, w: (0, _perm_block(r))),
                  pl.BlockSpec((c, s), lambda r, w: (half_of(w), 0))],
        out_specs=[pl.BlockSpec((IN_BLOCK, c), lambda r, w: (r, 0))],
        out_shape=[jax.ShapeDtypeStruct((IN_WIDTH, c), BF16)], scratch=[], operands=(dproj, h_t), name=name,
        prefetch=where)
    return (dw, carried) if carry else dw


def _d_h(dproj, w_near, far, where, *, carry=None):
    s = dproj.shape[0]
    d = w_near.shape[1]
    tm = min(s, 256)

    def body(ins, outs, scr):
        where_ref, a_ref, w_hbm, far_hbm = ins
        w_ref, sem = scr

        @pl.when(pl.program_id(0) == 0)
        def _():
            whole = pltpu.make_async_copy(w_hbm, w_ref, sem)
            whole.start()
            whole.wait()
            rows = pl.ds(pl.multiple_of(where_ref[2] * SHARD_W, BF16_SUBLANES), SHARD_W)
            part = pltpu.make_async_copy(far_hbm, w_ref.at[rows], sem)
            part.start()
            part.wait()

        acc = None
        for ref0, perm0, n in BLOCK_RUNS:
            term = jnp.dot(a_ref[:, perm0 * IN_BLOCK:(perm0 + n) * IN_BLOCK],
                           w_ref[ref0 * IN_BLOCK:(ref0 + n) * IN_BLOCK, :], preferred_element_type=F32)
            acc = term if acc is None else acc + term
        outs[0][...] = acc

    anysp = pl.BlockSpec(memory_space=pl.ANY)
    (dh,), carried = _carried_call(
        body, carry, grid=(s // tm,),
        in_specs=[pl.BlockSpec(memory_space=pltpu.SMEM), pl.BlockSpec((tm, IN_WIDTH), lambda i: (i, 0)), anysp, anysp],
        out_specs=[pl.BlockSpec((tm, d), lambda i: (i, 0))],
        out_shape=[jax.ShapeDtypeStruct((s, d), F32)],
        scratch=[pltpu.VMEM((IN_WIDTH, d), BF16), pltpu.SemaphoreType.DMA],
        operands=(where, dproj, w_near, far), name="d_h", heavy=True)
    return (dh, carried) if carry else dh


def _rmsnorm_fwd(x, g, *, name, transposed=False, carry=None):
    s, d = x.shape
    ts = min(512, s)

    def body(ins, outs, scr):
        xv = ins[0][...]
        r = lax.rsqrt(jnp.mean(xv * xv, axis=-1, keepdims=True) + EPS)
        hv = (xv * r) * ins[1][...]
        outs[0][...] = hv.astype(BF16)
        if transposed:
            outs[1][...] = hv.T.astype(BF16)

    out_specs = [pl.BlockSpec((ts, d), lambda i: (i, 0))]
    out_shape = [jax.ShapeDtypeStruct((s, d), BF16)]
    if transposed:
        out_specs.append(pl.BlockSpec((d, ts), lambda i: (0, i)))
        out_shape.append(jax.ShapeDtypeStruct((d, s), BF16))
    outs, carried = _carried_call(
        body, carry, grid=(s // ts,),
        in_specs=[pl.BlockSpec((ts, d), lambda i: (i, 0)), pl.BlockSpec((1, d), lambda i: (0, 0))],
        out_specs=out_specs, out_shape=out_shape, scratch=[], operands=(x, g), name=name)
    result = tuple(outs) if transposed else outs[0]
    return (result, carried) if carry else result


def _rmsnorm_bwd(dh, x, g, res, *, name, carry=None):
    s, d = x.shape
    ts = min(256, s)

    def body(ins, outs, scr):
        dh_ref, x_ref, g_ref, res_ref = ins
        dx_ref, dg_ref = outs
        xv = x_ref[...]
        r = lax.rsqrt(jnp.mean(xv * xv, axis=-1, keepdims=True) + EPS)
        xh = xv * r
        dhv = dh_ref[...]
        part = jnp.sum(dhv * xh, axis=0, keepdims=True)

        @pl.when(pl.program_id(0) == 0)
        def _():
            dg_ref[...] = part

        @pl.when(pl.program_id(0) > 0)
        def _():
            dg_ref[...] += part

        dxh = dhv * g_ref[...]
        dx_ref[...] = res_ref[...] + r * (dxh - xh * jnp.mean(dxh * xh, axis=-1, keepdims=True))

    row = pl.BlockSpec((ts, d), lambda i: (i, 0))
    vec = pl.BlockSpec((1, d), lambda i: (0, 0))
    outs, carried = _carried_call(
        body, carry, grid=(s // ts,), in_specs=[row, row, vec, row], out_specs=[row, vec],
        out_shape=[jax.ShapeDtypeStruct((s, d), F32), jax.ShapeDtypeStruct((1, d), F32)],
        scratch=[], operands=(dh, x, g, res), name=name)
    return (*outs, carried) if carry else tuple(outs)


MID_TILE = 256


def _gated_branches(y_refs, wup_ref, gl):
    d = D_MODEL
    us = [jnp.dot(y_refs[k][...], wup_ref[k], preferred_element_type=F32) for k in range(3)]
    sg = [_sigmoid(gl[:, k * d:(k + 1) * d]) for k in range(3)]
    return us, sg


def _mid_fwd(ya, yb, ym, proj, x, tgt, w_up, w_out, g_post):
    s, d = x.shape
    ts = MID_TILE

    def body(ya_ref, yb_ref, ym_ref, g_ref, x_ref, t_ref, wup_ref, wout_ref, gp_ref,
             m_ref, do_ref, dy_ref, dg_ref, loss_ref):
        us, sg = _gated_branches((ya_ref, yb_ref, ym_ref), wup_ref, g_ref[...])
        merged = (sg[0] * us[0] + sg[1] * us[1] + sg[2] * us[2]).astype(BF16)
        m_ref[...] = merged
        ov = jnp.dot(merged, wout_ref[...], preferred_element_type=F32)
        r = lax.rsqrt(jnp.mean(ov * ov, axis=-1, keepdims=True) + EPS)
        nh = ov * r
        gv = gp_ref[...]
        e = (x_ref[...] + nh * gv) - t_ref[...]
        lpart = 0.5 * jnp.sum(jnp.mean(e * e, axis=-1, keepdims=True), axis=0, keepdims=True)
        dy = e * (1.0 / d)
        dgp = jnp.sum(dy * nh, axis=0, keepdims=True)

        @pl.when(pl.program_id(0) == 0)
        def _():
            dg_ref[...] = dgp
            loss_ref[...] = jnp.broadcast_to(lpart, loss_ref.shape)

        @pl.when(pl.program_id(0) > 0)
        def _():
            dg_ref[...] += dgp
            loss_ref[...] += jnp.broadcast_to(lpart, loss_ref.shape)

        dn = dy * gv
        dy_ref[...] = dy
        do_ref[...] = (r * (dn - nh * jnp.mean(dn * nh, axis=-1, keepdims=True))).astype(BF16)

    row = pl.BlockSpec((ts, d), lambda i: (i, 0))
    ysp = pl.BlockSpec((ts, A_WIDTH), lambda i: (i, 0))
    vec = pl.BlockSpec((1, d), lambda i: (0, 0))
    return pl.pallas_call(
        body, grid=(s // ts,),
        in_specs=[ysp, ysp, ysp, pl.BlockSpec((ts, W_G), lambda i: (i, COL_G)), row, row,
                  pl.BlockSpec((3, A_WIDTH, d), lambda i: (0, 0, 0)), pl.BlockSpec((d, d), lambda i: (0, 0)), vec],
        out_specs=[row, row, row, vec, pl.BlockSpec((1, LANES), lambda i: (0, 0))],
        out_shape=[jax.ShapeDtypeStruct((s, d), BF16), jax.ShapeDtypeStruct((s, d), BF16),
                   jax.ShapeDtypeStruct((s, d), F32), jax.ShapeDtypeStruct((1, d), F32),
                   jax.ShapeDtypeStruct((1, LANES), F32)],
        name="mid_fwd", compiler_params=_params(heavy=True))(ya, yb, ym, proj, x, tgt, w_up, w_out, g_post)


def _mid_bwd(d_out, merged, ya, yb, ym, proj, w_up, w_out):
    s, d = merged.shape
    ts = MID_TILE
    last = s // ts - 1

    def body(do_ref, m_ref, ya_ref, yb_ref, ym_ref, g_ref, wup_ref, wout_ref,
             dp_ref, dya_ref, dyb_ref, dym_ref, dwup_hbm, dwout_hbm, dwup_acc, dwout_acc):
        i = pl.program_id(0)

        @pl.when(i == 0)
        def _():
            dwup_acc[...] = jnp.zeros_like(dwup_acc)
            dwout_acc[...] = jnp.zeros_like(dwout_acc)

        y_refs = (ya_ref, yb_ref, ym_ref)
        us, sg = _gated_branches(y_refs, wup_ref, g_ref[...])
        dov = do_ref[...]
        dwout_acc[...] += lax.dot_general(m_ref[...], dov, _DIMS["tn"], preferred_element_type=F32)
        dm = lax.dot_general(dov, wout_ref[...], _DIMS["nt"], preferred_element_type=F32)
        for k, dy_ref in enumerate((dya_ref, dyb_ref, dym_ref)):
            dp_ref[:, k * d:(k + 1) * d] = ((dm * us[k]) * (sg[k] * (1.0 - sg[k]))).astype(BF16)
            du = (sg[k] * dm).astype(BF16)
            dy_ref[...] = lax.dot_general(du, wup_ref[k], _DIMS["nt"], preferred_element_type=F32)
            dwup_acc[k] += lax.dot_general(y_refs[k][...], du, _DIMS["tn"], preferred_element_type=F32)

        @pl.when(i == last)
        def _():
            pltpu.sync_copy(dwup_acc, dwup_hbm)
            pltpu.sync_copy(dwout_acc, dwout_hbm)

    row = pl.BlockSpec((ts, d), lambda i: (i, 0))
    ysp = pl.BlockSpec((ts, A_WIDTH), lambda i: (i, 0))
    gsp = pl.BlockSpec((ts, W_G), lambda i: (i, COL_G))
    anysp = pl.BlockSpec(memory_space=pl.ANY)
    yshape = jax.ShapeDtypeStruct((s, A_WIDTH), F32)
    return pl.pallas_call(
        body, grid=(s // ts,),
        in_specs=[row, row, ysp, ysp, ysp, gsp, pl.BlockSpec((3, A_WIDTH, d), lambda i: (0, 0, 0)),
                  pl.BlockSpec((d, d), lambda i: (0, 0))],
        out_specs=[gsp, ysp, ysp, ysp, anysp, anysp],
        out_shape=[jax.ShapeDtypeStruct((s, IN_WIDTH), BF16), yshape, yshape, yshape,
                   jax.ShapeDtypeStruct((3, A_WIDTH, d), F32), jax.ShapeDtypeStruct((d, d), F32)],
        scratch_shapes=[pltpu.VMEM((3, A_WIDTH, d), F32), pltpu.VMEM((d, d), F32)],
        name="mid_bwd", compiler_params=_params(heavy=True))(d_out, merged, ya, yb, ym, proj, w_up, w_out)


def _conv_core(blk, prev, nxt, w, i, last, ts):
    c = A_WIDTH
    ab, ac, ax, az = blk[:, :c], blk[:, c:2 * c], blk[:, 2 * c:3 * c], blk[:, 3 * c:]
    cu = ac * ax
    cu_prev = (prev[7:8, c:2 * c] * prev[7:8, 2 * c:3 * c]) * jnp.where(i > 0, 1.0, 0.0)
    cu_next = (nxt[0:1, c:2 * c] * nxt[0:1, 2 * c:3 * c]) * jnp.where(i < last, 1.0, 0.0)
    row = lax.broadcasted_iota(jnp.int32, (ts, c), 0)
    cm1 = jnp.where(row == 0, cu_prev, pltpu.roll(cu, 1, 0))
    cp1 = jnp.where(row == ts - 1, cu_next, pltpu.roll(cu, ts - 1, 0))
    yc = cm1 * w[0:1] + cu * w[1:2] + cp1 * w[2:3]
    return ab, ac, ax, az, cu, cm1, cp1, yc, row


def _halo_specs(ts, width, col, nblk8):
    prev = pl.BlockSpec((8, width), lambda i: (jnp.maximum(i * (ts // 8) - 1, 0), col))
    nxt = pl.BlockSpec((8, width), lambda i: (jnp.minimum((i + 1) * (ts // 8), nblk8 - 1), col))
    return prev, nxt


def _conv_fwd(proj, w_conv):
    s = proj.shape[0]
    ts = 256
    last = s // ts - 1

    def body(a_ref, ap_ref, an_ref, w_ref, ya_ref):
        i = pl.program_id(0)
        ab, _, _, az, _, _, _, yc, _ = _conv_core(a_ref[...], ap_ref[...], an_ref[...], w_ref[...], i, last, ts)
        ya_ref[...] = ((ab * yc) * (az * _sigmoid(az))).astype(BF16)

    prev, nxt = _halo_specs(ts, W_A, COL_A, s // 8)
    return pl.pallas_call(
        body, grid=(s // ts,),
        in_specs=[pl.BlockSpec((ts, W_A), lambda i: (i, COL_A)), prev, nxt,
                  pl.BlockSpec((3, A_WIDTH), lambda i: (0, 0))],
        out_specs=pl.BlockSpec((ts, A_WIDTH), lambda i: (i, 0)),
        out_shape=jax.ShapeDtypeStruct((s, A_WIDTH), BF16), name="conv_fwd",
        compiler_params=_params())(proj, proj, proj, w_conv)


def _conv_bwd(proj, w_conv, dya, dproj):
    s = proj.shape[0]
    ts = 256
    last = s // ts - 1
    c = A_WIDTH

    def body(a_ref, ap_ref, an_ref, w_ref, d_ref, dp_ref, dn_ref, _, dproj_ref, dw_ref):
        i = pl.program_id(0)
        w = w_ref[...]
        prev, nxt = ap_ref[...], an_ref[...]
        ab, ac, ax, az, cu, cm1, cp1, yc, row = _conv_core(a_ref[...], prev, nxt, w, i, last, ts)
        sg = _sigmoid(az)
        sz = az * sg
        dya_v = d_ref[...]
        dyc = dya_v * sz * ab
        dproj_ref[:, :c] = (dya_v * sz * yc).astype(BF16)
        dproj_ref[:, 3 * c:] = (dya_v * (ab * yc) * (sg * (1.0 + az * (1.0 - sg)))).astype(BF16)

        def halo_dyc(a_row, d_row):
            azr = a_row[:, 3 * c:]
            return d_row * (azr * _sigmoid(azr)) * a_row[:, :c]

        dyc_prev = halo_dyc(prev[7:8], dp_ref[...][7:8]) * jnp.where(i > 0, 1.0, 0.0)
        dyc_next = halo_dyc(nxt[0:1], dn_ref[...][0:1]) * jnp.where(i < last, 1.0, 0.0)
        dyc_m1 = jnp.where(row == 0, dyc_prev, pltpu.roll(dyc, 1, 0))
        dyc_p1 = jnp.where(row == ts - 1, dyc_next, pltpu.roll(dyc, ts - 1, 0))
        dcu = dyc_p1 * w[0:1] + dyc * w[1:2] + dyc_m1 * w[2:3]
        dproj_ref[:, c:2 * c] = (dcu * ax).astype(BF16)
        dproj_ref[:, 2 * c:3 * c] = (dcu * ac).astype(BF16)
        dw = [jnp.sum(dyc * t, axis=0, keepdims=True) for t in (cm1, cu, cp1)]

        @pl.when(i == 0)
        def _():
            for k in range(3):
                dw_ref[k:k + 1, :] = dw[k]

        @pl.when(i > 0)
        def _():
            for k in range(3):
                dw_ref[k:k + 1, :] += dw[k]

    prev, nxt = _halo_specs(ts, W_A, COL_A, s // 8)
    dprev, dnxt = _halo_specs(ts, A_WIDTH, 0, s // 8)
    return pl.pallas_call(
        body, grid=(s // ts,),
        in_specs=[pl.BlockSpec((ts, W_A), lambda i: (i, COL_A)), prev, nxt,
                  pl.BlockSpec((3, A_WIDTH), lambda i: (0, 0)),
                  pl.BlockSpec((ts, A_WIDTH), lambda i: (i, 0)), dprev, dnxt,
                  pl.BlockSpec(memory_space=pl.ANY)],
        out_specs=[pl.BlockSpec((ts, W_A), lambda i: (i, COL_A)), pl.BlockSpec((3, A_WIDTH), lambda i: (0, 0))],
        out_shape=[jax.ShapeDtypeStruct(dproj.shape, BF16), jax.ShapeDtypeStruct((3, A_WIDTH), F32)],
        input_output_aliases={7: 0}, name="conv_bwd",
        compiler_params=_params())(proj, proj, proj, w_conv, dya, dya, dya, dproj)


def _rope_tables(s):
    half = ROT_DIM // 2
    dim = jnp.arange(LANES) % HEAD_DIM
    inv_freq = jnp.power(jnp.float32(ROPE_THETA), -(dim % half).astype(F32) * (2.0 / ROT_DIM))
    ang = jnp.arange(s).astype(F32)[:, None] * inv_freq[None, :]
    cos, sin = jnp.cos(ang), jnp.sin(ang)
    first, second = (dim < half)[None, :], ((dim >= half) & (dim < ROT_DIM))[None, :]
    c = jnp.where(first | second, cos, 1.0)
    s1 = jnp.where(first, -sin, 0.0)
    s2 = jnp.where(second, sin, 0.0)
    return jnp.concatenate([c, s1, s2], axis=1)


def _rope(t, tab):
    return (t * tab[:, :LANES] + pltpu.roll(t, LANES - 8, 1) * tab[:, LANES:2 * LANES]
            + pltpu.roll(t, 8, 1) * tab[:, 2 * LANES:])


def _rope_transpose(dt, tab):
    return (dt * tab[:, :LANES] + pltpu.roll(dt * tab[:, LANES:2 * LANES], 8, 1)
            + pltpu.roll(dt * tab[:, 2 * LANES:], LANES - 8, 1))


def _rope_kv(proj, tab):
    s = proj.shape[0]
    nb = s // KV_PAD

    def body(kv_ref, t_ref, k_ref, v_ref):
        j = pl.program_id(0)
        inside = jnp.where((j > 0) & (j <= nb), 1.0, 0.0)
        kv = kv_ref[...]
        k_ref[...] = (_rope(kv[:, :LANES], t_ref[...]) * inside).astype(BF16)
        v_ref[...] = (kv[:, LANES:] * inside).astype(BF16)

    def src(j):
        return jnp.clip(j - 1, 0, nb - 1)

    o_spec = pl.BlockSpec((KV_PAD, LANES), lambda j: (j, 0))
    shp = jax.ShapeDtypeStruct((s + 2 * KV_PAD, LANES), BF16)
    return pl.pallas_call(
        body, grid=(nb + 2,),
        in_specs=[pl.BlockSpec((KV_PAD, W_KV), lambda j: (src(j), COL_KV)),
                  pl.BlockSpec((KV_PAD, 3 * LANES), lambda j: (src(j), 0))],
        out_specs=[o_spec, o_spec], out_shape=[shp, shp], name="rope_kv",
        compiler_params=_params())(proj, tab)


def _rope_kv_bwd(dkpad, dvpad, tab, dproj):
    s = tab.shape[0]
    nb = s // KV_PAD

    def body(dk_ref, dv_ref, t_ref, _, dp_ref):
        dp_ref[:, :LANES] = _rope_transpose(dk_ref[...], t_ref[...]).astype(BF16)
        dp_ref[:, LANES:] = dv_ref[...].astype(BF16)

    pad_spec = pl.BlockSpec((KV_PAD, LANES), lambda j: (j + 1, 0))
    return pl.pallas_call(
        body, grid=(nb,),
        in_specs=[pad_spec, pad_spec, pl.BlockSpec((KV_PAD, 3 * LANES), lambda j: (j, 0)),
                  pl.BlockSpec(memory_space=pl.ANY)],
        out_specs=pl.BlockSpec((KV_PAD, W_KV), lambda j: (j, COL_KV)),
        out_shape=jax.ShapeDtypeStruct(dproj.shape, BF16), input_output_aliases={3: 0},
        name="rope_kv_bwd", compiler_params=_params())(dkpad, dvpad, tab, dproj)


def _window_start(n):
    return pl.multiple_of((n - 1) * WINDOW_BLOCK + KV_PAD, WINDOW_BLOCK)


def _window_operands(k_ref, v_ref, n, lo):
    start = _window_start(n)
    kw = k_ref[pl.ds(start, 3 * WINDOW_BLOCK), :].astype(F32)
    vw = v_ref[pl.ds(start, 3 * WINDOW_BLOCK), :].astype(F32)
    kr, vr = pltpu.roll(kw, HALF_LANES, 1), pltpu.roll(vw, HALF_LANES, 1)
    k2 = (jnp.where(lo, kw, kr).astype(BF16), jnp.where(lo, kr, kw).astype(BF16))
    v2 = (jnp.where(lo, vw, vr).astype(BF16), jnp.where(lo, vr, vw).astype(BF16))
    return k2, v2


HEADS_PER_GROUP = 4
SWA_FWD_BLOCKS = 1
SWA_BWD_BLOCKS = 2


def _window_bias():
    wb = WINDOW_BLOCK
    qi = lax.broadcasted_iota(jnp.int32, (wb, 3 * wb), 0)
    kj = lax.broadcasted_iota(jnp.int32, (wb, 3 * wb), 1)
    band = (kj >= qi) & (kj <= qi + 2 * wb)
    cases = jnp.stack([band & (kj >= wb), band, band & (kj < 2 * wb)])
    return jnp.where(cases, 0.0, -jnp.inf).astype(F32)


def _block_bias(bias_ref, n, n_blocks):
    case = jnp.where(n == 0, 0, jnp.where(n == n_blocks - 1, 2, 1))
    one = bias_ref[case]
    return jnp.concatenate([one] * HEADS_PER_GROUP, axis=0)


def _stack_heads(pair0, pair1, lo):
    return jnp.concatenate([jnp.where(lo, pair0, 0.0), jnp.where(lo, 0.0, pair0),
                            jnp.where(lo, pair1, 0.0), jnp.where(lo, 0.0, pair1)], axis=0)


def _unstack_pair(stacked, i, lo):
    wb = WINDOW_BLOCK
    return jnp.where(lo, stacked[2 * i * wb:(2 * i + 1) * wb], stacked[(2 * i + 1) * wb:(2 * i + 2) * wb])


def _sink_column(sink_ref, g):
    wb = WINDOW_BLOCK
    return jnp.concatenate([jnp.full((wb, 1), sink_ref[0, HEADS_PER_GROUP * g + i], F32)
                            for i in range(HEADS_PER_GROUP)], axis=0)


def _head_exp(q4, k2g, bias, sink):
    sc = lax.dot_general(q4, k2g, _DIMS["nt"], preferred_element_type=F32) * (HEAD_DIM ** -0.5) + bias
    m = jnp.maximum(jnp.max(sc, axis=1, keepdims=True), sink)
    return jnp.exp(sc - m).astype(BF16), jnp.exp(sink - m)


def _swa_fwd(proj, kpad, vpad, tab, bias, sink, *, carry=None):
    s = proj.shape[0]
    wb = WINDOW_BLOCK

    def body(b_ref, k_ref, v_ref, t_ref, bias_ref, sink_ref, o_ref, y_ref):
        lo = lax.broadcasted_iota(jnp.int32, (wb, LANES), 1) < HALF_LANES
        lo_w = lax.broadcasted_iota(jnp.int32, (3 * wb, LANES), 1) < HALF_LANES
        for sub in range(SWA_FWD_BLOCKS):
            n = pl.program_id(0) * SWA_FWD_BLOCKS + sub
            rows = slice(sub * wb, (sub + 1) * wb)
            k2, v2 = _window_operands(k_ref, v_ref, n, lo_w)
            valid = _block_bias(bias_ref, n, s // wb)
            tab_v = t_ref[rows, :]
            ones = jnp.ones((3 * wb, LANES), BF16)
            for g in range(2):
                qr = [_rope(b_ref[rows, (2 * g + i) * LANES:(2 * g + i + 1) * LANES], tab_v) for i in range(2)]
                q4 = _stack_heads(qr[0], qr[1], lo).astype(BF16)
                e, es = _head_exp(q4, k2[g], valid, _sink_column(sink_ref, g))
                ox = jnp.dot(e, jnp.concatenate([v2[g], ones], axis=1), preferred_element_type=F32)
                o4 = ox[:, :LANES] * (1.0 / (ox[:, LANES:] + es))
                for i in range(2):
                    cols = slice((2 * g + i) * LANES, (2 * g + i + 1) * LANES)
                    op = _unstack_pair(o4, i, lo)
                    o_ref[rows, cols] = op
                    zp = b_ref[rows, A_WIDTH + cols.start:A_WIDTH + cols.stop]
                    y_ref[rows, cols] = (op * (zp * _sigmoid(zp))).astype(BF16)

    tq = SWA_FWD_BLOCKS * wb
    pad_spec = pl.BlockSpec((s + 2 * KV_PAD, LANES), lambda n: (0, 0))
    o_spec = pl.BlockSpec((tq, A_WIDTH), lambda n: (n, 0))
    outs, carried = _carried_call(
        lambda ins, outs, scr: body(*ins, *outs), carry, grid=(s // tq,),
        in_specs=[pl.BlockSpec((tq, W_B), lambda n: (n, COL_B)), pad_spec, pad_spec,
                  pl.BlockSpec((tq, 3 * LANES), lambda n: (n, 0)),
                  pl.BlockSpec(bias.shape, lambda n: (0, 0, 0)), pl.BlockSpec(memory_space=pltpu.SMEM)],
        out_specs=[o_spec, o_spec],
        out_shape=[jax.ShapeDtypeStruct((s, A_WIDTH), F32), jax.ShapeDtypeStruct((s, A_WIDTH), BF16)],
        scratch=[], operands=(proj, kpad, vpad, tab, bias, sink), name="swa_fwd")
    return (*outs, carried) if carry else tuple(outs)


def _swa_bwd(proj, kpad, vpad, tab, bias, sink, o_attn, dyb, dproj):
    s = proj.shape[0]
    wb = WINDOW_BLOCK
    scale = HEAD_DIM ** -0.5

    def body(b_ref, k_ref, v_ref, t_ref, bias_ref, sink_ref, o_ref, dy_ref, _, dp_ref, dk_ref, dv_ref, ds_ref):
        @pl.when(pl.program_id(0) == 0)
        def _():
            dk_ref[...] = jnp.zeros_like(dk_ref)
            dv_ref[...] = jnp.zeros_like(dv_ref)
            ds_ref[...] = jnp.zeros_like(ds_ref)

        lo = lax.broadcasted_iota(jnp.int32, (wb, LANES), 1) < HALF_LANES
        lo_w = lax.broadcasted_iota(jnp.int32, (3 * wb, LANES), 1) < HALF_LANES
        for sub in range(SWA_BWD_BLOCKS):
            n = pl.program_id(0) * SWA_BWD_BLOCKS + sub
            rows = slice(sub * wb, (sub + 1) * wb)
            k2, v2 = _window_operands(k_ref, v_ref, n, lo_w)
            valid = _block_bias(bias_ref, n, s // wb)
            tab_v = t_ref[rows, :]
            ones = jnp.ones((3 * wb, LANES), BF16)
            dks, dvs = [], []
            for g in range(2):
                qr, op, do = [], [], []
                for i in range(2):
                    cols = slice((2 * g + i) * LANES, (2 * g + i + 1) * LANES)
                    zcols = slice(A_WIDTH + cols.start, A_WIDTH + cols.stop)
                    qr.append(_rope(b_ref[rows, cols], tab_v))
                    zp = b_ref[rows, zcols]
                    sg = _sigmoid(zp)
                    op.append(o_ref[rows, cols])
                    dyp = dy_ref[rows, cols]
                    do.append(dyp * (zp * sg))
                    dp_ref[rows, zcols] = (dyp * op[i] * (sg * (1.0 + zp * (1.0 - sg)))).astype(BF16)
                q4 = _stack_heads(qr[0], qr[1], lo).astype(BF16)
                do4 = _stack_heads(do[0], do[1], lo)
                o4 = jnp.concatenate([op[0], op[0], op[1], op[1]], axis=0)
                e, es = _head_exp(q4, k2[g], valid, _sink_column(sink_ref, g))
                inv = 1.0 / (jnp.dot(e, ones, preferred_element_type=F32) + es)
                prob = e.astype(F32) * jnp.concatenate([inv, inv, inv], axis=1)
                delta = jnp.sum(do4 * o4, axis=1, keepdims=True)
                do4b = do4.astype(BF16)
                dprob = lax.dot_general(do4b, v2[g], _DIMS["nt"], preferred_element_type=F32)
                dsc = (prob * (dprob - delta)).astype(BF16)
                sink_terms = (es * inv[:, :1]) * delta
                for i in range(HEADS_PER_GROUP):
                    h = HEADS_PER_GROUP * g + i
                    dsink = -jnp.sum(sink_terms[i * wb:(i + 1) * wb], axis=0, keepdims=True)
                    ds_ref[h:h + 1, :] += jnp.broadcast_to(dsink, (1, LANES))
                dq4 = jnp.dot(dsc, k2[g], preferred_element_type=F32) * scale
                for i in range(2):
                    cols = slice((2 * g + i) * LANES, (2 * g + i + 1) * LANES)
                    dp_ref[rows, cols] = _rope_transpose(_unstack_pair(dq4, i, lo), tab_v).astype(BF16)
                dk2 = lax.dot_general(dsc, q4, _DIMS["tn"], preferred_element_type=F32) * scale
                dv2 = lax.dot_general(prob.astype(BF16), do4b, _DIMS["tn"], preferred_element_type=F32)
                dks.append(dk2 + pltpu.roll(dk2, HALF_LANES, 1))
                dvs.append(dv2 + pltpu.roll(dv2, HALF_LANES, 1))
            start = _window_start(n)
            dk_ref[pl.ds(start, 3 * wb), :] += jnp.where(lo_w, dks[0], dks[1])
            dv_ref[pl.ds(start, 3 * wb), :] += jnp.where(lo_w, dvs[0], dvs[1])

    tq = SWA_BWD_BLOCKS * wb
    pad_spec = pl.BlockSpec((s + 2 * KV_PAD, LANES), lambda n: (0, 0))
    blk = pl.BlockSpec((tq, A_WIDTH), lambda n: (n, 0))
    bsp = pl.BlockSpec((tq, W_B), lambda n: (n, COL_B))
    pad_shape = jax.ShapeDtypeStruct((s + 2 * KV_PAD, LANES), F32)
    return pl.pallas_call(
        body, grid=(s // tq,),
        in_specs=[bsp, pad_spec, pad_spec, pl.BlockSpec((tq, 3 * LANES), lambda n: (n, 0)),
                  pl.BlockSpec(bias.shape, lambda n: (0, 0, 0)), pl.BlockSpec(memory_space=pltpu.SMEM), blk, blk,
                  pl.BlockSpec(memory_space=pl.ANY)],
        out_specs=[bsp, pad_spec, pad_spec, pl.BlockSpec((8, LANES), lambda n: (0, 0))],
        out_shape=[jax.ShapeDtypeStruct(dproj.shape, BF16), pad_shape, pad_shape,
                   jax.ShapeDtypeStruct((8, LANES), F32)],
        input_output_aliases={8: 0}, name="swa_bwd",
        compiler_params=_params())(proj, kpad, vpad, tab, bias, sink, o_attn, dyb, dproj)


def _mem_exp(qh, mk):
    sc = lax.dot_general(qh, mk, _DIMS["nt"], preferred_element_type=F32) * (MEM_HEAD_DIM ** -0.5)
    return jnp.exp(sc - jnp.max(sc, axis=1, keepdims=True)).astype(BF16)


def _mem_fwd(proj, mkv):
    s = proj.shape[0]
    ts = 512
    mlen = mkv.shape[0]

    def body(m_ref, kv_ref, o_ref, y_ref):
        ones = jnp.ones((mlen, LANES), BF16)
        for h in range(MEM_HEADS):
            cols = slice(h * LANES, (h + 1) * LANES)
            mk = kv_ref[:, cols].astype(BF16)
            mv = kv_ref[:, MEM_WIDTH + h * LANES:MEM_WIDTH + (h + 1) * LANES].astype(BF16)
            e = _mem_exp(m_ref[:, cols].astype(BF16), mk)
            ox = jnp.dot(e, jnp.concatenate([mv, ones], axis=1), preferred_element_type=F32)
            oh = ox[:, :LANES] * (1.0 / ox[:, LANES:])
            o_ref[:, cols] = oh
            zh = m_ref[:, MEM_WIDTH + h * LANES:MEM_WIDTH + (h + 1) * LANES]
            y_ref[:, cols] = (oh * (zh * _sigmoid(zh))).astype(BF16)

    o_spec = pl.BlockSpec((ts, MEM_WIDTH), lambda i: (i, 0))
    return pl.pallas_call(
        body, grid=(s // ts,),
        in_specs=[pl.BlockSpec((ts, W_M), lambda i: (i, COL_M)),
                  pl.BlockSpec((mlen, 2 * MEM_WIDTH), lambda i: (0, 0))],
        out_specs=[o_spec, o_spec],
        out_shape=[jax.ShapeDtypeStruct((s, MEM_WIDTH), F32), jax.ShapeDtypeStruct((s, MEM_WIDTH), BF16)],
        name="mem_fwd", compiler_params=_params())(proj, mkv)


def _mem_bwd(proj, mkv, o_mem, dym, dproj, *, carry=None):
    s = proj.shape[0]
    ts = 512
    mlen = mkv.shape[0]
    scale = MEM_HEAD_DIM ** -0.5

    def body(m_ref, kv_ref, o_ref, dy_ref, _, dp_ref, dkv_ref):
        @pl.when(pl.program_id(0) == 0)
        def _():
            dkv_ref[...] = jnp.zeros_like(dkv_ref)

        ones = jnp.ones((mlen, LANES), BF16)
        for h in range(MEM_HEADS):
            cols = slice(h * LANES, (h + 1) * LANES)
            vcols = slice(MEM_WIDTH + h * LANES, MEM_WIDTH + (h + 1) * LANES)
            mk = kv_ref[:, cols].astype(BF16)
            mv = kv_ref[:, vcols].astype(BF16)
            qh = m_ref[:, cols].astype(BF16)
            zh = m_ref[:, vcols]
            sg = _sigmoid(zh)
            oh = o_ref[:, cols]
            dyh = dy_ref[:, cols]
            doh = dyh * (zh * sg)
            dp_ref[:, vcols] = (dyh * oh * (sg * (1.0 + zh * (1.0 - sg)))).astype(BF16)
            e = _mem_exp(qh, mk)
            inv = 1.0 / jnp.dot(e, ones, preferred_element_type=F32)
            prob = e.astype(F32) * jnp.concatenate([inv] * (mlen // LANES), axis=1)
            delta = jnp.sum(doh * oh, axis=1, keepdims=True)
            dohb = doh.astype(BF16)
            dprob = lax.dot_general(dohb, mv, _DIMS["nt"], preferred_element_type=F32)
            dsc = (prob * (dprob - delta)).astype(BF16)
            dp_ref[:, cols] = (jnp.dot(dsc, mk, preferred_element_type=F32) * scale).astype(BF16)
            dkv_ref[:, cols] += lax.dot_general(dsc, qh, _DIMS["tn"], preferred_element_type=F32) * scale
            dkv_ref[:, vcols] += lax.dot_general(prob.astype(BF16), dohb, _DIMS["tn"],
                                                 preferred_element_type=F32)

    blk = pl.BlockSpec((ts, MEM_WIDTH), lambda i: (i, 0))
    msp = pl.BlockSpec((ts, W_M), lambda i: (i, COL_M))
    kvsp = pl.BlockSpec((mlen, 2 * MEM_WIDTH), lambda i: (0, 0))
    outs, carried = _carried_call(
        lambda ins, outs, scr: body(*ins, *outs), carry, grid=(s // ts,),
        in_specs=[msp, kvsp, blk, blk, pl.BlockSpec(memory_space=pl.ANY)],
        out_specs=[msp, kvsp],
        out_shape=[jax.ShapeDtypeStruct(dproj.shape, BF16), jax.ShapeDtypeStruct(mkv.shape, F32)],
        scratch=[], operands=(proj, mkv, o_mem, dym, dproj), name="mem_bwd", aliases={4: 0})
    return (*outs, carried) if carry else tuple(outs)


def _forward_backward(x, mem, tgt, proj, w_conv, sink, g_mem, late_weights, g_post, early_exchange):
    s = x.shape[0]
    tab = _rope_tables(s)
    bias = _window_bias()

    ya = _conv_fwd(proj, w_conv)
    kpad, vpad = _rope_kv(proj, tab)
    o_attn, yb, *arrived = _swa_fwd(proj, kpad, vpad, tab, bias, sink, carry=late_weights[0])
    w_kv, w_up, w_out = late_weights[1](arrived[0] if arrived else None)
    mn = _rmsnorm_fwd(mem, g_mem, name="mem_norm")
    mkv = _matmul(mn, w_kv, mode="nn", out_dtype=F32, tm=256, tn=1024, tk=D_MODEL, name="mem_kv")
    o_mem, ym = _mem_fwd(proj, mkv)
    merged, d_out, dy, dg_post, loss = _mid_fwd(ya, yb, ym, proj, x, tgt, w_up, w_out, g_post)
    dproj, d_ya, d_yb, d_ym, dw_up, dw_out = _mid_bwd(d_out, merged, ya, yb, ym, proj, w_up, w_out)

    dproj, dw_conv = _conv_bwd(proj, w_conv, d_ya, dproj)
    dproj, dkpad, dvpad, dsink = _swa_bwd(proj, kpad, vpad, tab, bias, sink, o_attn, d_yb, dproj)
    dproj = _rope_kv_bwd(dkpad, dvpad, tab, dproj)
    dproj, d_mkv, *early = _mem_bwd(proj, mkv, o_mem, d_ym, dproj, carry=early_exchange(dw_up, dw_out))

    dw_kv = _matmul(mn, d_mkv, mode="tn", out_dtype=F32, tm=1024, tn=1024, tk=256, name="dw_kv")
    d_mn = _matmul(d_mkv, w_kv, mode="nt", out_dtype=F32, tm=256, tn=1024, tk=D_MODEL, name="d_mn")
    _, dg_mem = _rmsnorm_bwd(d_mn, mem, g_mem, d_mn, name="mem_norm_bwd")

    return dict(loss=loss, dproj=dproj, dy=dy, w_conv=dw_conv, sink=dsink, g_mem=dg_mem,
                w_kv=dw_kv, w_up=dw_up, w_out=dw_out, g_post=dg_post, early=early[0] if early else None)


N_DEV = 8


def _position():
    return lax.axis_index("x"), lax.axis_index("y"), lax.axis_index("c")


def _other_chips(x, y):
    return (((1 - x, y), 2 * (1 - x) + y), ((x, 1 - y), 2 * x + (1 - y)), ((1 - x, 1 - y), 2 * (1 - x) + (1 - y)))


def _remote(src, dst, send_sems, recv_sems, k, device):
    return pltpu.make_async_remote_copy(src_ref=src, dst_ref=dst, send_sem=send_sems.at[k], recv_sem=recv_sems.at[k],
                                        device_id=device, device_id_type=MESH)


def _rows_half(ref, hf):
    rh = ref.shape[0] // 2
    return ref.at[pl.ds(pl.multiple_of(hf * rh, 8), rh)]


def _gather_weights(shards, small=None, relations=(0, 1, 2), into=None):
    n = len(shards)
    k = 0 if small is None else 1

    def peers(x, y):
        return [(r, chip, idx) for r, (chip, idx) in enumerate(_other_chips(x, y)) if r in relations]

    def ici(ins, outs, sems, a, r, chip, src_chip, c):
        return _remote(_rows_half(ins[a], c), _rows_half(outs[a].at[src_chip], c), sems[0], sems[1], 3 * a + r,
                       (*chip, c))

    def whole(ins, outs, sems, r, chip, src_chip, c):
        return _remote(ins[n], outs[n].at[src_chip], sems[0], sems[1], 3 * n + r, (*chip, c))

    def d2d(outs, sems, a, r, idx, hf, x, y, c):
        half = _rows_half(outs[a].at[idx], hf)
        return _remote(half, half, sems[2], sems[3], 3 * a + r, (x, y, 1 - c))

    def start(ins, outs, sems):
        x, y, c = _position()
        me = 2 * x + y
        for a in range(n):
            for r, chip, _ in peers(x, y):
                ici(ins, outs, sems, a, r, chip, me, c).start()
        for r, (chip, _) in enumerate(_other_chips(x, y)):
            if k:
                whole(ins, outs, sems, r, chip, me, c).start()

    def finish(ins, outs, sems):
        x, y, c = _position()
        me = 2 * x + y
        for a in range(n):
            for r, chip, idx in peers(x, y):
                ici(ins, outs, sems, a, r, chip, idx, c).wait_recv()
                d2d(outs, sems, a, r, idx, c, x, y, c).start()
        for a in range(n):
            for r, chip, idx in peers(x, y):
                d2d(outs, sems, a, r, idx, 1 - c, x, y, c).wait_recv()
        for r, (chip, idx) in enumerate(_other_chips(x, y)):
            if k:
                whole(ins, outs, sems, r, chip, idx, c).wait_recv()
                whole(ins, outs, sems, r, chip, me, c).wait_send()
        for a in range(n):
            for r, chip, idx in peers(x, y):
                ici(ins, outs, sems, a, r, chip, me, c).wait_send()
                d2d(outs, sems, a, r, idx, c, x, y, c).wait_send()

    operands = list(shards) + ([small] if k else [])
    shapes = [jax.ShapeDtypeStruct((N_CHIPS,) + s.shape, s.dtype) for s in operands]
    aliases = {}
    if into is not None:
        assert len(into) == len(operands)
        aliases = {len(operands) + a: a for a in range(len(into))}
        operands += list(into)
    return _Carry(operands, shapes,
                  [pltpu.SemaphoreType.DMA((3 * (n + k),)), pltpu.SemaphoreType.DMA((3 * (n + k),)),
                   pltpu.SemaphoreType.DMA((3 * n,)), pltpu.SemaphoreType.DMA((3 * n,))], start, finish, aliases)


def _pair_exchange(send):
    n = len(send)

    def copies(ins, outs, sems):
        x, y, c = _position()
        return [_remote(ins[a], outs[a], sems[0], sems[1], a, (x, y, 1 - c)) for a in range(n)]

    def start(ins, outs, sems):
        for cp in copies(ins, outs, sems):
            cp.start()

    def finish(ins, outs, sems):
        for cp in copies(ins, outs, sems):
            cp.wait()

    return _Carry(send, [jax.ShapeDtypeStruct(p.shape, p.dtype) for p in send],
                  [pltpu.SemaphoreType.DMA((n,)), pltpu.SemaphoreType.DMA((n,))], start, finish)


def _chip_exchange(sums):
    n = len(sums)

    def copies(ins, outs, sems):
        x, y, c = _position()
        return [_remote(ins[a].at[idx], outs[a].at[r], sems[0], sems[1], 3 * a + r, (*chip, c))
                for a in range(n) for r, (chip, idx) in enumerate(_other_chips(x, y))]

    def start(ins, outs, sems):
        for cp in copies(ins, outs, sems):
            cp.start()

    def finish(ins, outs, sems):
        for cp in copies(ins, outs, sems):
            cp.wait()

    return _Carry(sums, [jax.ShapeDtypeStruct((3,) + p.shape[1:], p.dtype) for p in sums],
                  [pltpu.SemaphoreType.DMA((3 * n,)), pltpu.SemaphoreType.DMA((3 * n,))], start, finish)


def _pair_share(pairs):
    n = len(pairs)

    def start(ins, outs, sems):
        x, y, c = _position()
        for a in range(n):
            _remote(outs[a].at[c], outs[a].at[c], sems[0], sems[1], a, (x, y, 1 - c)).start()

    def finish(ins, outs, sems):
        x, y, c = _position()
        for a in range(n):
            _remote(outs[a].at[1 - c], outs[a].at[1 - c], sems[0], sems[1], a, (x, y, 1 - c)).wait_recv()
        for a in range(n):
            _remote(outs[a].at[c], outs[a].at[c], sems[0], sems[1], a, (x, y, 1 - c)).wait_send()

    return _Carry(pairs, [jax.ShapeDtypeStruct(p.shape, p.dtype) for p in pairs],
                  [pltpu.SemaphoreType.DMA((n,)), pltpu.SemaphoreType.DMA((n,))], start, finish,
                  aliases={a: a for a in range(n)})


def _small_allreduce(pack, share):
    rows, width = pack.shape
    n_share = len(share.ins)

    def body(p_ref, *refs):
        share_in, o_ref, share_out = refs[:n_share], refs[n_share], refs[n_share + 1:2 * n_share + 1]
        buf, send_sems, recv_sems = refs[2 * n_share + 1:2 * n_share + 4]
        share_sems = refs[2 * n_share + 4:]
        share.start(share_in, share_out, share_sems)
        x, y, c = _position()
        me = 4 * x + 2 * y + c
        buf[me] = p_ref[...]
        peers = []
        for r in range(1, N_DEV):
            fx, fy, fc = (r >> 2) & 1, (r >> 1) & 1, r & 1
            px, py, pc = (1 - x if fx else x), (1 - y if fy else y), (1 - c if fc else c)
            peers.append(((px, py, pc), 4 * px + 2 * py + pc))
        sends = [_remote(p_ref, buf.at[me], send_sems, recv_sems, r, dev) for r, (dev, _) in enumerate(peers)]
        for cp in sends:
            cp.start()
        for r, (dev, idx) in enumerate(peers):
            _remote(p_ref, buf.at[idx], send_sems, recv_sems, r, dev).wait_recv()
        for cp in sends:
            cp.wait_send()
        acc = buf[0]
        for k in range(1, N_DEV):
            acc = acc + buf[k]
        o_ref[...] = acc
        share.finish(share_in, share_out, share_sems)

    vm = pl.BlockSpec(memory_space=pltpu.VMEM)
    red, *shared = pl.pallas_call(
        body, in_specs=[vm] + [_HBM] * n_share, out_specs=[vm] + [_HBM] * n_share,
        out_shape=[jax.ShapeDtypeStruct(pack.shape, F32)] + share.out_shapes,
        scratch_shapes=[pltpu.VMEM((N_DEV, rows, width), F32), pltpu.SemaphoreType.DMA((N_DEV - 1,)),
                        pltpu.SemaphoreType.DMA((N_DEV - 1,))] + share.sems,
        input_output_aliases={1 + i: 1 + o for i, o in share.aliases.items()},
        name="small_allreduce")(pack, *share.ins)
    return red, shared


ROW_TILE_MAX = 512
SUM_TILE_MAX = 2048
BF16_SUBLANES = 16


def _row_tile(rows, most=ROW_TILE_MAX):
    if rows <= most:
        return rows
    return max(t for t in range(BF16_SUBLANES, most + 1, BF16_SUBLANES) if rows % t == 0)


def _pair_add(keep, recv, name):
    nj, rh, cols = keep.shape
    tr = _row_tile(rh, SUM_TILE_MAX)

    def body(k_ref, r_ref, o_ref):
        o_ref[...] = (k_ref[...].astype(F32) + r_ref[...].astype(F32)).astype(BF16)

    blk = pl.BlockSpec((None, tr, cols), lambda j, i: (j, i, 0))
    return pl.pallas_call(body, grid=(nj, rh // tr), in_specs=[blk, blk], out_specs=blk,
                          out_shape=jax.ShapeDtypeStruct(keep.shape, BF16), name=name,
                          compiler_params=_params())(keep, recv)


def _chip_add(sums, recv, where, name):
    _, rh, cols = sums.shape
    tr = _row_tile(rh, SUM_TILE_MAX)

    def body(w_ref, s_ref, r_ref, o_ref):
        o_ref[...] = ((s_ref[...].astype(F32) + r_ref[0].astype(F32)) + r_ref[1].astype(F32)) + r_ref[2].astype(F32)

    grid_spec = pltpu.PrefetchScalarGridSpec(
        num_scalar_prefetch=1, grid=(rh // tr,),
        in_specs=[pl.BlockSpec((None, tr, cols), lambda i, w_ref: (w_ref[0], i, 0)),
                  pl.BlockSpec((3, tr, cols), lambda i, w_ref: (0, i, 0))],
        out_specs=pl.BlockSpec((None, tr, cols), lambda i, w_ref: (w_ref[1], i, 0)))
    return pl.pallas_call(body, grid_spec=grid_spec, out_shape=jax.ShapeDtypeStruct((2, rh, cols), F32),
                          name=name, compiler_params=_params())(where, sums, recv)


def _adamw(w, g, m, v, name):
    rows, cols = w.shape
    tr = _row_tile(rows)
    assert rows % tr == 0

    def body(w_ref, g_ref, m_ref, v_ref, d_ref, mo_ref, vo_ref):
        gv = g_ref[...]
        m_new = ADAM_B1 * m_ref[...] + (1.0 - ADAM_B1) * gv
        v_new = ADAM_B2 * v_ref[...] + (1.0 - ADAM_B2) * jnp.square(gv)
        m_hat = m_new / (1.0 - ADAM_B1 ** ADAM_STEP)
        v_hat = v_new / (1.0 - ADAM_B2 ** ADAM_STEP)
        d_ref[...] = -ADAM_LR * (m_hat / (jnp.sqrt(v_hat) + ADAM_EPS) + ADAM_WD * w_ref[...])
        mo_ref[...] = m_new
        vo_ref[...] = v_new

    blk = pl.BlockSpec((tr, cols), lambda i: (i, 0))
    shp = jax.ShapeDtypeStruct((rows, cols), F32)
    return pl.pallas_call(body, grid=(rows // tr,), in_specs=[blk] * 4, out_specs=[blk] * 3,
                          out_shape=[shp] * 3, name=name, compiler_params=_params())(w, g, m, v)


def _adamw_halves(w, g2, m, v, name):
    rows, cols = w.shape
    half = cols // 2
    tr = _row_tile(rows)

    def body(w_ref, g_ref, m_ref, v_ref, go_ref, d_ref, mo_ref, vo_ref):
        gv = g_ref[...]
        go_ref[...] = gv
        m_new = ADAM_B1 * m_ref[...] + (1.0 - ADAM_B1) * gv
        v_new = ADAM_B2 * v_ref[...] + (1.0 - ADAM_B2) * jnp.square(gv)
        m_hat = m_new / (1.0 - ADAM_B1 ** ADAM_STEP)
        v_hat = v_new / (1.0 - ADAM_B2 ** ADAM_STEP)
        d_ref[...] = -ADAM_LR * (m_hat / (jnp.sqrt(v_hat) + ADAM_EPS) + ADAM_WD * w_ref[...])
        mo_ref[...] = m_new
        vo_ref[...] = v_new

    blk = pl.BlockSpec((tr, half), lambda hf, i: (i, hf))
    gsp = pl.BlockSpec((None, tr, half), lambda hf, i: (hf, i, 0))
    shp = jax.ShapeDtypeStruct((rows, cols), F32)
    return pl.pallas_call(body, grid=(2, rows // tr), in_specs=[blk, gsp, blk, blk], out_specs=[blk] * 4,
                          out_shape=[shp] * 4, name=name, compiler_params=_params())(w, g2, m, v)


SHARD_W = IN_WIDTH // N_CHIPS


def _half_major(a):
    r, c = a.shape
    return a.reshape(N_CHIPS, 2, r // N_CHIPS // 2, c).transpose(1, 0, 2, 3)


def kernel(x, mem, g_pre, w_in, w_conv, attn_sink, g_mem, w_mem_kv, w_up_a, w_up_b, w_up_m, w_out, g_post, loss_target, m_g_pre, m_w_in, m_w_conv, m_attn_sink, m_g_mem, m_w_mem_kv, m_w_up_a, m_w_up_b, m_w_up_m, m_w_out, m_g_post, v_g_pre, v_w_in, v_w_conv, v_attn_sink, v_g_mem, v_w_mem_kv, v_w_up_a, v_w_up_b, v_w_up_m, v_w_out, v_g_post):
    xi, yi, ci = _position()
    chip = 2 * xi + yi
    where = jnp.stack([chip, ci, N_CHIPS - 1 - chip]).astype(jnp.int32)

    own = [w_in[0].T.astype(BF16), w_mem_kv[0].astype(BF16),
           jnp.concatenate([w_up_a[0], w_up_b[0], w_up_m[0]], axis=0).astype(BF16), w_out[0].astype(BF16)]
    own_conv = jnp.pad(w_conv[0], ((0, 5), (0, 0)))

    def pieces(mine, got):
        got = lax.dynamic_update_slice_in_dim(got, mine[None], chip, axis=0)
        return [got[j] for j in range(N_CHIPS)]

    diag = N_CHIPS - 1 - chip
    diag_blocks = SHARD_BLOCKS + 1
    (h, h_t), (got_near, got_conv) = _rmsnorm_fwd(x[0], g_pre, name="pre_norm", transposed=True,
                                                  carry=_gather_weights(own[:1], own_conv, relations=(0, 1)))
    w_near = lax.dynamic_update_slice_in_dim(got_near, own[0][None], chip, axis=0).reshape(IN_WIDTH, D_MODEL)
    proj, (got_far, *got_rest) = _proj(
        h, w_near, n_blocks=N_IN_BLOCKS - diag_blocks, where=where, name="proj_near",
        block_of=lambda i, w: i + diag_blocks * (i >= SHARD_BLOCKS * w[2]).astype(jnp.int32),
        carry=_join(_gather_weights(own[:1], relations=(2,)), _gather_weights(own[1:], relations=(0, 1))))
    far = lax.dynamic_index_in_dim(got_far, diag, 0, keepdims=False)
    proj = _proj_far(h, w_near, far, where, into=proj)
    w_conv_full = jnp.concatenate([p[:3] for p in pieces(own_conv, got_conv)], axis=1)

    def late_weights(gathered):
        w_kv_full = jnp.concatenate(pieces(own[1], gathered[0]), axis=0)
        up_pieces = pieces(own[2], gathered[1])
        w_up_full = jnp.stack([jnp.concatenate([p[k * A_WIDTH:(k + 1) * A_WIDTH] for p in up_pieces], axis=1)
                               for k in range(3)])
        return w_kv_full, w_up_full, jnp.concatenate(pieces(own[3], gathered[2]), axis=0)

    def pick(parts, hf):
        return [lax.dynamic_index_in_dim(p, hf, 0, keepdims=False) for p in parts]

    def up_out_parts(dw_up, dw_out):
        up = (dw_up.reshape(3, A_WIDTH, N_CHIPS, D_MODEL // N_CHIPS).transpose(2, 0, 1, 3)
              .reshape(N_CHIPS, 2, 3 * A_WIDTH // 2, D_MODEL // N_CHIPS).transpose(1, 0, 2, 3))
        return [up.astype(BF16), _half_major(dw_out).astype(BF16)]

    g = _forward_backward(x[0], mem[0], loss_target[0], proj, w_conv_full, attn_sink, g_mem,
                          (_gather_weights(own[1:], relations=(2,), into=got_rest), late_weights), g_post,
                          lambda dw_up, dw_out: _pair_exchange(pick(up_out_parts(dw_up, dw_out), 1 - ci)))

    half_rows = D_MODEL // 2

    def dw_in_half(half_of, name, carry):
        dw, carried = _dw_in_t(g["dproj"], h_t, half_of=half_of, where=where, name=name, carry=carry)
        return dw.reshape(N_CHIPS, SHARD_W, half_rows), carried

    sums_up_out = [_pair_add(k, r, "pair_add_" + nm)
                   for k, r, nm in zip(pick(up_out_parts(g["w_up"], g["w_out"]), ci), g["early"], ["w_up", "w_out"])]
    kv_parts = [_half_major(g["w_kv"]).astype(BF16)]
    dw_send, recv3_up_out = dw_in_half(lambda w: 1 - w[1], "dw_in_send", _chip_exchange(sums_up_out))
    dw_keep, (recv_in, recv_kv) = dw_in_half(lambda w: w[1], "dw_in_keep",
                                             _pair_exchange([dw_send] + pick(kv_parts, 1 - ci)))
    sum_in = _pair_add(dw_keep, recv_in, "pair_add_w_in")
    sum_kv = _pair_add(pick(kv_parts, ci)[0], recv_kv, "pair_add_w_kv")
    d_h, (recv3_in, recv3_kv) = _d_h(g["dproj"], w_near, far, where, carry=_chip_exchange([sum_in, sum_kv]))
    pairs = [_chip_add(s, r, where, "chip_add_" + nm)
             for s, r, nm in zip([sum_in, sum_kv] + sums_up_out, [recv3_in, recv3_kv] + recv3_up_out,
                                 ["w_in", "w_kv", "w_up", "w_out"])]
    grad_x, dg_pre = _rmsnorm_bwd(d_h, x[0], g_pre, g["dy"], name="pre_norm_bwd")

    zeros512 = jnp.zeros((1, D_MODEL - A_WIDTH), F32)
    conv_rows = [jnp.concatenate([g["w_conv"][k:k + 1], zeros512], axis=1) for k in range(3)]
    sink_row = jnp.pad(g["sink"][:, 0].reshape(1, N_Q_HEADS), ((0, 0), (0, D_MODEL - N_Q_HEADS)))
    loss_row = jnp.pad(g["loss"], ((0, 0), (0, D_MODEL - LANES)))
    pack = jnp.concatenate([dg_pre, g["g_mem"], g["g_post"]] + conv_rows + [sink_row, loss_row], axis=0)
    red, full = _small_allreduce(pack, _pair_share(pairs))
    loss = red[7, 0]
    small_grads = dict(
        g_pre=red[0:1], g_mem=red[1:2], g_post=red[2:3], attn_sink=red[6:7, :N_Q_HEADS],
        w_conv=lax.dynamic_slice(red[3:6, :A_WIDTH], (0, chip * LANES), (3, LANES)))

    gw_up = full[2].reshape(3, A_WIDTH, D_MODEL // N_CHIPS)
    grads = dict(small_grads, w_mem_kv=full[1].reshape(D_MODEL // N_CHIPS, 2 * MEM_WIDTH),
                 w_up_a=gw_up[0], w_up_b=gw_up[1], w_up_m=gw_up[2],
                 w_out=full[3].reshape(D_MODEL // N_CHIPS, D_MODEL))

    weights = dict(g_pre=g_pre, w_in=w_in, w_conv=w_conv, attn_sink=attn_sink, g_mem=g_mem, w_mem_kv=w_mem_kv,
                   w_up_a=w_up_a, w_up_b=w_up_b, w_up_m=w_up_m, w_out=w_out, g_post=g_post)
    m_in = dict(g_pre=m_g_pre, w_in=m_w_in, w_conv=m_w_conv, attn_sink=m_attn_sink, g_mem=m_g_mem,
                w_mem_kv=m_w_mem_kv, w_up_a=m_w_up_a, w_up_b=m_w_up_b, w_up_m=m_w_up_m, w_out=m_w_out,
                g_post=m_g_post)
    v_in = dict(g_pre=v_g_pre, w_in=v_w_in, w_conv=v_w_conv, attn_sink=v_attn_sink, g_mem=v_g_mem,
                w_mem_kv=v_w_mem_kv, w_up_a=v_w_up_a, w_up_b=v_w_up_b, w_up_m=v_w_up_m, w_out=v_w_out,
                g_post=v_g_post)
    out_g, out_d, out_m, out_v = [], [], [], []
    for nm in ("g_pre", "w_in", "w_conv", "attn_sink", "g_mem", "w_mem_kv", "w_up_a", "w_up_b", "w_up_m", "w_out",
               "g_post"):
        shape = weights[nm].shape
        if nm == "w_in":
            results = _adamw_halves(w_in[0].T, full[0], m_w_in[0].T, v_w_in[0].T, "adamw_w_in")
            for out, t in zip((out_g, out_d, out_m, out_v), results):
                out.append(t.T.reshape(shape))
            continue
        two_d = shape[-2:]
        gr = grads[nm].reshape(two_d)
        d, m_new, v_new = _adamw(weights[nm].reshape(two_d), gr, m_in[nm].reshape(two_d), v_in[nm].reshape(two_d),
                                 "adamw_" + nm)
        out_g.append(gr.reshape(shape))
        out_d.append(d.reshape(shape))
        out_m.append(m_new.reshape(shape))
        out_v.append(v_new.reshape(shape))
    return (loss, grad_x.reshape(x.shape), *out_g, *out_d, *out_m, *out_v)
```

```python
import jax
import jax.numpy as jnp
from jax import lax
from jax.experimental import pallas as pl
from jax.experimental.pallas import tpu as pltpu

F32 = jnp.float32
BF16 = jnp.bfloat16
MESH = pl.DeviceIdType.MESH

D_MODEL = 1024
EPS = 1e-6
A_WIDTH = 512
HEAD_DIM = 64
N_Q_HEADS = 8
WINDOW_BLOCK = 128
KV_PAD = 512
ROPE_THETA = 500000.0
ROT_DIM = 16
MEM_HEADS = 4
MEM_HEAD_DIM = 128
MEM_WIDTH = 512
IN_WIDTH = 7424
N_CHIPS = 4
LANES = 128
HALF_LANES = 64

PERM_SEGS = ((0, 2560), (2816, 3328), (4352, 7424), (3328, 4352), (2560, 2816))
COL_A, W_A = 0, 2048
COL_B, W_B = 2, 1024
COL_G, W_G = 1, 3072
COL_M, W_M = 6, 1024
COL_KV, W_KV = 28, 256

ADAM_LR = 0.001
ADAM_B1 = 0.9
ADAM_B2 = 0.999
ADAM_EPS = 1e-08
ADAM_WD = 0.01
ADAM_STEP = 10

VMEM_LIGHT_BYTES = 48 * 1024 * 1024
VMEM_HEAVY_BYTES = 48 * 1024 * 1024


_HBM = pl.BlockSpec(memory_space=pltpu.HBM)


def _params(heavy=False):
    return pltpu.CompilerParams(vmem_limit_bytes=VMEM_HEAVY_BYTES if heavy else VMEM_LIGHT_BYTES)


def _sigmoid(v):
    return jax.nn.sigmoid(v)


_DIMS = {"nn": (((1,), (0,)), ((), ())), "nt": (((1,), (1,)), ((), ())), "tn": (((0,), (0,)), ((), ()))}


class _Carry:
    def __init__(self, ins, out_shapes, sems, start, finish, aliases=None):
        self.ins, self.out_shapes, self.sems = list(ins), list(out_shapes), list(sems)
        self.start, self.finish, self.aliases = start, finish, dict(aliases or {})


def _join(*carries):
    def split(seq, counts):
        pos, parts = 0, []
        for n in counts:
            parts.append(seq[pos:pos + n])
            pos += n
        return parts

    n_in = [len(c.ins) for c in carries]
    n_out = [len(c.out_shapes) for c in carries]
    n_sem = [len(c.sems) for c in carries]

    def run(which):
        def go(ins, outs, sems):
            for c, i, o, sm in zip(carries, split(ins, n_in), split(outs, n_out), split(sems, n_sem)):
                getattr(c, which)(i, o, sm)
        return go

    aliases = {}
    for k, c in enumerate(carries):
        aliases.update({sum(n_in[:k]) + i: sum(n_out[:k]) + o for i, o in c.aliases.items()})
    return _Carry([a for c in carries for a in c.ins], [sh for c in carries for sh in c.out_shapes],
                  [sm for c in carries for sm in c.sems], run("start"), run("finish"), aliases)


def _carried_call(body, carry, *, grid, in_specs, out_specs, out_shape, scratch, operands, name, prefetch=None,
                  aliases=None, heavy=False):
    n_in, n_out, n_scr = len(in_specs), len(out_specs), len(scratch)
    c_in = len(carry.ins) if carry else 0
    c_out = len(carry.out_shapes) if carry else 0
    n_pre = 0 if prefetch is None else 1
    steps = 1
    for g in grid:
        steps *= g

    def wrapped(*refs):
        refs = refs[n_pre:]
        ins, cins = refs[:n_in], refs[n_in:n_in + c_in]
        outs = refs[n_in + c_in:n_in + c_in + n_out]
        couts = refs[n_in + c_in + n_out:n_in + c_in + n_out + c_out]
        rest = refs[n_in + c_in + n_out + c_out:]
        scr, sems = rest[:n_scr], rest[n_scr:]
        if carry:
            step = pl.program_id(0)
            for ax in range(1, len(grid)):
                step = step * grid[ax] + pl.program_id(ax)

            @pl.when(step == 0)
            def _():
                carry.start(cins, couts, sems)

        body(ins, outs, scr)
        if carry:
            @pl.when(step == steps - 1)
            def _():
                carry.finish(cins, couts, sems)

    all_aliases = {n_pre + i: o for i, o in (aliases or {}).items()}
    if carry:
        all_aliases.update({n_pre + n_in + i: n_out + o for i, o in carry.aliases.items()})
    all_in = list(in_specs) + [_HBM] * c_in
    all_out = list(out_specs) + [_HBM] * c_out
    all_scratch = list(scratch) + (carry.sems if carry else [])
    if n_pre:
        spec = dict(grid_spec=pltpu.PrefetchScalarGridSpec(num_scalar_prefetch=1, grid=grid, in_specs=all_in,
                                                           out_specs=all_out, scratch_shapes=all_scratch))
        pre = (prefetch,)
    else:
        spec = dict(grid=grid, in_specs=all_in, out_specs=all_out, scratch_shapes=all_scratch)
        pre = ()
    results = pl.pallas_call(
        wrapped, out_shape=list(out_shape) + (carry.out_shapes if carry else []), input_output_aliases=all_aliases,
        name=name, compiler_params=_params(heavy), **spec)(*pre, *operands, *(carry.ins if carry else []))
    return list(results[:n_out]), list(results[n_out:])


def _matmul(a, b, *, mode, out_dtype, tm, tn, tk, name):
    if mode == "nn":
        (m, k), (_, n) = a.shape, b.shape
    elif mode == "nt":
        (m, k), (n, _) = a.shape, b.shape
    else:
        (k, m), (_, n) = a.shape, b.shape
    tm, tn, tk = min(tm, m), min(tn, n), min(tk, k)
    assert m % tm == 0 and n % tn == 0 and k % tk == 0
    nk = k // tk
    dims = _DIMS[mode]

    if mode == "nn":
        a_spec = pl.BlockSpec((tm, tk), lambda i, j, kk: (i, kk))
        b_spec = pl.BlockSpec((tk, tn), lambda i, j, kk: (kk, j))
    elif mode == "nt":
        a_spec = pl.BlockSpec((tm, tk), lambda i, j, kk: (i, kk))
        b_spec = pl.BlockSpec((tn, tk), lambda i, j, kk: (j, kk))
    else:
        a_spec = pl.BlockSpec((tk, tm), lambda i, j, kk: (kk, i))
        b_spec = pl.BlockSpec((tk, tn), lambda i, j, kk: (kk, j))
    o_spec = pl.BlockSpec((tm, tn), lambda i, j, kk: (i, j))

    def part(a_ref, b_ref):
        return lax.dot_general(a_ref[...].astype(BF16), b_ref[...].astype(BF16), dims,
                               preferred_element_type=F32)

    if nk == 1:
        def body(a_ref, b_ref, o_ref):
            o_ref[...] = part(a_ref, b_ref).astype(out_dtype)
        scratch = []
    else:
        def body(a_ref, b_ref, o_ref, acc_ref):
            kk = pl.program_id(2)

            @pl.when(kk == 0)
            def _():
                acc_ref[...] = part(a_ref, b_ref)

            @pl.when(kk > 0)
            def _():
                acc_ref[...] += part(a_ref, b_ref)

            @pl.when(kk == nk - 1)
            def _():
                o_ref[...] = acc_ref[...].astype(out_dtype)
        scratch = [pltpu.VMEM((tm, tn), F32)]

    return pl.pallas_call(
        body, grid=(m // tm, n // tn, nk), in_specs=[a_spec, b_spec], out_specs=o_spec,
        out_shape=jax.ShapeDtypeStruct((m, n), out_dtype), scratch_shapes=scratch,
        name=name, compiler_params=_params())(a, b)


IN_BLOCK = 256
N_IN_BLOCKS = IN_WIDTH // IN_BLOCK
SHARD_BLOCKS = (IN_WIDTH // N_CHIPS) // IN_BLOCK
BLOCK_RUNS = tuple((a // IN_BLOCK, sum(d - c for c, d in PERM_SEGS[:k]) // IN_BLOCK, (b - a) // IN_BLOCK)
                   for k, (a, b) in enumerate(PERM_SEGS))


def _perm_block(r):
    p = r
    for ref0, perm0, n in BLOCK_RUNS:
        p = jnp.where((r >= ref0) & (r < ref0 + n), r - ref0 + perm0, p)
    return p


def _proj_near(h, own_w, small, where):
    s, d = h.shape
    n_own = SHARD_BLOCKS - 1
    n_diag = SHARD_BLOCKS + 1
    n_blocks = N_IN_BLOCKS - n_diag
    piece = IN_WIDTH // N_CHIPS - SHARD_BLOCKS * IN_BLOCK
    near = _gather_weights([own_w], small, relations=(0, 1))
    far = _gather_weights([own_w], relations=(2,))
    both = _join(near, far)
    n_cin, n_cout = len(both.ins), len(both.out_shapes)

    def block_of(i, w):
        me, dg = w[0], w[2]
        own0 = SHARD_BLOCKS * me + jnp.minimum(me, 1)
        dg0 = SHARD_BLOCKS * dg
        lo0, hi0 = jnp.minimum(own0, dg0), jnp.maximum(own0, dg0)
        lo_n = jnp.where(own0 < dg0, n_own, n_diag)
        hi_n = jnp.where(own0 < dg0, n_diag, n_own)
        r = i - n_own
        r = r + lo_n * (r >= lo0).astype(jnp.int32)
        r = r + hi_n * (r >= hi0).astype(jnp.int32)
        return jnp.where(i < n_own, own0 + i, r)

    def body(w_ref, h_ref, own_hbm, *refs):
        cins, o_ref, couts = refs[:n_cin], refs[n_cin], refs[n_cin + 1:n_cin + 1 + n_cout]
        blocks, block_sems = refs[n_cin + 1 + n_cout:n_cin + 3 + n_cout]
        sems = refs[n_cin + 3 + n_cout:]
        near_refs = (cins[:len(near.ins)], couts[:len(near.out_shapes)], sems[:len(near.sems)])
        far_refs = (cins[len(near.ins):], couts[len(near.out_shapes):], sems[len(near.sems):])
        gathered = couts[0]
        i = pl.program_id(0)
        me = w_ref[0]

        def fetch(step, slot):
            r = block_of(step, w_ref)
            for p in range(IN_BLOCK // piece):
                row = r * IN_BLOCK + p * piece
                j = row // (IN_WIDTH // N_CHIPS)
                off = pl.multiple_of(row - j * (IN_WIDTH // N_CHIPS), BF16_SUBLANES)
                dst = blocks.at[slot, pl.ds(p * piece, piece)]

                @pl.when(j == me)
                def _():
                    pltpu.make_async_copy(own_hbm.at[pl.ds(off, piece)], dst, block_sems.at[slot]).start()

                @pl.when(j != me)
                def _():
                    pltpu.make_async_copy(gathered.at[j, pl.ds(off, piece)], dst, block_sems.at[slot]).start()

        def arrived(slot):
            pltpu.make_async_copy(own_hbm.at[pl.ds(0, IN_BLOCK)], blocks.at[slot], block_sems.at[slot]).wait()

        slot = i % 2

        @pl.when(i == 0)
        def _():
            near.start(*near_refs)
            fetch(i, slot)

        @pl.when(i == n_own)
        def _():
            near.finish(*near_refs)
            far.start(*far_refs)
            fetch(i, slot)

        arrived(slot)

        @pl.when((i + 1 < n_blocks) & (i + 1 != n_own))
        def _():
            fetch(i + 1, 1 - slot)

        o_ref[...] = lax.dot_general(h_ref[...], blocks[slot], _DIMS["nt"], preferred_element_type=F32)

        @pl.when(i == n_blocks - 1)
        def _():
            far.finish(*far_refs)

    grid_spec = pltpu.PrefetchScalarGridSpec(
        num_scalar_prefetch=1, grid=(n_blocks,),
        in_specs=[pl.BlockSpec((s, d), lambda i, w: (0, 0)), pl.BlockSpec(memory_space=pl.ANY)] + [_HBM] * n_cin,
        out_specs=[pl.BlockSpec((s, IN_BLOCK), lambda i, w: (0, _perm_block(block_of(i, w))))] + [_HBM] * n_cout,
        scratch_shapes=[pltpu.VMEM((2, IN_BLOCK, d), BF16), pltpu.SemaphoreType.DMA((2,))] + both.sems)
    return pl.pallas_call(
        body, grid_spec=grid_spec, out_shape=[jax.ShapeDtypeStruct((s, IN_WIDTH), F32)] + both.out_shapes,
        name="proj_near", compiler_params=_params())(where, h, own_w, *both.ins)


def _proj_far(h, w_near, far, where, *, into, carry=None):
    s, d = h.shape
    n_blocks = SHARD_BLOCKS + 1
    lead = IN_WIDTH // N_CHIPS - SHARD_BLOCKS * IN_BLOCK

    def body(ins, outs, scr):
        where_ref, h_ref, w_hbm, far_hbm, _ = ins
        win, sem = scr
        i = pl.program_id(0)

        @pl.when(i == 0)
        def _():
            dg = where_ref[2]
            rows = pl.ds(pl.multiple_of(dg * (SHARD_BLOCKS * IN_BLOCK), IN_BLOCK), n_blocks * IN_BLOCK)
            window = pltpu.make_async_copy(w_hbm.at[rows], win, sem)
            window.start()
            window.wait()
            shard = pltpu.make_async_copy(far_hbm, win.at[pl.ds(pl.multiple_of(dg * lead, BF16_SUBLANES), SHARD_W)], sem)
            shard.start()
            shard.wait()

        blk = win[pl.ds(pl.multiple_of(i * IN_BLOCK, IN_BLOCK), IN_BLOCK), :]
        outs[0][...] = lax.dot_general(h_ref[...], blk, _DIMS["nt"], preferred_element_type=F32)

    anysp = pl.BlockSpec(memory_space=pl.ANY)
    (proj,), carried = _carried_call(
        body, carry, grid=(n_blocks,),
        in_specs=[pl.BlockSpec(memory_space=pltpu.SMEM), pl.BlockSpec((s, d), lambda i, w: (0, 0)), anysp, anysp, anysp],
        out_specs=[pl.BlockSpec((s, IN_BLOCK), lambda i, w: (0, _perm_block(i + SHARD_BLOCKS * w[2])))],
        out_shape=[jax.ShapeDtypeStruct((s, IN_WIDTH), F32)],
        scratch=[pltpu.VMEM((n_blocks * IN_BLOCK, d), BF16), pltpu.SemaphoreType.DMA],
        operands=(where, h, w_near, far, into), name="proj_far", prefetch=where, aliases={4: 0})
    return (proj, carried) if carry else proj


def _dw_in_t(dproj, h_t, *, half_of, where, name, carry=None):
    d, s = h_t.shape
    c = d // 2

    def body(ins, outs, scr):
        outs[0][...] = lax.dot_general(ins[1][...], ins[0][...], _DIMS["nn"], preferred_element_type=F32).T.astype(BF16)

    (dw,), carried = _carried_call(
        body, carry, grid=(N_IN_BLOCKS,),
        in_specs=[pl.BlockSpec((s, IN_BLOCK), lambda r, w: (0, _perm_block(r))),
                  pl.BlockSpec((c, s), lambda r, w: (half_of(w), 0))],
        out_specs=[pl.BlockSpec((IN_BLOCK, c), lambda r, w: (r, 0))],
        out_shape=[jax.ShapeDtypeStruct((IN_WIDTH, c), BF16)], scratch=[], operands=(dproj, h_t), name=name,
        prefetch=where)
    return (dw, carried) if carry else dw


def _d_h(dproj, w_near, far, where, *, carry=None):
    s = dproj.shape[0]
    d = w_near.shape[1]
    tm = min(s, 256)

    def body(ins, outs, scr):
        where_ref, a_ref, w_hbm, far_hbm = ins
        w_ref, sem = scr

        @pl.when(pl.program_id(0) == 0)
        def _():
            whole = pltpu.make_async_copy(w_hbm, w_ref, sem)
            whole.start()
            whole.wait()
            rows = pl.ds(pl.multiple_of(where_ref[2] * SHARD_W, BF16_SUBLANES), SHARD_W)
            part = pltpu.make_async_copy(far_hbm, w_ref.at[rows], sem)
            part.start()
            part.wait()

        acc = None
        for ref0, perm0, n in BLOCK_RUNS:
            term = jnp.dot(a_ref[:, perm0 * IN_BLOCK:(perm0 + n) * IN_BLOCK],
                           w_ref[ref0 * IN_BLOCK:(ref0 + n) * IN_BLOCK, :], preferred_element_type=F32)
            acc = term if acc is None else acc + term
        outs[0][...] = acc

    anysp = pl.BlockSpec(memory_space=pl.ANY)
    (dh,), carried = _carried_call(
        body, carry, grid=(s // tm,),
        in_specs=[pl.BlockSpec(memory_space=pltpu.SMEM), pl.BlockSpec((tm, IN_WIDTH), lambda i: (i, 0)), anysp, anysp],
        out_specs=[pl.BlockSpec((tm, d), lambda i: (i, 0))],
        out_shape=[jax.ShapeDtypeStruct((s, d), F32)],
        scratch=[pltpu.VMEM((IN_WIDTH, d), BF16), pltpu.SemaphoreType.DMA],
        operands=(where, dproj, w_near, far), name="d_h", heavy=True)
    return (dh, carried) if carry else dh


def _rmsnorm_fwd(x, g, *, name, transposed=False, carry=None):
    s, d = x.shape
    ts = min(512, s)

    def body(ins, outs, scr):
        xv = ins[0][...]
        r = lax.rsqrt(jnp.mean(xv * xv, axis=-1, keepdims=True) + EPS)
        hv = (xv * r) * ins[1][...]
        outs[0][...] = hv.astype(BF16)
        if transposed:
            outs[1][...] = hv.T.astype(BF16)

    out_specs = [pl.BlockSpec((ts, d), lambda i: (i, 0))]
    out_shape = [jax.ShapeDtypeStruct((s, d), BF16)]
    if transposed:
        out_specs.append(pl.BlockSpec((d, ts), lambda i: (0, i)))
        out_shape.append(jax.ShapeDtypeStruct((d, s), BF16))
    outs, carried = _carried_call(
        body, carry, grid=(s // ts,),
        in_specs=[pl.BlockSpec((ts, d), lambda i: (i, 0)), pl.BlockSpec((1, d), lambda i: (0, 0))],
        out_specs=out_specs, out_shape=out_shape, scratch=[], operands=(x, g), name=name)
    result = tuple(outs) if transposed else outs[0]
    return (result, carried) if carry else result


def _rmsnorm_bwd(dh, x, g, res, *, name, carry=None):
    s, d = x.shape
    ts = min(256, s)

    def body(ins, outs, scr):
        dh_ref, x_ref, g_ref, res_ref = ins
        dx_ref, dg_ref = outs
        xv = x_ref[...]
        r = lax.rsqrt(jnp.mean(xv * xv, axis=-1, keepdims=True) + EPS)
        xh = xv * r
        dhv = dh_ref[...]
        part = jnp.sum(dhv * xh, axis=0, keepdims=True)

        @pl.when(pl.program_id(0) == 0)
        def _():
            dg_ref[...] = part

        @pl.when(pl.program_id(0) > 0)
        def _():
            dg_ref[...] += part

        dxh = dhv * g_ref[...]
        dx_ref[...] = res_ref[...] + r * (dxh - xh * jnp.mean(dxh * xh, axis=-1, keepdims=True))

    row = pl.BlockSpec((ts, d), lambda i: (i, 0))
    vec = pl.BlockSpec((1, d), lambda i: (0, 0))
    outs, carried = _carried_call(
        body, carry, grid=(s // ts,), in_specs=[row, row, vec, row], out_specs=[row, vec],
        out_shape=[jax.ShapeDtypeStruct((s, d), F32), jax.ShapeDtypeStruct((1, d), F32)],
        scratch=[], operands=(dh, x, g, res), name=name)
    return (*outs, carried) if carry else tuple(outs)


MID_TILE = 256


def _gated_branches(y_refs, wup_ref, gl):
    d = D_MODEL
    us = [jnp.dot(y_refs[k][...], wup_ref[k], preferred_element_type=F32) for k in range(3)]
    sg = [_sigmoid(gl[:, k * d:(k + 1) * d]) for k in range(3)]
    return us, sg


def _mid_fwd(ya, yb, ym, proj, x, tgt, w_up, w_out, g_post):
    s, d = x.shape
    ts = MID_TILE

    def body(ya_ref, yb_ref, ym_ref, g_ref, x_ref, t_ref, wup_ref, wout_ref, gp_ref,
             m_ref, do_ref, dy_ref, dg_ref, loss_ref):
        us, sg = _gated_branches((ya_ref, yb_ref, ym_ref), wup_ref, g_ref[...])
        merged = (sg[0] * us[0] + sg[1] * us[1] + sg[2] * us[2]).astype(BF16)
        m_ref[...] = merged
        ov = jnp.dot(merged, wout_ref[...], preferred_element_type=F32)
        r = lax.rsqrt(jnp.mean(ov * ov, axis=-1, keepdims=True) + EPS)
        nh = ov * r
        gv = gp_ref[...]
        e = (x_ref[...] + nh * gv) - t_ref[...]
        lpart = 0.5 * jnp.sum(jnp.mean(e * e, axis=-1, keepdims=True), axis=0, keepdims=True)
        dy = e * (1.0 / d)
        dgp = jnp.sum(dy * nh, axis=0, keepdims=True)

        @pl.when(pl.program_id(0) == 0)
        def _():
            dg_ref[...] = dgp
            loss_ref[...] = jnp.broadcast_to(lpart, loss_ref.shape)

        @pl.when(pl.program_id(0) > 0)
        def _():
            dg_ref[...] += dgp
            loss_ref[...] += jnp.broadcast_to(lpart, loss_ref.shape)

        dn = dy * gv
        dy_ref[...] = dy
        do_ref[...] = (r * (dn - nh * jnp.mean(dn * nh, axis=-1, keepdims=True))).astype(BF16)

    row = pl.BlockSpec((ts, d), lambda i: (i, 0))
    ysp = pl.BlockSpec((ts, A_WIDTH), lambda i: (i, 0))
    vec = pl.BlockSpec((1, d), lambda i: (0, 0))
    return pl.pallas_call(
        body, grid=(s // ts,),
        in_specs=[ysp, ysp, ysp, pl.BlockSpec((ts, W_G), lambda i: (i, COL_G)), row, row,
                  pl.BlockSpec((3, A_WIDTH, d), lambda i: (0, 0, 0)), pl.BlockSpec((d, d), lambda i: (0, 0)), vec],
        out_specs=[row, row, row, vec, pl.BlockSpec((1, LANES), lambda i: (0, 0))],
        out_shape=[jax.ShapeDtypeStruct((s, d), BF16), jax.ShapeDtypeStruct((s, d), BF16),
                   jax.ShapeDtypeStruct((s, d), F32), jax.ShapeDtypeStruct((1, d), F32),
                   jax.ShapeDtypeStruct((1, LANES), F32)],
        name="mid_fwd", compiler_params=_params(heavy=True))(ya, yb, ym, proj, x, tgt, w_up, w_out, g_post)


def _mid_bwd(d_out, merged, ya, yb, ym, proj, w_up, w_out):
    s, d = merged.shape
    ts = MID_TILE
    last = s // ts - 1

    def body(do_ref, m_ref, ya_ref, yb_ref, ym_ref, g_ref, wup_ref, wout_ref,
             dp_ref, dya_ref, dyb_ref, dym_ref, dwup_hbm, dwout_hbm, dwup_acc, dwout_acc):
        i = pl.program_id(0)

        @pl.when(i == 0)
        def _():
            dwup_acc[...] = jnp.zeros_like(dwup_acc)
            dwout_acc[...] = jnp.zeros_like(dwout_acc)

        y_refs = (ya_ref, yb_ref, ym_ref)
        us, sg = _gated_branches(y_refs, wup_ref, g_ref[...])
        dov = do_ref[...]
        dwout_acc[...] += lax.dot_general(m_ref[...], dov, _DIMS["tn"], preferred_element_type=F32)
        dm = lax.dot_general(dov, wout_ref[...], _DIMS["nt"], preferred_element_type=F32)
        for k, dy_ref in enumerate((dya_ref, dyb_ref, dym_ref)):
            dp_ref[:, k * d:(k + 1) * d] = ((dm * us[k]) * (sg[k] * (1.0 - sg[k]))).astype(BF16)
            du = (sg[k] * dm).astype(BF16)
            dy_ref[...] = lax.dot_general(du, wup_ref[k], _DIMS["nt"], preferred_element_type=F32)
            dwup_acc[k] += lax.dot_general(y_refs[k][...], du, _DIMS["tn"], preferred_element_type=F32)

        @pl.when(i == last)
        def _():
            pltpu.sync_copy(dwup_acc, dwup_hbm)
            pltpu.sync_copy(dwout_acc, dwout_hbm)

    row = pl.BlockSpec((ts, d), lambda i: (i, 0))
    ysp = pl.BlockSpec((ts, A_WIDTH), lambda i: (i, 0))
    gsp = pl.BlockSpec((ts, W_G), lambda i: (i, COL_G))
    anysp = pl.BlockSpec(memory_space=pl.ANY)
    yshape = jax.ShapeDtypeStruct((s, A_WIDTH), F32)
    return pl.pallas_call(
        body, grid=(s // ts,),
        in_specs=[row, row, ysp, ysp, ysp, gsp, pl.BlockSpec((3, A_WIDTH, d), lambda i: (0, 0, 0)),
                  pl.BlockSpec((d, d), lambda i: (0, 0))],
        out_specs=[gsp, ysp, ysp, ysp, anysp, anysp],
        out_shape=[jax.ShapeDtypeStruct((s, IN_WIDTH), BF16), yshape, yshape, yshape,
                   jax.ShapeDtypeStruct((3, A_WIDTH, d), F32), jax.ShapeDtypeStruct((d, d), F32)],
        scratch_shapes=[pltpu.VMEM((3, A_WIDTH, d), F32), pltpu.VMEM((d, d), F32)],
        name="mid_bwd", compiler_params=_params(heavy=True))(d_out, merged, ya, yb, ym, proj, w_up, w_out)


def _conv_core(blk, prev, nxt, w, i, last, ts):
    c = A_WIDTH
    ab, ac, ax, az = blk[:, :c], blk[:, c:2 * c], blk[:, 2 * c:3 * c], blk[:, 3 * c:]
    cu = ac * ax
    cu_prev = (prev[7:8, c:2 * c] * prev[7:8, 2 * c:3 * c]) * jnp.where(i > 0, 1.0, 0.0)
    cu_next = (nxt[0:1, c:2 * c] * nxt[0:1, 2 * c:3 * c]) * jnp.where(i < last, 1.0, 0.0)
    row = lax.broadcasted_iota(jnp.int32, (ts, c), 0)
    cm1 = jnp.where(row == 0, cu_prev, pltpu.roll(cu, 1, 0))
    cp1 = jnp.where(row == ts - 1, cu_next, pltpu.roll(cu, ts - 1, 0))
    yc = cm1 * w[0:1] + cu * w[1:2] + cp1 * w[2:3]
    return ab, ac, ax, az, cu, cm1, cp1, yc, row


def _halo_specs(ts, width, col, nblk8):
    prev = pl.BlockSpec((8, width), lambda i: (jnp.maximum(i * (ts // 8) - 1, 0), col))
    nxt = pl.BlockSpec((8, width), lambda i: (jnp.minimum((i + 1) * (ts // 8), nblk8 - 1), col))
    return prev, nxt


def _conv_fwd(proj, w_conv):
    s = proj.shape[0]
    ts = 256
    last = s // ts - 1

    def body(a_ref, ap_ref, an_ref, w_ref, ya_ref):
        i = pl.program_id(0)
        ab, _, _, az, _, _, _, yc, _ = _conv_core(a_ref[...], ap_ref[...], an_ref[...], w_ref[...], i, last, ts)
        ya_ref[...] = ((ab * yc) * (az * _sigmoid(az))).astype(BF16)

    prev, nxt = _halo_specs(ts, W_A, COL_A, s // 8)
    return pl.pallas_call(
        body, grid=(s // ts,),
        in_specs=[pl.BlockSpec((ts, W_A), lambda i: (i, COL_A)), prev, nxt,
                  pl.BlockSpec((3, A_WIDTH), lambda i: (0, 0))],
        out_specs=pl.BlockSpec((ts, A_WIDTH), lambda i: (i, 0)),
        out_shape=jax.ShapeDtypeStruct((s, A_WIDTH), BF16), name="conv_fwd",
        compiler_params=_params())(proj, proj, proj, w_conv)


def _conv_bwd(proj, w_conv, dya, dproj):
    s = proj.shape[0]
    ts = 256
    last = s // ts - 1
    c = A_WIDTH

    def body(a_ref, ap_ref, an_ref, w_ref, d_ref, dp_ref, dn_ref, _, dproj_ref, dw_ref):
        i = pl.program_id(0)
        w = w_ref[...]
        prev, nxt = ap_ref[...], an_ref[...]
        ab, ac, ax, az, cu, cm1, cp1, yc, row = _conv_core(a_ref[...], prev, nxt, w, i, last, ts)
        sg = _sigmoid(az)
        sz = az * sg
        dya_v = d_ref[...]
        dyc = dya_v * sz * ab
        dproj_ref[:, :c] = (dya_v * sz * yc).astype(BF16)
        dproj_ref[:, 3 * c:] = (dya_v * (ab * yc) * (sg * (1.0 + az * (1.0 - sg)))).astype(BF16)

        def halo_dyc(a_row, d_row):
            azr = a_row[:, 3 * c:]
            return d_row * (azr * _sigmoid(azr)) * a_row[:, :c]

        dyc_prev = halo_dyc(prev[7:8], dp_ref[...][7:8]) * jnp.where(i > 0, 1.0, 0.0)
        dyc_next = halo_dyc(nxt[0:1], dn_ref[...][0:1]) * jnp.where(i < last, 1.0, 0.0)
        dyc_m1 = jnp.where(row == 0, dyc_prev, pltpu.roll(dyc, 1, 0))
        dyc_p1 = jnp.where(row == ts - 1, dyc_next, pltpu.roll(dyc, ts - 1, 0))
        dcu = dyc_p1 * w[0:1] + dyc * w[1:2] + dyc_m1 * w[2:3]
        dproj_ref[:, c:2 * c] = (dcu * ax).astype(BF16)
        dproj_ref[:, 2 * c:3 * c] = (dcu * ac).astype(BF16)
        dw = [jnp.sum(dyc * t, axis=0, keepdims=True) for t in (cm1, cu, cp1)]

        @pl.when(i == 0)
        def _():
            for k in range(3):
                dw_ref[k:k + 1, :] = dw[k]

        @pl.when(i > 0)
        def _():
            for k in range(3):
                dw_ref[k:k + 1, :] += dw[k]

    prev, nxt = _halo_specs(ts, W_A, COL_A, s // 8)
    dprev, dnxt = _halo_specs(ts, A_WIDTH, 0, s // 8)
    return pl.pallas_call(
        body, grid=(s // ts,),
        in_specs=[pl.BlockSpec((ts, W_A), lambda i: (i, COL_A)), prev, nxt,
                  pl.BlockSpec((3, A_WIDTH), lambda i: (0, 0)),
                  pl.BlockSpec((ts, A_WIDTH), lambda i: (i, 0)), dprev, dnxt,
                  pl.BlockSpec(memory_space=pl.ANY)],
        out_specs=[pl.BlockSpec((ts, W_A), lambda i: (i, COL_A)), pl.BlockSpec((3, A_WIDTH), lambda i: (0, 0))],
        out_shape=[jax.ShapeDtypeStruct(dproj.shape, BF16), jax.ShapeDtypeStruct((3, A_WIDTH), F32)],
        input_output_aliases={7: 0}, name="conv_bwd",
        compiler_params=_params())(proj, proj, proj, w_conv, dya, dya, dya, dproj)


def _rope_tables(s):
    half = ROT_DIM // 2
    dim = jnp.arange(LANES) % HEAD_DIM
    inv_freq = jnp.power(jnp.float32(ROPE_THETA), -(dim % half).astype(F32) * (2.0 / ROT_DIM))
    ang = jnp.arange(s).astype(F32)[:, None] * inv_freq[None, :]
    cos, sin = jnp.cos(ang), jnp.sin(ang)
    first, second = (dim < half)[None, :], ((dim >= half) & (dim < ROT_DIM))[None, :]
    c = jnp.where(first | second, cos, 1.0)
    s1 = jnp.where(first, -sin, 0.0)
    s2 = jnp.where(second, sin, 0.0)
    return jnp.concatenate([c, s1, s2], axis=1)


def _rope(t, tab):
    return (t * tab[:, :LANES] + pltpu.roll(t, LANES - 8, 1) * tab[:, LANES:2 * LANES]
            + pltpu.roll(t, 8, 1) * tab[:, 2 * LANES:])


def _rope_transpose(dt, tab):
    return (dt * tab[:, :LANES] + pltpu.roll(dt * tab[:, LANES:2 * LANES], 8, 1)
            + pltpu.roll(dt * tab[:, 2 * LANES:], LANES - 8, 1))


def _rope_kv(proj, tab):
    s = proj.shape[0]
    nb = s // KV_PAD

    def body(kv_ref, t_ref, k_ref, v_ref):
        j = pl.program_id(0)
        inside = jnp.where((j > 0) & (j <= nb), 1.0, 0.0)
        kv = kv_ref[...]
        k_ref[...] = (_rope(kv[:, :LANES], t_ref[...]) * inside).astype(BF16)
        v_ref[...] = (kv[:, LANES:] * inside).astype(BF16)

    def src(j):
        return jnp.clip(j - 1, 0, nb - 1)

    o_spec = pl.BlockSpec((KV_PAD, LANES), lambda j: (j, 0))
    shp = jax.ShapeDtypeStruct((s + 2 * KV_PAD, LANES), BF16)
    return pl.pallas_call(
        body, grid=(nb + 2,),
        in_specs=[pl.BlockSpec((KV_PAD, W_KV), lambda j: (src(j), COL_KV)),
                  pl.BlockSpec((KV_PAD, 3 * LANES), lambda j: (src(j), 0))],
        out_specs=[o_spec, o_spec], out_shape=[shp, shp], name="rope_kv",
        compiler_params=_params())(proj, tab)


def _rope_kv_bwd(dkpad, dvpad, tab, dproj):
    s = tab.shape[0]
    nb = s // KV_PAD

    def body(dk_ref, dv_ref, t_ref, _, dp_ref):
        dp_ref[:, :LANES] = _rope_transpose(dk_ref[...], t_ref[...]).astype(BF16)
        dp_ref[:, LANES:] = dv_ref[...].astype(BF16)

    pad_spec = pl.BlockSpec((KV_PAD, LANES), lambda j: (j + 1, 0))
    return pl.pallas_call(
        body, grid=(nb,),
        in_specs=[pad_spec, pad_spec, pl.BlockSpec((KV_PAD, 3 * LANES), lambda j: (j, 0)),
                  pl.BlockSpec(memory_space=pl.ANY)],
        out_specs=pl.BlockSpec((KV_PAD, W_KV), lambda j: (j, COL_KV)),
        out_shape=jax.ShapeDtypeStruct(dproj.shape, BF16), input_output_aliases={3: 0},
        name="rope_kv_bwd", compiler_params=_params())(dkpad, dvpad, tab, dproj)


def _window_start(n):
    return pl.multiple_of((n - 1) * WINDOW_BLOCK + KV_PAD, WINDOW_BLOCK)


def _window_operands(k_ref, v_ref, n, lo):
    start = _window_start(n)
    kw = k_ref[pl.ds(start, 3 * WINDOW_BLOCK), :].astype(F32)
    vw = v_ref[pl.ds(start, 3 * WINDOW_BLOCK), :].astype(F32)
    kr, vr = pltpu.roll(kw, HALF_LANES, 1), pltpu.roll(vw, HALF_LANES, 1)
    k2 = (jnp.where(lo, kw, kr).astype(BF16), jnp.where(lo, kr, kw).astype(BF16))
    v2 = (jnp.where(lo, vw, vr).astype(BF16), jnp.where(lo, vr, vw).astype(BF16))
    return k2, v2


HEADS_PER_GROUP = 4
SWA_FWD_BLOCKS = 1
SWA_BWD_BLOCKS = 2


def _window_bias():
    wb = WINDOW_BLOCK
    qi = lax.broadcasted_iota(jnp.int32, (wb, 3 * wb), 0)
    kj = lax.broadcasted_iota(jnp.int32, (wb, 3 * wb), 1)
    band = (kj >= qi) & (kj <= qi + 2 * wb)
    cases = jnp.stack([band & (kj >= wb), band, band & (kj < 2 * wb)])
    return jnp.where(cases, 0.0, -jnp.inf).astype(F32)


def _block_bias(bias_ref, n, n_blocks):
    case = jnp.where(n == 0, 0, jnp.where(n == n_blocks - 1, 2, 1))
    one = bias_ref[case]
    return jnp.concatenate([one] * HEADS_PER_GROUP, axis=0)


def _stack_heads(pair0, pair1, lo):
    return jnp.concatenate([jnp.where(lo, pair0, 0.0), jnp.where(lo, 0.0, pair0),
                            jnp.where(lo, pair1, 0.0), jnp.where(lo, 0.0, pair1)], axis=0)


def _unstack_pair(stacked, i, lo):
    wb = WINDOW_BLOCK
    return jnp.where(lo, stacked[2 * i * wb:(2 * i + 1) * wb], stacked[(2 * i + 1) * wb:(2 * i + 2) * wb])


def _sink_column(sink_ref, g):
    wb = WINDOW_BLOCK
    return jnp.concatenate([jnp.full((wb, 1), sink_ref[0, HEADS_PER_GROUP * g + i], F32)
                            for i in range(HEADS_PER_GROUP)], axis=0)


def _head_exp(q4, k2g, bias, sink):
    sc = lax.dot_general(q4, k2g, _DIMS["nt"], preferred_element_type=F32) * (HEAD_DIM ** -0.5) + bias
    m = jnp.maximum(jnp.max(sc, axis=1, keepdims=True), sink)
    return jnp.exp(sc - m).astype(BF16), jnp.exp(sink - m)


def _swa_fwd(proj, kpad, vpad, tab, bias, sink, *, carry=None):
    s = proj.shape[0]
    wb = WINDOW_BLOCK

    def body(b_ref, k_ref, v_ref, t_ref, bias_ref, sink_ref, o_ref, y_ref):
        lo = lax.broadcasted_iota(jnp.int32, (wb, LANES), 1) < HALF_LANES
        lo_w = lax.broadcasted_iota(jnp.int32, (3 * wb, LANES), 1) < HALF_LANES
        for sub in range(SWA_FWD_BLOCKS):
            n = pl.program_id(0) * SWA_FWD_BLOCKS + sub
            rows = slice(sub * wb, (sub + 1) * wb)
            k2, v2 = _window_operands(k_ref, v_ref, n, lo_w)
            valid = _block_bias(bias_ref, n, s // wb)
            tab_v = t_ref[rows, :]
            ones = jnp.ones((3 * wb, LANES), BF16)
            for g in range(2):
                qr = [_rope(b_ref[rows, (2 * g + i) * LANES:(2 * g + i + 1) * LANES], tab_v) for i in range(2)]
                q4 = _stack_heads(qr[0], qr[1], lo).astype(BF16)
                e, es = _head_exp(q4, k2[g], valid, _sink_column(sink_ref, g))
                ox = jnp.dot(e, jnp.concatenate([v2[g], ones], axis=1), preferred_element_type=F32)
                o4 = ox[:, :LANES] * (1.0 / (ox[:, LANES:] + es))
                for i in range(2):
                    cols = slice((2 * g + i) * LANES, (2 * g + i + 1) * LANES)
                    op = _unstack_pair(o4, i, lo)
                    o_ref[rows, cols] = op
                    zp = b_ref[rows, A_WIDTH + cols.start:A_WIDTH + cols.stop]
                    y_ref[rows, cols] = (op * (zp * _sigmoid(zp))).astype(BF16)

    tq = SWA_FWD_BLOCKS * wb
    pad_spec = pl.BlockSpec((s + 2 * KV_PAD, LANES), lambda n: (0, 0))
    o_spec = pl.BlockSpec((tq, A_WIDTH), lambda n: (n, 0))
    outs, carried = _carried_call(
        lambda ins, outs, scr: body(*ins, *outs), carry, grid=(s // tq,),
        in_specs=[pl.BlockSpec((tq, W_B), lambda n: (n, COL_B)), pad_spec, pad_spec,
                  pl.BlockSpec((tq, 3 * LANES), lambda n: (n, 0)),
                  pl.BlockSpec(bias.shape, lambda n: (0, 0, 0)), pl.BlockSpec(memory_space=pltpu.SMEM)],
        out_specs=[o_spec, o_spec],
        out_shape=[jax.ShapeDtypeStruct((s, A_WIDTH), F32), jax.ShapeDtypeStruct((s, A_WIDTH), BF16)],
        scratch=[], operands=(proj, kpad, vpad, tab, bias, sink), name="swa_fwd")
    return (*outs, carried) if carry else tuple(outs)


def _swa_bwd(proj, kpad, vpad, tab, bias, sink, o_attn, dyb, dproj):
    s = proj.shape[0]
    wb = WINDOW_BLOCK
    scale = HEAD_DIM ** -0.5

    def body(b_ref, k_ref, v_ref, t_ref, bias_ref, sink_ref, o_ref, dy_ref, _, dp_ref, dk_ref, dv_ref, ds_ref):
        @pl.when(pl.program_id(0) == 0)
        def _():
            dk_ref[...] = jnp.zeros_like(dk_ref)
            dv_ref[...] = jnp.zeros_like(dv_ref)
            ds_ref[...] = jnp.zeros_like(ds_ref)

        lo = lax.broadcasted_iota(jnp.int32, (wb, LANES), 1) < HALF_LANES
        lo_w = lax.broadcasted_iota(jnp.int32, (3 * wb, LANES), 1) < HALF_LANES
        for sub in range(SWA_BWD_BLOCKS):
            n = pl.program_id(0) * SWA_BWD_BLOCKS + sub
            rows = slice(sub * wb, (sub + 1) * wb)
            k2, v2 = _window_operands(k_ref, v_ref, n, lo_w)
            valid = _block_bias(bias_ref, n, s // wb)
            tab_v = t_ref[rows, :]
            ones = jnp.ones((3 * wb, LANES), BF16)
            dks, dvs = [], []
            for g in range(2):
                qr, op, do = [], [], []
                for i in range(2):
                    cols = slice((2 * g + i) * LANES, (2 * g + i + 1) * LANES)
                    zcols = slice(A_WIDTH + cols.start, A_WIDTH + cols.stop)
                    qr.append(_rope(b_ref[rows, cols], tab_v))
                    zp = b_ref[rows, zcols]
                    sg = _sigmoid(zp)
                    op.append(o_ref[rows, cols])
                    dyp = dy_ref[rows, cols]
                    do.append(dyp * (zp * sg))
                    dp_ref[rows, zcols] = (dyp * op[i] * (sg * (1.0 + zp * (1.0 - sg)))).astype(BF16)
                q4 = _stack_heads(qr[0], qr[1], lo).astype(BF16)
                do4 = _stack_heads(do[0], do[1], lo)
                o4 = jnp.concatenate([op[0], op[0], op[1], op[1]], axis=0)
                e, es = _head_exp(q4, k2[g], valid, _sink_column(sink_ref, g))
                inv = 1.0 / (jnp.dot(e, ones, preferred_element_type=F32) + es)
                prob = e.astype(F32) * jnp.concatenate([inv, inv, inv], axis=1)
                delta = jnp.sum(do4 * o4, axis=1, keepdims=True)
                do4b = do4.astype(BF16)
                dprob = lax.dot_general(do4b, v2[g], _DIMS["nt"], preferred_element_type=F32)
                dsc = (prob * (dprob - delta)).astype(BF16)
                sink_terms = (es * inv[:, :1]) * delta
                for i in range(HEADS_PER_GROUP):
                    h = HEADS_PER_GROUP * g + i
                    dsink = -jnp.sum(sink_terms[i * wb:(i + 1) * wb], axis=0, keepdims=True)
                    ds_ref[h:h + 1, :] += jnp.broadcast_to(dsink, (1, LANES))
                dq4 = jnp.dot(dsc, k2[g], preferred_element_type=F32) * scale
                for i in range(2):
                    cols = slice((2 * g + i) * LANES, (2 * g + i + 1) * LANES)
                    dp_ref[rows, cols] = _rope_transpose(_unstack_pair(dq4, i, lo), tab_v).astype(BF16)
                dk2 = lax.dot_general(dsc, q4, _DIMS["tn"], preferred_element_type=F32) * scale
                dv2 = lax.dot_general(prob.astype(BF16), do4b, _DIMS["tn"], preferred_element_type=F32)
                dks.append(dk2 + pltpu.roll(dk2, HALF_LANES, 1))
                dvs.append(dv2 + pltpu.roll(dv2, HALF_LANES, 1))
            start = _window_start(n)
            dk_ref[pl.ds(start, 3 * wb), :] += jnp.where(lo_w, dks[0], dks[1])
            dv_ref[pl.ds(start, 3 * wb), :] += jnp.where(lo_w, dvs[0], dvs[1])

    tq = SWA_BWD_BLOCKS * wb
    pad_spec = pl.BlockSpec((s + 2 * KV_PAD, LANES), lambda n: (0, 0))
    blk = pl.BlockSpec((tq, A_WIDTH), lambda n: (n, 0))
    bsp = pl.BlockSpec((tq, W_B), lambda n: (n, COL_B))
    pad_shape = jax.ShapeDtypeStruct((s + 2 * KV_PAD, LANES), F32)
    return pl.pallas_call(
        body, grid=(s // tq,),
        in_specs=[bsp, pad_spec, pad_spec, pl.BlockSpec((tq, 3 * LANES), lambda n: (n, 0)),
                  pl.BlockSpec(bias.shape, lambda n: (0, 0, 0)), pl.BlockSpec(memory_space=pltpu.SMEM), blk, blk,
                  pl.BlockSpec(memory_space=pl.ANY)],
        out_specs=[bsp, pad_spec, pad_spec, pl.BlockSpec((8, LANES), lambda n: (0, 0))],
        out_shape=[jax.ShapeDtypeStruct(dproj.shape, BF16), pad_shape, pad_shape,
                   jax.ShapeDtypeStruct((8, LANES), F32)],
        input_output_aliases={8: 0}, name="swa_bwd",
        compiler_params=_params())(proj, kpad, vpad, tab, bias, sink, o_attn, dyb, dproj)


def _mem_exp(qh, mk):
    sc = lax.dot_general(qh, mk, _DIMS["nt"], preferred_element_type=F32) * (MEM_HEAD_DIM ** -0.5)
    return jnp.exp(sc - jnp.max(sc, axis=1, keepdims=True)).astype(BF16)


def _mem_fwd(proj, mkv):
    s = proj.shape[0]
    ts = 512
    mlen = mkv.shape[0]

    def body(m_ref, kv_ref, o_ref, y_ref):
        ones = jnp.ones((mlen, LANES), BF16)
        for h in range(MEM_HEADS):
            cols = slice(h * LANES, (h + 1) * LANES)
            mk = kv_ref[:, cols].astype(BF16)
            mv = kv_ref[:, MEM_WIDTH + h * LANES:MEM_WIDTH + (h + 1) * LANES].astype(BF16)
            e = _mem_exp(m_ref[:, cols].astype(BF16), mk)
            ox = jnp.dot(e, jnp.concatenate([mv, ones], axis=1), preferred_element_type=F32)
            oh = ox[:, :LANES] * (1.0 / ox[:, LANES:])
            o_ref[:, cols] = oh
            zh = m_ref[:, MEM_WIDTH + h * LANES:MEM_WIDTH + (h + 1) * LANES]
            y_ref[:, cols] = (oh * (zh * _sigmoid(zh))).astype(BF16)

    o_spec = pl.BlockSpec((ts, MEM_WIDTH), lambda i: (i, 0))
    return pl.pallas_call(
        body, grid=(s // ts,),
        in_specs=[pl.BlockSpec((ts, W_M), lambda i: (i, COL_M)),
                  pl.BlockSpec((mlen, 2 * MEM_WIDTH), lambda i: (0, 0))],
        out_specs=[o_spec, o_spec],
        out_shape=[jax.ShapeDtypeStruct((s, MEM_WIDTH), F32), jax.ShapeDtypeStruct((s, MEM_WIDTH), BF16)],
        name="mem_fwd", compiler_params=_params())(proj, mkv)


def _mem_bwd(proj, mkv, o_mem, dym, dproj, *, carry=None):
    s = proj.shape[0]
    ts = 512
    mlen = mkv.shape[0]
    scale = MEM_HEAD_DIM ** -0.5

    def body(m_ref, kv_ref, o_ref, dy_ref, _, dp_ref, dkv_ref):
        @pl.when(pl.program_id(0) == 0)
        def _():
            dkv_ref[...] = jnp.zeros_like(dkv_ref)

        ones = jnp.ones((mlen, LANES), BF16)
        for h in range(MEM_HEADS):
            cols = slice(h * LANES, (h + 1) * LANES)
            vcols = slice(MEM_WIDTH + h * LANES, MEM_WIDTH + (h + 1) * LANES)
            mk = kv_ref[:, cols].astype(BF16)
            mv = kv_ref[:, vcols].astype(BF16)
            qh = m_ref[:, cols].astype(BF16)
            zh = m_ref[:, vcols]
            sg = _sigmoid(zh)
            oh = o_ref[:, cols]
            dyh = dy_ref[:, cols]
            doh = dyh * (zh * sg)
            dp_ref[:, vcols] = (dyh * oh * (sg * (1.0 + zh * (1.0 - sg)))).astype(BF16)
            e = _mem_exp(qh, mk)
            inv = 1.0 / jnp.dot(e, ones, preferred_element_type=F32)
            prob = e.astype(F32) * jnp.concatenate([inv] * (mlen // LANES), axis=1)
            delta = jnp.sum(doh * oh, axis=1, keepdims=True)
            dohb = doh.astype(BF16)
            dprob = lax.dot_general(dohb, mv, _DIMS["nt"], preferred_element_type=F32)
            dsc = (prob * (dprob - delta)).astype(BF16)
            dp_ref[:, cols] = (jnp.dot(dsc, mk, preferred_element_type=F32) * scale).astype(BF16)
            dkv_ref[:, cols] += lax.dot_general(dsc, qh, _DIMS["tn"], preferred_element_type=F32) * scale
            dkv_ref[:, vcols] += lax.dot_general(prob.astype(BF16), dohb, _DIMS["tn"],
                                                 preferred_element_type=F32)

    blk = pl.BlockSpec((ts, MEM_WIDTH), lambda i: (i, 0))
    msp = pl.BlockSpec((ts, W_M), lambda i: (i, COL_M))
    kvsp = pl.BlockSpec((mlen, 2 * MEM_WIDTH), lambda i: (0, 0))
    outs, carried = _carried_call(
        lambda ins, outs, scr: body(*ins, *outs), carry, grid=(s // ts,),
        in_specs=[msp, kvsp, blk, blk, pl.BlockSpec(memory_space=pl.ANY)],
        out_specs=[msp, kvsp],
        out_shape=[jax.ShapeDtypeStruct(dproj.shape, BF16), jax.ShapeDtypeStruct(mkv.shape, F32)],
        scratch=[], operands=(proj, mkv, o_mem, dym, dproj), name="mem_bwd", aliases={4: 0})
    return (*outs, carried) if carry else tuple(outs)


def _forward_backward(x, mem, tgt, proj, w_conv, sink, g_mem, late_weights, g_post, early_exchange):
    s = x.shape[0]
    tab = _rope_tables(s)
    bias = _window_bias()

    ya = _conv_fwd(proj, w_conv)
    kpad, vpad = _rope_kv(proj, tab)
    o_attn, yb, *arrived = _swa_fwd(proj, kpad, vpad, tab, bias, sink, carry=late_weights[0])
    w_kv, w_up, w_out = late_weights[1](arrived[0] if arrived else None)
    mn = _rmsnorm_fwd(mem, g_mem, name="mem_norm")
    mkv = _matmul(mn, w_kv, mode="nn", out_dtype=F32, tm=256, tn=1024, tk=D_MODEL, name="mem_kv")
    o_mem, ym = _mem_fwd(proj, mkv)
    merged, d_out, dy, dg_post, loss = _mid_fwd(ya, yb, ym, proj, x, tgt, w_up, w_out, g_post)
    dproj, d_ya, d_yb, d_ym, dw_up, dw_out = _mid_bwd(d_out, merged, ya, yb, ym, proj, w_up, w_out)

    dproj, dw_conv = _conv_bwd(proj, w_conv, d_ya, dproj)
    dproj, dkpad, dvpad, dsink = _swa_bwd(proj, kpad, vpad, tab, bias, sink, o_attn, d_yb, dproj)
    dproj = _rope_kv_bwd(dkpad, dvpad, tab, dproj)
    dproj, d_mkv, *early = _mem_bwd(proj, mkv, o_mem, d_ym, dproj, carry=early_exchange(dw_up, dw_out))

    dw_kv = _matmul(mn, d_mkv, mode="tn", out_dtype=F32, tm=1024, tn=1024, tk=256, name="dw_kv")
    d_mn = _matmul(d_mkv, w_kv, mode="nt", out_dtype=F32, tm=256, tn=1024, tk=D_MODEL, name="d_mn")
    _, dg_mem = _rmsnorm_bwd(d_mn, mem, g_mem, d_mn, name="mem_norm_bwd")

    return dict(loss=loss, dproj=dproj, dy=dy, w_conv=dw_conv, sink=dsink, g_mem=dg_mem,
                w_kv=dw_kv, w_up=dw_up, w_out=dw_out, g_post=dg_post, early=early[0] if early else None)


N_DEV = 8


def _position():
    return lax.axis_index("x"), lax.axis_index("y"), lax.axis_index("c")


def _other_chips(x, y):
    return (((1 - x, y), 2 * (1 - x) + y), ((x, 1 - y), 2 * x + (1 - y)), ((1 - x, 1 - y), 2 * (1 - x) + (1 - y)))


def _remote(src, dst, send_sems, recv_sems, k, device):
    return pltpu.make_async_remote_copy(src_ref=src, dst_ref=dst, send_sem=send_sems.at[k], recv_sem=recv_sems.at[k],
                                        device_id=device, device_id_type=MESH)


def _rows_half(ref, hf):
    rh = ref.shape[0] // 2
    return ref.at[pl.ds(pl.multiple_of(hf * rh, 8), rh)]


def _gather_weights(shards, small=None, relations=(0, 1, 2), into=None):
    n = len(shards)
    k = 0 if small is None else 1

    def peers(x, y):
        return [(r, chip, idx) for r, (chip, idx) in enumerate(_other_chips(x, y)) if r in relations]

    def ici(ins, outs, sems, a, r, chip, src_chip, c):
        return _remote(_rows_half(ins[a], c), _rows_half(outs[a].at[src_chip], c), sems[0], sems[1], 3 * a + r,
                       (*chip, c))

    def whole(ins, outs, sems, r, chip, src_chip, c):
        return _remote(ins[n], outs[n].at[src_chip], sems[0], sems[1], 3 * n + r, (*chip, c))

    def d2d(outs, sems, a, r, idx, hf, x, y, c):
        half = _rows_half(outs[a].at[idx], hf)
        return _remote(half, half, sems[2], sems[3], 3 * a + r, (x, y, 1 - c))

    def start(ins, outs, sems):
        x, y, c = _position()
        me = 2 * x + y
        for a in range(n):
            for r, chip, _ in peers(x, y):
                ici(ins, outs, sems, a, r, chip, me, c).start()
        for r, (chip, _) in enumerate(_other_chips(x, y)):
            if k:
                whole(ins, outs, sems, r, chip, me, c).start()

    def finish(ins, outs, sems):
        x, y, c = _position()
        me = 2 * x + y
        for a in range(n):
            for r, chip, idx in peers(x, y):
                ici(ins, outs, sems, a, r, chip, idx, c).wait_recv()
                d2d(outs, sems, a, r, idx, c, x, y, c).start()
        for a in range(n):
            for r, chip, idx in peers(x, y):
                d2d(outs, sems, a, r, idx, 1 - c, x, y, c).wait_recv()
        for r, (chip, idx) in enumerate(_other_chips(x, y)):
            if k:
                whole(ins, outs, sems, r, chip, idx, c).wait_recv()
                whole(ins, outs, sems, r, chip, me, c).wait_send()
        for a in range(n):
            for r, chip, idx in peers(x, y):
                ici(ins, outs, sems, a, r, chip, me, c).wait_send()
                d2d(outs, sems, a, r, idx, c, x, y, c).wait_send()

    operands = list(shards) + ([small] if k else [])
    shapes = [jax.ShapeDtypeStruct((N_CHIPS,) + s.shape, s.dtype) for s in operands]
    aliases = {}
    if into is not None:
        assert len(into) == len(operands)
        aliases = {len(operands) + a: a for a in range(len(into))}
        operands += list(into)
    return _Carry(operands, shapes,
                  [pltpu.SemaphoreType.DMA((3 * (n + k),)), pltpu.SemaphoreType.DMA((3 * (n + k),)),
                   pltpu.SemaphoreType.DMA((3 * n,)), pltpu.SemaphoreType.DMA((3 * n,))], start, finish, aliases)


def _pair_exchange(send):
    n = len(send)

    def copies(ins, outs, sems):
        x, y, c = _position()
        return [_remote(ins[a], outs[a], sems[0], sems[1], a, (x, y, 1 - c)) for a in range(n)]

    def start(ins, outs, sems):
        for cp in copies(ins, outs, sems):
            cp.start()

    def finish(ins, outs, sems):
        for cp in copies(ins, outs, sems):
            cp.wait()

    return _Carry(send, [jax.ShapeDtypeStruct(p.shape, p.dtype) for p in send],
                  [pltpu.SemaphoreType.DMA((n,)), pltpu.SemaphoreType.DMA((n,))], start, finish)


def _chip_exchange(sums):
    n = len(sums)

    def copies(ins, outs, sems):
        x, y, c = _position()
        return [_remote(ins[a].at[idx], outs[a].at[r], sems[0], sems[1], 3 * a + r, (*chip, c))
                for a in range(n) for r, (chip, idx) in enumerate(_other_chips(x, y))]

    def start(ins, outs, sems):
        for cp in copies(ins, outs, sems):
            cp.start()

    def finish(ins, outs, sems):
        for cp in copies(ins, outs, sems):
            cp.wait()

    return _Carry(sums, [jax.ShapeDtypeStruct((3,) + p.shape[1:], p.dtype) for p in sums],
                  [pltpu.SemaphoreType.DMA((3 * n,)), pltpu.SemaphoreType.DMA((3 * n,))], start, finish)


def _pair_share(pairs):
    n = len(pairs)

    def start(ins, outs, sems):
        x, y, c = _position()
        for a in range(n):
            _remote(outs[a].at[c], outs[a].at[c], sems[0], sems[1], a, (x, y, 1 - c)).start()

    def finish(ins, outs, sems):
        x, y, c = _position()
        for a in range(n):
            _remote(outs[a].at[1 - c], outs[a].at[1 - c], sems[0], sems[1], a, (x, y, 1 - c)).wait_recv()
        for a in range(n):
            _remote(outs[a].at[c], outs[a].at[c], sems[0], sems[1], a, (x, y, 1 - c)).wait_send()

    return _Carry(pairs, [jax.ShapeDtypeStruct(p.shape, p.dtype) for p in pairs],
                  [pltpu.SemaphoreType.DMA((n,)), pltpu.SemaphoreType.DMA((n,))], start, finish,
                  aliases={a: a for a in range(n)})


def _small_allreduce(pack, share):
    rows, width = pack.shape
    n_share = len(share.ins)

    def body(p_ref, *refs):
        share_in, o_ref, share_out = refs[:n_share], refs[n_share], refs[n_share + 1:2 * n_share + 1]
        buf, send_sems, recv_sems = refs[2 * n_share + 1:2 * n_share + 4]
        share_sems = refs[2 * n_share + 4:]
        share.start(share_in, share_out, share_sems)
        x, y, c = _position()
        me = 4 * x + 2 * y + c
        buf[me] = p_ref[...]
        peers = []
        for r in range(1, N_DEV):
            fx, fy, fc = (r >> 2) & 1, (r >> 1) & 1, r & 1
            px, py, pc = (1 - x if fx else x), (1 - y if fy else y), (1 - c if fc else c)
            peers.append(((px, py, pc), 4 * px + 2 * py + pc))
        sends = [_remote(p_ref, buf.at[me], send_sems, recv_sems, r, dev) for r, (dev, _) in enumerate(peers)]
        for cp in sends:
            cp.start()
        for r, (dev, idx) in enumerate(peers):
            _remote(p_ref, buf.at[idx], send_sems, recv_sems, r, dev).wait_recv()
        for cp in sends:
            cp.wait_send()
        acc = buf[0]
        for k in range(1, N_DEV):
            acc = acc + buf[k]
        o_ref[...] = acc
        share.finish(share_in, share_out, share_sems)

    vm = pl.BlockSpec(memory_space=pltpu.VMEM)
    red, *shared = pl.pallas_call(
        body, in_specs=[vm] + [_HBM] * n_share, out_specs=[vm] + [_HBM] * n_share,
        out_shape=[jax.ShapeDtypeStruct(pack.shape, F32)] + share.out_shapes,
        scratch_shapes=[pltpu.VMEM((N_DEV, rows, width), F32), pltpu.SemaphoreType.DMA((N_DEV - 1,)),
                        pltpu.SemaphoreType.DMA((N_DEV - 1,))] + share.sems,
        input_output_aliases={1 + i: 1 + o for i, o in share.aliases.items()},
        name="small_allreduce")(pack, *share.ins)
    return red, shared


ROW_TILE_MAX = 512
SUM_TILE_MAX = 2048
BF16_SUBLANES = 16


def _row_tile(rows, most=ROW_TILE_MAX):
    if rows <= most:
        return rows
    return max(t for t in range(BF16_SUBLANES, most + 1, BF16_SUBLANES) if rows % t == 0)


def _pair_add(keep, recv, name):
    nj, rh, cols = keep.shape
    tr = _row_tile(rh, SUM_TILE_MAX)

    def body(k_ref, r_ref, o_ref):
        o_ref[...] = (k_ref[...].astype(F32) + r_ref[...].astype(F32)).astype(BF16)

    blk = pl.BlockSpec((None, tr, cols), lambda j, i: (j, i, 0))
    return pl.pallas_call(body, grid=(nj, rh // tr), in_specs=[blk, blk], out_specs=blk,
                          out_shape=jax.ShapeDtypeStruct(keep.shape, BF16), name=name,
                          compiler_params=_params())(keep, recv)


def _chip_add(sums, recv, where, name):
    _, rh, cols = sums.shape
    tr = _row_tile(rh, SUM_TILE_MAX)

    def body(w_ref, s_ref, r_ref, o_ref):
        o_ref[...] = ((s_ref[...].astype(F32) + r_ref[0].astype(F32)) + r_ref[1].astype(F32)) + r_ref[2].astype(F32)

    grid_spec = pltpu.PrefetchScalarGridSpec(
        num_scalar_prefetch=1, grid=(rh // tr,),
        in_specs=[pl.BlockSpec((None, tr, cols), lambda i, w_ref: (w_ref[0], i, 0)),
                  pl.BlockSpec((3, tr, cols), lambda i, w_ref: (0, i, 0))],
        out_specs=pl.BlockSpec((None, tr, cols), lambda i, w_ref: (w_ref[1], i, 0)))
    return pl.pallas_call(body, grid_spec=grid_spec, out_shape=jax.ShapeDtypeStruct((2, rh, cols), F32),
                          name=name, compiler_params=_params())(where, sums, recv)


def _adamw(w, g, m, v, name):
    rows, cols = w.shape
    tr = _row_tile(rows)
    assert rows % tr == 0

    def body(w_ref, g_ref, m_ref, v_ref, d_ref, mo_ref, vo_ref):
        gv = g_ref[...]
        m_new = ADAM_B1 * m_ref[...] + (1.0 - ADAM_B1) * gv
        v_new = ADAM_B2 * v_ref[...] + (1.0 - ADAM_B2) * jnp.square(gv)
        m_hat = m_new / (1.0 - ADAM_B1 ** ADAM_STEP)
        v_hat = v_new / (1.0 - ADAM_B2 ** ADAM_STEP)
        d_ref[...] = -ADAM_LR * (m_hat / (jnp.sqrt(v_hat) + ADAM_EPS) + ADAM_WD * w_ref[...])
        mo_ref[...] = m_new
        vo_ref[...] = v_new

    blk = pl.BlockSpec((tr, cols), lambda i: (i, 0))
    shp = jax.ShapeDtypeStruct((rows, cols), F32)
    return pl.pallas_call(body, grid=(rows // tr,), in_specs=[blk] * 4, out_specs=[blk] * 3,
                          out_shape=[shp] * 3, name=name, compiler_params=_params())(w, g, m, v)


def _adamw_halves(w, g2, m, v, name):
    rows, cols = w.shape
    half = cols // 2
    tr = _row_tile(rows)

    def body(w_ref, g_ref, m_ref, v_ref, go_ref, d_ref, mo_ref, vo_ref):
        gv = g_ref[...]
        go_ref[...] = gv
        m_new = ADAM_B1 * m_ref[...] + (1.0 - ADAM_B1) * gv
        v_new = ADAM_B2 * v_ref[...] + (1.0 - ADAM_B2) * jnp.square(gv)
        m_hat = m_new / (1.0 - ADAM_B1 ** ADAM_STEP)
        v_hat = v_new / (1.0 - ADAM_B2 ** ADAM_STEP)
        d_ref[...] = -ADAM_LR * (m_hat / (jnp.sqrt(v_hat) + ADAM_EPS) + ADAM_WD * w_ref[...])
        mo_ref[...] = m_new
        vo_ref[...] = v_new

    blk = pl.BlockSpec((tr, half), lambda hf, i: (i, hf))
    gsp = pl.BlockSpec((None, tr, half), lambda hf, i: (hf, i, 0))
    shp = jax.ShapeDtypeStruct((rows, cols), F32)
    return pl.pallas_call(body, grid=(2, rows // tr), in_specs=[blk, gsp, blk, blk], out_specs=[blk] * 4,
                          out_shape=[shp] * 4, name=name, compiler_params=_params())(w, g2, m, v)


SHARD_W = IN_WIDTH // N_CHIPS


def _half_major(a):
    r, c = a.shape
    return a.reshape(N_CHIPS, 2, r // N_CHIPS // 2, c).transpose(1, 0, 2, 3)


def kernel(x, mem, g_pre, w_in, w_conv, attn_sink, g_mem, w_mem_kv, w_up_a, w_up_b, w_up_m, w_out, g_post, loss_target, m_g_pre, m_w_in, m_w_conv, m_attn_sink, m_g_mem, m_w_mem_kv, m_w_up_a, m_w_up_b, m_w_up_m, m_w_out, m_g_post, v_g_pre, v_w_in, v_w_conv, v_attn_sink, v_g_mem, v_w_mem_kv, v_w_up_a, v_w_up_b, v_w_up_m, v_w_out, v_g_post):
    xi, yi, ci = _position()
    chip = 2 * xi + yi
    where = jnp.stack([chip, ci, N_CHIPS - 1 - chip]).astype(jnp.int32)

    own = [w_in[0].T.astype(BF16), w_mem_kv[0].astype(BF16),
           jnp.concatenate([w_up_a[0], w_up_b[0], w_up_m[0]], axis=0).astype(BF16), w_out[0].astype(BF16)]
    own_conv = jnp.pad(w_conv[0], ((0, 5), (0, 0)))

    def pieces(mine, got):
        got = lax.dynamic_update_slice_in_dim(got, mine[None], chip, axis=0)
        return [got[j] for j in range(N_CHIPS)]

    diag = N_CHIPS - 1 - chip
    h, h_t = _rmsnorm_fwd(x[0], g_pre, name="pre_norm", transposed=True)
    proj, got_near, got_conv, got_far = _proj_near(h, own[0], own_conv, where)
    w_near = lax.dynamic_update_slice_in_dim(got_near, own[0][None], chip, axis=0).reshape(IN_WIDTH, D_MODEL)
    far = lax.dynamic_index_in_dim(got_far, diag, 0, keepdims=False)
    proj = _proj_far(h, w_near, far, where, into=proj)
    w_conv_full = jnp.concatenate([p[:3] for p in pieces(own_conv, got_conv)], axis=1)

    def late_weights(gathered):
        w_kv_full = jnp.concatenate(pieces(own[1], gathered[0]), axis=0)
        up_pieces = pieces(own[2], gathered[1])
        w_up_full = jnp.stack([jnp.concatenate([p[k * A_WIDTH:(k + 1) * A_WIDTH] for p in up_pieces], axis=1)
                               for k in range(3)])
        return w_kv_full, w_up_full, jnp.concatenate(pieces(own[3], gathered[2]), axis=0)

    def pick(parts, hf):
        return [lax.dynamic_index_in_dim(p, hf, 0, keepdims=False) for p in parts]

    def up_out_parts(dw_up, dw_out):
        up = (dw_up.reshape(3, A_WIDTH, N_CHIPS, D_MODEL // N_CHIPS).transpose(2, 0, 1, 3)
              .reshape(N_CHIPS, 2, 3 * A_WIDTH // 2, D_MODEL // N_CHIPS).transpose(1, 0, 2, 3))
        return [up.astype(BF16), _half_major(dw_out).astype(BF16)]

    g = _forward_backward(x[0], mem[0], loss_target[0], proj, w_conv_full, attn_sink, g_mem,
                          (_gather_weights(own[1:]), late_weights), g_post,
                          lambda dw_up, dw_out: _pair_exchange(pick(up_out_parts(dw_up, dw_out), 1 - ci)))

    half_rows = D_MODEL // 2

    def dw_in_half(half_of, name, carry):
        dw, carried = _dw_in_t(g["dproj"], h_t, half_of=half_of, where=where, name=name, carry=carry)
        return dw.reshape(N_CHIPS, SHARD_W, half_rows), carried

    sums_up_out = [_pair_add(k, r, "pair_add_" + nm)
                   for k, r, nm in zip(pick(up_out_parts(g["w_up"], g["w_out"]), ci), g["early"], ["w_up", "w_out"])]
    kv_parts = [_half_major(g["w_kv"]).astype(BF16)]
    dw_send, recv3_up_out = dw_in_half(lambda w: 1 - w[1], "dw_in_send", _chip_exchange(sums_up_out))
    dw_keep, (recv_in, recv_kv) = dw_in_half(lambda w: w[1], "dw_in_keep",
                                             _pair_exchange([dw_send] + pick(kv_parts, 1 - ci)))
    sum_in = _pair_add(dw_keep, recv_in, "pair_add_w_in")
    sum_kv = _pair_add(pick(kv_parts, ci)[0], recv_kv, "pair_add_w_kv")
    d_h, (recv3_in, recv3_kv) = _d_h(g["dproj"], w_near, far, where, carry=_chip_exchange([sum_in, sum_kv]))
    pairs = [_chip_add(s, r, where, "chip_add_" + nm)
             for s, r, nm in zip([sum_in, sum_kv] + sums_up_out, [recv3_in, recv3_kv] + recv3_up_out,
                                 ["w_in", "w_kv", "w_up", "w_out"])]
    grad_x, dg_pre = _rmsnorm_bwd(d_h, x[0], g_pre, g["dy"], name="pre_norm_bwd")

    zeros512 = jnp.zeros((1, D_MODEL - A_WIDTH), F32)
    conv_rows = [jnp.concatenate([g["w_conv"][k:k + 1], zeros512], axis=1) for k in range(3)]
    sink_row = jnp.pad(g["sink"][:, 0].reshape(1, N_Q_HEADS), ((0, 0), (0, D_MODEL - N_Q_HEADS)))
    loss_row = jnp.pad(g["loss"], ((0, 0), (0, D_MODEL - LANES)))
    pack = jnp.concatenate([dg_pre, g["g_mem"], g["g_post"]] + conv_rows + [sink_row, loss_row], axis=0)
    red, full = _small_allreduce(pack, _pair_share(pairs))
    loss = red[7, 0]
    small_grads = dict(
        g_pre=red[0:1], g_mem=red[1:2], g_post=red[2:3], attn_sink=red[6:7, :N_Q_HEADS],
        w_conv=lax.dynamic_slice(red[3:6, :A_WIDTH], (0, chip * LANES), (3, LANES)))

    gw_up = full[2].reshape(3, A_WIDTH, D_MODEL // N_CHIPS)
    grads = dict(small_grads, w_mem_kv=full[1].reshape(D_MODEL // N_CHIPS, 2 * MEM_WIDTH),
                 w_up_a=gw_up[0], w_up_b=gw_up[1], w_up_m=gw_up[2],
                 w_out=full[3].reshape(D_MODEL // N_CHIPS, D_MODEL))

    weights = dict(g_pre=g_pre, w_in=w_in, w_conv=w_conv, attn_sink=attn_sink, g_mem=g_mem, w_mem_kv=w_mem_kv,
                   w_up_a=w_up_a, w_up_b=w_up_b, w_up_m=w_up_m, w_out=w_out, g_post=g_post)
    m_in = dict(g_pre=m_g_pre, w_in=m_w_in, w_conv=m_w_conv, attn_sink=m_attn_sink, g_mem=m_g_mem,
                w_mem_kv=m_w_mem_kv, w_up_a=m_w_up_a, w_up_b=m_w_up_b, w_up_m=m_w_up_m, w_out=m_w_out,
                g_post=m_g_post)
    v_in = dict(g_pre=v_g_pre, w_in=v_w_in, w_conv=v_w_conv, attn_sink=v_attn_sink, g_mem=v_g_mem,
                w_mem_kv=v_w_mem_kv, w_up_a=v_w_up_a, w_up_b=v_w_up_b, w_up_m=v_w_up_m, w_out=v_w_out,
                g_post=v_g_post)
    out_g, out_d, out_m, out_v = [], [], [], []
    for nm in ("g_pre", "w_in", "w_conv", "attn_sink", "g_mem", "w_mem_kv", "w_up_a", "w_up_b", "w_up_m", "w_out",
               "g_post"):
        shape = weights[nm].shape
        if nm == "w_in":
            results = _adamw_halves(w_in[0].T, full[0], m_w_in[0].T, v_w_in[0].T, "adamw_w_in")
            for out, t in zip((out_g, out_d, out_m, out_v), results):
                out.append(t.T.reshape(shape))
            continue
        two_d = shape[-2:]
        gr = grads[nm].reshape(two_d)
        d, m_new, v_new = _adamw(weights[nm].reshape(two_d), gr, m_in[nm].reshape(two_d), v_in[nm].reshape(two_d),
                                 "adamw_" + nm)
        out_g.append(gr.reshape(shape))
        out_d.append(d.reshape(shape))
        out_m.append(m_new.reshape(shape))
        out_v.append(v_new.reshape(shape))
    return (loss, grad_x.reshape(x.shape), *out_g, *out_d, *out_m, *out_v)
```

```python
import jax
import jax.numpy as jnp
from jax import lax
from jax.experimental import pallas as pl
from jax.experimental.pallas import tpu as pltpu

F32 = jnp.float32
BF16 = jnp.bfloat16
MESH = pl.DeviceIdType.MESH

D_MODEL = 1024
EPS = 1e-6
A_WIDTH = 512
HEAD_DIM = 64
N_Q_HEADS = 8
WINDOW_BLOCK = 128
KV_PAD = 512
ROPE_THETA = 500000.0
ROT_DIM = 16
MEM_HEADS = 4
MEM_HEAD_DIM = 128
MEM_WIDTH = 512
IN_WIDTH = 7424
N_CHIPS = 4
LANES = 128
HALF_LANES = 64

PERM_SEGS = ((0, 2560), (2816, 3328), (4352, 7424), (3328, 4352), (2560, 2816))
COL_A, W_A = 0, 2048
COL_B, W_B = 2, 1024
COL_G, W_G = 1, 3072
COL_M, W_M = 6, 1024
COL_KV, W_KV = 28, 256

ADAM_LR = 0.001
ADAM_B1 = 0.9
ADAM_B2 = 0.999
ADAM_EPS = 1e-08
ADAM_WD = 0.01
ADAM_STEP = 10

VMEM_LIGHT_BYTES = 48 * 1024 * 1024
VMEM_HEAVY_BYTES = 48 * 1024 * 1024


_HBM = pl.BlockSpec(memory_space=pltpu.HBM)


def _params(heavy=False):
    return pltpu.CompilerParams(vmem_limit_bytes=VMEM_HEAVY_BYTES if heavy else VMEM_LIGHT_BYTES)


def _sigmoid(v):
    return jax.nn.sigmoid(v)


_DIMS = {"nn": (((1,), (0,)), ((), ())), "nt": (((1,), (1,)), ((), ())), "tn": (((0,), (0,)), ((), ()))}


class _Carry:
    def __init__(self, ins, out_shapes, sems, start, finish, aliases=None):
        self.ins, self.out_shapes, self.sems = list(ins), list(out_shapes), list(sems)
        self.start, self.finish, self.aliases = start, finish, dict(aliases or {})


def _join(*carries):
    def split(seq, counts):
        pos, parts = 0, []
        for n in counts:
            parts.append(seq[pos:pos + n])
            pos += n
        return parts

    n_in = [len(c.ins) for c in carries]
    n_out = [len(c.out_shapes) for c in carries]
    n_sem = [len(c.sems) for c in carries]

    def run(which):
        def go(ins, outs, sems):
            for c, i, o, sm in zip(carries, split(ins, n_in), split(outs, n_out), split(sems, n_sem)):
                getattr(c, which)(i, o, sm)
        return go

    aliases = {}
    for k, c in enumerate(carries):
        aliases.update({sum(n_in[:k]) + i: sum(n_out[:k]) + o for i, o in c.aliases.items()})
    return _Carry([a for c in carries for a in c.ins], [sh for c in carries for sh in c.out_shapes],
                  [sm for c in carries for sm in c.sems], run("start"), run("finish"), aliases)


def _carried_call(body, carry, *, grid, in_specs, out_specs, out_shape, scratch, operands, name, prefetch=None,
                  aliases=None, heavy=False):
    n_in, n_out, n_scr = len(in_specs), len(out_specs), len(scratch)
    c_in = len(carry.ins) if carry else 0
    c_out = len(carry.out_shapes) if carry else 0
    n_pre = 0 if prefetch is None else 1
    steps = 1
    for g in grid:
        steps *= g

    def wrapped(*refs):
        refs = refs[n_pre:]
        ins, cins = refs[:n_in], refs[n_in:n_in + c_in]
        outs = refs[n_in + c_in:n_in + c_in + n_out]
        couts = refs[n_in + c_in + n_out:n_in + c_in + n_out + c_out]
        rest = refs[n_in + c_in + n_out + c_out:]
        scr, sems = rest[:n_scr], rest[n_scr:]
        if carry:
            step = pl.program_id(0)
            for ax in range(1, len(grid)):
                step = step * grid[ax] + pl.program_id(ax)

            @pl.when(step == 0)
            def _():
                carry.start(cins, couts, sems)

        body(ins, outs, scr)
        if carry:
            @pl.when(step == steps - 1)
            def _():
                carry.finish(cins, couts, sems)

    all_aliases = {n_pre + i: o for i, o in (aliases or {}).items()}
    if carry:
        all_aliases.update({n_pre + n_in + i: n_out + o for i, o in carry.aliases.items()})
    all_in = list(in_specs) + [_HBM] * c_in
    all_out = list(out_specs) + [_HBM] * c_out
    all_scratch = list(scratch) + (carry.sems if carry else [])
    if n_pre:
        spec = dict(grid_spec=pltpu.PrefetchScalarGridSpec(num_scalar_prefetch=1, grid=grid, in_specs=all_in,
                                                           out_specs=all_out, scratch_shapes=all_scratch))
        pre = (prefetch,)
    else:
        spec = dict(grid=grid, in_specs=all_in, out_specs=all_out, scratch_shapes=all_scratch)
        pre = ()
    results = pl.pallas_call(
        wrapped, out_shape=list(out_shape) + (carry.out_shapes if carry else []), input_output_aliases=all_aliases,
        name=name, compiler_params=_params(heavy), **spec)(*pre, *operands, *(carry.ins if carry else []))
    return list(results[:n_out]), list(results[n_out:])


def _matmul(a, b, *, mode, out_dtype, tm, tn, tk, name):
    if mode == "nn":
        (m, k), (_, n) = a.shape, b.shape
    elif mode == "nt":
        (m, k), (n, _) = a.shape, b.shape
    else:
        (k, m), (_, n) = a.shape, b.shape
    tm, tn, tk = min(tm, m), min(tn, n), min(tk, k)
    assert m % tm == 0 and n % tn == 0 and k % tk == 0
    nk = k // tk
    dims = _DIMS[mode]

    if mode == "nn":
        a_spec = pl.BlockSpec((tm, tk), lambda i, j, kk: (i, kk))
        b_spec = pl.BlockSpec((tk, tn), lambda i, j, kk: (kk, j))
    elif mode == "nt":
        a_spec = pl.BlockSpec((tm, tk), lambda i, j, kk: (i, kk))
        b_spec = pl.BlockSpec((tn, tk), lambda i, j, kk: (j, kk))
    else:
        a_spec = pl.BlockSpec((tk, tm), lambda i, j, kk: (kk, i))
        b_spec = pl.BlockSpec((tk, tn), lambda i, j, kk: (kk, j))
    o_spec = pl.BlockSpec((tm, tn), lambda i, j, kk: (i, j))

    def part(a_ref, b_ref):
        return lax.dot_general(a_ref[...].astype(BF16), b_ref[...].astype(BF16), dims,
                               preferred_element_type=F32)

    if nk == 1:
        def body(a_ref, b_ref, o_ref):
            o_ref[...] = part(a_ref, b_ref).astype(out_dtype)
        scratch = []
    else:
        def body(a_ref, b_ref, o_ref, acc_ref):
            kk = pl.program_id(2)

            @pl.when(kk == 0)
            def _():
                acc_ref[...] = part(a_ref, b_ref)

            @pl.when(kk > 0)
            def _():
                acc_ref[...] += part(a_ref, b_ref)

            @pl.when(kk == nk - 1)
            def _():
                o_ref[...] = acc_ref[...].astype(out_dtype)
        scratch = [pltpu.VMEM((tm, tn), F32)]

    return pl.pallas_call(
        body, grid=(m // tm, n // tn, nk), in_specs=[a_spec, b_spec], out_specs=o_spec,
        out_shape=jax.ShapeDtypeStruct((m, n), out_dtype), scratch_shapes=scratch,
        name=name, compiler_params=_params())(a, b)


IN_BLOCK = 256
N_IN_BLOCKS = IN_WIDTH // IN_BLOCK
SHARD_BLOCKS = (IN_WIDTH // N_CHIPS) // IN_BLOCK
BLOCK_RUNS = tuple((a // IN_BLOCK, sum(d - c for c, d in PERM_SEGS[:k]) // IN_BLOCK, (b - a) // IN_BLOCK)
                   for k, (a, b) in enumerate(PERM_SEGS))


def _perm_block(r):
    p = r
    for ref0, perm0, n in BLOCK_RUNS:
        p = jnp.where((r >= ref0) & (r < ref0 + n), r - ref0 + perm0, p)
    return p


def _proj_near(h, own_w, small, where):
    s, d = h.shape
    n_own = SHARD_BLOCKS - 1
    n_diag = SHARD_BLOCKS + 1
    n_blocks = N_IN_BLOCKS - n_diag
    piece = IN_WIDTH // N_CHIPS - SHARD_BLOCKS * IN_BLOCK
    near = _gather_weights([own_w], small, relations=(0, 1))
    far = _gather_weights([own_w], relations=(2,))
    both = _join(near, far)
    n_cin, n_cout = len(both.ins), len(both.out_shapes)

    def block_of(i, w):
        me, dg = w[0], w[2]
        own0 = SHARD_BLOCKS * me + jnp.minimum(me, 1)
        dg0 = SHARD_BLOCKS * dg
        lo0, hi0 = jnp.minimum(own0, dg0), jnp.maximum(own0, dg0)
        lo_n = jnp.where(own0 < dg0, n_own, n_diag)
        hi_n = jnp.where(own0 < dg0, n_diag, n_own)
        r = i - n_own
        r = r + lo_n * (r >= lo0).astype(jnp.int32)
        r = r + hi_n * (r >= hi0).astype(jnp.int32)
        return jnp.where(i < n_own, own0 + i, r)

    def body(w_ref, h_ref, own_hbm, *refs):
        cins, o_ref, couts = refs[:n_cin], refs[n_cin], refs[n_cin + 1:n_cin + 1 + n_cout]
        blocks, block_sems = refs[n_cin + 1 + n_cout:n_cin + 3 + n_cout]
        sems = refs[n_cin + 3 + n_cout:]
        near_refs = (cins[:len(near.ins)], couts[:len(near.out_shapes)], sems[:len(near.sems)])
        far_refs = (cins[len(near.ins):], couts[len(near.out_shapes):], sems[len(near.sems):])
        gathered = couts[0]
        i = pl.program_id(0)
        me = w_ref[0]

        def fetch(step, slot):
            r = block_of(step, w_ref)
            for p in range(IN_BLOCK // piece):
                row = r * IN_BLOCK + p * piece
                j = row // (IN_WIDTH // N_CHIPS)
                off = pl.multiple_of(row - j * (IN_WIDTH // N_CHIPS), BF16_SUBLANES)
                dst = blocks.at[slot, pl.ds(p * piece, piece)]

                @pl.when(j == me)
                def _():
                    pltpu.make_async_copy(own_hbm.at[pl.ds(off, piece)], dst, block_sems.at[slot]).start()

                @pl.when(j != me)
                def _():
                    pltpu.make_async_copy(gathered.at[j, pl.ds(off, piece)], dst, block_sems.at[slot]).start()

        def arrived(slot):
            pltpu.make_async_copy(own_hbm.at[pl.ds(0, IN_BLOCK)], blocks.at[slot], block_sems.at[slot]).wait()

        slot = i % 2

        @pl.when(i == 0)
        def _():
            near.start(*near_refs)
            fetch(i, slot)

        @pl.when(i == n_own)
        def _():
            near.finish(*near_refs)
            far.start(*far_refs)
            fetch(i, slot)

        arrived(slot)

        @pl.when((i + 1 < n_blocks) & (i + 1 != n_own))
        def _():
            fetch(i + 1, 1 - slot)

        o_ref[...] = lax.dot_general(h_ref[...], blocks[slot], _DIMS["nt"], preferred_element_type=F32)

        @pl.when(i == n_blocks - 1)
        def _():
            far.finish(*far_refs)

    grid_spec = pltpu.PrefetchScalarGridSpec(
        num_scalar_prefetch=1, grid=(n_blocks,),
        in_specs=[pl.BlockSpec((s, d), lambda i, w: (0, 0)), pl.BlockSpec(memory_space=pl.ANY)] + [_HBM] * n_cin,
        out_specs=[pl.BlockSpec((s, IN_BLOCK), lambda i, w: (0, _perm_block(block_of(i, w))))] + [_HBM] * n_cout,
        scratch_shapes=[pltpu.VMEM((2, IN_BLOCK, d), BF16), pltpu.SemaphoreType.DMA((2,))] + both.sems)
    return pl.pallas_call(
        body, grid_spec=grid_spec, out_shape=[jax.ShapeDtypeStruct((s, IN_WIDTH), F32)] + both.out_shapes,
        name="proj_near", compiler_params=_params())(where, h, own_w, *both.ins)


def _proj_far(h, w_near, far, where, *, into, carry=None):
    s, d = h.shape
    n_blocks = SHARD_BLOCKS + 1
    lead = IN_WIDTH // N_CHIPS - SHARD_BLOCKS * IN_BLOCK

    def body(ins, outs, scr):
        where_ref, h_ref, w_hbm, far_hbm, _ = ins
        win, sem = scr
        i = pl.program_id(0)

        @pl.when(i == 0)
        def _():
            dg = where_ref[2]
            rows = pl.ds(pl.multiple_of(dg * (SHARD_BLOCKS * IN_BLOCK), IN_BLOCK), n_blocks * IN_BLOCK)
            window = pltpu.make_async_copy(w_hbm.at[rows], win, sem)
            window.start()
            window.wait()
            shard = pltpu.make_async_copy(far_hbm, win.at[pl.ds(pl.multiple_of(dg * lead, BF16_SUBLANES), SHARD_W)], sem)
            shard.start()
            shard.wait()

        blk = win[pl.ds(pl.multiple_of(i * IN_BLOCK, IN_BLOCK), IN_BLOCK), :]
        outs[0][...] = lax.dot_general(h_ref[...], blk, _DIMS["nt"], preferred_element_type=F32)

    anysp = pl.BlockSpec(memory_space=pl.ANY)
    (proj,), carried = _carried_call(
        body, carry, grid=(n_blocks,),
        in_specs=[pl.BlockSpec(memory_space=pltpu.SMEM), pl.BlockSpec((s, d), lambda i, w: (0, 0)), anysp, anysp, anysp],
        out_specs=[pl.BlockSpec((s, IN_BLOCK), lambda i, w: (0, _perm_block(i + SHARD_BLOCKS * w[2])))],
        out_shape=[jax.ShapeDtypeStruct((s, IN_WIDTH), F32)],
        scratch=[pltpu.VMEM((n_blocks * IN_BLOCK, d), BF16), pltpu.SemaphoreType.DMA],
        operands=(where, h, w_near, far, into), name="proj_far", prefetch=where, aliases={4: 0})
    return (proj, carried) if carry else proj


def _dw_in_t(dproj, h_t, *, half_of, where, name, carry=None):
    d, s = h_t.shape
    c = d // 2

    def body(ins, outs, scr):
        outs[0][...] = lax.dot_general(ins[1][...], ins[0][...], _DIMS["nn"], preferred_element_type=F32).T.astype(BF16)

    (dw,), carried = _carried_call(
        body, carry, grid=(N_IN_BLOCKS,),
        in_specs=[pl.BlockSpec((s, IN_BLOCK), lambda r, w: (0, _perm_block(r))),
                  pl.BlockSpec((c, s), lambda r, w: (half_of(w), 0))],
        out_specs=[pl.BlockSpec((IN_BLOCK, c), lambda r, w: (r, 0))],
        out_shape=[jax.ShapeDtypeStruct((IN_WIDTH, c), BF16)], scratch=[], operands=(dproj, h_t), name=name,
        prefetch=where)
    return (dw, carried) if carry else dw


def _d_h(dproj, w_near, far, where, *, carry=None):
    s = dproj.shape[0]
    d = w_near.shape[1]
    tm = min(s, 256)

    def body(ins, outs, scr):
        where_ref, a_ref, w_hbm, far_hbm = ins
        w_ref, sem = scr

        @pl.when(pl.program_id(0) == 0)
        def _():
            whole = pltpu.make_async_copy(w_hbm, w_ref, sem)
            whole.start()
            whole.wait()
            rows = pl.ds(pl.multiple_of(where_ref[2] * SHARD_W, BF16_SUBLANES), SHARD_W)
            part = pltpu.make_async_copy(far_hbm, w_ref.at[rows], sem)
            part.start()
            part.wait()

        acc = None
        for ref0, perm0, n in BLOCK_RUNS:
            term = jnp.dot(a_ref[:, perm0 * IN_BLOCK:(perm0 + n) * IN_BLOCK],
                           w_ref[ref0 * IN_BLOCK:(ref0 + n) * IN_BLOCK, :], preferred_element_type=F32)
            acc = term if acc is None else acc + term
        outs[0][...] = acc

    anysp = pl.BlockSpec(memory_space=pl.ANY)
    (dh,), carried = _carried_call(
        body, carry, grid=(s // tm,),
        in_specs=[pl.BlockSpec(memory_space=pltpu.SMEM), pl.BlockSpec((tm, IN_WIDTH), lambda i: (i, 0)), anysp, anysp],
        out_specs=[pl.BlockSpec((tm, d), lambda i: (i, 0))],
        out_shape=[jax.ShapeDtypeStruct((s, d), F32)],
        scratch=[pltpu.VMEM((IN_WIDTH, d), BF16), pltpu.SemaphoreType.DMA],
        operands=(where, dproj, w_near, far), name="d_h", heavy=True)
    return (dh, carried) if carry else dh


def _rmsnorm_fwd(x, g, *, name, transposed=False, carry=None):
    s, d = x.shape
    ts = min(512, s)

    def body(ins, outs, scr):
        xv = ins[0][...]
        r = lax.rsqrt(jnp.mean(xv * xv, axis=-1, keepdims=True) + EPS)
        hv = (xv * r) * ins[1][...]
        outs[0][...] = hv.astype(BF16)
        if transposed:
            outs[1][...] = hv.T.astype(BF16)

    out_specs = [pl.BlockSpec((ts, d), lambda i: (i, 0))]
    out_shape = [jax.ShapeDtypeStruct((s, d), BF16)]
    if transposed:
        out_specs.append(pl.BlockSpec((d, ts), lambda i: (0, i)))
        out_shape.append(jax.ShapeDtypeStruct((d, s), BF16))
    outs, carried = _carried_call(
        body, carry, grid=(s // ts,),
        in_specs=[pl.BlockSpec((ts, d), lambda i: (i, 0)), pl.BlockSpec((1, d), lambda i: (0, 0))],
        out_specs=out_specs, out_shape=out_shape, scratch=[], operands=(x, g), name=name)
    result = tuple(outs) if transposed else outs[0]
    return (result, carried) if carry else result


def _rmsnorm_bwd(dh, x, g, res, *, name, carry=None):
    s, d = x.shape
    ts = min(256, s)

    def body(ins, outs, scr):
        dh_ref, x_ref, g_ref, res_ref = ins
        dx_ref, dg_ref = outs
        xv = x_ref[...]
        r = lax.rsqrt(jnp.mean(xv * xv, axis=-1, keepdims=True) + EPS)
        xh = xv * r
        dhv = dh_ref[...]
        part = jnp.sum(dhv * xh, axis=0, keepdims=True)

        @pl.when(pl.program_id(0) == 0)
        def _():
            dg_ref[...] = part

        @pl.when(pl.program_id(0) > 0)
        def _():
            dg_ref[...] += part

        dxh = dhv * g_ref[...]
        dx_ref[...] = res_ref[...] + r * (dxh - xh * jnp.mean(dxh * xh, axis=-1, keepdims=True))

    row = pl.BlockSpec((ts, d), lambda i: (i, 0))
    vec = pl.BlockSpec((1, d), lambda i: (0, 0))
    outs, carried = _carried_call(
        body, carry, grid=(s // ts,), in_specs=[row, row, vec, row], out_specs=[row, vec],
        out_shape=[jax.ShapeDtypeStruct((s, d), F32), jax.ShapeDtypeStruct((1, d), F32)],
        scratch=[], operands=(dh, x, g, res), name=name)
    return (*outs, carried) if carry else tuple(outs)


MID_TILE = 256


def _gated_branches(y_refs, wup_ref, gl):
    d = D_MODEL
    us = [jnp.dot(y_refs[k][...], wup_ref[k], preferred_element_type=F32) for k in range(3)]
    sg = [_sigmoid(gl[:, k * d:(k + 1) * d]) for k in range(3)]
    return us, sg


def _mid_fwd(ya, yb, ym, proj, x, tgt, w_up, w_out, g_post):
    s, d = x.shape
    ts = MID_TILE

    def body(ya_ref, yb_ref, ym_ref, g_ref, x_ref, t_ref, wup_ref, wout_ref, gp_ref,
             m_ref, do_ref, dy_ref, dg_ref, loss_ref):
        us, sg = _gated_branches((ya_ref, yb_ref, ym_ref), wup_ref, g_ref[...])
        merged = (sg[0] * us[0] + sg[1] * us[1] + sg[2] * us[2]).astype(BF16)
        m_ref[...] = merged
        ov = jnp.dot(merged, wout_ref[...], preferred_element_type=F32)
        r = lax.rsqrt(jnp.mean(ov * ov, axis=-1, keepdims=True) + EPS)
        nh = ov * r
        gv = gp_ref[...]
        e = (x_ref[...] + nh * gv) - t_ref[...]
        lpart = 0.5 * jnp.sum(jnp.mean(e * e, axis=-1, keepdims=True), axis=0, keepdims=True)
        dy = e * (1.0 / d)
        dgp = jnp.sum(dy * nh, axis=0, keepdims=True)

        @pl.when(pl.program_id(0) == 0)
        def _():
            dg_ref[...] = dgp
            loss_ref[...] = jnp.broadcast_to(lpart, loss_ref.shape)

        @pl.when(pl.program_id(0) > 0)
        def _():
            dg_ref[...] += dgp
            loss_ref[...] += jnp.broadcast_to(lpart, loss_ref.shape)

        dn = dy * gv
        dy_ref[...] = dy
        do_ref[...] = (r * (dn - nh * jnp.mean(dn * nh, axis=-1, keepdims=True))).astype(BF16)

    row = pl.BlockSpec((ts, d), lambda i: (i, 0))
    ysp = pl.BlockSpec((ts, A_WIDTH), lambda i: (i, 0))
    vec = pl.BlockSpec((1, d), lambda i: (0, 0))
    return pl.pallas_call(
        body, grid=(s // ts,),
        in_specs=[ysp, ysp, ysp, pl.BlockSpec((ts, W_G), lambda i: (i, COL_G)), row, row,
                  pl.BlockSpec((3, A_WIDTH, d), lambda i: (0, 0, 0)), pl.BlockSpec((d, d), lambda i: (0, 0)), vec],
        out_specs=[row, row, row, vec, pl.BlockSpec((1, LANES), lambda i: (0, 0))],
        out_shape=[jax.ShapeDtypeStruct((s, d), BF16), jax.ShapeDtypeStruct((s, d), BF16),
                   jax.ShapeDtypeStruct((s, d), F32), jax.ShapeDtypeStruct((1, d), F32),
                   jax.ShapeDtypeStruct((1, LANES), F32)],
        name="mid_fwd", compiler_params=_params(heavy=True))(ya, yb, ym, proj, x, tgt, w_up, w_out, g_post)


def _mid_bwd(d_out, merged, ya, yb, ym, proj, w_up, w_out):
    s, d = merged.shape
    ts = MID_TILE
    last = s // ts - 1

    def body(do_ref, m_ref, ya_ref, yb_ref, ym_ref, g_ref, wup_ref, wout_ref,
             dp_ref, dya_ref, dyb_ref, dym_ref, dwup_hbm, dwout_hbm, dwup_acc, dwout_acc):
        i = pl.program_id(0)

        @pl.when(i == 0)
        def _():
            dwup_acc[...] = jnp.zeros_like(dwup_acc)
            dwout_acc[...] = jnp.zeros_like(dwout_acc)

        y_refs = (ya_ref, yb_ref, ym_ref)
        us, sg = _gated_branches(y_refs, wup_ref, g_ref[...])
        dov = do_ref[...]
        dwout_acc[...] += lax.dot_general(m_ref[...], dov, _DIMS["tn"], preferred_element_type=F32)
        dm = lax.dot_general(dov, wout_ref[...], _DIMS["nt"], preferred_element_type=F32)
        for k, dy_ref in enumerate((dya_ref, dyb_ref, dym_ref)):
            dp_ref[:, k * d:(k + 1) * d] = ((dm * us[k]) * (sg[k] * (1.0 - sg[k]))).astype(BF16)
            du = (sg[k] * dm).astype(BF16)
            dy_ref[...] = lax.dot_general(du, wup_ref[k], _DIMS["nt"], preferred_element_type=F32)
            dwup_acc[k] += lax.dot_general(y_refs[k][...], du, _DIMS["tn"], preferred_element_type=F32)

        @pl.when(i == last)
        def _():
            pltpu.sync_copy(dwup_acc, dwup_hbm)
            pltpu.sync_copy(dwout_acc, dwout_hbm)

    row = pl.BlockSpec((ts, d), lambda i: (i, 0))
    ysp = pl.BlockSpec((ts, A_WIDTH), lambda i: (i, 0))
    gsp = pl.BlockSpec((ts, W_G), lambda i: (i, COL_G))
    anysp = pl.BlockSpec(memory_space=pl.ANY)
    yshape = jax.ShapeDtypeStruct((s, A_WIDTH), F32)
    return pl.pallas_call(
        body, grid=(s // ts,),
        in_specs=[row, row, ysp, ysp, ysp, gsp, pl.BlockSpec((3, A_WIDTH, d), lambda i: (0, 0, 0)),
                  pl.BlockSpec((d, d), lambda i: (0, 0))],
        out_specs=[gsp, ysp, ysp, ysp, anysp, anysp],
        out_shape=[jax.ShapeDtypeStruct((s, IN_WIDTH), BF16), yshape, yshape, yshape,
                   jax.ShapeDtypeStruct((3, A_WIDTH, d), F32), jax.ShapeDtypeStruct((d, d), F32)],
        scratch_shapes=[pltpu.VMEM((3, A_WIDTH, d), F32), pltpu.VMEM((d, d), F32)],
        name="mid_bwd", compiler_params=_params(heavy=True))(d_out, merged, ya, yb, ym, proj, w_up, w_out)


def _conv_core(blk, prev, nxt, w, i, last, ts):
    c = A_WIDTH
    ab, ac, ax, az = blk[:, :c], blk[:, c:2 * c], blk[:, 2 * c:3 * c], blk[:, 3 * c:]
    cu = ac * ax
    cu_prev = (prev[7:8, c:2 * c] * prev[7:8, 2 * c:3 * c]) * jnp.where(i > 0, 1.0, 0.0)
    cu_next = (nxt[0:1, c:2 * c] * nxt[0:1, 2 * c:3 * c]) * jnp.where(i < last, 1.0, 0.0)
    row = lax.broadcasted_iota(jnp.int32, (ts, c), 0)
    cm1 = jnp.where(row == 0, cu_prev, pltpu.roll(cu, 1, 0))
    cp1 = jnp.where(row == ts - 1, cu_next, pltpu.roll(cu, ts - 1, 0))
    yc = cm1 * w[0:1] + cu * w[1:2] + cp1 * w[2:3]
    return ab, ac, ax, az, cu, cm1, cp1, yc, row


def _halo_specs(ts, width, col, nblk8):
    prev = pl.BlockSpec((8, width), lambda i: (jnp.maximum(i * (ts // 8) - 1, 0), col))
    nxt = pl.BlockSpec((8, width), lambda i: (jnp.minimum((i + 1) * (ts // 8), nblk8 - 1), col))
    return prev, nxt


def _conv_fwd(proj, w_conv):
    s = proj.shape[0]
    ts = 256
    last = s // ts - 1

    def body(a_ref, ap_ref, an_ref, w_ref, ya_ref):
        i = pl.program_id(0)
        ab, _, _, az, _, _, _, yc, _ = _conv_core(a_ref[...], ap_ref[...], an_ref[...], w_ref[...], i, last, ts)
        ya_ref[...] = ((ab * yc) * (az * _sigmoid(az))).astype(BF16)

    prev, nxt = _halo_specs(ts, W_A, COL_A, s // 8)
    return pl.pallas_call(
        body, grid=(s // ts,),
        in_specs=[pl.BlockSpec((ts, W_A), lambda i: (i, COL_A)), prev, nxt,
                  pl.BlockSpec((3, A_WIDTH), lambda i: (0, 0))],
        out_specs=pl.BlockSpec((ts, A_WIDTH), lambda i: (i, 0)),
        out_shape=jax.ShapeDtypeStruct((s, A_WIDTH), BF16), name="conv_fwd",
        compiler_params=_params())(proj, proj, proj, w_conv)


def _conv_bwd(proj, w_conv, dya, dproj):
    s = proj.shape[0]
    ts = 256
    last = s // ts - 1
    c = A_WIDTH

    def body(a_ref, ap_ref, an_ref, w_ref, d_ref, dp_ref, dn_ref, _, dproj_ref, dw_ref):
        i = pl.program_id(0)
        w = w_ref[...]
        prev, nxt = ap_ref[...], an_ref[...]
        ab, ac, ax, az, cu, cm1, cp1, yc, row = _conv_core(a_ref[...], prev, nxt, w, i, last, ts)
        sg = _sigmoid(az)
        sz = az * sg
        dya_v = d_ref[...]
        dyc = dya_v * sz * ab
        dproj_ref[:, :c] = (dya_v * sz * yc).astype(BF16)
        dproj_ref[:, 3 * c:] = (dya_v * (ab * yc) * (sg * (1.0 + az * (1.0 - sg)))).astype(BF16)

        def halo_dyc(a_row, d_row):
            azr = a_row[:, 3 * c:]
            return d_row * (azr * _sigmoid(azr)) * a_row[:, :c]

        dyc_prev = halo_dyc(prev[7:8], dp_ref[...][7:8]) * jnp.where(i > 0, 1.0, 0.0)
        dyc_next = halo_dyc(nxt[0:1], dn_ref[...][0:1]) * jnp.where(i < last, 1.0, 0.0)
        dyc_m1 = jnp.where(row == 0, dyc_prev, pltpu.roll(dyc, 1, 0))
        dyc_p1 = jnp.where(row == ts - 1, dyc_next, pltpu.roll(dyc, ts - 1, 0))
        dcu = dyc_p1 * w[0:1] + dyc * w[1:2] + dyc_m1 * w[2:3]
        dproj_ref[:, c:2 * c] = (dcu * ax).astype(BF16)
        dproj_ref[:, 2 * c:3 * c] = (dcu * ac).astype(BF16)
        dw = [jnp.sum(dyc * t, axis=0, keepdims=True) for t in (cm1, cu, cp1)]

        @pl.when(i == 0)
        def _():
            for k in range(3):
                dw_ref[k:k + 1, :] = dw[k]

        @pl.when(i > 0)
        def _():
            for k in range(3):
                dw_ref[k:k + 1, :] += dw[k]

    prev, nxt = _halo_specs(ts, W_A, COL_A, s // 8)
    dprev, dnxt = _halo_specs(ts, A_WIDTH, 0, s // 8)
    return pl.pallas_call(
        body, grid=(s // ts,),
        in_specs=[pl.BlockSpec((ts, W_A), lambda i: (i, COL_A)), prev, nxt,
                  pl.BlockSpec((3, A_WIDTH), lambda i: (0, 0)),
                  pl.BlockSpec((ts, A_WIDTH), lambda i: (i, 0)), dprev, dnxt,
                  pl.BlockSpec(memory_space=pl.ANY)],
        out_specs=[pl.BlockSpec((ts, W_A), lambda i: (i, COL_A)), pl.BlockSpec((3, A_WIDTH), lambda i: (0, 0))],
        out_shape=[jax.ShapeDtypeStruct(dproj.shape, BF16), jax.ShapeDtypeStruct((3, A_WIDTH), F32)],
        input_output_aliases={7: 0}, name="conv_bwd",
        compiler_params=_params())(proj, proj, proj, w_conv, dya, dya, dya, dproj)


def _rope_tables(s):
    half = ROT_DIM // 2
    dim = jnp.arange(LANES) % HEAD_DIM
    inv_freq = jnp.power(jnp.float32(ROPE_THETA), -(dim % half).astype(F32) * (2.0 / ROT_DIM))
    ang = jnp.arange(s).astype(F32)[:, None] * inv_freq[None, :]
    cos, sin = jnp.cos(ang), jnp.sin(ang)
    first, second = (dim < half)[None, :], ((dim >= half) & (dim < ROT_DIM))[None, :]
    c = jnp.where(first | second, cos, 1.0)
    s1 = jnp.where(first, -sin, 0.0)
    s2 = jnp.where(second, sin, 0.0)
    return jnp.concatenate([c, s1, s2], axis=1)


def _rope(t, tab):
    return (t * tab[:, :LANES] + pltpu.roll(t, LANES - 8, 1) * tab[:, LANES:2 * LANES]
            + pltpu.roll(t, 8, 1) * tab[:, 2 * LANES:])


def _rope_transpose(dt, tab):
    return (dt * tab[:, :LANES] + pltpu.roll(dt * tab[:, LANES:2 * LANES], 8, 1)
            + pltpu.roll(dt * tab[:, 2 * LANES:], LANES - 8, 1))


def _rope_kv(proj, tab):
    s = proj.shape[0]
    nb = s // KV_PAD

    def body(kv_ref, t_ref, k_ref, v_ref):
        j = pl.program_id(0)
        inside = jnp.where((j > 0) & (j <= nb), 1.0, 0.0)
        kv = kv_ref[...]
        k_ref[...] = (_rope(kv[:, :LANES], t_ref[...]) * inside).astype(BF16)
        v_ref[...] = (kv[:, LANES:] * inside).astype(BF16)

    def src(j):
        return jnp.clip(j - 1, 0, nb - 1)

    o_spec = pl.BlockSpec((KV_PAD, LANES), lambda j: (j, 0))
    shp = jax.ShapeDtypeStruct((s + 2 * KV_PAD, LANES), BF16)
    return pl.pallas_call(
        body, grid=(nb + 2,),
        in_specs=[pl.BlockSpec((KV_PAD, W_KV), lambda j: (src(j), COL_KV)),
                  pl.BlockSpec((KV_PAD, 3 * LANES), lambda j: (src(j), 0))],
        out_specs=[o_spec, o_spec], out_shape=[shp, shp], name="rope_kv",
        compiler_params=_params())(proj, tab)


def _rope_kv_bwd(dkpad, dvpad, tab, dproj):
    s = tab.shape[0]
    nb = s // KV_PAD

    def body(dk_ref, dv_ref, t_ref, _, dp_ref):
        dp_ref[:, :LANES] = _rope_transpose(dk_ref[...], t_ref[...]).astype(BF16)
        dp_ref[:, LANES:] = dv_ref[...].astype(BF16)

    pad_spec = pl.BlockSpec((KV_PAD, LANES), lambda j: (j + 1, 0))
    return pl.pallas_call(
        body, grid=(nb,),
        in_specs=[pad_spec, pad_spec, pl.BlockSpec((KV_PAD, 3 * LANES), lambda j: (j, 0)),
                  pl.BlockSpec(memory_space=pl.ANY)],
        out_specs=pl.BlockSpec((KV_PAD, W_KV), lambda j: (j, COL_KV)),
        out_shape=jax.ShapeDtypeStruct(dproj.shape, BF16), input_output_aliases={3: 0},
        name="rope_kv_bwd", compiler_params=_params())(dkpad, dvpad, tab, dproj)


def _window_start(n):
    return pl.multiple_of((n - 1) * WINDOW_BLOCK + KV_PAD, WINDOW_BLOCK)


def _window_operands(k_ref, v_ref, n, lo):
    start = _window_start(n)
    kw = k_ref[pl.ds(start, 3 * WINDOW_BLOCK), :].astype(F32)
    vw = v_ref[pl.ds(start, 3 * WINDOW_BLOCK), :].astype(F32)
    kr, vr = pltpu.roll(kw, HALF_LANES, 1), pltpu.roll(vw, HALF_LANES, 1)
    k2 = (jnp.where(lo, kw, kr).astype(BF16), jnp.where(lo, kr, kw).astype(BF16))
    v2 = (jnp.where(lo, vw, vr).astype(BF16), jnp.where(lo, vr, vw).astype(BF16))
    return k2, v2


HEADS_PER_GROUP = 4
SWA_FWD_BLOCKS = 1
SWA_BWD_BLOCKS = 2


def _window_bias():
    wb = WINDOW_BLOCK
    qi = lax.broadcasted_iota(jnp.int32, (wb, 3 * wb), 0)
    kj = lax.broadcasted_iota(jnp.int32, (wb, 3 * wb), 1)
    band = (kj >= qi) & (kj <= qi + 2 * wb)
    cases = jnp.stack([band & (kj >= wb), band, band & (kj < 2 * wb)])
    return jnp.where(cases, 0.0, -jnp.inf).astype(F32)


def _block_bias(bias_ref, n, n_blocks):
    case = jnp.where(n == 0, 0, jnp.where(n == n_blocks - 1, 2, 1))
    one = bias_ref[case]
    return jnp.concatenate([one] * HEADS_PER_GROUP, axis=0)


def _stack_heads(pair0, pair1, lo):
    return jnp.concatenate([jnp.where(lo, pair0, 0.0), jnp.where(lo, 0.0, pair0),
                            jnp.where(lo, pair1, 0.0), jnp.where(lo, 0.0, pair1)], axis=0)


def _unstack_pair(stacked, i, lo):
    wb = WINDOW_BLOCK
    return jnp.where(lo, stacked[2 * i * wb:(2 * i + 1) * wb], stacked[(2 * i + 1) * wb:(2 * i + 2) * wb])


def _sink_column(sink_ref, g):
    wb = WINDOW_BLOCK
    return jnp.concatenate([jnp.full((wb, 1), sink_ref[0, HEADS_PER_GROUP * g + i], F32)
                            for i in range(HEADS_PER_GROUP)], axis=0)


def _head_exp(q4, k2g, bias, sink):
    sc = lax.dot_general(q4, k2g, _DIMS["nt"], preferred_element_type=F32) * (HEAD_DIM ** -0.5) + bias
    m = jnp.maximum(jnp.max(sc, axis=1, keepdims=True), sink)
    return jnp.exp(sc - m).astype(BF16), jnp.exp(sink - m)


def _swa_fwd(proj, kpad, vpad, tab, bias, sink, *, carry=None):
    s = proj.shape[0]
    wb = WINDOW_BLOCK

    def body(b_ref, k_ref, v_ref, t_ref, bias_ref, sink_ref, o_ref, y_ref):
        lo = lax.broadcasted_iota(jnp.int32, (wb, LANES), 1) < HALF_LANES
        lo_w = lax.broadcasted_iota(jnp.int32, (3 * wb, LANES), 1) < HALF_LANES
        for sub in range(SWA_FWD_BLOCKS):
            n = pl.program_id(0) * SWA_FWD_BLOCKS + sub
            rows = slice(sub * wb, (sub + 1) * wb)
            k2, v2 = _window_operands(k_ref, v_ref, n, lo_w)
            valid = _block_bias(bias_ref, n, s // wb)
            tab_v = t_ref[rows, :]
            ones = jnp.ones((3 * wb, LANES), BF16)
            for g in range(2):
                qr = [_rope(b_ref[rows, (2 * g + i) * LANES:(2 * g + i + 1) * LANES], tab_v) for i in range(2)]
                q4 = _stack_heads(qr[0], qr[1], lo).astype(BF16)
                e, es = _head_exp(q4, k2[g], valid, _sink_column(sink_ref, g))
                ox = jnp.dot(e, jnp.concatenate([v2[g], ones], axis=1), preferred_element_type=F32)
                o4 = ox[:, :LANES] * (1.0 / (ox[:, LANES:] + es))
                for i in range(2):
                    cols = slice((2 * g + i) * LANES, (2 * g + i + 1) * LANES)
                    op = _unstack_pair(o4, i, lo)
                    o_ref[rows, cols] = op
                    zp = b_ref[rows, A_WIDTH + cols.start:A_WIDTH + cols.stop]
                    y_ref[rows, cols] = (op * (zp * _sigmoid(zp))).astype(BF16)

    tq = SWA_FWD_BLOCKS * wb
    pad_spec = pl.BlockSpec((s + 2 * KV_PAD, LANES), lambda n: (0, 0))
    o_spec = pl.BlockSpec((tq, A_WIDTH), lambda n: (n, 0))
    outs, carried = _carried_call(
        lambda ins, outs, scr: body(*ins, *outs), carry, grid=(s // tq,),
        in_specs=[pl.BlockSpec((tq, W_B), lambda n: (n, COL_B)), pad_spec, pad_spec,
                  pl.BlockSpec((tq, 3 * LANES), lambda n: (n, 0)),
                  pl.BlockSpec(bias.shape, lambda n: (0, 0, 0)), pl.BlockSpec(memory_space=pltpu.SMEM)],
        out_specs=[o_spec, o_spec],
        out_shape=[jax.ShapeDtypeStruct((s, A_WIDTH), F32), jax.ShapeDtypeStruct((s, A_WIDTH), BF16)],
        scratch=[], operands=(proj, kpad, vpad, tab, bias, sink), name="swa_fwd")
    return (*outs, carried) if carry else tuple(outs)


def _swa_bwd(proj, kpad, vpad, tab, bias, sink, o_attn, dyb, dproj):
    s = proj.shape[0]
    wb = WINDOW_BLOCK
    scale = HEAD_DIM ** -0.5

    def body(b_ref, k_ref, v_ref, t_ref, bias_ref, sink_ref, o_ref, dy_ref, _, dp_ref, dk_ref, dv_ref, ds_ref):
        @pl.when(pl.program_id(0) == 0)
        def _():
            dk_ref[...] = jnp.zeros_like(dk_ref)
            dv_ref[...] = jnp.zeros_like(dv_ref)
            ds_ref[...] = jnp.zeros_like(ds_ref)

        lo = lax.broadcasted_iota(jnp.int32, (wb, LANES), 1) < HALF_LANES
        lo_w = lax.broadcasted_iota(jnp.int32, (3 * wb, LANES), 1) < HALF_LANES
        for sub in range(SWA_BWD_BLOCKS):
            n = pl.program_id(0) * SWA_BWD_BLOCKS + sub
            rows = slice(sub * wb, (sub + 1) * wb)
            k2, v2 = _window_operands(k_ref, v_ref, n, lo_w)
            valid = _block_bias(bias_ref, n, s // wb)
            tab_v = t_ref[rows, :]
            ones = jnp.ones((3 * wb, LANES), BF16)
            dks, dvs = [], []
            for g in range(2):
                qr, op, do = [], [], []
                for i in range(2):
                    cols = slice((2 * g + i) * LANES, (2 * g + i + 1) * LANES)
                    zcols = slice(A_WIDTH + cols.start, A_WIDTH + cols.stop)
                    qr.append(_rope(b_ref[rows, cols], tab_v))
                    zp = b_ref[rows, zcols]
                    sg = _sigmoid(zp)
                    op.append(o_ref[rows, cols])
                    dyp = dy_ref[rows, cols]
                    do.append(dyp * (zp * sg))
                    dp_ref[rows, zcols] = (dyp * op[i] * (sg * (1.0 + zp * (1.0 - sg)))).astype(BF16)
                q4 = _stack_heads(qr[0], qr[1], lo).astype(BF16)
                do4 = _stack_heads(do[0], do[1], lo)
                o4 = jnp.concatenate([op[0], op[0], op[1], op[1]], axis=0)
                e, es = _head_exp(q4, k2[g], valid, _sink_column(sink_ref, g))
                inv = 1.0 / (jnp.dot(e, ones, preferred_element_type=F32) + es)
                prob = e.astype(F32) * jnp.concatenate([inv, inv, inv], axis=1)
                delta = jnp.sum(do4 * o4, axis=1, keepdims=True)
                do4b = do4.astype(BF16)
                dprob = lax.dot_general(do4b, v2[g], _DIMS["nt"], preferred_element_type=F32)
                dsc = (prob * (dprob - delta)).astype(BF16)
                sink_terms = (es * inv[:, :1]) * delta
                for i in range(HEADS_PER_GROUP):
                    h = HEADS_PER_GROUP * g + i
                    dsink = -jnp.sum(sink_terms[i * wb:(i + 1) * wb], axis=0, keepdims=True)
                    ds_ref[h:h + 1, :] += jnp.broadcast_to(dsink, (1, LANES))
                dq4 = jnp.dot(dsc, k2[g], preferred_element_type=F32) * scale
                for i in range(2):
                    cols = slice((2 * g + i) * LANES, (2 * g + i + 1) * LANES)
                    dp_ref[rows, cols] = _rope_transpose(_unstack_pair(dq4, i, lo), tab_v).astype(BF16)
                dk2 = lax.dot_general(dsc, q4, _DIMS["tn"], preferred_element_type=F32) * scale
                dv2 = lax.dot_general(prob.astype(BF16), do4b, _DIMS["tn"], preferred_element_type=F32)
                dks.append(dk2 + pltpu.roll(dk2, HALF_LANES, 1))
                dvs.append(dv2 + pltpu.roll(dv2, HALF_LANES, 1))
            start = _window_start(n)
            dk_ref[pl.ds(start, 3 * wb), :] += jnp.where(lo_w, dks[0], dks[1])
            dv_ref[pl.ds(start, 3 * wb), :] += jnp.where(lo_w, dvs[0], dvs[1])

    tq = SWA_BWD_BLOCKS * wb
    pad_spec = pl.BlockSpec((s + 2 * KV_PAD, LANES), lambda n: (0, 0))
    blk = pl.BlockSpec((tq, A_WIDTH), lambda n: (n, 0))
    bsp = pl.BlockSpec((tq, W_B), lambda n: (n, COL_B))
    pad_shape = jax.ShapeDtypeStruct((s + 2 * KV_PAD, LANES), F32)
    return pl.pallas_call(
        body, grid=(s // tq,),
        in_specs=[bsp, pad_spec, pad_spec, pl.BlockSpec((tq, 3 * LANES), lambda n: (n, 0)),
                  pl.BlockSpec(bias.shape, lambda n: (0, 0, 0)), pl.BlockSpec(memory_space=pltpu.SMEM), blk, blk,
                  pl.BlockSpec(memory_space=pl.ANY)],
        out_specs=[bsp, pad_spec, pad_spec, pl.BlockSpec((8, LANES), lambda n: (0, 0))],
        out_shape=[jax.ShapeDtypeStruct(dproj.shape, BF16), pad_shape, pad_shape,
                   jax.ShapeDtypeStruct((8, LANES), F32)],
        input_output_aliases={8: 0}, name="swa_bwd",
        compiler_params=_params())(proj, kpad, vpad, tab, bias, sink, o_attn, dyb, dproj)


def _mem_exp(qh, mk):
    sc = lax.dot_general(qh, mk, _DIMS["nt"], preferred_element_type=F32) * (MEM_HEAD_DIM ** -0.5)
    return jnp.exp(sc - jnp.max(sc, axis=1, keepdims=True)).astype(BF16)


def _mem_fwd(proj, mkv):
    s = proj.shape[0]
    ts = 512
    mlen = mkv.shape[0]

    def body(m_ref, kv_ref, o_ref, y_ref):
        ones = jnp.ones((mlen, LANES), BF16)
        for h in range(MEM_HEADS):
            cols = slice(h * LANES, (h + 1) * LANES)
            mk = kv_ref[:, cols].astype(BF16)
            mv = kv_ref[:, MEM_WIDTH + h * LANES:MEM_WIDTH + (h + 1) * LANES].astype(BF16)
            e = _mem_exp(m_ref[:, cols].astype(BF16), mk)
            ox = jnp.dot(e, jnp.concatenate([mv, ones], axis=1), preferred_element_type=F32)
            oh = ox[:, :LANES] * (1.0 / ox[:, LANES:])
            o_ref[:, cols] = oh
            zh = m_ref[:, MEM_WIDTH + h * LANES:MEM_WIDTH + (h + 1) * LANES]
            y_ref[:, cols] = (oh * (zh * _sigmoid(zh))).astype(BF16)

    o_spec = pl.BlockSpec((ts, MEM_WIDTH), lambda i: (i, 0))
    return pl.pallas_call(
        body, grid=(s // ts,),
        in_specs=[pl.BlockSpec((ts, W_M), lambda i: (i, COL_M)),
                  pl.BlockSpec((mlen, 2 * MEM_WIDTH), lambda i: (0, 0))],
        out_specs=[o_spec, o_spec],
        out_shape=[jax.ShapeDtypeStruct((s, MEM_WIDTH), F32), jax.ShapeDtypeStruct((s, MEM_WIDTH), BF16)],
        name="mem_fwd", compiler_params=_params())(proj, mkv)


def _mem_bwd(proj, mkv, o_mem, dym, dproj, *, carry=None):
    s = proj.shape[0]
    ts = 512
    mlen = mkv.shape[0]
    scale = MEM_HEAD_DIM ** -0.5

    def body(m_ref, kv_ref, o_ref, dy_ref, _, dp_ref, dkv_ref):
        @pl.when(pl.program_id(0) == 0)
        def _():
            dkv_ref[...] = jnp.zeros_like(dkv_ref)

        ones = jnp.ones((mlen, LANES), BF16)
        for h in range(MEM_HEADS):
            cols = slice(h * LANES, (h + 1) * LANES)
            vcols = slice(MEM_WIDTH + h * LANES, MEM_WIDTH + (h + 1) * LANES)
            mk = kv_ref[:, cols].astype(BF16)
            mv = kv_ref[:, vcols].astype(BF16)
            qh = m_ref[:, cols].astype(BF16)
            zh = m_ref[:, vcols]
            sg = _sigmoid(zh)
            oh = o_ref[:, cols]
            dyh = dy_ref[:, cols]
            doh = dyh * (zh * sg)
            dp_ref[:, vcols] = (dyh * oh * (sg * (1.0 + zh * (1.0 - sg)))).astype(BF16)
            e = _mem_exp(qh, mk)
            inv = 1.0 / jnp.dot(e, ones, preferred_element_type=F32)
            prob = e.astype(F32) * jnp.concatenate([inv] * (mlen // LANES), axis=1)
            delta = jnp.sum(doh * oh, axis=1, keepdims=True)
            dohb = doh.astype(BF16)
            dprob = lax.dot_general(dohb, mv, _DIMS["nt"], preferred_element_type=F32)
            dsc = (prob * (dprob - delta)).astype(BF16)
            dp_ref[:, cols] = (jnp.dot(dsc, mk, preferred_element_type=F32) * scale).astype(BF16)
            dkv_ref[:, cols] += lax.dot_general(dsc, qh, _DIMS["tn"], preferred_element_type=F32) * scale
            dkv_ref[:, vcols] += lax.dot_general(prob.astype(BF16), dohb, _DIMS["tn"],
                                                 preferred_element_type=F32)

    blk = pl.BlockSpec((ts, MEM_WIDTH), lambda i: (i, 0))
    msp = pl.BlockSpec((ts, W_M), lambda i: (i, COL_M))
    kvsp = pl.BlockSpec((mlen, 2 * MEM_WIDTH), lambda i: (0, 0))
    outs, carried = _carried_call(
        lambda ins, outs, scr: body(*ins, *outs), carry, grid=(s // ts,),
        in_specs=[msp, kvsp, blk, blk, pl.BlockSpec(memory_space=pl.ANY)],
        out_specs=[msp, kvsp],
        out_shape=[jax.ShapeDtypeStruct(dproj.shape, BF16), jax.ShapeDtypeStruct(mkv.shape, F32)],
        scratch=[], operands=(proj, mkv, o_mem, dym, dproj), name="mem_bwd", aliases={4: 0})
    return (*outs, carried) if carry else tuple(outs)


def _forward_backward(x, mem, tgt, proj, w_conv, sink, g_mem, late_weights, g_post, early_exchange, kv_exchange):
    s = x.shape[0]
    tab = _rope_tables(s)
    bias = _window_bias()

    ya = _conv_fwd(proj, w_conv)
    kpad, vpad = _rope_kv(proj, tab)
    o_attn, yb, *arrived = _swa_fwd(proj, kpad, vpad, tab, bias, sink, carry=late_weights[0])
    w_kv, w_up, w_out = late_weights[1](arrived[0] if arrived else None)
    mn = _rmsnorm_fwd(mem, g_mem, name="mem_norm")
    mkv = _matmul(mn, w_kv, mode="nn", out_dtype=F32, tm=256, tn=1024, tk=D_MODEL, name="mem_kv")
    o_mem, ym = _mem_fwd(proj, mkv)
    merged, d_out, dy, dg_post, loss = _mid_fwd(ya, yb, ym, proj, x, tgt, w_up, w_out, g_post)
    dproj, d_ya, d_yb, d_ym, dw_up, dw_out = _mid_bwd(d_out, merged, ya, yb, ym, proj, w_up, w_out)

    dproj, dw_conv = _conv_bwd(proj, w_conv, d_ya, dproj)
    dproj, dkpad, dvpad, dsink = _swa_bwd(proj, kpad, vpad, tab, bias, sink, o_attn, d_yb, dproj)
    dproj = _rope_kv_bwd(dkpad, dvpad, tab, dproj)
    dproj, d_mkv, *early = _mem_bwd(proj, mkv, o_mem, d_ym, dproj, carry=early_exchange(dw_up, dw_out))

    dw_kv = _matmul(mn, d_mkv, mode="tn", out_dtype=F32, tm=1024, tn=1024, tk=256, name="dw_kv")
    d_mn = _matmul(d_mkv, w_kv, mode="nt", out_dtype=F32, tm=256, tn=1024, tk=D_MODEL, name="d_mn")
    _, dg_mem, *early_kv = _rmsnorm_bwd(d_mn, mem, g_mem, d_mn, name="mem_norm_bwd", carry=kv_exchange(dw_kv))

    return dict(loss=loss, dproj=dproj, dy=dy, w_conv=dw_conv, sink=dsink, g_mem=dg_mem,
                w_kv=dw_kv, w_up=dw_up, w_out=dw_out, g_post=dg_post, early=early[0] if early else None,
                early_kv=early_kv[0] if early_kv else None)


N_DEV = 8


def _position():
    return lax.axis_index("x"), lax.axis_index("y"), lax.axis_index("c")


def _other_chips(x, y):
    return (((1 - x, y), 2 * (1 - x) + y), ((x, 1 - y), 2 * x + (1 - y)), ((1 - x, 1 - y), 2 * (1 - x) + (1 - y)))


def _remote(src, dst, send_sems, recv_sems, k, device):
    return pltpu.make_async_remote_copy(src_ref=src, dst_ref=dst, send_sem=send_sems.at[k], recv_sem=recv_sems.at[k],
                                        device_id=device, device_id_type=MESH)


def _rows_half(ref, hf):
    rh = ref.shape[0] // 2
    return ref.at[pl.ds(pl.multiple_of(hf * rh, 8), rh)]


def _gather_weights(shards, small=None, relations=(0, 1, 2), into=None):
    n = len(shards)
    k = 0 if small is None else 1

    def peers(x, y):
        return [(r, chip, idx) for r, (chip, idx) in enumerate(_other_chips(x, y)) if r in relations]

    def ici(ins, outs, sems, a, r, chip, src_chip, c):
        return _remote(_rows_half(ins[a], c), _rows_half(outs[a].at[src_chip], c), sems[0], sems[1], 3 * a + r,
                       (*chip, c))

    def whole(ins, outs, sems, r, chip, src_chip, c):
        return _remote(ins[n], outs[n].at[src_chip], sems[0], sems[1], 3 * n + r, (*chip, c))

    def d2d(outs, sems, a, r, idx, hf, x, y, c):
        half = _rows_half(outs[a].at[idx], hf)
        return _remote(half, half, sems[2], sems[3], 3 * a + r, (x, y, 1 - c))

    def start(ins, outs, sems):
        x, y, c = _position()
        me = 2 * x + y
        for a in range(n):
            for r, chip, _ in peers(x, y):
                ici(ins, outs, sems, a, r, chip, me, c).start()
        for r, (chip, _) in enumerate(_other_chips(x, y)):
            if k:
                whole(ins, outs, sems, r, chip, me, c).start()

    def finish(ins, outs, sems):
        x, y, c = _position()
        me = 2 * x + y
        for a in range(n):
            for r, chip, idx in peers(x, y):
                ici(ins, outs, sems, a, r, chip, idx, c).wait_recv()
                d2d(outs, sems, a, r, idx, c, x, y, c).start()
        for a in range(n):
            for r, chip, idx in peers(x, y):
                d2d(outs, sems, a, r, idx, 1 - c, x, y, c).wait_recv()
        for r, (chip, idx) in enumerate(_other_chips(x, y)):
            if k:
                whole(ins, outs, sems, r, chip, idx, c).wait_recv()
                whole(ins, outs, sems, r, chip, me, c).wait_send()
        for a in range(n):
            for r, chip, idx in peers(x, y):
                ici(ins, outs, sems, a, r, chip, me, c).wait_send()
                d2d(outs, sems, a, r, idx, c, x, y, c).wait_send()

    operands = list(shards) + ([small] if k else [])
    shapes = [jax.ShapeDtypeStruct((N_CHIPS,) + s.shape, s.dtype) for s in operands]
    aliases = {}
    if into is not None:
        assert len(into) == len(operands)
        aliases = {len(operands) + a: a for a in range(len(into))}
        operands += list(into)
    return _Carry(operands, shapes,
                  [pltpu.SemaphoreType.DMA((3 * (n + k),)), pltpu.SemaphoreType.DMA((3 * (n + k),)),
                   pltpu.SemaphoreType.DMA((3 * n,)), pltpu.SemaphoreType.DMA((3 * n,))], start, finish, aliases)


def _pair_exchange(send):
    n = len(send)

    def copies(ins, outs, sems):
        x, y, c = _position()
        return [_remote(ins[a], outs[a], sems[0], sems[1], a, (x, y, 1 - c)) for a in range(n)]

    def start(ins, outs, sems):
        for cp in copies(ins, outs, sems):
            cp.start()

    def finish(ins, outs, sems):
        for cp in copies(ins, outs, sems):
            cp.wait()

    return _Carry(send, [jax.ShapeDtypeStruct(p.shape, p.dtype) for p in send],
                  [pltpu.SemaphoreType.DMA((n,)), pltpu.SemaphoreType.DMA((n,))], start, finish)


def _chip_exchange(sums):
    n = len(sums)

    def copies(ins, outs, sems):
        x, y, c = _position()
        return [_remote(ins[a].at[idx], outs[a].at[r], sems[0], sems[1], 3 * a + r, (*chip, c))
                for a in range(n) for r, (chip, idx) in enumerate(_other_chips(x, y))]

    def start(ins, outs, sems):
        for cp in copies(ins, outs, sems):
            cp.start()

    def finish(ins, outs, sems):
        for cp in copies(ins, outs, sems):
            cp.wait()

    return _Carry(sums, [jax.ShapeDtypeStruct((3,) + p.shape[1:], p.dtype) for p in sums],
                  [pltpu.SemaphoreType.DMA((3 * n,)), pltpu.SemaphoreType.DMA((3 * n,))], start, finish)


def _pair_share(pairs):
    n = len(pairs)

    def start(ins, outs, sems):
        x, y, c = _position()
        for a in range(n):
            _remote(outs[a].at[c], outs[a].at[c], sems[0], sems[1], a, (x, y, 1 - c)).start()

    def finish(ins, outs, sems):
        x, y, c = _position()
        for a in range(n):
            _remote(outs[a].at[1 - c], outs[a].at[1 - c], sems[0], sems[1], a, (x, y, 1 - c)).wait_recv()
        for a in range(n):
            _remote(outs[a].at[c], outs[a].at[c], sems[0], sems[1], a, (x, y, 1 - c)).wait_send()

    return _Carry(pairs, [jax.ShapeDtypeStruct(p.shape, p.dtype) for p in pairs],
                  [pltpu.SemaphoreType.DMA((n,)), pltpu.SemaphoreType.DMA((n,))], start, finish,
                  aliases={a: a for a in range(n)})


def _small_allreduce(pack, share):
    rows, width = pack.shape
    n_share = len(share.ins)

    def body(p_ref, *refs):
        share_in, o_ref, share_out = refs[:n_share], refs[n_share], refs[n_share + 1:2 * n_share + 1]
        buf, send_sems, recv_sems = refs[2 * n_share + 1:2 * n_share + 4]
        share_sems = refs[2 * n_share + 4:]
        share.start(share_in, share_out, share_sems)
        x, y, c = _position()
        me = 4 * x + 2 * y + c
        buf[me] = p_ref[...]
        peers = []
        for r in range(1, N_DEV):
            fx, fy, fc = (r >> 2) & 1, (r >> 1) & 1, r & 1
            px, py, pc = (1 - x if fx else x), (1 - y if fy else y), (1 - c if fc else c)
            peers.append(((px, py, pc), 4 * px + 2 * py + pc))
        sends = [_remote(p_ref, buf.at[me], send_sems, recv_sems, r, dev) for r, (dev, _) in enumerate(peers)]
        for cp in sends:
            cp.start()
        for r, (dev, idx) in enumerate(peers):
            _remote(p_ref, buf.at[idx], send_sems, recv_sems, r, dev).wait_recv()
        for cp in sends:
            cp.wait_send()
        acc = buf[0]
        for k in range(1, N_DEV):
            acc = acc + buf[k]
        o_ref[...] = acc
        share.finish(share_in, share_out, share_sems)

    vm = pl.BlockSpec(memory_space=pltpu.VMEM)
    red, *shared = pl.pallas_call(
        body, in_specs=[vm] + [_HBM] * n_share, out_specs=[vm] + [_HBM] * n_share,
        out_shape=[jax.ShapeDtypeStruct(pack.shape, F32)] + share.out_shapes,
        scratch_shapes=[pltpu.VMEM((N_DEV, rows, width), F32), pltpu.SemaphoreType.DMA((N_DEV - 1,)),
                        pltpu.SemaphoreType.DMA((N_DEV - 1,))] + share.sems,
        input_output_aliases={1 + i: 1 + o for i, o in share.aliases.items()},
        name="small_allreduce")(pack, *share.ins)
    return red, shared


ROW_TILE_MAX = 512
SUM_TILE_MAX = 2048
BF16_SUBLANES = 16


def _row_tile(rows, most=ROW_TILE_MAX):
    if rows <= most:
        return rows
    return max(t for t in range(BF16_SUBLANES, most + 1, BF16_SUBLANES) if rows % t == 0)


def _pair_add(keep, recv, name):
    nj, rh, cols = keep.shape
    tr = _row_tile(rh, SUM_TILE_MAX)

    def body(k_ref, r_ref, o_ref):
        o_ref[...] = (k_ref[...].astype(F32) + r_ref[...].astype(F32)).astype(BF16)

    blk = pl.BlockSpec((None, tr, cols), lambda j, i: (j, i, 0))
    return pl.pallas_call(body, grid=(nj, rh // tr), in_specs=[blk, blk], out_specs=blk,
                          out_shape=jax.ShapeDtypeStruct(keep.shape, BF16), name=name,
                          compiler_params=_params())(keep, recv)


def _chip_add(sums, recv, where, name):
    _, rh, cols = sums.shape
    tr = _row_tile(rh, SUM_TILE_MAX)

    def body(w_ref, s_ref, r_ref, o_ref):
        o_ref[...] = ((s_ref[...].astype(F32) + r_ref[0].astype(F32)) + r_ref[1].astype(F32)) + r_ref[2].astype(F32)

    grid_spec = pltpu.PrefetchScalarGridSpec(
        num_scalar_prefetch=1, grid=(rh // tr,),
        in_specs=[pl.BlockSpec((None, tr, cols), lambda i, w_ref: (w_ref[0], i, 0)),
                  pl.BlockSpec((3, tr, cols), lambda i, w_ref: (0, i, 0))],
        out_specs=pl.BlockSpec((None, tr, cols), lambda i, w_ref: (w_ref[1], i, 0)))
    return pl.pallas_call(body, grid_spec=grid_spec, out_shape=jax.ShapeDtypeStruct((2, rh, cols), F32),
                          name=name, compiler_params=_params())(where, sums, recv)


def _adamw(w, g, m, v, name):
    rows, cols = w.shape
    tr = _row_tile(rows)
    assert rows % tr == 0

    def body(w_ref, g_ref, m_ref, v_ref, d_ref, mo_ref, vo_ref):
        gv = g_ref[...]
        m_new = ADAM_B1 * m_ref[...] + (1.0 - ADAM_B1) * gv
        v_new = ADAM_B2 * v_ref[...] + (1.0 - ADAM_B2) * jnp.square(gv)
        m_hat = m_new / (1.0 - ADAM_B1 ** ADAM_STEP)
        v_hat = v_new / (1.0 - ADAM_B2 ** ADAM_STEP)
        d_ref[...] = -ADAM_LR * (m_hat / (jnp.sqrt(v_hat) + ADAM_EPS) + ADAM_WD * w_ref[...])
        mo_ref[...] = m_new
        vo_ref[...] = v_new

    blk = pl.BlockSpec((tr, cols), lambda i: (i, 0))
    shp = jax.ShapeDtypeStruct((rows, cols), F32)
    return pl.pallas_call(body, grid=(rows // tr,), in_specs=[blk] * 4, out_specs=[blk] * 3,
                          out_shape=[shp] * 3, name=name, compiler_params=_params())(w, g, m, v)


def _adamw_halves(w, g2, m, v, name):
    rows, cols = w.shape
    half = cols // 2
    tr = _row_tile(rows)

    def body(w_ref, g_ref, m_ref, v_ref, go_ref, d_ref, mo_ref, vo_ref):
        gv = g_ref[...]
        go_ref[...] = gv
        m_new = ADAM_B1 * m_ref[...] + (1.0 - ADAM_B1) * gv
        v_new = ADAM_B2 * v_ref[...] + (1.0 - ADAM_B2) * jnp.square(gv)
        m_hat = m_new / (1.0 - ADAM_B1 ** ADAM_STEP)
        v_hat = v_new / (1.0 - ADAM_B2 ** ADAM_STEP)
        d_ref[...] = -ADAM_LR * (m_hat / (jnp.sqrt(v_hat) + ADAM_EPS) + ADAM_WD * w_ref[...])
        mo_ref[...] = m_new
        vo_ref[...] = v_new

    blk = pl.BlockSpec((tr, half), lambda hf, i: (i, hf))
    gsp = pl.BlockSpec((None, tr, half), lambda hf, i: (hf, i, 0))
    shp = jax.ShapeDtypeStruct((rows, cols), F32)
    return pl.pallas_call(body, grid=(2, rows // tr), in_specs=[blk, gsp, blk, blk], out_specs=[blk] * 4,
                          out_shape=[shp] * 4, name=name, compiler_params=_params())(w, g2, m, v)


SHARD_W = IN_WIDTH // N_CHIPS


def _half_major(a):
    r, c = a.shape
    return a.reshape(N_CHIPS, 2, r // N_CHIPS // 2, c).transpose(1, 0, 2, 3)


def kernel(x, mem, g_pre, w_in, w_conv, attn_sink, g_mem, w_mem_kv, w_up_a, w_up_b, w_up_m, w_out, g_post, loss_target, m_g_pre, m_w_in, m_w_conv, m_attn_sink, m_g_mem, m_w_mem_kv, m_w_up_a, m_w_up_b, m_w_up_m, m_w_out, m_g_post, v_g_pre, v_w_in, v_w_conv, v_attn_sink, v_g_mem, v_w_mem_kv, v_w_up_a, v_w_up_b, v_w_up_m, v_w_out, v_g_post):
    xi, yi, ci = _position()
    chip = 2 * xi + yi
    where = jnp.stack([chip, ci, N_CHIPS - 1 - chip]).astype(jnp.int32)

    own = [w_in[0].T.astype(BF16), w_mem_kv[0].astype(BF16),
           jnp.concatenate([w_up_a[0], w_up_b[0], w_up_m[0]], axis=0).astype(BF16), w_out[0].astype(BF16)]
    own_conv = jnp.pad(w_conv[0], ((0, 5), (0, 0)))

    def pieces(mine, got):
        got = lax.dynamic_update_slice_in_dim(got, mine[None], chip, axis=0)
        return [got[j] for j in range(N_CHIPS)]

    diag = N_CHIPS - 1 - chip
    h, h_t = _rmsnorm_fwd(x[0], g_pre, name="pre_norm", transposed=True)
    proj, got_near, got_conv, got_far = _proj_near(h, own[0], own_conv, where)
    w_near = lax.dynamic_update_slice_in_dim(got_near, own[0][None], chip, axis=0).reshape(IN_WIDTH, D_MODEL)
    far = lax.dynamic_index_in_dim(got_far, diag, 0, keepdims=False)
    proj = _proj_far(h, w_near, far, where, into=proj)
    w_conv_full = jnp.concatenate([p[:3] for p in pieces(own_conv, got_conv)], axis=1)

    def late_weights(gathered):
        w_kv_full = jnp.concatenate(pieces(own[1], gathered[0]), axis=0)
        up_pieces = pieces(own[2], gathered[1])
        w_up_full = jnp.stack([jnp.concatenate([p[k * A_WIDTH:(k + 1) * A_WIDTH] for p in up_pieces], axis=1)
                               for k in range(3)])
        return w_kv_full, w_up_full, jnp.concatenate(pieces(own[3], gathered[2]), axis=0)

    def pick(parts, hf):
        return [lax.dynamic_index_in_dim(p, hf, 0, keepdims=False) for p in parts]

    def up_out_parts(dw_up, dw_out):
        up = (dw_up.reshape(3, A_WIDTH, N_CHIPS, D_MODEL // N_CHIPS).transpose(2, 0, 1, 3)
              .reshape(N_CHIPS, 2, 3 * A_WIDTH // 2, D_MODEL // N_CHIPS).transpose(1, 0, 2, 3))
        return [up.astype(BF16), _half_major(dw_out).astype(BF16)]

    g = _forward_backward(x[0], mem[0], loss_target[0], proj, w_conv_full, attn_sink, g_mem,
                          (_gather_weights(own[1:]), late_weights), g_post,
                          lambda dw_up, dw_out: _pair_exchange(pick(up_out_parts(dw_up, dw_out), 1 - ci)),
                          lambda dw_kv: _pair_exchange(pick([_half_major(dw_kv).astype(BF16)], 1 - ci)))

    half_rows = D_MODEL // 2

    def dw_in_half(half_of, name, carry):
        dw, carried = _dw_in_t(g["dproj"], h_t, half_of=half_of, where=where, name=name, carry=carry)
        return dw.reshape(N_CHIPS, SHARD_W, half_rows), carried

    small_keep = pick([_half_major(g["w_kv"]).astype(BF16)] + up_out_parts(g["w_up"], g["w_out"]), ci)
    small_names = ["w_kv", "w_up", "w_out"]
    sums_small = [_pair_add(k, r, "pair_add_" + nm)
                  for k, r, nm in zip(small_keep, g["early_kv"] + g["early"], small_names)]
    dw_send, recv3_small = dw_in_half(lambda w: 1 - w[1], "dw_in_send", _chip_exchange(sums_small))
    dw_keep, (recv_in,) = dw_in_half(lambda w: w[1], "dw_in_keep", _pair_exchange([dw_send]))
    sum_in = _pair_add(dw_keep, recv_in, "pair_add_w_in")
    d_h, (recv3_in,) = _d_h(g["dproj"], w_near, far, where, carry=_chip_exchange([sum_in]))
    pairs = [_chip_add(s, r, where, "chip_add_" + nm)
             for s, r, nm in zip([sum_in] + sums_small, [recv3_in] + recv3_small, ["w_in"] + small_names)]
    grad_x, dg_pre = _rmsnorm_bwd(d_h, x[0], g_pre, g["dy"], name="pre_norm_bwd")

    zeros512 = jnp.zeros((1, D_MODEL - A_WIDTH), F32)
    conv_rows = [jnp.concatenate([g["w_conv"][k:k + 1], zeros512], axis=1) for k in range(3)]
    sink_row = jnp.pad(g["sink"][:, 0].reshape(1, N_Q_HEADS), ((0, 0), (0, D_MODEL - N_Q_HEADS)))
    loss_row = jnp.pad(g["loss"], ((0, 0), (0, D_MODEL - LANES)))
    pack = jnp.concatenate([dg_pre, g["g_mem"], g["g_post"]] + conv_rows + [sink_row, loss_row], axis=0)
    red, full = _small_allreduce(pack, _pair_share(pairs))
    loss = red[7, 0]
    small_grads = dict(
        g_pre=red[0:1], g_mem=red[1:2], g_post=red[2:3], attn_sink=red[6:7, :N_Q_HEADS],
        w_conv=lax.dynamic_slice(red[3:6, :A_WIDTH], (0, chip * LANES), (3, LANES)))

    gw_up = full[2].reshape(3, A_WIDTH, D_MODEL // N_CHIPS)
    grads = dict(small_grads, w_mem_kv=full[1].reshape(D_MODEL // N_CHIPS, 2 * MEM_WIDTH),
                 w_up_a=gw_up[0], w_up_b=gw_up[1], w_up_m=gw_up[2],
                 w_out=full[3].reshape(D_MODEL // N_CHIPS, D_MODEL))

    weights = dict(g_pre=g_pre, w_in=w_in, w_conv=w_conv, attn_sink=attn_sink, g_mem=g_mem, w_mem_kv=w_mem_kv,
                   w_up_a=w_up_a, w_up_b=w_up_b, w_up_m=w_up_m, w_out=w_out, g_post=g_post)
    m_in = dict(g_pre=m_g_pre, w_in=m_w_in, w_conv=m_w_conv, attn_sink=m_attn_sink, g_mem=m_g_mem,
                w_mem_kv=m_w_mem_kv, w_up_a=m_w_up_a, w_up_b=m_w_up_b, w_up_m=m_w_up_m, w_out=m_w_out,
                g_post=m_g_post)
    v_in = dict(g_pre=v_g_pre, w_in=v_w_in, w_conv=v_w_conv, attn_sink=v_attn_sink, g_mem=v_g_mem,
                w_mem_kv=v_w_mem_kv, w_up_a=v_w_up_a, w_up_b=v_w_up_b, w_up_m=v_w_up_m, w_out=v_w_out,
                g_post=v_g_post)
    out_g, out_d, out_m, out_v = [], [], [], []
    for nm in ("g_pre", "w_in", "w_conv", "attn_sink", "g_mem", "w_mem_kv", "w_up_a", "w_up_b", "w_up_m", "w_out",
               "g_post"):
        shape = weights[nm].shape
        if nm == "w_in":
            results = _adamw_halves(w_in[0].T, full[0], m_w_in[0].T, v_w_in[0].T, "adamw_w_in")
            for out, t in zip((out_g, out_d, out_m, out_v), results):
                out.append(t.T.reshape(shape))
            continue
        two_d = shape[-2:]
        gr = grads[nm].reshape(two_d)
        d, m_new, v_new = _adamw(weights[nm].reshape(two_d), gr, m_in[nm].reshape(two_d), v_in[nm].reshape(two_d),
                                 "adamw_" + nm)
        out_g.append(gr.reshape(shape))
        out_d.append(d.reshape(shape))
        out_m.append(m_new.reshape(shape))
        out_v.append(v_new.reshape(shape))
    return (loss, grad_x.reshape(x.shape), *out_g, *out_d, *out_m, *out_v)
```

```python
import jax
import jax.numpy as jnp
from jax import lax
from jax.experimental import pallas as pl
from jax.experimental.pallas import tpu as pltpu

F32 = jnp.float32
BF16 = jnp.bfloat16
MESH = pl.DeviceIdType.MESH

D_MODEL = 1024
EPS = 1e-6
A_WIDTH = 512
HEAD_DIM = 64
N_Q_HEADS = 8
WINDOW_BLOCK = 128
KV_PAD = 512
ROPE_THETA = 500000.0
ROT_DIM = 16
MEM_HEADS = 4
MEM_HEAD_DIM = 128
MEM_WIDTH = 512
IN_WIDTH = 7424
N_CHIPS = 4
LANES = 128
HALF_LANES = 64

PERM_SEGS = ((0, 2560), (2816, 3328), (4352, 7424), (3328, 4352), (2560, 2816))
COL_A, W_A = 0, 2048
COL_B, W_B = 2, 1024
COL_G, W_G = 1, 3072
COL_M, W_M = 6, 1024
COL_KV, W_KV = 28, 256

ADAM_LR = 0.001
ADAM_B1 = 0.9
ADAM_B2 = 0.999
ADAM_EPS = 1e-08
ADAM_WD = 0.01
ADAM_STEP = 10

VMEM_LIGHT_BYTES = 48 * 1024 * 1024
VMEM_HEAVY_BYTES = 48 * 1024 * 1024


_HBM = pl.BlockSpec(memory_space=pltpu.HBM)


def _params(heavy=False):
    return pltpu.CompilerParams(vmem_limit_bytes=VMEM_HEAVY_BYTES if heavy else VMEM_LIGHT_BYTES)


def _sigmoid(v):
    return jax.nn.sigmoid(v)


_DIMS = {"nn": (((1,), (0,)), ((), ())), "nt": (((1,), (1,)), ((), ())), "tn": (((0,), (0,)), ((), ()))}


class _Carry:
    def __init__(self, ins, out_shapes, sems, start, finish, aliases=None):
        self.ins, self.out_shapes, self.sems = list(ins), list(out_shapes), list(sems)
        self.start, self.finish, self.aliases = start, finish, dict(aliases or {})


def _join(*carries):
    def split(seq, counts):
        pos, parts = 0, []
        for n in counts:
            parts.append(seq[pos:pos + n])
            pos += n
        return parts

    n_in = [len(c.ins) for c in carries]
    n_out = [len(c.out_shapes) for c in carries]
    n_sem = [len(c.sems) for c in carries]

    def run(which):
        def go(ins, outs, sems):
            for c, i, o, sm in zip(carries, split(ins, n_in), split(outs, n_out), split(sems, n_sem)):
                getattr(c, which)(i, o, sm)
        return go

    aliases = {}
    for k, c in enumerate(carries):
        aliases.update({sum(n_in[:k]) + i: sum(n_out[:k]) + o for i, o in c.aliases.items()})
    return _Carry([a for c in carries for a in c.ins], [sh for c in carries for sh in c.out_shapes],
                  [sm for c in carries for sm in c.sems], run("start"), run("finish"), aliases)


def _carried_call(body, carry, *, grid, in_specs, out_specs, out_shape, scratch, operands, name, prefetch=None,
                  aliases=None, heavy=False):
    n_in, n_out, n_scr = len(in_specs), len(out_specs), len(scratch)
    c_in = len(carry.ins) if carry else 0
    c_out = len(carry.out_shapes) if carry else 0
    n_pre = 0 if prefetch is None else 1
    steps = 1
    for g in grid:
        steps *= g

    def wrapped(*refs):
        refs = refs[n_pre:]
        ins, cins = refs[:n_in], refs[n_in:n_in + c_in]
        outs = refs[n_in + c_in:n_in + c_in + n_out]
        couts = refs[n_in + c_in + n_out:n_in + c_in + n_out + c_out]
        rest = refs[n_in + c_in + n_out + c_out:]
        scr, sems = rest[:n_scr], rest[n_scr:]
        if carry:
            step = pl.program_id(0)
            for ax in range(1, len(grid)):
                step = step * grid[ax] + pl.program_id(ax)

            @pl.when(step == 0)
            def _():
                carry.start(cins, couts, sems)

        body(ins, outs, scr)
        if carry:
            @pl.when(step == steps - 1)
            def _():
                carry.finish(cins, couts, sems)

    all_aliases = {n_pre + i: o for i, o in (aliases or {}).items()}
    if carry:
        all_aliases.update({n_pre + n_in + i: n_out + o for i, o in carry.aliases.items()})
    all_in = list(in_specs) + [_HBM] * c_in
    all_out = list(out_specs) + [_HBM] * c_out
    all_scratch = list(scratch) + (carry.sems if carry else [])
    if n_pre:
        spec = dict(grid_spec=pltpu.PrefetchScalarGridSpec(num_scalar_prefetch=1, grid=grid, in_specs=all_in,
                                                           out_specs=all_out, scratch_shapes=all_scratch))
        pre = (prefetch,)
    else:
        spec = dict(grid=grid, in_specs=all_in, out_specs=all_out, scratch_shapes=all_scratch)
        pre = ()
    results = pl.pallas_call(
        wrapped, out_shape=list(out_shape) + (carry.out_shapes if carry else []), input_output_aliases=all_aliases,
        name=name, compiler_params=_params(heavy), **spec)(*pre, *operands, *(carry.ins if carry else []))
    return list(results[:n_out]), list(results[n_out:])


def _matmul(a, b, *, mode, out_dtype, tm, tn, tk, name):
    if mode == "nn":
        (m, k), (_, n) = a.shape, b.shape
    elif mode == "nt":
        (m, k), (n, _) = a.shape, b.shape
    else:
        (k, m), (_, n) = a.shape, b.shape
    tm, tn, tk = min(tm, m), min(tn, n), min(tk, k)
    assert m % tm == 0 and n % tn == 0 and k % tk == 0
    nk = k // tk
    dims = _DIMS[mode]

    if mode == "nn":
        a_spec = pl.BlockSpec((tm, tk), lambda i, j, kk: (i, kk))
        b_spec = pl.BlockSpec((tk, tn), lambda i, j, kk: (kk, j))
    elif mode == "nt":
        a_spec = pl.BlockSpec((tm, tk), lambda i, j, kk: (i, kk))
        b_spec = pl.BlockSpec((tn, tk), lambda i, j, kk: (j, kk))
    else:
        a_spec = pl.BlockSpec((tk, tm), lambda i, j, kk: (kk, i))
        b_spec = pl.BlockSpec((tk, tn), lambda i, j, kk: (kk, j))
    o_spec = pl.BlockSpec((tm, tn), lambda i, j, kk: (i, j))

    def part(a_ref, b_ref):
        return lax.dot_general(a_ref[...].astype(BF16), b_ref[...].astype(BF16), dims,
                               preferred_element_type=F32)

    if nk == 1:
        def body(a_ref, b_ref, o_ref):
            o_ref[...] = part(a_ref, b_ref).astype(out_dtype)
        scratch = []
    else:
        def body(a_ref, b_ref, o_ref, acc_ref):
            kk = pl.program_id(2)

            @pl.when(kk == 0)
            def _():
                acc_ref[...] = part(a_ref, b_ref)

            @pl.when(kk > 0)
            def _():
                acc_ref[...] += part(a_ref, b_ref)

            @pl.when(kk == nk - 1)
            def _():
                o_ref[...] = acc_ref[...].astype(out_dtype)
        scratch = [pltpu.VMEM((tm, tn), F32)]

    return pl.pallas_call(
        body, grid=(m // tm, n // tn, nk), in_specs=[a_spec, b_spec], out_specs=o_spec,
        out_shape=jax.ShapeDtypeStruct((m, n), out_dtype), scratch_shapes=scratch,
        name=name, compiler_params=_params())(a, b)


IN_BLOCK = 256
N_IN_BLOCKS = IN_WIDTH // IN_BLOCK
SHARD_BLOCKS = (IN_WIDTH // N_CHIPS) // IN_BLOCK
BLOCK_RUNS = tuple((a // IN_BLOCK, sum(d - c for c, d in PERM_SEGS[:k]) // IN_BLOCK, (b - a) // IN_BLOCK)
                   for k, (a, b) in enumerate(PERM_SEGS))


def _perm_block(r):
    p = r
    for ref0, perm0, n in BLOCK_RUNS:
        p = jnp.where((r >= ref0) & (r < ref0 + n), r - ref0 + perm0, p)
    return p


def _proj_near(x, g_pre, own_w, small, where):
    s, d = x.shape
    norm_tile = min(512, s)
    n_own = SHARD_BLOCKS - 1
    n_diag = SHARD_BLOCKS + 1
    n_blocks = N_IN_BLOCKS - n_diag
    piece = IN_WIDTH // N_CHIPS - SHARD_BLOCKS * IN_BLOCK
    near = _gather_weights([own_w], small, relations=(0, 1))
    far = _gather_weights([own_w], relations=(2,))
    both = _join(near, far)
    n_cin, n_cout = len(both.ins), len(both.out_shapes)

    def block_of(i, w):
        me, dg = w[0], w[2]
        own0 = SHARD_BLOCKS * me + jnp.minimum(me, 1)
        dg0 = SHARD_BLOCKS * dg
        lo0, hi0 = jnp.minimum(own0, dg0), jnp.maximum(own0, dg0)
        lo_n = jnp.where(own0 < dg0, n_own, n_diag)
        hi_n = jnp.where(own0 < dg0, n_diag, n_own)
        r = i - n_own
        r = r + lo_n * (r >= lo0).astype(jnp.int32)
        r = r + hi_n * (r >= hi0).astype(jnp.int32)
        return jnp.where(i < n_own, own0 + i, r)

    def body(w_ref, x_hbm, g_ref, own_hbm, *refs):
        cins, (o_ref, h_hbm, ht_hbm) = refs[:n_cin], refs[n_cin:n_cin + 3]
        couts = refs[n_cin + 3:n_cin + 3 + n_cout]
        blocks, block_sems, h_ref, x_tile, ht_tile, io_sem = refs[n_cin + 3 + n_cout:n_cin + 9 + n_cout]
        sems = refs[n_cin + 9 + n_cout:]
        near_refs = (cins[:len(near.ins)], couts[:len(near.out_shapes)], sems[:len(near.sems)])
        far_refs = (cins[len(near.ins):], couts[len(near.out_shapes):], sems[len(near.sems):])
        gathered = couts[0]
        i = pl.program_id(0)
        me = w_ref[0]

        def fetch(step, slot):
            r = block_of(step, w_ref)
            for p in range(IN_BLOCK // piece):
                row = r * IN_BLOCK + p * piece
                j = row // (IN_WIDTH // N_CHIPS)
                off = pl.multiple_of(row - j * (IN_WIDTH // N_CHIPS), BF16_SUBLANES)
                dst = blocks.at[slot, pl.ds(p * piece, piece)]

                @pl.when(j == me)
                def _():
                    pltpu.make_async_copy(own_hbm.at[pl.ds(off, piece)], dst, block_sems.at[slot]).start()

                @pl.when(j != me)
                def _():
                    pltpu.make_async_copy(gathered.at[j, pl.ds(off, piece)], dst, block_sems.at[slot]).start()

        def arrived(slot):
            pltpu.make_async_copy(own_hbm.at[pl.ds(0, IN_BLOCK)], blocks.at[slot], block_sems.at[slot]).wait()

        slot = i % 2

        def norm_rows(k):
            rows = pl.ds(k * norm_tile, norm_tile)
            pltpu.sync_copy(x_hbm.at[rows], x_tile)
            xv = x_tile[...]
            hv = (xv * lax.rsqrt(jnp.mean(xv * xv, axis=-1, keepdims=True) + EPS)) * g_ref[...]
            h_ref[rows, :] = hv.astype(BF16)
            ht_tile[...] = hv.T.astype(BF16)
            to_h = pltpu.make_async_copy(h_ref.at[rows], h_hbm.at[rows], io_sem.at[0])
            to_ht = pltpu.make_async_copy(ht_tile, ht_hbm.at[:, rows], io_sem.at[1])
            to_h.start()
            to_ht.start()
            to_h.wait()
            to_ht.wait()

        @pl.when(i == 0)
        def _():
            near.start(*near_refs)
            fetch(i, slot)
            for k in range(s // norm_tile):
                norm_rows(k)

        @pl.when(i == n_own)
        def _():
            near.finish(*near_refs)
            far.start(*far_refs)
            fetch(i, slot)

        arrived(slot)

        @pl.when((i + 1 < n_blocks) & (i + 1 != n_own))
        def _():
            fetch(i + 1, 1 - slot)

        o_ref[...] = lax.dot_general(h_ref[...], blocks[slot], _DIMS["nt"], preferred_element_type=F32)

        @pl.when(i == n_blocks - 1)
        def _():
            far.finish(*far_refs)

    anysp = pl.BlockSpec(memory_space=pl.ANY)
    grid_spec = pltpu.PrefetchScalarGridSpec(
        num_scalar_prefetch=1, grid=(n_blocks,),
        in_specs=[anysp, pl.BlockSpec((1, d), lambda i, w: (0, 0)), anysp] + [_HBM] * n_cin,
        out_specs=[pl.BlockSpec((s, IN_BLOCK), lambda i, w: (0, _perm_block(block_of(i, w)))), anysp, anysp]
        + [_HBM] * n_cout,
        scratch_shapes=[pltpu.VMEM((2, IN_BLOCK, d), BF16), pltpu.SemaphoreType.DMA((2,)), pltpu.VMEM((s, d), BF16),
                        pltpu.VMEM((norm_tile, d), F32), pltpu.VMEM((d, norm_tile), BF16),
                        pltpu.SemaphoreType.DMA((2,))] + both.sems)
    return pl.pallas_call(
        body, grid_spec=grid_spec,
        out_shape=[jax.ShapeDtypeStruct((s, IN_WIDTH), F32), jax.ShapeDtypeStruct((s, d), BF16),
                   jax.ShapeDtypeStruct((d, s), BF16)] + both.out_shapes,
        name="proj_near", compiler_params=_params())(where, x, g_pre, own_w, *both.ins)


def _proj_far(h, w_near, far, where, *, into, carry=None):
    s, d = h.shape
    n_blocks = SHARD_BLOCKS + 1
    lead = IN_WIDTH // N_CHIPS - SHARD_BLOCKS * IN_BLOCK

    def body(ins, outs, scr):
        where_ref, h_ref, w_hbm, far_hbm, _ = ins
        win, sem = scr
        i = pl.program_id(0)

        @pl.when(i == 0)
        def _():
            dg = where_ref[2]
            rows = pl.ds(pl.multiple_of(dg * (SHARD_BLOCKS * IN_BLOCK), IN_BLOCK), n_blocks * IN_BLOCK)
            window = pltpu.make_async_copy(w_hbm.at[rows], win, sem)
            window.start()
            window.wait()
            shard = pltpu.make_async_copy(far_hbm, win.at[pl.ds(pl.multiple_of(dg * lead, BF16_SUBLANES), SHARD_W)], sem)
            shard.start()
            shard.wait()

        blk = win[pl.ds(pl.multiple_of(i * IN_BLOCK, IN_BLOCK), IN_BLOCK), :]
        outs[0][...] = lax.dot_general(h_ref[...], blk, _DIMS["nt"], preferred_element_type=F32)

    anysp = pl.BlockSpec(memory_space=pl.ANY)
    (proj,), carried = _carried_call(
        body, carry, grid=(n_blocks,),
        in_specs=[pl.BlockSpec(memory_space=pltpu.SMEM), pl.BlockSpec((s, d), lambda i, w: (0, 0)), anysp, anysp, anysp],
        out_specs=[pl.BlockSpec((s, IN_BLOCK), lambda i, w: (0, _perm_block(i + SHARD_BLOCKS * w[2])))],
        out_shape=[jax.ShapeDtypeStruct((s, IN_WIDTH), F32)],
        scratch=[pltpu.VMEM((n_blocks * IN_BLOCK, d), BF16), pltpu.SemaphoreType.DMA],
        operands=(where, h, w_near, far, into), name="proj_far", prefetch=where, aliases={4: 0})
    return (proj, carried) if carry else proj


def _dw_in_t(dproj, h_t, *, half_of, where, name, carry=None):
    d, s = h_t.shape
    c = d // 2

    def body(ins, outs, scr):
        outs[0][...] = lax.dot_general(ins[1][...], ins[0][...], _DIMS["nn"], preferred_element_type=F32).T.astype(BF16)

    (dw,), carried = _carried_call(
        body, carry, grid=(N_IN_BLOCKS,),
        in_specs=[pl.BlockSpec((s, IN_BLOCK), lambda r, w: (0, _perm_block(r))),
                  pl.BlockSpec((c, s), lambda r, w: (half_of(w), 0))],
        out_specs=[pl.BlockSpec((IN_BLOCK, c), lambda r, w: (r, 0))],
        out_shape=[jax.ShapeDtypeStruct((IN_WIDTH, c), BF16)], scratch=[], operands=(dproj, h_t), name=name,
        prefetch=where)
    return (dw, carried) if carry else dw


def _d_h(dproj, w_near, far, where, *, carry=None):
    s = dproj.shape[0]
    d = w_near.shape[1]
    tm = min(s, 256)

    def body(ins, outs, scr):
        where_ref, a_ref, w_hbm, far_hbm = ins
        w_ref, sem = scr

        @pl.when(pl.program_id(0) == 0)
        def _():
            whole = pltpu.make_async_copy(w_hbm, w_ref, sem)
            whole.start()
            whole.wait()
            rows = pl.ds(pl.multiple_of(where_ref[2] * SHARD_W, BF16_SUBLANES), SHARD_W)
            part = pltpu.make_async_copy(far_hbm, w_ref.at[rows], sem)
            part.start()
            part.wait()

        acc = None
        for ref0, perm0, n in BLOCK_RUNS:
            term = jnp.dot(a_ref[:, perm0 * IN_BLOCK:(perm0 + n) * IN_BLOCK],
                           w_ref[ref0 * IN_BLOCK:(ref0 + n) * IN_BLOCK, :], preferred_element_type=F32)
            acc = term if acc is None else acc + term
        outs[0][...] = acc

    anysp = pl.BlockSpec(memory_space=pl.ANY)
    (dh,), carried = _carried_call(
        body, carry, grid=(s // tm,),
        in_specs=[pl.BlockSpec(memory_space=pltpu.SMEM), pl.BlockSpec((tm, IN_WIDTH), lambda i: (i, 0)), anysp, anysp],
        out_specs=[pl.BlockSpec((tm, d), lambda i: (i, 0))],
        out_shape=[jax.ShapeDtypeStruct((s, d), F32)],
        scratch=[pltpu.VMEM((IN_WIDTH, d), BF16), pltpu.SemaphoreType.DMA],
        operands=(where, dproj, w_near, far), name="d_h", heavy=True)
    return (dh, carried) if carry else dh


def _rmsnorm_fwd(x, g, *, name):
    s, d = x.shape
    ts = min(512, s)

    def body(x_ref, g_ref, o_ref):
        xv = x_ref[...]
        r = lax.rsqrt(jnp.mean(xv * xv, axis=-1, keepdims=True) + EPS)
        o_ref[...] = ((xv * r) * g_ref[...]).astype(BF16)

    return pl.pallas_call(
        body, grid=(s // ts,),
        in_specs=[pl.BlockSpec((ts, d), lambda i: (i, 0)), pl.BlockSpec((1, d), lambda i: (0, 0))],
        out_specs=pl.BlockSpec((ts, d), lambda i: (i, 0)),
        out_shape=jax.ShapeDtypeStruct((s, d), BF16), name=name, compiler_params=_params())(x, g)


def _rmsnorm_bwd(dh, x, g, res, *, name, carry=None):
    s, d = x.shape
    ts = min(256, s)

    def body(ins, outs, scr):
        dh_ref, x_ref, g_ref, res_ref = ins
        dx_ref, dg_ref = outs
        xv = x_ref[...]
        r = lax.rsqrt(jnp.mean(xv * xv, axis=-1, keepdims=True) + EPS)
        xh = xv * r
        dhv = dh_ref[...]
        part = jnp.sum(dhv * xh, axis=0, keepdims=True)

        @pl.when(pl.program_id(0) == 0)
        def _():
            dg_ref[...] = part

        @pl.when(pl.program_id(0) > 0)
        def _():
            dg_ref[...] += part

        dxh = dhv * g_ref[...]
        dx_ref[...] = res_ref[...] + r * (dxh - xh * jnp.mean(dxh * xh, axis=-1, keepdims=True))

    row = pl.BlockSpec((ts, d), lambda i: (i, 0))
    vec = pl.BlockSpec((1, d), lambda i: (0, 0))
    outs, carried = _carried_call(
        body, carry, grid=(s // ts,), in_specs=[row, row, vec, row], out_specs=[row, vec],
        out_shape=[jax.ShapeDtypeStruct((s, d), F32), jax.ShapeDtypeStruct((1, d), F32)],
        scratch=[], operands=(dh, x, g, res), name=name)
    return (*outs, carried) if carry else tuple(outs)


MID_TILE = 256


def _gated_branches(y_refs, wup_ref, gl):
    d = D_MODEL
    us = [jnp.dot(y_refs[k][...], wup_ref[k], preferred_element_type=F32) for k in range(3)]
    sg = [_sigmoid(gl[:, k * d:(k + 1) * d]) for k in range(3)]
    return us, sg


def _mid_fwd(ya, yb, ym, proj, x, tgt, w_up, w_out, g_post):
    s, d = x.shape
    ts = MID_TILE

    def body(ya_ref, yb_ref, ym_ref, g_ref, x_ref, t_ref, wup_ref, wout_ref, gp_ref,
             m_ref, do_ref, dy_ref, dg_ref, loss_ref):
        us, sg = _gated_branches((ya_ref, yb_ref, ym_ref), wup_ref, g_ref[...])
        merged = (sg[0] * us[0] + sg[1] * us[1] + sg[2] * us[2]).astype(BF16)
        m_ref[...] = merged
        ov = jnp.dot(merged, wout_ref[...], preferred_element_type=F32)
        r = lax.rsqrt(jnp.mean(ov * ov, axis=-1, keepdims=True) + EPS)
        nh = ov * r
        gv = gp_ref[...]
        e = (x_ref[...] + nh * gv) - t_ref[...]
        lpart = 0.5 * jnp.sum(jnp.mean(e * e, axis=-1, keepdims=True), axis=0, keepdims=True)
        dy = e * (1.0 / d)
        dgp = jnp.sum(dy * nh, axis=0, keepdims=True)

        @pl.when(pl.program_id(0) == 0)
        def _():
            dg_ref[...] = dgp
            loss_ref[...] = jnp.broadcast_to(lpart, loss_ref.shape)

        @pl.when(pl.program_id(0) > 0)
        def _():
            dg_ref[...] += dgp
            loss_ref[...] += jnp.broadcast_to(lpart, loss_ref.shape)

        dn = dy * gv
        dy_ref[...] = dy
        do_ref[...] = (r * (dn - nh * jnp.mean(dn * nh, axis=-1, keepdims=True))).astype(BF16)

    row = pl.BlockSpec((ts, d), lambda i: (i, 0))
    ysp = pl.BlockSpec((ts, A_WIDTH), lambda i: (i, 0))
    vec = pl.BlockSpec((1, d), lambda i: (0, 0))
    return pl.pallas_call(
        body, grid=(s // ts,),
        in_specs=[ysp, ysp, ysp, pl.BlockSpec((ts, W_G), lambda i: (i, COL_G)), row, row,
                  pl.BlockSpec((3, A_WIDTH, d), lambda i: (0, 0, 0)), pl.BlockSpec((d, d), lambda i: (0, 0)), vec],
        out_specs=[row, row, row, vec, pl.BlockSpec((1, LANES), lambda i: (0, 0))],
        out_shape=[jax.ShapeDtypeStruct((s, d), BF16), jax.ShapeDtypeStruct((s, d), BF16),
                   jax.ShapeDtypeStruct((s, d), F32), jax.ShapeDtypeStruct((1, d), F32),
                   jax.ShapeDtypeStruct((1, LANES), F32)],
        name="mid_fwd", compiler_params=_params(heavy=True))(ya, yb, ym, proj, x, tgt, w_up, w_out, g_post)


def _mid_bwd(d_out, merged, ya, yb, ym, proj, w_up, w_out):
    s, d = merged.shape
    ts = MID_TILE
    last = s // ts - 1

    def body(do_ref, m_ref, ya_ref, yb_ref, ym_ref, g_ref, wup_ref, wout_ref,
             dp_ref, dya_ref, dyb_ref, dym_ref, dwup_hbm, dwout_hbm, dwup_acc, dwout_acc):
        i = pl.program_id(0)

        @pl.when(i == 0)
        def _():
            dwup_acc[...] = jnp.zeros_like(dwup_acc)
            dwout_acc[...] = jnp.zeros_like(dwout_acc)

        y_refs = (ya_ref, yb_ref, ym_ref)
        us, sg = _gated_branches(y_refs, wup_ref, g_ref[...])
        dov = do_ref[...]
        dwout_acc[...] += lax.dot_general(m_ref[...], dov, _DIMS["tn"], preferred_element_type=F32)
        dm = lax.dot_general(dov, wout_ref[...], _DIMS["nt"], preferred_element_type=F32)
        for k, dy_ref in enumerate((dya_ref, dyb_ref, dym_ref)):
            dp_ref[:, k * d:(k + 1) * d] = ((dm * us[k]) * (sg[k] * (1.0 - sg[k]))).astype(BF16)
            du = (sg[k] * dm).astype(BF16)
            dy_ref[...] = lax.dot_general(du, wup_ref[k], _DIMS["nt"], preferred_element_type=F32)
            dwup_acc[k] += lax.dot_general(y_refs[k][...], du, _DIMS["tn"], preferred_element_type=F32)

        @pl.when(i == last)
        def _():
            pltpu.sync_copy(dwup_acc, dwup_hbm)
            pltpu.sync_copy(dwout_acc, dwout_hbm)

    row = pl.BlockSpec((ts, d), lambda i: (i, 0))
    ysp = pl.BlockSpec((ts, A_WIDTH), lambda i: (i, 0))
    gsp = pl.BlockSpec((ts, W_G), lambda i: (i, COL_G))
    anysp = pl.BlockSpec(memory_space=pl.ANY)
    yshape = jax.ShapeDtypeStruct((s, A_WIDTH), F32)
    return pl.pallas_call(
        body, grid=(s // ts,),
        in_specs=[row, row, ysp, ysp, ysp, gsp, pl.BlockSpec((3, A_WIDTH, d), lambda i: (0, 0, 0)),
                  pl.BlockSpec((d, d), lambda i: (0, 0))],
        out_specs=[gsp, ysp, ysp, ysp, anysp, anysp],
        out_shape=[jax.ShapeDtypeStruct((s, IN_WIDTH), BF16), yshape, yshape, yshape,
                   jax.ShapeDtypeStruct((3, A_WIDTH, d), F32), jax.ShapeDtypeStruct((d, d), F32)],
        scratch_shapes=[pltpu.VMEM((3, A_WIDTH, d), F32), pltpu.VMEM((d, d), F32)],
        name="mid_bwd", compiler_params=_params(heavy=True))(d_out, merged, ya, yb, ym, proj, w_up, w_out)


def _conv_core(blk, prev, nxt, w, i, last, ts):
    c = A_WIDTH
    ab, ac, ax, az = blk[:, :c], blk[:, c:2 * c], blk[:, 2 * c:3 * c], blk[:, 3 * c:]
    cu = ac * ax
    cu_prev = (prev[7:8, c:2 * c] * prev[7:8, 2 * c:3 * c]) * jnp.where(i > 0, 1.0, 0.0)
    cu_next = (nxt[0:1, c:2 * c] * nxt[0:1, 2 * c:3 * c]) * jnp.where(i < last, 1.0, 0.0)
    row = lax.broadcasted_iota(jnp.int32, (ts, c), 0)
    cm1 = jnp.where(row == 0, cu_prev, pltpu.roll(cu, 1, 0))
    cp1 = jnp.where(row == ts - 1, cu_next, pltpu.roll(cu, ts - 1, 0))
    yc = cm1 * w[0:1] + cu * w[1:2] + cp1 * w[2:3]
    return ab, ac, ax, az, cu, cm1, cp1, yc, row


def _halo_specs(ts, width, col, nblk8):
    prev = pl.BlockSpec((8, width), lambda i: (jnp.maximum(i * (ts // 8) - 1, 0), col))
    nxt = pl.BlockSpec((8, width), lambda i: (jnp.minimum((i + 1) * (ts // 8), nblk8 - 1), col))
    return prev, nxt


def _conv_fwd(proj, w_conv):
    s = proj.shape[0]
    ts = 256
    last = s // ts - 1

    def body(a_ref, ap_ref, an_ref, w_ref, ya_ref):
        i = pl.program_id(0)
        ab, _, _, az, _, _, _, yc, _ = _conv_core(a_ref[...], ap_ref[...], an_ref[...], w_ref[...], i, last, ts)
        ya_ref[...] = ((ab * yc) * (az * _sigmoid(az))).astype(BF16)

    prev, nxt = _halo_specs(ts, W_A, COL_A, s // 8)
    return pl.pallas_call(
        body, grid=(s // ts,),
        in_specs=[pl.BlockSpec((ts, W_A), lambda i: (i, COL_A)), prev, nxt,
                  pl.BlockSpec((3, A_WIDTH), lambda i: (0, 0))],
        out_specs=pl.BlockSpec((ts, A_WIDTH), lambda i: (i, 0)),
        out_shape=jax.ShapeDtypeStruct((s, A_WIDTH), BF16), name="conv_fwd",
        compiler_params=_params())(proj, proj, proj, w_conv)


def _conv_bwd(proj, w_conv, dya, dproj):
    s = proj.shape[0]
    ts = 256
    last = s // ts - 1
    c = A_WIDTH

    def body(a_ref, ap_ref, an_ref, w_ref, d_ref, dp_ref, dn_ref, _, dproj_ref, dw_ref):
        i = pl.program_id(0)
        w = w_ref[...]
        prev, nxt = ap_ref[...], an_ref[...]
        ab, ac, ax, az, cu, cm1, cp1, yc, row = _conv_core(a_ref[...], prev, nxt, w, i, last, ts)
        sg = _sigmoid(az)
        sz = az * sg
        dya_v = d_ref[...]
        dyc = dya_v * sz * ab
        dproj_ref[:, :c] = (dya_v * sz * yc).astype(BF16)
        dproj_ref[:, 3 * c:] = (dya_v * (ab * yc) * (sg * (1.0 + az * (1.0 - sg)))).astype(BF16)

        def halo_dyc(a_row, d_row):
            azr = a_row[:, 3 * c:]
            return d_row * (azr * _sigmoid(azr)) * a_row[:, :c]

        dyc_prev = halo_dyc(prev[7:8], dp_ref[...][7:8]) * jnp.where(i > 0, 1.0, 0.0)
        dyc_next = halo_dyc(nxt[0:1], dn_ref[...][0:1]) * jnp.where(i < last, 1.0, 0.0)
        dyc_m1 = jnp.where(row == 0, dyc_prev, pltpu.roll(dyc, 1, 0))
        dyc_p1 = jnp.where(row == ts - 1, dyc_next, pltpu.roll(dyc, ts - 1, 0))
        dcu = dyc_p1 * w[0:1] + dyc * w[1:2] + dyc_m1 * w[2:3]
        dproj_ref[:, c:2 * c] = (dcu * ax).astype(BF16)
        dproj_ref[:, 2 * c:3 * c] = (dcu * ac).astype(BF16)
        dw = [jnp.sum(dyc * t, axis=0, keepdims=True) for t in (cm1, cu, cp1)]

        @pl.when(i == 0)
        def _():
            for k in range(3):
                dw_ref[k:k + 1, :] = dw[k]

        @pl.when(i > 0)
        def _():
            for k in range(3):
                dw_ref[k:k + 1, :] += dw[k]

    prev, nxt = _halo_specs(ts, W_A, COL_A, s // 8)
    dprev, dnxt = _halo_specs(ts, A_WIDTH, 0, s // 8)
    return pl.pallas_call(
        body, grid=(s // ts,),
        in_specs=[pl.BlockSpec((ts, W_A), lambda i: (i, COL_A)), prev, nxt,
                  pl.BlockSpec((3, A_WIDTH), lambda i: (0, 0)),
                  pl.BlockSpec((ts, A_WIDTH), lambda i: (i, 0)), dprev, dnxt,
                  pl.BlockSpec(memory_space=pl.ANY)],
        out_specs=[pl.BlockSpec((ts, W_A), lambda i: (i, COL_A)), pl.BlockSpec((3, A_WIDTH), lambda i: (0, 0))],
        out_shape=[jax.ShapeDtypeStruct(dproj.shape, BF16), jax.ShapeDtypeStruct((3, A_WIDTH), F32)],
        input_output_aliases={7: 0}, name="conv_bwd",
        compiler_params=_params())(proj, proj, proj, w_conv, dya, dya, dya, dproj)


def _rope_tables(s):
    half = ROT_DIM // 2
    dim = jnp.arange(LANES) % HEAD_DIM
    inv_freq = jnp.power(jnp.float32(ROPE_THETA), -(dim % half).astype(F32) * (2.0 / ROT_DIM))
    ang = jnp.arange(s).astype(F32)[:, None] * inv_freq[None, :]
    cos, sin = jnp.cos(ang), jnp.sin(ang)
    first, second = (dim < half)[None, :], ((dim >= half) & (dim < ROT_DIM))[None, :]
    c = jnp.where(first | second, cos, 1.0)
    s1 = jnp.where(first, -sin, 0.0)
    s2 = jnp.where(second, sin, 0.0)
    return jnp.concatenate([c, s1, s2], axis=1)


def _rope(t, tab):
    return (t * tab[:, :LANES] + pltpu.roll(t, LANES - 8, 1) * tab[:, LANES:2 * LANES]
            + pltpu.roll(t, 8, 1) * tab[:, 2 * LANES:])


def _rope_transpose(dt, tab):
    return (dt * tab[:, :LANES] + pltpu.roll(dt * tab[:, LANES:2 * LANES], 8, 1)
            + pltpu.roll(dt * tab[:, 2 * LANES:], LANES - 8, 1))


def _rope_kv(proj, tab):
    s = proj.shape[0]
    nb = s // KV_PAD

    def body(kv_ref, t_ref, k_ref, v_ref):
        j = pl.program_id(0)
        inside = jnp.where((j > 0) & (j <= nb), 1.0, 0.0)
        kv = kv_ref[...]
        k_ref[...] = (_rope(kv[:, :LANES], t_ref[...]) * inside).astype(BF16)
        v_ref[...] = (kv[:, LANES:] * inside).astype(BF16)

    def src(j):
        return jnp.clip(j - 1, 0, nb - 1)

    o_spec = pl.BlockSpec((KV_PAD, LANES), lambda j: (j, 0))
    shp = jax.ShapeDtypeStruct((s + 2 * KV_PAD, LANES), BF16)
    return pl.pallas_call(
        body, grid=(nb + 2,),
        in_specs=[pl.BlockSpec((KV_PAD, W_KV), lambda j: (src(j), COL_KV)),
                  pl.BlockSpec((KV_PAD, 3 * LANES), lambda j: (src(j), 0))],
        out_specs=[o_spec, o_spec], out_shape=[shp, shp], name="rope_kv",
        compiler_params=_params())(proj, tab)


def _rope_kv_bwd(dkpad, dvpad, tab, dproj):
    s = tab.shape[0]
    nb = s // KV_PAD

    def body(dk_ref, dv_ref, t_ref, _, dp_ref):
        dp_ref[:, :LANES] = _rope_transpose(dk_ref[...], t_ref[...]).astype(BF16)
        dp_ref[:, LANES:] = dv_ref[...].astype(BF16)

    pad_spec = pl.BlockSpec((KV_PAD, LANES), lambda j: (j + 1, 0))
    return pl.pallas_call(
        body, grid=(nb,),
        in_specs=[pad_spec, pad_spec, pl.BlockSpec((KV_PAD, 3 * LANES), lambda j: (j, 0)),
                  pl.BlockSpec(memory_space=pl.ANY)],
        out_specs=pl.BlockSpec((KV_PAD, W_KV), lambda j: (j, COL_KV)),
        out_shape=jax.ShapeDtypeStruct(dproj.shape, BF16), input_output_aliases={3: 0},
        name="rope_kv_bwd", compiler_params=_params())(dkpad, dvpad, tab, dproj)


def _window_start(n):
    return pl.multiple_of((n - 1) * WINDOW_BLOCK + KV_PAD, WINDOW_BLOCK)


def _window_operands(k_ref, v_ref, n, lo):
    start = _window_start(n)
    kw = k_ref[pl.ds(start, 3 * WINDOW_BLOCK), :].astype(F32)
    vw = v_ref[pl.ds(start, 3 * WINDOW_BLOCK), :].astype(F32)
    kr, vr = pltpu.roll(kw, HALF_LANES, 1), pltpu.roll(vw, HALF_LANES, 1)
    k2 = (jnp.where(lo, kw, kr).astype(BF16), jnp.where(lo, kr, kw).astype(BF16))
    v2 = (jnp.where(lo, vw, vr).astype(BF16), jnp.where(lo, vr, vw).astype(BF16))
    return k2, v2


HEADS_PER_GROUP = 4
SWA_FWD_BLOCKS = 1
SWA_BWD_BLOCKS = 2


def _window_bias():
    wb = WINDOW_BLOCK
    qi = lax.broadcasted_iota(jnp.int32, (wb, 3 * wb), 0)
    kj = lax.broadcasted_iota(jnp.int32, (wb, 3 * wb), 1)
    band = (kj >= qi) & (kj <= qi + 2 * wb)
    cases = jnp.stack([band & (kj >= wb), band, band & (kj < 2 * wb)])
    return jnp.where(cases, 0.0, -jnp.inf).astype(F32)


def _block_bias(bias_ref, n, n_blocks):
    case = jnp.where(n == 0, 0, jnp.where(n == n_blocks - 1, 2, 1))
    one = bias_ref[case]
    return jnp.concatenate([one] * HEADS_PER_GROUP, axis=0)


def _stack_heads(pair0, pair1, lo):
    return jnp.concatenate([jnp.where(lo, pair0, 0.0), jnp.where(lo, 0.0, pair0),
                            jnp.where(lo, pair1, 0.0), jnp.where(lo, 0.0, pair1)], axis=0)


def _unstack_pair(stacked, i, lo):
    wb = WINDOW_BLOCK
    return jnp.where(lo, stacked[2 * i * wb:(2 * i + 1) * wb], stacked[(2 * i + 1) * wb:(2 * i + 2) * wb])


def _sink_column(sink_ref, g):
    wb = WINDOW_BLOCK
    return jnp.concatenate([jnp.full((wb, 1), sink_ref[0, HEADS_PER_GROUP * g + i], F32)
                            for i in range(HEADS_PER_GROUP)], axis=0)


def _head_exp(q4, k2g, bias, sink):
    sc = lax.dot_general(q4, k2g, _DIMS["nt"], preferred_element_type=F32) * (HEAD_DIM ** -0.5) + bias
    m = jnp.maximum(jnp.max(sc, axis=1, keepdims=True), sink)
    return jnp.exp(sc - m).astype(BF16), jnp.exp(sink - m)


def _swa_fwd(proj, kpad, vpad, tab, bias, sink, *, carry=None):
    s = proj.shape[0]
    wb = WINDOW_BLOCK

    def body(b_ref, k_ref, v_ref, t_ref, bias_ref, sink_ref, o_ref, y_ref):
        lo = lax.broadcasted_iota(jnp.int32, (wb, LANES), 1) < HALF_LANES
        lo_w = lax.broadcasted_iota(jnp.int32, (3 * wb, LANES), 1) < HALF_LANES
        for sub in range(SWA_FWD_BLOCKS):
            n = pl.program_id(0) * SWA_FWD_BLOCKS + sub
            rows = slice(sub * wb, (sub + 1) * wb)
            k2, v2 = _window_operands(k_ref, v_ref, n, lo_w)
            valid = _block_bias(bias_ref, n, s // wb)
            tab_v = t_ref[rows, :]
            ones = jnp.ones((3 * wb, LANES), BF16)
            for g in range(2):
                qr = [_rope(b_ref[rows, (2 * g + i) * LANES:(2 * g + i + 1) * LANES], tab_v) for i in range(2)]
                q4 = _stack_heads(qr[0], qr[1], lo).astype(BF16)
                e, es = _head_exp(q4, k2[g], valid, _sink_column(sink_ref, g))
                ox = jnp.dot(e, jnp.concatenate([v2[g], ones], axis=1), preferred_element_type=F32)
                o4 = ox[:, :LANES] * (1.0 / (ox[:, LANES:] + es))
                for i in range(2):
                    cols = slice((2 * g + i) * LANES, (2 * g + i + 1) * LANES)
                    op = _unstack_pair(o4, i, lo)
                    o_ref[rows, cols] = op
                    zp = b_ref[rows, A_WIDTH + cols.start:A_WIDTH + cols.stop]
                    y_ref[rows, cols] = (op * (zp * _sigmoid(zp))).astype(BF16)

    tq = SWA_FWD_BLOCKS * wb
    pad_spec = pl.BlockSpec((s + 2 * KV_PAD, LANES), lambda n: (0, 0))
    o_spec = pl.BlockSpec((tq, A_WIDTH), lambda n: (n, 0))
    outs, carried = _carried_call(
        lambda ins, outs, scr: body(*ins, *outs), carry, grid=(s // tq,),
        in_specs=[pl.BlockSpec((tq, W_B), lambda n: (n, COL_B)), pad_spec, pad_spec,
                  pl.BlockSpec((tq, 3 * LANES), lambda n: (n, 0)),
                  pl.BlockSpec(bias.shape, lambda n: (0, 0, 0)), pl.BlockSpec(memory_space=pltpu.SMEM)],
        out_specs=[o_spec, o_spec],
        out_shape=[jax.ShapeDtypeStruct((s, A_WIDTH), F32), jax.ShapeDtypeStruct((s, A_WIDTH), BF16)],
        scratch=[], operands=(proj, kpad, vpad, tab, bias, sink), name="swa_fwd")
    return (*outs, carried) if carry else tuple(outs)


def _swa_bwd(proj, kpad, vpad, tab, bias, sink, o_attn, dyb, dproj):
    s = proj.shape[0]
    wb = WINDOW_BLOCK
    scale = HEAD_DIM ** -0.5

    def body(b_ref, k_ref, v_ref, t_ref, bias_ref, sink_ref, o_ref, dy_ref, _, dp_ref, dk_ref, dv_ref, ds_ref):
        @pl.when(pl.program_id(0) == 0)
        def _():
            dk_ref[...] = jnp.zeros_like(dk_ref)
            dv_ref[...] = jnp.zeros_like(dv_ref)
            ds_ref[...] = jnp.zeros_like(ds_ref)

        lo = lax.broadcasted_iota(jnp.int32, (wb, LANES), 1) < HALF_LANES
        lo_w = lax.broadcasted_iota(jnp.int32, (3 * wb, LANES), 1) < HALF_LANES
        for sub in range(SWA_BWD_BLOCKS):
            n = pl.program_id(0) * SWA_BWD_BLOCKS + sub
            rows = slice(sub * wb, (sub + 1) * wb)
            k2, v2 = _window_operands(k_ref, v_ref, n, lo_w)
            valid = _block_bias(bias_ref, n, s // wb)
            tab_v = t_ref[rows, :]
            ones = jnp.ones((3 * wb, LANES), BF16)
            dks, dvs = [], []
            for g in range(2):
                qr, op, do = [], [], []
                for i in range(2):
                    cols = slice((2 * g + i) * LANES, (2 * g + i + 1) * LANES)
                    zcols = slice(A_WIDTH + cols.start, A_WIDTH + cols.stop)
                    qr.append(_rope(b_ref[rows, cols], tab_v))
                    zp = b_ref[rows, zcols]
                    sg = _sigmoid(zp)
                    op.append(o_ref[rows, cols])
                    dyp = dy_ref[rows, cols]
                    do.append(dyp * (zp * sg))
                    dp_ref[rows, zcols] = (dyp * op[i] * (sg * (1.0 + zp * (1.0 - sg)))).astype(BF16)
                q4 = _stack_heads(qr[0], qr[1], lo).astype(BF16)
                do4 = _stack_heads(do[0], do[1], lo)
                o4 = jnp.concatenate([op[0], op[0], op[1], op[1]], axis=0)
                e, es = _head_exp(q4, k2[g], valid, _sink_column(sink_ref, g))
                inv = 1.0 / (jnp.dot(e, ones, preferred_element_type=F32) + es)
                prob = e.astype(F32) * jnp.concatenate([inv, inv, inv], axis=1)
                delta = jnp.sum(do4 * o4, axis=1, keepdims=True)
                do4b = do4.astype(BF16)
                dprob = lax.dot_general(do4b, v2[g], _DIMS["nt"], preferred_element_type=F32)
                dsc = (prob * (dprob - delta)).astype(BF16)
                sink_terms = (es * inv[:, :1]) * delta
                for i in range(HEADS_PER_GROUP):
                    h = HEADS_PER_GROUP * g + i
                    dsink = -jnp.sum(sink_terms[i * wb:(i + 1) * wb], axis=0, keepdims=True)
                    ds_ref[h:h + 1, :] += jnp.broadcast_to(dsink, (1, LANES))
                dq4 = jnp.dot(dsc, k2[g], preferred_element_type=F32) * scale
                for i in range(2):
                    cols = slice((2 * g + i) * LANES, (2 * g + i + 1) * LANES)
                    dp_ref[rows, cols] = _rope_transpose(_unstack_pair(dq4, i, lo), tab_v).astype(BF16)
                dk2 = lax.dot_general(dsc, q4, _DIMS["tn"], preferred_element_type=F32) * scale
                dv2 = lax.dot_general(prob.astype(BF16), do4b, _DIMS["tn"], preferred_element_type=F32)
                dks.append(dk2 + pltpu.roll(dk2, HALF_LANES, 1))
                dvs.append(dv2 + pltpu.roll(dv2, HALF_LANES, 1))
            start = _window_start(n)
            dk_ref[pl.ds(start, 3 * wb), :] += jnp.where(lo_w, dks[0], dks[1])
            dv_ref[pl.ds(start, 3 * wb), :] += jnp.where(lo_w, dvs[0], dvs[1])

    tq = SWA_BWD_BLOCKS * wb
    pad_spec = pl.BlockSpec((s + 2 * KV_PAD, LANES), lambda n: (0, 0))
    blk = pl.BlockSpec((tq, A_WIDTH), lambda n: (n, 0))
    bsp = pl.BlockSpec((tq, W_B), lambda n: (n, COL_B))
    pad_shape = jax.ShapeDtypeStruct((s + 2 * KV_PAD, LANES), F32)
    return pl.pallas_call(
        body, grid=(s // tq,),
        in_specs=[bsp, pad_spec, pad_spec, pl.BlockSpec((tq, 3 * LANES), lambda n: (n, 0)),
                  pl.BlockSpec(bias.shape, lambda n: (0, 0, 0)), pl.BlockSpec(memory_space=pltpu.SMEM), blk, blk,
                  pl.BlockSpec(memory_space=pl.ANY)],
        out_specs=[bsp, pad_spec, pad_spec, pl.BlockSpec((8, LANES), lambda n: (0, 0))],
        out_shape=[jax.ShapeDtypeStruct(dproj.shape, BF16), pad_shape, pad_shape,
                   jax.ShapeDtypeStruct((8, LANES), F32)],
        input_output_aliases={8: 0}, name="swa_bwd",
        compiler_params=_params())(proj, kpad, vpad, tab, bias, sink, o_attn, dyb, dproj)


def _mem_exp(qh, mk):
    sc = lax.dot_general(qh, mk, _DIMS["nt"], preferred_element_type=F32) * (MEM_HEAD_DIM ** -0.5)
    return jnp.exp(sc - jnp.max(sc, axis=1, keepdims=True)).astype(BF16)


def _mem_fwd(proj, mkv):
    s = proj.shape[0]
    ts = 512
    mlen = mkv.shape[0]

    def body(m_ref, kv_ref, o_ref, y_ref):
        ones = jnp.ones((mlen, LANES), BF16)
        for h in range(MEM_HEADS):
            cols = slice(h * LANES, (h + 1) * LANES)
            mk = kv_ref[:, cols].astype(BF16)
            mv = kv_ref[:, MEM_WIDTH + h * LANES:MEM_WIDTH + (h + 1) * LANES].astype(BF16)
            e = _mem_exp(m_ref[:, cols].astype(BF16), mk)
            ox = jnp.dot(e, jnp.concatenate([mv, ones], axis=1), preferred_element_type=F32)
            oh = ox[:, :LANES] * (1.0 / ox[:, LANES:])
            o_ref[:, cols] = oh
            zh = m_ref[:, MEM_WIDTH + h * LANES:MEM_WIDTH + (h + 1) * LANES]
            y_ref[:, cols] = (oh * (zh * _sigmoid(zh))).astype(BF16)

    o_spec = pl.BlockSpec((ts, MEM_WIDTH), lambda i: (i, 0))
    return pl.pallas_call(
        body, grid=(s // ts,),
        in_specs=[pl.BlockSpec((ts, W_M), lambda i: (i, COL_M)),
                  pl.BlockSpec((mlen, 2 * MEM_WIDTH), lambda i: (0, 0))],
        out_specs=[o_spec, o_spec],
        out_shape=[jax.ShapeDtypeStruct((s, MEM_WIDTH), F32), jax.ShapeDtypeStruct((s, MEM_WIDTH), BF16)],
        name="mem_fwd", compiler_params=_params())(proj, mkv)


def _mem_bwd(proj, mkv, o_mem, dym, dproj, *, carry=None):
    s = proj.shape[0]
    ts = 512
    mlen = mkv.shape[0]
    scale = MEM_HEAD_DIM ** -0.5

    def body(m_ref, kv_ref, o_ref, dy_ref, _, dp_ref, dkv_ref):
        @pl.when(pl.program_id(0) == 0)
        def _():
            dkv_ref[...] = jnp.zeros_like(dkv_ref)

        ones = jnp.ones((mlen, LANES), BF16)
        for h in range(MEM_HEADS):
            cols = slice(h * LANES, (h + 1) * LANES)
            vcols = slice(MEM_WIDTH + h * LANES, MEM_WIDTH + (h + 1) * LANES)
            mk = kv_ref[:, cols].astype(BF16)
            mv = kv_ref[:, vcols].astype(BF16)
            qh = m_ref[:, cols].astype(BF16)
            zh = m_ref[:, vcols]
            sg = _sigmoid(zh)
            oh = o_ref[:, cols]
            dyh = dy_ref[:, cols]
            doh = dyh * (zh * sg)
            dp_ref[:, vcols] = (dyh * oh * (sg * (1.0 + zh * (1.0 - sg)))).astype(BF16)
            e = _mem_exp(qh, mk)
            inv = 1.0 / jnp.dot(e, ones, preferred_element_type=F32)
            prob = e.astype(F32) * jnp.concatenate([inv] * (mlen // LANES), axis=1)
            delta = jnp.sum(doh * oh, axis=1, keepdims=True)
            dohb = doh.astype(BF16)
            dprob = lax.dot_general(dohb, mv, _DIMS["nt"], preferred_element_type=F32)
            dsc = (prob * (dprob - delta)).astype(BF16)
            dp_ref[:, cols] = (jnp.dot(dsc, mk, preferred_element_type=F32) * scale).astype(BF16)
            dkv_ref[:, cols] += lax.dot_general(dsc, qh, _DIMS["tn"], preferred_element_type=F32) * scale
            dkv_ref[:, vcols] += lax.dot_general(prob.astype(BF16), dohb, _DIMS["tn"],
                                                 preferred_element_type=F32)

    blk = pl.BlockSpec((ts, MEM_WIDTH), lambda i: (i, 0))
    msp = pl.BlockSpec((ts, W_M), lambda i: (i, COL_M))
    kvsp = pl.BlockSpec((mlen, 2 * MEM_WIDTH), lambda i: (0, 0))
    outs, carried = _carried_call(
        lambda ins, outs, scr: body(*ins, *outs), carry, grid=(s // ts,),
        in_specs=[msp, kvsp, blk, blk, pl.BlockSpec(memory_space=pl.ANY)],
        out_specs=[msp, kvsp],
        out_shape=[jax.ShapeDtypeStruct(dproj.shape, BF16), jax.ShapeDtypeStruct(mkv.shape, F32)],
        scratch=[], operands=(proj, mkv, o_mem, dym, dproj), name="mem_bwd", aliases={4: 0})
    return (*outs, carried) if carry else tuple(outs)


def _forward_backward(x, mem, tgt, proj, w_conv, sink, g_mem, late_weights, g_post, early_exchange, kv_exchange):
    s = x.shape[0]
    tab = _rope_tables(s)
    bias = _window_bias()

    ya = _conv_fwd(proj, w_conv)
    kpad, vpad = _rope_kv(proj, tab)
    o_attn, yb, *arrived = _swa_fwd(proj, kpad, vpad, tab, bias, sink, carry=late_weights[0])
    w_kv, w_up, w_out = late_weights[1](arrived[0] if arrived else None)
    mn = _rmsnorm_fwd(mem, g_mem, name="mem_norm")
    mkv = _matmul(mn, w_kv, mode="nn", out_dtype=F32, tm=256, tn=1024, tk=D_MODEL, name="mem_kv")
    o_mem, ym = _mem_fwd(proj, mkv)
    merged, d_out, dy, dg_post, loss = _mid_fwd(ya, yb, ym, proj, x, tgt, w_up, w_out, g_post)
    dproj, d_ya, d_yb, d_ym, dw_up, dw_out = _mid_bwd(d_out, merged, ya, yb, ym, proj, w_up, w_out)

    dproj, dw_conv = _conv_bwd(proj, w_conv, d_ya, dproj)
    dproj, dkpad, dvpad, dsink = _swa_bwd(proj, kpad, vpad, tab, bias, sink, o_attn, d_yb, dproj)
    dproj = _rope_kv_bwd(dkpad, dvpad, tab, dproj)
    dproj, d_mkv, *early = _mem_bwd(proj, mkv, o_mem, d_ym, dproj, carry=early_exchange(dw_up, dw_out))

    dw_kv = _matmul(mn, d_mkv, mode="tn", out_dtype=F32, tm=1024, tn=1024, tk=256, name="dw_kv")
    d_mn = _matmul(d_mkv, w_kv, mode="nt", out_dtype=F32, tm=256, tn=1024, tk=D_MODEL, name="d_mn")
    _, dg_mem, *early_kv = _rmsnorm_bwd(d_mn, mem, g_mem, d_mn, name="mem_norm_bwd", carry=kv_exchange(dw_kv))

    return dict(loss=loss, dproj=dproj, dy=dy, w_conv=dw_conv, sink=dsink, g_mem=dg_mem,
                w_kv=dw_kv, w_up=dw_up, w_out=dw_out, g_post=dg_post, early=early[0] if early else None,
                early_kv=early_kv[0] if early_kv else None)


N_DEV = 8


def _position():
    return lax.axis_index("x"), lax.axis_index("y"), lax.axis_index("c")


def _other_chips(x, y):
    return (((1 - x, y), 2 * (1 - x) + y), ((x, 1 - y), 2 * x + (1 - y)), ((1 - x, 1 - y), 2 * (1 - x) + (1 - y)))


def _remote(src, dst, send_sems, recv_sems, k, device):
    return pltpu.make_async_remote_copy(src_ref=src, dst_ref=dst, send_sem=send_sems.at[k], recv_sem=recv_sems.at[k],
                                        device_id=device, device_id_type=MESH)


def _rows_half(ref, hf):
    rh = ref.shape[0] // 2
    return ref.at[pl.ds(pl.multiple_of(hf * rh, 8), rh)]


def _gather_weights(shards, small=None, relations=(0, 1, 2), into=None):
    n = len(shards)
    k = 0 if small is None else 1

    def peers(x, y):
        return [(r, chip, idx) for r, (chip, idx) in enumerate(_other_chips(x, y)) if r in relations]

    def ici(ins, outs, sems, a, r, chip, src_chip, c):
        return _remote(_rows_half(ins[a], c), _rows_half(outs[a].at[src_chip], c), sems[0], sems[1], 3 * a + r,
                       (*chip, c))

    def whole(ins, outs, sems, r, chip, src_chip, c):
        return _remote(ins[n], outs[n].at[src_chip], sems[0], sems[1], 3 * n + r, (*chip, c))

    def d2d(outs, sems, a, r, idx, hf, x, y, c):
        half = _rows_half(outs[a].at[idx], hf)
        return _remote(half, half, sems[2], sems[3], 3 * a + r, (x, y, 1 - c))

    def start(ins, outs, sems):
        x, y, c = _position()
        me = 2 * x + y
        for a in range(n):
            for r, chip, _ in peers(x, y):
                ici(ins, outs, sems, a, r, chip, me, c).start()
        for r, (chip, _) in enumerate(_other_chips(x, y)):
            if k:
                whole(ins, outs, sems, r, chip, me, c).start()

    def finish(ins, outs, sems):
        x, y, c = _position()
        me = 2 * x + y
        for a in range(n):
            for r, chip, idx in peers(x, y):
                ici(ins, outs, sems, a, r, chip, idx, c).wait_recv()
                d2d(outs, sems, a, r, idx, c, x, y, c).start()
        for a in range(n):
            for r, chip, idx in peers(x, y):
                d2d(outs, sems, a, r, idx, 1 - c, x, y, c).wait_recv()
        for r, (chip, idx) in enumerate(_other_chips(x, y)):
            if k:
                whole(ins, outs, sems, r, chip, idx, c).wait_recv()
                whole(ins, outs, sems, r, chip, me, c).wait_send()
        for a in range(n):
            for r, chip, idx in peers(x, y):
                ici(ins, outs, sems, a, r, chip, me, c).wait_send()
                d2d(outs, sems, a, r, idx, c, x, y, c).wait_send()

    operands = list(shards) + ([small] if k else [])
    shapes = [jax.ShapeDtypeStruct((N_CHIPS,) + s.shape, s.dtype) for s in operands]
    aliases = {}
    if into is not None:
        assert len(into) == len(operands)
        aliases = {len(operands) + a: a for a in range(len(into))}
        operands += list(into)
    return _Carry(operands, shapes,
                  [pltpu.SemaphoreType.DMA((3 * (n + k),)), pltpu.SemaphoreType.DMA((3 * (n + k),)),
                   pltpu.SemaphoreType.DMA((3 * n,)), pltpu.SemaphoreType.DMA((3 * n,))], start, finish, aliases)


def _pair_exchange(send):
    n = len(send)

    def copies(ins, outs, sems):
        x, y, c = _position()
        return [_remote(ins[a], outs[a], sems[0], sems[1], a, (x, y, 1 - c)) for a in range(n)]

    def start(ins, outs, sems):
        for cp in copies(ins, outs, sems):
            cp.start()

    def finish(ins, outs, sems):
        for cp in copies(ins, outs, sems):
            cp.wait()

    return _Carry(send, [jax.ShapeDtypeStruct(p.shape, p.dtype) for p in send],
                  [pltpu.SemaphoreType.DMA((n,)), pltpu.SemaphoreType.DMA((n,))], start, finish)


def _chip_exchange(sums):
    n = len(sums)

    def copies(ins, outs, sems):
        x, y, c = _position()
        return [_remote(ins[a].at[idx], outs[a].at[r], sems[0], sems[1], 3 * a + r, (*chip, c))
                for a in range(n) for r, (chip, idx) in enumerate(_other_chips(x, y))]

    def start(ins, outs, sems):
        for cp in copies(ins, outs, sems):
            cp.start()

    def finish(ins, outs, sems):
        for cp in copies(ins, outs, sems):
            cp.wait()

    return _Carry(sums, [jax.ShapeDtypeStruct((3,) + p.shape[1:], p.dtype) for p in sums],
                  [pltpu.SemaphoreType.DMA((3 * n,)), pltpu.SemaphoreType.DMA((3 * n,))], start, finish)


def _pair_share(pairs):
    n = len(pairs)

    def start(ins, outs, sems):
        x, y, c = _position()
        for a in range(n):
            _remote(outs[a].at[c], outs[a].at[c], sems[0], sems[1], a, (x, y, 1 - c)).start()

    def finish(ins, outs, sems):
        x, y, c = _position()
        for a in range(n):
            _remote(outs[a].at[1 - c], outs[a].at[1 - c], sems[0], sems[1], a, (x, y, 1 - c)).wait_recv()
        for a in range(n):
            _remote(outs[a].at[c], outs[a].at[c], sems[0], sems[1], a, (x, y, 1 - c)).wait_send()

    return _Carry(pairs, [jax.ShapeDtypeStruct(p.shape, p.dtype) for p in pairs],
                  [pltpu.SemaphoreType.DMA((n,)), pltpu.SemaphoreType.DMA((n,))], start, finish,
                  aliases={a: a for a in range(n)})


def _small_allreduce(pack, share):
    rows, width = pack.shape
    n_share = len(share.ins)

    def body(p_ref, *refs):
        share_in, o_ref, share_out = refs[:n_share], refs[n_share], refs[n_share + 1:2 * n_share + 1]
        buf, send_sems, recv_sems = refs[2 * n_share + 1:2 * n_share + 4]
        share_sems = refs[2 * n_share + 4:]
        share.start(share_in, share_out, share_sems)
        x, y, c = _position()
        me = 4 * x + 2 * y + c
        buf[me] = p_ref[...]
        peers = []
        for r in range(1, N_DEV):
            fx, fy, fc = (r >> 2) & 1, (r >> 1) & 1, r & 1
            px, py, pc = (1 - x if fx else x), (1 - y if fy else y), (1 - c if fc else c)
            peers.append(((px, py, pc), 4 * px + 2 * py + pc))
        sends = [_remote(p_ref, buf.at[me], send_sems, recv_sems, r, dev) for r, (dev, _) in enumerate(peers)]
        for cp in sends:
            cp.start()
        for r, (dev, idx) in enumerate(peers):
            _remote(p_ref, buf.at[idx], send_sems, recv_sems, r, dev).wait_recv()
        for cp in sends:
            cp.wait_send()
        acc = buf[0]
        for k in range(1, N_DEV):
            acc = acc + buf[k]
        o_ref[...] = acc
        share.finish(share_in, share_out, share_sems)

    vm = pl.BlockSpec(memory_space=pltpu.VMEM)
    red, *shared = pl.pallas_call(
        body, in_specs=[vm] + [_HBM] * n_share, out_specs=[vm] + [_HBM] * n_share,
        out_shape=[jax.ShapeDtypeStruct(pack.shape, F32)] + share.out_shapes,
        scratch_shapes=[pltpu.VMEM((N_DEV, rows, width), F32), pltpu.SemaphoreType.DMA((N_DEV - 1,)),
                        pltpu.SemaphoreType.DMA((N_DEV - 1,))] + share.sems,
        input_output_aliases={1 + i: 1 + o for i, o in share.aliases.items()},
        name="small_allreduce")(pack, *share.ins)
    return red, shared


ROW_TILE_MAX = 512
SUM_TILE_MAX = 2048
BF16_SUBLANES = 16


def _row_tile(rows, most=ROW_TILE_MAX):
    if rows <= most:
        return rows
    return max(t for t in range(BF16_SUBLANES, most + 1, BF16_SUBLANES) if rows % t == 0)


def _pair_add(keep, recv, name):
    nj, rh, cols = keep.shape
    tr = _row_tile(rh, SUM_TILE_MAX)

    def body(k_ref, r_ref, o_ref):
        o_ref[...] = (k_ref[...].astype(F32) + r_ref[...].astype(F32)).astype(BF16)

    blk = pl.BlockSpec((None, tr, cols), lambda j, i: (j, i, 0))
    return pl.pallas_call(body, grid=(nj, rh // tr), in_specs=[blk, blk], out_specs=blk,
                          out_shape=jax.ShapeDtypeStruct(keep.shape, BF16), name=name,
                          compiler_params=_params())(keep, recv)


def _chip_add(sums, recv, where, name):
    _, rh, cols = sums.shape
    tr = _row_tile(rh, SUM_TILE_MAX)

    def body(w_ref, s_ref, r_ref, o_ref):
        o_ref[...] = ((s_ref[...].astype(F32) + r_ref[0].astype(F32)) + r_ref[1].astype(F32)) + r_ref[2].astype(F32)

    grid_spec = pltpu.PrefetchScalarGridSpec(
        num_scalar_prefetch=1, grid=(rh // tr,),
        in_specs=[pl.BlockSpec((None, tr, cols), lambda i, w_ref: (w_ref[0], i, 0)),
                  pl.BlockSpec((3, tr, cols), lambda i, w_ref: (0, i, 0))],
        out_specs=pl.BlockSpec((None, tr, cols), lambda i, w_ref: (w_ref[1], i, 0)))
    return pl.pallas_call(body, grid_spec=grid_spec, out_shape=jax.ShapeDtypeStruct((2, rh, cols), F32),
                          name=name, compiler_params=_params())(where, sums, recv)


def _adamw(w, g, m, v, name):
    rows, cols = w.shape
    tr = _row_tile(rows)
    assert rows % tr == 0

    def body(w_ref, g_ref, m_ref, v_ref, d_ref, mo_ref, vo_ref):
        gv = g_ref[...]
        m_new = ADAM_B1 * m_ref[...] + (1.0 - ADAM_B1) * gv
        v_new = ADAM_B2 * v_ref[...] + (1.0 - ADAM_B2) * jnp.square(gv)
        m_hat = m_new / (1.0 - ADAM_B1 ** ADAM_STEP)
        v_hat = v_new / (1.0 - ADAM_B2 ** ADAM_STEP)
        d_ref[...] = -ADAM_LR * (m_hat / (jnp.sqrt(v_hat) + ADAM_EPS) + ADAM_WD * w_ref[...])
        mo_ref[...] = m_new
        vo_ref[...] = v_new

    blk = pl.BlockSpec((tr, cols), lambda i: (i, 0))
    shp = jax.ShapeDtypeStruct((rows, cols), F32)
    return pl.pallas_call(body, grid=(rows // tr,), in_specs=[blk] * 4, out_specs=[blk] * 3,
                          out_shape=[shp] * 3, name=name, compiler_params=_params())(w, g, m, v)


def _adamw_halves(w, g2, m, v, name):
    rows, cols = w.shape
    half = cols // 2
    tr = _row_tile(rows)

    def body(w_ref, g_ref, m_ref, v_ref, go_ref, d_ref, mo_ref, vo_ref):
        gv = g_ref[...]
        go_ref[...] = gv
        m_new = ADAM_B1 * m_ref[...] + (1.0 - ADAM_B1) * gv
        v_new = ADAM_B2 * v_ref[...] + (1.0 - ADAM_B2) * jnp.square(gv)
        m_hat = m_new / (1.0 - ADAM_B1 ** ADAM_STEP)
        v_hat = v_new / (1.0 - ADAM_B2 ** ADAM_STEP)
        d_ref[...] = -ADAM_LR * (m_hat / (jnp.sqrt(v_hat) + ADAM_EPS) + ADAM_WD * w_ref[...])
        mo_ref[...] = m_new
        vo_ref[...] = v_new

    blk = pl.BlockSpec((tr, half), lambda hf, i: (i, hf))
    gsp = pl.BlockSpec((None, tr, half), lambda hf, i: (hf, i, 0))
    shp = jax.ShapeDtypeStruct((rows, cols), F32)
    return pl.pallas_call(body, grid=(2, rows // tr), in_specs=[blk, gsp, blk, blk], out_specs=[blk] * 4,
                          out_shape=[shp] * 4, name=name, compiler_params=_params())(w, g2, m, v)


SHARD_W = IN_WIDTH // N_CHIPS


def _half_major(a):
    r, c = a.shape
    return a.reshape(N_CHIPS, 2, r // N_CHIPS // 2, c).transpose(1, 0, 2, 3)


def kernel(x, mem, g_pre, w_in, w_conv, attn_sink, g_mem, w_mem_kv, w_up_a, w_up_b, w_up_m, w_out, g_post, loss_target, m_g_pre, m_w_in, m_w_conv, m_attn_sink, m_g_mem, m_w_mem_kv, m_w_up_a, m_w_up_b, m_w_up_m, m_w_out, m_g_post, v_g_pre, v_w_in, v_w_conv, v_attn_sink, v_g_mem, v_w_mem_kv, v_w_up_a, v_w_up_b, v_w_up_m, v_w_out, v_g_post):
    xi, yi, ci = _position()
    chip = 2 * xi + yi
    where = jnp.stack([chip, ci, N_CHIPS - 1 - chip]).astype(jnp.int32)

    own = [w_in[0].T.astype(BF16), w_mem_kv[0].astype(BF16),
           jnp.concatenate([w_up_a[0], w_up_b[0], w_up_m[0]], axis=0).astype(BF16), w_out[0].astype(BF16)]
    own_conv = jnp.pad(w_conv[0], ((0, 5), (0, 0)))

    def pieces(mine, got):
        got = lax.dynamic_update_slice_in_dim(got, mine[None], chip, axis=0)
        return [got[j] for j in range(N_CHIPS)]

    diag = N_CHIPS - 1 - chip
    proj, h, h_t, got_near, got_conv, got_far = _proj_near(x[0], g_pre, own[0], own_conv, where)
    w_near = lax.dynamic_update_slice_in_dim(got_near, own[0][None], chip, axis=0).reshape(IN_WIDTH, D_MODEL)
    far = lax.dynamic_index_in_dim(got_far, diag, 0, keepdims=False)
    proj = _proj_far(h, w_near, far, where, into=proj)
    w_conv_full = jnp.concatenate([p[:3] for p in pieces(own_conv, got_conv)], axis=1)

    def late_weights(gathered):
        w_kv_full = jnp.concatenate(pieces(own[1], gathered[0]), axis=0)
        up_pieces = pieces(own[2], gathered[1])
        w_up_full = jnp.stack([jnp.concatenate([p[k * A_WIDTH:(k + 1) * A_WIDTH] for p in up_pieces], axis=1)
                               for k in range(3)])
        return w_kv_full, w_up_full, jnp.concatenate(pieces(own[3], gathered[2]), axis=0)

    def pick(parts, hf):
        return [lax.dynamic_index_in_dim(p, hf, 0, keepdims=False) for p in parts]

    def up_out_parts(dw_up, dw_out):
        up = (dw_up.reshape(3, A_WIDTH, N_CHIPS, D_MODEL // N_CHIPS).transpose(2, 0, 1, 3)
              .reshape(N_CHIPS, 2, 3 * A_WIDTH // 2, D_MODEL // N_CHIPS).transpose(1, 0, 2, 3))
        return [up.astype(BF16), _half_major(dw_out).astype(BF16)]

    g = _forward_backward(x[0], mem[0], loss_target[0], proj, w_conv_full, attn_sink, g_mem,
                          (_gather_weights(own[1:]), late_weights), g_post,
                          lambda dw_up, dw_out: _pair_exchange(pick(up_out_parts(dw_up, dw_out), 1 - ci)),
                          lambda dw_kv: _pair_exchange(pick([_half_major(dw_kv).astype(BF16)], 1 - ci)))

    half_rows = D_MODEL // 2

    def dw_in_half(half_of, name, carry):
        dw, carried = _dw_in_t(g["dproj"], h_t, half_of=half_of, where=where, name=name, carry=carry)
        return dw.reshape(N_CHIPS, SHARD_W, half_rows), carried

    small_keep = pick([_half_major(g["w_kv"]).astype(BF16)] + up_out_parts(g["w_up"], g["w_out"]), ci)
    small_names = ["w_kv", "w_up", "w_out"]
    sums_small = [_pair_add(k, r, "pair_add_" + nm)
                  for k, r, nm in zip(small_keep, g["early_kv"] + g["early"], small_names)]
    dw_send, recv3_small = dw_in_half(lambda w: 1 - w[1], "dw_in_send", _chip_exchange(sums_small))
    dw_keep, (recv_in,) = dw_in_half(lambda w: w[1], "dw_in_keep", _pair_exchange([dw_send]))
    sum_in = _pair_add(dw_keep, recv_in, "pair_add_w_in")
    d_h, (recv3_in,) = _d_h(g["dproj"], w_near, far, where, carry=_chip_exchange([sum_in]))
    pairs = [_chip_add(s, r, where, "chip_add_" + nm)
             for s, r, nm in zip([sum_in] + sums_small, [recv3_in] + recv3_small, ["w_in"] + small_names)]
    grad_x, dg_pre = _rmsnorm_bwd(d_h, x[0], g_pre, g["dy"], name="pre_norm_bwd")

    zeros512 = jnp.zeros((1, D_MODEL - A_WIDTH), F32)
    conv_rows = [jnp.concatenate([g["w_conv"][k:k + 1], zeros512], axis=1) for k in range(3)]
    sink_row = jnp.pad(g["sink"][:, 0].reshape(1, N_Q_HEADS), ((0, 0), (0, D_MODEL - N_Q_HEADS)))
    loss_row = jnp.pad(g["loss"], ((0, 0), (0, D_MODEL - LANES)))
    pack = jnp.concatenate([dg_pre, g["g_mem"], g["g_post"]] + conv_rows + [sink_row, loss_row], axis=0)
    red, full = _small_allreduce(pack, _pair_share(pairs))
    loss = red[7, 0]
    small_grads = dict(
        g_pre=red[0:1], g_mem=red[1:2], g_post=red[2:3], attn_sink=red[6:7, :N_Q_HEADS],
        w_conv=lax.dynamic_slice(red[3:6, :A_WIDTH], (0, chip * LANES), (3, LANES)))

    gw_up = full[2].reshape(3, A_WIDTH, D_MODEL // N_CHIPS)
    grads = dict(small_grads, w_mem_kv=full[1].reshape(D_MODEL // N_CHIPS, 2 * MEM_WIDTH),
                 w_up_a=gw_up[0], w_up_b=gw_up[1], w_up_m=gw_up[2],
                 w_out=full[3].reshape(D_MODEL // N_CHIPS, D_MODEL))

    weights = dict(g_pre=g_pre, w_in=w_in, w_conv=w_conv, attn_sink=attn_sink, g_mem=g_mem, w_mem_kv=w_mem_kv,
                   w_up_a=w_up_a, w_up_b=w_up_b, w_up_m=w_up_m, w_out=w_out, g_post=g_post)
    m_in = dict(g_pre=m_g_pre, w_in=m_w_in, w_conv=m_w_conv, attn_sink=m_attn_sink, g_mem=m_g_mem,
                w_mem_kv=m_w_mem_kv, w_up_a=m_w_up_a, w_up_b=m_w_up_b, w_up_m=m_w_up_m, w_out=m_w_out,
                g_post=m_g_post)
    v_in = dict(g_pre=v_g_pre, w_in=v_w_in, w_conv=v_w_conv, attn_sink=v_attn_sink, g_mem=v_g_mem,
                w_mem_kv=v_w_mem_kv, w_up_a=v_w_up_a, w_up_b=v_w_up_b, w_up_m=v_w_up_m, w_out=v_w_out,
                g_post=v_g_post)
    out_g, out_d, out_m, out_v = [], [], [], []
    for nm in ("g_pre", "w_in", "w_conv", "attn_sink", "g_mem", "w_mem_kv", "w_up_a", "w_up_b", "w_up_m", "w_out",
               "g_post"):
        shape = weights[nm].shape
        if nm == "w_in":
            results = _adamw_halves(w_in[0].T, full[0], m_w_in[0].T, v_w_in[0].T, "adamw_w_in")
            for out, t in zip((out_g, out_d, out_m, out_v), results):
                out.append(t.T.reshape(shape))
            continue
        two_d = shape[-2:]
        gr = grads[nm].reshape(two_d)
        d, m_new, v_new = _adamw(weights[nm].reshape(two_d), gr, m_in[nm].reshape(two_d), v_in[nm].reshape(two_d),
                                 "adamw_" + nm)
        out_g.append(gr.reshape(shape))
        out_d.append(d.reshape(shape))
        out_m.append(m_new.reshape(shape))
        out_v.append(v_new.reshape(shape))
    return (loss, grad_x.reshape(x.shape), *out_g, *out_d, *out_m, *out_v)
```

```python
import jax
import jax.numpy as jnp
from jax import lax
from jax.experimental import pallas as pl
from jax.experimental.pallas import tpu as pltpu

F32 = jnp.float32
BF16 = jnp.bfloat16
MESH = pl.DeviceIdType.MESH

D_MODEL = 1024
EPS = 1e-6
A_WIDTH = 512
HEAD_DIM = 64
N_Q_HEADS = 8
WINDOW_BLOCK = 128
KV_PAD = 512
ROPE_THETA = 500000.0
ROT_DIM = 16
MEM_HEADS = 4
MEM_HEAD_DIM = 128
MEM_WIDTH = 512
IN_WIDTH = 7424
N_CHIPS = 4
LANES = 128
HALF_LANES = 64

PERM_SEGS = ((0, 2560), (2816, 3328), (4352, 7424), (3328, 4352), (2560, 2816))
COL_A, W_A = 0, 2048
COL_B, W_B = 2, 1024
COL_G, W_G = 1, 3072
COL_M, W_M = 6, 1024
COL_KV, W_KV = 28, 256

ADAM_LR = 0.001
ADAM_B1 = 0.9
ADAM_B2 = 0.999
ADAM_EPS = 1e-08
ADAM_WD = 0.01
ADAM_STEP = 10

VMEM_LIGHT_BYTES = 48 * 1024 * 1024
VMEM_HEAVY_BYTES = 48 * 1024 * 1024


_HBM = pl.BlockSpec(memory_space=pltpu.HBM)


def _params(heavy=False):
    return pltpu.CompilerParams(vmem_limit_bytes=VMEM_HEAVY_BYTES if heavy else VMEM_LIGHT_BYTES)


def _sigmoid(v):
    return jax.nn.sigmoid(v)


_DIMS = {"nn": (((1,), (0,)), ((), ())), "nt": (((1,), (1,)), ((), ())), "tn": (((0,), (0,)), ((), ()))}


class _Carry:
    def __init__(self, ins, out_shapes, sems, start, finish, aliases=None):
        self.ins, self.out_shapes, self.sems = list(ins), list(out_shapes), list(sems)
        self.start, self.finish, self.aliases = start, finish, dict(aliases or {})


def _join(*carries):
    def split(seq, counts):
        pos, parts = 0, []
        for n in counts:
            parts.append(seq[pos:pos + n])
            pos += n
        return parts

    n_in = [len(c.ins) for c in carries]
    n_out = [len(c.out_shapes) for c in carries]
    n_sem = [len(c.sems) for c in carries]

    def run(which):
        def go(ins, outs, sems):
            for c, i, o, sm in zip(carries, split(ins, n_in), split(outs, n_out), split(sems, n_sem)):
                getattr(c, which)(i, o, sm)
        return go

    aliases = {}
    for k, c in enumerate(carries):
        aliases.update({sum(n_in[:k]) + i: sum(n_out[:k]) + o for i, o in c.aliases.items()})
    return _Carry([a for c in carries for a in c.ins], [sh for c in carries for sh in c.out_shapes],
                  [sm for c in carries for sm in c.sems], run("start"), run("finish"), aliases)


def _carried_call(body, carry, *, grid, in_specs, out_specs, out_shape, scratch, operands, name, prefetch=None,
                  aliases=None, heavy=False):
    n_in, n_out, n_scr = len(in_specs), len(out_specs), len(scratch)
    c_in = len(carry.ins) if carry else 0
    c_out = len(carry.out_shapes) if carry else 0
    n_pre = 0 if prefetch is None else 1
    steps = 1
    for g in grid:
        steps *= g

    def wrapped(*refs):
        refs = refs[n_pre:]
        ins, cins = refs[:n_in], refs[n_in:n_in + c_in]
        outs = refs[n_in + c_in:n_in + c_in + n_out]
        couts = refs[n_in + c_in + n_out:n_in + c_in + n_out + c_out]
        rest = refs[n_in + c_in + n_out + c_out:]
        scr, sems = rest[:n_scr], rest[n_scr:]
        if carry:
            step = pl.program_id(0)
            for ax in range(1, len(grid)):
                step = step * grid[ax] + pl.program_id(ax)

            @pl.when(step == 0)
            def _():
                carry.start(cins, couts, sems)

        body(ins, outs, scr)
        if carry:
            @pl.when(step == steps - 1)
            def _():
                carry.finish(cins, couts, sems)

    all_aliases = {n_pre + i: o for i, o in (aliases or {}).items()}
    if carry:
        all_aliases.update({n_pre + n_in + i: n_out + o for i, o in carry.aliases.items()})
    all_in = list(in_specs) + [_HBM] * c_in
    all_out = list(out_specs) + [_HBM] * c_out
    all_scratch = list(scratch) + (carry.sems if carry else [])
    if n_pre:
        spec = dict(grid_spec=pltpu.PrefetchScalarGridSpec(num_scalar_prefetch=1, grid=grid, in_specs=all_in,
                                                           out_specs=all_out, scratch_shapes=all_scratch))
        pre = (prefetch,)
    else:
        spec = dict(grid=grid, in_specs=all_in, out_specs=all_out, scratch_shapes=all_scratch)
        pre = ()
    results = pl.pallas_call(
        wrapped, out_shape=list(out_shape) + (carry.out_shapes if carry else []), input_output_aliases=all_aliases,
        name=name, compiler_params=_params(heavy), **spec)(*pre, *operands, *(carry.ins if carry else []))
    return list(results[:n_out]), list(results[n_out:])


def _matmul(a, b, *, mode, out_dtype, tm, tn, tk, name):
    if mode == "nn":
        (m, k), (_, n) = a.shape, b.shape
    elif mode == "nt":
        (m, k), (n, _) = a.shape, b.shape
    else:
        (k, m), (_, n) = a.shape, b.shape
    tm, tn, tk = min(tm, m), min(tn, n), min(tk, k)
    assert m % tm == 0 and n % tn == 0 and k % tk == 0
    nk = k // tk
    dims = _DIMS[mode]

    if mode == "nn":
        a_spec = pl.BlockSpec((tm, tk), lambda i, j, kk: (i, kk))
        b_spec = pl.BlockSpec((tk, tn), lambda i, j, kk: (kk, j))
    elif mode == "nt":
        a_spec = pl.BlockSpec((tm, tk), lambda i, j, kk: (i, kk))
        b_spec = pl.BlockSpec((tn, tk), lambda i, j, kk: (j, kk))
    else:
        a_spec = pl.BlockSpec((tk, tm), lambda i, j, kk: (kk, i))
        b_spec = pl.BlockSpec((tk, tn), lambda i, j, kk: (kk, j))
    o_spec = pl.BlockSpec((tm, tn), lambda i, j, kk: (i, j))

    def part(a_ref, b_ref):
        return lax.dot_general(a_ref[...].astype(BF16), b_ref[...].astype(BF16), dims,
                               preferred_element_type=F32)

    if nk == 1:
        def body(a_ref, b_ref, o_ref):
            o_ref[...] = part(a_ref, b_ref).astype(out_dtype)
        scratch = []
    else:
        def body(a_ref, b_ref, o_ref, acc_ref):
            kk = pl.program_id(2)

            @pl.when(kk == 0)
            def _():
                acc_ref[...] = part(a_ref, b_ref)

            @pl.when(kk > 0)
            def _():
                acc_ref[...] += part(a_ref, b_ref)

            @pl.when(kk == nk - 1)
            def _():
                o_ref[...] = acc_ref[...].astype(out_dtype)
        scratch = [pltpu.VMEM((tm, tn), F32)]

    return pl.pallas_call(
        body, grid=(m // tm, n // tn, nk), in_specs=[a_spec, b_spec], out_specs=o_spec,
        out_shape=jax.ShapeDtypeStruct((m, n), out_dtype), scratch_shapes=scratch,
        name=name, compiler_params=_params())(a, b)


IN_BLOCK = 256
N_IN_BLOCKS = IN_WIDTH // IN_BLOCK
SHARD_BLOCKS = (IN_WIDTH // N_CHIPS) // IN_BLOCK
BLOCK_RUNS = tuple((a // IN_BLOCK, sum(d - c for c, d in PERM_SEGS[:k]) // IN_BLOCK, (b - a) // IN_BLOCK)
                   for k, (a, b) in enumerate(PERM_SEGS))


def _perm_block(r):
    p = r
    for ref0, perm0, n in BLOCK_RUNS:
        p = jnp.where((r >= ref0) & (r < ref0 + n), r - ref0 + perm0, p)
    return p


def _proj_near(x, g_pre, own_w, small, where):
    s, d = x.shape
    norm_tile = min(512, s)
    n_own = SHARD_BLOCKS - 1
    n_diag = SHARD_BLOCKS + 1
    n_blocks = N_IN_BLOCKS - n_diag
    piece = IN_WIDTH // N_CHIPS - SHARD_BLOCKS * IN_BLOCK
    near = _gather_weights([own_w], small, relations=(0, 1))
    far = _gather_weights([own_w], relations=(2,))
    both = _join(near, far)
    n_cin, n_cout = len(both.ins), len(both.out_shapes)

    def block_of(i, w):
        me, dg = w[0], w[2]
        own0 = SHARD_BLOCKS * me + jnp.minimum(me, 1)
        dg0 = SHARD_BLOCKS * dg
        lo0, hi0 = jnp.minimum(own0, dg0), jnp.maximum(own0, dg0)
        lo_n = jnp.where(own0 < dg0, n_own, n_diag)
        hi_n = jnp.where(own0 < dg0, n_diag, n_own)
        r = i - n_own
        r = r + lo_n * (r >= lo0).astype(jnp.int32)
        r = r + hi_n * (r >= hi0).astype(jnp.int32)
        return jnp.where(i < n_own, own0 + i, r)

    def body(w_ref, x_hbm, g_ref, own_hbm, *refs):
        cins, (o_ref, h_hbm, ht_hbm) = refs[:n_cin], refs[n_cin:n_cin + 3]
        couts = refs[n_cin + 3:n_cin + 3 + n_cout]
        blocks, block_sems, h_ref, x_tile, ht_tile, io_sem = refs[n_cin + 3 + n_cout:n_cin + 9 + n_cout]
        sems = refs[n_cin + 9 + n_cout:]
        near_refs = (cins[:len(near.ins)], couts[:len(near.out_shapes)], sems[:len(near.sems)])
        far_refs = (cins[len(near.ins):], couts[len(near.out_shapes):], sems[len(near.sems):])
        gathered = couts[0]
        i = pl.program_id(0)
        me = w_ref[0]

        def fetch(step, slot):
            r = block_of(step, w_ref)
            for p in range(IN_BLOCK // piece):
                row = r * IN_BLOCK + p * piece
                j = row // (IN_WIDTH // N_CHIPS)
                off = pl.multiple_of(row - j * (IN_WIDTH // N_CHIPS), BF16_SUBLANES)
                dst = blocks.at[slot, pl.ds(p * piece, piece)]

                @pl.when(j == me)
                def _():
                    pltpu.make_async_copy(own_hbm.at[pl.ds(off, piece)], dst, block_sems.at[slot]).start()

                @pl.when(j != me)
                def _():
                    pltpu.make_async_copy(gathered.at[j, pl.ds(off, piece)], dst, block_sems.at[slot]).start()

        def arrived(slot):
            pltpu.make_async_copy(own_hbm.at[pl.ds(0, IN_BLOCK)], blocks.at[slot], block_sems.at[slot]).wait()

        slot = i % 2

        def norm_rows(k):
            rows = pl.ds(k * norm_tile, norm_tile)
            pltpu.sync_copy(x_hbm.at[rows], x_tile)
            xv = x_tile[...]
            hv = (xv * lax.rsqrt(jnp.mean(xv * xv, axis=-1, keepdims=True) + EPS)) * g_ref[...]
            h_ref[rows, :] = hv.astype(BF16)
            ht_tile[...] = hv.T.astype(BF16)
            to_h = pltpu.make_async_copy(h_ref.at[rows], h_hbm.at[rows], io_sem.at[0])
            to_ht = pltpu.make_async_copy(ht_tile, ht_hbm.at[:, rows], io_sem.at[1])
            to_h.start()
            to_ht.start()
            to_h.wait()
            to_ht.wait()

        @pl.when(i == 0)
        def _():
            near.start(*near_refs)
            fetch(i, slot)
            for k in range(s // norm_tile):
                norm_rows(k)

        @pl.when(i == n_own)
        def _():
            near.finish(*near_refs)
            far.start(*far_refs)
            fetch(i, slot)

        arrived(slot)

        @pl.when((i + 1 < n_blocks) & (i + 1 != n_own))
        def _():
            fetch(i + 1, 1 - slot)

        o_ref[...] = lax.dot_general(h_ref[...], blocks[slot], _DIMS["nt"], preferred_element_type=F32)

        @pl.when(i == n_blocks - 1)
        def _():
            far.finish(*far_refs)

    anysp = pl.BlockSpec(memory_space=pl.ANY)
    grid_spec = pltpu.PrefetchScalarGridSpec(
        num_scalar_prefetch=1, grid=(n_blocks,),
        in_specs=[anysp, pl.BlockSpec((1, d), lambda i, w: (0, 0)), anysp] + [_HBM] * n_cin,
        out_specs=[pl.BlockSpec((s, IN_BLOCK), lambda i, w: (0, _perm_block(block_of(i, w)))), anysp, anysp]
        + [_HBM] * n_cout,
        scratch_shapes=[pltpu.VMEM((2, IN_BLOCK, d), BF16), pltpu.SemaphoreType.DMA((2,)), pltpu.VMEM((s, d), BF16),
                        pltpu.VMEM((norm_tile, d), F32), pltpu.VMEM((d, norm_tile), BF16),
                        pltpu.SemaphoreType.DMA((2,))] + both.sems)
    return pl.pallas_call(
        body, grid_spec=grid_spec,
        out_shape=[jax.ShapeDtypeStruct((s, IN_WIDTH), F32), jax.ShapeDtypeStruct((s, d), BF16),
                   jax.ShapeDtypeStruct((d, s), BF16)] + both.out_shapes,
        name="proj_near", compiler_params=_params())(where, x, g_pre, own_w, *both.ins)


def _proj_far(h, w_near, far, where, *, into, carry=None):
    s, d = h.shape
    n_blocks = SHARD_BLOCKS + 1
    lead = IN_WIDTH // N_CHIPS - SHARD_BLOCKS * IN_BLOCK

    def body(ins, outs, scr):
        where_ref, h_ref, w_hbm, far_hbm, _ = ins
        win, sem = scr
        i = pl.program_id(0)

        @pl.when(i == 0)
        def _():
            dg = where_ref[2]
            rows = pl.ds(pl.multiple_of(dg * (SHARD_BLOCKS * IN_BLOCK), IN_BLOCK), n_blocks * IN_BLOCK)
            window = pltpu.make_async_copy(w_hbm.at[rows], win, sem)
            window.start()
            window.wait()
            shard = pltpu.make_async_copy(far_hbm, win.at[pl.ds(pl.multiple_of(dg * lead, BF16_SUBLANES), SHARD_W)], sem)
            shard.start()
            shard.wait()

        blk = win[pl.ds(pl.multiple_of(i * IN_BLOCK, IN_BLOCK), IN_BLOCK), :]
        outs[0][...] = lax.dot_general(h_ref[...], blk, _DIMS["nt"], preferred_element_type=F32)

    anysp = pl.BlockSpec(memory_space=pl.ANY)
    (proj,), carried = _carried_call(
        body, carry, grid=(n_blocks,),
        in_specs=[pl.BlockSpec(memory_space=pltpu.SMEM), pl.BlockSpec((s, d), lambda i, w: (0, 0)), anysp, anysp, anysp],
        out_specs=[pl.BlockSpec((s, IN_BLOCK), lambda i, w: (0, _perm_block(i + SHARD_BLOCKS * w[2])))],
        out_shape=[jax.ShapeDtypeStruct((s, IN_WIDTH), F32)],
        scratch=[pltpu.VMEM((n_blocks * IN_BLOCK, d), BF16), pltpu.SemaphoreType.DMA],
        operands=(where, h, w_near, far, into), name="proj_far", prefetch=where, aliases={4: 0})
    return (proj, carried) if carry else proj


def _dw_in_t(dproj, h_t, *, half_of, where, name, carry=None):
    d, s = h_t.shape
    c = d // 2

    def body(ins, outs, scr):
        outs[0][...] = lax.dot_general(ins[1][...], ins[0][...], _DIMS["nn"], preferred_element_type=F32).T.astype(BF16)

    (dw,), carried = _carried_call(
        body, carry, grid=(N_IN_BLOCKS,),
        in_specs=[pl.BlockSpec((s, IN_BLOCK), lambda r, w: (0, _perm_block(r))),
                  pl.BlockSpec((c, s), lambda r, w: (half_of(w), 0))],
        out_specs=[pl.BlockSpec((IN_BLOCK, c), lambda r, w: (r, 0))],
        out_shape=[jax.ShapeDtypeStruct((IN_WIDTH, c), BF16)], scratch=[], operands=(dproj, h_t), name=name,
        prefetch=where)
    return (dw, carried) if carry else dw


def _d_h(dproj, w_near, far, where, *, carry=None):
    s = dproj.shape[0]
    d = w_near.shape[1]
    tm = min(s, 256)

    def body(ins, outs, scr):
        where_ref, a_ref, w_hbm, far_hbm = ins
        w_ref, sem = scr

        @pl.when(pl.program_id(0) == 0)
        def _():
            whole = pltpu.make_async_copy(w_hbm, w_ref, sem)
            whole.start()
            whole.wait()
            rows = pl.ds(pl.multiple_of(where_ref[2] * SHARD_W, BF16_SUBLANES), SHARD_W)
            part = pltpu.make_async_copy(far_hbm, w_ref.at[rows], sem)
            part.start()
            part.wait()

        acc = None
        for ref0, perm0, n in BLOCK_RUNS:
            term = jnp.dot(a_ref[:, perm0 * IN_BLOCK:(perm0 + n) * IN_BLOCK],
                           w_ref[ref0 * IN_BLOCK:(ref0 + n) * IN_BLOCK, :], preferred_element_type=F32)
            acc = term if acc is None else acc + term
        outs[0][...] = acc

    anysp = pl.BlockSpec(memory_space=pl.ANY)
    (dh,), carried = _carried_call(
        body, carry, grid=(s // tm,),
        in_specs=[pl.BlockSpec(memory_space=pltpu.SMEM), pl.BlockSpec((tm, IN_WIDTH), lambda i: (i, 0)), anysp, anysp],
        out_specs=[pl.BlockSpec((tm, d), lambda i: (i, 0))],
        out_shape=[jax.ShapeDtypeStruct((s, d), F32)],
        scratch=[pltpu.VMEM((IN_WIDTH, d), BF16), pltpu.SemaphoreType.DMA],
        operands=(where, dproj, w_near, far), name="d_h", heavy=True)
    return (dh, carried) if carry else dh


def _rmsnorm_fwd(x, g, *, name):
    s, d = x.shape
    ts = min(512, s)

    def body(x_ref, g_ref, o_ref):
        xv = x_ref[...]
        r = lax.rsqrt(jnp.mean(xv * xv, axis=-1, keepdims=True) + EPS)
        o_ref[...] = ((xv * r) * g_ref[...]).astype(BF16)

    return pl.pallas_call(
        body, grid=(s // ts,),
        in_specs=[pl.BlockSpec((ts, d), lambda i: (i, 0)), pl.BlockSpec((1, d), lambda i: (0, 0))],
        out_specs=pl.BlockSpec((ts, d), lambda i: (i, 0)),
        out_shape=jax.ShapeDtypeStruct((s, d), BF16), name=name, compiler_params=_params())(x, g)


def _rmsnorm_bwd(dh, x, g, res, *, name, carry=None):
    s, d = x.shape
    ts = min(256, s)

    def body(ins, outs, scr):
        dh_ref, x_ref, g_ref, res_ref = ins
        dx_ref, dg_ref = outs
        xv = x_ref[...]
        r = lax.rsqrt(jnp.mean(xv * xv, axis=-1, keepdims=True) + EPS)
        xh = xv * r
        dhv = dh_ref[...]
        part = jnp.sum(dhv * xh, axis=0, keepdims=True)

        @pl.when(pl.program_id(0) == 0)
        def _():
            dg_ref[...] = part

        @pl.when(pl.program_id(0) > 0)
        def _():
            dg_ref[...] += part

        dxh = dhv * g_ref[...]
        dx_ref[...] = res_ref[...] + r * (dxh - xh * jnp.mean(dxh * xh, axis=-1, keepdims=True))

    row = pl.BlockSpec((ts, d), lambda i: (i, 0))
    vec = pl.BlockSpec((1, d), lambda i: (0, 0))
    outs, carried = _carried_call(
        body, carry, grid=(s // ts,), in_specs=[row, row, vec, row], out_specs=[row, vec],
        out_shape=[jax.ShapeDtypeStruct((s, d), F32), jax.ShapeDtypeStruct((1, d), F32)],
        scratch=[], operands=(dh, x, g, res), name=name)
    return (*outs, carried) if carry else tuple(outs)


MID_TILE = 256


def _gated_branches(y_refs, wup_ref, gl):
    d = D_MODEL
    us = [jnp.dot(y_refs[k][...], wup_ref[k], preferred_element_type=F32) for k in range(3)]
    sg = [_sigmoid(gl[:, k * d:(k + 1) * d]) for k in range(3)]
    return us, sg


def _mid_fwd(ya, yb, ym, proj, x, tgt, w_up, w_out, g_post):
    s, d = x.shape
    ts = MID_TILE

    def body(ya_ref, yb_ref, ym_ref, g_ref, x_ref, t_ref, wup_ref, wout_ref, gp_ref,
             m_ref, do_ref, dy_ref, dg_ref, loss_ref):
        us, sg = _gated_branches((ya_ref, yb_ref, ym_ref), wup_ref, g_ref[...])
        merged = (sg[0] * us[0] + sg[1] * us[1] + sg[2] * us[2]).astype(BF16)
        m_ref[...] = merged
        ov = jnp.dot(merged, wout_ref[...], preferred_element_type=F32)
        r = lax.rsqrt(jnp.mean(ov * ov, axis=-1, keepdims=True) + EPS)
        nh = ov * r
        gv = gp_ref[...]
        e = (x_ref[...] + nh * gv) - t_ref[...]
        lpart = 0.5 * jnp.sum(jnp.mean(e * e, axis=-1, keepdims=True), axis=0, keepdims=True)
        dy = e * (1.0 / d)
        dgp = jnp.sum(dy * nh, axis=0, keepdims=True)

        @pl.when(pl.program_id(0) == 0)
        def _():
            dg_ref[...] = dgp
            loss_ref[...] = jnp.broadcast_to(lpart, loss_ref.shape)

        @pl.when(pl.program_id(0) > 0)
        def _():
            dg_ref[...] += dgp
            loss_ref[...] += jnp.broadcast_to(lpart, loss_ref.shape)

        dn = dy * gv
        dy_ref[...] = dy
        do_ref[...] = (r * (dn - nh * jnp.mean(dn * nh, axis=-1, keepdims=True))).astype(BF16)

    row = pl.BlockSpec((ts, d), lambda i: (i, 0))
    ysp = pl.BlockSpec((ts, A_WIDTH), lambda i: (i, 0))
    vec = pl.BlockSpec((1, d), lambda i: (0, 0))
    return pl.pallas_call(
        body, grid=(s // ts,),
        in_specs=[ysp, ysp, ysp, pl.BlockSpec((ts, W_G), lambda i: (i, COL_G)), row, row,
                  pl.BlockSpec((3, A_WIDTH, d), lambda i: (0, 0, 0)), pl.BlockSpec((d, d), lambda i: (0, 0)), vec],
        out_specs=[row, row, row, vec, pl.BlockSpec((1, LANES), lambda i: (0, 0))],
        out_shape=[jax.ShapeDtypeStruct((s, d), BF16), jax.ShapeDtypeStruct((s, d), BF16),
                   jax.ShapeDtypeStruct((s, d), F32), jax.ShapeDtypeStruct((1, d), F32),
                   jax.ShapeDtypeStruct((1, LANES), F32)],
        name="mid_fwd", compiler_params=_params(heavy=True))(ya, yb, ym, proj, x, tgt, w_up, w_out, g_post)


def _mid_bwd(d_out, merged, ya, yb, ym, proj, w_up, w_out):
    s, d = merged.shape
    ts = MID_TILE
    last = s // ts - 1

    def body(do_ref, m_ref, ya_ref, yb_ref, ym_ref, g_ref, wup_ref, wout_ref,
             dp_ref, dya_ref, dyb_ref, dym_ref, dwup_hbm, dwout_hbm, dwup_acc, dwout_acc):
        i = pl.program_id(0)

        @pl.when(i == 0)
        def _():
            dwup_acc[...] = jnp.zeros_like(dwup_acc)
            dwout_acc[...] = jnp.zeros_like(dwout_acc)

        y_refs = (ya_ref, yb_ref, ym_ref)
        us, sg = _gated_branches(y_refs, wup_ref, g_ref[...])
        dov = do_ref[...]
        dwout_acc[...] += lax.dot_general(m_ref[...], dov, _DIMS["tn"], preferred_element_type=F32)
        dm = lax.dot_general(dov, wout_ref[...], _DIMS["nt"], preferred_element_type=F32)
        for k, dy_ref in enumerate((dya_ref, dyb_ref, dym_ref)):
            dp_ref[:, k * d:(k + 1) * d] = ((dm * us[k]) * (sg[k] * (1.0 - sg[k]))).astype(BF16)
            du = (sg[k] * dm).astype(BF16)
            dy_ref[...] = lax.dot_general(du, wup_ref[k], _DIMS["nt"], preferred_element_type=F32)
            dwup_acc[k] += lax.dot_general(y_refs[k][...], du, _DIMS["tn"], preferred_element_type=F32)

        @pl.when(i == last)
        def _():
            pltpu.sync_copy(dwup_acc, dwup_hbm)
            pltpu.sync_copy(dwout_acc, dwout_hbm)

    row = pl.BlockSpec((ts, d), lambda i: (i, 0))
    ysp = pl.BlockSpec((ts, A_WIDTH), lambda i: (i, 0))
    gsp = pl.BlockSpec((ts, W_G), lambda i: (i, COL_G))
    anysp = pl.BlockSpec(memory_space=pl.ANY)
    yshape = jax.ShapeDtypeStruct((s, A_WIDTH), F32)
    return pl.pallas_call(
        body, grid=(s // ts,),
        in_specs=[row, row, ysp, ysp, ysp, gsp, pl.BlockSpec((3, A_WIDTH, d), lambda i: (0, 0, 0)),
                  pl.BlockSpec((d, d), lambda i: (0, 0))],
        out_specs=[gsp, ysp, ysp, ysp, anysp, anysp],
        out_shape=[jax.ShapeDtypeStruct((s, IN_WIDTH), BF16), yshape, yshape, yshape,
                   jax.ShapeDtypeStruct((3, A_WIDTH, d), F32), jax.ShapeDtypeStruct((d, d), F32)],
        scratch_shapes=[pltpu.VMEM((3, A_WIDTH, d), F32), pltpu.VMEM((d, d), F32)],
        name="mid_bwd", compiler_params=_params(heavy=True))(d_out, merged, ya, yb, ym, proj, w_up, w_out)


def _conv_core(blk, prev, nxt, w, i, last, ts):
    c = A_WIDTH
    ab, ac, ax, az = blk[:, :c], blk[:, c:2 * c], blk[:, 2 * c:3 * c], blk[:, 3 * c:]
    cu = ac * ax
    cu_prev = (prev[7:8, c:2 * c] * prev[7:8, 2 * c:3 * c]) * jnp.where(i > 0, 1.0, 0.0)
    cu_next = (nxt[0:1, c:2 * c] * nxt[0:1, 2 * c:3 * c]) * jnp.where(i < last, 1.0, 0.0)
    row = lax.broadcasted_iota(jnp.int32, (ts, c), 0)
    cm1 = jnp.where(row == 0, cu_prev, pltpu.roll(cu, 1, 0))
    cp1 = jnp.where(row == ts - 1, cu_next, pltpu.roll(cu, ts - 1, 0))
    yc = cm1 * w[0:1] + cu * w[1:2] + cp1 * w[2:3]
    return ab, ac, ax, az, cu, cm1, cp1, yc, row


def _halo_specs(ts, width, col, nblk8):
    prev = pl.BlockSpec((8, width), lambda i: (jnp.maximum(i * (ts // 8) - 1, 0), col))
    nxt = pl.BlockSpec((8, width), lambda i: (jnp.minimum((i + 1) * (ts // 8), nblk8 - 1), col))
    return prev, nxt


def _conv_fwd(proj, w_conv):
    s = proj.shape[0]
    ts = 256
    last = s // ts - 1

    def body(a_ref, ap_ref, an_ref, w_ref, ya_ref):
        i = pl.program_id(0)
        ab, _, _, az, _, _, _, yc, _ = _conv_core(a_ref[...], ap_ref[...], an_ref[...], w_ref[...], i, last, ts)
        ya_ref[...] = ((ab * yc) * (az * _sigmoid(az))).astype(BF16)

    prev, nxt = _halo_specs(ts, W_A, COL_A, s // 8)
    return pl.pallas_call(
        body, grid=(s // ts,),
        in_specs=[pl.BlockSpec((ts, W_A), lambda i: (i, COL_A)), prev, nxt,
                  pl.BlockSpec((3, A_WIDTH), lambda i: (0, 0))],
        out_specs=pl.BlockSpec((ts, A_WIDTH), lambda i: (i, 0)),
        out_shape=jax.ShapeDtypeStruct((s, A_WIDTH), BF16), name="conv_fwd",
        compiler_params=_params())(proj, proj, proj, w_conv)


def _conv_bwd(proj, w_conv, dya, dproj):
    s = proj.shape[0]
    ts = 256
    last = s // ts - 1
    c = A_WIDTH

    def body(a_ref, ap_ref, an_ref, w_ref, d_ref, dp_ref, dn_ref, _, dproj_ref, dw_ref):
        i = pl.program_id(0)
        w = w_ref[...]
        prev, nxt = ap_ref[...], an_ref[...]
        ab, ac, ax, az, cu, cm1, cp1, yc, row = _conv_core(a_ref[...], prev, nxt, w, i, last, ts)
        sg = _sigmoid(az)
        sz = az * sg
        dya_v = d_ref[...]
        dyc = dya_v * sz * ab
        dproj_ref[:, :c] = (dya_v * sz * yc).astype(BF16)
        dproj_ref[:, 3 * c:] = (dya_v * (ab * yc) * (sg * (1.0 + az * (1.0 - sg)))).astype(BF16)

        def halo_dyc(a_row, d_row):
            azr = a_row[:, 3 * c:]
            return d_row * (azr * _sigmoid(azr)) * a_row[:, :c]

        dyc_prev = halo_dyc(prev[7:8], dp_ref[...][7:8]) * jnp.where(i > 0, 1.0, 0.0)
        dyc_next = halo_dyc(nxt[0:1], dn_ref[...][0:1]) * jnp.where(i < last, 1.0, 0.0)
        dyc_m1 = jnp.where(row == 0, dyc_prev, pltpu.roll(dyc, 1, 0))
        dyc_p1 = jnp.where(row == ts - 1, dyc_next, pltpu.roll(dyc, ts - 1, 0))
        dcu = dyc_p1 * w[0:1] + dyc * w[1:2] + dyc_m1 * w[2:3]
        dproj_ref[:, c:2 * c] = (dcu * ax).astype(BF16)
        dproj_ref[:, 2 * c:3 * c] = (dcu * ac).astype(BF16)
        dw = [jnp.sum(dyc * t, axis=0, keepdims=True) for t in (cm1, cu, cp1)]

        @pl.when(i == 0)
        def _():
            for k in range(3):
                dw_ref[k:k + 1, :] = dw[k]

        @pl.when(i > 0)
        def _():
            for k in range(3):
                dw_ref[k:k + 1, :] += dw[k]

    prev, nxt = _halo_specs(ts, W_A, COL_A, s // 8)
    dprev, dnxt = _halo_specs(ts, A_WIDTH, 0, s // 8)
    return pl.pallas_call(
        body, grid=(s // ts,),
        in_specs=[pl.BlockSpec((ts, W_A), lambda i: (i, COL_A)), prev, nxt,
                  pl.BlockSpec((3, A_WIDTH), lambda i: (0, 0)),
                  pl.BlockSpec((ts, A_WIDTH), lambda i: (i, 0)), dprev, dnxt,
                  pl.BlockSpec(memory_space=pl.ANY)],
        out_specs=[pl.BlockSpec((ts, W_A), lambda i: (i, COL_A)), pl.BlockSpec((3, A_WIDTH), lambda i: (0, 0))],
        out_shape=[jax.ShapeDtypeStruct(dproj.shape, BF16), jax.ShapeDtypeStruct((3, A_WIDTH), F32)],
        input_output_aliases={7: 0}, name="conv_bwd",
        compiler_params=_params())(proj, proj, proj, w_conv, dya, dya, dya, dproj)


def _rope_tables(s):
    half = ROT_DIM // 2
    dim = jnp.arange(LANES) % HEAD_DIM
    inv_freq = jnp.power(jnp.float32(ROPE_THETA), -(dim % half).astype(F32) * (2.0 / ROT_DIM))
    coarse = (jnp.arange(s // LANES) * LANES).astype(F32)[:, None] * inv_freq[None, :]
    fine = jnp.arange(LANES).astype(F32)[:, None] * inv_freq[None, :]
    cos_a, sin_a = jnp.cos(coarse)[:, None, :], jnp.sin(coarse)[:, None, :]
    cos_b, sin_b = jnp.cos(fine)[None], jnp.sin(fine)[None]
    cos = (cos_a * cos_b - sin_a * sin_b).reshape(s, LANES)
    sin = (sin_a * cos_b + cos_a * sin_b).reshape(s, LANES)
    first, second = (dim < half)[None, :], ((dim >= half) & (dim < ROT_DIM))[None, :]
    c = jnp.where(first | second, cos, 1.0)
    s1 = jnp.where(first, -sin, 0.0)
    s2 = jnp.where(second, sin, 0.0)
    return jnp.concatenate([c, s1, s2], axis=1)


def _rope(t, tab):
    return (t * tab[:, :LANES] + pltpu.roll(t, LANES - 8, 1) * tab[:, LANES:2 * LANES]
            + pltpu.roll(t, 8, 1) * tab[:, 2 * LANES:])


def _rope_transpose(dt, tab):
    return (dt * tab[:, :LANES] + pltpu.roll(dt * tab[:, LANES:2 * LANES], 8, 1)
            + pltpu.roll(dt * tab[:, 2 * LANES:], LANES - 8, 1))


def _rope_kv(proj, tab):
    s = proj.shape[0]
    nb = s // KV_PAD

    def body(kv_ref, t_ref, k_ref, v_ref):
        j = pl.program_id(0)
        inside = jnp.where((j > 0) & (j <= nb), 1.0, 0.0)
        kv = kv_ref[...]
        k_ref[...] = (_rope(kv[:, :LANES], t_ref[...]) * inside).astype(BF16)
        v_ref[...] = (kv[:, LANES:] * inside).astype(BF16)

    def src(j):
        return jnp.clip(j - 1, 0, nb - 1)

    o_spec = pl.BlockSpec((KV_PAD, LANES), lambda j: (j, 0))
    shp = jax.ShapeDtypeStruct((s + 2 * KV_PAD, LANES), BF16)
    return pl.pallas_call(
        body, grid=(nb + 2,),
        in_specs=[pl.BlockSpec((KV_PAD, W_KV), lambda j: (src(j), COL_KV)),
                  pl.BlockSpec((KV_PAD, 3 * LANES), lambda j: (src(j), 0))],
        out_specs=[o_spec, o_spec], out_shape=[shp, shp], name="rope_kv",
        compiler_params=_params())(proj, tab)


def _rope_kv_bwd(dkpad, dvpad, tab, dproj):
    s = tab.shape[0]
    nb = s // KV_PAD

    def body(dk_ref, dv_ref, t_ref, _, dp_ref):
        dp_ref[:, :LANES] = _rope_transpose(dk_ref[...], t_ref[...]).astype(BF16)
        dp_ref[:, LANES:] = dv_ref[...].astype(BF16)

    pad_spec = pl.BlockSpec((KV_PAD, LANES), lambda j: (j + 1, 0))
    return pl.pallas_call(
        body, grid=(nb,),
        in_specs=[pad_spec, pad_spec, pl.BlockSpec((KV_PAD, 3 * LANES), lambda j: (j, 0)),
                  pl.BlockSpec(memory_space=pl.ANY)],
        out_specs=pl.BlockSpec((KV_PAD, W_KV), lambda j: (j, COL_KV)),
        out_shape=jax.ShapeDtypeStruct(dproj.shape, BF16), input_output_aliases={3: 0},
        name="rope_kv_bwd", compiler_params=_params())(dkpad, dvpad, tab, dproj)


def _window_start(n):
    return pl.multiple_of((n - 1) * WINDOW_BLOCK + KV_PAD, WINDOW_BLOCK)


def _window_operands(k_ref, v_ref, n, lo):
    start = _window_start(n)
    kw = k_ref[pl.ds(start, 3 * WINDOW_BLOCK), :].astype(F32)
    vw = v_ref[pl.ds(start, 3 * WINDOW_BLOCK), :].astype(F32)
    kr, vr = pltpu.roll(kw, HALF_LANES, 1), pltpu.roll(vw, HALF_LANES, 1)
    k2 = (jnp.where(lo, kw, kr).astype(BF16), jnp.where(lo, kr, kw).astype(BF16))
    v2 = (jnp.where(lo, vw, vr).astype(BF16), jnp.where(lo, vr, vw).astype(BF16))
    return k2, v2


HEADS_PER_GROUP = 4
SWA_FWD_BLOCKS = 1
SWA_BWD_BLOCKS = 2


def _window_bias():
    wb = WINDOW_BLOCK
    qi = lax.broadcasted_iota(jnp.int32, (wb, 3 * wb), 0)
    kj = lax.broadcasted_iota(jnp.int32, (wb, 3 * wb), 1)
    band = (kj >= qi) & (kj <= qi + 2 * wb)
    cases = jnp.stack([band & (kj >= wb), band, band & (kj < 2 * wb)])
    return jnp.where(cases, 0.0, -jnp.inf).astype(F32)


def _block_bias(bias_ref, n, n_blocks):
    case = jnp.where(n == 0, 0, jnp.where(n == n_blocks - 1, 2, 1))
    one = bias_ref[case]
    return jnp.concatenate([one] * HEADS_PER_GROUP, axis=0)


def _stack_heads(pair0, pair1, lo):
    return jnp.concatenate([jnp.where(lo, pair0, 0.0), jnp.where(lo, 0.0, pair0),
                            jnp.where(lo, pair1, 0.0), jnp.where(lo, 0.0, pair1)], axis=0)


def _unstack_pair(stacked, i, lo):
    wb = WINDOW_BLOCK
    return jnp.where(lo, stacked[2 * i * wb:(2 * i + 1) * wb], stacked[(2 * i + 1) * wb:(2 * i + 2) * wb])


def _sink_column(sink_ref, g):
    wb = WINDOW_BLOCK
    return jnp.concatenate([jnp.full((wb, 1), sink_ref[0, HEADS_PER_GROUP * g + i], F32)
                            for i in range(HEADS_PER_GROUP)], axis=0)


def _head_exp(q4, k2g, bias, sink):
    sc = lax.dot_general(q4, k2g, _DIMS["nt"], preferred_element_type=F32) * (HEAD_DIM ** -0.5) + bias
    m = jnp.maximum(jnp.max(sc, axis=1, keepdims=True), sink)
    return jnp.exp(sc - m).astype(BF16), jnp.exp(sink - m)


def _swa_fwd(proj, kpad, vpad, tab, bias, sink, *, carry=None):
    s = proj.shape[0]
    wb = WINDOW_BLOCK

    def body(b_ref, k_ref, v_ref, t_ref, bias_ref, sink_ref, o_ref, y_ref):
        lo = lax.broadcasted_iota(jnp.int32, (wb, LANES), 1) < HALF_LANES
        lo_w = lax.broadcasted_iota(jnp.int32, (3 * wb, LANES), 1) < HALF_LANES
        for sub in range(SWA_FWD_BLOCKS):
            n = pl.program_id(0) * SWA_FWD_BLOCKS + sub
            rows = slice(sub * wb, (sub + 1) * wb)
            k2, v2 = _window_operands(k_ref, v_ref, n, lo_w)
            valid = _block_bias(bias_ref, n, s // wb)
            tab_v = t_ref[rows, :]
            ones = jnp.ones((3 * wb, LANES), BF16)
            for g in range(2):
                qr = [_rope(b_ref[rows, (2 * g + i) * LANES:(2 * g + i + 1) * LANES], tab_v) for i in range(2)]
                q4 = _stack_heads(qr[0], qr[1], lo).astype(BF16)
                e, es = _head_exp(q4, k2[g], valid, _sink_column(sink_ref, g))
                ox = jnp.dot(e, jnp.concatenate([v2[g], ones], axis=1), preferred_element_type=F32)
                o4 = ox[:, :LANES] * (1.0 / (ox[:, LANES:] + es))
                for i in range(2):
                    cols = slice((2 * g + i) * LANES, (2 * g + i + 1) * LANES)
                    op = _unstack_pair(o4, i, lo)
                    o_ref[rows, cols] = op
                    zp = b_ref[rows, A_WIDTH + cols.start:A_WIDTH + cols.stop]
                    y_ref[rows, cols] = (op * (zp * _sigmoid(zp))).astype(BF16)

    tq = SWA_FWD_BLOCKS * wb
    pad_spec = pl.BlockSpec((s + 2 * KV_PAD, LANES), lambda n: (0, 0))
    o_spec = pl.BlockSpec((tq, A_WIDTH), lambda n: (n, 0))
    outs, carried = _carried_call(
        lambda ins, outs, scr: body(*ins, *outs), carry, grid=(s // tq,),
        in_specs=[pl.BlockSpec((tq, W_B), lambda n: (n, COL_B)), pad_spec, pad_spec,
                  pl.BlockSpec((tq, 3 * LANES), lambda n: (n, 0)),
                  pl.BlockSpec(bias.shape, lambda n: (0, 0, 0)), pl.BlockSpec(memory_space=pltpu.SMEM)],
        out_specs=[o_spec, o_spec],
        out_shape=[jax.ShapeDtypeStruct((s, A_WIDTH), F32), jax.ShapeDtypeStruct((s, A_WIDTH), BF16)],
        scratch=[], operands=(proj, kpad, vpad, tab, bias, sink), name="swa_fwd")
    return (*outs, carried) if carry else tuple(outs)


def _swa_bwd(proj, kpad, vpad, tab, bias, sink, o_attn, dyb, dproj):
    s = proj.shape[0]
    wb = WINDOW_BLOCK
    scale = HEAD_DIM ** -0.5

    def body(b_ref, k_ref, v_ref, t_ref, bias_ref, sink_ref, o_ref, dy_ref, _, dp_ref, dk_ref, dv_ref, ds_ref):
        @pl.when(pl.program_id(0) == 0)
        def _():
            dk_ref[...] = jnp.zeros_like(dk_ref)
            dv_ref[...] = jnp.zeros_like(dv_ref)
            ds_ref[...] = jnp.zeros_like(ds_ref)

        lo = lax.broadcasted_iota(jnp.int32, (wb, LANES), 1) < HALF_LANES
        lo_w = lax.broadcasted_iota(jnp.int32, (3 * wb, LANES), 1) < HALF_LANES
        for sub in range(SWA_BWD_BLOCKS):
            n = pl.program_id(0) * SWA_BWD_BLOCKS + sub
            rows = slice(sub * wb, (sub + 1) * wb)
            k2, v2 = _window_operands(k_ref, v_ref, n, lo_w)
            valid = _block_bias(bias_ref, n, s // wb)
            tab_v = t_ref[rows, :]
            ones = jnp.ones((3 * wb, LANES), BF16)
            dks, dvs = [], []
            for g in range(2):
                qr, op, do = [], [], []
                for i in range(2):
                    cols = slice((2 * g + i) * LANES, (2 * g + i + 1) * LANES)
                    zcols = slice(A_WIDTH + cols.start, A_WIDTH + cols.stop)
                    qr.append(_rope(b_ref[rows, cols], tab_v))
                    zp = b_ref[rows, zcols]
                    sg = _sigmoid(zp)
                    op.append(o_ref[rows, cols])
                    dyp = dy_ref[rows, cols]
                    do.append(dyp * (zp * sg))
                    dp_ref[rows, zcols] = (dyp * op[i] * (sg * (1.0 + zp * (1.0 - sg)))).astype(BF16)
                q4 = _stack_heads(qr[0], qr[1], lo).astype(BF16)
                do4 = _stack_heads(do[0], do[1], lo)
                o4 = jnp.concatenate([op[0], op[0], op[1], op[1]], axis=0)
                e, es = _head_exp(q4, k2[g], valid, _sink_column(sink_ref, g))
                inv = 1.0 / (jnp.dot(e, ones, preferred_element_type=F32) + es)
                prob = e.astype(F32) * jnp.concatenate([inv, inv, inv], axis=1)
                delta = jnp.sum(do4 * o4, axis=1, keepdims=True)
                do4b = do4.astype(BF16)
                dprob = lax.dot_general(do4b, v2[g], _DIMS["nt"], preferred_element_type=F32)
                dsc = (prob * (dprob - delta)).astype(BF16)
                sink_terms = (es * inv[:, :1]) * delta
                for i in range(HEADS_PER_GROUP):
                    h = HEADS_PER_GROUP * g + i
                    dsink = -jnp.sum(sink_terms[i * wb:(i + 1) * wb], axis=0, keepdims=True)
                    ds_ref[h:h + 1, :] += jnp.broadcast_to(dsink, (1, LANES))
                dq4 = jnp.dot(dsc, k2[g], preferred_element_type=F32) * scale
                for i in range(2):
                    cols = slice((2 * g + i) * LANES, (2 * g + i + 1) * LANES)
                    dp_ref[rows, cols] = _rope_transpose(_unstack_pair(dq4, i, lo), tab_v).astype(BF16)
                dk2 = lax.dot_general(dsc, q4, _DIMS["tn"], preferred_element_type=F32) * scale
                dv2 = lax.dot_general(prob.astype(BF16), do4b, _DIMS["tn"], preferred_element_type=F32)
                dks.append(dk2 + pltpu.roll(dk2, HALF_LANES, 1))
                dvs.append(dv2 + pltpu.roll(dv2, HALF_LANES, 1))
            start = _window_start(n)
            dk_ref[pl.ds(start, 3 * wb), :] += jnp.where(lo_w, dks[0], dks[1])
            dv_ref[pl.ds(start, 3 * wb), :] += jnp.where(lo_w, dvs[0], dvs[1])

    tq = SWA_BWD_BLOCKS * wb
    pad_spec = pl.BlockSpec((s + 2 * KV_PAD, LANES), lambda n: (0, 0))
    blk = pl.BlockSpec((tq, A_WIDTH), lambda n: (n, 0))
    bsp = pl.BlockSpec((tq, W_B), lambda n: (n, COL_B))
    pad_shape = jax.ShapeDtypeStruct((s + 2 * KV_PAD, LANES), F32)
    return pl.pallas_call(
        body, grid=(s // tq,),
        in_specs=[bsp, pad_spec, pad_spec, pl.BlockSpec((tq, 3 * LANES), lambda n: (n, 0)),
                  pl.BlockSpec(bias.shape, lambda n: (0, 0, 0)), pl.BlockSpec(memory_space=pltpu.SMEM), blk, blk,
                  pl.BlockSpec(memory_space=pl.ANY)],
        out_specs=[bsp, pad_spec, pad_spec, pl.BlockSpec((8, LANES), lambda n: (0, 0))],
        out_shape=[jax.ShapeDtypeStruct(dproj.shape, BF16), pad_shape, pad_shape,
                   jax.ShapeDtypeStruct((8, LANES), F32)],
        input_output_aliases={8: 0}, name="swa_bwd",
        compiler_params=_params())(proj, kpad, vpad, tab, bias, sink, o_attn, dyb, dproj)


def _mem_exp(qh, mk):
    sc = lax.dot_general(qh, mk, _DIMS["nt"], preferred_element_type=F32) * (MEM_HEAD_DIM ** -0.5)
    return jnp.exp(sc - jnp.max(sc, axis=1, keepdims=True)).astype(BF16)


def _mem_fwd(proj, mkv):
    s = proj.shape[0]
    ts = 512
    mlen = mkv.shape[0]

    def body(m_ref, kv_ref, o_ref, y_ref):
        ones = jnp.ones((mlen, LANES), BF16)
        for h in range(MEM_HEADS):
            cols = slice(h * LANES, (h + 1) * LANES)
            mk = kv_ref[:, cols].astype(BF16)
            mv = kv_ref[:, MEM_WIDTH + h * LANES:MEM_WIDTH + (h + 1) * LANES].astype(BF16)
            e = _mem_exp(m_ref[:, cols].astype(BF16), mk)
            ox = jnp.dot(e, jnp.concatenate([mv, ones], axis=1), preferred_element_type=F32)
            oh = ox[:, :LANES] * (1.0 / ox[:, LANES:])
            o_ref[:, cols] = oh
            zh = m_ref[:, MEM_WIDTH + h * LANES:MEM_WIDTH + (h + 1) * LANES]
            y_ref[:, cols] = (oh * (zh * _sigmoid(zh))).astype(BF16)

    o_spec = pl.BlockSpec((ts, MEM_WIDTH), lambda i: (i, 0))
    return pl.pallas_call(
        body, grid=(s // ts,),
        in_specs=[pl.BlockSpec((ts, W_M), lambda i: (i, COL_M)),
                  pl.BlockSpec((mlen, 2 * MEM_WIDTH), lambda i: (0, 0))],
        out_specs=[o_spec, o_spec],
        out_shape=[jax.ShapeDtypeStruct((s, MEM_WIDTH), F32), jax.ShapeDtypeStruct((s, MEM_WIDTH), BF16)],
        name="mem_fwd", compiler_params=_params())(proj, mkv)


def _mem_bwd(proj, mkv, o_mem, dym, dproj, *, carry=None):
    s = proj.shape[0]
    ts = 512
    mlen = mkv.shape[0]
    scale = MEM_HEAD_DIM ** -0.5

    def body(m_ref, kv_ref, o_ref, dy_ref, _, dp_ref, dkv_ref):
        @pl.when(pl.program_id(0) == 0)
        def _():
            dkv_ref[...] = jnp.zeros_like(dkv_ref)

        ones = jnp.ones((mlen, LANES), BF16)
        for h in range(MEM_HEADS):
            cols = slice(h * LANES, (h + 1) * LANES)
            vcols = slice(MEM_WIDTH + h * LANES, MEM_WIDTH + (h + 1) * LANES)
            mk = kv_ref[:, cols].astype(BF16)
            mv = kv_ref[:, vcols].astype(BF16)
            qh = m_ref[:, cols].astype(BF16)
            zh = m_ref[:, vcols]
            sg = _sigmoid(zh)
            oh = o_ref[:, cols]
            dyh = dy_ref[:, cols]
            doh = dyh * (zh * sg)
            dp_ref[:, vcols] = (dyh * oh * (sg * (1.0 + zh * (1.0 - sg)))).astype(BF16)
            e = _mem_exp(qh, mk)
            inv = 1.0 / jnp.dot(e, ones, preferred_element_type=F32)
            prob = e.astype(F32) * jnp.concatenate([inv] * (mlen // LANES), axis=1)
            delta = jnp.sum(doh * oh, axis=1, keepdims=True)
            dohb = doh.astype(BF16)
            dprob = lax.dot_general(dohb, mv, _DIMS["nt"], preferred_element_type=F32)
            dsc = (prob * (dprob - delta)).astype(BF16)
            dp_ref[:, cols] = (jnp.dot(dsc, mk, preferred_element_type=F32) * scale).astype(BF16)
            dkv_ref[:, cols] += lax.dot_general(dsc, qh, _DIMS["tn"], preferred_element_type=F32) * scale
            dkv_ref[:, vcols] += lax.dot_general(prob.astype(BF16), dohb, _DIMS["tn"],
                                                 preferred_element_type=F32)

    blk = pl.BlockSpec((ts, MEM_WIDTH), lambda i: (i, 0))
    msp = pl.BlockSpec((ts, W_M), lambda i: (i, COL_M))
    kvsp = pl.BlockSpec((mlen, 2 * MEM_WIDTH), lambda i: (0, 0))
    outs, carried = _carried_call(
        lambda ins, outs, scr: body(*ins, *outs), carry, grid=(s // ts,),
        in_specs=[msp, kvsp, blk, blk, pl.BlockSpec(memory_space=pl.ANY)],
        out_specs=[msp, kvsp],
        out_shape=[jax.ShapeDtypeStruct(dproj.shape, BF16), jax.ShapeDtypeStruct(mkv.shape, F32)],
        scratch=[], operands=(proj, mkv, o_mem, dym, dproj), name="mem_bwd", aliases={4: 0})
    return (*outs, carried) if carry else tuple(outs)


def _forward_backward(x, mem, tgt, proj, w_conv, sink, g_mem, late_weights, g_post, early_exchange, kv_exchange):
    s = x.shape[0]
    tab = _rope_tables(s)
    bias = _window_bias()

    ya = _conv_fwd(proj, w_conv)
    kpad, vpad = _rope_kv(proj, tab)
    o_attn, yb, *arrived = _swa_fwd(proj, kpad, vpad, tab, bias, sink, carry=late_weights[0])
    w_kv, w_up, w_out = late_weights[1](arrived[0] if arrived else None)
    mn = _rmsnorm_fwd(mem, g_mem, name="mem_norm")
    mkv = _matmul(mn, w_kv, mode="nn", out_dtype=F32, tm=256, tn=1024, tk=D_MODEL, name="mem_kv")
    o_mem, ym = _mem_fwd(proj, mkv)
    merged, d_out, dy, dg_post, loss = _mid_fwd(ya, yb, ym, proj, x, tgt, w_up, w_out, g_post)
    dproj, d_ya, d_yb, d_ym, dw_up, dw_out = _mid_bwd(d_out, merged, ya, yb, ym, proj, w_up, w_out)

    dproj, dw_conv = _conv_bwd(proj, w_conv, d_ya, dproj)
    dproj, dkpad, dvpad, dsink = _swa_bwd(proj, kpad, vpad, tab, bias, sink, o_attn, d_yb, dproj)
    dproj = _rope_kv_bwd(dkpad, dvpad, tab, dproj)
    dproj, d_mkv, *early = _mem_bwd(proj, mkv, o_mem, d_ym, dproj, carry=early_exchange(dw_up, dw_out))

    dw_kv = _matmul(mn, d_mkv, mode="tn", out_dtype=F32, tm=1024, tn=1024, tk=256, name="dw_kv")
    d_mn = _matmul(d_mkv, w_kv, mode="nt", out_dtype=F32, tm=256, tn=1024, tk=D_MODEL, name="d_mn")
    _, dg_mem, *early_kv = _rmsnorm_bwd(d_mn, mem, g_mem, d_mn, name="mem_norm_bwd", carry=kv_exchange(dw_kv))

    return dict(loss=loss, dproj=dproj, dy=dy, w_conv=dw_conv, sink=dsink, g_mem=dg_mem,
                w_kv=dw_kv, w_up=dw_up, w_out=dw_out, g_post=dg_post, early=early[0] if early else None,
                early_kv=early_kv[0] if early_kv else None)


N_DEV = 8


def _position():
    return lax.axis_index("x"), lax.axis_index("y"), lax.axis_index("c")


def _other_chips(x, y):
    return (((1 - x, y), 2 * (1 - x) + y), ((x, 1 - y), 2 * x + (1 - y)), ((1 - x, 1 - y), 2 * (1 - x) + (1 - y)))


def _remote(src, dst, send_sems, recv_sems, k, device):
    return pltpu.make_async_remote_copy(src_ref=src, dst_ref=dst, send_sem=send_sems.at[k], recv_sem=recv_sems.at[k],
                                        device_id=device, device_id_type=MESH)


def _rows_half(ref, hf):
    rh = ref.shape[0] // 2
    return ref.at[pl.ds(pl.multiple_of(hf * rh, 8), rh)]


def _gather_weights(shards, small=None, relations=(0, 1, 2), into=None):
    n = len(shards)
    k = 0 if small is None else 1

    def peers(x, y):
        return [(r, chip, idx) for r, (chip, idx) in enumerate(_other_chips(x, y)) if r in relations]

    def ici(ins, outs, sems, a, r, chip, src_chip, c):
        return _remote(_rows_half(ins[a], c), _rows_half(outs[a].at[src_chip], c), sems[0], sems[1], 3 * a + r,
                       (*chip, c))

    def whole(ins, outs, sems, r, chip, src_chip, c):
        return _remote(ins[n], outs[n].at[src_chip], sems[0], sems[1], 3 * n + r, (*chip, c))

    def d2d(outs, sems, a, r, idx, hf, x, y, c):
        half = _rows_half(outs[a].at[idx], hf)
        return _remote(half, half, sems[2], sems[3], 3 * a + r, (x, y, 1 - c))

    def start(ins, outs, sems):
        x, y, c = _position()
        me = 2 * x + y
        for a in range(n):
            for r, chip, _ in peers(x, y):
                ici(ins, outs, sems, a, r, chip, me, c).start()
        for r, (chip, _) in enumerate(_other_chips(x, y)):
            if k:
                whole(ins, outs, sems, r, chip, me, c).start()

    def finish(ins, outs, sems):
        x, y, c = _position()
        me = 2 * x + y
        for a in range(n):
            for r, chip, idx in peers(x, y):
                ici(ins, outs, sems, a, r, chip, idx, c).wait_recv()
                d2d(outs, sems, a, r, idx, c, x, y, c).start()
        for a in range(n):
            for r, chip, idx in peers(x, y):
                d2d(outs, sems, a, r, idx, 1 - c, x, y, c).wait_recv()
        for r, (chip, idx) in enumerate(_other_chips(x, y)):
            if k:
                whole(ins, outs, sems, r, chip, idx, c).wait_recv()
                whole(ins, outs, sems, r, chip, me, c).wait_send()
        for a in range(n):
            for r, chip, idx in peers(x, y):
                ici(ins, outs, sems, a, r, chip, me, c).wait_send()
                d2d(outs, sems, a, r, idx, c, x, y, c).wait_send()

    operands = list(shards) + ([small] if k else [])
    shapes = [jax.ShapeDtypeStruct((N_CHIPS,) + s.shape, s.dtype) for s in operands]
    aliases = {}
    if into is not None:
        assert len(into) == len(operands)
        aliases = {len(operands) + a: a for a in range(len(into))}
        operands += list(into)
    return _Carry(operands, shapes,
                  [pltpu.SemaphoreType.DMA((3 * (n + k),)), pltpu.SemaphoreType.DMA((3 * (n + k),)),
                   pltpu.SemaphoreType.DMA((3 * n,)), pltpu.SemaphoreType.DMA((3 * n,))], start, finish, aliases)


def _pair_exchange(send):
    n = len(send)

    def copies(ins, outs, sems):
        x, y, c = _position()
        return [_remote(ins[a], outs[a], sems[0], sems[1], a, (x, y, 1 - c)) for a in range(n)]

    def start(ins, outs, sems):
        for cp in copies(ins, outs, sems):
            cp.start()

    def finish(ins, outs, sems):
        for cp in copies(ins, outs, sems):
            cp.wait()

    return _Carry(send, [jax.ShapeDtypeStruct(p.shape, p.dtype) for p in send],
                  [pltpu.SemaphoreType.DMA((n,)), pltpu.SemaphoreType.DMA((n,))], start, finish)


def _chip_exchange(sums):
    n = len(sums)

    def copies(ins, outs, sems):
        x, y, c = _position()
        return [_remote(ins[a].at[idx], outs[a].at[r], sems[0], sems[1], 3 * a + r, (*chip, c))
                for a in range(n) for r, (chip, idx) in enumerate(_other_chips(x, y))]

    def start(ins, outs, sems):
        for cp in copies(ins, outs, sems):
            cp.start()

    def finish(ins, outs, sems):
        for cp in copies(ins, outs, sems):
            cp.wait()

    return _Carry(sums, [jax.ShapeDtypeStruct((3,) + p.shape[1:], p.dtype) for p in sums],
                  [pltpu.SemaphoreType.DMA((3 * n,)), pltpu.SemaphoreType.DMA((3 * n,))], start, finish)


def _pair_share(pairs):
    n = len(pairs)

    def start(ins, outs, sems):
        x, y, c = _position()
        for a in range(n):
            _remote(outs[a].at[c], outs[a].at[c], sems[0], sems[1], a, (x, y, 1 - c)).start()

    def finish(ins, outs, sems):
        x, y, c = _position()
        for a in range(n):
            _remote(outs[a].at[1 - c], outs[a].at[1 - c], sems[0], sems[1], a, (x, y, 1 - c)).wait_recv()
        for a in range(n):
            _remote(outs[a].at[c], outs[a].at[c], sems[0], sems[1], a, (x, y, 1 - c)).wait_send()

    return _Carry(pairs, [jax.ShapeDtypeStruct(p.shape, p.dtype) for p in pairs],
                  [pltpu.SemaphoreType.DMA((n,)), pltpu.SemaphoreType.DMA((n,))], start, finish,
                  aliases={a: a for a in range(n)})


def _small_allreduce(pack, share):
    rows, width = pack.shape
    n_share = len(share.ins)

    def body(p_ref, *refs):
        share_in, o_ref, share_out = refs[:n_share], refs[n_share], refs[n_share + 1:2 * n_share + 1]
        buf, send_sems, recv_sems = refs[2 * n_share + 1:2 * n_share + 4]
        share_sems = refs[2 * n_share + 4:]
        share.start(share_in, share_out, share_sems)
        x, y, c = _position()
        me = 4 * x + 2 * y + c
        buf[me] = p_ref[...]
        peers = []
        for r in range(1, N_DEV):
            fx, fy, fc = (r >> 2) & 1, (r >> 1) & 1, r & 1
            px, py, pc = (1 - x if fx else x), (1 - y if fy else y), (1 - c if fc else c)
            peers.append(((px, py, pc), 4 * px + 2 * py + pc))
        sends = [_remote(p_ref, buf.at[me], send_sems, recv_sems, r, dev) for r, (dev, _) in enumerate(peers)]
        for cp in sends:
            cp.start()
        for r, (dev, idx) in enumerate(peers):
            _remote(p_ref, buf.at[idx], send_sems, recv_sems, r, dev).wait_recv()
        for cp in sends:
            cp.wait_send()
        acc = buf[0]
        for k in range(1, N_DEV):
            acc = acc + buf[k]
        o_ref[...] = acc
        share.finish(share_in, share_out, share_sems)

    vm = pl.BlockSpec(memory_space=pltpu.VMEM)
    red, *shared = pl.pallas_call(
        body, in_specs=[vm] + [_HBM] * n_share, out_specs=[vm] + [_HBM] * n_share,
        out_shape=[jax.ShapeDtypeStruct(pack.shape, F32)] + share.out_shapes,
        scratch_shapes=[pltpu.VMEM((N_DEV, rows, width), F32), pltpu.SemaphoreType.DMA((N_DEV - 1,)),
                        pltpu.SemaphoreType.DMA((N_DEV - 1,))] + share.sems,
        input_output_aliases={1 + i: 1 + o for i, o in share.aliases.items()},
        name="small_allreduce")(pack, *share.ins)
    return red, shared


ROW_TILE_MAX = 512
SUM_TILE_MAX = 2048
BF16_SUBLANES = 16


def _row_tile(rows, most=ROW_TILE_MAX):
    if rows <= most:
        return rows
    return max(t for t in range(BF16_SUBLANES, most + 1, BF16_SUBLANES) if rows % t == 0)


def _pair_add(keep, recv, name):
    nj, rh, cols = keep.shape
    tr = _row_tile(rh, SUM_TILE_MAX)

    def body(k_ref, r_ref, o_ref):
        o_ref[...] = (k_ref[...].astype(F32) + r_ref[...].astype(F32)).astype(BF16)

    blk = pl.BlockSpec((None, tr, cols), lambda j, i: (j, i, 0))
    return pl.pallas_call(body, grid=(nj, rh // tr), in_specs=[blk, blk], out_specs=blk,
                          out_shape=jax.ShapeDtypeStruct(keep.shape, BF16), name=name,
                          compiler_params=_params())(keep, recv)


def _chip_add(sums, recv, where, name):
    _, rh, cols = sums.shape
    tr = _row_tile(rh, SUM_TILE_MAX)

    def body(w_ref, s_ref, r_ref, o_ref):
        o_ref[...] = ((s_ref[...].astype(F32) + r_ref[0].astype(F32)) + r_ref[1].astype(F32)) + r_ref[2].astype(F32)

    grid_spec = pltpu.PrefetchScalarGridSpec(
        num_scalar_prefetch=1, grid=(rh // tr,),
        in_specs=[pl.BlockSpec((None, tr, cols), lambda i, w_ref: (w_ref[0], i, 0)),
                  pl.BlockSpec((3, tr, cols), lambda i, w_ref: (0, i, 0))],
        out_specs=pl.BlockSpec((None, tr, cols), lambda i, w_ref: (w_ref[1], i, 0)))
    return pl.pallas_call(body, grid_spec=grid_spec, out_shape=jax.ShapeDtypeStruct((2, rh, cols), F32),
                          name=name, compiler_params=_params())(where, sums, recv)


def _adamw(w, g, m, v, name):
    rows, cols = w.shape
    tr = _row_tile(rows)
    assert rows % tr == 0

    def body(w_ref, g_ref, m_ref, v_ref, d_ref, mo_ref, vo_ref):
        gv = g_ref[...]
        m_new = ADAM_B1 * m_ref[...] + (1.0 - ADAM_B1) * gv
        v_new = ADAM_B2 * v_ref[...] + (1.0 - ADAM_B2) * jnp.square(gv)
        m_hat = m_new / (1.0 - ADAM_B1 ** ADAM_STEP)
        v_hat = v_new / (1.0 - ADAM_B2 ** ADAM_STEP)
        d_ref[...] = -ADAM_LR * (m_hat / (jnp.sqrt(v_hat) + ADAM_EPS) + ADAM_WD * w_ref[...])
        mo_ref[...] = m_new
        vo_ref[...] = v_new

    blk = pl.BlockSpec((tr, cols), lambda i: (i, 0))
    shp = jax.ShapeDtypeStruct((rows, cols), F32)
    return pl.pallas_call(body, grid=(rows // tr,), in_specs=[blk] * 4, out_specs=[blk] * 3,
                          out_shape=[shp] * 3, name=name, compiler_params=_params())(w, g, m, v)


def _adamw_halves(w, g2, m, v, name):
    rows, cols = w.shape
    half = cols // 2
    tr = _row_tile(rows)

    def body(w_ref, g_ref, m_ref, v_ref, go_ref, d_ref, mo_ref, vo_ref):
        gv = g_ref[...]
        go_ref[...] = gv
        m_new = ADAM_B1 * m_ref[...] + (1.0 - ADAM_B1) * gv
        v_new = ADAM_B2 * v_ref[...] + (1.0 - ADAM_B2) * jnp.square(gv)
        m_hat = m_new / (1.0 - ADAM_B1 ** ADAM_STEP)
        v_hat = v_new / (1.0 - ADAM_B2 ** ADAM_STEP)
        d_ref[...] = -ADAM_LR * (m_hat / (jnp.sqrt(v_hat) + ADAM_EPS) + ADAM_WD * w_ref[...])
        mo_ref[...] = m_new
        vo_ref[...] = v_new

    blk = pl.BlockSpec((tr, half), lambda hf, i: (i, hf))
    gsp = pl.BlockSpec((None, tr, half), lambda hf, i: (hf, i, 0))
    shp = jax.ShapeDtypeStruct((rows, cols), F32)
    return pl.pallas_call(body, grid=(2, rows // tr), in_specs=[blk, gsp, blk, blk], out_specs=[blk] * 4,
                          out_shape=[shp] * 4, name=name, compiler_params=_params())(w, g2, m, v)


SHARD_W = IN_WIDTH // N_CHIPS


def _half_major(a):
    r, c = a.shape
    return a.reshape(N_CHIPS, 2, r // N_CHIPS // 2, c).transpose(1, 0, 2, 3)


def kernel(x, mem, g_pre, w_in, w_conv, attn_sink, g_mem, w_mem_kv, w_up_a, w_up_b, w_up_m, w_out, g_post, loss_target, m_g_pre, m_w_in, m_w_conv, m_attn_sink, m_g_mem, m_w_mem_kv, m_w_up_a, m_w_up_b, m_w_up_m, m_w_out, m_g_post, v_g_pre, v_w_in, v_w_conv, v_attn_sink, v_g_mem, v_w_mem_kv, v_w_up_a, v_w_up_b, v_w_up_m, v_w_out, v_g_post):
    xi, yi, ci = _position()
    chip = 2 * xi + yi
    where = jnp.stack([chip, ci, N_CHIPS - 1 - chip]).astype(jnp.int32)

    own = [w_in[0].T.astype(BF16), w_mem_kv[0].astype(BF16),
           jnp.concatenate([w_up_a[0], w_up_b[0], w_up_m[0]], axis=0).astype(BF16), w_out[0].astype(BF16)]
    own_conv = jnp.pad(w_conv[0], ((0, 5), (0, 0)))

    def pieces(mine, got):
        got = lax.dynamic_update_slice_in_dim(got, mine[None], chip, axis=0)
        return [got[j] for j in range(N_CHIPS)]

    diag = N_CHIPS - 1 - chip
    proj, h, h_t, got_near, got_conv, got_far = _proj_near(x[0], g_pre, own[0], own_conv, where)
    w_near = lax.dynamic_update_slice_in_dim(got_near, own[0][None], chip, axis=0).reshape(IN_WIDTH, D_MODEL)
    far = lax.dynamic_index_in_dim(got_far, diag, 0, keepdims=False)
    proj = _proj_far(h, w_near, far, where, into=proj)
    w_conv_full = jnp.concatenate([p[:3] for p in pieces(own_conv, got_conv)], axis=1)

    def late_weights(gathered):
        w_kv_full = jnp.concatenate(pieces(own[1], gathered[0]), axis=0)
        up_pieces = pieces(own[2], gathered[1])
        w_up_full = jnp.stack([jnp.concatenate([p[k * A_WIDTH:(k + 1) * A_WIDTH] for p in up_pieces], axis=1)
                               for k in range(3)])
        return w_kv_full, w_up_full, jnp.concatenate(pieces(own[3], gathered[2]), axis=0)

    def pick(parts, hf):
        return [lax.dynamic_index_in_dim(p, hf, 0, keepdims=False) for p in parts]

    def up_out_parts(dw_up, dw_out):
        up = (dw_up.reshape(3, A_WIDTH, N_CHIPS, D_MODEL // N_CHIPS).transpose(2, 0, 1, 3)
              .reshape(N_CHIPS, 2, 3 * A_WIDTH // 2, D_MODEL // N_CHIPS).transpose(1, 0, 2, 3))
        return [up.astype(BF16), _half_major(dw_out).astype(BF16)]

    g = _forward_backward(x[0], mem[0], loss_target[0], proj, w_conv_full, attn_sink, g_mem,
                          (_gather_weights(own[1:]), late_weights), g_post,
                          lambda dw_up, dw_out: _pair_exchange(pick(up_out_parts(dw_up, dw_out), 1 - ci)),
                          lambda dw_kv: _pair_exchange(pick([_half_major(dw_kv).astype(BF16)], 1 - ci)))

    half_rows = D_MODEL // 2

    def dw_in_half(half_of, name, carry):
        dw, carried = _dw_in_t(g["dproj"], h_t, half_of=half_of, where=where, name=name, carry=carry)
        return dw.reshape(N_CHIPS, SHARD_W, half_rows), carried

    small_keep = pick([_half_major(g["w_kv"]).astype(BF16)] + up_out_parts(g["w_up"], g["w_out"]), ci)
    small_names = ["w_kv", "w_up", "w_out"]
    sums_small = [_pair_add(k, r, "pair_add_" + nm)
                  for k, r, nm in zip(small_keep, g["early_kv"] + g["early"], small_names)]
    dw_send, recv3_small = dw_in_half(lambda w: 1 - w[1], "dw_in_send", _chip_exchange(sums_small))
    dw_keep, (recv_in,) = dw_in_half(lambda w: w[1], "dw_in_keep", _pair_exchange([dw_send]))
    sum_in = _pair_add(dw_keep, recv_in, "pair_add_w_in")
    d_h, (recv3_in,) = _d_h(g["dproj"], w_near, far, where, carry=_chip_exchange([sum_in]))
    pairs = [_chip_add(s, r, where, "chip_add_" + nm)
             for s, r, nm in zip([sum_in] + sums_small, [recv3_in] + recv3_small, ["w_in"] + small_names)]
    grad_x, dg_pre = _rmsnorm_bwd(d_h, x[0], g_pre, g["dy"], name="pre_norm_bwd")

    zeros512 = jnp.zeros((1, D_MODEL - A_WIDTH), F32)
    conv_rows = [jnp.concatenate([g["w_conv"][k:k + 1], zeros512], axis=1) for k in range(3)]
    sink_row = jnp.pad(g["sink"][:, 0].reshape(1, N_Q_HEADS), ((0, 0), (0, D_MODEL - N_Q_HEADS)))
    loss_row = jnp.pad(g["loss"], ((0, 0), (0, D_MODEL - LANES)))
    pack = jnp.concatenate([dg_pre, g["g_mem"], g["g_post"]] + conv_rows + [sink_row, loss_row], axis=0)
    red, full = _small_allreduce(pack, _pair_share(pairs))
    loss = red[7, 0]
    small_grads = dict(
        g_pre=red[0:1], g_mem=red[1:2], g_post=red[2:3], attn_sink=red[6:7, :N_Q_HEADS],
        w_conv=lax.dynamic_slice(red[3:6, :A_WIDTH], (0, chip * LANES), (3, LANES)))

    gw_up = full[2].reshape(3, A_WIDTH, D_MODEL // N_CHIPS)
    grads = dict(small_grads, w_mem_kv=full[1].reshape(D_MODEL // N_CHIPS, 2 * MEM_WIDTH),
                 w_up_a=gw_up[0], w_up_b=gw_up[1], w_up_m=gw_up[2],
                 w_out=full[3].reshape(D_MODEL // N_CHIPS, D_MODEL))

    weights = dict(g_pre=g_pre, w_in=w_in, w_conv=w_conv, attn_sink=attn_sink, g_mem=g_mem, w_mem_kv=w_mem_kv,
                   w_up_a=w_up_a, w_up_b=w_up_b, w_up_m=w_up_m, w_out=w_out, g_post=g_post)
    m_in = dict(g_pre=m_g_pre, w_in=m_w_in, w_conv=m_w_conv, attn_sink=m_attn_sink, g_mem=m_g_mem,
                w_mem_kv=m_w_mem_kv, w_up_a=m_w_up_a, w_up_b=m_w_up_b, w_up_m=m_w_up_m, w_out=m_w_out,
                g_post=m_g_post)
    v_in = dict(g_pre=v_g_pre, w_in=v_w_in, w_conv=v_w_conv, attn_sink=v_attn_sink, g_mem=v_g_mem,
                w_mem_kv=v_w_mem_kv, w_up_a=v_w_up_a, w_up_b=v_w_up_b, w_up_m=v_w_up_m, w_out=v_w_out,
                g_post=v_g_post)
    out_g, out_d, out_m, out_v = [], [], [], []
    for nm in ("g_pre", "w_in", "w_conv", "attn_sink", "g_mem", "w_mem_kv", "w_up_a", "w_up_b", "w_up_m", "w_out",
               "g_post"):
        shape = weights[nm].shape
        if nm == "w_in":
            results = _adamw_halves(w_in[0].T, full[0], m_w_in[0].T, v_w_in[0].T, "adamw_w_in")
            for out, t in zip((out_g, out_d, out_m, out_v), results):
                out.append(t.T.reshape(shape))
            continue
        two_d = shape[-2:]
        gr = grads[nm].reshape(two_d)
        d, m_new, v_new = _adamw(weights[nm].reshape(two_d), gr, m_in[nm].reshape(two_d), v_in[nm].reshape(two_d),
                                 "adamw_" + nm)
        out_g.append(gr.reshape(shape))
        out_d.append(d.reshape(shape))
        out_m.append(m_new.reshape(shape))
        out_v.append(v_new.reshape(shape))
    return (loss, grad_x.reshape(x.shape), *out_g, *out_d, *out_m, *out_v)
```

```python
import jax
import jax.numpy as jnp
from jax import lax
from jax.experimental import pallas as pl
from jax.experimental.pallas import tpu as pltpu

F32 = jnp.float32
BF16 = jnp.bfloat16
MESH = pl.DeviceIdType.MESH

D_MODEL = 1024
EPS = 1e-6
A_WIDTH = 512
HEAD_DIM = 64
N_Q_HEADS = 8
WINDOW_BLOCK = 128
KV_PAD = 512
ROPE_THETA = 500000.0
ROT_DIM = 16
MEM_HEADS = 4
MEM_HEAD_DIM = 128
MEM_WIDTH = 512
IN_WIDTH = 7424
N_CHIPS = 4
LANES = 128
HALF_LANES = 64

PERM_SEGS = ((0, 2560), (2816, 3328), (4352, 7424), (3328, 4352), (2560, 2816))
COL_A, W_A = 0, 2048
COL_B, W_B = 2, 1024
COL_G, W_G = 1, 3072
COL_M, W_M = 6, 1024
COL_KV, W_KV = 28, 256

ADAM_LR = 0.001
ADAM_B1 = 0.9
ADAM_B2 = 0.999
ADAM_EPS = 1e-08
ADAM_WD = 0.01
ADAM_STEP = 10

VMEM_LIGHT_BYTES = 48 * 1024 * 1024
VMEM_HEAVY_BYTES = 48 * 1024 * 1024


_HBM = pl.BlockSpec(memory_space=pltpu.HBM)


def _params(heavy=False):
    return pltpu.CompilerParams(vmem_limit_bytes=VMEM_HEAVY_BYTES if heavy else VMEM_LIGHT_BYTES)


def _sigmoid(v):
    return jax.nn.sigmoid(v)


_DIMS = {"nn": (((1,), (0,)), ((), ())), "nt": (((1,), (1,)), ((), ())), "tn": (((0,), (0,)), ((), ()))}


class _Carry:
    def __init__(self, ins, out_shapes, sems, start, finish, aliases=None):
        self.ins, self.out_shapes, self.sems = list(ins), list(out_shapes), list(sems)
        self.start, self.finish, self.aliases = start, finish, dict(aliases or {})


def _join(*carries):
    def split(seq, counts):
        pos, parts = 0, []
        for n in counts:
            parts.append(seq[pos:pos + n])
            pos += n
        return parts

    n_in = [len(c.ins) for c in carries]
    n_out = [len(c.out_shapes) for c in carries]
    n_sem = [len(c.sems) for c in carries]

    def run(which):
        def go(ins, outs, sems):
            for c, i, o, sm in zip(carries, split(ins, n_in), split(outs, n_out), split(sems, n_sem)):
                getattr(c, which)(i, o, sm)
        return go

    aliases = {}
    for k, c in enumerate(carries):
        aliases.update({sum(n_in[:k]) + i: sum(n_out[:k]) + o for i, o in c.aliases.items()})
    return _Carry([a for c in carries for a in c.ins], [sh for c in carries for sh in c.out_shapes],
                  [sm for c in carries for sm in c.sems], run("start"), run("finish"), aliases)


def _carried_call(body, carry, *, grid, in_specs, out_specs, out_shape, scratch, operands, name, prefetch=None,
                  aliases=None, heavy=False):
    n_in, n_out, n_scr = len(in_specs), len(out_specs), len(scratch)
    c_in = len(carry.ins) if carry else 0
    c_out = len(carry.out_shapes) if carry else 0
    n_pre = 0 if prefetch is None else 1
    steps = 1
    for g in grid:
        steps *= g

    def wrapped(*refs):
        refs = refs[n_pre:]
        ins, cins = refs[:n_in], refs[n_in:n_in + c_in]
        outs = refs[n_in + c_in:n_in + c_in + n_out]
        couts = refs[n_in + c_in + n_out:n_in + c_in + n_out + c_out]
        rest = refs[n_in + c_in + n_out + c_out:]
        scr, sems = rest[:n_scr], rest[n_scr:]
        if carry:
            step = pl.program_id(0)
            for ax in range(1, len(grid)):
                step = step * grid[ax] + pl.program_id(ax)

            @pl.when(step == 0)
            def _():
                carry.start(cins, couts, sems)

        body(ins, outs, scr)
        if carry:
            @pl.when(step == steps - 1)
            def _():
                carry.finish(cins, couts, sems)

    all_aliases = {n_pre + i: o for i, o in (aliases or {}).items()}
    if carry:
        all_aliases.update({n_pre + n_in + i: n_out + o for i, o in carry.aliases.items()})
    all_in = list(in_specs) + [_HBM] * c_in
    all_out = list(out_specs) + [_HBM] * c_out
    all_scratch = list(scratch) + (carry.sems if carry else [])
    if n_pre:
        spec = dict(grid_spec=pltpu.PrefetchScalarGridSpec(num_scalar_prefetch=1, grid=grid, in_specs=all_in,
                                                           out_specs=all_out, scratch_shapes=all_scratch))
        pre = (prefetch,)
    else:
        spec = dict(grid=grid, in_specs=all_in, out_specs=all_out, scratch_shapes=all_scratch)
        pre = ()
    results = pl.pallas_call(
        wrapped, out_shape=list(out_shape) + (carry.out_shapes if carry else []), input_output_aliases=all_aliases,
        name=name, compiler_params=_params(heavy), **spec)(*pre, *operands, *(carry.ins if carry else []))
    return list(results[:n_out]), list(results[n_out:])


def _matmul(a, b, *, mode, out_dtype, tm, tn, tk, name):
    if mode == "nn":
        (m, k), (_, n) = a.shape, b.shape
    elif mode == "nt":
        (m, k), (n, _) = a.shape, b.shape
    else:
        (k, m), (_, n) = a.shape, b.shape
    tm, tn, tk = min(tm, m), min(tn, n), min(tk, k)
    assert m % tm == 0 and n % tn == 0 and k % tk == 0
    nk = k // tk
    dims = _DIMS[mode]

    if mode == "nn":
        a_spec = pl.BlockSpec((tm, tk), lambda i, j, kk: (i, kk))
        b_spec = pl.BlockSpec((tk, tn), lambda i, j, kk: (kk, j))
    elif mode == "nt":
        a_spec = pl.BlockSpec((tm, tk), lambda i, j, kk: (i, kk))
        b_spec = pl.BlockSpec((tn, tk), lambda i, j, kk: (j, kk))
    else:
        a_spec = pl.BlockSpec((tk, tm), lambda i, j, kk: (kk, i))
        b_spec = pl.BlockSpec((tk, tn), lambda i, j, kk: (kk, j))
    o_spec = pl.BlockSpec((tm, tn), lambda i, j, kk: (i, j))

    def part(a_ref, b_ref):
        return lax.dot_general(a_ref[...].astype(BF16), b_ref[...].astype(BF16), dims,
                               preferred_element_type=F32)

    if nk == 1:
        def body(a_ref, b_ref, o_ref):
            o_ref[...] = part(a_ref, b_ref).astype(out_dtype)
        scratch = []
    else:
        def body(a_ref, b_ref, o_ref, acc_ref):
            kk = pl.program_id(2)

            @pl.when(kk == 0)
            def _():
                acc_ref[...] = part(a_ref, b_ref)

            @pl.when(kk > 0)
            def _():
                acc_ref[...] += part(a_ref, b_ref)

            @pl.when(kk == nk - 1)
            def _():
                o_ref[...] = acc_ref[...].astype(out_dtype)
        scratch = [pltpu.VMEM((tm, tn), F32)]

    return pl.pallas_call(
        body, grid=(m // tm, n // tn, nk), in_specs=[a_spec, b_spec], out_specs=o_spec,
        out_shape=jax.ShapeDtypeStruct((m, n), out_dtype), scratch_shapes=scratch,
        name=name, compiler_params=_params())(a, b)


IN_BLOCK = 256
N_IN_BLOCKS = IN_WIDTH // IN_BLOCK
SHARD_BLOCKS = (IN_WIDTH // N_CHIPS) // IN_BLOCK
BLOCK_RUNS = tuple((a // IN_BLOCK, sum(d - c for c, d in PERM_SEGS[:k]) // IN_BLOCK, (b - a) // IN_BLOCK)
                   for k, (a, b) in enumerate(PERM_SEGS))


def _perm_block(r):
    p = r
    for ref0, perm0, n in BLOCK_RUNS:
        p = jnp.where((r >= ref0) & (r < ref0 + n), r - ref0 + perm0, p)
    return p


def _proj_near(x, g_pre, own_w, small, where):
    s, d = x.shape
    norm_tile = min(512, s)
    n_own = SHARD_BLOCKS - 1
    n_diag = SHARD_BLOCKS + 1
    n_blocks = N_IN_BLOCKS - n_diag
    piece = IN_WIDTH // N_CHIPS - SHARD_BLOCKS * IN_BLOCK
    near = _gather_weights([own_w], small, relations=(0, 1))
    far = _gather_weights([own_w], relations=(2,))
    both = _join(near, far)
    n_cin, n_cout = len(both.ins), len(both.out_shapes)

    def block_of(i, w):
        me, dg = w[0], w[2]
        own0 = SHARD_BLOCKS * me + jnp.minimum(me, 1)
        dg0 = SHARD_BLOCKS * dg
        lo0, hi0 = jnp.minimum(own0, dg0), jnp.maximum(own0, dg0)
        lo_n = jnp.where(own0 < dg0, n_own, n_diag)
        hi_n = jnp.where(own0 < dg0, n_diag, n_own)
        r = i - n_own
        r = r + lo_n * (r >= lo0).astype(jnp.int32)
        r = r + hi_n * (r >= hi0).astype(jnp.int32)
        return jnp.where(i < n_own, own0 + i, r)

    def body(w_ref, x_hbm, g_ref, own_hbm, *refs):
        cins, (o_ref, h_hbm, ht_hbm) = refs[:n_cin], refs[n_cin:n_cin + 3]
        couts = refs[n_cin + 3:n_cin + 3 + n_cout]
        blocks, block_sems, h_ref, x_tile, ht_tile, io_sem = refs[n_cin + 3 + n_cout:n_cin + 9 + n_cout]
        sems = refs[n_cin + 9 + n_cout:]
        near_refs = (cins[:len(near.ins)], couts[:len(near.out_shapes)], sems[:len(near.sems)])
        far_refs = (cins[len(near.ins):], couts[len(near.out_shapes):], sems[len(near.sems):])
        gathered = couts[0]
        i = pl.program_id(0)
        me = w_ref[0]

        def fetch(step, slot):
            r = block_of(step, w_ref)
            for p in range(IN_BLOCK // piece):
                row = r * IN_BLOCK + p * piece
                j = row // (IN_WIDTH // N_CHIPS)
                off = pl.multiple_of(row - j * (IN_WIDTH // N_CHIPS), BF16_SUBLANES)
                dst = blocks.at[slot, pl.ds(p * piece, piece)]

                @pl.when(j == me)
                def _():
                    pltpu.make_async_copy(own_hbm.at[pl.ds(off, piece)], dst, block_sems.at[slot]).start()

                @pl.when(j != me)
                def _():
                    pltpu.make_async_copy(gathered.at[j, pl.ds(off, piece)], dst, block_sems.at[slot]).start()

        def arrived(slot):
            pltpu.make_async_copy(own_hbm.at[pl.ds(0, IN_BLOCK)], blocks.at[slot], block_sems.at[slot]).wait()

        slot = i % 2

        def norm_rows(k):
            rows = pl.ds(k * norm_tile, norm_tile)
            pltpu.sync_copy(x_hbm.at[rows], x_tile)
            xv = x_tile[...]
            hv = (xv * lax.rsqrt(jnp.mean(xv * xv, axis=-1, keepdims=True) + EPS)) * g_ref[...]
            h_ref[rows, :] = hv.astype(BF16)
            ht_tile[...] = hv.T.astype(BF16)
            to_h = pltpu.make_async_copy(h_ref.at[rows], h_hbm.at[rows], io_sem.at[0])
            to_ht = pltpu.make_async_copy(ht_tile, ht_hbm.at[:, rows], io_sem.at[1])
            to_h.start()
            to_ht.start()
            to_h.wait()
            to_ht.wait()

        @pl.when(i == 0)
        def _():
            near.start(*near_refs)
            fetch(i, slot)
            for k in range(s // norm_tile):
                norm_rows(k)

        @pl.when(i == n_own)
        def _():
            near.finish(*near_refs)
            far.start(*far_refs)
            fetch(i, slot)

        arrived(slot)

        @pl.when((i + 1 < n_blocks) & (i + 1 != n_own))
        def _():
            fetch(i + 1, 1 - slot)

        o_ref[...] = lax.dot_general(h_ref[...], blocks[slot], _DIMS["nt"], preferred_element_type=F32)

        @pl.when(i == n_blocks - 1)
        def _():
            far.finish(*far_refs)

    anysp = pl.BlockSpec(memory_space=pl.ANY)
    grid_spec = pltpu.PrefetchScalarGridSpec(
        num_scalar_prefetch=1, grid=(n_blocks,),
        in_specs=[anysp, pl.BlockSpec((1, d), lambda i, w: (0, 0)), anysp] + [_HBM] * n_cin,
        out_specs=[pl.BlockSpec((s, IN_BLOCK), lambda i, w: (0, _perm_block(block_of(i, w)))), anysp, anysp]
        + [_HBM] * n_cout,
        scratch_shapes=[pltpu.VMEM((2, IN_BLOCK, d), BF16), pltpu.SemaphoreType.DMA((2,)), pltpu.VMEM((s, d), BF16),
                        pltpu.VMEM((norm_tile, d), F32), pltpu.VMEM((d, norm_tile), BF16),
                        pltpu.SemaphoreType.DMA((2,))] + both.sems)
    return pl.pallas_call(
        body, grid_spec=grid_spec,
        out_shape=[jax.ShapeDtypeStruct((s, IN_WIDTH), F32), jax.ShapeDtypeStruct((s, d), BF16),
                   jax.ShapeDtypeStruct((d, s), BF16)] + both.out_shapes,
        name="proj_near", compiler_params=_params())(where, x, g_pre, own_w, *both.ins)


def _proj_far(h, w_near, far, where, *, into, carry=None):
    s, d = h.shape
    n_blocks = SHARD_BLOCKS + 1
    lead = IN_WIDTH // N_CHIPS - SHARD_BLOCKS * IN_BLOCK

    def body(ins, outs, scr):
        where_ref, h_ref, w_hbm, far_hbm, _ = ins
        win, sem = scr
        i = pl.program_id(0)

        @pl.when(i == 0)
        def _():
            dg = where_ref[2]
            rows = pl.ds(pl.multiple_of(dg * (SHARD_BLOCKS * IN_BLOCK), IN_BLOCK), n_blocks * IN_BLOCK)
            window = pltpu.make_async_copy(w_hbm.at[rows], win, sem)
            window.start()
            window.wait()
            shard = pltpu.make_async_copy(far_hbm, win.at[pl.ds(pl.multiple_of(dg * lead, BF16_SUBLANES), SHARD_W)], sem)
            shard.start()
            shard.wait()

        blk = win[pl.ds(pl.multiple_of(i * IN_BLOCK, IN_BLOCK), IN_BLOCK), :]
        outs[0][...] = lax.dot_general(h_ref[...], blk, _DIMS["nt"], preferred_element_type=F32)

    anysp = pl.BlockSpec(memory_space=pl.ANY)
    (proj,), carried = _carried_call(
        body, carry, grid=(n_blocks,),
        in_specs=[pl.BlockSpec(memory_space=pltpu.SMEM), pl.BlockSpec((s, d), lambda i, w: (0, 0)), anysp, anysp, anysp],
        out_specs=[pl.BlockSpec((s, IN_BLOCK), lambda i, w: (0, _perm_block(i + SHARD_BLOCKS * w[2])))],
        out_shape=[jax.ShapeDtypeStruct((s, IN_WIDTH), F32)],
        scratch=[pltpu.VMEM((n_blocks * IN_BLOCK, d), BF16), pltpu.SemaphoreType.DMA],
        operands=(where, h, w_near, far, into), name="proj_far", prefetch=where, aliases={4: 0})
    return (proj, carried) if carry else proj


def _dw_in_t(dproj, h_t, *, half_of, where, name, carry=None):
    d, s = h_t.shape
    c = d // 2

    def body(ins, outs, scr):
        outs[0][...] = lax.dot_general(ins[1][...], ins[0][...], _DIMS["nn"], preferred_element_type=F32).T.astype(BF16)

    (dw,), carried = _carried_call(
        body, carry, grid=(N_IN_BLOCKS,),
        in_specs=[pl.BlockSpec((s, IN_BLOCK), lambda r, w: (0, _perm_block(r))),
                  pl.BlockSpec((c, s), lambda r, w: (half_of(w), 0))],
        out_specs=[pl.BlockSpec((IN_BLOCK, c), lambda r, w: (r, 0))],
        out_shape=[jax.ShapeDtypeStruct((IN_WIDTH, c), BF16)], scratch=[], operands=(dproj, h_t), name=name,
        prefetch=where)
    return (dw, carried) if carry else dw


def _norm_bwd_tile(dhv, xv, gv, resv):
    r = lax.rsqrt(jnp.mean(xv * xv, axis=-1, keepdims=True) + EPS)
    xh = xv * r
    dxh = dhv * gv
    dx = resv + r * (dxh - xh * jnp.mean(dxh * xh, axis=-1, keepdims=True))
    return dx, jnp.sum(dhv * xh, axis=0, keepdims=True)


def _d_h(dproj, w_near, far, where, x, g, res, *, carry=None):
    s = dproj.shape[0]
    d = w_near.shape[1]
    tm = min(s, 256)
    n = s // tm
    assert n % 2 == 0

    def body(ins, outs, scr):
        where_ref, a_ref, w_hbm, far_hbm, x_ref, g_ref, res_ref = ins
        dx_ref, dg_ref = outs
        w_ref, sem, dh_even, dh_odd = scr
        i = pl.program_id(0)

        def norm_bwd(dh_ref):
            dx, part = _norm_bwd_tile(dh_ref[...], x_ref[...], g_ref[...], res_ref[...])
            dx_ref[...] = dx
            dg_ref[...] += part

        def matmul(dh_ref):
            acc = None
            for ref0, perm0, nb in BLOCK_RUNS:
                term = jnp.dot(a_ref[:, perm0 * IN_BLOCK:(perm0 + nb) * IN_BLOCK],
                               w_ref[ref0 * IN_BLOCK:(ref0 + nb) * IN_BLOCK, :], preferred_element_type=F32)
                acc = term if acc is None else acc + term
            dh_ref[...] = acc

        @pl.when(i == 0)
        def _():
            whole = pltpu.make_async_copy(w_hbm, w_ref, sem)
            whole.start()
            whole.wait()
            rows = pl.ds(pl.multiple_of(where_ref[2] * SHARD_W, BF16_SUBLANES), SHARD_W)
            part = pltpu.make_async_copy(far_hbm, w_ref.at[rows], sem)
            part.start()
            part.wait()
            dg_ref[...] = jnp.zeros_like(dg_ref)
            matmul(dh_even)

        @pl.when((i % 2 == 0) & (i > 0) & (i < n))
        def _():
            norm_bwd(dh_odd)
            matmul(dh_even)

        @pl.when(i % 2 == 1)
        def _():
            norm_bwd(dh_even)
            matmul(dh_odd)

        @pl.when(i == n)
        def _():
            norm_bwd(dh_odd)

    anysp = pl.BlockSpec(memory_space=pl.ANY)
    before = pl.BlockSpec((tm, d), lambda i: (jnp.maximum(i - 1, 0), 0))
    vec = pl.BlockSpec((1, d), lambda i: (0, 0))
    outs, carried = _carried_call(
        body, carry, grid=(n + 1,),
        in_specs=[pl.BlockSpec(memory_space=pltpu.SMEM),
                  pl.BlockSpec((tm, IN_WIDTH), lambda i: (jnp.minimum(i, n - 1), 0)), anysp, anysp, before, vec, before],
        out_specs=[before, vec],
        out_shape=[jax.ShapeDtypeStruct((s, d), F32), jax.ShapeDtypeStruct((1, d), F32)],
        scratch=[pltpu.VMEM((IN_WIDTH, d), BF16), pltpu.SemaphoreType.DMA, pltpu.VMEM((tm, d), F32),
                 pltpu.VMEM((tm, d), F32)],
        operands=(where, dproj, w_near, far, x, g, res), name="d_h", heavy=True)
    return (outs, carried) if carry else outs


def _rmsnorm_fwd(x, g, *, name):
    s, d = x.shape
    ts = min(512, s)

    def body(x_ref, g_ref, o_ref):
        xv = x_ref[...]
        r = lax.rsqrt(jnp.mean(xv * xv, axis=-1, keepdims=True) + EPS)
        o_ref[...] = ((xv * r) * g_ref[...]).astype(BF16)

    return pl.pallas_call(
        body, grid=(s // ts,),
        in_specs=[pl.BlockSpec((ts, d), lambda i: (i, 0)), pl.BlockSpec((1, d), lambda i: (0, 0))],
        out_specs=pl.BlockSpec((ts, d), lambda i: (i, 0)),
        out_shape=jax.ShapeDtypeStruct((s, d), BF16), name=name, compiler_params=_params())(x, g)


def _rmsnorm_bwd(dh, x, g, res, *, name, carry=None):
    s, d = x.shape
    ts = min(256, s)

    def body(ins, outs, scr):
        dh_ref, x_ref, g_ref, res_ref = ins
        dx_ref, dg_ref = outs
        dx, part = _norm_bwd_tile(dh_ref[...], x_ref[...], g_ref[...], res_ref[...])

        @pl.when(pl.program_id(0) == 0)
        def _():
            dg_ref[...] = part

        @pl.when(pl.program_id(0) > 0)
        def _():
            dg_ref[...] += part

        dx_ref[...] = dx

    row = pl.BlockSpec((ts, d), lambda i: (i, 0))
    vec = pl.BlockSpec((1, d), lambda i: (0, 0))
    outs, carried = _carried_call(
        body, carry, grid=(s // ts,), in_specs=[row, row, vec, row], out_specs=[row, vec],
        out_shape=[jax.ShapeDtypeStruct((s, d), F32), jax.ShapeDtypeStruct((1, d), F32)],
        scratch=[], operands=(dh, x, g, res), name=name)
    return (*outs, carried) if carry else tuple(outs)


MID_TILE = 256


def _gated_branches(y_refs, wup_ref, gl):
    d = D_MODEL
    us = [jnp.dot(y_refs[k][...], wup_ref[k], preferred_element_type=F32) for k in range(3)]
    sg = [_sigmoid(gl[:, k * d:(k + 1) * d]) for k in range(3)]
    return us, sg


def _mid_fwd(ya, yb, ym, proj, x, tgt, w_up, w_out, g_post):
    s, d = x.shape
    ts = MID_TILE

    def body(ya_ref, yb_ref, ym_ref, g_ref, x_ref, t_ref, wup_ref, wout_ref, gp_ref,
             m_ref, do_ref, dy_ref, dg_ref, loss_ref):
        us, sg = _gated_branches((ya_ref, yb_ref, ym_ref), wup_ref, g_ref[...])
        merged = (sg[0] * us[0] + sg[1] * us[1] + sg[2] * us[2]).astype(BF16)
        m_ref[...] = merged
        ov = jnp.dot(merged, wout_ref[...], preferred_element_type=F32)
        r = lax.rsqrt(jnp.mean(ov * ov, axis=-1, keepdims=True) + EPS)
        nh = ov * r
        gv = gp_ref[...]
        e = (x_ref[...] + nh * gv) - t_ref[...]
        lpart = 0.5 * jnp.sum(jnp.mean(e * e, axis=-1, keepdims=True), axis=0, keepdims=True)
        dy = e * (1.0 / d)
        dgp = jnp.sum(dy * nh, axis=0, keepdims=True)

        @pl.when(pl.program_id(0) == 0)
        def _():
            dg_ref[...] = dgp
            loss_ref[...] = jnp.broadcast_to(lpart, loss_ref.shape)

        @pl.when(pl.program_id(0) > 0)
        def _():
            dg_ref[...] += dgp
            loss_ref[...] += jnp.broadcast_to(lpart, loss_ref.shape)

        dn = dy * gv
        dy_ref[...] = dy
        do_ref[...] = (r * (dn - nh * jnp.mean(dn * nh, axis=-1, keepdims=True))).astype(BF16)

    row = pl.BlockSpec((ts, d), lambda i: (i, 0))
    ysp = pl.BlockSpec((ts, A_WIDTH), lambda i: (i, 0))
    vec = pl.BlockSpec((1, d), lambda i: (0, 0))
    return pl.pallas_call(
        body, grid=(s // ts,),
        in_specs=[ysp, ysp, ysp, pl.BlockSpec((ts, W_G), lambda i: (i, COL_G)), row, row,
                  pl.BlockSpec((3, A_WIDTH, d), lambda i: (0, 0, 0)), pl.BlockSpec((d, d), lambda i: (0, 0)), vec],
        out_specs=[row, row, row, vec, pl.BlockSpec((1, LANES), lambda i: (0, 0))],
        out_shape=[jax.ShapeDtypeStruct((s, d), BF16), jax.ShapeDtypeStruct((s, d), BF16),
                   jax.ShapeDtypeStruct((s, d), F32), jax.ShapeDtypeStruct((1, d), F32),
                   jax.ShapeDtypeStruct((1, LANES), F32)],
        name="mid_fwd", compiler_params=_params(heavy=True))(ya, yb, ym, proj, x, tgt, w_up, w_out, g_post)


def _mid_bwd(d_out, merged, ya, yb, ym, proj, w_up, w_out):
    s, d = merged.shape
    ts = MID_TILE
    last = s // ts - 1

    def body(do_ref, m_ref, ya_ref, yb_ref, ym_ref, g_ref, wup_ref, wout_ref,
             dp_ref, dya_ref, dyb_ref, dym_ref, dwup_hbm, dwout_hbm, dwup_acc, dwout_acc):
        i = pl.program_id(0)

        @pl.when(i == 0)
        def _():
            dwup_acc[...] = jnp.zeros_like(dwup_acc)
            dwout_acc[...] = jnp.zeros_like(dwout_acc)

        y_refs = (ya_ref, yb_ref, ym_ref)
        us, sg = _gated_branches(y_refs, wup_ref, g_ref[...])
        dov = do_ref[...]
        dwout_acc[...] += lax.dot_general(m_ref[...], dov, _DIMS["tn"], preferred_element_type=F32)
        dm = lax.dot_general(dov, wout_ref[...], _DIMS["nt"], preferred_element_type=F32)
        for k, dy_ref in enumerate((dya_ref, dyb_ref, dym_ref)):
            dp_ref[:, k * d:(k + 1) * d] = ((dm * us[k]) * (sg[k] * (1.0 - sg[k]))).astype(BF16)
            du = (sg[k] * dm).astype(BF16)
            dy_ref[...] = lax.dot_general(du, wup_ref[k], _DIMS["nt"], preferred_element_type=F32)
            dwup_acc[k] += lax.dot_general(y_refs[k][...], du, _DIMS["tn"], preferred_element_type=F32)

        @pl.when(i == last)
        def _():
            pltpu.sync_copy(dwup_acc, dwup_hbm)
            pltpu.sync_copy(dwout_acc, dwout_hbm)

    row = pl.BlockSpec((ts, d), lambda i: (i, 0))
    ysp = pl.BlockSpec((ts, A_WIDTH), lambda i: (i, 0))
    gsp = pl.BlockSpec((ts, W_G), lambda i: (i, COL_G))
    anysp = pl.BlockSpec(memory_space=pl.ANY)
    yshape = jax.ShapeDtypeStruct((s, A_WIDTH), F32)
    return pl.pallas_call(
        body, grid=(s // ts,),
        in_specs=[row, row, ysp, ysp, ysp, gsp, pl.BlockSpec((3, A_WIDTH, d), lambda i: (0, 0, 0)),
                  pl.BlockSpec((d, d), lambda i: (0, 0))],
        out_specs=[gsp, ysp, ysp, ysp, anysp, anysp],
        out_shape=[jax.ShapeDtypeStruct((s, IN_WIDTH), BF16), yshape, yshape, yshape,
                   jax.ShapeDtypeStruct((3, A_WIDTH, d), F32), jax.ShapeDtypeStruct((d, d), F32)],
        scratch_shapes=[pltpu.VMEM((3, A_WIDTH, d), F32), pltpu.VMEM((d, d), F32)],
        name="mid_bwd", compiler_params=_params(heavy=True))(d_out, merged, ya, yb, ym, proj, w_up, w_out)


def _conv_core(blk, prev, nxt, w, i, last, ts):
    c = A_WIDTH
    ab, ac, ax, az = blk[:, :c], blk[:, c:2 * c], blk[:, 2 * c:3 * c], blk[:, 3 * c:]
    cu = ac * ax
    cu_prev = (prev[7:8, c:2 * c] * prev[7:8, 2 * c:3 * c]) * jnp.where(i > 0, 1.0, 0.0)
    cu_next = (nxt[0:1, c:2 * c] * nxt[0:1, 2 * c:3 * c]) * jnp.where(i < last, 1.0, 0.0)
    row = lax.broadcasted_iota(jnp.int32, (ts, c), 0)
    cm1 = jnp.where(row == 0, cu_prev, pltpu.roll(cu, 1, 0))
    cp1 = jnp.where(row == ts - 1, cu_next, pltpu.roll(cu, ts - 1, 0))
    yc = cm1 * w[0:1] + cu * w[1:2] + cp1 * w[2:3]
    return ab, ac, ax, az, cu, cm1, cp1, yc, row


def _halo_specs(ts, width, col, nblk8):
    prev = pl.BlockSpec((8, width), lambda i: (jnp.maximum(i * (ts // 8) - 1, 0), col))
    nxt = pl.BlockSpec((8, width), lambda i: (jnp.minimum((i + 1) * (ts // 8), nblk8 - 1), col))
    return prev, nxt


def _conv_fwd(proj, w_conv):
    s = proj.shape[0]
    ts = 256
    last = s // ts - 1

    def body(a_ref, ap_ref, an_ref, w_ref, ya_ref):
        i = pl.program_id(0)
        ab, _, _, az, _, _, _, yc, _ = _conv_core(a_ref[...], ap_ref[...], an_ref[...], w_ref[...], i, last, ts)
        ya_ref[...] = ((ab * yc) * (az * _sigmoid(az))).astype(BF16)

    prev, nxt = _halo_specs(ts, W_A, COL_A, s // 8)
    return pl.pallas_call(
        body, grid=(s // ts,),
        in_specs=[pl.BlockSpec((ts, W_A), lambda i: (i, COL_A)), prev, nxt,
                  pl.BlockSpec((3, A_WIDTH), lambda i: (0, 0))],
        out_specs=pl.BlockSpec((ts, A_WIDTH), lambda i: (i, 0)),
        out_shape=jax.ShapeDtypeStruct((s, A_WIDTH), BF16), name="conv_fwd",
        compiler_params=_params())(proj, proj, proj, w_conv)


def _conv_bwd(proj, w_conv, dya, dproj):
    s = proj.shape[0]
    ts = 256
    last = s // ts - 1
    c = A_WIDTH

    def body(a_ref, ap_ref, an_ref, w_ref, d_ref, dp_ref, dn_ref, _, dproj_ref, dw_ref):
        i = pl.program_id(0)
        w = w_ref[...]
        prev, nxt = ap_ref[...], an_ref[...]
        ab, ac, ax, az, cu, cm1, cp1, yc, row = _conv_core(a_ref[...], prev, nxt, w, i, last, ts)
        sg = _sigmoid(az)
        sz = az * sg
        dya_v = d_ref[...]
        dyc = dya_v * sz * ab
        dproj_ref[:, :c] = (dya_v * sz * yc).astype(BF16)
        dproj_ref[:, 3 * c:] = (dya_v * (ab * yc) * (sg * (1.0 + az * (1.0 - sg)))).astype(BF16)

        def halo_dyc(a_row, d_row):
            azr = a_row[:, 3 * c:]
            return d_row * (azr * _sigmoid(azr)) * a_row[:, :c]

        dyc_prev = halo_dyc(prev[7:8], dp_ref[...][7:8]) * jnp.where(i > 0, 1.0, 0.0)
        dyc_next = halo_dyc(nxt[0:1], dn_ref[...][0:1]) * jnp.where(i < last, 1.0, 0.0)
        dyc_m1 = jnp.where(row == 0, dyc_prev, pltpu.roll(dyc, 1, 0))
        dyc_p1 = jnp.where(row == ts - 1, dyc_next, pltpu.roll(dyc, ts - 1, 0))
        dcu = dyc_p1 * w[0:1] + dyc * w[1:2] + dyc_m1 * w[2:3]
        dproj_ref[:, c:2 * c] = (dcu * ax).astype(BF16)
        dproj_ref[:, 2 * c:3 * c] = (dcu * ac).astype(BF16)
        dw = [jnp.sum(dyc * t, axis=0, keepdims=True) for t in (cm1, cu, cp1)]

        @pl.when(i == 0)
        def _():
            for k in range(3):
                dw_ref[k:k + 1, :] = dw[k]

        @pl.when(i > 0)
        def _():
            for k in range(3):
                dw_ref[k:k + 1, :] += dw[k]

    prev, nxt = _halo_specs(ts, W_A, COL_A, s // 8)
    dprev, dnxt = _halo_specs(ts, A_WIDTH, 0, s // 8)
    return pl.pallas_call(
        body, grid=(s // ts,),
        in_specs=[pl.BlockSpec((ts, W_A), lambda i: (i, COL_A)), prev, nxt,
                  pl.BlockSpec((3, A_WIDTH), lambda i: (0, 0)),
                  pl.BlockSpec((ts, A_WIDTH), lambda i: (i, 0)), dprev, dnxt,
                  pl.BlockSpec(memory_space=pl.ANY)],
        out_specs=[pl.BlockSpec((ts, W_A), lambda i: (i, COL_A)), pl.BlockSpec((3, A_WIDTH), lambda i: (0, 0))],
        out_shape=[jax.ShapeDtypeStruct(dproj.shape, BF16), jax.ShapeDtypeStruct((3, A_WIDTH), F32)],
        input_output_aliases={7: 0}, name="conv_bwd",
        compiler_params=_params())(proj, proj, proj, w_conv, dya, dya, dya, dproj)


def _rope_tables(s):
    half = ROT_DIM // 2
    dim = jnp.arange(LANES) % HEAD_DIM
    inv_freq = jnp.power(jnp.float32(ROPE_THETA), -(dim % half).astype(F32) * (2.0 / ROT_DIM))
    coarse = (jnp.arange(s // LANES) * LANES).astype(F32)[:, None] * inv_freq[None, :]
    fine = jnp.arange(LANES).astype(F32)[:, None] * inv_freq[None, :]
    cos_a, sin_a = jnp.cos(coarse)[:, None, :], jnp.sin(coarse)[:, None, :]
    cos_b, sin_b = jnp.cos(fine)[None], jnp.sin(fine)[None]
    cos = (cos_a * cos_b - sin_a * sin_b).reshape(s, LANES)
    sin = (sin_a * cos_b + cos_a * sin_b).reshape(s, LANES)
    first, second = (dim < half)[None, :], ((dim >= half) & (dim < ROT_DIM))[None, :]
    c = jnp.where(first | second, cos, 1.0)
    s1 = jnp.where(first, -sin, 0.0)
    s2 = jnp.where(second, sin, 0.0)
    return jnp.concatenate([c, s1, s2], axis=1)


def _rope(t, tab):
    return (t * tab[:, :LANES] + pltpu.roll(t, LANES - 8, 1) * tab[:, LANES:2 * LANES]
            + pltpu.roll(t, 8, 1) * tab[:, 2 * LANES:])


def _rope_transpose(dt, tab):
    return (dt * tab[:, :LANES] + pltpu.roll(dt * tab[:, LANES:2 * LANES], 8, 1)
            + pltpu.roll(dt * tab[:, 2 * LANES:], LANES - 8, 1))


def _rope_kv(proj, tab):
    s = proj.shape[0]
    nb = s // KV_PAD

    def body(kv_ref, t_ref, k_ref, v_ref):
        j = pl.program_id(0)
        inside = jnp.where((j > 0) & (j <= nb), 1.0, 0.0)
        kv = kv_ref[...]
        k_ref[...] = (_rope(kv[:, :LANES], t_ref[...]) * inside).astype(BF16)
        v_ref[...] = (kv[:, LANES:] * inside).astype(BF16)

    def src(j):
        return jnp.clip(j - 1, 0, nb - 1)

    o_spec = pl.BlockSpec((KV_PAD, LANES), lambda j: (j, 0))
    shp = jax.ShapeDtypeStruct((s + 2 * KV_PAD, LANES), BF16)
    return pl.pallas_call(
        body, grid=(nb + 2,),
        in_specs=[pl.BlockSpec((KV_PAD, W_KV), lambda j: (src(j), COL_KV)),
                  pl.BlockSpec((KV_PAD, 3 * LANES), lambda j: (src(j), 0))],
        out_specs=[o_spec, o_spec], out_shape=[shp, shp], name="rope_kv",
        compiler_params=_params())(proj, tab)


def _rope_kv_bwd(dkpad, dvpad, tab, dproj):
    s = tab.shape[0]
    nb = s // KV_PAD

    def body(dk_ref, dv_ref, t_ref, _, dp_ref):
        dp_ref[:, :LANES] = _rope_transpose(dk_ref[...], t_ref[...]).astype(BF16)
        dp_ref[:, LANES:] = dv_ref[...].astype(BF16)

    pad_spec = pl.BlockSpec((KV_PAD, LANES), lambda j: (j + 1, 0))
    return pl.pallas_call(
        body, grid=(nb,),
        in_specs=[pad_spec, pad_spec, pl.BlockSpec((KV_PAD, 3 * LANES), lambda j: (j, 0)),
                  pl.BlockSpec(memory_space=pl.ANY)],
        out_specs=pl.BlockSpec((KV_PAD, W_KV), lambda j: (j, COL_KV)),
        out_shape=jax.ShapeDtypeStruct(dproj.shape, BF16), input_output_aliases={3: 0},
        name="rope_kv_bwd", compiler_params=_params())(dkpad, dvpad, tab, dproj)


def _window_start(n):
    return pl.multiple_of((n - 1) * WINDOW_BLOCK + KV_PAD, WINDOW_BLOCK)


def _window_operands(k_ref, v_ref, n, lo):
    start = _window_start(n)
    kw = k_ref[pl.ds(start, 3 * WINDOW_BLOCK), :].astype(F32)
    vw = v_ref[pl.ds(start, 3 * WINDOW_BLOCK), :].astype(F32)
    kr, vr = pltpu.roll(kw, HALF_LANES, 1), pltpu.roll(vw, HALF_LANES, 1)
    k2 = (jnp.where(lo, kw, kr).astype(BF16), jnp.where(lo, kr, kw).astype(BF16))
    v2 = (jnp.where(lo, vw, vr).astype(BF16), jnp.where(lo, vr, vw).astype(BF16))
    return k2, v2


HEADS_PER_GROUP = 4
SWA_FWD_BLOCKS = 1
SWA_BWD_BLOCKS = 2


def _window_bias():
    wb = WINDOW_BLOCK
    qi = lax.broadcasted_iota(jnp.int32, (wb, 3 * wb), 0)
    kj = lax.broadcasted_iota(jnp.int32, (wb, 3 * wb), 1)
    band = (kj >= qi) & (kj <= qi + 2 * wb)
    cases = jnp.stack([band & (kj >= wb), band, band & (kj < 2 * wb)])
    return jnp.where(cases, 0.0, -jnp.inf).astype(F32)


def _block_bias(bias_ref, n, n_blocks):
    case = jnp.where(n == 0, 0, jnp.where(n == n_blocks - 1, 2, 1))
    one = bias_ref[case]
    return jnp.concatenate([one] * HEADS_PER_GROUP, axis=0)


def _stack_heads(pair0, pair1, lo):
    return jnp.concatenate([jnp.where(lo, pair0, 0.0), jnp.where(lo, 0.0, pair0),
                            jnp.where(lo, pair1, 0.0), jnp.where(lo, 0.0, pair1)], axis=0)


def _unstack_pair(stacked, i, lo):
    wb = WINDOW_BLOCK
    return jnp.where(lo, stacked[2 * i * wb:(2 * i + 1) * wb], stacked[(2 * i + 1) * wb:(2 * i + 2) * wb])


def _sink_column(sink_ref, g):
    wb = WINDOW_BLOCK
    return jnp.concatenate([jnp.full((wb, 1), sink_ref[0, HEADS_PER_GROUP * g + i], F32)
                            for i in range(HEADS_PER_GROUP)], axis=0)


def _head_exp(q4, k2g, bias, sink):
    sc = lax.dot_general(q4, k2g, _DIMS["nt"], preferred_element_type=F32) * (HEAD_DIM ** -0.5) + bias
    m = jnp.maximum(jnp.max(sc, axis=1, keepdims=True), sink)
    return jnp.exp(sc - m).astype(BF16), jnp.exp(sink - m)


def _swa_fwd(proj, kpad, vpad, tab, bias, sink, *, carry=None):
    s = proj.shape[0]
    wb = WINDOW_BLOCK

    def body(b_ref, k_ref, v_ref, t_ref, bias_ref, sink_ref, o_ref, y_ref):
        lo = lax.broadcasted_iota(jnp.int32, (wb, LANES), 1) < HALF_LANES
        lo_w = lax.broadcasted_iota(jnp.int32, (3 * wb, LANES), 1) < HALF_LANES
        for sub in range(SWA_FWD_BLOCKS):
            n = pl.program_id(0) * SWA_FWD_BLOCKS + sub
            rows = slice(sub * wb, (sub + 1) * wb)
            k2, v2 = _window_operands(k_ref, v_ref, n, lo_w)
            valid = _block_bias(bias_ref, n, s // wb)
            tab_v = t_ref[rows, :]
            ones = jnp.ones((3 * wb, LANES), BF16)
            for g in range(2):
                qr = [_rope(b_ref[rows, (2 * g + i) * LANES:(2 * g + i + 1) * LANES], tab_v) for i in range(2)]
                q4 = _stack_heads(qr[0], qr[1], lo).astype(BF16)
                e, es = _head_exp(q4, k2[g], valid, _sink_column(sink_ref, g))
                ox = jnp.dot(e, jnp.concatenate([v2[g], ones], axis=1), preferred_element_type=F32)
                o4 = ox[:, :LANES] * (1.0 / (ox[:, LANES:] + es))
                for i in range(2):
                    cols = slice((2 * g + i) * LANES, (2 * g + i + 1) * LANES)
                    op = _unstack_pair(o4, i, lo)
                    o_ref[rows, cols] = op
                    zp = b_ref[rows, A_WIDTH + cols.start:A_WIDTH + cols.stop]
                    y_ref[rows, cols] = (op * (zp * _sigmoid(zp))).astype(BF16)

    tq = SWA_FWD_BLOCKS * wb
    pad_spec = pl.BlockSpec((s + 2 * KV_PAD, LANES), lambda n: (0, 0))
    o_spec = pl.BlockSpec((tq, A_WIDTH), lambda n: (n, 0))
    outs, carried = _carried_call(
        lambda ins, outs, scr: body(*ins, *outs), carry, grid=(s // tq,),
        in_specs=[pl.BlockSpec((tq, W_B), lambda n: (n, COL_B)), pad_spec, pad_spec,
                  pl.BlockSpec((tq, 3 * LANES), lambda n: (n, 0)),
                  pl.BlockSpec(bias.shape, lambda n: (0, 0, 0)), pl.BlockSpec(memory_space=pltpu.SMEM)],
        out_specs=[o_spec, o_spec],
        out_shape=[jax.ShapeDtypeStruct((s, A_WIDTH), F32), jax.ShapeDtypeStruct((s, A_WIDTH), BF16)],
        scratch=[], operands=(proj, kpad, vpad, tab, bias, sink), name="swa_fwd")
    return (*outs, carried) if carry else tuple(outs)


def _swa_bwd(proj, kpad, vpad, tab, bias, sink, o_attn, dyb, dproj):
    s = proj.shape[0]
    wb = WINDOW_BLOCK
    scale = HEAD_DIM ** -0.5

    def body(b_ref, k_ref, v_ref, t_ref, bias_ref, sink_ref, o_ref, dy_ref, _, dp_ref, dk_ref, dv_ref, ds_ref):
        @pl.when(pl.program_id(0) == 0)
        def _():
            dk_ref[...] = jnp.zeros_like(dk_ref)
            dv_ref[...] = jnp.zeros_like(dv_ref)
            ds_ref[...] = jnp.zeros_like(ds_ref)

        lo = lax.broadcasted_iota(jnp.int32, (wb, LANES), 1) < HALF_LANES
        lo_w = lax.broadcasted_iota(jnp.int32, (3 * wb, LANES), 1) < HALF_LANES
        for sub in range(SWA_BWD_BLOCKS):
            n = pl.program_id(0) * SWA_BWD_BLOCKS + sub
            rows = slice(sub * wb, (sub + 1) * wb)
            k2, v2 = _window_operands(k_ref, v_ref, n, lo_w)
            valid = _block_bias(bias_ref, n, s // wb)
            tab_v = t_ref[rows, :]
            ones = jnp.ones((3 * wb, LANES), BF16)
            dks, dvs = [], []
            for g in range(2):
                qr, op, do = [], [], []
                for i in range(2):
                    cols = slice((2 * g + i) * LANES, (2 * g + i + 1) * LANES)
                    zcols = slice(A_WIDTH + cols.start, A_WIDTH + cols.stop)
                    qr.append(_rope(b_ref[rows, cols], tab_v))
                    zp = b_ref[rows, zcols]
                    sg = _sigmoid(zp)
                    op.append(o_ref[rows, cols])
                    dyp = dy_ref[rows, cols]
                    do.append(dyp * (zp * sg))
                    dp_ref[rows, zcols] = (dyp * op[i] * (sg * (1.0 + zp * (1.0 - sg)))).astype(BF16)
                q4 = _stack_heads(qr[0], qr[1], lo).astype(BF16)
                do4 = _stack_heads(do[0], do[1], lo)
                o4 = jnp.concatenate([op[0], op[0], op[1], op[1]], axis=0)
                e, es = _head_exp(q4, k2[g], valid, _sink_column(sink_ref, g))
                inv = 1.0 / (jnp.dot(e, ones, preferred_element_type=F32) + es)
                prob = e.astype(F32) * jnp.concatenate([inv, inv, inv], axis=1)
                delta = jnp.sum(do4 * o4, axis=1, keepdims=True)
                do4b = do4.astype(BF16)
                dprob = lax.dot_general(do4b, v2[g], _DIMS["nt"], preferred_element_type=F32)
                dsc = (prob * (dprob - delta)).astype(BF16)
                sink_terms = (es * inv[:, :1]) * delta
                for i in range(HEADS_PER_GROUP):
                    h = HEADS_PER_GROUP * g + i
                    dsink = -jnp.sum(sink_terms[i * wb:(i + 1) * wb], axis=0, keepdims=True)
                    ds_ref[h:h + 1, :] += jnp.broadcast_to(dsink, (1, LANES))
                dq4 = jnp.dot(dsc, k2[g], preferred_element_type=F32) * scale
                for i in range(2):
                    cols = slice((2 * g + i) * LANES, (2 * g + i + 1) * LANES)
                    dp_ref[rows, cols] = _rope_transpose(_unstack_pair(dq4, i, lo), tab_v).astype(BF16)
                dk2 = lax.dot_general(dsc, q4, _DIMS["tn"], preferred_element_type=F32) * scale
                dv2 = lax.dot_general(prob.astype(BF16), do4b, _DIMS["tn"], preferred_element_type=F32)
                dks.append(dk2 + pltpu.roll(dk2, HALF_LANES, 1))
                dvs.append(dv2 + pltpu.roll(dv2, HALF_LANES, 1))
            start = _window_start(n)
            dk_ref[pl.ds(start, 3 * wb), :] += jnp.where(lo_w, dks[0], dks[1])
            dv_ref[pl.ds(start, 3 * wb), :] += jnp.where(lo_w, dvs[0], dvs[1])

    tq = SWA_BWD_BLOCKS * wb
    pad_spec = pl.BlockSpec((s + 2 * KV_PAD, LANES), lambda n: (0, 0))
    blk = pl.BlockSpec((tq, A_WIDTH), lambda n: (n, 0))
    bsp = pl.BlockSpec((tq, W_B), lambda n: (n, COL_B))
    pad_shape = jax.ShapeDtypeStruct((s + 2 * KV_PAD, LANES), F32)
    return pl.pallas_call(
        body, grid=(s // tq,),
        in_specs=[bsp, pad_spec, pad_spec, pl.BlockSpec((tq, 3 * LANES), lambda n: (n, 0)),
                  pl.BlockSpec(bias.shape, lambda n: (0, 0, 0)), pl.BlockSpec(memory_space=pltpu.SMEM), blk, blk,
                  pl.BlockSpec(memory_space=pl.ANY)],
        out_specs=[bsp, pad_spec, pad_spec, pl.BlockSpec((8, LANES), lambda n: (0, 0))],
        out_shape=[jax.ShapeDtypeStruct(dproj.shape, BF16), pad_shape, pad_shape,
                   jax.ShapeDtypeStruct((8, LANES), F32)],
        input_output_aliases={8: 0}, name="swa_bwd",
        compiler_params=_params())(proj, kpad, vpad, tab, bias, sink, o_attn, dyb, dproj)


def _mem_exp(qh, mk):
    sc = lax.dot_general(qh, mk, _DIMS["nt"], preferred_element_type=F32) * (MEM_HEAD_DIM ** -0.5)
    return jnp.exp(sc - jnp.max(sc, axis=1, keepdims=True)).astype(BF16)


def _mem_fwd(proj, mkv):
    s = proj.shape[0]
    ts = 512
    mlen = mkv.shape[0]

    def body(m_ref, kv_ref, o_ref, y_ref):
        ones = jnp.ones((mlen, LANES), BF16)
        for h in range(MEM_HEADS):
            cols = slice(h * LANES, (h + 1) * LANES)
            mk = kv_ref[:, cols].astype(BF16)
            mv = kv_ref[:, MEM_WIDTH + h * LANES:MEM_WIDTH + (h + 1) * LANES].astype(BF16)
            e = _mem_exp(m_ref[:, cols].astype(BF16), mk)
            ox = jnp.dot(e, jnp.concatenate([mv, ones], axis=1), preferred_element_type=F32)
            oh = ox[:, :LANES] * (1.0 / ox[:, LANES:])
            o_ref[:, cols] = oh
            zh = m_ref[:, MEM_WIDTH + h * LANES:MEM_WIDTH + (h + 1) * LANES]
            y_ref[:, cols] = (oh * (zh * _sigmoid(zh))).astype(BF16)

    o_spec = pl.BlockSpec((ts, MEM_WIDTH), lambda i: (i, 0))
    return pl.pallas_call(
        body, grid=(s // ts,),
        in_specs=[pl.BlockSpec((ts, W_M), lambda i: (i, COL_M)),
                  pl.BlockSpec((mlen, 2 * MEM_WIDTH), lambda i: (0, 0))],
        out_specs=[o_spec, o_spec],
        out_shape=[jax.ShapeDtypeStruct((s, MEM_WIDTH), F32), jax.ShapeDtypeStruct((s, MEM_WIDTH), BF16)],
        name="mem_fwd", compiler_params=_params())(proj, mkv)


def _mem_bwd(proj, mkv, o_mem, dym, dproj, *, carry=None):
    s = proj.shape[0]
    ts = 512
    mlen = mkv.shape[0]
    scale = MEM_HEAD_DIM ** -0.5

    def body(m_ref, kv_ref, o_ref, dy_ref, _, dp_ref, dkv_ref):
        @pl.when(pl.program_id(0) == 0)
        def _():
            dkv_ref[...] = jnp.zeros_like(dkv_ref)

        ones = jnp.ones((mlen, LANES), BF16)
        for h in range(MEM_HEADS):
            cols = slice(h * LANES, (h + 1) * LANES)
            vcols = slice(MEM_WIDTH + h * LANES, MEM_WIDTH + (h + 1) * LANES)
            mk = kv_ref[:, cols].astype(BF16)
            mv = kv_ref[:, vcols].astype(BF16)
            qh = m_ref[:, cols].astype(BF16)
            zh = m_ref[:, vcols]
            sg = _sigmoid(zh)
            oh = o_ref[:, cols]
            dyh = dy_ref[:, cols]
            doh = dyh * (zh * sg)
            dp_ref[:, vcols] = (dyh * oh * (sg * (1.0 + zh * (1.0 - sg)))).astype(BF16)
            e = _mem_exp(qh, mk)
            inv = 1.0 / jnp.dot(e, ones, preferred_element_type=F32)
            prob = e.astype(F32) * jnp.concatenate([inv] * (mlen // LANES), axis=1)
            delta = jnp.sum(doh * oh, axis=1, keepdims=True)
            dohb = doh.astype(BF16)
            dprob = lax.dot_general(dohb, mv, _DIMS["nt"], preferred_element_type=F32)
            dsc = (prob * (dprob - delta)).astype(BF16)
            dp_ref[:, cols] = (jnp.dot(dsc, mk, preferred_element_type=F32) * scale).astype(BF16)
            dkv_ref[:, cols] += lax.dot_general(dsc, qh, _DIMS["tn"], preferred_element_type=F32) * scale
            dkv_ref[:, vcols] += lax.dot_general(prob.astype(BF16), dohb, _DIMS["tn"],
                                                 preferred_element_type=F32)

    blk = pl.BlockSpec((ts, MEM_WIDTH), lambda i: (i, 0))
    msp = pl.BlockSpec((ts, W_M), lambda i: (i, COL_M))
    kvsp = pl.BlockSpec((mlen, 2 * MEM_WIDTH), lambda i: (0, 0))
    outs, carried = _carried_call(
        lambda ins, outs, scr: body(*ins, *outs), carry, grid=(s // ts,),
        in_specs=[msp, kvsp, blk, blk, pl.BlockSpec(memory_space=pl.ANY)],
        out_specs=[msp, kvsp],
        out_shape=[jax.ShapeDtypeStruct(dproj.shape, BF16), jax.ShapeDtypeStruct(mkv.shape, F32)],
        scratch=[], operands=(proj, mkv, o_mem, dym, dproj), name="mem_bwd", aliases={4: 0})
    return (*outs, carried) if carry else tuple(outs)


def _forward_backward(x, mem, tgt, proj, w_conv, sink, g_mem, late_weights, g_post, early_exchange, kv_exchange):
    s = x.shape[0]
    tab = _rope_tables(s)
    bias = _window_bias()

    ya = _conv_fwd(proj, w_conv)
    kpad, vpad = _rope_kv(proj, tab)
    o_attn, yb, *arrived = _swa_fwd(proj, kpad, vpad, tab, bias, sink, carry=late_weights[0])
    w_kv, w_up, w_out = late_weights[1](arrived[0] if arrived else None)
    mn = _rmsnorm_fwd(mem, g_mem, name="mem_norm")
    mkv = _matmul(mn, w_kv, mode="nn", out_dtype=F32, tm=256, tn=1024, tk=D_MODEL, name="mem_kv")
    o_mem, ym = _mem_fwd(proj, mkv)
    merged, d_out, dy, dg_post, loss = _mid_fwd(ya, yb, ym, proj, x, tgt, w_up, w_out, g_post)
    dproj, d_ya, d_yb, d_ym, dw_up, dw_out = _mid_bwd(d_out, merged, ya, yb, ym, proj, w_up, w_out)

    dproj, dw_conv = _conv_bwd(proj, w_conv, d_ya, dproj)
    dproj, dkpad, dvpad, dsink = _swa_bwd(proj, kpad, vpad, tab, bias, sink, o_attn, d_yb, dproj)
    dproj = _rope_kv_bwd(dkpad, dvpad, tab, dproj)
    dproj, d_mkv, *early = _mem_bwd(proj, mkv, o_mem, d_ym, dproj, carry=early_exchange(dw_up, dw_out))

    dw_kv = _matmul(mn, d_mkv, mode="tn", out_dtype=F32, tm=1024, tn=1024, tk=256, name="dw_kv")
    d_mn = _matmul(d_mkv, w_kv, mode="nt", out_dtype=F32, tm=256, tn=1024, tk=D_MODEL, name="d_mn")
    _, dg_mem, *early_kv = _rmsnorm_bwd(d_mn, mem, g_mem, d_mn, name="mem_norm_bwd", carry=kv_exchange(dw_kv))

    return dict(loss=loss, dproj=dproj, dy=dy, w_conv=dw_conv, sink=dsink, g_mem=dg_mem,
                w_kv=dw_kv, w_up=dw_up, w_out=dw_out, g_post=dg_post, early=early[0] if early else None,
                early_kv=early_kv[0] if early_kv else None)


N_DEV = 8


def _position():
    return lax.axis_index("x"), lax.axis_index("y"), lax.axis_index("c")


def _other_chips(x, y):
    return (((1 - x, y), 2 * (1 - x) + y), ((x, 1 - y), 2 * x + (1 - y)), ((1 - x, 1 - y), 2 * (1 - x) + (1 - y)))


def _remote(src, dst, send_sems, recv_sems, k, device):
    return pltpu.make_async_remote_copy(src_ref=src, dst_ref=dst, send_sem=send_sems.at[k], recv_sem=recv_sems.at[k],
                                        device_id=device, device_id_type=MESH)


def _rows_half(ref, hf):
    rh = ref.shape[0] // 2
    return ref.at[pl.ds(pl.multiple_of(hf * rh, 8), rh)]


def _gather_weights(shards, small=None, relations=(0, 1, 2), into=None):
    n = len(shards)
    k = 0 if small is None else 1

    def peers(x, y):
        return [(r, chip, idx) for r, (chip, idx) in enumerate(_other_chips(x, y)) if r in relations]

    def ici(ins, outs, sems, a, r, chip, src_chip, c):
        return _remote(_rows_half(ins[a], c), _rows_half(outs[a].at[src_chip], c), sems[0], sems[1], 3 * a + r,
                       (*chip, c))

    def whole(ins, outs, sems, r, chip, src_chip, c):
        return _remote(ins[n], outs[n].at[src_chip], sems[0], sems[1], 3 * n + r, (*chip, c))

    def d2d(outs, sems, a, r, idx, hf, x, y, c):
        half = _rows_half(outs[a].at[idx], hf)
        return _remote(half, half, sems[2], sems[3], 3 * a + r, (x, y, 1 - c))

    def start(ins, outs, sems):
        x, y, c = _position()
        me = 2 * x + y
        for a in range(n):
            for r, chip, _ in peers(x, y):
                ici(ins, outs, sems, a, r, chip, me, c).start()
        for r, (chip, _) in enumerate(_other_chips(x, y)):
            if k:
                whole(ins, outs, sems, r, chip, me, c).start()

    def finish(ins, outs, sems):
        x, y, c = _position()
        me = 2 * x + y
        for a in range(n):
            for r, chip, idx in peers(x, y):
                ici(ins, outs, sems, a, r, chip, idx, c).wait_recv()
                d2d(outs, sems, a, r, idx, c, x, y, c).start()
        for a in range(n):
            for r, chip, idx in peers(x, y):
                d2d(outs, sems, a, r, idx, 1 - c, x, y, c).wait_recv()
        for r, (chip, idx) in enumerate(_other_chips(x, y)):
            if k:
                whole(ins, outs, sems, r, chip, idx, c).wait_recv()
                whole(ins, outs, sems, r, chip, me, c).wait_send()
        for a in range(n):
            for r, chip, idx in peers(x, y):
                ici(ins, outs, sems, a, r, chip, me, c).wait_send()
                d2d(outs, sems, a, r, idx, c, x, y, c).wait_send()

    operands = list(shards) + ([small] if k else [])
    shapes = [jax.ShapeDtypeStruct((N_CHIPS,) + s.shape, s.dtype) for s in operands]
    aliases = {}
    if into is not None:
        assert len(into) == len(operands)
        aliases = {len(operands) + a: a for a in range(len(into))}
        operands += list(into)
    return _Carry(operands, shapes,
                  [pltpu.SemaphoreType.DMA((3 * (n + k),)), pltpu.SemaphoreType.DMA((3 * (n + k),)),
                   pltpu.SemaphoreType.DMA((3 * n,)), pltpu.SemaphoreType.DMA((3 * n,))], start, finish, aliases)


def _pair_exchange(send):
    n = len(send)

    def copies(ins, outs, sems):
        x, y, c = _position()
        return [_remote(ins[a], outs[a], sems[0], sems[1], a, (x, y, 1 - c)) for a in range(n)]

    def start(ins, outs, sems):
        for cp in copies(ins, outs, sems):
            cp.start()

    def finish(ins, outs, sems):
        for cp in copies(ins, outs, sems):
            cp.wait()

    return _Carry(send, [jax.ShapeDtypeStruct(p.shape, p.dtype) for p in send],
                  [pltpu.SemaphoreType.DMA((n,)), pltpu.SemaphoreType.DMA((n,))], start, finish)


def _chip_exchange(sums):
    n = len(sums)

    def copies(ins, outs, sems):
        x, y, c = _position()
        return [_remote(ins[a].at[idx], outs[a].at[r], sems[0], sems[1], 3 * a + r, (*chip, c))
                for a in range(n) for r, (chip, idx) in enumerate(_other_chips(x, y))]

    def start(ins, outs, sems):
        for cp in copies(ins, outs, sems):
            cp.start()

    def finish(ins, outs, sems):
        for cp in copies(ins, outs, sems):
            cp.wait()

    return _Carry(sums, [jax.ShapeDtypeStruct((3,) + p.shape[1:], p.dtype) for p in sums],
                  [pltpu.SemaphoreType.DMA((3 * n,)), pltpu.SemaphoreType.DMA((3 * n,))], start, finish)


def _pair_share(pairs):
    n = len(pairs)

    def start(ins, outs, sems):
        x, y, c = _position()
        for a in range(n):
            _remote(outs[a].at[c], outs[a].at[c], sems[0], sems[1], a, (x, y, 1 - c)).start()

    def finish(ins, outs, sems):
        x, y, c = _position()
        for a in range(n):
            _remote(outs[a].at[1 - c], outs[a].at[1 - c], sems[0], sems[1], a, (x, y, 1 - c)).wait_recv()
        for a in range(n):
            _remote(outs[a].at[c], outs[a].at[c], sems[0], sems[1], a, (x, y, 1 - c)).wait_send()

    return _Carry(pairs, [jax.ShapeDtypeStruct(p.shape, p.dtype) for p in pairs],
                  [pltpu.SemaphoreType.DMA((n,)), pltpu.SemaphoreType.DMA((n,))], start, finish,
                  aliases={a: a for a in range(n)})


def _small_allreduce(pack, share):
    rows, width = pack.shape
    n_share = len(share.ins)

    def body(p_ref, *refs):
        share_in, o_ref, share_out = refs[:n_share], refs[n_share], refs[n_share + 1:2 * n_share + 1]
        buf, send_sems, recv_sems = refs[2 * n_share + 1:2 * n_share + 4]
        share_sems = refs[2 * n_share + 4:]
        share.start(share_in, share_out, share_sems)
        x, y, c = _position()
        me = 4 * x + 2 * y + c
        buf[me] = p_ref[...]
        peers = []
        for r in range(1, N_DEV):
            fx, fy, fc = (r >> 2) & 1, (r >> 1) & 1, r & 1
            px, py, pc = (1 - x if fx else x), (1 - y if fy else y), (1 - c if fc else c)
            peers.append(((px, py, pc), 4 * px + 2 * py + pc))
        sends = [_remote(p_ref, buf.at[me], send_sems, recv_sems, r, dev) for r, (dev, _) in enumerate(peers)]
        for cp in sends:
            cp.start()
        for r, (dev, idx) in enumerate(peers):
            _remote(p_ref, buf.at[idx], send_sems, recv_sems, r, dev).wait_recv()
        for cp in sends:
            cp.wait_send()
        acc = buf[0]
        for k in range(1, N_DEV):
            acc = acc + buf[k]
        o_ref[...] = acc
        share.finish(share_in, share_out, share_sems)

    vm = pl.BlockSpec(memory_space=pltpu.VMEM)
    red, *shared = pl.pallas_call(
        body, in_specs=[vm] + [_HBM] * n_share, out_specs=[vm] + [_HBM] * n_share,
        out_shape=[jax.ShapeDtypeStruct(pack.shape, F32)] + share.out_shapes,
        scratch_shapes=[pltpu.VMEM((N_DEV, rows, width), F32), pltpu.SemaphoreType.DMA((N_DEV - 1,)),
                        pltpu.SemaphoreType.DMA((N_DEV - 1,))] + share.sems,
        input_output_aliases={1 + i: 1 + o for i, o in share.aliases.items()},
        name="small_allreduce")(pack, *share.ins)
    return red, shared


ROW_TILE_MAX = 512
SUM_TILE_MAX = 2048
BF16_SUBLANES = 16


def _row_tile(rows, most=ROW_TILE_MAX):
    if rows <= most:
        return rows
    return max(t for t in range(BF16_SUBLANES, most + 1, BF16_SUBLANES) if rows % t == 0)


def _pair_add(keep, recv, name):
    nj, rh, cols = keep.shape
    tr = _row_tile(rh, SUM_TILE_MAX)

    def body(k_ref, r_ref, o_ref):
        o_ref[...] = (k_ref[...].astype(F32) + r_ref[...].astype(F32)).astype(BF16)

    blk = pl.BlockSpec((None, tr, cols), lambda j, i: (j, i, 0))
    return pl.pallas_call(body, grid=(nj, rh // tr), in_specs=[blk, blk], out_specs=blk,
                          out_shape=jax.ShapeDtypeStruct(keep.shape, BF16), name=name,
                          compiler_params=_params())(keep, recv)


def _chip_add(sums, recv, where, name):
    _, rh, cols = sums.shape
    tr = _row_tile(rh, SUM_TILE_MAX)

    def body(w_ref, s_ref, r_ref, o_ref):
        o_ref[...] = ((s_ref[...].astype(F32) + r_ref[0].astype(F32)) + r_ref[1].astype(F32)) + r_ref[2].astype(F32)

    grid_spec = pltpu.PrefetchScalarGridSpec(
        num_scalar_prefetch=1, grid=(rh // tr,),
        in_specs=[pl.BlockSpec((None, tr, cols), lambda i, w_ref: (w_ref[0], i, 0)),
                  pl.BlockSpec((3, tr, cols), lambda i, w_ref: (0, i, 0))],
        out_specs=pl.BlockSpec((None, tr, cols), lambda i, w_ref: (w_ref[1], i, 0)))
    return pl.pallas_call(body, grid_spec=grid_spec, out_shape=jax.ShapeDtypeStruct((2, rh, cols), F32),
                          name=name, compiler_params=_params())(where, sums, recv)


def _adamw(w, g, m, v, name):
    rows, cols = w.shape
    tr = _row_tile(rows)
    assert rows % tr == 0

    def body(w_ref, g_ref, m_ref, v_ref, d_ref, mo_ref, vo_ref):
        gv = g_ref[...]
        m_new = ADAM_B1 * m_ref[...] + (1.0 - ADAM_B1) * gv
        v_new = ADAM_B2 * v_ref[...] + (1.0 - ADAM_B2) * jnp.square(gv)
        m_hat = m_new / (1.0 - ADAM_B1 ** ADAM_STEP)
        v_hat = v_new / (1.0 - ADAM_B2 ** ADAM_STEP)
        d_ref[...] = -ADAM_LR * (m_hat / (jnp.sqrt(v_hat) + ADAM_EPS) + ADAM_WD * w_ref[...])
        mo_ref[...] = m_new
        vo_ref[...] = v_new

    blk = pl.BlockSpec((tr, cols), lambda i: (i, 0))
    shp = jax.ShapeDtypeStruct((rows, cols), F32)
    return pl.pallas_call(body, grid=(rows // tr,), in_specs=[blk] * 4, out_specs=[blk] * 3,
                          out_shape=[shp] * 3, name=name, compiler_params=_params())(w, g, m, v)


def _adamw_halves(w, g2, m, v, name):
    rows, cols = w.shape
    half = cols // 2
    tr = _row_tile(rows)

    def body(w_ref, g_ref, m_ref, v_ref, go_ref, d_ref, mo_ref, vo_ref):
        gv = g_ref[...]
        go_ref[...] = gv
        m_new = ADAM_B1 * m_ref[...] + (1.0 - ADAM_B1) * gv
        v_new = ADAM_B2 * v_ref[...] + (1.0 - ADAM_B2) * jnp.square(gv)
        m_hat = m_new / (1.0 - ADAM_B1 ** ADAM_STEP)
        v_hat = v_new / (1.0 - ADAM_B2 ** ADAM_STEP)
        d_ref[...] = -ADAM_LR * (m_hat / (jnp.sqrt(v_hat) + ADAM_EPS) + ADAM_WD * w_ref[...])
        mo_ref[...] = m_new
        vo_ref[...] = v_new

    blk = pl.BlockSpec((tr, half), lambda hf, i: (i, hf))
    gsp = pl.BlockSpec((None, tr, half), lambda hf, i: (hf, i, 0))
    shp = jax.ShapeDtypeStruct((rows, cols), F32)
    return pl.pallas_call(body, grid=(2, rows // tr), in_specs=[blk, gsp, blk, blk], out_specs=[blk] * 4,
                          out_shape=[shp] * 4, name=name, compiler_params=_params())(w, g2, m, v)


SHARD_W = IN_WIDTH // N_CHIPS


def _half_major(a):
    r, c = a.shape
    return a.reshape(N_CHIPS, 2, r // N_CHIPS // 2, c).transpose(1, 0, 2, 3)


def kernel(x, mem, g_pre, w_in, w_conv, attn_sink, g_mem, w_mem_kv, w_up_a, w_up_b, w_up_m, w_out, g_post, loss_target, m_g_pre, m_w_in, m_w_conv, m_attn_sink, m_g_mem, m_w_mem_kv, m_w_up_a, m_w_up_b, m_w_up_m, m_w_out, m_g_post, v_g_pre, v_w_in, v_w_conv, v_attn_sink, v_g_mem, v_w_mem_kv, v_w_up_a, v_w_up_b, v_w_up_m, v_w_out, v_g_post):
    xi, yi, ci = _position()
    chip = 2 * xi + yi
    where = jnp.stack([chip, ci, N_CHIPS - 1 - chip]).astype(jnp.int32)

    own = [w_in[0].T.astype(BF16), w_mem_kv[0].astype(BF16),
           jnp.concatenate([w_up_a[0], w_up_b[0], w_up_m[0]], axis=0).astype(BF16), w_out[0].astype(BF16)]
    own_conv = jnp.pad(w_conv[0], ((0, 5), (0, 0)))

    def pieces(mine, got):
        got = lax.dynamic_update_slice_in_dim(got, mine[None], chip, axis=0)
        return [got[j] for j in range(N_CHIPS)]

    diag = N_CHIPS - 1 - chip
    proj, h, h_t, got_near, got_conv, got_far = _proj_near(x[0], g_pre, own[0], own_conv, where)
    w_near = lax.dynamic_update_slice_in_dim(got_near, own[0][None], chip, axis=0).reshape(IN_WIDTH, D_MODEL)
    far = lax.dynamic_index_in_dim(got_far, diag, 0, keepdims=False)
    proj = _proj_far(h, w_near, far, where, into=proj)
    w_conv_full = jnp.concatenate([p[:3] for p in pieces(own_conv, got_conv)], axis=1)

    def late_weights(gathered):
        w_kv_full = jnp.concatenate(pieces(own[1], gathered[0]), axis=0)
        up_pieces = pieces(own[2], gathered[1])
        w_up_full = jnp.stack([jnp.concatenate([p[k * A_WIDTH:(k + 1) * A_WIDTH] for p in up_pieces], axis=1)
                               for k in range(3)])
        return w_kv_full, w_up_full, jnp.concatenate(pieces(own[3], gathered[2]), axis=0)

    def pick(parts, hf):
        return [lax.dynamic_index_in_dim(p, hf, 0, keepdims=False) for p in parts]

    def up_out_parts(dw_up, dw_out):
        up = (dw_up.reshape(3, A_WIDTH, N_CHIPS, D_MODEL // N_CHIPS).transpose(2, 0, 1, 3)
              .reshape(N_CHIPS, 2, 3 * A_WIDTH // 2, D_MODEL // N_CHIPS).transpose(1, 0, 2, 3))
        return [up.astype(BF16), _half_major(dw_out).astype(BF16)]

    g = _forward_backward(x[0], mem[0], loss_target[0], proj, w_conv_full, attn_sink, g_mem,
                          (_gather_weights(own[1:]), late_weights), g_post,
                          lambda dw_up, dw_out: _pair_exchange(pick(up_out_parts(dw_up, dw_out), 1 - ci)),
                          lambda dw_kv: _pair_exchange(pick([_half_major(dw_kv).astype(BF16)], 1 - ci)))

    half_rows = D_MODEL // 2

    def dw_in_half(half_of, name, carry):
        dw, carried = _dw_in_t(g["dproj"], h_t, half_of=half_of, where=where, name=name, carry=carry)
        return dw.reshape(N_CHIPS, SHARD_W, half_rows), carried

    small_keep = pick([_half_major(g["w_kv"]).astype(BF16)] + up_out_parts(g["w_up"], g["w_out"]), ci)
    small_names = ["w_kv", "w_up", "w_out"]
    sums_small = [_pair_add(k, r, "pair_add_" + nm)
                  for k, r, nm in zip(small_keep, g["early_kv"] + g["early"], small_names)]
    dw_send, recv3_small = dw_in_half(lambda w: 1 - w[1], "dw_in_send", _chip_exchange(sums_small))
    dw_keep, (recv_in,) = dw_in_half(lambda w: w[1], "dw_in_keep", _pair_exchange([dw_send]))
    sum_in = _pair_add(dw_keep, recv_in, "pair_add_w_in")
    (grad_x, dg_pre), (recv3_in,) = _d_h(g["dproj"], w_near, far, where, x[0], g_pre, g["dy"],
                                         carry=_chip_exchange([sum_in]))
    pairs = [_chip_add(s, r, where, "chip_add_" + nm)
             for s, r, nm in zip([sum_in] + sums_small, [recv3_in] + recv3_small, ["w_in"] + small_names)]

    zeros512 = jnp.zeros((1, D_MODEL - A_WIDTH), F32)
    conv_rows = [jnp.concatenate([g["w_conv"][k:k + 1], zeros512], axis=1) for k in range(3)]
    sink_row = jnp.pad(g["sink"][:, 0].reshape(1, N_Q_HEADS), ((0, 0), (0, D_MODEL - N_Q_HEADS)))
    loss_row = jnp.pad(g["loss"], ((0, 0), (0, D_MODEL - LANES)))
    pack = jnp.concatenate([dg_pre, g["g_mem"], g["g_post"]] + conv_rows + [sink_row, loss_row], axis=0)
    red, full = _small_allreduce(pack, _pair_share(pairs))
    loss = red[7, 0]
    small_grads = dict(
        g_pre=red[0:1], g_mem=red[1:2], g_post=red[2:3], attn_sink=red[6:7, :N_Q_HEADS],
        w_conv=lax.dynamic_slice(red[3:6, :A_WIDTH], (0, chip * LANES), (3, LANES)))

    gw_up = full[2].reshape(3, A_WIDTH, D_MODEL // N_CHIPS)
    grads = dict(small_grads, w_mem_kv=full[1].reshape(D_MODEL // N_CHIPS, 2 * MEM_WIDTH),
                 w_up_a=gw_up[0], w_up_b=gw_up[1], w_up_m=gw_up[2],
                 w_out=full[3].reshape(D_MODEL // N_CHIPS, D_MODEL))

    weights = dict(g_pre=g_pre, w_in=w_in, w_conv=w_conv, attn_sink=attn_sink, g_mem=g_mem, w_mem_kv=w_mem_kv,
                   w_up_a=w_up_a, w_up_b=w_up_b, w_up_m=w_up_m, w_out=w_out, g_post=g_post)
    m_in = dict(g_pre=m_g_pre, w_in=m_w_in, w_conv=m_w_conv, attn_sink=m_attn_sink, g_mem=m_g_mem,
                w_mem_kv=m_w_mem_kv, w_up_a=m_w_up_a, w_up_b=m_w_up_b, w_up_m=m_w_up_m, w_out=m_w_out,
                g_post=m_g_post)
    v_in = dict(g_pre=v_g_pre, w_in=v_w_in, w_conv=v_w_conv, attn_sink=v_attn_sink, g_mem=v_g_mem,
                w_mem_kv=v_w_mem_kv, w_up_a=v_w_up_a, w_up_b=v_w_up_b, w_up_m=v_w_up_m, w_out=v_w_out,
                g_post=v_g_post)
    out_g, out_d, out_m, out_v = [], [], [], []
    for nm in ("g_pre", "w_in", "w_conv", "attn_sink", "g_mem", "w_mem_kv", "w_up_a", "w_up_b", "w_up_m", "w_out",
               "g_post"):
        shape = weights[nm].shape
        if nm == "w_in":
            results = _adamw_halves(w_in[0].T, full[0], m_w_in[0].T, v_w_in[0].T, "adamw_w_in")
            for out, t in zip((out_g, out_d, out_m, out_v), results):
                out.append(t.T.reshape(shape))
            continue
        two_d = shape[-2:]
        gr = grads[nm].reshape(two_d)
        d, m_new, v_new = _adamw(weights[nm].reshape(two_d), gr, m_in[nm].reshape(two_d), v_in[nm].reshape(two_d),
                                 "adamw_" + nm)
        out_g.append(gr.reshape(shape))
        out_d.append(d.reshape(shape))
        out_m.append(m_new.reshape(shape))
        out_v.append(v_new.reshape(shape))
    return (loss, grad_x.reshape(x.shape), *out_g, *out_d, *out_m, *out_v)
```

```python
import jax
import jax.numpy as jnp
from jax import lax
from jax.experimental import pallas as pl
from jax.experimental.pallas import tpu as pltpu

F32 = jnp.float32
BF16 = jnp.bfloat16
MESH = pl.DeviceIdType.MESH

D_MODEL = 1024
EPS = 1e-6
A_WIDTH = 512
HEAD_DIM = 64
N_Q_HEADS = 8
WINDOW_BLOCK = 128
KV_PAD = 512
ROPE_THETA = 500000.0
ROT_DIM = 16
MEM_HEADS = 4
MEM_HEAD_DIM = 128
MEM_WIDTH = 512
IN_WIDTH = 7424
N_CHIPS = 4
LANES = 128
HALF_LANES = 64

PERM_SEGS = ((0, 2560), (2816, 3328), (4352, 7424), (3328, 4352), (2560, 2816))
COL_A, W_A = 0, 2048
COL_B, W_B = 2, 1024
COL_G, W_G = 1, 3072
COL_M, W_M = 6, 1024
COL_KV, W_KV = 28, 256

ADAM_LR = 0.001
ADAM_B1 = 0.9
ADAM_B2 = 0.999
ADAM_EPS = 1e-08
ADAM_WD = 0.01
ADAM_STEP = 10

VMEM_LIGHT_BYTES = 48 * 1024 * 1024
VMEM_HEAVY_BYTES = 48 * 1024 * 1024


_HBM = pl.BlockSpec(memory_space=pltpu.HBM)


def _params(heavy=False):
    return pltpu.CompilerParams(vmem_limit_bytes=VMEM_HEAVY_BYTES if heavy else VMEM_LIGHT_BYTES)


def _sigmoid(v):
    return jax.nn.sigmoid(v)


_DIMS = {"nn": (((1,), (0,)), ((), ())), "nt": (((1,), (1,)), ((), ())), "tn": (((0,), (0,)), ((), ()))}


class _Carry:
    def __init__(self, ins, out_shapes, sems, start, finish, aliases=None):
        self.ins, self.out_shapes, self.sems = list(ins), list(out_shapes), list(sems)
        self.start, self.finish, self.aliases = start, finish, dict(aliases or {})


def _join(*carries):
    def split(seq, counts):
        pos, parts = 0, []
        for n in counts:
            parts.append(seq[pos:pos + n])
            pos += n
        return parts

    n_in = [len(c.ins) for c in carries]
    n_out = [len(c.out_shapes) for c in carries]
    n_sem = [len(c.sems) for c in carries]

    def run(which):
        def go(ins, outs, sems):
            for c, i, o, sm in zip(carries, split(ins, n_in), split(outs, n_out), split(sems, n_sem)):
                getattr(c, which)(i, o, sm)
        return go

    aliases = {}
    for k, c in enumerate(carries):
        aliases.update({sum(n_in[:k]) + i: sum(n_out[:k]) + o for i, o in c.aliases.items()})
    return _Carry([a for c in carries for a in c.ins], [sh for c in carries for sh in c.out_shapes],
                  [sm for c in carries for sm in c.sems], run("start"), run("finish"), aliases)


def _carried_call(body, carry, *, grid, in_specs, out_specs, out_shape, scratch, operands, name, prefetch=None,
                  aliases=None, heavy=False):
    n_in, n_out, n_scr = len(in_specs), len(out_specs), len(scratch)
    c_in = len(carry.ins) if carry else 0
    c_out = len(carry.out_shapes) if carry else 0
    n_pre = 0 if prefetch is None else 1
    steps = 1
    for g in grid:
        steps *= g

    def wrapped(*refs):
        refs = refs[n_pre:]
        ins, cins = refs[:n_in], refs[n_in:n_in + c_in]
        outs = refs[n_in + c_in:n_in + c_in + n_out]
        couts = refs[n_in + c_in + n_out:n_in + c_in + n_out + c_out]
        rest = refs[n_in + c_in + n_out + c_out:]
        scr, sems = rest[:n_scr], rest[n_scr:]
        if carry:
            step = pl.program_id(0)
            for ax in range(1, len(grid)):
                step = step * grid[ax] + pl.program_id(ax)

            @pl.when(step == 0)
            def _():
                carry.start(cins, couts, sems)

        body(ins, outs, scr)
        if carry:
            @pl.when(step == steps - 1)
            def _():
                carry.finish(cins, couts, sems)

    all_aliases = {n_pre + i: o for i, o in (aliases or {}).items()}
    if carry:
        all_aliases.update({n_pre + n_in + i: n_out + o for i, o in carry.aliases.items()})
    all_in = list(in_specs) + [_HBM] * c_in
    all_out = list(out_specs) + [_HBM] * c_out
    all_scratch = list(scratch) + (carry.sems if carry else [])
    if n_pre:
        spec = dict(grid_spec=pltpu.PrefetchScalarGridSpec(num_scalar_prefetch=1, grid=grid, in_specs=all_in,
                                                           out_specs=all_out, scratch_shapes=all_scratch))
        pre = (prefetch,)
    else:
        spec = dict(grid=grid, in_specs=all_in, out_specs=all_out, scratch_shapes=all_scratch)
        pre = ()
    results = pl.pallas_call(
        wrapped, out_shape=list(out_shape) + (carry.out_shapes if carry else []), input_output_aliases=all_aliases,
        name=name, compiler_params=_params(heavy), **spec)(*pre, *operands, *(carry.ins if carry else []))
    return list(results[:n_out]), list(results[n_out:])


def _matmul(a, b, *, mode, out_dtype, tm, tn, tk, name):
    if mode == "nn":
        (m, k), (_, n) = a.shape, b.shape
    elif mode == "nt":
        (m, k), (n, _) = a.shape, b.shape
    else:
        (k, m), (_, n) = a.shape, b.shape
    tm, tn, tk = min(tm, m), min(tn, n), min(tk, k)
    assert m % tm == 0 and n % tn == 0 and k % tk == 0
    nk = k // tk
    dims = _DIMS[mode]

    if mode == "nn":
        a_spec = pl.BlockSpec((tm, tk), lambda i, j, kk: (i, kk))
        b_spec = pl.BlockSpec((tk, tn), lambda i, j, kk: (kk, j))
    elif mode == "nt":
        a_spec = pl.BlockSpec((tm, tk), lambda i, j, kk: (i, kk))
        b_spec = pl.BlockSpec((tn, tk), lambda i, j, kk: (j, kk))
    else:
        a_spec = pl.BlockSpec((tk, tm), lambda i, j, kk: (kk, i))
        b_spec = pl.BlockSpec((tk, tn), lambda i, j, kk: (kk, j))
    o_spec = pl.BlockSpec((tm, tn), lambda i, j, kk: (i, j))

    def part(a_ref, b_ref):
        return lax.dot_general(a_ref[...].astype(BF16), b_ref[...].astype(BF16), dims,
                               preferred_element_type=F32)

    if nk == 1:
        def body(a_ref, b_ref, o_ref):
            o_ref[...] = part(a_ref, b_ref).astype(out_dtype)
        scratch = []
    else:
        def body(a_ref, b_ref, o_ref, acc_ref):
            kk = pl.program_id(2)

            @pl.when(kk == 0)
            def _():
                acc_ref[...] = part(a_ref, b_ref)

            @pl.when(kk > 0)
            def _():
                acc_ref[...] += part(a_ref, b_ref)

            @pl.when(kk == nk - 1)
            def _():
                o_ref[...] = acc_ref[...].astype(out_dtype)
        scratch = [pltpu.VMEM((tm, tn), F32)]

    return pl.pallas_call(
        body, grid=(m // tm, n // tn, nk), in_specs=[a_spec, b_spec], out_specs=o_spec,
        out_shape=jax.ShapeDtypeStruct((m, n), out_dtype), scratch_shapes=scratch,
        name=name, compiler_params=_params())(a, b)


IN_BLOCK = 256
N_IN_BLOCKS = IN_WIDTH // IN_BLOCK
SHARD_BLOCKS = (IN_WIDTH // N_CHIPS) // IN_BLOCK
BLOCK_RUNS = tuple((a // IN_BLOCK, sum(d - c for c, d in PERM_SEGS[:k]) // IN_BLOCK, (b - a) // IN_BLOCK)
                   for k, (a, b) in enumerate(PERM_SEGS))


def _perm_block(r):
    p = r
    for ref0, perm0, n in BLOCK_RUNS:
        p = jnp.where((r >= ref0) & (r < ref0 + n), r - ref0 + perm0, p)
    return p


def _proj_near(x, g_pre, own_w, small, where):
    s, d = x.shape
    norm_tile = min(512, s)
    n_own = SHARD_BLOCKS - 1
    n_diag = SHARD_BLOCKS + 1
    n_blocks = N_IN_BLOCKS - n_diag
    piece = IN_WIDTH // N_CHIPS - SHARD_BLOCKS * IN_BLOCK
    near = _gather_weights([own_w], small, relations=(0, 1))
    far = _gather_weights([own_w], relations=(2,))
    both = _join(near, far)
    n_cin, n_cout = len(both.ins), len(both.out_shapes)

    def block_of(i, w):
        me, dg = w[0], w[2]
        own0 = SHARD_BLOCKS * me + jnp.minimum(me, 1)
        dg0 = SHARD_BLOCKS * dg
        lo0, hi0 = jnp.minimum(own0, dg0), jnp.maximum(own0, dg0)
        lo_n = jnp.where(own0 < dg0, n_own, n_diag)
        hi_n = jnp.where(own0 < dg0, n_diag, n_own)
        r = i - n_own
        r = r + lo_n * (r >= lo0).astype(jnp.int32)
        r = r + hi_n * (r >= hi0).astype(jnp.int32)
        return jnp.where(i < n_own, own0 + i, r)

    def body(w_ref, x_hbm, g_ref, own_hbm, *refs):
        cins, (o_ref, h_hbm, ht_hbm) = refs[:n_cin], refs[n_cin:n_cin + 3]
        couts = refs[n_cin + 3:n_cin + 3 + n_cout]
        blocks, block_sems, h_ref, x_tile, ht_tile, io_sem = refs[n_cin + 3 + n_cout:n_cin + 9 + n_cout]
        sems = refs[n_cin + 9 + n_cout:]
        near_refs = (cins[:len(near.ins)], couts[:len(near.out_shapes)], sems[:len(near.sems)])
        far_refs = (cins[len(near.ins):], couts[len(near.out_shapes):], sems[len(near.sems):])
        gathered = couts[0]
        i = pl.program_id(0)
        me = w_ref[0]

        def fetch(step, slot):
            r = block_of(step, w_ref)
            for p in range(IN_BLOCK // piece):
                row = r * IN_BLOCK + p * piece
                j = row // (IN_WIDTH // N_CHIPS)
                off = pl.multiple_of(row - j * (IN_WIDTH // N_CHIPS), BF16_SUBLANES)
                dst = blocks.at[slot, pl.ds(p * piece, piece)]

                @pl.when(j == me)
                def _():
                    pltpu.make_async_copy(own_hbm.at[pl.ds(off, piece)], dst, block_sems.at[slot]).start()

                @pl.when(j != me)
                def _():
                    pltpu.make_async_copy(gathered.at[j, pl.ds(off, piece)], dst, block_sems.at[slot]).start()

        def arrived(slot):
            pltpu.make_async_copy(own_hbm.at[pl.ds(0, IN_BLOCK)], blocks.at[slot], block_sems.at[slot]).wait()

        slot = i % 2

        def norm_rows(k):
            rows = pl.ds(k * norm_tile, norm_tile)
            pltpu.sync_copy(x_hbm.at[rows], x_tile)
            xv = x_tile[...]
            hv = (xv * lax.rsqrt(jnp.mean(xv * xv, axis=-1, keepdims=True) + EPS)) * g_ref[...]
            h_ref[rows, :] = hv.astype(BF16)
            ht_tile[...] = hv.T.astype(BF16)
            to_h = pltpu.make_async_copy(h_ref.at[rows], h_hbm.at[rows], io_sem.at[0])
            to_ht = pltpu.make_async_copy(ht_tile, ht_hbm.at[:, rows], io_sem.at[1])
            to_h.start()
            to_ht.start()
            to_h.wait()
            to_ht.wait()

        @pl.when(i == 0)
        def _():
            near.start(*near_refs)
            fetch(i, slot)
            for k in range(s // norm_tile):
                norm_rows(k)

        @pl.when(i == n_own)
        def _():
            near.finish(*near_refs)
            far.start(*far_refs)
            fetch(i, slot)

        arrived(slot)

        @pl.when((i + 1 < n_blocks) & (i + 1 != n_own))
        def _():
            fetch(i + 1, 1 - slot)

        o_ref[...] = lax.dot_general(h_ref[...], blocks[slot], _DIMS["nt"], preferred_element_type=F32)

        @pl.when(i == n_blocks - 1)
        def _():
            far.finish(*far_refs)

    anysp = pl.BlockSpec(memory_space=pl.ANY)
    grid_spec = pltpu.PrefetchScalarGridSpec(
        num_scalar_prefetch=1, grid=(n_blocks,),
        in_specs=[anysp, pl.BlockSpec((1, d), lambda i, w: (0, 0)), anysp] + [_HBM] * n_cin,
        out_specs=[pl.BlockSpec((s, IN_BLOCK), lambda i, w: (0, _perm_block(block_of(i, w)))), anysp, anysp]
        + [_HBM] * n_cout,
        scratch_shapes=[pltpu.VMEM((2, IN_BLOCK, d), BF16), pltpu.SemaphoreType.DMA((2,)), pltpu.VMEM((s, d), BF16),
                        pltpu.VMEM((norm_tile, d), F32), pltpu.VMEM((d, norm_tile), BF16),
                        pltpu.SemaphoreType.DMA((2,))] + both.sems)
    return pl.pallas_call(
        body, grid_spec=grid_spec,
        out_shape=[jax.ShapeDtypeStruct((s, IN_WIDTH), F32), jax.ShapeDtypeStruct((s, d), BF16),
                   jax.ShapeDtypeStruct((d, s), BF16)] + both.out_shapes,
        name="proj_near", compiler_params=_params())(where, x, g_pre, own_w, *both.ins)


def _proj_far(h, w_near, far, where, *, into, carry=None):
    s, d = h.shape
    n_blocks = SHARD_BLOCKS + 1
    lead = IN_WIDTH // N_CHIPS - SHARD_BLOCKS * IN_BLOCK

    def body(ins, outs, scr):
        where_ref, h_ref, w_hbm, far_hbm, _ = ins
        win, sem = scr
        i = pl.program_id(0)

        @pl.when(i == 0)
        def _():
            dg = where_ref[2]
            rows = pl.ds(pl.multiple_of(dg * (SHARD_BLOCKS * IN_BLOCK), IN_BLOCK), n_blocks * IN_BLOCK)
            window = pltpu.make_async_copy(w_hbm.at[rows], win, sem)
            window.start()
            window.wait()
            shard = pltpu.make_async_copy(far_hbm, win.at[pl.ds(pl.multiple_of(dg * lead, BF16_SUBLANES), SHARD_W)], sem)
            shard.start()
            shard.wait()

        blk = win[pl.ds(pl.multiple_of(i * IN_BLOCK, IN_BLOCK), IN_BLOCK), :]
        outs[0][...] = lax.dot_general(h_ref[...], blk, _DIMS["nt"], preferred_element_type=F32)

    anysp = pl.BlockSpec(memory_space=pl.ANY)
    (proj,), carried = _carried_call(
        body, carry, grid=(n_blocks,),
        in_specs=[pl.BlockSpec(memory_space=pltpu.SMEM), pl.BlockSpec((s, d), lambda i, w: (0, 0)), anysp, anysp, anysp],
        out_specs=[pl.BlockSpec((s, IN_BLOCK), lambda i, w: (0, _perm_block(i + SHARD_BLOCKS * w[2])))],
        out_shape=[jax.ShapeDtypeStruct((s, IN_WIDTH), F32)],
        scratch=[pltpu.VMEM((n_blocks * IN_BLOCK, d), BF16), pltpu.SemaphoreType.DMA],
        operands=(where, h, w_near, far, into), name="proj_far", prefetch=where, aliases={4: 0})
    return (proj, carried) if carry else proj


def _dw_in_t(dproj, h_t, *, half_of, where, name, carry=None):
    d, s = h_t.shape
    c = d // 2

    def body(ins, outs, scr):
        outs[0][...] = lax.dot_general(ins[1][...], ins[0][...], _DIMS["nn"], preferred_element_type=F32).T.astype(BF16)

    (dw,), carried = _carried_call(
        body, carry, grid=(N_IN_BLOCKS,),
        in_specs=[pl.BlockSpec((s, IN_BLOCK), lambda r, w: (0, _perm_block(r))),
                  pl.BlockSpec((c, s), lambda r, w: (half_of(w), 0))],
        out_specs=[pl.BlockSpec((IN_BLOCK, c), lambda r, w: (r, 0))],
        out_shape=[jax.ShapeDtypeStruct((IN_WIDTH, c), BF16)], scratch=[], operands=(dproj, h_t), name=name,
        prefetch=where)
    return (dw, carried) if carry else dw


def _norm_bwd_tile(dhv, xv, gv, resv):
    r = lax.rsqrt(jnp.mean(xv * xv, axis=-1, keepdims=True) + EPS)
    xh = xv * r
    dxh = dhv * gv
    dx = resv + r * (dxh - xh * jnp.mean(dxh * xh, axis=-1, keepdims=True))
    return dx, jnp.sum(dhv * xh, axis=0, keepdims=True)


def _d_h(dproj, w_near, far, where, x, g, res, *, carry=None):
    s = dproj.shape[0]
    d = w_near.shape[1]
    tm = min(s, 256)
    n = s // tm
    assert n % 2 == 0

    def body(ins, outs, scr):
        where_ref, a_ref, w_hbm, far_hbm, x_ref, g_ref, res_ref = ins
        dx_ref, dg_ref = outs
        w_ref, sem, dh_even, dh_odd = scr
        i = pl.program_id(0)

        def norm_bwd(dh_ref):
            dx, part = _norm_bwd_tile(dh_ref[...], x_ref[...], g_ref[...], res_ref[...])
            dx_ref[...] = dx
            dg_ref[...] += part

        def matmul(dh_ref):
            acc = None
            for ref0, perm0, nb in BLOCK_RUNS:
                term = jnp.dot(a_ref[:, perm0 * IN_BLOCK:(perm0 + nb) * IN_BLOCK],
                               w_ref[ref0 * IN_BLOCK:(ref0 + nb) * IN_BLOCK, :], preferred_element_type=F32)
                acc = term if acc is None else acc + term
            dh_ref[...] = acc

        @pl.when(i == 0)
        def _():
            whole = pltpu.make_async_copy(w_hbm, w_ref, sem)
            whole.start()
            whole.wait()
            rows = pl.ds(pl.multiple_of(where_ref[2] * SHARD_W, BF16_SUBLANES), SHARD_W)
            part = pltpu.make_async_copy(far_hbm, w_ref.at[rows], sem)
            part.start()
            part.wait()
            dg_ref[...] = jnp.zeros_like(dg_ref)
            matmul(dh_even)

        @pl.when((i % 2 == 0) & (i > 0) & (i < n))
        def _():
            norm_bwd(dh_odd)
            matmul(dh_even)

        @pl.when(i % 2 == 1)
        def _():
            norm_bwd(dh_even)
            matmul(dh_odd)

        @pl.when(i == n)
        def _():
            norm_bwd(dh_odd)

    anysp = pl.BlockSpec(memory_space=pl.ANY)
    before = pl.BlockSpec((tm, d), lambda i: (jnp.maximum(i - 1, 0), 0))
    vec = pl.BlockSpec((1, d), lambda i: (0, 0))
    outs, carried = _carried_call(
        body, carry, grid=(n + 1,),
        in_specs=[pl.BlockSpec(memory_space=pltpu.SMEM),
                  pl.BlockSpec((tm, IN_WIDTH), lambda i: (jnp.minimum(i, n - 1), 0)), anysp, anysp, before, vec, before],
        out_specs=[before, vec],
        out_shape=[jax.ShapeDtypeStruct((s, d), F32), jax.ShapeDtypeStruct((1, d), F32)],
        scratch=[pltpu.VMEM((IN_WIDTH, d), BF16), pltpu.SemaphoreType.DMA, pltpu.VMEM((tm, d), F32),
                 pltpu.VMEM((tm, d), F32)],
        operands=(where, dproj, w_near, far, x, g, res), name="d_h", heavy=True)
    return (outs, carried) if carry else outs


def _rmsnorm_fwd(x, g, *, name):
    s, d = x.shape
    ts = min(512, s)

    def body(x_ref, g_ref, o_ref):
        xv = x_ref[...]
        r = lax.rsqrt(jnp.mean(xv * xv, axis=-1, keepdims=True) + EPS)
        o_ref[...] = ((xv * r) * g_ref[...]).astype(BF16)

    return pl.pallas_call(
        body, grid=(s // ts,),
        in_specs=[pl.BlockSpec((ts, d), lambda i: (i, 0)), pl.BlockSpec((1, d), lambda i: (0, 0))],
        out_specs=pl.BlockSpec((ts, d), lambda i: (i, 0)),
        out_shape=jax.ShapeDtypeStruct((s, d), BF16), name=name, compiler_params=_params())(x, g)


def _rmsnorm_bwd(dh, x, g, res, *, name, carry=None):
    s, d = x.shape
    ts = min(256, s)

    def body(ins, outs, scr):
        dh_ref, x_ref, g_ref, res_ref = ins
        dx_ref, dg_ref = outs
        dx, part = _norm_bwd_tile(dh_ref[...], x_ref[...], g_ref[...], res_ref[...])

        @pl.when(pl.program_id(0) == 0)
        def _():
            dg_ref[...] = part

        @pl.when(pl.program_id(0) > 0)
        def _():
            dg_ref[...] += part

        dx_ref[...] = dx

    row = pl.BlockSpec((ts, d), lambda i: (i, 0))
    vec = pl.BlockSpec((1, d), lambda i: (0, 0))
    outs, carried = _carried_call(
        body, carry, grid=(s // ts,), in_specs=[row, row, vec, row], out_specs=[row, vec],
        out_shape=[jax.ShapeDtypeStruct((s, d), F32), jax.ShapeDtypeStruct((1, d), F32)],
        scratch=[], operands=(dh, x, g, res), name=name)
    return (*outs, carried) if carry else tuple(outs)


MID_TILE = 256


def _gated_branches(y_refs, wup_ref, gl):
    d = D_MODEL
    us = [jnp.dot(y_refs[k][...], wup_ref[k], preferred_element_type=F32) for k in range(3)]
    sg = [_sigmoid(gl[:, k * d:(k + 1) * d]) for k in range(3)]
    return us, sg


def _mid_fwd(ya, yb, ym, proj, x, tgt, w_up, w_out, g_post):
    s, d = x.shape
    ts = MID_TILE
    n = s // ts
    assert n % 2 == 0

    def body(ya_ref, yb_ref, ym_ref, g_ref, x_ref, t_ref, wup_ref, wout_ref, gp_ref,
             m_ref, do_ref, dy_ref, dg_ref, loss_ref, m_even, m_odd):
        i = pl.program_id(0)

        def merge(keep_ref):
            us, sg = _gated_branches((ya_ref, yb_ref, ym_ref), wup_ref, g_ref[...])
            merged = (sg[0] * us[0] + sg[1] * us[1] + sg[2] * us[2]).astype(BF16)
            m_ref[...] = merged
            keep_ref[...] = merged

        def head(kept_ref):
            ov = jnp.dot(kept_ref[...], wout_ref[...], preferred_element_type=F32)
            r = lax.rsqrt(jnp.mean(ov * ov, axis=-1, keepdims=True) + EPS)
            nh = ov * r
            gv = gp_ref[...]
            e = (x_ref[...] + nh * gv) - t_ref[...]
            lpart = 0.5 * jnp.sum(jnp.mean(e * e, axis=-1, keepdims=True), axis=0, keepdims=True)
            dy = e * (1.0 / d)
            dg_ref[...] += jnp.sum(dy * nh, axis=0, keepdims=True)
            loss_ref[...] += jnp.broadcast_to(lpart, loss_ref.shape)
            dn = dy * gv
            dy_ref[...] = dy
            do_ref[...] = (r * (dn - nh * jnp.mean(dn * nh, axis=-1, keepdims=True))).astype(BF16)

        @pl.when(i == 0)
        def _():
            dg_ref[...] = jnp.zeros_like(dg_ref)
            loss_ref[...] = jnp.zeros_like(loss_ref)
            merge(m_even)

        @pl.when((i % 2 == 0) & (i > 0) & (i < n))
        def _():
            head(m_odd)
            merge(m_even)

        @pl.when(i % 2 == 1)
        def _():
            head(m_even)
            merge(m_odd)

        @pl.when(i == n)
        def _():
            head(m_odd)

    def now(i):
        return jnp.minimum(i, n - 1)

    def before(i):
        return jnp.maximum(i - 1, 0)

    ysp = pl.BlockSpec((ts, A_WIDTH), lambda i: (now(i), 0))
    held = pl.BlockSpec((ts, d), lambda i: (before(i), 0))
    vec = pl.BlockSpec((1, d), lambda i: (0, 0))
    return pl.pallas_call(
        body, grid=(n + 1,),
        in_specs=[ysp, ysp, ysp, pl.BlockSpec((ts, W_G), lambda i: (now(i), COL_G)), held, held,
                  pl.BlockSpec((3, A_WIDTH, d), lambda i: (0, 0, 0)), pl.BlockSpec((d, d), lambda i: (0, 0)), vec],
        out_specs=[pl.BlockSpec((ts, d), lambda i: (now(i), 0)), held, held, vec,
                   pl.BlockSpec((1, LANES), lambda i: (0, 0))],
        out_shape=[jax.ShapeDtypeStruct((s, d), BF16), jax.ShapeDtypeStruct((s, d), BF16),
                   jax.ShapeDtypeStruct((s, d), F32), jax.ShapeDtypeStruct((1, d), F32),
                   jax.ShapeDtypeStruct((1, LANES), F32)],
        scratch_shapes=[pltpu.VMEM((ts, d), BF16), pltpu.VMEM((ts, d), BF16)],
        name="mid_fwd", compiler_params=_params(heavy=True))(ya, yb, ym, proj, x, tgt, w_up, w_out, g_post)


def _mid_bwd(d_out, merged, ya, yb, ym, proj, w_up, w_out):
    s, d = merged.shape
    ts = MID_TILE
    last = s // ts - 1

    def body(do_ref, m_ref, ya_ref, yb_ref, ym_ref, g_ref, wup_ref, wout_ref,
             dp_ref, dya_ref, dyb_ref, dym_ref, dwup_hbm, dwout_hbm, dwup_acc, dwout_acc):
        i = pl.program_id(0)

        @pl.when(i == 0)
        def _():
            dwup_acc[...] = jnp.zeros_like(dwup_acc)
            dwout_acc[...] = jnp.zeros_like(dwout_acc)

        y_refs = (ya_ref, yb_ref, ym_ref)
        us, sg = _gated_branches(y_refs, wup_ref, g_ref[...])
        dov = do_ref[...]
        dwout_acc[...] += lax.dot_general(m_ref[...], dov, _DIMS["tn"], preferred_element_type=F32)
        dm = lax.dot_general(dov, wout_ref[...], _DIMS["nt"], preferred_element_type=F32)
        for k, dy_ref in enumerate((dya_ref, dyb_ref, dym_ref)):
            dp_ref[:, k * d:(k + 1) * d] = ((dm * us[k]) * (sg[k] * (1.0 - sg[k]))).astype(BF16)
            du = (sg[k] * dm).astype(BF16)
            dy_ref[...] = lax.dot_general(du, wup_ref[k], _DIMS["nt"], preferred_element_type=F32)
            dwup_acc[k] += lax.dot_general(y_refs[k][...], du, _DIMS["tn"], preferred_element_type=F32)

        @pl.when(i == last)
        def _():
            pltpu.sync_copy(dwup_acc, dwup_hbm)
            pltpu.sync_copy(dwout_acc, dwout_hbm)

    row = pl.BlockSpec((ts, d), lambda i: (i, 0))
    ysp = pl.BlockSpec((ts, A_WIDTH), lambda i: (i, 0))
    gsp = pl.BlockSpec((ts, W_G), lambda i: (i, COL_G))
    anysp = pl.BlockSpec(memory_space=pl.ANY)
    yshape = jax.ShapeDtypeStruct((s, A_WIDTH), F32)
    return pl.pallas_call(
        body, grid=(s // ts,),
        in_specs=[row, row, ysp, ysp, ysp, gsp, pl.BlockSpec((3, A_WIDTH, d), lambda i: (0, 0, 0)),
                  pl.BlockSpec((d, d), lambda i: (0, 0))],
        out_specs=[gsp, ysp, ysp, ysp, anysp, anysp],
        out_shape=[jax.ShapeDtypeStruct((s, IN_WIDTH), BF16), yshape, yshape, yshape,
                   jax.ShapeDtypeStruct((3, A_WIDTH, d), F32), jax.ShapeDtypeStruct((d, d), F32)],
        scratch_shapes=[pltpu.VMEM((3, A_WIDTH, d), F32), pltpu.VMEM((d, d), F32)],
        name="mid_bwd", compiler_params=_params(heavy=True))(d_out, merged, ya, yb, ym, proj, w_up, w_out)


def _conv_core(blk, prev, nxt, w, i, last, ts):
    c = A_WIDTH
    ab, ac, ax, az = blk[:, :c], blk[:, c:2 * c], blk[:, 2 * c:3 * c], blk[:, 3 * c:]
    cu = ac * ax
    cu_prev = (prev[7:8, c:2 * c] * prev[7:8, 2 * c:3 * c]) * jnp.where(i > 0, 1.0, 0.0)
    cu_next = (nxt[0:1, c:2 * c] * nxt[0:1, 2 * c:3 * c]) * jnp.where(i < last, 1.0, 0.0)
    row = lax.broadcasted_iota(jnp.int32, (ts, c), 0)
    cm1 = jnp.where(row == 0, cu_prev, pltpu.roll(cu, 1, 0))
    cp1 = jnp.where(row == ts - 1, cu_next, pltpu.roll(cu, ts - 1, 0))
    yc = cm1 * w[0:1] + cu * w[1:2] + cp1 * w[2:3]
    return ab, ac, ax, az, cu, cm1, cp1, yc, row


def _halo_specs(ts, width, col, nblk8):
    prev = pl.BlockSpec((8, width), lambda i: (jnp.maximum(i * (ts // 8) - 1, 0), col))
    nxt = pl.BlockSpec((8, width), lambda i: (jnp.minimum((i + 1) * (ts // 8), nblk8 - 1), col))
    return prev, nxt


def _conv_fwd(proj, w_conv):
    s = proj.shape[0]
    ts = 256
    last = s // ts - 1

    def body(a_ref, ap_ref, an_ref, w_ref, ya_ref):
        i = pl.program_id(0)
        ab, _, _, az, _, _, _, yc, _ = _conv_core(a_ref[...], ap_ref[...], an_ref[...], w_ref[...], i, last, ts)
        ya_ref[...] = ((ab * yc) * (az * _sigmoid(az))).astype(BF16)

    prev, nxt = _halo_specs(ts, W_A, COL_A, s // 8)
    return pl.pallas_call(
        body, grid=(s // ts,),
        in_specs=[pl.BlockSpec((ts, W_A), lambda i: (i, COL_A)), prev, nxt,
                  pl.BlockSpec((3, A_WIDTH), lambda i: (0, 0))],
        out_specs=pl.BlockSpec((ts, A_WIDTH), lambda i: (i, 0)),
        out_shape=jax.ShapeDtypeStruct((s, A_WIDTH), BF16), name="conv_fwd",
        compiler_params=_params())(proj, proj, proj, w_conv)


def _conv_bwd(proj, w_conv, dya, dproj):
    s = proj.shape[0]
    ts = 256
    last = s // ts - 1
    c = A_WIDTH

    def body(a_ref, ap_ref, an_ref, w_ref, d_ref, dp_ref, dn_ref, _, dproj_ref, dw_ref):
        i = pl.program_id(0)
        w = w_ref[...]
        prev, nxt = ap_ref[...], an_ref[...]
        ab, ac, ax, az, cu, cm1, cp1, yc, row = _conv_core(a_ref[...], prev, nxt, w, i, last, ts)
        sg = _sigmoid(az)
        sz = az * sg
        dya_v = d_ref[...]
        dyc = dya_v * sz * ab
        dproj_ref[:, :c] = (dya_v * sz * yc).astype(BF16)
        dproj_ref[:, 3 * c:] = (dya_v * (ab * yc) * (sg * (1.0 + az * (1.0 - sg)))).astype(BF16)

        def halo_dyc(a_row, d_row):
            azr = a_row[:, 3 * c:]
            return d_row * (azr * _sigmoid(azr)) * a_row[:, :c]

        dyc_prev = halo_dyc(prev[7:8], dp_ref[...][7:8]) * jnp.where(i > 0, 1.0, 0.0)
        dyc_next = halo_dyc(nxt[0:1], dn_ref[...][0:1]) * jnp.where(i < last, 1.0, 0.0)
        dyc_m1 = jnp.where(row == 0, dyc_prev, pltpu.roll(dyc, 1, 0))
        dyc_p1 = jnp.where(row == ts - 1, dyc_next, pltpu.roll(dyc, ts - 1, 0))
        dcu = dyc_p1 * w[0:1] + dyc * w[1:2] + dyc_m1 * w[2:3]
        dproj_ref[:, c:2 * c] = (dcu * ax).astype(BF16)
        dproj_ref[:, 2 * c:3 * c] = (dcu * ac).astype(BF16)
        dw = [jnp.sum(dyc * t, axis=0, keepdims=True) for t in (cm1, cu, cp1)]

        @pl.when(i == 0)
        def _():
            for k in range(3):
                dw_ref[k:k + 1, :] = dw[k]

        @pl.when(i > 0)
        def _():
            for k in range(3):
                dw_ref[k:k + 1, :] += dw[k]

    prev, nxt = _halo_specs(ts, W_A, COL_A, s // 8)
    dprev, dnxt = _halo_specs(ts, A_WIDTH, 0, s // 8)
    return pl.pallas_call(
        body, grid=(s // ts,),
        in_specs=[pl.BlockSpec((ts, W_A), lambda i: (i, COL_A)), prev, nxt,
                  pl.BlockSpec((3, A_WIDTH), lambda i: (0, 0)),
                  pl.BlockSpec((ts, A_WIDTH), lambda i: (i, 0)), dprev, dnxt,
                  pl.BlockSpec(memory_space=pl.ANY)],
        out_specs=[pl.BlockSpec((ts, W_A), lambda i: (i, COL_A)), pl.BlockSpec((3, A_WIDTH), lambda i: (0, 0))],
        out_shape=[jax.ShapeDtypeStruct(dproj.shape, BF16), jax.ShapeDtypeStruct((3, A_WIDTH), F32)],
        input_output_aliases={7: 0}, name="conv_bwd",
        compiler_params=_params())(proj, proj, proj, w_conv, dya, dya, dya, dproj)


def _rope_tables(s):
    half = ROT_DIM // 2
    dim = jnp.arange(LANES) % HEAD_DIM
    inv_freq = jnp.power(jnp.float32(ROPE_THETA), -(dim % half).astype(F32) * (2.0 / ROT_DIM))
    coarse = (jnp.arange(s // LANES) * LANES).astype(F32)[:, None] * inv_freq[None, :]
    fine = jnp.arange(LANES).astype(F32)[:, None] * inv_freq[None, :]
    cos_a, sin_a = jnp.cos(coarse)[:, None, :], jnp.sin(coarse)[:, None, :]
    cos_b, sin_b = jnp.cos(fine)[None], jnp.sin(fine)[None]
    cos = (cos_a * cos_b - sin_a * sin_b).reshape(s, LANES)
    sin = (sin_a * cos_b + cos_a * sin_b).reshape(s, LANES)
    first, second = (dim < half)[None, :], ((dim >= half) & (dim < ROT_DIM))[None, :]
    c = jnp.where(first | second, cos, 1.0)
    s1 = jnp.where(first, -sin, 0.0)
    s2 = jnp.where(second, sin, 0.0)
    return jnp.concatenate([c, s1, s2], axis=1)


def _rope(t, tab):
    return (t * tab[:, :LANES] + pltpu.roll(t, LANES - 8, 1) * tab[:, LANES:2 * LANES]
            + pltpu.roll(t, 8, 1) * tab[:, 2 * LANES:])


def _rope_transpose(dt, tab):
    return (dt * tab[:, :LANES] + pltpu.roll(dt * tab[:, LANES:2 * LANES], 8, 1)
            + pltpu.roll(dt * tab[:, 2 * LANES:], LANES - 8, 1))


def _rope_kv(proj, tab):
    s = proj.shape[0]
    nb = s // KV_PAD

    def body(kv_ref, t_ref, k_ref, v_ref):
        j = pl.program_id(0)
        inside = jnp.where((j > 0) & (j <= nb), 1.0, 0.0)
        kv = kv_ref[...]
        k_ref[...] = (_rope(kv[:, :LANES], t_ref[...]) * inside).astype(BF16)
        v_ref[...] = (kv[:, LANES:] * inside).astype(BF16)

    def src(j):
        return jnp.clip(j - 1, 0, nb - 1)

    o_spec = pl.BlockSpec((KV_PAD, LANES), lambda j: (j, 0))
    shp = jax.ShapeDtypeStruct((s + 2 * KV_PAD, LANES), BF16)
    return pl.pallas_call(
        body, grid=(nb + 2,),
        in_specs=[pl.BlockSpec((KV_PAD, W_KV), lambda j: (src(j), COL_KV)),
                  pl.BlockSpec((KV_PAD, 3 * LANES), lambda j: (src(j), 0))],
        out_specs=[o_spec, o_spec], out_shape=[shp, shp], name="rope_kv",
        compiler_params=_params())(proj, tab)


def _rope_kv_bwd(dkpad, dvpad, tab, dproj):
    s = tab.shape[0]
    nb = s // KV_PAD

    def body(dk_ref, dv_ref, t_ref, _, dp_ref):
        dp_ref[:, :LANES] = _rope_transpose(dk_ref[...], t_ref[...]).astype(BF16)
        dp_ref[:, LANES:] = dv_ref[...].astype(BF16)

    pad_spec = pl.BlockSpec((KV_PAD, LANES), lambda j: (j + 1, 0))
    return pl.pallas_call(
        body, grid=(nb,),
        in_specs=[pad_spec, pad_spec, pl.BlockSpec((KV_PAD, 3 * LANES), lambda j: (j, 0)),
                  pl.BlockSpec(memory_space=pl.ANY)],
        out_specs=pl.BlockSpec((KV_PAD, W_KV), lambda j: (j, COL_KV)),
        out_shape=jax.ShapeDtypeStruct(dproj.shape, BF16), input_output_aliases={3: 0},
        name="rope_kv_bwd", compiler_params=_params())(dkpad, dvpad, tab, dproj)


def _window_start(n):
    return pl.multiple_of((n - 1) * WINDOW_BLOCK + KV_PAD, WINDOW_BLOCK)


def _window_operands(k_ref, v_ref, n, lo):
    start = _window_start(n)
    kw = k_ref[pl.ds(start, 3 * WINDOW_BLOCK), :].astype(F32)
    vw = v_ref[pl.ds(start, 3 * WINDOW_BLOCK), :].astype(F32)
    kr, vr = pltpu.roll(kw, HALF_LANES, 1), pltpu.roll(vw, HALF_LANES, 1)
    k2 = (jnp.where(lo, kw, kr).astype(BF16), jnp.where(lo, kr, kw).astype(BF16))
    v2 = (jnp.where(lo, vw, vr).astype(BF16), jnp.where(lo, vr, vw).astype(BF16))
    return k2, v2


HEADS_PER_GROUP = 4
SWA_FWD_BLOCKS = 1
SWA_BWD_BLOCKS = 2


def _window_bias():
    wb = WINDOW_BLOCK
    qi = lax.broadcasted_iota(jnp.int32, (wb, 3 * wb), 0)
    kj = lax.broadcasted_iota(jnp.int32, (wb, 3 * wb), 1)
    band = (kj >= qi) & (kj <= qi + 2 * wb)
    cases = jnp.stack([band & (kj >= wb), band, band & (kj < 2 * wb)])
    return jnp.where(cases, 0.0, -jnp.inf).astype(F32)


def _block_bias(bias_ref, n, n_blocks):
    case = jnp.where(n == 0, 0, jnp.where(n == n_blocks - 1, 2, 1))
    one = bias_ref[case]
    return jnp.concatenate([one] * HEADS_PER_GROUP, axis=0)


def _stack_heads(pair0, pair1, lo):
    return jnp.concatenate([jnp.where(lo, pair0, 0.0), jnp.where(lo, 0.0, pair0),
                            jnp.where(lo, pair1, 0.0), jnp.where(lo, 0.0, pair1)], axis=0)


def _unstack_pair(stacked, i, lo):
    wb = WINDOW_BLOCK
    return jnp.where(lo, stacked[2 * i * wb:(2 * i + 1) * wb], stacked[(2 * i + 1) * wb:(2 * i + 2) * wb])


def _sink_column(sink_ref, g):
    wb = WINDOW_BLOCK
    return jnp.concatenate([jnp.full((wb, 1), sink_ref[0, HEADS_PER_GROUP * g + i], F32)
                            for i in range(HEADS_PER_GROUP)], axis=0)


def _head_exp(q4, k2g, bias, sink):
    sc = lax.dot_general(q4, k2g, _DIMS["nt"], preferred_element_type=F32) * (HEAD_DIM ** -0.5) + bias
    m = jnp.maximum(jnp.max(sc, axis=1, keepdims=True), sink)
    return jnp.exp(sc - m).astype(BF16), jnp.exp(sink - m)


def _swa_fwd(proj, kpad, vpad, tab, bias, sink, *, carry=None):
    s = proj.shape[0]
    wb = WINDOW_BLOCK

    def body(b_ref, k_ref, v_ref, t_ref, bias_ref, sink_ref, o_ref, y_ref):
        lo = lax.broadcasted_iota(jnp.int32, (wb, LANES), 1) < HALF_LANES
        lo_w = lax.broadcasted_iota(jnp.int32, (3 * wb, LANES), 1) < HALF_LANES
        for sub in range(SWA_FWD_BLOCKS):
            n = pl.program_id(0) * SWA_FWD_BLOCKS + sub
            rows = slice(sub * wb, (sub + 1) * wb)
            k2, v2 = _window_operands(k_ref, v_ref, n, lo_w)
            valid = _block_bias(bias_ref, n, s // wb)
            tab_v = t_ref[rows, :]
            ones = jnp.ones((3 * wb, LANES), BF16)
            for g in range(2):
                qr = [_rope(b_ref[rows, (2 * g + i) * LANES:(2 * g + i + 1) * LANES], tab_v) for i in range(2)]
                q4 = _stack_heads(qr[0], qr[1], lo).astype(BF16)
                e, es = _head_exp(q4, k2[g], valid, _sink_column(sink_ref, g))
                ox = jnp.dot(e, jnp.concatenate([v2[g], ones], axis=1), preferred_element_type=F32)
                o4 = ox[:, :LANES] * (1.0 / (ox[:, LANES:] + es))
                for i in range(2):
                    cols = slice((2 * g + i) * LANES, (2 * g + i + 1) * LANES)
                    op = _unstack_pair(o4, i, lo)
                    o_ref[rows, cols] = op
                    zp = b_ref[rows, A_WIDTH + cols.start:A_WIDTH + cols.stop]
                    y_ref[rows, cols] = (op * (zp * _sigmoid(zp))).astype(BF16)

    tq = SWA_FWD_BLOCKS * wb
    pad_spec = pl.BlockSpec((s + 2 * KV_PAD, LANES), lambda n: (0, 0))
    o_spec = pl.BlockSpec((tq, A_WIDTH), lambda n: (n, 0))
    outs, carried = _carried_call(
        lambda ins, outs, scr: body(*ins, *outs), carry, grid=(s // tq,),
        in_specs=[pl.BlockSpec((tq, W_B), lambda n: (n, COL_B)), pad_spec, pad_spec,
                  pl.BlockSpec((tq, 3 * LANES), lambda n: (n, 0)),
                  pl.BlockSpec(bias.shape, lambda n: (0, 0, 0)), pl.BlockSpec(memory_space=pltpu.SMEM)],
        out_specs=[o_spec, o_spec],
        out_shape=[jax.ShapeDtypeStruct((s, A_WIDTH), F32), jax.ShapeDtypeStruct((s, A_WIDTH), BF16)],
        scratch=[], operands=(proj, kpad, vpad, tab, bias, sink), name="swa_fwd")
    return (*outs, carried) if carry else tuple(outs)


def _swa_bwd(proj, kpad, vpad, tab, bias, sink, o_attn, dyb, dproj):
    s = proj.shape[0]
    wb = WINDOW_BLOCK
    scale = HEAD_DIM ** -0.5

    def body(b_ref, k_ref, v_ref, t_ref, bias_ref, sink_ref, o_ref, dy_ref, _, dp_ref, dk_ref, dv_ref, ds_ref):
        @pl.when(pl.program_id(0) == 0)
        def _():
            dk_ref[...] = jnp.zeros_like(dk_ref)
            dv_ref[...] = jnp.zeros_like(dv_ref)
            ds_ref[...] = jnp.zeros_like(ds_ref)

        lo = lax.broadcasted_iota(jnp.int32, (wb, LANES), 1) < HALF_LANES
        lo_w = lax.broadcasted_iota(jnp.int32, (3 * wb, LANES), 1) < HALF_LANES
        for sub in range(SWA_BWD_BLOCKS):
            n = pl.program_id(0) * SWA_BWD_BLOCKS + sub
            rows = slice(sub * wb, (sub + 1) * wb)
            k2, v2 = _window_operands(k_ref, v_ref, n, lo_w)
            valid = _block_bias(bias_ref, n, s // wb)
            tab_v = t_ref[rows, :]
            ones = jnp.ones((3 * wb, LANES), BF16)
            dks, dvs = [], []
            for g in range(2):
                qr, op, do = [], [], []
                for i in range(2):
                    cols = slice((2 * g + i) * LANES, (2 * g + i + 1) * LANES)
                    zcols = slice(A_WIDTH + cols.start, A_WIDTH + cols.stop)
                    qr.append(_rope(b_ref[rows, cols], tab_v))
                    zp = b_ref[rows, zcols]
                    sg = _sigmoid(zp)
                    op.append(o_ref[rows, cols])
                    dyp = dy_ref[rows, cols]
                    do.append(dyp * (zp * sg))
                    dp_ref[rows, zcols] = (dyp * op[i] * (sg * (1.0 + zp * (1.0 - sg)))).astype(BF16)
                q4 = _stack_heads(qr[0], qr[1], lo).astype(BF16)
                do4 = _stack_heads(do[0], do[1], lo)
                o4 = jnp.concatenate([op[0], op[0], op[1], op[1]], axis=0)
                e, es = _head_exp(q4, k2[g], valid, _sink_column(sink_ref, g))
                inv = 1.0 / (jnp.dot(e, ones, preferred_element_type=F32) + es)
                prob = e.astype(F32) * jnp.concatenate([inv, inv, inv], axis=1)
                delta = jnp.sum(do4 * o4, axis=1, keepdims=True)
                do4b = do4.astype(BF16)
                dprob = lax.dot_general(do4b, v2[g], _DIMS["nt"], preferred_element_type=F32)
                dsc = (prob * (dprob - delta)).astype(BF16)
                sink_terms = (es * inv[:, :1]) * delta
                for i in range(HEADS_PER_GROUP):
                    h = HEADS_PER_GROUP * g + i
                    dsink = -jnp.sum(sink_terms[i * wb:(i + 1) * wb], axis=0, keepdims=True)
                    ds_ref[h:h + 1, :] += jnp.broadcast_to(dsink, (1, LANES))
                dq4 = jnp.dot(dsc, k2[g], preferred_element_type=F32) * scale
                for i in range(2):
                    cols = slice((2 * g + i) * LANES, (2 * g + i + 1) * LANES)
                    dp_ref[rows, cols] = _rope_transpose(_unstack_pair(dq4, i, lo), tab_v).astype(BF16)
                dk2 = lax.dot_general(dsc, q4, _DIMS["tn"], preferred_element_type=F32) * scale
                dv2 = lax.dot_general(prob.astype(BF16), do4b, _DIMS["tn"], preferred_element_type=F32)
                dks.append(dk2 + pltpu.roll(dk2, HALF_LANES, 1))
                dvs.append(dv2 + pltpu.roll(dv2, HALF_LANES, 1))
            start = _window_start(n)
            dk_ref[pl.ds(start, 3 * wb), :] += jnp.where(lo_w, dks[0], dks[1])
            dv_ref[pl.ds(start, 3 * wb), :] += jnp.where(lo_w, dvs[0], dvs[1])

    tq = SWA_BWD_BLOCKS * wb
    pad_spec = pl.BlockSpec((s + 2 * KV_PAD, LANES), lambda n: (0, 0))
    blk = pl.BlockSpec((tq, A_WIDTH), lambda n: (n, 0))
    bsp = pl.BlockSpec((tq, W_B), lambda n: (n, COL_B))
    pad_shape = jax.ShapeDtypeStruct((s + 2 * KV_PAD, LANES), F32)
    return pl.pallas_call(
        body, grid=(s // tq,),
        in_specs=[bsp, pad_spec, pad_spec, pl.BlockSpec((tq, 3 * LANES), lambda n: (n, 0)),
                  pl.BlockSpec(bias.shape, lambda n: (0, 0, 0)), pl.BlockSpec(memory_space=pltpu.SMEM), blk, blk,
                  pl.BlockSpec(memory_space=pl.ANY)],
        out_specs=[bsp, pad_spec, pad_spec, pl.BlockSpec((8, LANES), lambda n: (0, 0))],
        out_shape=[jax.ShapeDtypeStruct(dproj.shape, BF16), pad_shape, pad_shape,
                   jax.ShapeDtypeStruct((8, LANES), F32)],
        input_output_aliases={8: 0}, name="swa_bwd",
        compiler_params=_params())(proj, kpad, vpad, tab, bias, sink, o_attn, dyb, dproj)


def _mem_exp(qh, mk):
    sc = lax.dot_general(qh, mk, _DIMS["nt"], preferred_element_type=F32) * (MEM_HEAD_DIM ** -0.5)
    return jnp.exp(sc - jnp.max(sc, axis=1, keepdims=True)).astype(BF16)


def _mem_fwd(proj, mkv):
    s = proj.shape[0]
    ts = 512
    mlen = mkv.shape[0]

    def body(m_ref, kv_ref, o_ref, y_ref):
        ones = jnp.ones((mlen, LANES), BF16)
        for h in range(MEM_HEADS):
            cols = slice(h * LANES, (h + 1) * LANES)
            mk = kv_ref[:, cols].astype(BF16)
            mv = kv_ref[:, MEM_WIDTH + h * LANES:MEM_WIDTH + (h + 1) * LANES].astype(BF16)
            e = _mem_exp(m_ref[:, cols].astype(BF16), mk)
            ox = jnp.dot(e, jnp.concatenate([mv, ones], axis=1), preferred_element_type=F32)
            oh = ox[:, :LANES] * (1.0 / ox[:, LANES:])
            o_ref[:, cols] = oh
            zh = m_ref[:, MEM_WIDTH + h * LANES:MEM_WIDTH + (h + 1) * LANES]
            y_ref[:, cols] = (oh * (zh * _sigmoid(zh))).astype(BF16)

    o_spec = pl.BlockSpec((ts, MEM_WIDTH), lambda i: (i, 0))
    return pl.pallas_call(
        body, grid=(s // ts,),
        in_specs=[pl.BlockSpec((ts, W_M), lambda i: (i, COL_M)),
                  pl.BlockSpec((mlen, 2 * MEM_WIDTH), lambda i: (0, 0))],
        out_specs=[o_spec, o_spec],
        out_shape=[jax.ShapeDtypeStruct((s, MEM_WIDTH), F32), jax.ShapeDtypeStruct((s, MEM_WIDTH), BF16)],
        name="mem_fwd", compiler_params=_params())(proj, mkv)


def _mem_bwd(proj, mkv, o_mem, dym, dproj, *, carry=None):
    s = proj.shape[0]
    ts = 512
    mlen = mkv.shape[0]
    scale = MEM_HEAD_DIM ** -0.5

    def body(m_ref, kv_ref, o_ref, dy_ref, _, dp_ref, dkv_ref):
        @pl.when(pl.program_id(0) == 0)
        def _():
            dkv_ref[...] = jnp.zeros_like(dkv_ref)

        ones = jnp.ones((mlen, LANES), BF16)
        for h in range(MEM_HEADS):
            cols = slice(h * LANES, (h + 1) * LANES)
            vcols = slice(MEM_WIDTH + h * LANES, MEM_WIDTH + (h + 1) * LANES)
            mk = kv_ref[:, cols].astype(BF16)
            mv = kv_ref[:, vcols].astype(BF16)
            qh = m_ref[:, cols].astype(BF16)
            zh = m_ref[:, vcols]
            sg = _sigmoid(zh)
            oh = o_ref[:, cols]
            dyh = dy_ref[:, cols]
            doh = dyh * (zh * sg)
            dp_ref[:, vcols] = (dyh * oh * (sg * (1.0 + zh * (1.0 - sg)))).astype(BF16)
            e = _mem_exp(qh, mk)
            inv = 1.0 / jnp.dot(e, ones, preferred_element_type=F32)
            prob = e.astype(F32) * jnp.concatenate([inv] * (mlen // LANES), axis=1)
            delta = jnp.sum(doh * oh, axis=1, keepdims=True)
            dohb = doh.astype(BF16)
            dprob = lax.dot_general(dohb, mv, _DIMS["nt"], preferred_element_type=F32)
            dsc = (prob * (dprob - delta)).astype(BF16)
            dp_ref[:, cols] = (jnp.dot(dsc, mk, preferred_element_type=F32) * scale).astype(BF16)
            dkv_ref[:, cols] += lax.dot_general(dsc, qh, _DIMS["tn"], preferred_element_type=F32) * scale
            dkv_ref[:, vcols] += lax.dot_general(prob.astype(BF16), dohb, _DIMS["tn"],
                                                 preferred_element_type=F32)

    blk = pl.BlockSpec((ts, MEM_WIDTH), lambda i: (i, 0))
    msp = pl.BlockSpec((ts, W_M), lambda i: (i, COL_M))
    kvsp = pl.BlockSpec((mlen, 2 * MEM_WIDTH), lambda i: (0, 0))
    outs, carried = _carried_call(
        lambda ins, outs, scr: body(*ins, *outs), carry, grid=(s // ts,),
        in_specs=[msp, kvsp, blk, blk, pl.BlockSpec(memory_space=pl.ANY)],
        out_specs=[msp, kvsp],
        out_shape=[jax.ShapeDtypeStruct(dproj.shape, BF16), jax.ShapeDtypeStruct(mkv.shape, F32)],
        scratch=[], operands=(proj, mkv, o_mem, dym, dproj), name="mem_bwd", aliases={4: 0})
    return (*outs, carried) if carry else tuple(outs)


def _forward_backward(x, mem, tgt, proj, w_conv, sink, g_mem, late_weights, g_post, early_exchange, kv_exchange):
    s = x.shape[0]
    tab = _rope_tables(s)
    bias = _window_bias()

    ya = _conv_fwd(proj, w_conv)
    kpad, vpad = _rope_kv(proj, tab)
    o_attn, yb, *arrived = _swa_fwd(proj, kpad, vpad, tab, bias, sink, carry=late_weights[0])
    w_kv, w_up, w_out = late_weights[1](arrived[0] if arrived else None)
    mn = _rmsnorm_fwd(mem, g_mem, name="mem_norm")
    mkv = _matmul(mn, w_kv, mode="nn", out_dtype=F32, tm=256, tn=1024, tk=D_MODEL, name="mem_kv")
    o_mem, ym = _mem_fwd(proj, mkv)
    merged, d_out, dy, dg_post, loss = _mid_fwd(ya, yb, ym, proj, x, tgt, w_up, w_out, g_post)
    dproj, d_ya, d_yb, d_ym, dw_up, dw_out = _mid_bwd(d_out, merged, ya, yb, ym, proj, w_up, w_out)

    dproj, dw_conv = _conv_bwd(proj, w_conv, d_ya, dproj)
    dproj, dkpad, dvpad, dsink = _swa_bwd(proj, kpad, vpad, tab, bias, sink, o_attn, d_yb, dproj)
    dproj = _rope_kv_bwd(dkpad, dvpad, tab, dproj)
    dproj, d_mkv, *early = _mem_bwd(proj, mkv, o_mem, d_ym, dproj, carry=early_exchange(dw_up, dw_out))

    dw_kv = _matmul(mn, d_mkv, mode="tn", out_dtype=F32, tm=1024, tn=1024, tk=256, name="dw_kv")
    d_mn = _matmul(d_mkv, w_kv, mode="nt", out_dtype=F32, tm=256, tn=1024, tk=D_MODEL, name="d_mn")
    _, dg_mem, *early_kv = _rmsnorm_bwd(d_mn, mem, g_mem, d_mn, name="mem_norm_bwd", carry=kv_exchange(dw_kv))

    return dict(loss=loss, dproj=dproj, dy=dy, w_conv=dw_conv, sink=dsink, g_mem=dg_mem,
                w_kv=dw_kv, w_up=dw_up, w_out=dw_out, g_post=dg_post, early=early[0] if early else None,
                early_kv=early_kv[0] if early_kv else None)


N_DEV = 8


def _position():
    return lax.axis_index("x"), lax.axis_index("y"), lax.axis_index("c")


def _other_chips(x, y):
    return (((1 - x, y), 2 * (1 - x) + y), ((x, 1 - y), 2 * x + (1 - y)), ((1 - x, 1 - y), 2 * (1 - x) + (1 - y)))


def _remote(src, dst, send_sems, recv_sems, k, device):
    return pltpu.make_async_remote_copy(src_ref=src, dst_ref=dst, send_sem=send_sems.at[k], recv_sem=recv_sems.at[k],
                                        device_id=device, device_id_type=MESH)


def _rows_half(ref, hf):
    rh = ref.shape[0] // 2
    return ref.at[pl.ds(pl.multiple_of(hf * rh, 8), rh)]


def _gather_weights(shards, small=None, relations=(0, 1, 2), into=None):
    n = len(shards)
    k = 0 if small is None else 1

    def peers(x, y):
        return [(r, chip, idx) for r, (chip, idx) in enumerate(_other_chips(x, y)) if r in relations]

    def ici(ins, outs, sems, a, r, chip, src_chip, c):
        return _remote(_rows_half(ins[a], c), _rows_half(outs[a].at[src_chip], c), sems[0], sems[1], 3 * a + r,
                       (*chip, c))

    def whole(ins, outs, sems, r, chip, src_chip, c):
        return _remote(ins[n], outs[n].at[src_chip], sems[0], sems[1], 3 * n + r, (*chip, c))

    def d2d(outs, sems, a, r, idx, hf, x, y, c):
        half = _rows_half(outs[a].at[idx], hf)
        return _remote(half, half, sems[2], sems[3], 3 * a + r, (x, y, 1 - c))

    def start(ins, outs, sems):
        x, y, c = _position()
        me = 2 * x + y
        for a in range(n):
            for r, chip, _ in peers(x, y):
                ici(ins, outs, sems, a, r, chip, me, c).start()
        for r, (chip, _) in enumerate(_other_chips(x, y)):
            if k:
                whole(ins, outs, sems, r, chip, me, c).start()

    def finish(ins, outs, sems):
        x, y, c = _position()
        me = 2 * x + y
        for a in range(n):
            for r, chip, idx in peers(x, y):
                ici(ins, outs, sems, a, r, chip, idx, c).wait_recv()
                d2d(outs, sems, a, r, idx, c, x, y, c).start()
        for a in range(n):
            for r, chip, idx in peers(x, y):
                d2d(outs, sems, a, r, idx, 1 - c, x, y, c).wait_recv()
        for r, (chip, idx) in enumerate(_other_chips(x, y)):
            if k:
                whole(ins, outs, sems, r, chip, idx, c).wait_recv()
                whole(ins, outs, sems, r, chip, me, c).wait_send()
        for a in range(n):
            for r, chip, idx in peers(x, y):
                ici(ins, outs, sems, a, r, chip, me, c).wait_send()
                d2d(outs, sems, a, r, idx, c, x, y, c).wait_send()

    operands = list(shards) + ([small] if k else [])
    shapes = [jax.ShapeDtypeStruct((N_CHIPS,) + s.shape, s.dtype) for s in operands]
    aliases = {}
    if into is not None:
        assert len(into) == len(operands)
        aliases = {len(operands) + a: a for a in range(len(into))}
        operands += list(into)
    return _Carry(operands, shapes,
                  [pltpu.SemaphoreType.DMA((3 * (n + k),)), pltpu.SemaphoreType.DMA((3 * (n + k),)),
                   pltpu.SemaphoreType.DMA((3 * n,)), pltpu.SemaphoreType.DMA((3 * n,))], start, finish, aliases)


def _pair_exchange(send):
    n = len(send)

    def copies(ins, outs, sems):
        x, y, c = _position()
        return [_remote(ins[a], outs[a], sems[0], sems[1], a, (x, y, 1 - c)) for a in range(n)]

    def start(ins, outs, sems):
        for cp in copies(ins, outs, sems):
            cp.start()

    def finish(ins, outs, sems):
        for cp in copies(ins, outs, sems):
            cp.wait()

    return _Carry(send, [jax.ShapeDtypeStruct(p.shape, p.dtype) for p in send],
                  [pltpu.SemaphoreType.DMA((n,)), pltpu.SemaphoreType.DMA((n,))], start, finish)


def _chip_exchange(sums):
    n = len(sums)

    def copies(ins, outs, sems):
        x, y, c = _position()
        return [_remote(ins[a].at[idx], outs[a].at[r], sems[0], sems[1], 3 * a + r, (*chip, c))
                for a in range(n) for r, (chip, idx) in enumerate(_other_chips(x, y))]

    def start(ins, outs, sems):
        for cp in copies(ins, outs, sems):
            cp.start()

    def finish(ins, outs, sems):
        for cp in copies(ins, outs, sems):
            cp.wait()

    return _Carry(sums, [jax.ShapeDtypeStruct((3,) + p.shape[1:], p.dtype) for p in sums],
                  [pltpu.SemaphoreType.DMA((3 * n,)), pltpu.SemaphoreType.DMA((3 * n,))], start, finish)


def _pair_share(pairs):
    n = len(pairs)

    def start(ins, outs, sems):
        x, y, c = _position()
        for a in range(n):
            _remote(outs[a].at[c], outs[a].at[c], sems[0], sems[1], a, (x, y, 1 - c)).start()

    def finish(ins, outs, sems):
        x, y, c = _position()
        for a in range(n):
            _remote(outs[a].at[1 - c], outs[a].at[1 - c], sems[0], sems[1], a, (x, y, 1 - c)).wait_recv()
        for a in range(n):
            _remote(outs[a].at[c], outs[a].at[c], sems[0], sems[1], a, (x, y, 1 - c)).wait_send()

    return _Carry(pairs, [jax.ShapeDtypeStruct(p.shape, p.dtype) for p in pairs],
                  [pltpu.SemaphoreType.DMA((n,)), pltpu.SemaphoreType.DMA((n,))], start, finish,
                  aliases={a: a for a in range(n)})


def _small_allreduce(pack, share):
    rows, width = pack.shape
    n_share = len(share.ins)

    def body(p_ref, *refs):
        share_in, o_ref, share_out = refs[:n_share], refs[n_share], refs[n_share + 1:2 * n_share + 1]
        buf, send_sems, recv_sems = refs[2 * n_share + 1:2 * n_share + 4]
        share_sems = refs[2 * n_share + 4:]
        share.start(share_in, share_out, share_sems)
        x, y, c = _position()
        me = 4 * x + 2 * y + c
        buf[me] = p_ref[...]
        peers = []
        for r in range(1, N_DEV):
            fx, fy, fc = (r >> 2) & 1, (r >> 1) & 1, r & 1
            px, py, pc = (1 - x if fx else x), (1 - y if fy else y), (1 - c if fc else c)
            peers.append(((px, py, pc), 4 * px + 2 * py + pc))
        sends = [_remote(p_ref, buf.at[me], send_sems, recv_sems, r, dev) for r, (dev, _) in enumerate(peers)]
        for cp in sends:
            cp.start()
        for r, (dev, idx) in enumerate(peers):
            _remote(p_ref, buf.at[idx], send_sems, recv_sems, r, dev).wait_recv()
        for cp in sends:
            cp.wait_send()
        acc = buf[0]
        for k in range(1, N_DEV):
            acc = acc + buf[k]
        o_ref[...] = acc
        share.finish(share_in, share_out, share_sems)

    vm = pl.BlockSpec(memory_space=pltpu.VMEM)
    red, *shared = pl.pallas_call(
        body, in_specs=[vm] + [_HBM] * n_share, out_specs=[vm] + [_HBM] * n_share,
        out_shape=[jax.ShapeDtypeStruct(pack.shape, F32)] + share.out_shapes,
        scratch_shapes=[pltpu.VMEM((N_DEV, rows, width), F32), pltpu.SemaphoreType.DMA((N_DEV - 1,)),
                        pltpu.SemaphoreType.DMA((N_DEV - 1,))] + share.sems,
        input_output_aliases={1 + i: 1 + o for i, o in share.aliases.items()},
        name="small_allreduce")(pack, *share.ins)
    return red, shared


ROW_TILE_MAX = 512
SUM_TILE_MAX = 2048
BF16_SUBLANES = 16


def _row_tile(rows, most=ROW_TILE_MAX):
    if rows <= most:
        return rows
    return max(t for t in range(BF16_SUBLANES, most + 1, BF16_SUBLANES) if rows % t == 0)


def _pair_add(keep, recv, name):
    nj, rh, cols = keep.shape
    tr = _row_tile(rh, SUM_TILE_MAX)

    def body(k_ref, r_ref, o_ref):
        o_ref[...] = (k_ref[...].astype(F32) + r_ref[...].astype(F32)).astype(BF16)

    blk = pl.BlockSpec((None, tr, cols), lambda j, i: (j, i, 0))
    return pl.pallas_call(body, grid=(nj, rh // tr), in_specs=[blk, blk], out_specs=blk,
                          out_shape=jax.ShapeDtypeStruct(keep.shape, BF16), name=name,
                          compiler_params=_params())(keep, recv)


def _chip_add(sums, recv, where, name):
    _, rh, cols = sums.shape
    tr = _row_tile(rh, SUM_TILE_MAX)

    def body(w_ref, s_ref, r_ref, o_ref):
        o_ref[...] = ((s_ref[...].astype(F32) + r_ref[0].astype(F32)) + r_ref[1].astype(F32)) + r_ref[2].astype(F32)

    grid_spec = pltpu.PrefetchScalarGridSpec(
        num_scalar_prefetch=1, grid=(rh // tr,),
        in_specs=[pl.BlockSpec((None, tr, cols), lambda i, w_ref: (w_ref[0], i, 0)),
                  pl.BlockSpec((3, tr, cols), lambda i, w_ref: (0, i, 0))],
        out_specs=pl.BlockSpec((None, tr, cols), lambda i, w_ref: (w_ref[1], i, 0)))
    return pl.pallas_call(body, grid_spec=grid_spec, out_shape=jax.ShapeDtypeStruct((2, rh, cols), F32),
                          name=name, compiler_params=_params())(where, sums, recv)


def _adamw(w, g, m, v, name):
    rows, cols = w.shape
    tr = _row_tile(rows)
    assert rows % tr == 0

    def body(w_ref, g_ref, m_ref, v_ref, d_ref, mo_ref, vo_ref):
        gv = g_ref[...]
        m_new = ADAM_B1 * m_ref[...] + (1.0 - ADAM_B1) * gv
        v_new = ADAM_B2 * v_ref[...] + (1.0 - ADAM_B2) * jnp.square(gv)
        m_hat = m_new / (1.0 - ADAM_B1 ** ADAM_STEP)
        v_hat = v_new / (1.0 - ADAM_B2 ** ADAM_STEP)
        d_ref[...] = -ADAM_LR * (m_hat / (jnp.sqrt(v_hat) + ADAM_EPS) + ADAM_WD * w_ref[...])
        mo_ref[...] = m_new
        vo_ref[...] = v_new

    blk = pl.BlockSpec((tr, cols), lambda i: (i, 0))
    shp = jax.ShapeDtypeStruct((rows, cols), F32)
    return pl.pallas_call(body, grid=(rows // tr,), in_specs=[blk] * 4, out_specs=[blk] * 3,
                          out_shape=[shp] * 3, name=name, compiler_params=_params())(w, g, m, v)


def _adamw_halves(w, g2, m, v, name):
    rows, cols = w.shape
    half = cols // 2
    tr = _row_tile(rows)

    def body(w_ref, g_ref, m_ref, v_ref, go_ref, d_ref, mo_ref, vo_ref):
        gv = g_ref[...]
        go_ref[...] = gv
        m_new = ADAM_B1 * m_ref[...] + (1.0 - ADAM_B1) * gv
        v_new = ADAM_B2 * v_ref[...] + (1.0 - ADAM_B2) * jnp.square(gv)
        m_hat = m_new / (1.0 - ADAM_B1 ** ADAM_STEP)
        v_hat = v_new / (1.0 - ADAM_B2 ** ADAM_STEP)
        d_ref[...] = -ADAM_LR * (m_hat / (jnp.sqrt(v_hat) + ADAM_EPS) + ADAM_WD * w_ref[...])
        mo_ref[...] = m_new
        vo_ref[...] = v_new

    blk = pl.BlockSpec((tr, half), lambda hf, i: (i, hf))
    gsp = pl.BlockSpec((None, tr, half), lambda hf, i: (hf, i, 0))
    shp = jax.ShapeDtypeStruct((rows, cols), F32)
    return pl.pallas_call(body, grid=(2, rows // tr), in_specs=[blk, gsp, blk, blk], out_specs=[blk] * 4,
                          out_shape=[shp] * 4, name=name, compiler_params=_params())(w, g2, m, v)


SHARD_W = IN_WIDTH // N_CHIPS


def _half_major(a):
    r, c = a.shape
    return a.reshape(N_CHIPS, 2, r // N_CHIPS // 2, c).transpose(1, 0, 2, 3)


def kernel(x, mem, g_pre, w_in, w_conv, attn_sink, g_mem, w_mem_kv, w_up_a, w_up_b, w_up_m, w_out, g_post, loss_target, m_g_pre, m_w_in, m_w_conv, m_attn_sink, m_g_mem, m_w_mem_kv, m_w_up_a, m_w_up_b, m_w_up_m, m_w_out, m_g_post, v_g_pre, v_w_in, v_w_conv, v_attn_sink, v_g_mem, v_w_mem_kv, v_w_up_a, v_w_up_b, v_w_up_m, v_w_out, v_g_post):
    xi, yi, ci = _position()
    chip = 2 * xi + yi
    where = jnp.stack([chip, ci, N_CHIPS - 1 - chip]).astype(jnp.int32)

    own = [w_in[0].T.astype(BF16), w_mem_kv[0].astype(BF16),
           jnp.concatenate([w_up_a[0], w_up_b[0], w_up_m[0]], axis=0).astype(BF16), w_out[0].astype(BF16)]
    own_conv = jnp.pad(w_conv[0], ((0, 5), (0, 0)))

    def pieces(mine, got):
        got = lax.dynamic_update_slice_in_dim(got, mine[None], chip, axis=0)
        return [got[j] for j in range(N_CHIPS)]

    diag = N_CHIPS - 1 - chip
    proj, h, h_t, got_near, got_conv, got_far = _proj_near(x[0], g_pre, own[0], own_conv, where)
    w_near = lax.dynamic_update_slice_in_dim(got_near, own[0][None], chip, axis=0).reshape(IN_WIDTH, D_MODEL)
    far = lax.dynamic_index_in_dim(got_far, diag, 0, keepdims=False)
    proj = _proj_far(h, w_near, far, where, into=proj)
    w_conv_full = jnp.concatenate([p[:3] for p in pieces(own_conv, got_conv)], axis=1)

    def late_weights(gathered):
        w_kv_full = jnp.concatenate(pieces(own[1], gathered[0]), axis=0)
        up_pieces = pieces(own[2], gathered[1])
        w_up_full = jnp.stack([jnp.concatenate([p[k * A_WIDTH:(k + 1) * A_WIDTH] for p in up_pieces], axis=1)
                               for k in range(3)])
        return w_kv_full, w_up_full, jnp.concatenate(pieces(own[3], gathered[2]), axis=0)

    def pick(parts, hf):
        return [lax.dynamic_index_in_dim(p, hf, 0, keepdims=False) for p in parts]

    def up_out_parts(dw_up, dw_out):
        up = (dw_up.reshape(3, A_WIDTH, N_CHIPS, D_MODEL // N_CHIPS).transpose(2, 0, 1, 3)
              .reshape(N_CHIPS, 2, 3 * A_WIDTH // 2, D_MODEL // N_CHIPS).transpose(1, 0, 2, 3))
        return [up.astype(BF16), _half_major(dw_out).astype(BF16)]

    g = _forward_backward(x[0], mem[0], loss_target[0], proj, w_conv_full, attn_sink, g_mem,
                          (_gather_weights(own[1:]), late_weights), g_post,
                          lambda dw_up, dw_out: _pair_exchange(pick(up_out_parts(dw_up, dw_out), 1 - ci)),
                          lambda dw_kv: _pair_exchange(pick([_half_major(dw_kv).astype(BF16)], 1 - ci)))

    half_rows = D_MODEL // 2

    def dw_in_half(half_of, name, carry):
        dw, carried = _dw_in_t(g["dproj"], h_t, half_of=half_of, where=where, name=name, carry=carry)
        return dw.reshape(N_CHIPS, SHARD_W, half_rows), carried

    small_keep = pick([_half_major(g["w_kv"]).astype(BF16)] + up_out_parts(g["w_up"], g["w_out"]), ci)
    small_names = ["w_kv", "w_up", "w_out"]
    sums_small = [_pair_add(k, r, "pair_add_" + nm)
                  for k, r, nm in zip(small_keep, g["early_kv"] + g["early"], small_names)]
    dw_send, recv3_small = dw_in_half(lambda w: 1 - w[1], "dw_in_send", _chip_exchange(sums_small))
    dw_keep, (recv_in,) = dw_in_half(lambda w: w[1], "dw_in_keep", _pair_exchange([dw_send]))
    sum_in = _pair_add(dw_keep, recv_in, "pair_add_w_in")
    (grad_x, dg_pre), (recv3_in,) = _d_h(g["dproj"], w_near, far, where, x[0], g_pre, g["dy"],
                                         carry=_chip_exchange([sum_in]))
    pairs = [_chip_add(s, r, where, "chip_add_" + nm)
             for s, r, nm in zip([sum_in] + sums_small, [recv3_in] + recv3_small, ["w_in"] + small_names)]

    zeros512 = jnp.zeros((1, D_MODEL - A_WIDTH), F32)
    conv_rows = [jnp.concatenate([g["w_conv"][k:k + 1], zeros512], axis=1) for k in range(3)]
    sink_row = jnp.pad(g["sink"][:, 0].reshape(1, N_Q_HEADS), ((0, 0), (0, D_MODEL - N_Q_HEADS)))
    loss_row = jnp.pad(g["loss"], ((0, 0), (0, D_MODEL - LANES)))
    pack = jnp.concatenate([dg_pre, g["g_mem"], g["g_post"]] + conv_rows + [sink_row, loss_row], axis=0)
    red, full = _small_allreduce(pack, _pair_share(pairs))
    loss = red[7, 0]
    small_grads = dict(
        g_pre=red[0:1], g_mem=red[1:2], g_post=red[2:3], attn_sink=red[6:7, :N_Q_HEADS],
        w_conv=lax.dynamic_slice(red[3:6, :A_WIDTH], (0, chip * LANES), (3, LANES)))

    gw_up = full[2].reshape(3, A_WIDTH, D_MODEL // N_CHIPS)
    grads = dict(small_grads, w_mem_kv=full[1].reshape(D_MODEL // N_CHIPS, 2 * MEM_WIDTH),
                 w_up_a=gw_up[0], w_up_b=gw_up[1], w_up_m=gw_up[2],
                 w_out=full[3].reshape(D_MODEL // N_CHIPS, D_MODEL))

    weights = dict(g_pre=g_pre, w_in=w_in, w_conv=w_conv, attn_sink=attn_sink, g_mem=g_mem, w_mem_kv=w_mem_kv,
                   w_up_a=w_up_a, w_up_b=w_up_b, w_up_m=w_up_m, w_out=w_out, g_post=g_post)
    m_in = dict(g_pre=m_g_pre, w_in=m_w_in, w_conv=m_w_conv, attn_sink=m_attn_sink, g_mem=m_g_mem,
                w_mem_kv=m_w_mem_kv, w_up_a=m_w_up_a, w_up_b=m_w_up_b, w_up_m=m_w_up_m, w_out=m_w_out,
                g_post=m_g_post)
    v_in = dict(g_pre=v_g_pre, w_in=v_w_in, w_conv=v_w_conv, attn_sink=v_attn_sink, g_mem=v_g_mem,
                w_mem_kv=v_w_mem_kv, w_up_a=v_w_up_a, w_up_b=v_w_up_b, w_up_m=v_w_up_m, w_out=v_w_out,
                g_post=v_g_post)
    out_g, out_d, out_m, out_v = [], [], [], []
    for nm in ("g_pre", "w_in", "w_conv", "attn_sink", "g_mem", "w_mem_kv", "w_up_a", "w_up_b", "w_up_m", "w_out",
               "g_post"):
        shape = weights[nm].shape
        if nm == "w_in":
            results = _adamw_halves(w_in[0].T, full[0], m_w_in[0].T, v_w_in[0].T, "adamw_w_in")
            for out, t in zip((out_g, out_d, out_m, out_v), results):
                out.append(t.T.reshape(shape))
            continue
        two_d = shape[-2:]
        gr = grads[nm].reshape(two_d)
        d, m_new, v_new = _adamw(weights[nm].reshape(two_d), gr, m_in[nm].reshape(two_d), v_in[nm].reshape(two_d),
                                 "adamw_" + nm)
        out_g.append(gr.reshape(shape))
        out_d.append(d.reshape(shape))
        out_m.append(m_new.reshape(shape))
        out_v.append(v_new.reshape(shape))
    return (loss, grad_x.reshape(x.shape), *out_g, *out_d, *out_m, *out_v)
```

```python
import jax
import jax.numpy as jnp
from jax import lax
from jax.experimental import pallas as pl
from jax.experimental.pallas import tpu as pltpu

F32 = jnp.float32
BF16 = jnp.bfloat16
MESH = pl.DeviceIdType.MESH

D_MODEL = 1024
EPS = 1e-6
A_WIDTH = 512
HEAD_DIM = 64
N_Q_HEADS = 8
WINDOW_BLOCK = 128
KV_PAD = 512
ROPE_THETA = 500000.0
ROT_DIM = 16
MEM_HEADS = 4
MEM_HEAD_DIM = 128
MEM_WIDTH = 512
IN_WIDTH = 7424
N_CHIPS = 4
LANES = 128
HALF_LANES = 64

PERM_SEGS = ((0, 2560), (2816, 3328), (4352, 7424), (3328, 4352), (2560, 2816))
COL_A, W_A = 0, 2048
COL_B, W_B = 2, 1024
COL_G, W_G = 1, 3072
COL_M, W_M = 6, 1024
COL_KV, W_KV = 28, 256

ADAM_LR = 0.001
ADAM_B1 = 0.9
ADAM_B2 = 0.999
ADAM_EPS = 1e-08
ADAM_WD = 0.01
ADAM_STEP = 10

VMEM_LIGHT_BYTES = 48 * 1024 * 1024
VMEM_HEAVY_BYTES = 48 * 1024 * 1024


_HBM = pl.BlockSpec(memory_space=pltpu.HBM)


def _params(heavy=False):
    return pltpu.CompilerParams(vmem_limit_bytes=VMEM_HEAVY_BYTES if heavy else VMEM_LIGHT_BYTES)


def _sigmoid(v):
    return jax.nn.sigmoid(v)


_DIMS = {"nn": (((1,), (0,)), ((), ())), "nt": (((1,), (1,)), ((), ())), "tn": (((0,), (0,)), ((), ()))}


class _Carry:
    def __init__(self, ins, out_shapes, sems, start, finish, aliases=None):
        self.ins, self.out_shapes, self.sems = list(ins), list(out_shapes), list(sems)
        self.start, self.finish, self.aliases = start, finish, dict(aliases or {})


def _join(*carries):
    def split(seq, counts):
        pos, parts = 0, []
        for n in counts:
            parts.append(seq[pos:pos + n])
            pos += n
        return parts

    n_in = [len(c.ins) for c in carries]
    n_out = [len(c.out_shapes) for c in carries]
    n_sem = [len(c.sems) for c in carries]

    def run(which):
        def go(ins, outs, sems):
            for c, i, o, sm in zip(carries, split(ins, n_in), split(outs, n_out), split(sems, n_sem)):
                getattr(c, which)(i, o, sm)
        return go

    aliases = {}
    for k, c in enumerate(carries):
        aliases.update({sum(n_in[:k]) + i: sum(n_out[:k]) + o for i, o in c.aliases.items()})
    return _Carry([a for c in carries for a in c.ins], [sh for c in carries for sh in c.out_shapes],
                  [sm for c in carries for sm in c.sems], run("start"), run("finish"), aliases)


def _carried_call(body, carry, *, grid, in_specs, out_specs, out_shape, scratch, operands, name, prefetch=None,
                  aliases=None, heavy=False):
    n_in, n_out, n_scr = len(in_specs), len(out_specs), len(scratch)
    c_in = len(carry.ins) if carry else 0
    c_out = len(carry.out_shapes) if carry else 0
    n_pre = 0 if prefetch is None else 1
    steps = 1
    for g in grid:
        steps *= g

    def wrapped(*refs):
        refs = refs[n_pre:]
        ins, cins = refs[:n_in], refs[n_in:n_in + c_in]
        outs = refs[n_in + c_in:n_in + c_in + n_out]
        couts = refs[n_in + c_in + n_out:n_in + c_in + n_out + c_out]
        rest = refs[n_in + c_in + n_out + c_out:]
        scr, sems = rest[:n_scr], rest[n_scr:]
        if carry:
            step = pl.program_id(0)
            for ax in range(1, len(grid)):
                step = step * grid[ax] + pl.program_id(ax)

            @pl.when(step == 0)
            def _():
                carry.start(cins, couts, sems)

        body(ins, outs, scr)
        if carry:
            @pl.when(step == steps - 1)
            def _():
                carry.finish(cins, couts, sems)

    all_aliases = {n_pre + i: o for i, o in (aliases or {}).items()}
    if carry:
        all_aliases.update({n_pre + n_in + i: n_out + o for i, o in carry.aliases.items()})
    all_in = list(in_specs) + [_HBM] * c_in
    all_out = list(out_specs) + [_HBM] * c_out
    all_scratch = list(scratch) + (carry.sems if carry else [])
    if n_pre:
        spec = dict(grid_spec=pltpu.PrefetchScalarGridSpec(num_scalar_prefetch=1, grid=grid, in_specs=all_in,
                                                           out_specs=all_out, scratch_shapes=all_scratch))
        pre = (prefetch,)
    else:
        spec = dict(grid=grid, in_specs=all_in, out_specs=all_out, scratch_shapes=all_scratch)
        pre = ()
    results = pl.pallas_call(
        wrapped, out_shape=list(out_shape) + (carry.out_shapes if carry else []), input_output_aliases=all_aliases,
        name=name, compiler_params=_params(heavy), **spec)(*pre, *operands, *(carry.ins if carry else []))
    return list(results[:n_out]), list(results[n_out:])


def _matmul(a, b, *, mode, out_dtype, tm, tn, tk, name):
    if mode == "nn":
        (m, k), (_, n) = a.shape, b.shape
    elif mode == "nt":
        (m, k), (n, _) = a.shape, b.shape
    else:
        (k, m), (_, n) = a.shape, b.shape
    tm, tn, tk = min(tm, m), min(tn, n), min(tk, k)
    assert m % tm == 0 and n % tn == 0 and k % tk == 0
    nk = k // tk
    dims = _DIMS[mode]

    if mode == "nn":
        a_spec = pl.BlockSpec((tm, tk), lambda i, j, kk: (i, kk))
        b_spec = pl.BlockSpec((tk, tn), lambda i, j, kk: (kk, j))
    elif mode == "nt":
        a_spec = pl.BlockSpec((tm, tk), lambda i, j, kk: (i, kk))
        b_spec = pl.BlockSpec((tn, tk), lambda i, j, kk: (j, kk))
    else:
        a_spec = pl.BlockSpec((tk, tm), lambda i, j, kk: (kk, i))
        b_spec = pl.BlockSpec((tk, tn), lambda i, j, kk: (kk, j))
    o_spec = pl.BlockSpec((tm, tn), lambda i, j, kk: (i, j))

    def part(a_ref, b_ref):
        return lax.dot_general(a_ref[...].astype(BF16), b_ref[...].astype(BF16), dims,
                               preferred_element_type=F32)

    if nk == 1:
        def body(a_ref, b_ref, o_ref):
            o_ref[...] = part(a_ref, b_ref).astype(out_dtype)
        scratch = []
    else:
        def body(a_ref, b_ref, o_ref, acc_ref):
            kk = pl.program_id(2)

            @pl.when(kk == 0)
            def _():
                acc_ref[...] = part(a_ref, b_ref)

            @pl.when(kk > 0)
            def _():
                acc_ref[...] += part(a_ref, b_ref)

            @pl.when(kk == nk - 1)
            def _():
                o_ref[...] = acc_ref[...].astype(out_dtype)
        scratch = [pltpu.VMEM((tm, tn), F32)]

    return pl.pallas_call(
        body, grid=(m // tm, n // tn, nk), in_specs=[a_spec, b_spec], out_specs=o_spec,
        out_shape=jax.ShapeDtypeStruct((m, n), out_dtype), scratch_shapes=scratch,
        name=name, compiler_params=_params())(a, b)


IN_BLOCK = 256
N_IN_BLOCKS = IN_WIDTH // IN_BLOCK
SHARD_BLOCKS = (IN_WIDTH // N_CHIPS) // IN_BLOCK
BLOCK_RUNS = tuple((a // IN_BLOCK, sum(d - c for c, d in PERM_SEGS[:k]) // IN_BLOCK, (b - a) // IN_BLOCK)
                   for k, (a, b) in enumerate(PERM_SEGS))


def _perm_block(r):
    p = r
    for ref0, perm0, n in BLOCK_RUNS:
        p = jnp.where((r >= ref0) & (r < ref0 + n), r - ref0 + perm0, p)
    return p


def _proj_near(x, g_pre, own_w, small, where):
    s, d = x.shape
    norm_tile = min(512, s)
    n_own = SHARD_BLOCKS - 1
    n_diag = SHARD_BLOCKS + 1
    n_blocks = N_IN_BLOCKS - n_diag
    piece = IN_WIDTH // N_CHIPS - SHARD_BLOCKS * IN_BLOCK
    near = _gather_weights([own_w], small, relations=(0, 1))
    far = _gather_weights([own_w], relations=(2,))
    both = _join(near, far)
    n_cin, n_cout = len(both.ins), len(both.out_shapes)

    def block_of(i, w):
        me, dg = w[0], w[2]
        own0 = SHARD_BLOCKS * me + jnp.minimum(me, 1)
        dg0 = SHARD_BLOCKS * dg
        lo0, hi0 = jnp.minimum(own0, dg0), jnp.maximum(own0, dg0)
        lo_n = jnp.where(own0 < dg0, n_own, n_diag)
        hi_n = jnp.where(own0 < dg0, n_diag, n_own)
        r = i - n_own
        r = r + lo_n * (r >= lo0).astype(jnp.int32)
        r = r + hi_n * (r >= hi0).astype(jnp.int32)
        return jnp.where(i < n_own, own0 + i, r)

    def body(w_ref, x_hbm, g_ref, own_hbm, *refs):
        cins, (o_ref, h_hbm, ht_hbm) = refs[:n_cin], refs[n_cin:n_cin + 3]
        couts = refs[n_cin + 3:n_cin + 3 + n_cout]
        blocks, block_sems, h_ref, x_tile, ht_tile, io_sem = refs[n_cin + 3 + n_cout:n_cin + 9 + n_cout]
        sems = refs[n_cin + 9 + n_cout:]
        near_refs = (cins[:len(near.ins)], couts[:len(near.out_shapes)], sems[:len(near.sems)])
        far_refs = (cins[len(near.ins):], couts[len(near.out_shapes):], sems[len(near.sems):])
        gathered = couts[0]
        i = pl.program_id(0)
        me = w_ref[0]

        def fetch(step, slot):
            r = block_of(step, w_ref)
            for p in range(IN_BLOCK // piece):
                row = r * IN_BLOCK + p * piece
                j = row // (IN_WIDTH // N_CHIPS)
                off = pl.multiple_of(row - j * (IN_WIDTH // N_CHIPS), BF16_SUBLANES)
                dst = blocks.at[slot, pl.ds(p * piece, piece)]

                @pl.when(j == me)
                def _():
                    pltpu.make_async_copy(own_hbm.at[pl.ds(off, piece)], dst, block_sems.at[slot]).start()

                @pl.when(j != me)
                def _():
                    pltpu.make_async_copy(gathered.at[j, pl.ds(off, piece)], dst, block_sems.at[slot]).start()

        def arrived(slot):
            pltpu.make_async_copy(own_hbm.at[pl.ds(0, IN_BLOCK)], blocks.at[slot], block_sems.at[slot]).wait()

        slot = i % 2

        def norm_rows(k):
            rows = pl.ds(k * norm_tile, norm_tile)
            pltpu.sync_copy(x_hbm.at[rows], x_tile)
            xv = x_tile[...]
            hv = (xv * lax.rsqrt(jnp.mean(xv * xv, axis=-1, keepdims=True) + EPS)) * g_ref[...]
            h_ref[rows, :] = hv.astype(BF16)
            ht_tile[...] = hv.T.astype(BF16)
            to_h = pltpu.make_async_copy(h_ref.at[rows], h_hbm.at[rows], io_sem.at[0])
            to_ht = pltpu.make_async_copy(ht_tile, ht_hbm.at[:, rows], io_sem.at[1])
            to_h.start()
            to_ht.start()
            to_h.wait()
            to_ht.wait()

        @pl.when(i == 0)
        def _():
            near.start(*near_refs)
            fetch(i, slot)
            for k in range(s // norm_tile):
                norm_rows(k)

        @pl.when(i == n_own)
        def _():
            near.finish(*near_refs)
            far.start(*far_refs)
            fetch(i, slot)

        arrived(slot)

        @pl.when((i + 1 < n_blocks) & (i + 1 != n_own))
        def _():
            fetch(i + 1, 1 - slot)

        o_ref[...] = lax.dot_general(h_ref[...], blocks[slot], _DIMS["nt"], preferred_element_type=F32)

        @pl.when(i == n_blocks - 1)
        def _():
            far.finish(*far_refs)

    anysp = pl.BlockSpec(memory_space=pl.ANY)
    grid_spec = pltpu.PrefetchScalarGridSpec(
        num_scalar_prefetch=1, grid=(n_blocks,),
        in_specs=[anysp, pl.BlockSpec((1, d), lambda i, w: (0, 0)), anysp] + [_HBM] * n_cin,
        out_specs=[pl.BlockSpec((s, IN_BLOCK), lambda i, w: (0, _perm_block(block_of(i, w)))), anysp, anysp]
        + [_HBM] * n_cout,
        scratch_shapes=[pltpu.VMEM((2, IN_BLOCK, d), BF16), pltpu.SemaphoreType.DMA((2,)), pltpu.VMEM((s, d), BF16),
                        pltpu.VMEM((norm_tile, d), F32), pltpu.VMEM((d, norm_tile), BF16),
                        pltpu.SemaphoreType.DMA((2,))] + both.sems)
    return pl.pallas_call(
        body, grid_spec=grid_spec,
        out_shape=[jax.ShapeDtypeStruct((s, IN_WIDTH), F32), jax.ShapeDtypeStruct((s, d), BF16),
                   jax.ShapeDtypeStruct((d, s), BF16)] + both.out_shapes,
        name="proj_near", compiler_params=_params())(where, x, g_pre, own_w, *both.ins)


def _proj_far(h, w_near, far, where, *, into, carry=None):
    s, d = h.shape
    n_blocks = SHARD_BLOCKS + 1
    lead = IN_WIDTH // N_CHIPS - SHARD_BLOCKS * IN_BLOCK

    def body(ins, outs, scr):
        where_ref, h_ref, w_hbm, far_hbm, _ = ins
        win, sem = scr
        i = pl.program_id(0)

        @pl.when(i == 0)
        def _():
            dg = where_ref[2]
            rows = pl.ds(pl.multiple_of(dg * (SHARD_BLOCKS * IN_BLOCK), IN_BLOCK), n_blocks * IN_BLOCK)
            window = pltpu.make_async_copy(w_hbm.at[rows], win, sem)
            window.start()
            window.wait()
            shard = pltpu.make_async_copy(far_hbm, win.at[pl.ds(pl.multiple_of(dg * lead, BF16_SUBLANES), SHARD_W)], sem)
            shard.start()
            shard.wait()

        blk = win[pl.ds(pl.multiple_of(i * IN_BLOCK, IN_BLOCK), IN_BLOCK), :]
        outs[0][...] = lax.dot_general(h_ref[...], blk, _DIMS["nt"], preferred_element_type=F32)

    anysp = pl.BlockSpec(memory_space=pl.ANY)
    (proj,), carried = _carried_call(
        body, carry, grid=(n_blocks,),
        in_specs=[pl.BlockSpec(memory_space=pltpu.SMEM), pl.BlockSpec((s, d), lambda i, w: (0, 0)), anysp, anysp, anysp],
        out_specs=[pl.BlockSpec((s, IN_BLOCK), lambda i, w: (0, _perm_block(i + SHARD_BLOCKS * w[2])))],
        out_shape=[jax.ShapeDtypeStruct((s, IN_WIDTH), F32)],
        scratch=[pltpu.VMEM((n_blocks * IN_BLOCK, d), BF16), pltpu.SemaphoreType.DMA],
        operands=(where, h, w_near, far, into), name="proj_far", prefetch=where, aliases={4: 0})
    return (proj, carried) if carry else proj


def _dw_in_t(dproj, h_t, *, half_of, where, name, carry=None):
    d, s = h_t.shape
    c = d // 2

    def body(ins, outs, scr):
        outs[0][...] = lax.dot_general(ins[1][...], ins[0][...], _DIMS["nn"], preferred_element_type=F32).T.astype(BF16)

    (dw,), carried = _carried_call(
        body, carry, grid=(N_IN_BLOCKS,),
        in_specs=[pl.BlockSpec((s, IN_BLOCK), lambda r, w: (0, _perm_block(r))),
                  pl.BlockSpec((c, s), lambda r, w: (half_of(w), 0))],
        out_specs=[pl.BlockSpec((IN_BLOCK, c), lambda r, w: (r, 0))],
        out_shape=[jax.ShapeDtypeStruct((IN_WIDTH, c), BF16)], scratch=[], operands=(dproj, h_t), name=name,
        prefetch=where)
    return (dw, carried) if carry else dw


def _norm_bwd_tile(dhv, xv, gv, resv):
    r = lax.rsqrt(jnp.mean(xv * xv, axis=-1, keepdims=True) + EPS)
    xh = xv * r
    dxh = dhv * gv
    dx = resv + r * (dxh - xh * jnp.mean(dxh * xh, axis=-1, keepdims=True))
    return dx, jnp.sum(dhv * xh, axis=0, keepdims=True)


def _d_h(dproj, w_near, far, where, x, g, res, *, carry=None):
    s = dproj.shape[0]
    d = w_near.shape[1]
    tm = min(s, 256)
    n = s // tm
    assert n % 2 == 0

    def body(ins, outs, scr):
        where_ref, a_ref, w_hbm, far_hbm, x_ref, g_ref, res_ref = ins
        dx_ref, dg_ref = outs
        w_ref, sem, dh_even, dh_odd = scr
        i = pl.program_id(0)

        def norm_bwd(dh_ref):
            dx, part = _norm_bwd_tile(dh_ref[...], x_ref[...], g_ref[...], res_ref[...])
            dx_ref[...] = dx
            dg_ref[...] += part

        def matmul(dh_ref):
            acc = None
            for ref0, perm0, nb in BLOCK_RUNS:
                term = jnp.dot(a_ref[:, perm0 * IN_BLOCK:(perm0 + nb) * IN_BLOCK],
                               w_ref[ref0 * IN_BLOCK:(ref0 + nb) * IN_BLOCK, :], preferred_element_type=F32)
                acc = term if acc is None else acc + term
            dh_ref[...] = acc

        @pl.when(i == 0)
        def _():
            whole = pltpu.make_async_copy(w_hbm, w_ref, sem)
            whole.start()
            whole.wait()
            rows = pl.ds(pl.multiple_of(where_ref[2] * SHARD_W, BF16_SUBLANES), SHARD_W)
            part = pltpu.make_async_copy(far_hbm, w_ref.at[rows], sem)
            part.start()
            part.wait()
            dg_ref[...] = jnp.zeros_like(dg_ref)
            matmul(dh_even)

        @pl.when((i % 2 == 0) & (i > 0) & (i < n))
        def _():
            norm_bwd(dh_odd)
            matmul(dh_even)

        @pl.when(i % 2 == 1)
        def _():
            norm_bwd(dh_even)
            matmul(dh_odd)

        @pl.when(i == n)
        def _():
            norm_bwd(dh_odd)

    anysp = pl.BlockSpec(memory_space=pl.ANY)
    before = pl.BlockSpec((tm, d), lambda i: (jnp.maximum(i - 1, 0), 0))
    vec = pl.BlockSpec((1, d), lambda i: (0, 0))
    outs, carried = _carried_call(
        body, carry, grid=(n + 1,),
        in_specs=[pl.BlockSpec(memory_space=pltpu.SMEM),
                  pl.BlockSpec((tm, IN_WIDTH), lambda i: (jnp.minimum(i, n - 1), 0)), anysp, anysp, before, vec, before],
        out_specs=[before, vec],
        out_shape=[jax.ShapeDtypeStruct((s, d), F32), jax.ShapeDtypeStruct((1, d), F32)],
        scratch=[pltpu.VMEM((IN_WIDTH, d), BF16), pltpu.SemaphoreType.DMA, pltpu.VMEM((tm, d), F32),
                 pltpu.VMEM((tm, d), F32)],
        operands=(where, dproj, w_near, far, x, g, res), name="d_h", heavy=True)
    return (outs, carried) if carry else outs


def _rmsnorm_fwd(x, g, *, name):
    s, d = x.shape
    ts = min(512, s)

    def body(x_ref, g_ref, o_ref):
        xv = x_ref[...]
        r = lax.rsqrt(jnp.mean(xv * xv, axis=-1, keepdims=True) + EPS)
        o_ref[...] = ((xv * r) * g_ref[...]).astype(BF16)

    return pl.pallas_call(
        body, grid=(s // ts,),
        in_specs=[pl.BlockSpec((ts, d), lambda i: (i, 0)), pl.BlockSpec((1, d), lambda i: (0, 0))],
        out_specs=pl.BlockSpec((ts, d), lambda i: (i, 0)),
        out_shape=jax.ShapeDtypeStruct((s, d), BF16), name=name, compiler_params=_params())(x, g)


def _rmsnorm_bwd(dh, x, g, res, *, name, carry=None):
    s, d = x.shape
    ts = min(256, s)

    def body(ins, outs, scr):
        dh_ref, x_ref, g_ref, res_ref = ins
        dx_ref, dg_ref = outs
        dx, part = _norm_bwd_tile(dh_ref[...], x_ref[...], g_ref[...], res_ref[...])

        @pl.when(pl.program_id(0) == 0)
        def _():
            dg_ref[...] = part

        @pl.when(pl.program_id(0) > 0)
        def _():
            dg_ref[...] += part

        dx_ref[...] = dx

    row = pl.BlockSpec((ts, d), lambda i: (i, 0))
    vec = pl.BlockSpec((1, d), lambda i: (0, 0))
    outs, carried = _carried_call(
        body, carry, grid=(s // ts,), in_specs=[row, row, vec, row], out_specs=[row, vec],
        out_shape=[jax.ShapeDtypeStruct((s, d), F32), jax.ShapeDtypeStruct((1, d), F32)],
        scratch=[], operands=(dh, x, g, res), name=name)
    return (*outs, carried) if carry else tuple(outs)


MID_TILE = 256


def _gated_branches(y_refs, wup_ref, gl):
    d = D_MODEL
    us = [jnp.dot(y_refs[k][...], wup_ref[k], preferred_element_type=F32) for k in range(3)]
    sg = [_sigmoid(gl[:, k * d:(k + 1) * d]) for k in range(3)]
    return us, sg


def _mid_fwd(ya, yb, ym, proj, x, tgt, w_up, w_out, g_post):
    s, d = x.shape
    ts = MID_TILE

    def body(ya_ref, yb_ref, ym_ref, g_ref, x_ref, t_ref, wup_ref, wout_ref, gp_ref,
             m_ref, do_ref, dy_ref, dg_ref, loss_ref):
        us, sg = _gated_branches((ya_ref, yb_ref, ym_ref), wup_ref, g_ref[...])
        merged = (sg[0] * us[0] + sg[1] * us[1] + sg[2] * us[2]).astype(BF16)
        m_ref[...] = merged
        ov = jnp.dot(merged, wout_ref[...], preferred_element_type=F32)
        r = lax.rsqrt(jnp.mean(ov * ov, axis=-1, keepdims=True) + EPS)
        nh = ov * r
        gv = gp_ref[...]
        e = (x_ref[...] + nh * gv) - t_ref[...]
        lpart = 0.5 * jnp.sum(jnp.mean(e * e, axis=-1, keepdims=True), axis=0, keepdims=True)
        dy = e * (1.0 / d)
        dgp = jnp.sum(dy * nh, axis=0, keepdims=True)

        @pl.when(pl.program_id(0) == 0)
        def _():
            dg_ref[...] = dgp
            loss_ref[...] = jnp.broadcast_to(lpart, loss_ref.shape)

        @pl.when(pl.program_id(0) > 0)
        def _():
            dg_ref[...] += dgp
            loss_ref[...] += jnp.broadcast_to(lpart, loss_ref.shape)

        dn = dy * gv
        dy_ref[...] = dy
        do_ref[...] = (r * (dn - nh * jnp.mean(dn * nh, axis=-1, keepdims=True))).astype(BF16)

    row = pl.BlockSpec((ts, d), lambda i: (i, 0))
    ysp = pl.BlockSpec((ts, A_WIDTH), lambda i: (i, 0))
    vec = pl.BlockSpec((1, d), lambda i: (0, 0))
    return pl.pallas_call(
        body, grid=(s // ts,),
        in_specs=[ysp, ysp, ysp, pl.BlockSpec((ts, W_G), lambda i: (i, COL_G)), row, row,
                  pl.BlockSpec((3, A_WIDTH, d), lambda i: (0, 0, 0)), pl.BlockSpec((d, d), lambda i: (0, 0)), vec],
        out_specs=[row, row, row, vec, pl.BlockSpec((1, LANES), lambda i: (0, 0))],
        out_shape=[jax.ShapeDtypeStruct((s, d), BF16), jax.ShapeDtypeStruct((s, d), BF16),
                   jax.ShapeDtypeStruct((s, d), F32), jax.ShapeDtypeStruct((1, d), F32),
                   jax.ShapeDtypeStruct((1, LANES), F32)],
        name="mid_fwd", compiler_params=_params(heavy=True))(ya, yb, ym, proj, x, tgt, w_up, w_out, g_post)


def _mid_bwd(d_out, merged, ya, yb, ym, proj, w_up, w_out):
    s, d = merged.shape
    ts = MID_TILE
    n = s // ts
    assert n % 2 == 0

    def body(do_ref, m_ref, ya_ref, yb_ref, ym_ref, pa_ref, pb_ref, pm_ref, g_ref, wup_ref, wout_ref,
             dp_ref, dya_ref, dyb_ref, dym_ref, dwup_hbm, dwout_hbm, dwup_acc, dwout_acc, du_even, du_odd):
        i = pl.program_id(0)

        def gates(du_ref):
            us, sg = _gated_branches((ya_ref, yb_ref, ym_ref), wup_ref, g_ref[...])
            dov = do_ref[...]
            dwout_acc[...] += lax.dot_general(m_ref[...], dov, _DIMS["tn"], preferred_element_type=F32)
            dm = lax.dot_general(dov, wout_ref[...], _DIMS["nt"], preferred_element_type=F32)
            for k in range(3):
                dp_ref[:, k * d:(k + 1) * d] = ((dm * us[k]) * (sg[k] * (1.0 - sg[k]))).astype(BF16)
                du_ref[k] = (sg[k] * dm).astype(BF16)

        def ups(du_ref):
            for k, (y_ref, dy_ref) in enumerate(zip((pa_ref, pb_ref, pm_ref), (dya_ref, dyb_ref, dym_ref))):
                du = du_ref[k]
                dy_ref[...] = lax.dot_general(du, wup_ref[k], _DIMS["nt"], preferred_element_type=F32)
                dwup_acc[k] += lax.dot_general(y_ref[...], du, _DIMS["tn"], preferred_element_type=F32)

        @pl.when(i == 0)
        def _():
            dwup_acc[...] = jnp.zeros_like(dwup_acc)
            dwout_acc[...] = jnp.zeros_like(dwout_acc)
            gates(du_even)

        @pl.when((i % 2 == 0) & (i > 0) & (i < n))
        def _():
            gates(du_even)
            ups(du_odd)

        @pl.when(i % 2 == 1)
        def _():
            gates(du_odd)
            ups(du_even)

        @pl.when(i == n)
        def _():
            ups(du_odd)
            pltpu.sync_copy(dwup_acc, dwup_hbm)
            pltpu.sync_copy(dwout_acc, dwout_hbm)

    def now(i):
        return jnp.minimum(i, n - 1)

    def before(i):
        return jnp.maximum(i - 1, 0)

    row = pl.BlockSpec((ts, d), lambda i: (now(i), 0))
    ysp = pl.BlockSpec((ts, A_WIDTH), lambda i: (now(i), 0))
    held = pl.BlockSpec((ts, A_WIDTH), lambda i: (before(i), 0))
    gsp = pl.BlockSpec((ts, W_G), lambda i: (now(i), COL_G))
    anysp = pl.BlockSpec(memory_space=pl.ANY)
    yshape = jax.ShapeDtypeStruct((s, A_WIDTH), F32)
    return pl.pallas_call(
        body, grid=(n + 1,),
        in_specs=[row, row, ysp, ysp, ysp, held, held, held, gsp, pl.BlockSpec((3, A_WIDTH, d), lambda i: (0, 0, 0)),
                  pl.BlockSpec((d, d), lambda i: (0, 0))],
        out_specs=[gsp, held, held, held, anysp, anysp],
        out_shape=[jax.ShapeDtypeStruct((s, IN_WIDTH), BF16), yshape, yshape, yshape,
                   jax.ShapeDtypeStruct((3, A_WIDTH, d), F32), jax.ShapeDtypeStruct((d, d), F32)],
        scratch_shapes=[pltpu.VMEM((3, A_WIDTH, d), F32), pltpu.VMEM((d, d), F32),
                        pltpu.VMEM((3, ts, d), BF16), pltpu.VMEM((3, ts, d), BF16)],
        name="mid_bwd", compiler_params=_params(heavy=True))(d_out, merged, ya, yb, ym, ya, yb, ym, proj, w_up, w_out)


def _conv_core(blk, prev, nxt, w, i, last, ts):
    c = A_WIDTH
    ab, ac, ax, az = blk[:, :c], blk[:, c:2 * c], blk[:, 2 * c:3 * c], blk[:, 3 * c:]
    cu = ac * ax
    cu_prev = (prev[7:8, c:2 * c] * prev[7:8, 2 * c:3 * c]) * jnp.where(i > 0, 1.0, 0.0)
    cu_next = (nxt[0:1, c:2 * c] * nxt[0:1, 2 * c:3 * c]) * jnp.where(i < last, 1.0, 0.0)
    row = lax.broadcasted_iota(jnp.int32, (ts, c), 0)
    cm1 = jnp.where(row == 0, cu_prev, pltpu.roll(cu, 1, 0))
    cp1 = jnp.where(row == ts - 1, cu_next, pltpu.roll(cu, ts - 1, 0))
    yc = cm1 * w[0:1] + cu * w[1:2] + cp1 * w[2:3]
    return ab, ac, ax, az, cu, cm1, cp1, yc, row


def _halo_specs(ts, width, col, nblk8):
    prev = pl.BlockSpec((8, width), lambda i: (jnp.maximum(i * (ts // 8) - 1, 0), col))
    nxt = pl.BlockSpec((8, width), lambda i: (jnp.minimum((i + 1) * (ts // 8), nblk8 - 1), col))
    return prev, nxt


def _conv_fwd(proj, w_conv):
    s = proj.shape[0]
    ts = 256
    last = s // ts - 1

    def body(a_ref, ap_ref, an_ref, w_ref, ya_ref):
        i = pl.program_id(0)
        ab, _, _, az, _, _, _, yc, _ = _conv_core(a_ref[...], ap_ref[...], an_ref[...], w_ref[...], i, last, ts)
        ya_ref[...] = ((ab * yc) * (az * _sigmoid(az))).astype(BF16)

    prev, nxt = _halo_specs(ts, W_A, COL_A, s // 8)
    return pl.pallas_call(
        body, grid=(s // ts,),
        in_specs=[pl.BlockSpec((ts, W_A), lambda i: (i, COL_A)), prev, nxt,
                  pl.BlockSpec((3, A_WIDTH), lambda i: (0, 0))],
        out_specs=pl.BlockSpec((ts, A_WIDTH), lambda i: (i, 0)),
        out_shape=jax.ShapeDtypeStruct((s, A_WIDTH), BF16), name="conv_fwd",
        compiler_params=_params())(proj, proj, proj, w_conv)


def _conv_bwd(proj, w_conv, dya, dproj):
    s = proj.shape[0]
    ts = 256
    last = s // ts - 1
    c = A_WIDTH

    def body(a_ref, ap_ref, an_ref, w_ref, d_ref, dp_ref, dn_ref, _, dproj_ref, dw_ref):
        i = pl.program_id(0)
        w = w_ref[...]
        prev, nxt = ap_ref[...], an_ref[...]
        ab, ac, ax, az, cu, cm1, cp1, yc, row = _conv_core(a_ref[...], prev, nxt, w, i, last, ts)
        sg = _sigmoid(az)
        sz = az * sg
        dya_v = d_ref[...]
        dyc = dya_v * sz * ab
        dproj_ref[:, :c] = (dya_v * sz * yc).astype(BF16)
        dproj_ref[:, 3 * c:] = (dya_v * (ab * yc) * (sg * (1.0 + az * (1.0 - sg)))).astype(BF16)

        def halo_dyc(a_row, d_row):
            azr = a_row[:, 3 * c:]
            return d_row * (azr * _sigmoid(azr)) * a_row[:, :c]

        dyc_prev = halo_dyc(prev[7:8], dp_ref[...][7:8]) * jnp.where(i > 0, 1.0, 0.0)
        dyc_next = halo_dyc(nxt[0:1], dn_ref[...][0:1]) * jnp.where(i < last, 1.0, 0.0)
        dyc_m1 = jnp.where(row == 0, dyc_prev, pltpu.roll(dyc, 1, 0))
        dyc_p1 = jnp.where(row == ts - 1, dyc_next, pltpu.roll(dyc, ts - 1, 0))
        dcu = dyc_p1 * w[0:1] + dyc * w[1:2] + dyc_m1 * w[2:3]
        dproj_ref[:, c:2 * c] = (dcu * ax).astype(BF16)
        dproj_ref[:, 2 * c:3 * c] = (dcu * ac).astype(BF16)
        dw = [jnp.sum(dyc * t, axis=0, keepdims=True) for t in (cm1, cu, cp1)]

        @pl.when(i == 0)
        def _():
            for k in range(3):
                dw_ref[k:k + 1, :] = dw[k]

        @pl.when(i > 0)
        def _():
            for k in range(3):
                dw_ref[k:k + 1, :] += dw[k]

    prev, nxt = _halo_specs(ts, W_A, COL_A, s // 8)
    dprev, dnxt = _halo_specs(ts, A_WIDTH, 0, s // 8)
    return pl.pallas_call(
        body, grid=(s // ts,),
        in_specs=[pl.BlockSpec((ts, W_A), lambda i: (i, COL_A)), prev, nxt,
                  pl.BlockSpec((3, A_WIDTH), lambda i: (0, 0)),
                  pl.BlockSpec((ts, A_WIDTH), lambda i: (i, 0)), dprev, dnxt,
                  pl.BlockSpec(memory_space=pl.ANY)],
        out_specs=[pl.BlockSpec((ts, W_A), lambda i: (i, COL_A)), pl.BlockSpec((3, A_WIDTH), lambda i: (0, 0))],
        out_shape=[jax.ShapeDtypeStruct(dproj.shape, BF16), jax.ShapeDtypeStruct((3, A_WIDTH), F32)],
        input_output_aliases={7: 0}, name="conv_bwd",
        compiler_params=_params())(proj, proj, proj, w_conv, dya, dya, dya, dproj)


def _rope_tables(s):
    half = ROT_DIM // 2
    dim = jnp.arange(LANES) % HEAD_DIM
    inv_freq = jnp.power(jnp.float32(ROPE_THETA), -(dim % half).astype(F32) * (2.0 / ROT_DIM))
    coarse = (jnp.arange(s // LANES) * LANES).astype(F32)[:, None] * inv_freq[None, :]
    fine = jnp.arange(LANES).astype(F32)[:, None] * inv_freq[None, :]
    cos_a, sin_a = jnp.cos(coarse)[:, None, :], jnp.sin(coarse)[:, None, :]
    cos_b, sin_b = jnp.cos(fine)[None], jnp.sin(fine)[None]
    cos = (cos_a * cos_b - sin_a * sin_b).reshape(s, LANES)
    sin = (sin_a * cos_b + cos_a * sin_b).reshape(s, LANES)
    first, second = (dim < half)[None, :], ((dim >= half) & (dim < ROT_DIM))[None, :]
    c = jnp.where(first | second, cos, 1.0)
    s1 = jnp.where(first, -sin, 0.0)
    s2 = jnp.where(second, sin, 0.0)
    return jnp.concatenate([c, s1, s2], axis=1)


def _rope(t, tab):
    return (t * tab[:, :LANES] + pltpu.roll(t, LANES - 8, 1) * tab[:, LANES:2 * LANES]
            + pltpu.roll(t, 8, 1) * tab[:, 2 * LANES:])


def _rope_transpose(dt, tab):
    return (dt * tab[:, :LANES] + pltpu.roll(dt * tab[:, LANES:2 * LANES], 8, 1)
            + pltpu.roll(dt * tab[:, 2 * LANES:], LANES - 8, 1))


def _rope_kv(proj, tab):
    s = proj.shape[0]
    nb = s // KV_PAD

    def body(kv_ref, t_ref, k_ref, v_ref):
        j = pl.program_id(0)
        inside = jnp.where((j > 0) & (j <= nb), 1.0, 0.0)
        kv = kv_ref[...]
        k_ref[...] = (_rope(kv[:, :LANES], t_ref[...]) * inside).astype(BF16)
        v_ref[...] = (kv[:, LANES:] * inside).astype(BF16)

    def src(j):
        return jnp.clip(j - 1, 0, nb - 1)

    o_spec = pl.BlockSpec((KV_PAD, LANES), lambda j: (j, 0))
    shp = jax.ShapeDtypeStruct((s + 2 * KV_PAD, LANES), BF16)
    return pl.pallas_call(
        body, grid=(nb + 2,),
        in_specs=[pl.BlockSpec((KV_PAD, W_KV), lambda j: (src(j), COL_KV)),
                  pl.BlockSpec((KV_PAD, 3 * LANES), lambda j: (src(j), 0))],
        out_specs=[o_spec, o_spec], out_shape=[shp, shp], name="rope_kv",
        compiler_params=_params())(proj, tab)


def _rope_kv_bwd(dkpad, dvpad, tab, dproj):
    s = tab.shape[0]
    nb = s // KV_PAD

    def body(dk_ref, dv_ref, t_ref, _, dp_ref):
        dp_ref[:, :LANES] = _rope_transpose(dk_ref[...], t_ref[...]).astype(BF16)
        dp_ref[:, LANES:] = dv_ref[...].astype(BF16)

    pad_spec = pl.BlockSpec((KV_PAD, LANES), lambda j: (j + 1, 0))
    return pl.pallas_call(
        body, grid=(nb,),
        in_specs=[pad_spec, pad_spec, pl.BlockSpec((KV_PAD, 3 * LANES), lambda j: (j, 0)),
                  pl.BlockSpec(memory_space=pl.ANY)],
        out_specs=pl.BlockSpec((KV_PAD, W_KV), lambda j: (j, COL_KV)),
        out_shape=jax.ShapeDtypeStruct(dproj.shape, BF16), input_output_aliases={3: 0},
        name="rope_kv_bwd", compiler_params=_params())(dkpad, dvpad, tab, dproj)


def _window_start(n):
    return pl.multiple_of((n - 1) * WINDOW_BLOCK + KV_PAD, WINDOW_BLOCK)


def _window_operands(k_ref, v_ref, n, lo):
    start = _window_start(n)
    kw = k_ref[pl.ds(start, 3 * WINDOW_BLOCK), :].astype(F32)
    vw = v_ref[pl.ds(start, 3 * WINDOW_BLOCK), :].astype(F32)
    kr, vr = pltpu.roll(kw, HALF_LANES, 1), pltpu.roll(vw, HALF_LANES, 1)
    k2 = (jnp.where(lo, kw, kr).astype(BF16), jnp.where(lo, kr, kw).astype(BF16))
    v2 = (jnp.where(lo, vw, vr).astype(BF16), jnp.where(lo, vr, vw).astype(BF16))
    return k2, v2


HEADS_PER_GROUP = 4
SWA_FWD_BLOCKS = 1
SWA_BWD_BLOCKS = 2


def _window_bias():
    wb = WINDOW_BLOCK
    qi = lax.broadcasted_iota(jnp.int32, (wb, 3 * wb), 0)
    kj = lax.broadcasted_iota(jnp.int32, (wb, 3 * wb), 1)
    band = (kj >= qi) & (kj <= qi + 2 * wb)
    cases = jnp.stack([band & (kj >= wb), band, band & (kj < 2 * wb)])
    return jnp.where(cases, 0.0, -jnp.inf).astype(F32)


def _block_bias(bias_ref, n, n_blocks):
    case = jnp.where(n == 0, 0, jnp.where(n == n_blocks - 1, 2, 1))
    one = bias_ref[case]
    return jnp.concatenate([one] * HEADS_PER_GROUP, axis=0)


def _stack_heads(pair0, pair1, lo):
    return jnp.concatenate([jnp.where(lo, pair0, 0.0), jnp.where(lo, 0.0, pair0),
                            jnp.where(lo, pair1, 0.0), jnp.where(lo, 0.0, pair1)], axis=0)


def _unstack_pair(stacked, i, lo):
    wb = WINDOW_BLOCK
    return jnp.where(lo, stacked[2 * i * wb:(2 * i + 1) * wb], stacked[(2 * i + 1) * wb:(2 * i + 2) * wb])


def _sink_column(sink_ref, g):
    wb = WINDOW_BLOCK
    return jnp.concatenate([jnp.full((wb, 1), sink_ref[0, HEADS_PER_GROUP * g + i], F32)
                            for i in range(HEADS_PER_GROUP)], axis=0)


def _head_exp(q4, k2g, bias, sink):
    sc = lax.dot_general(q4, k2g, _DIMS["nt"], preferred_element_type=F32) * (HEAD_DIM ** -0.5) + bias
    m = jnp.maximum(jnp.max(sc, axis=1, keepdims=True), sink)
    return jnp.exp(sc - m).astype(BF16), jnp.exp(sink - m)


def _swa_fwd(proj, kpad, vpad, tab, bias, sink, *, carry=None):
    s = proj.shape[0]
    wb = WINDOW_BLOCK

    def body(b_ref, k_ref, v_ref, t_ref, bias_ref, sink_ref, o_ref, y_ref):
        lo = lax.broadcasted_iota(jnp.int32, (wb, LANES), 1) < HALF_LANES
        lo_w = lax.broadcasted_iota(jnp.int32, (3 * wb, LANES), 1) < HALF_LANES
        for sub in range(SWA_FWD_BLOCKS):
            n = pl.program_id(0) * SWA_FWD_BLOCKS + sub
            rows = slice(sub * wb, (sub + 1) * wb)
            k2, v2 = _window_operands(k_ref, v_ref, n, lo_w)
            valid = _block_bias(bias_ref, n, s // wb)
            tab_v = t_ref[rows, :]
            ones = jnp.ones((3 * wb, LANES), BF16)
            for g in range(2):
                qr = [_rope(b_ref[rows, (2 * g + i) * LANES:(2 * g + i + 1) * LANES], tab_v) for i in range(2)]
                q4 = _stack_heads(qr[0], qr[1], lo).astype(BF16)
                e, es = _head_exp(q4, k2[g], valid, _sink_column(sink_ref, g))
                ox = jnp.dot(e, jnp.concatenate([v2[g], ones], axis=1), preferred_element_type=F32)
                o4 = ox[:, :LANES] * (1.0 / (ox[:, LANES:] + es))
                for i in range(2):
                    cols = slice((2 * g + i) * LANES, (2 * g + i + 1) * LANES)
                    op = _unstack_pair(o4, i, lo)
                    o_ref[rows, cols] = op
                    zp = b_ref[rows, A_WIDTH + cols.start:A_WIDTH + cols.stop]
                    y_ref[rows, cols] = (op * (zp * _sigmoid(zp))).astype(BF16)

    tq = SWA_FWD_BLOCKS * wb
    pad_spec = pl.BlockSpec((s + 2 * KV_PAD, LANES), lambda n: (0, 0))
    o_spec = pl.BlockSpec((tq, A_WIDTH), lambda n: (n, 0))
    outs, carried = _carried_call(
        lambda ins, outs, scr: body(*ins, *outs), carry, grid=(s // tq,),
        in_specs=[pl.BlockSpec((tq, W_B), lambda n: (n, COL_B)), pad_spec, pad_spec,
                  pl.BlockSpec((tq, 3 * LANES), lambda n: (n, 0)),
                  pl.BlockSpec(bias.shape, lambda n: (0, 0, 0)), pl.BlockSpec(memory_space=pltpu.SMEM)],
        out_specs=[o_spec, o_spec],
        out_shape=[jax.ShapeDtypeStruct((s, A_WIDTH), F32), jax.ShapeDtypeStruct((s, A_WIDTH), BF16)],
        scratch=[], operands=(proj, kpad, vpad, tab, bias, sink), name="swa_fwd")
    return (*outs, carried) if carry else tuple(outs)


def _swa_bwd(proj, kpad, vpad, tab, bias, sink, o_attn, dyb, dproj):
    s = proj.shape[0]
    wb = WINDOW_BLOCK
    scale = HEAD_DIM ** -0.5

    def body(b_ref, k_ref, v_ref, t_ref, bias_ref, sink_ref, o_ref, dy_ref, _, dp_ref, dk_ref, dv_ref, ds_ref):
        @pl.when(pl.program_id(0) == 0)
        def _():
            dk_ref[...] = jnp.zeros_like(dk_ref)
            dv_ref[...] = jnp.zeros_like(dv_ref)
            ds_ref[...] = jnp.zeros_like(ds_ref)

        lo = lax.broadcasted_iota(jnp.int32, (wb, LANES), 1) < HALF_LANES
        lo_w = lax.broadcasted_iota(jnp.int32, (3 * wb, LANES), 1) < HALF_LANES
        for sub in range(SWA_BWD_BLOCKS):
            n = pl.program_id(0) * SWA_BWD_BLOCKS + sub
            rows = slice(sub * wb, (sub + 1) * wb)
            k2, v2 = _window_operands(k_ref, v_ref, n, lo_w)
            valid = _block_bias(bias_ref, n, s // wb)
            tab_v = t_ref[rows, :]
            ones = jnp.ones((3 * wb, LANES), BF16)
            dks, dvs = [], []
            for g in range(2):
                qr, op, do = [], [], []
                for i in range(2):
                    cols = slice((2 * g + i) * LANES, (2 * g + i + 1) * LANES)
                    zcols = slice(A_WIDTH + cols.start, A_WIDTH + cols.stop)
                    qr.append(_rope(b_ref[rows, cols], tab_v))
                    zp = b_ref[rows, zcols]
                    sg = _sigmoid(zp)
                    op.append(o_ref[rows, cols])
                    dyp = dy_ref[rows, cols]
                    do.append(dyp * (zp * sg))
                    dp_ref[rows, zcols] = (dyp * op[i] * (sg * (1.0 + zp * (1.0 - sg)))).astype(BF16)
                q4 = _stack_heads(qr[0], qr[1], lo).astype(BF16)
                do4 = _stack_heads(do[0], do[1], lo)
                o4 = jnp.concatenate([op[0], op[0], op[1], op[1]], axis=0)
                e, es = _head_exp(q4, k2[g], valid, _sink_column(sink_ref, g))
                inv = 1.0 / (jnp.dot(e, ones, preferred_element_type=F32) + es)
                prob = e.astype(F32) * jnp.concatenate([inv, inv, inv], axis=1)
                delta = jnp.sum(do4 * o4, axis=1, keepdims=True)
                do4b = do4.astype(BF16)
                dprob = lax.dot_general(do4b, v2[g], _DIMS["nt"], preferred_element_type=F32)
                dsc = (prob * (dprob - delta)).astype(BF16)
                sink_terms = (es * inv[:, :1]) * delta
                for i in range(HEADS_PER_GROUP):
                    h = HEADS_PER_GROUP * g + i
                    dsink = -jnp.sum(sink_terms[i * wb:(i + 1) * wb], axis=0, keepdims=True)
                    ds_ref[h:h + 1, :] += jnp.broadcast_to(dsink, (1, LANES))
                dq4 = jnp.dot(dsc, k2[g], preferred_element_type=F32) * scale
                for i in range(2):
                    cols = slice((2 * g + i) * LANES, (2 * g + i + 1) * LANES)
                    dp_ref[rows, cols] = _rope_transpose(_unstack_pair(dq4, i, lo), tab_v).astype(BF16)
                dk2 = lax.dot_general(dsc, q4, _DIMS["tn"], preferred_element_type=F32) * scale
                dv2 = lax.dot_general(prob.astype(BF16), do4b, _DIMS["tn"], preferred_element_type=F32)
                dks.append(dk2 + pltpu.roll(dk2, HALF_LANES, 1))
                dvs.append(dv2 + pltpu.roll(dv2, HALF_LANES, 1))
            start = _window_start(n)
            dk_ref[pl.ds(start, 3 * wb), :] += jnp.where(lo_w, dks[0], dks[1])
            dv_ref[pl.ds(start, 3 * wb), :] += jnp.where(lo_w, dvs[0], dvs[1])

    tq = SWA_BWD_BLOCKS * wb
    pad_spec = pl.BlockSpec((s + 2 * KV_PAD, LANES), lambda n: (0, 0))
    blk = pl.BlockSpec((tq, A_WIDTH), lambda n: (n, 0))
    bsp = pl.BlockSpec((tq, W_B), lambda n: (n, COL_B))
    pad_shape = jax.ShapeDtypeStruct((s + 2 * KV_PAD, LANES), F32)
    return pl.pallas_call(
        body, grid=(s // tq,),
        in_specs=[bsp, pad_spec, pad_spec, pl.BlockSpec((tq, 3 * LANES), lambda n: (n, 0)),
                  pl.BlockSpec(bias.shape, lambda n: (0, 0, 0)), pl.BlockSpec(memory_space=pltpu.SMEM), blk, blk,
                  pl.BlockSpec(memory_space=pl.ANY)],
        out_specs=[bsp, pad_spec, pad_spec, pl.BlockSpec((8, LANES), lambda n: (0, 0))],
        out_shape=[jax.ShapeDtypeStruct(dproj.shape, BF16), pad_shape, pad_shape,
                   jax.ShapeDtypeStruct((8, LANES), F32)],
        input_output_aliases={8: 0}, name="swa_bwd",
        compiler_params=_params())(proj, kpad, vpad, tab, bias, sink, o_attn, dyb, dproj)


def _mem_exp(qh, mk):
    sc = lax.dot_general(qh, mk, _DIMS["nt"], preferred_element_type=F32) * (MEM_HEAD_DIM ** -0.5)
    return jnp.exp(sc - jnp.max(sc, axis=1, keepdims=True)).astype(BF16)


def _mem_fwd(proj, mkv):
    s = proj.shape[0]
    ts = 512
    mlen = mkv.shape[0]

    def body(m_ref, kv_ref, o_ref, y_ref):
        ones = jnp.ones((mlen, LANES), BF16)
        for h in range(MEM_HEADS):
            cols = slice(h * LANES, (h + 1) * LANES)
            mk = kv_ref[:, cols].astype(BF16)
            mv = kv_ref[:, MEM_WIDTH + h * LANES:MEM_WIDTH + (h + 1) * LANES].astype(BF16)
            e = _mem_exp(m_ref[:, cols].astype(BF16), mk)
            ox = jnp.dot(e, jnp.concatenate([mv, ones], axis=1), preferred_element_type=F32)
            oh = ox[:, :LANES] * (1.0 / ox[:, LANES:])
            o_ref[:, cols] = oh
            zh = m_ref[:, MEM_WIDTH + h * LANES:MEM_WIDTH + (h + 1) * LANES]
            y_ref[:, cols] = (oh * (zh * _sigmoid(zh))).astype(BF16)

    o_spec = pl.BlockSpec((ts, MEM_WIDTH), lambda i: (i, 0))
    return pl.pallas_call(
        body, grid=(s // ts,),
        in_specs=[pl.BlockSpec((ts, W_M), lambda i: (i, COL_M)),
                  pl.BlockSpec((mlen, 2 * MEM_WIDTH), lambda i: (0, 0))],
        out_specs=[o_spec, o_spec],
        out_shape=[jax.ShapeDtypeStruct((s, MEM_WIDTH), F32), jax.ShapeDtypeStruct((s, MEM_WIDTH), BF16)],
        name="mem_fwd", compiler_params=_params())(proj, mkv)


def _mem_bwd(proj, mkv, o_mem, dym, dproj, *, carry=None):
    s = proj.shape[0]
    ts = 512
    mlen = mkv.shape[0]
    scale = MEM_HEAD_DIM ** -0.5

    def body(m_ref, kv_ref, o_ref, dy_ref, _, dp_ref, dkv_ref):
        @pl.when(pl.program_id(0) == 0)
        def _():
            dkv_ref[...] = jnp.zeros_like(dkv_ref)

        ones = jnp.ones((mlen, LANES), BF16)
        for h in range(MEM_HEADS):
            cols = slice(h * LANES, (h + 1) * LANES)
            vcols = slice(MEM_WIDTH + h * LANES, MEM_WIDTH + (h + 1) * LANES)
            mk = kv_ref[:, cols].astype(BF16)
            mv = kv_ref[:, vcols].astype(BF16)
            qh = m_ref[:, cols].astype(BF16)
            zh = m_ref[:, vcols]
            sg = _sigmoid(zh)
            oh = o_ref[:, cols]
            dyh = dy_ref[:, cols]
            doh = dyh * (zh * sg)
            dp_ref[:, vcols] = (dyh * oh * (sg * (1.0 + zh * (1.0 - sg)))).astype(BF16)
            e = _mem_exp(qh, mk)
            inv = 1.0 / jnp.dot(e, ones, preferred_element_type=F32)
            prob = e.astype(F32) * jnp.concatenate([inv] * (mlen // LANES), axis=1)
            delta = jnp.sum(doh * oh, axis=1, keepdims=True)
            dohb = doh.astype(BF16)
            dprob = lax.dot_general(dohb, mv, _DIMS["nt"], preferred_element_type=F32)
            dsc = (prob * (dprob - delta)).astype(BF16)
            dp_ref[:, cols] = (jnp.dot(dsc, mk, preferred_element_type=F32) * scale).astype(BF16)
            dkv_ref[:, cols] += lax.dot_general(dsc, qh, _DIMS["tn"], preferred_element_type=F32) * scale
            dkv_ref[:, vcols] += lax.dot_general(prob.astype(BF16), dohb, _DIMS["tn"],
                                                 preferred_element_type=F32)

    blk = pl.BlockSpec((ts, MEM_WIDTH), lambda i: (i, 0))
    msp = pl.BlockSpec((ts, W_M), lambda i: (i, COL_M))
    kvsp = pl.BlockSpec((mlen, 2 * MEM_WIDTH), lambda i: (0, 0))
    outs, carried = _carried_call(
        lambda ins, outs, scr: body(*ins, *outs), carry, grid=(s // ts,),
        in_specs=[msp, kvsp, blk, blk, pl.BlockSpec(memory_space=pl.ANY)],
        out_specs=[msp, kvsp],
        out_shape=[jax.ShapeDtypeStruct(dproj.shape, BF16), jax.ShapeDtypeStruct(mkv.shape, F32)],
        scratch=[], operands=(proj, mkv, o_mem, dym, dproj), name="mem_bwd", aliases={4: 0})
    return (*outs, carried) if carry else tuple(outs)


def _forward_backward(x, mem, tgt, proj, w_conv, sink, g_mem, late_weights, g_post, early_exchange, kv_exchange):
    s = x.shape[0]
    tab = _rope_tables(s)
    bias = _window_bias()

    ya = _conv_fwd(proj, w_conv)
    kpad, vpad = _rope_kv(proj, tab)
    o_attn, yb, *arrived = _swa_fwd(proj, kpad, vpad, tab, bias, sink, carry=late_weights[0])
    w_kv, w_up, w_out = late_weights[1](arrived[0] if arrived else None)
    mn = _rmsnorm_fwd(mem, g_mem, name="mem_norm")
    mkv = _matmul(mn, w_kv, mode="nn", out_dtype=F32, tm=256, tn=1024, tk=D_MODEL, name="mem_kv")
    o_mem, ym = _mem_fwd(proj, mkv)
    merged, d_out, dy, dg_post, loss = _mid_fwd(ya, yb, ym, proj, x, tgt, w_up, w_out, g_post)
    dproj, d_ya, d_yb, d_ym, dw_up, dw_out = _mid_bwd(d_out, merged, ya, yb, ym, proj, w_up, w_out)

    dproj, dw_conv = _conv_bwd(proj, w_conv, d_ya, dproj)
    dproj, dkpad, dvpad, dsink = _swa_bwd(proj, kpad, vpad, tab, bias, sink, o_attn, d_yb, dproj)
    dproj = _rope_kv_bwd(dkpad, dvpad, tab, dproj)
    dproj, d_mkv, *early = _mem_bwd(proj, mkv, o_mem, d_ym, dproj, carry=early_exchange(dw_up, dw_out))

    dw_kv = _matmul(mn, d_mkv, mode="tn", out_dtype=F32, tm=1024, tn=1024, tk=256, name="dw_kv")
    d_mn = _matmul(d_mkv, w_kv, mode="nt", out_dtype=F32, tm=256, tn=1024, tk=D_MODEL, name="d_mn")
    _, dg_mem, *early_kv = _rmsnorm_bwd(d_mn, mem, g_mem, d_mn, name="mem_norm_bwd", carry=kv_exchange(dw_kv))

    return dict(loss=loss, dproj=dproj, dy=dy, w_conv=dw_conv, sink=dsink, g_mem=dg_mem,
                w_kv=dw_kv, w_up=dw_up, w_out=dw_out, g_post=dg_post, early=early[0] if early else None,
                early_kv=early_kv[0] if early_kv else None)


N_DEV = 8


def _position():
    return lax.axis_index("x"), lax.axis_index("y"), lax.axis_index("c")


def _other_chips(x, y):
    return (((1 - x, y), 2 * (1 - x) + y), ((x, 1 - y), 2 * x + (1 - y)), ((1 - x, 1 - y), 2 * (1 - x) + (1 - y)))


def _remote(src, dst, send_sems, recv_sems, k, device):
    return pltpu.make_async_remote_copy(src_ref=src, dst_ref=dst, send_sem=send_sems.at[k], recv_sem=recv_sems.at[k],
                                        device_id=device, device_id_type=MESH)


def _rows_half(ref, hf):
    rh = ref.shape[0] // 2
    return ref.at[pl.ds(pl.multiple_of(hf * rh, 8), rh)]


def _gather_weights(shards, small=None, relations=(0, 1, 2), into=None):
    n = len(shards)
    k = 0 if small is None else 1

    def peers(x, y):
        return [(r, chip, idx) for r, (chip, idx) in enumerate(_other_chips(x, y)) if r in relations]

    def ici(ins, outs, sems, a, r, chip, src_chip, c):
        return _remote(_rows_half(ins[a], c), _rows_half(outs[a].at[src_chip], c), sems[0], sems[1], 3 * a + r,
                       (*chip, c))

    def whole(ins, outs, sems, r, chip, src_chip, c):
        return _remote(ins[n], outs[n].at[src_chip], sems[0], sems[1], 3 * n + r, (*chip, c))

    def d2d(outs, sems, a, r, idx, hf, x, y, c):
        half = _rows_half(outs[a].at[idx], hf)
        return _remote(half, half, sems[2], sems[3], 3 * a + r, (x, y, 1 - c))

    def start(ins, outs, sems):
        x, y, c = _position()
        me = 2 * x + y
        for a in range(n):
            for r, chip, _ in peers(x, y):
                ici(ins, outs, sems, a, r, chip, me, c).start()
        for r, (chip, _) in enumerate(_other_chips(x, y)):
            if k:
                whole(ins, outs, sems, r, chip, me, c).start()

    def finish(ins, outs, sems):
        x, y, c = _position()
        me = 2 * x + y
        for a in range(n):
            for r, chip, idx in peers(x, y):
                ici(ins, outs, sems, a, r, chip, idx, c).wait_recv()
                d2d(outs, sems, a, r, idx, c, x, y, c).start()
        for a in range(n):
            for r, chip, idx in peers(x, y):
                d2d(outs, sems, a, r, idx, 1 - c, x, y, c).wait_recv()
        for r, (chip, idx) in enumerate(_other_chips(x, y)):
            if k:
                whole(ins, outs, sems, r, chip, idx, c).wait_recv()
                whole(ins, outs, sems, r, chip, me, c).wait_send()
        for a in range(n):
            for r, chip, idx in peers(x, y):
                ici(ins, outs, sems, a, r, chip, me, c).wait_send()
                d2d(outs, sems, a, r, idx, c, x, y, c).wait_send()

    operands = list(shards) + ([small] if k else [])
    shapes = [jax.ShapeDtypeStruct((N_CHIPS,) + s.shape, s.dtype) for s in operands]
    aliases = {}
    if into is not None:
        assert len(into) == len(operands)
        aliases = {len(operands) + a: a for a in range(len(into))}
        operands += list(into)
    return _Carry(operands, shapes,
                  [pltpu.SemaphoreType.DMA((3 * (n + k),)), pltpu.SemaphoreType.DMA((3 * (n + k),)),
                   pltpu.SemaphoreType.DMA((3 * n,)), pltpu.SemaphoreType.DMA((3 * n,))], start, finish, aliases)


def _pair_exchange(send):
    n = len(send)

    def copies(ins, outs, sems):
        x, y, c = _position()
        return [_remote(ins[a], outs[a], sems[0], sems[1], a, (x, y, 1 - c)) for a in range(n)]

    def start(ins, outs, sems):
        for cp in copies(ins, outs, sems):
            cp.start()

    def finish(ins, outs, sems):
        for cp in copies(ins, outs, sems):
            cp.wait()

    return _Carry(send, [jax.ShapeDtypeStruct(p.shape, p.dtype) for p in send],
                  [pltpu.SemaphoreType.DMA((n,)), pltpu.SemaphoreType.DMA((n,))], start, finish)


def _chip_exchange(sums):
    n = len(sums)

    def copies(ins, outs, sems):
        x, y, c = _position()
        return [_remote(ins[a].at[idx], outs[a].at[r], sems[0], sems[1], 3 * a + r, (*chip, c))
                for a in range(n) for r, (chip, idx) in enumerate(_other_chips(x, y))]

    def start(ins, outs, sems):
        for cp in copies(ins, outs, sems):
            cp.start()

    def finish(ins, outs, sems):
        for cp in copies(ins, outs, sems):
            cp.wait()

    return _Carry(sums, [jax.ShapeDtypeStruct((3,) + p.shape[1:], p.dtype) for p in sums],
                  [pltpu.SemaphoreType.DMA((3 * n,)), pltpu.SemaphoreType.DMA((3 * n,))], start, finish)


def _pair_share(pairs):
    n = len(pairs)

    def start(ins, outs, sems):
        x, y, c = _position()
        for a in range(n):
            _remote(outs[a].at[c], outs[a].at[c], sems[0], sems[1], a, (x, y, 1 - c)).start()

    def finish(ins, outs, sems):
        x, y, c = _position()
        for a in range(n):
            _remote(outs[a].at[1 - c], outs[a].at[1 - c], sems[0], sems[1], a, (x, y, 1 - c)).wait_recv()
        for a in range(n):
            _remote(outs[a].at[c], outs[a].at[c], sems[0], sems[1], a, (x, y, 1 - c)).wait_send()

    return _Carry(pairs, [jax.ShapeDtypeStruct(p.shape, p.dtype) for p in pairs],
                  [pltpu.SemaphoreType.DMA((n,)), pltpu.SemaphoreType.DMA((n,))], start, finish,
                  aliases={a: a for a in range(n)})


def _small_allreduce(pack, share):
    rows, width = pack.shape
    n_share = len(share.ins)

    def body(p_ref, *refs):
        share_in, o_ref, share_out = refs[:n_share], refs[n_share], refs[n_share + 1:2 * n_share + 1]
        buf, send_sems, recv_sems = refs[2 * n_share + 1:2 * n_share + 4]
        share_sems = refs[2 * n_share + 4:]
        share.start(share_in, share_out, share_sems)
        x, y, c = _position()
        me = 4 * x + 2 * y + c
        buf[me] = p_ref[...]
        peers = []
        for r in range(1, N_DEV):
            fx, fy, fc = (r >> 2) & 1, (r >> 1) & 1, r & 1
            px, py, pc = (1 - x if fx else x), (1 - y if fy else y), (1 - c if fc else c)
            peers.append(((px, py, pc), 4 * px + 2 * py + pc))
        sends = [_remote(p_ref, buf.at[me], send_sems, recv_sems, r, dev) for r, (dev, _) in enumerate(peers)]
        for cp in sends:
            cp.start()
        for r, (dev, idx) in enumerate(peers):
            _remote(p_ref, buf.at[idx], send_sems, recv_sems, r, dev).wait_recv()
        for cp in sends:
            cp.wait_send()
        acc = buf[0]
        for k in range(1, N_DEV):
            acc = acc + buf[k]
        o_ref[...] = acc
        share.finish(share_in, share_out, share_sems)

    vm = pl.BlockSpec(memory_space=pltpu.VMEM)
    red, *shared = pl.pallas_call(
        body, in_specs=[vm] + [_HBM] * n_share, out_specs=[vm] + [_HBM] * n_share,
        out_shape=[jax.ShapeDtypeStruct(pack.shape, F32)] + share.out_shapes,
        scratch_shapes=[pltpu.VMEM((N_DEV, rows, width), F32), pltpu.SemaphoreType.DMA((N_DEV - 1,)),
                        pltpu.SemaphoreType.DMA((N_DEV - 1,))] + share.sems,
        input_output_aliases={1 + i: 1 + o for i, o in share.aliases.items()},
        name="small_allreduce")(pack, *share.ins)
    return red, shared


ROW_TILE_MAX = 512
SUM_TILE_MAX = 2048
BF16_SUBLANES = 16


def _row_tile(rows, most=ROW_TILE_MAX):
    if rows <= most:
        return rows
    return max(t for t in range(BF16_SUBLANES, most + 1, BF16_SUBLANES) if rows % t == 0)


def _pair_add(keep, recv, name):
    nj, rh, cols = keep.shape
    tr = _row_tile(rh, SUM_TILE_MAX)

    def body(k_ref, r_ref, o_ref):
        o_ref[...] = (k_ref[...].astype(F32) + r_ref[...].astype(F32)).astype(BF16)

    blk = pl.BlockSpec((None, tr, cols), lambda j, i: (j, i, 0))
    return pl.pallas_call(body, grid=(nj, rh // tr), in_specs=[blk, blk], out_specs=blk,
                          out_shape=jax.ShapeDtypeStruct(keep.shape, BF16), name=name,
                          compiler_params=_params())(keep, recv)


def _chip_add(sums, recv, where, name):
    _, rh, cols = sums.shape
    tr = _row_tile(rh, SUM_TILE_MAX)

    def body(w_ref, s_ref, r_ref, o_ref):
        o_ref[...] = ((s_ref[...].astype(F32) + r_ref[0].astype(F32)) + r_ref[1].astype(F32)) + r_ref[2].astype(F32)

    grid_spec = pltpu.PrefetchScalarGridSpec(
        num_scalar_prefetch=1, grid=(rh // tr,),
        in_specs=[pl.BlockSpec((None, tr, cols), lambda i, w_ref: (w_ref[0], i, 0)),
                  pl.BlockSpec((3, tr, cols), lambda i, w_ref: (0, i, 0))],
        out_specs=pl.BlockSpec((None, tr, cols), lambda i, w_ref: (w_ref[1], i, 0)))
    return pl.pallas_call(body, grid_spec=grid_spec, out_shape=jax.ShapeDtypeStruct((2, rh, cols), F32),
                          name=name, compiler_params=_params())(where, sums, recv)


def _adamw(w, g, m, v, name):
    rows, cols = w.shape
    tr = _row_tile(rows)
    assert rows % tr == 0

    def body(w_ref, g_ref, m_ref, v_ref, d_ref, mo_ref, vo_ref):
        gv = g_ref[...]
        m_new = ADAM_B1 * m_ref[...] + (1.0 - ADAM_B1) * gv
        v_new = ADAM_B2 * v_ref[...] + (1.0 - ADAM_B2) * jnp.square(gv)
        m_hat = m_new / (1.0 - ADAM_B1 ** ADAM_STEP)
        v_hat = v_new / (1.0 - ADAM_B2 ** ADAM_STEP)
        d_ref[...] = -ADAM_LR * (m_hat / (jnp.sqrt(v_hat) + ADAM_EPS) + ADAM_WD * w_ref[...])
        mo_ref[...] = m_new
        vo_ref[...] = v_new

    blk = pl.BlockSpec((tr, cols), lambda i: (i, 0))
    shp = jax.ShapeDtypeStruct((rows, cols), F32)
    return pl.pallas_call(body, grid=(rows // tr,), in_specs=[blk] * 4, out_specs=[blk] * 3,
                          out_shape=[shp] * 3, name=name, compiler_params=_params())(w, g, m, v)


def _adamw_halves(w, g2, m, v, name):
    rows, cols = w.shape
    half = cols // 2
    tr = _row_tile(rows)

    def body(w_ref, g_ref, m_ref, v_ref, go_ref, d_ref, mo_ref, vo_ref):
        gv = g_ref[...]
        go_ref[...] = gv
        m_new = ADAM_B1 * m_ref[...] + (1.0 - ADAM_B1) * gv
        v_new = ADAM_B2 * v_ref[...] + (1.0 - ADAM_B2) * jnp.square(gv)
        m_hat = m_new / (1.0 - ADAM_B1 ** ADAM_STEP)
        v_hat = v_new / (1.0 - ADAM_B2 ** ADAM_STEP)
        d_ref[...] = -ADAM_LR * (m_hat / (jnp.sqrt(v_hat) + ADAM_EPS) + ADAM_WD * w_ref[...])
        mo_ref[...] = m_new
        vo_ref[...] = v_new

    blk = pl.BlockSpec((tr, half), lambda hf, i: (i, hf))
    gsp = pl.BlockSpec((None, tr, half), lambda hf, i: (hf, i, 0))
    shp = jax.ShapeDtypeStruct((rows, cols), F32)
    return pl.pallas_call(body, grid=(2, rows // tr), in_specs=[blk, gsp, blk, blk], out_specs=[blk] * 4,
                          out_shape=[shp] * 4, name=name, compiler_params=_params())(w, g2, m, v)


SHARD_W = IN_WIDTH // N_CHIPS


def _half_major(a):
    r, c = a.shape
    return a.reshape(N_CHIPS, 2, r // N_CHIPS // 2, c).transpose(1, 0, 2, 3)


def kernel(x, mem, g_pre, w_in, w_conv, attn_sink, g_mem, w_mem_kv, w_up_a, w_up_b, w_up_m, w_out, g_post, loss_target, m_g_pre, m_w_in, m_w_conv, m_attn_sink, m_g_mem, m_w_mem_kv, m_w_up_a, m_w_up_b, m_w_up_m, m_w_out, m_g_post, v_g_pre, v_w_in, v_w_conv, v_attn_sink, v_g_mem, v_w_mem_kv, v_w_up_a, v_w_up_b, v_w_up_m, v_w_out, v_g_post):
    xi, yi, ci = _position()
    chip = 2 * xi + yi
    where = jnp.stack([chip, ci, N_CHIPS - 1 - chip]).astype(jnp.int32)

    own = [w_in[0].T.astype(BF16), w_mem_kv[0].astype(BF16),
           jnp.concatenate([w_up_a[0], w_up_b[0], w_up_m[0]], axis=0).astype(BF16), w_out[0].astype(BF16)]
    own_conv = jnp.pad(w_conv[0], ((0, 5), (0, 0)))

    def pieces(mine, got):
        got = lax.dynamic_update_slice_in_dim(got, mine[None], chip, axis=0)
        return [got[j] for j in range(N_CHIPS)]

    diag = N_CHIPS - 1 - chip
    proj, h, h_t, got_near, got_conv, got_far = _proj_near(x[0], g_pre, own[0], own_conv, where)
    w_near = lax.dynamic_update_slice_in_dim(got_near, own[0][None], chip, axis=0).reshape(IN_WIDTH, D_MODEL)
    far = lax.dynamic_index_in_dim(got_far, diag, 0, keepdims=False)
    proj = _proj_far(h, w_near, far, where, into=proj)
    w_conv_full = jnp.concatenate([p[:3] for p in pieces(own_conv, got_conv)], axis=1)

    def late_weights(gathered):
        w_kv_full = jnp.concatenate(pieces(own[1], gathered[0]), axis=0)
        up_pieces = pieces(own[2], gathered[1])
        w_up_full = jnp.stack([jnp.concatenate([p[k * A_WIDTH:(k + 1) * A_WIDTH] for p in up_pieces], axis=1)
                               for k in range(3)])
        return w_kv_full, w_up_full, jnp.concatenate(pieces(own[3], gathered[2]), axis=0)

    def pick(parts, hf):
        return [lax.dynamic_index_in_dim(p, hf, 0, keepdims=False) for p in parts]

    def up_out_parts(dw_up, dw_out):
        up = (dw_up.reshape(3, A_WIDTH, N_CHIPS, D_MODEL // N_CHIPS).transpose(2, 0, 1, 3)
              .reshape(N_CHIPS, 2, 3 * A_WIDTH // 2, D_MODEL // N_CHIPS).transpose(1, 0, 2, 3))
        return [up.astype(BF16), _half_major(dw_out).astype(BF16)]

    g = _forward_backward(x[0], mem[0], loss_target[0], proj, w_conv_full, attn_sink, g_mem,
                          (_gather_weights(own[1:]), late_weights), g_post,
                          lambda dw_up, dw_out: _pair_exchange(pick(up_out_parts(dw_up, dw_out), 1 - ci)),
                          lambda dw_kv: _pair_exchange(pick([_half_major(dw_kv).astype(BF16)], 1 - ci)))

    half_rows = D_MODEL // 2

    def dw_in_half(half_of, name, carry):
        dw, carried = _dw_in_t(g["dproj"], h_t, half_of=half_of, where=where, name=name, carry=carry)
        return dw.reshape(N_CHIPS, SHARD_W, half_rows), carried

    small_keep = pick([_half_major(g["w_kv"]).astype(BF16)] + up_out_parts(g["w_up"], g["w_out"]), ci)
    small_names = ["w_kv", "w_up", "w_out"]
    sums_small = [_pair_add(k, r, "pair_add_" + nm)
                  for k, r, nm in zip(small_keep, g["early_kv"] + g["early"], small_names)]
    dw_send, recv3_small = dw_in_half(lambda w: 1 - w[1], "dw_in_send", _chip_exchange(sums_small))
    dw_keep, (recv_in,) = dw_in_half(lambda w: w[1], "dw_in_keep", _pair_exchange([dw_send]))
    sum_in = _pair_add(dw_keep, recv_in, "pair_add_w_in")
    (grad_x, dg_pre), (recv3_in,) = _d_h(g["dproj"], w_near, far, where, x[0], g_pre, g["dy"],
                                         carry=_chip_exchange([sum_in]))
    pairs = [_chip_add(s, r, where, "chip_add_" + nm)
             for s, r, nm in zip([sum_in] + sums_small, [recv3_in] + recv3_small, ["w_in"] + small_names)]

    zeros512 = jnp.zeros((1, D_MODEL - A_WIDTH), F32)
    conv_rows = [jnp.concatenate([g["w_conv"][k:k + 1], zeros512], axis=1) for k in range(3)]
    sink_row = jnp.pad(g["sink"][:, 0].reshape(1, N_Q_HEADS), ((0, 0), (0, D_MODEL - N_Q_HEADS)))
    loss_row = jnp.pad(g["loss"], ((0, 0), (0, D_MODEL - LANES)))
    pack = jnp.concatenate([dg_pre, g["g_mem"], g["g_post"]] + conv_rows + [sink_row, loss_row], axis=0)
    red, full = _small_allreduce(pack, _pair_share(pairs))
    loss = red[7, 0]
    small_grads = dict(
        g_pre=red[0:1], g_mem=red[1:2], g_post=red[2:3], attn_sink=red[6:7, :N_Q_HEADS],
        w_conv=lax.dynamic_slice(red[3:6, :A_WIDTH], (0, chip * LANES), (3, LANES)))

    gw_up = full[2].reshape(3, A_WIDTH, D_MODEL // N_CHIPS)
    grads = dict(small_grads, w_mem_kv=full[1].reshape(D_MODEL // N_CHIPS, 2 * MEM_WIDTH),
                 w_up_a=gw_up[0], w_up_b=gw_up[1], w_up_m=gw_up[2],
                 w_out=full[3].reshape(D_MODEL // N_CHIPS, D_MODEL))

    weights = dict(g_pre=g_pre, w_in=w_in, w_conv=w_conv, attn_sink=attn_sink, g_mem=g_mem, w_mem_kv=w_mem_kv,
                   w_up_a=w_up_a, w_up_b=w_up_b, w_up_m=w_up_m, w_out=w_out, g_post=g_post)
    m_in = dict(g_pre=m_g_pre, w_in=m_w_in, w_conv=m_w_conv, attn_sink=m_attn_sink, g_mem=m_g_mem,
                w_mem_kv=m_w_mem_kv, w_up_a=m_w_up_a, w_up_b=m_w_up_b, w_up_m=m_w_up_m, w_out=m_w_out,
                g_post=m_g_post)
    v_in = dict(g_pre=v_g_pre, w_in=v_w_in, w_conv=v_w_conv, attn_sink=v_attn_sink, g_mem=v_g_mem,
                w_mem_kv=v_w_mem_kv, w_up_a=v_w_up_a, w_up_b=v_w_up_b, w_up_m=v_w_up_m, w_out=v_w_out,
                g_post=v_g_post)
    out_g, out_d, out_m, out_v = [], [], [], []
    for nm in ("g_pre", "w_in", "w_conv", "attn_sink", "g_mem", "w_mem_kv", "w_up_a", "w_up_b", "w_up_m", "w_out",
               "g_post"):
        shape = weights[nm].shape
        if nm == "w_in":
            results = _adamw_halves(w_in[0].T, full[0], m_w_in[0].T, v_w_in[0].T, "adamw_w_in")
            for out, t in zip((out_g, out_d, out_m, out_v), results):
                out.append(t.T.reshape(shape))
            continue
        two_d = shape[-2:]
        gr = grads[nm].reshape(two_d)
        d, m_new, v_new = _adamw(weights[nm].reshape(two_d), gr, m_in[nm].reshape(two_d), v_in[nm].reshape(two_d),
                                 "adamw_" + nm)
        out_g.append(gr.reshape(shape))
        out_d.append(d.reshape(shape))
        out_m.append(m_new.reshape(shape))
        out_v.append(v_new.reshape(shape))
    return (loss, grad_x.reshape(x.shape), *out_g, *out_d, *out_m, *out_v)
```

```python
import jax
import jax.numpy as jnp
from jax import lax
from jax.experimental import pallas as pl
from jax.experimental.pallas import tpu as pltpu

F32 = jnp.float32
BF16 = jnp.bfloat16
MESH = pl.DeviceIdType.MESH

D_MODEL = 1024
EPS = 1e-6
A_WIDTH = 512
HEAD_DIM = 64
N_Q_HEADS = 8
WINDOW_BLOCK = 128
KV_PAD = 512
ROPE_THETA = 500000.0
ROT_DIM = 16
MEM_HEADS = 4
MEM_HEAD_DIM = 128
MEM_WIDTH = 512
IN_WIDTH = 7424
N_CHIPS = 4
LANES = 128
HALF_LANES = 64

PERM_SEGS = ((0, 2560), (2816, 3328), (4352, 7424), (3328, 4352), (2560, 2816))
COL_A, W_A = 0, 2048
COL_B, W_B = 2, 1024
COL_G, W_G = 1, 3072
COL_M, W_M = 6, 1024
COL_KV, W_KV = 28, 256

ADAM_LR = 0.001
ADAM_B1 = 0.9
ADAM_B2 = 0.999
ADAM_EPS = 1e-08
ADAM_WD = 0.01
ADAM_STEP = 10

VMEM_LIGHT_BYTES = 48 * 1024 * 1024
VMEM_HEAVY_BYTES = 48 * 1024 * 1024


_HBM = pl.BlockSpec(memory_space=pltpu.HBM)


def _params(heavy=False):
    return pltpu.CompilerParams(vmem_limit_bytes=VMEM_HEAVY_BYTES if heavy else VMEM_LIGHT_BYTES)


def _sigmoid(v):
    return jax.nn.sigmoid(v)


_DIMS = {"nn": (((1,), (0,)), ((), ())), "nt": (((1,), (1,)), ((), ())), "tn": (((0,), (0,)), ((), ()))}


class _Carry:
    def __init__(self, ins, out_shapes, sems, start, finish, aliases=None):
        self.ins, self.out_shapes, self.sems = list(ins), list(out_shapes), list(sems)
        self.start, self.finish, self.aliases = start, finish, dict(aliases or {})


def _join(*carries):
    def split(seq, counts):
        pos, parts = 0, []
        for n in counts:
            parts.append(seq[pos:pos + n])
            pos += n
        return parts

    n_in = [len(c.ins) for c in carries]
    n_out = [len(c.out_shapes) for c in carries]
    n_sem = [len(c.sems) for c in carries]

    def run(which):
        def go(ins, outs, sems):
            for c, i, o, sm in zip(carries, split(ins, n_in), split(outs, n_out), split(sems, n_sem)):
                getattr(c, which)(i, o, sm)
        return go

    aliases = {}
    for k, c in enumerate(carries):
        aliases.update({sum(n_in[:k]) + i: sum(n_out[:k]) + o for i, o in c.aliases.items()})
    return _Carry([a for c in carries for a in c.ins], [sh for c in carries for sh in c.out_shapes],
                  [sm for c in carries for sm in c.sems], run("start"), run("finish"), aliases)


def _carried_call(body, carry, *, grid, in_specs, out_specs, out_shape, scratch, operands, name, prefetch=None,
                  aliases=None, heavy=False):
    n_in, n_out, n_scr = len(in_specs), len(out_specs), len(scratch)
    c_in = len(carry.ins) if carry else 0
    c_out = len(carry.out_shapes) if carry else 0
    n_pre = 0 if prefetch is None else 1
    steps = 1
    for g in grid:
        steps *= g

    def wrapped(*refs):
        refs = refs[n_pre:]
        ins, cins = refs[:n_in], refs[n_in:n_in + c_in]
        outs = refs[n_in + c_in:n_in + c_in + n_out]
        couts = refs[n_in + c_in + n_out:n_in + c_in + n_out + c_out]
        rest = refs[n_in + c_in + n_out + c_out:]
        scr, sems = rest[:n_scr], rest[n_scr:]
        if carry:
            step = pl.program_id(0)
            for ax in range(1, len(grid)):
                step = step * grid[ax] + pl.program_id(ax)

            @pl.when(step == 0)
            def _():
                carry.start(cins, couts, sems)

        body(ins, outs, scr)
        if carry:
            @pl.when(step == steps - 1)
            def _():
                carry.finish(cins, couts, sems)

    all_aliases = {n_pre + i: o for i, o in (aliases or {}).items()}
    if carry:
        all_aliases.update({n_pre + n_in + i: n_out + o for i, o in carry.aliases.items()})
    all_in = list(in_specs) + [_HBM] * c_in
    all_out = list(out_specs) + [_HBM] * c_out
    all_scratch = list(scratch) + (carry.sems if carry else [])
    if n_pre:
        spec = dict(grid_spec=pltpu.PrefetchScalarGridSpec(num_scalar_prefetch=1, grid=grid, in_specs=all_in,
                                                           out_specs=all_out, scratch_shapes=all_scratch))
        pre = (prefetch,)
    else:
        spec = dict(grid=grid, in_specs=all_in, out_specs=all_out, scratch_shapes=all_scratch)
        pre = ()
    results = pl.pallas_call(
        wrapped, out_shape=list(out_shape) + (carry.out_shapes if carry else []), input_output_aliases=all_aliases,
        name=name, compiler_params=_params(heavy), **spec)(*pre, *operands, *(carry.ins if carry else []))
    return list(results[:n_out]), list(results[n_out:])


def _matmul(a, b, *, mode, out_dtype, tm, tn, tk, name):
    if mode == "nn":
        (m, k), (_, n) = a.shape, b.shape
    elif mode == "nt":
        (m, k), (n, _) = a.shape, b.shape
    else:
        (k, m), (_, n) = a.shape, b.shape
    tm, tn, tk = min(tm, m), min(tn, n), min(tk, k)
    assert m % tm == 0 and n % tn == 0 and k % tk == 0
    nk = k // tk
    dims = _DIMS[mode]

    if mode == "nn":
        a_spec = pl.BlockSpec((tm, tk), lambda i, j, kk: (i, kk))
        b_spec = pl.BlockSpec((tk, tn), lambda i, j, kk: (kk, j))
    elif mode == "nt":
        a_spec = pl.BlockSpec((tm, tk), lambda i, j, kk: (i, kk))
        b_spec = pl.BlockSpec((tn, tk), lambda i, j, kk: (j, kk))
    else:
        a_spec = pl.BlockSpec((tk, tm), lambda i, j, kk: (kk, i))
        b_spec = pl.BlockSpec((tk, tn), lambda i, j, kk: (kk, j))
    o_spec = pl.BlockSpec((tm, tn), lambda i, j, kk: (i, j))

    def part(a_ref, b_ref):
        return lax.dot_general(a_ref[...].astype(BF16), b_ref[...].astype(BF16), dims,
                               preferred_element_type=F32)

    if nk == 1:
        def body(a_ref, b_ref, o_ref):
            o_ref[...] = part(a_ref, b_ref).astype(out_dtype)
        scratch = []
    else:
        def body(a_ref, b_ref, o_ref, acc_ref):
            kk = pl.program_id(2)

            @pl.when(kk == 0)
            def _():
                acc_ref[...] = part(a_ref, b_ref)

            @pl.when(kk > 0)
            def _():
                acc_ref[...] += part(a_ref, b_ref)

            @pl.when(kk == nk - 1)
            def _():
                o_ref[...] = acc_ref[...].astype(out_dtype)
        scratch = [pltpu.VMEM((tm, tn), F32)]

    return pl.pallas_call(
        body, grid=(m // tm, n // tn, nk), in_specs=[a_spec, b_spec], out_specs=o_spec,
        out_shape=jax.ShapeDtypeStruct((m, n), out_dtype), scratch_shapes=scratch,
        name=name, compiler_params=_params())(a, b)


IN_BLOCK = 256
N_IN_BLOCKS = IN_WIDTH // IN_BLOCK
SHARD_BLOCKS = (IN_WIDTH // N_CHIPS) // IN_BLOCK
BLOCK_RUNS = tuple((a // IN_BLOCK, sum(d - c for c, d in PERM_SEGS[:k]) // IN_BLOCK, (b - a) // IN_BLOCK)
                   for k, (a, b) in enumerate(PERM_SEGS))


def _perm_block(r):
    p = r
    for ref0, perm0, n in BLOCK_RUNS:
        p = jnp.where((r >= ref0) & (r < ref0 + n), r - ref0 + perm0, p)
    return p


def _proj_near(x, g_pre, own_w, small, where):
    s, d = x.shape
    norm_tile = min(512, s)
    n_own = SHARD_BLOCKS - 1
    n_diag = SHARD_BLOCKS + 1
    n_blocks = N_IN_BLOCKS - n_diag
    piece = IN_WIDTH // N_CHIPS - SHARD_BLOCKS * IN_BLOCK
    near = _gather_weights([own_w], small, relations=(0, 1))
    far = _gather_weights([own_w], relations=(2,))
    both = _join(near, far)
    n_cin, n_cout = len(both.ins), len(both.out_shapes)

    def block_of(i, w):
        me, dg = w[0], w[2]
        own0 = SHARD_BLOCKS * me + jnp.minimum(me, 1)
        dg0 = SHARD_BLOCKS * dg
        lo0, hi0 = jnp.minimum(own0, dg0), jnp.maximum(own0, dg0)
        lo_n = jnp.where(own0 < dg0, n_own, n_diag)
        hi_n = jnp.where(own0 < dg0, n_diag, n_own)
        r = i - n_own
        r = r + lo_n * (r >= lo0).astype(jnp.int32)
        r = r + hi_n * (r >= hi0).astype(jnp.int32)
        return jnp.where(i < n_own, own0 + i, r)

    def body(w_ref, x_hbm, g_ref, own_hbm, *refs):
        cins, (o_ref, h_hbm, ht_hbm) = refs[:n_cin], refs[n_cin:n_cin + 3]
        couts = refs[n_cin + 3:n_cin + 3 + n_cout]
        blocks, block_sems, h_ref, x_tile, ht_tile, io_sem = refs[n_cin + 3 + n_cout:n_cin + 9 + n_cout]
        sems = refs[n_cin + 9 + n_cout:]
        near_refs = (cins[:len(near.ins)], couts[:len(near.out_shapes)], sems[:len(near.sems)])
        far_refs = (cins[len(near.ins):], couts[len(near.out_shapes):], sems[len(near.sems):])
        gathered = couts[0]
        i = pl.program_id(0)
        me = w_ref[0]

        def fetch(step, slot):
            r = block_of(step, w_ref)
            for p in range(IN_BLOCK // piece):
                row = r * IN_BLOCK + p * piece
                j = row // (IN_WIDTH // N_CHIPS)
                off = pl.multiple_of(row - j * (IN_WIDTH // N_CHIPS), BF16_SUBLANES)
                dst = blocks.at[slot, pl.ds(p * piece, piece)]

                @pl.when(j == me)
                def _():
                    pltpu.make_async_copy(own_hbm.at[pl.ds(off, piece)], dst, block_sems.at[slot]).start()

                @pl.when(j != me)
                def _():
                    pltpu.make_async_copy(gathered.at[j, pl.ds(off, piece)], dst, block_sems.at[slot]).start()

        def arrived(slot):
            pltpu.make_async_copy(own_hbm.at[pl.ds(0, IN_BLOCK)], blocks.at[slot], block_sems.at[slot]).wait()

        slot = i % 2

        def norm_rows(k):
            rows = pl.ds(k * norm_tile, norm_tile)
            pltpu.sync_copy(x_hbm.at[rows], x_tile)
            xv = x_tile[...]
            hv = (xv * lax.rsqrt(jnp.mean(xv * xv, axis=-1, keepdims=True) + EPS)) * g_ref[...]
            h_ref[rows, :] = hv.astype(BF16)
            ht_tile[...] = hv.T.astype(BF16)
            to_h = pltpu.make_async_copy(h_ref.at[rows], h_hbm.at[rows], io_sem.at[0])
            to_ht = pltpu.make_async_copy(ht_tile, ht_hbm.at[:, rows], io_sem.at[1])
            to_h.start()
            to_ht.start()
            to_h.wait()
            to_ht.wait()

        @pl.when(i == 0)
        def _():
            near.start(*near_refs)
            fetch(i, slot)
            for k in range(s // norm_tile):
                norm_rows(k)

        @pl.when(i == n_own)
        def _():
            near.finish(*near_refs)
            far.start(*far_refs)
            fetch(i, slot)

        arrived(slot)

        @pl.when((i + 1 < n_blocks) & (i + 1 != n_own))
        def _():
            fetch(i + 1, 1 - slot)

        o_ref[...] = lax.dot_general(h_ref[...], blocks[slot], _DIMS["nt"], preferred_element_type=F32)

        @pl.when(i == n_blocks - 1)
        def _():
            far.finish(*far_refs)

    anysp = pl.BlockSpec(memory_space=pl.ANY)
    grid_spec = pltpu.PrefetchScalarGridSpec(
        num_scalar_prefetch=1, grid=(n_blocks,),
        in_specs=[anysp, pl.BlockSpec((1, d), lambda i, w: (0, 0)), anysp] + [_HBM] * n_cin,
        out_specs=[pl.BlockSpec((s, IN_BLOCK), lambda i, w: (0, _perm_block(block_of(i, w)))), anysp, anysp]
        + [_HBM] * n_cout,
        scratch_shapes=[pltpu.VMEM((2, IN_BLOCK, d), BF16), pltpu.SemaphoreType.DMA((2,)), pltpu.VMEM((s, d), BF16),
                        pltpu.VMEM((norm_tile, d), F32), pltpu.VMEM((d, norm_tile), BF16),
                        pltpu.SemaphoreType.DMA((2,))] + both.sems)
    return pl.pallas_call(
        body, grid_spec=grid_spec,
        out_shape=[jax.ShapeDtypeStruct((s, IN_WIDTH), F32), jax.ShapeDtypeStruct((s, d), BF16),
                   jax.ShapeDtypeStruct((d, s), BF16)] + both.out_shapes,
        name="proj_near", compiler_params=_params())(where, x, g_pre, own_w, *both.ins)


def _proj_far(h, w_near, far, where, *, into, carry=None):
    s, d = h.shape
    n_blocks = SHARD_BLOCKS + 1
    lead = IN_WIDTH // N_CHIPS - SHARD_BLOCKS * IN_BLOCK

    def body(ins, outs, scr):
        where_ref, h_ref, w_hbm, far_hbm, _ = ins
        win, sem = scr
        i = pl.program_id(0)

        @pl.when(i == 0)
        def _():
            dg = where_ref[2]
            rows = pl.ds(pl.multiple_of(dg * (SHARD_BLOCKS * IN_BLOCK), IN_BLOCK), n_blocks * IN_BLOCK)
            window = pltpu.make_async_copy(w_hbm.at[rows], win, sem)
            window.start()
            window.wait()
            shard = pltpu.make_async_copy(far_hbm, win.at[pl.ds(pl.multiple_of(dg * lead, BF16_SUBLANES), SHARD_W)], sem)
            shard.start()
            shard.wait()

        blk = win[pl.ds(pl.multiple_of(i * IN_BLOCK, IN_BLOCK), IN_BLOCK), :]
        outs[0][...] = lax.dot_general(h_ref[...], blk, _DIMS["nt"], preferred_element_type=F32)

    anysp = pl.BlockSpec(memory_space=pl.ANY)
    (proj,), carried = _carried_call(
        body, carry, grid=(n_blocks,),
        in_specs=[pl.BlockSpec(memory_space=pltpu.SMEM), pl.BlockSpec((s, d), lambda i, w: (0, 0)), anysp, anysp, anysp],
        out_specs=[pl.BlockSpec((s, IN_BLOCK), lambda i, w: (0, _perm_block(i + SHARD_BLOCKS * w[2])))],
        out_shape=[jax.ShapeDtypeStruct((s, IN_WIDTH), F32)],
        scratch=[pltpu.VMEM((n_blocks * IN_BLOCK, d), BF16), pltpu.SemaphoreType.DMA],
        operands=(where, h, w_near, far, into), name="proj_far", prefetch=where, aliases={4: 0})
    return (proj, carried) if carry else proj


def _dw_in_t(dproj, h_t, *, half_of, where, name, carry=None):
    d, s = h_t.shape
    c = d // 2

    def body(ins, outs, scr):
        outs[0][...] = lax.dot_general(ins[1][...], ins[0][...], _DIMS["nn"], preferred_element_type=F32).T.astype(BF16)

    (dw,), carried = _carried_call(
        body, carry, grid=(N_IN_BLOCKS,),
        in_specs=[pl.BlockSpec((s, IN_BLOCK), lambda r, w: (0, _perm_block(r))),
                  pl.BlockSpec((c, s), lambda r, w: (half_of(w), 0))],
        out_specs=[pl.BlockSpec((IN_BLOCK, c), lambda r, w: (r, 0))],
        out_shape=[jax.ShapeDtypeStruct((IN_WIDTH, c), BF16)], scratch=[], operands=(dproj, h_t), name=name,
        prefetch=where)
    return (dw, carried) if carry else dw


def _norm_bwd_tile(dhv, xv, gv, resv):
    r = lax.rsqrt(jnp.mean(xv * xv, axis=-1, keepdims=True) + EPS)
    xh = xv * r
    dxh = dhv * gv
    dx = resv + r * (dxh - xh * jnp.mean(dxh * xh, axis=-1, keepdims=True))
    return dx, jnp.sum(dhv * xh, axis=0, keepdims=True)


def _d_h(dproj, w_near, far, where, x, g, res, *, carry=None):
    s = dproj.shape[0]
    d = w_near.shape[1]
    tm = min(s, 256)
    n = s // tm
    assert n % 2 == 0

    def body(ins, outs, scr):
        where_ref, a_ref, w_hbm, far_hbm, x_ref, g_ref, res_ref = ins
        dx_ref, dg_ref = outs
        w_ref, sem, dh_even, dh_odd = scr
        i = pl.program_id(0)

        def norm_bwd(dh_ref):
            dx, part = _norm_bwd_tile(dh_ref[...], x_ref[...], g_ref[...], res_ref[...])
            dx_ref[...] = dx
            dg_ref[...] += part

        def matmul(dh_ref):
            acc = None
            for ref0, perm0, nb in BLOCK_RUNS:
                term = jnp.dot(a_ref[:, perm0 * IN_BLOCK:(perm0 + nb) * IN_BLOCK],
                               w_ref[ref0 * IN_BLOCK:(ref0 + nb) * IN_BLOCK, :], preferred_element_type=F32)
                acc = term if acc is None else acc + term
            dh_ref[...] = acc

        @pl.when(i == 0)
        def _():
            whole = pltpu.make_async_copy(w_hbm, w_ref, sem)
            whole.start()
            whole.wait()
            rows = pl.ds(pl.multiple_of(where_ref[2] * SHARD_W, BF16_SUBLANES), SHARD_W)
            part = pltpu.make_async_copy(far_hbm, w_ref.at[rows], sem)
            part.start()
            part.wait()
            dg_ref[...] = jnp.zeros_like(dg_ref)
            matmul(dh_even)

        @pl.when((i % 2 == 0) & (i > 0) & (i < n))
        def _():
            norm_bwd(dh_odd)
            matmul(dh_even)

        @pl.when(i % 2 == 1)
        def _():
            norm_bwd(dh_even)
            matmul(dh_odd)

        @pl.when(i == n)
        def _():
            norm_bwd(dh_odd)

    anysp = pl.BlockSpec(memory_space=pl.ANY)
    before = pl.BlockSpec((tm, d), lambda i: (jnp.maximum(i - 1, 0), 0))
    vec = pl.BlockSpec((1, d), lambda i: (0, 0))
    outs, carried = _carried_call(
        body, carry, grid=(n + 1,),
        in_specs=[pl.BlockSpec(memory_space=pltpu.SMEM),
                  pl.BlockSpec((tm, IN_WIDTH), lambda i: (jnp.minimum(i, n - 1), 0)), anysp, anysp, before, vec, before],
        out_specs=[before, vec],
        out_shape=[jax.ShapeDtypeStruct((s, d), F32), jax.ShapeDtypeStruct((1, d), F32)],
        scratch=[pltpu.VMEM((IN_WIDTH, d), BF16), pltpu.SemaphoreType.DMA, pltpu.VMEM((tm, d), F32),
                 pltpu.VMEM((tm, d), F32)],
        operands=(where, dproj, w_near, far, x, g, res), name="d_h", heavy=True)
    return (outs, carried) if carry else outs


def _rmsnorm_fwd(x, g, *, name):
    s, d = x.shape
    ts = min(512, s)

    def body(x_ref, g_ref, o_ref):
        xv = x_ref[...]
        r = lax.rsqrt(jnp.mean(xv * xv, axis=-1, keepdims=True) + EPS)
        o_ref[...] = ((xv * r) * g_ref[...]).astype(BF16)

    return pl.pallas_call(
        body, grid=(s // ts,),
        in_specs=[pl.BlockSpec((ts, d), lambda i: (i, 0)), pl.BlockSpec((1, d), lambda i: (0, 0))],
        out_specs=pl.BlockSpec((ts, d), lambda i: (i, 0)),
        out_shape=jax.ShapeDtypeStruct((s, d), BF16), name=name, compiler_params=_params())(x, g)


def _rmsnorm_bwd(dh, x, g, res, *, name, carry=None):
    s, d = x.shape
    ts = min(256, s)

    def body(ins, outs, scr):
        dh_ref, x_ref, g_ref, res_ref = ins
        dx_ref, dg_ref = outs
        dx, part = _norm_bwd_tile(dh_ref[...], x_ref[...], g_ref[...], res_ref[...])

        @pl.when(pl.program_id(0) == 0)
        def _():
            dg_ref[...] = part

        @pl.when(pl.program_id(0) > 0)
        def _():
            dg_ref[...] += part

        dx_ref[...] = dx

    row = pl.BlockSpec((ts, d), lambda i: (i, 0))
    vec = pl.BlockSpec((1, d), lambda i: (0, 0))
    outs, carried = _carried_call(
        body, carry, grid=(s // ts,), in_specs=[row, row, vec, row], out_specs=[row, vec],
        out_shape=[jax.ShapeDtypeStruct((s, d), F32), jax.ShapeDtypeStruct((1, d), F32)],
        scratch=[], operands=(dh, x, g, res), name=name)
    return (*outs, carried) if carry else tuple(outs)


MID_TILE = 256


def _gated_branches(y_refs, wup_ref, gl):
    d = D_MODEL
    us = [jnp.dot(y_refs[k][...], wup_ref[k], preferred_element_type=F32) for k in range(3)]
    sg = [_sigmoid(gl[:, k * d:(k + 1) * d]) for k in range(3)]
    return us, sg


def _mid_fwd(ya, yb, ym, proj, x, tgt, w_up, w_out, g_post):
    s, d = x.shape
    ts = MID_TILE

    def body(ya_ref, yb_ref, ym_ref, g_ref, x_ref, t_ref, wup_ref, wout_ref, gp_ref,
             m_ref, do_ref, dy_ref, dg_ref, loss_ref):
        us, sg = _gated_branches((ya_ref, yb_ref, ym_ref), wup_ref, g_ref[...])
        merged = (sg[0] * us[0] + sg[1] * us[1] + sg[2] * us[2]).astype(BF16)
        m_ref[...] = merged
        ov = jnp.dot(merged, wout_ref[...], preferred_element_type=F32)
        r = lax.rsqrt(jnp.mean(ov * ov, axis=-1, keepdims=True) + EPS)
        nh = ov * r
        gv = gp_ref[...]
        e = (x_ref[...] + nh * gv) - t_ref[...]
        lpart = 0.5 * jnp.sum(jnp.mean(e * e, axis=-1, keepdims=True), axis=0, keepdims=True)
        dy = e * (1.0 / d)
        dgp = jnp.sum(dy * nh, axis=0, keepdims=True)

        @pl.when(pl.program_id(0) == 0)
        def _():
            dg_ref[...] = dgp
            loss_ref[...] = jnp.broadcast_to(lpart, loss_ref.shape)

        @pl.when(pl.program_id(0) > 0)
        def _():
            dg_ref[...] += dgp
            loss_ref[...] += jnp.broadcast_to(lpart, loss_ref.shape)

        dn = dy * gv
        dy_ref[...] = dy
        do_ref[...] = (r * (dn - nh * jnp.mean(dn * nh, axis=-1, keepdims=True))).astype(BF16)

    row = pl.BlockSpec((ts, d), lambda i: (i, 0))
    ysp = pl.BlockSpec((ts, A_WIDTH), lambda i: (i, 0))
    vec = pl.BlockSpec((1, d), lambda i: (0, 0))
    return pl.pallas_call(
        body, grid=(s // ts,),
        in_specs=[ysp, ysp, ysp, pl.BlockSpec((ts, W_G), lambda i: (i, COL_G)), row, row,
                  pl.BlockSpec((3, A_WIDTH, d), lambda i: (0, 0, 0)), pl.BlockSpec((d, d), lambda i: (0, 0)), vec],
        out_specs=[row, row, row, vec, pl.BlockSpec((1, LANES), lambda i: (0, 0))],
        out_shape=[jax.ShapeDtypeStruct((s, d), BF16), jax.ShapeDtypeStruct((s, d), BF16),
                   jax.ShapeDtypeStruct((s, d), F32), jax.ShapeDtypeStruct((1, d), F32),
                   jax.ShapeDtypeStruct((1, LANES), F32)],
        name="mid_fwd", compiler_params=_params(heavy=True))(ya, yb, ym, proj, x, tgt, w_up, w_out, g_post)


def _mid_bwd(d_out, merged, ya, yb, ym, proj, w_up, w_out):
    s, d = merged.shape
    ts = MID_TILE
    last = s // ts - 1

    def body(do_ref, m_ref, ya_ref, yb_ref, ym_ref, g_ref, wup_ref, wout_ref,
             dp_ref, dya_ref, dyb_ref, dym_ref, dwup_hbm, dwout_hbm, dwup_acc, dwout_acc, dwup_cast, dwout_cast):
        i = pl.program_id(0)

        @pl.when(i == 0)
        def _():
            dwup_acc[...] = jnp.zeros_like(dwup_acc)
            dwout_acc[...] = jnp.zeros_like(dwout_acc)

        y_refs = (ya_ref, yb_ref, ym_ref)
        us, sg = _gated_branches(y_refs, wup_ref, g_ref[...])
        dov = do_ref[...]
        dwout_acc[...] += lax.dot_general(m_ref[...], dov, _DIMS["tn"], preferred_element_type=F32)
        dm = lax.dot_general(dov, wout_ref[...], _DIMS["nt"], preferred_element_type=F32)
        for k, dy_ref in enumerate((dya_ref, dyb_ref, dym_ref)):
            dp_ref[:, k * d:(k + 1) * d] = ((dm * us[k]) * (sg[k] * (1.0 - sg[k]))).astype(BF16)
            du = (sg[k] * dm).astype(BF16)
            dy_ref[...] = lax.dot_general(du, wup_ref[k], _DIMS["nt"], preferred_element_type=F32)
            dwup_acc[k] += lax.dot_general(y_refs[k][...], du, _DIMS["tn"], preferred_element_type=F32)

        @pl.when(i == last)
        def _():
            dwup_cast[...] = dwup_acc[...].astype(BF16)
            dwout_cast[...] = dwout_acc[...].astype(BF16)
            pltpu.sync_copy(dwup_cast, dwup_hbm)
            pltpu.sync_copy(dwout_cast, dwout_hbm)

    row = pl.BlockSpec((ts, d), lambda i: (i, 0))
    ysp = pl.BlockSpec((ts, A_WIDTH), lambda i: (i, 0))
    gsp = pl.BlockSpec((ts, W_G), lambda i: (i, COL_G))
    anysp = pl.BlockSpec(memory_space=pl.ANY)
    yshape = jax.ShapeDtypeStruct((s, A_WIDTH), F32)
    return pl.pallas_call(
        body, grid=(s // ts,),
        in_specs=[row, row, ysp, ysp, ysp, gsp, pl.BlockSpec((3, A_WIDTH, d), lambda i: (0, 0, 0)),
                  pl.BlockSpec((d, d), lambda i: (0, 0))],
        out_specs=[gsp, ysp, ysp, ysp, anysp, anysp],
        out_shape=[jax.ShapeDtypeStruct((s, IN_WIDTH), BF16), yshape, yshape, yshape,
                   jax.ShapeDtypeStruct((3, A_WIDTH, d), BF16), jax.ShapeDtypeStruct((d, d), BF16)],
        scratch_shapes=[pltpu.VMEM((3, A_WIDTH, d), F32), pltpu.VMEM((d, d), F32),
                        pltpu.VMEM((3, A_WIDTH, d), BF16), pltpu.VMEM((d, d), BF16)],
        name="mid_bwd", compiler_params=_params(heavy=True))(d_out, merged, ya, yb, ym, proj, w_up, w_out)


def _conv_core(blk, prev, nxt, w, i, last, ts):
    c = A_WIDTH
    ab, ac, ax, az = blk[:, :c], blk[:, c:2 * c], blk[:, 2 * c:3 * c], blk[:, 3 * c:]
    cu = ac * ax
    cu_prev = (prev[7:8, c:2 * c] * prev[7:8, 2 * c:3 * c]) * jnp.where(i > 0, 1.0, 0.0)
    cu_next = (nxt[0:1, c:2 * c] * nxt[0:1, 2 * c:3 * c]) * jnp.where(i < last, 1.0, 0.0)
    row = lax.broadcasted_iota(jnp.int32, (ts, c), 0)
    cm1 = jnp.where(row == 0, cu_prev, pltpu.roll(cu, 1, 0))
    cp1 = jnp.where(row == ts - 1, cu_next, pltpu.roll(cu, ts - 1, 0))
    yc = cm1 * w[0:1] + cu * w[1:2] + cp1 * w[2:3]
    return ab, ac, ax, az, cu, cm1, cp1, yc, row


def _halo_specs(ts, width, col, nblk8):
    prev = pl.BlockSpec((8, width), lambda i: (jnp.maximum(i * (ts // 8) - 1, 0), col))
    nxt = pl.BlockSpec((8, width), lambda i: (jnp.minimum((i + 1) * (ts // 8), nblk8 - 1), col))
    return prev, nxt


def _conv_fwd(proj, w_conv):
    s = proj.shape[0]
    ts = 256
    last = s // ts - 1

    def body(a_ref, ap_ref, an_ref, w_ref, ya_ref):
        i = pl.program_id(0)
        ab, _, _, az, _, _, _, yc, _ = _conv_core(a_ref[...], ap_ref[...], an_ref[...], w_ref[...], i, last, ts)
        ya_ref[...] = ((ab * yc) * (az * _sigmoid(az))).astype(BF16)

    prev, nxt = _halo_specs(ts, W_A, COL_A, s // 8)
    return pl.pallas_call(
        body, grid=(s // ts,),
        in_specs=[pl.BlockSpec((ts, W_A), lambda i: (i, COL_A)), prev, nxt,
                  pl.BlockSpec((3, A_WIDTH), lambda i: (0, 0))],
        out_specs=pl.BlockSpec((ts, A_WIDTH), lambda i: (i, 0)),
        out_shape=jax.ShapeDtypeStruct((s, A_WIDTH), BF16), name="conv_fwd",
        compiler_params=_params())(proj, proj, proj, w_conv)


def _conv_bwd(proj, w_conv, dya, dproj):
    s = proj.shape[0]
    ts = 256
    last = s // ts - 1
    c = A_WIDTH

    def body(a_ref, ap_ref, an_ref, w_ref, d_ref, dp_ref, dn_ref, _, dproj_ref, dw_ref):
        i = pl.program_id(0)
        w = w_ref[...]
        prev, nxt = ap_ref[...], an_ref[...]
        ab, ac, ax, az, cu, cm1, cp1, yc, row = _conv_core(a_ref[...], prev, nxt, w, i, last, ts)
        sg = _sigmoid(az)
        sz = az * sg
        dya_v = d_ref[...]
        dyc = dya_v * sz * ab
        dproj_ref[:, :c] = (dya_v * sz * yc).astype(BF16)
        dproj_ref[:, 3 * c:] = (dya_v * (ab * yc) * (sg * (1.0 + az * (1.0 - sg)))).astype(BF16)

        def halo_dyc(a_row, d_row):
            azr = a_row[:, 3 * c:]
            return d_row * (azr * _sigmoid(azr)) * a_row[:, :c]

        dyc_prev = halo_dyc(prev[7:8], dp_ref[...][7:8]) * jnp.where(i > 0, 1.0, 0.0)
        dyc_next = halo_dyc(nxt[0:1], dn_ref[...][0:1]) * jnp.where(i < last, 1.0, 0.0)
        dyc_m1 = jnp.where(row == 0, dyc_prev, pltpu.roll(dyc, 1, 0))
        dyc_p1 = jnp.where(row == ts - 1, dyc_next, pltpu.roll(dyc, ts - 1, 0))
        dcu = dyc_p1 * w[0:1] + dyc * w[1:2] + dyc_m1 * w[2:3]
        dproj_ref[:, c:2 * c] = (dcu * ax).astype(BF16)
        dproj_ref[:, 2 * c:3 * c] = (dcu * ac).astype(BF16)
        dw = [jnp.sum(dyc * t, axis=0, keepdims=True) for t in (cm1, cu, cp1)]

        @pl.when(i == 0)
        def _():
            for k in range(3):
                dw_ref[k:k + 1, :] = dw[k]

        @pl.when(i > 0)
        def _():
            for k in range(3):
                dw_ref[k:k + 1, :] += dw[k]

    prev, nxt = _halo_specs(ts, W_A, COL_A, s // 8)
    dprev, dnxt = _halo_specs(ts, A_WIDTH, 0, s // 8)
    return pl.pallas_call(
        body, grid=(s // ts,),
        in_specs=[pl.BlockSpec((ts, W_A), lambda i: (i, COL_A)), prev, nxt,
                  pl.BlockSpec((3, A_WIDTH), lambda i: (0, 0)),
                  pl.BlockSpec((ts, A_WIDTH), lambda i: (i, 0)), dprev, dnxt,
                  pl.BlockSpec(memory_space=pl.ANY)],
        out_specs=[pl.BlockSpec((ts, W_A), lambda i: (i, COL_A)), pl.BlockSpec((3, A_WIDTH), lambda i: (0, 0))],
        out_shape=[jax.ShapeDtypeStruct(dproj.shape, BF16), jax.ShapeDtypeStruct((3, A_WIDTH), F32)],
        input_output_aliases={7: 0}, name="conv_bwd",
        compiler_params=_params())(proj, proj, proj, w_conv, dya, dya, dya, dproj)


def _rope_tables(s):
    half = ROT_DIM // 2
    dim = jnp.arange(LANES) % HEAD_DIM
    inv_freq = jnp.power(jnp.float32(ROPE_THETA), -(dim % half).astype(F32) * (2.0 / ROT_DIM))
    coarse = (jnp.arange(s // LANES) * LANES).astype(F32)[:, None] * inv_freq[None, :]
    fine = jnp.arange(LANES).astype(F32)[:, None] * inv_freq[None, :]
    cos_a, sin_a = jnp.cos(coarse)[:, None, :], jnp.sin(coarse)[:, None, :]
    cos_b, sin_b = jnp.cos(fine)[None], jnp.sin(fine)[None]
    cos = (cos_a * cos_b - sin_a * sin_b).reshape(s, LANES)
    sin = (sin_a * cos_b + cos_a * sin_b).reshape(s, LANES)
    first, second = (dim < half)[None, :], ((dim >= half) & (dim < ROT_DIM))[None, :]
    c = jnp.where(first | second, cos, 1.0)
    s1 = jnp.where(first, -sin, 0.0)
    s2 = jnp.where(second, sin, 0.0)
    return jnp.concatenate([c, s1, s2], axis=1)


def _rope(t, tab):
    return (t * tab[:, :LANES] + pltpu.roll(t, LANES - 8, 1) * tab[:, LANES:2 * LANES]
            + pltpu.roll(t, 8, 1) * tab[:, 2 * LANES:])


def _rope_transpose(dt, tab):
    return (dt * tab[:, :LANES] + pltpu.roll(dt * tab[:, LANES:2 * LANES], 8, 1)
            + pltpu.roll(dt * tab[:, 2 * LANES:], LANES - 8, 1))


def _rope_kv(proj, tab):
    s = proj.shape[0]
    nb = s // KV_PAD

    def body(kv_ref, t_ref, k_ref, v_ref):
        j = pl.program_id(0)
        inside = jnp.where((j > 0) & (j <= nb), 1.0, 0.0)
        kv = kv_ref[...]
        k_ref[...] = (_rope(kv[:, :LANES], t_ref[...]) * inside).astype(BF16)
        v_ref[...] = (kv[:, LANES:] * inside).astype(BF16)

    def src(j):
        return jnp.clip(j - 1, 0, nb - 1)

    o_spec = pl.BlockSpec((KV_PAD, LANES), lambda j: (j, 0))
    shp = jax.ShapeDtypeStruct((s + 2 * KV_PAD, LANES), BF16)
    return pl.pallas_call(
        body, grid=(nb + 2,),
        in_specs=[pl.BlockSpec((KV_PAD, W_KV), lambda j: (src(j), COL_KV)),
                  pl.BlockSpec((KV_PAD, 3 * LANES), lambda j: (src(j), 0))],
        out_specs=[o_spec, o_spec], out_shape=[shp, shp], name="rope_kv",
        compiler_params=_params())(proj, tab)


def _rope_kv_bwd(dkpad, dvpad, tab, dproj):
    s = tab.shape[0]
    nb = s // KV_PAD

    def body(dk_ref, dv_ref, t_ref, _, dp_ref):
        dp_ref[:, :LANES] = _rope_transpose(dk_ref[...], t_ref[...]).astype(BF16)
        dp_ref[:, LANES:] = dv_ref[...].astype(BF16)

    pad_spec = pl.BlockSpec((KV_PAD, LANES), lambda j: (j + 1, 0))
    return pl.pallas_call(
        body, grid=(nb,),
        in_specs=[pad_spec, pad_spec, pl.BlockSpec((KV_PAD, 3 * LANES), lambda j: (j, 0)),
                  pl.BlockSpec(memory_space=pl.ANY)],
        out_specs=pl.BlockSpec((KV_PAD, W_KV), lambda j: (j, COL_KV)),
        out_shape=jax.ShapeDtypeStruct(dproj.shape, BF16), input_output_aliases={3: 0},
        name="rope_kv_bwd", compiler_params=_params())(dkpad, dvpad, tab, dproj)


def _window_start(n):
    return pl.multiple_of((n - 1) * WINDOW_BLOCK + KV_PAD, WINDOW_BLOCK)


def _window_operands(k_ref, v_ref, n, lo):
    start = _window_start(n)
    kw = k_ref[pl.ds(start, 3 * WINDOW_BLOCK), :].astype(F32)
    vw = v_ref[pl.ds(start, 3 * WINDOW_BLOCK), :].astype(F32)
    kr, vr = pltpu.roll(kw, HALF_LANES, 1), pltpu.roll(vw, HALF_LANES, 1)
    k2 = (jnp.where(lo, kw, kr).astype(BF16), jnp.where(lo, kr, kw).astype(BF16))
    v2 = (jnp.where(lo, vw, vr).astype(BF16), jnp.where(lo, vr, vw).astype(BF16))
    return k2, v2


HEADS_PER_GROUP = 4
SWA_FWD_BLOCKS = 1
SWA_BWD_BLOCKS = 2


def _window_bias():
    wb = WINDOW_BLOCK
    qi = lax.broadcasted_iota(jnp.int32, (wb, 3 * wb), 0)
    kj = lax.broadcasted_iota(jnp.int32, (wb, 3 * wb), 1)
    band = (kj >= qi) & (kj <= qi + 2 * wb)
    cases = jnp.stack([band & (kj >= wb), band, band & (kj < 2 * wb)])
    return jnp.where(cases, 0.0, -jnp.inf).astype(F32)


def _block_bias(bias_ref, n, n_blocks):
    case = jnp.where(n == 0, 0, jnp.where(n == n_blocks - 1, 2, 1))
    one = bias_ref[case]
    return jnp.concatenate([one] * HEADS_PER_GROUP, axis=0)


def _stack_heads(pair0, pair1, lo):
    return jnp.concatenate([jnp.where(lo, pair0, 0.0), jnp.where(lo, 0.0, pair0),
                            jnp.where(lo, pair1, 0.0), jnp.where(lo, 0.0, pair1)], axis=0)


def _unstack_pair(stacked, i, lo):
    wb = WINDOW_BLOCK
    return jnp.where(lo, stacked[2 * i * wb:(2 * i + 1) * wb], stacked[(2 * i + 1) * wb:(2 * i + 2) * wb])


def _sink_column(sink_ref, g):
    wb = WINDOW_BLOCK
    return jnp.concatenate([jnp.full((wb, 1), sink_ref[0, HEADS_PER_GROUP * g + i], F32)
                            for i in range(HEADS_PER_GROUP)], axis=0)


def _head_exp(q4, k2g, bias, sink):
    sc = lax.dot_general(q4, k2g, _DIMS["nt"], preferred_element_type=F32) * (HEAD_DIM ** -0.5) + bias
    m = jnp.maximum(jnp.max(sc, axis=1, keepdims=True), sink)
    return jnp.exp(sc - m).astype(BF16), jnp.exp(sink - m)


def _swa_fwd(proj, kpad, vpad, tab, bias, sink, *, carry=None):
    s = proj.shape[0]
    wb = WINDOW_BLOCK

    def body(b_ref, k_ref, v_ref, t_ref, bias_ref, sink_ref, o_ref, y_ref):
        lo = lax.broadcasted_iota(jnp.int32, (wb, LANES), 1) < HALF_LANES
        lo_w = lax.broadcasted_iota(jnp.int32, (3 * wb, LANES), 1) < HALF_LANES
        for sub in range(SWA_FWD_BLOCKS):
            n = pl.program_id(0) * SWA_FWD_BLOCKS + sub
            rows = slice(sub * wb, (sub + 1) * wb)
            k2, v2 = _window_operands(k_ref, v_ref, n, lo_w)
            valid = _block_bias(bias_ref, n, s // wb)
            tab_v = t_ref[rows, :]
            ones = jnp.ones((3 * wb, LANES), BF16)
            for g in range(2):
                qr = [_rope(b_ref[rows, (2 * g + i) * LANES:(2 * g + i + 1) * LANES], tab_v) for i in range(2)]
                q4 = _stack_heads(qr[0], qr[1], lo).astype(BF16)
                e, es = _head_exp(q4, k2[g], valid, _sink_column(sink_ref, g))
                ox = jnp.dot(e, jnp.concatenate([v2[g], ones], axis=1), preferred_element_type=F32)
                o4 = ox[:, :LANES] * (1.0 / (ox[:, LANES:] + es))
                for i in range(2):
                    cols = slice((2 * g + i) * LANES, (2 * g + i + 1) * LANES)
                    op = _unstack_pair(o4, i, lo)
                    o_ref[rows, cols] = op
                    zp = b_ref[rows, A_WIDTH + cols.start:A_WIDTH + cols.stop]
                    y_ref[rows, cols] = (op * (zp * _sigmoid(zp))).astype(BF16)

    tq = SWA_FWD_BLOCKS * wb
    pad_spec = pl.BlockSpec((s + 2 * KV_PAD, LANES), lambda n: (0, 0))
    o_spec = pl.BlockSpec((tq, A_WIDTH), lambda n: (n, 0))
    outs, carried = _carried_call(
        lambda ins, outs, scr: body(*ins, *outs), carry, grid=(s // tq,),
        in_specs=[pl.BlockSpec((tq, W_B), lambda n: (n, COL_B)), pad_spec, pad_spec,
                  pl.BlockSpec((tq, 3 * LANES), lambda n: (n, 0)),
                  pl.BlockSpec(bias.shape, lambda n: (0, 0, 0)), pl.BlockSpec(memory_space=pltpu.SMEM)],
        out_specs=[o_spec, o_spec],
        out_shape=[jax.ShapeDtypeStruct((s, A_WIDTH), F32), jax.ShapeDtypeStruct((s, A_WIDTH), BF16)],
        scratch=[], operands=(proj, kpad, vpad, tab, bias, sink), name="swa_fwd")
    return (*outs, carried) if carry else tuple(outs)


def _swa_bwd(proj, kpad, vpad, tab, bias, sink, o_attn, dyb, dproj):
    s = proj.shape[0]
    wb = WINDOW_BLOCK
    scale = HEAD_DIM ** -0.5

    def body(b_ref, k_ref, v_ref, t_ref, bias_ref, sink_ref, o_ref, dy_ref, _, dp_ref, dk_ref, dv_ref, ds_ref):
        @pl.when(pl.program_id(0) == 0)
        def _():
            dk_ref[...] = jnp.zeros_like(dk_ref)
            dv_ref[...] = jnp.zeros_like(dv_ref)
            ds_ref[...] = jnp.zeros_like(ds_ref)

        lo = lax.broadcasted_iota(jnp.int32, (wb, LANES), 1) < HALF_LANES
        lo_w = lax.broadcasted_iota(jnp.int32, (3 * wb, LANES), 1) < HALF_LANES
        for sub in range(SWA_BWD_BLOCKS):
            n = pl.program_id(0) * SWA_BWD_BLOCKS + sub
            rows = slice(sub * wb, (sub + 1) * wb)
            k2, v2 = _window_operands(k_ref, v_ref, n, lo_w)
            valid = _block_bias(bias_ref, n, s // wb)
            tab_v = t_ref[rows, :]
            ones = jnp.ones((3 * wb, LANES), BF16)
            dks, dvs = [], []
            for g in range(2):
                qr, op, do = [], [], []
                for i in range(2):
                    cols = slice((2 * g + i) * LANES, (2 * g + i + 1) * LANES)
                    zcols = slice(A_WIDTH + cols.start, A_WIDTH + cols.stop)
                    qr.append(_rope(b_ref[rows, cols], tab_v))
                    zp = b_ref[rows, zcols]
                    sg = _sigmoid(zp)
                    op.append(o_ref[rows, cols])
                    dyp = dy_ref[rows, cols]
                    do.append(dyp * (zp * sg))
                    dp_ref[rows, zcols] = (dyp * op[i] * (sg * (1.0 + zp * (1.0 - sg)))).astype(BF16)
                q4 = _stack_heads(qr[0], qr[1], lo).astype(BF16)
                do4 = _stack_heads(do[0], do[1], lo)
                o4 = jnp.concatenate([op[0], op[0], op[1], op[1]], axis=0)
                e, es = _head_exp(q4, k2[g], valid, _sink_column(sink_ref, g))
                inv = 1.0 / (jnp.dot(e, ones, preferred_element_type=F32) + es)
                prob = e.astype(F32) * jnp.concatenate([inv, inv, inv], axis=1)
                delta = jnp.sum(do4 * o4, axis=1, keepdims=True)
                do4b = do4.astype(BF16)
                dprob = lax.dot_general(do4b, v2[g], _DIMS["nt"], preferred_element_type=F32)
                dsc = (prob * (dprob - delta)).astype(BF16)
                sink_terms = (es * inv[:, :1]) * delta
                for i in range(HEADS_PER_GROUP):
                    h = HEADS_PER_GROUP * g + i
                    dsink = -jnp.sum(sink_terms[i * wb:(i + 1) * wb], axis=0, keepdims=True)
                    ds_ref[h:h + 1, :] += jnp.broadcast_to(dsink, (1, LANES))
                dq4 = jnp.dot(dsc, k2[g], preferred_element_type=F32) * scale
                for i in range(2):
                    cols = slice((2 * g + i) * LANES, (2 * g + i + 1) * LANES)
                    dp_ref[rows, cols] = _rope_transpose(_unstack_pair(dq4, i, lo), tab_v).astype(BF16)
                dk2 = lax.dot_general(dsc, q4, _DIMS["tn"], preferred_element_type=F32) * scale
                dv2 = lax.dot_general(prob.astype(BF16), do4b, _DIMS["tn"], preferred_element_type=F32)
                dks.append(dk2 + pltpu.roll(dk2, HALF_LANES, 1))
                dvs.append(dv2 + pltpu.roll(dv2, HALF_LANES, 1))
            start = _window_start(n)
            dk_ref[pl.ds(start, 3 * wb), :] += jnp.where(lo_w, dks[0], dks[1])
            dv_ref[pl.ds(start, 3 * wb), :] += jnp.where(lo_w, dvs[0], dvs[1])

    tq = SWA_BWD_BLOCKS * wb
    pad_spec = pl.BlockSpec((s + 2 * KV_PAD, LANES), lambda n: (0, 0))
    blk = pl.BlockSpec((tq, A_WIDTH), lambda n: (n, 0))
    bsp = pl.BlockSpec((tq, W_B), lambda n: (n, COL_B))
    pad_shape = jax.ShapeDtypeStruct((s + 2 * KV_PAD, LANES), F32)
    return pl.pallas_call(
        body, grid=(s // tq,),
        in_specs=[bsp, pad_spec, pad_spec, pl.BlockSpec((tq, 3 * LANES), lambda n: (n, 0)),
                  pl.BlockSpec(bias.shape, lambda n: (0, 0, 0)), pl.BlockSpec(memory_space=pltpu.SMEM), blk, blk,
                  pl.BlockSpec(memory_space=pl.ANY)],
        out_specs=[bsp, pad_spec, pad_spec, pl.BlockSpec((8, LANES), lambda n: (0, 0))],
        out_shape=[jax.ShapeDtypeStruct(dproj.shape, BF16), pad_shape, pad_shape,
                   jax.ShapeDtypeStruct((8, LANES), F32)],
        input_output_aliases={8: 0}, name="swa_bwd",
        compiler_params=_params())(proj, kpad, vpad, tab, bias, sink, o_attn, dyb, dproj)


def _mem_exp(qh, mk):
    sc = lax.dot_general(qh, mk, _DIMS["nt"], preferred_element_type=F32) * (MEM_HEAD_DIM ** -0.5)
    return jnp.exp(sc - jnp.max(sc, axis=1, keepdims=True)).astype(BF16)


def _mem_fwd(proj, mkv):
    s = proj.shape[0]
    ts = 512
    mlen = mkv.shape[0]

    def body(m_ref, kv_ref, o_ref, y_ref):
        ones = jnp.ones((mlen, LANES), BF16)
        for h in range(MEM_HEADS):
            cols = slice(h * LANES, (h + 1) * LANES)
            mk = kv_ref[:, cols].astype(BF16)
            mv = kv_ref[:, MEM_WIDTH + h * LANES:MEM_WIDTH + (h + 1) * LANES].astype(BF16)
            e = _mem_exp(m_ref[:, cols].astype(BF16), mk)
            ox = jnp.dot(e, jnp.concatenate([mv, ones], axis=1), preferred_element_type=F32)
            oh = ox[:, :LANES] * (1.0 / ox[:, LANES:])
            o_ref[:, cols] = oh
            zh = m_ref[:, MEM_WIDTH + h * LANES:MEM_WIDTH + (h + 1) * LANES]
            y_ref[:, cols] = (oh * (zh * _sigmoid(zh))).astype(BF16)

    o_spec = pl.BlockSpec((ts, MEM_WIDTH), lambda i: (i, 0))
    return pl.pallas_call(
        body, grid=(s // ts,),
        in_specs=[pl.BlockSpec((ts, W_M), lambda i: (i, COL_M)),
                  pl.BlockSpec((mlen, 2 * MEM_WIDTH), lambda i: (0, 0))],
        out_specs=[o_spec, o_spec],
        out_shape=[jax.ShapeDtypeStruct((s, MEM_WIDTH), F32), jax.ShapeDtypeStruct((s, MEM_WIDTH), BF16)],
        name="mem_fwd", compiler_params=_params())(proj, mkv)


def _mem_bwd(proj, mkv, o_mem, dym, dproj, *, carry=None):
    s = proj.shape[0]
    ts = 512
    mlen = mkv.shape[0]
    scale = MEM_HEAD_DIM ** -0.5

    def body(m_ref, kv_ref, o_ref, dy_ref, _, dp_ref, dkv_ref):
        @pl.when(pl.program_id(0) == 0)
        def _():
            dkv_ref[...] = jnp.zeros_like(dkv_ref)

        ones = jnp.ones((mlen, LANES), BF16)
        for h in range(MEM_HEADS):
            cols = slice(h * LANES, (h + 1) * LANES)
            vcols = slice(MEM_WIDTH + h * LANES, MEM_WIDTH + (h + 1) * LANES)
            mk = kv_ref[:, cols].astype(BF16)
            mv = kv_ref[:, vcols].astype(BF16)
            qh = m_ref[:, cols].astype(BF16)
            zh = m_ref[:, vcols]
            sg = _sigmoid(zh)
            oh = o_ref[:, cols]
            dyh = dy_ref[:, cols]
            doh = dyh * (zh * sg)
            dp_ref[:, vcols] = (dyh * oh * (sg * (1.0 + zh * (1.0 - sg)))).astype(BF16)
            e = _mem_exp(qh, mk)
            inv = 1.0 / jnp.dot(e, ones, preferred_element_type=F32)
            prob = e.astype(F32) * jnp.concatenate([inv] * (mlen // LANES), axis=1)
            delta = jnp.sum(doh * oh, axis=1, keepdims=True)
            dohb = doh.astype(BF16)
            dprob = lax.dot_general(dohb, mv, _DIMS["nt"], preferred_element_type=F32)
            dsc = (prob * (dprob - delta)).astype(BF16)
            dp_ref[:, cols] = (jnp.dot(dsc, mk, preferred_element_type=F32) * scale).astype(BF16)
            dkv_ref[:, cols] += lax.dot_general(dsc, qh, _DIMS["tn"], preferred_element_type=F32) * scale
            dkv_ref[:, vcols] += lax.dot_general(prob.astype(BF16), dohb, _DIMS["tn"],
                                                 preferred_element_type=F32)

    blk = pl.BlockSpec((ts, MEM_WIDTH), lambda i: (i, 0))
    msp = pl.BlockSpec((ts, W_M), lambda i: (i, COL_M))
    kvsp = pl.BlockSpec((mlen, 2 * MEM_WIDTH), lambda i: (0, 0))
    outs, carried = _carried_call(
        lambda ins, outs, scr: body(*ins, *outs), carry, grid=(s // ts,),
        in_specs=[msp, kvsp, blk, blk, pl.BlockSpec(memory_space=pl.ANY)],
        out_specs=[msp, kvsp],
        out_shape=[jax.ShapeDtypeStruct(dproj.shape, BF16), jax.ShapeDtypeStruct(mkv.shape, F32)],
        scratch=[], operands=(proj, mkv, o_mem, dym, dproj), name="mem_bwd", aliases={4: 0})
    return (*outs, carried) if carry else tuple(outs)


def _forward_backward(x, mem, tgt, proj, w_conv, sink, g_mem, late_weights, g_post, early_exchange, kv_exchange):
    s = x.shape[0]
    tab = _rope_tables(s)
    bias = _window_bias()

    ya = _conv_fwd(proj, w_conv)
    kpad, vpad = _rope_kv(proj, tab)
    o_attn, yb, *arrived = _swa_fwd(proj, kpad, vpad, tab, bias, sink, carry=late_weights[0])
    w_kv, w_up, w_out = late_weights[1](arrived[0] if arrived else None)
    mn = _rmsnorm_fwd(mem, g_mem, name="mem_norm")
    mkv = _matmul(mn, w_kv, mode="nn", out_dtype=F32, tm=256, tn=1024, tk=D_MODEL, name="mem_kv")
    o_mem, ym = _mem_fwd(proj, mkv)
    merged, d_out, dy, dg_post, loss = _mid_fwd(ya, yb, ym, proj, x, tgt, w_up, w_out, g_post)
    dproj, d_ya, d_yb, d_ym, dw_up, dw_out = _mid_bwd(d_out, merged, ya, yb, ym, proj, w_up, w_out)

    dproj, dw_conv = _conv_bwd(proj, w_conv, d_ya, dproj)
    dproj, dkpad, dvpad, dsink = _swa_bwd(proj, kpad, vpad, tab, bias, sink, o_attn, d_yb, dproj)
    dproj = _rope_kv_bwd(dkpad, dvpad, tab, dproj)
    dproj, d_mkv, *early = _mem_bwd(proj, mkv, o_mem, d_ym, dproj, carry=early_exchange(dw_up, dw_out))

    dw_kv = _matmul(mn, d_mkv, mode="tn", out_dtype=F32, tm=1024, tn=1024, tk=256, name="dw_kv")
    d_mn = _matmul(d_mkv, w_kv, mode="nt", out_dtype=F32, tm=256, tn=1024, tk=D_MODEL, name="d_mn")
    _, dg_mem, *early_kv = _rmsnorm_bwd(d_mn, mem, g_mem, d_mn, name="mem_norm_bwd", carry=kv_exchange(dw_kv))

    return dict(loss=loss, dproj=dproj, dy=dy, w_conv=dw_conv, sink=dsink, g_mem=dg_mem,
                w_kv=dw_kv, w_up=dw_up, w_out=dw_out, g_post=dg_post, early=early[0] if early else None,
                early_kv=early_kv[0] if early_kv else None)


N_DEV = 8


def _position():
    return lax.axis_index("x"), lax.axis_index("y"), lax.axis_index("c")


def _other_chips(x, y):
    return (((1 - x, y), 2 * (1 - x) + y), ((x, 1 - y), 2 * x + (1 - y)), ((1 - x, 1 - y), 2 * (1 - x) + (1 - y)))


def _remote(src, dst, send_sems, recv_sems, k, device):
    return pltpu.make_async_remote_copy(src_ref=src, dst_ref=dst, send_sem=send_sems.at[k], recv_sem=recv_sems.at[k],
                                        device_id=device, device_id_type=MESH)


def _rows_half(ref, hf):
    rh = ref.shape[0] // 2
    return ref.at[pl.ds(pl.multiple_of(hf * rh, 8), rh)]


def _gather_weights(shards, small=None, relations=(0, 1, 2), into=None):
    n = len(shards)
    k = 0 if small is None else 1

    def peers(x, y):
        return [(r, chip, idx) for r, (chip, idx) in enumerate(_other_chips(x, y)) if r in relations]

    def ici(ins, outs, sems, a, r, chip, src_chip, c):
        return _remote(_rows_half(ins[a], c), _rows_half(outs[a].at[src_chip], c), sems[0], sems[1], 3 * a + r,
                       (*chip, c))

    def whole(ins, outs, sems, r, chip, src_chip, c):
        return _remote(ins[n], outs[n].at[src_chip], sems[0], sems[1], 3 * n + r, (*chip, c))

    def d2d(outs, sems, a, r, idx, hf, x, y, c):
        half = _rows_half(outs[a].at[idx], hf)
        return _remote(half, half, sems[2], sems[3], 3 * a + r, (x, y, 1 - c))

    def start(ins, outs, sems):
        x, y, c = _position()
        me = 2 * x + y
        for a in range(n):
            for r, chip, _ in peers(x, y):
                ici(ins, outs, sems, a, r, chip, me, c).start()
        for r, (chip, _) in enumerate(_other_chips(x, y)):
            if k:
                whole(ins, outs, sems, r, chip, me, c).start()

    def finish(ins, outs, sems):
        x, y, c = _position()
        me = 2 * x + y
        for a in range(n):
            for r, chip, idx in peers(x, y):
                ici(ins, outs, sems, a, r, chip, idx, c).wait_recv()
                d2d(outs, sems, a, r, idx, c, x, y, c).start()
        for a in range(n):
            for r, chip, idx in peers(x, y):
                d2d(outs, sems, a, r, idx, 1 - c, x, y, c).wait_recv()
        for r, (chip, idx) in enumerate(_other_chips(x, y)):
            if k:
                whole(ins, outs, sems, r, chip, idx, c).wait_recv()
                whole(ins, outs, sems, r, chip, me, c).wait_send()
        for a in range(n):
            for r, chip, idx in peers(x, y):
                ici(ins, outs, sems, a, r, chip, me, c).wait_send()
                d2d(outs, sems, a, r, idx, c, x, y, c).wait_send()

    operands = list(shards) + ([small] if k else [])
    shapes = [jax.ShapeDtypeStruct((N_CHIPS,) + s.shape, s.dtype) for s in operands]
    aliases = {}
    if into is not None:
        assert len(into) == len(operands)
        aliases = {len(operands) + a: a for a in range(len(into))}
        operands += list(into)
    return _Carry(operands, shapes,
                  [pltpu.SemaphoreType.DMA((3 * (n + k),)), pltpu.SemaphoreType.DMA((3 * (n + k),)),
                   pltpu.SemaphoreType.DMA((3 * n,)), pltpu.SemaphoreType.DMA((3 * n,))], start, finish, aliases)


def _pair_exchange(send):
    n = len(send)

    def copies(ins, outs, sems):
        x, y, c = _position()
        return [_remote(ins[a], outs[a], sems[0], sems[1], a, (x, y, 1 - c)) for a in range(n)]

    def start(ins, outs, sems):
        for cp in copies(ins, outs, sems):
            cp.start()

    def finish(ins, outs, sems):
        for cp in copies(ins, outs, sems):
            cp.wait()

    return _Carry(send, [jax.ShapeDtypeStruct(p.shape, p.dtype) for p in send],
                  [pltpu.SemaphoreType.DMA((n,)), pltpu.SemaphoreType.DMA((n,))], start, finish)


def _chip_exchange(sums):
    n = len(sums)

    def copies(ins, outs, sems):
        x, y, c = _position()
        return [_remote(ins[a].at[idx], outs[a].at[r], sems[0], sems[1], 3 * a + r, (*chip, c))
                for a in range(n) for r, (chip, idx) in enumerate(_other_chips(x, y))]

    def start(ins, outs, sems):
        for cp in copies(ins, outs, sems):
            cp.start()

    def finish(ins, outs, sems):
        for cp in copies(ins, outs, sems):
            cp.wait()

    return _Carry(sums, [jax.ShapeDtypeStruct((3,) + p.shape[1:], p.dtype) for p in sums],
                  [pltpu.SemaphoreType.DMA((3 * n,)), pltpu.SemaphoreType.DMA((3 * n,))], start, finish)


def _pair_share(pairs):
    n = len(pairs)

    def start(ins, outs, sems):
        x, y, c = _position()
        for a in range(n):
            _remote(outs[a].at[c], outs[a].at[c], sems[0], sems[1], a, (x, y, 1 - c)).start()

    def finish(ins, outs, sems):
        x, y, c = _position()
        for a in range(n):
            _remote(outs[a].at[1 - c], outs[a].at[1 - c], sems[0], sems[1], a, (x, y, 1 - c)).wait_recv()
        for a in range(n):
            _remote(outs[a].at[c], outs[a].at[c], sems[0], sems[1], a, (x, y, 1 - c)).wait_send()

    return _Carry(pairs, [jax.ShapeDtypeStruct(p.shape, p.dtype) for p in pairs],
                  [pltpu.SemaphoreType.DMA((n,)), pltpu.SemaphoreType.DMA((n,))], start, finish,
                  aliases={a: a for a in range(n)})


def _small_allreduce(pack, share):
    rows, width = pack.shape
    n_share = len(share.ins)

    def body(p_ref, *refs):
        share_in, o_ref, share_out = refs[:n_share], refs[n_share], refs[n_share + 1:2 * n_share + 1]
        buf, send_sems, recv_sems = refs[2 * n_share + 1:2 * n_share + 4]
        share_sems = refs[2 * n_share + 4:]
        share.start(share_in, share_out, share_sems)
        x, y, c = _position()
        me = 4 * x + 2 * y + c
        buf[me] = p_ref[...]
        peers = []
        for r in range(1, N_DEV):
            fx, fy, fc = (r >> 2) & 1, (r >> 1) & 1, r & 1
            px, py, pc = (1 - x if fx else x), (1 - y if fy else y), (1 - c if fc else c)
            peers.append(((px, py, pc), 4 * px + 2 * py + pc))
        sends = [_remote(p_ref, buf.at[me], send_sems, recv_sems, r, dev) for r, (dev, _) in enumerate(peers)]
        for cp in sends:
            cp.start()
        for r, (dev, idx) in enumerate(peers):
            _remote(p_ref, buf.at[idx], send_sems, recv_sems, r, dev).wait_recv()
        for cp in sends:
            cp.wait_send()
        acc = buf[0]
        for k in range(1, N_DEV):
            acc = acc + buf[k]
        o_ref[...] = acc
        share.finish(share_in, share_out, share_sems)

    vm = pl.BlockSpec(memory_space=pltpu.VMEM)
    red, *shared = pl.pallas_call(
        body, in_specs=[vm] + [_HBM] * n_share, out_specs=[vm] + [_HBM] * n_share,
        out_shape=[jax.ShapeDtypeStruct(pack.shape, F32)] + share.out_shapes,
        scratch_shapes=[pltpu.VMEM((N_DEV, rows, width), F32), pltpu.SemaphoreType.DMA((N_DEV - 1,)),
                        pltpu.SemaphoreType.DMA((N_DEV - 1,))] + share.sems,
        input_output_aliases={1 + i: 1 + o for i, o in share.aliases.items()},
        name="small_allreduce")(pack, *share.ins)
    return red, shared


ROW_TILE_MAX = 512
SUM_TILE_MAX = 2048
BF16_SUBLANES = 16


def _row_tile(rows, most=ROW_TILE_MAX):
    if rows <= most:
        return rows
    return max(t for t in range(BF16_SUBLANES, most + 1, BF16_SUBLANES) if rows % t == 0)


def _pair_add(keep, recv, name):
    nj, rh, cols = keep.shape
    tr = _row_tile(rh, SUM_TILE_MAX)

    def body(k_ref, r_ref, o_ref):
        o_ref[...] = (k_ref[...].astype(F32) + r_ref[...].astype(F32)).astype(BF16)

    blk = pl.BlockSpec((None, tr, cols), lambda j, i: (j, i, 0))
    return pl.pallas_call(body, grid=(nj, rh // tr), in_specs=[blk, blk], out_specs=blk,
                          out_shape=jax.ShapeDtypeStruct(keep.shape, BF16), name=name,
                          compiler_params=_params())(keep, recv)


def _chip_add(sums, recv, where, name):
    _, rh, cols = sums.shape
    tr = _row_tile(rh, SUM_TILE_MAX)

    def body(w_ref, s_ref, r_ref, o_ref):
        o_ref[...] = ((s_ref[...].astype(F32) + r_ref[0].astype(F32)) + r_ref[1].astype(F32)) + r_ref[2].astype(F32)

    grid_spec = pltpu.PrefetchScalarGridSpec(
        num_scalar_prefetch=1, grid=(rh // tr,),
        in_specs=[pl.BlockSpec((None, tr, cols), lambda i, w_ref: (w_ref[0], i, 0)),
                  pl.BlockSpec((3, tr, cols), lambda i, w_ref: (0, i, 0))],
        out_specs=pl.BlockSpec((None, tr, cols), lambda i, w_ref: (w_ref[1], i, 0)))
    return pl.pallas_call(body, grid_spec=grid_spec, out_shape=jax.ShapeDtypeStruct((2, rh, cols), F32),
                          name=name, compiler_params=_params())(where, sums, recv)


def _adamw(w, g, m, v, name):
    rows, cols = w.shape
    tr = _row_tile(rows)
    assert rows % tr == 0

    def body(w_ref, g_ref, m_ref, v_ref, d_ref, mo_ref, vo_ref):
        gv = g_ref[...]
        m_new = ADAM_B1 * m_ref[...] + (1.0 - ADAM_B1) * gv
        v_new = ADAM_B2 * v_ref[...] + (1.0 - ADAM_B2) * jnp.square(gv)
        m_hat = m_new / (1.0 - ADAM_B1 ** ADAM_STEP)
        v_hat = v_new / (1.0 - ADAM_B2 ** ADAM_STEP)
        d_ref[...] = -ADAM_LR * (m_hat / (jnp.sqrt(v_hat) + ADAM_EPS) + ADAM_WD * w_ref[...])
        mo_ref[...] = m_new
        vo_ref[...] = v_new

    blk = pl.BlockSpec((tr, cols), lambda i: (i, 0))
    shp = jax.ShapeDtypeStruct((rows, cols), F32)
    return pl.pallas_call(body, grid=(rows // tr,), in_specs=[blk] * 4, out_specs=[blk] * 3,
                          out_shape=[shp] * 3, name=name, compiler_params=_params())(w, g, m, v)


def _adamw_halves(w, g2, m, v, name):
    rows, cols = w.shape
    half = cols // 2
    tr = _row_tile(rows)

    def body(w_ref, g_ref, m_ref, v_ref, go_ref, d_ref, mo_ref, vo_ref):
        gv = g_ref[...]
        go_ref[...] = gv
        m_new = ADAM_B1 * m_ref[...] + (1.0 - ADAM_B1) * gv
        v_new = ADAM_B2 * v_ref[...] + (1.0 - ADAM_B2) * jnp.square(gv)
        m_hat = m_new / (1.0 - ADAM_B1 ** ADAM_STEP)
        v_hat = v_new / (1.0 - ADAM_B2 ** ADAM_STEP)
        d_ref[...] = -ADAM_LR * (m_hat / (jnp.sqrt(v_hat) + ADAM_EPS) + ADAM_WD * w_ref[...])
        mo_ref[...] = m_new
        vo_ref[...] = v_new

    blk = pl.BlockSpec((tr, half), lambda hf, i: (i, hf))
    gsp = pl.BlockSpec((None, tr, half), lambda hf, i: (hf, i, 0))
    shp = jax.ShapeDtypeStruct((rows, cols), F32)
    return pl.pallas_call(body, grid=(2, rows // tr), in_specs=[blk, gsp, blk, blk], out_specs=[blk] * 4,
                          out_shape=[shp] * 4, name=name, compiler_params=_params())(w, g2, m, v)


SHARD_W = IN_WIDTH // N_CHIPS


def _half_major(a):
    r, c = a.shape
    return a.reshape(N_CHIPS, 2, r // N_CHIPS // 2, c).transpose(1, 0, 2, 3)


def kernel(x, mem, g_pre, w_in, w_conv, attn_sink, g_mem, w_mem_kv, w_up_a, w_up_b, w_up_m, w_out, g_post, loss_target, m_g_pre, m_w_in, m_w_conv, m_attn_sink, m_g_mem, m_w_mem_kv, m_w_up_a, m_w_up_b, m_w_up_m, m_w_out, m_g_post, v_g_pre, v_w_in, v_w_conv, v_attn_sink, v_g_mem, v_w_mem_kv, v_w_up_a, v_w_up_b, v_w_up_m, v_w_out, v_g_post):
    xi, yi, ci = _position()
    chip = 2 * xi + yi
    where = jnp.stack([chip, ci, N_CHIPS - 1 - chip]).astype(jnp.int32)

    own = [w_in[0].T.astype(BF16), w_mem_kv[0].astype(BF16),
           jnp.concatenate([w_up_a[0], w_up_b[0], w_up_m[0]], axis=0).astype(BF16), w_out[0].astype(BF16)]
    own_conv = jnp.pad(w_conv[0], ((0, 5), (0, 0)))

    def pieces(mine, got):
        got = lax.dynamic_update_slice_in_dim(got, mine[None], chip, axis=0)
        return [got[j] for j in range(N_CHIPS)]

    diag = N_CHIPS - 1 - chip
    proj, h, h_t, got_near, got_conv, got_far = _proj_near(x[0], g_pre, own[0], own_conv, where)
    w_near = lax.dynamic_update_slice_in_dim(got_near, own[0][None], chip, axis=0).reshape(IN_WIDTH, D_MODEL)
    far = lax.dynamic_index_in_dim(got_far, diag, 0, keepdims=False)
    proj = _proj_far(h, w_near, far, where, into=proj)
    w_conv_full = jnp.concatenate([p[:3] for p in pieces(own_conv, got_conv)], axis=1)

    def late_weights(gathered):
        w_kv_full = jnp.concatenate(pieces(own[1], gathered[0]), axis=0)
        up_pieces = pieces(own[2], gathered[1])
        w_up_full = jnp.stack([jnp.concatenate([p[k * A_WIDTH:(k + 1) * A_WIDTH] for p in up_pieces], axis=1)
                               for k in range(3)])
        return w_kv_full, w_up_full, jnp.concatenate(pieces(own[3], gathered[2]), axis=0)

    def pick(parts, hf):
        return [lax.dynamic_index_in_dim(p, hf, 0, keepdims=False) for p in parts]

    def up_out_parts(dw_up, dw_out):
        up = (dw_up.reshape(3, A_WIDTH, N_CHIPS, D_MODEL // N_CHIPS).transpose(2, 0, 1, 3)
              .reshape(N_CHIPS, 2, 3 * A_WIDTH // 2, D_MODEL // N_CHIPS).transpose(1, 0, 2, 3))
        return [up.astype(BF16), _half_major(dw_out).astype(BF16)]

    g = _forward_backward(x[0], mem[0], loss_target[0], proj, w_conv_full, attn_sink, g_mem,
                          (_gather_weights(own[1:]), late_weights), g_post,
                          lambda dw_up, dw_out: _pair_exchange(pick(up_out_parts(dw_up, dw_out), 1 - ci)),
                          lambda dw_kv: _pair_exchange(pick([_half_major(dw_kv).astype(BF16)], 1 - ci)))

    half_rows = D_MODEL // 2

    def dw_in_half(half_of, name, carry):
        dw, carried = _dw_in_t(g["dproj"], h_t, half_of=half_of, where=where, name=name, carry=carry)
        return dw.reshape(N_CHIPS, SHARD_W, half_rows), carried

    small_keep = pick([_half_major(g["w_kv"]).astype(BF16)] + up_out_parts(g["w_up"], g["w_out"]), ci)
    small_names = ["w_kv", "w_up", "w_out"]
    sums_small = [_pair_add(k, r, "pair_add_" + nm)
                  for k, r, nm in zip(small_keep, g["early_kv"] + g["early"], small_names)]
    dw_send, recv3_small = dw_in_half(lambda w: 1 - w[1], "dw_in_send", _chip_exchange(sums_small))
    dw_keep, (recv_in,) = dw_in_half(lambda w: w[1], "dw_in_keep", _pair_exchange([dw_send]))
    sum_in = _pair_add(dw_keep, recv_in, "pair_add_w_in")
    (grad_x, dg_pre), (recv3_in,) = _d_h(g["dproj"], w_near, far, where, x[0], g_pre, g["dy"],
                                         carry=_chip_exchange([sum_in]))
    pairs = [_chip_add(s, r, where, "chip_add_" + nm)
             for s, r, nm in zip([sum_in] + sums_small, [recv3_in] + recv3_small, ["w_in"] + small_names)]

    zeros512 = jnp.zeros((1, D_MODEL - A_WIDTH), F32)
    conv_rows = [jnp.concatenate([g["w_conv"][k:k + 1], zeros512], axis=1) for k in range(3)]
    sink_row = jnp.pad(g["sink"][:, 0].reshape(1, N_Q_HEADS), ((0, 0), (0, D_MODEL - N_Q_HEADS)))
    loss_row = jnp.pad(g["loss"], ((0, 0), (0, D_MODEL - LANES)))
    pack = jnp.concatenate([dg_pre, g["g_mem"], g["g_post"]] + conv_rows + [sink_row, loss_row], axis=0)
    red, full = _small_allreduce(pack, _pair_share(pairs))
    loss = red[7, 0]
    small_grads = dict(
        g_pre=red[0:1], g_mem=red[1:2], g_post=red[2:3], attn_sink=red[6:7, :N_Q_HEADS],
        w_conv=lax.dynamic_slice(red[3:6, :A_WIDTH], (0, chip * LANES), (3, LANES)))

    gw_up = full[2].reshape(3, A_WIDTH, D_MODEL // N_CHIPS)
    grads = dict(small_grads, w_mem_kv=full[1].reshape(D_MODEL // N_CHIPS, 2 * MEM_WIDTH),
                 w_up_a=gw_up[0], w_up_b=gw_up[1], w_up_m=gw_up[2],
                 w_out=full[3].reshape(D_MODEL // N_CHIPS, D_MODEL))

    weights = dict(g_pre=g_pre, w_in=w_in, w_conv=w_conv, attn_sink=attn_sink, g_mem=g_mem, w_mem_kv=w_mem_kv,
                   w_up_a=w_up_a, w_up_b=w_up_b, w_up_m=w_up_m, w_out=w_out, g_post=g_post)
    m_in = dict(g_pre=m_g_pre, w_in=m_w_in, w_conv=m_w_conv, attn_sink=m_attn_sink, g_mem=m_g_mem,
                w_mem_kv=m_w_mem_kv, w_up_a=m_w_up_a, w_up_b=m_w_up_b, w_up_m=m_w_up_m, w_out=m_w_out,
                g_post=m_g_post)
    v_in = dict(g_pre=v_g_pre, w_in=v_w_in, w_conv=v_w_conv, attn_sink=v_attn_sink, g_mem=v_g_mem,
                w_mem_kv=v_w_mem_kv, w_up_a=v_w_up_a, w_up_b=v_w_up_b, w_up_m=v_w_up_m, w_out=v_w_out,
                g_post=v_g_post)
    out_g, out_d, out_m, out_v = [], [], [], []
    for nm in ("g_pre", "w_in", "w_conv", "attn_sink", "g_mem", "w_mem_kv", "w_up_a", "w_up_b", "w_up_m", "w_out",
               "g_post"):
        shape = weights[nm].shape
        if nm == "w_in":
            results = _adamw_halves(w_in[0].T, full[0], m_w_in[0].T, v_w_in[0].T, "adamw_w_in")
            for out, t in zip((out_g, out_d, out_m, out_v), results):
                out.append(t.T.reshape(shape))
            continue
        two_d = shape[-2:]
        gr = grads[nm].reshape(two_d)
        d, m_new, v_new = _adamw(weights[nm].reshape(two_d), gr, m_in[nm].reshape(two_d), v_in[nm].reshape(two_d),
                                 "adamw_" + nm)
        out_g.append(gr.reshape(shape))
        out_d.append(d.reshape(shape))
        out_m.append(m_new.reshape(shape))
        out_v.append(v_new.reshape(shape))
    return (loss, grad_x.reshape(x.shape), *out_g, *out_d, *out_m, *out_v)
```

```python
import jax
import jax.numpy as jnp
from jax import lax
from jax.experimental import pallas as pl
from jax.experimental.pallas import tpu as pltpu

F32 = jnp.float32
BF16 = jnp.bfloat16
MESH = pl.DeviceIdType.MESH

D_MODEL = 1024
EPS = 1e-6
A_WIDTH = 512
HEAD_DIM = 64
N_Q_HEADS = 8
WINDOW_BLOCK = 128
KV_PAD = 512
ROPE_THETA = 500000.0
ROT_DIM = 16
MEM_HEADS = 4
MEM_HEAD_DIM = 128
MEM_WIDTH = 512
IN_WIDTH = 7424
N_CHIPS = 4
LANES = 128
HALF_LANES = 64

PERM_SEGS = ((0, 2560), (2816, 3328), (4352, 7424), (3328, 4352), (2560, 2816))
COL_A, W_A = 0, 2048
COL_B, W_B = 2, 1024
COL_G, W_G = 1, 3072
COL_M, W_M = 6, 1024
COL_KV, W_KV = 28, 256

ADAM_LR = 0.001
ADAM_B1 = 0.9
ADAM_B2 = 0.999
ADAM_EPS = 1e-08
ADAM_WD = 0.01
ADAM_STEP = 10

VMEM_LIGHT_BYTES = 48 * 1024 * 1024
VMEM_HEAVY_BYTES = 48 * 1024 * 1024


_HBM = pl.BlockSpec(memory_space=pltpu.HBM)


def _params(heavy=False):
    return pltpu.CompilerParams(vmem_limit_bytes=VMEM_HEAVY_BYTES if heavy else VMEM_LIGHT_BYTES)


def _sigmoid(v):
    return jax.nn.sigmoid(v)


_DIMS = {"nn": (((1,), (0,)), ((), ())), "nt": (((1,), (1,)), ((), ())), "tn": (((0,), (0,)), ((), ()))}


class _Carry:
    def __init__(self, ins, out_shapes, sems, start, finish, aliases=None):
        self.ins, self.out_shapes, self.sems = list(ins), list(out_shapes), list(sems)
        self.start, self.finish, self.aliases = start, finish, dict(aliases or {})


def _join(*carries):
    def split(seq, counts):
        pos, parts = 0, []
        for n in counts:
            parts.append(seq[pos:pos + n])
            pos += n
        return parts

    n_in = [len(c.ins) for c in carries]
    n_out = [len(c.out_shapes) for c in carries]
    n_sem = [len(c.sems) for c in carries]

    def run(which):
        def go(ins, outs, sems):
            for c, i, o, sm in zip(carries, split(ins, n_in), split(outs, n_out), split(sems, n_sem)):
                getattr(c, which)(i, o, sm)
        return go

    aliases = {}
    for k, c in enumerate(carries):
        aliases.update({sum(n_in[:k]) + i: sum(n_out[:k]) + o for i, o in c.aliases.items()})
    return _Carry([a for c in carries for a in c.ins], [sh for c in carries for sh in c.out_shapes],
                  [sm for c in carries for sm in c.sems], run("start"), run("finish"), aliases)


def _carried_call(body, carry, *, grid, in_specs, out_specs, out_shape, scratch, operands, name, prefetch=None,
                  aliases=None, heavy=False):
    n_in, n_out, n_scr = len(in_specs), len(out_specs), len(scratch)
    c_in = len(carry.ins) if carry else 0
    c_out = len(carry.out_shapes) if carry else 0
    n_pre = 0 if prefetch is None else 1
    steps = 1
    for g in grid:
        steps *= g

    def wrapped(*refs):
        refs = refs[n_pre:]
        ins, cins = refs[:n_in], refs[n_in:n_in + c_in]
        outs = refs[n_in + c_in:n_in + c_in + n_out]
        couts = refs[n_in + c_in + n_out:n_in + c_in + n_out + c_out]
        rest = refs[n_in + c_in + n_out + c_out:]
        scr, sems = rest[:n_scr], rest[n_scr:]
        if carry:
            step = pl.program_id(0)
            for ax in range(1, len(grid)):
                step = step * grid[ax] + pl.program_id(ax)

            @pl.when(step == 0)
            def _():
                carry.start(cins, couts, sems)

        body(ins, outs, scr)
        if carry:
            @pl.when(step == steps - 1)
            def _():
                carry.finish(cins, couts, sems)

    all_aliases = {n_pre + i: o for i, o in (aliases or {}).items()}
    if carry:
        all_aliases.update({n_pre + n_in + i: n_out + o for i, o in carry.aliases.items()})
    all_in = list(in_specs) + [_HBM] * c_in
    all_out = list(out_specs) + [_HBM] * c_out
    all_scratch = list(scratch) + (carry.sems if carry else [])
    if n_pre:
        spec = dict(grid_spec=pltpu.PrefetchScalarGridSpec(num_scalar_prefetch=1, grid=grid, in_specs=all_in,
                                                           out_specs=all_out, scratch_shapes=all_scratch))
        pre = (prefetch,)
    else:
        spec = dict(grid=grid, in_specs=all_in, out_specs=all_out, scratch_shapes=all_scratch)
        pre = ()
    results = pl.pallas_call(
        wrapped, out_shape=list(out_shape) + (carry.out_shapes if carry else []), input_output_aliases=all_aliases,
        name=name, compiler_params=_params(heavy), **spec)(*pre, *operands, *(carry.ins if carry else []))
    return list(results[:n_out]), list(results[n_out:])


def _matmul(a, b, *, mode, out_dtype, tm, tn, tk, name):
    if mode == "nn":
        (m, k), (_, n) = a.shape, b.shape
    elif mode == "nt":
        (m, k), (n, _) = a.shape, b.shape
    else:
        (k, m), (_, n) = a.shape, b.shape
    tm, tn, tk = min(tm, m), min(tn, n), min(tk, k)
    assert m % tm == 0 and n % tn == 0 and k % tk == 0
    nk = k // tk
    dims = _DIMS[mode]

    if mode == "nn":
        a_spec = pl.BlockSpec((tm, tk), lambda i, j, kk: (i, kk))
        b_spec = pl.BlockSpec((tk, tn), lambda i, j, kk: (kk, j))
    elif mode == "nt":
        a_spec = pl.BlockSpec((tm, tk), lambda i, j, kk: (i, kk))
        b_spec = pl.BlockSpec((tn, tk), lambda i, j, kk: (j, kk))
    else:
        a_spec = pl.BlockSpec((tk, tm), lambda i, j, kk: (kk, i))
        b_spec = pl.BlockSpec((tk, tn), lambda i, j, kk: (kk, j))
    o_spec = pl.BlockSpec((tm, tn), lambda i, j, kk: (i, j))

    def part(a_ref, b_ref):
        return lax.dot_general(a_ref[...].astype(BF16), b_ref[...].astype(BF16), dims,
                               preferred_element_type=F32)

    if nk == 1:
        def body(a_ref, b_ref, o_ref):
            o_ref[...] = part(a_ref, b_ref).astype(out_dtype)
        scratch = []
    else:
        def body(a_ref, b_ref, o_ref, acc_ref):
            kk = pl.program_id(2)

            @pl.when(kk == 0)
            def _():
                acc_ref[...] = part(a_ref, b_ref)

            @pl.when(kk > 0)
            def _():
                acc_ref[...] += part(a_ref, b_ref)

            @pl.when(kk == nk - 1)
            def _():
                o_ref[...] = acc_ref[...].astype(out_dtype)
        scratch = [pltpu.VMEM((tm, tn), F32)]

    return pl.pallas_call(
        body, grid=(m // tm, n // tn, nk), in_specs=[a_spec, b_spec], out_specs=o_spec,
        out_shape=jax.ShapeDtypeStruct((m, n), out_dtype), scratch_shapes=scratch,
        name=name, compiler_params=_params())(a, b)


IN_BLOCK = 256
N_IN_BLOCKS = IN_WIDTH // IN_BLOCK
SHARD_BLOCKS = (IN_WIDTH // N_CHIPS) // IN_BLOCK
BLOCK_RUNS = tuple((a // IN_BLOCK, sum(d - c for c, d in PERM_SEGS[:k]) // IN_BLOCK, (b - a) // IN_BLOCK)
                   for k, (a, b) in enumerate(PERM_SEGS))


def _perm_block(r):
    p = r
    for ref0, perm0, n in BLOCK_RUNS:
        p = jnp.where((r >= ref0) & (r < ref0 + n), r - ref0 + perm0, p)
    return p


def _proj_near(x, g_pre, own_w, small, where):
    s, d = x.shape
    norm_tile = min(512, s)
    n_own = SHARD_BLOCKS - 1
    n_diag = SHARD_BLOCKS + 1
    n_blocks = N_IN_BLOCKS - n_diag
    piece = IN_WIDTH // N_CHIPS - SHARD_BLOCKS * IN_BLOCK
    near = _gather_weights([own_w], small, relations=(0, 1))
    far = _gather_weights([own_w], relations=(2,))
    both = _join(near, far)
    n_cin, n_cout = len(both.ins), len(both.out_shapes)

    def block_of(i, w):
        me, dg = w[0], w[2]
        own0 = SHARD_BLOCKS * me + jnp.minimum(me, 1)
        dg0 = SHARD_BLOCKS * dg
        lo0, hi0 = jnp.minimum(own0, dg0), jnp.maximum(own0, dg0)
        lo_n = jnp.where(own0 < dg0, n_own, n_diag)
        hi_n = jnp.where(own0 < dg0, n_diag, n_own)
        r = i - n_own
        r = r + lo_n * (r >= lo0).astype(jnp.int32)
        r = r + hi_n * (r >= hi0).astype(jnp.int32)
        return jnp.where(i < n_own, own0 + i, r)

    def body(w_ref, x_hbm, g_ref, own_hbm, *refs):
        cins, (o_ref, h_hbm, ht_hbm) = refs[:n_cin], refs[n_cin:n_cin + 3]
        couts = refs[n_cin + 3:n_cin + 3 + n_cout]
        blocks, block_sems, h_ref, x_tile, ht_tile, io_sem = refs[n_cin + 3 + n_cout:n_cin + 9 + n_cout]
        sems = refs[n_cin + 9 + n_cout:]
        near_refs = (cins[:len(near.ins)], couts[:len(near.out_shapes)], sems[:len(near.sems)])
        far_refs = (cins[len(near.ins):], couts[len(near.out_shapes):], sems[len(near.sems):])
        gathered = couts[0]
        i = pl.program_id(0)
        me = w_ref[0]

        def fetch(step, slot):
            r = block_of(step, w_ref)
            for p in range(IN_BLOCK // piece):
                row = r * IN_BLOCK + p * piece
                j = row // (IN_WIDTH // N_CHIPS)
                off = pl.multiple_of(row - j * (IN_WIDTH // N_CHIPS), BF16_SUBLANES)
                dst = blocks.at[slot, pl.ds(p * piece, piece)]

                @pl.when(j == me)
                def _():
                    pltpu.make_async_copy(own_hbm.at[pl.ds(off, piece)], dst, block_sems.at[slot]).start()

                @pl.when(j != me)
                def _():
                    pltpu.make_async_copy(gathered.at[j, pl.ds(off, piece)], dst, block_sems.at[slot]).start()

        def arrived(slot):
            pltpu.make_async_copy(own_hbm.at[pl.ds(0, IN_BLOCK)], blocks.at[slot], block_sems.at[slot]).wait()

        slot = i % 2

        def norm_rows(k):
            rows = pl.ds(k * norm_tile, norm_tile)
            pltpu.sync_copy(x_hbm.at[rows], x_tile)
            xv = x_tile[...]
            hv = (xv * lax.rsqrt(jnp.mean(xv * xv, axis=-1, keepdims=True) + EPS)) * g_ref[...]
            h_ref[rows, :] = hv.astype(BF16)
            ht_tile[...] = hv.T.astype(BF16)
            to_h = pltpu.make_async_copy(h_ref.at[rows], h_hbm.at[rows], io_sem.at[0])
            to_ht = pltpu.make_async_copy(ht_tile, ht_hbm.at[:, rows], io_sem.at[1])
            to_h.start()
            to_ht.start()
            to_h.wait()
            to_ht.wait()

        @pl.when(i == 0)
        def _():
            near.start(*near_refs)
            fetch(i, slot)
            for k in range(s // norm_tile):
                norm_rows(k)

        @pl.when(i == n_own)
        def _():
            near.finish(*near_refs)
            far.start(*far_refs)
            fetch(i, slot)

        arrived(slot)

        @pl.when((i + 1 < n_blocks) & (i + 1 != n_own))
        def _():
            fetch(i + 1, 1 - slot)

        o_ref[...] = lax.dot_general(h_ref[...], blocks[slot], _DIMS["nt"], preferred_element_type=F32)

        @pl.when(i == n_blocks - 1)
        def _():
            far.finish(*far_refs)

    anysp = pl.BlockSpec(memory_space=pl.ANY)
    grid_spec = pltpu.PrefetchScalarGridSpec(
        num_scalar_prefetch=1, grid=(n_blocks,),
        in_specs=[anysp, pl.BlockSpec((1, d), lambda i, w: (0, 0)), anysp] + [_HBM] * n_cin,
        out_specs=[pl.BlockSpec((s, IN_BLOCK), lambda i, w: (0, _perm_block(block_of(i, w)))), anysp, anysp]
        + [_HBM] * n_cout,
        scratch_shapes=[pltpu.VMEM((2, IN_BLOCK, d), BF16), pltpu.SemaphoreType.DMA((2,)), pltpu.VMEM((s, d), BF16),
                        pltpu.VMEM((norm_tile, d), F32), pltpu.VMEM((d, norm_tile), BF16),
                        pltpu.SemaphoreType.DMA((2,))] + both.sems)
    return pl.pallas_call(
        body, grid_spec=grid_spec,
        out_shape=[jax.ShapeDtypeStruct((s, IN_WIDTH), F32), jax.ShapeDtypeStruct((s, d), BF16),
                   jax.ShapeDtypeStruct((d, s), BF16)] + both.out_shapes,
        name="proj_near", compiler_params=_params())(where, x, g_pre, own_w, *both.ins)


def _proj_far(h, w_near, far, where, *, into, carry=None):
    s, d = h.shape
    n_blocks = SHARD_BLOCKS + 1
    lead = IN_WIDTH // N_CHIPS - SHARD_BLOCKS * IN_BLOCK

    def body(ins, outs, scr):
        where_ref, h_ref, w_hbm, far_hbm, _ = ins
        win, sem = scr
        i = pl.program_id(0)

        @pl.when(i == 0)
        def _():
            dg = where_ref[2]
            rows = pl.ds(pl.multiple_of(dg * (SHARD_BLOCKS * IN_BLOCK), IN_BLOCK), n_blocks * IN_BLOCK)
            window = pltpu.make_async_copy(w_hbm.at[rows], win, sem)
            window.start()
            window.wait()
            shard = pltpu.make_async_copy(far_hbm, win.at[pl.ds(pl.multiple_of(dg * lead, BF16_SUBLANES), SHARD_W)], sem)
            shard.start()
            shard.wait()

        blk = win[pl.ds(pl.multiple_of(i * IN_BLOCK, IN_BLOCK), IN_BLOCK), :]
        outs[0][...] = lax.dot_general(h_ref[...], blk, _DIMS["nt"], preferred_element_type=F32)

    anysp = pl.BlockSpec(memory_space=pl.ANY)
    (proj,), carried = _carried_call(
        body, carry, grid=(n_blocks,),
        in_specs=[pl.BlockSpec(memory_space=pltpu.SMEM), pl.BlockSpec((s, d), lambda i, w: (0, 0)), anysp, anysp, anysp],
        out_specs=[pl.BlockSpec((s, IN_BLOCK), lambda i, w: (0, _perm_block(i + SHARD_BLOCKS * w[2])))],
        out_shape=[jax.ShapeDtypeStruct((s, IN_WIDTH), F32)],
        scratch=[pltpu.VMEM((n_blocks * IN_BLOCK, d), BF16), pltpu.SemaphoreType.DMA],
        operands=(where, h, w_near, far, into), name="proj_far", prefetch=where, aliases={4: 0})
    return (proj, carried) if carry else proj


def _dw_in_t(dproj, h_t, *, half_of, where, name, carry=None):
    d, s = h_t.shape
    c = d // 2

    def body(ins, outs, scr):
        outs[0][...] = lax.dot_general(ins[1][...], ins[0][...], _DIMS["nn"], preferred_element_type=F32).T.astype(BF16)

    (dw,), carried = _carried_call(
        body, carry, grid=(N_IN_BLOCKS,),
        in_specs=[pl.BlockSpec((s, IN_BLOCK), lambda r, w: (0, _perm_block(r))),
                  pl.BlockSpec((c, s), lambda r, w: (half_of(w), 0))],
        out_specs=[pl.BlockSpec((IN_BLOCK, c), lambda r, w: (r, 0))],
        out_shape=[jax.ShapeDtypeStruct((IN_WIDTH, c), BF16)], scratch=[], operands=(dproj, h_t), name=name,
        prefetch=where)
    return (dw, carried) if carry else dw


def _norm_bwd_tile(dhv, xv, gv, resv):
    r = lax.rsqrt(jnp.mean(xv * xv, axis=-1, keepdims=True) + EPS)
    xh = xv * r
    dxh = dhv * gv
    dx = resv + r * (dxh - xh * jnp.mean(dxh * xh, axis=-1, keepdims=True))
    return dx, jnp.sum(dhv * xh, axis=0, keepdims=True)


def _d_h(dproj, w_near, far, where, x, g, res, *, carry=None):
    s = dproj.shape[0]
    d = w_near.shape[1]
    tm = min(s, 256)
    n = s // tm
    assert n % 2 == 0

    def body(ins, outs, scr):
        where_ref, a_ref, w_hbm, far_hbm, x_ref, g_ref, res_ref = ins
        dx_ref, dg_ref = outs
        w_ref, sem, dh_even, dh_odd = scr
        i = pl.program_id(0)

        def norm_bwd(dh_ref):
            dx, part = _norm_bwd_tile(dh_ref[...], x_ref[...], g_ref[...], res_ref[...])
            dx_ref[...] = dx
            dg_ref[...] += part

        def matmul(dh_ref):
            acc = None
            for ref0, perm0, nb in BLOCK_RUNS:
                term = jnp.dot(a_ref[:, perm0 * IN_BLOCK:(perm0 + nb) * IN_BLOCK],
                               w_ref[ref0 * IN_BLOCK:(ref0 + nb) * IN_BLOCK, :], preferred_element_type=F32)
                acc = term if acc is None else acc + term
            dh_ref[...] = acc

        @pl.when(i == 0)
        def _():
            whole = pltpu.make_async_copy(w_hbm, w_ref, sem)
            whole.start()
            whole.wait()
            rows = pl.ds(pl.multiple_of(where_ref[2] * SHARD_W, BF16_SUBLANES), SHARD_W)
            part = pltpu.make_async_copy(far_hbm, w_ref.at[rows], sem)
            part.start()
            part.wait()
            dg_ref[...] = jnp.zeros_like(dg_ref)
            matmul(dh_even)

        @pl.when((i % 2 == 0) & (i > 0) & (i < n))
        def _():
            norm_bwd(dh_odd)
            matmul(dh_even)

        @pl.when(i % 2 == 1)
        def _():
            norm_bwd(dh_even)
            matmul(dh_odd)

        @pl.when(i == n)
        def _():
            norm_bwd(dh_odd)

    anysp = pl.BlockSpec(memory_space=pl.ANY)
    before = pl.BlockSpec((tm, d), lambda i: (jnp.maximum(i - 1, 0), 0))
    vec = pl.BlockSpec((1, d), lambda i: (0, 0))
    outs, carried = _carried_call(
        body, carry, grid=(n + 1,),
        in_specs=[pl.BlockSpec(memory_space=pltpu.SMEM),
                  pl.BlockSpec((tm, IN_WIDTH), lambda i: (jnp.minimum(i, n - 1), 0)), anysp, anysp, before, vec, before],
        out_specs=[before, vec],
        out_shape=[jax.ShapeDtypeStruct((s, d), F32), jax.ShapeDtypeStruct((1, d), F32)],
        scratch=[pltpu.VMEM((IN_WIDTH, d), BF16), pltpu.SemaphoreType.DMA, pltpu.VMEM((tm, d), F32),
                 pltpu.VMEM((tm, d), F32)],
        operands=(where, dproj, w_near, far, x, g, res), name="d_h", heavy=True)
    return (outs, carried) if carry else outs


def _rmsnorm_fwd(x, g, *, name):
    s, d = x.shape
    ts = min(512, s)

    def body(x_ref, g_ref, o_ref):
        xv = x_ref[...]
        r = lax.rsqrt(jnp.mean(xv * xv, axis=-1, keepdims=True) + EPS)
        o_ref[...] = ((xv * r) * g_ref[...]).astype(BF16)

    return pl.pallas_call(
        body, grid=(s // ts,),
        in_specs=[pl.BlockSpec((ts, d), lambda i: (i, 0)), pl.BlockSpec((1, d), lambda i: (0, 0))],
        out_specs=pl.BlockSpec((ts, d), lambda i: (i, 0)),
        out_shape=jax.ShapeDtypeStruct((s, d), BF16), name=name, compiler_params=_params())(x, g)


def _rmsnorm_bwd(dh, x, g, res, *, name, carry=None):
    s, d = x.shape
    ts = min(256, s)

    def body(ins, outs, scr):
        dh_ref, x_ref, g_ref, res_ref = ins
        dx_ref, dg_ref = outs
        dx, part = _norm_bwd_tile(dh_ref[...], x_ref[...], g_ref[...], res_ref[...])

        @pl.when(pl.program_id(0) == 0)
        def _():
            dg_ref[...] = part

        @pl.when(pl.program_id(0) > 0)
        def _():
            dg_ref[...] += part

        dx_ref[...] = dx

    row = pl.BlockSpec((ts, d), lambda i: (i, 0))
    vec = pl.BlockSpec((1, d), lambda i: (0, 0))
    outs, carried = _carried_call(
        body, carry, grid=(s // ts,), in_specs=[row, row, vec, row], out_specs=[row, vec],
        out_shape=[jax.ShapeDtypeStruct((s, d), F32), jax.ShapeDtypeStruct((1, d), F32)],
        scratch=[], operands=(dh, x, g, res), name=name)
    return (*outs, carried) if carry else tuple(outs)


MID_TILE = 256


def _gated_branches(y_refs, wup_ref, gl):
    d = D_MODEL
    us = [jnp.dot(y_refs[k][...], wup_ref[k], preferred_element_type=F32) for k in range(3)]
    sg = [_sigmoid(gl[:, k * d:(k + 1) * d]) for k in range(3)]
    return us, sg


def _mid_fwd(ya, yb, ym, proj, x, tgt, w_up, w_out, g_post):
    s, d = x.shape
    ts = MID_TILE

    def body(ya_ref, yb_ref, ym_ref, g_ref, x_ref, t_ref, wup_ref, wout_ref, gp_ref,
             m_ref, do_ref, dy_ref, dg_ref, loss_ref):
        us, sg = _gated_branches((ya_ref, yb_ref, ym_ref), wup_ref, g_ref[...])
        merged = (sg[0] * us[0] + sg[1] * us[1] + sg[2] * us[2]).astype(BF16)
        m_ref[...] = merged
        ov = jnp.dot(merged, wout_ref[...], preferred_element_type=F32)
        r = lax.rsqrt(jnp.mean(ov * ov, axis=-1, keepdims=True) + EPS)
        nh = ov * r
        gv = gp_ref[...]
        e = (x_ref[...] + nh * gv) - t_ref[...]
        lpart = 0.5 * jnp.sum(jnp.mean(e * e, axis=-1, keepdims=True), axis=0, keepdims=True)
        dy = e * (1.0 / d)
        dgp = jnp.sum(dy * nh, axis=0, keepdims=True)

        @pl.when(pl.program_id(0) == 0)
        def _():
            dg_ref[...] = dgp
            loss_ref[...] = jnp.broadcast_to(lpart, loss_ref.shape)

        @pl.when(pl.program_id(0) > 0)
        def _():
            dg_ref[...] += dgp
            loss_ref[...] += jnp.broadcast_to(lpart, loss_ref.shape)

        dn = dy * gv
        dy_ref[...] = dy
        do_ref[...] = (r * (dn - nh * jnp.mean(dn * nh, axis=-1, keepdims=True))).astype(BF16)

    row = pl.BlockSpec((ts, d), lambda i: (i, 0))
    ysp = pl.BlockSpec((ts, A_WIDTH), lambda i: (i, 0))
    vec = pl.BlockSpec((1, d), lambda i: (0, 0))
    return pl.pallas_call(
        body, grid=(s // ts,),
        in_specs=[ysp, ysp, ysp, pl.BlockSpec((ts, W_G), lambda i: (i, COL_G)), row, row,
                  pl.BlockSpec((3, A_WIDTH, d), lambda i: (0, 0, 0)), pl.BlockSpec((d, d), lambda i: (0, 0)), vec],
        out_specs=[row, row, row, vec, pl.BlockSpec((1, LANES), lambda i: (0, 0))],
        out_shape=[jax.ShapeDtypeStruct((s, d), BF16), jax.ShapeDtypeStruct((s, d), BF16),
                   jax.ShapeDtypeStruct((s, d), F32), jax.ShapeDtypeStruct((1, d), F32),
                   jax.ShapeDtypeStruct((1, LANES), F32)],
        name="mid_fwd", compiler_params=_params(heavy=True))(ya, yb, ym, proj, x, tgt, w_up, w_out, g_post)


def _mid_bwd(d_out, merged, ya, yb, ym, proj, w_up, w_out):
    s, d = merged.shape
    ts = MID_TILE
    last = s // ts - 1

    def body(do_ref, m_ref, ya_ref, yb_ref, ym_ref, g_ref, wup_ref, wout_ref,
             dp_ref, dya_ref, dyb_ref, dym_ref, dwup_hbm, dwout_hbm, dwup_acc, dwout_acc, dwup_cast, dwout_cast):
        i = pl.program_id(0)

        @pl.when(i == 0)
        def _():
            dwup_acc[...] = jnp.zeros_like(dwup_acc)
            dwout_acc[...] = jnp.zeros_like(dwout_acc)

        y_refs = (ya_ref, yb_ref, ym_ref)
        us, sg = _gated_branches(y_refs, wup_ref, g_ref[...])
        dov = do_ref[...]
        dwout_acc[...] += lax.dot_general(m_ref[...], dov, _DIMS["tn"], preferred_element_type=F32)
        dm = lax.dot_general(dov, wout_ref[...], _DIMS["nt"], preferred_element_type=F32)
        for k, dy_ref in enumerate((dya_ref, dyb_ref, dym_ref)):
            dp_ref[:, k * d:(k + 1) * d] = ((dm * us[k]) * (sg[k] * (1.0 - sg[k]))).astype(BF16)
            du = (sg[k] * dm).astype(BF16)
            dy_ref[...] = lax.dot_general(du, wup_ref[k], _DIMS["nt"], preferred_element_type=F32)
            dwup_acc[k] += lax.dot_general(y_refs[k][...], du, _DIMS["tn"], preferred_element_type=F32)

        @pl.when(i == last)
        def _():
            dwup_cast[...] = dwup_acc[...].astype(BF16)
            dwout_cast[...] = dwout_acc[...].astype(BF16)
            pltpu.sync_copy(dwup_cast, dwup_hbm)
            pltpu.sync_copy(dwout_cast, dwout_hbm)

    row = pl.BlockSpec((ts, d), lambda i: (i, 0))
    ysp = pl.BlockSpec((ts, A_WIDTH), lambda i: (i, 0))
    gsp = pl.BlockSpec((ts, W_G), lambda i: (i, COL_G))
    anysp = pl.BlockSpec(memory_space=pl.ANY)
    yshape = jax.ShapeDtypeStruct((s, A_WIDTH), F32)
    return pl.pallas_call(
        body, grid=(s // ts,),
        in_specs=[row, row, ysp, ysp, ysp, gsp, pl.BlockSpec((3, A_WIDTH, d), lambda i: (0, 0, 0)),
                  pl.BlockSpec((d, d), lambda i: (0, 0))],
        out_specs=[gsp, ysp, ysp, ysp, anysp, anysp],
        out_shape=[jax.ShapeDtypeStruct((s, IN_WIDTH), BF16), yshape, yshape, yshape,
                   jax.ShapeDtypeStruct((3, A_WIDTH, d), BF16), jax.ShapeDtypeStruct((d, d), BF16)],
        scratch_shapes=[pltpu.VMEM((3, A_WIDTH, d), F32), pltpu.VMEM((d, d), F32),
                        pltpu.VMEM((3, A_WIDTH, d), BF16), pltpu.VMEM((d, d), BF16)],
        name="mid_bwd", compiler_params=_params(heavy=True))(d_out, merged, ya, yb, ym, proj, w_up, w_out)


def _conv_core(blk, prev, nxt, w, i, last, ts):
    c = A_WIDTH
    ab, ac, ax, az = blk[:, :c], blk[:, c:2 * c], blk[:, 2 * c:3 * c], blk[:, 3 * c:]
    cu = ac * ax
    cu_prev = (prev[7:8, c:2 * c] * prev[7:8, 2 * c:3 * c]) * jnp.where(i > 0, 1.0, 0.0)
    cu_next = (nxt[0:1, c:2 * c] * nxt[0:1, 2 * c:3 * c]) * jnp.where(i < last, 1.0, 0.0)
    row = lax.broadcasted_iota(jnp.int32, (ts, c), 0)
    cm1 = jnp.where(row == 0, cu_prev, pltpu.roll(cu, 1, 0))
    cp1 = jnp.where(row == ts - 1, cu_next, pltpu.roll(cu, ts - 1, 0))
    yc = cm1 * w[0:1] + cu * w[1:2] + cp1 * w[2:3]
    return ab, ac, ax, az, cu, cm1, cp1, yc, row


def _halo_specs(ts, width, col, nblk8):
    prev = pl.BlockSpec((8, width), lambda i: (jnp.maximum(i * (ts // 8) - 1, 0), col))
    nxt = pl.BlockSpec((8, width), lambda i: (jnp.minimum((i + 1) * (ts // 8), nblk8 - 1), col))
    return prev, nxt


def _conv_fwd(proj, w_conv):
    s = proj.shape[0]
    ts = 256
    last = s // ts - 1

    def body(a_ref, ap_ref, an_ref, w_ref, ya_ref):
        i = pl.program_id(0)
        ab, _, _, az, _, _, _, yc, _ = _conv_core(a_ref[...], ap_ref[...], an_ref[...], w_ref[...], i, last, ts)
        ya_ref[...] = ((ab * yc) * (az * _sigmoid(az))).astype(BF16)

    prev, nxt = _halo_specs(ts, W_A, COL_A, s // 8)
    return pl.pallas_call(
        body, grid=(s // ts,),
        in_specs=[pl.BlockSpec((ts, W_A), lambda i: (i, COL_A)), prev, nxt,
                  pl.BlockSpec((3, A_WIDTH), lambda i: (0, 0))],
        out_specs=pl.BlockSpec((ts, A_WIDTH), lambda i: (i, 0)),
        out_shape=jax.ShapeDtypeStruct((s, A_WIDTH), BF16), name="conv_fwd",
        compiler_params=_params())(proj, proj, proj, w_conv)


def _conv_bwd(proj, w_conv, dya, dproj):
    s = proj.shape[0]
    ts = 256
    last = s // ts - 1
    c = A_WIDTH

    def body(a_ref, ap_ref, an_ref, w_ref, d_ref, dp_ref, dn_ref, _, dproj_ref, dw_ref):
        i = pl.program_id(0)
        w = w_ref[...]
        prev, nxt = ap_ref[...], an_ref[...]
        ab, ac, ax, az, cu, cm1, cp1, yc, row = _conv_core(a_ref[...], prev, nxt, w, i, last, ts)
        sg = _sigmoid(az)
        sz = az * sg
        dya_v = d_ref[...]
        dyc = dya_v * sz * ab
        dproj_ref[:, :c] = (dya_v * sz * yc).astype(BF16)
        dproj_ref[:, 3 * c:] = (dya_v * (ab * yc) * (sg * (1.0 + az * (1.0 - sg)))).astype(BF16)

        def halo_dyc(a_row, d_row):
            azr = a_row[:, 3 * c:]
            return d_row * (azr * _sigmoid(azr)) * a_row[:, :c]

        dyc_prev = halo_dyc(prev[7:8], dp_ref[...][7:8]) * jnp.where(i > 0, 1.0, 0.0)
        dyc_next = halo_dyc(nxt[0:1], dn_ref[...][0:1]) * jnp.where(i < last, 1.0, 0.0)
        dyc_m1 = jnp.where(row == 0, dyc_prev, pltpu.roll(dyc, 1, 0))
        dyc_p1 = jnp.where(row == ts - 1, dyc_next, pltpu.roll(dyc, ts - 1, 0))
        dcu = dyc_p1 * w[0:1] + dyc * w[1:2] + dyc_m1 * w[2:3]
        dproj_ref[:, c:2 * c] = (dcu * ax).astype(BF16)
        dproj_ref[:, 2 * c:3 * c] = (dcu * ac).astype(BF16)
        dw = [jnp.sum(dyc * t, axis=0, keepdims=True) for t in (cm1, cu, cp1)]

        @pl.when(i == 0)
        def _():
            for k in range(3):
                dw_ref[k:k + 1, :] = dw[k]

        @pl.when(i > 0)
        def _():
            for k in range(3):
                dw_ref[k:k + 1, :] += dw[k]

    prev, nxt = _halo_specs(ts, W_A, COL_A, s // 8)
    dprev, dnxt = _halo_specs(ts, A_WIDTH, 0, s // 8)
    return pl.pallas_call(
        body, grid=(s // ts,),
        in_specs=[pl.BlockSpec((ts, W_A), lambda i: (i, COL_A)), prev, nxt,
                  pl.BlockSpec((3, A_WIDTH), lambda i: (0, 0)),
                  pl.BlockSpec((ts, A_WIDTH), lambda i: (i, 0)), dprev, dnxt,
                  pl.BlockSpec(memory_space=pl.ANY)],
        out_specs=[pl.BlockSpec((ts, W_A), lambda i: (i, COL_A)), pl.BlockSpec((3, A_WIDTH), lambda i: (0, 0))],
        out_shape=[jax.ShapeDtypeStruct(dproj.shape, BF16), jax.ShapeDtypeStruct((3, A_WIDTH), F32)],
        input_output_aliases={7: 0}, name="conv_bwd",
        compiler_params=_params())(proj, proj, proj, w_conv, dya, dya, dya, dproj)


def _rope_tables(s):
    half = ROT_DIM // 2
    dim = jnp.arange(LANES) % HEAD_DIM
    inv_freq = jnp.power(jnp.float32(ROPE_THETA), -(dim % half).astype(F32) * (2.0 / ROT_DIM))
    coarse = (jnp.arange(s // LANES) * LANES).astype(F32)[:, None] * inv_freq[None, :]
    fine = jnp.arange(LANES).astype(F32)[:, None] * inv_freq[None, :]
    cos_a, sin_a = jnp.cos(coarse)[:, None, :], jnp.sin(coarse)[:, None, :]
    cos_b, sin_b = jnp.cos(fine)[None], jnp.sin(fine)[None]
    cos = (cos_a * cos_b - sin_a * sin_b).reshape(s, LANES)
    sin = (sin_a * cos_b + cos_a * sin_b).reshape(s, LANES)
    first, second = (dim < half)[None, :], ((dim >= half) & (dim < ROT_DIM))[None, :]
    c = jnp.where(first | second, cos, 1.0)
    s1 = jnp.where(first, -sin, 0.0)
    s2 = jnp.where(second, sin, 0.0)
    return jnp.concatenate([c, s1, s2], axis=1)


def _rope(t, tab):
    return (t * tab[:, :LANES] + pltpu.roll(t, LANES - 8, 1) * tab[:, LANES:2 * LANES]
            + pltpu.roll(t, 8, 1) * tab[:, 2 * LANES:])


def _rope_transpose(dt, tab):
    return (dt * tab[:, :LANES] + pltpu.roll(dt * tab[:, LANES:2 * LANES], 8, 1)
            + pltpu.roll(dt * tab[:, 2 * LANES:], LANES - 8, 1))


def _rope_kv(proj, tab):
    s = proj.shape[0]
    nb = s // KV_PAD

    def body(kv_ref, t_ref, k_ref, v_ref):
        j = pl.program_id(0)
        inside = jnp.where((j > 0) & (j <= nb), 1.0, 0.0)
        kv = kv_ref[...]
        k_ref[...] = (_rope(kv[:, :LANES], t_ref[...]) * inside).astype(BF16)
        v_ref[...] = (kv[:, LANES:] * inside).astype(BF16)

    def src(j):
        return jnp.clip(j - 1, 0, nb - 1)

    o_spec = pl.BlockSpec((KV_PAD, LANES), lambda j: (j, 0))
    shp = jax.ShapeDtypeStruct((s + 2 * KV_PAD, LANES), BF16)
    return pl.pallas_call(
        body, grid=(nb + 2,),
        in_specs=[pl.BlockSpec((KV_PAD, W_KV), lambda j: (src(j), COL_KV)),
                  pl.BlockSpec((KV_PAD, 3 * LANES), lambda j: (src(j), 0))],
        out_specs=[o_spec, o_spec], out_shape=[shp, shp], name="rope_kv",
        compiler_params=_params())(proj, tab)


def _rope_kv_bwd(dkpad, dvpad, tab, dproj):
    s = tab.shape[0]
    nb = s // KV_PAD

    def body(dk_ref, dv_ref, t_ref, _, dp_ref):
        dp_ref[:, :LANES] = _rope_transpose(dk_ref[...], t_ref[...]).astype(BF16)
        dp_ref[:, LANES:] = dv_ref[...].astype(BF16)

    pad_spec = pl.BlockSpec((KV_PAD, LANES), lambda j: (j + 1, 0))
    return pl.pallas_call(
        body, grid=(nb,),
        in_specs=[pad_spec, pad_spec, pl.BlockSpec((KV_PAD, 3 * LANES), lambda j: (j, 0)),
                  pl.BlockSpec(memory_space=pl.ANY)],
        out_specs=pl.BlockSpec((KV_PAD, W_KV), lambda j: (j, COL_KV)),
        out_shape=jax.ShapeDtypeStruct(dproj.shape, BF16), input_output_aliases={3: 0},
        name="rope_kv_bwd", compiler_params=_params())(dkpad, dvpad, tab, dproj)


def _window_start(n):
    return pl.multiple_of((n - 1) * WINDOW_BLOCK + KV_PAD, WINDOW_BLOCK)


def _window_operands(k_ref, v_ref, n, lo):
    start = _window_start(n)
    kw = k_ref[pl.ds(start, 3 * WINDOW_BLOCK), :].astype(F32)
    vw = v_ref[pl.ds(start, 3 * WINDOW_BLOCK), :].astype(F32)
    kr, vr = pltpu.roll(kw, HALF_LANES, 1), pltpu.roll(vw, HALF_LANES, 1)
    k2 = (jnp.where(lo, kw, kr).astype(BF16), jnp.where(lo, kr, kw).astype(BF16))
    v2 = (jnp.where(lo, vw, vr).astype(BF16), jnp.where(lo, vr, vw).astype(BF16))
    return k2, v2


HEADS_PER_GROUP = 4
SWA_FWD_BLOCKS = 1
SWA_BWD_BLOCKS = 2


def _window_bias():
    wb = WINDOW_BLOCK
    qi = lax.broadcasted_iota(jnp.int32, (wb, 3 * wb), 0)
    kj = lax.broadcasted_iota(jnp.int32, (wb, 3 * wb), 1)
    band = (kj >= qi) & (kj <= qi + 2 * wb)
    cases = jnp.stack([band & (kj >= wb), band, band & (kj < 2 * wb)])
    return jnp.where(cases, 0.0, -jnp.inf).astype(F32)


def _block_bias(bias_ref, n, n_blocks):
    case = jnp.where(n == 0, 0, jnp.where(n == n_blocks - 1, 2, 1))
    one = bias_ref[case]
    return jnp.concatenate([one] * HEADS_PER_GROUP, axis=0)


def _stack_heads(pair0, pair1, lo):
    return jnp.concatenate([jnp.where(lo, pair0, 0.0), jnp.where(lo, 0.0, pair0),
                            jnp.where(lo, pair1, 0.0), jnp.where(lo, 0.0, pair1)], axis=0)


def _unstack_pair(stacked, i, lo):
    wb = WINDOW_BLOCK
    return jnp.where(lo, stacked[2 * i * wb:(2 * i + 1) * wb], stacked[(2 * i + 1) * wb:(2 * i + 2) * wb])


def _sink_column(sink_ref, g):
    wb = WINDOW_BLOCK
    return jnp.concatenate([jnp.full((wb, 1), sink_ref[0, HEADS_PER_GROUP * g + i], F32)
                            for i in range(HEADS_PER_GROUP)], axis=0)


def _head_exp(q4, k2g, bias, sink):
    sc = lax.dot_general(q4, k2g, _DIMS["nt"], preferred_element_type=F32) * (HEAD_DIM ** -0.5) + bias
    m = jnp.maximum(jnp.max(sc, axis=1, keepdims=True), sink)
    return jnp.exp(sc - m).astype(BF16), jnp.exp(sink - m)


def _swa_fwd(proj, kpad, vpad, tab, bias, sink, *, carry=None):
    s = proj.shape[0]
    wb = WINDOW_BLOCK

    def body(b_ref, k_ref, v_ref, t_ref, bias_ref, sink_ref, o_ref, y_ref):
        lo = lax.broadcasted_iota(jnp.int32, (wb, LANES), 1) < HALF_LANES
        lo_w = lax.broadcasted_iota(jnp.int32, (3 * wb, LANES), 1) < HALF_LANES
        for sub in range(SWA_FWD_BLOCKS):
            n = pl.program_id(0) * SWA_FWD_BLOCKS + sub
            rows = slice(sub * wb, (sub + 1) * wb)
            k2, v2 = _window_operands(k_ref, v_ref, n, lo_w)
            valid = _block_bias(bias_ref, n, s // wb)
            tab_v = t_ref[rows, :]
            ones = jnp.ones((3 * wb, LANES), BF16)
            for g in range(2):
                qr = [_rope(b_ref[rows, (2 * g + i) * LANES:(2 * g + i + 1) * LANES], tab_v) for i in range(2)]
                q4 = _stack_heads(qr[0], qr[1], lo).astype(BF16)
                e, es = _head_exp(q4, k2[g], valid, _sink_column(sink_ref, g))
                ox = jnp.dot(e, jnp.concatenate([v2[g], ones], axis=1), preferred_element_type=F32)
                o4 = ox[:, :LANES] * (1.0 / (ox[:, LANES:] + es))
                for i in range(2):
                    cols = slice((2 * g + i) * LANES, (2 * g + i + 1) * LANES)
                    op = _unstack_pair(o4, i, lo)
                    o_ref[rows, cols] = op
                    zp = b_ref[rows, A_WIDTH + cols.start:A_WIDTH + cols.stop]
                    y_ref[rows, cols] = (op * (zp * _sigmoid(zp))).astype(BF16)

    tq = SWA_FWD_BLOCKS * wb
    pad_spec = pl.BlockSpec((s + 2 * KV_PAD, LANES), lambda n: (0, 0))
    o_spec = pl.BlockSpec((tq, A_WIDTH), lambda n: (n, 0))
    outs, carried = _carried_call(
        lambda ins, outs, scr: body(*ins, *outs), carry, grid=(s // tq,),
        in_specs=[pl.BlockSpec((tq, W_B), lambda n: (n, COL_B)), pad_spec, pad_spec,
                  pl.BlockSpec((tq, 3 * LANES), lambda n: (n, 0)),
                  pl.BlockSpec(bias.shape, lambda n: (0, 0, 0)), pl.BlockSpec(memory_space=pltpu.SMEM)],
        out_specs=[o_spec, o_spec],
        out_shape=[jax.ShapeDtypeStruct((s, A_WIDTH), F32), jax.ShapeDtypeStruct((s, A_WIDTH), BF16)],
        scratch=[], operands=(proj, kpad, vpad, tab, bias, sink), name="swa_fwd")
    return (*outs, carried) if carry else tuple(outs)


def _swa_bwd(proj, kpad, vpad, tab, bias, sink, o_attn, dyb, dproj):
    s = proj.shape[0]
    wb = WINDOW_BLOCK
    scale = HEAD_DIM ** -0.5

    def body(b_ref, k_ref, v_ref, t_ref, bias_ref, sink_ref, o_ref, dy_ref, _, dp_ref, dk_ref, dv_ref, ds_ref):
        @pl.when(pl.program_id(0) == 0)
        def _():
            dk_ref[...] = jnp.zeros_like(dk_ref)
            dv_ref[...] = jnp.zeros_like(dv_ref)
            ds_ref[...] = jnp.zeros_like(ds_ref)

        lo = lax.broadcasted_iota(jnp.int32, (wb, LANES), 1) < HALF_LANES
        lo_w = lax.broadcasted_iota(jnp.int32, (3 * wb, LANES), 1) < HALF_LANES
        for sub in range(SWA_BWD_BLOCKS):
            n = pl.program_id(0) * SWA_BWD_BLOCKS + sub
            rows = slice(sub * wb, (sub + 1) * wb)
            k2, v2 = _window_operands(k_ref, v_ref, n, lo_w)
            valid = _block_bias(bias_ref, n, s // wb)
            tab_v = t_ref[rows, :]
            ones = jnp.ones((3 * wb, LANES), BF16)
            dks, dvs = [], []
            for g in range(2):
                qr, op, do = [], [], []
                for i in range(2):
                    cols = slice((2 * g + i) * LANES, (2 * g + i + 1) * LANES)
                    zcols = slice(A_WIDTH + cols.start, A_WIDTH + cols.stop)
                    qr.append(_rope(b_ref[rows, cols], tab_v))
                    zp = b_ref[rows, zcols]
                    sg = _sigmoid(zp)
                    op.append(o_ref[rows, cols])
                    dyp = dy_ref[rows, cols]
                    do.append(dyp * (zp * sg))
                    dp_ref[rows, zcols] = (dyp * op[i] * (sg * (1.0 + zp * (1.0 - sg)))).astype(BF16)
                q4 = _stack_heads(qr[0], qr[1], lo).astype(BF16)
                do4 = _stack_heads(do[0], do[1], lo)
                o4 = jnp.concatenate([op[0], op[0], op[1], op[1]], axis=0)
                e, es = _head_exp(q4, k2[g], valid, _sink_column(sink_ref, g))
                inv = 1.0 / (jnp.dot(e, ones, preferred_element_type=F32) + es)
                prob = e.astype(F32) * jnp.concatenate([inv, inv, inv], axis=1)
                delta = jnp.sum(do4 * o4, axis=1, keepdims=True)
                do4b = do4.astype(BF16)
                dprob = lax.dot_general(do4b, v2[g], _DIMS["nt"], preferred_element_type=F32)
                dsc = (prob * (dprob - delta)).astype(BF16)
                sink_terms = (es * inv[:, :1]) * delta
                for i in range(HEADS_PER_GROUP):
                    h = HEADS_PER_GROUP * g + i
                    dsink = -jnp.sum(sink_terms[i * wb:(i + 1) * wb], axis=0, keepdims=True)
                    ds_ref[h:h + 1, :] += jnp.broadcast_to(dsink, (1, LANES))
                dq4 = jnp.dot(dsc, k2[g], preferred_element_type=F32) * scale
                for i in range(2):
                    cols = slice((2 * g + i) * LANES, (2 * g + i + 1) * LANES)
                    dp_ref[rows, cols] = _rope_transpose(_unstack_pair(dq4, i, lo), tab_v).astype(BF16)
                dk2 = lax.dot_general(dsc, q4, _DIMS["tn"], preferred_element_type=F32) * scale
                dv2 = lax.dot_general(prob.astype(BF16), do4b, _DIMS["tn"], preferred_element_type=F32)
                dks.append(dk2 + pltpu.roll(dk2, HALF_LANES, 1))
                dvs.append(dv2 + pltpu.roll(dv2, HALF_LANES, 1))
            start = _window_start(n)
            dk_ref[pl.ds(start, 3 * wb), :] += jnp.where(lo_w, dks[0], dks[1])
            dv_ref[pl.ds(start, 3 * wb), :] += jnp.where(lo_w, dvs[0], dvs[1])

    tq = SWA_BWD_BLOCKS * wb
    pad_spec = pl.BlockSpec((s + 2 * KV_PAD, LANES), lambda n: (0, 0))
    blk = pl.BlockSpec((tq, A_WIDTH), lambda n: (n, 0))
    bsp = pl.BlockSpec((tq, W_B), lambda n: (n, COL_B))
    pad_shape = jax.ShapeDtypeStruct((s + 2 * KV_PAD, LANES), F32)
    return pl.pallas_call(
        body, grid=(s // tq,),
        in_specs=[bsp, pad_spec, pad_spec, pl.BlockSpec((tq, 3 * LANES), lambda n: (n, 0)),
                  pl.BlockSpec(bias.shape, lambda n: (0, 0, 0)), pl.BlockSpec(memory_space=pltpu.SMEM), blk, blk,
                  pl.BlockSpec(memory_space=pl.ANY)],
        out_specs=[bsp, pad_spec, pad_spec, pl.BlockSpec((8, LANES), lambda n: (0, 0))],
        out_shape=[jax.ShapeDtypeStruct(dproj.shape, BF16), pad_shape, pad_shape,
                   jax.ShapeDtypeStruct((8, LANES), F32)],
        input_output_aliases={8: 0}, name="swa_bwd",
        compiler_params=_params())(proj, kpad, vpad, tab, bias, sink, o_attn, dyb, dproj)


def _mem_exp(qh, mk):
    sc = lax.dot_general(qh, mk, _DIMS["nt"], preferred_element_type=F32) * (MEM_HEAD_DIM ** -0.5)
    return jnp.exp(sc - jnp.max(sc, axis=1, keepdims=True)).astype(BF16)


def _mem_fwd(proj, mkv):
    s = proj.shape[0]
    ts = 512
    mlen = mkv.shape[0]

    def body(m_ref, kv_ref, o_ref, y_ref):
        ones = jnp.ones((mlen, LANES), BF16)
        for h in range(MEM_HEADS):
            cols = slice(h * LANES, (h + 1) * LANES)
            mk = kv_ref[:, cols].astype(BF16)
            mv = kv_ref[:, MEM_WIDTH + h * LANES:MEM_WIDTH + (h + 1) * LANES].astype(BF16)
            e = _mem_exp(m_ref[:, cols].astype(BF16), mk)
            ox = jnp.dot(e, jnp.concatenate([mv, ones], axis=1), preferred_element_type=F32)
            oh = ox[:, :LANES] * (1.0 / ox[:, LANES:])
            o_ref[:, cols] = oh
            zh = m_ref[:, MEM_WIDTH + h * LANES:MEM_WIDTH + (h + 1) * LANES]
            y_ref[:, cols] = (oh * (zh * _sigmoid(zh))).astype(BF16)

    o_spec = pl.BlockSpec((ts, MEM_WIDTH), lambda i: (i, 0))
    return pl.pallas_call(
        body, grid=(s // ts,),
        in_specs=[pl.BlockSpec((ts, W_M), lambda i: (i, COL_M)),
                  pl.BlockSpec((mlen, 2 * MEM_WIDTH), lambda i: (0, 0))],
        out_specs=[o_spec, o_spec],
        out_shape=[jax.ShapeDtypeStruct((s, MEM_WIDTH), F32), jax.ShapeDtypeStruct((s, MEM_WIDTH), BF16)],
        name="mem_fwd", compiler_params=_params())(proj, mkv)


def _mem_bwd(proj, mkv, o_mem, dym, dproj, *, carry=None):
    s = proj.shape[0]
    ts = 512
    mlen = mkv.shape[0]
    scale = MEM_HEAD_DIM ** -0.5

    def body(m_ref, kv_ref, o_ref, dy_ref, _, dp_ref, dkv_ref):
        @pl.when(pl.program_id(0) == 0)
        def _():
            dkv_ref[...] = jnp.zeros_like(dkv_ref)

        ones = jnp.ones((mlen, LANES), BF16)
        for h in range(MEM_HEADS):
            cols = slice(h * LANES, (h + 1) * LANES)
            vcols = slice(MEM_WIDTH + h * LANES, MEM_WIDTH + (h + 1) * LANES)
            mk = kv_ref[:, cols].astype(BF16)
            mv = kv_ref[:, vcols].astype(BF16)
            qh = m_ref[:, cols].astype(BF16)
            zh = m_ref[:, vcols]
            sg = _sigmoid(zh)
            oh = o_ref[:, cols]
            dyh = dy_ref[:, cols]
            doh = dyh * (zh * sg)
            dp_ref[:, vcols] = (dyh * oh * (sg * (1.0 + zh * (1.0 - sg)))).astype(BF16)
            e = _mem_exp(qh, mk)
            inv = 1.0 / jnp.dot(e, ones, preferred_element_type=F32)
            prob = e.astype(F32) * jnp.concatenate([inv] * (mlen // LANES), axis=1)
            delta = jnp.sum(doh * oh, axis=1, keepdims=True)
            dohb = doh.astype(BF16)
            dprob = lax.dot_general(dohb, mv, _DIMS["nt"], preferred_element_type=F32)
            dsc = (prob * (dprob - delta)).astype(BF16)
            dp_ref[:, cols] = (jnp.dot(dsc, mk, preferred_element_type=F32) * scale).astype(BF16)
            dkv_ref[:, cols] += lax.dot_general(dsc, qh, _DIMS["tn"], preferred_element_type=F32) * scale
            dkv_ref[:, vcols] += lax.dot_general(prob.astype(BF16), dohb, _DIMS["tn"],
                                                 preferred_element_type=F32)

    blk = pl.BlockSpec((ts, MEM_WIDTH), lambda i: (i, 0))
    msp = pl.BlockSpec((ts, W_M), lambda i: (i, COL_M))
    kvsp = pl.BlockSpec((mlen, 2 * MEM_WIDTH), lambda i: (0, 0))
    outs, carried = _carried_call(
        lambda ins, outs, scr: body(*ins, *outs), carry, grid=(s // ts,),
        in_specs=[msp, kvsp, blk, blk, pl.BlockSpec(memory_space=pl.ANY)],
        out_specs=[msp, kvsp],
        out_shape=[jax.ShapeDtypeStruct(dproj.shape, BF16), jax.ShapeDtypeStruct(mkv.shape, F32)],
        scratch=[], operands=(proj, mkv, o_mem, dym, dproj), name="mem_bwd", aliases={4: 0})
    return (*outs, carried) if carry else tuple(outs)


def _forward_backward(x, mem, tgt, proj, w_conv, sink, g_mem, late_weights, g_post, early_exchange, kv_exchange):
    s = x.shape[0]
    tab = _rope_tables(s)
    bias = _window_bias()

    ya = _conv_fwd(proj, w_conv)
    kpad, vpad = _rope_kv(proj, tab)
    o_attn, yb, *arrived = _swa_fwd(proj, kpad, vpad, tab, bias, sink, carry=late_weights[0])
    w_kv, w_up, w_out = late_weights[1](arrived[0] if arrived else None)
    mn = _rmsnorm_fwd(mem, g_mem, name="mem_norm")
    mkv = _matmul(mn, w_kv, mode="nn", out_dtype=F32, tm=256, tn=1024, tk=D_MODEL, name="mem_kv")
    o_mem, ym = _mem_fwd(proj, mkv)
    merged, d_out, dy, dg_post, loss = _mid_fwd(ya, yb, ym, proj, x, tgt, w_up, w_out, g_post)
    dproj, d_ya, d_yb, d_ym, dw_up, dw_out = _mid_bwd(d_out, merged, ya, yb, ym, proj, w_up, w_out)

    dproj, dw_conv = _conv_bwd(proj, w_conv, d_ya, dproj)
    dproj, dkpad, dvpad, dsink = _swa_bwd(proj, kpad, vpad, tab, bias, sink, o_attn, d_yb, dproj)
    dproj = _rope_kv_bwd(dkpad, dvpad, tab, dproj)
    dproj, d_mkv, *early = _mem_bwd(proj, mkv, o_mem, d_ym, dproj, carry=early_exchange(dw_up, dw_out))

    dw_kv = _matmul(mn, d_mkv, mode="tn", out_dtype=BF16, tm=1024, tn=1024, tk=256, name="dw_kv")
    d_mn = _matmul(d_mkv, w_kv, mode="nt", out_dtype=F32, tm=256, tn=1024, tk=D_MODEL, name="d_mn")
    _, dg_mem, *early_kv = _rmsnorm_bwd(d_mn, mem, g_mem, d_mn, name="mem_norm_bwd", carry=kv_exchange(dw_kv))

    return dict(loss=loss, dproj=dproj, dy=dy, w_conv=dw_conv, sink=dsink, g_mem=dg_mem,
                w_kv=dw_kv, w_up=dw_up, w_out=dw_out, g_post=dg_post, early=early[0] if early else None,
                early_kv=early_kv[0] if early_kv else None)


N_DEV = 8


def _position():
    return lax.axis_index("x"), lax.axis_index("y"), lax.axis_index("c")


def _other_chips(x, y):
    return (((1 - x, y), 2 * (1 - x) + y), ((x, 1 - y), 2 * x + (1 - y)), ((1 - x, 1 - y), 2 * (1 - x) + (1 - y)))


def _remote(src, dst, send_sems, recv_sems, k, device):
    return pltpu.make_async_remote_copy(src_ref=src, dst_ref=dst, send_sem=send_sems.at[k], recv_sem=recv_sems.at[k],
                                        device_id=device, device_id_type=MESH)


def _rows_half(ref, hf):
    rh = ref.shape[0] // 2
    return ref.at[pl.ds(pl.multiple_of(hf * rh, 8), rh)]


def _gather_weights(shards, small=None, relations=(0, 1, 2), into=None):
    n = len(shards)
    k = 0 if small is None else 1

    def peers(x, y):
        return [(r, chip, idx) for r, (chip, idx) in enumerate(_other_chips(x, y)) if r in relations]

    def ici(ins, outs, sems, a, r, chip, src_chip, c):
        return _remote(_rows_half(ins[a], c), _rows_half(outs[a].at[src_chip], c), sems[0], sems[1], 3 * a + r,
                       (*chip, c))

    def whole(ins, outs, sems, r, chip, src_chip, c):
        return _remote(ins[n], outs[n].at[src_chip], sems[0], sems[1], 3 * n + r, (*chip, c))

    def d2d(outs, sems, a, r, idx, hf, x, y, c):
        half = _rows_half(outs[a].at[idx], hf)
        return _remote(half, half, sems[2], sems[3], 3 * a + r, (x, y, 1 - c))

    def start(ins, outs, sems):
        x, y, c = _position()
        me = 2 * x + y
        for a in range(n):
            for r, chip, _ in peers(x, y):
                ici(ins, outs, sems, a, r, chip, me, c).start()
        for r, (chip, _) in enumerate(_other_chips(x, y)):
            if k:
                whole(ins, outs, sems, r, chip, me, c).start()

    def finish(ins, outs, sems):
        x, y, c = _position()
        me = 2 * x + y
        for a in range(n):
            for r, chip, idx in peers(x, y):
                ici(ins, outs, sems, a, r, chip, idx, c).wait_recv()
                d2d(outs, sems, a, r, idx, c, x, y, c).start()
        for a in range(n):
            for r, chip, idx in peers(x, y):
                d2d(outs, sems, a, r, idx, 1 - c, x, y, c).wait_recv()
        for r, (chip, idx) in enumerate(_other_chips(x, y)):
            if k:
                whole(ins, outs, sems, r, chip, idx, c).wait_recv()
                whole(ins, outs, sems, r, chip, me, c).wait_send()
        for a in range(n):
            for r, chip, idx in peers(x, y):
                ici(ins, outs, sems, a, r, chip, me, c).wait_send()
                d2d(outs, sems, a, r, idx, c, x, y, c).wait_send()

    operands = list(shards) + ([small] if k else [])
    shapes = [jax.ShapeDtypeStruct((N_CHIPS,) + s.shape, s.dtype) for s in operands]
    aliases = {}
    if into is not None:
        assert len(into) == len(operands)
        aliases = {len(operands) + a: a for a in range(len(into))}
        operands += list(into)
    return _Carry(operands, shapes,
                  [pltpu.SemaphoreType.DMA((3 * (n + k),)), pltpu.SemaphoreType.DMA((3 * (n + k),)),
                   pltpu.SemaphoreType.DMA((3 * n,)), pltpu.SemaphoreType.DMA((3 * n,))], start, finish, aliases)


def _pair_exchange(send):
    n = len(send)

    def copies(ins, outs, sems):
        x, y, c = _position()
        return [_remote(ins[a], outs[a], sems[0], sems[1], a, (x, y, 1 - c)) for a in range(n)]

    def start(ins, outs, sems):
        for cp in copies(ins, outs, sems):
            cp.start()

    def finish(ins, outs, sems):
        for cp in copies(ins, outs, sems):
            cp.wait()

    return _Carry(send, [jax.ShapeDtypeStruct(p.shape, p.dtype) for p in send],
                  [pltpu.SemaphoreType.DMA((n,)), pltpu.SemaphoreType.DMA((n,))], start, finish)


def _chip_exchange(sums):
    n = len(sums)

    def copies(ins, outs, sems):
        x, y, c = _position()
        return [_remote(ins[a].at[idx], outs[a].at[r], sems[0], sems[1], 3 * a + r, (*chip, c))
                for a in range(n) for r, (chip, idx) in enumerate(_other_chips(x, y))]

    def start(ins, outs, sems):
        for cp in copies(ins, outs, sems):
            cp.start()

    def finish(ins, outs, sems):
        for cp in copies(ins, outs, sems):
            cp.wait()

    return _Carry(sums, [jax.ShapeDtypeStruct((3,) + p.shape[1:], p.dtype) for p in sums],
                  [pltpu.SemaphoreType.DMA((3 * n,)), pltpu.SemaphoreType.DMA((3 * n,))], start, finish)


def _pair_share(pairs):
    n = len(pairs)

    def start(ins, outs, sems):
        x, y, c = _position()
        for a in range(n):
            _remote(outs[a].at[c], outs[a].at[c], sems[0], sems[1], a, (x, y, 1 - c)).start()

    def finish(ins, outs, sems):
        x, y, c = _position()
        for a in range(n):
            _remote(outs[a].at[1 - c], outs[a].at[1 - c], sems[0], sems[1], a, (x, y, 1 - c)).wait_recv()
        for a in range(n):
            _remote(outs[a].at[c], outs[a].at[c], sems[0], sems[1], a, (x, y, 1 - c)).wait_send()

    return _Carry(pairs, [jax.ShapeDtypeStruct(p.shape, p.dtype) for p in pairs],
                  [pltpu.SemaphoreType.DMA((n,)), pltpu.SemaphoreType.DMA((n,))], start, finish,
                  aliases={a: a for a in range(n)})


def _small_allreduce(pack, share):
    rows, width = pack.shape
    n_share = len(share.ins)

    def body(p_ref, *refs):
        share_in, o_ref, share_out = refs[:n_share], refs[n_share], refs[n_share + 1:2 * n_share + 1]
        buf, send_sems, recv_sems = refs[2 * n_share + 1:2 * n_share + 4]
        share_sems = refs[2 * n_share + 4:]
        share.start(share_in, share_out, share_sems)
        x, y, c = _position()
        me = 4 * x + 2 * y + c
        buf[me] = p_ref[...]
        peers = []
        for r in range(1, N_DEV):
            fx, fy, fc = (r >> 2) & 1, (r >> 1) & 1, r & 1
            px, py, pc = (1 - x if fx else x), (1 - y if fy else y), (1 - c if fc else c)
            peers.append(((px, py, pc), 4 * px + 2 * py + pc))
        sends = [_remote(p_ref, buf.at[me], send_sems, recv_sems, r, dev) for r, (dev, _) in enumerate(peers)]
        for cp in sends:
            cp.start()
        for r, (dev, idx) in enumerate(peers):
            _remote(p_ref, buf.at[idx], send_sems, recv_sems, r, dev).wait_recv()
        for cp in sends:
            cp.wait_send()
        acc = buf[0]
        for k in range(1, N_DEV):
            acc = acc + buf[k]
        o_ref[...] = acc
        share.finish(share_in, share_out, share_sems)

    vm = pl.BlockSpec(memory_space=pltpu.VMEM)
    red, *shared = pl.pallas_call(
        body, in_specs=[vm] + [_HBM] * n_share, out_specs=[vm] + [_HBM] * n_share,
        out_shape=[jax.ShapeDtypeStruct(pack.shape, F32)] + share.out_shapes,
        scratch_shapes=[pltpu.VMEM((N_DEV, rows, width), F32), pltpu.SemaphoreType.DMA((N_DEV - 1,)),
                        pltpu.SemaphoreType.DMA((N_DEV - 1,))] + share.sems,
        input_output_aliases={1 + i: 1 + o for i, o in share.aliases.items()},
        name="small_allreduce")(pack, *share.ins)
    return red, shared


ROW_TILE_MAX = 512
SUM_TILE_MAX = 2048
BF16_SUBLANES = 16


def _row_tile(rows, most=ROW_TILE_MAX):
    if rows <= most:
        return rows
    return max(t for t in range(BF16_SUBLANES, most + 1, BF16_SUBLANES) if rows % t == 0)


def _pair_add(keep, recv, name):
    nj, rh, cols = keep.shape
    tr = _row_tile(rh, SUM_TILE_MAX)

    def body(k_ref, r_ref, o_ref):
        o_ref[...] = (k_ref[...].astype(F32) + r_ref[...].astype(F32)).astype(BF16)

    blk = pl.BlockSpec((None, tr, cols), lambda j, i: (j, i, 0))
    return pl.pallas_call(body, grid=(nj, rh // tr), in_specs=[blk, blk], out_specs=blk,
                          out_shape=jax.ShapeDtypeStruct(keep.shape, BF16), name=name,
                          compiler_params=_params())(keep, recv)


def _chip_add(sums, recv, where, name):
    _, rh, cols = sums.shape
    tr = _row_tile(rh, SUM_TILE_MAX)

    def body(w_ref, s_ref, r_ref, o_ref):
        o_ref[...] = ((s_ref[...].astype(F32) + r_ref[0].astype(F32)) + r_ref[1].astype(F32)) + r_ref[2].astype(F32)

    grid_spec = pltpu.PrefetchScalarGridSpec(
        num_scalar_prefetch=1, grid=(rh // tr,),
        in_specs=[pl.BlockSpec((None, tr, cols), lambda i, w_ref: (w_ref[0], i, 0)),
                  pl.BlockSpec((3, tr, cols), lambda i, w_ref: (0, i, 0))],
        out_specs=pl.BlockSpec((None, tr, cols), lambda i, w_ref: (w_ref[1], i, 0)))
    return pl.pallas_call(body, grid_spec=grid_spec, out_shape=jax.ShapeDtypeStruct((2, rh, cols), F32),
                          name=name, compiler_params=_params())(where, sums, recv)


def _adamw(w, g, m, v, name):
    rows, cols = w.shape
    tr = _row_tile(rows)
    assert rows % tr == 0

    def body(w_ref, g_ref, m_ref, v_ref, d_ref, mo_ref, vo_ref):
        gv = g_ref[...]
        m_new = ADAM_B1 * m_ref[...] + (1.0 - ADAM_B1) * gv
        v_new = ADAM_B2 * v_ref[...] + (1.0 - ADAM_B2) * jnp.square(gv)
        m_hat = m_new / (1.0 - ADAM_B1 ** ADAM_STEP)
        v_hat = v_new / (1.0 - ADAM_B2 ** ADAM_STEP)
        d_ref[...] = -ADAM_LR * (m_hat / (jnp.sqrt(v_hat) + ADAM_EPS) + ADAM_WD * w_ref[...])
        mo_ref[...] = m_new
        vo_ref[...] = v_new

    blk = pl.BlockSpec((tr, cols), lambda i: (i, 0))
    shp = jax.ShapeDtypeStruct((rows, cols), F32)
    return pl.pallas_call(body, grid=(rows // tr,), in_specs=[blk] * 4, out_specs=[blk] * 3,
                          out_shape=[shp] * 3, name=name, compiler_params=_params())(w, g, m, v)


def _adamw_halves(w, g2, m, v, name):
    rows, cols = w.shape
    half = cols // 2
    tr = _row_tile(rows)

    def body(w_ref, g_ref, m_ref, v_ref, go_ref, d_ref, mo_ref, vo_ref):
        gv = g_ref[...]
        go_ref[...] = gv
        m_new = ADAM_B1 * m_ref[...] + (1.0 - ADAM_B1) * gv
        v_new = ADAM_B2 * v_ref[...] + (1.0 - ADAM_B2) * jnp.square(gv)
        m_hat = m_new / (1.0 - ADAM_B1 ** ADAM_STEP)
        v_hat = v_new / (1.0 - ADAM_B2 ** ADAM_STEP)
        d_ref[...] = -ADAM_LR * (m_hat / (jnp.sqrt(v_hat) + ADAM_EPS) + ADAM_WD * w_ref[...])
        mo_ref[...] = m_new
        vo_ref[...] = v_new

    blk = pl.BlockSpec((tr, half), lambda hf, i: (i, hf))
    gsp = pl.BlockSpec((None, tr, half), lambda hf, i: (hf, i, 0))
    shp = jax.ShapeDtypeStruct((rows, cols), F32)
    return pl.pallas_call(body, grid=(2, rows // tr), in_specs=[blk, gsp, blk, blk], out_specs=[blk] * 4,
                          out_shape=[shp] * 4, name=name, compiler_params=_params())(w, g2, m, v)


SHARD_W = IN_WIDTH // N_CHIPS


def _half_major(a):
    r, c = a.shape
    return a.reshape(N_CHIPS, 2, r // N_CHIPS // 2, c).transpose(1, 0, 2, 3)


def kernel(x, mem, g_pre, w_in, w_conv, attn_sink, g_mem, w_mem_kv, w_up_a, w_up_b, w_up_m, w_out, g_post, loss_target, m_g_pre, m_w_in, m_w_conv, m_attn_sink, m_g_mem, m_w_mem_kv, m_w_up_a, m_w_up_b, m_w_up_m, m_w_out, m_g_post, v_g_pre, v_w_in, v_w_conv, v_attn_sink, v_g_mem, v_w_mem_kv, v_w_up_a, v_w_up_b, v_w_up_m, v_w_out, v_g_post):
    xi, yi, ci = _position()
    chip = 2 * xi + yi
    where = jnp.stack([chip, ci, N_CHIPS - 1 - chip]).astype(jnp.int32)

    own = [w_in[0].T.astype(BF16), w_mem_kv[0].astype(BF16),
           jnp.concatenate([w_up_a[0], w_up_b[0], w_up_m[0]], axis=0).astype(BF16), w_out[0].astype(BF16)]
    own_conv = jnp.pad(w_conv[0], ((0, 5), (0, 0)))

    def pieces(mine, got):
        got = lax.dynamic_update_slice_in_dim(got, mine[None], chip, axis=0)
        return [got[j] for j in range(N_CHIPS)]

    diag = N_CHIPS - 1 - chip
    proj, h, h_t, got_near, got_conv, got_far = _proj_near(x[0], g_pre, own[0], own_conv, where)
    w_near = lax.dynamic_update_slice_in_dim(got_near, own[0][None], chip, axis=0).reshape(IN_WIDTH, D_MODEL)
    far = lax.dynamic_index_in_dim(got_far, diag, 0, keepdims=False)
    proj = _proj_far(h, w_near, far, where, into=proj)
    w_conv_full = jnp.concatenate([p[:3] for p in pieces(own_conv, got_conv)], axis=1)

    def late_weights(gathered):
        w_kv_full = jnp.concatenate(pieces(own[1], gathered[0]), axis=0)
        up_pieces = pieces(own[2], gathered[1])
        w_up_full = jnp.stack([jnp.concatenate([p[k * A_WIDTH:(k + 1) * A_WIDTH] for p in up_pieces], axis=1)
                               for k in range(3)])
        return w_kv_full, w_up_full, jnp.concatenate(pieces(own[3], gathered[2]), axis=0)

    def pick(parts, hf):
        return [lax.dynamic_index_in_dim(p, hf, 0, keepdims=False) for p in parts]

    def up_out_parts(dw_up, dw_out):
        up = (dw_up.reshape(3, A_WIDTH, N_CHIPS, D_MODEL // N_CHIPS).transpose(2, 0, 1, 3)
              .reshape(N_CHIPS, 2, 3 * A_WIDTH // 2, D_MODEL // N_CHIPS).transpose(1, 0, 2, 3))
        return [up.astype(BF16), _half_major(dw_out).astype(BF16)]

    g = _forward_backward(x[0], mem[0], loss_target[0], proj, w_conv_full, attn_sink, g_mem,
                          (_gather_weights(own[1:]), late_weights), g_post,
                          lambda dw_up, dw_out: _pair_exchange(pick(up_out_parts(dw_up, dw_out), 1 - ci)),
                          lambda dw_kv: _pair_exchange(pick([_half_major(dw_kv).astype(BF16)], 1 - ci)))

    half_rows = D_MODEL // 2

    def dw_in_half(half_of, name, carry):
        dw, carried = _dw_in_t(g["dproj"], h_t, half_of=half_of, where=where, name=name, carry=carry)
        return dw.reshape(N_CHIPS, SHARD_W, half_rows), carried

    small_keep = pick([_half_major(g["w_kv"]).astype(BF16)] + up_out_parts(g["w_up"], g["w_out"]), ci)
    small_names = ["w_kv", "w_up", "w_out"]
    sums_small = [_pair_add(k, r, "pair_add_" + nm)
                  for k, r, nm in zip(small_keep, g["early_kv"] + g["early"], small_names)]
    dw_send, recv3_small = dw_in_half(lambda w: 1 - w[1], "dw_in_send", _chip_exchange(sums_small))
    dw_keep, (recv_in,) = dw_in_half(lambda w: w[1], "dw_in_keep", _pair_exchange([dw_send]))
    sum_in = _pair_add(dw_keep, recv_in, "pair_add_w_in")
    (grad_x, dg_pre), (recv3_in,) = _d_h(g["dproj"], w_near, far, where, x[0], g_pre, g["dy"],
                                         carry=_chip_exchange([sum_in]))
    pairs = [_chip_add(s, r, where, "chip_add_" + nm)
             for s, r, nm in zip([sum_in] + sums_small, [recv3_in] + recv3_small, ["w_in"] + small_names)]

    zeros512 = jnp.zeros((1, D_MODEL - A_WIDTH), F32)
    conv_rows = [jnp.concatenate([g["w_conv"][k:k + 1], zeros512], axis=1) for k in range(3)]
    sink_row = jnp.pad(g["sink"][:, 0].reshape(1, N_Q_HEADS), ((0, 0), (0, D_MODEL - N_Q_HEADS)))
    loss_row = jnp.pad(g["loss"], ((0, 0), (0, D_MODEL - LANES)))
    pack = jnp.concatenate([dg_pre, g["g_mem"], g["g_post"]] + conv_rows + [sink_row, loss_row], axis=0)
    red, full = _small_allreduce(pack, _pair_share(pairs))
    loss = red[7, 0]
    small_grads = dict(
        g_pre=red[0:1], g_mem=red[1:2], g_post=red[2:3], attn_sink=red[6:7, :N_Q_HEADS],
        w_conv=lax.dynamic_slice(red[3:6, :A_WIDTH], (0, chip * LANES), (3, LANES)))

    gw_up = full[2].reshape(3, A_WIDTH, D_MODEL // N_CHIPS)
    grads = dict(small_grads, w_mem_kv=full[1].reshape(D_MODEL // N_CHIPS, 2 * MEM_WIDTH),
                 w_up_a=gw_up[0], w_up_b=gw_up[1], w_up_m=gw_up[2],
                 w_out=full[3].reshape(D_MODEL // N_CHIPS, D_MODEL))

    weights = dict(g_pre=g_pre, w_in=w_in, w_conv=w_conv, attn_sink=attn_sink, g_mem=g_mem, w_mem_kv=w_mem_kv,
                   w_up_a=w_up_a, w_up_b=w_up_b, w_up_m=w_up_m, w_out=w_out, g_post=g_post)
    m_in = dict(g_pre=m_g_pre, w_in=m_w_in, w_conv=m_w_conv, attn_sink=m_attn_sink, g_mem=m_g_mem,
                w_mem_kv=m_w_mem_kv, w_up_a=m_w_up_a, w_up_b=m_w_up_b, w_up_m=m_w_up_m, w_out=m_w_out,
                g_post=m_g_post)
    v_in = dict(g_pre=v_g_pre, w_in=v_w_in, w_conv=v_w_conv, attn_sink=v_attn_sink, g_mem=v_g_mem,
                w_mem_kv=v_w_mem_kv, w_up_a=v_w_up_a, w_up_b=v_w_up_b, w_up_m=v_w_up_m, w_out=v_w_out,
                g_post=v_g_post)
    out_g, out_d, out_m, out_v = [], [], [], []
    for nm in ("g_pre", "w_in", "w_conv", "attn_sink", "g_mem", "w_mem_kv", "w_up_a", "w_up_b", "w_up_m", "w_out",
               "g_post"):
        shape = weights[nm].shape
        if nm == "w_in":
            results = _adamw_halves(w_in[0].T, full[0], m_w_in[0].T, v_w_in[0].T, "adamw_w_in")
            for out, t in zip((out_g, out_d, out_m, out_v), results):
                out.append(t.T.reshape(shape))
            continue
        two_d = shape[-2:]
        gr = grads[nm].reshape(two_d)
        d, m_new, v_new = _adamw(weights[nm].reshape(two_d), gr, m_in[nm].reshape(two_d), v_in[nm].reshape(two_d),
                                 "adamw_" + nm)
        out_g.append(gr.reshape(shape))
        out_d.append(d.reshape(shape))
        out_m.append(m_new.reshape(shape))
        out_v.append(v_new.reshape(shape))
    return (loss, grad_x.reshape(x.shape), *out_g, *out_d, *out_m, *out_v)
```

```python
import jax
import jax.numpy as jnp
from jax import lax
from jax.experimental import pallas as pl
from jax.experimental.pallas import tpu as pltpu

F32 = jnp.float32
BF16 = jnp.bfloat16
MESH = pl.DeviceIdType.MESH

D_MODEL = 1024
EPS = 1e-6
A_WIDTH = 512
HEAD_DIM = 64
N_Q_HEADS = 8
WINDOW_BLOCK = 128
KV_PAD = 512
ROPE_THETA = 500000.0
ROT_DIM = 16
MEM_HEADS = 4
MEM_HEAD_DIM = 128
MEM_WIDTH = 512
IN_WIDTH = 7424
N_CHIPS = 4
LANES = 128
HALF_LANES = 64

PERM_SEGS = ((0, 2560), (2816, 3328), (4352, 7424), (3328, 4352), (2560, 2816))
COL_A, W_A = 0, 2048
COL_B, W_B = 2, 1024
COL_G, W_G = 1, 3072
COL_M, W_M = 6, 1024
COL_KV, W_KV = 28, 256

ADAM_LR = 0.001
ADAM_B1 = 0.9
ADAM_B2 = 0.999
ADAM_EPS = 1e-08
ADAM_WD = 0.01
ADAM_STEP = 10

VMEM_LIGHT_BYTES = 48 * 1024 * 1024
VMEM_HEAVY_BYTES = 48 * 1024 * 1024


_HBM = pl.BlockSpec(memory_space=pltpu.HBM)


def _params(heavy=False):
    return pltpu.CompilerParams(vmem_limit_bytes=VMEM_HEAVY_BYTES if heavy else VMEM_LIGHT_BYTES)


def _sigmoid(v):
    return jax.nn.sigmoid(v)


_DIMS = {"nn": (((1,), (0,)), ((), ())), "nt": (((1,), (1,)), ((), ())), "tn": (((0,), (0,)), ((), ()))}


class _Carry:
    def __init__(self, ins, out_shapes, sems, start, finish, aliases=None):
        self.ins, self.out_shapes, self.sems = list(ins), list(out_shapes), list(sems)
        self.start, self.finish, self.aliases = start, finish, dict(aliases or {})


def _join(*carries):
    def split(seq, counts):
        pos, parts = 0, []
        for n in counts:
            parts.append(seq[pos:pos + n])
            pos += n
        return parts

    n_in = [len(c.ins) for c in carries]
    n_out = [len(c.out_shapes) for c in carries]
    n_sem = [len(c.sems) for c in carries]

    def run(which):
        def go(ins, outs, sems):
            for c, i, o, sm in zip(carries, split(ins, n_in), split(outs, n_out), split(sems, n_sem)):
                getattr(c, which)(i, o, sm)
        return go

    aliases = {}
    for k, c in enumerate(carries):
        aliases.update({sum(n_in[:k]) + i: sum(n_out[:k]) + o for i, o in c.aliases.items()})
    return _Carry([a for c in carries for a in c.ins], [sh for c in carries for sh in c.out_shapes],
                  [sm for c in carries for sm in c.sems], run("start"), run("finish"), aliases)


def _carried_call(body, carry, *, grid, in_specs, out_specs, out_shape, scratch, operands, name, prefetch=None,
                  aliases=None, heavy=False):
    n_in, n_out, n_scr = len(in_specs), len(out_specs), len(scratch)
    c_in = len(carry.ins) if carry else 0
    c_out = len(carry.out_shapes) if carry else 0
    n_pre = 0 if prefetch is None else 1
    steps = 1
    for g in grid:
        steps *= g

    def wrapped(*refs):
        refs = refs[n_pre:]
        ins, cins = refs[:n_in], refs[n_in:n_in + c_in]
        outs = refs[n_in + c_in:n_in + c_in + n_out]
        couts = refs[n_in + c_in + n_out:n_in + c_in + n_out + c_out]
        rest = refs[n_in + c_in + n_out + c_out:]
        scr, sems = rest[:n_scr], rest[n_scr:]
        if carry:
            step = pl.program_id(0)
            for ax in range(1, len(grid)):
                step = step * grid[ax] + pl.program_id(ax)

            @pl.when(step == 0)
            def _():
                carry.start(cins, couts, sems)

        body(ins, outs, scr)
        if carry:
            @pl.when(step == steps - 1)
            def _():
                carry.finish(cins, couts, sems)

    all_aliases = {n_pre + i: o for i, o in (aliases or {}).items()}
    if carry:
        all_aliases.update({n_pre + n_in + i: n_out + o for i, o in carry.aliases.items()})
    all_in = list(in_specs) + [_HBM] * c_in
    all_out = list(out_specs) + [_HBM] * c_out
    all_scratch = list(scratch) + (carry.sems if carry else [])
    if n_pre:
        spec = dict(grid_spec=pltpu.PrefetchScalarGridSpec(num_scalar_prefetch=1, grid=grid, in_specs=all_in,
                                                           out_specs=all_out, scratch_shapes=all_scratch))
        pre = (prefetch,)
    else:
        spec = dict(grid=grid, in_specs=all_in, out_specs=all_out, scratch_shapes=all_scratch)
        pre = ()
    results = pl.pallas_call(
        wrapped, out_shape=list(out_shape) + (carry.out_shapes if carry else []), input_output_aliases=all_aliases,
        name=name, compiler_params=_params(heavy), **spec)(*pre, *operands, *(carry.ins if carry else []))
    return list(results[:n_out]), list(results[n_out:])


def _matmul(a, b, *, mode, out_dtype, tm, tn, tk, name):
    if mode == "nn":
        (m, k), (_, n) = a.shape, b.shape
    elif mode == "nt":
        (m, k), (n, _) = a.shape, b.shape
    else:
        (k, m), (_, n) = a.shape, b.shape
    tm, tn, tk = min(tm, m), min(tn, n), min(tk, k)
    assert m % tm == 0 and n % tn == 0 and k % tk == 0
    nk = k // tk
    dims = _DIMS[mode]

    if mode == "nn":
        a_spec = pl.BlockSpec((tm, tk), lambda i, j, kk: (i, kk))
        b_spec = pl.BlockSpec((tk, tn), lambda i, j, kk: (kk, j))
    elif mode == "nt":
        a_spec = pl.BlockSpec((tm, tk), lambda i, j, kk: (i, kk))
        b_spec = pl.BlockSpec((tn, tk), lambda i, j, kk: (j, kk))
    else:
        a_spec = pl.BlockSpec((tk, tm), lambda i, j, kk: (kk, i))
        b_spec = pl.BlockSpec((tk, tn), lambda i, j, kk: (kk, j))
    o_spec = pl.BlockSpec((tm, tn), lambda i, j, kk: (i, j))

    def part(a_ref, b_ref):
        return lax.dot_general(a_ref[...].astype(BF16), b_ref[...].astype(BF16), dims,
                               preferred_element_type=F32)

    if nk == 1:
        def body(a_ref, b_ref, o_ref):
            o_ref[...] = part(a_ref, b_ref).astype(out_dtype)
        scratch = []
    else:
        def body(a_ref, b_ref, o_ref, acc_ref):
            kk = pl.program_id(2)

            @pl.when(kk == 0)
            def _():
                acc_ref[...] = part(a_ref, b_ref)

            @pl.when(kk > 0)
            def _():
                acc_ref[...] += part(a_ref, b_ref)

            @pl.when(kk == nk - 1)
            def _():
                o_ref[...] = acc_ref[...].astype(out_dtype)
        scratch = [pltpu.VMEM((tm, tn), F32)]

    return pl.pallas_call(
        body, grid=(m // tm, n // tn, nk), in_specs=[a_spec, b_spec], out_specs=o_spec,
        out_shape=jax.ShapeDtypeStruct((m, n), out_dtype), scratch_shapes=scratch,
        name=name, compiler_params=_params())(a, b)


IN_BLOCK = 256
N_IN_BLOCKS = IN_WIDTH // IN_BLOCK
SHARD_BLOCKS = (IN_WIDTH // N_CHIPS) // IN_BLOCK
BLOCK_RUNS = tuple((a // IN_BLOCK, sum(d - c for c, d in PERM_SEGS[:k]) // IN_BLOCK, (b - a) // IN_BLOCK)
                   for k, (a, b) in enumerate(PERM_SEGS))


def _perm_block(r):
    p = r
    for ref0, perm0, n in BLOCK_RUNS:
        p = jnp.where((r >= ref0) & (r < ref0 + n), r - ref0 + perm0, p)
    return p


def _proj_near(x, g_pre, own_w, small, where):
    s, d = x.shape
    norm_tile = min(512, s)
    n_own = SHARD_BLOCKS - 1
    n_diag = SHARD_BLOCKS + 1
    n_blocks = N_IN_BLOCKS - n_diag
    piece = IN_WIDTH // N_CHIPS - SHARD_BLOCKS * IN_BLOCK
    near = _gather_weights([own_w], small, relations=(0, 1))
    far = _gather_weights([own_w], relations=(2,))
    both = _join(near, far)
    n_cin, n_cout = len(both.ins), len(both.out_shapes)

    def block_of(i, w):
        me, dg = w[0], w[2]
        own0 = SHARD_BLOCKS * me + jnp.minimum(me, 1)
        dg0 = SHARD_BLOCKS * dg
        lo0, hi0 = jnp.minimum(own0, dg0), jnp.maximum(own0, dg0)
        lo_n = jnp.where(own0 < dg0, n_own, n_diag)
        hi_n = jnp.where(own0 < dg0, n_diag, n_own)
        r = i - n_own
        r = r + lo_n * (r >= lo0).astype(jnp.int32)
        r = r + hi_n * (r >= hi0).astype(jnp.int32)
        return jnp.where(i < n_own, own0 + i, r)

    def body(w_ref, x_hbm, g_ref, own_hbm, *refs):
        cins, (o_ref, h_hbm, ht_hbm) = refs[:n_cin], refs[n_cin:n_cin + 3]
        couts = refs[n_cin + 3:n_cin + 3 + n_cout]
        blocks, block_sems, h_ref, x_tile, ht_tile, io_sem = refs[n_cin + 3 + n_cout:n_cin + 9 + n_cout]
        sems = refs[n_cin + 9 + n_cout:]
        near_refs = (cins[:len(near.ins)], couts[:len(near.out_shapes)], sems[:len(near.sems)])
        far_refs = (cins[len(near.ins):], couts[len(near.out_shapes):], sems[len(near.sems):])
        gathered = couts[0]
        i = pl.program_id(0)
        me = w_ref[0]

        def fetch(step, slot):
            r = block_of(step, w_ref)
            for p in range(IN_BLOCK // piece):
                row = r * IN_BLOCK + p * piece
                j = row // (IN_WIDTH // N_CHIPS)
                off = pl.multiple_of(row - j * (IN_WIDTH // N_CHIPS), BF16_SUBLANES)
                dst = blocks.at[slot, pl.ds(p * piece, piece)]

                @pl.when(j == me)
                def _():
                    pltpu.make_async_copy(own_hbm.at[pl.ds(off, piece)], dst, block_sems.at[slot]).start()

                @pl.when(j != me)
                def _():
                    pltpu.make_async_copy(gathered.at[j, pl.ds(off, piece)], dst, block_sems.at[slot]).start()

        def arrived(slot):
            pltpu.make_async_copy(own_hbm.at[pl.ds(0, IN_BLOCK)], blocks.at[slot], block_sems.at[slot]).wait()

        slot = i % 2

        def norm_rows(k):
            rows = pl.ds(k * norm_tile, norm_tile)
            pltpu.sync_copy(x_hbm.at[rows], x_tile)
            xv = x_tile[...]
            hv = (xv * lax.rsqrt(jnp.mean(xv * xv, axis=-1, keepdims=True) + EPS)) * g_ref[...]
            h_ref[rows, :] = hv.astype(BF16)
            ht_tile[...] = hv.T.astype(BF16)
            to_h = pltpu.make_async_copy(h_ref.at[rows], h_hbm.at[rows], io_sem.at[0])
            to_ht = pltpu.make_async_copy(ht_tile, ht_hbm.at[:, rows], io_sem.at[1])
            to_h.start()
            to_ht.start()
            to_h.wait()
            to_ht.wait()

        @pl.when(i == 0)
        def _():
            near.start(*near_refs)
            fetch(i, slot)
            for k in range(s // norm_tile):
                norm_rows(k)

        @pl.when(i == n_own)
        def _():
            near.finish(*near_refs)
            far.start(*far_refs)
            fetch(i, slot)

        arrived(slot)

        @pl.when((i + 1 < n_blocks) & (i + 1 != n_own))
        def _():
            fetch(i + 1, 1 - slot)

        o_ref[...] = lax.dot_general(h_ref[...], blocks[slot], _DIMS["nt"], preferred_element_type=F32)

        @pl.when(i == n_blocks - 1)
        def _():
            far.finish(*far_refs)

    anysp = pl.BlockSpec(memory_space=pl.ANY)
    grid_spec = pltpu.PrefetchScalarGridSpec(
        num_scalar_prefetch=1, grid=(n_blocks,),
        in_specs=[anysp, pl.BlockSpec((1, d), lambda i, w: (0, 0)), anysp] + [_HBM] * n_cin,
        out_specs=[pl.BlockSpec((s, IN_BLOCK), lambda i, w: (0, _perm_block(block_of(i, w)))), anysp, anysp]
        + [_HBM] * n_cout,
        scratch_shapes=[pltpu.VMEM((2, IN_BLOCK, d), BF16), pltpu.SemaphoreType.DMA((2,)), pltpu.VMEM((s, d), BF16),
                        pltpu.VMEM((norm_tile, d), F32), pltpu.VMEM((d, norm_tile), BF16),
                        pltpu.SemaphoreType.DMA((2,))] + both.sems)
    return pl.pallas_call(
        body, grid_spec=grid_spec,
        out_shape=[jax.ShapeDtypeStruct((s, IN_WIDTH), F32), jax.ShapeDtypeStruct((s, d), BF16),
                   jax.ShapeDtypeStruct((d, s), BF16)] + both.out_shapes,
        name="proj_near", compiler_params=_params())(where, x, g_pre, own_w, *both.ins)


def _proj_far(h, w_near, far, where, *, into, carry=None):
    s, d = h.shape
    n_blocks = SHARD_BLOCKS + 1
    lead = IN_WIDTH // N_CHIPS - SHARD_BLOCKS * IN_BLOCK

    def body(ins, outs, scr):
        where_ref, h_ref, w_hbm, far_hbm, _ = ins
        win, sem = scr
        i = pl.program_id(0)

        @pl.when(i == 0)
        def _():
            dg = where_ref[2]
            rows = pl.ds(pl.multiple_of(dg * (SHARD_BLOCKS * IN_BLOCK), IN_BLOCK), n_blocks * IN_BLOCK)
            window = pltpu.make_async_copy(w_hbm.at[rows], win, sem)
            window.start()
            window.wait()
            shard = pltpu.make_async_copy(far_hbm, win.at[pl.ds(pl.multiple_of(dg * lead, BF16_SUBLANES), SHARD_W)], sem)
            shard.start()
            shard.wait()

        blk = win[pl.ds(pl.multiple_of(i * IN_BLOCK, IN_BLOCK), IN_BLOCK), :]
        outs[0][...] = lax.dot_general(h_ref[...], blk, _DIMS["nt"], preferred_element_type=F32)

    anysp = pl.BlockSpec(memory_space=pl.ANY)
    (proj,), carried = _carried_call(
        body, carry, grid=(n_blocks,),
        in_specs=[pl.BlockSpec(memory_space=pltpu.SMEM), pl.BlockSpec((s, d), lambda i, w: (0, 0)), anysp, anysp, anysp],
        out_specs=[pl.BlockSpec((s, IN_BLOCK), lambda i, w: (0, _perm_block(i + SHARD_BLOCKS * w[2])))],
        out_shape=[jax.ShapeDtypeStruct((s, IN_WIDTH), F32)],
        scratch=[pltpu.VMEM((n_blocks * IN_BLOCK, d), BF16), pltpu.SemaphoreType.DMA],
        operands=(where, h, w_near, far, into), name="proj_far", prefetch=where, aliases={4: 0})
    return (proj, carried) if carry else proj


def _dw_in_t(dproj, h_t, *, half_of, where, name, carry=None):
    d, s = h_t.shape
    c = d // 2

    def body(ins, outs, scr):
        outs[0][...] = lax.dot_general(ins[1][...], ins[0][...], _DIMS["nn"], preferred_element_type=F32).T.astype(BF16)

    (dw,), carried = _carried_call(
        body, carry, grid=(N_IN_BLOCKS,),
        in_specs=[pl.BlockSpec((s, IN_BLOCK), lambda r, w: (0, _perm_block(r))),
                  pl.BlockSpec((c, s), lambda r, w: (half_of(w), 0))],
        out_specs=[pl.BlockSpec((IN_BLOCK, c), lambda r, w: (r, 0))],
        out_shape=[jax.ShapeDtypeStruct((IN_WIDTH, c), BF16)], scratch=[], operands=(dproj, h_t), name=name,
        prefetch=where)
    return (dw, carried) if carry else dw


def _norm_bwd_tile(dhv, xv, gv, resv):
    r = lax.rsqrt(jnp.mean(xv * xv, axis=-1, keepdims=True) + EPS)
    xh = xv * r
    dxh = dhv * gv
    dx = resv + r * (dxh - xh * jnp.mean(dxh * xh, axis=-1, keepdims=True))
    return dx, jnp.sum(dhv * xh, axis=0, keepdims=True)


def _d_h(dproj, w_near, far, where, x, g, res, *, carry=None):
    s = dproj.shape[0]
    d = w_near.shape[1]
    tm = min(s, 256)
    n = s // tm
    assert n % 2 == 0

    def body(ins, outs, scr):
        where_ref, a_ref, w_hbm, far_hbm, x_ref, g_ref, res_ref = ins
        dx_ref, dg_ref = outs
        w_ref, sem, dh_even, dh_odd = scr
        i = pl.program_id(0)

        def norm_bwd(dh_ref):
            dx, part = _norm_bwd_tile(dh_ref[...], x_ref[...], g_ref[...], res_ref[...])
            dx_ref[...] = dx
            dg_ref[...] += part

        def matmul(dh_ref):
            acc = None
            for ref0, perm0, nb in BLOCK_RUNS:
                term = jnp.dot(a_ref[:, perm0 * IN_BLOCK:(perm0 + nb) * IN_BLOCK],
                               w_ref[ref0 * IN_BLOCK:(ref0 + nb) * IN_BLOCK, :], preferred_element_type=F32)
                acc = term if acc is None else acc + term
            dh_ref[...] = acc

        @pl.when(i == 0)
        def _():
            whole = pltpu.make_async_copy(w_hbm, w_ref, sem)
            whole.start()
            whole.wait()
            rows = pl.ds(pl.multiple_of(where_ref[2] * SHARD_W, BF16_SUBLANES), SHARD_W)
            part = pltpu.make_async_copy(far_hbm, w_ref.at[rows], sem)
            part.start()
            part.wait()
            dg_ref[...] = jnp.zeros_like(dg_ref)
            matmul(dh_even)

        @pl.when((i % 2 == 0) & (i > 0) & (i < n))
        def _():
            norm_bwd(dh_odd)
            matmul(dh_even)

        @pl.when(i % 2 == 1)
        def _():
            norm_bwd(dh_even)
            matmul(dh_odd)

        @pl.when(i == n)
        def _():
            norm_bwd(dh_odd)

    anysp = pl.BlockSpec(memory_space=pl.ANY)
    before = pl.BlockSpec((tm, d), lambda i: (jnp.maximum(i - 1, 0), 0))
    vec = pl.BlockSpec((1, d), lambda i: (0, 0))
    outs, carried = _carried_call(
        body, carry, grid=(n + 1,),
        in_specs=[pl.BlockSpec(memory_space=pltpu.SMEM),
                  pl.BlockSpec((tm, IN_WIDTH), lambda i: (jnp.minimum(i, n - 1), 0)), anysp, anysp, before, vec, before],
        out_specs=[before, vec],
        out_shape=[jax.ShapeDtypeStruct((s, d), F32), jax.ShapeDtypeStruct((1, d), F32)],
        scratch=[pltpu.VMEM((IN_WIDTH, d), BF16), pltpu.SemaphoreType.DMA, pltpu.VMEM((tm, d), F32),
                 pltpu.VMEM((tm, d), F32)],
        operands=(where, dproj, w_near, far, x, g, res), name="d_h", heavy=True)
    return (outs, carried) if carry else outs


def _rmsnorm_fwd(x, g, *, name):
    s, d = x.shape
    ts = min(512, s)

    def body(x_ref, g_ref, o_ref):
        xv = x_ref[...]
        r = lax.rsqrt(jnp.mean(xv * xv, axis=-1, keepdims=True) + EPS)
        o_ref[...] = ((xv * r) * g_ref[...]).astype(BF16)

    return pl.pallas_call(
        body, grid=(s // ts,),
        in_specs=[pl.BlockSpec((ts, d), lambda i: (i, 0)), pl.BlockSpec((1, d), lambda i: (0, 0))],
        out_specs=pl.BlockSpec((ts, d), lambda i: (i, 0)),
        out_shape=jax.ShapeDtypeStruct((s, d), BF16), name=name, compiler_params=_params())(x, g)


def _rmsnorm_bwd(dh, x, g, res, *, name, carry=None):
    s, d = x.shape
    ts = min(256, s)

    def body(ins, outs, scr):
        dh_ref, x_ref, g_ref, res_ref = ins
        dx_ref, dg_ref = outs
        dx, part = _norm_bwd_tile(dh_ref[...], x_ref[...], g_ref[...], res_ref[...])

        @pl.when(pl.program_id(0) == 0)
        def _():
            dg_ref[...] = part

        @pl.when(pl.program_id(0) > 0)
        def _():
            dg_ref[...] += part

        dx_ref[...] = dx

    row = pl.BlockSpec((ts, d), lambda i: (i, 0))
    vec = pl.BlockSpec((1, d), lambda i: (0, 0))
    outs, carried = _carried_call(
        body, carry, grid=(s // ts,), in_specs=[row, row, vec, row], out_specs=[row, vec],
        out_shape=[jax.ShapeDtypeStruct((s, d), F32), jax.ShapeDtypeStruct((1, d), F32)],
        scratch=[], operands=(dh, x, g, res), name=name)
    return (*outs, carried) if carry else tuple(outs)


MID_TILE = 256


def _gated_branches(y_refs, wup_ref, gl):
    d = D_MODEL
    us = [jnp.dot(y_refs[k][...], wup_ref[k], preferred_element_type=F32) for k in range(3)]
    sg = [_sigmoid(gl[:, k * d:(k + 1) * d]) for k in range(3)]
    return us, sg


def _mid_fwd(ya, yb, ym, proj, x, tgt, w_up, w_out, g_post):
    s, d = x.shape
    ts = MID_TILE

    def body(ya_ref, yb_ref, ym_ref, g_ref, x_ref, t_ref, wup_ref, wout_ref, gp_ref,
             m_ref, do_ref, dy_ref, dg_ref, loss_ref):
        us, sg = _gated_branches((ya_ref, yb_ref, ym_ref), wup_ref, g_ref[...])
        merged = (sg[0] * us[0] + sg[1] * us[1] + sg[2] * us[2]).astype(BF16)
        m_ref[...] = merged
        ov = jnp.dot(merged, wout_ref[...], preferred_element_type=F32)
        r = lax.rsqrt(jnp.mean(ov * ov, axis=-1, keepdims=True) + EPS)
        nh = ov * r
        gv = gp_ref[...]
        e = (x_ref[...] + nh * gv) - t_ref[...]
        lpart = 0.5 * jnp.sum(jnp.mean(e * e, axis=-1, keepdims=True), axis=0, keepdims=True)
        dy = e * (1.0 / d)
        dgp = jnp.sum(dy * nh, axis=0, keepdims=True)

        @pl.when(pl.program_id(0) == 0)
        def _():
            dg_ref[...] = dgp
            loss_ref[...] = jnp.broadcast_to(lpart, loss_ref.shape)

        @pl.when(pl.program_id(0) > 0)
        def _():
            dg_ref[...] += dgp
            loss_ref[...] += jnp.broadcast_to(lpart, loss_ref.shape)

        dn = dy * gv
        dy_ref[...] = dy
        do_ref[...] = (r * (dn - nh * jnp.mean(dn * nh, axis=-1, keepdims=True))).astype(BF16)

    row = pl.BlockSpec((ts, d), lambda i: (i, 0))
    ysp = pl.BlockSpec((ts, A_WIDTH), lambda i: (i, 0))
    vec = pl.BlockSpec((1, d), lambda i: (0, 0))
    return pl.pallas_call(
        body, grid=(s // ts,),
        in_specs=[ysp, ysp, ysp, pl.BlockSpec((ts, W_G), lambda i: (i, COL_G)), row, row,
                  pl.BlockSpec((3, A_WIDTH, d), lambda i: (0, 0, 0)), pl.BlockSpec((d, d), lambda i: (0, 0)), vec],
        out_specs=[row, row, row, vec, pl.BlockSpec((1, LANES), lambda i: (0, 0))],
        out_shape=[jax.ShapeDtypeStruct((s, d), BF16), jax.ShapeDtypeStruct((s, d), BF16),
                   jax.ShapeDtypeStruct((s, d), F32), jax.ShapeDtypeStruct((1, d), F32),
                   jax.ShapeDtypeStruct((1, LANES), F32)],
        name="mid_fwd", compiler_params=_params(heavy=True))(ya, yb, ym, proj, x, tgt, w_up, w_out, g_post)


def _mid_bwd(d_out, merged, ya, yb, ym, proj, w_up, w_out):
    s, d = merged.shape
    ts = MID_TILE
    last = s // ts - 1

    def body(do_ref, m_ref, ya_ref, yb_ref, ym_ref, proj_hbm, wup_ref, wout_ref,
             dp_ref, dya_ref, dyb_ref, dym_ref, dwup_hbm, dwout_hbm, dwup_acc, dwout_acc, dwup_cast, dwout_cast,
             g_ring, g_sems):
        i = pl.program_id(0)

        def gate_copy(step, slot):
            rows = pl.ds(pl.multiple_of(step * ts, ts), ts)
            return pltpu.make_async_copy(proj_hbm.at[rows, pl.ds(COL_G * W_G, W_G)], g_ring.at[slot], g_sems.at[slot])

        @pl.when(i == 0)
        def _():
            dwup_acc[...] = jnp.zeros_like(dwup_acc)
            dwout_acc[...] = jnp.zeros_like(dwout_acc)
            for first in range(min(2, last + 1)):
                gate_copy(first, first).start()

        @pl.when(i + 2 <= last)
        def _():
            gate_copy(i + 2, (i + 2) % 3).start()

        gate_copy(i, i % 3).wait()
        y_refs = (ya_ref, yb_ref, ym_ref)
        us, sg = _gated_branches(y_refs, wup_ref, g_ring[i % 3])
        dov = do_ref[...]
        dwout_acc[...] += lax.dot_general(m_ref[...], dov, _DIMS["tn"], preferred_element_type=F32)
        dm = lax.dot_general(dov, wout_ref[...], _DIMS["nt"], preferred_element_type=F32)
        for k, dy_ref in enumerate((dya_ref, dyb_ref, dym_ref)):
            dp_ref[:, k * d:(k + 1) * d] = ((dm * us[k]) * (sg[k] * (1.0 - sg[k]))).astype(BF16)
            du = (sg[k] * dm).astype(BF16)
            dy_ref[...] = lax.dot_general(du, wup_ref[k], _DIMS["nt"], preferred_element_type=F32)
            dwup_acc[k] += lax.dot_general(y_refs[k][...], du, _DIMS["tn"], preferred_element_type=F32)

        @pl.when(i == last)
        def _():
            dwup_cast[...] = dwup_acc[...].astype(BF16)
            dwout_cast[...] = dwout_acc[...].astype(BF16)
            pltpu.sync_copy(dwup_cast, dwup_hbm)
            pltpu.sync_copy(dwout_cast, dwout_hbm)

    row = pl.BlockSpec((ts, d), lambda i: (i, 0))
    ysp = pl.BlockSpec((ts, A_WIDTH), lambda i: (i, 0))
    gsp = pl.BlockSpec((ts, W_G), lambda i: (i, COL_G))
    anysp = pl.BlockSpec(memory_space=pl.ANY)
    yshape = jax.ShapeDtypeStruct((s, A_WIDTH), F32)
    return pl.pallas_call(
        body, grid=(s // ts,),
        in_specs=[row, row, ysp, ysp, ysp, anysp, pl.BlockSpec((3, A_WIDTH, d), lambda i: (0, 0, 0)),
                  pl.BlockSpec((d, d), lambda i: (0, 0))],
        out_specs=[gsp, ysp, ysp, ysp, anysp, anysp],
        out_shape=[jax.ShapeDtypeStruct((s, IN_WIDTH), BF16), yshape, yshape, yshape,
                   jax.ShapeDtypeStruct((3, A_WIDTH, d), BF16), jax.ShapeDtypeStruct((d, d), BF16)],
        scratch_shapes=[pltpu.VMEM((3, A_WIDTH, d), F32), pltpu.VMEM((d, d), F32),
                        pltpu.VMEM((3, A_WIDTH, d), BF16), pltpu.VMEM((d, d), BF16),
                        pltpu.VMEM((3, ts, W_G), F32), pltpu.SemaphoreType.DMA((3,))],
        name="mid_bwd", compiler_params=_params(heavy=True))(d_out, merged, ya, yb, ym, proj, w_up, w_out)


def _conv_core(blk, prev, nxt, w, i, last, ts):
    c = A_WIDTH
    ab, ac, ax, az = blk[:, :c], blk[:, c:2 * c], blk[:, 2 * c:3 * c], blk[:, 3 * c:]
    cu = ac * ax
    cu_prev = (prev[7:8, c:2 * c] * prev[7:8, 2 * c:3 * c]) * jnp.where(i > 0, 1.0, 0.0)
    cu_next = (nxt[0:1, c:2 * c] * nxt[0:1, 2 * c:3 * c]) * jnp.where(i < last, 1.0, 0.0)
    row = lax.broadcasted_iota(jnp.int32, (ts, c), 0)
    cm1 = jnp.where(row == 0, cu_prev, pltpu.roll(cu, 1, 0))
    cp1 = jnp.where(row == ts - 1, cu_next, pltpu.roll(cu, ts - 1, 0))
    yc = cm1 * w[0:1] + cu * w[1:2] + cp1 * w[2:3]
    return ab, ac, ax, az, cu, cm1, cp1, yc, row


def _halo_specs(ts, width, col, nblk8):
    prev = pl.BlockSpec((8, width), lambda i: (jnp.maximum(i * (ts // 8) - 1, 0), col))
    nxt = pl.BlockSpec((8, width), lambda i: (jnp.minimum((i + 1) * (ts // 8), nblk8 - 1), col))
    return prev, nxt


def _conv_fwd(proj, w_conv):
    s = proj.shape[0]
    ts = 256
    last = s // ts - 1

    def body(a_ref, ap_ref, an_ref, w_ref, ya_ref):
        i = pl.program_id(0)
        ab, _, _, az, _, _, _, yc, _ = _conv_core(a_ref[...], ap_ref[...], an_ref[...], w_ref[...], i, last, ts)
        ya_ref[...] = ((ab * yc) * (az * _sigmoid(az))).astype(BF16)

    prev, nxt = _halo_specs(ts, W_A, COL_A, s // 8)
    return pl.pallas_call(
        body, grid=(s // ts,),
        in_specs=[pl.BlockSpec((ts, W_A), lambda i: (i, COL_A)), prev, nxt,
                  pl.BlockSpec((3, A_WIDTH), lambda i: (0, 0))],
        out_specs=pl.BlockSpec((ts, A_WIDTH), lambda i: (i, 0)),
        out_shape=jax.ShapeDtypeStruct((s, A_WIDTH), BF16), name="conv_fwd",
        compiler_params=_params())(proj, proj, proj, w_conv)


def _conv_bwd(proj, w_conv, dya, dproj):
    s = proj.shape[0]
    ts = 256
    last = s // ts - 1
    c = A_WIDTH

    def body(a_ref, ap_ref, an_ref, w_ref, d_ref, dp_ref, dn_ref, _, dproj_ref, dw_ref):
        i = pl.program_id(0)
        w = w_ref[...]
        prev, nxt = ap_ref[...], an_ref[...]
        ab, ac, ax, az, cu, cm1, cp1, yc, row = _conv_core(a_ref[...], prev, nxt, w, i, last, ts)
        sg = _sigmoid(az)
        sz = az * sg
        dya_v = d_ref[...]
        dyc = dya_v * sz * ab
        dproj_ref[:, :c] = (dya_v * sz * yc).astype(BF16)
        dproj_ref[:, 3 * c:] = (dya_v * (ab * yc) * (sg * (1.0 + az * (1.0 - sg)))).astype(BF16)

        def halo_dyc(a_row, d_row):
            azr = a_row[:, 3 * c:]
            return d_row * (azr * _sigmoid(azr)) * a_row[:, :c]

        dyc_prev = halo_dyc(prev[7:8], dp_ref[...][7:8]) * jnp.where(i > 0, 1.0, 0.0)
        dyc_next = halo_dyc(nxt[0:1], dn_ref[...][0:1]) * jnp.where(i < last, 1.0, 0.0)
        dyc_m1 = jnp.where(row == 0, dyc_prev, pltpu.roll(dyc, 1, 0))
        dyc_p1 = jnp.where(row == ts - 1, dyc_next, pltpu.roll(dyc, ts - 1, 0))
        dcu = dyc_p1 * w[0:1] + dyc * w[1:2] + dyc_m1 * w[2:3]
        dproj_ref[:, c:2 * c] = (dcu * ax).astype(BF16)
        dproj_ref[:, 2 * c:3 * c] = (dcu * ac).astype(BF16)
        dw = [jnp.sum(dyc * t, axis=0, keepdims=True) for t in (cm1, cu, cp1)]

        @pl.when(i == 0)
        def _():
            for k in range(3):
                dw_ref[k:k + 1, :] = dw[k]

        @pl.when(i > 0)
        def _():
            for k in range(3):
                dw_ref[k:k + 1, :] += dw[k]

    prev, nxt = _halo_specs(ts, W_A, COL_A, s // 8)
    dprev, dnxt = _halo_specs(ts, A_WIDTH, 0, s // 8)
    return pl.pallas_call(
        body, grid=(s // ts,),
        in_specs=[pl.BlockSpec((ts, W_A), lambda i: (i, COL_A)), prev, nxt,
                  pl.BlockSpec((3, A_WIDTH), lambda i: (0, 0)),
                  pl.BlockSpec((ts, A_WIDTH), lambda i: (i, 0)), dprev, dnxt,
                  pl.BlockSpec(memory_space=pl.ANY)],
        out_specs=[pl.BlockSpec((ts, W_A), lambda i: (i, COL_A)), pl.BlockSpec((3, A_WIDTH), lambda i: (0, 0))],
        out_shape=[jax.ShapeDtypeStruct(dproj.shape, BF16), jax.ShapeDtypeStruct((3, A_WIDTH), F32)],
        input_output_aliases={7: 0}, name="conv_bwd",
        compiler_params=_params())(proj, proj, proj, w_conv, dya, dya, dya, dproj)


def _rope_tables(s):
    half = ROT_DIM // 2
    dim = jnp.arange(LANES) % HEAD_DIM
    inv_freq = jnp.power(jnp.float32(ROPE_THETA), -(dim % half).astype(F32) * (2.0 / ROT_DIM))
    coarse = (jnp.arange(s // LANES) * LANES).astype(F32)[:, None] * inv_freq[None, :]
    fine = jnp.arange(LANES).astype(F32)[:, None] * inv_freq[None, :]
    cos_a, sin_a = jnp.cos(coarse)[:, None, :], jnp.sin(coarse)[:, None, :]
    cos_b, sin_b = jnp.cos(fine)[None], jnp.sin(fine)[None]
    cos = (cos_a * cos_b - sin_a * sin_b).reshape(s, LANES)
    sin = (sin_a * cos_b + cos_a * sin_b).reshape(s, LANES)
    first, second = (dim < half)[None, :], ((dim >= half) & (dim < ROT_DIM))[None, :]
    c = jnp.where(first | second, cos, 1.0)
    s1 = jnp.where(first, -sin, 0.0)
    s2 = jnp.where(second, sin, 0.0)
    return jnp.concatenate([c, s1, s2], axis=1)


def _rope(t, tab):
    return (t * tab[:, :LANES] + pltpu.roll(t, LANES - 8, 1) * tab[:, LANES:2 * LANES]
            + pltpu.roll(t, 8, 1) * tab[:, 2 * LANES:])


def _rope_transpose(dt, tab):
    return (dt * tab[:, :LANES] + pltpu.roll(dt * tab[:, LANES:2 * LANES], 8, 1)
            + pltpu.roll(dt * tab[:, 2 * LANES:], LANES - 8, 1))


def _rope_kv(proj, tab):
    s = proj.shape[0]
    nb = s // KV_PAD

    def body(kv_ref, t_ref, k_ref, v_ref):
        j = pl.program_id(0)
        inside = jnp.where((j > 0) & (j <= nb), 1.0, 0.0)
        kv = kv_ref[...]
        k_ref[...] = (_rope(kv[:, :LANES], t_ref[...]) * inside).astype(BF16)
        v_ref[...] = (kv[:, LANES:] * inside).astype(BF16)

    def src(j):
        return jnp.clip(j - 1, 0, nb - 1)

    o_spec = pl.BlockSpec((KV_PAD, LANES), lambda j: (j, 0))
    shp = jax.ShapeDtypeStruct((s + 2 * KV_PAD, LANES), BF16)
    return pl.pallas_call(
        body, grid=(nb + 2,),
        in_specs=[pl.BlockSpec((KV_PAD, W_KV), lambda j: (src(j), COL_KV)),
                  pl.BlockSpec((KV_PAD, 3 * LANES), lambda j: (src(j), 0))],
        out_specs=[o_spec, o_spec], out_shape=[shp, shp], name="rope_kv",
        compiler_params=_params())(proj, tab)


def _rope_kv_bwd(dkpad, dvpad, tab, dproj):
    s = tab.shape[0]
    nb = s // KV_PAD

    def body(dk_ref, dv_ref, t_ref, _, dp_ref):
        dp_ref[:, :LANES] = _rope_transpose(dk_ref[...], t_ref[...]).astype(BF16)
        dp_ref[:, LANES:] = dv_ref[...].astype(BF16)

    pad_spec = pl.BlockSpec((KV_PAD, LANES), lambda j: (j + 1, 0))
    return pl.pallas_call(
        body, grid=(nb,),
        in_specs=[pad_spec, pad_spec, pl.BlockSpec((KV_PAD, 3 * LANES), lambda j: (j, 0)),
                  pl.BlockSpec(memory_space=pl.ANY)],
        out_specs=pl.BlockSpec((KV_PAD, W_KV), lambda j: (j, COL_KV)),
        out_shape=jax.ShapeDtypeStruct(dproj.shape, BF16), input_output_aliases={3: 0},
        name="rope_kv_bwd", compiler_params=_params())(dkpad, dvpad, tab, dproj)


def _window_start(n):
    return pl.multiple_of((n - 1) * WINDOW_BLOCK + KV_PAD, WINDOW_BLOCK)


def _window_operands(k_ref, v_ref, n, lo):
    start = _window_start(n)
    kw = k_ref[pl.ds(start, 3 * WINDOW_BLOCK), :].astype(F32)
    vw = v_ref[pl.ds(start, 3 * WINDOW_BLOCK), :].astype(F32)
    kr, vr = pltpu.roll(kw, HALF_LANES, 1), pltpu.roll(vw, HALF_LANES, 1)
    k2 = (jnp.where(lo, kw, kr).astype(BF16), jnp.where(lo, kr, kw).astype(BF16))
    v2 = (jnp.where(lo, vw, vr).astype(BF16), jnp.where(lo, vr, vw).astype(BF16))
    return k2, v2


HEADS_PER_GROUP = 4
SWA_FWD_BLOCKS = 1
SWA_BWD_BLOCKS = 2


def _window_bias():
    wb = WINDOW_BLOCK
    qi = lax.broadcasted_iota(jnp.int32, (wb, 3 * wb), 0)
    kj = lax.broadcasted_iota(jnp.int32, (wb, 3 * wb), 1)
    band = (kj >= qi) & (kj <= qi + 2 * wb)
    cases = jnp.stack([band & (kj >= wb), band, band & (kj < 2 * wb)])
    return jnp.where(cases, 0.0, -jnp.inf).astype(F32)


def _block_bias(bias_ref, n, n_blocks):
    case = jnp.where(n == 0, 0, jnp.where(n == n_blocks - 1, 2, 1))
    one = bias_ref[case]
    return jnp.concatenate([one] * HEADS_PER_GROUP, axis=0)


def _stack_heads(pair0, pair1, lo):
    return jnp.concatenate([jnp.where(lo, pair0, 0.0), jnp.where(lo, 0.0, pair0),
                            jnp.where(lo, pair1, 0.0), jnp.where(lo, 0.0, pair1)], axis=0)


def _unstack_pair(stacked, i, lo):
    wb = WINDOW_BLOCK
    return jnp.where(lo, stacked[2 * i * wb:(2 * i + 1) * wb], stacked[(2 * i + 1) * wb:(2 * i + 2) * wb])


def _sink_column(sink_ref, g):
    wb = WINDOW_BLOCK
    return jnp.concatenate([jnp.full((wb, 1), sink_ref[0, HEADS_PER_GROUP * g + i], F32)
                            for i in range(HEADS_PER_GROUP)], axis=0)


def _head_exp(q4, k2g, bias, sink):
    sc = lax.dot_general(q4, k2g, _DIMS["nt"], preferred_element_type=F32) * (HEAD_DIM ** -0.5) + bias
    m = jnp.maximum(jnp.max(sc, axis=1, keepdims=True), sink)
    return jnp.exp(sc - m).astype(BF16), jnp.exp(sink - m)


def _swa_fwd(proj, kpad, vpad, tab, bias, sink, *, carry=None):
    s = proj.shape[0]
    wb = WINDOW_BLOCK

    def body(b_ref, k_ref, v_ref, t_ref, bias_ref, sink_ref, o_ref, y_ref):
        lo = lax.broadcasted_iota(jnp.int32, (wb, LANES), 1) < HALF_LANES
        lo_w = lax.broadcasted_iota(jnp.int32, (3 * wb, LANES), 1) < HALF_LANES
        for sub in range(SWA_FWD_BLOCKS):
            n = pl.program_id(0) * SWA_FWD_BLOCKS + sub
            rows = slice(sub * wb, (sub + 1) * wb)
            k2, v2 = _window_operands(k_ref, v_ref, n, lo_w)
            valid = _block_bias(bias_ref, n, s // wb)
            tab_v = t_ref[rows, :]
            ones = jnp.ones((3 * wb, LANES), BF16)
            for g in range(2):
                qr = [_rope(b_ref[rows, (2 * g + i) * LANES:(2 * g + i + 1) * LANES], tab_v) for i in range(2)]
                q4 = _stack_heads(qr[0], qr[1], lo).astype(BF16)
                e, es = _head_exp(q4, k2[g], valid, _sink_column(sink_ref, g))
                ox = jnp.dot(e, jnp.concatenate([v2[g], ones], axis=1), preferred_element_type=F32)
                o4 = ox[:, :LANES] * (1.0 / (ox[:, LANES:] + es))
                for i in range(2):
                    cols = slice((2 * g + i) * LANES, (2 * g + i + 1) * LANES)
                    op = _unstack_pair(o4, i, lo)
                    o_ref[rows, cols] = op
                    zp = b_ref[rows, A_WIDTH + cols.start:A_WIDTH + cols.stop]
                    y_ref[rows, cols] = (op * (zp * _sigmoid(zp))).astype(BF16)

    tq = SWA_FWD_BLOCKS * wb
    pad_spec = pl.BlockSpec((s + 2 * KV_PAD, LANES), lambda n: (0, 0))
    o_spec = pl.BlockSpec((tq, A_WIDTH), lambda n: (n, 0))
    outs, carried = _carried_call(
        lambda ins, outs, scr: body(*ins, *outs), carry, grid=(s // tq,),
        in_specs=[pl.BlockSpec((tq, W_B), lambda n: (n, COL_B)), pad_spec, pad_spec,
                  pl.BlockSpec((tq, 3 * LANES), lambda n: (n, 0)),
                  pl.BlockSpec(bias.shape, lambda n: (0, 0, 0)), pl.BlockSpec(memory_space=pltpu.SMEM)],
        out_specs=[o_spec, o_spec],
        out_shape=[jax.ShapeDtypeStruct((s, A_WIDTH), F32), jax.ShapeDtypeStruct((s, A_WIDTH), BF16)],
        scratch=[], operands=(proj, kpad, vpad, tab, bias, sink), name="swa_fwd")
    return (*outs, carried) if carry else tuple(outs)


def _swa_bwd(proj, kpad, vpad, tab, bias, sink, o_attn, dyb, dproj):
    s = proj.shape[0]
    wb = WINDOW_BLOCK
    scale = HEAD_DIM ** -0.5

    def body(b_ref, k_ref, v_ref, t_ref, bias_ref, sink_ref, o_ref, dy_ref, _, dp_ref, dk_ref, dv_ref, ds_ref):
        @pl.when(pl.program_id(0) == 0)
        def _():
            dk_ref[...] = jnp.zeros_like(dk_ref)
            dv_ref[...] = jnp.zeros_like(dv_ref)
            ds_ref[...] = jnp.zeros_like(ds_ref)

        lo = lax.broadcasted_iota(jnp.int32, (wb, LANES), 1) < HALF_LANES
        lo_w = lax.broadcasted_iota(jnp.int32, (3 * wb, LANES), 1) < HALF_LANES
        for sub in range(SWA_BWD_BLOCKS):
            n = pl.program_id(0) * SWA_BWD_BLOCKS + sub
            rows = slice(sub * wb, (sub + 1) * wb)
            k2, v2 = _window_operands(k_ref, v_ref, n, lo_w)
            valid = _block_bias(bias_ref, n, s // wb)
            tab_v = t_ref[rows, :]
            ones = jnp.ones((3 * wb, LANES), BF16)
            dks, dvs = [], []
            for g in range(2):
                qr, op, do = [], [], []
                for i in range(2):
                    cols = slice((2 * g + i) * LANES, (2 * g + i + 1) * LANES)
                    zcols = slice(A_WIDTH + cols.start, A_WIDTH + cols.stop)
                    qr.append(_rope(b_ref[rows, cols], tab_v))
                    zp = b_ref[rows, zcols]
                    sg = _sigmoid(zp)
                    op.append(o_ref[rows, cols])
                    dyp = dy_ref[rows, cols]
                    do.append(dyp * (zp * sg))
                    dp_ref[rows, zcols] = (dyp * op[i] * (sg * (1.0 + zp * (1.0 - sg)))).astype(BF16)
                q4 = _stack_heads(qr[0], qr[1], lo).astype(BF16)
                do4 = _stack_heads(do[0], do[1], lo)
                o4 = jnp.concatenate([op[0], op[0], op[1], op[1]], axis=0)
                e, es = _head_exp(q4, k2[g], valid, _sink_column(sink_ref, g))
                inv = 1.0 / (jnp.dot(e, ones, preferred_element_type=F32) + es)
                prob = e.astype(F32) * jnp.concatenate([inv, inv, inv], axis=1)
                delta = jnp.sum(do4 * o4, axis=1, keepdims=True)
                do4b = do4.astype(BF16)
                dprob = lax.dot_general(do4b, v2[g], _DIMS["nt"], preferred_element_type=F32)
                dsc = (prob * (dprob - delta)).astype(BF16)
                sink_terms = (es * inv[:, :1]) * delta
                for i in range(HEADS_PER_GROUP):
                    h = HEADS_PER_GROUP * g + i
                    dsink = -jnp.sum(sink_terms[i * wb:(i + 1) * wb], axis=0, keepdims=True)
                    ds_ref[h:h + 1, :] += jnp.broadcast_to(dsink, (1, LANES))
                dq4 = jnp.dot(dsc, k2[g], preferred_element_type=F32) * scale
                for i in range(2):
                    cols = slice((2 * g + i) * LANES, (2 * g + i + 1) * LANES)
                    dp_ref[rows, cols] = _rope_transpose(_unstack_pair(dq4, i, lo), tab_v).astype(BF16)
                dk2 = lax.dot_general(dsc, q4, _DIMS["tn"], preferred_element_type=F32) * scale
                dv2 = lax.dot_general(prob.astype(BF16), do4b, _DIMS["tn"], preferred_element_type=F32)
                dks.append(dk2 + pltpu.roll(dk2, HALF_LANES, 1))
                dvs.append(dv2 + pltpu.roll(dv2, HALF_LANES, 1))
            start = _window_start(n)
            dk_ref[pl.ds(start, 3 * wb), :] += jnp.where(lo_w, dks[0], dks[1])
            dv_ref[pl.ds(start, 3 * wb), :] += jnp.where(lo_w, dvs[0], dvs[1])

    tq = SWA_BWD_BLOCKS * wb
    pad_spec = pl.BlockSpec((s + 2 * KV_PAD, LANES), lambda n: (0, 0))
    blk = pl.BlockSpec((tq, A_WIDTH), lambda n: (n, 0))
    bsp = pl.BlockSpec((tq, W_B), lambda n: (n, COL_B))
    pad_shape = jax.ShapeDtypeStruct((s + 2 * KV_PAD, LANES), F32)
    return pl.pallas_call(
        body, grid=(s // tq,),
        in_specs=[bsp, pad_spec, pad_spec, pl.BlockSpec((tq, 3 * LANES), lambda n: (n, 0)),
                  pl.BlockSpec(bias.shape, lambda n: (0, 0, 0)), pl.BlockSpec(memory_space=pltpu.SMEM), blk, blk,
                  pl.BlockSpec(memory_space=pl.ANY)],
        out_specs=[bsp, pad_spec, pad_spec, pl.BlockSpec((8, LANES), lambda n: (0, 0))],
        out_shape=[jax.ShapeDtypeStruct(dproj.shape, BF16), pad_shape, pad_shape,
                   jax.ShapeDtypeStruct((8, LANES), F32)],
        input_output_aliases={8: 0}, name="swa_bwd",
        compiler_params=_params())(proj, kpad, vpad, tab, bias, sink, o_attn, dyb, dproj)


def _mem_exp(qh, mk):
    sc = lax.dot_general(qh, mk, _DIMS["nt"], preferred_element_type=F32) * (MEM_HEAD_DIM ** -0.5)
    return jnp.exp(sc - jnp.max(sc, axis=1, keepdims=True)).astype(BF16)


def _mem_fwd(proj, mkv):
    s = proj.shape[0]
    ts = 512
    mlen = mkv.shape[0]

    def body(m_ref, kv_ref, o_ref, y_ref):
        ones = jnp.ones((mlen, LANES), BF16)
        for h in range(MEM_HEADS):
            cols = slice(h * LANES, (h + 1) * LANES)
            mk = kv_ref[:, cols].astype(BF16)
            mv = kv_ref[:, MEM_WIDTH + h * LANES:MEM_WIDTH + (h + 1) * LANES].astype(BF16)
            e = _mem_exp(m_ref[:, cols].astype(BF16), mk)
            ox = jnp.dot(e, jnp.concatenate([mv, ones], axis=1), preferred_element_type=F32)
            oh = ox[:, :LANES] * (1.0 / ox[:, LANES:])
            o_ref[:, cols] = oh
            zh = m_ref[:, MEM_WIDTH + h * LANES:MEM_WIDTH + (h + 1) * LANES]
            y_ref[:, cols] = (oh * (zh * _sigmoid(zh))).astype(BF16)

    o_spec = pl.BlockSpec((ts, MEM_WIDTH), lambda i: (i, 0))
    return pl.pallas_call(
        body, grid=(s // ts,),
        in_specs=[pl.BlockSpec((ts, W_M), lambda i: (i, COL_M)),
                  pl.BlockSpec((mlen, 2 * MEM_WIDTH), lambda i: (0, 0))],
        out_specs=[o_spec, o_spec],
        out_shape=[jax.ShapeDtypeStruct((s, MEM_WIDTH), F32), jax.ShapeDtypeStruct((s, MEM_WIDTH), BF16)],
        name="mem_fwd", compiler_params=_params())(proj, mkv)


def _mem_bwd(proj, mkv, o_mem, dym, dproj, *, carry=None):
    s = proj.shape[0]
    ts = 512
    mlen = mkv.shape[0]
    scale = MEM_HEAD_DIM ** -0.5

    def body(m_ref, kv_ref, o_ref, dy_ref, _, dp_ref, dkv_ref):
        @pl.when(pl.program_id(0) == 0)
        def _():
            dkv_ref[...] = jnp.zeros_like(dkv_ref)

        ones = jnp.ones((mlen, LANES), BF16)
        for h in range(MEM_HEADS):
            cols = slice(h * LANES, (h + 1) * LANES)
            vcols = slice(MEM_WIDTH + h * LANES, MEM_WIDTH + (h + 1) * LANES)
            mk = kv_ref[:, cols].astype(BF16)
            mv = kv_ref[:, vcols].astype(BF16)
            qh = m_ref[:, cols].astype(BF16)
            zh = m_ref[:, vcols]
            sg = _sigmoid(zh)
            oh = o_ref[:, cols]
            dyh = dy_ref[:, cols]
            doh = dyh * (zh * sg)
            dp_ref[:, vcols] = (dyh * oh * (sg * (1.0 + zh * (1.0 - sg)))).astype(BF16)
            e = _mem_exp(qh, mk)
            inv = 1.0 / jnp.dot(e, ones, preferred_element_type=F32)
            prob = e.astype(F32) * jnp.concatenate([inv] * (mlen // LANES), axis=1)
            delta = jnp.sum(doh * oh, axis=1, keepdims=True)
            dohb = doh.astype(BF16)
            dprob = lax.dot_general(dohb, mv, _DIMS["nt"], preferred_element_type=F32)
            dsc = (prob * (dprob - delta)).astype(BF16)
            dp_ref[:, cols] = (jnp.dot(dsc, mk, preferred_element_type=F32) * scale).astype(BF16)
            dkv_ref[:, cols] += lax.dot_general(dsc, qh, _DIMS["tn"], preferred_element_type=F32) * scale
            dkv_ref[:, vcols] += lax.dot_general(prob.astype(BF16), dohb, _DIMS["tn"],
                                                 preferred_element_type=F32)

    blk = pl.BlockSpec((ts, MEM_WIDTH), lambda i: (i, 0))
    msp = pl.BlockSpec((ts, W_M), lambda i: (i, COL_M))
    kvsp = pl.BlockSpec((mlen, 2 * MEM_WIDTH), lambda i: (0, 0))
    outs, carried = _carried_call(
        lambda ins, outs, scr: body(*ins, *outs), carry, grid=(s // ts,),
        in_specs=[msp, kvsp, blk, blk, pl.BlockSpec(memory_space=pl.ANY)],
        out_specs=[msp, kvsp],
        out_shape=[jax.ShapeDtypeStruct(dproj.shape, BF16), jax.ShapeDtypeStruct(mkv.shape, F32)],
        scratch=[], operands=(proj, mkv, o_mem, dym, dproj), name="mem_bwd", aliases={4: 0})
    return (*outs, carried) if carry else tuple(outs)


def _forward_backward(x, mem, tgt, proj, w_conv, sink, g_mem, late_weights, g_post, early_exchange, kv_exchange):
    s = x.shape[0]
    tab = _rope_tables(s)
    bias = _window_bias()

    ya = _conv_fwd(proj, w_conv)
    kpad, vpad = _rope_kv(proj, tab)
    o_attn, yb, *arrived = _swa_fwd(proj, kpad, vpad, tab, bias, sink, carry=late_weights[0])
    w_kv, w_up, w_out = late_weights[1](arrived[0] if arrived else None)
    mn = _rmsnorm_fwd(mem, g_mem, name="mem_norm")
    mkv = _matmul(mn, w_kv, mode="nn", out_dtype=F32, tm=256, tn=1024, tk=D_MODEL, name="mem_kv")
    o_mem, ym = _mem_fwd(proj, mkv)
    merged, d_out, dy, dg_post, loss = _mid_fwd(ya, yb, ym, proj, x, tgt, w_up, w_out, g_post)
    dproj, d_ya, d_yb, d_ym, dw_up, dw_out = _mid_bwd(d_out, merged, ya, yb, ym, proj, w_up, w_out)

    dproj, dw_conv = _conv_bwd(proj, w_conv, d_ya, dproj)
    dproj, dkpad, dvpad, dsink = _swa_bwd(proj, kpad, vpad, tab, bias, sink, o_attn, d_yb, dproj)
    dproj = _rope_kv_bwd(dkpad, dvpad, tab, dproj)
    dproj, d_mkv, *early = _mem_bwd(proj, mkv, o_mem, d_ym, dproj, carry=early_exchange(dw_up, dw_out))

    dw_kv = _matmul(mn, d_mkv, mode="tn", out_dtype=F32, tm=1024, tn=1024, tk=256, name="dw_kv")
    d_mn = _matmul(d_mkv, w_kv, mode="nt", out_dtype=F32, tm=256, tn=1024, tk=D_MODEL, name="d_mn")
    _, dg_mem, *early_kv = _rmsnorm_bwd(d_mn, mem, g_mem, d_mn, name="mem_norm_bwd", carry=kv_exchange(dw_kv))

    return dict(loss=loss, dproj=dproj, dy=dy, w_conv=dw_conv, sink=dsink, g_mem=dg_mem,
                w_kv=dw_kv, w_up=dw_up, w_out=dw_out, g_post=dg_post, early=early[0] if early else None,
                early_kv=early_kv[0] if early_kv else None)


N_DEV = 8


def _position():
    return lax.axis_index("x"), lax.axis_index("y"), lax.axis_index("c")


def _other_chips(x, y):
    return (((1 - x, y), 2 * (1 - x) + y), ((x, 1 - y), 2 * x + (1 - y)), ((1 - x, 1 - y), 2 * (1 - x) + (1 - y)))


def _remote(src, dst, send_sems, recv_sems, k, device):
    return pltpu.make_async_remote_copy(src_ref=src, dst_ref=dst, send_sem=send_sems.at[k], recv_sem=recv_sems.at[k],
                                        device_id=device, device_id_type=MESH)


def _rows_half(ref, hf):
    rh = ref.shape[0] // 2
    return ref.at[pl.ds(pl.multiple_of(hf * rh, 8), rh)]


def _gather_weights(shards, small=None, relations=(0, 1, 2), into=None):
    n = len(shards)
    k = 0 if small is None else 1

    def peers(x, y):
        return [(r, chip, idx) for r, (chip, idx) in enumerate(_other_chips(x, y)) if r in relations]

    def ici(ins, outs, sems, a, r, chip, src_chip, c):
        return _remote(_rows_half(ins[a], c), _rows_half(outs[a].at[src_chip], c), sems[0], sems[1], 3 * a + r,
                       (*chip, c))

    def whole(ins, outs, sems, r, chip, src_chip, c):
        return _remote(ins[n], outs[n].at[src_chip], sems[0], sems[1], 3 * n + r, (*chip, c))

    def d2d(outs, sems, a, r, idx, hf, x, y, c):
        half = _rows_half(outs[a].at[idx], hf)
        return _remote(half, half, sems[2], sems[3], 3 * a + r, (x, y, 1 - c))

    def start(ins, outs, sems):
        x, y, c = _position()
        me = 2 * x + y
        for a in range(n):
            for r, chip, _ in peers(x, y):
                ici(ins, outs, sems, a, r, chip, me, c).start()
        for r, (chip, _) in enumerate(_other_chips(x, y)):
            if k:
                whole(ins, outs, sems, r, chip, me, c).start()

    def finish(ins, outs, sems):
        x, y, c = _position()
        me = 2 * x + y
        for a in range(n):
            for r, chip, idx in peers(x, y):
                ici(ins, outs, sems, a, r, chip, idx, c).wait_recv()
                d2d(outs, sems, a, r, idx, c, x, y, c).start()
        for a in range(n):
            for r, chip, idx in peers(x, y):
                d2d(outs, sems, a, r, idx, 1 - c, x, y, c).wait_recv()
        for r, (chip, idx) in enumerate(_other_chips(x, y)):
            if k:
                whole(ins, outs, sems, r, chip, idx, c).wait_recv()
                whole(ins, outs, sems, r, chip, me, c).wait_send()
        for a in range(n):
            for r, chip, idx in peers(x, y):
                ici(ins, outs, sems, a, r, chip, me, c).wait_send()
                d2d(outs, sems, a, r, idx, c, x, y, c).wait_send()

    operands = list(shards) + ([small] if k else [])
    shapes = [jax.ShapeDtypeStruct((N_CHIPS,) + s.shape, s.dtype) for s in operands]
    aliases = {}
    if into is not None:
        assert len(into) == len(operands)
        aliases = {len(operands) + a: a for a in range(len(into))}
        operands += list(into)
    return _Carry(operands, shapes,
                  [pltpu.SemaphoreType.DMA((3 * (n + k),)), pltpu.SemaphoreType.DMA((3 * (n + k),)),
                   pltpu.SemaphoreType.DMA((3 * n,)), pltpu.SemaphoreType.DMA((3 * n,))], start, finish, aliases)


def _pair_exchange(send):
    n = len(send)

    def copies(ins, outs, sems):
        x, y, c = _position()
        return [_remote(ins[a], outs[a], sems[0], sems[1], a, (x, y, 1 - c)) for a in range(n)]

    def start(ins, outs, sems):
        for cp in copies(ins, outs, sems):
            cp.start()

    def finish(ins, outs, sems):
        for cp in copies(ins, outs, sems):
            cp.wait()

    return _Carry(send, [jax.ShapeDtypeStruct(p.shape, p.dtype) for p in send],
                  [pltpu.SemaphoreType.DMA((n,)), pltpu.SemaphoreType.DMA((n,))], start, finish)


def _chip_exchange(sums):
    n = len(sums)

    def copies(ins, outs, sems):
        x, y, c = _position()
        return [_remote(ins[a].at[idx], outs[a].at[r], sems[0], sems[1], 3 * a + r, (*chip, c))
                for a in range(n) for r, (chip, idx) in enumerate(_other_chips(x, y))]

    def start(ins, outs, sems):
        for cp in copies(ins, outs, sems):
            cp.start()

    def finish(ins, outs, sems):
        for cp in copies(ins, outs, sems):
            cp.wait()

    return _Carry(sums, [jax.ShapeDtypeStruct((3,) + p.shape[1:], p.dtype) for p in sums],
                  [pltpu.SemaphoreType.DMA((3 * n,)), pltpu.SemaphoreType.DMA((3 * n,))], start, finish)


def _pair_share(pairs):
    n = len(pairs)

    def start(ins, outs, sems):
        x, y, c = _position()
        for a in range(n):
            _remote(outs[a].at[c], outs[a].at[c], sems[0], sems[1], a, (x, y, 1 - c)).start()

    def finish(ins, outs, sems):
        x, y, c = _position()
        for a in range(n):
            _remote(outs[a].at[1 - c], outs[a].at[1 - c], sems[0], sems[1], a, (x, y, 1 - c)).wait_recv()
        for a in range(n):
            _remote(outs[a].at[c], outs[a].at[c], sems[0], sems[1], a, (x, y, 1 - c)).wait_send()

    return _Carry(pairs, [jax.ShapeDtypeStruct(p.shape, p.dtype) for p in pairs],
                  [pltpu.SemaphoreType.DMA((n,)), pltpu.SemaphoreType.DMA((n,))], start, finish,
                  aliases={a: a for a in range(n)})


def _small_allreduce(pack, share):
    rows, width = pack.shape
    n_share = len(share.ins)

    def body(p_ref, *refs):
        share_in, o_ref, share_out = refs[:n_share], refs[n_share], refs[n_share + 1:2 * n_share + 1]
        buf, send_sems, recv_sems = refs[2 * n_share + 1:2 * n_share + 4]
        share_sems = refs[2 * n_share + 4:]
        share.start(share_in, share_out, share_sems)
        x, y, c = _position()
        me = 4 * x + 2 * y + c
        buf[me] = p_ref[...]
        peers = []
        for r in range(1, N_DEV):
            fx, fy, fc = (r >> 2) & 1, (r >> 1) & 1, r & 1
            px, py, pc = (1 - x if fx else x), (1 - y if fy else y), (1 - c if fc else c)
            peers.append(((px, py, pc), 4 * px + 2 * py + pc))
        sends = [_remote(p_ref, buf.at[me], send_sems, recv_sems, r, dev) for r, (dev, _) in enumerate(peers)]
        for cp in sends:
            cp.start()
        for r, (dev, idx) in enumerate(peers):
            _remote(p_ref, buf.at[idx], send_sems, recv_sems, r, dev).wait_recv()
        for cp in sends:
            cp.wait_send()
        acc = buf[0]
        for k in range(1, N_DEV):
            acc = acc + buf[k]
        o_ref[...] = acc
        share.finish(share_in, share_out, share_sems)

    vm = pl.BlockSpec(memory_space=pltpu.VMEM)
    red, *shared = pl.pallas_call(
        body, in_specs=[vm] + [_HBM] * n_share, out_specs=[vm] + [_HBM] * n_share,
        out_shape=[jax.ShapeDtypeStruct(pack.shape, F32)] + share.out_shapes,
        scratch_shapes=[pltpu.VMEM((N_DEV, rows, width), F32), pltpu.SemaphoreType.DMA((N_DEV - 1,)),
                        pltpu.SemaphoreType.DMA((N_DEV - 1,))] + share.sems,
        input_output_aliases={1 + i: 1 + o for i, o in share.aliases.items()},
        name="small_allreduce")(pack, *share.ins)
    return red, shared


ROW_TILE_MAX = 512
SUM_TILE_MAX = 2048
BF16_SUBLANES = 16


def _row_tile(rows, most=ROW_TILE_MAX):
    if rows <= most:
        return rows
    return max(t for t in range(BF16_SUBLANES, most + 1, BF16_SUBLANES) if rows % t == 0)


def _pair_add(keep, recv, name):
    nj, rh, cols = keep.shape
    tr = _row_tile(rh, SUM_TILE_MAX)

    def body(k_ref, r_ref, o_ref):
        o_ref[...] = (k_ref[...].astype(F32) + r_ref[...].astype(F32)).astype(BF16)

    blk = pl.BlockSpec((None, tr, cols), lambda j, i: (j, i, 0))
    return pl.pallas_call(body, grid=(nj, rh // tr), in_specs=[blk, blk], out_specs=blk,
                          out_shape=jax.ShapeDtypeStruct(keep.shape, BF16), name=name,
                          compiler_params=_params())(keep, recv)


def _chip_add(sums, recv, where, name):
    _, rh, cols = sums.shape
    tr = _row_tile(rh, SUM_TILE_MAX)

    def body(w_ref, s_ref, r_ref, o_ref):
        o_ref[...] = ((s_ref[...].astype(F32) + r_ref[0].astype(F32)) + r_ref[1].astype(F32)) + r_ref[2].astype(F32)

    grid_spec = pltpu.PrefetchScalarGridSpec(
        num_scalar_prefetch=1, grid=(rh // tr,),
        in_specs=[pl.BlockSpec((None, tr, cols), lambda i, w_ref: (w_ref[0], i, 0)),
                  pl.BlockSpec((3, tr, cols), lambda i, w_ref: (0, i, 0))],
        out_specs=pl.BlockSpec((None, tr, cols), lambda i, w_ref: (w_ref[1], i, 0)))
    return pl.pallas_call(body, grid_spec=grid_spec, out_shape=jax.ShapeDtypeStruct((2, rh, cols), F32),
                          name=name, compiler_params=_params())(where, sums, recv)


def _adamw(w, g, m, v, name):
    rows, cols = w.shape
    tr = _row_tile(rows)
    assert rows % tr == 0

    def body(w_ref, g_ref, m_ref, v_ref, d_ref, mo_ref, vo_ref):
        gv = g_ref[...]
        m_new = ADAM_B1 * m_ref[...] + (1.0 - ADAM_B1) * gv
        v_new = ADAM_B2 * v_ref[...] + (1.0 - ADAM_B2) * jnp.square(gv)
        m_hat = m_new / (1.0 - ADAM_B1 ** ADAM_STEP)
        v_hat = v_new / (1.0 - ADAM_B2 ** ADAM_STEP)
        d_ref[...] = -ADAM_LR * (m_hat / (jnp.sqrt(v_hat) + ADAM_EPS) + ADAM_WD * w_ref[...])
        mo_ref[...] = m_new
        vo_ref[...] = v_new

    blk = pl.BlockSpec((tr, cols), lambda i: (i, 0))
    shp = jax.ShapeDtypeStruct((rows, cols), F32)
    return pl.pallas_call(body, grid=(rows // tr,), in_specs=[blk] * 4, out_specs=[blk] * 3,
                          out_shape=[shp] * 3, name=name, compiler_params=_params())(w, g, m, v)


def _adamw_halves(w, g2, m, v, name):
    rows, cols = w.shape
    half = cols // 2
    tr = _row_tile(rows)

    def body(w_ref, g_ref, m_ref, v_ref, go_ref, d_ref, mo_ref, vo_ref):
        gv = g_ref[...]
        go_ref[...] = gv
        m_new = ADAM_B1 * m_ref[...] + (1.0 - ADAM_B1) * gv
        v_new = ADAM_B2 * v_ref[...] + (1.0 - ADAM_B2) * jnp.square(gv)
        m_hat = m_new / (1.0 - ADAM_B1 ** ADAM_STEP)
        v_hat = v_new / (1.0 - ADAM_B2 ** ADAM_STEP)
        d_ref[...] = -ADAM_LR * (m_hat / (jnp.sqrt(v_hat) + ADAM_EPS) + ADAM_WD * w_ref[...])
        mo_ref[...] = m_new
        vo_ref[...] = v_new

    blk = pl.BlockSpec((tr, half), lambda hf, i: (i, hf))
    gsp = pl.BlockSpec((None, tr, half), lambda hf, i: (hf, i, 0))
    shp = jax.ShapeDtypeStruct((rows, cols), F32)
    return pl.pallas_call(body, grid=(2, rows // tr), in_specs=[blk, gsp, blk, blk], out_specs=[blk] * 4,
                          out_shape=[shp] * 4, name=name, compiler_params=_params())(w, g2, m, v)


SHARD_W = IN_WIDTH // N_CHIPS


def _half_major(a):
    r, c = a.shape
    return a.reshape(N_CHIPS, 2, r // N_CHIPS // 2, c).transpose(1, 0, 2, 3)


def kernel(x, mem, g_pre, w_in, w_conv, attn_sink, g_mem, w_mem_kv, w_up_a, w_up_b, w_up_m, w_out, g_post, loss_target, m_g_pre, m_w_in, m_w_conv, m_attn_sink, m_g_mem, m_w_mem_kv, m_w_up_a, m_w_up_b, m_w_up_m, m_w_out, m_g_post, v_g_pre, v_w_in, v_w_conv, v_attn_sink, v_g_mem, v_w_mem_kv, v_w_up_a, v_w_up_b, v_w_up_m, v_w_out, v_g_post):
    xi, yi, ci = _position()
    chip = 2 * xi + yi
    where = jnp.stack([chip, ci, N_CHIPS - 1 - chip]).astype(jnp.int32)

    own = [w_in[0].T.astype(BF16), w_mem_kv[0].astype(BF16),
           jnp.concatenate([w_up_a[0], w_up_b[0], w_up_m[0]], axis=0).astype(BF16), w_out[0].astype(BF16)]
    own_conv = jnp.pad(w_conv[0], ((0, 5), (0, 0)))

    def pieces(mine, got):
        got = lax.dynamic_update_slice_in_dim(got, mine[None], chip, axis=0)
        return [got[j] for j in range(N_CHIPS)]

    diag = N_CHIPS - 1 - chip
    proj, h, h_t, got_near, got_conv, got_far = _proj_near(x[0], g_pre, own[0], own_conv, where)
    w_near = lax.dynamic_update_slice_in_dim(got_near, own[0][None], chip, axis=0).reshape(IN_WIDTH, D_MODEL)
    far = lax.dynamic_index_in_dim(got_far, diag, 0, keepdims=False)
    proj = _proj_far(h, w_near, far, where, into=proj)
    w_conv_full = jnp.concatenate([p[:3] for p in pieces(own_conv, got_conv)], axis=1)

    def late_weights(gathered):
        w_kv_full = jnp.concatenate(pieces(own[1], gathered[0]), axis=0)
        up_pieces = pieces(own[2], gathered[1])
        w_up_full = jnp.stack([jnp.concatenate([p[k * A_WIDTH:(k + 1) * A_WIDTH] for p in up_pieces], axis=1)
                               for k in range(3)])
        return w_kv_full, w_up_full, jnp.concatenate(pieces(own[3], gathered[2]), axis=0)

    def pick(parts, hf):
        return [lax.dynamic_index_in_dim(p, hf, 0, keepdims=False) for p in parts]

    def up_out_parts(dw_up, dw_out):
        up = (dw_up.reshape(3, A_WIDTH, N_CHIPS, D_MODEL // N_CHIPS).transpose(2, 0, 1, 3)
              .reshape(N_CHIPS, 2, 3 * A_WIDTH // 2, D_MODEL // N_CHIPS).transpose(1, 0, 2, 3))
        return [up.astype(BF16), _half_major(dw_out).astype(BF16)]

    g = _forward_backward(x[0], mem[0], loss_target[0], proj, w_conv_full, attn_sink, g_mem,
                          (_gather_weights(own[1:]), late_weights), g_post,
                          lambda dw_up, dw_out: _pair_exchange(pick(up_out_parts(dw_up, dw_out), 1 - ci)),
                          lambda dw_kv: _pair_exchange(pick([_half_major(dw_kv).astype(BF16)], 1 - ci)))

    half_rows = D_MODEL // 2

    def dw_in_half(half_of, name, carry):
        dw, carried = _dw_in_t(g["dproj"], h_t, half_of=half_of, where=where, name=name, carry=carry)
        return dw.reshape(N_CHIPS, SHARD_W, half_rows), carried

    small_keep = pick([_half_major(g["w_kv"]).astype(BF16)] + up_out_parts(g["w_up"], g["w_out"]), ci)
    small_names = ["w_kv", "w_up", "w_out"]
    sums_small = [_pair_add(k, r, "pair_add_" + nm)
                  for k, r, nm in zip(small_keep, g["early_kv"] + g["early"], small_names)]
    dw_send, recv3_small = dw_in_half(lambda w: 1 - w[1], "dw_in_send", _chip_exchange(sums_small))
    dw_keep, (recv_in,) = dw_in_half(lambda w: w[1], "dw_in_keep", _pair_exchange([dw_send]))
    sum_in = _pair_add(dw_keep, recv_in, "pair_add_w_in")
    (grad_x, dg_pre), (recv3_in,) = _d_h(g["dproj"], w_near, far, where, x[0], g_pre, g["dy"],
                                         carry=_chip_exchange([sum_in]))
    pairs = [_chip_add(s, r, where, "chip_add_" + nm)
             for s, r, nm in zip([sum_in] + sums_small, [recv3_in] + recv3_small, ["w_in"] + small_names)]

    zeros512 = jnp.zeros((1, D_MODEL - A_WIDTH), F32)
    conv_rows = [jnp.concatenate([g["w_conv"][k:k + 1], zeros512], axis=1) for k in range(3)]
    sink_row = jnp.pad(g["sink"][:, 0].reshape(1, N_Q_HEADS), ((0, 0), (0, D_MODEL - N_Q_HEADS)))
    loss_row = jnp.pad(g["loss"], ((0, 0), (0, D_MODEL - LANES)))
    pack = jnp.concatenate([dg_pre, g["g_mem"], g["g_post"]] + conv_rows + [sink_row, loss_row], axis=0)
    red, full = _small_allreduce(pack, _pair_share(pairs))
    loss = red[7, 0]
    small_grads = dict(
        g_pre=red[0:1], g_mem=red[1:2], g_post=red[2:3], attn_sink=red[6:7, :N_Q_HEADS],
        w_conv=lax.dynamic_slice(red[3:6, :A_WIDTH], (0, chip * LANES), (3, LANES)))

    gw_up = full[2].reshape(3, A_WIDTH, D_MODEL // N_CHIPS)
    grads = dict(small_grads, w_mem_kv=full[1].reshape(D_MODEL // N_CHIPS, 2 * MEM_WIDTH),
                 w_up_a=gw_up[0], w_up_b=gw_up[1], w_up_m=gw_up[2],
                 w_out=full[3].reshape(D_MODEL // N_CHIPS, D_MODEL))

    weights = dict(g_pre=g_pre, w_in=w_in, w_conv=w_conv, attn_sink=attn_sink, g_mem=g_mem, w_mem_kv=w_mem_kv,
                   w_up_a=w_up_a, w_up_b=w_up_b, w_up_m=w_up_m, w_out=w_out, g_post=g_post)
    m_in = dict(g_pre=m_g_pre, w_in=m_w_in, w_conv=m_w_conv, attn_sink=m_attn_sink, g_mem=m_g_mem,
                w_mem_kv=m_w_mem_kv, w_up_a=m_w_up_a, w_up_b=m_w_up_b, w_up_m=m_w_up_m, w_out=m_w_out,
                g_post=m_g_post)
    v_in = dict(g_pre=v_g_pre, w_in=v_w_in, w_conv=v_w_conv, attn_sink=v_attn_sink, g_mem=v_g_mem,
                w_mem_kv=v_w_mem_kv, w_up_a=v_w_up_a, w_up_b=v_w_up_b, w_up_m=v_w_up_m, w_out=v_w_out,
                g_post=v_g_post)
    out_g, out_d, out_m, out_v = [], [], [], []
    for nm in ("g_pre", "w_in", "w_conv", "attn_sink", "g_mem", "w_mem_kv", "w_up_a", "w_up_b", "w_up_m", "w_out",
               "g_post"):
        shape = weights[nm].shape
        if nm == "w_in":
            results = _adamw_halves(w_in[0].T, full[0], m_w_in[0].T, v_w_in[0].T, "adamw_w_in")
            for out, t in zip((out_g, out_d, out_m, out_v), results):
                out.append(t.T.reshape(shape))
            continue
        two_d = shape[-2:]
        gr = grads[nm].reshape(two_d)
        d, m_new, v_new = _adamw(weights[nm].reshape(two_d), gr, m_in[nm].reshape(two_d), v_in[nm].reshape(two_d),
                                 "adamw_" + nm)
        out_g.append(gr.reshape(shape))
        out_d.append(d.reshape(shape))
        out_m.append(m_new.reshape(shape))
        out_v.append(v_new.reshape(shape))
    return (loss, grad_x.reshape(x.shape), *out_g, *out_d, *out_m, *out_v)
```
